```python
import jax, jax.numpy as jnp
from jax import lax
import numpy as np

D_MODEL = 1024
BATCH = 8
SEQ = 8192
DEPTH = 2

CONV_WIDTH = D_MODEL
CONV_K = 3
ATTN_GROUPS = ((128, 1), (512, 4), (2048, 16))
HEADS_PER_GROUP = 8
HEAD_DIM = 64
ATTN_BLK = 128
ATTN_OUT = HEADS_PER_GROUP * HEAD_DIM
ROPE_THETA = 10000.0
GMLP_WIDTH = D_MODEL
GMLP_GROUPS = 8
GMLP_GROUP_DIM = GMLP_WIDTH // GMLP_GROUPS
GMLP_CHUNK = 128
D_FF = ((-(-8 * D_MODEL // 3)) + 255) // 256 * 256
ALPHA = (2 * DEPTH) ** 0.25
BETA = (8 * DEPTH) ** -0.25
LN_EPS = 1e-5

GROUP_COL = HEADS_PER_GROUP * HEAD_DIM
SPLIT_SIZES = ([D_MODEL] * 3 + [CONV_WIDTH] * 3 + [GROUP_COL] * (3 * len(ATTN_GROUPS)) + [GMLP_WIDTH] * 2)
N_IN = sum(SPLIT_SIZES)

kernel_name = 'hybrid_conv_dilattn_gmlp_deepnorm'


def layer_norm(x, g, b):
    x32 = x.astype(jnp.float32)
    mu = jnp.mean(x32, axis=-1, keepdims=True)
    var = jnp.mean(jnp.square(x32 - mu), axis=-1, keepdims=True)
    y = (x32 - mu) * lax.rsqrt(var + LN_EPS) * g.astype(jnp.float32) + b.astype(jnp.float32)
    return y.astype(x.dtype)


def rope(t, positions):
    half = HEAD_DIM // 2
    inv_freq = ROPE_THETA ** (-jnp.arange(half, dtype=jnp.float32) / half)
    ang = positions.astype(jnp.float32)[..., None] * inv_freq
    cos = jnp.cos(ang)[:, :, None, :]
    sin = jnp.sin(ang)[:, :, None, :]
    t32 = t.astype(jnp.float32)
    t1, t2 = t32[..., :half], t32[..., half:]
    return jnp.concatenate([t1 * cos - t2 * sin, t2 * cos + t1 * sin], axis=-1).astype(t.dtype)


def short_conv_mixer(b_gate, c_gate, h, conv_w):
    z = c_gate * h
    conv = lax.conv_general_dilated(
        z, conv_w[:, None, :].astype(z.dtype), window_strides=(1,), padding=[(CONV_K - 1, 0)],
        dimension_numbers=('NWC', 'WIO', 'NWC'), feature_group_count=CONV_WIDTH)
    return b_gate * conv


def dilated_window_attention(q, k, v, window, dil):
    B, S, H, Dh = q.shape
    steps = window // dil
    span = dil * ATTN_BLK
    s_pad = -(-S // span) * span
    L = s_pad // dil
    nb = L // ATTN_BLK
    pad = ((0, 0), (0, s_pad - S), (0, 0), (0, 0))

    def fold(a):
        a = jnp.pad(a, pad).reshape(B, L, dil, H, Dh).transpose(0, 2, 1, 3, 4)
        return a.reshape(B, dil, nb, ATTN_BLK, H, Dh)

    qb, kb, vb = fold(q), fold(k), fold(v)
    blk_pad = ((0, 0), (0, 0), (1, 0), (0, 0), (0, 0), (0, 0))
    kw = jnp.concatenate([jnp.pad(kb, blk_pad)[:, :, :-1], kb], axis=3)
    vw = jnp.concatenate([jnp.pad(vb, blk_pad)[:, :, :-1], vb], axis=3)
    s = jnp.einsum('brnqhd,brnkhd->brnhqk', qb, kw).astype(jnp.float32) * (HEAD_DIM ** -0.5)
    qi = jnp.arange(ATTN_BLK)[:, None] + ATTN_BLK
    ki = jnp.arange(2 * ATTN_BLK)[None, :]
    dist = qi - ki
    band = (dist >= 0) & (dist <= steps)
    first = (jnp.arange(nb)[:, None, None] == 0) & (ki[None] < ATTN_BLK)
    valid = band[None] & jnp.logical_not(first)
    s = jnp.where(valid[None, None, :, None], s, -jnp.inf)
    lse = jax.nn.logsumexp(s, axis=-1)
    p = jnp.exp(s - lse[..., None]).astype(v.dtype)
    o = jnp.einsum('brnhqk,brnkhd->brnqhd', p, vw)
    o = o.reshape(B, dil, L, H, Dh).transpose(0, 2, 1, 3, 4).reshape(B, s_pad, H, Dh)[:, :S]
    lse = lse.transpose(0, 1, 2, 4, 3).reshape(B, dil, L, H).transpose(0, 2, 1, 3).reshape(B, s_pad, H)[:, :S]
    return o, lse


def dilated_attention_mixer(qkv_parts, positions):
    B, S, _ = qkv_parts[0].shape
    outs, lses = [], []
    for g, (window, dil) in enumerate(ATTN_GROUPS):
        q = rope(qkv_parts[3 * g].reshape(B, S, HEADS_PER_GROUP, HEAD_DIM), positions)
        k = rope(qkv_parts[3 * g + 1].reshape(B, S, HEADS_PER_GROUP, HEAD_DIM), positions)
        v = qkv_parts[3 * g + 2].reshape(B, S, HEADS_PER_GROUP, HEAD_DIM)
        o, lse = dilated_window_attention(q, k, v, window, dil)
        outs.append(o)
        lses.append(lse)
    wts = jax.nn.softmax(jnp.stack(lses, axis=0), axis=0)
    o = sum(wts[g][..., None].astype(outs[g].dtype) * outs[g] for g in range(len(ATTN_GROUPS)))
    return o.reshape(B, S, ATTN_OUT)


def chunked_spatial_gating(u_pre, v_pre, ln_g, ln_b, w_s, b_s):
    B, S, _ = u_pre.shape
    u = jax.nn.gelu(u_pre, approximate=False)
    v = layer_norm(jax.nn.gelu(v_pre, approximate=False), ln_g, ln_b)
    n = S // GMLP_CHUNK
    vc = v.reshape(B, n, GMLP_CHUNK, GMLP_GROUPS, GMLP_GROUP_DIM)
    tril = jnp.tril(jnp.ones((GMLP_CHUNK, GMLP_CHUNK), dtype=w_s.dtype))
    w_causal = w_s * tril[None]
    sp = jnp.einsum('gij,bnjgc->bnigc', w_causal, vc) + b_s.T[None, None, :, :, None]
    return u * sp.reshape(B, S, GMLP_WIDTH)


def _fwd_setup_inputs(seed: int = 0) -> dict:
    key = jax.random.key(seed)
    ks = jax.random.split(key, 24)
    f32 = jnp.float32

    def nrm(k, shape, fan_in, scale=1.0):
        return jax.random.normal(k, shape, f32) * (scale * fan_in ** -0.5)

    col_scale = np.ones((N_IN,), np.float32)
    off = 3 * D_MODEL + 3 * CONV_WIDTH
    for g in range(len(ATTN_GROUPS)):
        vs = off + (3 * g + 2) * GROUP_COL
        col_scale[vs:vs + GROUP_COL] = BETA
    x = jax.random.normal(ks[0], (BATCH, SEQ, D_MODEL), f32)
    offset = jax.random.randint(ks[1], (BATCH, 1), 0, 4096, dtype=jnp.int32)
    positions = (offset + jnp.arange(SEQ, dtype=jnp.int32)[None, :]).astype(jnp.int32)
    return {
        'x': x,
        'positions': positions,
        'w_in': nrm(ks[2], (DEPTH, D_MODEL, N_IN), D_MODEL) * jnp.asarray(col_scale),
        'conv_w': nrm(ks[3], (DEPTH, CONV_K, CONV_WIDTH), CONV_K),
        'gmlp_ln_g': 1.0 + 0.02 * jax.random.normal(ks[4], (DEPTH, GMLP_WIDTH), f32),
        'gmlp_ln_b': 0.02 * jax.random.normal(ks[5], (DEPTH, GMLP_WIDTH), f32),
        'w_s': nrm(ks[6], (DEPTH, GMLP_GROUPS, GMLP_CHUNK, GMLP_CHUNK), GMLP_CHUNK),
        'b_s': 1.0 + 0.1 * jax.random.normal(ks[7], (DEPTH, GMLP_GROUPS, GMLP_CHUNK), f32),
        'p_a': nrm(ks[8], (DEPTH, CONV_WIDTH, D_MODEL), CONV_WIDTH, BETA),
        'p_b': nrm(ks[9], (DEPTH, ATTN_OUT, D_MODEL), ATTN_OUT, BETA),
        'p_c': nrm(ks[10], (DEPTH, GMLP_WIDTH, D_MODEL), GMLP_WIDTH, BETA),
        'w_o': nrm(ks[11], (DEPTH, D_MODEL, D_MODEL), D_MODEL, BETA),
        'ln1_g': 1.0 + 0.02 * jax.random.normal(ks[12], (DEPTH, D_MODEL), f32),
        'ln1_b': 0.02 * jax.random.normal(ks[13], (DEPTH, D_MODEL), f32),
        'w_gate': nrm(ks[14], (DEPTH, D_MODEL, D_FF), D_MODEL),
        'w_up': nrm(ks[15], (DEPTH, D_MODEL, D_FF), D_MODEL, BETA),
        'w_down': nrm(ks[16], (DEPTH, D_FF, D_MODEL), D_FF, BETA),
        'ln2_g': 1.0 + 0.02 * jax.random.normal(ks[17], (DEPTH, D_MODEL), f32),
        'ln2_b': 0.02 * jax.random.normal(ks[18], (DEPTH, D_MODEL), f32),
    }


def _fwd_reference(x, positions, w_in, conv_w, gmlp_ln_g, gmlp_ln_b, w_s, b_s, p_a, p_b, p_c, w_o,
              ln1_g, ln1_b, w_gate, w_up, w_down, ln2_g, ln2_b):
    split_points = np.cumsum(np.array(SPLIT_SIZES))[:-1].tolist()
    n_attn = 3 * len(ATTN_GROUPS)
    for l in range(DEPTH):
        proj = x @ w_in[l].astype(x.dtype)
        parts = jnp.split(proj, split_points, axis=-1)
        g_a, g_b, g_c = (jax.nn.sigmoid(p) for p in parts[0:3])
        y_a = short_conv_mixer(parts[3], parts[4], parts[5], conv_w[l])
        y_b = dilated_attention_mixer(parts[6:6 + n_attn], positions)
        y_c = chunked_spatial_gating(parts[6 + n_attn], parts[7 + n_attn], gmlp_ln_g[l], gmlp_ln_b[l],
                                     w_s[l], b_s[l])
        m = g_a * (y_a @ p_a[l]) + g_b * (y_b @ p_b[l]) + g_c * (y_c @ p_c[l])
        x = layer_norm(ALPHA * x + m @ w_o[l], ln1_g[l], ln1_b[l])
        h = jax.nn.silu(x @ w_gate[l]) * (x @ w_up[l])
        x = layer_norm(ALPHA * x + h @ w_down[l], ln2_g[l], ln2_b[l])
    return x


import jax as _jax
import jax.numpy as _jnp

TWIN_FORMAT = 'train_step'
FWD_PARAMS = ['x', 'positions', 'w_in', 'conv_w', 'gmlp_ln_g', 'gmlp_ln_b', 'w_s', 'b_s', 'p_a', 'p_b', 'p_c', 'w_o', 'ln1_g', 'ln1_b', 'w_gate', 'w_up', 'w_down', 'ln2_g', 'ln2_b']
TWIN_WEIGHTS = ['w_in', 'conv_w', 'gmlp_ln_g', 'gmlp_ln_b', 'w_s', 'b_s', 'p_a', 'p_b', 'p_c', 'w_o', 'ln1_g', 'ln1_b', 'w_gate', 'w_up', 'w_down', 'ln2_g', 'ln2_b']
TWIN_DIFF_INPUT = 'x'
TWIN_INPUTS = ['x', 'positions', 'w_in', 'conv_w', 'gmlp_ln_g', 'gmlp_ln_b', 'w_s', 'b_s', 'p_a', 'p_b', 'p_c', 'w_o', 'ln1_g', 'ln1_b', 'w_gate', 'w_up', 'w_down', 'ln2_g', 'ln2_b', 'loss_target', 'm_w_in', 'm_conv_w', 'm_gmlp_ln_g', 'm_gmlp_ln_b', 'm_w_s', 'm_b_s', 'm_p_a', 'm_p_b', 'm_p_c', 'm_w_o', 'm_ln1_g', 'm_ln1_b', 'm_w_gate', 'm_w_up', 'm_w_down', 'm_ln2_g', 'm_ln2_b', 'v_w_in', 'v_conv_w', 'v_gmlp_ln_g', 'v_gmlp_ln_b', 'v_w_s', 'v_b_s', 'v_p_a', 'v_p_b', 'v_p_c', 'v_w_o', 'v_ln1_g', 'v_ln1_b', 'v_w_gate', 'v_w_up', 'v_w_down', 'v_ln2_g', 'v_ln2_b']
TWIN_OUTPUTS = ['loss', 'grad_x', 'grad_w_in', 'grad_conv_w', 'grad_gmlp_ln_g', 'grad_gmlp_ln_b', 'grad_w_s', 'grad_b_s', 'grad_p_a', 'grad_p_b', 'grad_p_c', 'grad_w_o', 'grad_ln1_g', 'grad_ln1_b', 'grad_w_gate', 'grad_w_up', 'grad_w_down', 'grad_ln2_g', 'grad_ln2_b', 'delta_w_in', 'delta_conv_w', 'delta_gmlp_ln_g', 'delta_gmlp_ln_b', 'delta_w_s', 'delta_b_s', 'delta_p_a', 'delta_p_b', 'delta_p_c', 'delta_w_o', 'delta_ln1_g', 'delta_ln1_b', 'delta_w_gate', 'delta_w_up', 'delta_w_down', 'delta_ln2_g', 'delta_ln2_b', 'new_m_w_in', 'new_m_conv_w', 'new_m_gmlp_ln_g', 'new_m_gmlp_ln_b', 'new_m_w_s', 'new_m_b_s', 'new_m_p_a', 'new_m_p_b', 'new_m_p_c', 'new_m_w_o', 'new_m_ln1_g', 'new_m_ln1_b', 'new_m_w_gate', 'new_m_w_up', 'new_m_w_down', 'new_m_ln2_g', 'new_m_ln2_b', 'new_v_w_in', 'new_v_conv_w', 'new_v_gmlp_ln_g', 'new_v_gmlp_ln_b', 'new_v_w_s', 'new_v_b_s', 'new_v_p_a', 'new_v_p_b', 'new_v_p_c', 'new_v_w_o', 'new_v_ln1_g', 'new_v_ln1_b', 'new_v_w_gate', 'new_v_w_up', 'new_v_w_down', 'new_v_ln2_g', 'new_v_ln2_b']
TWIN_LEAF_KINDS = {'loss': 'loss', 'grad_x': 'grad_x', 'grad_w_in': 'grad_w', 'grad_conv_w': 'grad_w', 'grad_gmlp_ln_g': 'grad_w', 'grad_gmlp_ln_b': 'grad_w', 'grad_w_s': 'grad_w', 'grad_b_s': 'grad_w', 'grad_p_a': 'grad_w', 'grad_p_b': 'grad_w', 'grad_p_c': 'grad_w', 'grad_w_o': 'grad_w', 'grad_ln1_g': 'grad_w', 'grad_ln1_b': 'grad_w', 'grad_w_gate': 'grad_w', 'grad_w_up': 'grad_w', 'grad_w_down': 'grad_w', 'grad_ln2_g': 'grad_w', 'grad_ln2_b': 'grad_w', 'delta_w_in': 'delta_w', 'delta_conv_w': 'delta_w', 'delta_gmlp_ln_g': 'delta_w', 'delta_gmlp_ln_b': 'delta_w', 'delta_w_s': 'delta_w', 'delta_b_s': 'delta_w', 'delta_p_a': 'delta_w', 'delta_p_b': 'delta_w', 'delta_p_c': 'delta_w', 'delta_w_o': 'delta_w', 'delta_ln1_g': 'delta_w', 'delta_ln1_b': 'delta_w', 'delta_w_gate': 'delta_w', 'delta_w_up': 'delta_w', 'delta_w_down': 'delta_w', 'delta_ln2_g': 'delta_w', 'delta_ln2_b': 'delta_w', 'new_m_w_in': 'new_m', 'new_m_conv_w': 'new_m', 'new_m_gmlp_ln_g': 'new_m', 'new_m_gmlp_ln_b': 'new_m', 'new_m_w_s': 'new_m', 'new_m_b_s': 'new_m', 'new_m_p_a': 'new_m', 'new_m_p_b': 'new_m', 'new_m_p_c': 'new_m', 'new_m_w_o': 'new_m', 'new_m_ln1_g': 'new_m', 'new_m_ln1_b': 'new_m', 'new_m_w_gate': 'new_m', 'new_m_w_up': 'new_m', 'new_m_w_down': 'new_m', 'new_m_ln2_g': 'new_m', 'new_m_ln2_b': 'new_m', 'new_v_w_in': 'new_v', 'new_v_conv_w': 'new_v', 'new_v_gmlp_ln_g': 'new_v', 'new_v_gmlp_ln_b': 'new_v', 'new_v_w_s': 'new_v', 'new_v_b_s': 'new_v', 'new_v_p_a': 'new_v', 'new_v_p_b': 'new_v', 'new_v_p_c': 'new_v', 'new_v_w_o': 'new_v', 'new_v_ln1_g': 'new_v', 'new_v_ln1_b': 'new_v', 'new_v_w_gate': 'new_v', 'new_v_w_up': 'new_v', 'new_v_w_down': 'new_v', 'new_v_ln2_g': 'new_v', 'new_v_ln2_b': 'new_v'}


def _forward(args):
    return _fwd_reference(*[args[k] for k in FWD_PARAMS])


def _output_shape():
    def fwd():
        inp = _fwd_setup_inputs(0)
        return _fwd_reference(*[inp[k] for k in FWD_PARAMS])
    out = _jax.eval_shape(fwd)
    return out.shape, out.dtype

N_MICROBATCH = 1
ADAM_LR = 0.001
ADAM_B1 = 0.9
ADAM_B2 = 0.999
ADAM_EPS = 1e-08
ADAM_WD = 0.01
ADAM_STEP = 10
PER_EXAMPLE_BATCH_AXIS = {'x': 0, 'positions': 0, 'loss_target': 0}
SHARED_INPUTS = []
_WEIGHT_DTYPES = {'w_in': _jnp.float32, 'conv_w': _jnp.float32, 'gmlp_ln_g': _jnp.float32, 'gmlp_ln_b': _jnp.float32, 'w_s': _jnp.float32, 'b_s': _jnp.float32, 'p_a': _jnp.float32, 'p_b': _jnp.float32, 'p_c': _jnp.float32, 'w_o': _jnp.float32, 'ln1_g': _jnp.float32, 'ln1_b': _jnp.float32, 'w_gate': _jnp.float32, 'w_up': _jnp.float32, 'w_down': _jnp.float32, 'ln2_g': _jnp.float32, 'ln2_b': _jnp.float32}
MOMENT_SCALE = {'w_in': 1.407906e-02, 'conv_w': 2.417736e-02, 'gmlp_ln_g': 1.171193e-02, 'gmlp_ln_b': 1.127733e-02, 'w_s': 1.121086e-02, 'b_s': 1.567211e-02, 'p_a': 4.845095e-02, 'p_b': 4.632041e-03, 'p_c': 5.528434e-02, 'w_o': 7.120506e-02, 'ln1_g': 2.234451e+00, 'ln1_b': 1.051004e+00, 'w_gate': 1.652545e-02, 'w_up': 3.211814e-02, 'w_down': 5.335253e-02, 'ln2_g': 4.535720e+01, 'ln2_b': 2.284454e+00}


def _to_microbatches(a, axis):
    t = _jnp.moveaxis(a, axis, 0)
    t = t.reshape((N_MICROBATCH, t.shape[0] // N_MICROBATCH) + t.shape[1:])
    return _jnp.moveaxis(t, 1, axis + 1)


def setup_inputs(seed: int = 0) -> dict:
    inp = _fwd_setup_inputs(seed)
    key = _jax.random.fold_in(_jax.random.key(seed), 7919)
    shape, _ = _output_shape()
    out = dict(inp)
    out["loss_target"] = _jax.random.normal(_jax.random.fold_in(key, 0), shape, _jnp.float32)
    for i, name in enumerate(TWIN_WEIGHTS):
        w = inp[name].astype(_jnp.float32)
        if MOMENT_SCALE is None:
            s = _jnp.sqrt(_jnp.mean(_jnp.square(w)) + 1e-30)
        else:
            s = MOMENT_SCALE[name]
        km, kv = _jax.random.split(_jax.random.fold_in(key, i + 1))
        out[name] = w
        out["m_" + name] = s * _jax.random.normal(km, w.shape, _jnp.float32)
        out["v_" + name] = (s * s) * _jax.random.uniform(kv, w.shape, _jnp.float32, 0.5, 1.5)
    if N_MICROBATCH > 1:
        for name, axis in PER_EXAMPLE_BATCH_AXIS.items():
            out[name] = _to_microbatches(out[name], axis)
    return {'x': out['x'], 'positions': out['positions'], 'w_in': out['w_in'], 'conv_w': out['conv_w'], 'gmlp_ln_g': out['gmlp_ln_g'], 'gmlp_ln_b': out['gmlp_ln_b'], 'w_s': out['w_s'], 'b_s': out['b_s'], 'p_a': out['p_a'], 'p_b': out['p_b'], 'p_c': out['p_c'], 'w_o': out['w_o'], 'ln1_g': out['ln1_g'], 'ln1_b': out['ln1_b'], 'w_gate': out['w_gate'], 'w_up': out['w_up'], 'w_down': out['w_down'], 'ln2_g': out['ln2_g'], 'ln2_b': out['ln2_b'], 'loss_target': out['loss_target'], 'm_w_in': out['m_w_in'], 'm_conv_w': out['m_conv_w'], 'm_gmlp_ln_g': out['m_gmlp_ln_g'], 'm_gmlp_ln_b': out['m_gmlp_ln_b'], 'm_w_s': out['m_w_s'], 'm_b_s': out['m_b_s'], 'm_p_a': out['m_p_a'], 'm_p_b': out['m_p_b'], 'm_p_c': out['m_p_c'], 'm_w_o': out['m_w_o'], 'm_ln1_g': out['m_ln1_g'], 'm_ln1_b': out['m_ln1_b'], 'm_w_gate': out['m_w_gate'], 'm_w_up': out['m_w_up'], 'm_w_down': out['m_w_down'], 'm_ln2_g': out['m_ln2_g'], 'm_ln2_b': out['m_ln2_b'], 'v_w_in': out['v_w_in'], 'v_conv_w': out['v_conv_w'], 'v_gmlp_ln_g': out['v_gmlp_ln_g'], 'v_gmlp_ln_b': out['v_gmlp_ln_b'], 'v_w_s': out['v_w_s'], 'v_b_s': out['v_b_s'], 'v_p_a': out['v_p_a'], 'v_p_b': out['v_p_b'], 'v_p_c': out['v_p_c'], 'v_w_o': out['v_w_o'], 'v_ln1_g': out['v_ln1_g'], 'v_ln1_b': out['v_ln1_b'], 'v_w_gate': out['v_w_gate'], 'v_w_up': out['v_w_up'], 'v_w_down': out['v_w_down'], 'v_ln2_g': out['v_ln2_g'], 'v_ln2_b': out['v_ln2_b']}


def _loss(weights, diff, rest, loss_target):
    with _jax.named_scope("forward"):
        args = {**rest, TWIN_DIFF_INPUT: diff, **{k: w.astype(_WEIGHT_DTYPES[k]) for k, w in weights.items()}}
        y = _forward(args)
    with _jax.named_scope("loss_head"):
        err = _jnp.square(y.astype(_jnp.float32) - loss_target)
        return 0.5 * _jnp.sum(_jnp.mean(err, axis=-1)) if err.ndim else 0.5 * err


def _adamw(w, g, m, v):
    m = ADAM_B1 * m + (1.0 - ADAM_B1) * g
    v = ADAM_B2 * v + (1.0 - ADAM_B2) * _jnp.square(g)
    m_hat = m / (1.0 - ADAM_B1 ** ADAM_STEP)
    v_hat = v / (1.0 - ADAM_B2 ** ADAM_STEP)
    delta = -ADAM_LR * (m_hat / (_jnp.sqrt(v_hat) + ADAM_EPS) + ADAM_WD * w)
    return delta, m, v


def reference(x, positions, w_in, conv_w, gmlp_ln_g, gmlp_ln_b, w_s, b_s, p_a, p_b, p_c, w_o, ln1_g, ln1_b, w_gate, w_up, w_down, ln2_g, ln2_b, loss_target, m_w_in, m_conv_w, m_gmlp_ln_g, m_gmlp_ln_b, m_w_s, m_b_s, m_p_a, m_p_b, m_p_c, m_w_o, m_ln1_g, m_ln1_b, m_w_gate, m_w_up, m_w_down, m_ln2_g, m_ln2_b, v_w_in, v_conv_w, v_gmlp_ln_g, v_gmlp_ln_b, v_w_s, v_b_s, v_p_a, v_p_b, v_p_c, v_w_o, v_ln1_g, v_ln1_b, v_w_gate, v_w_up, v_w_down, v_ln2_g, v_ln2_b):
    given = dict(x=x, positions=positions, w_in=w_in, conv_w=conv_w, gmlp_ln_g=gmlp_ln_g, gmlp_ln_b=gmlp_ln_b, w_s=w_s, b_s=b_s, p_a=p_a, p_b=p_b, p_c=p_c, w_o=w_o, ln1_g=ln1_g, ln1_b=ln1_b, w_gate=w_gate, w_up=w_up, w_down=w_down, ln2_g=ln2_g, ln2_b=ln2_b, loss_target=loss_target, m_w_in=m_w_in, m_conv_w=m_conv_w, m_gmlp_ln_g=m_gmlp_ln_g, m_gmlp_ln_b=m_gmlp_ln_b, m_w_s=m_w_s, m_b_s=m_b_s, m_p_a=m_p_a, m_p_b=m_p_b, m_p_c=m_p_c, m_w_o=m_w_o, m_ln1_g=m_ln1_g, m_ln1_b=m_ln1_b, m_w_gate=m_w_gate, m_w_up=m_w_up, m_w_down=m_w_down, m_ln2_g=m_ln2_g, m_ln2_b=m_ln2_b, v_w_in=v_w_in, v_conv_w=v_conv_w, v_gmlp_ln_g=v_gmlp_ln_g, v_gmlp_ln_b=v_gmlp_ln_b, v_w_s=v_w_s, v_b_s=v_b_s, v_p_a=v_p_a, v_p_b=v_p_b, v_p_c=v_p_c, v_w_o=v_w_o, v_ln1_g=v_ln1_g, v_ln1_b=v_ln1_b, v_w_gate=v_w_gate, v_w_up=v_w_up, v_w_down=v_w_down, v_ln2_g=v_ln2_g, v_ln2_b=v_ln2_b)
    weights = {n: given[n] for n in TWIN_WEIGHTS}
    shared = {n: given[n] for n in SHARED_INPUTS}
    per_example = {n: given[n] for n in ['x', 'positions']}
    grad_fn = _jax.value_and_grad(_loss, argnums=(0, 1))

    def one_microbatch(ex, loss_target):
        ex = dict(ex)
        diff = ex.pop(TWIN_DIFF_INPUT)
        return grad_fn(weights, diff, {**shared, **ex}, loss_target)

    if N_MICROBATCH == 1:
        loss, (grad_w, grad_x) = one_microbatch(per_example, given["loss_target"])
    else:
        def body(carry, xs):
            loss_sum, grad_sum = carry
            l_k, (gw_k, gx_k) = one_microbatch(xs[0], xs[1])
            with _jax.named_scope("update"):
                return (loss_sum + l_k, _jax.tree.map(_jnp.add, grad_sum, gw_k)), gx_k

        init = (_jnp.zeros((), _jnp.float32), _jax.tree.map(_jnp.zeros_like, weights))
        (loss, grad_w), grad_x = _jax.lax.scan(body, init, (per_example, given["loss_target"]))
    with _jax.named_scope("update"):
        delta_w, new_m, new_v = {}, {}, {}
        for n in TWIN_WEIGHTS:
            delta_w[n], new_m[n], new_v[n] = _adamw(weights[n], grad_w[n], given["m_" + n], given["v_" + n])
    return (loss, grad_x, *[grad_w[n] for n in TWIN_WEIGHTS], *[delta_w[n] for n in TWIN_WEIGHTS],
            *[new_m[n] for n in TWIN_WEIGHTS], *[new_v[n] for n in TWIN_WEIGHTS])
```

```python
import functools
import math

import jax
import jax.numpy as jnp
from jax import lax
from jax.experimental import pallas as pl
from jax.experimental.pallas import tpu as pltpu

D = 1024
NIN = 12800
DFF = 2816
NCHIP = 4
FB = DFF // NCHIP
WIN_SHARD = NIN // NCHIP
DEPTH = 2
GROUPS = ((128, 1), (512, 4), (2048, 16))
HD = 64
BLK = 128
AO = 512
ALPHA = (2 * DEPTH) ** 0.25
EPS = 1e-5
ROPE_THETA = 10000.0
LANES = 128
NEG = -1e30

C_GATES, C_BCH, C_QKV, C_UV = 0, 3 * D, 6 * D, 6 * D + 9 * AO

MX = jnp.bfloat16
ACT = jnp.float32

ADAM_LR, ADAM_B1, ADAM_B2, ADAM_EPS, ADAM_WD, ADAM_STEP = 0.001, 0.9, 0.999, 1e-08, 0.01, 10

f32 = jnp.float32
NT = (((1,), (1,)), ((), ()))
TN = (((0,), (0,)), ((), ()))


def _cp(sem, vmem_mb=48):
    return pltpu.CompilerParams(dimension_semantics=sem, vmem_limit_bytes=vmem_mb << 20)


def _dot(a, b, dims=None):
    if dims is None:
        return jnp.dot(a, b, preferred_element_type=f32)
    return lax.dot_general(a, b, dims, preferred_element_type=f32)


def _ln_stats(r):
    mu = jnp.mean(r, axis=-1, keepdims=True)
    xc = r - mu
    var = jnp.mean(xc * xc, axis=-1, keepdims=True)
    rstd = lax.rsqrt(var + EPS)
    return xc * rstd, rstd


def _ln_bwd(dy, xhat, rstd, g):
    dxh = dy * g
    return rstd * (dxh - jnp.mean(dxh, axis=-1, keepdims=True) - xhat * jnp.mean(dxh * xhat, axis=-1, keepdims=True))


def _gelu(x):
    return 0.5 * x * (1.0 + lax.erf(x * (1.0 / math.sqrt(2.0))))


def _gelu_grad(x):
    return 0.5 * (1.0 + lax.erf(x * (1.0 / math.sqrt(2.0)))) + x * jnp.exp(-0.5 * x * x) * (1.0 / math.sqrt(2.0 * math.pi))


def _sigmoid(x):
    return 1.0 / (1.0 + jnp.exp(-x))


def _acc_rows(o_ref, first, val):
    @pl.when(first)
    def _():
        o_ref[...] = jnp.zeros_like(o_ref)
    o_ref[...] += jnp.sum(val, axis=0, keepdims=True)


def mm_in(x, w):
    T = x.shape[0]
    tm, tn = min(1024, T), 512

    def body(x_ref, w_ref, o_ref, xb):
        @pl.when(pl.program_id(1) == 0)
        def _():
            xb[...] = x_ref[...].astype(MX)
        o_ref[...] = _dot(xb[...], w_ref[...]).astype(o_ref.dtype)

    return pl.pallas_call(
        body, name="mm_in", grid=(T // tm, NIN // tn),
        in_specs=[pl.BlockSpec((tm, D), lambda i, j: (i, 0)), pl.BlockSpec((D, tn), lambda i, j: (0, j))],
        out_specs=pl.BlockSpec((tm, tn), lambda i, j: (i, j)),
        out_shape=jax.ShapeDtypeStruct((T, NIN), ACT),
        scratch_shapes=[pltpu.VMEM((tm, D), MX)],
        compiler_params=_cp(("parallel", "arbitrary")),
    )(x, w)


HALO = 16
TM_AC = 256


def _uv_specs():
    return [pl.BlockSpec((TM_AC, 512), functools.partial(lambda i, j: (i, j), j=C_UV // 512 + j)) for j in range(4)]


def _gmlp_fwd(up, vp, ws_ref, bs_ref, lg, lb):
    u = _gelu(up)
    xhat, rstd = _ln_stats(_gelu(vp))
    vn = xhat * lg + lb
    vnb = vn.astype(MX)
    rows = []
    for c in range(up.shape[0] // BLK):
        r = slice(c * BLK, (c + 1) * BLK)
        rows.append(jnp.concatenate(
            [_dot(ws_ref[g], vnb[r, g * BLK:(g + 1) * BLK]) + bs_ref[g] for g in range(8)], axis=1))
    return u, vn, xhat, rstd, jnp.concatenate(rows, axis=0)


def mix_ac_fwd(proj, conv_w, wst, bsx, lg, lb):
    T = proj.shape[0]
    tm = TM_AC

    def body(bch, halo, u0, u1, v0, v1, cw, ws, bs, lg_ref, lb_ref, ya, yc, zs):
        i = pl.program_id(0)
        pb = bch[...].astype(f32)
        z = pb[:, D:2 * D] * pb[:, 2 * D:]
        hz = halo[:, :D].astype(f32) * halo[:, D:].astype(f32)
        zs[0:HALO, :] = jnp.where(i > 0, hz, 0.0)
        zs[HALO:HALO + tm, :] = z
        cv = cw[0:1, :] * zs[HALO - 2:HALO - 2 + tm, :] + cw[1:2, :] * zs[HALO - 1:HALO - 1 + tm, :] + cw[2:3, :] * z
        ya[...] = (pb[:, :D] * cv).astype(ya.dtype)
        up = jnp.concatenate([u0[...], u1[...]], axis=1).astype(f32)
        vp = jnp.concatenate([v0[...], v1[...]], axis=1).astype(f32)
        u, _, _, _, sp = _gmlp_fwd(up, vp, ws, bs, lg_ref[...], lb_ref[...])
        yc[...] = (u * sp).astype(yc.dtype)

    full = lambda shape: pl.BlockSpec(shape, lambda i: (0,) * len(shape))
    return pl.pallas_call(
        body, name="mix_ac_fwd", grid=(T // tm,),
        in_specs=[pl.BlockSpec((tm, 3 * D), lambda i: (i, 1)),
                  pl.BlockSpec((HALO, 2 * D), lambda i: (jnp.maximum(i * (tm // HALO) - 1, 0), 2)),
                  *_uv_specs(), full((3, D)), full((8, BLK, BLK)), full((8, BLK, BLK)), full((1, D)), full((1, D))],
        out_specs=[pl.BlockSpec((tm, D), lambda i: (i, 0))] * 2,
        out_shape=[jax.ShapeDtypeStruct((T, D), MX)] * 2,
        scratch_shapes=[pltpu.VMEM((HALO + tm, D), f32)],
        compiler_params=_cp(("parallel",)),
    )(proj, proj, proj, proj, proj, proj, conv_w, wst, bsx, lg, lb)


def _swap_halves(x):
    lane = lax.broadcasted_iota(jnp.int32, x.shape, 1)
    return jnp.where((lane % HD) < HD // 2, pltpu.roll(x, x.shape[1] - HD // 2, 1), pltpu.roll(x, HD // 2, 1))


def _tile4(t):
    return jnp.concatenate([t] * (AO // LANES), axis=1)


TM_FOLD = 512


def _fold_out(nat, x, out_ref, d):
    if d == 1:
        out_ref[0] = x.astype(out_ref.dtype)
        return
    rows = x.shape[0] // d
    for j in range(AO // LANES):
        nat[j] = x[:, j * LANES:(j + 1) * LANES]
    for r in range(d):
        out_ref[r] = jnp.concatenate(
            [nat.at[j][pl.ds(r, rows, stride=d), :] for j in range(AO // LANES)], axis=1).astype(out_ref.dtype)


def _unfold_in(nat, in_ref, d):
    if d == 1:
        return in_ref[0].astype(f32)
    rows = in_ref.shape[1]
    for r in range(d):
        v = in_ref[r].astype(f32)
        for j in range(AO // LANES):
            nat.at[j][pl.ds(r, rows, stride=d), :] = v[:, j * LANES:(j + 1) * LANES]
    return jnp.concatenate([nat[j] for j in range(AO // LANES)], axis=1)


def fold_rope(proj, cos_t, sin_t, g, d):
    T = proj.shape[0]
    tm = TM_FOLD
    rows = tm // d

    def body(x_ref, c_ref, s_ref, q_o, k_o, v_o, nat):
        cos, sin = _tile4(c_ref[...]), _tile4(s_ref[...])
        for part, out, scale in ((0, q_o, HD ** -0.5), (1, k_o, 1.0), (2, v_o, None)):
            x = x_ref[:, part * AO:(part + 1) * AO].astype(f32)
            if scale is not None:
                x = (x * cos + _swap_halves(x) * sin) * scale
            _fold_out(nat, x, out, d)

    fold_spec = pl.BlockSpec((d, rows, AO), lambda i: (0, i, 0))
    return pl.pallas_call(
        body, name=f"fold_rope{g}", grid=(T // tm,),
        in_specs=[pl.BlockSpec((tm, 3 * AO), lambda i: (i, C_QKV // (3 * AO) + g)),
                  pl.BlockSpec((tm, LANES), lambda i: (i, 0)), pl.BlockSpec((tm, LANES), lambda i: (i, 0))],
        out_specs=[fold_spec] * 3,
        out_shape=[jax.ShapeDtypeStruct((d, T // d, AO), MX)] * 3,
        scratch_shapes=[pltpu.VMEM((AO // LANES, tm, LANES), f32)],
        compiler_params=_cp(("parallel",)),
    )(proj, cos_t, sin_t)


def _attn_masks():
    row = lax.broadcasted_iota(jnp.int32, (BLK, BLK), 0)
    col = lax.broadcasted_iota(jnp.int32, (BLK, BLK), 1)
    return col <= row, col >= row, col < HD


def attn_fwd(qf, kf, vf, g, nb):
    T = qf.shape[0]

    def body(q_ref, k_ref, v_ref, o_ref, l_ref):
        lower, upper, head0 = _attn_masks()

        def step(b, carry):
            r0 = pl.multiple_of(b * BLK, BLK)
            rp = pl.multiple_of(jnp.maximum(b - 1, 0) * BLK, BLK)
            q, kc, vc = q_ref[pl.ds(r0, BLK), :], k_ref[pl.ds(r0, BLK), :], v_ref[pl.ds(r0, BLK), :]
            kp, vp = k_ref[pl.ds(rp, BLK), :], v_ref[pl.ds(rp, BLK), :]
            mask_p = upper & ((b % nb) != 0)
            outs = []
            for hm in (head0, ~head0):
                qh = jnp.where(hm, q, jnp.zeros_like(q))
                sc = jnp.where(lower, _dot(qh, kc, NT), NEG)
                sp = jnp.where(mask_p, _dot(qh, kp, NT), NEG)
                m = jnp.maximum(jnp.max(sc, axis=-1, keepdims=True), jnp.max(sp, axis=-1, keepdims=True))
                pc, pp = jnp.exp(sc - m), jnp.exp(sp - m)
                l = jnp.sum(pc, axis=-1, keepdims=True) + jnp.sum(pp, axis=-1, keepdims=True)
                o = (_dot(pc.astype(MX), vc) + _dot(pp.astype(MX), vp)) / l
                outs.append((o, m + jnp.log(l)))
            o_ref[pl.ds(r0, BLK), :] = jnp.where(head0, outs[0][0], outs[1][0])
            l_ref[pl.ds(r0, BLK), :] = jnp.where(head0, outs[0][1], outs[1][1])
            return carry

        lax.fori_loop(0, T // BLK, step, 0)

    spec = pl.BlockSpec((T, LANES), lambda j: (0, j))
    return pl.pallas_call(
        body, name=f"attn_fwd{g}", grid=(AO // LANES,),
        in_specs=[spec] * 3, out_specs=[spec] * 2,
        out_shape=[jax.ShapeDtypeStruct((T, AO), f32)] * 2,
        compiler_params=_cp(("parallel",), 56),
    )(qf, kf, vf)


def _group_weights(lses):
    m = jnp.maximum(jnp.maximum(lses[0], lses[1]), lses[2])
    e = [jnp.exp(l - m) for l in lses]
    inv = 1.0 / (e[0] + e[1] + e[2])
    return [x * inv for x in e]


def _fold_specs(T, tm):
    specs = []
    for _, d in GROUPS:
        specs.append(pl.BlockSpec((d, tm // d, AO), lambda i: (0, i, 0)))
    return specs


def combine_fwd(os_, lses):
    T = os_[0].shape[0] * os_[0].shape[1]
    tm = TM_FOLD

    def body(o0, o1, o2, l0, l1, l2, y_ref, nat):
        o = [_unfold_in(nat, r, d) for r, (_, d) in zip((o0, o1, o2), GROUPS)]
        ls = [_unfold_in(nat, r, d) for r, (_, d) in zip((l0, l1, l2), GROUPS)]
        w = _group_weights(ls)
        y_ref[...] = (w[0] * o[0] + w[1] * o[1] + w[2] * o[2]).astype(y_ref.dtype)

    specs = _fold_specs(T, tm)
    return pl.pallas_call(
        body, name="combine_fwd", grid=(T // tm,),
        in_specs=specs + specs, out_specs=pl.BlockSpec((tm, AO), lambda i: (i, 0)),
        out_shape=jax.ShapeDtypeStruct((T, AO), MX),
        scratch_shapes=[pltpu.VMEM((AO // LANES, tm, LANES), f32)],
        compiler_params=_cp(("parallel",)),
    )(*os_, *lses)


TM_MIX = 256


def mix_out_fwd(proj, ya, yb, yc, x0, pa, pb, pc, wo, g1, b1):
    T = x0.shape[0]
    tm = min(TM_MIX, T)

    def body(gt, ya_r, yb_r, yc_r, x0_r, pa_r, pb_r, pc_r, wo_r, g_r, b_r, mabc, m_o, r1_o, x1_o):
        ma = _dot(ya_r[...], pa_r[...])
        ybv = yb_r[...]
        mb = jnp.concatenate([_dot(ybv, pb_r[k]) for k in range(NCHIP)], axis=1)
        mc = _dot(yc_r[...], pc_r[...])
        m = jnp.zeros((tm, D), f32)
        for j, mm in enumerate((ma, mb, mc)):
            mabc[:, j * D:(j + 1) * D] = mm.astype(mabc.dtype)
            m = m + _sigmoid(gt[:, j * D:(j + 1) * D].astype(f32)) * mm
        mb16 = m.astype(MX)
        m_o[...] = mb16
        r1 = ALPHA * x0_r[...] + _dot(mb16, wo_r[...])
        r1_o[...] = r1
        xhat, _ = _ln_stats(r1)
        x1_o[...] = xhat * g_r[...] + b_r[...]

    full = lambda shape: pl.BlockSpec(shape, lambda i: (0,) * len(shape))
    tile = lambda w: pl.BlockSpec((tm, w), lambda i: (i, 0))
    return pl.pallas_call(
        body, name="mix_out_fwd", grid=(T // tm,),
        in_specs=[tile(3 * D), tile(D), tile(AO), tile(D), tile(D), full((D, D)), full((NCHIP, AO, D // NCHIP)),
                  full((D, D)), full((D, D)), full((1, D)), full((1, D))],
        out_specs=[tile(3 * D), tile(D), tile(D), tile(D)],
        out_shape=[jax.ShapeDtypeStruct((T, 3 * D), MX), jax.ShapeDtypeStruct((T, D), MX),
                   jax.ShapeDtypeStruct((T, D), f32), jax.ShapeDtypeStruct((T, D), f32)],
        compiler_params=_cp(("parallel",), 56),
    )(proj, ya, yb, yc, x0, pa, pb, pc, wo, g1, b1)


TM_FF = 512


def ffn_up_fwd(x1, wg, wu):
    T = x1.shape[0]
    tm = min(TM_FF, T)

    def body(x_r, wg_r, wu_r, g_o, u_o, h_o, xb):
        @pl.when(pl.program_id(1) == 0)
        def _():
            xb[...] = x_r[...].astype(MX)
        gate = _dot(xb[...], wg_r[0])
        up = _dot(xb[...], wu_r[0])
        g_o[0] = gate.astype(g_o.dtype)
        u_o[0] = up.astype(u_o.dtype)
        h_o[0] = (gate * _sigmoid(gate) * up).astype(h_o.dtype)

    wspec = pl.BlockSpec((1, D, FB), lambda i, k: (k, 0, 0))
    ospec = pl.BlockSpec((1, tm, FB), lambda i, k: (k, i, 0))
    return pl.pallas_call(
        body, name="ffn_up_fwd", grid=(T // tm, NCHIP),
        in_specs=[pl.BlockSpec((tm, D), lambda i, k: (i, 0)), wspec, wspec],
        out_specs=[ospec] * 3,
        out_shape=[jax.ShapeDtypeStruct((NCHIP, T, FB), ACT)] * 2 + [jax.ShapeDtypeStruct((NCHIP, T, FB), MX)],
        scratch_shapes=[pltpu.VMEM((tm, D), MX)],
        compiler_params=_cp(("parallel", "arbitrary")),
    )(x1, wg, wu)


def ffn_down_fwd(hh, wd, x1, g2, b2):
    T = x1.shape[0]
    tm = min(TM_FF, T)

    def body(h_r, w_r, x_r, g_r, b_r, r2_o, x2_o):
        r2 = ALPHA * x_r[...]
        for k in range(NCHIP):
            r2 = r2 + _dot(h_r[k], w_r[k])
        r2_o[...] = r2
        xhat, _ = _ln_stats(r2)
        x2_o[...] = xhat * g_r[...] + b_r[...]

    tile = pl.BlockSpec((tm, D), lambda i: (i, 0))
    vec = pl.BlockSpec((1, D), lambda i: (0, 0))
    return pl.pallas_call(
        body, name="ffn_down_fwd", grid=(T // tm,),
        in_specs=[pl.BlockSpec((NCHIP, tm, FB), lambda i: (0, i, 0)), pl.BlockSpec((NCHIP, FB, D), lambda i: (0, 0, 0)),
                  tile, vec, vec],
        out_specs=[tile, tile], out_shape=[jax.ShapeDtypeStruct((T, D), f32)] * 2,
        compiler_params=_cp(("parallel",)),
    )(hh, wd, x1, g2, b2)


def loss_grad(y, tgt):
    T = y.shape[0]
    tm = min(512, T)

    def body(y_r, t_r, l_o, dy_o):
        e = y_r[...] - t_r[...]
        dy_o[...] = e * (1.0 / D)

        @pl.when(pl.program_id(0) == 0)
        def _():
            l_o[...] = jnp.zeros_like(l_o)
        l_o[...] += (0.5 / D) * jnp.sum(e * e)

    tile = pl.BlockSpec((tm, D), lambda i: (i, 0))
    return pl.pallas_call(
        body, name="loss_grad", grid=(T // tm,),
        in_specs=[tile, tile], out_specs=[pl.BlockSpec((8, LANES), lambda i: (0, 0)), tile],
        out_shape=[jax.ShapeDtypeStruct((8, LANES), f32), jax.ShapeDtypeStruct((T, D), f32)],
        compiler_params=_cp(("arbitrary",)),
    )(y, tgt)


def ffn_down_bwd(dx2, r2, g2, wd, gate, up):
    T = dx2.shape[0]
    tm = min(TM_FF, T)

    def body(dx_r, r_r, g_r, w_r, ga_r, up_r, dr_o, dg_o, du_o, dlg_o, dlb_o, drb):
        i, k = pl.program_id(0), pl.program_id(1)

        @pl.when(k == 0)
        def _():
            xhat, rstd = _ln_stats(r_r[...])
            dx = dx_r[...]
            _acc_rows(dlg_o, i == 0, dx * xhat)
            _acc_rows(dlb_o, i == 0, dx)
            dr = _ln_bwd(dx, xhat, rstd, g_r[...])
            dr_o[...] = dr
            drb[...] = dr.astype(MX)

        dhh = _dot(drb[...], w_r[0], NT)
        gate_v, up_v = ga_r[0].astype(f32), up_r[0].astype(f32)
        sg = _sigmoid(gate_v)
        dg_o[0] = (dhh * up_v * sg * (1.0 + gate_v * (1.0 - sg))).astype(dg_o.dtype)
        du_o[0] = (dhh * gate_v * sg).astype(du_o.dtype)

    tile = pl.BlockSpec((tm, D), lambda i, k: (i, 0))
    vec = pl.BlockSpec((1, D), lambda i, k: (0, 0))
    blk = pl.BlockSpec((1, tm, FB), lambda i, k: (k, i, 0))
    return pl.pallas_call(
        body, name="ffn_down_bwd", grid=(T // tm, NCHIP),
        in_specs=[tile, tile, vec, pl.BlockSpec((1, FB, D), lambda i, k: (k, 0, 0)), blk, blk],
        out_specs=[tile, blk, blk, vec, vec],
        out_shape=[jax.ShapeDtypeStruct((T, D), f32)] + [jax.ShapeDtypeStruct((NCHIP, T, FB), MX)] * 2
        + [jax.ShapeDtypeStruct((1, D), f32)] * 2,
        scratch_shapes=[pltpu.VMEM((tm, D), MX)],
        compiler_params=_cp(("arbitrary", "arbitrary")),
    )(dx2, r2, g2, wd, gate, up)


def ffn_up_bwd(dr2, dgate, dup, wg, wu, r1, g1):
    T = dr2.shape[0]
    tm = min(TM_FF, T)

    def body(dr2_r, dg_r, du_r, wg_r, wu_r, r1_r, g_r, dr1_o, dlg_o, dlb_o, acc):
        i, k = pl.program_id(0), pl.program_id(1)

        @pl.when(k == 0)
        def _():
            acc[...] = ALPHA * dr2_r[...]
        acc[...] += _dot(dg_r[0], wg_r[0], NT) + _dot(du_r[0], wu_r[0], NT)

        @pl.when(k == NCHIP - 1)
        def _():
            dx = acc[...]
            xhat, rstd = _ln_stats(r1_r[...])
            _acc_rows(dlg_o, i == 0, dx * xhat)
            _acc_rows(dlb_o, i == 0, dx)
            dr1_o[...] = _ln_bwd(dx, xhat, rstd, g_r[...])

    tile = pl.BlockSpec((tm, D), lambda i, k: (i, 0))
    vec = pl.BlockSpec((1, D), lambda i, k: (0, 0))
    blk = pl.BlockSpec((1, tm, FB), lambda i, k: (k, i, 0))
    wspec = pl.BlockSpec((1, D, FB), lambda i, k: (k, 0, 0))
    return pl.pallas_call(
        body, name="ffn_up_bwd", grid=(T // tm, NCHIP),
        in_specs=[tile, blk, blk, wspec, wspec, tile, vec],
        out_specs=[tile, vec, vec],
        out_shape=[jax.ShapeDtypeStruct((T, D), f32)] + [jax.ShapeDtypeStruct((1, D), f32)] * 2,
        scratch_shapes=[pltpu.VMEM((tm, D), f32)],
        compiler_params=_cp(("arbitrary", "arbitrary")),
    )(dr2, dgate, dup, wg, wu, r1, g1)


def mix_out_bwd(dr1, proj, mabc, wo, pa, pb, pc):
    T = dr1.shape[0]
    tm = min(TM_MIX, T)

    def body(dr_r, gt, mabc_r, wo_r, pa_r, pb_r, pc_r, dmabc_o, dgt_o, dya_o, dyb_o, dyc_o):
        dm = _dot(dr_r[...].astype(MX), wo_r[...], NT)
        dmx = []
        for j in range(3):
            s = _sigmoid(gt[:, j * D:(j + 1) * D].astype(f32))
            v = (dm * s).astype(MX)
            dmx.append(v)
            dmabc_o[:, j * D:(j + 1) * D] = v
            dgt_o[:, j * D:(j + 1) * D] = (dm * mabc_r[:, j * D:(j + 1) * D].astype(f32) * s * (1.0 - s)).astype(dgt_o.dtype)
        dya_o[...] = _dot(dmx[0], pa_r[...], NT)
        dyb = jnp.zeros((tm, AO), f32)
        for k in range(NCHIP):
            dyb = dyb + _dot(dmx[1][:, k * (D // NCHIP):(k + 1) * (D // NCHIP)], pb_r[k], NT)
        dyb_o[...] = dyb
        dyc_o[...] = _dot(dmx[2], pc_r[...], NT)

    full = lambda shape: pl.BlockSpec(shape, lambda i: (0,) * len(shape))
    tile = lambda w: pl.BlockSpec((tm, w), lambda i: (i, 0))
    return pl.pallas_call(
        body, name="mix_out_bwd", grid=(T // tm,),
        in_specs=[tile(D), tile(3 * D), tile(3 * D), full((D, D)), full((D, D)), full((NCHIP, AO, D // NCHIP)), full((D, D))],
        out_specs=[tile(3 * D), tile(3 * D), tile(D), tile(AO), tile(D)],
        out_shape=[jax.ShapeDtypeStruct((T, 3 * D), MX), jax.ShapeDtypeStruct((T, 3 * D), MX),
                   jax.ShapeDtypeStruct((T, D), f32), jax.ShapeDtypeStruct((T, AO), f32), jax.ShapeDtypeStruct((T, D), f32)],
        compiler_params=_cp(("parallel",), 56),
    )(dr1, proj, mabc, wo, pa, pb, pc)


def tn_matmul(name, a, b, a_spec, b_spec, out_shape, out_spec, grid):
    nt = len(grid) - 1

    def body(a_r, b_r, o_r):
        @pl.when(pl.program_id(nt) == 0)
        def _():
            o_r[...] = jnp.zeros_like(o_r)
        av = a_r[...].reshape(a_r.shape[-2:]).astype(MX)
        bv = b_r[...].reshape(b_r.shape[-2:]).astype(MX)
        o_r[...] += _dot(av, bv, TN).reshape(o_r.shape)

    return pl.pallas_call(
        body, name=name, grid=grid, in_specs=[a_spec, b_spec], out_specs=out_spec,
        out_shape=jax.ShapeDtypeStruct(out_shape, f32),
        compiler_params=_cp(("parallel",) * nt + ("arbitrary",), 56),
    )(a, b)


def attn_pre_bwd(dyb, os_, lses):
    T = dyb.shape[0]
    tm = TM_FOLD

    def body(dy_r, o0, o1, o2, l0, l1, l2, ones_r, d0, d1, d2, f0, f1, f2, nat):
        o = [_unfold_in(nat, r, d) for r, (_, d) in zip((o0, o1, o2), GROUPS)]
        ls = [_unfold_in(nat, r, d) for r, (_, d) in zip((l0, l1, l2), GROUPS)]
        w = _group_weights(ls)
        dy = dy_r[...]
        t = dy * (w[0] * o[0] + w[1] * o[1] + w[2] * o[2])
        hi = t.astype(MX)
        lo = (t - hi.astype(f32)).astype(MX)
        c = _dot(hi, ones_r[...]) + _dot(lo, ones_r[...])
        for wg, do_o, df_o, (_, d) in zip(w, (d0, d1, d2), (f0, f1, f2), GROUPS):
            _fold_out(nat, wg * dy, do_o, d)
            _fold_out(nat, -wg * c, df_o, d)

    specs = _fold_specs(T, tm)
    return pl.pallas_call(
        body, name="attn_pre_bwd", grid=(T // tm,),
        in_specs=[pl.BlockSpec((tm, AO), lambda i: (i, 0))] + specs + specs + [pl.BlockSpec((AO, AO), lambda i: (0, 0))],
        out_specs=specs + specs,
        out_shape=[jax.ShapeDtypeStruct((d, T // d, AO), MX) for _, d in GROUPS]
        + [jax.ShapeDtypeStruct((d, T // d, AO), f32) for _, d in GROUPS],
        scratch_shapes=[pltpu.VMEM((AO // LANES, tm, LANES), f32)],
        compiler_params=_cp(("parallel",)),
    )(dyb, *os_, *lses, _head_ones())


def _head_ones():
    i = jnp.arange(AO) // HD
    return (i[:, None] == i[None, :]).astype(MX)


def attn_bwd(qf, kf, vf, dof, lse, df, g, nb):
    T = qf.shape[0]

    def body(q_ref, k_ref, v_ref, do_ref, l_ref, d_ref, dq_ref, dk_ref, dv_ref):
        lower, upper, head0 = _attn_masks()

        def step(b, carry):
            dk_c, dv_c = carry
            r0 = pl.multiple_of(b * BLK, BLK)
            rp = pl.multiple_of(jnp.maximum(b - 1, 0) * BLK, BLK)
            q, kc, vc = q_ref[pl.ds(r0, BLK), :], k_ref[pl.ds(r0, BLK), :], v_ref[pl.ds(r0, BLK), :]
            kp, vp = k_ref[pl.ds(rp, BLK), :], v_ref[pl.ds(rp, BLK), :]
            do, lse_v, d_v = do_ref[pl.ds(r0, BLK), :], l_ref[pl.ds(r0, BLK), :], d_ref[pl.ds(r0, BLK), :]
            mask_p = upper & ((b % nb) != 0)
            dq = []
            dk_new, dv_new, dk_prev, dv_prev = (jnp.zeros((BLK, LANES), f32) for _ in range(4))
            for h, hm in enumerate((head0, ~head0)):
                zero = jnp.zeros_like(q)
                qh, doh = jnp.where(hm, q, zero), jnp.where(hm, do, zero)
                lse_h, d_h = lse_v[:, h * HD:h * HD + 1], d_v[:, h * HD:h * HD + 1]
                pc = jnp.where(lower, jnp.exp(_dot(qh, kc, NT) - lse_h), 0.0)
                pp = jnp.where(mask_p, jnp.exp(_dot(qh, kp, NT) - lse_h), 0.0)
                dsc = (pc * (_dot(doh, vc, NT) + d_h)).astype(MX)
                dsp = (pp * (_dot(doh, vp, NT) + d_h)).astype(MX)
                dq.append(_dot(dsc, kc) + _dot(dsp, kp))
                dk_new = dk_new + _dot(dsc, qh, TN)
                dk_prev = dk_prev + _dot(dsp, qh, TN)
                dv_new = dv_new + _dot(pc.astype(MX), doh, TN)
                dv_prev = dv_prev + _dot(pp.astype(MX), doh, TN)
            dq_ref[pl.ds(r0, BLK), :] = jnp.where(head0, dq[0], dq[1]).astype(dq_ref.dtype)
            dk_ref[pl.ds(rp, BLK), :] = (dk_c + dk_prev).astype(dk_ref.dtype)
            dv_ref[pl.ds(rp, BLK), :] = (dv_c + dv_prev).astype(dv_ref.dtype)
            return dk_new, dv_new

        zero = jnp.zeros((BLK, LANES), f32)
        dk_c, dv_c = lax.fori_loop(0, T // BLK, step, (zero, zero))
        dk_ref[pl.ds(T - BLK, BLK), :] = dk_c.astype(dk_ref.dtype)
        dv_ref[pl.ds(T - BLK, BLK), :] = dv_c.astype(dv_ref.dtype)

    spec = pl.BlockSpec((T, LANES), lambda j: (0, j))
    return pl.pallas_call(
        body, name=f"attn_bwd{g}", grid=(AO // LANES,),
        in_specs=[spec] * 6, out_specs=[spec] * 3,
        out_shape=[jax.ShapeDtypeStruct((T, AO), MX)] * 3,
        compiler_params=_cp(("parallel",), 60),
    )(qf, kf, vf, dof, lse, df)


def unfold_rope_bwd(dqf, dkf, dvf, cos_t, sin_t, g, d):
    T = dqf.shape[0] * dqf.shape[1]
    tm = TM_FOLD

    def body(q_r, k_r, v_r, c_ref, s_ref, o_ref, nat):
        cos, sin = _tile4(c_ref[...]), _tile4(s_ref[...])
        for part, ref, scale in ((0, q_r, HD ** -0.5), (1, k_r, 1.0), (2, v_r, None)):
            x = _unfold_in(nat, ref, d)
            if scale is not None:
                x = (x * cos - _swap_halves(x) * sin) * scale
            o_ref[:, part * AO:(part + 1) * AO] = x.astype(o_ref.dtype)

    fold_spec = pl.BlockSpec((d, tm // d, AO), lambda i: (0, i, 0))
    tab = pl.BlockSpec((tm, LANES), lambda i: (i, 0))
    return pl.pallas_call(
        body, name=f"unfold_rope_bwd{g}", grid=(T // tm,),
        in_specs=[fold_spec] * 3 + [tab, tab],
        out_specs=pl.BlockSpec((tm, 3 * AO), lambda i: (i, 0)),
        out_shape=jax.ShapeDtypeStruct((T, 3 * AO), MX),
        scratch_shapes=[pltpu.VMEM((AO // LANES, tm, LANES), f32)],
        compiler_params=_cp(("parallel",)),
    )(dqf, dkf, dvf, cos_t, sin_t)


def conv_bwd(dya, proj, conv_w):
    T = dya.shape[0]
    tm = TM_AC
    last = T // tm - 1

    def body(dy_r, bch, hprev, dy_next, b_next, cw, d_o, dw_o, zs, ds):
        i = pl.program_id(0)
        pb = bch[...].astype(f32)
        bp, cp, hp = pb[:, :D], pb[:, D:2 * D], pb[:, 2 * D:]
        z = cp * hp
        hz = hprev[:, :D].astype(f32) * hprev[:, D:].astype(f32)
        zs[0:HALO, :] = jnp.where(i > 0, hz, 0.0)
        zs[HALO:HALO + tm, :] = z
        z2, z1 = zs[HALO - 2:HALO - 2 + tm, :], zs[HALO - 1:HALO - 1 + tm, :]
        cv = cw[0:1, :] * z2 + cw[1:2, :] * z1 + cw[2:3, :] * z
        dy = dy_r[...]
        dcv = dy * bp
        ds[0:tm, :] = dcv
        ds[tm:tm + HALO, :] = jnp.where(i < last, dy_next[...] * b_next[...].astype(f32), 0.0)
        dz = cw[2:3, :] * dcv + cw[1:2, :] * ds[1:1 + tm, :] + cw[0:1, :] * ds[2:2 + tm, :]
        d_o[:, :D] = (dy * cv).astype(d_o.dtype)
        d_o[:, D:2 * D] = (dz * hp).astype(d_o.dtype)
        d_o[:, 2 * D:] = (dz * cp).astype(d_o.dtype)

        @pl.when(i == 0)
        def _():
            dw_o[...] = jnp.zeros_like(dw_o)
        dw_o[0:1, :] += jnp.sum(dcv * z2, axis=0, keepdims=True)
        dw_o[1:2, :] += jnp.sum(dcv * z1, axis=0, keepdims=True)
        dw_o[2:3, :] += jnp.sum(dcv * z, axis=0, keepdims=True)

    nh = tm // HALO
    return pl.pallas_call(
        body, name="conv_bwd", grid=(T // tm,),
        in_specs=[pl.BlockSpec((tm, D), lambda i: (i, 0)), pl.BlockSpec((tm, 3 * D), lambda i: (i, 1)),
                  pl.BlockSpec((HALO, 2 * D), lambda i: (jnp.maximum(i * nh - 1, 0), 2)),
                  pl.BlockSpec((HALO, D), lambda i: (jnp.minimum((i + 1) * nh, T // HALO - 1), 0)),
                  pl.BlockSpec((HALO, D), lambda i: (jnp.minimum((i + 1) * nh, T // HALO - 1), 3)),
                  pl.BlockSpec((3, D), lambda i: (0, 0))],
        out_specs=[pl.BlockSpec((tm, 3 * D), lambda i: (i, 0)), pl.BlockSpec((3, D), lambda i: (0, 0))],
        out_shape=[jax.ShapeDtypeStruct((T, 3 * D), MX), jax.ShapeDtypeStruct((3, D), f32)],
        scratch_shapes=[pltpu.VMEM((HALO + tm, D), f32), pltpu.VMEM((tm + HALO, D), f32)],
        compiler_params=_cp(("arbitrary",)),
    )(dya, proj, proj, dya, proj, conv_w)


def gmlp_bwd(dyc, proj, wst, bsx, lg, lb):
    T = dyc.shape[0]
    tm = TM_AC
    last = T // tm - 1

    def body(dy_r, u0, u1, v0, v1, ws, bs, lg_r, lb_r, d_o, dws_o, dbs_o, dlg_o, dlb_o, bacc):
        i = pl.program_id(0)
        up = jnp.concatenate([u0[...], u1[...]], axis=1).astype(f32)
        vp = jnp.concatenate([v0[...], v1[...]], axis=1).astype(f32)
        u, vn, xhat, rstd, sp = _gmlp_fwd(up, vp, ws, bs, lg_r[...], lb_r[...])
        dy = dy_r[...]
        d_o[:, :D] = (dy * sp * _gelu_grad(up)).astype(d_o.dtype)
        dsp = dy * u
        dspb, vnb = dsp.astype(MX), vn.astype(MX)

        @pl.when(i == 0)
        def _():
            dws_o[...] = jnp.zeros_like(dws_o)
            bacc[...] = jnp.zeros_like(bacc)

        rows = []
        for c in range(tm // BLK):
            r = slice(c * BLK, (c + 1) * BLK)
            cols = []
            for g in range(8):
                cs = slice(g * BLK, (g + 1) * BLK)
                dws_o[g] += _dot(dspb[r, cs], vnb[r, cs], NT)
                bacc[g] += dsp[r, cs]
                cols.append(_dot(ws[g], dspb[r, cs], TN))
            rows.append(jnp.concatenate(cols, axis=1))
        dvn = jnp.concatenate(rows, axis=0)
        _acc_rows(dlg_o, i == 0, dvn * xhat)
        _acc_rows(dlb_o, i == 0, dvn)
        d_o[:, D:] = (_ln_bwd(dvn, xhat, rstd, lg_r[...]) * _gelu_grad(vp)).astype(d_o.dtype)

        @pl.when(i == last)
        def _():
            row = lax.broadcasted_iota(jnp.int32, (BLK, BLK), 0)
            col = lax.broadcasted_iota(jnp.int32, (BLK, BLK), 1)
            ones = jnp.ones((8, BLK), MX)
            for g in range(8):
                dws_o[g] = jnp.where(col <= row, dws_o[g], 0.0)
                a = bacc[g]
                hi = a.astype(MX)
                lo = (a - hi.astype(f32)).astype(MX)
                dbs_o[g:g + 1, :] = (_dot(ones, hi, NT) + _dot(ones, lo, NT))[0:1, :]

    full = lambda shape: pl.BlockSpec(shape, lambda i: (0,) * len(shape))
    return pl.pallas_call(
        body, name="gmlp_bwd", grid=(T // tm,),
        in_specs=[pl.BlockSpec((tm, D), lambda i: (i, 0)), *_uv_specs(), full((8, BLK, BLK)), full((8, BLK, BLK)),
                  full((1, D)), full((1, D))],
        out_specs=[pl.BlockSpec((tm, 2 * D), lambda i: (i, 0)), full((8, BLK, BLK)), full((8, BLK)), full((1, D)), full((1, D))],
        out_shape=[jax.ShapeDtypeStruct((T, 2 * D), MX), jax.ShapeDtypeStruct((8, BLK, BLK), f32),
                   jax.ShapeDtypeStruct((8, BLK), f32), jax.ShapeDtypeStruct((1, D), f32), jax.ShapeDtypeStruct((1, D), f32)],
        scratch_shapes=[pltpu.VMEM((8, BLK, BLK), f32)],
        compiler_params=_cp(("arbitrary",)),
    )(dyc, proj, proj, proj, proj, wst, bsx, lg, lb)


PART_TILES = (6, 6, 3, 3, 3, 4)
PART_START = (0, 6, 12, 15, 18, 21)
TJ = 512


def _part_specs(tm, rows_axis):
    specs = []
    for n, s in zip(PART_TILES, PART_START):
        def imap(*idx, n=n, s=s):
            i, j = idx[rows_axis], idx[1 - rows_axis]
            return (i, jnp.clip(j - s, 0, n - 1))
        specs.append(pl.BlockSpec((tm, TJ), imap))
    return specs


def _select_part(j, refs):
    out = refs[0][...]
    for r, s in zip(refs[1:], PART_START[1:]):
        out = jnp.where(j >= s, r[...], out)
    return out


def dx_in(dr1, parts, w):
    T = dr1.shape[0]
    tm = min(1024, T)

    def body(dr_r, p0, p1, p2, p3, p4, p5, w_r, o_r):
        j = pl.program_id(1)

        @pl.when(j == 0)
        def _():
            o_r[...] = ALPHA * dr_r[...]
        o_r[...] += _dot(_select_part(j, (p0, p1, p2, p3, p4, p5)), w_r[...], NT)

    return pl.pallas_call(
        body, name="dx_in", grid=(T // tm, NIN // TJ),
        in_specs=[pl.BlockSpec((tm, D), lambda i, j: (i, 0))] + _part_specs(tm, 0) + [pl.BlockSpec((D, TJ), lambda i, j: (0, j))],
        out_specs=pl.BlockSpec((tm, D), lambda i, j: (i, 0)),
        out_shape=jax.ShapeDtypeStruct((T, D), f32),
        compiler_params=_cp(("parallel", "arbitrary"), 56),
    )(dr1, *parts, w)


def dw_in(x0, parts):
    T = x0.shape[0]
    tk = min(512, T)

    def body(x_r, p0, p1, p2, p3, p4, p5, o_r):
        j, t = pl.program_id(0), pl.program_id(1)

        @pl.when(t == 0)
        def _():
            o_r[...] = jnp.zeros_like(o_r)
        o_r[...] += _dot(x_r[...].astype(MX), _select_part(j, (p0, p1, p2, p3, p4, p5)), TN)

    return pl.pallas_call(
        body, name="dw_in", grid=(NIN // TJ, T // tk),
        in_specs=[pl.BlockSpec((tk, D), lambda j, t: (t, 0))] + _part_specs(tk, 1),
        out_specs=pl.BlockSpec((D, TJ), lambda j, t: (0, j)),
        out_shape=jax.ShapeDtypeStruct((D, NIN), f32),
        compiler_params=_cp(("parallel", "arbitrary")),
    )(x0, *parts)


def rope_tables(positions):
    half = HD // 2
    inv_freq = ROPE_THETA ** (-jnp.arange(half, dtype=f32) / half)
    ang = positions.astype(f32)[:, None] * inv_freq
    cos, sin = jnp.cos(ang), jnp.sin(ang)
    return jnp.tile(cos, (1, LANES // half)), jnp.tile(jnp.concatenate([-sin, sin], axis=1), (1, LANES // HD))


def _flat(a):
    return a.reshape(a.shape[0] * a.shape[1], a.shape[2])


def layer_fwd(x0, W, cos_t, sin_t):
    T = x0.shape[0]
    proj = mm_in(x0, W["w_in"])
    ya, yc = mix_ac_fwd(proj, W["conv_w"], W["wst"], W["bsx"], W["gmlp_ln_g"], W["gmlp_ln_b"])
    folded, os_, lses = [], [], []
    for g, (_, d) in enumerate(GROUPS):
        qf, kf, vf = fold_rope(proj, cos_t, sin_t, g, d)
        o, lse = attn_fwd(_flat(qf), _flat(kf), _flat(vf), g, T // d // BLK)
        folded.append((qf, kf, vf))
        os_.append(o.reshape(d, T // d, AO))
        lses.append(lse.reshape(d, T // d, AO))
    yb = combine_fwd(os_, lses)
    mabc, m, r1, x1 = mix_out_fwd(proj, ya, yb, yc, x0, W["p_a"], W["p_b"], W["p_c"], W["w_o"], W["ln1_g"], W["ln1_b"])
    gate, up, hh = ffn_up_fwd(x1, W["w_gate"], W["w_up"])
    r2, x2 = ffn_down_fwd(hh, W["w_down"], x1, W["ln2_g"], W["ln2_b"])
    saved = dict(x0=x0, proj=proj, ya=ya, yb=yb, yc=yc, folded=folded, os=os_, lses=lses, mabc=mabc, m=m, r1=r1,
                 x1=x1, gate=gate, up=up, hh=hh, r2=r2)
    return x2, saved


def layer_bwd(dx2, S, W, cos_t, sin_t):
    T = dx2.shape[0]
    tk = min(512, T)
    G = {}
    dr2, dgate, dup, G["ln2_g"], G["ln2_b"] = ffn_down_bwd(dx2, S["r2"], W["ln2_g"], W["w_down"], S["gate"], S["up"])
    blk_a = pl.BlockSpec((1, tk, FB), lambda k, t: (k, t, 0))
    row_b = pl.BlockSpec((tk, D), lambda k, t: (t, 0))
    G["w_down"] = tn_matmul("dw_down", S["hh"], dr2, blk_a, row_b, (NCHIP, FB, D),
                            pl.BlockSpec((1, FB, D), lambda k, t: (k, 0, 0)), (NCHIP, T // tk))
    for nm, dv in (("w_gate", dgate), ("w_up", dup)):
        G[nm] = tn_matmul("d" + nm, S["x1"], dv, row_b, blk_a, (NCHIP, D, FB),
                          pl.BlockSpec((1, D, FB), lambda k, t: (k, 0, 0)), (NCHIP, T // tk))
    dr1, G["ln1_g"], G["ln1_b"] = ffn_up_bwd(dr2, dgate, dup, W["w_gate"], W["w_up"], S["r1"], W["ln1_g"])
    dmabc, dgates, dya, dyb, dyc = mix_out_bwd(dr1, S["proj"], S["mabc"], W["w_o"], W["p_a"], W["p_b"], W["p_c"])
    one = (1, T // tk)
    full_o = pl.BlockSpec((D, D), lambda k, t: (0, 0))
    G["w_o"] = tn_matmul("dw_o", S["m"], dr1, row_b, row_b, (D, D), full_o, one)
    G["p_a"] = tn_matmul("dp_a", S["ya"], dmabc, row_b, pl.BlockSpec((tk, D), lambda k, t: (t, 0)), (D, D), full_o, one)
    G["p_c"] = tn_matmul("dp_c", S["yc"], dmabc, row_b, pl.BlockSpec((tk, D), lambda k, t: (t, 2)), (D, D), full_o, one)
    G["p_b"] = tn_matmul("dp_b", S["yb"], dmabc, pl.BlockSpec((tk, AO), lambda k, t: (t, 0)),
                         pl.BlockSpec((tk, D // NCHIP), lambda k, t: (t, NCHIP + k)), (NCHIP, AO, D // NCHIP),
                         pl.BlockSpec((1, AO, D // NCHIP), lambda k, t: (k, 0, 0)), (NCHIP, T // tk))
    dbch, G["conv_w"] = conv_bwd(dya, S["proj"], W["conv_w"])
    duv, G["w_s"], G["b_s"], G["gmlp_ln_g"], G["gmlp_ln_b"] = gmlp_bwd(
        dyc, S["proj"], W["wst"], W["bsx"], W["gmlp_ln_g"], W["gmlp_ln_b"])
    pre = attn_pre_bwd(dyb, S["os"], S["lses"])
    dqkv = []
    for g, (_, d) in enumerate(GROUPS):
        qf, kf, vf = S["folded"][g]
        dqf, dkf, dvf = attn_bwd(_flat(qf), _flat(kf), _flat(vf), _flat(pre[g]), _flat(S["lses"][g]), _flat(pre[3 + g]),
                                 g, T // d // BLK)
        shp = (d, T // d, AO)
        dqkv.append(unfold_rope_bwd(dqf.reshape(shp), dkf.reshape(shp), dvf.reshape(shp), cos_t, sin_t, g, d))
    parts = (dgates, dbch, *dqkv, duv)
    G["w_in"] = dw_in(S["x0"], parts)
    dx0 = dx_in(dr1, parts, W["w_in"])
    return dx0, G


def prep_layer_weights(Wl):
    W = dict(Wl)
    tril = jnp.tril(jnp.ones((BLK, BLK), f32))
    W["wst"] = (Wl["w_s"] * tril[None]).astype(MX)
    W["bsx"] = jnp.broadcast_to(Wl["b_s"][:, :, None], (8, BLK, BLK))
    for n in ("gmlp_ln_g", "gmlp_ln_b", "ln1_g", "ln1_b", "ln2_g", "ln2_b"):
        W[n] = Wl[n].reshape(1, D)
    return W


def local_step(x, positions, target, layers):
    cos_t, sin_t = rope_tables(positions)
    Ws = [prep_layer_weights(Wl) for Wl in layers]
    saved = []
    h = x
    for W in Ws:
        h, S = layer_fwd(h, W, cos_t, sin_t)
        saved.append(S)
    lsum, dh = loss_grad(h, target)
    grads = [None] * len(Ws)
    for l in reversed(range(len(Ws))):
        dh, grads[l] = layer_bwd(dh, saved[l], Ws[l], cos_t, sin_t)
    return lsum, dh, grads


MESH = pl.DeviceIdType.MESH
ANY = pl.BlockSpec(memory_space=pl.ANY)
BIG = ("w_in", "w_gate", "w_up", "w_down", "p_a", "p_b", "p_c", "w_o")
NBIG = len(BIG)


def _place():
    x, y, c = lax.axis_index("x"), lax.axis_index("y"), lax.axis_index("c")
    return x, y, c, 2 * x + y


def _rcopy(src, dst, send, recv, dev):
    return pltpu.make_async_remote_copy(src_ref=src, dst_ref=dst, send_sem=send, recv_sem=recv, device_id=dev,
                                        device_id_type=MESH)


def _cols(ref, k, width):
    start = k * width if isinstance(k, int) else pl.multiple_of(k * width, LANES)
    return ref.at[:, pl.ds(start, width)]


def gather_weights(shards):
    n = len(shards)

    def body(*refs):
        srcs, dsts = refs[:n], refs[n:2 * n]
        send, recv, loc = refs[2 * n:]
        x, y, c, k = _place()
        sib = (x, y, 1 - c)
        chips = [(1 - x, y), (x, 1 - y), (1 - x, 1 - y)]

        def slot(a, layer, pos):
            if a == 0:
                return _cols(dsts[0].at[layer], pos, WIN_SHARD)
            return dsts[a].at[layer, pos]

        local = []
        for a in range(n):
            for layer in range(DEPTH):
                local.append(pltpu.make_async_copy(srcs[a].at[layer], slot(a, layer, k), loc.at[a, layer]))
                local[-1].start()
        sends = []
        for a in range(n):
            for j, (cx, cy) in enumerate(chips):
                sends.append(_rcopy(srcs[a].at[c], slot(a, c, k), send.at[a, j], recv.at[a, j], (cx, cy, c)))
                sends[-1].start()
        for a in range(n):
            for j, (cx, cy) in enumerate(chips):
                landed = slot(a, c, 2 * cx + cy)
                _rcopy(landed, landed, send.at[a, j], recv.at[a, j], (cx, cy, c)).wait_recv()
                sends.append(_rcopy(landed, landed, send.at[a, 3 + j], recv.at[a, 3 + j], sib))
                sends[-1].start()
        for a in range(n):
            for j, (cx, cy) in enumerate(chips):
                passed = slot(a, 1 - c, 2 * cx + cy)
                _rcopy(passed, passed, send.at[a, 3 + j], recv.at[a, 3 + j], sib).wait_recv()
        for cp in sends:
            cp.wait_send()
        for cp in local:
            cp.wait()

    outs = [jax.ShapeDtypeStruct((DEPTH, D, NIN), shards[0].dtype)]
    outs += [jax.ShapeDtypeStruct((DEPTH, NCHIP) + s.shape[1:], s.dtype) for s in shards[1:]]
    return pl.pallas_call(
        body, name="gather_weights", in_specs=[ANY] * n, out_specs=[ANY] * n, out_shape=outs,
        scratch_shapes=[pltpu.SemaphoreType.DMA((n, 6)), pltpu.SemaphoreType.DMA((n, 6)), pltpu.SemaphoreType.DMA((n, DEPTH))],
    )(*shards)


def _half(ref, h, a):
    rows = ref.shape[-2] // 2
    start = pl.multiple_of(h * rows, 16)
    if a == 0:
        return ref.at[pl.ds(start, rows), :]
    return ref.at[:, pl.ds(start, rows), :]


def rs_pair(l, grads):
    def body(*refs):
        g, mine, theirs = refs[:NBIG], refs[NBIG:2 * NBIG], refs[2 * NBIG:3 * NBIG]
        send, recv, loc = refs[3 * NBIG:]
        x, y, c, _ = _place()
        sib = (x, y, 1 - c)
        cps = []
        for a in range(NBIG):
            cps.append(pltpu.make_async_copy(_half(g[a], c, a), mine[a], loc.at[a]))
            cps[-1].start()
            cps.append(_rcopy(_half(g[a], 1 - c, a), theirs[a], send.at[a], recv.at[a], sib))
            cps[-1].start()
        for cp in cps:
            cp.wait()

    def hshape(s, a):
        return (s[0] // 2, s[1]) if a == 0 else (s[0], s[1] // 2, s[2])

    outs = [jax.ShapeDtypeStruct(hshape(g.shape, a), g.dtype) for a, g in enumerate(grads)]
    res = pl.pallas_call(
        body, name=f"rs_pair{l}", in_specs=[ANY] * NBIG, out_specs=[ANY] * (2 * NBIG), out_shape=outs + outs,
        scratch_shapes=[pltpu.SemaphoreType.DMA((NBIG,))] * 3,
    )(*grads)
    return res[:NBIG], res[NBIG:]


def rs_chips(l, sums):
    def body(*refs):
        s, land = refs[:NBIG], refs[NBIG:2 * NBIG]
        send, recv, loc = refs[2 * NBIG:]
        x, y, c, me = _place()
        for k in range(NCHIP):
            for a in range(NBIG):
                piece = _cols(s[a], k, WIN_SHARD) if a == 0 else s[a].at[k]

                @pl.when(me == k)
                def _():
                    pltpu.make_async_copy(piece, land[a].at[k], loc.at[a]).start()

                @pl.when(me != k)
                def _():
                    _rcopy(piece, land[a].at[me], send.at[a, k], recv.at[a, me], (k // 2, k % 2, c)).start()
        for k in range(NCHIP):
            for a in range(NBIG):
                piece = _cols(s[a], k, WIN_SHARD) if a == 0 else s[a].at[k]

                @pl.when(me == k)
                def _():
                    pltpu.make_async_copy(piece, land[a].at[k], loc.at[a]).wait()

                @pl.when(me != k)
                def _():
                    cp = _rcopy(piece, land[a].at[k], send.at[a, k], recv.at[a, k], (k // 2, k % 2, c))
                    cp.wait_send()
                    cp.wait_recv()

    def pshape(s, a):
        return (NCHIP, s[0], WIN_SHARD) if a == 0 else s

    outs = [jax.ShapeDtypeStruct(pshape(v.shape, a), v.dtype) for a, v in enumerate(sums)]
    return pl.pallas_call(
        body, name=f"rs_chips{l}", in_specs=[ANY] * NBIG, out_specs=[ANY] * NBIG, out_shape=outs,
        scratch_shapes=[pltpu.SemaphoreType.DMA((NBIG, NCHIP)), pltpu.SemaphoreType.DMA((NBIG, NCHIP)),
                        pltpu.SemaphoreType.DMA((NBIG,))],
    )(*sums)


def rs_join(l, halves):
    def body(*refs):
        h, full = refs[:NBIG], refs[NBIG:2 * NBIG]
        send, recv, loc = refs[2 * NBIG:]
        x, y, c, _ = _place()
        sib = (x, y, 1 - c)
        cps = []
        for a in range(NBIG):
            rows = h[a].shape[0]
            mine = full[a].at[pl.ds(pl.multiple_of(c * rows, 16), rows), :]
            cps.append(pltpu.make_async_copy(h[a], mine, loc.at[a]))
            cps[-1].start()
            cps.append(_rcopy(h[a], mine, send.at[a], recv.at[a], sib))
            cps[-1].start()
        for cp in cps:
            cp.wait()

    outs = [jax.ShapeDtypeStruct((2 * v.shape[0], v.shape[1]), v.dtype) for v in halves]
    return pl.pallas_call(
        body, name=f"rs_join{l}", in_specs=[ANY] * NBIG, out_specs=[ANY] * NBIG, out_shape=outs,
        scratch_shapes=[pltpu.SemaphoreType.DMA((NBIG,))] * 3,
    )(*halves)


def _row_tile(rows, cols, itemsize=4, target=2 << 20):
    best = 8
    for t in range(8, rows + 1, 8):
        if rows % t == 0 and t * cols * itemsize <= target:
            best = t
    return best


def add_n(name, terms):
    stacked = not isinstance(terms, (list, tuple))
    shape = terms.shape[1:] if stacked else terms[0].shape
    cols = shape[-1]
    rows = math.prod(shape[:-1])
    tr = _row_tile(rows, cols)

    def body(*refs):
        o = refs[-1]
        if stacked:
            acc = refs[0][0]
            for j in range(1, refs[0].shape[0]):
                acc = acc + refs[0][j]
        else:
            acc = refs[0][...]
            for r in refs[1:-1]:
                acc = acc + r[...]
        o[...] = acc

    if stacked:
        n = terms.shape[0]
        args = [terms.reshape(n, rows, cols)]
        in_specs = [pl.BlockSpec((n, tr, cols), lambda i: (0, i, 0))]
    else:
        args = [t.reshape(rows, cols) for t in terms]
        in_specs = [pl.BlockSpec((tr, cols), lambda i: (i, 0))] * len(args)
    out = pl.pallas_call(
        body, name=name, grid=(rows // tr,), in_specs=in_specs, out_specs=pl.BlockSpec((tr, cols), lambda i: (i, 0)),
        out_shape=jax.ShapeDtypeStruct((rows, cols), f32), compiler_params=_cp(("parallel",)),
    )(*args)
    return out.reshape(shape)


def reduce_scatter_layer(l, G):
    grads = [G[n] if G[n].ndim == 3 or n == "w_in" else G[n].reshape(NCHIP, D // NCHIP, D) for n in BIG]
    mine, theirs = rs_pair(l, grads)
    sums = [add_n(f"rs_add_pair{l}_{n}", [m, t]) for n, m, t in zip(BIG, mine, theirs)]
    landed = rs_chips(l, sums)
    halves = [add_n(f"rs_add_chips{l}_{n}", v) for n, v in zip(BIG, landed)]
    return dict(zip(BIG, rs_join(l, halves)))


NDEV = 8


def allreduce_small(pack):
    rows = pack.shape[0]

    def body(p_ref, o_ref, buf, send, recv):
        x, y, c, _ = _place()
        me = 4 * x + 2 * y + c
        buf[me] = p_ref[...]
        cps = []
        for r in range(1, NDEV):
            peer = (x ^ (r >> 2), y ^ ((r >> 1) & 1), c ^ (r & 1))
            cps.append(_rcopy(p_ref, buf.at[me], send.at[r - 1], recv.at[r - 1], peer))
            cps[-1].start()
        for r in range(1, NDEV):
            cps[r - 1].wait_send()
            src = 4 * (x ^ (r >> 2)) + 2 * (y ^ ((r >> 1) & 1)) + (c ^ (r & 1))
            _rcopy(p_ref, buf.at[src], send.at[r - 1], recv.at[r - 1], (x, y, c)).wait_recv()
        acc = buf[0]
        for d in range(1, NDEV):
            acc = acc + buf[d]
        o_ref[...] = acc

    vm = pl.BlockSpec(memory_space=pltpu.VMEM)
    return pl.pallas_call(
        body, name="allreduce_small", in_specs=[vm], out_specs=vm, out_shape=jax.ShapeDtypeStruct(pack.shape, f32),
        scratch_shapes=[pltpu.VMEM((NDEV, rows, LANES), f32), pltpu.SemaphoreType.DMA((NDEV - 1,)),
                        pltpu.SemaphoreType.DMA((NDEV - 1,))],
        compiler_params=pltpu.CompilerParams(vmem_limit_bytes=40 << 20),
    )(pack)


def _adamw_math(w, g, m, v):
    m = ADAM_B1 * m + (1.0 - ADAM_B1) * g
    v = ADAM_B2 * v + (1.0 - ADAM_B2) * (g * g)
    m_hat = m / (1.0 - ADAM_B1 ** ADAM_STEP)
    v_hat = v / (1.0 - ADAM_B2 ** ADAM_STEP)
    return -ADAM_LR * (m_hat / (jnp.sqrt(v_hat) + ADAM_EPS) + ADAM_WD * w), m, v


def adamw_big(name, g0, g1, w, m, v):
    _, R, C = w.shape
    tr = _row_tile(R, C, target=1 << 20)
    nt = R // tr

    def body(g0_r, g1_r, w_r, m_r, v_r, g_o, d_o, m_o, v_o):
        g = jnp.where(pl.program_id(0) == 0, g0_r[...], g1_r[...])
        g_o[...] = g
        d_o[...], m_o[...], v_o[...] = _adamw_math(w_r[...], g, m_r[...], v_r[...])

    stk = pl.BlockSpec((None, tr, C), lambda l, i: (l, i, 0))
    return pl.pallas_call(
        body, name=name, grid=(DEPTH, nt),
        in_specs=[pl.BlockSpec((tr, C), lambda l, i: (jnp.where(l == 0, i, nt - 1), 0)),
                  pl.BlockSpec((tr, C), lambda l, i: (jnp.where(l == 0, 0, i), 0)), stk, stk, stk],
        out_specs=[stk] * 4, out_shape=[jax.ShapeDtypeStruct(w.shape, f32)] * 4,
        compiler_params=_cp(("arbitrary", "arbitrary")),
    )(g0, g1, w, m, v)


def adamw_small(name, g, w, m, v):
    def body(g_r, w_r, m_r, v_r, d_o, m_o, v_o):
        d_o[...], m_o[...], v_o[...] = _adamw_math(w_r[...], g_r[...], m_r[...], v_r[...])

    return pl.pallas_call(body, name=name, out_shape=[jax.ShapeDtypeStruct(w.shape, f32)] * 3)(g, w, m, v)


WEIGHTS = ("w_in", "conv_w", "gmlp_ln_g", "gmlp_ln_b", "w_s", "b_s", "p_a", "p_b", "p_c", "w_o", "ln1_g", "ln1_b",
           "w_gate", "w_up", "w_down", "ln2_g", "ln2_b")
VECS = ("ln1_g", "ln1_b", "ln2_g", "ln2_b", "gmlp_ln_g", "gmlp_ln_b")
ROWS_VEC, ROWS_BS, ROWS_WS, ROWS_CONV = D // LANES, 8, 8 * BLK, 3 * D // LANES
ROWS_LAYER = len(VECS) * ROWS_VEC + ROWS_BS + ROWS_WS + ROWS_CONV


def _pack_small(per_layer, tail):
    parts = []
    for P in per_layer:
        parts += [P[n].reshape(ROWS_VEC, LANES) for n in VECS]
        parts += [P["b_s"].reshape(ROWS_BS, LANES), P["w_s"].reshape(ROWS_WS, LANES), P["conv_w"].reshape(ROWS_CONV, LANES)]
    return jnp.concatenate(parts + [tail], axis=0)


def _unpack_small(pack):
    out = []
    for l in range(DEPTH):
        r = l * ROWS_LAYER
        P = {}
        for n in VECS:
            P[n] = pack[r:r + ROWS_VEC].reshape(D)
            r += ROWS_VEC
        P["b_s"] = pack[r:r + ROWS_BS].reshape(8, BLK)
        r += ROWS_BS
        P["w_s"] = pack[r:r + ROWS_WS].reshape(8, BLK, BLK)
        r += ROWS_WS
        P["conv_w"] = pack[r:r + ROWS_CONV].reshape(3, D)
        out.append(P)
    return out, pack[DEPTH * ROWS_LAYER:]


def kernel(x, positions, w_in, conv_w, gmlp_ln_g, gmlp_ln_b, w_s, b_s, p_a, p_b, p_c, w_o, ln1_g, ln1_b, w_gate, w_up, w_down, ln2_g, ln2_b, loss_target, m_w_in, m_conv_w, m_gmlp_ln_g, m_gmlp_ln_b, m_w_s, m_b_s, m_p_a, m_p_b, m_p_c, m_w_o, m_ln1_g, m_ln1_b, m_w_gate, m_w_up, m_w_down, m_ln2_g, m_ln2_b, v_w_in, v_conv_w, v_gmlp_ln_g, v_gmlp_ln_b, v_w_s, v_b_s, v_p_a, v_p_b, v_p_c, v_w_o, v_ln1_g, v_ln1_b, v_w_gate, v_w_up, v_w_down, v_ln2_g, v_ln2_b):
    Wt = dict(w_in=w_in, conv_w=conv_w, gmlp_ln_g=gmlp_ln_g, gmlp_ln_b=gmlp_ln_b, w_s=w_s, b_s=b_s, p_a=p_a, p_b=p_b,
              p_c=p_c, w_o=w_o, ln1_g=ln1_g, ln1_b=ln1_b, w_gate=w_gate, w_up=w_up, w_down=w_down, ln2_g=ln2_g, ln2_b=ln2_b)
    Mt = dict(w_in=m_w_in, conv_w=m_conv_w, gmlp_ln_g=m_gmlp_ln_g, gmlp_ln_b=m_gmlp_ln_b, w_s=m_w_s, b_s=m_b_s, p_a=m_p_a,
              p_b=m_p_b, p_c=m_p_c, w_o=m_w_o, ln1_g=m_ln1_g, ln1_b=m_ln1_b, w_gate=m_w_gate, w_up=m_w_up,
              w_down=m_w_down, ln2_g=m_ln2_g, ln2_b=m_ln2_b)
    Vt = dict(w_in=v_w_in, conv_w=v_conv_w, gmlp_ln_g=v_gmlp_ln_g, gmlp_ln_b=v_gmlp_ln_b, w_s=v_w_s, b_s=v_b_s, p_a=v_p_a,
              p_b=v_p_b, p_c=v_p_c, w_o=v_w_o, ln1_g=v_ln1_g, ln1_b=v_ln1_b, w_gate=v_w_gate, w_up=v_w_up,
              w_down=v_w_down, ln2_g=v_ln2_g, ln2_b=v_ln2_b)
    chip = 2 * lax.axis_index("x") + lax.axis_index("y")
    cw = D // NCHIP

    full = gather_weights([Wt[n].astype(MX) for n in BIG] + [conv_w])
    layers = []
    for l in range(DEPTH):
        Wl = dict(zip(BIG, (f[l] for f in full[:NBIG])))
        for n in ("p_a", "p_c", "w_o"):
            Wl[n] = Wl[n].reshape(D, D)
        Wl["conv_w"] = full[NBIG][l].transpose(1, 0, 2).reshape(3, D)
        for n in VECS + ("w_s", "b_s"):
            Wl[n] = Wt[n][l]
        layers.append(Wl)

    lsum, grad_x, grads = local_step(x[0], positions[0], loss_target[0], layers)

    red = [None] * DEPTH
    for l in reversed(range(DEPTH)):
        red[l] = reduce_scatter_layer(l, grads[l])
    pack = _pack_small([{n: (g[n] if n not in VECS else g[n]) for n in VECS + ("b_s", "w_s", "conv_w")} for g in grads], lsum)
    small, tail = _unpack_small(allreduce_small(pack))
    loss = tail[0, 0]

    G, DW, NM, NV = {}, {}, {}, {}
    for n in BIG:
        G[n], DW[n], NM[n], NV[n] = adamw_big("adamw_" + n, red[0][n], red[1][n], Wt[n], Mt[n], Vt[n])
    zc = jnp.zeros((3, D), f32)
    wp = _pack_small([{**{n: Wt[n][l] for n in VECS + ("b_s", "w_s")}, "conv_w": zc} for l in range(DEPTH)], jnp.zeros((8, LANES), f32))
    mp = _pack_small([{**{n: Mt[n][l] for n in VECS + ("b_s", "w_s")}, "conv_w": zc} for l in range(DEPTH)], jnp.zeros((8, LANES), f32))
    vp = _pack_small([{**{n: Vt[n][l] for n in VECS + ("b_s", "w_s")}, "conv_w": zc} for l in range(DEPTH)], jnp.ones((8, LANES), f32))
    gp = _pack_small(small, jnp.zeros((8, LANES), f32))
    outs = [_unpack_small(a)[0] for a in adamw_small("adamw_small", gp, wp, mp, vp)]
    for n in VECS + ("b_s", "w_s"):
        G[n] = jnp.stack([small[l][n] for l in range(DEPTH)])
        DW[n], NM[n], NV[n] = (jnp.stack([o[l][n] for l in range(DEPTH)]) for o in outs)
    gconv = jnp.stack([lax.dynamic_slice(small[l]["conv_w"], (0, chip * cw), (3, cw)) for l in range(DEPTH)])
    G["conv_w"] = gconv
    flat = lambda a: a.reshape(DEPTH * 3, cw)
    d, m2, v2 = adamw_small("adamw_conv", flat(gconv), flat(conv_w), flat(m_conv_w), flat(v_conv_w))
    DW["conv_w"], NM["conv_w"], NV["conv_w"] = (a.reshape(DEPTH, 3, cw) for a in (d, m2, v2))

    return (loss, grad_x[None], *[G[n] for n in WEIGHTS], *[DW[n] for n in WEIGHTS], *[NM[n] for n in WEIGHTS],
            *[NV[n] for n in WEIGHTS])
```

```python
import functools
import math

import jax
import jax.numpy as jnp
from jax import lax
from jax.experimental import pallas as pl
from jax.experimental.pallas import tpu as pltpu

D = 1024
NIN = 12800
DFF = 2816
NCHIP = 4
FB = DFF // NCHIP
WIN_SHARD = NIN // NCHIP
DEPTH = 2
GROUPS = ((128, 1), (512, 4), (2048, 16))
HD = 64
BLK = 128
AO = 512
ALPHA = (2 * DEPTH) ** 0.25
EPS = 1e-5
ROPE_THETA = 10000.0
LANES = 128
NEG = -1e30

C_GATES, C_BCH, C_QKV, C_UV = 0, 3 * D, 6 * D, 6 * D + 9 * AO

MX = jnp.bfloat16
ACT = jnp.bfloat16

ADAM_LR, ADAM_B1, ADAM_B2, ADAM_EPS, ADAM_WD, ADAM_STEP = 0.001, 0.9, 0.999, 1e-08, 0.01, 10

f32 = jnp.float32
NT = (((1,), (1,)), ((), ()))
TN = (((0,), (0,)), ((), ()))


def _cp(sem, vmem_mb=48):
    return pltpu.CompilerParams(dimension_semantics=sem, vmem_limit_bytes=vmem_mb << 20)


def _dot(a, b, dims=None):
    if dims is None:
        return jnp.dot(a, b, preferred_element_type=f32)
    return lax.dot_general(a, b, dims, preferred_element_type=f32)


def _ln_stats(r):
    mu = jnp.mean(r, axis=-1, keepdims=True)
    xc = r - mu
    var = jnp.mean(xc * xc, axis=-1, keepdims=True)
    rstd = lax.rsqrt(var + EPS)
    return xc * rstd, rstd


def _ln_bwd(dy, xhat, rstd, g):
    dxh = dy * g
    return rstd * (dxh - jnp.mean(dxh, axis=-1, keepdims=True) - xhat * jnp.mean(dxh * xhat, axis=-1, keepdims=True))


def _gelu(x):
    return 0.5 * x * (1.0 + lax.erf(x * (1.0 / math.sqrt(2.0))))


def _gelu_grad(x):
    return 0.5 * (1.0 + lax.erf(x * (1.0 / math.sqrt(2.0)))) + x * jnp.exp(-0.5 * x * x) * (1.0 / math.sqrt(2.0 * math.pi))


def _sigmoid(x):
    return 1.0 / (1.0 + jnp.exp(-x))


def _acc_rows(o_ref, first, val):
    @pl.when(first)
    def _():
        o_ref[...] = jnp.zeros_like(o_ref)
    o_ref[...] += jnp.sum(val, axis=0, keepdims=True)


def mm_in(x, w):
    T = x.shape[0]
    tm, tn = min(1024, T), 512

    def body(x_ref, w_ref, o_ref, xb):
        @pl.when(pl.program_id(1) == 0)
        def _():
            xb[...] = x_ref[...].astype(MX)
        o_ref[...] = _dot(xb[...], w_ref[...]).astype(o_ref.dtype)

    return pl.pallas_call(
        body, name="mm_in", grid=(T // tm, NIN // tn),
        in_specs=[pl.BlockSpec((tm, D), lambda i, j: (i, 0)), pl.BlockSpec((D, tn), lambda i, j: (0, j))],
        out_specs=pl.BlockSpec((tm, tn), lambda i, j: (i, j)),
        out_shape=jax.ShapeDtypeStruct((T, NIN), ACT),
        scratch_shapes=[pltpu.VMEM((tm, D), MX)],
        compiler_params=_cp(("parallel", "arbitrary")),
    )(x, w)


HALO = 16
TM_AC = 256


def _uv_specs():
    return [pl.BlockSpec((TM_AC, 512), functools.partial(lambda i, j: (i, j), j=C_UV // 512 + j)) for j in range(4)]


def _gmlp_fwd(up, vp, ws_ref, bs_ref, lg, lb):
    u = _gelu(up)
    xhat, rstd = _ln_stats(_gelu(vp))
    vn = xhat * lg + lb
    vnb = vn.astype(MX)
    rows = []
    for c in range(up.shape[0] // BLK):
        r = slice(c * BLK, (c + 1) * BLK)
        rows.append(jnp.concatenate(
            [_dot(ws_ref[g], vnb[r, g * BLK:(g + 1) * BLK]) + bs_ref[g] for g in range(8)], axis=1))
    return u, vn, xhat, rstd, jnp.concatenate(rows, axis=0)


def mix_ac_fwd(proj, conv_w, wst, bsx, lg, lb):
    T = proj.shape[0]
    tm = TM_AC

    def body(bch, halo, u0, u1, v0, v1, cw, ws, bs, lg_ref, lb_ref, ya, yc, zs):
        i = pl.program_id(0)
        pb = bch[...].astype(f32)
        z = pb[:, D:2 * D] * pb[:, 2 * D:]
        hz = halo[:, :D].astype(f32) * halo[:, D:].astype(f32)
        zs[0:HALO, :] = jnp.where(i > 0, hz, 0.0)
        zs[HALO:HALO + tm, :] = z
        cv = cw[0:1, :] * zs[HALO - 2:HALO - 2 + tm, :] + cw[1:2, :] * zs[HALO - 1:HALO - 1 + tm, :] + cw[2:3, :] * z
        ya[...] = (pb[:, :D] * cv).astype(ya.dtype)
        up = jnp.concatenate([u0[...], u1[...]], axis=1).astype(f32)
        vp = jnp.concatenate([v0[...], v1[...]], axis=1).astype(f32)
        u, _, _, _, sp = _gmlp_fwd(up, vp, ws, bs, lg_ref[...], lb_ref[...])
        yc[...] = (u * sp).astype(yc.dtype)

    full = lambda shape: pl.BlockSpec(shape, lambda i: (0,) * len(shape))
    return pl.pallas_call(
        body, name="mix_ac_fwd", grid=(T // tm,),
        in_specs=[pl.BlockSpec((tm, 3 * D), lambda i: (i, 1)),
                  pl.BlockSpec((HALO, 2 * D), lambda i: (jnp.maximum(i * (tm // HALO) - 1, 0), 2)),
                  *_uv_specs(), full((3, D)), full((8, BLK, BLK)), full((8, BLK, BLK)), full((1, D)), full((1, D))],
        out_specs=[pl.BlockSpec((tm, D), lambda i: (i, 0))] * 2,
        out_shape=[jax.ShapeDtypeStruct((T, D), MX)] * 2,
        scratch_shapes=[pltpu.VMEM((HALO + tm, D), f32)],
        compiler_params=_cp(("parallel",)),
    )(proj, proj, proj, proj, proj, proj, conv_w, wst, bsx, lg, lb)


def _swap_halves(x):
    lane = lax.broadcasted_iota(jnp.int32, x.shape, 1)
    return jnp.where((lane % HD) < HD // 2, pltpu.roll(x, x.shape[1] - HD // 2, 1), pltpu.roll(x, HD // 2, 1))


def _tile4(t):
    return jnp.concatenate([t] * (AO // LANES), axis=1)


TM_FOLD = 512


def _fold_out(nat, x, out_ref, d):
    if d == 1:
        out_ref[0] = x.astype(out_ref.dtype)
        return
    rows = x.shape[0] // d
    for j in range(AO // LANES):
        nat[j] = x[:, j * LANES:(j + 1) * LANES]
    for r in range(d):
        out_ref[r] = jnp.concatenate(
            [nat.at[j][pl.ds(r, rows, stride=d), :] for j in range(AO // LANES)], axis=1).astype(out_ref.dtype)


def _unfold_in(nat, in_ref, d):
    if d == 1:
        return in_ref[0].astype(f32)
    rows = in_ref.shape[1]
    for r in range(d):
        v = in_ref[r].astype(f32)
        for j in range(AO // LANES):
            nat.at[j][pl.ds(r, rows, stride=d), :] = v[:, j * LANES:(j + 1) * LANES]
    return jnp.concatenate([nat[j] for j in range(AO // LANES)], axis=1)


def fold_rope(proj, cos_t, sin_t, g, d):
    T = proj.shape[0]
    tm = TM_FOLD
    rows = tm // d

    def body(x_ref, c_ref, s_ref, q_o, k_o, v_o, nat):
        cos, sin = _tile4(c_ref[...]), _tile4(s_ref[...])
        for part, out, scale in ((0, q_o, HD ** -0.5), (1, k_o, 1.0), (2, v_o, None)):
            x = x_ref[:, part * AO:(part + 1) * AO].astype(f32)
            if scale is not None:
                x = (x * cos + _swap_halves(x) * sin) * scale
            _fold_out(nat, x, out, d)

    fold_spec = pl.BlockSpec((d, rows, AO), lambda i: (0, i, 0))
    return pl.pallas_call(
        body, name=f"fold_rope{g}", grid=(T // tm,),
        in_specs=[pl.BlockSpec((tm, 3 * AO), lambda i: (i, C_QKV // (3 * AO) + g)),
                  pl.BlockSpec((tm, LANES), lambda i: (i, 0)), pl.BlockSpec((tm, LANES), lambda i: (i, 0))],
        out_specs=[fold_spec] * 3,
        out_shape=[jax.ShapeDtypeStruct((d, T // d, AO), MX)] * 3,
        scratch_shapes=[pltpu.VMEM((AO // LANES, tm, LANES), f32)],
        compiler_params=_cp(("parallel",)),
    )(proj, cos_t, sin_t)


def _attn_masks():
    row = lax.broadcasted_iota(jnp.int32, (BLK, BLK), 0)
    col = lax.broadcasted_iota(jnp.int32, (BLK, BLK), 1)
    return col <= row, col >= row, col < HD


def attn_fwd(qf, kf, vf, g, nb):
    T = qf.shape[0]

    def body(q_ref, k_ref, v_ref, o_ref, l_ref):
        lower, upper, head0 = _attn_masks()

        def step(b, carry):
            r0 = pl.multiple_of(b * BLK, BLK)
            rp = pl.multiple_of(jnp.maximum(b - 1, 0) * BLK, BLK)
            q, kc, vc = q_ref[pl.ds(r0, BLK), :], k_ref[pl.ds(r0, BLK), :], v_ref[pl.ds(r0, BLK), :]
            kp, vp = k_ref[pl.ds(rp, BLK), :], v_ref[pl.ds(rp, BLK), :]
            mask_p = upper & ((b % nb) != 0)
            outs = []
            for hm in (head0, ~head0):
                qh = jnp.where(hm, q, jnp.zeros_like(q))
                sc = jnp.where(lower, _dot(qh, kc, NT), NEG)
                sp = jnp.where(mask_p, _dot(qh, kp, NT), NEG)
                m = jnp.maximum(jnp.max(sc, axis=-1, keepdims=True), jnp.max(sp, axis=-1, keepdims=True))
                pc, pp = jnp.exp(sc - m), jnp.exp(sp - m)
                l = jnp.sum(pc, axis=-1, keepdims=True) + jnp.sum(pp, axis=-1, keepdims=True)
                o = (_dot(pc.astype(MX), vc) + _dot(pp.astype(MX), vp)) / l
                outs.append((o, m + jnp.log(l)))
            o_ref[pl.ds(r0, BLK), :] = jnp.where(head0, outs[0][0], outs[1][0])
            l_ref[pl.ds(r0, BLK), :] = jnp.where(head0, outs[0][1], outs[1][1])
            return carry

        lax.fori_loop(0, T // BLK, step, 0)

    spec = pl.BlockSpec((T, LANES), lambda j: (0, j))
    return pl.pallas_call(
        body, name=f"attn_fwd{g}", grid=(AO // LANES,),
        in_specs=[spec] * 3, out_specs=[spec] * 2,
        out_shape=[jax.ShapeDtypeStruct((T, AO), f32)] * 2,
        compiler_params=_cp(("parallel",), 56),
    )(qf, kf, vf)


def _group_weights(lses):
    m = jnp.maximum(jnp.maximum(lses[0], lses[1]), lses[2])
    e = [jnp.exp(l - m) for l in lses]
    inv = 1.0 / (e[0] + e[1] + e[2])
    return [x * inv for x in e]


def _fold_specs(T, tm):
    specs = []
    for _, d in GROUPS:
        specs.append(pl.BlockSpec((d, tm // d, AO), lambda i: (0, i, 0)))
    return specs


def combine_fwd(os_, lses):
    T = os_[0].shape[0] * os_[0].shape[1]
    tm = TM_FOLD

    def body(o0, o1, o2, l0, l1, l2, y_ref, nat):
        o = [_unfold_in(nat, r, d) for r, (_, d) in zip((o0, o1, o2), GROUPS)]
        ls = [_unfold_in(nat, r, d) for r, (_, d) in zip((l0, l1, l2), GROUPS)]
        w = _group_weights(ls)
        y_ref[...] = (w[0] * o[0] + w[1] * o[1] + w[2] * o[2]).astype(y_ref.dtype)

    specs = _fold_specs(T, tm)
    return pl.pallas_call(
        body, name="combine_fwd", grid=(T // tm,),
        in_specs=specs + specs, out_specs=pl.BlockSpec((tm, AO), lambda i: (i, 0)),
        out_shape=jax.ShapeDtypeStruct((T, AO), MX),
        scratch_shapes=[pltpu.VMEM((AO // LANES, tm, LANES), f32)],
        compiler_params=_cp(("parallel",)),
    )(*os_, *lses)


TM_MIX = 256


def mix_out_fwd(proj, ya, yb, yc, x0, pa, pb, pc, wo, g1, b1):
    T = x0.shape[0]
    tm = min(TM_MIX, T)

    def body(gt, ya_r, yb_r, yc_r, x0_r, pa_r, pb_r, pc_r, wo_r, g_r, b_r, mabc, m_o, r1_o, x1_o):
        ma = _dot(ya_r[...], pa_r[...])
        ybv = yb_r[...]
        mb = jnp.concatenate([_dot(ybv, pb_r[k]) for k in range(NCHIP)], axis=1)
        mc = _dot(yc_r[...], pc_r[...])
        m = jnp.zeros((tm, D), f32)
        for j, mm in enumerate((ma, mb, mc)):
            mabc[:, j * D:(j + 1) * D] = mm.astype(mabc.dtype)
            m = m + _sigmoid(gt[:, j * D:(j + 1) * D].astype(f32)) * mm
        mb16 = m.astype(MX)
        m_o[...] = mb16
        r1 = ALPHA * x0_r[...] + _dot(mb16, wo_r[...])
        r1_o[...] = r1
        xhat, _ = _ln_stats(r1)
        x1_o[...] = xhat * g_r[...] + b_r[...]

    full = lambda shape: pl.BlockSpec(shape, lambda i: (0,) * len(shape))
    tile = lambda w: pl.BlockSpec((tm, w), lambda i: (i, 0))
    return pl.pallas_call(
        body, name="mix_out_fwd", grid=(T // tm,),
        in_specs=[tile(3 * D), tile(D), tile(AO), tile(D), tile(D), full((D, D)), full((NCHIP, AO, D // NCHIP)),
                  full((D, D)), full((D, D)), full((1, D)), full((1, D))],
        out_specs=[tile(3 * D), tile(D), tile(D), tile(D)],
        out_shape=[jax.ShapeDtypeStruct((T, 3 * D), MX), jax.ShapeDtypeStruct((T, D), MX),
                   jax.ShapeDtypeStruct((T, D), f32), jax.ShapeDtypeStruct((T, D), f32)],
        compiler_params=_cp(("parallel",), 56),
    )(proj, ya, yb, yc, x0, pa, pb, pc, wo, g1, b1)


TM_FF = 512


def ffn_up_fwd(x1, wg, wu):
    T = x1.shape[0]
    tm = min(TM_FF, T)

    def body(x_r, wg_r, wu_r, g_o, u_o, h_o, xb):
        @pl.when(pl.program_id(1) == 0)
        def _():
            xb[...] = x_r[...].astype(MX)
        gate = _dot(xb[...], wg_r[0])
        up = _dot(xb[...], wu_r[0])
        g_o[0] = gate.astype(g_o.dtype)
        u_o[0] = up.astype(u_o.dtype)
        h_o[0] = (gate * _sigmoid(gate) * up).astype(h_o.dtype)

    wspec = pl.BlockSpec((1, D, FB), lambda i, k: (k, 0, 0))
    ospec = pl.BlockSpec((1, tm, FB), lambda i, k: (k, i, 0))
    return pl.pallas_call(
        body, name="ffn_up_fwd", grid=(T // tm, NCHIP),
        in_specs=[pl.BlockSpec((tm, D), lambda i, k: (i, 0)), wspec, wspec],
        out_specs=[ospec] * 3,
        out_shape=[jax.ShapeDtypeStruct((NCHIP, T, FB), ACT)] * 2 + [jax.ShapeDtypeStruct((NCHIP, T, FB), MX)],
        scratch_shapes=[pltpu.VMEM((tm, D), MX)],
        compiler_params=_cp(("parallel", "arbitrary")),
    )(x1, wg, wu)


def ffn_down_fwd(hh, wd, x1, g2, b2):
    T = x1.shape[0]
    tm = min(TM_FF, T)

    def body(h_r, w_r, x_r, g_r, b_r, r2_o, x2_o):
        r2 = ALPHA * x_r[...]
        for k in range(NCHIP):
            r2 = r2 + _dot(h_r[k], w_r[k])
        r2_o[...] = r2
        xhat, _ = _ln_stats(r2)
        x2_o[...] = xhat * g_r[...] + b_r[...]

    tile = pl.BlockSpec((tm, D), lambda i: (i, 0))
    vec = pl.BlockSpec((1, D), lambda i: (0, 0))
    return pl.pallas_call(
        body, name="ffn_down_fwd", grid=(T // tm,),
        in_specs=[pl.BlockSpec((NCHIP, tm, FB), lambda i: (0, i, 0)), pl.BlockSpec((NCHIP, FB, D), lambda i: (0, 0, 0)),
                  tile, vec, vec],
        out_specs=[tile, tile], out_shape=[jax.ShapeDtypeStruct((T, D), f32)] * 2,
        compiler_params=_cp(("parallel",)),
    )(hh, wd, x1, g2, b2)


def loss_grad(y, tgt):
    T = y.shape[0]
    tm = min(512, T)

    def body(y_r, t_r, l_o, dy_o):
        e = y_r[...] - t_r[...]
        dy_o[...] = e * (1.0 / D)

        @pl.when(pl.program_id(0) == 0)
        def _():
            l_o[...] = jnp.zeros_like(l_o)
        l_o[...] += (0.5 / D) * jnp.sum(e * e)

    tile = pl.BlockSpec((tm, D), lambda i: (i, 0))
    return pl.pallas_call(
        body, name="loss_grad", grid=(T // tm,),
        in_specs=[tile, tile], out_specs=[pl.BlockSpec((8, LANES), lambda i: (0, 0)), tile],
        out_shape=[jax.ShapeDtypeStruct((8, LANES), f32), jax.ShapeDtypeStruct((T, D), f32)],
        compiler_params=_cp(("arbitrary",)),
    )(y, tgt)


def ffn_down_bwd(dx2, r2, g2, wd, gate, up):
    T = dx2.shape[0]
    tm = min(TM_FF, T)

    def body(dx_r, r_r, g_r, w_r, ga_r, up_r, dr_o, dg_o, du_o, dlg_o, dlb_o, drb):
        i, k = pl.program_id(0), pl.program_id(1)

        @pl.when(k == 0)
        def _():
            xhat, rstd = _ln_stats(r_r[...])
            dx = dx_r[...]
            _acc_rows(dlg_o, i == 0, dx * xhat)
            _acc_rows(dlb_o, i == 0, dx)
            dr = _ln_bwd(dx, xhat, rstd, g_r[...])
            dr_o[...] = dr
            drb[...] = dr.astype(MX)

        dhh = _dot(drb[...], w_r[0], NT)
        gate_v, up_v = ga_r[0].astype(f32), up_r[0].astype(f32)
        sg = _sigmoid(gate_v)
        dg_o[0] = (dhh * up_v * sg * (1.0 + gate_v * (1.0 - sg))).astype(dg_o.dtype)
        du_o[0] = (dhh * gate_v * sg).astype(du_o.dtype)

    tile = pl.BlockSpec((tm, D), lambda i, k: (i, 0))
    vec = pl.BlockSpec((1, D), lambda i, k: (0, 0))
    blk = pl.BlockSpec((1, tm, FB), lambda i, k: (k, i, 0))
    return pl.pallas_call(
        body, name="ffn_down_bwd", grid=(T // tm, NCHIP),
        in_specs=[tile, tile, vec, pl.BlockSpec((1, FB, D), lambda i, k: (k, 0, 0)), blk, blk],
        out_specs=[tile, blk, blk, vec, vec],
        out_shape=[jax.ShapeDtypeStruct((T, D), f32)] + [jax.ShapeDtypeStruct((NCHIP, T, FB), MX)] * 2
        + [jax.ShapeDtypeStruct((1, D), f32)] * 2,
        scratch_shapes=[pltpu.VMEM((tm, D), MX)],
        compiler_params=_cp(("arbitrary", "arbitrary")),
    )(dx2, r2, g2, wd, gate, up)


def ffn_up_bwd(dr2, dgate, dup, wg, wu, r1, g1):
    T = dr2.shape[0]
    tm = min(TM_FF, T)

    def body(dr2_r, dg_r, du_r, wg_r, wu_r, r1_r, g_r, dr1_o, dlg_o, dlb_o, acc):
        i, k = pl.program_id(0), pl.program_id(1)

        @pl.when(k == 0)
        def _():
            acc[...] = ALPHA * dr2_r[...]
        acc[...] += _dot(dg_r[0], wg_r[0], NT) + _dot(du_r[0], wu_r[0], NT)

        @pl.when(k == NCHIP - 1)
        def _():
            dx = acc[...]
            xhat, rstd = _ln_stats(r1_r[...])
            _acc_rows(dlg_o, i == 0, dx * xhat)
            _acc_rows(dlb_o, i == 0, dx)
            dr1_o[...] = _ln_bwd(dx, xhat, rstd, g_r[...])

    tile = pl.BlockSpec((tm, D), lambda i, k: (i, 0))
    vec = pl.BlockSpec((1, D), lambda i, k: (0, 0))
    blk = pl.BlockSpec((1, tm, FB), lambda i, k: (k, i, 0))
    wspec = pl.BlockSpec((1, D, FB), lambda i, k: (k, 0, 0))
    return pl.pallas_call(
        body, name="ffn_up_bwd", grid=(T // tm, NCHIP),
        in_specs=[tile, blk, blk, wspec, wspec, tile, vec],
        out_specs=[tile, vec, vec],
        out_shape=[jax.ShapeDtypeStruct((T, D), f32)] + [jax.ShapeDtypeStruct((1, D), f32)] * 2,
        scratch_shapes=[pltpu.VMEM((tm, D), f32)],
        compiler_params=_cp(("arbitrary", "arbitrary")),
    )(dr2, dgate, dup, wg, wu, r1, g1)


def mix_out_bwd(dr1, proj, mabc, wo, pa, pb, pc):
    T = dr1.shape[0]
    tm = min(TM_MIX, T)

    def body(dr_r, gt, mabc_r, wo_r, pa_r, pb_r, pc_r, dmabc_o, dgt_o, dya_o, dyb_o, dyc_o):
        dm = _dot(dr_r[...].astype(MX), wo_r[...], NT)
        dmx = []
        for j in range(3):
            s = _sigmoid(gt[:, j * D:(j + 1) * D].astype(f32))
            v = (dm * s).astype(MX)
            dmx.append(v)
            dmabc_o[:, j * D:(j + 1) * D] = v
            dgt_o[:, j * D:(j + 1) * D] = (dm * mabc_r[:, j * D:(j + 1) * D].astype(f32) * s * (1.0 - s)).astype(dgt_o.dtype)
        dya_o[...] = _dot(dmx[0], pa_r[...], NT)
        dyb = jnp.zeros((tm, AO), f32)
        for k in range(NCHIP):
            dyb = dyb + _dot(dmx[1][:, k * (D // NCHIP):(k + 1) * (D // NCHIP)], pb_r[k], NT)
        dyb_o[...] = dyb
        dyc_o[...] = _dot(dmx[2], pc_r[...], NT)

    full = lambda shape: pl.BlockSpec(shape, lambda i: (0,) * len(shape))
    tile = lambda w: pl.BlockSpec((tm, w), lambda i: (i, 0))
    return pl.pallas_call(
        body, name="mix_out_bwd", grid=(T // tm,),
        in_specs=[tile(D), tile(3 * D), tile(3 * D), full((D, D)), full((D, D)), full((NCHIP, AO, D // NCHIP)), full((D, D))],
        out_specs=[tile(3 * D), tile(3 * D), tile(D), tile(AO), tile(D)],
        out_shape=[jax.ShapeDtypeStruct((T, 3 * D), MX), jax.ShapeDtypeStruct((T, 3 * D), MX),
                   jax.ShapeDtypeStruct((T, D), f32), jax.ShapeDtypeStruct((T, AO), f32), jax.ShapeDtypeStruct((T, D), f32)],
        compiler_params=_cp(("parallel",), 56),
    )(dr1, proj, mabc, wo, pa, pb, pc)


def transpose_cast(x):
    T = x.shape[0]
    tm = min(512, T)

    def body(x_r, o_r):
        o_r[...] = x_r[...].T.astype(o_r.dtype)

    return pl.pallas_call(
        body, name="transpose_cast", grid=(T // tm,),
        in_specs=[pl.BlockSpec((tm, D), lambda i: (i, 0))], out_specs=pl.BlockSpec((D, tm), lambda i: (0, i)),
        out_shape=jax.ShapeDtypeStruct((D, T), MX), compiler_params=_cp(("parallel",)),
    )(x)


def tn_matmul(name, a, b, a_spec, b_spec, out_shape, out_spec, grid, a_is_t=False):
    nt = len(grid) - 1

    def body(a_r, b_r, o_r):
        @pl.when(pl.program_id(nt) == 0)
        def _():
            o_r[...] = jnp.zeros_like(o_r)
        av = a_r[...].reshape(a_r.shape[-2:]).astype(MX)
        bv = b_r[...].reshape(b_r.shape[-2:]).astype(MX)
        o_r[...] += _dot(av, bv, None if a_is_t else TN).reshape(o_r.shape)

    return pl.pallas_call(
        body, name=name, grid=grid, in_specs=[a_spec, b_spec], out_specs=out_spec,
        out_shape=jax.ShapeDtypeStruct(out_shape, f32),
        compiler_params=_cp(("parallel",) * nt + ("arbitrary",), 56),
    )(a, b)


def attn_pre_bwd(dyb, os_, lses):
    T = dyb.shape[0]
    tm = TM_FOLD

    def body(dy_r, o0, o1, o2, l0, l1, l2, ones_r, d0, d1, d2, f0, f1, f2, nat):
        o = [_unfold_in(nat, r, d) for r, (_, d) in zip((o0, o1, o2), GROUPS)]
        ls = [_unfold_in(nat, r, d) for r, (_, d) in zip((l0, l1, l2), GROUPS)]
        w = _group_weights(ls)
        dy = dy_r[...]
        t = dy * (w[0] * o[0] + w[1] * o[1] + w[2] * o[2])
        hi = t.astype(MX)
        lo = (t - hi.astype(f32)).astype(MX)
        c = _dot(hi, ones_r[...]) + _dot(lo, ones_r[...])
        for wg, do_o, df_o, (_, d) in zip(w, (d0, d1, d2), (f0, f1, f2), GROUPS):
            _fold_out(nat, wg * dy, do_o, d)
            _fold_out(nat, -wg * c, df_o, d)

    specs = _fold_specs(T, tm)
    return pl.pallas_call(
        body, name="attn_pre_bwd", grid=(T // tm,),
        in_specs=[pl.BlockSpec((tm, AO), lambda i: (i, 0))] + specs + specs + [pl.BlockSpec((AO, AO), lambda i: (0, 0))],
        out_specs=specs + specs,
        out_shape=[jax.ShapeDtypeStruct((d, T // d, AO), MX) for _, d in GROUPS]
        + [jax.ShapeDtypeStruct((d, T // d, AO), f32) for _, d in GROUPS],
        scratch_shapes=[pltpu.VMEM((AO // LANES, tm, LANES), f32)],
        compiler_params=_cp(("parallel",)),
    )(dyb, *os_, *lses, _head_ones())


def _head_ones():
    i = jnp.arange(AO) // HD
    return (i[:, None] == i[None, :]).astype(MX)


def attn_bwd(qf, kf, vf, dof, lse, df, g, nb):
    T = qf.shape[0]

    def body(q_ref, k_ref, v_ref, do_ref, l_ref, d_ref, dq_ref, dk_ref, dv_ref):
        lower, upper, head0 = _attn_masks()

        def step(b, carry):
            dk_c, dv_c = carry
            r0 = pl.multiple_of(b * BLK, BLK)
            rp = pl.multiple_of(jnp.maximum(b - 1, 0) * BLK, BLK)
            q, kc, vc = q_ref[pl.ds(r0, BLK), :], k_ref[pl.ds(r0, BLK), :], v_ref[pl.ds(r0, BLK), :]
            kp, vp = k_ref[pl.ds(rp, BLK), :], v_ref[pl.ds(rp, BLK), :]
            do, lse_v, d_v = do_ref[pl.ds(r0, BLK), :], l_ref[pl.ds(r0, BLK), :], d_ref[pl.ds(r0, BLK), :]
            mask_p = upper & ((b % nb) != 0)
            dq = []
            dk_new, dv_new, dk_prev, dv_prev = (jnp.zeros((BLK, LANES), f32) for _ in range(4))
            for h, hm in enumerate((head0, ~head0)):
                zero = jnp.zeros_like(q)
                qh, doh = jnp.where(hm, q, zero), jnp.where(hm, do, zero)
                lse_h, d_h = lse_v[:, h * HD:h * HD + 1], d_v[:, h * HD:h * HD + 1]
                pc = jnp.where(lower, jnp.exp(_dot(qh, kc, NT) - lse_h), 0.0)
                pp = jnp.where(mask_p, jnp.exp(_dot(qh, kp, NT) - lse_h), 0.0)
                dsc = (pc * (_dot(doh, vc, NT) + d_h)).astype(MX)
                dsp = (pp * (_dot(doh, vp, NT) + d_h)).astype(MX)
                dq.append(_dot(dsc, kc) + _dot(dsp, kp))
                dk_new = dk_new + _dot(dsc, qh, TN)
                dk_prev = dk_prev + _dot(dsp, qh, TN)
                dv_new = dv_new + _dot(pc.astype(MX), doh, TN)
                dv_prev = dv_prev + _dot(pp.astype(MX), doh, TN)
            dq_ref[pl.ds(r0, BLK), :] = jnp.where(head0, dq[0], dq[1]).astype(dq_ref.dtype)
            dk_ref[pl.ds(rp, BLK), :] = (dk_c + dk_prev).astype(dk_ref.dtype)
            dv_ref[pl.ds(rp, BLK), :] = (dv_c + dv_prev).astype(dv_ref.dtype)
            return dk_new, dv_new

        zero = jnp.zeros((BLK, LANES), f32)
        dk_c, dv_c = lax.fori_loop(0, T // BLK, step, (zero, zero))
        dk_ref[pl.ds(T - BLK, BLK), :] = dk_c.astype(dk_ref.dtype)
        dv_ref[pl.ds(T - BLK, BLK), :] = dv_c.astype(dv_ref.dtype)

    spec = pl.BlockSpec((T, LANES), lambda j: (0, j))
    return pl.pallas_call(
        body, name=f"attn_bwd{g}", grid=(AO // LANES,),
        in_specs=[spec] * 6, out_specs=[spec] * 3,
        out_shape=[jax.ShapeDtypeStruct((T, AO), MX)] * 3,
        compiler_params=_cp(("parallel",), 60),
    )(qf, kf, vf, dof, lse, df)


def unfold_rope_bwd(dqf, dkf, dvf, cos_t, sin_t, g, d):
    T = dqf.shape[0] * dqf.shape[1]
    tm = TM_FOLD

    def body(q_r, k_r, v_r, c_ref, s_ref, o_ref, nat):
        cos, sin = _tile4(c_ref[...]), _tile4(s_ref[...])
        for part, ref, scale in ((0, q_r, HD ** -0.5), (1, k_r, 1.0), (2, v_r, None)):
            x = _unfold_in(nat, ref, d)
            if scale is not None:
                x = (x * cos - _swap_halves(x) * sin) * scale
            o_ref[:, part * AO:(part + 1) * AO] = x.astype(o_ref.dtype)

    fold_spec = pl.BlockSpec((d, tm // d, AO), lambda i: (0, i, 0))
    tab = pl.BlockSpec((tm, LANES), lambda i: (i, 0))
    return pl.pallas_call(
        body, name=f"unfold_rope_bwd{g}", grid=(T // tm,),
        in_specs=[fold_spec] * 3 + [tab, tab],
        out_specs=pl.BlockSpec((tm, 3 * AO), lambda i: (i, 0)),
        out_shape=jax.ShapeDtypeStruct((T, 3 * AO), MX),
        scratch_shapes=[pltpu.VMEM((AO // LANES, tm, LANES), f32)],
        compiler_params=_cp(("parallel",)),
    )(dqf, dkf, dvf, cos_t, sin_t)


def conv_bwd(dya, proj, conv_w):
    T = dya.shape[0]
    tm = TM_AC
    last = T // tm - 1

    def body(dy_r, bch, hprev, dy_next, b_next, cw, d_o, dw_o, zs, ds):
        i = pl.program_id(0)
        pb = bch[...].astype(f32)
        bp, cp, hp = pb[:, :D], pb[:, D:2 * D], pb[:, 2 * D:]
        z = cp * hp
        hz = hprev[:, :D].astype(f32) * hprev[:, D:].astype(f32)
        zs[0:HALO, :] = jnp.where(i > 0, hz, 0.0)
        zs[HALO:HALO + tm, :] = z
        z2, z1 = zs[HALO - 2:HALO - 2 + tm, :], zs[HALO - 1:HALO - 1 + tm, :]
        cv = cw[0:1, :] * z2 + cw[1:2, :] * z1 + cw[2:3, :] * z
        dy = dy_r[...]
        dcv = dy * bp
        ds[0:tm, :] = dcv
        ds[tm:tm + HALO, :] = jnp.where(i < last, dy_next[...] * b_next[...].astype(f32), 0.0)
        dz = cw[2:3, :] * dcv + cw[1:2, :] * ds[1:1 + tm, :] + cw[0:1, :] * ds[2:2 + tm, :]
        d_o[:, :D] = (dy * cv).astype(d_o.dtype)
        d_o[:, D:2 * D] = (dz * hp).astype(d_o.dtype)
        d_o[:, 2 * D:] = (dz * cp).astype(d_o.dtype)

        @pl.when(i == 0)
        def _():
            dw_o[...] = jnp.zeros_like(dw_o)
        dw_o[0:1, :] += jnp.sum(dcv * z2, axis=0, keepdims=True)
        dw_o[1:2, :] += jnp.sum(dcv * z1, axis=0, keepdims=True)
        dw_o[2:3, :] += jnp.sum(dcv * z, axis=0, keepdims=True)

    nh = tm // HALO
    return pl.pallas_call(
        body, name="conv_bwd", grid=(T // tm,),
        in_specs=[pl.BlockSpec((tm, D), lambda i: (i, 0)), pl.BlockSpec((tm, 3 * D), lambda i: (i, 1)),
                  pl.BlockSpec((HALO, 2 * D), lambda i: (jnp.maximum(i * nh - 1, 0), 2)),
                  pl.BlockSpec((HALO, D), lambda i: (jnp.minimum((i + 1) * nh, T // HALO - 1), 0)),
                  pl.BlockSpec((HALO, D), lambda i: (jnp.minimum((i + 1) * nh, T // HALO - 1), 3)),
                  pl.BlockSpec((3, D), lambda i: (0, 0))],
        out_specs=[pl.BlockSpec((tm, 3 * D), lambda i: (i, 0)), pl.BlockSpec((3, D), lambda i: (0, 0))],
        out_shape=[jax.ShapeDtypeStruct((T, 3 * D), MX), jax.ShapeDtypeStruct((3, D), f32)],
        scratch_shapes=[pltpu.VMEM((HALO + tm, D), f32), pltpu.VMEM((tm + HALO, D), f32)],
        compiler_params=_cp(("arbitrary",)),
    )(dya, proj, proj, dya, proj, conv_w)


def gmlp_bwd(dyc, proj, wst, bsx, lg, lb):
    T = dyc.shape[0]
    tm = TM_AC
    last = T // tm - 1

    def body(dy_r, u0, u1, v0, v1, ws, bs, lg_r, lb_r, d_o, dws_o, dbs_o, dlg_o, dlb_o, bacc):
        i = pl.program_id(0)
        up = jnp.concatenate([u0[...], u1[...]], axis=1).astype(f32)
        vp = jnp.concatenate([v0[...], v1[...]], axis=1).astype(f32)
        u, vn, xhat, rstd, sp = _gmlp_fwd(up, vp, ws, bs, lg_r[...], lb_r[...])
        dy = dy_r[...]
        d_o[:, :D] = (dy * sp * _gelu_grad(up)).astype(d_o.dtype)
        dsp = dy * u
        dspb, vnb = dsp.astype(MX), vn.astype(MX)

        @pl.when(i == 0)
        def _():
            dws_o[...] = jnp.zeros_like(dws_o)
            bacc[...] = jnp.zeros_like(bacc)

        rows = []
        for c in range(tm // BLK):
            r = slice(c * BLK, (c + 1) * BLK)
            cols = []
            for g in range(8):
                cs = slice(g * BLK, (g + 1) * BLK)
                dws_o[g] += _dot(dspb[r, cs], vnb[r, cs], NT)
                bacc[g] += dsp[r, cs]
                cols.append(_dot(ws[g], dspb[r, cs], TN))
            rows.append(jnp.concatenate(cols, axis=1))
        dvn = jnp.concatenate(rows, axis=0)
        _acc_rows(dlg_o, i == 0, dvn * xhat)
        _acc_rows(dlb_o, i == 0, dvn)
        d_o[:, D:] = (_ln_bwd(dvn, xhat, rstd, lg_r[...]) * _gelu_grad(vp)).astype(d_o.dtype)

        @pl.when(i == last)
        def _():
            row = lax.broadcasted_iota(jnp.int32, (BLK, BLK), 0)
            col = lax.broadcasted_iota(jnp.int32, (BLK, BLK), 1)
            ones = jnp.ones((8, BLK), MX)
            for g in range(8):
                dws_o[g] = jnp.where(col <= row, dws_o[g], 0.0)
                a = bacc[g]
                hi = a.astype(MX)
                lo = (a - hi.astype(f32)).astype(MX)
                dbs_o[g:g + 1, :] = (_dot(ones, hi, NT) + _dot(ones, lo, NT))[0:1, :]

    full = lambda shape: pl.BlockSpec(shape, lambda i: (0,) * len(shape))
    return pl.pallas_call(
        body, name="gmlp_bwd", grid=(T // tm,),
        in_specs=[pl.BlockSpec((tm, D), lambda i: (i, 0)), *_uv_specs(), full((8, BLK, BLK)), full((8, BLK, BLK)),
                  full((1, D)), full((1, D))],
        out_specs=[pl.BlockSpec((tm, 2 * D), lambda i: (i, 0)), full((8, BLK, BLK)), full((8, BLK)), full((1, D)), full((1, D))],
        out_shape=[jax.ShapeDtypeStruct((T, 2 * D), MX), jax.ShapeDtypeStruct((8, BLK, BLK), f32),
                   jax.ShapeDtypeStruct((8, BLK), f32), jax.ShapeDtypeStruct((1, D), f32), jax.ShapeDtypeStruct((1, D), f32)],
        scratch_shapes=[pltpu.VMEM((8, BLK, BLK), f32)],
        compiler_params=_cp(("arbitrary",)),
    )(dyc, proj, proj, proj, proj, wst, bsx, lg, lb)


PART_TILES = (6, 6, 3, 3, 3, 4)
PART_START = (0, 6, 12, 15, 18, 21)
TJ = 512


def _part_specs(tm, rows_axis):
    specs = []
    for n, s in zip(PART_TILES, PART_START):
        def imap(*idx, n=n, s=s):
            i, j = idx[rows_axis], idx[1 - rows_axis]
            return (i, jnp.clip(j - s, 0, n - 1))
        specs.append(pl.BlockSpec((tm, TJ), imap))
    return specs


def _select_part(j, refs):
    out = refs[0][...]
    for r, s in zip(refs[1:], PART_START[1:]):
        out = jnp.where(j >= s, r[...], out)
    return out


def dx_in(dr1, parts, w):
    T = dr1.shape[0]
    tm = min(1024, T)

    def body(dr_r, p0, p1, p2, p3, p4, p5, w_r, o_r):
        j = pl.program_id(1)

        @pl.when(j == 0)
        def _():
            o_r[...] = ALPHA * dr_r[...]
        o_r[...] += _dot(_select_part(j, (p0, p1, p2, p3, p4, p5)), w_r[...], NT)

    return pl.pallas_call(
        body, name="dx_in", grid=(T // tm, NIN // TJ),
        in_specs=[pl.BlockSpec((tm, D), lambda i, j: (i, 0))] + _part_specs(tm, 0) + [pl.BlockSpec((D, TJ), lambda i, j: (0, j))],
        out_specs=pl.BlockSpec((tm, D), lambda i, j: (i, 0)),
        out_shape=jax.ShapeDtypeStruct((T, D), f32),
        compiler_params=_cp(("parallel", "arbitrary"), 56),
    )(dr1, *parts, w)


def dw_in(x0t, parts):
    T = x0t.shape[1]
    tk = min(1024, T)

    def body(x_r, p0, p1, p2, p3, p4, p5, o_r):
        j, t = pl.program_id(0), pl.program_id(1)

        @pl.when(t == 0)
        def _():
            o_r[...] = jnp.zeros_like(o_r)
        o_r[...] += _dot(x_r[...], _select_part(j, (p0, p1, p2, p3, p4, p5)))

    return pl.pallas_call(
        body, name="dw_in", grid=(NIN // TJ, T // tk),
        in_specs=[pl.BlockSpec((D, tk), lambda j, t: (0, t))] + _part_specs(tk, 1),
        out_specs=pl.BlockSpec((D, TJ), lambda j, t: (0, j)),
        out_shape=jax.ShapeDtypeStruct((D, NIN), f32),
        compiler_params=_cp(("parallel", "arbitrary")),
    )(x0t, *parts)


def rope_tables(positions):
    half = HD // 2
    inv_freq = ROPE_THETA ** (-jnp.arange(half, dtype=f32) / half)
    ang = positions.astype(f32)[:, None] * inv_freq
    cos, sin = jnp.cos(ang), jnp.sin(ang)
    return jnp.tile(cos, (1, LANES // half)), jnp.tile(jnp.concatenate([-sin, sin], axis=1), (1, LANES // HD))


def _flat(a):
    return a.reshape(a.shape[0] * a.shape[1], a.shape[2])


def layer_fwd(x0, W, cos_t, sin_t):
    T = x0.shape[0]
    proj = mm_in(x0, W["w_in"])
    ya, yc = mix_ac_fwd(proj, W["conv_w"], W["wst"], W["bsx"], W["gmlp_ln_g"], W["gmlp_ln_b"])
    folded, os_, lses = [], [], []
    for g, (_, d) in enumerate(GROUPS):
        qf, kf, vf = fold_rope(proj, cos_t, sin_t, g, d)
        o, lse = attn_fwd(_flat(qf), _flat(kf), _flat(vf), g, T // d // BLK)
        folded.append((qf, kf, vf))
        os_.append(o.reshape(d, T // d, AO))
        lses.append(lse.reshape(d, T // d, AO))
    yb = combine_fwd(os_, lses)
    mabc, m, r1, x1 = mix_out_fwd(proj, ya, yb, yc, x0, W["p_a"], W["p_b"], W["p_c"], W["w_o"], W["ln1_g"], W["ln1_b"])
    gate, up, hh = ffn_up_fwd(x1, W["w_gate"], W["w_up"])
    r2, x2 = ffn_down_fwd(hh, W["w_down"], x1, W["ln2_g"], W["ln2_b"])
    saved = dict(x0=x0, proj=proj, ya=ya, yb=yb, yc=yc, folded=folded, os=os_, lses=lses, mabc=mabc, m=m, r1=r1,
                 x1=x1, gate=gate, up=up, hh=hh, r2=r2)
    return x2, saved


def layer_bwd(dx2, S, W, cos_t, sin_t):
    T = dx2.shape[0]
    tk = min(512, T)
    G = {}
    dr2, dgate, dup, G["ln2_g"], G["ln2_b"] = ffn_down_bwd(dx2, S["r2"], W["ln2_g"], W["w_down"], S["gate"], S["up"])
    blk_a = pl.BlockSpec((1, tk, FB), lambda k, t: (k, t, 0))
    row_b = pl.BlockSpec((tk, D), lambda k, t: (t, 0))
    G["w_down"] = tn_matmul("dw_down", S["hh"], dr2, blk_a, row_b, (NCHIP, FB, D),
                            pl.BlockSpec((1, FB, D), lambda k, t: (k, 0, 0)), (NCHIP, T // tk))
    x1t = transpose_cast(S["x1"])
    for nm, dv in (("w_gate", dgate), ("w_up", dup)):
        G[nm] = tn_matmul("d" + nm, x1t, dv, pl.BlockSpec((D, tk), lambda k, t: (0, t)), blk_a, (NCHIP, D, FB),
                          pl.BlockSpec((1, D, FB), lambda k, t: (k, 0, 0)), (NCHIP, T // tk), a_is_t=True)
    dr1, G["ln1_g"], G["ln1_b"] = ffn_up_bwd(dr2, dgate, dup, W["w_gate"], W["w_up"], S["r1"], W["ln1_g"])
    dmabc, dgates, dya, dyb, dyc = mix_out_bwd(dr1, S["proj"], S["mabc"], W["w_o"], W["p_a"], W["p_b"], W["p_c"])
    one = (1, T // tk)
    full_o = pl.BlockSpec((D, D), lambda k, t: (0, 0))
    G["w_o"] = tn_matmul("dw_o", S["m"], dr1, row_b, row_b, (D, D), full_o, one)
    G["p_a"] = tn_matmul("dp_a", S["ya"], dmabc, row_b, pl.BlockSpec((tk, D), lambda k, t: (t, 0)), (D, D), full_o, one)
    G["p_c"] = tn_matmul("dp_c", S["yc"], dmabc, row_b, pl.BlockSpec((tk, D), lambda k, t: (t, 2)), (D, D), full_o, one)
    G["p_b"] = tn_matmul("dp_b", S["yb"], dmabc, pl.BlockSpec((tk, AO), lambda k, t: (t, 0)),
                         pl.BlockSpec((tk, D // NCHIP), lambda k, t: (t, NCHIP + k)), (NCHIP, AO, D // NCHIP),
                         pl.BlockSpec((1, AO, D // NCHIP), lambda k, t: (k, 0, 0)), (NCHIP, T // tk))
    dbch, G["conv_w"] = conv_bwd(dya, S["proj"], W["conv_w"])
    duv, G["w_s"], G["b_s"], G["gmlp_ln_g"], G["gmlp_ln_b"] = gmlp_bwd(
        dyc, S["proj"], W["wst"], W["bsx"], W["gmlp_ln_g"], W["gmlp_ln_b"])
    pre = attn_pre_bwd(dyb, S["os"], S["lses"])
    dqkv = []
    for g, (_, d) in enumerate(GROUPS):
        qf, kf, vf = S["folded"][g]
        dqf, dkf, dvf = attn_bwd(_flat(qf), _flat(kf), _flat(vf), _flat(pre[g]), _flat(S["lses"][g]), _flat(pre[3 + g]),
                                 g, T // d // BLK)
        shp = (d, T // d, AO)
        dqkv.append(unfold_rope_bwd(dqf.reshape(shp), dkf.reshape(shp), dvf.reshape(shp), cos_t, sin_t, g, d))
    parts = (dgates, dbch, *dqkv, duv)
    G["w_in"] = dw_in(transpose_cast(S["x0"]), parts)
    dx0 = dx_in(dr1, parts, W["w_in"])
    return dx0, G


def prep_layer_weights(Wl):
    W = dict(Wl)
    tril = jnp.tril(jnp.ones((BLK, BLK), f32))
    W["wst"] = (Wl["w_s"] * tril[None]).astype(MX)
    W["bsx"] = jnp.broadcast_to(Wl["b_s"][:, :, None], (8, BLK, BLK))
    for n in ("gmlp_ln_g", "gmlp_ln_b", "ln1_g", "ln1_b", "ln2_g", "ln2_b"):
        W[n] = Wl[n].reshape(1, D)
    return W


def local_step(x, positions, target, layers):
    cos_t, sin_t = rope_tables(positions)
    Ws = [prep_layer_weights(Wl) for Wl in layers]
    saved = []
    h = x
    for W in Ws:
        h, S = layer_fwd(h, W, cos_t, sin_t)
        saved.append(S)
    lsum, dh = loss_grad(h, target)
    grads = [None] * len(Ws)
    for l in reversed(range(len(Ws))):
        dh, grads[l] = layer_bwd(dh, saved[l], Ws[l], cos_t, sin_t)
    return lsum, dh, grads


MESH = pl.DeviceIdType.MESH
ANY = pl.BlockSpec(memory_space=pl.ANY)
BIG = ("w_in", "w_gate", "w_up", "w_down", "p_a", "p_b", "p_c", "w_o")
NBIG = len(BIG)


def _place():
    x, y, c = lax.axis_index("x"), lax.axis_index("y"), lax.axis_index("c")
    return x, y, c, 2 * x + y


def _rcopy(src, dst, send, recv, dev):
    return pltpu.make_async_remote_copy(src_ref=src, dst_ref=dst, send_sem=send, recv_sem=recv, device_id=dev,
                                        device_id_type=MESH)


def _cols(ref, k, width):
    start = k * width if isinstance(k, int) else pl.multiple_of(k * width, LANES)
    return ref.at[:, pl.ds(start, width)]


CHUNK_BYTES = 1 << 20


def _pieces(shape, itemsize, nbytes=CHUNK_BYTES):
    rows, cols = shape[-2], shape[-1]
    per = max(16, nbytes // (cols * itemsize) // 16 * 16)
    out = []
    for lead in (range(shape[0]) if len(shape) == 3 else (None,)):
        for r in range(0, rows, per):
            sl = (pl.ds(r, min(per, rows - r)), slice(None))
            out.append(sl if lead is None else (lead,) + sl)
    return out


def _start_pieces(src, dst, make, nbytes=CHUNK_BYTES):
    for idx in _pieces(src.shape, jnp.dtype(src.dtype).itemsize, nbytes):
        make(src.at[idx], dst.at[idx]).start()


def gather_weights(shards):
    n = len(shards)

    def body(*refs):
        srcs, dsts = refs[:n], refs[n:2 * n]
        send, recv, loc = refs[2 * n:]
        x, y, c, k = _place()
        sib = (x, y, 1 - c)
        chips = [(1 - x, y), (x, 1 - y), (1 - x, 1 - y)]

        def slot(a, layer, pos):
            if a == 0:
                return _cols(dsts[0].at[layer], pos, WIN_SHARD)
            return dsts[a].at[layer, pos]

        def ici(a, j, src, dst):
            return _rcopy(src, dst, send.at[a, j], recv.at[a, j], (*chips[j], c))

        def d2d(a, j, src, dst):
            return _rcopy(src, dst, send.at[a, 3 + j], recv.at[a, 3 + j], sib)

        def keep(a, layer, src, dst):
            return pltpu.make_async_copy(src, dst, loc.at[a, layer])

        for a in range(n):
            for j in range(3):
                _start_pieces(srcs[a].at[c], slot(a, c, k), functools.partial(ici, a, j))
        for a in range(n):
            for layer in range(DEPTH):
                _start_pieces(srcs[a].at[layer], slot(a, layer, k), functools.partial(keep, a, layer))
        for a in range(n):
            for j, (cx, cy) in enumerate(chips):
                landed = slot(a, c, 2 * cx + cy)
                ici(a, j, landed, landed).wait_recv()
                _start_pieces(landed, landed, functools.partial(d2d, a, j))
        for a in range(n):
            for j, (cx, cy) in enumerate(chips):
                passed = slot(a, 1 - c, 2 * cx + cy)
                d2d(a, j, passed, passed).wait_recv()
                landed = slot(a, c, 2 * cx + cy)
                d2d(a, j, landed, landed).wait_send()
                ici(a, j, srcs[a].at[c], slot(a, c, k)).wait_send()
            for layer in range(DEPTH):
                keep(a, layer, srcs[a].at[layer], slot(a, layer, k)).wait()

    outs = [jax.ShapeDtypeStruct((DEPTH, D, NIN), shards[0].dtype)]
    outs += [jax.ShapeDtypeStruct((DEPTH, NCHIP) + s.shape[1:], s.dtype) for s in shards[1:]]
    return pl.pallas_call(
        body, name="gather_weights", in_specs=[ANY] * n, out_specs=[ANY] * n, out_shape=outs,
        scratch_shapes=[pltpu.SemaphoreType.DMA((n, 6)), pltpu.SemaphoreType.DMA((n, 6)), pltpu.SemaphoreType.DMA((n, DEPTH))],
    )(*shards)


def _half(ref, h, a):
    rows = ref.shape[-2] // 2
    start = pl.multiple_of(h * rows, 16)
    if a == 0:
        return ref.at[pl.ds(start, rows), :]
    return ref.at[:, pl.ds(start, rows), :]


def rs_pair(l, grads):
    def body(*refs):
        g, mine, theirs = refs[:NBIG], refs[NBIG:2 * NBIG], refs[2 * NBIG:3 * NBIG]
        send, recv, loc = refs[3 * NBIG:]
        x, y, c, _ = _place()
        sib = (x, y, 1 - c)

        def keep(a, s, d):
            return pltpu.make_async_copy(s, d, loc.at[a])

        def give(a, s, d):
            return _rcopy(s, d, send.at[a], recv.at[a], sib)

        for a in range(NBIG):
            _start_pieces(_half(g[a], 1 - c, a), theirs[a], functools.partial(give, a))
        for a in range(NBIG):
            _start_pieces(_half(g[a], c, a), mine[a], functools.partial(keep, a))
        for a in range(NBIG):
            give(a, _half(g[a], 1 - c, a), theirs[a]).wait()
            keep(a, _half(g[a], c, a), mine[a]).wait()

    def hshape(s, a):
        return (s[0] // 2, s[1]) if a == 0 else (s[0], s[1] // 2, s[2])

    outs = [jax.ShapeDtypeStruct(hshape(g.shape, a), g.dtype) for a, g in enumerate(grads)]
    res = pl.pallas_call(
        body, name=f"rs_pair{l}", in_specs=[ANY] * NBIG, out_specs=[ANY] * (2 * NBIG), out_shape=outs + outs,
        scratch_shapes=[pltpu.SemaphoreType.DMA((NBIG,))] * 3,
    )(*grads)
    return res[:NBIG], res[NBIG:]


def rs_chips(l, sums):
    def body(*refs):
        s, land = refs[:NBIG], refs[NBIG:2 * NBIG]
        send, recv, loc = refs[2 * NBIG:]
        x, y, c, me = _place()

        def piece(a, k):
            return _cols(s[a], k, WIN_SHARD) if a == 0 else s[a].at[k]

        def keep(a, src, dst):
            return pltpu.make_async_copy(src, dst, loc.at[a])

        def give(a, k, src, dst):
            return _rcopy(src, dst, send.at[a, k], recv.at[a, me], (k // 2, k % 2, c))

        for k in range(NCHIP):
            @pl.when(me != k)
            def _():
                for a in range(NBIG):
                    _start_pieces(piece(a, k), land[a].at[me], functools.partial(give, a, k))
        for k in range(NCHIP):
            @pl.when(me == k)
            def _():
                for a in range(NBIG):
                    _start_pieces(piece(a, k), land[a].at[k], functools.partial(keep, a))
        for k in range(NCHIP):
            @pl.when(me == k)
            def _():
                for a in range(NBIG):
                    keep(a, piece(a, k), land[a].at[k]).wait()

            @pl.when(me != k)
            def _():
                for a in range(NBIG):
                    give(a, k, piece(a, k), land[a].at[me]).wait_send()
                    _rcopy(piece(a, k), land[a].at[k], send.at[a, k], recv.at[a, k], (k // 2, k % 2, c)).wait_recv()

    def pshape(s, a):
        return (NCHIP, s[0], WIN_SHARD) if a == 0 else s

    outs = [jax.ShapeDtypeStruct(pshape(v.shape, a), v.dtype) for a, v in enumerate(sums)]
    return pl.pallas_call(
        body, name=f"rs_chips{l}", in_specs=[ANY] * NBIG, out_specs=[ANY] * NBIG, out_shape=outs,
        scratch_shapes=[pltpu.SemaphoreType.DMA((NBIG, NCHIP)), pltpu.SemaphoreType.DMA((NBIG, NCHIP)),
                        pltpu.SemaphoreType.DMA((NBIG,))],
    )(*sums)


def rs_join(l, halves):
    def body(*refs):
        h, full = refs[:NBIG], refs[NBIG:2 * NBIG]
        send, recv, loc = refs[2 * NBIG:]
        x, y, c, _ = _place()
        sib = (x, y, 1 - c)

        def place(a):
            rows = h[a].shape[0]
            return full[a].at[pl.ds(pl.multiple_of(c * rows, 16), rows), :]

        def keep(a, s, d):
            return pltpu.make_async_copy(s, d, loc.at[a])

        def give(a, s, d):
            return _rcopy(s, d, send.at[a], recv.at[a], sib)

        for a in range(NBIG):
            _start_pieces(h[a], place(a), functools.partial(give, a))
        for a in range(NBIG):
            _start_pieces(h[a], place(a), functools.partial(keep, a))
        for a in range(NBIG):
            give(a, h[a], place(a)).wait()
            keep(a, h[a], place(a)).wait()

    outs = [jax.ShapeDtypeStruct((2 * v.shape[0], v.shape[1]), v.dtype) for v in halves]
    return pl.pallas_call(
        body, name=f"rs_join{l}", in_specs=[ANY] * NBIG, out_specs=[ANY] * NBIG, out_shape=outs,
        scratch_shapes=[pltpu.SemaphoreType.DMA((NBIG,))] * 3,
    )(*halves)


def _row_tile(rows, cols, itemsize=4, target=2 << 20):
    best = 8
    for t in range(8, rows + 1, 8):
        if rows % t == 0 and t * cols * itemsize <= target:
            best = t
    return best


def add_n(name, terms):
    stacked = not isinstance(terms, (list, tuple))
    shape = terms.shape[1:] if stacked else terms[0].shape
    cols = shape[-1]
    rows = math.prod(shape[:-1])
    tr = _row_tile(rows, cols)

    def body(*refs):
        o = refs[-1]
        if stacked:
            acc = refs[0][0]
            for j in range(1, refs[0].shape[0]):
                acc = acc + refs[0][j]
        else:
            acc = refs[0][...]
            for r in refs[1:-1]:
                acc = acc + r[...]
        o[...] = acc

    if stacked:
        n = terms.shape[0]
        args = [terms.reshape(n, rows, cols)]
        in_specs = [pl.BlockSpec((n, tr, cols), lambda i: (0, i, 0))]
    else:
        args = [t.reshape(rows, cols) for t in terms]
        in_specs = [pl.BlockSpec((tr, cols), lambda i: (i, 0))] * len(args)
    out = pl.pallas_call(
        body, name=name, grid=(rows // tr,), in_specs=in_specs, out_specs=pl.BlockSpec((tr, cols), lambda i: (i, 0)),
        out_shape=jax.ShapeDtypeStruct((rows, cols), f32), compiler_params=_cp(("parallel",)),
    )(*args)
    return out.reshape(shape)


def reduce_scatter_layer(l, G):
    grads = [G[n] if G[n].ndim == 3 or n == "w_in" else G[n].reshape(NCHIP, D // NCHIP, D) for n in BIG]
    mine, theirs = rs_pair(l, grads)
    sums = [add_n(f"rs_add_pair{l}_{n}", [m, t]) for n, m, t in zip(BIG, mine, theirs)]
    landed = rs_chips(l, sums)
    halves = [add_n(f"rs_add_chips{l}_{n}", v) for n, v in zip(BIG, landed)]
    return dict(zip(BIG, rs_join(l, halves)))


NDEV = 8


def allreduce_small(pack):
    rows = pack.shape[0]

    def body(p_ref, o_ref, buf, send, recv):
        x, y, c, _ = _place()
        me = 4 * x + 2 * y + c
        buf[me] = p_ref[...]

        def give(r, s, d):
            return _rcopy(s, d, send.at[r - 1], recv.at[r - 1], (x ^ (r >> 2), y ^ ((r >> 1) & 1), c ^ (r & 1)))

        for r in range(1, NDEV):
            _start_pieces(p_ref, buf.at[me], functools.partial(give, r), 128 << 10)
        for r in range(1, NDEV):
            give(r, p_ref, buf.at[me]).wait_send()
            src = 4 * (x ^ (r >> 2)) + 2 * (y ^ ((r >> 1) & 1)) + (c ^ (r & 1))
            give(r, p_ref, buf.at[src]).wait_recv()
        acc = buf[0]
        for d in range(1, NDEV):
            acc = acc + buf[d]
        o_ref[...] = acc

    vm = pl.BlockSpec(memory_space=pltpu.VMEM)
    return pl.pallas_call(
        body, name="allreduce_small", in_specs=[vm], out_specs=vm, out_shape=jax.ShapeDtypeStruct(pack.shape, f32),
        scratch_shapes=[pltpu.VMEM((NDEV, rows, LANES), f32), pltpu.SemaphoreType.DMA((NDEV - 1,)),
                        pltpu.SemaphoreType.DMA((NDEV - 1,))],
        compiler_params=pltpu.CompilerParams(vmem_limit_bytes=40 << 20),
    )(pack)


def _adamw_math(w, g, m, v):
    m = ADAM_B1 * m + (1.0 - ADAM_B1) * g
    v = ADAM_B2 * v + (1.0 - ADAM_B2) * (g * g)
    m_hat = m / (1.0 - ADAM_B1 ** ADAM_STEP)
    v_hat = v / (1.0 - ADAM_B2 ** ADAM_STEP)
    return -ADAM_LR * (m_hat / (jnp.sqrt(v_hat) + ADAM_EPS) + ADAM_WD * w), m, v


def adamw_big(name, g0, g1, w, m, v):
    _, R, C = w.shape
    tr = _row_tile(R, C, target=1 << 20)
    nt = R // tr

    def body(g0_r, g1_r, w_r, m_r, v_r, g_o, d_o, m_o, v_o):
        g = jnp.where(pl.program_id(0) == 0, g0_r[...], g1_r[...])
        g_o[...] = g
        d_o[...], m_o[...], v_o[...] = _adamw_math(w_r[...], g, m_r[...], v_r[...])

    stk = pl.BlockSpec((None, tr, C), lambda l, i: (l, i, 0))
    return pl.pallas_call(
        body, name=name, grid=(DEPTH, nt),
        in_specs=[pl.BlockSpec((tr, C), lambda l, i: (jnp.where(l == 0, i, nt - 1), 0)),
                  pl.BlockSpec((tr, C), lambda l, i: (jnp.where(l == 0, 0, i), 0)), stk, stk, stk],
        out_specs=[stk] * 4, out_shape=[jax.ShapeDtypeStruct(w.shape, f32)] * 4,
        compiler_params=_cp(("arbitrary", "arbitrary")),
    )(g0, g1, w, m, v)


def adamw_small(name, g, w, m, v):
    def body(g_r, w_r, m_r, v_r, d_o, m_o, v_o):
        d_o[...], m_o[...], v_o[...] = _adamw_math(w_r[...], g_r[...], m_r[...], v_r[...])

    return pl.pallas_call(body, name=name, out_shape=[jax.ShapeDtypeStruct(w.shape, f32)] * 3)(g, w, m, v)


WEIGHTS = ("w_in", "conv_w", "gmlp_ln_g", "gmlp_ln_b", "w_s", "b_s", "p_a", "p_b", "p_c", "w_o", "ln1_g", "ln1_b",
           "w_gate", "w_up", "w_down", "ln2_g", "ln2_b")
VECS = ("ln1_g", "ln1_b", "ln2_g", "ln2_b", "gmlp_ln_g", "gmlp_ln_b")
ROWS_VEC, ROWS_BS, ROWS_WS, ROWS_CONV = D // LANES, 8, 8 * BLK, 3 * D // LANES
ROWS_LAYER = len(VECS) * ROWS_VEC + ROWS_BS + ROWS_WS + ROWS_CONV


def _pack_small(per_layer, tail):
    parts = []
    for P in per_layer:
        parts += [P[n].reshape(ROWS_VEC, LANES) for n in VECS]
        parts += [P["b_s"].reshape(ROWS_BS, LANES), P["w_s"].reshape(ROWS_WS, LANES), P["conv_w"].reshape(ROWS_CONV, LANES)]
    return jnp.concatenate(parts + [tail], axis=0)


def _unpack_small(pack):
    out = []
    for l in range(DEPTH):
        r = l * ROWS_LAYER
        P = {}
        for n in VECS:
            P[n] = pack[r:r + ROWS_VEC].reshape(D)
            r += ROWS_VEC
        P["b_s"] = pack[r:r + ROWS_BS].reshape(8, BLK)
        r += ROWS_BS
        P["w_s"] = pack[r:r + ROWS_WS].reshape(8, BLK, BLK)
        r += ROWS_WS
        P["conv_w"] = pack[r:r + ROWS_CONV].reshape(3, D)
        out.append(P)
    return out, pack[DEPTH * ROWS_LAYER:]


def kernel(x, positions, w_in, conv_w, gmlp_ln_g, gmlp_ln_b, w_s, b_s, p_a, p_b, p_c, w_o, ln1_g, ln1_b, w_gate, w_up, w_down, ln2_g, ln2_b, loss_target, m_w_in, m_conv_w, m_gmlp_ln_g, m_gmlp_ln_b, m_w_s, m_b_s, m_p_a, m_p_b, m_p_c, m_w_o, m_ln1_g, m_ln1_b, m_w_gate, m_w_up, m_w_down, m_ln2_g, m_ln2_b, v_w_in, v_conv_w, v_gmlp_ln_g, v_gmlp_ln_b, v_w_s, v_b_s, v_p_a, v_p_b, v_p_c, v_w_o, v_ln1_g, v_ln1_b, v_w_gate, v_w_up, v_w_down, v_ln2_g, v_ln2_b):
    Wt = dict(w_in=w_in, conv_w=conv_w, gmlp_ln_g=gmlp_ln_g, gmlp_ln_b=gmlp_ln_b, w_s=w_s, b_s=b_s, p_a=p_a, p_b=p_b,
              p_c=p_c, w_o=w_o, ln1_g=ln1_g, ln1_b=ln1_b, w_gate=w_gate, w_up=w_up, w_down=w_down, ln2_g=ln2_g, ln2_b=ln2_b)
    Mt = dict(w_in=m_w_in, conv_w=m_conv_w, gmlp_ln_g=m_gmlp_ln_g, gmlp_ln_b=m_gmlp_ln_b, w_s=m_w_s, b_s=m_b_s, p_a=m_p_a,
              p_b=m_p_b, p_c=m_p_c, w_o=m_w_o, ln1_g=m_ln1_g, ln1_b=m_ln1_b, w_gate=m_w_gate, w_up=m_w_up,
              w_down=m_w_down, ln2_g=m_ln2_g, ln2_b=m_ln2_b)
    Vt = dict(w_in=v_w_in, conv_w=v_conv_w, gmlp_ln_g=v_gmlp_ln_g, gmlp_ln_b=v_gmlp_ln_b, w_s=v_w_s, b_s=v_b_s, p_a=v_p_a,
              p_b=v_p_b, p_c=v_p_c, w_o=v_w_o, ln1_g=v_ln1_g, ln1_b=v_ln1_b, w_gate=v_w_gate, w_up=v_w_up,
              w_down=v_w_down, ln2_g=v_ln2_g, ln2_b=v_ln2_b)
    chip = 2 * lax.axis_index("x") + lax.axis_index("y")
    cw = D // NCHIP

    full = gather_weights([Wt[n].astype(MX) for n in BIG] + [conv_w])
    layers = []
    for l in range(DEPTH):
        Wl = dict(zip(BIG, (f[l] for f in full[:NBIG])))
        for n in ("p_a", "p_c", "w_o"):
            Wl[n] = Wl[n].reshape(D, D)
        Wl["conv_w"] = full[NBIG][l].transpose(1, 0, 2).reshape(3, D)
        for n in VECS + ("w_s", "b_s"):
            Wl[n] = Wt[n][l]
        layers.append(Wl)

    lsum, grad_x, grads = local_step(x[0], positions[0], loss_target[0], layers)

    red = [None] * DEPTH
    for l in reversed(range(DEPTH)):
        red[l] = reduce_scatter_layer(l, grads[l])
    pack = _pack_small([{n: (g[n] if n not in VECS else g[n]) for n in VECS + ("b_s", "w_s", "conv_w")} for g in grads], lsum)
    small, tail = _unpack_small(allreduce_small(pack))
    loss = tail[0, 0]

    G, DW, NM, NV = {}, {}, {}, {}
    for n in BIG:
        G[n], DW[n], NM[n], NV[n] = adamw_big("adamw_" + n, red[0][n], red[1][n], Wt[n], Mt[n], Vt[n])
    zc = jnp.zeros((3, D), f32)
    wp = _pack_small([{**{n: Wt[n][l] for n in VECS + ("b_s", "w_s")}, "conv_w": zc} for l in range(DEPTH)], jnp.zeros((8, LANES), f32))
    mp = _pack_small([{**{n: Mt[n][l] for n in VECS + ("b_s", "w_s")}, "conv_w": zc} for l in range(DEPTH)], jnp.zeros((8, LANES), f32))
    vp = _pack_small([{**{n: Vt[n][l] for n in VECS + ("b_s", "w_s")}, "conv_w": zc} for l in range(DEPTH)], jnp.ones((8, LANES), f32))
    gp = _pack_small(small, jnp.zeros((8, LANES), f32))
    outs = [_unpack_small(a)[0] for a in adamw_small("adamw_small", gp, wp, mp, vp)]
    for n in VECS + ("b_s", "w_s"):
        G[n] = jnp.stack([small[l][n] for l in range(DEPTH)])
        DW[n], NM[n], NV[n] = (jnp.stack([o[l][n] for l in range(DEPTH)]) for o in outs)
    gconv = jnp.stack([lax.dynamic_slice(small[l]["conv_w"], (0, chip * cw), (3, cw)) for l in range(DEPTH)])
    G["conv_w"] = gconv
    flat = lambda a: a.reshape(DEPTH * 3, cw)
    d, m2, v2 = adamw_small("adamw_conv", flat(gconv), flat(conv_w), flat(m_conv_w), flat(v_conv_w))
    DW["conv_w"], NM["conv_w"], NV["conv_w"] = (a.reshape(DEPTH, 3, cw) for a in (d, m2, v2))

    return (loss, grad_x[None], *[G[n] for n in WEIGHTS], *[DW[n] for n in WEIGHTS], *[NM[n] for n in WEIGHTS],
            *[NV[n] for n in WEIGHTS])
```

```python
import functools
import math

import jax
import jax.numpy as jnp
from jax import lax
from jax.experimental import pallas as pl
from jax.experimental.pallas import tpu as pltpu

D = 1024
NIN = 12800
DFF = 2816
NCHIP = 4
FB = DFF // NCHIP
WIN_SHARD = NIN // NCHIP
DEPTH = 2
GROUPS = ((128, 1), (512, 4), (2048, 16))
HD = 64
BLK = 128
AO = 512
ALPHA = (2 * DEPTH) ** 0.25
EPS = 1e-5
ROPE_THETA = 10000.0
LANES = 128
NEG = -1e30

C_GATES, C_BCH, C_QKV, C_UV = 0, 3 * D, 6 * D, 6 * D + 9 * AO

MX = jnp.bfloat16
ACT = jnp.bfloat16

ADAM_LR, ADAM_B1, ADAM_B2, ADAM_EPS, ADAM_WD, ADAM_STEP = 0.001, 0.9, 0.999, 1e-08, 0.01, 10

f32 = jnp.float32
NT = (((1,), (1,)), ((), ()))
TN = (((0,), (0,)), ((), ()))


def _cp(sem, vmem_mb=48):
    return pltpu.CompilerParams(dimension_semantics=sem, vmem_limit_bytes=vmem_mb << 20)


def _dot(a, b, dims=None):
    if dims is None:
        return jnp.dot(a, b, preferred_element_type=f32)
    return lax.dot_general(a, b, dims, preferred_element_type=f32)


def _ln_stats(r):
    mu = jnp.mean(r, axis=-1, keepdims=True)
    xc = r - mu
    var = jnp.mean(xc * xc, axis=-1, keepdims=True)
    rstd = lax.rsqrt(var + EPS)
    return xc * rstd, rstd


def _ln_bwd(dy, xhat, rstd, g):
    dxh = dy * g
    return rstd * (dxh - jnp.mean(dxh, axis=-1, keepdims=True) - xhat * jnp.mean(dxh * xhat, axis=-1, keepdims=True))


def _gelu(x):
    return 0.5 * x * (1.0 + lax.erf(x * (1.0 / math.sqrt(2.0))))


def _gelu_grad(x):
    return 0.5 * (1.0 + lax.erf(x * (1.0 / math.sqrt(2.0)))) + x * jnp.exp(-0.5 * x * x) * (1.0 / math.sqrt(2.0 * math.pi))


def _sigmoid(x):
    return 1.0 / (1.0 + jnp.exp(-x))


def _acc_rows(o_ref, first, val):
    @pl.when(first)
    def _():
        o_ref[...] = jnp.zeros_like(o_ref)
    o_ref[...] += jnp.sum(val, axis=0, keepdims=True)


def mm_in(x, w):
    T = x.shape[0]
    tm, tn = min(1024, T), 512

    def body(x_ref, w_ref, o_ref, xb):
        @pl.when(pl.program_id(1) == 0)
        def _():
            xb[...] = x_ref[...].astype(MX)
        o_ref[...] = _dot(xb[...], w_ref[...]).astype(o_ref.dtype)

    return pl.pallas_call(
        body, name="mm_in", grid=(T // tm, NIN // tn),
        in_specs=[pl.BlockSpec((tm, D), lambda i, j: (i, 0)), pl.BlockSpec((D, tn), lambda i, j: (0, j))],
        out_specs=pl.BlockSpec((tm, tn), lambda i, j: (i, j)),
        out_shape=jax.ShapeDtypeStruct((T, NIN), ACT),
        scratch_shapes=[pltpu.VMEM((tm, D), MX)],
        compiler_params=_cp(("parallel", "arbitrary")),
    )(x, w)


HALO = 16
TM_AC = 256


def _uv_specs():
    return [pl.BlockSpec((TM_AC, 512), functools.partial(lambda i, j: (i, j), j=C_UV // 512 + j)) for j in range(4)]


def _gmlp_fwd(up, vp, ws_ref, bs_ref, lg, lb):
    u = _gelu(up)
    xhat, rstd = _ln_stats(_gelu(vp))
    vn = xhat * lg + lb
    vnb = vn.astype(MX)
    rows = []
    for c in range(up.shape[0] // BLK):
        r = slice(c * BLK, (c + 1) * BLK)
        rows.append(jnp.concatenate(
            [_dot(ws_ref[g], vnb[r, g * BLK:(g + 1) * BLK]) + bs_ref[g] for g in range(8)], axis=1))
    return u, vn, xhat, rstd, jnp.concatenate(rows, axis=0)


def mix_ac_fwd(proj, conv_w, wst, bsx, lg, lb):
    T = proj.shape[0]
    tm = TM_AC

    def body(bch, halo, u0, u1, v0, v1, cw, ws, bs, lg_ref, lb_ref, ya, yc, zs):
        i = pl.program_id(0)
        pb = bch[...].astype(f32)
        z = pb[:, D:2 * D] * pb[:, 2 * D:]
        hz = halo[:, :D].astype(f32) * halo[:, D:].astype(f32)
        zs[0:HALO, :] = jnp.where(i > 0, hz, 0.0)
        zs[HALO:HALO + tm, :] = z
        cv = cw[0:1, :] * zs[HALO - 2:HALO - 2 + tm, :] + cw[1:2, :] * zs[HALO - 1:HALO - 1 + tm, :] + cw[2:3, :] * z
        ya[...] = (pb[:, :D] * cv).astype(ya.dtype)
        up = jnp.concatenate([u0[...], u1[...]], axis=1).astype(f32)
        vp = jnp.concatenate([v0[...], v1[...]], axis=1).astype(f32)
        u, _, _, _, sp = _gmlp_fwd(up, vp, ws, bs, lg_ref[...], lb_ref[...])
        yc[...] = (u * sp).astype(yc.dtype)

    full = lambda shape: pl.BlockSpec(shape, lambda i: (0,) * len(shape))
    return pl.pallas_call(
        body, name="mix_ac_fwd", grid=(T // tm,),
        in_specs=[pl.BlockSpec((tm, 3 * D), lambda i: (i, 1)),
                  pl.BlockSpec((HALO, 2 * D), lambda i: (jnp.maximum(i * (tm // HALO) - 1, 0), 2)),
                  *_uv_specs(), full((3, D)), full((8, BLK, BLK)), full((8, BLK, BLK)), full((1, D)), full((1, D))],
        out_specs=[pl.BlockSpec((tm, D), lambda i: (i, 0))] * 2,
        out_shape=[jax.ShapeDtypeStruct((T, D), MX)] * 2,
        scratch_shapes=[pltpu.VMEM((HALO + tm, D), f32)],
        compiler_params=_cp(("parallel",)),
    )(proj, proj, proj, proj, proj, proj, conv_w, wst, bsx, lg, lb)


def _swap_halves(x):
    lane = lax.broadcasted_iota(jnp.int32, x.shape, 1)
    return jnp.where((lane % HD) < HD // 2, pltpu.roll(x, x.shape[1] - HD // 2, 1), pltpu.roll(x, HD // 2, 1))


def _tile4(t):
    return jnp.concatenate([t] * (AO // LANES), axis=1)


TM_FOLD = 512


def _fold_out(nat, x, out_ref, d):
    if d == 1:
        out_ref[0] = x.astype(out_ref.dtype)
        return
    rows = x.shape[0] // d
    for j in range(AO // LANES):
        nat[j] = x[:, j * LANES:(j + 1) * LANES]
    for r in range(d):
        out_ref[r] = jnp.concatenate(
            [nat.at[j][pl.ds(r, rows, stride=d), :] for j in range(AO // LANES)], axis=1).astype(out_ref.dtype)


def _unfold_in(nat, in_ref, d):
    if d == 1:
        return in_ref[0].astype(f32)
    rows = in_ref.shape[1]
    for r in range(d):
        v = in_ref[r].astype(f32)
        for j in range(AO // LANES):
            nat.at[j][pl.ds(r, rows, stride=d), :] = v[:, j * LANES:(j + 1) * LANES]
    return jnp.concatenate([nat[j] for j in range(AO // LANES)], axis=1)


def fold_rope(proj, cos_t, sin_t, g, d):
    T = proj.shape[0]
    tm = TM_FOLD
    rows = tm // d

    def body(x_ref, c_ref, s_ref, q_o, k_o, v_o, nat):
        cos, sin = _tile4(c_ref[...]), _tile4(s_ref[...])
        for part, out, scale in ((0, q_o, HD ** -0.5), (1, k_o, 1.0), (2, v_o, None)):
            x = x_ref[:, part * AO:(part + 1) * AO].astype(f32)
            if scale is not None:
                x = (x * cos + _swap_halves(x) * sin) * scale
            _fold_out(nat, x, out, d)

    fold_spec = pl.BlockSpec((d, rows, AO), lambda i: (0, i, 0))
    return pl.pallas_call(
        body, name=f"fold_rope{g}", grid=(T // tm,),
        in_specs=[pl.BlockSpec((tm, 3 * AO), lambda i: (i, C_QKV // (3 * AO) + g)),
                  pl.BlockSpec((tm, LANES), lambda i: (i, 0)), pl.BlockSpec((tm, LANES), lambda i: (i, 0))],
        out_specs=[fold_spec] * 3,
        out_shape=[jax.ShapeDtypeStruct((d, T // d, AO), MX)] * 3,
        scratch_shapes=[pltpu.VMEM((AO // LANES, tm, LANES), f32)],
        compiler_params=_cp(("parallel",)),
    )(proj, cos_t, sin_t)


def _attn_masks():
    row = lax.broadcasted_iota(jnp.int32, (BLK, BLK), 0)
    col = lax.broadcasted_iota(jnp.int32, (BLK, BLK), 1)
    return col <= row, col >= row, col < HD


def attn_fwd(qf, kf, vf, g, nb):
    T = qf.shape[0]

    def body(q_ref, k_ref, v_ref, o_ref, l_ref):
        lower, upper, head0 = _attn_masks()

        def step(b, carry):
            r0 = pl.multiple_of(b * BLK, BLK)
            rp = pl.multiple_of(jnp.maximum(b - 1, 0) * BLK, BLK)
            q, kc, vc = q_ref[pl.ds(r0, BLK), :], k_ref[pl.ds(r0, BLK), :], v_ref[pl.ds(r0, BLK), :]
            kp, vp = k_ref[pl.ds(rp, BLK), :], v_ref[pl.ds(rp, BLK), :]
            mask_p = upper & ((b % nb) != 0)
            outs = []
            for hm in (head0, ~head0):
                qh = jnp.where(hm, q, jnp.zeros_like(q))
                sc = jnp.where(lower, _dot(qh, kc, NT), NEG)
                sp = jnp.where(mask_p, _dot(qh, kp, NT), NEG)
                m = jnp.maximum(jnp.max(sc, axis=-1, keepdims=True), jnp.max(sp, axis=-1, keepdims=True))
                pc, pp = jnp.exp(sc - m), jnp.exp(sp - m)
                l = jnp.sum(pc, axis=-1, keepdims=True) + jnp.sum(pp, axis=-1, keepdims=True)
                o = (_dot(pc.astype(MX), vc) + _dot(pp.astype(MX), vp)) / l
                outs.append((o, m + jnp.log(l)))
            o_ref[pl.ds(r0, BLK), :] = jnp.where(head0, outs[0][0], outs[1][0])
            l_ref[pl.ds(r0, BLK), :] = jnp.where(head0, outs[0][1], outs[1][1])
            return carry

        lax.fori_loop(0, T // BLK, step, 0)

    spec = pl.BlockSpec((T, LANES), lambda j: (0, j))
    return pl.pallas_call(
        body, name=f"attn_fwd{g}", grid=(AO // LANES,),
        in_specs=[spec] * 3, out_specs=[spec] * 2,
        out_shape=[jax.ShapeDtypeStruct((T, AO), f32)] * 2,
        compiler_params=_cp(("parallel",), 56),
    )(qf, kf, vf)


def _group_weights(lses):
    m = jnp.maximum(jnp.maximum(lses[0], lses[1]), lses[2])
    e = [jnp.exp(l - m) for l in lses]
    inv = 1.0 / (e[0] + e[1] + e[2])
    return [x * inv for x in e]


def _fold_specs(T, tm):
    specs = []
    for _, d in GROUPS:
        specs.append(pl.BlockSpec((d, tm // d, AO), lambda i: (0, i, 0)))
    return specs


def combine_fwd(os_, lses):
    T = os_[0].shape[0] * os_[0].shape[1]
    tm = TM_FOLD

    def body(o0, o1, o2, l0, l1, l2, y_ref, nat):
        o = [_unfold_in(nat, r, d) for r, (_, d) in zip((o0, o1, o2), GROUPS)]
        ls = [_unfold_in(nat, r, d) for r, (_, d) in zip((l0, l1, l2), GROUPS)]
        w = _group_weights(ls)
        y_ref[...] = (w[0] * o[0] + w[1] * o[1] + w[2] * o[2]).astype(y_ref.dtype)

    specs = _fold_specs(T, tm)
    return pl.pallas_call(
        body, name="combine_fwd", grid=(T // tm,),
        in_specs=specs + specs, out_specs=pl.BlockSpec((tm, AO), lambda i: (i, 0)),
        out_shape=jax.ShapeDtypeStruct((T, AO), MX),
        scratch_shapes=[pltpu.VMEM((AO // LANES, tm, LANES), f32)],
        compiler_params=_cp(("parallel",)),
    )(*os_, *lses)


TM_MIX = 256


def mix_out_fwd(proj, ya, yb, yc, x0, pa, pb, pc, wo, g1, b1):
    T = x0.shape[0]
    tm = min(TM_MIX, T)

    def body(gt, ya_r, yb_r, yc_r, x0_r, pa_r, pb_r, pc_r, wo_r, g_r, b_r, mabc, m_o, r1_o, x1_o):
        ma = _dot(ya_r[...], pa_r[...])
        ybv = yb_r[...]
        mb = jnp.concatenate([_dot(ybv, pb_r[k]) for k in range(NCHIP)], axis=1)
        mc = _dot(yc_r[...], pc_r[...])
        m = jnp.zeros((tm, D), f32)
        for j, mm in enumerate((ma, mb, mc)):
            mabc[:, j * D:(j + 1) * D] = mm.astype(mabc.dtype)
            m = m + _sigmoid(gt[:, j * D:(j + 1) * D].astype(f32)) * mm
        mb16 = m.astype(MX)
        m_o[...] = mb16
        r1 = ALPHA * x0_r[...] + _dot(mb16, wo_r[...])
        r1_o[...] = r1
        xhat, _ = _ln_stats(r1)
        x1_o[...] = xhat * g_r[...] + b_r[...]

    full = lambda shape: pl.BlockSpec(shape, lambda i: (0,) * len(shape))
    tile = lambda w: pl.BlockSpec((tm, w), lambda i: (i, 0))
    return pl.pallas_call(
        body, name="mix_out_fwd", grid=(T // tm,),
        in_specs=[tile(3 * D), tile(D), tile(AO), tile(D), tile(D), full((D, D)), full((NCHIP, AO, D // NCHIP)),
                  full((D, D)), full((D, D)), full((1, D)), full((1, D))],
        out_specs=[tile(3 * D), tile(D), tile(D), tile(D)],
        out_shape=[jax.ShapeDtypeStruct((T, 3 * D), MX), jax.ShapeDtypeStruct((T, D), MX),
                   jax.ShapeDtypeStruct((T, D), f32), jax.ShapeDtypeStruct((T, D), f32)],
        compiler_params=_cp(("parallel",), 56),
    )(proj, ya, yb, yc, x0, pa, pb, pc, wo, g1, b1)


TM_FF = 512


def ffn_up_fwd(x1, wg, wu):
    T = x1.shape[0]
    tm = min(TM_FF, T)

    def body(x_r, wg_r, wu_r, g_o, u_o, h_o, xb):
        @pl.when(pl.program_id(1) == 0)
        def _():
            xb[...] = x_r[...].astype(MX)
        gate = _dot(xb[...], wg_r[0])
        up = _dot(xb[...], wu_r[0])
        g_o[0] = gate.astype(g_o.dtype)
        u_o[0] = up.astype(u_o.dtype)
        h_o[0] = (gate * _sigmoid(gate) * up).astype(h_o.dtype)

    wspec = pl.BlockSpec((1, D, FB), lambda i, k: (k, 0, 0))
    ospec = pl.BlockSpec((1, tm, FB), lambda i, k: (k, i, 0))
    return pl.pallas_call(
        body, name="ffn_up_fwd", grid=(T // tm, NCHIP),
        in_specs=[pl.BlockSpec((tm, D), lambda i, k: (i, 0)), wspec, wspec],
        out_specs=[ospec] * 3,
        out_shape=[jax.ShapeDtypeStruct((NCHIP, T, FB), ACT)] * 2 + [jax.ShapeDtypeStruct((NCHIP, T, FB), MX)],
        scratch_shapes=[pltpu.VMEM((tm, D), MX)],
        compiler_params=_cp(("parallel", "arbitrary")),
    )(x1, wg, wu)


def ffn_down_fwd(hh, wd, x1, g2, b2):
    T = x1.shape[0]
    tm = min(TM_FF, T)

    def body(h_r, w_r, x_r, g_r, b_r, r2_o, x2_o):
        r2 = ALPHA * x_r[...]
        for k in range(NCHIP):
            r2 = r2 + _dot(h_r[k], w_r[k])
        r2_o[...] = r2
        xhat, _ = _ln_stats(r2)
        x2_o[...] = xhat * g_r[...] + b_r[...]

    tile = pl.BlockSpec((tm, D), lambda i: (i, 0))
    vec = pl.BlockSpec((1, D), lambda i: (0, 0))
    return pl.pallas_call(
        body, name="ffn_down_fwd", grid=(T // tm,),
        in_specs=[pl.BlockSpec((NCHIP, tm, FB), lambda i: (0, i, 0)), pl.BlockSpec((NCHIP, FB, D), lambda i: (0, 0, 0)),
                  tile, vec, vec],
        out_specs=[tile, tile], out_shape=[jax.ShapeDtypeStruct((T, D), f32)] * 2,
        compiler_params=_cp(("parallel",)),
    )(hh, wd, x1, g2, b2)


def loss_grad(y, tgt):
    T = y.shape[0]
    tm = min(512, T)

    def body(y_r, t_r, l_o, dy_o):
        e = y_r[...] - t_r[...]
        dy_o[...] = e * (1.0 / D)

        @pl.when(pl.program_id(0) == 0)
        def _():
            l_o[...] = jnp.zeros_like(l_o)
        l_o[...] += (0.5 / D) * jnp.sum(e * e)

    tile = pl.BlockSpec((tm, D), lambda i: (i, 0))
    return pl.pallas_call(
        body, name="loss_grad", grid=(T // tm,),
        in_specs=[tile, tile], out_specs=[pl.BlockSpec((8, LANES), lambda i: (0, 0)), tile],
        out_shape=[jax.ShapeDtypeStruct((8, LANES), f32), jax.ShapeDtypeStruct((T, D), f32)],
        compiler_params=_cp(("arbitrary",)),
    )(y, tgt)


def ffn_down_bwd(dx2, r2, g2, wd, gate, up):
    T = dx2.shape[0]
    tm = min(TM_FF, T)

    def body(dx_r, r_r, g_r, w_r, ga_r, up_r, dr_o, dg_o, du_o, dlg_o, dlb_o, drb):
        i, k = pl.program_id(0), pl.program_id(1)

        @pl.when(k == 0)
        def _():
            xhat, rstd = _ln_stats(r_r[...])
            dx = dx_r[...]
            _acc_rows(dlg_o, i == 0, dx * xhat)
            _acc_rows(dlb_o, i == 0, dx)
            dr = _ln_bwd(dx, xhat, rstd, g_r[...])
            dr_o[...] = dr
            drb[...] = dr.astype(MX)

        dhh = _dot(drb[...], w_r[0], NT)
        gate_v, up_v = ga_r[0].astype(f32), up_r[0].astype(f32)
        sg = _sigmoid(gate_v)
        dg_o[0] = (dhh * up_v * sg * (1.0 + gate_v * (1.0 - sg))).astype(dg_o.dtype)
        du_o[0] = (dhh * gate_v * sg).astype(du_o.dtype)

    tile = pl.BlockSpec((tm, D), lambda i, k: (i, 0))
    vec = pl.BlockSpec((1, D), lambda i, k: (0, 0))
    blk = pl.BlockSpec((1, tm, FB), lambda i, k: (k, i, 0))
    return pl.pallas_call(
        body, name="ffn_down_bwd", grid=(T // tm, NCHIP),
        in_specs=[tile, tile, vec, pl.BlockSpec((1, FB, D), lambda i, k: (k, 0, 0)), blk, blk],
        out_specs=[tile, blk, blk, vec, vec],
        out_shape=[jax.ShapeDtypeStruct((T, D), f32)] + [jax.ShapeDtypeStruct((NCHIP, T, FB), MX)] * 2
        + [jax.ShapeDtypeStruct((1, D), f32)] * 2,
        scratch_shapes=[pltpu.VMEM((tm, D), MX)],
        compiler_params=_cp(("arbitrary", "arbitrary")),
    )(dx2, r2, g2, wd, gate, up)


def ffn_up_bwd(dr2, dgate, dup, wg, wu, r1, g1):
    T = dr2.shape[0]
    tm = min(TM_FF, T)

    def body(dr2_r, dg_r, du_r, wg_r, wu_r, r1_r, g_r, dr1_o, dlg_o, dlb_o, acc):
        i, k = pl.program_id(0), pl.program_id(1)

        @pl.when(k == 0)
        def _():
            acc[...] = ALPHA * dr2_r[...]
        acc[...] += _dot(dg_r[0], wg_r[0], NT) + _dot(du_r[0], wu_r[0], NT)

        @pl.when(k == NCHIP - 1)
        def _():
            dx = acc[...]
            xhat, rstd = _ln_stats(r1_r[...])
            _acc_rows(dlg_o, i == 0, dx * xhat)
            _acc_rows(dlb_o, i == 0, dx)
            dr1_o[...] = _ln_bwd(dx, xhat, rstd, g_r[...])

    tile = pl.BlockSpec((tm, D), lambda i, k: (i, 0))
    vec = pl.BlockSpec((1, D), lambda i, k: (0, 0))
    blk = pl.BlockSpec((1, tm, FB), lambda i, k: (k, i, 0))
    wspec = pl.BlockSpec((1, D, FB), lambda i, k: (k, 0, 0))
    return pl.pallas_call(
        body, name="ffn_up_bwd", grid=(T // tm, NCHIP),
        in_specs=[tile, blk, blk, wspec, wspec, tile, vec],
        out_specs=[tile, vec, vec],
        out_shape=[jax.ShapeDtypeStruct((T, D), f32)] + [jax.ShapeDtypeStruct((1, D), f32)] * 2,
        scratch_shapes=[pltpu.VMEM((tm, D), f32)],
        compiler_params=_cp(("arbitrary", "arbitrary")),
    )(dr2, dgate, dup, wg, wu, r1, g1)


def mix_out_bwd(dr1, proj, mabc, wo, pa, pb, pc):
    T = dr1.shape[0]
    tm = min(TM_MIX, T)

    def body(dr_r, gt, mabc_r, wo_r, pa_r, pb_r, pc_r, dmabc_o, dgt_o, dya_o, dyb_o, dyc_o):
        dm = _dot(dr_r[...].astype(MX), wo_r[...], NT)
        dmx = []
        for j in range(3):
            s = _sigmoid(gt[:, j * D:(j + 1) * D].astype(f32))
            v = (dm * s).astype(MX)
            dmx.append(v)
            dmabc_o[:, j * D:(j + 1) * D] = v
            dgt_o[:, j * D:(j + 1) * D] = (dm * mabc_r[:, j * D:(j + 1) * D].astype(f32) * s * (1.0 - s)).astype(dgt_o.dtype)
        dya_o[...] = _dot(dmx[0], pa_r[...], NT)
        dyb = jnp.zeros((tm, AO), f32)
        for k in range(NCHIP):
            dyb = dyb + _dot(dmx[1][:, k * (D // NCHIP):(k + 1) * (D // NCHIP)], pb_r[k], NT)
        dyb_o[...] = dyb
        dyc_o[...] = _dot(dmx[2], pc_r[...], NT)

    full = lambda shape: pl.BlockSpec(shape, lambda i: (0,) * len(shape))
    tile = lambda w: pl.BlockSpec((tm, w), lambda i: (i, 0))
    return pl.pallas_call(
        body, name="mix_out_bwd", grid=(T // tm,),
        in_specs=[tile(D), tile(3 * D), tile(3 * D), full((D, D)), full((D, D)), full((NCHIP, AO, D // NCHIP)), full((D, D))],
        out_specs=[tile(3 * D), tile(3 * D), tile(D), tile(AO), tile(D)],
        out_shape=[jax.ShapeDtypeStruct((T, 3 * D), MX), jax.ShapeDtypeStruct((T, 3 * D), MX),
                   jax.ShapeDtypeStruct((T, D), f32), jax.ShapeDtypeStruct((T, AO), f32), jax.ShapeDtypeStruct((T, D), f32)],
        compiler_params=_cp(("parallel",), 56),
    )(dr1, proj, mabc, wo, pa, pb, pc)


def transpose_cast(x):
    T = x.shape[0]
    tm = min(512, T)

    def body(x_r, o_r):
        o_r[...] = x_r[...].T.astype(o_r.dtype)

    return pl.pallas_call(
        body, name="transpose_cast", grid=(T // tm,),
        in_specs=[pl.BlockSpec((tm, D), lambda i: (i, 0))], out_specs=pl.BlockSpec((D, tm), lambda i: (0, i)),
        out_shape=jax.ShapeDtypeStruct((D, T), MX), compiler_params=_cp(("parallel",)),
    )(x)


def tn_matmul(name, a, b, a_spec, b_spec, out_shape, out_spec, grid, a_is_t=False):
    nt = len(grid) - 1

    def body(a_r, b_r, o_r):
        @pl.when(pl.program_id(nt) == 0)
        def _():
            o_r[...] = jnp.zeros_like(o_r)
        av = a_r[...].reshape(a_r.shape[-2:]).astype(MX)
        bv = b_r[...].reshape(b_r.shape[-2:]).astype(MX)
        o_r[...] += _dot(av, bv, None if a_is_t else TN).reshape(o_r.shape)

    return pl.pallas_call(
        body, name=name, grid=grid, in_specs=[a_spec, b_spec], out_specs=out_spec,
        out_shape=jax.ShapeDtypeStruct(out_shape, f32),
        compiler_params=_cp(("parallel",) * nt + ("arbitrary",), 56),
    )(a, b)


def attn_pre_bwd(dyb, os_, lses):
    T = dyb.shape[0]
    tm = TM_FOLD

    def body(dy_r, o0, o1, o2, l0, l1, l2, ones_r, d0, d1, d2, f0, f1, f2, nat):
        o = [_unfold_in(nat, r, d) for r, (_, d) in zip((o0, o1, o2), GROUPS)]
        ls = [_unfold_in(nat, r, d) for r, (_, d) in zip((l0, l1, l2), GROUPS)]
        w = _group_weights(ls)
        dy = dy_r[...]
        t = dy * (w[0] * o[0] + w[1] * o[1] + w[2] * o[2])
        hi = t.astype(MX)
        lo = (t - hi.astype(f32)).astype(MX)
        c = _dot(hi, ones_r[...]) + _dot(lo, ones_r[...])
        for wg, do_o, df_o, (_, d) in zip(w, (d0, d1, d2), (f0, f1, f2), GROUPS):
            _fold_out(nat, wg * dy, do_o, d)
            _fold_out(nat, -wg * c, df_o, d)

    specs = _fold_specs(T, tm)
    return pl.pallas_call(
        body, name="attn_pre_bwd", grid=(T // tm,),
        in_specs=[pl.BlockSpec((tm, AO), lambda i: (i, 0))] + specs + specs + [pl.BlockSpec((AO, AO), lambda i: (0, 0))],
        out_specs=specs + specs,
        out_shape=[jax.ShapeDtypeStruct((d, T // d, AO), MX) for _, d in GROUPS]
        + [jax.ShapeDtypeStruct((d, T // d, AO), f32) for _, d in GROUPS],
        scratch_shapes=[pltpu.VMEM((AO // LANES, tm, LANES), f32)],
        compiler_params=_cp(("parallel",)),
    )(dyb, *os_, *lses, _head_ones())


def _head_ones():
    i = jnp.arange(AO) // HD
    return (i[:, None] == i[None, :]).astype(MX)


def attn_bwd(qf, kf, vf, dof, lse, df, g, nb):
    T = qf.shape[0]

    def body(q_ref, k_ref, v_ref, do_ref, l_ref, d_ref, dq_ref, dk_ref, dv_ref):
        lower, upper, head0 = _attn_masks()

        def step(b, carry):
            dk_c, dv_c = carry
            r0 = pl.multiple_of(b * BLK, BLK)
            rp = pl.multiple_of(jnp.maximum(b - 1, 0) * BLK, BLK)
            q, kc, vc = q_ref[pl.ds(r0, BLK), :], k_ref[pl.ds(r0, BLK), :], v_ref[pl.ds(r0, BLK), :]
            kp, vp = k_ref[pl.ds(rp, BLK), :], v_ref[pl.ds(rp, BLK), :]
            do, lse_v, d_v = do_ref[pl.ds(r0, BLK), :], l_ref[pl.ds(r0, BLK), :], d_ref[pl.ds(r0, BLK), :]
            mask_p = upper & ((b % nb) != 0)
            dq = []
            dk_new, dv_new, dk_prev, dv_prev = (jnp.zeros((BLK, LANES), f32) for _ in range(4))
            for h, hm in enumerate((head0, ~head0)):
                zero = jnp.zeros_like(q)
                qh, doh = jnp.where(hm, q, zero), jnp.where(hm, do, zero)
                lse_h, d_h = lse_v[:, h * HD:h * HD + 1], d_v[:, h * HD:h * HD + 1]
                pc = jnp.where(lower, jnp.exp(_dot(qh, kc, NT) - lse_h), 0.0)
                pp = jnp.where(mask_p, jnp.exp(_dot(qh, kp, NT) - lse_h), 0.0)
                dsc = (pc * (_dot(doh, vc, NT) + d_h)).astype(MX)
                dsp = (pp * (_dot(doh, vp, NT) + d_h)).astype(MX)
                dq.append(_dot(dsc, kc) + _dot(dsp, kp))
                dk_new = dk_new + _dot(dsc, qh, TN)
                dk_prev = dk_prev + _dot(dsp, qh, TN)
                dv_new = dv_new + _dot(pc.astype(MX), doh, TN)
                dv_prev = dv_prev + _dot(pp.astype(MX), doh, TN)
            dq_ref[pl.ds(r0, BLK), :] = jnp.where(head0, dq[0], dq[1]).astype(dq_ref.dtype)
            dk_ref[pl.ds(rp, BLK), :] = (dk_c + dk_prev).astype(dk_ref.dtype)
            dv_ref[pl.ds(rp, BLK), :] = (dv_c + dv_prev).astype(dv_ref.dtype)
            return dk_new, dv_new

        zero = jnp.zeros((BLK, LANES), f32)
        dk_c, dv_c = lax.fori_loop(0, T // BLK, step, (zero, zero))
        dk_ref[pl.ds(T - BLK, BLK), :] = dk_c.astype(dk_ref.dtype)
        dv_ref[pl.ds(T - BLK, BLK), :] = dv_c.astype(dv_ref.dtype)

    spec = pl.BlockSpec((T, LANES), lambda j: (0, j))
    return pl.pallas_call(
        body, name=f"attn_bwd{g}", grid=(AO // LANES,),
        in_specs=[spec] * 6, out_specs=[spec] * 3,
        out_shape=[jax.ShapeDtypeStruct((T, AO), MX)] * 3,
        compiler_params=_cp(("parallel",), 60),
    )(qf, kf, vf, dof, lse, df)


def unfold_rope_bwd(dqf, dkf, dvf, cos_t, sin_t, g, d):
    T = dqf.shape[0] * dqf.shape[1]
    tm = TM_FOLD

    def body(q_r, k_r, v_r, c_ref, s_ref, o_ref, nat):
        cos, sin = _tile4(c_ref[...]), _tile4(s_ref[...])
        for part, ref, scale in ((0, q_r, HD ** -0.5), (1, k_r, 1.0), (2, v_r, None)):
            x = _unfold_in(nat, ref, d)
            if scale is not None:
                x = (x * cos - _swap_halves(x) * sin) * scale
            o_ref[:, part * AO:(part + 1) * AO] = x.astype(o_ref.dtype)

    fold_spec = pl.BlockSpec((d, tm // d, AO), lambda i: (0, i, 0))
    tab = pl.BlockSpec((tm, LANES), lambda i: (i, 0))
    return pl.pallas_call(
        body, name=f"unfold_rope_bwd{g}", grid=(T // tm,),
        in_specs=[fold_spec] * 3 + [tab, tab],
        out_specs=pl.BlockSpec((tm, 3 * AO), lambda i: (i, 0)),
        out_shape=jax.ShapeDtypeStruct((T, 3 * AO), MX),
        scratch_shapes=[pltpu.VMEM((AO // LANES, tm, LANES), f32)],
        compiler_params=_cp(("parallel",)),
    )(dqf, dkf, dvf, cos_t, sin_t)


def conv_bwd(dya, proj, conv_w):
    T = dya.shape[0]
    tm = TM_AC
    last = T // tm - 1

    def body(dy_r, bch, hprev, dy_next, b_next, cw, d_o, dw_o, zs, ds):
        i = pl.program_id(0)
        pb = bch[...].astype(f32)
        bp, cp, hp = pb[:, :D], pb[:, D:2 * D], pb[:, 2 * D:]
        z = cp * hp
        hz = hprev[:, :D].astype(f32) * hprev[:, D:].astype(f32)
        zs[0:HALO, :] = jnp.where(i > 0, hz, 0.0)
        zs[HALO:HALO + tm, :] = z
        z2, z1 = zs[HALO - 2:HALO - 2 + tm, :], zs[HALO - 1:HALO - 1 + tm, :]
        cv = cw[0:1, :] * z2 + cw[1:2, :] * z1 + cw[2:3, :] * z
        dy = dy_r[...]
        dcv = dy * bp
        ds[0:tm, :] = dcv
        ds[tm:tm + HALO, :] = jnp.where(i < last, dy_next[...] * b_next[...].astype(f32), 0.0)
        dz = cw[2:3, :] * dcv + cw[1:2, :] * ds[1:1 + tm, :] + cw[0:1, :] * ds[2:2 + tm, :]
        d_o[:, :D] = (dy * cv).astype(d_o.dtype)
        d_o[:, D:2 * D] = (dz * hp).astype(d_o.dtype)
        d_o[:, 2 * D:] = (dz * cp).astype(d_o.dtype)

        @pl.when(i == 0)
        def _():
            dw_o[...] = jnp.zeros_like(dw_o)
        dw_o[0:1, :] += jnp.sum(dcv * z2, axis=0, keepdims=True)
        dw_o[1:2, :] += jnp.sum(dcv * z1, axis=0, keepdims=True)
        dw_o[2:3, :] += jnp.sum(dcv * z, axis=0, keepdims=True)

    nh = tm // HALO
    return pl.pallas_call(
        body, name="conv_bwd", grid=(T // tm,),
        in_specs=[pl.BlockSpec((tm, D), lambda i: (i, 0)), pl.BlockSpec((tm, 3 * D), lambda i: (i, 1)),
                  pl.BlockSpec((HALO, 2 * D), lambda i: (jnp.maximum(i * nh - 1, 0), 2)),
                  pl.BlockSpec((HALO, D), lambda i: (jnp.minimum((i + 1) * nh, T // HALO - 1), 0)),
                  pl.BlockSpec((HALO, D), lambda i: (jnp.minimum((i + 1) * nh, T // HALO - 1), 3)),
                  pl.BlockSpec((3, D), lambda i: (0, 0))],
        out_specs=[pl.BlockSpec((tm, 3 * D), lambda i: (i, 0)), pl.BlockSpec((3, D), lambda i: (0, 0))],
        out_shape=[jax.ShapeDtypeStruct((T, 3 * D), MX), jax.ShapeDtypeStruct((3, D), f32)],
        scratch_shapes=[pltpu.VMEM((HALO + tm, D), f32), pltpu.VMEM((tm + HALO, D), f32)],
        compiler_params=_cp(("arbitrary",)),
    )(dya, proj, proj, dya, proj, conv_w)


def gmlp_bwd(dyc, proj, wst, bsx, lg, lb):
    T = dyc.shape[0]
    tm = TM_AC
    last = T // tm - 1

    def body(dy_r, u0, u1, v0, v1, ws, bs, lg_r, lb_r, d_o, dws_o, dbs_o, dlg_o, dlb_o, bacc):
        i = pl.program_id(0)
        up = jnp.concatenate([u0[...], u1[...]], axis=1).astype(f32)
        vp = jnp.concatenate([v0[...], v1[...]], axis=1).astype(f32)
        u, vn, xhat, rstd, sp = _gmlp_fwd(up, vp, ws, bs, lg_r[...], lb_r[...])
        dy = dy_r[...]
        d_o[:, :D] = (dy * sp * _gelu_grad(up)).astype(d_o.dtype)
        dsp = dy * u
        dspb, vnb = dsp.astype(MX), vn.astype(MX)

        @pl.when(i == 0)
        def _():
            dws_o[...] = jnp.zeros_like(dws_o)
            bacc[...] = jnp.zeros_like(bacc)

        rows = []
        for c in range(tm // BLK):
            r = slice(c * BLK, (c + 1) * BLK)
            cols = []
            for g in range(8):
                cs = slice(g * BLK, (g + 1) * BLK)
                dws_o[g] += _dot(dspb[r, cs], vnb[r, cs], NT)
                bacc[g] += dsp[r, cs]
                cols.append(_dot(ws[g], dspb[r, cs], TN))
            rows.append(jnp.concatenate(cols, axis=1))
        dvn = jnp.concatenate(rows, axis=0)
        _acc_rows(dlg_o, i == 0, dvn * xhat)
        _acc_rows(dlb_o, i == 0, dvn)
        d_o[:, D:] = (_ln_bwd(dvn, xhat, rstd, lg_r[...]) * _gelu_grad(vp)).astype(d_o.dtype)

        @pl.when(i == last)
        def _():
            row = lax.broadcasted_iota(jnp.int32, (BLK, BLK), 0)
            col = lax.broadcasted_iota(jnp.int32, (BLK, BLK), 1)
            ones = jnp.ones((8, BLK), MX)
            for g in range(8):
                dws_o[g] = jnp.where(col <= row, dws_o[g], 0.0)
                a = bacc[g]
                hi = a.astype(MX)
                lo = (a - hi.astype(f32)).astype(MX)
                dbs_o[g:g + 1, :] = (_dot(ones, hi, NT) + _dot(ones, lo, NT))[0:1, :]

    full = lambda shape: pl.BlockSpec(shape, lambda i: (0,) * len(shape))
    return pl.pallas_call(
        body, name="gmlp_bwd", grid=(T // tm,),
        in_specs=[pl.BlockSpec((tm, D), lambda i: (i, 0)), *_uv_specs(), full((8, BLK, BLK)), full((8, BLK, BLK)),
                  full((1, D)), full((1, D))],
        out_specs=[pl.BlockSpec((tm, 2 * D), lambda i: (i, 0)), full((8, BLK, BLK)), full((8, BLK)), full((1, D)), full((1, D))],
        out_shape=[jax.ShapeDtypeStruct((T, 2 * D), MX), jax.ShapeDtypeStruct((8, BLK, BLK), f32),
                   jax.ShapeDtypeStruct((8, BLK), f32), jax.ShapeDtypeStruct((1, D), f32), jax.ShapeDtypeStruct((1, D), f32)],
        scratch_shapes=[pltpu.VMEM((8, BLK, BLK), f32)],
        compiler_params=_cp(("arbitrary",)),
    )(dyc, proj, proj, proj, proj, wst, bsx, lg, lb)


PART_TILES = (6, 6, 3, 3, 3, 4)
PART_START = (0, 6, 12, 15, 18, 21)
TJ = 512


def _part_specs(tm, rows_axis):
    specs = []
    for n, s in zip(PART_TILES, PART_START):
        def imap(*idx, n=n, s=s):
            i, j = idx[rows_axis], idx[1 - rows_axis]
            return (i, jnp.clip(j - s, 0, n - 1))
        specs.append(pl.BlockSpec((tm, TJ), imap))
    return specs


def _select_part(j, refs):
    out = refs[0][...]
    for r, s in zip(refs[1:], PART_START[1:]):
        out = jnp.where(j >= s, r[...], out)
    return out


def dx_in(dr1, parts, w):
    T = dr1.shape[0]
    tm = min(1024, T)

    def body(dr_r, p0, p1, p2, p3, p4, p5, w_r, o_r):
        j = pl.program_id(1)

        @pl.when(j == 0)
        def _():
            o_r[...] = ALPHA * dr_r[...]
        o_r[...] += _dot(_select_part(j, (p0, p1, p2, p3, p4, p5)), w_r[...], NT)

    return pl.pallas_call(
        body, name="dx_in", grid=(T // tm, NIN // TJ),
        in_specs=[pl.BlockSpec((tm, D), lambda i, j: (i, 0))] + _part_specs(tm, 0) + [pl.BlockSpec((D, TJ), lambda i, j: (0, j))],
        out_specs=pl.BlockSpec((tm, D), lambda i, j: (i, 0)),
        out_shape=jax.ShapeDtypeStruct((T, D), f32),
        compiler_params=_cp(("parallel", "arbitrary"), 56),
    )(dr1, *parts, w)


def dw_in(x0t, parts):
    T = x0t.shape[1]
    tk = min(1024, T)

    def body(x_r, p0, p1, p2, p3, p4, p5, o_r):
        j, t = pl.program_id(0), pl.program_id(1)

        @pl.when(t == 0)
        def _():
            o_r[...] = jnp.zeros_like(o_r)
        o_r[...] += _dot(x_r[...], _select_part(j, (p0, p1, p2, p3, p4, p5)))

    return pl.pallas_call(
        body, name="dw_in", grid=(NIN // TJ, T // tk),
        in_specs=[pl.BlockSpec((D, tk), lambda j, t: (0, t))] + _part_specs(tk, 1),
        out_specs=pl.BlockSpec((D, TJ), lambda j, t: (0, j)),
        out_shape=jax.ShapeDtypeStruct((D, NIN), f32),
        compiler_params=_cp(("parallel", "arbitrary")),
    )(x0t, *parts)


def rope_tables(positions):
    half = HD // 2
    inv_freq = ROPE_THETA ** (-jnp.arange(half, dtype=f32) / half)
    ang = positions.astype(f32)[:, None] * inv_freq
    cos, sin = jnp.cos(ang), jnp.sin(ang)
    return jnp.tile(cos, (1, LANES // half)), jnp.tile(jnp.concatenate([-sin, sin], axis=1), (1, LANES // HD))


def _flat(a):
    return a.reshape(a.shape[0] * a.shape[1], a.shape[2])


def layer_fwd(x0, W, cos_t, sin_t):
    T = x0.shape[0]
    proj = mm_in(x0, W["w_in"])
    ya, yc = mix_ac_fwd(proj, W["conv_w"], W["wst"], W["bsx"], W["gmlp_ln_g"], W["gmlp_ln_b"])
    folded, os_, lses = [], [], []
    for g, (_, d) in enumerate(GROUPS):
        qf, kf, vf = fold_rope(proj, cos_t, sin_t, g, d)
        o, lse = attn_fwd(_flat(qf), _flat(kf), _flat(vf), g, T // d // BLK)
        folded.append((qf, kf, vf))
        os_.append(o.reshape(d, T // d, AO))
        lses.append(lse.reshape(d, T // d, AO))
    yb = combine_fwd(os_, lses)
    mabc, m, r1, x1 = mix_out_fwd(proj, ya, yb, yc, x0, W["p_a"], W["p_b"], W["p_c"], W["w_o"], W["ln1_g"], W["ln1_b"])
    gate, up, hh = ffn_up_fwd(x1, W["w_gate"], W["w_up"])
    r2, x2 = ffn_down_fwd(hh, W["w_down"], x1, W["ln2_g"], W["ln2_b"])
    saved = dict(x0=x0, proj=proj, ya=ya, yb=yb, yc=yc, folded=folded, os=os_, lses=lses, mabc=mabc, m=m, r1=r1,
                 x1=x1, gate=gate, up=up, hh=hh, r2=r2)
    return x2, saved


def layer_bwd(dx2, S, W, cos_t, sin_t):
    T = dx2.shape[0]
    tk = min(512, T)
    G = {}
    dr2, dgate, dup, G["ln2_g"], G["ln2_b"] = ffn_down_bwd(dx2, S["r2"], W["ln2_g"], W["w_down"], S["gate"], S["up"])
    blk_a = pl.BlockSpec((1, tk, FB), lambda k, t: (k, t, 0))
    row_b = pl.BlockSpec((tk, D), lambda k, t: (t, 0))
    G["w_down"] = tn_matmul("dw_down", S["hh"], dr2, blk_a, row_b, (NCHIP, FB, D),
                            pl.BlockSpec((1, FB, D), lambda k, t: (k, 0, 0)), (NCHIP, T // tk))
    x1t = transpose_cast(S["x1"])
    for nm, dv in (("w_gate", dgate), ("w_up", dup)):
        G[nm] = tn_matmul("d" + nm, x1t, dv, pl.BlockSpec((D, tk), lambda k, t: (0, t)), blk_a, (NCHIP, D, FB),
                          pl.BlockSpec((1, D, FB), lambda k, t: (k, 0, 0)), (NCHIP, T // tk), a_is_t=True)
    dr1, G["ln1_g"], G["ln1_b"] = ffn_up_bwd(dr2, dgate, dup, W["w_gate"], W["w_up"], S["r1"], W["ln1_g"])
    dmabc, dgates, dya, dyb, dyc = mix_out_bwd(dr1, S["proj"], S["mabc"], W["w_o"], W["p_a"], W["p_b"], W["p_c"])
    one = (1, T // tk)
    full_o = pl.BlockSpec((D, D), lambda k, t: (0, 0))
    G["w_o"] = tn_matmul("dw_o", S["m"], dr1, row_b, row_b, (D, D), full_o, one)
    G["p_a"] = tn_matmul("dp_a", S["ya"], dmabc, row_b, pl.BlockSpec((tk, D), lambda k, t: (t, 0)), (D, D), full_o, one)
    G["p_c"] = tn_matmul("dp_c", S["yc"], dmabc, row_b, pl.BlockSpec((tk, D), lambda k, t: (t, 2)), (D, D), full_o, one)
    G["p_b"] = tn_matmul("dp_b", S["yb"], dmabc, pl.BlockSpec((tk, AO), lambda k, t: (t, 0)),
                         pl.BlockSpec((tk, D // NCHIP), lambda k, t: (t, NCHIP + k)), (NCHIP, AO, D // NCHIP),
                         pl.BlockSpec((1, AO, D // NCHIP), lambda k, t: (k, 0, 0)), (NCHIP, T // tk))
    dbch, G["conv_w"] = conv_bwd(dya, S["proj"], W["conv_w"])
    duv, G["w_s"], G["b_s"], G["gmlp_ln_g"], G["gmlp_ln_b"] = gmlp_bwd(
        dyc, S["proj"], W["wst"], W["bsx"], W["gmlp_ln_g"], W["gmlp_ln_b"])
    pre = attn_pre_bwd(dyb, S["os"], S["lses"])
    dqkv = []
    for g, (_, d) in enumerate(GROUPS):
        qf, kf, vf = S["folded"][g]
        dqf, dkf, dvf = attn_bwd(_flat(qf), _flat(kf), _flat(vf), _flat(pre[g]), _flat(S["lses"][g]), _flat(pre[3 + g]),
                                 g, T // d // BLK)
        shp = (d, T // d, AO)
        dqkv.append(unfold_rope_bwd(dqf.reshape(shp), dkf.reshape(shp), dvf.reshape(shp), cos_t, sin_t, g, d))
    parts = (dgates, dbch, *dqkv, duv)
    G["w_in"] = dw_in(transpose_cast(S["x0"]), parts)
    dx0 = dx_in(dr1, parts, W["w_in"])
    return dx0, G


def prep_layer_weights(Wl):
    W = dict(Wl)
    tril = jnp.tril(jnp.ones((BLK, BLK), f32))
    W["wst"] = (Wl["w_s"] * tril[None]).astype(MX)
    W["bsx"] = jnp.broadcast_to(Wl["b_s"][:, :, None], (8, BLK, BLK))
    for n in ("gmlp_ln_g", "gmlp_ln_b", "ln1_g", "ln1_b", "ln2_g", "ln2_b"):
        W[n] = Wl[n].reshape(1, D)
    return W


def local_step(x, positions, target, layers):
    cos_t, sin_t = rope_tables(positions)
    Ws = [prep_layer_weights(Wl) for Wl in layers]
    saved = []
    h = x
    for W in Ws:
        h, S = layer_fwd(h, W, cos_t, sin_t)
        saved.append(S)
    lsum, dh = loss_grad(h, target)
    grads = [None] * len(Ws)
    for l in reversed(range(len(Ws))):
        dh, grads[l] = layer_bwd(dh, saved[l], Ws[l], cos_t, sin_t)
    return lsum, dh, grads


MESH = pl.DeviceIdType.MESH
ANY = pl.BlockSpec(memory_space=pl.ANY)
BIG = ("w_in", "w_gate", "w_up", "w_down", "p_a", "p_b", "p_c", "w_o")
NBIG = len(BIG)


def _place():
    x, y, c = lax.axis_index("x"), lax.axis_index("y"), lax.axis_index("c")
    return x, y, c, 2 * x + y


def _rcopy(src, dst, send, recv, dev):
    return pltpu.make_async_remote_copy(src_ref=src, dst_ref=dst, send_sem=send, recv_sem=recv, device_id=dev,
                                        device_id_type=MESH)


def _cols(ref, k, width):
    start = k * width if isinstance(k, int) else pl.multiple_of(k * width, LANES)
    return ref.at[:, pl.ds(start, width)]


CHUNK_BYTES = 1 << 20


def _pieces(shape, itemsize, nbytes=CHUNK_BYTES):
    rows, cols = shape[-2], shape[-1]
    per = max(16, nbytes // (cols * itemsize) // 16 * 16)
    out = []
    for lead in (range(shape[0]) if len(shape) == 3 else (None,)):
        for r in range(0, rows, per):
            sl = (pl.ds(r, min(per, rows - r)), slice(None))
            out.append(sl if lead is None else (lead,) + sl)
    return out


def _start_pieces(src, dst, make, nbytes=CHUNK_BYTES):
    for idx in _pieces(src.shape, jnp.dtype(src.dtype).itemsize, nbytes):
        make(src.at[idx], dst.at[idx]).start()


def gather_weights(shards):
    n = len(shards)

    def body(*refs):
        srcs, dsts = refs[:n], refs[n:2 * n]
        send, recv, own_send, own_recv = refs[2 * n:]
        x, y, c, k = _place()
        sib = (x, y, 1 - c)
        chips = [(1 - x, y), (x, 1 - y), (1 - x, 1 - y)]

        def slot(a, layer, pos):
            if a == 0:
                return _cols(dsts[0].at[layer], pos, WIN_SHARD)
            return dsts[a].at[layer, pos]

        def ici(a, j, src, dst):
            return _rcopy(src, dst, send.at[a, j], recv.at[a, j], (*chips[j], c))

        def d2d(a, j, src, dst):
            return _rcopy(src, dst, send.at[a, 3 + j], recv.at[a, 3 + j], sib)

        def own(a, layer, src, dst):
            return _rcopy(src, dst, own_send.at[a, layer], own_recv.at[a, layer], sib)

        for a in range(n):
            for j in range(3):
                _start_pieces(srcs[a].at[c], slot(a, c, k), functools.partial(ici, a, j))
        for a in range(n):
            for layer in range(DEPTH):
                _start_pieces(srcs[a].at[layer], slot(a, layer, k), functools.partial(own, a, layer))
        for a in range(n):
            for j, (cx, cy) in enumerate(chips):
                landed = slot(a, c, 2 * cx + cy)
                ici(a, j, landed, landed).wait_recv()
                _start_pieces(landed, landed, functools.partial(d2d, a, j))
        for a in range(n):
            for j, (cx, cy) in enumerate(chips):
                passed = slot(a, 1 - c, 2 * cx + cy)
                d2d(a, j, passed, passed).wait_recv()
                landed = slot(a, c, 2 * cx + cy)
                d2d(a, j, landed, landed).wait_send()
                ici(a, j, srcs[a].at[c], slot(a, c, k)).wait_send()
            for layer in range(DEPTH):
                own(a, layer, srcs[a].at[layer], slot(a, layer, k)).wait()

    outs = [jax.ShapeDtypeStruct((DEPTH, D, NIN), shards[0].dtype)]
    outs += [jax.ShapeDtypeStruct((DEPTH, NCHIP) + s.shape[1:], s.dtype) for s in shards[1:]]
    return pl.pallas_call(
        body, name="gather_weights", in_specs=[ANY] * n, out_specs=[ANY] * n, out_shape=outs,
        scratch_shapes=[pltpu.SemaphoreType.DMA((n, 6)), pltpu.SemaphoreType.DMA((n, 6)),
                        pltpu.SemaphoreType.DMA((n, DEPTH)), pltpu.SemaphoreType.DMA((n, DEPTH))],
    )(*shards)


def _half(ref, h, a):
    rows = ref.shape[-2] // 2
    start = pl.multiple_of(h * rows, 16)
    if a == 0:
        return ref.at[pl.ds(start, rows), :]
    return ref.at[:, pl.ds(start, rows), :]


def rs_pair(l, grads):
    def body(*refs):
        g, theirs = refs[:NBIG], refs[NBIG:2 * NBIG]
        send, recv = refs[2 * NBIG:]
        x, y, c, _ = _place()

        def give(a, s, d):
            return _rcopy(s, d, send.at[a], recv.at[a], (x, y, 1 - c))

        for a in range(NBIG):
            _start_pieces(_half(g[a], 1 - c, a), theirs[a], functools.partial(give, a))
        for a in range(NBIG):
            give(a, _half(g[a], 1 - c, a), theirs[a]).wait()

    def hshape(s, a):
        return (s[0] // 2, s[1]) if a == 0 else (s[0], s[1] // 2, s[2])

    outs = [jax.ShapeDtypeStruct(hshape(g.shape, a), g.dtype) for a, g in enumerate(grads)]
    return pl.pallas_call(
        body, name=f"rs_pair{l}", in_specs=[ANY] * NBIG, out_specs=[ANY] * NBIG, out_shape=outs,
        scratch_shapes=[pltpu.SemaphoreType.DMA((NBIG,))] * 2,
    )(*grads)


def rs_chips(l, sums):
    def body(*refs):
        s, land = refs[:NBIG], refs[NBIG:2 * NBIG]
        send, recv = refs[2 * NBIG:]
        x, y, c, me = _place()

        def piece(a, k):
            return _cols(s[a], k, WIN_SHARD) if a == 0 else s[a].at[k]

        def give(a, k, src, dst):
            return _rcopy(src, dst, send.at[a, k], recv.at[a, me], (k // 2, k % 2, c))

        for k in range(NCHIP):
            @pl.when(me != k)
            def _():
                for a in range(NBIG):
                    _start_pieces(piece(a, k), land[a].at[me], functools.partial(give, a, k))
        for k in range(NCHIP):
            @pl.when(me != k)
            def _():
                for a in range(NBIG):
                    give(a, k, piece(a, k), land[a].at[me]).wait_send()
                    _rcopy(piece(a, k), land[a].at[k], send.at[a, k], recv.at[a, k], (k // 2, k % 2, c)).wait_recv()

    def pshape(s, a):
        return (NCHIP, s[0], WIN_SHARD) if a == 0 else s

    outs = [jax.ShapeDtypeStruct(pshape(v.shape, a), v.dtype) for a, v in enumerate(sums)]
    return pl.pallas_call(
        body, name=f"rs_chips{l}", in_specs=[ANY] * NBIG, out_specs=[ANY] * NBIG, out_shape=outs,
        scratch_shapes=[pltpu.SemaphoreType.DMA((NBIG, NCHIP)), pltpu.SemaphoreType.DMA((NBIG, NCHIP))],
    )(*sums)


def rs_join(l, halves):
    def body(*refs):
        h, other = refs[:NBIG], refs[NBIG:2 * NBIG]
        send, recv = refs[2 * NBIG:]
        x, y, c, _ = _place()

        def give(a, s, d):
            return _rcopy(s, d, send.at[a], recv.at[a], (x, y, 1 - c))

        for a in range(NBIG):
            _start_pieces(h[a], other[a], functools.partial(give, a))
        for a in range(NBIG):
            give(a, h[a], other[a]).wait()

    outs = [jax.ShapeDtypeStruct(v.shape, v.dtype) for v in halves]
    return pl.pallas_call(
        body, name=f"rs_join{l}", in_specs=[ANY] * NBIG, out_specs=[ANY] * NBIG, out_shape=outs,
        scratch_shapes=[pltpu.SemaphoreType.DMA((NBIG,))] * 2,
    )(*halves)


def _row_tile(rows, cols, itemsize=4, target=2 << 20):
    best = 8
    for t in range(8, rows + 1, 8):
        if rows % t == 0 and t * cols * itemsize <= target:
            best = t
    return best


def add_n(name, terms):
    shape = terms[0].shape
    cols = shape[-1]
    rows = math.prod(shape[:-1])
    tr = _row_tile(rows, cols)

    def body(*refs):
        acc = refs[0][...]
        for r in refs[1:-1]:
            acc = acc + r[...]
        refs[-1][...] = acc

    tile = pl.BlockSpec((tr, cols), lambda i: (i, 0))
    out = pl.pallas_call(
        body, name=name, grid=(rows // tr,), in_specs=[tile] * len(terms), out_specs=tile,
        out_shape=jax.ShapeDtypeStruct((rows, cols), f32), compiler_params=_cp(("parallel",)),
    )(*[t.reshape(rows, cols) for t in terms])
    return out.reshape(shape)


def add_chips(name, land, own):
    _, rows, cols = land.shape
    tr = _row_tile(rows, cols, target=1 << 20)

    def body(land_r, own_r, o_r):
        me = 2 * lax.axis_index("x") + lax.axis_index("y")
        for k in range(NCHIP):
            @pl.when(me == k)
            def _():
                acc = None
                for j in range(NCHIP):
                    t = own_r[...] if j == k else land_r[j]
                    acc = t if acc is None else acc + t
                o_r[...] = acc

    tile = pl.BlockSpec((tr, cols), lambda i: (i, 0))
    return pl.pallas_call(
        body, name=name, grid=(rows // tr,), in_specs=[pl.BlockSpec((NCHIP, tr, cols), lambda i: (0, i, 0)), tile],
        out_specs=tile, out_shape=jax.ShapeDtypeStruct((rows, cols), f32), compiler_params=_cp(("parallel",)),
    )(land, own)


def reduce_scatter_layer(l, G):
    c = lax.axis_index("c")
    me = 2 * lax.axis_index("x") + lax.axis_index("y")
    grads = [G[n] if G[n].ndim == 3 or n == "w_in" else G[n].reshape(NCHIP, D // NCHIP, D) for n in BIG]
    theirs = rs_pair(l, grads)
    sums = []
    for n, g, t in zip(BIG, grads, theirs):
        rows = g.shape[-2] // 2
        mine = lax.dynamic_slice_in_dim(g, c * rows, rows, axis=g.ndim - 2)
        sums.append(add_n(f"rs_add_pair{l}_{n}", [mine, t]))
    landed = rs_chips(l, sums)
    halves = []
    for a, (n, s, v) in enumerate(zip(BIG, sums, landed)):
        own = lax.dynamic_slice_in_dim(s, me * WIN_SHARD, WIN_SHARD, axis=1) if a == 0 else \
            lax.dynamic_index_in_dim(s, me, 0, keepdims=False)
        halves.append(add_chips(f"rs_add_chips{l}_{n}", v, own))
    return dict(zip(BIG, zip(halves, rs_join(l, halves))))


NDEV = 8


def allreduce_small(pack):
    rows = pack.shape[0]

    def body(p_ref, o_ref, buf, send, recv):
        x, y, c, _ = _place()
        me = 4 * x + 2 * y + c
        buf[me] = p_ref[...]

        def give(r, s, d):
            return _rcopy(s, d, send.at[r - 1], recv.at[r - 1], (x ^ (r >> 2), y ^ ((r >> 1) & 1), c ^ (r & 1)))

        for r in range(1, NDEV):
            _start_pieces(p_ref, buf.at[me], functools.partial(give, r), 128 << 10)
        for r in range(1, NDEV):
            give(r, p_ref, buf.at[me]).wait_send()
            src = 4 * (x ^ (r >> 2)) + 2 * (y ^ ((r >> 1) & 1)) + (c ^ (r & 1))
            give(r, p_ref, buf.at[src]).wait_recv()
        acc = buf[0]
        for d in range(1, NDEV):
            acc = acc + buf[d]
        o_ref[...] = acc

    vm = pl.BlockSpec(memory_space=pltpu.VMEM)
    return pl.pallas_call(
        body, name="allreduce_small", in_specs=[vm], out_specs=vm, out_shape=jax.ShapeDtypeStruct(pack.shape, f32),
        scratch_shapes=[pltpu.VMEM((NDEV, rows, LANES), f32), pltpu.SemaphoreType.DMA((NDEV - 1,)),
                        pltpu.SemaphoreType.DMA((NDEV - 1,))],
        compiler_params=pltpu.CompilerParams(vmem_limit_bytes=40 << 20),
    )(pack)


def _adamw_math(w, g, m, v):
    m = ADAM_B1 * m + (1.0 - ADAM_B1) * g
    v = ADAM_B2 * v + (1.0 - ADAM_B2) * (g * g)
    m_hat = m / (1.0 - ADAM_B1 ** ADAM_STEP)
    v_hat = v / (1.0 - ADAM_B2 ** ADAM_STEP)
    return -ADAM_LR * (m_hat / (jnp.sqrt(v_hat) + ADAM_EPS) + ADAM_WD * w), m, v


def adamw_big(name, halves, w, m, v):
    _, R, C = w.shape
    tr = _row_tile(R // 2, C, target=1 << 20)
    nt = R // 2 // tr

    def body(a0, b0, a1, b1, w_r, m_r, v_r, g_o, d_o, m_o, v_o):
        mine = pl.program_id(1) == lax.axis_index("c")
        g = jnp.where(pl.program_id(0) == 0, jnp.where(mine, a0[...], b0[...]), jnp.where(mine, a1[...], b1[...]))
        g_o[...] = g
        d_o[...], m_o[...], v_o[...] = _adamw_math(w_r[...], g, m_r[...], v_r[...])

    stk = pl.BlockSpec((None, tr, C), lambda l, h, i: (l, h * nt + i, 0))
    lay0 = pl.BlockSpec((tr, C), lambda l, h, i: (jnp.where(l == 0, i, nt - 1), 0))
    lay1 = pl.BlockSpec((tr, C), lambda l, h, i: (jnp.where(l == 0, 0, i), 0))
    return pl.pallas_call(
        body, name=name, grid=(DEPTH, 2, nt),
        in_specs=[lay0, lay0, lay1, lay1, stk, stk, stk],
        out_specs=[stk] * 4, out_shape=[jax.ShapeDtypeStruct(w.shape, f32)] * 4,
        compiler_params=_cp(("arbitrary", "arbitrary", "arbitrary")),
    )(*halves[0], *halves[1], w, m, v)


def adamw_small(name, g, w, m, v):
    def body(g_r, w_r, m_r, v_r, d_o, m_o, v_o):
        d_o[...], m_o[...], v_o[...] = _adamw_math(w_r[...], g_r[...], m_r[...], v_r[...])

    return pl.pallas_call(body, name=name, out_shape=[jax.ShapeDtypeStruct(w.shape, f32)] * 3)(g, w, m, v)


WEIGHTS = ("w_in", "conv_w", "gmlp_ln_g", "gmlp_ln_b", "w_s", "b_s", "p_a", "p_b", "p_c", "w_o", "ln1_g", "ln1_b",
           "w_gate", "w_up", "w_down", "ln2_g", "ln2_b")
VECS = ("ln1_g", "ln1_b", "ln2_g", "ln2_b", "gmlp_ln_g", "gmlp_ln_b")
ROWS_VEC, ROWS_BS, ROWS_WS, ROWS_CONV = D // LANES, 8, 8 * BLK, 3 * D // LANES
ROWS_LAYER = len(VECS) * ROWS_VEC + ROWS_BS + ROWS_WS + ROWS_CONV


def _pack_small(per_layer, tail):
    parts = []
    for P in per_layer:
        parts += [P[n].reshape(ROWS_VEC, LANES) for n in VECS]
        parts += [P["b_s"].reshape(ROWS_BS, LANES), P["w_s"].reshape(ROWS_WS, LANES), P["conv_w"].reshape(ROWS_CONV, LANES)]
    return jnp.concatenate(parts + [tail], axis=0)


def _unpack_small(pack):
    out = []
    for l in range(DEPTH):
        r = l * ROWS_LAYER
        P = {}
        for n in VECS:
            P[n] = pack[r:r + ROWS_VEC].reshape(D)
            r += ROWS_VEC
        P["b_s"] = pack[r:r + ROWS_BS].reshape(8, BLK)
        r += ROWS_BS
        P["w_s"] = pack[r:r + ROWS_WS].reshape(8, BLK, BLK)
        r += ROWS_WS
        P["conv_w"] = pack[r:r + ROWS_CONV].reshape(3, D)
        out.append(P)
    return out, pack[DEPTH * ROWS_LAYER:]


def kernel(x, positions, w_in, conv_w, gmlp_ln_g, gmlp_ln_b, w_s, b_s, p_a, p_b, p_c, w_o, ln1_g, ln1_b, w_gate, w_up, w_down, ln2_g, ln2_b, loss_target, m_w_in, m_conv_w, m_gmlp_ln_g, m_gmlp_ln_b, m_w_s, m_b_s, m_p_a, m_p_b, m_p_c, m_w_o, m_ln1_g, m_ln1_b, m_w_gate, m_w_up, m_w_down, m_ln2_g, m_ln2_b, v_w_in, v_conv_w, v_gmlp_ln_g, v_gmlp_ln_b, v_w_s, v_b_s, v_p_a, v_p_b, v_p_c, v_w_o, v_ln1_g, v_ln1_b, v_w_gate, v_w_up, v_w_down, v_ln2_g, v_ln2_b):
    Wt = dict(w_in=w_in, conv_w=conv_w, gmlp_ln_g=gmlp_ln_g, gmlp_ln_b=gmlp_ln_b, w_s=w_s, b_s=b_s, p_a=p_a, p_b=p_b,
              p_c=p_c, w_o=w_o, ln1_g=ln1_g, ln1_b=ln1_b, w_gate=w_gate, w_up=w_up, w_down=w_down, ln2_g=ln2_g, ln2_b=ln2_b)
    Mt = dict(w_in=m_w_in, conv_w=m_conv_w, gmlp_ln_g=m_gmlp_ln_g, gmlp_ln_b=m_gmlp_ln_b, w_s=m_w_s, b_s=m_b_s, p_a=m_p_a,
              p_b=m_p_b, p_c=m_p_c, w_o=m_w_o, ln1_g=m_ln1_g, ln1_b=m_ln1_b, w_gate=m_w_gate, w_up=m_w_up,
              w_down=m_w_down, ln2_g=m_ln2_g, ln2_b=m_ln2_b)
    Vt = dict(w_in=v_w_in, conv_w=v_conv_w, gmlp_ln_g=v_gmlp_ln_g, gmlp_ln_b=v_gmlp_ln_b, w_s=v_w_s, b_s=v_b_s, p_a=v_p_a,
              p_b=v_p_b, p_c=v_p_c, w_o=v_w_o, ln1_g=v_ln1_g, ln1_b=v_ln1_b, w_gate=v_w_gate, w_up=v_w_up,
              w_down=v_w_down, ln2_g=v_ln2_g, ln2_b=v_ln2_b)
    chip = 2 * lax.axis_index("x") + lax.axis_index("y")
    cw = D // NCHIP

    full = gather_weights([Wt[n].astype(MX) for n in BIG] + [conv_w])
    layers = []
    for l in range(DEPTH):
        Wl = dict(zip(BIG, (f[l] for f in full[:NBIG])))
        for n in ("p_a", "p_c", "w_o"):
            Wl[n] = Wl[n].reshape(D, D)
        Wl["conv_w"] = full[NBIG][l].transpose(1, 0, 2).reshape(3, D)
        for n in VECS + ("w_s", "b_s"):
            Wl[n] = Wt[n][l]
        layers.append(Wl)

    lsum, grad_x, grads = local_step(x[0], positions[0], loss_target[0], layers)

    red = [None] * DEPTH
    for l in reversed(range(DEPTH)):
        red[l] = reduce_scatter_layer(l, grads[l])
    pack = _pack_small([{n: (g[n] if n not in VECS else g[n]) for n in VECS + ("b_s", "w_s", "conv_w")} for g in grads], lsum)
    small, tail = _unpack_small(allreduce_small(pack))
    loss = tail[0, 0]

    G, DW, NM, NV = {}, {}, {}, {}
    for n in BIG:
        G[n], DW[n], NM[n], NV[n] = adamw_big("adamw_" + n, (red[0][n], red[1][n]), Wt[n], Mt[n], Vt[n])
    zc = jnp.zeros((3, D), f32)
    wp = _pack_small([{**{n: Wt[n][l] for n in VECS + ("b_s", "w_s")}, "conv_w": zc} for l in range(DEPTH)], jnp.zeros((8, LANES), f32))
    mp = _pack_small([{**{n: Mt[n][l] for n in VECS + ("b_s", "w_s")}, "conv_w": zc} for l in range(DEPTH)], jnp.zeros((8, LANES), f32))
    vp = _pack_small([{**{n: Vt[n][l] for n in VECS + ("b_s", "w_s")}, "conv_w": zc} for l in range(DEPTH)], jnp.ones((8, LANES), f32))
    gp = _pack_small(small, jnp.zeros((8, LANES), f32))
    outs = [_unpack_small(a)[0] for a in adamw_small("adamw_small", gp, wp, mp, vp)]
    for n in VECS + ("b_s", "w_s"):
        G[n] = jnp.stack([small[l][n] for l in range(DEPTH)])
        DW[n], NM[n], NV[n] = (jnp.stack([o[l][n] for l in range(DEPTH)]) for o in outs)
    gconv = jnp.stack([lax.dynamic_slice(small[l]["conv_w"], (0, chip * cw), (3, cw)) for l in range(DEPTH)])
    G["conv_w"] = gconv
    flat = lambda a: a.reshape(DEPTH * 3, cw)
    d, m2, v2 = adamw_small("adamw_conv", flat(gconv), flat(conv_w), flat(m_conv_w), flat(v_conv_w))
    DW["conv_w"], NM["conv_w"], NV["conv_w"] = (a.reshape(DEPTH, 3, cw) for a in (d, m2, v2))

    return (loss, grad_x[None], *[G[n] for n in WEIGHTS], *[DW[n] for n in WEIGHTS], *[NM[n] for n in WEIGHTS],
            *[NV[n] for n in WEIGHTS])
```

```python
import functools
import math

import jax
import jax.numpy as jnp
from jax import lax
from jax.experimental import pallas as pl
from jax.experimental.pallas import tpu as pltpu

D = 1024
NIN = 12800
DFF = 2816
NCHIP = 4
FB = DFF // NCHIP
WIN_SHARD = NIN // NCHIP
DEPTH = 2
GROUPS = ((128, 1), (512, 4), (2048, 16))
HD = 64
BLK = 128
AO = 512
ALPHA = (2 * DEPTH) ** 0.25
EPS = 1e-5
ROPE_THETA = 10000.0
LANES = 128
NEG = -1e30

C_GATES, C_BCH, C_QKV, C_UV = 0, 3 * D, 6 * D, 6 * D + 9 * AO

MX = jnp.bfloat16
ACT = jnp.bfloat16

ADAM_LR, ADAM_B1, ADAM_B2, ADAM_EPS, ADAM_WD, ADAM_STEP = 0.001, 0.9, 0.999, 1e-08, 0.01, 10

f32 = jnp.float32
NT = (((1,), (1,)), ((), ()))
TN = (((0,), (0,)), ((), ()))


def _cp(sem, vmem_mb=48):
    return pltpu.CompilerParams(dimension_semantics=sem, vmem_limit_bytes=vmem_mb << 20)


def _dot(a, b, dims=None):
    if dims is None:
        return jnp.dot(a, b, preferred_element_type=f32)
    return lax.dot_general(a, b, dims, preferred_element_type=f32)


def _ln_stats(r):
    mu = jnp.mean(r, axis=-1, keepdims=True)
    xc = r - mu
    var = jnp.mean(xc * xc, axis=-1, keepdims=True)
    rstd = lax.rsqrt(var + EPS)
    return xc * rstd, rstd


def _ln_bwd(dy, xhat, rstd, g):
    dxh = dy * g
    return rstd * (dxh - jnp.mean(dxh, axis=-1, keepdims=True) - xhat * jnp.mean(dxh * xhat, axis=-1, keepdims=True))


def _gelu(x):
    return 0.5 * x * (1.0 + lax.erf(x * (1.0 / math.sqrt(2.0))))


def _gelu_grad(x):
    return 0.5 * (1.0 + lax.erf(x * (1.0 / math.sqrt(2.0)))) + x * jnp.exp(-0.5 * x * x) * (1.0 / math.sqrt(2.0 * math.pi))


def _sigmoid(x):
    return 1.0 / (1.0 + jnp.exp(-x))


def _acc_rows(o_ref, first, val):
    @pl.when(first)
    def _():
        o_ref[...] = jnp.zeros_like(o_ref)
    o_ref[...] += jnp.sum(val, axis=0, keepdims=True)


def mm_in(x, w):
    T = x.shape[0]
    tm, tn = min(1024, T), 512

    def body(x_ref, w_ref, o_ref, xb):
        @pl.when(pl.program_id(1) == 0)
        def _():
            xb[...] = x_ref[...].astype(MX)
        o_ref[...] = _dot(xb[...], w_ref[...]).astype(o_ref.dtype)

    return pl.pallas_call(
        body, name="mm_in", grid=(T // tm, NIN // tn),
        in_specs=[pl.BlockSpec((tm, D), lambda i, j: (i, 0)), pl.BlockSpec((D, tn), lambda i, j: (0, j))],
        out_specs=pl.BlockSpec((tm, tn), lambda i, j: (i, j)),
        out_shape=jax.ShapeDtypeStruct((T, NIN), ACT),
        scratch_shapes=[pltpu.VMEM((tm, D), MX)],
        compiler_params=_cp(("parallel", "arbitrary")),
    )(x, w)


HALO = 16
TM_AC = 256


def _uv_specs():
    return [pl.BlockSpec((TM_AC, 512), functools.partial(lambda i, j: (i, j), j=C_UV // 512 + j)) for j in range(4)]


def _gmlp_fwd(up, vp, ws_ref, bs_ref, lg, lb):
    u = _gelu(up)
    xhat, rstd = _ln_stats(_gelu(vp))
    vn = xhat * lg + lb
    vnb = vn.astype(MX)
    rows = []
    for c in range(up.shape[0] // BLK):
        r = slice(c * BLK, (c + 1) * BLK)
        rows.append(jnp.concatenate(
            [_dot(ws_ref[g], vnb[r, g * BLK:(g + 1) * BLK]) + bs_ref[g] for g in range(8)], axis=1))
    return u, vn, xhat, rstd, jnp.concatenate(rows, axis=0)


def mix_ac_fwd(proj, conv_w, wst, bsx, lg, lb):
    T = proj.shape[0]
    tm = TM_AC

    def body(bch, halo, u0, u1, v0, v1, cw, ws, bs, lg_ref, lb_ref, ya, yc, zs):
        i = pl.program_id(0)
        pb = bch[...].astype(f32)
        z = pb[:, D:2 * D] * pb[:, 2 * D:]
        hz = halo[:, :D].astype(f32) * halo[:, D:].astype(f32)
        zs[0:HALO, :] = jnp.where(i > 0, hz, 0.0)
        zs[HALO:HALO + tm, :] = z
        cv = cw[0:1, :] * zs[HALO - 2:HALO - 2 + tm, :] + cw[1:2, :] * zs[HALO - 1:HALO - 1 + tm, :] + cw[2:3, :] * z
        ya[...] = (pb[:, :D] * cv).astype(ya.dtype)
        up = jnp.concatenate([u0[...], u1[...]], axis=1).astype(f32)
        vp = jnp.concatenate([v0[...], v1[...]], axis=1).astype(f32)
        u, _, _, _, sp = _gmlp_fwd(up, vp, ws, bs, lg_ref[...], lb_ref[...])
        yc[...] = (u * sp).astype(yc.dtype)

    full = lambda shape: pl.BlockSpec(shape, lambda i: (0,) * len(shape))
    return pl.pallas_call(
        body, name="mix_ac_fwd", grid=(T // tm,),
        in_specs=[pl.BlockSpec((tm, 3 * D), lambda i: (i, 1)),
                  pl.BlockSpec((HALO, 2 * D), lambda i: (jnp.maximum(i * (tm // HALO) - 1, 0), 2)),
                  *_uv_specs(), full((3, D)), full((8, BLK, BLK)), full((8, BLK, BLK)), full((1, D)), full((1, D))],
        out_specs=[pl.BlockSpec((tm, D), lambda i: (i, 0))] * 2,
        out_shape=[jax.ShapeDtypeStruct((T, D), MX)] * 2,
        scratch_shapes=[pltpu.VMEM((HALO + tm, D), f32)],
        compiler_params=_cp(("parallel",)),
    )(proj, proj, proj, proj, proj, proj, conv_w, wst, bsx, lg, lb)


def _swap_halves(x):
    lane = lax.broadcasted_iota(jnp.int32, x.shape, 1)
    return jnp.where((lane % HD) < HD // 2, pltpu.roll(x, x.shape[1] - HD // 2, 1), pltpu.roll(x, HD // 2, 1))


def _tile4(t):
    return jnp.concatenate([t] * (AO // LANES), axis=1)


TM_FOLD = 512


def _fold_out(nat, x, out_ref, d):
    if d == 1:
        out_ref[0] = x.astype(out_ref.dtype)
        return
    rows = x.shape[0] // d
    for j in range(AO // LANES):
        nat[j] = x[:, j * LANES:(j + 1) * LANES]
    for r in range(d):
        out_ref[r] = jnp.concatenate(
            [nat.at[j][pl.ds(r, rows, stride=d), :] for j in range(AO // LANES)], axis=1).astype(out_ref.dtype)


def _unfold_in(nat, in_ref, d):
    if d == 1:
        return in_ref[0].astype(f32)
    rows = in_ref.shape[1]
    for r in range(d):
        v = in_ref[r].astype(f32)
        for j in range(AO // LANES):
            nat.at[j][pl.ds(r, rows, stride=d), :] = v[:, j * LANES:(j + 1) * LANES]
    return jnp.concatenate([nat[j] for j in range(AO // LANES)], axis=1)


def fold_rope(proj, cos_t, sin_t, g, d):
    T = proj.shape[0]
    tm = TM_FOLD
    rows = tm // d

    def body(x_ref, c_ref, s_ref, q_o, k_o, v_o, nat):
        cos, sin = _tile4(c_ref[...]), _tile4(s_ref[...])
        for part, out, scale in ((0, q_o, HD ** -0.5), (1, k_o, 1.0), (2, v_o, None)):
            x = x_ref[:, part * AO:(part + 1) * AO].astype(f32)
            if scale is not None:
                x = (x * cos + _swap_halves(x) * sin) * scale
            _fold_out(nat, x, out, d)

    fold_spec = pl.BlockSpec((d, rows, AO), lambda i: (0, i, 0))
    return pl.pallas_call(
        body, name=f"fold_rope{g}", grid=(T // tm,),
        in_specs=[pl.BlockSpec((tm, 3 * AO), lambda i: (i, C_QKV // (3 * AO) + g)),
                  pl.BlockSpec((tm, LANES), lambda i: (i, 0)), pl.BlockSpec((tm, LANES), lambda i: (i, 0))],
        out_specs=[fold_spec] * 3,
        out_shape=[jax.ShapeDtypeStruct((d, T // d, AO), MX)] * 3,
        scratch_shapes=[pltpu.VMEM((AO // LANES, tm, LANES), f32)],
        compiler_params=_cp(("parallel",)),
    )(proj, cos_t, sin_t)


def _attn_masks():
    row = lax.broadcasted_iota(jnp.int32, (BLK, BLK), 0)
    col = lax.broadcasted_iota(jnp.int32, (BLK, BLK), 1)
    return col <= row, col >= row, col < HD


def attn_fwd(qf, kf, vf, g, nb):
    T = qf.shape[0]

    def body(q_ref, k_ref, v_ref, o_ref, l_ref):
        lower, upper, head0 = _attn_masks()

        def step(b, carry):
            r0 = pl.multiple_of(b * BLK, BLK)
            rp = pl.multiple_of(jnp.maximum(b - 1, 0) * BLK, BLK)
            q, kc, vc = q_ref[pl.ds(r0, BLK), :], k_ref[pl.ds(r0, BLK), :], v_ref[pl.ds(r0, BLK), :]
            kp, vp = k_ref[pl.ds(rp, BLK), :], v_ref[pl.ds(rp, BLK), :]
            mask_p = upper & ((b % nb) != 0)
            outs = []
            for hm in (head0, ~head0):
                qh = jnp.where(hm, q, jnp.zeros_like(q))
                sc = jnp.where(lower, _dot(qh, kc, NT), NEG)
                sp = jnp.where(mask_p, _dot(qh, kp, NT), NEG)
                m = jnp.maximum(jnp.max(sc, axis=-1, keepdims=True), jnp.max(sp, axis=-1, keepdims=True))
                pc, pp = jnp.exp(sc - m), jnp.exp(sp - m)
                l = jnp.sum(pc, axis=-1, keepdims=True) + jnp.sum(pp, axis=-1, keepdims=True)
                o = (_dot(pc.astype(MX), vc) + _dot(pp.astype(MX), vp)) / l
                outs.append((o, m + jnp.log(l)))
            o_ref[pl.ds(r0, BLK), :] = jnp.where(head0, outs[0][0], outs[1][0])
            l_ref[pl.ds(r0, BLK), :] = jnp.where(head0, outs[0][1], outs[1][1])
            return carry

        lax.fori_loop(0, T // BLK, step, 0, unroll=4)

    spec = pl.BlockSpec((T, LANES), lambda j: (0, j))
    return pl.pallas_call(
        body, name=f"attn_fwd{g}", grid=(AO // LANES,),
        in_specs=[spec] * 3, out_specs=[spec] * 2,
        out_shape=[jax.ShapeDtypeStruct((T, AO), f32)] * 2,
        compiler_params=_cp(("parallel",), 56),
    )(qf, kf, vf)


def _group_weights(lses):
    m = jnp.maximum(jnp.maximum(lses[0], lses[1]), lses[2])
    e = [jnp.exp(l - m) for l in lses]
    inv = 1.0 / (e[0] + e[1] + e[2])
    return [x * inv for x in e]


def _fold_specs(T, tm):
    specs = []
    for _, d in GROUPS:
        specs.append(pl.BlockSpec((d, tm // d, AO), lambda i: (0, i, 0)))
    return specs


def combine_fwd(os_, lses):
    T = os_[0].shape[0] * os_[0].shape[1]
    tm = TM_FOLD

    def body(o0, o1, o2, l0, l1, l2, y_ref, nat):
        o = [_unfold_in(nat, r, d) for r, (_, d) in zip((o0, o1, o2), GROUPS)]
        ls = [_unfold_in(nat, r, d) for r, (_, d) in zip((l0, l1, l2), GROUPS)]
        w = _group_weights(ls)
        y_ref[...] = (w[0] * o[0] + w[1] * o[1] + w[2] * o[2]).astype(y_ref.dtype)

    specs = _fold_specs(T, tm)
    return pl.pallas_call(
        body, name="combine_fwd", grid=(T // tm,),
        in_specs=specs + specs, out_specs=pl.BlockSpec((tm, AO), lambda i: (i, 0)),
        out_shape=jax.ShapeDtypeStruct((T, AO), MX),
        scratch_shapes=[pltpu.VMEM((AO // LANES, tm, LANES), f32)],
        compiler_params=_cp(("parallel",)),
    )(*os_, *lses)


TM_MIX = 256


def mix_out_fwd(proj, ya, yb, yc, x0, pa, pb, pc, wo, g1, b1):
    T = x0.shape[0]
    tm = min(TM_MIX, T)

    def body(gt, ya_r, yb_r, yc_r, x0_r, pa_r, pb_r, pc_r, wo_r, g_r, b_r, mabc, m_o, r1_o, x1_o):
        ma = _dot(ya_r[...], pa_r[...])
        ybv = yb_r[...]
        mb = jnp.concatenate([_dot(ybv, pb_r[k]) for k in range(NCHIP)], axis=1)
        mc = _dot(yc_r[...], pc_r[...])
        m = jnp.zeros((tm, D), f32)
        for j, mm in enumerate((ma, mb, mc)):
            mabc[:, j * D:(j + 1) * D] = mm.astype(mabc.dtype)
            m = m + _sigmoid(gt[:, j * D:(j + 1) * D].astype(f32)) * mm
        mb16 = m.astype(MX)
        m_o[...] = mb16
        r1 = ALPHA * x0_r[...] + _dot(mb16, wo_r[...])
        r1_o[...] = r1
        xhat, _ = _ln_stats(r1)
        x1_o[...] = xhat * g_r[...] + b_r[...]

    full = lambda shape: pl.BlockSpec(shape, lambda i: (0,) * len(shape))
    tile = lambda w: pl.BlockSpec((tm, w), lambda i: (i, 0))
    return pl.pallas_call(
        body, name="mix_out_fwd", grid=(T // tm,),
        in_specs=[tile(3 * D), tile(D), tile(AO), tile(D), tile(D), full((D, D)), full((NCHIP, AO, D // NCHIP)),
                  full((D, D)), full((D, D)), full((1, D)), full((1, D))],
        out_specs=[tile(3 * D), tile(D), tile(D), tile(D)],
        out_shape=[jax.ShapeDtypeStruct((T, 3 * D), MX), jax.ShapeDtypeStruct((T, D), MX),
                   jax.ShapeDtypeStruct((T, D), f32), jax.ShapeDtypeStruct((T, D), f32)],
        compiler_params=_cp(("parallel",), 56),
    )(proj, ya, yb, yc, x0, pa, pb, pc, wo, g1, b1)


TM_FF = 512


def ffn_up_fwd(x1, wg, wu):
    T = x1.shape[0]
    tm = min(TM_FF, T)

    def body(x_r, wg_r, wu_r, g_o, u_o, h_o, xb):
        @pl.when(pl.program_id(1) == 0)
        def _():
            xb[...] = x_r[...].astype(MX)
        gate = _dot(xb[...], wg_r[0])
        up = _dot(xb[...], wu_r[0])
        g_o[0] = gate.astype(g_o.dtype)
        u_o[0] = up.astype(u_o.dtype)
        h_o[0] = (gate * _sigmoid(gate) * up).astype(h_o.dtype)

    wspec = pl.BlockSpec((1, D, FB), lambda i, k: (k, 0, 0))
    ospec = pl.BlockSpec((1, tm, FB), lambda i, k: (k, i, 0))
    return pl.pallas_call(
        body, name="ffn_up_fwd", grid=(T // tm, NCHIP),
        in_specs=[pl.BlockSpec((tm, D), lambda i, k: (i, 0)), wspec, wspec],
        out_specs=[ospec] * 3,
        out_shape=[jax.ShapeDtypeStruct((NCHIP, T, FB), ACT)] * 2 + [jax.ShapeDtypeStruct((NCHIP, T, FB), MX)],
        scratch_shapes=[pltpu.VMEM((tm, D), MX)],
        compiler_params=_cp(("parallel", "arbitrary")),
    )(x1, wg, wu)


def ffn_down_fwd(hh, wd, x1, g2, b2):
    T = x1.shape[0]
    tm = min(TM_FF, T)

    def body(h_r, w_r, x_r, g_r, b_r, r2_o, x2_o):
        r2 = ALPHA * x_r[...]
        for k in range(NCHIP):
            r2 = r2 + _dot(h_r[k], w_r[k])
        r2_o[...] = r2
        xhat, _ = _ln_stats(r2)
        x2_o[...] = xhat * g_r[...] + b_r[...]

    tile = pl.BlockSpec((tm, D), lambda i: (i, 0))
    vec = pl.BlockSpec((1, D), lambda i: (0, 0))
    return pl.pallas_call(
        body, name="ffn_down_fwd", grid=(T // tm,),
        in_specs=[pl.BlockSpec((NCHIP, tm, FB), lambda i: (0, i, 0)), pl.BlockSpec((NCHIP, FB, D), lambda i: (0, 0, 0)),
                  tile, vec, vec],
        out_specs=[tile, tile], out_shape=[jax.ShapeDtypeStruct((T, D), f32)] * 2,
        compiler_params=_cp(("parallel",)),
    )(hh, wd, x1, g2, b2)


def loss_grad(y, tgt):
    T = y.shape[0]
    tm = min(512, T)

    def body(y_r, t_r, l_o, dy_o):
        e = y_r[...] - t_r[...]
        dy_o[...] = e * (1.0 / D)

        @pl.when(pl.program_id(0) == 0)
        def _():
            l_o[...] = jnp.zeros_like(l_o)
        l_o[...] += (0.5 / D) * jnp.sum(e * e)

    tile = pl.BlockSpec((tm, D), lambda i: (i, 0))
    return pl.pallas_call(
        body, name="loss_grad", grid=(T // tm,),
        in_specs=[tile, tile], out_specs=[pl.BlockSpec((8, LANES), lambda i: (0, 0)), tile],
        out_shape=[jax.ShapeDtypeStruct((8, LANES), f32), jax.ShapeDtypeStruct((T, D), f32)],
        compiler_params=_cp(("arbitrary",)),
    )(y, tgt)


def ffn_down_bwd(dx2, r2, g2, wd, gate, up):
    T = dx2.shape[0]
    tm = min(TM_FF, T)

    def body(dx_r, r_r, g_r, w_r, ga_r, up_r, dr_o, dg_o, du_o, dlg_o, dlb_o, drb):
        i, k = pl.program_id(0), pl.program_id(1)

        @pl.when(k == 0)
        def _():
            xhat, rstd = _ln_stats(r_r[...])
            dx = dx_r[...]
            _acc_rows(dlg_o, i == 0, dx * xhat)
            _acc_rows(dlb_o, i == 0, dx)
            dr = _ln_bwd(dx, xhat, rstd, g_r[...])
            dr_o[...] = dr
            drb[...] = dr.astype(MX)

        dhh = _dot(drb[...], w_r[0], NT)
        gate_v, up_v = ga_r[0].astype(f32), up_r[0].astype(f32)
        sg = _sigmoid(gate_v)
        dg_o[0] = (dhh * up_v * sg * (1.0 + gate_v * (1.0 - sg))).astype(dg_o.dtype)
        du_o[0] = (dhh * gate_v * sg).astype(du_o.dtype)

    tile = pl.BlockSpec((tm, D), lambda i, k: (i, 0))
    vec = pl.BlockSpec((1, D), lambda i, k: (0, 0))
    blk = pl.BlockSpec((1, tm, FB), lambda i, k: (k, i, 0))
    return pl.pallas_call(
        body, name="ffn_down_bwd", grid=(T // tm, NCHIP),
        in_specs=[tile, tile, vec, pl.BlockSpec((1, FB, D), lambda i, k: (k, 0, 0)), blk, blk],
        out_specs=[tile, blk, blk, vec, vec],
        out_shape=[jax.ShapeDtypeStruct((T, D), f32)] + [jax.ShapeDtypeStruct((NCHIP, T, FB), MX)] * 2
        + [jax.ShapeDtypeStruct((1, D), f32)] * 2,
        scratch_shapes=[pltpu.VMEM((tm, D), MX)],
        compiler_params=_cp(("arbitrary", "arbitrary")),
    )(dx2, r2, g2, wd, gate, up)


def ffn_up_bwd(dr2, dgate, dup, wg, wu, r1, g1):
    T = dr2.shape[0]
    tm = min(TM_FF, T)

    def body(dr2_r, dg_r, du_r, wg_r, wu_r, r1_r, g_r, dr1_o, dlg_o, dlb_o, acc):
        i, k = pl.program_id(0), pl.program_id(1)

        @pl.when(k == 0)
        def _():
            acc[...] = ALPHA * dr2_r[...]
        acc[...] += _dot(dg_r[0], wg_r[0], NT) + _dot(du_r[0], wu_r[0], NT)

        @pl.when(k == NCHIP - 1)
        def _():
            dx = acc[...]
            xhat, rstd = _ln_stats(r1_r[...])
            _acc_rows(dlg_o, i == 0, dx * xhat)
            _acc_rows(dlb_o, i == 0, dx)
            dr1_o[...] = _ln_bwd(dx, xhat, rstd, g_r[...])

    tile = pl.BlockSpec((tm, D), lambda i, k: (i, 0))
    vec = pl.BlockSpec((1, D), lambda i, k: (0, 0))
    blk = pl.BlockSpec((1, tm, FB), lambda i, k: (k, i, 0))
    wspec = pl.BlockSpec((1, D, FB), lambda i, k: (k, 0, 0))
    return pl.pallas_call(
        body, name="ffn_up_bwd", grid=(T // tm, NCHIP),
        in_specs=[tile, blk, blk, wspec, wspec, tile, vec],
        out_specs=[tile, vec, vec],
        out_shape=[jax.ShapeDtypeStruct((T, D), f32)] + [jax.ShapeDtypeStruct((1, D), f32)] * 2,
        scratch_shapes=[pltpu.VMEM((tm, D), f32)],
        compiler_params=_cp(("arbitrary", "arbitrary")),
    )(dr2, dgate, dup, wg, wu, r1, g1)


def mix_out_bwd(dr1, proj, mabc, wo, pa, pb, pc):
    T = dr1.shape[0]
    tm = min(TM_MIX, T)

    def body(dr_r, gt, mabc_r, wo_r, pa_r, pb_r, pc_r, dmabc_o, dgt_o, dya_o, dyb_o, dyc_o):
        dm = _dot(dr_r[...].astype(MX), wo_r[...], NT)
        dmx = []
        for j in range(3):
            s = _sigmoid(gt[:, j * D:(j + 1) * D].astype(f32))
            v = (dm * s).astype(MX)
            dmx.append(v)
            dmabc_o[:, j * D:(j + 1) * D] = v
            dgt_o[:, j * D:(j + 1) * D] = (dm * mabc_r[:, j * D:(j + 1) * D].astype(f32) * s * (1.0 - s)).astype(dgt_o.dtype)
        dya_o[...] = _dot(dmx[0], pa_r[...], NT)
        dyb = jnp.zeros((tm, AO), f32)
        for k in range(NCHIP):
            dyb = dyb + _dot(dmx[1][:, k * (D // NCHIP):(k + 1) * (D // NCHIP)], pb_r[k], NT)
        dyb_o[...] = dyb
        dyc_o[...] = _dot(dmx[2], pc_r[...], NT)

    full = lambda shape: pl.BlockSpec(shape, lambda i: (0,) * len(shape))
    tile = lambda w: pl.BlockSpec((tm, w), lambda i: (i, 0))
    return pl.pallas_call(
        body, name="mix_out_bwd", grid=(T // tm,),
        in_specs=[tile(D), tile(3 * D), tile(3 * D), full((D, D)), full((D, D)), full((NCHIP, AO, D // NCHIP)), full((D, D))],
        out_specs=[tile(3 * D), tile(3 * D), tile(D), tile(AO), tile(D)],
        out_shape=[jax.ShapeDtypeStruct((T, 3 * D), MX), jax.ShapeDtypeStruct((T, 3 * D), MX),
                   jax.ShapeDtypeStruct((T, D), f32), jax.ShapeDtypeStruct((T, AO), f32), jax.ShapeDtypeStruct((T, D), f32)],
        compiler_params=_cp(("parallel",), 56),
    )(dr1, proj, mabc, wo, pa, pb, pc)


def transpose_cast(x):
    T = x.shape[0]
    tm = min(512, T)

    def body(x_r, o_r):
        o_r[...] = x_r[...].T.astype(o_r.dtype)

    return pl.pallas_call(
        body, name="transpose_cast", grid=(T // tm,),
        in_specs=[pl.BlockSpec((tm, D), lambda i: (i, 0))], out_specs=pl.BlockSpec((D, tm), lambda i: (0, i)),
        out_shape=jax.ShapeDtypeStruct((D, T), MX), compiler_params=_cp(("parallel",)),
    )(x)


def tn_matmul(name, a, b, a_spec, b_spec, out_shape, out_spec, grid, a_is_t=False):
    nt = len(grid) - 1

    def body(a_r, b_r, o_r):
        @pl.when(pl.program_id(nt) == 0)
        def _():
            o_r[...] = jnp.zeros_like(o_r)
        av = a_r[...].reshape(a_r.shape[-2:]).astype(MX)
        bv = b_r[...].reshape(b_r.shape[-2:]).astype(MX)
        o_r[...] += _dot(av, bv, None if a_is_t else TN).reshape(o_r.shape)

    return pl.pallas_call(
        body, name=name, grid=grid, in_specs=[a_spec, b_spec], out_specs=out_spec,
        out_shape=jax.ShapeDtypeStruct(out_shape, f32),
        compiler_params=_cp(("parallel",) * nt + ("arbitrary",), 56),
    )(a, b)


def attn_pre_bwd(dyb, os_, lses):
    T = dyb.shape[0]
    tm = TM_FOLD

    def body(dy_r, o0, o1, o2, l0, l1, l2, ones_r, d0, d1, d2, f0, f1, f2, nat):
        o = [_unfold_in(nat, r, d) for r, (_, d) in zip((o0, o1, o2), GROUPS)]
        ls = [_unfold_in(nat, r, d) for r, (_, d) in zip((l0, l1, l2), GROUPS)]
        w = _group_weights(ls)
        dy = dy_r[...]
        t = dy * (w[0] * o[0] + w[1] * o[1] + w[2] * o[2])
        hi = t.astype(MX)
        lo = (t - hi.astype(f32)).astype(MX)
        c = _dot(hi, ones_r[...]) + _dot(lo, ones_r[...])
        for wg, do_o, df_o, (_, d) in zip(w, (d0, d1, d2), (f0, f1, f2), GROUPS):
            _fold_out(nat, wg * dy, do_o, d)
            _fold_out(nat, -wg * c, df_o, d)

    specs = _fold_specs(T, tm)
    return pl.pallas_call(
        body, name="attn_pre_bwd", grid=(T // tm,),
        in_specs=[pl.BlockSpec((tm, AO), lambda i: (i, 0))] + specs + specs + [pl.BlockSpec((AO, AO), lambda i: (0, 0))],
        out_specs=specs + specs,
        out_shape=[jax.ShapeDtypeStruct((d, T // d, AO), MX) for _, d in GROUPS]
        + [jax.ShapeDtypeStruct((d, T // d, AO), f32) for _, d in GROUPS],
        scratch_shapes=[pltpu.VMEM((AO // LANES, tm, LANES), f32)],
        compiler_params=_cp(("parallel",)),
    )(dyb, *os_, *lses, _head_ones())


def _head_ones():
    i = jnp.arange(AO) // HD
    return (i[:, None] == i[None, :]).astype(MX)


def attn_bwd(qf, kf, vf, dof, lse, df, g, nb):
    T = qf.shape[0]

    def body(q_ref, k_ref, v_ref, do_ref, l_ref, d_ref, dq_ref, dk_ref, dv_ref):
        lower, upper, head0 = _attn_masks()

        def step(b, carry):
            dk_c, dv_c = carry
            r0 = pl.multiple_of(b * BLK, BLK)
            rp = pl.multiple_of(jnp.maximum(b - 1, 0) * BLK, BLK)
            q, kc, vc = q_ref[pl.ds(r0, BLK), :], k_ref[pl.ds(r0, BLK), :], v_ref[pl.ds(r0, BLK), :]
            kp, vp = k_ref[pl.ds(rp, BLK), :], v_ref[pl.ds(rp, BLK), :]
            do, lse_v, d_v = do_ref[pl.ds(r0, BLK), :], l_ref[pl.ds(r0, BLK), :], d_ref[pl.ds(r0, BLK), :]
            mask_p = upper & ((b % nb) != 0)
            dq = []
            dk_new, dv_new, dk_prev, dv_prev = (jnp.zeros((BLK, LANES), f32) for _ in range(4))
            for h, hm in enumerate((head0, ~head0)):
                zero = jnp.zeros_like(q)
                qh, doh = jnp.where(hm, q, zero), jnp.where(hm, do, zero)
                lse_h, d_h = lse_v[:, h * HD:h * HD + 1], d_v[:, h * HD:h * HD + 1]
                pc = jnp.where(lower, jnp.exp(_dot(qh, kc, NT) - lse_h), 0.0)
                pp = jnp.where(mask_p, jnp.exp(_dot(qh, kp, NT) - lse_h), 0.0)
                dsc = (pc * (_dot(doh, vc, NT) + d_h)).astype(MX)
                dsp = (pp * (_dot(doh, vp, NT) + d_h)).astype(MX)
                dq.append(_dot(dsc, kc) + _dot(dsp, kp))
                dk_new = dk_new + _dot(dsc, qh, TN)
                dk_prev = dk_prev + _dot(dsp, qh, TN)
                dv_new = dv_new + _dot(pc.astype(MX), doh, TN)
                dv_prev = dv_prev + _dot(pp.astype(MX), doh, TN)
            dq_ref[pl.ds(r0, BLK), :] = jnp.where(head0, dq[0], dq[1]).astype(dq_ref.dtype)
            dk_ref[pl.ds(rp, BLK), :] = (dk_c + dk_prev).astype(dk_ref.dtype)
            dv_ref[pl.ds(rp, BLK), :] = (dv_c + dv_prev).astype(dv_ref.dtype)
            return dk_new, dv_new

        zero = jnp.zeros((BLK, LANES), f32)
        dk_c, dv_c = lax.fori_loop(0, T // BLK, step, (zero, zero), unroll=2)
        dk_ref[pl.ds(T - BLK, BLK), :] = dk_c.astype(dk_ref.dtype)
        dv_ref[pl.ds(T - BLK, BLK), :] = dv_c.astype(dv_ref.dtype)

    spec = pl.BlockSpec((T, LANES), lambda j: (0, j))
    return pl.pallas_call(
        body, name=f"attn_bwd{g}", grid=(AO // LANES,),
        in_specs=[spec] * 6, out_specs=[spec] * 3,
        out_shape=[jax.ShapeDtypeStruct((T, AO), MX)] * 3,
        compiler_params=_cp(("parallel",), 60),
    )(qf, kf, vf, dof, lse, df)


def unfold_rope_bwd(dqf, dkf, dvf, cos_t, sin_t, g, d):
    T = dqf.shape[0] * dqf.shape[1]
    tm = TM_FOLD

    def body(q_r, k_r, v_r, c_ref, s_ref, o_ref, nat):
        cos, sin = _tile4(c_ref[...]), _tile4(s_ref[...])
        for part, ref, scale in ((0, q_r, HD ** -0.5), (1, k_r, 1.0), (2, v_r, None)):
            x = _unfold_in(nat, ref, d)
            if scale is not None:
                x = (x * cos - _swap_halves(x) * sin) * scale
            o_ref[:, part * AO:(part + 1) * AO] = x.astype(o_ref.dtype)

    fold_spec = pl.BlockSpec((d, tm // d, AO), lambda i: (0, i, 0))
    tab = pl.BlockSpec((tm, LANES), lambda i: (i, 0))
    return pl.pallas_call(
        body, name=f"unfold_rope_bwd{g}", grid=(T // tm,),
        in_specs=[fold_spec] * 3 + [tab, tab],
        out_specs=pl.BlockSpec((tm, 3 * AO), lambda i: (i, 0)),
        out_shape=jax.ShapeDtypeStruct((T, 3 * AO), MX),
        scratch_shapes=[pltpu.VMEM((AO // LANES, tm, LANES), f32)],
        compiler_params=_cp(("parallel",)),
    )(dqf, dkf, dvf, cos_t, sin_t)


def conv_bwd(dya, proj, conv_w):
    T = dya.shape[0]
    tm = TM_AC
    last = T // tm - 1

    def body(dy_r, bch, hprev, dy_next, b_next, cw, d_o, dw_o, zs, ds):
        i = pl.program_id(0)
        pb = bch[...].astype(f32)
        bp, cp, hp = pb[:, :D], pb[:, D:2 * D], pb[:, 2 * D:]
        z = cp * hp
        hz = hprev[:, :D].astype(f32) * hprev[:, D:].astype(f32)
        zs[0:HALO, :] = jnp.where(i > 0, hz, 0.0)
        zs[HALO:HALO + tm, :] = z
        z2, z1 = zs[HALO - 2:HALO - 2 + tm, :], zs[HALO - 1:HALO - 1 + tm, :]
        cv = cw[0:1, :] * z2 + cw[1:2, :] * z1 + cw[2:3, :] * z
        dy = dy_r[...]
        dcv = dy * bp
        ds[0:tm, :] = dcv
        ds[tm:tm + HALO, :] = jnp.where(i < last, dy_next[...] * b_next[...].astype(f32), 0.0)
        dz = cw[2:3, :] * dcv + cw[1:2, :] * ds[1:1 + tm, :] + cw[0:1, :] * ds[2:2 + tm, :]
        d_o[:, :D] = (dy * cv).astype(d_o.dtype)
        d_o[:, D:2 * D] = (dz * hp).astype(d_o.dtype)
        d_o[:, 2 * D:] = (dz * cp).astype(d_o.dtype)

        @pl.when(i == 0)
        def _():
            dw_o[...] = jnp.zeros_like(dw_o)
        dw_o[0:1, :] += jnp.sum(dcv * z2, axis=0, keepdims=True)
        dw_o[1:2, :] += jnp.sum(dcv * z1, axis=0, keepdims=True)
        dw_o[2:3, :] += jnp.sum(dcv * z, axis=0, keepdims=True)

    nh = tm // HALO
    return pl.pallas_call(
        body, name="conv_bwd", grid=(T // tm,),
        in_specs=[pl.BlockSpec((tm, D), lambda i: (i, 0)), pl.BlockSpec((tm, 3 * D), lambda i: (i, 1)),
                  pl.BlockSpec((HALO, 2 * D), lambda i: (jnp.maximum(i * nh - 1, 0), 2)),
                  pl.BlockSpec((HALO, D), lambda i: (jnp.minimum((i + 1) * nh, T // HALO - 1), 0)),
                  pl.BlockSpec((HALO, D), lambda i: (jnp.minimum((i + 1) * nh, T // HALO - 1), 3)),
                  pl.BlockSpec((3, D), lambda i: (0, 0))],
        out_specs=[pl.BlockSpec((tm, 3 * D), lambda i: (i, 0)), pl.BlockSpec((3, D), lambda i: (0, 0))],
        out_shape=[jax.ShapeDtypeStruct((T, 3 * D), MX), jax.ShapeDtypeStruct((3, D), f32)],
        scratch_shapes=[pltpu.VMEM((HALO + tm, D), f32), pltpu.VMEM((tm + HALO, D), f32)],
        compiler_params=_cp(("arbitrary",)),
    )(dya, proj, proj, dya, proj, conv_w)


def gmlp_bwd(dyc, proj, wst, bsx, lg, lb):
    T = dyc.shape[0]
    tm = TM_AC
    last = T // tm - 1

    def body(dy_r, u0, u1, v0, v1, ws, bs, lg_r, lb_r, d_o, dws_o, dbs_o, dlg_o, dlb_o, bacc):
        i = pl.program_id(0)
        up = jnp.concatenate([u0[...], u1[...]], axis=1).astype(f32)
        vp = jnp.concatenate([v0[...], v1[...]], axis=1).astype(f32)
        u, vn, xhat, rstd, sp = _gmlp_fwd(up, vp, ws, bs, lg_r[...], lb_r[...])
        dy = dy_r[...]
        d_o[:, :D] = (dy * sp * _gelu_grad(up)).astype(d_o.dtype)
        dsp = dy * u
        dspb, vnb = dsp.astype(MX), vn.astype(MX)

        @pl.when(i == 0)
        def _():
            dws_o[...] = jnp.zeros_like(dws_o)
            bacc[...] = jnp.zeros_like(bacc)

        rows = []
        for c in range(tm // BLK):
            r = slice(c * BLK, (c + 1) * BLK)
            cols = []
            for g in range(8):
                cs = slice(g * BLK, (g + 1) * BLK)
                dws_o[g] += _dot(dspb[r, cs], vnb[r, cs], NT)
                bacc[g] += dsp[r, cs]
                cols.append(_dot(ws[g], dspb[r, cs], TN))
            rows.append(jnp.concatenate(cols, axis=1))
        dvn = jnp.concatenate(rows, axis=0)
        _acc_rows(dlg_o, i == 0, dvn * xhat)
        _acc_rows(dlb_o, i == 0, dvn)
        d_o[:, D:] = (_ln_bwd(dvn, xhat, rstd, lg_r[...]) * _gelu_grad(vp)).astype(d_o.dtype)

        @pl.when(i == last)
        def _():
            row = lax.broadcasted_iota(jnp.int32, (BLK, BLK), 0)
            col = lax.broadcasted_iota(jnp.int32, (BLK, BLK), 1)
            ones = jnp.ones((8, BLK), MX)
            for g in range(8):
                dws_o[g] = jnp.where(col <= row, dws_o[g], 0.0)
                a = bacc[g]
                hi = a.astype(MX)
                lo = (a - hi.astype(f32)).astype(MX)
                dbs_o[g:g + 1, :] = (_dot(ones, hi, NT) + _dot(ones, lo, NT))[0:1, :]

    full = lambda shape: pl.BlockSpec(shape, lambda i: (0,) * len(shape))
    return pl.pallas_call(
        body, name="gmlp_bwd", grid=(T // tm,),
        in_specs=[pl.BlockSpec((tm, D), lambda i: (i, 0)), *_uv_specs(), full((8, BLK, BLK)), full((8, BLK, BLK)),
                  full((1, D)), full((1, D))],
        out_specs=[pl.BlockSpec((tm, 2 * D), lambda i: (i, 0)), full((8, BLK, BLK)), full((8, BLK)), full((1, D)), full((1, D))],
        out_shape=[jax.ShapeDtypeStruct((T, 2 * D), MX), jax.ShapeDtypeStruct((8, BLK, BLK), f32),
                   jax.ShapeDtypeStruct((8, BLK), f32), jax.ShapeDtypeStruct((1, D), f32), jax.ShapeDtypeStruct((1, D), f32)],
        scratch_shapes=[pltpu.VMEM((8, BLK, BLK), f32)],
        compiler_params=_cp(("arbitrary",)),
    )(dyc, proj, proj, proj, proj, wst, bsx, lg, lb)


PART_TILES = (6, 6, 3, 3, 3, 4)
PART_START = (0, 6, 12, 15, 18, 21)
TJ = 512


def _part_specs(tm, rows_axis):
    specs = []
    for n, s in zip(PART_TILES, PART_START):
        def imap(*idx, n=n, s=s):
            i, j = idx[rows_axis], idx[1 - rows_axis]
            return (i, jnp.clip(j - s, 0, n - 1))
        specs.append(pl.BlockSpec((tm, TJ), imap))
    return specs


def _with_part(j, refs, fn):
    for r, n, s in zip(refs, PART_TILES, PART_START):
        @pl.when((j >= s) & (j < s + n))
        def _():
            fn(r[...])


def dx_in(dr1, parts, w):
    T = dr1.shape[0]
    tm = min(1024, T)

    def body(dr_r, p0, p1, p2, p3, p4, p5, w_r, o_r):
        j = pl.program_id(1)

        @pl.when(j == 0)
        def _():
            o_r[...] = ALPHA * dr_r[...]

        def acc(tile):
            o_r[...] += _dot(tile, w_r[...], NT)
        _with_part(j, (p0, p1, p2, p3, p4, p5), acc)

    return pl.pallas_call(
        body, name="dx_in", grid=(T // tm, NIN // TJ),
        in_specs=[pl.BlockSpec((tm, D), lambda i, j: (i, 0))] + _part_specs(tm, 0) + [pl.BlockSpec((D, TJ), lambda i, j: (0, j))],
        out_specs=pl.BlockSpec((tm, D), lambda i, j: (i, 0)),
        out_shape=jax.ShapeDtypeStruct((T, D), f32),
        compiler_params=_cp(("parallel", "arbitrary"), 56),
    )(dr1, *parts, w)


def dw_in(x0t, parts):
    T = x0t.shape[1]
    tk = min(1024, T)

    def body(x_r, p0, p1, p2, p3, p4, p5, o_r):
        j, t = pl.program_id(0), pl.program_id(1)

        @pl.when(t == 0)
        def _():
            o_r[...] = jnp.zeros_like(o_r)

        def acc(tile):
            o_r[...] += _dot(x_r[...], tile)
        _with_part(j, (p0, p1, p2, p3, p4, p5), acc)

    return pl.pallas_call(
        body, name="dw_in", grid=(NIN // TJ, T // tk),
        in_specs=[pl.BlockSpec((D, tk), lambda j, t: (0, t))] + _part_specs(tk, 1),
        out_specs=pl.BlockSpec((D, TJ), lambda j, t: (0, j)),
        out_shape=jax.ShapeDtypeStruct((D, NIN), f32),
        compiler_params=_cp(("parallel", "arbitrary")),
    )(x0t, *parts)


def rope_tables(positions):
    half = HD // 2
    inv_freq = ROPE_THETA ** (-jnp.arange(half, dtype=f32) / half)
    ang = positions.astype(f32)[:, None] * inv_freq
    cos, sin = jnp.cos(ang), jnp.sin(ang)
    return jnp.tile(cos, (1, LANES // half)), jnp.tile(jnp.concatenate([-sin, sin], axis=1), (1, LANES // HD))


def _flat(a):
    return a.reshape(a.shape[0] * a.shape[1], a.shape[2])


def layer_fwd(x0, W, cos_t, sin_t):
    T = x0.shape[0]
    proj = mm_in(x0, W["w_in"])
    ya, yc = mix_ac_fwd(proj, W["conv_w"], W["wst"], W["bsx"], W["gmlp_ln_g"], W["gmlp_ln_b"])
    folded, os_, lses = [], [], []
    for g, (_, d) in enumerate(GROUPS):
        qf, kf, vf = fold_rope(proj, cos_t, sin_t, g, d)
        o, lse = attn_fwd(_flat(qf), _flat(kf), _flat(vf), g, T // d // BLK)
        folded.append((qf, kf, vf))
        os_.append(o.reshape(d, T // d, AO))
        lses.append(lse.reshape(d, T // d, AO))
    yb = combine_fwd(os_, lses)
    mabc, m, r1, x1 = mix_out_fwd(proj, ya, yb, yc, x0, W["p_a"], W["p_b"], W["p_c"], W["w_o"], W["ln1_g"], W["ln1_b"])
    gate, up, hh = ffn_up_fwd(x1, W["w_gate"], W["w_up"])
    r2, x2 = ffn_down_fwd(hh, W["w_down"], x1, W["ln2_g"], W["ln2_b"])
    saved = dict(x0=x0, proj=proj, ya=ya, yb=yb, yc=yc, folded=folded, os=os_, lses=lses, mabc=mabc, m=m, r1=r1,
                 x1=x1, gate=gate, up=up, hh=hh, r2=r2)
    return x2, saved


def layer_bwd(dx2, S, W, cos_t, sin_t):
    T = dx2.shape[0]
    tk = min(512, T)
    G = {}
    dr2, dgate, dup, G["ln2_g"], G["ln2_b"] = ffn_down_bwd(dx2, S["r2"], W["ln2_g"], W["w_down"], S["gate"], S["up"])
    blk_a = pl.BlockSpec((1, tk, FB), lambda k, t: (k, t, 0))
    row_b = pl.BlockSpec((tk, D), lambda k, t: (t, 0))
    G["w_down"] = tn_matmul("dw_down", S["hh"], dr2, blk_a, row_b, (NCHIP, FB, D),
                            pl.BlockSpec((1, FB, D), lambda k, t: (k, 0, 0)), (NCHIP, T // tk))
    x1t = transpose_cast(S["x1"])
    for nm, dv in (("w_gate", dgate), ("w_up", dup)):
        G[nm] = tn_matmul("d" + nm, x1t, dv, pl.BlockSpec((D, tk), lambda k, t: (0, t)), blk_a, (NCHIP, D, FB),
                          pl.BlockSpec((1, D, FB), lambda k, t: (k, 0, 0)), (NCHIP, T // tk), a_is_t=True)
    dr1, G["ln1_g"], G["ln1_b"] = ffn_up_bwd(dr2, dgate, dup, W["w_gate"], W["w_up"], S["r1"], W["ln1_g"])
    dmabc, dgates, dya, dyb, dyc = mix_out_bwd(dr1, S["proj"], S["mabc"], W["w_o"], W["p_a"], W["p_b"], W["p_c"])
    one = (1, T // tk)
    full_o = pl.BlockSpec((D, D), lambda k, t: (0, 0))
    G["w_o"] = tn_matmul("dw_o", S["m"], dr1, row_b, row_b, (D, D), full_o, one)
    G["p_a"] = tn_matmul("dp_a", S["ya"], dmabc, row_b, pl.BlockSpec((tk, D), lambda k, t: (t, 0)), (D, D), full_o, one)
    G["p_c"] = tn_matmul("dp_c", S["yc"], dmabc, row_b, pl.BlockSpec((tk, D), lambda k, t: (t, 2)), (D, D), full_o, one)
    G["p_b"] = tn_matmul("dp_b", S["yb"], dmabc, pl.BlockSpec((tk, AO), lambda k, t: (t, 0)),
                         pl.BlockSpec((tk, D // NCHIP), lambda k, t: (t, NCHIP + k)), (NCHIP, AO, D // NCHIP),
                         pl.BlockSpec((1, AO, D // NCHIP), lambda k, t: (k, 0, 0)), (NCHIP, T // tk))
    dbch, G["conv_w"] = conv_bwd(dya, S["proj"], W["conv_w"])
    duv, G["w_s"], G["b_s"], G["gmlp_ln_g"], G["gmlp_ln_b"] = gmlp_bwd(
        dyc, S["proj"], W["wst"], W["bsx"], W["gmlp_ln_g"], W["gmlp_ln_b"])
    pre = attn_pre_bwd(dyb, S["os"], S["lses"])
    dqkv = []
    for g, (_, d) in enumerate(GROUPS):
        qf, kf, vf = S["folded"][g]
        dqf, dkf, dvf = attn_bwd(_flat(qf), _flat(kf), _flat(vf), _flat(pre[g]), _flat(S["lses"][g]), _flat(pre[3 + g]),
                                 g, T // d // BLK)
        shp = (d, T // d, AO)
        dqkv.append(unfold_rope_bwd(dqf.reshape(shp), dkf.reshape(shp), dvf.reshape(shp), cos_t, sin_t, g, d))
    parts = (dgates, dbch, *dqkv, duv)
    G["w_in"] = dw_in(transpose_cast(S["x0"]), parts)
    dx0 = dx_in(dr1, parts, W["w_in"])
    return dx0, G


def prep_layer_weights(Wl):
    W = dict(Wl)
    tril = jnp.tril(jnp.ones((BLK, BLK), f32))
    W["wst"] = (Wl["w_s"] * tril[None]).astype(MX)
    W["bsx"] = jnp.broadcast_to(Wl["b_s"][:, :, None], (8, BLK, BLK))
    for n in ("gmlp_ln_g", "gmlp_ln_b", "ln1_g", "ln1_b", "ln2_g", "ln2_b"):
        W[n] = Wl[n].reshape(1, D)
    return W


def local_step(x, positions, target, layers):
    cos_t, sin_t = rope_tables(positions)
    Ws = [prep_layer_weights(Wl) for Wl in layers]
    saved = []
    h = x
    for W in Ws:
        h, S = layer_fwd(h, W, cos_t, sin_t)
        saved.append(S)
    lsum, dh = loss_grad(h, target)
    grads = [None] * len(Ws)
    for l in reversed(range(len(Ws))):
        dh, grads[l] = layer_bwd(dh, saved[l], Ws[l], cos_t, sin_t)
    return lsum, dh, grads


MESH = pl.DeviceIdType.MESH
ANY = pl.BlockSpec(memory_space=pl.ANY)
BIG = ("w_in", "w_gate", "w_up", "w_down", "p_a", "p_b", "p_c", "w_o")
NBIG = len(BIG)


def _place():
    x, y, c = lax.axis_index("x"), lax.axis_index("y"), lax.axis_index("c")
    return x, y, c, 2 * x + y


def _rcopy(src, dst, send, recv, dev):
    return pltpu.make_async_remote_copy(src_ref=src, dst_ref=dst, send_sem=send, recv_sem=recv, device_id=dev,
                                        device_id_type=MESH)


def _cols(ref, k, width):
    start = k * width if isinstance(k, int) else pl.multiple_of(k * width, LANES)
    return ref.at[:, pl.ds(start, width)]


CHUNK_BYTES = 1 << 20


def _pieces(shape, itemsize, nbytes=CHUNK_BYTES):
    rows, cols = shape[-2], shape[-1]
    per = max(16, nbytes // (cols * itemsize) // 16 * 16)
    out = []
    for lead in (range(shape[0]) if len(shape) == 3 else (None,)):
        for r in range(0, rows, per):
            sl = (pl.ds(r, min(per, rows - r)), slice(None))
            out.append(sl if lead is None else (lead,) + sl)
    return out


def _start_pieces(src, dst, make, nbytes=CHUNK_BYTES):
    for idx in _pieces(src.shape, jnp.dtype(src.dtype).itemsize, nbytes):
        make(src.at[idx], dst.at[idx]).start()


def gather_weights(shards):
    n = len(shards)

    def body(*refs):
        srcs, dsts = refs[:n], refs[n:2 * n]
        send, recv, own_send, own_recv = refs[2 * n:]
        x, y, c, k = _place()
        sib = (x, y, 1 - c)
        chips = [(1 - x, y), (x, 1 - y), (1 - x, 1 - y)]

        def slot(a, layer, pos):
            if a == 0:
                return _cols(dsts[0].at[layer], pos, WIN_SHARD)
            return dsts[a].at[layer, pos]

        def ici(a, j, src, dst):
            return _rcopy(src, dst, send.at[a, j], recv.at[a, j], (*chips[j], c))

        def d2d(a, j, src, dst):
            return _rcopy(src, dst, send.at[a, 3 + j], recv.at[a, 3 + j], sib)

        def own(a, layer, src, dst):
            return _rcopy(src, dst, own_send.at[a, layer], own_recv.at[a, layer], sib)

        for a in range(n):
            for j in range(3):
                _start_pieces(srcs[a].at[c], slot(a, c, k), functools.partial(ici, a, j))
        for a in range(n):
            for layer in range(DEPTH):
                _start_pieces(srcs[a].at[layer], slot(a, layer, k), functools.partial(own, a, layer))
        for a in range(n):
            for j, (cx, cy) in enumerate(chips):
                landed = slot(a, c, 2 * cx + cy)
                ici(a, j, landed, landed).wait_recv()
                _start_pieces(landed, landed, functools.partial(d2d, a, j))
        for a in range(n):
            for j, (cx, cy) in enumerate(chips):
                passed = slot(a, 1 - c, 2 * cx + cy)
                d2d(a, j, passed, passed).wait_recv()
                landed = slot(a, c, 2 * cx + cy)
                d2d(a, j, landed, landed).wait_send()
                ici(a, j, srcs[a].at[c], slot(a, c, k)).wait_send()
            for layer in range(DEPTH):
                own(a, layer, srcs[a].at[layer], slot(a, layer, k)).wait()

    outs = [jax.ShapeDtypeStruct((DEPTH, D, NIN), shards[0].dtype)]
    outs += [jax.ShapeDtypeStruct((DEPTH, NCHIP) + s.shape[1:], s.dtype) for s in shards[1:]]
    return pl.pallas_call(
        body, name="gather_weights", in_specs=[ANY] * n, out_specs=[ANY] * n, out_shape=outs,
        scratch_shapes=[pltpu.SemaphoreType.DMA((n, 6)), pltpu.SemaphoreType.DMA((n, 6)),
                        pltpu.SemaphoreType.DMA((n, DEPTH)), pltpu.SemaphoreType.DMA((n, DEPTH))],
    )(*shards)


def _half(ref, h, a):
    rows = ref.shape[-2] // 2
    start = pl.multiple_of(h * rows, 16)
    if a == 0:
        return ref.at[pl.ds(start, rows), :]
    return ref.at[:, pl.ds(start, rows), :]


def rs_pair(l, grads):
    def body(*refs):
        g, theirs = refs[:NBIG], refs[NBIG:2 * NBIG]
        send, recv = refs[2 * NBIG:]
        x, y, c, _ = _place()

        def give(a, s, d):
            return _rcopy(s, d, send.at[a], recv.at[a], (x, y, 1 - c))

        for a in range(NBIG):
            _start_pieces(_half(g[a], 1 - c, a), theirs[a], functools.partial(give, a))
        for a in range(NBIG):
            give(a, _half(g[a], 1 - c, a), theirs[a]).wait()

    def hshape(s, a):
        return (s[0] // 2, s[1]) if a == 0 else (s[0], s[1] // 2, s[2])

    outs = [jax.ShapeDtypeStruct(hshape(g.shape, a), g.dtype) for a, g in enumerate(grads)]
    return pl.pallas_call(
        body, name=f"rs_pair{l}", in_specs=[ANY] * NBIG, out_specs=[ANY] * NBIG, out_shape=outs,
        scratch_shapes=[pltpu.SemaphoreType.DMA((NBIG,))] * 2,
    )(*grads)


def rs_chips(l, sums):
    def body(*refs):
        s, land = refs[:NBIG], refs[NBIG:2 * NBIG]
        send, recv = refs[2 * NBIG:]
        x, y, c, me = _place()

        def piece(a, k):
            return _cols(s[a], k, WIN_SHARD) if a == 0 else s[a].at[k]

        def give(a, k, src, dst):
            return _rcopy(src, dst, send.at[a, k], recv.at[a, me], (k // 2, k % 2, c))

        for k in range(NCHIP):
            @pl.when(me != k)
            def _():
                for a in range(NBIG):
                    _start_pieces(piece(a, k), land[a].at[me], functools.partial(give, a, k))
        for k in range(NCHIP):
            @pl.when(me != k)
            def _():
                for a in range(NBIG):
                    give(a, k, piece(a, k), land[a].at[me]).wait_send()
                    _rcopy(piece(a, k), land[a].at[k], send.at[a, k], recv.at[a, k], (k // 2, k % 2, c)).wait_recv()

    def pshape(s, a):
        return (NCHIP, s[0], WIN_SHARD) if a == 0 else s

    outs = [jax.ShapeDtypeStruct(pshape(v.shape, a), v.dtype) for a, v in enumerate(sums)]
    return pl.pallas_call(
        body, name=f"rs_chips{l}", in_specs=[ANY] * NBIG, out_specs=[ANY] * NBIG, out_shape=outs,
        scratch_shapes=[pltpu.SemaphoreType.DMA((NBIG, NCHIP)), pltpu.SemaphoreType.DMA((NBIG, NCHIP))],
    )(*sums)


def rs_join(l, halves):
    def body(*refs):
        h, other = refs[:NBIG], refs[NBIG:2 * NBIG]
        send, recv = refs[2 * NBIG:]
        x, y, c, _ = _place()

        def give(a, s, d):
            return _rcopy(s, d, send.at[a], recv.at[a], (x, y, 1 - c))

        for a in range(NBIG):
            _start_pieces(h[a], other[a], functools.partial(give, a))
        for a in range(NBIG):
            give(a, h[a], other[a]).wait()

    outs = [jax.ShapeDtypeStruct(v.shape, v.dtype) for v in halves]
    return pl.pallas_call(
        body, name=f"rs_join{l}", in_specs=[ANY] * NBIG, out_specs=[ANY] * NBIG, out_shape=outs,
        scratch_shapes=[pltpu.SemaphoreType.DMA((NBIG,))] * 2,
    )(*halves)


def _row_tile(rows, cols, itemsize=4, target=2 << 20):
    best = 8
    for t in range(8, rows + 1, 8):
        if rows % t == 0 and t * cols * itemsize <= target:
            best = t
    return best


GRAD_WIRE = jnp.bfloat16


def add_n(name, terms, out_dtype=f32):
    shape = terms[0].shape
    cols = shape[-1]
    rows = math.prod(shape[:-1])
    tr = _row_tile(rows, cols)

    def body(*refs):
        acc = refs[0][...]
        for r in refs[1:-1]:
            acc = acc + r[...]
        refs[-1][...] = acc.astype(out_dtype)

    tile = pl.BlockSpec((tr, cols), lambda i: (i, 0))
    out = pl.pallas_call(
        body, name=name, grid=(rows // tr,), in_specs=[tile] * len(terms), out_specs=tile,
        out_shape=jax.ShapeDtypeStruct((rows, cols), out_dtype), compiler_params=_cp(("parallel",)),
    )(*[t.reshape(rows, cols) for t in terms])
    return out.reshape(shape)


def add_chips(name, land, own):
    _, rows, cols = land.shape
    tr = _row_tile(rows, cols, target=1 << 20)

    def body(land_r, own_r, o_r):
        me = 2 * lax.axis_index("x") + lax.axis_index("y")
        for k in range(NCHIP):
            @pl.when(me == k)
            def _():
                acc = None
                for j in range(NCHIP):
                    t = (own_r[...] if j == k else land_r[j]).astype(f32)
                    acc = t if acc is None else acc + t
                o_r[...] = acc

    tile = pl.BlockSpec((tr, cols), lambda i: (i, 0))
    return pl.pallas_call(
        body, name=name, grid=(rows // tr,), in_specs=[pl.BlockSpec((NCHIP, tr, cols), lambda i: (0, i, 0)), tile],
        out_specs=tile, out_shape=jax.ShapeDtypeStruct((rows, cols), f32), compiler_params=_cp(("parallel",)),
    )(land, own)


def reduce_scatter_layer(l, G):
    c = lax.axis_index("c")
    me = 2 * lax.axis_index("x") + lax.axis_index("y")
    grads = [G[n] if G[n].ndim == 3 or n == "w_in" else G[n].reshape(NCHIP, D // NCHIP, D) for n in BIG]
    theirs = rs_pair(l, grads)
    sums = []
    for n, g, t in zip(BIG, grads, theirs):
        rows = g.shape[-2] // 2
        mine = lax.dynamic_slice_in_dim(g, c * rows, rows, axis=g.ndim - 2)
        sums.append(add_n(f"rs_add_pair{l}_{n}", [mine, t], GRAD_WIRE))
    landed = rs_chips(l, sums)
    halves = []
    for a, (n, s, v) in enumerate(zip(BIG, sums, landed)):
        own = lax.dynamic_slice_in_dim(s, me * WIN_SHARD, WIN_SHARD, axis=1) if a == 0 else \
            lax.dynamic_index_in_dim(s, me, 0, keepdims=False)
        halves.append(add_chips(f"rs_add_chips{l}_{n}", v, own))
    return dict(zip(BIG, zip(halves, rs_join(l, halves))))


NDEV = 8


def allreduce_small(pack):
    rows = pack.shape[0]

    def body(p_ref, o_ref, buf, send, recv):
        x, y, c, _ = _place()
        me = 4 * x + 2 * y + c
        buf[me] = p_ref[...]

        def give(r, s, d):
            return _rcopy(s, d, send.at[r - 1], recv.at[r - 1], (x ^ (r >> 2), y ^ ((r >> 1) & 1), c ^ (r & 1)))

        for r in range(1, NDEV):
            _start_pieces(p_ref, buf.at[me], functools.partial(give, r), 128 << 10)
        for r in range(1, NDEV):
            give(r, p_ref, buf.at[me]).wait_send()
            src = 4 * (x ^ (r >> 2)) + 2 * (y ^ ((r >> 1) & 1)) + (c ^ (r & 1))
            give(r, p_ref, buf.at[src]).wait_recv()
        acc = buf[0]
        for d in range(1, NDEV):
            acc = acc + buf[d]
        o_ref[...] = acc

    vm = pl.BlockSpec(memory_space=pltpu.VMEM)
    return pl.pallas_call(
        body, name="allreduce_small", in_specs=[vm], out_specs=vm, out_shape=jax.ShapeDtypeStruct(pack.shape, f32),
        scratch_shapes=[pltpu.VMEM((NDEV, rows, LANES), f32), pltpu.SemaphoreType.DMA((NDEV - 1,)),
                        pltpu.SemaphoreType.DMA((NDEV - 1,))],
        compiler_params=pltpu.CompilerParams(vmem_limit_bytes=40 << 20),
    )(pack)


def _adamw_math(w, g, m, v):
    m = ADAM_B1 * m + (1.0 - ADAM_B1) * g
    v = ADAM_B2 * v + (1.0 - ADAM_B2) * (g * g)
    m_hat = m / (1.0 - ADAM_B1 ** ADAM_STEP)
    v_hat = v / (1.0 - ADAM_B2 ** ADAM_STEP)
    return -ADAM_LR * (m_hat / (jnp.sqrt(v_hat) + ADAM_EPS) + ADAM_WD * w), m, v


def adamw_big(name, halves, w, m, v):
    _, R, C = w.shape
    tr = _row_tile(R // 2, C, target=1 << 20)
    nt = R // 2 // tr

    def body(a0, b0, a1, b1, w_r, m_r, v_r, g_o, d_o, m_o, v_o):
        mine = pl.program_id(1) == lax.axis_index("c")
        g = jnp.where(pl.program_id(0) == 0, jnp.where(mine, a0[...], b0[...]), jnp.where(mine, a1[...], b1[...]))
        g_o[...] = g
        d_o[...], m_o[...], v_o[...] = _adamw_math(w_r[...], g, m_r[...], v_r[...])

    stk = pl.BlockSpec((None, tr, C), lambda l, h, i: (l, h * nt + i, 0))
    lay0 = pl.BlockSpec((tr, C), lambda l, h, i: (jnp.where(l == 0, i, nt - 1), 0))
    lay1 = pl.BlockSpec((tr, C), lambda l, h, i: (jnp.where(l == 0, 0, i), 0))
    return pl.pallas_call(
        body, name=name, grid=(DEPTH, 2, nt),
        in_specs=[lay0, lay0, lay1, lay1, stk, stk, stk],
        out_specs=[stk] * 4, out_shape=[jax.ShapeDtypeStruct(w.shape, f32)] * 4,
        compiler_params=_cp(("arbitrary", "arbitrary", "arbitrary")),
    )(*halves[0], *halves[1], w, m, v)


def adamw_small(name, g, w, m, v):
    def body(g_r, w_r, m_r, v_r, d_o, m_o, v_o):
        d_o[...], m_o[...], v_o[...] = _adamw_math(w_r[...], g_r[...], m_r[...], v_r[...])

    return pl.pallas_call(body, name=name, out_shape=[jax.ShapeDtypeStruct(w.shape, f32)] * 3)(g, w, m, v)


WEIGHTS = ("w_in", "conv_w", "gmlp_ln_g", "gmlp_ln_b", "w_s", "b_s", "p_a", "p_b", "p_c", "w_o", "ln1_g", "ln1_b",
           "w_gate", "w_up", "w_down", "ln2_g", "ln2_b")
VECS = ("ln1_g", "ln1_b", "ln2_g", "ln2_b", "gmlp_ln_g", "gmlp_ln_b")
ROWS_VEC, ROWS_BS, ROWS_WS, ROWS_CONV = D // LANES, 8, 8 * BLK, 3 * D // LANES
ROWS_LAYER = len(VECS) * ROWS_VEC + ROWS_BS + ROWS_WS + ROWS_CONV


def _pack_small(per_layer, tail):
    parts = []
    for P in per_layer:
        parts += [P[n].reshape(ROWS_VEC, LANES) for n in VECS]
        parts += [P["b_s"].reshape(ROWS_BS, LANES), P["w_s"].reshape(ROWS_WS, LANES), P["conv_w"].reshape(ROWS_CONV, LANES)]
    return jnp.concatenate(parts + [tail], axis=0)


def _unpack_small(pack):
    out = []
    for l in range(DEPTH):
        r = l * ROWS_LAYER
        P = {}
        for n in VECS:
            P[n] = pack[r:r + ROWS_VEC].reshape(D)
            r += ROWS_VEC
        P["b_s"] = pack[r:r + ROWS_BS].reshape(8, BLK)
        r += ROWS_BS
        P["w_s"] = pack[r:r + ROWS_WS].reshape(8, BLK, BLK)
        r += ROWS_WS
        P["conv_w"] = pack[r:r + ROWS_CONV].reshape(3, D)
        out.append(P)
    return out, pack[DEPTH * ROWS_LAYER:]


def kernel(x, positions, w_in, conv_w, gmlp_ln_g, gmlp_ln_b, w_s, b_s, p_a, p_b, p_c, w_o, ln1_g, ln1_b, w_gate, w_up, w_down, ln2_g, ln2_b, loss_target, m_w_in, m_conv_w, m_gmlp_ln_g, m_gmlp_ln_b, m_w_s, m_b_s, m_p_a, m_p_b, m_p_c, m_w_o, m_ln1_g, m_ln1_b, m_w_gate, m_w_up, m_w_down, m_ln2_g, m_ln2_b, v_w_in, v_conv_w, v_gmlp_ln_g, v_gmlp_ln_b, v_w_s, v_b_s, v_p_a, v_p_b, v_p_c, v_w_o, v_ln1_g, v_ln1_b, v_w_gate, v_w_up, v_w_down, v_ln2_g, v_ln2_b):
    Wt = dict(w_in=w_in, conv_w=conv_w, gmlp_ln_g=gmlp_ln_g, gmlp_ln_b=gmlp_ln_b, w_s=w_s, b_s=b_s, p_a=p_a, p_b=p_b,
              p_c=p_c, w_o=w_o, ln1_g=ln1_g, ln1_b=ln1_b, w_gate=w_gate, w_up=w_up, w_down=w_down, ln2_g=ln2_g, ln2_b=ln2_b)
    Mt = dict(w_in=m_w_in, conv_w=m_conv_w, gmlp_ln_g=m_gmlp_ln_g, gmlp_ln_b=m_gmlp_ln_b, w_s=m_w_s, b_s=m_b_s, p_a=m_p_a,
              p_b=m_p_b, p_c=m_p_c, w_o=m_w_o, ln1_g=m_ln1_g, ln1_b=m_ln1_b, w_gate=m_w_gate, w_up=m_w_up,
              w_down=m_w_down, ln2_g=m_ln2_g, ln2_b=m_ln2_b)
    Vt = dict(w_in=v_w_in, conv_w=v_conv_w, gmlp_ln_g=v_gmlp_ln_g, gmlp_ln_b=v_gmlp_ln_b, w_s=v_w_s, b_s=v_b_s, p_a=v_p_a,
              p_b=v_p_b, p_c=v_p_c, w_o=v_w_o, ln1_g=v_ln1_g, ln1_b=v_ln1_b, w_gate=v_w_gate, w_up=v_w_up,
              w_down=v_w_down, ln2_g=v_ln2_g, ln2_b=v_ln2_b)
    chip = 2 * lax.axis_index("x") + lax.axis_index("y")
    cw = D // NCHIP

    full = gather_weights([Wt[n].astype(MX) for n in BIG] + [conv_w])
    layers = []
    for l in range(DEPTH):
        Wl = dict(zip(BIG, (f[l] for f in full[:NBIG])))
        for n in ("p_a", "p_c", "w_o"):
            Wl[n] = Wl[n].reshape(D, D)
        Wl["conv_w"] = full[NBIG][l].transpose(1, 0, 2).reshape(3, D)
        for n in VECS + ("w_s", "b_s"):
            Wl[n] = Wt[n][l]
        layers.append(Wl)

    lsum, grad_x, grads = local_step(x[0], positions[0], loss_target[0], layers)

    red = [None] * DEPTH
    for l in reversed(range(DEPTH)):
        red[l] = reduce_scatter_layer(l, grads[l])
    pack = _pack_small([{n: (g[n] if n not in VECS else g[n]) for n in VECS + ("b_s", "w_s", "conv_w")} for g in grads], lsum)
    small, tail = _unpack_small(allreduce_small(pack))
    loss = tail[0, 0]

    G, DW, NM, NV = {}, {}, {}, {}
    for n in BIG:
        G[n], DW[n], NM[n], NV[n] = adamw_big("adamw_" + n, (red[0][n], red[1][n]), Wt[n], Mt[n], Vt[n])
    zc = jnp.zeros((3, D), f32)
    wp = _pack_small([{**{n: Wt[n][l] for n in VECS + ("b_s", "w_s")}, "conv_w": zc} for l in range(DEPTH)], jnp.zeros((8, LANES), f32))
    mp = _pack_small([{**{n: Mt[n][l] for n in VECS + ("b_s", "w_s")}, "conv_w": zc} for l in range(DEPTH)], jnp.zeros((8, LANES), f32))
    vp = _pack_small([{**{n: Vt[n][l] for n in VECS + ("b_s", "w_s")}, "conv_w": zc} for l in range(DEPTH)], jnp.ones((8, LANES), f32))
    gp = _pack_small(small, jnp.zeros((8, LANES), f32))
    outs = [_unpack_small(a)[0] for a in adamw_small("adamw_small", gp, wp, mp, vp)]
    for n in VECS + ("b_s", "w_s"):
        G[n] = jnp.stack([small[l][n] for l in range(DEPTH)])
        DW[n], NM[n], NV[n] = (jnp.stack([o[l][n] for l in range(DEPTH)]) for o in outs)
    gconv = jnp.stack([lax.dynamic_slice(small[l]["conv_w"], (0, chip * cw), (3, cw)) for l in range(DEPTH)])
    G["conv_w"] = gconv
    flat = lambda a: a.reshape(DEPTH * 3, cw)
    d, m2, v2 = adamw_small("adamw_conv", flat(gconv), flat(conv_w), flat(m_conv_w), flat(v_conv_w))
    DW["conv_w"], NM["conv_w"], NV["conv_w"] = (a.reshape(DEPTH, 3, cw) for a in (d, m2, v2))

    return (loss, grad_x[None], *[G[n] for n in WEIGHTS], *[DW[n] for n in WEIGHTS], *[NM[n] for n in WEIGHTS],
            *[NV[n] for n in WEIGHTS])
```

```python
import functools
import math

import jax
import jax.numpy as jnp
from jax import lax
from jax.experimental import pallas as pl
from jax.experimental.pallas import tpu as pltpu

D = 1024
NIN = 12800
DFF = 2816
NCHIP = 4
FB = DFF // NCHIP
WIN_SHARD = NIN // NCHIP
DEPTH = 2
GROUPS = ((128, 1), (512, 4), (2048, 16))
HD = 64
BLK = 128
AO = 512
ALPHA = (2 * DEPTH) ** 0.25
EPS = 1e-5
ROPE_THETA = 10000.0
LANES = 128
NEG = -1e30

C_GATES, C_BCH, C_QKV, C_UV = 0, 3 * D, 6 * D, 6 * D + 9 * AO

MX = jnp.bfloat16
ACT = jnp.bfloat16

ADAM_LR, ADAM_B1, ADAM_B2, ADAM_EPS, ADAM_WD, ADAM_STEP = 0.001, 0.9, 0.999, 1e-08, 0.01, 10

f32 = jnp.float32
NT = (((1,), (1,)), ((), ()))
TN = (((0,), (0,)), ((), ()))


def _cp(sem, vmem_mb=48):
    return pltpu.CompilerParams(dimension_semantics=sem, vmem_limit_bytes=vmem_mb << 20)


def _dot(a, b, dims=None):
    if dims is None:
        return jnp.dot(a, b, preferred_element_type=f32)
    return lax.dot_general(a, b, dims, preferred_element_type=f32)


def _ln_stats(r):
    mu = jnp.mean(r, axis=-1, keepdims=True)
    xc = r - mu
    var = jnp.mean(xc * xc, axis=-1, keepdims=True)
    rstd = lax.rsqrt(var + EPS)
    return xc * rstd, rstd


def _ln_bwd(dy, xhat, rstd, g):
    dxh = dy * g
    return rstd * (dxh - jnp.mean(dxh, axis=-1, keepdims=True) - xhat * jnp.mean(dxh * xhat, axis=-1, keepdims=True))


def _gelu(x):
    return 0.5 * x * (1.0 + lax.erf(x * (1.0 / math.sqrt(2.0))))


def _gelu_grad(x):
    return 0.5 * (1.0 + lax.erf(x * (1.0 / math.sqrt(2.0)))) + x * jnp.exp(-0.5 * x * x) * (1.0 / math.sqrt(2.0 * math.pi))


def _sigmoid(x):
    return 1.0 / (1.0 + jnp.exp(-x))


def _acc_rows(o_ref, first, val):
    @pl.when(first)
    def _():
        o_ref[...] = jnp.zeros_like(o_ref)
    o_ref[...] += jnp.sum(val, axis=0, keepdims=True)


def mm_in(x, w):
    T = x.shape[0]
    tm, tn = min(1024, T), 1280

    def body(x_ref, w_ref, o_ref, xb):
        @pl.when(pl.program_id(1) == 0)
        def _():
            xb[...] = x_ref[...].astype(MX)
        o_ref[...] = _dot(xb[...], w_ref[...]).astype(o_ref.dtype)

    return pl.pallas_call(
        body, name="mm_in", grid=(T // tm, NIN // tn),
        in_specs=[pl.BlockSpec((tm, D), lambda i, j: (i, 0)), pl.BlockSpec((D, tn), lambda i, j: (0, j))],
        out_specs=pl.BlockSpec((tm, tn), lambda i, j: (i, j)),
        out_shape=jax.ShapeDtypeStruct((T, NIN), ACT),
        scratch_shapes=[pltpu.VMEM((tm, D), MX)],
        compiler_params=_cp(("parallel", "arbitrary")),
    )(x, w)


HALO = 16
TM_AC = 256


def _uv_specs():
    return [pl.BlockSpec((TM_AC, 512), functools.partial(lambda i, j: (i, j), j=C_UV // 512 + j)) for j in range(4)]


def _gmlp_fwd(up, vp, ws_ref, bs_ref, lg, lb):
    u = _gelu(up)
    xhat, rstd = _ln_stats(_gelu(vp))
    vn = xhat * lg + lb
    vnb = vn.astype(MX)
    rows = []
    for c in range(up.shape[0] // BLK):
        r = slice(c * BLK, (c + 1) * BLK)
        rows.append(jnp.concatenate(
            [_dot(ws_ref[g], vnb[r, g * BLK:(g + 1) * BLK]) + bs_ref[g] for g in range(8)], axis=1))
    return u, vn, xhat, rstd, jnp.concatenate(rows, axis=0)


def mix_ac_fwd(proj, conv_w, wst, bsx, lg, lb):
    T = proj.shape[0]
    tm = TM_AC

    def body(bch, halo, u0, u1, v0, v1, cw, ws, bs, lg_ref, lb_ref, ya, yc, zs):
        i = pl.program_id(0)
        pb = bch[...].astype(f32)
        z = pb[:, D:2 * D] * pb[:, 2 * D:]
        hz = halo[:, :D].astype(f32) * halo[:, D:].astype(f32)
        zs[0:HALO, :] = jnp.where(i > 0, hz, 0.0)
        zs[HALO:HALO + tm, :] = z
        cv = cw[0:1, :] * zs[HALO - 2:HALO - 2 + tm, :] + cw[1:2, :] * zs[HALO - 1:HALO - 1 + tm, :] + cw[2:3, :] * z
        ya[...] = (pb[:, :D] * cv).astype(ya.dtype)
        up = jnp.concatenate([u0[...], u1[...]], axis=1).astype(f32)
        vp = jnp.concatenate([v0[...], v1[...]], axis=1).astype(f32)
        u, _, _, _, sp = _gmlp_fwd(up, vp, ws, bs, lg_ref[...], lb_ref[...])
        yc[...] = (u * sp).astype(yc.dtype)

    full = lambda shape: pl.BlockSpec(shape, lambda i: (0,) * len(shape))
    return pl.pallas_call(
        body, name="mix_ac_fwd", grid=(T // tm,),
        in_specs=[pl.BlockSpec((tm, 3 * D), lambda i: (i, 1)),
                  pl.BlockSpec((HALO, 2 * D), lambda i: (jnp.maximum(i * (tm // HALO) - 1, 0), 2)),
                  *_uv_specs(), full((3, D)), full((8, BLK, BLK)), full((8, BLK, BLK)), full((1, D)), full((1, D))],
        out_specs=[pl.BlockSpec((tm, D), lambda i: (i, 0))] * 2,
        out_shape=[jax.ShapeDtypeStruct((T, D), MX)] * 2,
        scratch_shapes=[pltpu.VMEM((HALO + tm, D), f32)],
        compiler_params=_cp(("parallel",)),
    )(proj, proj, proj, proj, proj, proj, conv_w, wst, bsx, lg, lb)


def _swap_halves(x):
    lane = lax.broadcasted_iota(jnp.int32, x.shape, 1)
    return jnp.where((lane % HD) < HD // 2, pltpu.roll(x, x.shape[1] - HD // 2, 1), pltpu.roll(x, HD // 2, 1))


def _tile4(t):
    return jnp.concatenate([t] * (AO // LANES), axis=1)


TM_FOLD = 512


def _fold_out(nat, x, out_ref, d):
    if d == 1:
        out_ref[0] = x.astype(out_ref.dtype)
        return
    rows = x.shape[0] // d
    for j in range(AO // LANES):
        nat[j] = x[:, j * LANES:(j + 1) * LANES]
    for r in range(d):
        out_ref[r] = jnp.concatenate(
            [nat.at[j][pl.ds(r, rows, stride=d), :] for j in range(AO // LANES)], axis=1).astype(out_ref.dtype)


def _unfold_in(nat, in_ref, d):
    if d == 1:
        return in_ref[0].astype(f32)
    rows = in_ref.shape[1]
    for r in range(d):
        v = in_ref[r].astype(f32)
        for j in range(AO // LANES):
            nat.at[j][pl.ds(r, rows, stride=d), :] = v[:, j * LANES:(j + 1) * LANES]
    return jnp.concatenate([nat[j] for j in range(AO // LANES)], axis=1)


def fold_rope(proj, cos_t, sin_t, g, d):
    T = proj.shape[0]
    tm = TM_FOLD
    rows = tm // d

    def body(x_ref, c_ref, s_ref, q_o, k_o, v_o, nat):
        cos, sin = _tile4(c_ref[...]), _tile4(s_ref[...])
        for part, out, scale in ((0, q_o, HD ** -0.5), (1, k_o, 1.0), (2, v_o, None)):
            x = x_ref[:, part * AO:(part + 1) * AO].astype(f32)
            if scale is not None:
                x = (x * cos + _swap_halves(x) * sin) * scale
            _fold_out(nat, x, out, d)

    fold_spec = pl.BlockSpec((d, rows, AO), lambda i: (0, i, 0))
    return pl.pallas_call(
        body, name=f"fold_rope{g}", grid=(T // tm,),
        in_specs=[pl.BlockSpec((tm, 3 * AO), lambda i: (i, C_QKV // (3 * AO) + g)),
                  pl.BlockSpec((tm, LANES), lambda i: (i, 0)), pl.BlockSpec((tm, LANES), lambda i: (i, 0))],
        out_specs=[fold_spec] * 3,
        out_shape=[jax.ShapeDtypeStruct((d, T // d, AO), MX)] * 3,
        scratch_shapes=[pltpu.VMEM((AO // LANES, tm, LANES), f32)],
        compiler_params=_cp(("parallel",)),
    )(proj, cos_t, sin_t)


def _stack_heads(x):
    lane = lax.broadcasted_iota(jnp.int32, x.shape, 1)
    z = jnp.zeros_like(x)
    return jnp.concatenate([jnp.where(lane < HD, x, z), jnp.where(lane >= HD, x, z)], axis=0)


def _unstack_heads(y):
    lane = lax.broadcasted_iota(jnp.int32, (BLK, LANES), 1)
    return jnp.where(lane < HD, y[:BLK], y[BLK:])


def _window_masks():
    row = lax.broadcasted_iota(jnp.int32, (2 * BLK, 2 * BLK), 0) % BLK
    col = lax.broadcasted_iota(jnp.int32, (2 * BLK, 2 * BLK), 1)
    return (col < BLK) & (col >= row), (col >= BLK) & (col - BLK <= row)


def _two_blocks(ref, b):
    r0 = pl.multiple_of(b * BLK, BLK)
    rp = pl.multiple_of(jnp.maximum(b - 1, 0) * BLK, BLK)
    return jnp.concatenate([ref[pl.ds(rp, BLK), :], ref[pl.ds(r0, BLK), :]], axis=0)


def attn_fwd(qf, kf, vf, g, nb):
    T = qf.shape[0]

    def body(q_ref, k_ref, v_ref, o_ref, l_ref):
        prev_m, cur_m = _window_masks()

        def step(b, carry):
            r0 = pl.multiple_of(b * BLK, BLK)
            qs = _stack_heads(q_ref[pl.ds(r0, BLK), :])
            s = _dot(qs, _two_blocks(k_ref, b), NT)
            s = jnp.where(cur_m | (prev_m & ((b % nb) != 0)), s, NEG)
            m = jnp.max(s, axis=-1, keepdims=True)
            p = jnp.exp(s - m)
            l = jnp.sum(p, axis=-1, keepdims=True)
            o = _dot(p.astype(MX), _two_blocks(v_ref, b)) / l
            o_ref[pl.ds(r0, BLK), :] = _unstack_heads(o)
            l_ref[pl.ds(r0, BLK), :] = _unstack_heads(jnp.broadcast_to(m + jnp.log(l), (2 * BLK, LANES)))
            return carry

        lax.fori_loop(0, T // BLK, step, 0, unroll=4)

    spec = pl.BlockSpec((T, LANES), lambda j: (0, j))
    return pl.pallas_call(
        body, name=f"attn_fwd{g}", grid=(AO // LANES,),
        in_specs=[spec] * 3, out_specs=[spec] * 2,
        out_shape=[jax.ShapeDtypeStruct((T, AO), f32)] * 2,
        compiler_params=_cp(("parallel",), 56),
    )(qf, kf, vf)


def _group_weights(lses):
    m = jnp.maximum(jnp.maximum(lses[0], lses[1]), lses[2])
    e = [jnp.exp(l - m) for l in lses]
    inv = 1.0 / (e[0] + e[1] + e[2])
    return [x * inv for x in e]


def _fold_specs(T, tm):
    specs = []
    for _, d in GROUPS:
        specs.append(pl.BlockSpec((d, tm // d, AO), lambda i: (0, i, 0)))
    return specs


def combine_fwd(os_, lses):
    T = os_[0].shape[0] * os_[0].shape[1]
    tm = TM_FOLD

    def body(o0, o1, o2, l0, l1, l2, y_ref, nat):
        o = [_unfold_in(nat, r, d) for r, (_, d) in zip((o0, o1, o2), GROUPS)]
        ls = [_unfold_in(nat, r, d) for r, (_, d) in zip((l0, l1, l2), GROUPS)]
        w = _group_weights(ls)
        y_ref[...] = (w[0] * o[0] + w[1] * o[1] + w[2] * o[2]).astype(y_ref.dtype)

    specs = _fold_specs(T, tm)
    return pl.pallas_call(
        body, name="combine_fwd", grid=(T // tm,),
        in_specs=specs + specs, out_specs=pl.BlockSpec((tm, AO), lambda i: (i, 0)),
        out_shape=jax.ShapeDtypeStruct((T, AO), MX),
        scratch_shapes=[pltpu.VMEM((AO // LANES, tm, LANES), f32)],
        compiler_params=_cp(("parallel",)),
    )(*os_, *lses)


TM_MIX = 256


def mix_out_fwd(proj, ya, yb, yc, x0, pa, pb, pc, wo, g1, b1):
    T = x0.shape[0]
    tm = min(TM_MIX, T)

    def body(gt, ya_r, yb_r, yc_r, x0_r, pa_r, pb_r, pc_r, wo_r, g_r, b_r, mabc, m_o, r1_o, x1_o):
        ma = _dot(ya_r[...], pa_r[...])
        ybv = yb_r[...]
        mb = jnp.concatenate([_dot(ybv, pb_r[k]) for k in range(NCHIP)], axis=1)
        mc = _dot(yc_r[...], pc_r[...])
        m = jnp.zeros((tm, D), f32)
        for j, mm in enumerate((ma, mb, mc)):
            mabc[:, j * D:(j + 1) * D] = mm.astype(mabc.dtype)
            m = m + _sigmoid(gt[:, j * D:(j + 1) * D].astype(f32)) * mm
        mb16 = m.astype(MX)
        m_o[...] = mb16
        r1 = ALPHA * x0_r[...] + _dot(mb16, wo_r[...])
        r1_o[...] = r1
        xhat, _ = _ln_stats(r1)
        x1_o[...] = xhat * g_r[...] + b_r[...]

    full = lambda shape: pl.BlockSpec(shape, lambda i: (0,) * len(shape))
    tile = lambda w: pl.BlockSpec((tm, w), lambda i: (i, 0))
    return pl.pallas_call(
        body, name="mix_out_fwd", grid=(T // tm,),
        in_specs=[tile(3 * D), tile(D), tile(AO), tile(D), tile(D), full((D, D)), full((NCHIP, AO, D // NCHIP)),
                  full((D, D)), full((D, D)), full((1, D)), full((1, D))],
        out_specs=[tile(3 * D), tile(D), tile(D), tile(D)],
        out_shape=[jax.ShapeDtypeStruct((T, 3 * D), MX), jax.ShapeDtypeStruct((T, D), MX),
                   jax.ShapeDtypeStruct((T, D), f32), jax.ShapeDtypeStruct((T, D), f32)],
        compiler_params=_cp(("parallel",), 56),
    )(proj, ya, yb, yc, x0, pa, pb, pc, wo, g1, b1)


TM_FF = 512


def ffn_up_fwd(x1, wg, wu):
    T = x1.shape[0]
    tm = min(TM_FF, T)

    def body(x_r, wg_r, wu_r, g_o, u_o, h_o, xb):
        @pl.when(pl.program_id(1) == 0)
        def _():
            xb[...] = x_r[...].astype(MX)
        gate = _dot(xb[...], wg_r[0])
        up = _dot(xb[...], wu_r[0])
        g_o[0] = gate.astype(g_o.dtype)
        u_o[0] = up.astype(u_o.dtype)
        h_o[0] = (gate * _sigmoid(gate) * up).astype(h_o.dtype)

    wspec = pl.BlockSpec((1, D, FB), lambda i, k: (k, 0, 0))
    ospec = pl.BlockSpec((1, tm, FB), lambda i, k: (k, i, 0))
    return pl.pallas_call(
        body, name="ffn_up_fwd", grid=(T // tm, NCHIP),
        in_specs=[pl.BlockSpec((tm, D), lambda i, k: (i, 0)), wspec, wspec],
        out_specs=[ospec] * 3,
        out_shape=[jax.ShapeDtypeStruct((NCHIP, T, FB), ACT)] * 2 + [jax.ShapeDtypeStruct((NCHIP, T, FB), MX)],
        scratch_shapes=[pltpu.VMEM((tm, D), MX)],
        compiler_params=_cp(("parallel", "arbitrary")),
    )(x1, wg, wu)


def ffn_down_fwd(hh, wd, x1, g2, b2):
    T = x1.shape[0]
    tm = min(TM_FF, T)

    def body(h_r, w_r, x_r, g_r, b_r, r2_o, x2_o):
        r2 = ALPHA * x_r[...]
        for k in range(NCHIP):
            r2 = r2 + _dot(h_r[k], w_r[k])
        r2_o[...] = r2
        xhat, _ = _ln_stats(r2)
        x2_o[...] = xhat * g_r[...] + b_r[...]

    tile = pl.BlockSpec((tm, D), lambda i: (i, 0))
    vec = pl.BlockSpec((1, D), lambda i: (0, 0))
    return pl.pallas_call(
        body, name="ffn_down_fwd", grid=(T // tm,),
        in_specs=[pl.BlockSpec((NCHIP, tm, FB), lambda i: (0, i, 0)), pl.BlockSpec((NCHIP, FB, D), lambda i: (0, 0, 0)),
                  tile, vec, vec],
        out_specs=[tile, tile], out_shape=[jax.ShapeDtypeStruct((T, D), f32)] * 2,
        compiler_params=_cp(("parallel",)),
    )(hh, wd, x1, g2, b2)


def loss_grad(y, tgt):
    T = y.shape[0]
    tm = min(512, T)

    def body(y_r, t_r, l_o, dy_o):
        e = y_r[...] - t_r[...]
        dy_o[...] = e * (1.0 / D)

        @pl.when(pl.program_id(0) == 0)
        def _():
            l_o[...] = jnp.zeros_like(l_o)
        l_o[...] += (0.5 / D) * jnp.sum(e * e)

    tile = pl.BlockSpec((tm, D), lambda i: (i, 0))
    return pl.pallas_call(
        body, name="loss_grad", grid=(T // tm,),
        in_specs=[tile, tile], out_specs=[pl.BlockSpec((8, LANES), lambda i: (0, 0)), tile],
        out_shape=[jax.ShapeDtypeStruct((8, LANES), f32), jax.ShapeDtypeStruct((T, D), f32)],
        compiler_params=_cp(("arbitrary",)),
    )(y, tgt)


def ffn_down_bwd(dx2, r2, g2, wd, gate, up):
    T = dx2.shape[0]
    tm = min(TM_FF, T)

    def body(dx_r, r_r, g_r, w_r, ga_r, up_r, dr_o, dg_o, du_o, dlg_o, dlb_o, drb):
        i, k = pl.program_id(0), pl.program_id(1)

        @pl.when(k == 0)
        def _():
            xhat, rstd = _ln_stats(r_r[...])
            dx = dx_r[...]
            _acc_rows(dlg_o, i == 0, dx * xhat)
            _acc_rows(dlb_o, i == 0, dx)
            dr = _ln_bwd(dx, xhat, rstd, g_r[...])
            dr_o[...] = dr
            drb[...] = dr.astype(MX)

        dhh = _dot(drb[...], w_r[0], NT)
        gate_v, up_v = ga_r[0].astype(f32), up_r[0].astype(f32)
        sg = _sigmoid(gate_v)
        dg_o[0] = (dhh * up_v * sg * (1.0 + gate_v * (1.0 - sg))).astype(dg_o.dtype)
        du_o[0] = (dhh * gate_v * sg).astype(du_o.dtype)

    tile = pl.BlockSpec((tm, D), lambda i, k: (i, 0))
    vec = pl.BlockSpec((1, D), lambda i, k: (0, 0))
    blk = pl.BlockSpec((1, tm, FB), lambda i, k: (k, i, 0))
    return pl.pallas_call(
        body, name="ffn_down_bwd", grid=(T // tm, NCHIP),
        in_specs=[tile, tile, vec, pl.BlockSpec((1, FB, D), lambda i, k: (k, 0, 0)), blk, blk],
        out_specs=[tile, blk, blk, vec, vec],
        out_shape=[jax.ShapeDtypeStruct((T, D), f32)] + [jax.ShapeDtypeStruct((NCHIP, T, FB), MX)] * 2
        + [jax.ShapeDtypeStruct((1, D), f32)] * 2,
        scratch_shapes=[pltpu.VMEM((tm, D), MX)],
        compiler_params=_cp(("arbitrary", "arbitrary")),
    )(dx2, r2, g2, wd, gate, up)


def ffn_up_bwd(dr2, dgate, dup, wg, wu, r1, g1):
    T = dr2.shape[0]
    tm = min(TM_FF, T)

    def body(dr2_r, dg_r, du_r, wg_r, wu_r, r1_r, g_r, dr1_o, dlg_o, dlb_o, acc):
        i, k = pl.program_id(0), pl.program_id(1)

        @pl.when(k == 0)
        def _():
            acc[...] = ALPHA * dr2_r[...]
        acc[...] += _dot(dg_r[0], wg_r[0], NT) + _dot(du_r[0], wu_r[0], NT)

        @pl.when(k == NCHIP - 1)
        def _():
            dx = acc[...]
            xhat, rstd = _ln_stats(r1_r[...])
            _acc_rows(dlg_o, i == 0, dx * xhat)
            _acc_rows(dlb_o, i == 0, dx)
            dr1_o[...] = _ln_bwd(dx, xhat, rstd, g_r[...])

    tile = pl.BlockSpec((tm, D), lambda i, k: (i, 0))
    vec = pl.BlockSpec((1, D), lambda i, k: (0, 0))
    blk = pl.BlockSpec((1, tm, FB), lambda i, k: (k, i, 0))
    wspec = pl.BlockSpec((1, D, FB), lambda i, k: (k, 0, 0))
    return pl.pallas_call(
        body, name="ffn_up_bwd", grid=(T // tm, NCHIP),
        in_specs=[tile, blk, blk, wspec, wspec, tile, vec],
        out_specs=[tile, vec, vec],
        out_shape=[jax.ShapeDtypeStruct((T, D), f32)] + [jax.ShapeDtypeStruct((1, D), f32)] * 2,
        scratch_shapes=[pltpu.VMEM((tm, D), f32)],
        compiler_params=_cp(("arbitrary", "arbitrary")),
    )(dr2, dgate, dup, wg, wu, r1, g1)


def mix_out_bwd(dr1, proj, mabc, wo, pa, pb, pc):
    T = dr1.shape[0]
    tm = min(TM_MIX, T)

    def body(dr_r, gt, mabc_r, wo_r, pa_r, pb_r, pc_r, dmabc_o, dgt_o, dya_o, dyb_o, dyc_o):
        dm = _dot(dr_r[...].astype(MX), wo_r[...], NT)
        dmx = []
        for j in range(3):
            s = _sigmoid(gt[:, j * D:(j + 1) * D].astype(f32))
            v = (dm * s).astype(MX)
            dmx.append(v)
            dmabc_o[:, j * D:(j + 1) * D] = v
            dgt_o[:, j * D:(j + 1) * D] = (dm * mabc_r[:, j * D:(j + 1) * D].astype(f32) * s * (1.0 - s)).astype(dgt_o.dtype)
        dya_o[...] = _dot(dmx[0], pa_r[...], NT)
        dyb = jnp.zeros((tm, AO), f32)
        for k in range(NCHIP):
            dyb = dyb + _dot(dmx[1][:, k * (D // NCHIP):(k + 1) * (D // NCHIP)], pb_r[k], NT)
        dyb_o[...] = dyb
        dyc_o[...] = _dot(dmx[2], pc_r[...], NT)

    full = lambda shape: pl.BlockSpec(shape, lambda i: (0,) * len(shape))
    tile = lambda w: pl.BlockSpec((tm, w), lambda i: (i, 0))
    return pl.pallas_call(
        body, name="mix_out_bwd", grid=(T // tm,),
        in_specs=[tile(D), tile(3 * D), tile(3 * D), full((D, D)), full((D, D)), full((NCHIP, AO, D // NCHIP)), full((D, D))],
        out_specs=[tile(3 * D), tile(3 * D), tile(D), tile(AO), tile(D)],
        out_shape=[jax.ShapeDtypeStruct((T, 3 * D), MX), jax.ShapeDtypeStruct((T, 3 * D), MX),
                   jax.ShapeDtypeStruct((T, D), f32), jax.ShapeDtypeStruct((T, AO), f32), jax.ShapeDtypeStruct((T, D), f32)],
        compiler_params=_cp(("parallel",), 56),
    )(dr1, proj, mabc, wo, pa, pb, pc)


def transpose_cast(x):
    T = x.shape[0]
    tm = min(512, T)

    def body(x_r, o_r):
        o_r[...] = x_r[...].T.astype(o_r.dtype)

    return pl.pallas_call(
        body, name="transpose_cast", grid=(T // tm,),
        in_specs=[pl.BlockSpec((tm, D), lambda i: (i, 0))], out_specs=pl.BlockSpec((D, tm), lambda i: (0, i)),
        out_shape=jax.ShapeDtypeStruct((D, T), MX), compiler_params=_cp(("parallel",)),
    )(x)


def tn_matmul(name, a, b, a_spec, b_spec, out_shape, out_spec, grid, a_is_t=False):
    nt = len(grid) - 1

    def body(a_r, b_r, o_r):
        @pl.when(pl.program_id(nt) == 0)
        def _():
            o_r[...] = jnp.zeros_like(o_r)
        av = a_r[...].reshape(a_r.shape[-2:]).astype(MX)
        bv = b_r[...].reshape(b_r.shape[-2:]).astype(MX)
        o_r[...] += _dot(av, bv, None if a_is_t else TN).reshape(o_r.shape)

    return pl.pallas_call(
        body, name=name, grid=grid, in_specs=[a_spec, b_spec], out_specs=out_spec,
        out_shape=jax.ShapeDtypeStruct(out_shape, f32),
        compiler_params=_cp(("parallel",) * nt + ("arbitrary",), 56),
    )(a, b)


def attn_pre_bwd(dyb, os_, lses):
    T = dyb.shape[0]
    tm = TM_FOLD

    def body(dy_r, o0, o1, o2, l0, l1, l2, ones_r, d0, d1, d2, f0, f1, f2, nat):
        o = [_unfold_in(nat, r, d) for r, (_, d) in zip((o0, o1, o2), GROUPS)]
        ls = [_unfold_in(nat, r, d) for r, (_, d) in zip((l0, l1, l2), GROUPS)]
        w = _group_weights(ls)
        dy = dy_r[...]
        t = dy * (w[0] * o[0] + w[1] * o[1] + w[2] * o[2])
        hi = t.astype(MX)
        lo = (t - hi.astype(f32)).astype(MX)
        c = _dot(hi, ones_r[...]) + _dot(lo, ones_r[...])
        for wg, do_o, df_o, (_, d) in zip(w, (d0, d1, d2), (f0, f1, f2), GROUPS):
            _fold_out(nat, wg * dy, do_o, d)
            _fold_out(nat, -wg * c, df_o, d)

    specs = _fold_specs(T, tm)
    return pl.pallas_call(
        body, name="attn_pre_bwd", grid=(T // tm,),
        in_specs=[pl.BlockSpec((tm, AO), lambda i: (i, 0))] + specs + specs + [pl.BlockSpec((AO, AO), lambda i: (0, 0))],
        out_specs=specs + specs,
        out_shape=[jax.ShapeDtypeStruct((d, T // d, AO), MX) for _, d in GROUPS]
        + [jax.ShapeDtypeStruct((d, T // d, AO), f32) for _, d in GROUPS],
        scratch_shapes=[pltpu.VMEM((AO // LANES, tm, LANES), f32)],
        compiler_params=_cp(("parallel",)),
    )(dyb, *os_, *lses, _head_ones())


def _head_ones():
    i = jnp.arange(AO) // HD
    return (i[:, None] == i[None, :]).astype(MX)


def attn_bwd(qf, kf, vf, dof, lse, df, g, nb):
    T = qf.shape[0]

    def body(q_ref, k_ref, v_ref, do_ref, l_ref, d_ref, dq_ref, dk_ref, dv_ref):
        prev_m, cur_m = _window_masks()

        def head_col(ref, r0):
            v = ref[pl.ds(r0, BLK), :]
            return jnp.concatenate([v[:, 0:1], v[:, HD:HD + 1]], axis=0)

        def step(b, carry):
            dk_c, dv_c = carry
            r0 = pl.multiple_of(b * BLK, BLK)
            rp = pl.multiple_of(jnp.maximum(b - 1, 0) * BLK, BLK)
            qs, dos = _stack_heads(q_ref[pl.ds(r0, BLK), :]), _stack_heads(do_ref[pl.ds(r0, BLK), :])
            k2, v2 = _two_blocks(k_ref, b), _two_blocks(v_ref, b)
            valid = cur_m | (prev_m & ((b % nb) != 0))
            p = jnp.where(valid, jnp.exp(_dot(qs, k2, NT) - head_col(l_ref, r0)), 0.0)
            ds = (p * (_dot(dos, v2, NT) + head_col(d_ref, r0))).astype(MX)
            dq_ref[pl.ds(r0, BLK), :] = _unstack_heads(_dot(ds, k2)).astype(dq_ref.dtype)
            dk2 = _dot(ds, qs, TN)
            dv2 = _dot(p.astype(MX), dos, TN)
            dk_ref[pl.ds(rp, BLK), :] = (dk_c + dk2[:BLK]).astype(dk_ref.dtype)
            dv_ref[pl.ds(rp, BLK), :] = (dv_c + dv2[:BLK]).astype(dv_ref.dtype)
            return dk2[BLK:], dv2[BLK:]

        zero = jnp.zeros((BLK, LANES), f32)
        dk_c, dv_c = lax.fori_loop(0, T // BLK, step, (zero, zero), unroll=2)
        dk_ref[pl.ds(T - BLK, BLK), :] = dk_c.astype(dk_ref.dtype)
        dv_ref[pl.ds(T - BLK, BLK), :] = dv_c.astype(dv_ref.dtype)

    spec = pl.BlockSpec((T, LANES), lambda j: (0, j))
    return pl.pallas_call(
        body, name=f"attn_bwd{g}", grid=(AO // LANES,),
        in_specs=[spec] * 6, out_specs=[spec] * 3,
        out_shape=[jax.ShapeDtypeStruct((T, AO), MX)] * 3,
        compiler_params=_cp(("parallel",), 60),
    )(qf, kf, vf, dof, lse, df)


def unfold_rope_bwd(dqf, dkf, dvf, cos_t, sin_t, g, d):
    T = dqf.shape[0] * dqf.shape[1]
    tm = TM_FOLD

    def body(q_r, k_r, v_r, c_ref, s_ref, o_ref, nat):
        cos, sin = _tile4(c_ref[...]), _tile4(s_ref[...])
        for part, ref, scale in ((0, q_r, HD ** -0.5), (1, k_r, 1.0), (2, v_r, None)):
            x = _unfold_in(nat, ref, d)
            if scale is not None:
                x = (x * cos - _swap_halves(x) * sin) * scale
            o_ref[:, part * AO:(part + 1) * AO] = x.astype(o_ref.dtype)

    fold_spec = pl.BlockSpec((d, tm // d, AO), lambda i: (0, i, 0))
    tab = pl.BlockSpec((tm, LANES), lambda i: (i, 0))
    return pl.pallas_call(
        body, name=f"unfold_rope_bwd{g}", grid=(T // tm,),
        in_specs=[fold_spec] * 3 + [tab, tab],
        out_specs=pl.BlockSpec((tm, 3 * AO), lambda i: (i, 0)),
        out_shape=jax.ShapeDtypeStruct((T, 3 * AO), MX),
        scratch_shapes=[pltpu.VMEM((AO // LANES, tm, LANES), f32)],
        compiler_params=_cp(("parallel",)),
    )(dqf, dkf, dvf, cos_t, sin_t)


def conv_bwd(dya, proj, conv_w):
    T = dya.shape[0]
    tm = TM_AC
    last = T // tm - 1

    def body(dy_r, bch, hprev, dy_next, b_next, cw, d_o, dw_o, zs, ds):
        i = pl.program_id(0)
        pb = bch[...].astype(f32)
        bp, cp, hp = pb[:, :D], pb[:, D:2 * D], pb[:, 2 * D:]
        z = cp * hp
        hz = hprev[:, :D].astype(f32) * hprev[:, D:].astype(f32)
        zs[0:HALO, :] = jnp.where(i > 0, hz, 0.0)
        zs[HALO:HALO + tm, :] = z
        z2, z1 = zs[HALO - 2:HALO - 2 + tm, :], zs[HALO - 1:HALO - 1 + tm, :]
        cv = cw[0:1, :] * z2 + cw[1:2, :] * z1 + cw[2:3, :] * z
        dy = dy_r[...]
        dcv = dy * bp
        ds[0:tm, :] = dcv
        ds[tm:tm + HALO, :] = jnp.where(i < last, dy_next[...] * b_next[...].astype(f32), 0.0)
        dz = cw[2:3, :] * dcv + cw[1:2, :] * ds[1:1 + tm, :] + cw[0:1, :] * ds[2:2 + tm, :]
        d_o[:, :D] = (dy * cv).astype(d_o.dtype)
        d_o[:, D:2 * D] = (dz * hp).astype(d_o.dtype)
        d_o[:, 2 * D:] = (dz * cp).astype(d_o.dtype)

        @pl.when(i == 0)
        def _():
            dw_o[...] = jnp.zeros_like(dw_o)
        dw_o[0:1, :] += jnp.sum(dcv * z2, axis=0, keepdims=True)
        dw_o[1:2, :] += jnp.sum(dcv * z1, axis=0, keepdims=True)
        dw_o[2:3, :] += jnp.sum(dcv * z, axis=0, keepdims=True)

    nh = tm // HALO
    return pl.pallas_call(
        body, name="conv_bwd", grid=(T // tm,),
        in_specs=[pl.BlockSpec((tm, D), lambda i: (i, 0)), pl.BlockSpec((tm, 3 * D), lambda i: (i, 1)),
                  pl.BlockSpec((HALO, 2 * D), lambda i: (jnp.maximum(i * nh - 1, 0), 2)),
                  pl.BlockSpec((HALO, D), lambda i: (jnp.minimum((i + 1) * nh, T // HALO - 1), 0)),
                  pl.BlockSpec((HALO, D), lambda i: (jnp.minimum((i + 1) * nh, T // HALO - 1), 3)),
                  pl.BlockSpec((3, D), lambda i: (0, 0))],
        out_specs=[pl.BlockSpec((tm, 3 * D), lambda i: (i, 0)), pl.BlockSpec((3, D), lambda i: (0, 0))],
        out_shape=[jax.ShapeDtypeStruct((T, 3 * D), MX), jax.ShapeDtypeStruct((3, D), f32)],
        scratch_shapes=[pltpu.VMEM((HALO + tm, D), f32), pltpu.VMEM((tm + HALO, D), f32)],
        compiler_params=_cp(("arbitrary",)),
    )(dya, proj, proj, dya, proj, conv_w)


def gmlp_bwd(dyc, proj, wst, bsx, lg, lb):
    T = dyc.shape[0]
    tm = TM_AC
    last = T // tm - 1

    def body(dy_r, u0, u1, v0, v1, ws, bs, lg_r, lb_r, d_o, dws_o, dbs_o, dlg_o, dlb_o, bacc):
        i = pl.program_id(0)
        up = jnp.concatenate([u0[...], u1[...]], axis=1).astype(f32)
        vp = jnp.concatenate([v0[...], v1[...]], axis=1).astype(f32)
        u, vn, xhat, rstd, sp = _gmlp_fwd(up, vp, ws, bs, lg_r[...], lb_r[...])
        dy = dy_r[...]
        d_o[:, :D] = (dy * sp * _gelu_grad(up)).astype(d_o.dtype)
        dsp = dy * u
        dspb, vnb = dsp.astype(MX), vn.astype(MX)

        @pl.when(i == 0)
        def _():
            dws_o[...] = jnp.zeros_like(dws_o)
            bacc[...] = jnp.zeros_like(bacc)

        rows = []
        for c in range(tm // BLK):
            r = slice(c * BLK, (c + 1) * BLK)
            cols = []
            for g in range(8):
                cs = slice(g * BLK, (g + 1) * BLK)
                dws_o[g] += _dot(dspb[r, cs], vnb[r, cs], NT)
                bacc[g] += dsp[r, cs]
                cols.append(_dot(ws[g], dspb[r, cs], TN))
            rows.append(jnp.concatenate(cols, axis=1))
        dvn = jnp.concatenate(rows, axis=0)
        _acc_rows(dlg_o, i == 0, dvn * xhat)
        _acc_rows(dlb_o, i == 0, dvn)
        d_o[:, D:] = (_ln_bwd(dvn, xhat, rstd, lg_r[...]) * _gelu_grad(vp)).astype(d_o.dtype)

        @pl.when(i == last)
        def _():
            row = lax.broadcasted_iota(jnp.int32, (BLK, BLK), 0)
            col = lax.broadcasted_iota(jnp.int32, (BLK, BLK), 1)
            ones = jnp.ones((8, BLK), MX)
            for g in range(8):
                dws_o[g] = jnp.where(col <= row, dws_o[g], 0.0)
                a = bacc[g]
                hi = a.astype(MX)
                lo = (a - hi.astype(f32)).astype(MX)
                dbs_o[g:g + 1, :] = (_dot(ones, hi, NT) + _dot(ones, lo, NT))[0:1, :]

    full = lambda shape: pl.BlockSpec(shape, lambda i: (0,) * len(shape))
    return pl.pallas_call(
        body, name="gmlp_bwd", grid=(T // tm,),
        in_specs=[pl.BlockSpec((tm, D), lambda i: (i, 0)), *_uv_specs(), full((8, BLK, BLK)), full((8, BLK, BLK)),
                  full((1, D)), full((1, D))],
        out_specs=[pl.BlockSpec((tm, 2 * D), lambda i: (i, 0)), full((8, BLK, BLK)), full((8, BLK)), full((1, D)), full((1, D))],
        out_shape=[jax.ShapeDtypeStruct((T, 2 * D), MX), jax.ShapeDtypeStruct((8, BLK, BLK), f32),
                   jax.ShapeDtypeStruct((8, BLK), f32), jax.ShapeDtypeStruct((1, D), f32), jax.ShapeDtypeStruct((1, D), f32)],
        scratch_shapes=[pltpu.VMEM((8, BLK, BLK), f32)],
        compiler_params=_cp(("arbitrary",)),
    )(dyc, proj, proj, proj, proj, wst, bsx, lg, lb)


PART_TILES = (6, 6, 3, 3, 3, 4)
PART_START = (0, 6, 12, 15, 18, 21)
TJ = 512


def _part_specs(tm, rows_axis):
    specs = []
    for n, s in zip(PART_TILES, PART_START):
        def imap(*idx, n=n, s=s):
            i, j = idx[rows_axis], idx[1 - rows_axis]
            inside = (j >= s) & (j < s + n)
            return (jnp.where(inside, i, 0), jnp.clip(j - s, 0, n - 1))
        specs.append(pl.BlockSpec((tm, TJ), imap))
    return specs


def _with_part(j, refs, fn):
    for r, n, s in zip(refs, PART_TILES, PART_START):
        @pl.when((j >= s) & (j < s + n))
        def _():
            fn(r[...])


def dx_in(dr1, parts, w):
    T = dr1.shape[0]
    tm = min(1024, T)

    def body(dr_r, p0, p1, p2, p3, p4, p5, w_r, o_r):
        j = pl.program_id(1)

        @pl.when(j == 0)
        def _():
            o_r[...] = ALPHA * dr_r[...]

        def acc(tile):
            o_r[...] += _dot(tile, w_r[...], NT)
        _with_part(j, (p0, p1, p2, p3, p4, p5), acc)

    return pl.pallas_call(
        body, name="dx_in", grid=(T // tm, NIN // TJ),
        in_specs=[pl.BlockSpec((tm, D), lambda i, j: (i, 0))] + _part_specs(tm, 0) + [pl.BlockSpec((D, TJ), lambda i, j: (0, j))],
        out_specs=pl.BlockSpec((tm, D), lambda i, j: (i, 0)),
        out_shape=jax.ShapeDtypeStruct((T, D), f32),
        compiler_params=_cp(("parallel", "arbitrary"), 56),
    )(dr1, *parts, w)


def dw_in(x0t, parts):
    T = x0t.shape[1]
    tk = min(2048, T)

    def body(x_r, p0, p1, p2, p3, p4, p5, o_r):
        j, t = pl.program_id(0), pl.program_id(1)

        @pl.when(t == 0)
        def _():
            o_r[...] = jnp.zeros_like(o_r)

        def acc(tile):
            o_r[...] += _dot(x_r[...], tile)
        _with_part(j, (p0, p1, p2, p3, p4, p5), acc)

    return pl.pallas_call(
        body, name="dw_in", grid=(NIN // TJ, T // tk),
        in_specs=[pl.BlockSpec((D, tk), lambda j, t: (0, t))] + _part_specs(tk, 1),
        out_specs=pl.BlockSpec((D, TJ), lambda j, t: (0, j)),
        out_shape=jax.ShapeDtypeStruct((D, NIN), f32),
        compiler_params=_cp(("parallel", "arbitrary")),
    )(x0t, *parts)


def rope_tables(positions):
    half = HD // 2
    inv_freq = ROPE_THETA ** (-jnp.arange(half, dtype=f32) / half)
    ang = positions.astype(f32)[:, None] * inv_freq
    cos, sin = jnp.cos(ang), jnp.sin(ang)
    return jnp.tile(cos, (1, LANES // half)), jnp.tile(jnp.concatenate([-sin, sin], axis=1), (1, LANES // HD))


def _flat(a):
    return a.reshape(a.shape[0] * a.shape[1], a.shape[2])


def layer_fwd(x0, W, cos_t, sin_t):
    T = x0.shape[0]
    proj = mm_in(x0, W["w_in"])
    ya, yc = mix_ac_fwd(proj, W["conv_w"], W["wst"], W["bsx"], W["gmlp_ln_g"], W["gmlp_ln_b"])
    folded, os_, lses = [], [], []
    for g, (_, d) in enumerate(GROUPS):
        qf, kf, vf = fold_rope(proj, cos_t, sin_t, g, d)
        o, lse = attn_fwd(_flat(qf), _flat(kf), _flat(vf), g, T // d // BLK)
        folded.append((qf, kf, vf))
        os_.append(o.reshape(d, T // d, AO))
        lses.append(lse.reshape(d, T // d, AO))
    yb = combine_fwd(os_, lses)
    mabc, m, r1, x1 = mix_out_fwd(proj, ya, yb, yc, x0, W["p_a"], W["p_b"], W["p_c"], W["w_o"], W["ln1_g"], W["ln1_b"])
    gate, up, hh = ffn_up_fwd(x1, W["w_gate"], W["w_up"])
    r2, x2 = ffn_down_fwd(hh, W["w_down"], x1, W["ln2_g"], W["ln2_b"])
    saved = dict(x0=x0, proj=proj, ya=ya, yb=yb, yc=yc, folded=folded, os=os_, lses=lses, mabc=mabc, m=m, r1=r1,
                 x1=x1, gate=gate, up=up, hh=hh, r2=r2)
    return x2, saved


def layer_bwd(dx2, S, W, cos_t, sin_t):
    T = dx2.shape[0]
    tk = min(512, T)
    G = {}
    dr2, dgate, dup, G["ln2_g"], G["ln2_b"] = ffn_down_bwd(dx2, S["r2"], W["ln2_g"], W["w_down"], S["gate"], S["up"])
    blk_a = pl.BlockSpec((1, tk, FB), lambda k, t: (k, t, 0))
    row_b = pl.BlockSpec((tk, D), lambda k, t: (t, 0))
    G["w_down"] = tn_matmul("dw_down", S["hh"], dr2, blk_a, row_b, (NCHIP, FB, D),
                            pl.BlockSpec((1, FB, D), lambda k, t: (k, 0, 0)), (NCHIP, T // tk))
    x1t = transpose_cast(S["x1"])
    for nm, dv in (("w_gate", dgate), ("w_up", dup)):
        G[nm] = tn_matmul("d" + nm, x1t, dv, pl.BlockSpec((D, tk), lambda k, t: (0, t)), blk_a, (NCHIP, D, FB),
                          pl.BlockSpec((1, D, FB), lambda k, t: (k, 0, 0)), (NCHIP, T // tk), a_is_t=True)
    dr1, G["ln1_g"], G["ln1_b"] = ffn_up_bwd(dr2, dgate, dup, W["w_gate"], W["w_up"], S["r1"], W["ln1_g"])
    dmabc, dgates, dya, dyb, dyc = mix_out_bwd(dr1, S["proj"], S["mabc"], W["w_o"], W["p_a"], W["p_b"], W["p_c"])
    one = (1, T // tk)
    full_o = pl.BlockSpec((D, D), lambda k, t: (0, 0))
    G["w_o"] = tn_matmul("dw_o", S["m"], dr1, row_b, row_b, (D, D), full_o, one)
    G["p_a"] = tn_matmul("dp_a", S["ya"], dmabc, row_b, pl.BlockSpec((tk, D), lambda k, t: (t, 0)), (D, D), full_o, one)
    G["p_c"] = tn_matmul("dp_c", S["yc"], dmabc, row_b, pl.BlockSpec((tk, D), lambda k, t: (t, 2)), (D, D), full_o, one)
    G["p_b"] = tn_matmul("dp_b", S["yb"], dmabc, pl.BlockSpec((tk, AO), lambda k, t: (t, 0)),
                         pl.BlockSpec((tk, D // NCHIP), lambda k, t: (t, NCHIP + k)), (NCHIP, AO, D // NCHIP),
                         pl.BlockSpec((1, AO, D // NCHIP), lambda k, t: (k, 0, 0)), (NCHIP, T // tk))
    dbch, G["conv_w"] = conv_bwd(dya, S["proj"], W["conv_w"])
    duv, G["w_s"], G["b_s"], G["gmlp_ln_g"], G["gmlp_ln_b"] = gmlp_bwd(
        dyc, S["proj"], W["wst"], W["bsx"], W["gmlp_ln_g"], W["gmlp_ln_b"])
    pre = attn_pre_bwd(dyb, S["os"], S["lses"])
    dqkv = []
    for g, (_, d) in enumerate(GROUPS):
        qf, kf, vf = S["folded"][g]
        dqf, dkf, dvf = attn_bwd(_flat(qf), _flat(kf), _flat(vf), _flat(pre[g]), _flat(S["lses"][g]), _flat(pre[3 + g]),
                                 g, T // d // BLK)
        shp = (d, T // d, AO)
        dqkv.append(unfold_rope_bwd(dqf.reshape(shp), dkf.reshape(shp), dvf.reshape(shp), cos_t, sin_t, g, d))
    parts = (dgates, dbch, *dqkv, duv)
    G["w_in"] = dw_in(transpose_cast(S["x0"]), parts)
    dx0 = dx_in(dr1, parts, W["w_in"])
    return dx0, G


def prep_layer_weights(Wl):
    W = dict(Wl)
    tril = jnp.tril(jnp.ones((BLK, BLK), f32))
    W["wst"] = (Wl["w_s"] * tril[None]).astype(MX)
    W["bsx"] = jnp.broadcast_to(Wl["b_s"][:, :, None], (8, BLK, BLK))
    for n in ("gmlp_ln_g", "gmlp_ln_b", "ln1_g", "ln1_b", "ln2_g", "ln2_b"):
        W[n] = Wl[n].reshape(1, D)
    return W


def local_step(x, positions, target, layers):
    cos_t, sin_t = rope_tables(positions)
    Ws = [prep_layer_weights(Wl) for Wl in layers]
    saved = []
    h = x
    for W in Ws:
        h, S = layer_fwd(h, W, cos_t, sin_t)
        saved.append(S)
    lsum, dh = loss_grad(h, target)
    grads = [None] * len(Ws)
    for l in reversed(range(len(Ws))):
        dh, grads[l] = layer_bwd(dh, saved[l], Ws[l], cos_t, sin_t)
    return lsum, dh, grads


MESH = pl.DeviceIdType.MESH
ANY = pl.BlockSpec(memory_space=pl.ANY)
BIG = ("w_in", "w_gate", "w_up", "w_down", "p_a", "p_b", "p_c", "w_o")
NBIG = len(BIG)


def _place():
    x, y, c = lax.axis_index("x"), lax.axis_index("y"), lax.axis_index("c")
    return x, y, c, 2 * x + y


def _rcopy(src, dst, send, recv, dev):
    return pltpu.make_async_remote_copy(src_ref=src, dst_ref=dst, send_sem=send, recv_sem=recv, device_id=dev,
                                        device_id_type=MESH)


def _cols(ref, k, width):
    start = k * width if isinstance(k, int) else pl.multiple_of(k * width, LANES)
    return ref.at[:, pl.ds(start, width)]


CHUNK_BYTES = 1 << 20


def _pieces(shape, itemsize, nbytes=CHUNK_BYTES):
    rows, cols = shape[-2], shape[-1]
    per = max(16, nbytes // (cols * itemsize) // 16 * 16)
    out = []
    for lead in (range(shape[0]) if len(shape) == 3 else (None,)):
        for r in range(0, rows, per):
            sl = (pl.ds(r, min(per, rows - r)), slice(None))
            out.append(sl if lead is None else (lead,) + sl)
    return out


def _start_pieces(src, dst, make, nbytes=CHUNK_BYTES):
    for idx in _pieces(src.shape, jnp.dtype(src.dtype).itemsize, nbytes):
        make(src.at[idx], dst.at[idx]).start()


def gather_weights(shards):
    n = len(shards)

    def body(*refs):
        srcs, dsts = refs[:n], refs[n:2 * n]
        send, recv, own_send, own_recv = refs[2 * n:]
        x, y, c, k = _place()
        sib = (x, y, 1 - c)
        chips = [(1 - x, y), (x, 1 - y), (1 - x, 1 - y)]

        def slot(a, layer, pos):
            if a == 0:
                return _cols(dsts[0].at[layer], pos, WIN_SHARD)
            return dsts[a].at[layer, pos]

        def ici(a, j, src, dst):
            return _rcopy(src, dst, send.at[a, j], recv.at[a, j], (*chips[j], c))

        def d2d(a, j, src, dst):
            return _rcopy(src, dst, send.at[a, 3 + j], recv.at[a, 3 + j], sib)

        def own(a, layer, src, dst):
            return _rcopy(src, dst, own_send.at[a, layer], own_recv.at[a, layer], sib)

        for a in range(n):
            for j in range(3):
                _start_pieces(srcs[a].at[c], slot(a, c, k), functools.partial(ici, a, j))
        for a in range(n):
            for layer in range(DEPTH):
                _start_pieces(srcs[a].at[layer], slot(a, layer, k), functools.partial(own, a, layer))
        for a in range(n):
            for j, (cx, cy) in enumerate(chips):
                landed = slot(a, c, 2 * cx + cy)
                ici(a, j, landed, landed).wait_recv()
                _start_pieces(landed, landed, functools.partial(d2d, a, j))
        for a in range(n):
            for j, (cx, cy) in enumerate(chips):
                passed = slot(a, 1 - c, 2 * cx + cy)
                d2d(a, j, passed, passed).wait_recv()
                landed = slot(a, c, 2 * cx + cy)
                d2d(a, j, landed, landed).wait_send()
                ici(a, j, srcs[a].at[c], slot(a, c, k)).wait_send()
            for layer in range(DEPTH):
                own(a, layer, srcs[a].at[layer], slot(a, layer, k)).wait()

    outs = [jax.ShapeDtypeStruct((DEPTH, D, NIN), shards[0].dtype)]
    outs += [jax.ShapeDtypeStruct((DEPTH, NCHIP) + s.shape[1:], s.dtype) for s in shards[1:]]
    return pl.pallas_call(
        body, name="gather_weights", in_specs=[ANY] * n, out_specs=[ANY] * n, out_shape=outs,
        scratch_shapes=[pltpu.SemaphoreType.DMA((n, 6)), pltpu.SemaphoreType.DMA((n, 6)),
                        pltpu.SemaphoreType.DMA((n, DEPTH)), pltpu.SemaphoreType.DMA((n, DEPTH))],
    )(*shards)


def _half(ref, h, a):
    rows = ref.shape[-2] // 2
    start = pl.multiple_of(h * rows, 16)
    if a == 0:
        return ref.at[pl.ds(start, rows), :]
    return ref.at[:, pl.ds(start, rows), :]


def rs_pair(l, grads):
    def body(*refs):
        g, theirs = refs[:NBIG], refs[NBIG:2 * NBIG]
        send, recv = refs[2 * NBIG:]
        x, y, c, _ = _place()

        def give(a, s, d):
            return _rcopy(s, d, send.at[a], recv.at[a], (x, y, 1 - c))

        for a in range(NBIG):
            _start_pieces(_half(g[a], 1 - c, a), theirs[a], functools.partial(give, a))
        for a in range(NBIG):
            give(a, _half(g[a], 1 - c, a), theirs[a]).wait()

    def hshape(s, a):
        return (s[0] // 2, s[1]) if a == 0 else (s[0], s[1] // 2, s[2])

    outs = [jax.ShapeDtypeStruct(hshape(g.shape, a), g.dtype) for a, g in enumerate(grads)]
    return pl.pallas_call(
        body, name=f"rs_pair{l}", in_specs=[ANY] * NBIG, out_specs=[ANY] * NBIG, out_shape=outs,
        scratch_shapes=[pltpu.SemaphoreType.DMA((NBIG,))] * 2,
    )(*grads)


def rs_chips(l, sums):
    def body(*refs):
        s, land = refs[:NBIG], refs[NBIG:2 * NBIG]
        send, recv = refs[2 * NBIG:]
        x, y, c, me = _place()

        def piece(a, k):
            return _cols(s[a], k, WIN_SHARD) if a == 0 else s[a].at[k]

        def give(a, k, src, dst):
            return _rcopy(src, dst, send.at[a, k], recv.at[a, me], (k // 2, k % 2, c))

        for k in range(NCHIP):
            @pl.when(me != k)
            def _():
                for a in range(NBIG):
                    _start_pieces(piece(a, k), land[a].at[me], functools.partial(give, a, k))
        for k in range(NCHIP):
            @pl.when(me != k)
            def _():
                for a in range(NBIG):
                    give(a, k, piece(a, k), land[a].at[me]).wait_send()
                    _rcopy(piece(a, k), land[a].at[k], send.at[a, k], recv.at[a, k], (k // 2, k % 2, c)).wait_recv()

    def pshape(s, a):
        return (NCHIP, s[0], WIN_SHARD) if a == 0 else s

    outs = [jax.ShapeDtypeStruct(pshape(v.shape, a), v.dtype) for a, v in enumerate(sums)]
    return pl.pallas_call(
        body, name=f"rs_chips{l}", in_specs=[ANY] * NBIG, out_specs=[ANY] * NBIG, out_shape=outs,
        scratch_shapes=[pltpu.SemaphoreType.DMA((NBIG, NCHIP)), pltpu.SemaphoreType.DMA((NBIG, NCHIP))],
    )(*sums)


def rs_join(l, halves):
    def body(*refs):
        h, other = refs[:NBIG], refs[NBIG:2 * NBIG]
        send, recv = refs[2 * NBIG:]
        x, y, c, _ = _place()

        def give(a, s, d):
            return _rcopy(s, d, send.at[a], recv.at[a], (x, y, 1 - c))

        for a in range(NBIG):
            _start_pieces(h[a], other[a], functools.partial(give, a))
        for a in range(NBIG):
            give(a, h[a], other[a]).wait()

    outs = [jax.ShapeDtypeStruct(v.shape, v.dtype) for v in halves]
    return pl.pallas_call(
        body, name=f"rs_join{l}", in_specs=[ANY] * NBIG, out_specs=[ANY] * NBIG, out_shape=outs,
        scratch_shapes=[pltpu.SemaphoreType.DMA((NBIG,))] * 2,
    )(*halves)


def _row_tile(rows, cols, itemsize=4, target=2 << 20):
    best = 8
    for t in range(8, rows + 1, 8):
        if rows % t == 0 and t * cols * itemsize <= target:
            best = t
    return best


GRAD_WIRE = jnp.bfloat16


def add_n(name, terms, out_dtype=f32):
    shape = terms[0].shape
    cols = shape[-1]
    rows = math.prod(shape[:-1])
    tr = _row_tile(rows, cols)

    def body(*refs):
        acc = refs[0][...]
        for r in refs[1:-1]:
            acc = acc + r[...]
        refs[-1][...] = acc.astype(out_dtype)

    tile = pl.BlockSpec((tr, cols), lambda i: (i, 0))
    out = pl.pallas_call(
        body, name=name, grid=(rows // tr,), in_specs=[tile] * len(terms), out_specs=tile,
        out_shape=jax.ShapeDtypeStruct((rows, cols), out_dtype), compiler_params=_cp(("parallel",)),
    )(*[t.reshape(rows, cols) for t in terms])
    return out.reshape(shape)


def add_chips(name, land, own):
    _, rows, cols = land.shape
    tr = _row_tile(rows, cols, target=1 << 20)

    def body(land_r, own_r, o_r):
        me = 2 * lax.axis_index("x") + lax.axis_index("y")
        for k in range(NCHIP):
            @pl.when(me == k)
            def _():
                acc = None
                for j in range(NCHIP):
                    t = (own_r[...] if j == k else land_r[j]).astype(f32)
                    acc = t if acc is None else acc + t
                o_r[...] = acc

    tile = pl.BlockSpec((tr, cols), lambda i: (i, 0))
    return pl.pallas_call(
        body, name=name, grid=(rows // tr,), in_specs=[pl.BlockSpec((NCHIP, tr, cols), lambda i: (0, i, 0)), tile],
        out_specs=tile, out_shape=jax.ShapeDtypeStruct((rows, cols), f32), compiler_params=_cp(("parallel",)),
    )(land, own)


def reduce_scatter_layer(l, G):
    c = lax.axis_index("c")
    me = 2 * lax.axis_index("x") + lax.axis_index("y")
    grads = [G[n] if G[n].ndim == 3 or n == "w_in" else G[n].reshape(NCHIP, D // NCHIP, D) for n in BIG]
    theirs = rs_pair(l, grads)
    sums = []
    for n, g, t in zip(BIG, grads, theirs):
        rows = g.shape[-2] // 2
        mine = lax.dynamic_slice_in_dim(g, c * rows, rows, axis=g.ndim - 2)
        sums.append(add_n(f"rs_add_pair{l}_{n}", [mine, t], GRAD_WIRE))
    landed = rs_chips(l, sums)
    halves = []
    for a, (n, s, v) in enumerate(zip(BIG, sums, landed)):
        own = lax.dynamic_slice_in_dim(s, me * WIN_SHARD, WIN_SHARD, axis=1) if a == 0 else \
            lax.dynamic_index_in_dim(s, me, 0, keepdims=False)
        halves.append(add_chips(f"rs_add_chips{l}_{n}", v, own))
    return dict(zip(BIG, zip(halves, rs_join(l, halves))))


NDEV = 8


def allreduce_small(pack):
    rows = pack.shape[0]

    def body(p_ref, o_ref, buf, send, recv):
        x, y, c, _ = _place()
        me = 4 * x + 2 * y + c
        buf[me] = p_ref[...]

        def give(r, s, d):
            return _rcopy(s, d, send.at[r - 1], recv.at[r - 1], (x ^ (r >> 2), y ^ ((r >> 1) & 1), c ^ (r & 1)))

        for r in range(1, NDEV):
            _start_pieces(p_ref, buf.at[me], functools.partial(give, r), 128 << 10)
        for r in range(1, NDEV):
            give(r, p_ref, buf.at[me]).wait_send()
            src = 4 * (x ^ (r >> 2)) + 2 * (y ^ ((r >> 1) & 1)) + (c ^ (r & 1))
            give(r, p_ref, buf.at[src]).wait_recv()
        acc = buf[0]
        for d in range(1, NDEV):
            acc = acc + buf[d]
        o_ref[...] = acc

    vm = pl.BlockSpec(memory_space=pltpu.VMEM)
    return pl.pallas_call(
        body, name="allreduce_small", in_specs=[vm], out_specs=vm, out_shape=jax.ShapeDtypeStruct(pack.shape, f32),
        scratch_shapes=[pltpu.VMEM((NDEV, rows, LANES), f32), pltpu.SemaphoreType.DMA((NDEV - 1,)),
                        pltpu.SemaphoreType.DMA((NDEV - 1,))],
        compiler_params=pltpu.CompilerParams(vmem_limit_bytes=40 << 20),
    )(pack)


def _adamw_math(w, g, m, v):
    m = ADAM_B1 * m + (1.0 - ADAM_B1) * g
    v = ADAM_B2 * v + (1.0 - ADAM_B2) * (g * g)
    m_hat = m / (1.0 - ADAM_B1 ** ADAM_STEP)
    v_hat = v / (1.0 - ADAM_B2 ** ADAM_STEP)
    return -ADAM_LR * (m_hat / (jnp.sqrt(v_hat) + ADAM_EPS) + ADAM_WD * w), m, v


def adamw_big(name, halves, w, m, v):
    _, R, C = w.shape
    tr = _row_tile(R // 2, C, target=1 << 20)
    nt = R // 2 // tr

    def body(a0, b0, a1, b1, w_r, m_r, v_r, g_o, d_o, m_o, v_o):
        mine = pl.program_id(1) == lax.axis_index("c")
        g = jnp.where(pl.program_id(0) == 0, jnp.where(mine, a0[...], b0[...]), jnp.where(mine, a1[...], b1[...]))
        g_o[...] = g
        d_o[...], m_o[...], v_o[...] = _adamw_math(w_r[...], g, m_r[...], v_r[...])

    stk = pl.BlockSpec((None, tr, C), lambda l, h, i: (l, h * nt + i, 0))
    lay0 = pl.BlockSpec((tr, C), lambda l, h, i: (jnp.where(l == 0, i, nt - 1), 0))
    lay1 = pl.BlockSpec((tr, C), lambda l, h, i: (jnp.where(l == 0, 0, i), 0))
    return pl.pallas_call(
        body, name=name, grid=(DEPTH, 2, nt),
        in_specs=[lay0, lay0, lay1, lay1, stk, stk, stk],
        out_specs=[stk] * 4, out_shape=[jax.ShapeDtypeStruct(w.shape, f32)] * 4,
        compiler_params=_cp(("arbitrary", "arbitrary", "arbitrary")),
    )(*halves[0], *halves[1], w, m, v)


def adamw_small(name, g, w, m, v):
    def body(g_r, w_r, m_r, v_r, d_o, m_o, v_o):
        d_o[...], m_o[...], v_o[...] = _adamw_math(w_r[...], g_r[...], m_r[...], v_r[...])

    return pl.pallas_call(body, name=name, out_shape=[jax.ShapeDtypeStruct(w.shape, f32)] * 3)(g, w, m, v)


WEIGHTS = ("w_in", "conv_w", "gmlp_ln_g", "gmlp_ln_b", "w_s", "b_s", "p_a", "p_b", "p_c", "w_o", "ln1_g", "ln1_b",
           "w_gate", "w_up", "w_down", "ln2_g", "ln2_b")
VECS = ("ln1_g", "ln1_b", "ln2_g", "ln2_b", "gmlp_ln_g", "gmlp_ln_b")
ROWS_VEC, ROWS_BS, ROWS_WS, ROWS_CONV = D // LANES, 8, 8 * BLK, 3 * D // LANES
ROWS_LAYER = len(VECS) * ROWS_VEC + ROWS_BS + ROWS_WS + ROWS_CONV


def _pack_small(per_layer, tail):
    parts = []
    for P in per_layer:
        parts += [P[n].reshape(ROWS_VEC, LANES) for n in VECS]
        parts += [P["b_s"].reshape(ROWS_BS, LANES), P["w_s"].reshape(ROWS_WS, LANES), P["conv_w"].reshape(ROWS_CONV, LANES)]
    return jnp.concatenate(parts + [tail], axis=0)


def _unpack_small(pack):
    out = []
    for l in range(DEPTH):
        r = l * ROWS_LAYER
        P = {}
        for n in VECS:
            P[n] = pack[r:r + ROWS_VEC].reshape(D)
            r += ROWS_VEC
        P["b_s"] = pack[r:r + ROWS_BS].reshape(8, BLK)
        r += ROWS_BS
        P["w_s"] = pack[r:r + ROWS_WS].reshape(8, BLK, BLK)
        r += ROWS_WS
        P["conv_w"] = pack[r:r + ROWS_CONV].reshape(3, D)
        out.append(P)
    return out, pack[DEPTH * ROWS_LAYER:]


def kernel(x, positions, w_in, conv_w, gmlp_ln_g, gmlp_ln_b, w_s, b_s, p_a, p_b, p_c, w_o, ln1_g, ln1_b, w_gate, w_up, w_down, ln2_g, ln2_b, loss_target, m_w_in, m_conv_w, m_gmlp_ln_g, m_gmlp_ln_b, m_w_s, m_b_s, m_p_a, m_p_b, m_p_c, m_w_o, m_ln1_g, m_ln1_b, m_w_gate, m_w_up, m_w_down, m_ln2_g, m_ln2_b, v_w_in, v_conv_w, v_gmlp_ln_g, v_gmlp_ln_b, v_w_s, v_b_s, v_p_a, v_p_b, v_p_c, v_w_o, v_ln1_g, v_ln1_b, v_w_gate, v_w_up, v_w_down, v_ln2_g, v_ln2_b):
    Wt = dict(w_in=w_in, conv_w=conv_w, gmlp_ln_g=gmlp_ln_g, gmlp_ln_b=gmlp_ln_b, w_s=w_s, b_s=b_s, p_a=p_a, p_b=p_b,
              p_c=p_c, w_o=w_o, ln1_g=ln1_g, ln1_b=ln1_b, w_gate=w_gate, w_up=w_up, w_down=w_down, ln2_g=ln2_g, ln2_b=ln2_b)
    Mt = dict(w_in=m_w_in, conv_w=m_conv_w, gmlp_ln_g=m_gmlp_ln_g, gmlp_ln_b=m_gmlp_ln_b, w_s=m_w_s, b_s=m_b_s, p_a=m_p_a,
              p_b=m_p_b, p_c=m_p_c, w_o=m_w_o, ln1_g=m_ln1_g, ln1_b=m_ln1_b, w_gate=m_w_gate, w_up=m_w_up,
              w_down=m_w_down, ln2_g=m_ln2_g, ln2_b=m_ln2_b)
    Vt = dict(w_in=v_w_in, conv_w=v_conv_w, gmlp_ln_g=v_gmlp_ln_g, gmlp_ln_b=v_gmlp_ln_b, w_s=v_w_s, b_s=v_b_s, p_a=v_p_a,
              p_b=v_p_b, p_c=v_p_c, w_o=v_w_o, ln1_g=v_ln1_g, ln1_b=v_ln1_b, w_gate=v_w_gate, w_up=v_w_up,
              w_down=v_w_down, ln2_g=v_ln2_g, ln2_b=v_ln2_b)
    chip = 2 * lax.axis_index("x") + lax.axis_index("y")
    cw = D // NCHIP

    full = gather_weights([Wt[n].astype(MX) for n in BIG] + [conv_w])
    layers = []
    for l in range(DEPTH):
        Wl = dict(zip(BIG, (f[l] for f in full[:NBIG])))
        for n in ("p_a", "p_c", "w_o"):
            Wl[n] = Wl[n].reshape(D, D)
        Wl["conv_w"] = full[NBIG][l].transpose(1, 0, 2).reshape(3, D)
        for n in VECS + ("w_s", "b_s"):
            Wl[n] = Wt[n][l]
        layers.append(Wl)

    lsum, grad_x, grads = local_step(x[0], positions[0], loss_target[0], layers)

    red = [None] * DEPTH
    for l in reversed(range(DEPTH)):
        red[l] = reduce_scatter_layer(l, grads[l])
    pack = _pack_small([{n: (g[n] if n not in VECS else g[n]) for n in VECS + ("b_s", "w_s", "conv_w")} for g in grads], lsum)
    small, tail = _unpack_small(allreduce_small(pack))
    loss = tail[0, 0]

    G, DW, NM, NV = {}, {}, {}, {}
    for n in BIG:
        G[n], DW[n], NM[n], NV[n] = adamw_big("adamw_" + n, (red[0][n], red[1][n]), Wt[n], Mt[n], Vt[n])
    zc = jnp.zeros((3, D), f32)
    wp = _pack_small([{**{n: Wt[n][l] for n in VECS + ("b_s", "w_s")}, "conv_w": zc} for l in range(DEPTH)], jnp.zeros((8, LANES), f32))
    mp = _pack_small([{**{n: Mt[n][l] for n in VECS + ("b_s", "w_s")}, "conv_w": zc} for l in range(DEPTH)], jnp.zeros((8, LANES), f32))
    vp = _pack_small([{**{n: Vt[n][l] for n in VECS + ("b_s", "w_s")}, "conv_w": zc} for l in range(DEPTH)], jnp.ones((8, LANES), f32))
    gp = _pack_small(small, jnp.zeros((8, LANES), f32))
    outs = [_unpack_small(a)[0] for a in adamw_small("adamw_small", gp, wp, mp, vp)]
    for n in VECS + ("b_s", "w_s"):
        G[n] = jnp.stack([small[l][n] for l in range(DEPTH)])
        DW[n], NM[n], NV[n] = (jnp.stack([o[l][n] for l in range(DEPTH)]) for o in outs)
    gconv = jnp.stack([lax.dynamic_slice(small[l]["conv_w"], (0, chip * cw), (3, cw)) for l in range(DEPTH)])
    G["conv_w"] = gconv
    flat = lambda a: a.reshape(DEPTH * 3, cw)
    d, m2, v2 = adamw_small("adamw_conv", flat(gconv), flat(conv_w), flat(m_conv_w), flat(v_conv_w))
    DW["conv_w"], NM["conv_w"], NV["conv_w"] = (a.reshape(DEPTH, 3, cw) for a in (d, m2, v2))

    return (loss, grad_x[None], *[G[n] for n in WEIGHTS], *[DW[n] for n in WEIGHTS], *[NM[n] for n in WEIGHTS],
            *[NV[n] for n in WEIGHTS])
```

```python
import functools
import math

import jax
import jax.numpy as jnp
from jax import lax
from jax.experimental import pallas as pl
from jax.experimental.pallas import tpu as pltpu

D = 1024
NIN = 12800
DFF = 2816
NCHIP = 4
FB = DFF // NCHIP
WIN_SHARD = NIN // NCHIP
DEPTH = 2
GROUPS = ((128, 1), (512, 4), (2048, 16))
HD = 64
BLK = 128
AO = 512
ALPHA = (2 * DEPTH) ** 0.25
EPS = 1e-5
ROPE_THETA = 10000.0
LANES = 128
NEG = -1e30

C_GATES, C_BCH, C_QKV, C_UV = 0, 3 * D, 6 * D, 6 * D + 9 * AO

MX = jnp.bfloat16
ACT = jnp.bfloat16

ADAM_LR, ADAM_B1, ADAM_B2, ADAM_EPS, ADAM_WD, ADAM_STEP = 0.001, 0.9, 0.999, 1e-08, 0.01, 10

f32 = jnp.float32
NT = (((1,), (1,)), ((), ()))
TN = (((0,), (0,)), ((), ()))


def _cp(sem, vmem_mb=48):
    return pltpu.CompilerParams(dimension_semantics=sem, vmem_limit_bytes=vmem_mb << 20)


def _dot(a, b, dims=None):
    if dims is None:
        return jnp.dot(a, b, preferred_element_type=f32)
    return lax.dot_general(a, b, dims, preferred_element_type=f32)


def _ln_stats(r):
    mu = jnp.mean(r, axis=-1, keepdims=True)
    xc = r - mu
    var = jnp.mean(xc * xc, axis=-1, keepdims=True)
    rstd = lax.rsqrt(var + EPS)
    return xc * rstd, rstd


def _ln_bwd(dy, xhat, rstd, g):
    dxh = dy * g
    return rstd * (dxh - jnp.mean(dxh, axis=-1, keepdims=True) - xhat * jnp.mean(dxh * xhat, axis=-1, keepdims=True))


def _gelu(x):
    return 0.5 * x * (1.0 + lax.erf(x * (1.0 / math.sqrt(2.0))))


def _gelu_grad(x):
    return 0.5 * (1.0 + lax.erf(x * (1.0 / math.sqrt(2.0)))) + x * jnp.exp(-0.5 * x * x) * (1.0 / math.sqrt(2.0 * math.pi))


def _sigmoid(x):
    return 0.5 * jnp.tanh(0.5 * x) + 0.5


def _acc_rows(o_ref, first, val):
    @pl.when(first)
    def _():
        o_ref[...] = jnp.zeros_like(o_ref)
    o_ref[...] += jnp.sum(val, axis=0, keepdims=True)


def mm_in(x, w):
    T = x.shape[0]
    tm, tn = min(1024, T), 1280

    def body(x_ref, w_ref, o_ref, xb):
        @pl.when(pl.program_id(1) == 0)
        def _():
            xb[...] = x_ref[...].astype(MX)
        o_ref[...] = _dot(xb[...], w_ref[...]).astype(o_ref.dtype)

    return pl.pallas_call(
        body, name="mm_in", grid=(T // tm, NIN // tn),
        in_specs=[pl.BlockSpec((tm, D), lambda i, j: (i, 0)), pl.BlockSpec((D, tn), lambda i, j: (0, j))],
        out_specs=pl.BlockSpec((tm, tn), lambda i, j: (i, j)),
        out_shape=jax.ShapeDtypeStruct((T, NIN), ACT),
        scratch_shapes=[pltpu.VMEM((tm, D), MX)],
        compiler_params=_cp(("parallel", "arbitrary")),
    )(x, w)


HALO = 16
TM_AC = 256


def _uv_specs():
    return [pl.BlockSpec((TM_AC, 512), functools.partial(lambda i, j: (i, j), j=C_UV // 512 + j)) for j in range(4)]


def _gmlp_fwd(up, vp, ws_ref, bs_ref, lg, lb):
    u = _gelu(up)
    xhat, rstd = _ln_stats(_gelu(vp))
    vn = xhat * lg + lb
    vnb = vn.astype(MX)
    rows = []
    for c in range(up.shape[0] // BLK):
        r = slice(c * BLK, (c + 1) * BLK)
        rows.append(jnp.concatenate(
            [_dot(ws_ref[g], vnb[r, g * BLK:(g + 1) * BLK]) + bs_ref[g] for g in range(8)], axis=1))
    return u, vn, xhat, rstd, jnp.concatenate(rows, axis=0)


def mix_ac_fwd(proj, conv_w, wst, bsx, lg, lb):
    T = proj.shape[0]
    tm = TM_AC

    def body(bch, halo, u0, u1, v0, v1, cw, ws, bs, lg_ref, lb_ref, ya, yc, zs):
        i = pl.program_id(0)
        pb = bch[...].astype(f32)
        z = pb[:, D:2 * D] * pb[:, 2 * D:]
        hz = halo[:, :D].astype(f32) * halo[:, D:].astype(f32)
        zs[0:HALO, :] = jnp.where(i > 0, hz, 0.0)
        zs[HALO:HALO + tm, :] = z
        cv = cw[0:1, :] * zs[HALO - 2:HALO - 2 + tm, :] + cw[1:2, :] * zs[HALO - 1:HALO - 1 + tm, :] + cw[2:3, :] * z
        ya[...] = (pb[:, :D] * cv).astype(ya.dtype)
        up = jnp.concatenate([u0[...], u1[...]], axis=1).astype(f32)
        vp = jnp.concatenate([v0[...], v1[...]], axis=1).astype(f32)
        u, _, _, _, sp = _gmlp_fwd(up, vp, ws, bs, lg_ref[...], lb_ref[...])
        yc[...] = (u * sp).astype(yc.dtype)

    full = lambda shape: pl.BlockSpec(shape, lambda i: (0,) * len(shape))
    return pl.pallas_call(
        body, name="mix_ac_fwd", grid=(T // tm,),
        in_specs=[pl.BlockSpec((tm, 3 * D), lambda i: (i, 1)),
                  pl.BlockSpec((HALO, 2 * D), lambda i: (jnp.maximum(i * (tm // HALO) - 1, 0), 2)),
                  *_uv_specs(), full((3, D)), full((8, BLK, BLK)), full((8, BLK, BLK)), full((1, D)), full((1, D))],
        out_specs=[pl.BlockSpec((tm, D), lambda i: (i, 0))] * 2,
        out_shape=[jax.ShapeDtypeStruct((T, D), MX)] * 2,
        scratch_shapes=[pltpu.VMEM((HALO + tm, D), f32)],
        compiler_params=_cp(("parallel",)),
    )(proj, proj, proj, proj, proj, proj, conv_w, wst, bsx, lg, lb)


def _swap_halves(x):
    lane = lax.broadcasted_iota(jnp.int32, x.shape, 1)
    return jnp.where((lane % HD) < HD // 2, pltpu.roll(x, x.shape[1] - HD // 2, 1), pltpu.roll(x, HD // 2, 1))


def _tile4(t):
    return jnp.concatenate([t] * (AO // LANES), axis=1)


TM_FOLD = 512


def _fold_out(nat, x, out_ref, d):
    if d == 1:
        out_ref[0] = x.astype(out_ref.dtype)
        return
    rows = x.shape[0] // d
    for j in range(AO // LANES):
        nat[j] = x[:, j * LANES:(j + 1) * LANES]
    for r in range(d):
        out_ref[r] = jnp.concatenate(
            [nat.at[j][pl.ds(r, rows, stride=d), :] for j in range(AO // LANES)], axis=1).astype(out_ref.dtype)


def _unfold_in(nat, in_ref, d):
    if d == 1:
        return in_ref[0].astype(f32)
    rows = in_ref.shape[1]
    for r in range(d):
        v = in_ref[r].astype(f32)
        for j in range(AO // LANES):
            nat.at[j][pl.ds(r, rows, stride=d), :] = v[:, j * LANES:(j + 1) * LANES]
    return jnp.concatenate([nat[j] for j in range(AO // LANES)], axis=1)


def fold_rope(proj, cos_t, sin_t, g, d):
    T = proj.shape[0]
    tm = TM_FOLD
    rows = tm // d

    def body(x_ref, c_ref, s_ref, q_o, k_o, v_o, nat):
        cos, sin = _tile4(c_ref[...]), _tile4(s_ref[...])
        for part, out, scale in ((0, q_o, HD ** -0.5), (1, k_o, 1.0), (2, v_o, None)):
            x = x_ref[:, part * AO:(part + 1) * AO].astype(f32)
            if scale is not None:
                x = (x * cos + _swap_halves(x) * sin) * scale
            _fold_out(nat, x, out, d)

    fold_spec = pl.BlockSpec((d, rows, AO), lambda i: (0, i, 0))
    return pl.pallas_call(
        body, name=f"fold_rope{g}", grid=(T // tm,),
        in_specs=[pl.BlockSpec((tm, 3 * AO), lambda i: (i, C_QKV // (3 * AO) + g)),
                  pl.BlockSpec((tm, LANES), lambda i: (i, 0)), pl.BlockSpec((tm, LANES), lambda i: (i, 0))],
        out_specs=[fold_spec] * 3,
        out_shape=[jax.ShapeDtypeStruct((d, T // d, AO), MX)] * 3,
        scratch_shapes=[pltpu.VMEM((AO // LANES, tm, LANES), f32)],
        compiler_params=_cp(("parallel",)),
    )(proj, cos_t, sin_t)


def _stack_heads(x):
    lane = lax.broadcasted_iota(jnp.int32, x.shape, 1)
    z = jnp.zeros_like(x)
    return jnp.concatenate([jnp.where(lane < HD, x, z), jnp.where(lane >= HD, x, z)], axis=0)


def _unstack_heads(y):
    lane = lax.broadcasted_iota(jnp.int32, (BLK, LANES), 1)
    return jnp.where(lane < HD, y[:BLK], y[BLK:])


def _window_masks():
    row = lax.broadcasted_iota(jnp.int32, (2 * BLK, 2 * BLK), 0) % BLK
    col = lax.broadcasted_iota(jnp.int32, (2 * BLK, 2 * BLK), 1)
    return (col < BLK) & (col >= row), (col >= BLK) & (col - BLK <= row)


def _two_blocks(ref, b):
    r0 = pl.multiple_of(b * BLK, BLK)
    rp = pl.multiple_of(jnp.maximum(b - 1, 0) * BLK, BLK)
    return jnp.concatenate([ref[pl.ds(rp, BLK), :], ref[pl.ds(r0, BLK), :]], axis=0)


def attn_fwd(qf, kf, vf, g, nb):
    T = qf.shape[0]

    def body(q_ref, k_ref, v_ref, o_ref, l_ref):
        prev_m, cur_m = _window_masks()

        def step(b, carry):
            r0 = pl.multiple_of(b * BLK, BLK)
            qs = _stack_heads(q_ref[pl.ds(r0, BLK), :])
            s = _dot(qs, _two_blocks(k_ref, b), NT)
            s = jnp.where(cur_m | (prev_m & ((b % nb) != 0)), s, NEG)
            m = jnp.max(s, axis=-1, keepdims=True)
            p = jnp.exp(s - m)
            l = jnp.sum(p, axis=-1, keepdims=True)
            o = _dot(p.astype(MX), _two_blocks(v_ref, b)) / l
            o_ref[pl.ds(r0, BLK), :] = _unstack_heads(o)
            l_ref[pl.ds(r0, BLK), :] = _unstack_heads(jnp.broadcast_to(m + jnp.log(l), (2 * BLK, LANES)))
            return carry

        lax.fori_loop(0, T // BLK, step, 0, unroll=4)

    spec = pl.BlockSpec((T, LANES), lambda j: (0, j))
    return pl.pallas_call(
        body, name=f"attn_fwd{g}", grid=(AO // LANES,),
        in_specs=[spec] * 3, out_specs=[spec] * 2,
        out_shape=[jax.ShapeDtypeStruct((T, AO), f32)] * 2,
        compiler_params=_cp(("parallel",), 56),
    )(qf, kf, vf)


def _group_weights(lses):
    m = jnp.maximum(jnp.maximum(lses[0], lses[1]), lses[2])
    e = [jnp.exp(l - m) for l in lses]
    inv = 1.0 / (e[0] + e[1] + e[2])
    return [x * inv for x in e]


def _fold_specs(T, tm):
    specs = []
    for _, d in GROUPS:
        specs.append(pl.BlockSpec((d, tm // d, AO), lambda i: (0, i, 0)))
    return specs


def combine_fwd(os_, lses):
    T = os_[0].shape[0] * os_[0].shape[1]
    tm = TM_FOLD

    def body(o0, o1, o2, l0, l1, l2, y_ref, nat):
        o = [_unfold_in(nat, r, d) for r, (_, d) in zip((o0, o1, o2), GROUPS)]
        ls = [_unfold_in(nat, r, d) for r, (_, d) in zip((l0, l1, l2), GROUPS)]
        w = _group_weights(ls)
        y_ref[...] = (w[0] * o[0] + w[1] * o[1] + w[2] * o[2]).astype(y_ref.dtype)

    specs = _fold_specs(T, tm)
    return pl.pallas_call(
        body, name="combine_fwd", grid=(T // tm,),
        in_specs=specs + specs, out_specs=pl.BlockSpec((tm, AO), lambda i: (i, 0)),
        out_shape=jax.ShapeDtypeStruct((T, AO), MX),
        scratch_shapes=[pltpu.VMEM((AO // LANES, tm, LANES), f32)],
        compiler_params=_cp(("parallel",)),
    )(*os_, *lses)


TM_MIX = 256


def mix_out_fwd(proj, ya, yb, yc, x0, pa, pb, pc, wo, g1, b1):
    T = x0.shape[0]
    tm = min(TM_MIX, T)

    def body(gt, ya_r, yb_r, yc_r, x0_r, pa_r, pb_r, pc_r, wo_r, g_r, b_r, mabc, m_o, r1_o, x1_o):
        ma = _dot(ya_r[...], pa_r[...])
        ybv = yb_r[...]
        mb = jnp.concatenate([_dot(ybv, pb_r[k]) for k in range(NCHIP)], axis=1)
        mc = _dot(yc_r[...], pc_r[...])
        m = jnp.zeros((tm, D), f32)
        for j, mm in enumerate((ma, mb, mc)):
            mabc[:, j * D:(j + 1) * D] = mm.astype(mabc.dtype)
            m = m + _sigmoid(gt[:, j * D:(j + 1) * D].astype(f32)) * mm
        mb16 = m.astype(MX)
        m_o[...] = mb16
        r1 = ALPHA * x0_r[...] + _dot(mb16, wo_r[...])
        r1_o[...] = r1
        xhat, _ = _ln_stats(r1)
        x1_o[...] = xhat * g_r[...] + b_r[...]

    full = lambda shape: pl.BlockSpec(shape, lambda i: (0,) * len(shape))
    tile = lambda w: pl.BlockSpec((tm, w), lambda i: (i, 0))
    return pl.pallas_call(
        body, name="mix_out_fwd", grid=(T // tm,),
        in_specs=[tile(3 * D), tile(D), tile(AO), tile(D), tile(D), full((D, D)), full((NCHIP, AO, D // NCHIP)),
                  full((D, D)), full((D, D)), full((1, D)), full((1, D))],
        out_specs=[tile(3 * D), tile(D), tile(D), tile(D)],
        out_shape=[jax.ShapeDtypeStruct((T, 3 * D), MX), jax.ShapeDtypeStruct((T, D), MX),
                   jax.ShapeDtypeStruct((T, D), f32), jax.ShapeDtypeStruct((T, D), f32)],
        compiler_params=_cp(("parallel",), 56),
    )(proj, ya, yb, yc, x0, pa, pb, pc, wo, g1, b1)


TM_FF = 512


def ffn_up_fwd(x1, wg, wu):
    T = x1.shape[0]
    tm = min(TM_FF, T)

    def body(x_r, wg_r, wu_r, g_o, u_o, h_o, xb):
        @pl.when(pl.program_id(1) == 0)
        def _():
            xb[...] = x_r[...].astype(MX)
        gate = _dot(xb[...], wg_r[0])
        up = _dot(xb[...], wu_r[0])
        g_o[0] = gate.astype(g_o.dtype)
        u_o[0] = up.astype(u_o.dtype)
        h_o[0] = (gate * _sigmoid(gate) * up).astype(h_o.dtype)

    wspec = pl.BlockSpec((1, D, FB), lambda i, k: (k, 0, 0))
    ospec = pl.BlockSpec((1, tm, FB), lambda i, k: (k, i, 0))
    return pl.pallas_call(
        body, name="ffn_up_fwd", grid=(T // tm, NCHIP),
        in_specs=[pl.BlockSpec((tm, D), lambda i, k: (i, 0)), wspec, wspec],
        out_specs=[ospec] * 3,
        out_shape=[jax.ShapeDtypeStruct((NCHIP, T, FB), ACT)] * 2 + [jax.ShapeDtypeStruct((NCHIP, T, FB), MX)],
        scratch_shapes=[pltpu.VMEM((tm, D), MX)],
        compiler_params=_cp(("parallel", "arbitrary")),
    )(x1, wg, wu)


def ffn_down_fwd(hh, wd, x1, g2, b2):
    T = x1.shape[0]
    tm = min(TM_FF, T)

    def body(h_r, w_r, x_r, g_r, b_r, r2_o, x2_o):
        r2 = ALPHA * x_r[...]
        for k in range(NCHIP):
            r2 = r2 + _dot(h_r[k], w_r[k])
        r2_o[...] = r2
        xhat, _ = _ln_stats(r2)
        x2_o[...] = xhat * g_r[...] + b_r[...]

    tile = pl.BlockSpec((tm, D), lambda i: (i, 0))
    vec = pl.BlockSpec((1, D), lambda i: (0, 0))
    return pl.pallas_call(
        body, name="ffn_down_fwd", grid=(T // tm,),
        in_specs=[pl.BlockSpec((NCHIP, tm, FB), lambda i: (0, i, 0)), pl.BlockSpec((NCHIP, FB, D), lambda i: (0, 0, 0)),
                  tile, vec, vec],
        out_specs=[tile, tile], out_shape=[jax.ShapeDtypeStruct((T, D), f32)] * 2,
        compiler_params=_cp(("parallel",)),
    )(hh, wd, x1, g2, b2)


def loss_grad(y, tgt):
    T = y.shape[0]
    tm = min(512, T)

    def body(y_r, t_r, l_o, dy_o):
        e = y_r[...] - t_r[...]
        dy_o[...] = e * (1.0 / D)

        @pl.when(pl.program_id(0) == 0)
        def _():
            l_o[...] = jnp.zeros_like(l_o)
        l_o[...] += (0.5 / D) * jnp.sum(e * e)

    tile = pl.BlockSpec((tm, D), lambda i: (i, 0))
    return pl.pallas_call(
        body, name="loss_grad", grid=(T // tm,),
        in_specs=[tile, tile], out_specs=[pl.BlockSpec((8, LANES), lambda i: (0, 0)), tile],
        out_shape=[jax.ShapeDtypeStruct((8, LANES), f32), jax.ShapeDtypeStruct((T, D), f32)],
        compiler_params=_cp(("arbitrary",)),
    )(y, tgt)


def ffn_down_bwd(dx2, r2, g2, wd, gate, up):
    T = dx2.shape[0]
    tm = min(TM_FF, T)

    def body(dx_r, r_r, g_r, w_r, ga_r, up_r, dr_o, dg_o, du_o, dlg_o, dlb_o, drb):
        i, k = pl.program_id(0), pl.program_id(1)

        @pl.when(k == 0)
        def _():
            xhat, rstd = _ln_stats(r_r[...])
            dx = dx_r[...]
            _acc_rows(dlg_o, i == 0, dx * xhat)
            _acc_rows(dlb_o, i == 0, dx)
            dr = _ln_bwd(dx, xhat, rstd, g_r[...])
            dr_o[...] = dr
            drb[...] = dr.astype(MX)

        dhh = _dot(drb[...], w_r[0], NT)
        gate_v, up_v = ga_r[0].astype(f32), up_r[0].astype(f32)
        sg = _sigmoid(gate_v)
        dg_o[0] = (dhh * up_v * sg * (1.0 + gate_v * (1.0 - sg))).astype(dg_o.dtype)
        du_o[0] = (dhh * gate_v * sg).astype(du_o.dtype)

    tile = pl.BlockSpec((tm, D), lambda i, k: (i, 0))
    vec = pl.BlockSpec((1, D), lambda i, k: (0, 0))
    blk = pl.BlockSpec((1, tm, FB), lambda i, k: (k, i, 0))
    return pl.pallas_call(
        body, name="ffn_down_bwd", grid=(T // tm, NCHIP),
        in_specs=[tile, tile, vec, pl.BlockSpec((1, FB, D), lambda i, k: (k, 0, 0)), blk, blk],
        out_specs=[tile, blk, blk, vec, vec],
        out_shape=[jax.ShapeDtypeStruct((T, D), f32)] + [jax.ShapeDtypeStruct((NCHIP, T, FB), MX)] * 2
        + [jax.ShapeDtypeStruct((1, D), f32)] * 2,
        scratch_shapes=[pltpu.VMEM((tm, D), MX)],
        compiler_params=_cp(("arbitrary", "arbitrary")),
    )(dx2, r2, g2, wd, gate, up)


def ffn_up_bwd(dr2, dgate, dup, wg, wu, r1, g1):
    T = dr2.shape[0]
    tm = min(TM_FF, T)

    def body(dr2_r, dg_r, du_r, wg_r, wu_r, r1_r, g_r, dr1_o, dlg_o, dlb_o, acc):
        i, k = pl.program_id(0), pl.program_id(1)

        @pl.when(k == 0)
        def _():
            acc[...] = ALPHA * dr2_r[...]
        acc[...] += _dot(dg_r[0], wg_r[0], NT) + _dot(du_r[0], wu_r[0], NT)

        @pl.when(k == NCHIP - 1)
        def _():
            dx = acc[...]
            xhat, rstd = _ln_stats(r1_r[...])
            _acc_rows(dlg_o, i == 0, dx * xhat)
            _acc_rows(dlb_o, i == 0, dx)
            dr1_o[...] = _ln_bwd(dx, xhat, rstd, g_r[...])

    tile = pl.BlockSpec((tm, D), lambda i, k: (i, 0))
    vec = pl.BlockSpec((1, D), lambda i, k: (0, 0))
    blk = pl.BlockSpec((1, tm, FB), lambda i, k: (k, i, 0))
    wspec = pl.BlockSpec((1, D, FB), lambda i, k: (k, 0, 0))
    return pl.pallas_call(
        body, name="ffn_up_bwd", grid=(T // tm, NCHIP),
        in_specs=[tile, blk, blk, wspec, wspec, tile, vec],
        out_specs=[tile, vec, vec],
        out_shape=[jax.ShapeDtypeStruct((T, D), f32)] + [jax.ShapeDtypeStruct((1, D), f32)] * 2,
        scratch_shapes=[pltpu.VMEM((tm, D), f32)],
        compiler_params=_cp(("arbitrary", "arbitrary")),
    )(dr2, dgate, dup, wg, wu, r1, g1)


def mix_out_bwd(dr1, proj, mabc, wo, pa, pb, pc):
    T = dr1.shape[0]
    tm = min(TM_MIX, T)

    def body(dr_r, gt, mabc_r, wo_r, pa_r, pb_r, pc_r, dmabc_o, dgt_o, dya_o, dyb_o, dyc_o):
        dm = _dot(dr_r[...].astype(MX), wo_r[...], NT)
        dmx = []
        for j in range(3):
            s = _sigmoid(gt[:, j * D:(j + 1) * D].astype(f32))
            v = (dm * s).astype(MX)
            dmx.append(v)
            dmabc_o[:, j * D:(j + 1) * D] = v
            dgt_o[:, j * D:(j + 1) * D] = (dm * mabc_r[:, j * D:(j + 1) * D].astype(f32) * s * (1.0 - s)).astype(dgt_o.dtype)
        dya_o[...] = _dot(dmx[0], pa_r[...], NT)
        dyb = jnp.zeros((tm, AO), f32)
        for k in range(NCHIP):
            dyb = dyb + _dot(dmx[1][:, k * (D // NCHIP):(k + 1) * (D // NCHIP)], pb_r[k], NT)
        dyb_o[...] = dyb
        dyc_o[...] = _dot(dmx[2], pc_r[...], NT)

    full = lambda shape: pl.BlockSpec(shape, lambda i: (0,) * len(shape))
    tile = lambda w: pl.BlockSpec((tm, w), lambda i: (i, 0))
    return pl.pallas_call(
        body, name="mix_out_bwd", grid=(T // tm,),
        in_specs=[tile(D), tile(3 * D), tile(3 * D), full((D, D)), full((D, D)), full((NCHIP, AO, D // NCHIP)), full((D, D))],
        out_specs=[tile(3 * D), tile(3 * D), tile(D), tile(AO), tile(D)],
        out_shape=[jax.ShapeDtypeStruct((T, 3 * D), MX), jax.ShapeDtypeStruct((T, 3 * D), MX),
                   jax.ShapeDtypeStruct((T, D), f32), jax.ShapeDtypeStruct((T, AO), f32), jax.ShapeDtypeStruct((T, D), f32)],
        compiler_params=_cp(("parallel",), 56),
    )(dr1, proj, mabc, wo, pa, pb, pc)


def transpose_cast(x):
    T = x.shape[0]
    tm = min(512, T)

    def body(x_r, o_r):
        o_r[...] = x_r[...].T.astype(o_r.dtype)

    return pl.pallas_call(
        body, name="transpose_cast", grid=(T // tm,),
        in_specs=[pl.BlockSpec((tm, D), lambda i: (i, 0))], out_specs=pl.BlockSpec((D, tm), lambda i: (0, i)),
        out_shape=jax.ShapeDtypeStruct((D, T), MX), compiler_params=_cp(("parallel",)),
    )(x)


def tn_matmul(name, a, b, a_spec, b_spec, out_shape, out_spec, grid, a_is_t=False):
    nt = len(grid) - 1

    def body(a_r, b_r, o_r):
        @pl.when(pl.program_id(nt) == 0)
        def _():
            o_r[...] = jnp.zeros_like(o_r)
        av = a_r[...].reshape(a_r.shape[-2:]).astype(MX)
        bv = b_r[...].reshape(b_r.shape[-2:]).astype(MX)
        o_r[...] += _dot(av, bv, None if a_is_t else TN).reshape(o_r.shape)

    return pl.pallas_call(
        body, name=name, grid=grid, in_specs=[a_spec, b_spec], out_specs=out_spec,
        out_shape=jax.ShapeDtypeStruct(out_shape, f32),
        compiler_params=_cp(("parallel",) * nt + ("arbitrary",), 56),
    )(a, b)


def attn_pre_bwd(dyb, os_, lses):
    T = dyb.shape[0]
    tm = TM_FOLD

    def body(dy_r, o0, o1, o2, l0, l1, l2, ones_r, d0, d1, d2, f0, f1, f2, nat):
        o = [_unfold_in(nat, r, d) for r, (_, d) in zip((o0, o1, o2), GROUPS)]
        ls = [_unfold_in(nat, r, d) for r, (_, d) in zip((l0, l1, l2), GROUPS)]
        w = _group_weights(ls)
        dy = dy_r[...]
        t = dy * (w[0] * o[0] + w[1] * o[1] + w[2] * o[2])
        hi = t.astype(MX)
        lo = (t - hi.astype(f32)).astype(MX)
        c = _dot(hi, ones_r[...]) + _dot(lo, ones_r[...])
        for wg, do_o, df_o, (_, d) in zip(w, (d0, d1, d2), (f0, f1, f2), GROUPS):
            _fold_out(nat, wg * dy, do_o, d)
            _fold_out(nat, -wg * c, df_o, d)

    specs = _fold_specs(T, tm)
    return pl.pallas_call(
        body, name="attn_pre_bwd", grid=(T // tm,),
        in_specs=[pl.BlockSpec((tm, AO), lambda i: (i, 0))] + specs + specs + [pl.BlockSpec((AO, AO), lambda i: (0, 0))],
        out_specs=specs + specs,
        out_shape=[jax.ShapeDtypeStruct((d, T // d, AO), MX) for _, d in GROUPS]
        + [jax.ShapeDtypeStruct((d, T // d, AO), f32) for _, d in GROUPS],
        scratch_shapes=[pltpu.VMEM((AO // LANES, tm, LANES), f32)],
        compiler_params=_cp(("parallel",)),
    )(dyb, *os_, *lses, _head_ones())


def _head_ones():
    i = jnp.arange(AO) // HD
    return (i[:, None] == i[None, :]).astype(MX)


def attn_bwd(qf, kf, vf, dof, lse, df, g, nb):
    T = qf.shape[0]

    def body(q_ref, k_ref, v_ref, do_ref, l_ref, d_ref, dq_ref, dk_ref, dv_ref):
        prev_m, cur_m = _window_masks()

        def head_col(ref, r0):
            v = ref[pl.ds(r0, BLK), :]
            return jnp.concatenate([v[:, 0:1], v[:, HD:HD + 1]], axis=0)

        def step(b, carry):
            dk_c, dv_c = carry
            r0 = pl.multiple_of(b * BLK, BLK)
            rp = pl.multiple_of(jnp.maximum(b - 1, 0) * BLK, BLK)
            qs, dos = _stack_heads(q_ref[pl.ds(r0, BLK), :]), _stack_heads(do_ref[pl.ds(r0, BLK), :])
            k2, v2 = _two_blocks(k_ref, b), _two_blocks(v_ref, b)
            valid = cur_m | (prev_m & ((b % nb) != 0))
            p = jnp.where(valid, jnp.exp(_dot(qs, k2, NT) - head_col(l_ref, r0)), 0.0)
            ds = (p * (_dot(dos, v2, NT) + head_col(d_ref, r0))).astype(MX)
            dq_ref[pl.ds(r0, BLK), :] = _unstack_heads(_dot(ds, k2)).astype(dq_ref.dtype)
            dk2 = _dot(ds, qs, TN)
            dv2 = _dot(p.astype(MX), dos, TN)
            dk_ref[pl.ds(rp, BLK), :] = (dk_c + dk2[:BLK]).astype(dk_ref.dtype)
            dv_ref[pl.ds(rp, BLK), :] = (dv_c + dv2[:BLK]).astype(dv_ref.dtype)
            return dk2[BLK:], dv2[BLK:]

        zero = jnp.zeros((BLK, LANES), f32)

        def two_steps(i, carry):
            return step(2 * i + 1, step(2 * i, carry))

        dk_c, dv_c = lax.fori_loop(0, T // BLK // 2, two_steps, (zero, zero))
        dk_ref[pl.ds(T - BLK, BLK), :] = dk_c.astype(dk_ref.dtype)
        dv_ref[pl.ds(T - BLK, BLK), :] = dv_c.astype(dv_ref.dtype)

    spec = pl.BlockSpec((T, LANES), lambda j: (0, j))
    return pl.pallas_call(
        body, name=f"attn_bwd{g}", grid=(AO // LANES,),
        in_specs=[spec] * 6, out_specs=[spec] * 3,
        out_shape=[jax.ShapeDtypeStruct((T, AO), MX)] * 3,
        compiler_params=_cp(("parallel",), 60),
    )(qf, kf, vf, dof, lse, df)


def unfold_rope_bwd(dqf, dkf, dvf, cos_t, sin_t, g, d):
    T = dqf.shape[0] * dqf.shape[1]
    tm = TM_FOLD

    def body(q_r, k_r, v_r, c_ref, s_ref, o_ref, nat):
        cos, sin = _tile4(c_ref[...]), _tile4(s_ref[...])
        for part, ref, scale in ((0, q_r, HD ** -0.5), (1, k_r, 1.0), (2, v_r, None)):
            x = _unfold_in(nat, ref, d)
            if scale is not None:
                x = (x * cos - _swap_halves(x) * sin) * scale
            o_ref[:, part * AO:(part + 1) * AO] = x.astype(o_ref.dtype)

    fold_spec = pl.BlockSpec((d, tm // d, AO), lambda i: (0, i, 0))
    tab = pl.BlockSpec((tm, LANES), lambda i: (i, 0))
    return pl.pallas_call(
        body, name=f"unfold_rope_bwd{g}", grid=(T // tm,),
        in_specs=[fold_spec] * 3 + [tab, tab],
        out_specs=pl.BlockSpec((tm, 3 * AO), lambda i: (i, 0)),
        out_shape=jax.ShapeDtypeStruct((T, 3 * AO), MX),
        scratch_shapes=[pltpu.VMEM((AO // LANES, tm, LANES), f32)],
        compiler_params=_cp(("parallel",)),
    )(dqf, dkf, dvf, cos_t, sin_t)


def conv_bwd(dya, proj, conv_w):
    T = dya.shape[0]
    tm = TM_AC
    last = T // tm - 1

    def body(dy_r, bch, hprev, dy_next, b_next, cw, d_o, dw_o, zs, ds):
        i = pl.program_id(0)
        pb = bch[...].astype(f32)
        bp, cp, hp = pb[:, :D], pb[:, D:2 * D], pb[:, 2 * D:]
        z = cp * hp
        hz = hprev[:, :D].astype(f32) * hprev[:, D:].astype(f32)
        zs[0:HALO, :] = jnp.where(i > 0, hz, 0.0)
        zs[HALO:HALO + tm, :] = z
        z2, z1 = zs[HALO - 2:HALO - 2 + tm, :], zs[HALO - 1:HALO - 1 + tm, :]
        cv = cw[0:1, :] * z2 + cw[1:2, :] * z1 + cw[2:3, :] * z
        dy = dy_r[...]
        dcv = dy * bp
        ds[0:tm, :] = dcv
        ds[tm:tm + HALO, :] = jnp.where(i < last, dy_next[...] * b_next[...].astype(f32), 0.0)
        dz = cw[2:3, :] * dcv + cw[1:2, :] * ds[1:1 + tm, :] + cw[0:1, :] * ds[2:2 + tm, :]
        d_o[:, :D] = (dy * cv).astype(d_o.dtype)
        d_o[:, D:2 * D] = (dz * hp).astype(d_o.dtype)
        d_o[:, 2 * D:] = (dz * cp).astype(d_o.dtype)

        @pl.when(i == 0)
        def _():
            dw_o[...] = jnp.zeros_like(dw_o)
        dw_o[0:1, :] += jnp.sum(dcv * z2, axis=0, keepdims=True)
        dw_o[1:2, :] += jnp.sum(dcv * z1, axis=0, keepdims=True)
        dw_o[2:3, :] += jnp.sum(dcv * z, axis=0, keepdims=True)

    nh = tm // HALO
    return pl.pallas_call(
        body, name="conv_bwd", grid=(T // tm,),
        in_specs=[pl.BlockSpec((tm, D), lambda i: (i, 0)), pl.BlockSpec((tm, 3 * D), lambda i: (i, 1)),
                  pl.BlockSpec((HALO, 2 * D), lambda i: (jnp.maximum(i * nh - 1, 0), 2)),
                  pl.BlockSpec((HALO, D), lambda i: (jnp.minimum((i + 1) * nh, T // HALO - 1), 0)),
                  pl.BlockSpec((HALO, D), lambda i: (jnp.minimum((i + 1) * nh, T // HALO - 1), 3)),
                  pl.BlockSpec((3, D), lambda i: (0, 0))],
        out_specs=[pl.BlockSpec((tm, 3 * D), lambda i: (i, 0)), pl.BlockSpec((3, D), lambda i: (0, 0))],
        out_shape=[jax.ShapeDtypeStruct((T, 3 * D), MX), jax.ShapeDtypeStruct((3, D), f32)],
        scratch_shapes=[pltpu.VMEM((HALO + tm, D), f32), pltpu.VMEM((tm + HALO, D), f32)],
        compiler_params=_cp(("arbitrary",)),
    )(dya, proj, proj, dya, proj, conv_w)


def gmlp_bwd(dyc, proj, wst, bsx, lg, lb):
    T = dyc.shape[0]
    tm = TM_AC
    last = T // tm - 1

    def body(dy_r, u0, u1, v0, v1, ws, bs, lg_r, lb_r, d_o, dws_o, dbs_o, dlg_o, dlb_o, bacc):
        i = pl.program_id(0)
        up = jnp.concatenate([u0[...], u1[...]], axis=1).astype(f32)
        vp = jnp.concatenate([v0[...], v1[...]], axis=1).astype(f32)
        u, vn, xhat, rstd, sp = _gmlp_fwd(up, vp, ws, bs, lg_r[...], lb_r[...])
        dy = dy_r[...]
        d_o[:, :D] = (dy * sp * _gelu_grad(up)).astype(d_o.dtype)
        dsp = dy * u
        dspb, vnb = dsp.astype(MX), vn.astype(MX)

        @pl.when(i == 0)
        def _():
            dws_o[...] = jnp.zeros_like(dws_o)
            bacc[...] = jnp.zeros_like(bacc)

        rows = []
        for c in range(tm // BLK):
            r = slice(c * BLK, (c + 1) * BLK)
            cols = []
            for g in range(8):
                cs = slice(g * BLK, (g + 1) * BLK)
                dws_o[g] += _dot(dspb[r, cs], vnb[r, cs], NT)
                bacc[g] += dsp[r, cs]
                cols.append(_dot(ws[g], dspb[r, cs], TN))
            rows.append(jnp.concatenate(cols, axis=1))
        dvn = jnp.concatenate(rows, axis=0)
        _acc_rows(dlg_o, i == 0, dvn * xhat)
        _acc_rows(dlb_o, i == 0, dvn)
        d_o[:, D:] = (_ln_bwd(dvn, xhat, rstd, lg_r[...]) * _gelu_grad(vp)).astype(d_o.dtype)

        @pl.when(i == last)
        def _():
            row = lax.broadcasted_iota(jnp.int32, (BLK, BLK), 0)
            col = lax.broadcasted_iota(jnp.int32, (BLK, BLK), 1)
            ones = jnp.ones((8, BLK), MX)
            for g in range(8):
                dws_o[g] = jnp.where(col <= row, dws_o[g], 0.0)
                a = bacc[g]
                hi = a.astype(MX)
                lo = (a - hi.astype(f32)).astype(MX)
                dbs_o[g:g + 1, :] = (_dot(ones, hi, NT) + _dot(ones, lo, NT))[0:1, :]

    full = lambda shape: pl.BlockSpec(shape, lambda i: (0,) * len(shape))
    return pl.pallas_call(
        body, name="gmlp_bwd", grid=(T // tm,),
        in_specs=[pl.BlockSpec((tm, D), lambda i: (i, 0)), *_uv_specs(), full((8, BLK, BLK)), full((8, BLK, BLK)),
                  full((1, D)), full((1, D))],
        out_specs=[pl.BlockSpec((tm, 2 * D), lambda i: (i, 0)), full((8, BLK, BLK)), full((8, BLK)), full((1, D)), full((1, D))],
        out_shape=[jax.ShapeDtypeStruct((T, 2 * D), MX), jax.ShapeDtypeStruct((8, BLK, BLK), f32),
                   jax.ShapeDtypeStruct((8, BLK), f32), jax.ShapeDtypeStruct((1, D), f32), jax.ShapeDtypeStruct((1, D), f32)],
        scratch_shapes=[pltpu.VMEM((8, BLK, BLK), f32)],
        compiler_params=_cp(("arbitrary",)),
    )(dyc, proj, proj, proj, proj, wst, bsx, lg, lb)


PART_TILES = (6, 6, 3, 3, 3, 4)
PART_START = (0, 6, 12, 15, 18, 21)
TJ = 512


def _part_specs(tm, rows_axis):
    specs = []
    for n, s in zip(PART_TILES, PART_START):
        def imap(*idx, n=n, s=s):
            i, j = idx[rows_axis], idx[1 - rows_axis]
            inside = (j >= s) & (j < s + n)
            return (jnp.where(inside, i, 0), jnp.clip(j - s, 0, n - 1))
        specs.append(pl.BlockSpec((tm, TJ), imap))
    return specs


def _with_part(j, refs, fn):
    for r, n, s in zip(refs, PART_TILES, PART_START):
        @pl.when((j >= s) & (j < s + n))
        def _():
            fn(r[...])


def dx_in(dr1, parts, w):
    T = dr1.shape[0]
    tm = min(1024, T)

    def body(dr_r, p0, p1, p2, p3, p4, p5, w_r, o_r):
        j = pl.program_id(1)

        @pl.when(j == 0)
        def _():
            o_r[...] = ALPHA * dr_r[...]

        def acc(tile):
            o_r[...] += _dot(tile, w_r[...], NT)
        _with_part(j, (p0, p1, p2, p3, p4, p5), acc)

    return pl.pallas_call(
        body, name="dx_in", grid=(T // tm, NIN // TJ),
        in_specs=[pl.BlockSpec((tm, D), lambda i, j: (i, 0))] + _part_specs(tm, 0) + [pl.BlockSpec((D, TJ), lambda i, j: (0, j))],
        out_specs=pl.BlockSpec((tm, D), lambda i, j: (i, 0)),
        out_shape=jax.ShapeDtypeStruct((T, D), f32),
        compiler_params=_cp(("parallel", "arbitrary"), 56),
    )(dr1, *parts, w)


def dw_in(x0t, parts):
    T = x0t.shape[1]
    tk = min(2048, T)

    def body(x_r, p0, p1, p2, p3, p4, p5, o_r):
        j, t = pl.program_id(0), pl.program_id(1)

        @pl.when(t == 0)
        def _():
            o_r[...] = jnp.zeros_like(o_r)

        def acc(tile):
            o_r[...] += _dot(x_r[...], tile)
        _with_part(j, (p0, p1, p2, p3, p4, p5), acc)

    return pl.pallas_call(
        body, name="dw_in", grid=(NIN // TJ, T // tk),
        in_specs=[pl.BlockSpec((D, tk), lambda j, t: (0, t))] + _part_specs(tk, 1),
        out_specs=pl.BlockSpec((D, TJ), lambda j, t: (0, j)),
        out_shape=jax.ShapeDtypeStruct((D, NIN), f32),
        compiler_params=_cp(("parallel", "arbitrary")),
    )(x0t, *parts)


def rope_tables(positions):
    half = HD // 2
    inv_freq = ROPE_THETA ** (-jnp.arange(half, dtype=f32) / half)
    ang = positions.astype(f32)[:, None] * inv_freq
    cos, sin = jnp.cos(ang), jnp.sin(ang)
    return jnp.tile(cos, (1, LANES // half)), jnp.tile(jnp.concatenate([-sin, sin], axis=1), (1, LANES // HD))


def _flat(a):
    return a.reshape(a.shape[0] * a.shape[1], a.shape[2])


def layer_fwd(x0, W, cos_t, sin_t):
    T = x0.shape[0]
    proj = mm_in(x0, W["w_in"])
    ya, yc = mix_ac_fwd(proj, W["conv_w"], W["wst"], W["bsx"], W["gmlp_ln_g"], W["gmlp_ln_b"])
    folded, os_, lses = [], [], []
    for g, (_, d) in enumerate(GROUPS):
        qf, kf, vf = fold_rope(proj, cos_t, sin_t, g, d)
        o, lse = attn_fwd(_flat(qf), _flat(kf), _flat(vf), g, T // d // BLK)
        folded.append((qf, kf, vf))
        os_.append(o.reshape(d, T // d, AO))
        lses.append(lse.reshape(d, T // d, AO))
    yb = combine_fwd(os_, lses)
    mabc, m, r1, x1 = mix_out_fwd(proj, ya, yb, yc, x0, W["p_a"], W["p_b"], W["p_c"], W["w_o"], W["ln1_g"], W["ln1_b"])
    gate, up, hh = ffn_up_fwd(x1, W["w_gate"], W["w_up"])
    r2, x2 = ffn_down_fwd(hh, W["w_down"], x1, W["ln2_g"], W["ln2_b"])
    saved = dict(x0=x0, proj=proj, ya=ya, yb=yb, yc=yc, folded=folded, os=os_, lses=lses, mabc=mabc, m=m, r1=r1,
                 x1=x1, gate=gate, up=up, hh=hh, r2=r2)
    return x2, saved


def layer_bwd(dx2, S, W, cos_t, sin_t):
    T = dx2.shape[0]
    tk = min(2048, T)
    G = {}
    dr2, dgate, dup, G["ln2_g"], G["ln2_b"] = ffn_down_bwd(dx2, S["r2"], W["ln2_g"], W["w_down"], S["gate"], S["up"])
    blk_a = pl.BlockSpec((1, tk, FB), lambda k, t: (k, t, 0))
    row_b = pl.BlockSpec((tk, D), lambda k, t: (t, 0))
    G["w_down"] = tn_matmul("dw_down", S["hh"], dr2, blk_a, row_b, (NCHIP, FB, D),
                            pl.BlockSpec((1, FB, D), lambda k, t: (k, 0, 0)), (NCHIP, T // tk))
    x1t = transpose_cast(S["x1"])
    for nm, dv in (("w_gate", dgate), ("w_up", dup)):
        G[nm] = tn_matmul("d" + nm, x1t, dv, pl.BlockSpec((D, tk), lambda k, t: (0, t)), blk_a, (NCHIP, D, FB),
                          pl.BlockSpec((1, D, FB), lambda k, t: (k, 0, 0)), (NCHIP, T // tk), a_is_t=True)
    dr1, G["ln1_g"], G["ln1_b"] = ffn_up_bwd(dr2, dgate, dup, W["w_gate"], W["w_up"], S["r1"], W["ln1_g"])
    dmabc, dgates, dya, dyb, dyc = mix_out_bwd(dr1, S["proj"], S["mabc"], W["w_o"], W["p_a"], W["p_b"], W["p_c"])
    one = (1, T // tk)
    full_o = pl.BlockSpec((D, D), lambda k, t: (0, 0))
    G["w_o"] = tn_matmul("dw_o", S["m"], dr1, row_b, row_b, (D, D), full_o, one)
    G["p_a"] = tn_matmul("dp_a", S["ya"], dmabc, row_b, pl.BlockSpec((tk, D), lambda k, t: (t, 0)), (D, D), full_o, one)
    G["p_c"] = tn_matmul("dp_c", S["yc"], dmabc, row_b, pl.BlockSpec((tk, D), lambda k, t: (t, 2)), (D, D), full_o, one)
    G["p_b"] = tn_matmul("dp_b", S["yb"], dmabc, pl.BlockSpec((tk, AO), lambda k, t: (t, 0)),
                         pl.BlockSpec((tk, D // NCHIP), lambda k, t: (t, NCHIP + k)), (NCHIP, AO, D // NCHIP),
                         pl.BlockSpec((1, AO, D // NCHIP), lambda k, t: (k, 0, 0)), (NCHIP, T // tk))
    dbch, G["conv_w"] = conv_bwd(dya, S["proj"], W["conv_w"])
    duv, G["w_s"], G["b_s"], G["gmlp_ln_g"], G["gmlp_ln_b"] = gmlp_bwd(
        dyc, S["proj"], W["wst"], W["bsx"], W["gmlp_ln_g"], W["gmlp_ln_b"])
    pre = attn_pre_bwd(dyb, S["os"], S["lses"])
    dqkv = []
    for g, (_, d) in enumerate(GROUPS):
        qf, kf, vf = S["folded"][g]
        dqf, dkf, dvf = attn_bwd(_flat(qf), _flat(kf), _flat(vf), _flat(pre[g]), _flat(S["lses"][g]), _flat(pre[3 + g]),
                                 g, T // d // BLK)
        shp = (d, T // d, AO)
        dqkv.append(unfold_rope_bwd(dqf.reshape(shp), dkf.reshape(shp), dvf.reshape(shp), cos_t, sin_t, g, d))
    parts = (dgates, dbch, *dqkv, duv)
    G["w_in"] = dw_in(transpose_cast(S["x0"]), parts)
    dx0 = dx_in(dr1, parts, W["w_in"])
    return dx0, G


def prep_layer_weights(Wl):
    W = dict(Wl)
    tril = jnp.tril(jnp.ones((BLK, BLK), f32))
    W["wst"] = (Wl["w_s"] * tril[None]).astype(MX)
    W["bsx"] = jnp.broadcast_to(Wl["b_s"][:, :, None], (8, BLK, BLK))
    for n in ("gmlp_ln_g", "gmlp_ln_b", "ln1_g", "ln1_b", "ln2_g", "ln2_b"):
        W[n] = Wl[n].reshape(1, D)
    return W


def local_step(x, positions, target, layers):
    cos_t, sin_t = rope_tables(positions)
    Ws = [prep_layer_weights(Wl) for Wl in layers]
    saved = []
    h = x
    for W in Ws:
        h, S = layer_fwd(h, W, cos_t, sin_t)
        saved.append(S)
    lsum, dh = loss_grad(h, target)
    grads = [None] * len(Ws)
    for l in reversed(range(len(Ws))):
        dh, grads[l] = layer_bwd(dh, saved[l], Ws[l], cos_t, sin_t)
    return lsum, dh, grads


MESH = pl.DeviceIdType.MESH
ANY = pl.BlockSpec(memory_space=pl.ANY)
BIG = ("w_in", "w_gate", "w_up", "w_down", "p_a", "p_b", "p_c", "w_o")
NBIG = len(BIG)


def _place():
    x, y, c = lax.axis_index("x"), lax.axis_index("y"), lax.axis_index("c")
    return x, y, c, 2 * x + y


def _rcopy(src, dst, send, recv, dev):
    return pltpu.make_async_remote_copy(src_ref=src, dst_ref=dst, send_sem=send, recv_sem=recv, device_id=dev,
                                        device_id_type=MESH)


def _cols(ref, k, width):
    start = k * width if isinstance(k, int) else pl.multiple_of(k * width, LANES)
    return ref.at[:, pl.ds(start, width)]


CHUNK_BYTES = 1 << 20


def _pieces(shape, itemsize, nbytes=CHUNK_BYTES):
    rows, cols = shape[-2], shape[-1]
    per = max(16, nbytes // (cols * itemsize) // 16 * 16)
    out = []
    for lead in (range(shape[0]) if len(shape) == 3 else (None,)):
        for r in range(0, rows, per):
            sl = (pl.ds(r, min(per, rows - r)), slice(None))
            out.append(sl if lead is None else (lead,) + sl)
    return out


def _start_pieces(src, dst, make, nbytes=CHUNK_BYTES):
    for idx in _pieces(src.shape, jnp.dtype(src.dtype).itemsize, nbytes):
        make(src.at[idx], dst.at[idx]).start()


def gather_weights(shards):
    n = len(shards)

    def body(*refs):
        srcs, dsts = refs[:n], refs[n:2 * n]
        send, recv, own_send, own_recv = refs[2 * n:]
        x, y, c, k = _place()
        sib = (x, y, 1 - c)
        chips = [(1 - x, y), (x, 1 - y), (1 - x, 1 - y)]

        def slot(a, layer, pos):
            if a == 0:
                return _cols(dsts[0].at[layer], pos, WIN_SHARD)
            return dsts[a].at[layer, pos]

        def ici(a, j, src, dst):
            return _rcopy(src, dst, send.at[a, j], recv.at[a, j], (*chips[j], c))

        def d2d(a, j, src, dst):
            return _rcopy(src, dst, send.at[a, 3 + j], recv.at[a, 3 + j], sib)

        def own(a, layer, src, dst):
            return _rcopy(src, dst, own_send.at[a, layer], own_recv.at[a, layer], sib)

        for a in range(n):
            for j in range(3):
                _start_pieces(srcs[a].at[c], slot(a, c, k), functools.partial(ici, a, j))
        for a in range(n):
            for layer in range(DEPTH):
                _start_pieces(srcs[a].at[layer], slot(a, layer, k), functools.partial(own, a, layer))
        for a in range(n):
            for j, (cx, cy) in enumerate(chips):
                landed = slot(a, c, 2 * cx + cy)
                ici(a, j, landed, landed).wait_recv()
                _start_pieces(landed, landed, functools.partial(d2d, a, j))
        for a in range(n):
            for j, (cx, cy) in enumerate(chips):
                passed = slot(a, 1 - c, 2 * cx + cy)
                d2d(a, j, passed, passed).wait_recv()
                landed = slot(a, c, 2 * cx + cy)
                d2d(a, j, landed, landed).wait_send()
                ici(a, j, srcs[a].at[c], slot(a, c, k)).wait_send()
            for layer in range(DEPTH):
                own(a, layer, srcs[a].at[layer], slot(a, layer, k)).wait()

    outs = [jax.ShapeDtypeStruct((DEPTH, D, NIN), shards[0].dtype)]
    outs += [jax.ShapeDtypeStruct((DEPTH, NCHIP) + s.shape[1:], s.dtype) for s in shards[1:]]
    return pl.pallas_call(
        body, name="gather_weights", in_specs=[ANY] * n, out_specs=[ANY] * n, out_shape=outs,
        scratch_shapes=[pltpu.SemaphoreType.DMA((n, 6)), pltpu.SemaphoreType.DMA((n, 6)),
                        pltpu.SemaphoreType.DMA((n, DEPTH)), pltpu.SemaphoreType.DMA((n, DEPTH))],
    )(*shards)


def _half(ref, h, a):
    rows = ref.shape[-2] // 2
    start = pl.multiple_of(h * rows, 16)
    if a == 0:
        return ref.at[pl.ds(start, rows), :]
    return ref.at[:, pl.ds(start, rows), :]


def rs_pair(l, grads):
    def body(*refs):
        g, theirs = refs[:NBIG], refs[NBIG:2 * NBIG]
        send, recv = refs[2 * NBIG:]
        x, y, c, _ = _place()

        def give(a, s, d):
            return _rcopy(s, d, send.at[a], recv.at[a], (x, y, 1 - c))

        for a in range(NBIG):
            _start_pieces(_half(g[a], 1 - c, a), theirs[a], functools.partial(give, a))
        for a in range(NBIG):
            give(a, _half(g[a], 1 - c, a), theirs[a]).wait()

    def hshape(s, a):
        return (s[0] // 2, s[1]) if a == 0 else (s[0], s[1] // 2, s[2])

    outs = [jax.ShapeDtypeStruct(hshape(g.shape, a), g.dtype) for a, g in enumerate(grads)]
    return pl.pallas_call(
        body, name=f"rs_pair{l}", in_specs=[ANY] * NBIG, out_specs=[ANY] * NBIG, out_shape=outs,
        scratch_shapes=[pltpu.SemaphoreType.DMA((NBIG,))] * 2,
    )(*grads)


def rs_chips(l, sums):
    def body(*refs):
        s, land = refs[:NBIG], refs[NBIG:2 * NBIG]
        send, recv = refs[2 * NBIG:]
        x, y, c, me = _place()

        def piece(a, k):
            return _cols(s[a], k, WIN_SHARD) if a == 0 else s[a].at[k]

        def give(a, k, src, dst):
            return _rcopy(src, dst, send.at[a, k], recv.at[a, me], (k // 2, k % 2, c))

        for k in range(NCHIP):
            @pl.when(me != k)
            def _():
                for a in range(NBIG):
                    _start_pieces(piece(a, k), land[a].at[me], functools.partial(give, a, k))
        for k in range(NCHIP):
            @pl.when(me != k)
            def _():
                for a in range(NBIG):
                    give(a, k, piece(a, k), land[a].at[me]).wait_send()
                    _rcopy(piece(a, k), land[a].at[k], send.at[a, k], recv.at[a, k], (k // 2, k % 2, c)).wait_recv()

    def pshape(s, a):
        return (NCHIP, s[0], WIN_SHARD) if a == 0 else s

    outs = [jax.ShapeDtypeStruct(pshape(v.shape, a), v.dtype) for a, v in enumerate(sums)]
    return pl.pallas_call(
        body, name=f"rs_chips{l}", in_specs=[ANY] * NBIG, out_specs=[ANY] * NBIG, out_shape=outs,
        scratch_shapes=[pltpu.SemaphoreType.DMA((NBIG, NCHIP)), pltpu.SemaphoreType.DMA((NBIG, NCHIP))],
    )(*sums)


def rs_join(l, halves):
    def body(*refs):
        h, other = refs[:NBIG], refs[NBIG:2 * NBIG]
        send, recv = refs[2 * NBIG:]
        x, y, c, _ = _place()

        def give(a, s, d):
            return _rcopy(s, d, send.at[a], recv.at[a], (x, y, 1 - c))

        for a in range(NBIG):
            _start_pieces(h[a], other[a], functools.partial(give, a))
        for a in range(NBIG):
            give(a, h[a], other[a]).wait()

    outs = [jax.ShapeDtypeStruct(v.shape, v.dtype) for v in halves]
    return pl.pallas_call(
        body, name=f"rs_join{l}", in_specs=[ANY] * NBIG, out_specs=[ANY] * NBIG, out_shape=outs,
        scratch_shapes=[pltpu.SemaphoreType.DMA((NBIG,))] * 2,
    )(*halves)


def _row_tile(rows, cols, itemsize=4, target=2 << 20):
    best = 8
    for t in range(8, rows + 1, 8):
        if rows % t == 0 and t * cols * itemsize <= target:
            best = t
    return best


GRAD_WIRE = jnp.bfloat16


def add_n(name, terms, out_dtype=f32):
    shape = terms[0].shape
    cols = shape[-1]
    rows = math.prod(shape[:-1])
    tr = _row_tile(rows, cols)

    def body(*refs):
        acc = refs[0][...]
        for r in refs[1:-1]:
            acc = acc + r[...]
        refs[-1][...] = acc.astype(out_dtype)

    tile = pl.BlockSpec((tr, cols), lambda i: (i, 0))
    out = pl.pallas_call(
        body, name=name, grid=(rows // tr,), in_specs=[tile] * len(terms), out_specs=tile,
        out_shape=jax.ShapeDtypeStruct((rows, cols), out_dtype), compiler_params=_cp(("parallel",)),
    )(*[t.reshape(rows, cols) for t in terms])
    return out.reshape(shape)


def add_chips(name, land, own):
    _, rows, cols = land.shape
    tr = _row_tile(rows, cols, target=1 << 20)

    def body(land_r, own_r, o_r):
        me = 2 * lax.axis_index("x") + lax.axis_index("y")
        for k in range(NCHIP):
            @pl.when(me == k)
            def _():
                acc = None
                for j in range(NCHIP):
                    t = (own_r[...] if j == k else land_r[j]).astype(f32)
                    acc = t if acc is None else acc + t
                o_r[...] = acc

    tile = pl.BlockSpec((tr, cols), lambda i: (i, 0))
    return pl.pallas_call(
        body, name=name, grid=(rows // tr,), in_specs=[pl.BlockSpec((NCHIP, tr, cols), lambda i: (0, i, 0)), tile],
        out_specs=tile, out_shape=jax.ShapeDtypeStruct((rows, cols), f32), compiler_params=_cp(("parallel",)),
    )(land, own)


def reduce_scatter_layer(l, G):
    c = lax.axis_index("c")
    me = 2 * lax.axis_index("x") + lax.axis_index("y")
    grads = [G[n] if G[n].ndim == 3 or n == "w_in" else G[n].reshape(NCHIP, D // NCHIP, D) for n in BIG]
    theirs = rs_pair(l, grads)
    sums = []
    for n, g, t in zip(BIG, grads, theirs):
        rows = g.shape[-2] // 2
        mine = lax.dynamic_slice_in_dim(g, c * rows, rows, axis=g.ndim - 2)
        sums.append(add_n(f"rs_add_pair{l}_{n}", [mine, t], GRAD_WIRE))
    landed = rs_chips(l, sums)
    halves = []
    for a, (n, s, v) in enumerate(zip(BIG, sums, landed)):
        own = lax.dynamic_slice_in_dim(s, me * WIN_SHARD, WIN_SHARD, axis=1) if a == 0 else \
            lax.dynamic_index_in_dim(s, me, 0, keepdims=False)
        halves.append(add_chips(f"rs_add_chips{l}_{n}", v, own))
    return dict(zip(BIG, zip(halves, rs_join(l, halves))))


NDEV = 8


def allreduce_small(pack):
    rows = pack.shape[0]

    def body(p_ref, o_ref, buf, send, recv):
        x, y, c, _ = _place()
        me = 4 * x + 2 * y + c
        buf[me] = p_ref[...]

        def give(r, s, d):
            return _rcopy(s, d, send.at[r - 1], recv.at[r - 1], (x ^ (r >> 2), y ^ ((r >> 1) & 1), c ^ (r & 1)))

        for r in range(1, NDEV):
            _start_pieces(p_ref, buf.at[me], functools.partial(give, r), 128 << 10)
        for r in range(1, NDEV):
            give(r, p_ref, buf.at[me]).wait_send()
            src = 4 * (x ^ (r >> 2)) + 2 * (y ^ ((r >> 1) & 1)) + (c ^ (r & 1))
            give(r, p_ref, buf.at[src]).wait_recv()
        acc = buf[0]
        for d in range(1, NDEV):
            acc = acc + buf[d]
        o_ref[...] = acc

    vm = pl.BlockSpec(memory_space=pltpu.VMEM)
    return pl.pallas_call(
        body, name="allreduce_small", in_specs=[vm], out_specs=vm, out_shape=jax.ShapeDtypeStruct(pack.shape, f32),
        scratch_shapes=[pltpu.VMEM((NDEV, rows, LANES), f32), pltpu.SemaphoreType.DMA((NDEV - 1,)),
                        pltpu.SemaphoreType.DMA((NDEV - 1,))],
        compiler_params=pltpu.CompilerParams(vmem_limit_bytes=40 << 20),
    )(pack)


def _adamw_math(w, g, m, v):
    m = ADAM_B1 * m + (1.0 - ADAM_B1) * g
    v = ADAM_B2 * v + (1.0 - ADAM_B2) * (g * g)
    m_hat = m / (1.0 - ADAM_B1 ** ADAM_STEP)
    v_hat = v / (1.0 - ADAM_B2 ** ADAM_STEP)
    return -ADAM_LR * (m_hat / (jnp.sqrt(v_hat) + ADAM_EPS) + ADAM_WD * w), m, v


def adamw_big(name, halves, w, m, v):
    _, R, C = w.shape
    tr = _row_tile(R // 2, C, target=1 << 20)
    nt = R // 2 // tr

    def body(a0, b0, a1, b1, w_r, m_r, v_r, g_o, d_o, m_o, v_o):
        mine = pl.program_id(1) == lax.axis_index("c")
        g = jnp.where(pl.program_id(0) == 0, jnp.where(mine, a0[...], b0[...]), jnp.where(mine, a1[...], b1[...]))
        g_o[...] = g
        d_o[...], m_o[...], v_o[...] = _adamw_math(w_r[...], g, m_r[...], v_r[...])

    stk = pl.BlockSpec((None, tr, C), lambda l, h, i: (l, h * nt + i, 0))
    lay0 = pl.BlockSpec((tr, C), lambda l, h, i: (jnp.where(l == 0, i, nt - 1), 0))
    lay1 = pl.BlockSpec((tr, C), lambda l, h, i: (jnp.where(l == 0, 0, i), 0))
    return pl.pallas_call(
        body, name=name, grid=(DEPTH, 2, nt),
        in_specs=[lay0, lay0, lay1, lay1, stk, stk, stk],
        out_specs=[stk] * 4, out_shape=[jax.ShapeDtypeStruct(w.shape, f32)] * 4,
        compiler_params=_cp(("arbitrary", "arbitrary", "arbitrary")),
    )(*halves[0], *halves[1], w, m, v)


def adamw_small(name, g, w, m, v):
    def body(g_r, w_r, m_r, v_r, d_o, m_o, v_o):
        d_o[...], m_o[...], v_o[...] = _adamw_math(w_r[...], g_r[...], m_r[...], v_r[...])

    return pl.pallas_call(body, name=name, out_shape=[jax.ShapeDtypeStruct(w.shape, f32)] * 3)(g, w, m, v)


WEIGHTS = ("w_in", "conv_w", "gmlp_ln_g", "gmlp_ln_b", "w_s", "b_s", "p_a", "p_b", "p_c", "w_o", "ln1_g", "ln1_b",
           "w_gate", "w_up", "w_down", "ln2_g", "ln2_b")
VECS = ("ln1_g", "ln1_b", "ln2_g", "ln2_b", "gmlp_ln_g", "gmlp_ln_b")
ROWS_VEC, ROWS_BS, ROWS_WS, ROWS_CONV = D // LANES, 8, 8 * BLK, 3 * D // LANES
ROWS_LAYER = len(VECS) * ROWS_VEC + ROWS_BS + ROWS_WS + ROWS_CONV


def _pack_small(per_layer, tail):
    parts = []
    for P in per_layer:
        parts += [P[n].reshape(ROWS_VEC, LANES) for n in VECS]
        parts += [P["b_s"].reshape(ROWS_BS, LANES), P["w_s"].reshape(ROWS_WS, LANES), P["conv_w"].reshape(ROWS_CONV, LANES)]
    return jnp.concatenate(parts + [tail], axis=0)


def _unpack_small(pack):
    out = []
    for l in range(DEPTH):
        r = l * ROWS_LAYER
        P = {}
        for n in VECS:
            P[n] = pack[r:r + ROWS_VEC].reshape(D)
            r += ROWS_VEC
        P["b_s"] = pack[r:r + ROWS_BS].reshape(8, BLK)
        r += ROWS_BS
        P["w_s"] = pack[r:r + ROWS_WS].reshape(8, BLK, BLK)
        r += ROWS_WS
        P["conv_w"] = pack[r:r + ROWS_CONV].reshape(3, D)
        out.append(P)
    return out, pack[DEPTH * ROWS_LAYER:]


def kernel(x, positions, w_in, conv_w, gmlp_ln_g, gmlp_ln_b, w_s, b_s, p_a, p_b, p_c, w_o, ln1_g, ln1_b, w_gate, w_up, w_down, ln2_g, ln2_b, loss_target, m_w_in, m_conv_w, m_gmlp_ln_g, m_gmlp_ln_b, m_w_s, m_b_s, m_p_a, m_p_b, m_p_c, m_w_o, m_ln1_g, m_ln1_b, m_w_gate, m_w_up, m_w_down, m_ln2_g, m_ln2_b, v_w_in, v_conv_w, v_gmlp_ln_g, v_gmlp_ln_b, v_w_s, v_b_s, v_p_a, v_p_b, v_p_c, v_w_o, v_ln1_g, v_ln1_b, v_w_gate, v_w_up, v_w_down, v_ln2_g, v_ln2_b):
    Wt = dict(w_in=w_in, conv_w=conv_w, gmlp_ln_g=gmlp_ln_g, gmlp_ln_b=gmlp_ln_b, w_s=w_s, b_s=b_s, p_a=p_a, p_b=p_b,
              p_c=p_c, w_o=w_o, ln1_g=ln1_g, ln1_b=ln1_b, w_gate=w_gate, w_up=w_up, w_down=w_down, ln2_g=ln2_g, ln2_b=ln2_b)
    Mt = dict(w_in=m_w_in, conv_w=m_conv_w, gmlp_ln_g=m_gmlp_ln_g, gmlp_ln_b=m_gmlp_ln_b, w_s=m_w_s, b_s=m_b_s, p_a=m_p_a,
              p_b=m_p_b, p_c=m_p_c, w_o=m_w_o, ln1_g=m_ln1_g, ln1_b=m_ln1_b, w_gate=m_w_gate, w_up=m_w_up,
              w_down=m_w_down, ln2_g=m_ln2_g, ln2_b=m_ln2_b)
    Vt = dict(w_in=v_w_in, conv_w=v_conv_w, gmlp_ln_g=v_gmlp_ln_g, gmlp_ln_b=v_gmlp_ln_b, w_s=v_w_s, b_s=v_b_s, p_a=v_p_a,
              p_b=v_p_b, p_c=v_p_c, w_o=v_w_o, ln1_g=v_ln1_g, ln1_b=v_ln1_b, w_gate=v_w_gate, w_up=v_w_up,
              w_down=v_w_down, ln2_g=v_ln2_g, ln2_b=v_ln2_b)
    chip = 2 * lax.axis_index("x") + lax.axis_index("y")
    cw = D // NCHIP

    full = gather_weights([Wt[n].astype(MX) for n in BIG] + [conv_w])
    layers = []
    for l in range(DEPTH):
        Wl = dict(zip(BIG, (f[l] for f in full[:NBIG])))
        for n in ("p_a", "p_c", "w_o"):
            Wl[n] = Wl[n].reshape(D, D)
        Wl["conv_w"] = full[NBIG][l].transpose(1, 0, 2).reshape(3, D)
        for n in VECS + ("w_s", "b_s"):
            Wl[n] = Wt[n][l]
        layers.append(Wl)

    lsum, grad_x, grads = local_step(x[0], positions[0], loss_target[0], layers)

    red = [None] * DEPTH
    for l in reversed(range(DEPTH)):
        red[l] = reduce_scatter_layer(l, grads[l])
    pack = _pack_small([{n: (g[n] if n not in VECS else g[n]) for n in VECS + ("b_s", "w_s", "conv_w")} for g in grads], lsum)
    small, tail = _unpack_small(allreduce_small(pack))
    loss = tail[0, 0]

    G, DW, NM, NV = {}, {}, {}, {}
    for n in BIG:
        G[n], DW[n], NM[n], NV[n] = adamw_big("adamw_" + n, (red[0][n], red[1][n]), Wt[n], Mt[n], Vt[n])
    zc = jnp.zeros((3, D), f32)
    wp = _pack_small([{**{n: Wt[n][l] for n in VECS + ("b_s", "w_s")}, "conv_w": zc} for l in range(DEPTH)], jnp.zeros((8, LANES), f32))
    mp = _pack_small([{**{n: Mt[n][l] for n in VECS + ("b_s", "w_s")}, "conv_w": zc} for l in range(DEPTH)], jnp.zeros((8, LANES), f32))
    vp = _pack_small([{**{n: Vt[n][l] for n in VECS + ("b_s", "w_s")}, "conv_w": zc} for l in range(DEPTH)], jnp.ones((8, LANES), f32))
    gp = _pack_small(small, jnp.zeros((8, LANES), f32))
    outs = [_unpack_small(a)[0] for a in adamw_small("adamw_small", gp, wp, mp, vp)]
    for n in VECS + ("b_s", "w_s"):
        G[n] = jnp.stack([small[l][n] for l in range(DEPTH)])
        DW[n], NM[n], NV[n] = (jnp.stack([o[l][n] for l in range(DEPTH)]) for o in outs)
    gconv = jnp.stack([lax.dynamic_slice(small[l]["conv_w"], (0, chip * cw), (3, cw)) for l in range(DEPTH)])
    G["conv_w"] = gconv
    flat = lambda a: a.reshape(DEPTH * 3, cw)
    d, m2, v2 = adamw_small("adamw_conv", flat(gconv), flat(conv_w), flat(m_conv_w), flat(v_conv_w))
    DW["conv_w"], NM["conv_w"], NV["conv_w"] = (a.reshape(DEPTH, 3, cw) for a in (d, m2, v2))

    return (loss, grad_x[None], *[G[n] for n in WEIGHTS], *[DW[n] for n in WEIGHTS], *[NM[n] for n in WEIGHTS],
            *[NV[n] for n in WEIGHTS])
```

```python
import functools
import math

import jax
import jax.numpy as jnp
from jax import lax
from jax.experimental import pallas as pl
from jax.experimental.pallas import tpu as pltpu

D = 1024
NIN = 12800
DFF = 2816
NCHIP = 4
FB = DFF // NCHIP
WIN_SHARD = NIN // NCHIP
DEPTH = 2
GROUPS = ((128, 1), (512, 4), (2048, 16))
HD = 64
BLK = 128
AO = 512
ALPHA = (2 * DEPTH) ** 0.25
EPS = 1e-5
ROPE_THETA = 10000.0
LANES = 128
NEG = -1e30

C_GATES, C_BCH, C_QKV, C_UV = 0, 3 * D, 6 * D, 6 * D + 9 * AO

MX = jnp.bfloat16
ACT = jnp.bfloat16

ADAM_LR, ADAM_B1, ADAM_B2, ADAM_EPS, ADAM_WD, ADAM_STEP = 0.001, 0.9, 0.999, 1e-08, 0.01, 10

f32 = jnp.float32
NT = (((1,), (1,)), ((), ()))
TN = (((0,), (0,)), ((), ()))


def _cp(sem, vmem_mb=48):
    return pltpu.CompilerParams(dimension_semantics=sem, vmem_limit_bytes=vmem_mb << 20)


def _dot(a, b, dims=None):
    if dims is None:
        return jnp.dot(a, b, preferred_element_type=f32)
    return lax.dot_general(a, b, dims, preferred_element_type=f32)


def _ln_stats(r):
    mu = jnp.mean(r, axis=-1, keepdims=True)
    xc = r - mu
    var = jnp.mean(xc * xc, axis=-1, keepdims=True)
    rstd = lax.rsqrt(var + EPS)
    return xc * rstd, rstd


def _ln_bwd(dy, xhat, rstd, g):
    dxh = dy * g
    return rstd * (dxh - jnp.mean(dxh, axis=-1, keepdims=True) - xhat * jnp.mean(dxh * xhat, axis=-1, keepdims=True))


def _gelu(x):
    return 0.5 * x * (1.0 + lax.erf(x * (1.0 / math.sqrt(2.0))))


def _gelu_grad(x):
    return 0.5 * (1.0 + lax.erf(x * (1.0 / math.sqrt(2.0)))) + x * jnp.exp(-0.5 * x * x) * (1.0 / math.sqrt(2.0 * math.pi))


def _sigmoid(x):
    return 0.5 * jnp.tanh(0.5 * x) + 0.5


def _acc_rows(o_ref, first, val):
    @pl.when(first)
    def _():
        o_ref[...] = jnp.zeros_like(o_ref)
    o_ref[...] += jnp.sum(val, axis=0, keepdims=True)


def mm_in(x, w):
    T = x.shape[0]
    tm, tn = min(1024, T), 1280

    def body(x_ref, w_ref, o_ref, xb):
        @pl.when(pl.program_id(1) == 0)
        def _():
            xb[...] = x_ref[...].astype(MX)
        o_ref[...] = _dot(xb[...], w_ref[...]).astype(o_ref.dtype)

    return pl.pallas_call(
        body, name="mm_in", grid=(T // tm, NIN // tn),
        in_specs=[pl.BlockSpec((tm, D), lambda i, j: (i, 0)), pl.BlockSpec((D, tn), lambda i, j: (0, j))],
        out_specs=pl.BlockSpec((tm, tn), lambda i, j: (i, j)),
        out_shape=jax.ShapeDtypeStruct((T, NIN), ACT),
        scratch_shapes=[pltpu.VMEM((tm, D), MX)],
        compiler_params=_cp(("parallel", "arbitrary")),
    )(x, w)


HALO = 16
TM_AC = 256


def _uv_specs():
    return [pl.BlockSpec((TM_AC, 512), functools.partial(lambda i, j: (i, j), j=C_UV // 512 + j)) for j in range(4)]


def _gmlp_fwd(up, vp, ws_ref, bs_ref, lg, lb):
    u = _gelu(up)
    xhat, rstd = _ln_stats(_gelu(vp))
    vn = xhat * lg + lb
    vnb = vn.astype(MX)
    rows = []
    for c in range(up.shape[0] // BLK):
        r = slice(c * BLK, (c + 1) * BLK)
        rows.append(jnp.concatenate(
            [_dot(ws_ref[g], vnb[r, g * BLK:(g + 1) * BLK]) + bs_ref[g] for g in range(8)], axis=1))
    return u, vn, xhat, rstd, jnp.concatenate(rows, axis=0)


def mix_ac_fwd(proj, conv_w, wst, bsx, lg, lb):
    T = proj.shape[0]
    tm = TM_AC

    def body(bch, halo, u0, u1, v0, v1, cw, ws, bs, lg_ref, lb_ref, ya, yc, zs):
        i = pl.program_id(0)
        pb = bch[...].astype(f32)
        z = pb[:, D:2 * D] * pb[:, 2 * D:]
        hz = halo[:, :D].astype(f32) * halo[:, D:].astype(f32)
        zs[0:HALO, :] = jnp.where(i > 0, hz, 0.0)
        zs[HALO:HALO + tm, :] = z
        cv = cw[0:1, :] * zs[HALO - 2:HALO - 2 + tm, :] + cw[1:2, :] * zs[HALO - 1:HALO - 1 + tm, :] + cw[2:3, :] * z
        ya[...] = (pb[:, :D] * cv).astype(ya.dtype)
        up = jnp.concatenate([u0[...], u1[...]], axis=1).astype(f32)
        vp = jnp.concatenate([v0[...], v1[...]], axis=1).astype(f32)
        u, _, _, _, sp = _gmlp_fwd(up, vp, ws, bs, lg_ref[...], lb_ref[...])
        yc[...] = (u * sp).astype(yc.dtype)

    full = lambda shape: pl.BlockSpec(shape, lambda i: (0,) * len(shape))
    return pl.pallas_call(
        body, name="mix_ac_fwd", grid=(T // tm,),
        in_specs=[pl.BlockSpec((tm, 3 * D), lambda i: (i, 1)),
                  pl.BlockSpec((HALO, 2 * D), lambda i: (jnp.maximum(i * (tm // HALO) - 1, 0), 2)),
                  *_uv_specs(), full((3, D)), full((8, BLK, BLK)), full((8, BLK, BLK)), full((1, D)), full((1, D))],
        out_specs=[pl.BlockSpec((tm, D), lambda i: (i, 0))] * 2,
        out_shape=[jax.ShapeDtypeStruct((T, D), MX)] * 2,
        scratch_shapes=[pltpu.VMEM((HALO + tm, D), f32)],
        compiler_params=_cp(("parallel",)),
    )(proj, proj, proj, proj, proj, proj, conv_w, wst, bsx, lg, lb)


def _swap_halves(x):
    lane = lax.broadcasted_iota(jnp.int32, x.shape, 1)
    return jnp.where((lane % HD) < HD // 2, pltpu.roll(x, x.shape[1] - HD // 2, 1), pltpu.roll(x, HD // 2, 1))


def _tile4(t):
    return jnp.concatenate([t] * (AO // LANES), axis=1)


TM_FOLD = 512


def _fold_out(nat, x, out_ref, d):
    if d == 1:
        out_ref[0] = x.astype(out_ref.dtype)
        return
    rows = x.shape[0] // d
    for j in range(AO // LANES):
        nat[j] = x[:, j * LANES:(j + 1) * LANES]
    for r in range(d):
        out_ref[r] = jnp.concatenate(
            [nat.at[j][pl.ds(r, rows, stride=d), :] for j in range(AO // LANES)], axis=1).astype(out_ref.dtype)


def _unfold_in(nat, in_ref, d):
    if d == 1:
        return in_ref[0].astype(f32)
    rows = in_ref.shape[1]
    for r in range(d):
        v = in_ref[r].astype(f32)
        for j in range(AO // LANES):
            nat.at[j][pl.ds(r, rows, stride=d), :] = v[:, j * LANES:(j + 1) * LANES]
    return jnp.concatenate([nat[j] for j in range(AO // LANES)], axis=1)


def fold_rope(proj, cos_t, sin_t, g, d):
    T = proj.shape[0]
    tm = TM_FOLD
    rows = tm // d

    def body(x_ref, c_ref, s_ref, q_o, k_o, v_o, nat):
        cos, sin = _tile4(c_ref[...]), _tile4(s_ref[...])
        for part, out, scale in ((0, q_o, HD ** -0.5), (1, k_o, 1.0), (2, v_o, None)):
            x = x_ref[:, part * AO:(part + 1) * AO].astype(f32)
            if scale is not None:
                x = (x * cos + _swap_halves(x) * sin) * scale
            _fold_out(nat, x, out, d)

    fold_spec = pl.BlockSpec((d, rows, AO), lambda i: (0, i, 0))
    return pl.pallas_call(
        body, name=f"fold_rope{g}", grid=(T // tm,),
        in_specs=[pl.BlockSpec((tm, 3 * AO), lambda i: (i, C_QKV // (3 * AO) + g)),
                  pl.BlockSpec((tm, LANES), lambda i: (i, 0)), pl.BlockSpec((tm, LANES), lambda i: (i, 0))],
        out_specs=[fold_spec] * 3,
        out_shape=[jax.ShapeDtypeStruct((d, T // d, AO), MX)] * 3,
        scratch_shapes=[pltpu.VMEM((AO // LANES, tm, LANES), f32)],
        compiler_params=_cp(("parallel",)),
    )(proj, cos_t, sin_t)


def _stack_heads(x):
    lane = lax.broadcasted_iota(jnp.int32, x.shape, 1)
    z = jnp.zeros_like(x)
    return jnp.concatenate([jnp.where(lane < HD, x, z), jnp.where(lane >= HD, x, z)], axis=0)


def _unstack_heads(y):
    lane = lax.broadcasted_iota(jnp.int32, (BLK, LANES), 1)
    return jnp.where(lane < HD, y[:BLK], y[BLK:])


def _window_masks():
    row = lax.broadcasted_iota(jnp.int32, (2 * BLK, 2 * BLK), 0) % BLK
    col = lax.broadcasted_iota(jnp.int32, (2 * BLK, 2 * BLK), 1)
    return (col < BLK) & (col >= row), (col >= BLK) & (col - BLK <= row)


def _two_blocks(ref, b):
    r0 = pl.multiple_of(b * BLK, BLK)
    rp = pl.multiple_of(jnp.maximum(b - 1, 0) * BLK, BLK)
    return jnp.concatenate([ref[pl.ds(rp, BLK), :], ref[pl.ds(r0, BLK), :]], axis=0)


def attn_fwd(qf, kf, vf, g, nb):
    T = qf.shape[0]

    def body(q_ref, k_ref, v_ref, o_ref, l_ref):
        prev_m, cur_m = _window_masks()

        def step(b, carry):
            r0 = pl.multiple_of(b * BLK, BLK)
            qs = _stack_heads(q_ref[pl.ds(r0, BLK), :])
            s = _dot(qs, _two_blocks(k_ref, b), NT)
            s = jnp.where(cur_m | (prev_m & ((b % nb) != 0)), s, NEG)
            m = jnp.max(s, axis=-1, keepdims=True)
            p = jnp.exp(s - m)
            l = jnp.sum(p, axis=-1, keepdims=True)
            o = _dot(p.astype(MX), _two_blocks(v_ref, b)) / l
            o_ref[pl.ds(r0, BLK), :] = _unstack_heads(o)
            l_ref[pl.ds(r0, BLK), :] = _unstack_heads(jnp.broadcast_to(m + jnp.log(l), (2 * BLK, LANES)))
            return carry

        lax.fori_loop(0, T // BLK, step, 0, unroll=4)

    spec = pl.BlockSpec((T, LANES), lambda j: (0, j))
    return pl.pallas_call(
        body, name=f"attn_fwd{g}", grid=(AO // LANES,),
        in_specs=[spec] * 3, out_specs=[spec] * 2,
        out_shape=[jax.ShapeDtypeStruct((T, AO), f32)] * 2,
        compiler_params=_cp(("parallel",), 56),
    )(qf, kf, vf)


def _group_weights(lses):
    m = jnp.maximum(jnp.maximum(lses[0], lses[1]), lses[2])
    e = [jnp.exp(l - m) for l in lses]
    inv = 1.0 / (e[0] + e[1] + e[2])
    return [x * inv for x in e]


def _fold_specs(T, tm):
    specs = []
    for _, d in GROUPS:
        specs.append(pl.BlockSpec((d, tm // d, AO), lambda i: (0, i, 0)))
    return specs


def combine_fwd(os_, lses):
    T = os_[0].shape[0] * os_[0].shape[1]
    tm = TM_FOLD

    def body(o0, o1, o2, l0, l1, l2, y_ref, nat):
        o = [_unfold_in(nat, r, d) for r, (_, d) in zip((o0, o1, o2), GROUPS)]
        ls = [_unfold_in(nat, r, d) for r, (_, d) in zip((l0, l1, l2), GROUPS)]
        w = _group_weights(ls)
        y_ref[...] = (w[0] * o[0] + w[1] * o[1] + w[2] * o[2]).astype(y_ref.dtype)

    specs = _fold_specs(T, tm)
    return pl.pallas_call(
        body, name="combine_fwd", grid=(T // tm,),
        in_specs=specs + specs, out_specs=pl.BlockSpec((tm, AO), lambda i: (i, 0)),
        out_shape=jax.ShapeDtypeStruct((T, AO), MX),
        scratch_shapes=[pltpu.VMEM((AO // LANES, tm, LANES), f32)],
        compiler_params=_cp(("parallel",)),
    )(*os_, *lses)


TM_MIX = 256


def mix_out_fwd(proj, ya, yb, yc, x0, pa, pb, pc, wo, g1, b1):
    T = x0.shape[0]
    tm = min(TM_MIX, T)

    def body(gt, ya_r, yb_r, yc_r, x0_r, pa_r, pb_r, pc_r, wo_r, g_r, b_r, mabc, m_o, r1_o, x1_o):
        ma = _dot(ya_r[...], pa_r[...])
        ybv = yb_r[...]
        mb = jnp.concatenate([_dot(ybv, pb_r[k]) for k in range(NCHIP)], axis=1)
        mc = _dot(yc_r[...], pc_r[...])
        m = jnp.zeros((tm, D), f32)
        for j, mm in enumerate((ma, mb, mc)):
            mabc[:, j * D:(j + 1) * D] = mm.astype(mabc.dtype)
            m = m + _sigmoid(gt[:, j * D:(j + 1) * D].astype(f32)) * mm
        mb16 = m.astype(MX)
        m_o[...] = mb16
        r1 = ALPHA * x0_r[...] + _dot(mb16, wo_r[...])
        r1_o[...] = r1
        xhat, _ = _ln_stats(r1)
        x1_o[...] = xhat * g_r[...] + b_r[...]

    full = lambda shape: pl.BlockSpec(shape, lambda i: (0,) * len(shape))
    tile = lambda w: pl.BlockSpec((tm, w), lambda i: (i, 0))
    return pl.pallas_call(
        body, name="mix_out_fwd", grid=(T // tm,),
        in_specs=[tile(3 * D), tile(D), tile(AO), tile(D), tile(D), full((D, D)), full((NCHIP, AO, D // NCHIP)),
                  full((D, D)), full((D, D)), full((1, D)), full((1, D))],
        out_specs=[tile(3 * D), tile(D), tile(D), tile(D)],
        out_shape=[jax.ShapeDtypeStruct((T, 3 * D), MX), jax.ShapeDtypeStruct((T, D), MX),
                   jax.ShapeDtypeStruct((T, D), f32), jax.ShapeDtypeStruct((T, D), f32)],
        compiler_params=_cp(("parallel",), 56),
    )(proj, ya, yb, yc, x0, pa, pb, pc, wo, g1, b1)


TM_FF = 512


def ffn_up_fwd(x1, wg, wu):
    T = x1.shape[0]
    tm = min(TM_FF, T)

    def body(x_r, wg_r, wu_r, g_o, u_o, h_o, xb):
        @pl.when(pl.program_id(1) == 0)
        def _():
            xb[...] = x_r[...].astype(MX)
        gate = _dot(xb[...], wg_r[0])
        up = _dot(xb[...], wu_r[0])
        g_o[0] = gate.astype(g_o.dtype)
        u_o[0] = up.astype(u_o.dtype)
        h_o[0] = (gate * _sigmoid(gate) * up).astype(h_o.dtype)

    wspec = pl.BlockSpec((1, D, FB), lambda i, k: (k, 0, 0))
    ospec = pl.BlockSpec((1, tm, FB), lambda i, k: (k, i, 0))
    return pl.pallas_call(
        body, name="ffn_up_fwd", grid=(T // tm, NCHIP),
        in_specs=[pl.BlockSpec((tm, D), lambda i, k: (i, 0)), wspec, wspec],
        out_specs=[ospec] * 3,
        out_shape=[jax.ShapeDtypeStruct((NCHIP, T, FB), ACT)] * 2 + [jax.ShapeDtypeStruct((NCHIP, T, FB), MX)],
        scratch_shapes=[pltpu.VMEM((tm, D), MX)],
        compiler_params=_cp(("parallel", "arbitrary")),
    )(x1, wg, wu)


def ffn_down_fwd(hh, wd, x1, g2, b2):
    T = x1.shape[0]
    tm = min(TM_FF, T)

    def body(h_r, w_r, x_r, g_r, b_r, r2_o, x2_o):
        r2 = ALPHA * x_r[...]
        for k in range(NCHIP):
            r2 = r2 + _dot(h_r[k], w_r[k])
        r2_o[...] = r2
        xhat, _ = _ln_stats(r2)
        x2_o[...] = xhat * g_r[...] + b_r[...]

    tile = pl.BlockSpec((tm, D), lambda i: (i, 0))
    vec = pl.BlockSpec((1, D), lambda i: (0, 0))
    return pl.pallas_call(
        body, name="ffn_down_fwd", grid=(T // tm,),
        in_specs=[pl.BlockSpec((NCHIP, tm, FB), lambda i: (0, i, 0)), pl.BlockSpec((NCHIP, FB, D), lambda i: (0, 0, 0)),
                  tile, vec, vec],
        out_specs=[tile, tile], out_shape=[jax.ShapeDtypeStruct((T, D), f32)] * 2,
        compiler_params=_cp(("parallel",)),
    )(hh, wd, x1, g2, b2)


def loss_grad(y, tgt):
    T = y.shape[0]
    tm = min(512, T)

    def body(y_r, t_r, l_o, dy_o):
        e = y_r[...] - t_r[...]
        dy_o[...] = e * (1.0 / D)

        @pl.when(pl.program_id(0) == 0)
        def _():
            l_o[...] = jnp.zeros_like(l_o)
        l_o[...] += (0.5 / D) * jnp.sum(e * e)

    tile = pl.BlockSpec((tm, D), lambda i: (i, 0))
    return pl.pallas_call(
        body, name="loss_grad", grid=(T // tm,),
        in_specs=[tile, tile], out_specs=[pl.BlockSpec((8, LANES), lambda i: (0, 0)), tile],
        out_shape=[jax.ShapeDtypeStruct((8, LANES), f32), jax.ShapeDtypeStruct((T, D), f32)],
        compiler_params=_cp(("arbitrary",)),
    )(y, tgt)


def ffn_down_bwd(dx2, r2, g2, wd, gate, up):
    T = dx2.shape[0]
    tm = min(TM_FF, T)

    def body(dx_r, r_r, g_r, w_r, ga_r, up_r, dr_o, dg_o, du_o, dlg_o, dlb_o, drb):
        i, k = pl.program_id(0), pl.program_id(1)

        @pl.when(k == 0)
        def _():
            xhat, rstd = _ln_stats(r_r[...])
            dx = dx_r[...]
            _acc_rows(dlg_o, i == 0, dx * xhat)
            _acc_rows(dlb_o, i == 0, dx)
            dr = _ln_bwd(dx, xhat, rstd, g_r[...])
            dr_o[...] = dr
            drb[...] = dr.astype(MX)

        dhh = _dot(drb[...], w_r[0], NT)
        gate_v, up_v = ga_r[0].astype(f32), up_r[0].astype(f32)
        sg = _sigmoid(gate_v)
        dg_o[0] = (dhh * up_v * sg * (1.0 + gate_v * (1.0 - sg))).astype(dg_o.dtype)
        du_o[0] = (dhh * gate_v * sg).astype(du_o.dtype)

    tile = pl.BlockSpec((tm, D), lambda i, k: (i, 0))
    vec = pl.BlockSpec((1, D), lambda i, k: (0, 0))
    blk = pl.BlockSpec((1, tm, FB), lambda i, k: (k, i, 0))
    return pl.pallas_call(
        body, name="ffn_down_bwd", grid=(T // tm, NCHIP),
        in_specs=[tile, tile, vec, pl.BlockSpec((1, FB, D), lambda i, k: (k, 0, 0)), blk, blk],
        out_specs=[tile, blk, blk, vec, vec],
        out_shape=[jax.ShapeDtypeStruct((T, D), f32)] + [jax.ShapeDtypeStruct((NCHIP, T, FB), MX)] * 2
        + [jax.ShapeDtypeStruct((1, D), f32)] * 2,
        scratch_shapes=[pltpu.VMEM((tm, D), MX)],
        compiler_params=_cp(("arbitrary", "arbitrary")),
    )(dx2, r2, g2, wd, gate, up)


def ffn_up_bwd(dr2, dgate, dup, wg, wu, r1, g1):
    T = dr2.shape[0]
    tm = min(TM_FF, T)

    def body(dr2_r, dg_r, du_r, wg_r, wu_r, r1_r, g_r, dr1_o, dlg_o, dlb_o, acc):
        i, k = pl.program_id(0), pl.program_id(1)

        @pl.when(k == 0)
        def _():
            acc[...] = ALPHA * dr2_r[...]
        acc[...] += _dot(dg_r[0], wg_r[0], NT) + _dot(du_r[0], wu_r[0], NT)

        @pl.when(k == NCHIP - 1)
        def _():
            dx = acc[...]
            xhat, rstd = _ln_stats(r1_r[...])
            _acc_rows(dlg_o, i == 0, dx * xhat)
            _acc_rows(dlb_o, i == 0, dx)
            dr1_o[...] = _ln_bwd(dx, xhat, rstd, g_r[...])

    tile = pl.BlockSpec((tm, D), lambda i, k: (i, 0))
    vec = pl.BlockSpec((1, D), lambda i, k: (0, 0))
    blk = pl.BlockSpec((1, tm, FB), lambda i, k: (k, i, 0))
    wspec = pl.BlockSpec((1, D, FB), lambda i, k: (k, 0, 0))
    return pl.pallas_call(
        body, name="ffn_up_bwd", grid=(T // tm, NCHIP),
        in_specs=[tile, blk, blk, wspec, wspec, tile, vec],
        out_specs=[tile, vec, vec],
        out_shape=[jax.ShapeDtypeStruct((T, D), f32)] + [jax.ShapeDtypeStruct((1, D), f32)] * 2,
        scratch_shapes=[pltpu.VMEM((tm, D), f32)],
        compiler_params=_cp(("arbitrary", "arbitrary")),
    )(dr2, dgate, dup, wg, wu, r1, g1)


def mix_out_bwd(dr1, proj, mabc, wo, pa, pb, pc):
    T = dr1.shape[0]
    tm = min(TM_MIX, T)

    def body(dr_r, gt, mabc_r, wo_r, pa_r, pb_r, pc_r, dmabc_o, dgt_o, dya_o, dyb_o, dyc_o):
        dm = _dot(dr_r[...].astype(MX), wo_r[...], NT)
        dmx = []
        for j in range(3):
            s = _sigmoid(gt[:, j * D:(j + 1) * D].astype(f32))
            v = (dm * s).astype(MX)
            dmx.append(v)
            dmabc_o[:, j * D:(j + 1) * D] = v
            dgt_o[:, j * D:(j + 1) * D] = (dm * mabc_r[:, j * D:(j + 1) * D].astype(f32) * s * (1.0 - s)).astype(dgt_o.dtype)
        dya_o[...] = _dot(dmx[0], pa_r[...], NT)
        dyb = jnp.zeros((tm, AO), f32)
        for k in range(NCHIP):
            dyb = dyb + _dot(dmx[1][:, k * (D // NCHIP):(k + 1) * (D // NCHIP)], pb_r[k], NT)
        dyb_o[...] = dyb
        dyc_o[...] = _dot(dmx[2], pc_r[...], NT)

    full = lambda shape: pl.BlockSpec(shape, lambda i: (0,) * len(shape))
    tile = lambda w: pl.BlockSpec((tm, w), lambda i: (i, 0))
    return pl.pallas_call(
        body, name="mix_out_bwd", grid=(T // tm,),
        in_specs=[tile(D), tile(3 * D), tile(3 * D), full((D, D)), full((D, D)), full((NCHIP, AO, D // NCHIP)), full((D, D))],
        out_specs=[tile(3 * D), tile(3 * D), tile(D), tile(AO), tile(D)],
        out_shape=[jax.ShapeDtypeStruct((T, 3 * D), MX), jax.ShapeDtypeStruct((T, 3 * D), MX),
                   jax.ShapeDtypeStruct((T, D), f32), jax.ShapeDtypeStruct((T, AO), f32), jax.ShapeDtypeStruct((T, D), f32)],
        compiler_params=_cp(("parallel",), 56),
    )(dr1, proj, mabc, wo, pa, pb, pc)


def transpose_cast(x):
    T = x.shape[0]
    tm = min(512, T)

    def body(x_r, o_r):
        o_r[...] = x_r[...].T.astype(o_r.dtype)

    return pl.pallas_call(
        body, name="transpose_cast", grid=(T // tm,),
        in_specs=[pl.BlockSpec((tm, D), lambda i: (i, 0))], out_specs=pl.BlockSpec((D, tm), lambda i: (0, i)),
        out_shape=jax.ShapeDtypeStruct((D, T), MX), compiler_params=_cp(("parallel",)),
    )(x)


def tn_matmul(name, a, b, a_spec, b_spec, out_shape, out_spec, grid, a_is_t=False):
    nt = len(grid) - 1

    def body(a_r, b_r, o_r):
        @pl.when(pl.program_id(nt) == 0)
        def _():
            o_r[...] = jnp.zeros_like(o_r)
        av = a_r[...].reshape(a_r.shape[-2:]).astype(MX)
        bv = b_r[...].reshape(b_r.shape[-2:]).astype(MX)
        o_r[...] += _dot(av, bv, None if a_is_t else TN).reshape(o_r.shape)

    return pl.pallas_call(
        body, name=name, grid=grid, in_specs=[a_spec, b_spec], out_specs=out_spec,
        out_shape=jax.ShapeDtypeStruct(out_shape, f32),
        compiler_params=_cp(("parallel",) * nt + ("arbitrary",), 56),
    )(a, b)


def attn_pre_bwd(dyb, os_, lses):
    T = dyb.shape[0]
    tm = TM_FOLD

    def body(dy_r, o0, o1, o2, l0, l1, l2, ones_r, d0, d1, d2, f0, f1, f2, nat):
        o = [_unfold_in(nat, r, d) for r, (_, d) in zip((o0, o1, o2), GROUPS)]
        ls = [_unfold_in(nat, r, d) for r, (_, d) in zip((l0, l1, l2), GROUPS)]
        w = _group_weights(ls)
        dy = dy_r[...]
        t = dy * (w[0] * o[0] + w[1] * o[1] + w[2] * o[2])
        hi = t.astype(MX)
        lo = (t - hi.astype(f32)).astype(MX)
        c = _dot(hi, ones_r[...]) + _dot(lo, ones_r[...])
        for wg, do_o, df_o, (_, d) in zip(w, (d0, d1, d2), (f0, f1, f2), GROUPS):
            _fold_out(nat, wg * dy, do_o, d)
            _fold_out(nat, -wg * c, df_o, d)

    specs = _fold_specs(T, tm)
    return pl.pallas_call(
        body, name="attn_pre_bwd", grid=(T // tm,),
        in_specs=[pl.BlockSpec((tm, AO), lambda i: (i, 0))] + specs + specs + [pl.BlockSpec((AO, AO), lambda i: (0, 0))],
        out_specs=specs + specs,
        out_shape=[jax.ShapeDtypeStruct((d, T // d, AO), MX) for _, d in GROUPS]
        + [jax.ShapeDtypeStruct((d, T // d, AO), f32) for _, d in GROUPS],
        scratch_shapes=[pltpu.VMEM((AO // LANES, tm, LANES), f32)],
        compiler_params=_cp(("parallel",)),
    )(dyb, *os_, *lses, _head_ones())


def _head_ones():
    i = jnp.arange(AO) // HD
    return (i[:, None] == i[None, :]).astype(MX)


def attn_bwd(qf, kf, vf, dof, lse, df, g, nb):
    T = qf.shape[0]

    def body(q_ref, k_ref, v_ref, do_ref, l_ref, d_ref, dq_ref, dk_ref, dv_ref):
        prev_m, cur_m = _window_masks()

        def head_col(ref, r0):
            v = ref[pl.ds(r0, BLK), :]
            return jnp.concatenate([v[:, 0:1], v[:, HD:HD + 1]], axis=0)

        def step(b, carry):
            dk_c, dv_c = carry
            r0 = pl.multiple_of(b * BLK, BLK)
            rp = pl.multiple_of(jnp.maximum(b - 1, 0) * BLK, BLK)
            qs, dos = _stack_heads(q_ref[pl.ds(r0, BLK), :]), _stack_heads(do_ref[pl.ds(r0, BLK), :])
            k2, v2 = _two_blocks(k_ref, b), _two_blocks(v_ref, b)
            valid = cur_m | (prev_m & ((b % nb) != 0))
            p = jnp.where(valid, jnp.exp(_dot(qs, k2, NT) - head_col(l_ref, r0)), 0.0)
            ds = (p * (_dot(dos, v2, NT) + head_col(d_ref, r0))).astype(MX)
            dq_ref[pl.ds(r0, BLK), :] = _unstack_heads(_dot(ds, k2)).astype(dq_ref.dtype)
            dk2 = _dot(ds, qs, TN)
            dv2 = _dot(p.astype(MX), dos, TN)
            dk_ref[pl.ds(rp, BLK), :] = (dk_c + dk2[:BLK]).astype(dk_ref.dtype)
            dv_ref[pl.ds(rp, BLK), :] = (dv_c + dv2[:BLK]).astype(dv_ref.dtype)
            return dk2[BLK:], dv2[BLK:]

        zero = jnp.zeros((BLK, LANES), f32)

        def two_steps(i, carry):
            return step(2 * i + 1, step(2 * i, carry))

        dk_c, dv_c = lax.fori_loop(0, T // BLK // 2, two_steps, (zero, zero))
        dk_ref[pl.ds(T - BLK, BLK), :] = dk_c.astype(dk_ref.dtype)
        dv_ref[pl.ds(T - BLK, BLK), :] = dv_c.astype(dv_ref.dtype)

    spec = pl.BlockSpec((T, LANES), lambda j: (0, j))
    return pl.pallas_call(
        body, name=f"attn_bwd{g}", grid=(AO // LANES,),
        in_specs=[spec] * 6, out_specs=[spec] * 3,
        out_shape=[jax.ShapeDtypeStruct((T, AO), MX)] * 3,
        compiler_params=_cp(("parallel",), 60),
    )(qf, kf, vf, dof, lse, df)


def unfold_rope_bwd(dqf, dkf, dvf, cos_t, sin_t, g, d):
    T = dqf.shape[0] * dqf.shape[1]
    tm = TM_FOLD

    def body(q_r, k_r, v_r, c_ref, s_ref, o_ref, nat):
        cos, sin = _tile4(c_ref[...]), _tile4(s_ref[...])
        for part, ref, scale in ((0, q_r, HD ** -0.5), (1, k_r, 1.0), (2, v_r, None)):
            x = _unfold_in(nat, ref, d)
            if scale is not None:
                x = (x * cos - _swap_halves(x) * sin) * scale
            o_ref[:, part * AO:(part + 1) * AO] = x.astype(o_ref.dtype)

    fold_spec = pl.BlockSpec((d, tm // d, AO), lambda i: (0, i, 0))
    tab = pl.BlockSpec((tm, LANES), lambda i: (i, 0))
    return pl.pallas_call(
        body, name=f"unfold_rope_bwd{g}", grid=(T // tm,),
        in_specs=[fold_spec] * 3 + [tab, tab],
        out_specs=pl.BlockSpec((tm, 3 * AO), lambda i: (i, 0)),
        out_shape=jax.ShapeDtypeStruct((T, 3 * AO), MX),
        scratch_shapes=[pltpu.VMEM((AO // LANES, tm, LANES), f32)],
        compiler_params=_cp(("parallel",)),
    )(dqf, dkf, dvf, cos_t, sin_t)


def conv_bwd(dya, proj, conv_w):
    T = dya.shape[0]
    tm = TM_AC
    last = T // tm - 1

    def body(dy_r, bch, hprev, dy_next, b_next, cw, d_o, dw_o, zs, ds):
        i = pl.program_id(0)
        pb = bch[...].astype(f32)
        bp, cp, hp = pb[:, :D], pb[:, D:2 * D], pb[:, 2 * D:]
        z = cp * hp
        hz = hprev[:, :D].astype(f32) * hprev[:, D:].astype(f32)
        zs[0:HALO, :] = jnp.where(i > 0, hz, 0.0)
        zs[HALO:HALO + tm, :] = z
        z2, z1 = zs[HALO - 2:HALO - 2 + tm, :], zs[HALO - 1:HALO - 1 + tm, :]
        cv = cw[0:1, :] * z2 + cw[1:2, :] * z1 + cw[2:3, :] * z
        dy = dy_r[...]
        dcv = dy * bp
        ds[0:tm, :] = dcv
        ds[tm:tm + HALO, :] = jnp.where(i < last, dy_next[...] * b_next[...].astype(f32), 0.0)
        dz = cw[2:3, :] * dcv + cw[1:2, :] * ds[1:1 + tm, :] + cw[0:1, :] * ds[2:2 + tm, :]
        d_o[:, :D] = (dy * cv).astype(d_o.dtype)
        d_o[:, D:2 * D] = (dz * hp).astype(d_o.dtype)
        d_o[:, 2 * D:] = (dz * cp).astype(d_o.dtype)

        @pl.when(i == 0)
        def _():
            dw_o[...] = jnp.zeros_like(dw_o)
        dw_o[0:1, :] += jnp.sum(dcv * z2, axis=0, keepdims=True)
        dw_o[1:2, :] += jnp.sum(dcv * z1, axis=0, keepdims=True)
        dw_o[2:3, :] += jnp.sum(dcv * z, axis=0, keepdims=True)

    nh = tm // HALO
    return pl.pallas_call(
        body, name="conv_bwd", grid=(T // tm,),
        in_specs=[pl.BlockSpec((tm, D), lambda i: (i, 0)), pl.BlockSpec((tm, 3 * D), lambda i: (i, 1)),
                  pl.BlockSpec((HALO, 2 * D), lambda i: (jnp.maximum(i * nh - 1, 0), 2)),
                  pl.BlockSpec((HALO, D), lambda i: (jnp.minimum((i + 1) * nh, T // HALO - 1), 0)),
                  pl.BlockSpec((HALO, D), lambda i: (jnp.minimum((i + 1) * nh, T // HALO - 1), 3)),
                  pl.BlockSpec((3, D), lambda i: (0, 0))],
        out_specs=[pl.BlockSpec((tm, 3 * D), lambda i: (i, 0)), pl.BlockSpec((3, D), lambda i: (0, 0))],
        out_shape=[jax.ShapeDtypeStruct((T, 3 * D), MX), jax.ShapeDtypeStruct((3, D), f32)],
        scratch_shapes=[pltpu.VMEM((HALO + tm, D), f32), pltpu.VMEM((tm + HALO, D), f32)],
        compiler_params=_cp(("arbitrary",)),
    )(dya, proj, proj, dya, proj, conv_w)


def gmlp_bwd(dyc, proj, wst, bsx, lg, lb):
    T = dyc.shape[0]
    tm = TM_AC
    last = T // tm - 1

    def body(dy_r, u0, u1, v0, v1, ws, bs, lg_r, lb_r, d_o, dws_o, dbs_o, dlg_o, dlb_o, bacc):
        i = pl.program_id(0)
        up = jnp.concatenate([u0[...], u1[...]], axis=1).astype(f32)
        vp = jnp.concatenate([v0[...], v1[...]], axis=1).astype(f32)
        u, vn, xhat, rstd, sp = _gmlp_fwd(up, vp, ws, bs, lg_r[...], lb_r[...])
        dy = dy_r[...]
        d_o[:, :D] = (dy * sp * _gelu_grad(up)).astype(d_o.dtype)
        dsp = dy * u
        dspb, vnb = dsp.astype(MX), vn.astype(MX)

        @pl.when(i == 0)
        def _():
            dws_o[...] = jnp.zeros_like(dws_o)
            bacc[...] = jnp.zeros_like(bacc)

        rows = []
        for c in range(tm // BLK):
            r = slice(c * BLK, (c + 1) * BLK)
            cols = []
            for g in range(8):
                cs = slice(g * BLK, (g + 1) * BLK)
                dws_o[g] += _dot(dspb[r, cs], vnb[r, cs], NT)
                bacc[g] += dsp[r, cs]
                cols.append(_dot(ws[g], dspb[r, cs], TN))
            rows.append(jnp.concatenate(cols, axis=1))
        dvn = jnp.concatenate(rows, axis=0)
        _acc_rows(dlg_o, i == 0, dvn * xhat)
        _acc_rows(dlb_o, i == 0, dvn)
        d_o[:, D:] = (_ln_bwd(dvn, xhat, rstd, lg_r[...]) * _gelu_grad(vp)).astype(d_o.dtype)

        @pl.when(i == last)
        def _():
            row = lax.broadcasted_iota(jnp.int32, (BLK, BLK), 0)
            col = lax.broadcasted_iota(jnp.int32, (BLK, BLK), 1)
            ones = jnp.ones((8, BLK), MX)
            for g in range(8):
                dws_o[g] = jnp.where(col <= row, dws_o[g], 0.0)
                a = bacc[g]
                hi = a.astype(MX)
                lo = (a - hi.astype(f32)).astype(MX)
                dbs_o[g:g + 1, :] = (_dot(ones, hi, NT) + _dot(ones, lo, NT))[0:1, :]

    full = lambda shape: pl.BlockSpec(shape, lambda i: (0,) * len(shape))
    return pl.pallas_call(
        body, name="gmlp_bwd", grid=(T // tm,),
        in_specs=[pl.BlockSpec((tm, D), lambda i: (i, 0)), *_uv_specs(), full((8, BLK, BLK)), full((8, BLK, BLK)),
                  full((1, D)), full((1, D))],
        out_specs=[pl.BlockSpec((tm, 2 * D), lambda i: (i, 0)), full((8, BLK, BLK)), full((8, BLK)), full((1, D)), full((1, D))],
        out_shape=[jax.ShapeDtypeStruct((T, 2 * D), MX), jax.ShapeDtypeStruct((8, BLK, BLK), f32),
                   jax.ShapeDtypeStruct((8, BLK), f32), jax.ShapeDtypeStruct((1, D), f32), jax.ShapeDtypeStruct((1, D), f32)],
        scratch_shapes=[pltpu.VMEM((8, BLK, BLK), f32)],
        compiler_params=_cp(("arbitrary",)),
    )(dyc, proj, proj, proj, proj, wst, bsx, lg, lb)


PART_TILES = (6, 6, 3, 3, 3, 4)
PART_START = (0, 6, 12, 15, 18, 21)
TJ = 512


def _part_specs(tm, rows_axis):
    specs = []
    for n, s in zip(PART_TILES, PART_START):
        def imap(*idx, n=n, s=s):
            i, j = idx[rows_axis], idx[1 - rows_axis]
            inside = (j >= s) & (j < s + n)
            return (jnp.where(inside, i, 0), jnp.clip(j - s, 0, n - 1))
        specs.append(pl.BlockSpec((tm, TJ), imap))
    return specs


def _with_part(j, refs, fn):
    for r, n, s in zip(refs, PART_TILES, PART_START):
        @pl.when((j >= s) & (j < s + n))
        def _():
            fn(r[...])


def dx_in(dr1, parts, w):
    T = dr1.shape[0]
    tm = min(1024, T)

    def body(dr_r, p0, p1, p2, p3, p4, p5, w_r, o_r):
        j = pl.program_id(1)

        @pl.when(j == 0)
        def _():
            o_r[...] = ALPHA * dr_r[...]

        def acc(tile):
            o_r[...] += _dot(tile, w_r[...], NT)
        _with_part(j, (p0, p1, p2, p3, p4, p5), acc)

    return pl.pallas_call(
        body, name="dx_in", grid=(T // tm, NIN // TJ),
        in_specs=[pl.BlockSpec((tm, D), lambda i, j: (i, 0))] + _part_specs(tm, 0) + [pl.BlockSpec((D, TJ), lambda i, j: (0, j))],
        out_specs=pl.BlockSpec((tm, D), lambda i, j: (i, 0)),
        out_shape=jax.ShapeDtypeStruct((T, D), f32),
        compiler_params=_cp(("parallel", "arbitrary"), 56),
    )(dr1, *parts, w)


def dw_in(x0t, parts):
    T = x0t.shape[1]
    tk = min(2048, T)

    def body(x_r, p0, p1, p2, p3, p4, p5, o_r):
        j, t = pl.program_id(0), pl.program_id(1)

        @pl.when(t == 0)
        def _():
            o_r[...] = jnp.zeros_like(o_r)

        def acc(tile):
            o_r[...] += _dot(x_r[...], tile)
        _with_part(j, (p0, p1, p2, p3, p4, p5), acc)

    return pl.pallas_call(
        body, name="dw_in", grid=(NIN // TJ, T // tk),
        in_specs=[pl.BlockSpec((D, tk), lambda j, t: (0, t))] + _part_specs(tk, 1),
        out_specs=pl.BlockSpec((D, TJ), lambda j, t: (0, j)),
        out_shape=jax.ShapeDtypeStruct((D, NIN), f32),
        compiler_params=_cp(("parallel", "arbitrary")),
    )(x0t, *parts)


def rope_tables(positions):
    half = HD // 2
    inv_freq = ROPE_THETA ** (-jnp.arange(half, dtype=f32) / half)
    ang = positions.astype(f32)[:, None] * inv_freq
    cos, sin = jnp.cos(ang), jnp.sin(ang)
    return jnp.tile(cos, (1, LANES // half)), jnp.tile(jnp.concatenate([-sin, sin], axis=1), (1, LANES // HD))


def _flat(a):
    return a.reshape(a.shape[0] * a.shape[1], a.shape[2])


def layer_fwd(x0, W, cos_t, sin_t):
    T = x0.shape[0]
    proj = mm_in(x0, W["w_in"])
    ya, yc = mix_ac_fwd(proj, W["conv_w"], W["wst"], W["bsx"], W["gmlp_ln_g"], W["gmlp_ln_b"])
    folded, os_, lses = [], [], []
    for g, (_, d) in enumerate(GROUPS):
        qf, kf, vf = fold_rope(proj, cos_t, sin_t, g, d)
        o, lse = attn_fwd(_flat(qf), _flat(kf), _flat(vf), g, T // d // BLK)
        folded.append((qf, kf, vf))
        os_.append(o.reshape(d, T // d, AO))
        lses.append(lse.reshape(d, T // d, AO))
    yb = combine_fwd(os_, lses)
    mabc, m, r1, x1 = mix_out_fwd(proj, ya, yb, yc, x0, W["p_a"], W["p_b"], W["p_c"], W["w_o"], W["ln1_g"], W["ln1_b"])
    gate, up, hh = ffn_up_fwd(x1, W["w_gate"], W["w_up"])
    r2, x2 = ffn_down_fwd(hh, W["w_down"], x1, W["ln2_g"], W["ln2_b"])
    saved = dict(x0=x0, proj=proj, ya=ya, yb=yb, yc=yc, folded=folded, os=os_, lses=lses, mabc=mabc, m=m, r1=r1,
                 x1=x1, gate=gate, up=up, hh=hh, r2=r2)
    return x2, saved


def layer_bwd(dx2, S, W, cos_t, sin_t):
    T = dx2.shape[0]
    tk = min(2048, T)
    G = {}
    dr2, dgate, dup, G["ln2_g"], G["ln2_b"] = ffn_down_bwd(dx2, S["r2"], W["ln2_g"], W["w_down"], S["gate"], S["up"])
    blk_a = pl.BlockSpec((1, tk, FB), lambda k, t: (k, t, 0))
    row_b = pl.BlockSpec((tk, D), lambda k, t: (t, 0))
    G["w_down"] = tn_matmul("dw_down", S["hh"], dr2, blk_a, row_b, (NCHIP, FB, D),
                            pl.BlockSpec((1, FB, D), lambda k, t: (k, 0, 0)), (NCHIP, T // tk))
    x1t = transpose_cast(S["x1"])
    for nm, dv in (("w_gate", dgate), ("w_up", dup)):
        G[nm] = tn_matmul("d" + nm, x1t, dv, pl.BlockSpec((D, tk), lambda k, t: (0, t)), blk_a, (NCHIP, D, FB),
                          pl.BlockSpec((1, D, FB), lambda k, t: (k, 0, 0)), (NCHIP, T // tk), a_is_t=True)
    dr1, G["ln1_g"], G["ln1_b"] = ffn_up_bwd(dr2, dgate, dup, W["w_gate"], W["w_up"], S["r1"], W["ln1_g"])
    dmabc, dgates, dya, dyb, dyc = mix_out_bwd(dr1, S["proj"], S["mabc"], W["w_o"], W["p_a"], W["p_b"], W["p_c"])
    one = (1, T // tk)
    full_o = pl.BlockSpec((D, D), lambda k, t: (0, 0))
    G["w_o"] = tn_matmul("dw_o", S["m"], dr1, row_b, row_b, (D, D), full_o, one)
    G["p_a"] = tn_matmul("dp_a", S["ya"], dmabc, row_b, pl.BlockSpec((tk, D), lambda k, t: (t, 0)), (D, D), full_o, one)
    G["p_c"] = tn_matmul("dp_c", S["yc"], dmabc, row_b, pl.BlockSpec((tk, D), lambda k, t: (t, 2)), (D, D), full_o, one)
    G["p_b"] = tn_matmul("dp_b", S["yb"], dmabc, pl.BlockSpec((tk, AO), lambda k, t: (t, 0)),
                         pl.BlockSpec((tk, D // NCHIP), lambda k, t: (t, NCHIP + k)), (NCHIP, AO, D // NCHIP),
                         pl.BlockSpec((1, AO, D // NCHIP), lambda k, t: (k, 0, 0)), (NCHIP, T // tk))
    dbch, G["conv_w"] = conv_bwd(dya, S["proj"], W["conv_w"])
    duv, G["w_s"], G["b_s"], G["gmlp_ln_g"], G["gmlp_ln_b"] = gmlp_bwd(
        dyc, S["proj"], W["wst"], W["bsx"], W["gmlp_ln_g"], W["gmlp_ln_b"])
    pre = attn_pre_bwd(dyb, S["os"], S["lses"])
    dqkv = []
    for g, (_, d) in enumerate(GROUPS):
        qf, kf, vf = S["folded"][g]
        dqf, dkf, dvf = attn_bwd(_flat(qf), _flat(kf), _flat(vf), _flat(pre[g]), _flat(S["lses"][g]), _flat(pre[3 + g]),
                                 g, T // d // BLK)
        shp = (d, T // d, AO)
        dqkv.append(unfold_rope_bwd(dqf.reshape(shp), dkf.reshape(shp), dvf.reshape(shp), cos_t, sin_t, g, d))
    parts = (dgates, dbch, *dqkv, duv)
    G["w_in"] = dw_in(transpose_cast(S["x0"]), parts)
    dx0 = dx_in(dr1, parts, W["w_in"])
    return dx0, G


def prep_layer_weights(Wl):
    W = dict(Wl)
    tril = jnp.tril(jnp.ones((BLK, BLK), f32))
    W["wst"] = (Wl["w_s"] * tril[None]).astype(MX)
    W["bsx"] = jnp.broadcast_to(Wl["b_s"][:, :, None], (8, BLK, BLK))
    for n in ("gmlp_ln_g", "gmlp_ln_b", "ln1_g", "ln1_b", "ln2_g", "ln2_b"):
        W[n] = Wl[n].reshape(1, D)
    return W


def local_step(x, positions, target, layers, on_grads=None):
    cos_t, sin_t = rope_tables(positions)
    Ws = [prep_layer_weights(Wl) for Wl in layers]
    saved = []
    h = x
    for W in Ws:
        h, S = layer_fwd(h, W, cos_t, sin_t)
        saved.append(S)
    lsum, dh = loss_grad(h, target)
    grads = [None] * len(Ws)
    started = None
    for l in reversed(range(len(Ws))):
        W = Ws[l]
        if started is not None:
            W = dict(W, ln2_g=W["ln2_g"] + started)
        dh, grads[l] = layer_bwd(dh, saved[l], W, cos_t, sin_t)
        if on_grads is not None:
            started = on_grads(l, grads[l])
    return lsum, dh, grads


MESH = pl.DeviceIdType.MESH
ANY = pl.BlockSpec(memory_space=pl.ANY)
BIG = ("w_in", "w_gate", "w_up", "w_down", "p_a", "p_b", "p_c", "w_o")
NBIG = len(BIG)


def _place():
    x, y, c = lax.axis_index("x"), lax.axis_index("y"), lax.axis_index("c")
    return x, y, c, 2 * x + y


def _rcopy(src, dst, send, recv, dev):
    return pltpu.make_async_remote_copy(src_ref=src, dst_ref=dst, send_sem=send, recv_sem=recv, device_id=dev,
                                        device_id_type=MESH)


def _cols(ref, k, width):
    start = k * width if isinstance(k, int) else pl.multiple_of(k * width, LANES)
    return ref.at[:, pl.ds(start, width)]


CHUNK_BYTES = 1 << 20


def _pieces(shape, itemsize, nbytes=CHUNK_BYTES):
    rows, cols = shape[-2], shape[-1]
    per = max(16, nbytes // (cols * itemsize) // 16 * 16)
    out = []
    for lead in (range(shape[0]) if len(shape) == 3 else (None,)):
        for r in range(0, rows, per):
            sl = (pl.ds(r, min(per, rows - r)), slice(None))
            out.append(sl if lead is None else (lead,) + sl)
    return out


def _start_pieces(src, dst, make, nbytes=CHUNK_BYTES):
    for idx in _pieces(src.shape, jnp.dtype(src.dtype).itemsize, nbytes):
        make(src.at[idx], dst.at[idx]).start()


def gather_weights(shards):
    n = len(shards)

    def body(*refs):
        srcs, dsts = refs[:n], refs[n:2 * n]
        send, recv, own_send, own_recv = refs[2 * n:]
        x, y, c, k = _place()
        sib = (x, y, 1 - c)
        chips = [(1 - x, y), (x, 1 - y), (1 - x, 1 - y)]

        def slot(a, layer, pos):
            if a == 0:
                return _cols(dsts[0].at[layer], pos, WIN_SHARD)
            return dsts[a].at[layer, pos]

        def ici(a, j, src, dst):
            return _rcopy(src, dst, send.at[a, j], recv.at[a, j], (*chips[j], c))

        def d2d(a, j, src, dst):
            return _rcopy(src, dst, send.at[a, 3 + j], recv.at[a, 3 + j], sib)

        def own(a, layer, src, dst):
            return _rcopy(src, dst, own_send.at[a, layer], own_recv.at[a, layer], sib)

        for a in range(n):
            for j in range(3):
                _start_pieces(srcs[a].at[c], slot(a, c, k), functools.partial(ici, a, j))
        for a in range(n):
            for layer in range(DEPTH):
                _start_pieces(srcs[a].at[layer], slot(a, layer, k), functools.partial(own, a, layer))
        for a in range(n):
            for j, (cx, cy) in enumerate(chips):
                landed = slot(a, c, 2 * cx + cy)
                ici(a, j, landed, landed).wait_recv()
                _start_pieces(landed, landed, functools.partial(d2d, a, j))
        for a in range(n):
            for j, (cx, cy) in enumerate(chips):
                passed = slot(a, 1 - c, 2 * cx + cy)
                d2d(a, j, passed, passed).wait_recv()
                landed = slot(a, c, 2 * cx + cy)
                d2d(a, j, landed, landed).wait_send()
                ici(a, j, srcs[a].at[c], slot(a, c, k)).wait_send()
            for layer in range(DEPTH):
                own(a, layer, srcs[a].at[layer], slot(a, layer, k)).wait()

    outs = [jax.ShapeDtypeStruct((DEPTH, D, NIN), shards[0].dtype)]
    outs += [jax.ShapeDtypeStruct((DEPTH, NCHIP) + s.shape[1:], s.dtype) for s in shards[1:]]
    return pl.pallas_call(
        body, name="gather_weights", in_specs=[ANY] * n, out_specs=[ANY] * n, out_shape=outs,
        scratch_shapes=[pltpu.SemaphoreType.DMA((n, 6)), pltpu.SemaphoreType.DMA((n, 6)),
                        pltpu.SemaphoreType.DMA((n, DEPTH)), pltpu.SemaphoreType.DMA((n, DEPTH))],
    )(*shards)


def _half(ref, h, a):
    rows = ref.shape[-2] // 2
    start = pl.multiple_of(h * rows, 16)
    if a == 0:
        return ref.at[pl.ds(start, rows), :]
    return ref.at[:, pl.ds(start, rows), :]


def rs_pair(l, grads):
    def body(*refs):
        g, theirs = refs[:NBIG], refs[NBIG:2 * NBIG]
        send, recv = refs[2 * NBIG:]
        x, y, c, _ = _place()

        def give(a, s, d):
            return _rcopy(s, d, send.at[a], recv.at[a], (x, y, 1 - c))

        for a in range(NBIG):
            _start_pieces(_half(g[a], 1 - c, a), theirs[a], functools.partial(give, a))
        for a in range(NBIG):
            give(a, _half(g[a], 1 - c, a), theirs[a]).wait()

    def hshape(s, a):
        return (s[0] // 2, s[1]) if a == 0 else (s[0], s[1] // 2, s[2])

    outs = [jax.ShapeDtypeStruct(hshape(g.shape, a), g.dtype) for a, g in enumerate(grads)]
    return pl.pallas_call(
        body, name=f"rs_pair{l}", in_specs=[ANY] * NBIG, out_specs=[ANY] * NBIG, out_shape=outs,
        scratch_shapes=[pltpu.SemaphoreType.DMA((NBIG,))] * 2,
    )(*grads)


HBM = pl.BlockSpec(memory_space=pltpu.HBM)
SEMS = pl.BlockSpec(memory_space=pltpu.SEMAPHORE)
EFFECT = pltpu.SideEffectType.DATAFLOW_SIDE_EFFECTING


def _chip_piece(s, a, k):
    return _cols(s[a], k, WIN_SHARD) if a == 0 else s[a].at[k]


def _chip_copy(a, k, src, dst, send, recv, me, c):
    return _rcopy(src, dst, send.at[a * NCHIP + k], recv.at[a * NCHIP + me], (k // 2, k % 2, c))


def rs_chips_start(l, sums):
    def pshape(s, a):
        return (NCHIP, s[0], WIN_SHARD) if a == 0 else s

    def body(*refs):
        s, land = refs[:NBIG], refs[NBIG:2 * NBIG]
        send, recv = refs[2 * NBIG], refs[2 * NBIG + 1]
        token = refs[-1]
        x, y, c, me = _place()
        for k in range(NCHIP):
            @pl.when(me != k)
            def _():
                for a in range(NBIG):
                    _start_pieces(_chip_piece(s, a, k), land[a].at[me],
                                  lambda src, dst, a=a: _chip_copy(a, k, src, dst, send, recv, me, c))
        token[...] = jnp.zeros_like(token)

    lands = [lax.empty(pshape(v.shape, a), v.dtype) for a, v in enumerate(sums)]
    ops = [pltpu.with_memory_space_constraint(v, pltpu.HBM) for v in list(sums) + lands]
    sem = pltpu.SemaphoreType.DMA((NBIG * NCHIP,))
    res = pl.pallas_call(
        body, name=f"rs_chips_start{l}", in_specs=[HBM] * (2 * NBIG),
        out_specs=[SEMS, SEMS] + [HBM] * (2 * NBIG) + [pl.BlockSpec(memory_space=pltpu.VMEM)],
        out_shape=[sem, sem] + [pltpu.HBM(v.shape, v.dtype) for v in ops] + [jax.ShapeDtypeStruct((8, LANES), f32)],
        input_output_aliases={i: 2 + i for i in range(2 * NBIG)},
        compiler_params=pltpu.CompilerParams(has_side_effects=EFFECT),
    )(*ops)
    return res[0], res[1], res[2:2 + NBIG], res[2 + NBIG:2 + 2 * NBIG], res[-1]


def rs_chips_wait(l, send, recv, sums, lands, after):
    n_after = len(after)

    def body(*refs):
        s, land = refs[:NBIG], refs[NBIG:2 * NBIG]
        send_r, recv_r = refs[2 * NBIG], refs[2 * NBIG + 1]
        x, y, c, me = _place()
        for k in range(NCHIP):
            @pl.when(me != k)
            def _():
                for a in range(NBIG):
                    piece = _chip_piece(s, a, k)
                    _chip_copy(a, k, piece, land[a].at[me], send_r, recv_r, me, c).wait_send()
                    _rcopy(piece, land[a].at[k], send_r.at[a * NCHIP + k], recv_r.at[a * NCHIP + k],
                           (k // 2, k % 2, c)).wait_recv()

    ops = list(sums) + list(lands)
    res = pl.pallas_call(
        body, name=f"rs_chips_wait{l}", in_specs=[HBM] * (2 * NBIG) + [SEMS, SEMS] + [ANY] * n_after,
        out_specs=[HBM] * (2 * NBIG), out_shape=[pltpu.HBM(v.shape, v.dtype) for v in ops],
        input_output_aliases={i: i for i in range(2 * NBIG)},
        compiler_params=pltpu.CompilerParams(has_side_effects=EFFECT),
    )(*ops, send, recv, *after)
    return res[:NBIG], res[NBIG:]


def rs_join(l, halves):
    def body(*refs):
        h, other = refs[:NBIG], refs[NBIG:2 * NBIG]
        send, recv = refs[2 * NBIG:]
        x, y, c, _ = _place()

        def give(a, s, d):
            return _rcopy(s, d, send.at[a], recv.at[a], (x, y, 1 - c))

        for a in range(NBIG):
            _start_pieces(h[a], other[a], functools.partial(give, a))
        for a in range(NBIG):
            give(a, h[a], other[a]).wait()

    outs = [jax.ShapeDtypeStruct(v.shape, v.dtype) for v in halves]
    return pl.pallas_call(
        body, name=f"rs_join{l}", in_specs=[ANY] * NBIG, out_specs=[ANY] * NBIG, out_shape=outs,
        scratch_shapes=[pltpu.SemaphoreType.DMA((NBIG,))] * 2,
    )(*halves)


def _row_tile(rows, cols, itemsize=4, target=2 << 20):
    best = 8
    for t in range(8, rows + 1, 8):
        if rows % t == 0 and t * cols * itemsize <= target:
            best = t
    return best


GRAD_WIRE = jnp.bfloat16


def add_n(name, terms, out_dtype=f32):
    shape = terms[0].shape
    cols = shape[-1]
    rows = math.prod(shape[:-1])
    tr = _row_tile(rows, cols)

    def body(*refs):
        acc = refs[0][...]
        for r in refs[1:-1]:
            acc = acc + r[...]
        refs[-1][...] = acc.astype(out_dtype)

    tile = pl.BlockSpec((tr, cols), lambda i: (i, 0))
    out = pl.pallas_call(
        body, name=name, grid=(rows // tr,), in_specs=[tile] * len(terms), out_specs=tile,
        out_shape=jax.ShapeDtypeStruct((rows, cols), out_dtype), compiler_params=_cp(("parallel",)),
    )(*[t.reshape(rows, cols) for t in terms])
    return out.reshape(shape)


def add_chips(name, land, own):
    _, rows, cols = land.shape
    tr = _row_tile(rows, cols, target=1 << 20)

    def body(land_r, own_r, o_r):
        me = 2 * lax.axis_index("x") + lax.axis_index("y")
        for k in range(NCHIP):
            @pl.when(me == k)
            def _():
                acc = None
                for j in range(NCHIP):
                    t = (own_r[...] if j == k else land_r[j]).astype(f32)
                    acc = t if acc is None else acc + t
                o_r[...] = acc

    tile = pl.BlockSpec((tr, cols), lambda i: (i, 0))
    return pl.pallas_call(
        body, name=name, grid=(rows // tr,), in_specs=[pl.BlockSpec((NCHIP, tr, cols), lambda i: (0, i, 0)), tile],
        out_specs=tile, out_shape=jax.ShapeDtypeStruct((rows, cols), f32), compiler_params=_cp(("parallel",)),
    )(land, own)


def reduce_scatter_begin(l, G):
    c = lax.axis_index("c")
    grads = [G[n] if G[n].ndim == 3 or n == "w_in" else G[n].reshape(NCHIP, D // NCHIP, D) for n in BIG]
    theirs = rs_pair(l, grads)
    sums = []
    for n, g, t in zip(BIG, grads, theirs):
        rows = g.shape[-2] // 2
        mine = lax.dynamic_slice_in_dim(g, c * rows, rows, axis=g.ndim - 2)
        sums.append(add_n(f"rs_add_pair{l}_{n}", [mine, t], GRAD_WIRE))
    send, recv, sums, lands, token = rs_chips_start(l, sums)
    return (send, recv, sums, lands), token[0, 0]


def reduce_scatter_finish(l, state, after):
    me = 2 * lax.axis_index("x") + lax.axis_index("y")
    send, recv, sums, lands = state
    sums, landed = rs_chips_wait(l, send, recv, sums, lands, after)
    halves = []
    for a, (n, s, v) in enumerate(zip(BIG, sums, landed)):
        own = lax.dynamic_slice_in_dim(s, me * WIN_SHARD, WIN_SHARD, axis=1) if a == 0 else \
            lax.dynamic_index_in_dim(s, me, 0, keepdims=False)
        halves.append(add_chips(f"rs_add_chips{l}_{n}", v, own))
    return dict(zip(BIG, zip(halves, rs_join(l, halves))))


NDEV = 8


def allreduce_small(pack):
    rows = pack.shape[0]

    def body(p_ref, o_ref, buf, send, recv):
        x, y, c, _ = _place()
        me = 4 * x + 2 * y + c
        buf[me] = p_ref[...]

        def give(r, s, d):
            return _rcopy(s, d, send.at[r - 1], recv.at[r - 1], (x ^ (r >> 2), y ^ ((r >> 1) & 1), c ^ (r & 1)))

        for r in range(1, NDEV):
            _start_pieces(p_ref, buf.at[me], functools.partial(give, r), 128 << 10)
        for r in range(1, NDEV):
            give(r, p_ref, buf.at[me]).wait_send()
            src = 4 * (x ^ (r >> 2)) + 2 * (y ^ ((r >> 1) & 1)) + (c ^ (r & 1))
            give(r, p_ref, buf.at[src]).wait_recv()
        acc = buf[0]
        for d in range(1, NDEV):
            acc = acc + buf[d]
        o_ref[...] = acc

    vm = pl.BlockSpec(memory_space=pltpu.VMEM)
    return pl.pallas_call(
        body, name="allreduce_small", in_specs=[vm], out_specs=vm, out_shape=jax.ShapeDtypeStruct(pack.shape, f32),
        scratch_shapes=[pltpu.VMEM((NDEV, rows, LANES), f32), pltpu.SemaphoreType.DMA((NDEV - 1,)),
                        pltpu.SemaphoreType.DMA((NDEV - 1,))],
        compiler_params=pltpu.CompilerParams(vmem_limit_bytes=40 << 20),
    )(pack)


def _adamw_math(w, g, m, v):
    m = ADAM_B1 * m + (1.0 - ADAM_B1) * g
    v = ADAM_B2 * v + (1.0 - ADAM_B2) * (g * g)
    m_hat = m / (1.0 - ADAM_B1 ** ADAM_STEP)
    v_hat = v / (1.0 - ADAM_B2 ** ADAM_STEP)
    return -ADAM_LR * (m_hat / (jnp.sqrt(v_hat) + ADAM_EPS) + ADAM_WD * w), m, v


def adamw_big(name, halves, w, m, v):
    _, R, C = w.shape
    tr = _row_tile(R // 2, C, target=1 << 20)
    nt = R // 2 // tr

    def body(a0, b0, a1, b1, w_r, m_r, v_r, g_o, d_o, m_o, v_o):
        mine = pl.program_id(1) == lax.axis_index("c")
        g = jnp.where(pl.program_id(0) == 0, jnp.where(mine, a0[...], b0[...]), jnp.where(mine, a1[...], b1[...]))
        g_o[...] = g
        d_o[...], m_o[...], v_o[...] = _adamw_math(w_r[...], g, m_r[...], v_r[...])

    stk = pl.BlockSpec((None, tr, C), lambda l, h, i: (l, h * nt + i, 0))
    lay0 = pl.BlockSpec((tr, C), lambda l, h, i: (jnp.where(l == 0, i, nt - 1), 0))
    lay1 = pl.BlockSpec((tr, C), lambda l, h, i: (jnp.where(l == 0, 0, i), 0))
    return pl.pallas_call(
        body, name=name, grid=(DEPTH, 2, nt),
        in_specs=[lay0, lay0, lay1, lay1, stk, stk, stk],
        out_specs=[stk] * 4, out_shape=[jax.ShapeDtypeStruct(w.shape, f32)] * 4,
        compiler_params=_cp(("arbitrary", "arbitrary", "arbitrary")),
    )(*halves[0], *halves[1], w, m, v)


def adamw_small(name, g, w, m, v):
    def body(g_r, w_r, m_r, v_r, d_o, m_o, v_o):
        d_o[...], m_o[...], v_o[...] = _adamw_math(w_r[...], g_r[...], m_r[...], v_r[...])

    return pl.pallas_call(body, name=name, out_shape=[jax.ShapeDtypeStruct(w.shape, f32)] * 3)(g, w, m, v)


WEIGHTS = ("w_in", "conv_w", "gmlp_ln_g", "gmlp_ln_b", "w_s", "b_s", "p_a", "p_b", "p_c", "w_o", "ln1_g", "ln1_b",
           "w_gate", "w_up", "w_down", "ln2_g", "ln2_b")
VECS = ("ln1_g", "ln1_b", "ln2_g", "ln2_b", "gmlp_ln_g", "gmlp_ln_b")
ROWS_VEC, ROWS_BS, ROWS_WS, ROWS_CONV = D // LANES, 8, 8 * BLK, 3 * D // LANES
ROWS_LAYER = len(VECS) * ROWS_VEC + ROWS_BS + ROWS_WS + ROWS_CONV


def _pack_small(per_layer, tail):
    parts = []
    for P in per_layer:
        parts += [P[n].reshape(ROWS_VEC, LANES) for n in VECS]
        parts += [P["b_s"].reshape(ROWS_BS, LANES), P["w_s"].reshape(ROWS_WS, LANES), P["conv_w"].reshape(ROWS_CONV, LANES)]
    return jnp.concatenate(parts + [tail], axis=0)


def _unpack_small(pack):
    out = []
    for l in range(DEPTH):
        r = l * ROWS_LAYER
        P = {}
        for n in VECS:
            P[n] = pack[r:r + ROWS_VEC].reshape(D)
            r += ROWS_VEC
        P["b_s"] = pack[r:r + ROWS_BS].reshape(8, BLK)
        r += ROWS_BS
        P["w_s"] = pack[r:r + ROWS_WS].reshape(8, BLK, BLK)
        r += ROWS_WS
        P["conv_w"] = pack[r:r + ROWS_CONV].reshape(3, D)
        out.append(P)
    return out, pack[DEPTH * ROWS_LAYER:]


def kernel(x, positions, w_in, conv_w, gmlp_ln_g, gmlp_ln_b, w_s, b_s, p_a, p_b, p_c, w_o, ln1_g, ln1_b, w_gate, w_up, w_down, ln2_g, ln2_b, loss_target, m_w_in, m_conv_w, m_gmlp_ln_g, m_gmlp_ln_b, m_w_s, m_b_s, m_p_a, m_p_b, m_p_c, m_w_o, m_ln1_g, m_ln1_b, m_w_gate, m_w_up, m_w_down, m_ln2_g, m_ln2_b, v_w_in, v_conv_w, v_gmlp_ln_g, v_gmlp_ln_b, v_w_s, v_b_s, v_p_a, v_p_b, v_p_c, v_w_o, v_ln1_g, v_ln1_b, v_w_gate, v_w_up, v_w_down, v_ln2_g, v_ln2_b):
    Wt = dict(w_in=w_in, conv_w=conv_w, gmlp_ln_g=gmlp_ln_g, gmlp_ln_b=gmlp_ln_b, w_s=w_s, b_s=b_s, p_a=p_a, p_b=p_b,
              p_c=p_c, w_o=w_o, ln1_g=ln1_g, ln1_b=ln1_b, w_gate=w_gate, w_up=w_up, w_down=w_down, ln2_g=ln2_g, ln2_b=ln2_b)
    Mt = dict(w_in=m_w_in, conv_w=m_conv_w, gmlp_ln_g=m_gmlp_ln_g, gmlp_ln_b=m_gmlp_ln_b, w_s=m_w_s, b_s=m_b_s, p_a=m_p_a,
              p_b=m_p_b, p_c=m_p_c, w_o=m_w_o, ln1_g=m_ln1_g, ln1_b=m_ln1_b, w_gate=m_w_gate, w_up=m_w_up,
              w_down=m_w_down, ln2_g=m_ln2_g, ln2_b=m_ln2_b)
    Vt = dict(w_in=v_w_in, conv_w=v_conv_w, gmlp_ln_g=v_gmlp_ln_g, gmlp_ln_b=v_gmlp_ln_b, w_s=v_w_s, b_s=v_b_s, p_a=v_p_a,
              p_b=v_p_b, p_c=v_p_c, w_o=v_w_o, ln1_g=v_ln1_g, ln1_b=v_ln1_b, w_gate=v_w_gate, w_up=v_w_up,
              w_down=v_w_down, ln2_g=v_ln2_g, ln2_b=v_ln2_b)
    chip = 2 * lax.axis_index("x") + lax.axis_index("y")
    cw = D // NCHIP

    full = gather_weights([Wt[n].astype(MX) for n in BIG] + [conv_w])
    layers = []
    for l in range(DEPTH):
        Wl = dict(zip(BIG, (f[l] for f in full[:NBIG])))
        for n in ("p_a", "p_c", "w_o"):
            Wl[n] = Wl[n].reshape(D, D)
        Wl["conv_w"] = full[NBIG][l].transpose(1, 0, 2).reshape(3, D)
        for n in VECS + ("w_s", "b_s"):
            Wl[n] = Wt[n][l]
        layers.append(Wl)

    rs_state = [None] * DEPTH

    def start_exchange(l, g):
        rs_state[l], started = reduce_scatter_begin(l, g)
        return started

    lsum, grad_x, grads = local_step(x[0], positions[0], loss_target[0], layers, start_exchange)

    red = [None] * DEPTH
    red[1] = reduce_scatter_finish(1, rs_state[1], [grad_x])
    pack = _pack_small([{n: g[n] for n in VECS + ("b_s", "w_s", "conv_w")} for g in grads], lsum)
    small, tail = _unpack_small(allreduce_small(pack))
    loss = tail[0, 0]

    G, DW, NM, NV = {}, {}, {}, {}
    zc = jnp.zeros((3, D), f32)
    wp = _pack_small([{**{n: Wt[n][l] for n in VECS + ("b_s", "w_s")}, "conv_w": zc} for l in range(DEPTH)], jnp.zeros((8, LANES), f32))
    mp = _pack_small([{**{n: Mt[n][l] for n in VECS + ("b_s", "w_s")}, "conv_w": zc} for l in range(DEPTH)], jnp.zeros((8, LANES), f32))
    vp = _pack_small([{**{n: Vt[n][l] for n in VECS + ("b_s", "w_s")}, "conv_w": zc} for l in range(DEPTH)], jnp.ones((8, LANES), f32))
    gp = _pack_small(small, jnp.zeros((8, LANES), f32))
    outs = [_unpack_small(a)[0] for a in adamw_small("adamw_small", gp, wp, mp, vp)]
    for n in VECS + ("b_s", "w_s"):
        G[n] = jnp.stack([small[l][n] for l in range(DEPTH)])
        DW[n], NM[n], NV[n] = (jnp.stack([o[l][n] for l in range(DEPTH)]) for o in outs)
    gconv = jnp.stack([lax.dynamic_slice(small[l]["conv_w"], (0, chip * cw), (3, cw)) for l in range(DEPTH)])
    G["conv_w"] = gconv
    flat = lambda a: a.reshape(DEPTH * 3, cw)
    d, m2, v2 = adamw_small("adamw_conv", flat(gconv), flat(conv_w), flat(m_conv_w), flat(v_conv_w))
    DW["conv_w"], NM["conv_w"], NV["conv_w"] = (a.reshape(DEPTH, 3, cw) for a in (d, m2, v2))

    red[0] = reduce_scatter_finish(0, rs_state[0], [d, DW["ln2_b"], red[1]["w_in"][1]])
    for n in BIG:
        G[n], DW[n], NM[n], NV[n] = adamw_big("adamw_" + n, (red[0][n], red[1][n]), Wt[n], Mt[n], Vt[n])

    return (loss, grad_x[None], *[G[n] for n in WEIGHTS], *[DW[n] for n in WEIGHTS], *[NM[n] for n in WEIGHTS],
            *[NV[n] for n in WEIGHTS])
```

```python
import functools
import math

import jax
import jax.numpy as jnp
from jax import lax
from jax.experimental import pallas as pl
from jax.experimental.pallas import tpu as pltpu

D = 1024
NIN = 12800
DFF = 2816
NCHIP = 4
FB = DFF // NCHIP
WIN_SHARD = NIN // NCHIP
DEPTH = 2
GROUPS = ((128, 1), (512, 4), (2048, 16))
HD = 64
BLK = 128
AO = 512
ALPHA = (2 * DEPTH) ** 0.25
EPS = 1e-5
ROPE_THETA = 10000.0
LANES = 128
NEG = -1e30

C_GATES, C_BCH, C_QKV, C_UV = 0, 3 * D, 6 * D, 6 * D + 9 * AO

MX = jnp.bfloat16
ACT = jnp.bfloat16

ADAM_LR, ADAM_B1, ADAM_B2, ADAM_EPS, ADAM_WD, ADAM_STEP = 0.001, 0.9, 0.999, 1e-08, 0.01, 10

f32 = jnp.float32
NT = (((1,), (1,)), ((), ()))
TN = (((0,), (0,)), ((), ()))


def _cp(sem, vmem_mb=48):
    return pltpu.CompilerParams(dimension_semantics=sem, vmem_limit_bytes=vmem_mb << 20)


def _dot(a, b, dims=None):
    if dims is None:
        return jnp.dot(a, b, preferred_element_type=f32)
    return lax.dot_general(a, b, dims, preferred_element_type=f32)


def _ln_stats(r):
    mu = jnp.mean(r, axis=-1, keepdims=True)
    xc = r - mu
    var = jnp.mean(xc * xc, axis=-1, keepdims=True)
    rstd = lax.rsqrt(var + EPS)
    return xc * rstd, rstd


def _ln_bwd(dy, xhat, rstd, g):
    dxh = dy * g
    return rstd * (dxh - jnp.mean(dxh, axis=-1, keepdims=True) - xhat * jnp.mean(dxh * xhat, axis=-1, keepdims=True))


def _gelu(x):
    return 0.5 * x * (1.0 + lax.erf(x * (1.0 / math.sqrt(2.0))))


def _gelu_grad(x):
    return 0.5 * (1.0 + lax.erf(x * (1.0 / math.sqrt(2.0)))) + x * jnp.exp(-0.5 * x * x) * (1.0 / math.sqrt(2.0 * math.pi))


def _sigmoid(x):
    return 0.5 * jnp.tanh(0.5 * x) + 0.5


def _acc_rows(o_ref, first, val):
    @pl.when(first)
    def _():
        o_ref[...] = jnp.zeros_like(o_ref)
    o_ref[...] += jnp.sum(val, axis=0, keepdims=True)


def mm_in(x, w):
    T = x.shape[0]
    tm, tn = min(1024, T), 1280

    def body(x_ref, w_ref, o_ref, xb):
        @pl.when(pl.program_id(1) == 0)
        def _():
            xb[...] = x_ref[...].astype(MX)
        o_ref[...] = _dot(xb[...], w_ref[...]).astype(o_ref.dtype)

    return pl.pallas_call(
        body, name="mm_in", grid=(T // tm, NIN // tn),
        in_specs=[pl.BlockSpec((tm, D), lambda i, j: (i, 0)), pl.BlockSpec((D, tn), lambda i, j: (0, j))],
        out_specs=pl.BlockSpec((tm, tn), lambda i, j: (i, j)),
        out_shape=jax.ShapeDtypeStruct((T, NIN), ACT),
        scratch_shapes=[pltpu.VMEM((tm, D), MX)],
        compiler_params=_cp(("parallel", "arbitrary")),
    )(x, w)


HALO = 16
TM_AC = 256


def _uv_specs():
    return [pl.BlockSpec((TM_AC, 512), functools.partial(lambda i, j: (i, j), j=C_UV // 512 + j)) for j in range(4)]


def _gmlp_fwd(up, vp, ws_ref, bs_ref, lg, lb):
    u = _gelu(up)
    xhat, rstd = _ln_stats(_gelu(vp))
    vn = xhat * lg + lb
    vnb = vn.astype(MX)
    rows = []
    for c in range(up.shape[0] // BLK):
        r = slice(c * BLK, (c + 1) * BLK)
        rows.append(jnp.concatenate(
            [_dot(ws_ref[g], vnb[r, g * BLK:(g + 1) * BLK]) + bs_ref[g] for g in range(8)], axis=1))
    return u, vn, xhat, rstd, jnp.concatenate(rows, axis=0)


def mix_ac_fwd(proj, conv_w, wst, bsx, lg, lb):
    T = proj.shape[0]
    tm = TM_AC

    def body(bch, halo, u0, u1, v0, v1, cw, ws, bs, lg_ref, lb_ref, ya, yc, zs):
        i = pl.program_id(0)
        pb = bch[...].astype(f32)
        z = pb[:, D:2 * D] * pb[:, 2 * D:]
        hz = halo[:, :D].astype(f32) * halo[:, D:].astype(f32)
        zs[0:HALO, :] = jnp.where(i > 0, hz, 0.0)
        zs[HALO:HALO + tm, :] = z
        cv = cw[0:1, :] * zs[HALO - 2:HALO - 2 + tm, :] + cw[1:2, :] * zs[HALO - 1:HALO - 1 + tm, :] + cw[2:3, :] * z
        ya[...] = (pb[:, :D] * cv).astype(ya.dtype)
        up = jnp.concatenate([u0[...], u1[...]], axis=1).astype(f32)
        vp = jnp.concatenate([v0[...], v1[...]], axis=1).astype(f32)
        u, _, _, _, sp = _gmlp_fwd(up, vp, ws, bs, lg_ref[...], lb_ref[...])
        yc[...] = (u * sp).astype(yc.dtype)

    full = lambda shape: pl.BlockSpec(shape, lambda i: (0,) * len(shape))
    return pl.pallas_call(
        body, name="mix_ac_fwd", grid=(T // tm,),
        in_specs=[pl.BlockSpec((tm, 3 * D), lambda i: (i, 1)),
                  pl.BlockSpec((HALO, 2 * D), lambda i: (jnp.maximum(i * (tm // HALO) - 1, 0), 2)),
                  *_uv_specs(), full((3, D)), full((8, BLK, BLK)), full((8, BLK, BLK)), full((1, D)), full((1, D))],
        out_specs=[pl.BlockSpec((tm, D), lambda i: (i, 0))] * 2,
        out_shape=[jax.ShapeDtypeStruct((T, D), MX)] * 2,
        scratch_shapes=[pltpu.VMEM((HALO + tm, D), f32)],
        compiler_params=_cp(("parallel",)),
    )(proj, proj, proj, proj, proj, proj, conv_w, wst, bsx, lg, lb)


def _swap_halves(x):
    lane = lax.broadcasted_iota(jnp.int32, x.shape, 1)
    return jnp.where((lane % HD) < HD // 2, pltpu.roll(x, x.shape[1] - HD // 2, 1), pltpu.roll(x, HD // 2, 1))


def _tile4(t):
    return jnp.concatenate([t] * (AO // LANES), axis=1)


TM_FOLD = 512


def _fold_out(nat, x, out_ref, d):
    if d == 1:
        out_ref[0] = x.astype(out_ref.dtype)
        return
    rows = x.shape[0] // d
    for j in range(AO // LANES):
        nat[j] = x[:, j * LANES:(j + 1) * LANES]
    for r in range(d):
        out_ref[r] = jnp.concatenate(
            [nat.at[j][pl.ds(r, rows, stride=d), :] for j in range(AO // LANES)], axis=1).astype(out_ref.dtype)


def _unfold_in(nat, in_ref, d):
    if d == 1:
        return in_ref[0].astype(f32)
    rows = in_ref.shape[1]
    for r in range(d):
        v = in_ref[r].astype(f32)
        for j in range(AO // LANES):
            nat.at[j][pl.ds(r, rows, stride=d), :] = v[:, j * LANES:(j + 1) * LANES]
    return jnp.concatenate([nat[j] for j in range(AO // LANES)], axis=1)


def fold_rope(proj, cos_t, sin_t, g, d):
    T = proj.shape[0]
    tm = TM_FOLD
    rows = tm // d

    def body(x_ref, c_ref, s_ref, q_o, k_o, v_o, nat):
        cos, sin = _tile4(c_ref[...]), _tile4(s_ref[...])
        for part, out, scale in ((0, q_o, HD ** -0.5), (1, k_o, 1.0), (2, v_o, None)):
            x = x_ref[:, part * AO:(part + 1) * AO].astype(f32)
            if scale is not None:
                x = (x * cos + _swap_halves(x) * sin) * scale
            _fold_out(nat, x, out, d)

    fold_spec = pl.BlockSpec((d, rows, AO), lambda i: (0, i, 0))
    return pl.pallas_call(
        body, name=f"fold_rope{g}", grid=(T // tm,),
        in_specs=[pl.BlockSpec((tm, 3 * AO), lambda i: (i, C_QKV // (3 * AO) + g)),
                  pl.BlockSpec((tm, LANES), lambda i: (i, 0)), pl.BlockSpec((tm, LANES), lambda i: (i, 0))],
        out_specs=[fold_spec] * 3,
        out_shape=[jax.ShapeDtypeStruct((d, T // d, AO), MX)] * 3,
        scratch_shapes=[pltpu.VMEM((AO // LANES, tm, LANES), f32)],
        compiler_params=_cp(("parallel",)),
    )(proj, cos_t, sin_t)


def _stack_heads(x):
    lane = lax.broadcasted_iota(jnp.int32, x.shape, 1)
    z = jnp.zeros_like(x)
    return jnp.concatenate([jnp.where(lane < HD, x, z), jnp.where(lane >= HD, x, z)], axis=0)


def _unstack_heads(y):
    lane = lax.broadcasted_iota(jnp.int32, (BLK, LANES), 1)
    return jnp.where(lane < HD, y[:BLK], y[BLK:])


def _window_masks():
    row = lax.broadcasted_iota(jnp.int32, (2 * BLK, 2 * BLK), 0) % BLK
    col = lax.broadcasted_iota(jnp.int32, (2 * BLK, 2 * BLK), 1)
    return (col < BLK) & (col >= row), (col >= BLK) & (col - BLK <= row)


def _two_blocks(ref, b):
    r0 = pl.multiple_of(b * BLK, BLK)
    rp = pl.multiple_of(jnp.maximum(b - 1, 0) * BLK, BLK)
    return jnp.concatenate([ref[pl.ds(rp, BLK), :], ref[pl.ds(r0, BLK), :]], axis=0)


def attn_fwd(qf, kf, vf, g, nb):
    T = qf.shape[0]

    def body(q_ref, k_ref, v_ref, o_ref, l_ref):
        prev_m, cur_m = _window_masks()

        def step(b, carry):
            r0 = pl.multiple_of(b * BLK, BLK)
            qs = _stack_heads(q_ref[pl.ds(r0, BLK), :])
            s = _dot(qs, _two_blocks(k_ref, b), NT)
            s = jnp.where(cur_m | (prev_m & ((b % nb) != 0)), s, NEG)
            m = jnp.max(s, axis=-1, keepdims=True)
            p = jnp.exp(s - m)
            l = jnp.sum(p, axis=-1, keepdims=True)
            o = _dot(p.astype(MX), _two_blocks(v_ref, b)) / l
            o_ref[pl.ds(r0, BLK), :] = _unstack_heads(o)
            l_ref[pl.ds(r0, BLK), :] = _unstack_heads(jnp.broadcast_to(m + jnp.log(l), (2 * BLK, LANES)))
            return carry

        lax.fori_loop(0, T // BLK, step, 0, unroll=4)

    spec = pl.BlockSpec((T, LANES), lambda j: (0, j))
    return pl.pallas_call(
        body, name=f"attn_fwd{g}", grid=(AO // LANES,),
        in_specs=[spec] * 3, out_specs=[spec] * 2,
        out_shape=[jax.ShapeDtypeStruct((T, AO), f32)] * 2,
        compiler_params=_cp(("parallel",), 56),
    )(qf, kf, vf)


def _group_weights(lses):
    m = jnp.maximum(jnp.maximum(lses[0], lses[1]), lses[2])
    e = [jnp.exp(l - m) for l in lses]
    inv = 1.0 / (e[0] + e[1] + e[2])
    return [x * inv for x in e]


def _fold_specs(T, tm):
    specs = []
    for _, d in GROUPS:
        specs.append(pl.BlockSpec((d, tm // d, AO), lambda i: (0, i, 0)))
    return specs


def combine_fwd(os_, lses):
    T = os_[0].shape[0] * os_[0].shape[1]
    tm = TM_FOLD

    def body(o0, o1, o2, l0, l1, l2, y_ref, nat):
        o = [_unfold_in(nat, r, d) for r, (_, d) in zip((o0, o1, o2), GROUPS)]
        ls = [_unfold_in(nat, r, d) for r, (_, d) in zip((l0, l1, l2), GROUPS)]
        w = _group_weights(ls)
        y_ref[...] = (w[0] * o[0] + w[1] * o[1] + w[2] * o[2]).astype(y_ref.dtype)

    specs = _fold_specs(T, tm)
    return pl.pallas_call(
        body, name="combine_fwd", grid=(T // tm,),
        in_specs=specs + specs, out_specs=pl.BlockSpec((tm, AO), lambda i: (i, 0)),
        out_shape=jax.ShapeDtypeStruct((T, AO), MX),
        scratch_shapes=[pltpu.VMEM((AO // LANES, tm, LANES), f32)],
        compiler_params=_cp(("parallel",)),
    )(*os_, *lses)


TM_MIX = 256


def mix_out_fwd(proj, ya, yb, yc, x0, pa, pb, pc, wo, g1, b1):
    T = x0.shape[0]
    tm = min(TM_MIX, T)

    def body(gt, ya_r, yb_r, yc_r, x0_r, pa_r, pb_r, pc_r, wo_r, g_r, b_r, mabc, m_o, r1_o, x1_o):
        ma = _dot(ya_r[...], pa_r[...])
        ybv = yb_r[...]
        mb = jnp.concatenate([_dot(ybv, pb_r[k]) for k in range(NCHIP)], axis=1)
        mc = _dot(yc_r[...], pc_r[...])
        m = jnp.zeros((tm, D), f32)
        for j, mm in enumerate((ma, mb, mc)):
            mabc[:, j * D:(j + 1) * D] = mm.astype(mabc.dtype)
            m = m + _sigmoid(gt[:, j * D:(j + 1) * D].astype(f32)) * mm
        mb16 = m.astype(MX)
        m_o[...] = mb16
        r1 = ALPHA * x0_r[...] + _dot(mb16, wo_r[...])
        r1_o[...] = r1
        xhat, _ = _ln_stats(r1)
        x1_o[...] = xhat * g_r[...] + b_r[...]

    full = lambda shape: pl.BlockSpec(shape, lambda i: (0,) * len(shape))
    tile = lambda w: pl.BlockSpec((tm, w), lambda i: (i, 0))
    return pl.pallas_call(
        body, name="mix_out_fwd", grid=(T // tm,),
        in_specs=[tile(3 * D), tile(D), tile(AO), tile(D), tile(D), full((D, D)), full((NCHIP, AO, D // NCHIP)),
                  full((D, D)), full((D, D)), full((1, D)), full((1, D))],
        out_specs=[tile(3 * D), tile(D), tile(D), tile(D)],
        out_shape=[jax.ShapeDtypeStruct((T, 3 * D), MX), jax.ShapeDtypeStruct((T, D), MX),
                   jax.ShapeDtypeStruct((T, D), f32), jax.ShapeDtypeStruct((T, D), f32)],
        compiler_params=_cp(("parallel",), 56),
    )(proj, ya, yb, yc, x0, pa, pb, pc, wo, g1, b1)


TM_FF = 512


def ffn_up_fwd(x1, wg, wu):
    T = x1.shape[0]
    tm = min(TM_FF, T)

    def body(x_r, wg_r, wu_r, g_o, u_o, h_o, xb):
        @pl.when(pl.program_id(1) == 0)
        def _():
            xb[...] = x_r[...].astype(MX)
        gate = _dot(xb[...], wg_r[0])
        up = _dot(xb[...], wu_r[0])
        g_o[0] = gate.astype(g_o.dtype)
        u_o[0] = up.astype(u_o.dtype)
        h_o[0] = (gate * _sigmoid(gate) * up).astype(h_o.dtype)

    wspec = pl.BlockSpec((1, D, FB), lambda i, k: (k, 0, 0))
    ospec = pl.BlockSpec((1, tm, FB), lambda i, k: (k, i, 0))
    return pl.pallas_call(
        body, name="ffn_up_fwd", grid=(T // tm, NCHIP),
        in_specs=[pl.BlockSpec((tm, D), lambda i, k: (i, 0)), wspec, wspec],
        out_specs=[ospec] * 3,
        out_shape=[jax.ShapeDtypeStruct((NCHIP, T, FB), ACT)] * 2 + [jax.ShapeDtypeStruct((NCHIP, T, FB), MX)],
        scratch_shapes=[pltpu.VMEM((tm, D), MX)],
        compiler_params=_cp(("parallel", "arbitrary")),
    )(x1, wg, wu)


def ffn_down_fwd(hh, wd, x1, g2, b2):
    T = x1.shape[0]
    tm = min(TM_FF, T)

    def body(h_r, w_r, x_r, g_r, b_r, r2_o, x2_o):
        r2 = ALPHA * x_r[...]
        for k in range(NCHIP):
            r2 = r2 + _dot(h_r[k], w_r[k])
        r2_o[...] = r2
        xhat, _ = _ln_stats(r2)
        x2_o[...] = xhat * g_r[...] + b_r[...]

    tile = pl.BlockSpec((tm, D), lambda i: (i, 0))
    vec = pl.BlockSpec((1, D), lambda i: (0, 0))
    return pl.pallas_call(
        body, name="ffn_down_fwd", grid=(T // tm,),
        in_specs=[pl.BlockSpec((NCHIP, tm, FB), lambda i: (0, i, 0)), pl.BlockSpec((NCHIP, FB, D), lambda i: (0, 0, 0)),
                  tile, vec, vec],
        out_specs=[tile, tile], out_shape=[jax.ShapeDtypeStruct((T, D), f32)] * 2,
        compiler_params=_cp(("parallel",)),
    )(hh, wd, x1, g2, b2)


def loss_grad(y, tgt):
    T = y.shape[0]
    tm = min(512, T)

    def body(y_r, t_r, l_o, dy_o):
        e = y_r[...] - t_r[...]
        dy_o[...] = e * (1.0 / D)

        @pl.when(pl.program_id(0) == 0)
        def _():
            l_o[...] = jnp.zeros_like(l_o)
        l_o[...] += (0.5 / D) * jnp.sum(e * e)

    tile = pl.BlockSpec((tm, D), lambda i: (i, 0))
    return pl.pallas_call(
        body, name="loss_grad", grid=(T // tm,),
        in_specs=[tile, tile], out_specs=[pl.BlockSpec((8, LANES), lambda i: (0, 0)), tile],
        out_shape=[jax.ShapeDtypeStruct((8, LANES), f32), jax.ShapeDtypeStruct((T, D), f32)],
        compiler_params=_cp(("arbitrary",)),
    )(y, tgt)


def ffn_down_bwd(dx2, r2, g2, wd, gate, up):
    T = dx2.shape[0]
    tm = min(TM_FF, T)

    def body(dx_r, r_r, g_r, w_r, ga_r, up_r, dr_o, dg_o, du_o, dlg_o, dlb_o, drb):
        i, k = pl.program_id(0), pl.program_id(1)

        @pl.when(k == 0)
        def _():
            xhat, rstd = _ln_stats(r_r[...])
            dx = dx_r[...]
            _acc_rows(dlg_o, i == 0, dx * xhat)
            _acc_rows(dlb_o, i == 0, dx)
            dr = _ln_bwd(dx, xhat, rstd, g_r[...])
            dr_o[...] = dr
            drb[...] = dr.astype(MX)

        dhh = _dot(drb[...], w_r[0], NT)
        gate_v, up_v = ga_r[0].astype(f32), up_r[0].astype(f32)
        sg = _sigmoid(gate_v)
        dg_o[0] = (dhh * up_v * sg * (1.0 + gate_v * (1.0 - sg))).astype(dg_o.dtype)
        du_o[0] = (dhh * gate_v * sg).astype(du_o.dtype)

    tile = pl.BlockSpec((tm, D), lambda i, k: (i, 0))
    vec = pl.BlockSpec((1, D), lambda i, k: (0, 0))
    blk = pl.BlockSpec((1, tm, FB), lambda i, k: (k, i, 0))
    return pl.pallas_call(
        body, name="ffn_down_bwd", grid=(T // tm, NCHIP),
        in_specs=[tile, tile, vec, pl.BlockSpec((1, FB, D), lambda i, k: (k, 0, 0)), blk, blk],
        out_specs=[tile, blk, blk, vec, vec],
        out_shape=[jax.ShapeDtypeStruct((T, D), f32)] + [jax.ShapeDtypeStruct((NCHIP, T, FB), MX)] * 2
        + [jax.ShapeDtypeStruct((1, D), f32)] * 2,
        scratch_shapes=[pltpu.VMEM((tm, D), MX)],
        compiler_params=_cp(("arbitrary", "arbitrary")),
    )(dx2, r2, g2, wd, gate, up)


def ffn_up_bwd(dr2, dgate, dup, wg, wu, r1, g1):
    T = dr2.shape[0]
    tm = min(TM_FF, T)

    def body(dr2_r, dg_r, du_r, wg_r, wu_r, r1_r, g_r, dr1_o, dlg_o, dlb_o, acc):
        i, k = pl.program_id(0), pl.program_id(1)

        @pl.when(k == 0)
        def _():
            acc[...] = ALPHA * dr2_r[...]
        acc[...] += _dot(dg_r[0], wg_r[0], NT) + _dot(du_r[0], wu_r[0], NT)

        @pl.when(k == NCHIP - 1)
        def _():
            dx = acc[...]
            xhat, rstd = _ln_stats(r1_r[...])
            _acc_rows(dlg_o, i == 0, dx * xhat)
            _acc_rows(dlb_o, i == 0, dx)
            dr1_o[...] = _ln_bwd(dx, xhat, rstd, g_r[...])

    tile = pl.BlockSpec((tm, D), lambda i, k: (i, 0))
    vec = pl.BlockSpec((1, D), lambda i, k: (0, 0))
    blk = pl.BlockSpec((1, tm, FB), lambda i, k: (k, i, 0))
    wspec = pl.BlockSpec((1, D, FB), lambda i, k: (k, 0, 0))
    return pl.pallas_call(
        body, name="ffn_up_bwd", grid=(T // tm, NCHIP),
        in_specs=[tile, blk, blk, wspec, wspec, tile, vec],
        out_specs=[tile, vec, vec],
        out_shape=[jax.ShapeDtypeStruct((T, D), f32)] + [jax.ShapeDtypeStruct((1, D), f32)] * 2,
        scratch_shapes=[pltpu.VMEM((tm, D), f32)],
        compiler_params=_cp(("arbitrary", "arbitrary")),
    )(dr2, dgate, dup, wg, wu, r1, g1)


def mix_out_bwd(dr1, proj, mabc, wo, pa, pb, pc):
    T = dr1.shape[0]
    tm = min(TM_MIX, T)

    def body(dr_r, gt, mabc_r, wo_r, pa_r, pb_r, pc_r, dmabc_o, dgt_o, dya_o, dyb_o, dyc_o):
        dm = _dot(dr_r[...].astype(MX), wo_r[...], NT)
        dmx = []
        for j in range(3):
            s = _sigmoid(gt[:, j * D:(j + 1) * D].astype(f32))
            v = (dm * s).astype(MX)
            dmx.append(v)
            dmabc_o[:, j * D:(j + 1) * D] = v
            dgt_o[:, j * D:(j + 1) * D] = (dm * mabc_r[:, j * D:(j + 1) * D].astype(f32) * s * (1.0 - s)).astype(dgt_o.dtype)
        dya_o[...] = _dot(dmx[0], pa_r[...], NT)
        dyb = jnp.zeros((tm, AO), f32)
        for k in range(NCHIP):
            dyb = dyb + _dot(dmx[1][:, k * (D // NCHIP):(k + 1) * (D // NCHIP)], pb_r[k], NT)
        dyb_o[...] = dyb
        dyc_o[...] = _dot(dmx[2], pc_r[...], NT)

    full = lambda shape: pl.BlockSpec(shape, lambda i: (0,) * len(shape))
    tile = lambda w: pl.BlockSpec((tm, w), lambda i: (i, 0))
    return pl.pallas_call(
        body, name="mix_out_bwd", grid=(T // tm,),
        in_specs=[tile(D), tile(3 * D), tile(3 * D), full((D, D)), full((D, D)), full((NCHIP, AO, D // NCHIP)), full((D, D))],
        out_specs=[tile(3 * D), tile(3 * D), tile(D), tile(AO), tile(D)],
        out_shape=[jax.ShapeDtypeStruct((T, 3 * D), MX), jax.ShapeDtypeStruct((T, 3 * D), MX),
                   jax.ShapeDtypeStruct((T, D), f32), jax.ShapeDtypeStruct((T, AO), f32), jax.ShapeDtypeStruct((T, D), f32)],
        compiler_params=_cp(("parallel",), 56),
    )(dr1, proj, mabc, wo, pa, pb, pc)


def transpose_cast(x):
    T = x.shape[0]
    tm = min(512, T)

    def body(x_r, o_r):
        o_r[...] = x_r[...].T.astype(o_r.dtype)

    return pl.pallas_call(
        body, name="transpose_cast", grid=(T // tm,),
        in_specs=[pl.BlockSpec((tm, D), lambda i: (i, 0))], out_specs=pl.BlockSpec((D, tm), lambda i: (0, i)),
        out_shape=jax.ShapeDtypeStruct((D, T), MX), compiler_params=_cp(("parallel",)),
    )(x)


def tn_matmul(name, a, b, a_spec, b_spec, out_shape, out_spec, grid, a_is_t=False):
    nt = len(grid) - 1

    def body(a_r, b_r, o_r):
        @pl.when(pl.program_id(nt) == 0)
        def _():
            o_r[...] = jnp.zeros_like(o_r)
        av = a_r[...].reshape(a_r.shape[-2:]).astype(MX)
        bv = b_r[...].reshape(b_r.shape[-2:]).astype(MX)
        o_r[...] += _dot(av, bv, None if a_is_t else TN).reshape(o_r.shape)

    return pl.pallas_call(
        body, name=name, grid=grid, in_specs=[a_spec, b_spec], out_specs=out_spec,
        out_shape=jax.ShapeDtypeStruct(out_shape, f32),
        compiler_params=_cp(("parallel",) * nt + ("arbitrary",), 56),
    )(a, b)


def attn_pre_bwd(dyb, os_, lses):
    T = dyb.shape[0]
    tm = TM_FOLD

    def body(dy_r, o0, o1, o2, l0, l1, l2, ones_r, d0, d1, d2, f0, f1, f2, nat):
        o = [_unfold_in(nat, r, d) for r, (_, d) in zip((o0, o1, o2), GROUPS)]
        ls = [_unfold_in(nat, r, d) for r, (_, d) in zip((l0, l1, l2), GROUPS)]
        w = _group_weights(ls)
        dy = dy_r[...]
        t = dy * (w[0] * o[0] + w[1] * o[1] + w[2] * o[2])
        hi = t.astype(MX)
        lo = (t - hi.astype(f32)).astype(MX)
        c = _dot(hi, ones_r[...]) + _dot(lo, ones_r[...])
        for wg, do_o, df_o, (_, d) in zip(w, (d0, d1, d2), (f0, f1, f2), GROUPS):
            _fold_out(nat, wg * dy, do_o, d)
            _fold_out(nat, -wg * c, df_o, d)

    specs = _fold_specs(T, tm)
    return pl.pallas_call(
        body, name="attn_pre_bwd", grid=(T // tm,),
        in_specs=[pl.BlockSpec((tm, AO), lambda i: (i, 0))] + specs + specs + [pl.BlockSpec((AO, AO), lambda i: (0, 0))],
        out_specs=specs + specs,
        out_shape=[jax.ShapeDtypeStruct((d, T // d, AO), MX) for _, d in GROUPS]
        + [jax.ShapeDtypeStruct((d, T // d, AO), f32) for _, d in GROUPS],
        scratch_shapes=[pltpu.VMEM((AO // LANES, tm, LANES), f32)],
        compiler_params=_cp(("parallel",)),
    )(dyb, *os_, *lses, _head_ones())


def _head_ones():
    i = jnp.arange(AO) // HD
    return (i[:, None] == i[None, :]).astype(MX)


def attn_bwd(qf, kf, vf, dof, lse, df, g, nb):
    T = qf.shape[0]

    def body(q_ref, k_ref, v_ref, do_ref, l_ref, d_ref, dq_ref, dk_ref, dv_ref):
        prev_m, cur_m = _window_masks()

        def head_col(ref, r0):
            v = ref[pl.ds(r0, BLK), :]
            return jnp.concatenate([v[:, 0:1], v[:, HD:HD + 1]], axis=0)

        def step(b, carry):
            dk_c, dv_c = carry
            r0 = pl.multiple_of(b * BLK, BLK)
            rp = pl.multiple_of(jnp.maximum(b - 1, 0) * BLK, BLK)
            qs, dos = _stack_heads(q_ref[pl.ds(r0, BLK), :]), _stack_heads(do_ref[pl.ds(r0, BLK), :])
            k2, v2 = _two_blocks(k_ref, b), _two_blocks(v_ref, b)
            valid = cur_m | (prev_m & ((b % nb) != 0))
            p = jnp.where(valid, jnp.exp(_dot(qs, k2, NT) - head_col(l_ref, r0)), 0.0)
            ds = (p * (_dot(dos, v2, NT) + head_col(d_ref, r0))).astype(MX)
            dq_ref[pl.ds(r0, BLK), :] = _unstack_heads(_dot(ds, k2)).astype(dq_ref.dtype)
            dk2 = _dot(ds, qs, TN)
            dv2 = _dot(p.astype(MX), dos, TN)
            dk_ref[pl.ds(rp, BLK), :] = (dk_c + dk2[:BLK]).astype(dk_ref.dtype)
            dv_ref[pl.ds(rp, BLK), :] = (dv_c + dv2[:BLK]).astype(dv_ref.dtype)
            return dk2[BLK:], dv2[BLK:]

        zero = jnp.zeros((BLK, LANES), f32)

        def two_steps(i, carry):
            return step(2 * i + 1, step(2 * i, carry))

        dk_c, dv_c = lax.fori_loop(0, T // BLK // 2, two_steps, (zero, zero))
        dk_ref[pl.ds(T - BLK, BLK), :] = dk_c.astype(dk_ref.dtype)
        dv_ref[pl.ds(T - BLK, BLK), :] = dv_c.astype(dv_ref.dtype)

    spec = pl.BlockSpec((T, LANES), lambda j: (0, j))
    return pl.pallas_call(
        body, name=f"attn_bwd{g}", grid=(AO // LANES,),
        in_specs=[spec] * 6, out_specs=[spec] * 3,
        out_shape=[jax.ShapeDtypeStruct((T, AO), MX)] * 3,
        compiler_params=_cp(("parallel",), 60),
    )(qf, kf, vf, dof, lse, df)


def unfold_rope_bwd(dqf, dkf, dvf, cos_t, sin_t, g, d):
    T = dqf.shape[0] * dqf.shape[1]
    tm = TM_FOLD

    def body(q_r, k_r, v_r, c_ref, s_ref, o_ref, nat):
        cos, sin = _tile4(c_ref[...]), _tile4(s_ref[...])
        for part, ref, scale in ((0, q_r, HD ** -0.5), (1, k_r, 1.0), (2, v_r, None)):
            x = _unfold_in(nat, ref, d)
            if scale is not None:
                x = (x * cos - _swap_halves(x) * sin) * scale
            o_ref[:, part * AO:(part + 1) * AO] = x.astype(o_ref.dtype)

    fold_spec = pl.BlockSpec((d, tm // d, AO), lambda i: (0, i, 0))
    tab = pl.BlockSpec((tm, LANES), lambda i: (i, 0))
    return pl.pallas_call(
        body, name=f"unfold_rope_bwd{g}", grid=(T // tm,),
        in_specs=[fold_spec] * 3 + [tab, tab],
        out_specs=pl.BlockSpec((tm, 3 * AO), lambda i: (i, 0)),
        out_shape=jax.ShapeDtypeStruct((T, 3 * AO), MX),
        scratch_shapes=[pltpu.VMEM((AO // LANES, tm, LANES), f32)],
        compiler_params=_cp(("parallel",)),
    )(dqf, dkf, dvf, cos_t, sin_t)


def conv_bwd(dya, proj, conv_w):
    T = dya.shape[0]
    tm = TM_AC
    last = T // tm - 1

    def body(dy_r, bch, hprev, dy_next, b_next, cw, d_o, dw_o, zs, ds):
        i = pl.program_id(0)
        pb = bch[...].astype(f32)
        bp, cp, hp = pb[:, :D], pb[:, D:2 * D], pb[:, 2 * D:]
        z = cp * hp
        hz = hprev[:, :D].astype(f32) * hprev[:, D:].astype(f32)
        zs[0:HALO, :] = jnp.where(i > 0, hz, 0.0)
        zs[HALO:HALO + tm, :] = z
        z2, z1 = zs[HALO - 2:HALO - 2 + tm, :], zs[HALO - 1:HALO - 1 + tm, :]
        cv = cw[0:1, :] * z2 + cw[1:2, :] * z1 + cw[2:3, :] * z
        dy = dy_r[...]
        dcv = dy * bp
        ds[0:tm, :] = dcv
        ds[tm:tm + HALO, :] = jnp.where(i < last, dy_next[...] * b_next[...].astype(f32), 0.0)
        dz = cw[2:3, :] * dcv + cw[1:2, :] * ds[1:1 + tm, :] + cw[0:1, :] * ds[2:2 + tm, :]
        d_o[:, :D] = (dy * cv).astype(d_o.dtype)
        d_o[:, D:2 * D] = (dz * hp).astype(d_o.dtype)
        d_o[:, 2 * D:] = (dz * cp).astype(d_o.dtype)

        @pl.when(i == 0)
        def _():
            dw_o[...] = jnp.zeros_like(dw_o)
        dw_o[0:1, :] += jnp.sum(dcv * z2, axis=0, keepdims=True)
        dw_o[1:2, :] += jnp.sum(dcv * z1, axis=0, keepdims=True)
        dw_o[2:3, :] += jnp.sum(dcv * z, axis=0, keepdims=True)

    nh = tm // HALO
    return pl.pallas_call(
        body, name="conv_bwd", grid=(T // tm,),
        in_specs=[pl.BlockSpec((tm, D), lambda i: (i, 0)), pl.BlockSpec((tm, 3 * D), lambda i: (i, 1)),
                  pl.BlockSpec((HALO, 2 * D), lambda i: (jnp.maximum(i * nh - 1, 0), 2)),
                  pl.BlockSpec((HALO, D), lambda i: (jnp.minimum((i + 1) * nh, T // HALO - 1), 0)),
                  pl.BlockSpec((HALO, D), lambda i: (jnp.minimum((i + 1) * nh, T // HALO - 1), 3)),
                  pl.BlockSpec((3, D), lambda i: (0, 0))],
        out_specs=[pl.BlockSpec((tm, 3 * D), lambda i: (i, 0)), pl.BlockSpec((3, D), lambda i: (0, 0))],
        out_shape=[jax.ShapeDtypeStruct((T, 3 * D), MX), jax.ShapeDtypeStruct((3, D), f32)],
        scratch_shapes=[pltpu.VMEM((HALO + tm, D), f32), pltpu.VMEM((tm + HALO, D), f32)],
        compiler_params=_cp(("arbitrary",)),
    )(dya, proj, proj, dya, proj, conv_w)


def gmlp_bwd(dyc, proj, wst, bsx, lg, lb):
    T = dyc.shape[0]
    tm = TM_AC
    last = T // tm - 1

    def body(dy_r, u0, u1, v0, v1, ws, bs, lg_r, lb_r, d_o, dws_o, dbs_o, dlg_o, dlb_o, bacc):
        i = pl.program_id(0)
        up = jnp.concatenate([u0[...], u1[...]], axis=1).astype(f32)
        vp = jnp.concatenate([v0[...], v1[...]], axis=1).astype(f32)
        u, vn, xhat, rstd, sp = _gmlp_fwd(up, vp, ws, bs, lg_r[...], lb_r[...])
        dy = dy_r[...]
        d_o[:, :D] = (dy * sp * _gelu_grad(up)).astype(d_o.dtype)
        dsp = dy * u
        dspb, vnb = dsp.astype(MX), vn.astype(MX)

        @pl.when(i == 0)
        def _():
            dws_o[...] = jnp.zeros_like(dws_o)
            bacc[...] = jnp.zeros_like(bacc)

        rows = []
        for c in range(tm // BLK):
            r = slice(c * BLK, (c + 1) * BLK)
            cols = []
            for g in range(8):
                cs = slice(g * BLK, (g + 1) * BLK)
                dws_o[g] += _dot(dspb[r, cs], vnb[r, cs], NT)
                bacc[g] += dsp[r, cs]
                cols.append(_dot(ws[g], dspb[r, cs], TN))
            rows.append(jnp.concatenate(cols, axis=1))
        dvn = jnp.concatenate(rows, axis=0)
        _acc_rows(dlg_o, i == 0, dvn * xhat)
        _acc_rows(dlb_o, i == 0, dvn)
        d_o[:, D:] = (_ln_bwd(dvn, xhat, rstd, lg_r[...]) * _gelu_grad(vp)).astype(d_o.dtype)

        @pl.when(i == last)
        def _():
            row = lax.broadcasted_iota(jnp.int32, (BLK, BLK), 0)
            col = lax.broadcasted_iota(jnp.int32, (BLK, BLK), 1)
            ones = jnp.ones((8, BLK), MX)
            for g in range(8):
                dws_o[g] = jnp.where(col <= row, dws_o[g], 0.0)
                a = bacc[g]
                hi = a.astype(MX)
                lo = (a - hi.astype(f32)).astype(MX)
                dbs_o[g:g + 1, :] = (_dot(ones, hi, NT) + _dot(ones, lo, NT))[0:1, :]

    full = lambda shape: pl.BlockSpec(shape, lambda i: (0,) * len(shape))
    return pl.pallas_call(
        body, name="gmlp_bwd", grid=(T // tm,),
        in_specs=[pl.BlockSpec((tm, D), lambda i: (i, 0)), *_uv_specs(), full((8, BLK, BLK)), full((8, BLK, BLK)),
                  full((1, D)), full((1, D))],
        out_specs=[pl.BlockSpec((tm, 2 * D), lambda i: (i, 0)), full((8, BLK, BLK)), full((8, BLK)), full((1, D)), full((1, D))],
        out_shape=[jax.ShapeDtypeStruct((T, 2 * D), MX), jax.ShapeDtypeStruct((8, BLK, BLK), f32),
                   jax.ShapeDtypeStruct((8, BLK), f32), jax.ShapeDtypeStruct((1, D), f32), jax.ShapeDtypeStruct((1, D), f32)],
        scratch_shapes=[pltpu.VMEM((8, BLK, BLK), f32)],
        compiler_params=_cp(("arbitrary",)),
    )(dyc, proj, proj, proj, proj, wst, bsx, lg, lb)


PART_TILES = (6, 6, 3, 3, 3, 4)
PART_START = (0, 6, 12, 15, 18, 21)
TJ = 512


def _part_specs(tm, rows_axis):
    specs = []
    for n, s in zip(PART_TILES, PART_START):
        def imap(*idx, n=n, s=s):
            i, j = idx[rows_axis], idx[1 - rows_axis]
            inside = (j >= s) & (j < s + n)
            return (jnp.where(inside, i, 0), jnp.clip(j - s, 0, n - 1))
        specs.append(pl.BlockSpec((tm, TJ), imap))
    return specs


def _with_part(j, refs, fn):
    for r, n, s in zip(refs, PART_TILES, PART_START):
        @pl.when((j >= s) & (j < s + n))
        def _():
            fn(r[...])


def dx_in(dr1, parts, w):
    T = dr1.shape[0]
    tm = min(1024, T)

    def body(dr_r, p0, p1, p2, p3, p4, p5, w_r, o_r):
        j = pl.program_id(1)

        @pl.when(j == 0)
        def _():
            o_r[...] = ALPHA * dr_r[...]

        def acc(tile):
            o_r[...] += _dot(tile, w_r[...], NT)
        _with_part(j, (p0, p1, p2, p3, p4, p5), acc)

    return pl.pallas_call(
        body, name="dx_in", grid=(T // tm, NIN // TJ),
        in_specs=[pl.BlockSpec((tm, D), lambda i, j: (i, 0))] + _part_specs(tm, 0) + [pl.BlockSpec((D, TJ), lambda i, j: (0, j))],
        out_specs=pl.BlockSpec((tm, D), lambda i, j: (i, 0)),
        out_shape=jax.ShapeDtypeStruct((T, D), f32),
        compiler_params=_cp(("parallel", "arbitrary"), 56),
    )(dr1, *parts, w)


def dw_in(x0t, parts):
    T = x0t.shape[1]
    tk = min(2048, T)

    def body(x_r, p0, p1, p2, p3, p4, p5, o_r):
        j, t = pl.program_id(0), pl.program_id(1)

        @pl.when(t == 0)
        def _():
            o_r[...] = jnp.zeros_like(o_r)

        def acc(tile):
            o_r[...] += _dot(x_r[...], tile)
        _with_part(j, (p0, p1, p2, p3, p4, p5), acc)

    return pl.pallas_call(
        body, name="dw_in", grid=(NIN // TJ, T // tk),
        in_specs=[pl.BlockSpec((D, tk), lambda j, t: (0, t))] + _part_specs(tk, 1),
        out_specs=pl.BlockSpec((D, TJ), lambda j, t: (0, j)),
        out_shape=jax.ShapeDtypeStruct((D, NIN), f32),
        compiler_params=_cp(("parallel", "arbitrary")),
    )(x0t, *parts)


def rope_tables(positions):
    half = HD // 2
    inv_freq = ROPE_THETA ** (-jnp.arange(half, dtype=f32) / half)
    ang = positions.astype(f32)[:, None] * inv_freq
    cos, sin = jnp.cos(ang), jnp.sin(ang)
    return jnp.tile(cos, (1, LANES // half)), jnp.tile(jnp.concatenate([-sin, sin], axis=1), (1, LANES // HD))


def _flat(a):
    return a.reshape(a.shape[0] * a.shape[1], a.shape[2])


def layer_fwd(x0, W, cos_t, sin_t):
    T = x0.shape[0]
    proj = mm_in(x0, W["w_in"])
    ya, yc = mix_ac_fwd(proj, W["conv_w"], W["wst"], W["bsx"], W["gmlp_ln_g"], W["gmlp_ln_b"])
    folded, os_, lses = [], [], []
    for g, (_, d) in enumerate(GROUPS):
        qf, kf, vf = fold_rope(proj, cos_t, sin_t, g, d)
        o, lse = attn_fwd(_flat(qf), _flat(kf), _flat(vf), g, T // d // BLK)
        folded.append((qf, kf, vf))
        os_.append(o.reshape(d, T // d, AO))
        lses.append(lse.reshape(d, T // d, AO))
    yb = combine_fwd(os_, lses)
    mabc, m, r1, x1 = mix_out_fwd(proj, ya, yb, yc, x0, W["p_a"], W["p_b"], W["p_c"], W["w_o"], W["ln1_g"], W["ln1_b"])
    gate, up, hh = ffn_up_fwd(x1, W["w_gate"], W["w_up"])
    r2, x2 = ffn_down_fwd(hh, W["w_down"], x1, W["ln2_g"], W["ln2_b"])
    saved = dict(x0=x0, proj=proj, ya=ya, yb=yb, yc=yc, folded=folded, os=os_, lses=lses, mabc=mabc, m=m, r1=r1,
                 x1=x1, gate=gate, up=up, hh=hh, r2=r2)
    return x2, saved


def layer_bwd(dx2, S, W, cos_t, sin_t, on_grads=None):
    T = dx2.shape[0]
    tk = min(2048, T)
    G = {}
    dr2, dgate, dup, G["ln2_g"], G["ln2_b"] = ffn_down_bwd(dx2, S["r2"], W["ln2_g"], W["w_down"], S["gate"], S["up"])
    blk_a = pl.BlockSpec((1, tk, FB), lambda k, t: (k, t, 0))
    row_b = pl.BlockSpec((tk, D), lambda k, t: (t, 0))
    G["w_down"] = tn_matmul("dw_down", S["hh"], dr2, blk_a, row_b, (NCHIP, FB, D),
                            pl.BlockSpec((1, FB, D), lambda k, t: (k, 0, 0)), (NCHIP, T // tk))
    x1t = transpose_cast(S["x1"])
    for nm, dv in (("w_gate", dgate), ("w_up", dup)):
        G[nm] = tn_matmul("d" + nm, x1t, dv, pl.BlockSpec((D, tk), lambda k, t: (0, t)), blk_a, (NCHIP, D, FB),
                          pl.BlockSpec((1, D, FB), lambda k, t: (k, 0, 0)), (NCHIP, T // tk), a_is_t=True)
    dr1, G["ln1_g"], G["ln1_b"] = ffn_up_bwd(dr2, dgate, dup, W["w_gate"], W["w_up"], S["r1"], W["ln1_g"])
    dmabc, dgates, dya, dyb, dyc = mix_out_bwd(dr1, S["proj"], S["mabc"], W["w_o"], W["p_a"], W["p_b"], W["p_c"])
    one = (1, T // tk)
    full_o = pl.BlockSpec((D, D), lambda k, t: (0, 0))
    G["w_o"] = tn_matmul("dw_o", S["m"], dr1, row_b, row_b, (D, D), full_o, one)
    G["p_a"] = tn_matmul("dp_a", S["ya"], dmabc, row_b, pl.BlockSpec((tk, D), lambda k, t: (t, 0)), (D, D), full_o, one)
    G["p_c"] = tn_matmul("dp_c", S["yc"], dmabc, row_b, pl.BlockSpec((tk, D), lambda k, t: (t, 2)), (D, D), full_o, one)
    G["p_b"] = tn_matmul("dp_b", S["yb"], dmabc, pl.BlockSpec((tk, AO), lambda k, t: (t, 0)),
                         pl.BlockSpec((tk, D // NCHIP), lambda k, t: (t, NCHIP + k)), (NCHIP, AO, D // NCHIP),
                         pl.BlockSpec((1, AO, D // NCHIP), lambda k, t: (k, 0, 0)), (NCHIP, T // tk))
    conv_w = W["conv_w"]
    if on_grads is not None:
        conv_w = conv_w + on_grads({n: G[n] for n in BIG if n != "w_in"})
    dbch, G["conv_w"] = conv_bwd(dya, S["proj"], conv_w)
    duv, G["w_s"], G["b_s"], G["gmlp_ln_g"], G["gmlp_ln_b"] = gmlp_bwd(
        dyc, S["proj"], W["wst"], W["bsx"], W["gmlp_ln_g"], W["gmlp_ln_b"])
    pre = attn_pre_bwd(dyb, S["os"], S["lses"])
    dqkv = []
    for g, (_, d) in enumerate(GROUPS):
        qf, kf, vf = S["folded"][g]
        dqf, dkf, dvf = attn_bwd(_flat(qf), _flat(kf), _flat(vf), _flat(pre[g]), _flat(S["lses"][g]), _flat(pre[3 + g]),
                                 g, T // d // BLK)
        shp = (d, T // d, AO)
        dqkv.append(unfold_rope_bwd(dqf.reshape(shp), dkf.reshape(shp), dvf.reshape(shp), cos_t, sin_t, g, d))
    parts = (dgates, dbch, *dqkv, duv)
    G["w_in"] = dw_in(transpose_cast(S["x0"]), parts)
    dx0 = dx_in(dr1, parts, W["w_in"])
    started = on_grads({"w_in": G["w_in"]}) if on_grads is not None else None
    return dx0, G, started


def prep_layer_weights(Wl):
    W = dict(Wl)
    tril = jnp.tril(jnp.ones((BLK, BLK), f32))
    W["wst"] = (Wl["w_s"] * tril[None]).astype(MX)
    W["bsx"] = jnp.broadcast_to(Wl["b_s"][:, :, None], (8, BLK, BLK))
    for n in ("gmlp_ln_g", "gmlp_ln_b", "ln1_g", "ln1_b", "ln2_g", "ln2_b"):
        W[n] = Wl[n].reshape(1, D)
    return W


def local_step(x, positions, target, layers, on_grads=None):
    cos_t, sin_t = rope_tables(positions)
    Ws = [prep_layer_weights(Wl) for Wl in layers]
    saved = []
    h = x
    for W in Ws:
        h, S = layer_fwd(h, W, cos_t, sin_t)
        saved.append(S)
    lsum, dh = loss_grad(h, target)
    grads = [None] * len(Ws)
    started = None
    for l in reversed(range(len(Ws))):
        W = Ws[l]
        if started is not None:
            W = dict(W, ln2_g=W["ln2_g"] + started)
        hook = functools.partial(on_grads, l) if on_grads is not None else None
        dh, grads[l], started = layer_bwd(dh, saved[l], W, cos_t, sin_t, hook)
    return lsum, dh, grads


MESH = pl.DeviceIdType.MESH
ANY = pl.BlockSpec(memory_space=pl.ANY)
BIG = ("w_in", "w_gate", "w_up", "w_down", "p_a", "p_b", "p_c", "w_o")
NBIG = len(BIG)


def _place():
    x, y, c = lax.axis_index("x"), lax.axis_index("y"), lax.axis_index("c")
    return x, y, c, 2 * x + y


def _rcopy(src, dst, send, recv, dev):
    return pltpu.make_async_remote_copy(src_ref=src, dst_ref=dst, send_sem=send, recv_sem=recv, device_id=dev,
                                        device_id_type=MESH)


def _cols(ref, k, width):
    start = k * width if isinstance(k, int) else pl.multiple_of(k * width, LANES)
    return ref.at[:, pl.ds(start, width)]


CHUNK_BYTES = 1 << 20


def _pieces(shape, itemsize, nbytes=CHUNK_BYTES):
    rows, cols = shape[-2], shape[-1]
    per = max(16, nbytes // (cols * itemsize) // 16 * 16)
    out = []
    for lead in (range(shape[0]) if len(shape) == 3 else (None,)):
        for r in range(0, rows, per):
            sl = (pl.ds(r, min(per, rows - r)), slice(None))
            out.append(sl if lead is None else (lead,) + sl)
    return out


def _start_pieces(src, dst, make, nbytes=CHUNK_BYTES):
    for idx in _pieces(src.shape, jnp.dtype(src.dtype).itemsize, nbytes):
        make(src.at[idx], dst.at[idx]).start()


def gather_weights(shards):
    n = len(shards)

    def body(*refs):
        srcs, dsts = refs[:n], refs[n:2 * n]
        send, recv, own_send, own_recv = refs[2 * n:]
        x, y, c, k = _place()
        sib = (x, y, 1 - c)
        chips = [(1 - x, y), (x, 1 - y), (1 - x, 1 - y)]

        def slot(a, layer, pos):
            if a == 0:
                return _cols(dsts[0].at[layer], pos, WIN_SHARD)
            return dsts[a].at[layer, pos]

        def ici(a, j, src, dst):
            return _rcopy(src, dst, send.at[a, j], recv.at[a, j], (*chips[j], c))

        def d2d(a, j, src, dst):
            return _rcopy(src, dst, send.at[a, 3 + j], recv.at[a, 3 + j], sib)

        def own(a, layer, src, dst):
            return _rcopy(src, dst, own_send.at[a, layer], own_recv.at[a, layer], sib)

        for a in range(n):
            for j in range(3):
                _start_pieces(srcs[a].at[c], slot(a, c, k), functools.partial(ici, a, j))
        for a in range(n):
            for layer in range(DEPTH):
                _start_pieces(srcs[a].at[layer], slot(a, layer, k), functools.partial(own, a, layer))
        for a in range(n):
            for j, (cx, cy) in enumerate(chips):
                landed = slot(a, c, 2 * cx + cy)
                ici(a, j, landed, landed).wait_recv()
                _start_pieces(landed, landed, functools.partial(d2d, a, j))
        for a in range(n):
            for j, (cx, cy) in enumerate(chips):
                passed = slot(a, 1 - c, 2 * cx + cy)
                d2d(a, j, passed, passed).wait_recv()
                landed = slot(a, c, 2 * cx + cy)
                d2d(a, j, landed, landed).wait_send()
                ici(a, j, srcs[a].at[c], slot(a, c, k)).wait_send()
            for layer in range(DEPTH):
                own(a, layer, srcs[a].at[layer], slot(a, layer, k)).wait()

    outs = [jax.ShapeDtypeStruct((DEPTH, D, NIN), shards[0].dtype)]
    outs += [jax.ShapeDtypeStruct((DEPTH, NCHIP) + s.shape[1:], s.dtype) for s in shards[1:]]
    return pl.pallas_call(
        body, name="gather_weights", in_specs=[ANY] * n, out_specs=[ANY] * n, out_shape=outs,
        scratch_shapes=[pltpu.SemaphoreType.DMA((n, 6)), pltpu.SemaphoreType.DMA((n, 6)),
                        pltpu.SemaphoreType.DMA((n, DEPTH)), pltpu.SemaphoreType.DMA((n, DEPTH))],
    )(*shards)


def _half(ref, h):
    rows = ref.shape[-2] // 2
    start = pl.multiple_of(h * rows, 16)
    if len(ref.shape) == 2:
        return ref.at[pl.ds(start, rows), :]
    return ref.at[:, pl.ds(start, rows), :]


def rs_pair(tag, grads):
    n = len(grads)

    def body(*refs):
        g, theirs = refs[:n], refs[n:2 * n]
        send, recv = refs[2 * n:]
        x, y, c, _ = _place()

        def give(a, s, d):
            return _rcopy(s, d, send.at[a], recv.at[a], (x, y, 1 - c))

        for a in range(n):
            _start_pieces(_half(g[a], 1 - c), theirs[a], functools.partial(give, a))
        for a in range(n):
            give(a, _half(g[a], 1 - c), theirs[a]).wait()

    def hshape(s):
        return s[:-2] + (s[-2] // 2, s[-1])

    outs = [jax.ShapeDtypeStruct(hshape(g.shape), g.dtype) for g in grads]
    return pl.pallas_call(
        body, name=f"rs_pair{tag}", in_specs=[ANY] * n, out_specs=[ANY] * n, out_shape=outs,
        scratch_shapes=[pltpu.SemaphoreType.DMA((n,))] * 2,
    )(*grads)


HBM = pl.BlockSpec(memory_space=pltpu.HBM)
SEMS = pl.BlockSpec(memory_space=pltpu.SEMAPHORE)
EFFECT = pltpu.SideEffectType.DATAFLOW_SIDE_EFFECTING


def _chip_piece(ref, k):
    return _cols(ref, k, WIN_SHARD) if len(ref.shape) == 2 else ref.at[k]


def _chip_copy(a, k, src, dst, send, recv, me, c):
    return _rcopy(src, dst, send.at[a * NCHIP + k], recv.at[a * NCHIP + me], (k // 2, k % 2, c))


def rs_chips_start(tag, sums):
    n = len(sums)

    def pshape(s):
        return (NCHIP, s[0], WIN_SHARD) if len(s) == 2 else s

    def body(*refs):
        s, land = refs[:n], refs[n:2 * n]
        send, recv = refs[2 * n], refs[2 * n + 1]
        token = refs[-1]
        x, y, c, me = _place()
        for k in range(NCHIP):
            @pl.when(me != k)
            def _():
                for a in range(n):
                    _start_pieces(_chip_piece(s[a], k), land[a].at[me],
                                  lambda src, dst, a=a: _chip_copy(a, k, src, dst, send, recv, me, c))
        token[...] = jnp.zeros_like(token)

    lands = [lax.empty(pshape(v.shape), v.dtype) for v in sums]
    ops = [pltpu.with_memory_space_constraint(v, pltpu.HBM) for v in list(sums) + lands]
    sem = pltpu.SemaphoreType.DMA((n * NCHIP,))
    res = pl.pallas_call(
        body, name=f"rs_chips_start{tag}", in_specs=[HBM] * (2 * n),
        out_specs=[SEMS, SEMS] + [HBM] * (2 * n) + [pl.BlockSpec(memory_space=pltpu.VMEM)],
        out_shape=[sem, sem] + [pltpu.HBM(v.shape, v.dtype) for v in ops] + [jax.ShapeDtypeStruct((8, LANES), f32)],
        input_output_aliases={i: 2 + i for i in range(2 * n)},
        compiler_params=pltpu.CompilerParams(has_side_effects=EFFECT),
    )(*ops)
    return res[0], res[1], res[2:2 + n], res[2 + n:2 + 2 * n], res[-1]


def rs_chips_wait(tag, send, recv, sums, lands, after):
    n = len(sums)

    def body(*refs):
        s, land = refs[:n], refs[n:2 * n]
        send_r, recv_r = refs[2 * n], refs[2 * n + 1]
        x, y, c, me = _place()
        for k in range(NCHIP):
            @pl.when(me != k)
            def _():
                for a in range(n):
                    piece = _chip_piece(s[a], k)
                    _chip_copy(a, k, piece, land[a].at[me], send_r, recv_r, me, c).wait_send()
                    _rcopy(piece, land[a].at[k], send_r.at[a * NCHIP + k], recv_r.at[a * NCHIP + k],
                           (k // 2, k % 2, c)).wait_recv()

    ops = list(sums) + list(lands)
    res = pl.pallas_call(
        body, name=f"rs_chips_wait{tag}", in_specs=[HBM] * (2 * n) + [SEMS, SEMS] + [ANY] * len(after),
        out_specs=[HBM] * (2 * n), out_shape=[pltpu.HBM(v.shape, v.dtype) for v in ops],
        input_output_aliases={i: i for i in range(2 * n)},
        compiler_params=pltpu.CompilerParams(has_side_effects=EFFECT),
    )(*ops, send, recv, *after)
    return res[:n], res[n:]


def rs_join(tag, halves):
    n = len(halves)

    def body(*refs):
        h, other = refs[:n], refs[n:2 * n]
        send, recv = refs[2 * n:]
        x, y, c, _ = _place()

        def give(a, s, d):
            return _rcopy(s, d, send.at[a], recv.at[a], (x, y, 1 - c))

        for a in range(n):
            _start_pieces(h[a], other[a], functools.partial(give, a))
        for a in range(n):
            give(a, h[a], other[a]).wait()

    outs = [jax.ShapeDtypeStruct(v.shape, v.dtype) for v in halves]
    return pl.pallas_call(
        body, name=f"rs_join{tag}", in_specs=[ANY] * n, out_specs=[ANY] * n, out_shape=outs,
        scratch_shapes=[pltpu.SemaphoreType.DMA((n,))] * 2,
    )(*halves)


def _row_tile(rows, cols, itemsize=4, target=2 << 20):
    best = 8
    for t in range(8, rows + 1, 8):
        if rows % t == 0 and t * cols * itemsize <= target:
            best = t
    return best


GRAD_WIRE = jnp.bfloat16


def add_n(name, terms, out_dtype=f32):
    shape = terms[0].shape
    cols = shape[-1]
    rows = math.prod(shape[:-1])
    tr = _row_tile(rows, cols)

    def body(*refs):
        acc = refs[0][...]
        for r in refs[1:-1]:
            acc = acc + r[...]
        refs[-1][...] = acc.astype(out_dtype)

    tile = pl.BlockSpec((tr, cols), lambda i: (i, 0))
    out = pl.pallas_call(
        body, name=name, grid=(rows // tr,), in_specs=[tile] * len(terms), out_specs=tile,
        out_shape=jax.ShapeDtypeStruct((rows, cols), out_dtype), compiler_params=_cp(("parallel",)),
    )(*[t.reshape(rows, cols) for t in terms])
    return out.reshape(shape)


def add_chips(name, land, own):
    _, rows, cols = land.shape
    tr = _row_tile(rows, cols, target=1 << 20)

    def body(land_r, own_r, o_r):
        me = 2 * lax.axis_index("x") + lax.axis_index("y")
        for k in range(NCHIP):
            @pl.when(me == k)
            def _():
                acc = None
                for j in range(NCHIP):
                    t = (own_r[...] if j == k else land_r[j]).astype(f32)
                    acc = t if acc is None else acc + t
                o_r[...] = acc

    tile = pl.BlockSpec((tr, cols), lambda i: (i, 0))
    return pl.pallas_call(
        body, name=name, grid=(rows // tr,), in_specs=[pl.BlockSpec((NCHIP, tr, cols), lambda i: (0, i, 0)), tile],
        out_specs=tile, out_shape=jax.ShapeDtypeStruct((rows, cols), f32), compiler_params=_cp(("parallel",)),
    )(land, own)


def reduce_scatter_begin(tag, G):
    c = lax.axis_index("c")
    names = tuple(G)
    grads = [G[n] if G[n].ndim == 3 or n == "w_in" else G[n].reshape(NCHIP, D // NCHIP, D) for n in names]
    theirs = rs_pair(tag, grads)
    sums = []
    for n, g, t in zip(names, grads, theirs):
        rows = g.shape[-2] // 2
        mine = lax.dynamic_slice_in_dim(g, c * rows, rows, axis=g.ndim - 2)
        sums.append(add_n(f"rs_add_pair{tag}_{n}", [mine, t], GRAD_WIRE))
    send, recv, sums, lands, token = rs_chips_start(tag, sums)
    return (tag, names, send, recv, sums, lands), token[0, 0]


def reduce_scatter_finish(state, after):
    me = 2 * lax.axis_index("x") + lax.axis_index("y")
    tag, names, send, recv, sums, lands = state
    sums, landed = rs_chips_wait(tag, send, recv, sums, lands, after)
    halves = []
    for n, s, v in zip(names, sums, landed):
        own = lax.dynamic_slice_in_dim(s, me * WIN_SHARD, WIN_SHARD, axis=1) if s.ndim == 2 else \
            lax.dynamic_index_in_dim(s, me, 0, keepdims=False)
        halves.append(add_chips(f"rs_add_chips{tag}_{n}", v, own))
    return dict(zip(names, zip(halves, rs_join(tag, halves))))


NDEV = 8


def allreduce_small(pack):
    rows = pack.shape[0]

    def body(p_ref, o_ref, buf, send, recv):
        x, y, c, _ = _place()
        me = 4 * x + 2 * y + c
        buf[me] = p_ref[...]

        def give(r, s, d):
            return _rcopy(s, d, send.at[r - 1], recv.at[r - 1], (x ^ (r >> 2), y ^ ((r >> 1) & 1), c ^ (r & 1)))

        for r in range(1, NDEV):
            _start_pieces(p_ref, buf.at[me], functools.partial(give, r), 128 << 10)
        for r in range(1, NDEV):
            give(r, p_ref, buf.at[me]).wait_send()
            src = 4 * (x ^ (r >> 2)) + 2 * (y ^ ((r >> 1) & 1)) + (c ^ (r & 1))
            give(r, p_ref, buf.at[src]).wait_recv()
        acc = buf[0]
        for d in range(1, NDEV):
            acc = acc + buf[d]
        o_ref[...] = acc

    vm = pl.BlockSpec(memory_space=pltpu.VMEM)
    return pl.pallas_call(
        body, name="allreduce_small", in_specs=[vm], out_specs=vm, out_shape=jax.ShapeDtypeStruct(pack.shape, f32),
        scratch_shapes=[pltpu.VMEM((NDEV, rows, LANES), f32), pltpu.SemaphoreType.DMA((NDEV - 1,)),
                        pltpu.SemaphoreType.DMA((NDEV - 1,))],
        compiler_params=pltpu.CompilerParams(vmem_limit_bytes=40 << 20),
    )(pack)


def _adamw_math(w, g, m, v):
    m = ADAM_B1 * m + (1.0 - ADAM_B1) * g
    v = ADAM_B2 * v + (1.0 - ADAM_B2) * (g * g)
    m_hat = m / (1.0 - ADAM_B1 ** ADAM_STEP)
    v_hat = v / (1.0 - ADAM_B2 ** ADAM_STEP)
    return -ADAM_LR * (m_hat / (jnp.sqrt(v_hat) + ADAM_EPS) + ADAM_WD * w), m, v


def adamw_big(name, halves, w, m, v):
    _, R, C = w.shape
    tr = _row_tile(R // 2, C, target=1 << 20)
    nt = R // 2 // tr

    def body(a0, b0, a1, b1, w_r, m_r, v_r, g_o, d_o, m_o, v_o):
        mine = pl.program_id(1) == lax.axis_index("c")
        g = jnp.where(pl.program_id(0) == 0, jnp.where(mine, a0[...], b0[...]), jnp.where(mine, a1[...], b1[...]))
        g_o[...] = g
        d_o[...], m_o[...], v_o[...] = _adamw_math(w_r[...], g, m_r[...], v_r[...])

    stk = pl.BlockSpec((None, tr, C), lambda l, h, i: (l, h * nt + i, 0))
    lay0 = pl.BlockSpec((tr, C), lambda l, h, i: (jnp.where(l == 0, i, nt - 1), 0))
    lay1 = pl.BlockSpec((tr, C), lambda l, h, i: (jnp.where(l == 0, 0, i), 0))
    return pl.pallas_call(
        body, name=name, grid=(DEPTH, 2, nt),
        in_specs=[lay0, lay0, lay1, lay1, stk, stk, stk],
        out_specs=[stk] * 4, out_shape=[jax.ShapeDtypeStruct(w.shape, f32)] * 4,
        compiler_params=_cp(("arbitrary", "arbitrary", "arbitrary")),
    )(*halves[0], *halves[1], w, m, v)


def adamw_small(name, g, w, m, v):
    def body(g_r, w_r, m_r, v_r, d_o, m_o, v_o):
        d_o[...], m_o[...], v_o[...] = _adamw_math(w_r[...], g_r[...], m_r[...], v_r[...])

    return pl.pallas_call(body, name=name, out_shape=[jax.ShapeDtypeStruct(w.shape, f32)] * 3)(g, w, m, v)


WEIGHTS = ("w_in", "conv_w", "gmlp_ln_g", "gmlp_ln_b", "w_s", "b_s", "p_a", "p_b", "p_c", "w_o", "ln1_g", "ln1_b",
           "w_gate", "w_up", "w_down", "ln2_g", "ln2_b")
VECS = ("ln1_g", "ln1_b", "ln2_g", "ln2_b", "gmlp_ln_g", "gmlp_ln_b")
ROWS_VEC, ROWS_BS, ROWS_WS, ROWS_CONV = D // LANES, 8, 8 * BLK, 3 * D // LANES
ROWS_LAYER = len(VECS) * ROWS_VEC + ROWS_BS + ROWS_WS + ROWS_CONV


def _pack_small(per_layer, tail):
    parts = []
    for P in per_layer:
        parts += [P[n].reshape(ROWS_VEC, LANES) for n in VECS]
        parts += [P["b_s"].reshape(ROWS_BS, LANES), P["w_s"].reshape(ROWS_WS, LANES), P["conv_w"].reshape(ROWS_CONV, LANES)]
    return jnp.concatenate(parts + [tail], axis=0)


def _unpack_small(pack):
    out = []
    for l in range(DEPTH):
        r = l * ROWS_LAYER
        P = {}
        for n in VECS:
            P[n] = pack[r:r + ROWS_VEC].reshape(D)
            r += ROWS_VEC
        P["b_s"] = pack[r:r + ROWS_BS].reshape(8, BLK)
        r += ROWS_BS
        P["w_s"] = pack[r:r + ROWS_WS].reshape(8, BLK, BLK)
        r += ROWS_WS
        P["conv_w"] = pack[r:r + ROWS_CONV].reshape(3, D)
        out.append(P)
    return out, pack[DEPTH * ROWS_LAYER:]


def kernel(x, positions, w_in, conv_w, gmlp_ln_g, gmlp_ln_b, w_s, b_s, p_a, p_b, p_c, w_o, ln1_g, ln1_b, w_gate, w_up, w_down, ln2_g, ln2_b, loss_target, m_w_in, m_conv_w, m_gmlp_ln_g, m_gmlp_ln_b, m_w_s, m_b_s, m_p_a, m_p_b, m_p_c, m_w_o, m_ln1_g, m_ln1_b, m_w_gate, m_w_up, m_w_down, m_ln2_g, m_ln2_b, v_w_in, v_conv_w, v_gmlp_ln_g, v_gmlp_ln_b, v_w_s, v_b_s, v_p_a, v_p_b, v_p_c, v_w_o, v_ln1_g, v_ln1_b, v_w_gate, v_w_up, v_w_down, v_ln2_g, v_ln2_b):
    Wt = dict(w_in=w_in, conv_w=conv_w, gmlp_ln_g=gmlp_ln_g, gmlp_ln_b=gmlp_ln_b, w_s=w_s, b_s=b_s, p_a=p_a, p_b=p_b,
              p_c=p_c, w_o=w_o, ln1_g=ln1_g, ln1_b=ln1_b, w_gate=w_gate, w_up=w_up, w_down=w_down, ln2_g=ln2_g, ln2_b=ln2_b)
    Mt = dict(w_in=m_w_in, conv_w=m_conv_w, gmlp_ln_g=m_gmlp_ln_g, gmlp_ln_b=m_gmlp_ln_b, w_s=m_w_s, b_s=m_b_s, p_a=m_p_a,
              p_b=m_p_b, p_c=m_p_c, w_o=m_w_o, ln1_g=m_ln1_g, ln1_b=m_ln1_b, w_gate=m_w_gate, w_up=m_w_up,
              w_down=m_w_down, ln2_g=m_ln2_g, ln2_b=m_ln2_b)
    Vt = dict(w_in=v_w_in, conv_w=v_conv_w, gmlp_ln_g=v_gmlp_ln_g, gmlp_ln_b=v_gmlp_ln_b, w_s=v_w_s, b_s=v_b_s, p_a=v_p_a,
              p_b=v_p_b, p_c=v_p_c, w_o=v_w_o, ln1_g=v_ln1_g, ln1_b=v_ln1_b, w_gate=v_w_gate, w_up=v_w_up,
              w_down=v_w_down, ln2_g=v_ln2_g, ln2_b=v_ln2_b)
    chip = 2 * lax.axis_index("x") + lax.axis_index("y")
    cw = D // NCHIP

    full = gather_weights([Wt[n].astype(MX) for n in BIG] + [conv_w])
    layers = []
    for l in range(DEPTH):
        Wl = dict(zip(BIG, (f[l] for f in full[:NBIG])))
        for n in ("p_a", "p_c", "w_o"):
            Wl[n] = Wl[n].reshape(D, D)
        Wl["conv_w"] = full[NBIG][l].transpose(1, 0, 2).reshape(3, D)
        for n in VECS + ("w_s", "b_s"):
            Wl[n] = Wt[n][l]
        layers.append(Wl)

    rs_state, rs_started = {}, {}

    def start_exchange(l, g):
        key = (l, "w_in" in g)
        rs_state[key], rs_started[key] = reduce_scatter_begin(f"{l}{'b' if key[1] else 'a'}", g)
        return rs_started[key]

    lsum, grad_x, grads = local_step(x[0], positions[0], loss_target[0], layers, start_exchange)

    last = rs_started[(0, True)]
    lsum = lsum + last
    behind = [grad_x, lsum]
    red = [dict() for _ in range(DEPTH)]
    for key in ((1, False), (1, True), (0, False)):
        red[key[0]].update(reduce_scatter_finish(rs_state[key], behind))
    pack = _pack_small([{n: g[n] for n in VECS + ("b_s", "w_s", "conv_w")} for g in grads], lsum)
    small, tail = _unpack_small(allreduce_small(pack))
    loss = tail[0, 0]

    G, DW, NM, NV = {}, {}, {}, {}
    zc = jnp.zeros((3, D), f32)
    wp = _pack_small([{**{n: Wt[n][l] for n in VECS + ("b_s", "w_s")}, "conv_w": zc} for l in range(DEPTH)], jnp.zeros((8, LANES), f32))
    mp = _pack_small([{**{n: Mt[n][l] for n in VECS + ("b_s", "w_s")}, "conv_w": zc} for l in range(DEPTH)], jnp.zeros((8, LANES), f32))
    vp = _pack_small([{**{n: Vt[n][l] for n in VECS + ("b_s", "w_s")}, "conv_w": zc} for l in range(DEPTH)], jnp.ones((8, LANES), f32))
    gp = _pack_small(small, jnp.zeros((8, LANES), f32))
    outs = [_unpack_small(a)[0] for a in adamw_small("adamw_small", gp, wp, mp, vp)]
    for n in VECS + ("b_s", "w_s"):
        G[n] = jnp.stack([small[l][n] for l in range(DEPTH)])
        DW[n], NM[n], NV[n] = (jnp.stack([o[l][n] for l in range(DEPTH)]) for o in outs)
    gconv = jnp.stack([lax.dynamic_slice(small[l]["conv_w"], (0, chip * cw), (3, cw)) for l in range(DEPTH)])
    G["conv_w"] = gconv
    flat = lambda a: a.reshape(DEPTH * 3, cw)
    d, m2, v2 = adamw_small("adamw_conv", flat(gconv), flat(conv_w), flat(m_conv_w), flat(v_conv_w))
    DW["conv_w"], NM["conv_w"], NV["conv_w"] = (a.reshape(DEPTH, 3, cw) for a in (d, m2, v2))

    red[0].update(reduce_scatter_finish(rs_state[(0, True)], [d, DW["ln2_b"], red[1]["w_in"][1], red[0]["w_o"][1]]))
    for n in BIG:
        G[n], DW[n], NM[n], NV[n] = adamw_big("adamw_" + n, (red[0][n], red[1][n]), Wt[n], Mt[n], Vt[n])

    return (loss, grad_x[None], *[G[n] for n in WEIGHTS], *[DW[n] for n in WEIGHTS], *[NM[n] for n in WEIGHTS],
            *[NV[n] for n in WEIGHTS])
```

```python
import functools
import math

import jax
import jax.numpy as jnp
from jax import lax
from jax.experimental import pallas as pl
from jax.experimental.pallas import tpu as pltpu

D = 1024
NIN = 12800
DFF = 2816
NCHIP = 4
FB = DFF // NCHIP
WIN_SHARD = NIN // NCHIP
DEPTH = 2
GROUPS = ((128, 1), (512, 4), (2048, 16))
HD = 64
BLK = 128
AO = 512
ALPHA = (2 * DEPTH) ** 0.25
EPS = 1e-5
ROPE_THETA = 10000.0
LANES = 128
NEG = -1e30

C_GATES, C_BCH, C_QKV, C_UV = 0, 3 * D, 6 * D, 6 * D + 9 * AO

MX = jnp.bfloat16
ACT = jnp.bfloat16

ADAM_LR, ADAM_B1, ADAM_B2, ADAM_EPS, ADAM_WD, ADAM_STEP = 0.001, 0.9, 0.999, 1e-08, 0.01, 10

f32 = jnp.float32
NT = (((1,), (1,)), ((), ()))
TN = (((0,), (0,)), ((), ()))


def _cp(sem, vmem_mb=48):
    return pltpu.CompilerParams(dimension_semantics=sem, vmem_limit_bytes=vmem_mb << 20)


def _dot(a, b, dims=None):
    if dims is None:
        return jnp.dot(a, b, preferred_element_type=f32)
    return lax.dot_general(a, b, dims, preferred_element_type=f32)


def _ln_stats(r):
    mu = jnp.mean(r, axis=-1, keepdims=True)
    xc = r - mu
    var = jnp.mean(xc * xc, axis=-1, keepdims=True)
    rstd = lax.rsqrt(var + EPS)
    return xc * rstd, rstd


def _ln_bwd(dy, xhat, rstd, g):
    dxh = dy * g
    return rstd * (dxh - jnp.mean(dxh, axis=-1, keepdims=True) - xhat * jnp.mean(dxh * xhat, axis=-1, keepdims=True))


def _gelu(x):
    return 0.5 * x * (1.0 + lax.erf(x * (1.0 / math.sqrt(2.0))))


def _gelu_grad(x):
    return 0.5 * (1.0 + lax.erf(x * (1.0 / math.sqrt(2.0)))) + x * jnp.exp(-0.5 * x * x) * (1.0 / math.sqrt(2.0 * math.pi))


def _sigmoid(x):
    return 0.5 * jnp.tanh(0.5 * x) + 0.5


def _acc_rows(o_ref, first, val):
    @pl.when(first)
    def _():
        o_ref[...] = jnp.zeros_like(o_ref)
    o_ref[...] += jnp.sum(val, axis=0, keepdims=True)


def mm_in(x, w):
    T = x.shape[0]
    tm, tn = min(1024, T), 1280

    def body(x_ref, w_ref, o_ref, xb):
        @pl.when(pl.program_id(1) == 0)
        def _():
            xb[...] = x_ref[...].astype(MX)
        o_ref[...] = _dot(xb[...], w_ref[...]).astype(o_ref.dtype)

    return pl.pallas_call(
        body, name="mm_in", grid=(T // tm, NIN // tn),
        in_specs=[pl.BlockSpec((tm, D), lambda i, j: (i, 0)), pl.BlockSpec((D, tn), lambda i, j: (0, j))],
        out_specs=pl.BlockSpec((tm, tn), lambda i, j: (i, j)),
        out_shape=jax.ShapeDtypeStruct((T, NIN), ACT),
        scratch_shapes=[pltpu.VMEM((tm, D), MX)],
        compiler_params=_cp(("parallel", "arbitrary")),
    )(x, w)


HALO = 16
TM_AC = 256


def _uv_specs():
    return [pl.BlockSpec((TM_AC, 512), functools.partial(lambda i, j: (i, j), j=C_UV // 512 + j)) for j in range(4)]


def _gmlp_fwd(up, vp, ws_ref, bs_ref, lg, lb):
    u = _gelu(up)
    xhat, rstd = _ln_stats(_gelu(vp))
    vn = xhat * lg + lb
    vnb = vn.astype(MX)
    rows = []
    for c in range(up.shape[0] // BLK):
        r = slice(c * BLK, (c + 1) * BLK)
        rows.append(jnp.concatenate(
            [_dot(ws_ref[g], vnb[r, g * BLK:(g + 1) * BLK]) + bs_ref[g] for g in range(8)], axis=1))
    return u, vn, xhat, rstd, jnp.concatenate(rows, axis=0)


def mix_ac_fwd(proj, conv_w, wst, bsx, lg, lb):
    T = proj.shape[0]
    tm = TM_AC

    def body(bch, halo, u0, u1, v0, v1, cw, ws, bs, lg_ref, lb_ref, ya, yc, zs):
        i = pl.program_id(0)
        pb = bch[...].astype(f32)
        z = pb[:, D:2 * D] * pb[:, 2 * D:]
        hz = halo[:, :D].astype(f32) * halo[:, D:].astype(f32)
        zs[0:HALO, :] = jnp.where(i > 0, hz, 0.0)
        zs[HALO:HALO + tm, :] = z
        cv = cw[0:1, :] * zs[HALO - 2:HALO - 2 + tm, :] + cw[1:2, :] * zs[HALO - 1:HALO - 1 + tm, :] + cw[2:3, :] * z
        ya[...] = (pb[:, :D] * cv).astype(ya.dtype)
        up = jnp.concatenate([u0[...], u1[...]], axis=1).astype(f32)
        vp = jnp.concatenate([v0[...], v1[...]], axis=1).astype(f32)
        u, _, _, _, sp = _gmlp_fwd(up, vp, ws, bs, lg_ref[...], lb_ref[...])
        yc[...] = (u * sp).astype(yc.dtype)

    full = lambda shape: pl.BlockSpec(shape, lambda i: (0,) * len(shape))
    return pl.pallas_call(
        body, name="mix_ac_fwd", grid=(T // tm,),
        in_specs=[pl.BlockSpec((tm, 3 * D), lambda i: (i, 1)),
                  pl.BlockSpec((HALO, 2 * D), lambda i: (jnp.maximum(i * (tm // HALO) - 1, 0), 2)),
                  *_uv_specs(), full((3, D)), full((8, BLK, BLK)), full((8, BLK, BLK)), full((1, D)), full((1, D))],
        out_specs=[pl.BlockSpec((tm, D), lambda i: (i, 0))] * 2,
        out_shape=[jax.ShapeDtypeStruct((T, D), MX)] * 2,
        scratch_shapes=[pltpu.VMEM((HALO + tm, D), f32)],
        compiler_params=_cp(("parallel",)),
    )(proj, proj, proj, proj, proj, proj, conv_w, wst, bsx, lg, lb)


def _swap_halves(x):
    lane = lax.broadcasted_iota(jnp.int32, x.shape, 1)
    return jnp.where((lane % HD) < HD // 2, pltpu.roll(x, x.shape[1] - HD // 2, 1), pltpu.roll(x, HD // 2, 1))


def _tile4(t):
    return jnp.concatenate([t] * (AO // LANES), axis=1)


TM_FOLD = 512


def _fold_out(nat, x, out_ref, d):
    if d == 1:
        out_ref[0] = x.astype(out_ref.dtype)
        return
    rows = x.shape[0] // d
    for j in range(AO // LANES):
        nat[j] = x[:, j * LANES:(j + 1) * LANES]
    for r in range(d):
        out_ref[r] = jnp.concatenate(
            [nat.at[j][pl.ds(r, rows, stride=d), :] for j in range(AO // LANES)], axis=1).astype(out_ref.dtype)


def _unfold_in(nat, in_ref, d):
    if d == 1:
        return in_ref[0].astype(f32)
    rows = in_ref.shape[1]
    for r in range(d):
        v = in_ref[r].astype(f32)
        for j in range(AO // LANES):
            nat.at[j][pl.ds(r, rows, stride=d), :] = v[:, j * LANES:(j + 1) * LANES]
    return jnp.concatenate([nat[j] for j in range(AO // LANES)], axis=1)


def fold_rope(proj, cos_t, sin_t, g, d):
    T = proj.shape[0]
    tm = TM_FOLD
    rows = tm // d

    def body(x_ref, c_ref, s_ref, q_o, k_o, v_o, nat):
        cos, sin = _tile4(c_ref[...]), _tile4(s_ref[...])
        for part, out, scale in ((0, q_o, HD ** -0.5), (1, k_o, 1.0), (2, v_o, None)):
            x = x_ref[:, part * AO:(part + 1) * AO].astype(f32)
            if scale is not None:
                x = (x * cos + _swap_halves(x) * sin) * scale
            _fold_out(nat, x, out, d)

    fold_spec = pl.BlockSpec((d, rows, AO), lambda i: (0, i, 0))
    return pl.pallas_call(
        body, name=f"fold_rope{g}", grid=(T // tm,),
        in_specs=[pl.BlockSpec((tm, 3 * AO), lambda i: (i, C_QKV // (3 * AO) + g)),
                  pl.BlockSpec((tm, LANES), lambda i: (i, 0)), pl.BlockSpec((tm, LANES), lambda i: (i, 0))],
        out_specs=[fold_spec] * 3,
        out_shape=[jax.ShapeDtypeStruct((d, T // d, AO), MX)] * 3,
        scratch_shapes=[pltpu.VMEM((AO // LANES, tm, LANES), f32)],
        compiler_params=_cp(("parallel",)),
    )(proj, cos_t, sin_t)


def _stack_heads(x):
    lane = lax.broadcasted_iota(jnp.int32, x.shape, 1)
    z = jnp.zeros_like(x)
    return jnp.concatenate([jnp.where(lane < HD, x, z), jnp.where(lane >= HD, x, z)], axis=0)


def _unstack_heads(y):
    lane = lax.broadcasted_iota(jnp.int32, (BLK, LANES), 1)
    return jnp.where(lane < HD, y[:BLK], y[BLK:])


def _window_masks():
    row = lax.broadcasted_iota(jnp.int32, (2 * BLK, 2 * BLK), 0) % BLK
    col = lax.broadcasted_iota(jnp.int32, (2 * BLK, 2 * BLK), 1)
    return (col < BLK) & (col >= row), (col >= BLK) & (col - BLK <= row)


def _two_blocks(ref, b):
    r0 = pl.multiple_of(b * BLK, BLK)
    rp = pl.multiple_of(jnp.maximum(b - 1, 0) * BLK, BLK)
    return jnp.concatenate([ref[pl.ds(rp, BLK), :], ref[pl.ds(r0, BLK), :]], axis=0)


def attn_fwd(qf, kf, vf, g, nb):
    T = qf.shape[0]

    def body(q_ref, k_ref, v_ref, o_ref, l_ref):
        prev_m, cur_m = _window_masks()

        def step(b, carry):
            r0 = pl.multiple_of(b * BLK, BLK)
            qs = _stack_heads(q_ref[pl.ds(r0, BLK), :])
            s = _dot(qs, _two_blocks(k_ref, b), NT)
            s = jnp.where(cur_m | (prev_m & ((b % nb) != 0)), s, NEG)
            m = jnp.max(s, axis=-1, keepdims=True)
            p = jnp.exp(s - m)
            l = jnp.sum(p, axis=-1, keepdims=True)
            o = _dot(p.astype(MX), _two_blocks(v_ref, b)) / l
            o_ref[pl.ds(r0, BLK), :] = _unstack_heads(o)
            l_ref[pl.ds(r0, BLK), :] = _unstack_heads(jnp.broadcast_to(m + jnp.log(l), (2 * BLK, LANES)))
            return carry

        lax.fori_loop(0, T // BLK, step, 0, unroll=4)

    spec = pl.BlockSpec((T, LANES), lambda j: (0, j))
    return pl.pallas_call(
        body, name=f"attn_fwd{g}", grid=(AO // LANES,),
        in_specs=[spec] * 3, out_specs=[spec] * 2,
        out_shape=[jax.ShapeDtypeStruct((T, AO), f32)] * 2,
        compiler_params=_cp(("parallel",), 56),
    )(qf, kf, vf)


def _group_weights(lses):
    m = jnp.maximum(jnp.maximum(lses[0], lses[1]), lses[2])
    e = [jnp.exp(l - m) for l in lses]
    inv = 1.0 / (e[0] + e[1] + e[2])
    return [x * inv for x in e]


def _fold_specs(T, tm):
    specs = []
    for _, d in GROUPS:
        specs.append(pl.BlockSpec((d, tm // d, AO), lambda i: (0, i, 0)))
    return specs


def combine_fwd(os_, lses):
    T = os_[0].shape[0] * os_[0].shape[1]
    tm = TM_FOLD

    def body(o0, o1, o2, l0, l1, l2, y_ref, nat):
        o = [_unfold_in(nat, r, d) for r, (_, d) in zip((o0, o1, o2), GROUPS)]
        ls = [_unfold_in(nat, r, d) for r, (_, d) in zip((l0, l1, l2), GROUPS)]
        w = _group_weights(ls)
        y_ref[...] = (w[0] * o[0] + w[1] * o[1] + w[2] * o[2]).astype(y_ref.dtype)

    specs = _fold_specs(T, tm)
    return pl.pallas_call(
        body, name="combine_fwd", grid=(T // tm,),
        in_specs=specs + specs, out_specs=pl.BlockSpec((tm, AO), lambda i: (i, 0)),
        out_shape=jax.ShapeDtypeStruct((T, AO), MX),
        scratch_shapes=[pltpu.VMEM((AO // LANES, tm, LANES), f32)],
        compiler_params=_cp(("parallel",)),
    )(*os_, *lses)


TM_MIX = 256


def mix_out_fwd(proj, ya, yb, yc, x0, pa, pb, pc, wo, g1, b1):
    T = x0.shape[0]
    tm = min(TM_MIX, T)

    def body(gt, ya_r, yb_r, yc_r, x0_r, pa_r, pb_r, pc_r, wo_r, g_r, b_r, mabc, m_o, r1_o, x1_o):
        ma = _dot(ya_r[...], pa_r[...])
        ybv = yb_r[...]
        mb = jnp.concatenate([_dot(ybv, pb_r[k]) for k in range(NCHIP)], axis=1)
        mc = _dot(yc_r[...], pc_r[...])
        m = jnp.zeros((tm, D), f32)
        for j, mm in enumerate((ma, mb, mc)):
            mabc[:, j * D:(j + 1) * D] = mm.astype(mabc.dtype)
            m = m + _sigmoid(gt[:, j * D:(j + 1) * D].astype(f32)) * mm
        mb16 = m.astype(MX)
        m_o[...] = mb16
        r1 = ALPHA * x0_r[...] + _dot(mb16, wo_r[...])
        r1_o[...] = r1
        xhat, _ = _ln_stats(r1)
        x1_o[...] = xhat * g_r[...] + b_r[...]

    full = lambda shape: pl.BlockSpec(shape, lambda i: (0,) * len(shape))
    tile = lambda w: pl.BlockSpec((tm, w), lambda i: (i, 0))
    return pl.pallas_call(
        body, name="mix_out_fwd", grid=(T // tm,),
        in_specs=[tile(3 * D), tile(D), tile(AO), tile(D), tile(D), full((D, D)), full((NCHIP, AO, D // NCHIP)),
                  full((D, D)), full((D, D)), full((1, D)), full((1, D))],
        out_specs=[tile(3 * D), tile(D), tile(D), tile(D)],
        out_shape=[jax.ShapeDtypeStruct((T, 3 * D), MX), jax.ShapeDtypeStruct((T, D), MX),
                   jax.ShapeDtypeStruct((T, D), f32), jax.ShapeDtypeStruct((T, D), f32)],
        compiler_params=_cp(("parallel",), 56),
    )(proj, ya, yb, yc, x0, pa, pb, pc, wo, g1, b1)


TM_FF = 512


def ffn_up_fwd(x1, wg, wu):
    T = x1.shape[0]
    tm = min(TM_FF, T)

    def body(x_r, wg_r, wu_r, g_o, u_o, h_o, xb):
        @pl.when(pl.program_id(1) == 0)
        def _():
            xb[...] = x_r[...].astype(MX)
        gate = _dot(xb[...], wg_r[0])
        up = _dot(xb[...], wu_r[0])
        g_o[0] = gate.astype(g_o.dtype)
        u_o[0] = up.astype(u_o.dtype)
        h_o[0] = (gate * _sigmoid(gate) * up).astype(h_o.dtype)

    wspec = pl.BlockSpec((1, D, FB), lambda i, k: (k, 0, 0))
    ospec = pl.BlockSpec((1, tm, FB), lambda i, k: (k, i, 0))
    return pl.pallas_call(
        body, name="ffn_up_fwd", grid=(T // tm, NCHIP),
        in_specs=[pl.BlockSpec((tm, D), lambda i, k: (i, 0)), wspec, wspec],
        out_specs=[ospec] * 3,
        out_shape=[jax.ShapeDtypeStruct((NCHIP, T, FB), ACT)] * 2 + [jax.ShapeDtypeStruct((NCHIP, T, FB), MX)],
        scratch_shapes=[pltpu.VMEM((tm, D), MX)],
        compiler_params=_cp(("parallel", "arbitrary")),
    )(x1, wg, wu)


def ffn_down_fwd(hh, wd, x1, g2, b2):
    T = x1.shape[0]
    tm = min(TM_FF, T)

    def body(h_r, w_r, x_r, g_r, b_r, r2_o, x2_o):
        r2 = ALPHA * x_r[...]
        for k in range(NCHIP):
            r2 = r2 + _dot(h_r[k], w_r[k])
        r2_o[...] = r2
        xhat, _ = _ln_stats(r2)
        x2_o[...] = xhat * g_r[...] + b_r[...]

    tile = pl.BlockSpec((tm, D), lambda i: (i, 0))
    vec = pl.BlockSpec((1, D), lambda i: (0, 0))
    return pl.pallas_call(
        body, name="ffn_down_fwd", grid=(T // tm,),
        in_specs=[pl.BlockSpec((NCHIP, tm, FB), lambda i: (0, i, 0)), pl.BlockSpec((NCHIP, FB, D), lambda i: (0, 0, 0)),
                  tile, vec, vec],
        out_specs=[tile, tile], out_shape=[jax.ShapeDtypeStruct((T, D), f32)] * 2,
        compiler_params=_cp(("parallel",)),
    )(hh, wd, x1, g2, b2)


def loss_grad(y, tgt):
    T = y.shape[0]
    tm = min(512, T)

    def body(y_r, t_r, l_o, dy_o):
        e = y_r[...] - t_r[...]
        dy_o[...] = e * (1.0 / D)

        @pl.when(pl.program_id(0) == 0)
        def _():
            l_o[...] = jnp.zeros_like(l_o)
        l_o[...] += (0.5 / D) * jnp.sum(e * e)

    tile = pl.BlockSpec((tm, D), lambda i: (i, 0))
    return pl.pallas_call(
        body, name="loss_grad", grid=(T // tm,),
        in_specs=[tile, tile], out_specs=[pl.BlockSpec((8, LANES), lambda i: (0, 0)), tile],
        out_shape=[jax.ShapeDtypeStruct((8, LANES), f32), jax.ShapeDtypeStruct((T, D), f32)],
        compiler_params=_cp(("arbitrary",)),
    )(y, tgt)


def ffn_down_bwd(dx2, r2, g2, wd, gate, up):
    T = dx2.shape[0]
    tm = min(TM_FF, T)

    def body(dx_r, r_r, g_r, w_r, ga_r, up_r, dr_o, dg_o, du_o, dlg_o, dlb_o, drb):
        i, k = pl.program_id(0), pl.program_id(1)

        @pl.when(k == 0)
        def _():
            xhat, rstd = _ln_stats(r_r[...])
            dx = dx_r[...]
            _acc_rows(dlg_o, i == 0, dx * xhat)
            _acc_rows(dlb_o, i == 0, dx)
            dr = _ln_bwd(dx, xhat, rstd, g_r[...])
            dr_o[...] = dr
            drb[...] = dr.astype(MX)

        dhh = _dot(drb[...], w_r[0], NT)
        gate_v, up_v = ga_r[0].astype(f32), up_r[0].astype(f32)
        sg = _sigmoid(gate_v)
        dg_o[0] = (dhh * up_v * sg * (1.0 + gate_v * (1.0 - sg))).astype(dg_o.dtype)
        du_o[0] = (dhh * gate_v * sg).astype(du_o.dtype)

    tile = pl.BlockSpec((tm, D), lambda i, k: (i, 0))
    vec = pl.BlockSpec((1, D), lambda i, k: (0, 0))
    blk = pl.BlockSpec((1, tm, FB), lambda i, k: (k, i, 0))
    return pl.pallas_call(
        body, name="ffn_down_bwd", grid=(T // tm, NCHIP),
        in_specs=[tile, tile, vec, pl.BlockSpec((1, FB, D), lambda i, k: (k, 0, 0)), blk, blk],
        out_specs=[tile, blk, blk, vec, vec],
        out_shape=[jax.ShapeDtypeStruct((T, D), f32)] + [jax.ShapeDtypeStruct((NCHIP, T, FB), MX)] * 2
        + [jax.ShapeDtypeStruct((1, D), f32)] * 2,
        scratch_shapes=[pltpu.VMEM((tm, D), MX)],
        compiler_params=_cp(("arbitrary", "arbitrary")),
    )(dx2, r2, g2, wd, gate, up)


def ffn_up_bwd(dr2, dgate, dup, wg, wu, r1, g1):
    T = dr2.shape[0]
    tm = min(TM_FF, T)

    def body(dr2_r, dg_r, du_r, wg_r, wu_r, r1_r, g_r, dr1_o, dlg_o, dlb_o, acc):
        i, k = pl.program_id(0), pl.program_id(1)

        @pl.when(k == 0)
        def _():
            acc[...] = ALPHA * dr2_r[...]
        acc[...] += _dot(dg_r[0], wg_r[0], NT) + _dot(du_r[0], wu_r[0], NT)

        @pl.when(k == NCHIP - 1)
        def _():
            dx = acc[...]
            xhat, rstd = _ln_stats(r1_r[...])
            _acc_rows(dlg_o, i == 0, dx * xhat)
            _acc_rows(dlb_o, i == 0, dx)
            dr1_o[...] = _ln_bwd(dx, xhat, rstd, g_r[...])

    tile = pl.BlockSpec((tm, D), lambda i, k: (i, 0))
    vec = pl.BlockSpec((1, D), lambda i, k: (0, 0))
    blk = pl.BlockSpec((1, tm, FB), lambda i, k: (k, i, 0))
    wspec = pl.BlockSpec((1, D, FB), lambda i, k: (k, 0, 0))
    return pl.pallas_call(
        body, name="ffn_up_bwd", grid=(T // tm, NCHIP),
        in_specs=[tile, blk, blk, wspec, wspec, tile, vec],
        out_specs=[tile, vec, vec],
        out_shape=[jax.ShapeDtypeStruct((T, D), f32)] + [jax.ShapeDtypeStruct((1, D), f32)] * 2,
        scratch_shapes=[pltpu.VMEM((tm, D), f32)],
        compiler_params=_cp(("arbitrary", "arbitrary")),
    )(dr2, dgate, dup, wg, wu, r1, g1)


def mix_out_bwd(dr1, proj, mabc, wo, pa, pb, pc):
    T = dr1.shape[0]
    tm = min(TM_MIX, T)

    def body(dr_r, gt, mabc_r, wo_r, pa_r, pb_r, pc_r, dmabc_o, dgt_o, dya_o, dyb_o, dyc_o):
        dm = _dot(dr_r[...].astype(MX), wo_r[...], NT)
        dmx = []
        for j in range(3):
            s = _sigmoid(gt[:, j * D:(j + 1) * D].astype(f32))
            v = (dm * s).astype(MX)
            dmx.append(v)
            dmabc_o[:, j * D:(j + 1) * D] = v
            dgt_o[:, j * D:(j + 1) * D] = (dm * mabc_r[:, j * D:(j + 1) * D].astype(f32) * s * (1.0 - s)).astype(dgt_o.dtype)
        dya_o[...] = _dot(dmx[0], pa_r[...], NT)
        dyb = jnp.zeros((tm, AO), f32)
        for k in range(NCHIP):
            dyb = dyb + _dot(dmx[1][:, k * (D // NCHIP):(k + 1) * (D // NCHIP)], pb_r[k], NT)
        dyb_o[...] = dyb
        dyc_o[...] = _dot(dmx[2], pc_r[...], NT)

    full = lambda shape: pl.BlockSpec(shape, lambda i: (0,) * len(shape))
    tile = lambda w: pl.BlockSpec((tm, w), lambda i: (i, 0))
    return pl.pallas_call(
        body, name="mix_out_bwd", grid=(T // tm,),
        in_specs=[tile(D), tile(3 * D), tile(3 * D), full((D, D)), full((D, D)), full((NCHIP, AO, D // NCHIP)), full((D, D))],
        out_specs=[tile(3 * D), tile(3 * D), tile(D), tile(AO), tile(D)],
        out_shape=[jax.ShapeDtypeStruct((T, 3 * D), MX), jax.ShapeDtypeStruct((T, 3 * D), MX),
                   jax.ShapeDtypeStruct((T, D), f32), jax.ShapeDtypeStruct((T, AO), f32), jax.ShapeDtypeStruct((T, D), f32)],
        compiler_params=_cp(("parallel",), 56),
    )(dr1, proj, mabc, wo, pa, pb, pc)


def transpose_cast(x):
    T = x.shape[0]
    tm = min(512, T)

    def body(x_r, o_r):
        o_r[...] = x_r[...].T.astype(o_r.dtype)

    return pl.pallas_call(
        body, name="transpose_cast", grid=(T // tm,),
        in_specs=[pl.BlockSpec((tm, D), lambda i: (i, 0))], out_specs=pl.BlockSpec((D, tm), lambda i: (0, i)),
        out_shape=jax.ShapeDtypeStruct((D, T), MX), compiler_params=_cp(("parallel",)),
    )(x)


def tn_matmul(name, a, b, a_spec, b_spec, out_shape, out_spec, grid, a_is_t=False):
    nt = len(grid) - 1

    def body(a_r, b_r, o_r):
        @pl.when(pl.program_id(nt) == 0)
        def _():
            o_r[...] = jnp.zeros_like(o_r)
        av = a_r[...].reshape(a_r.shape[-2:]).astype(MX)
        bv = b_r[...].reshape(b_r.shape[-2:]).astype(MX)
        o_r[...] += _dot(av, bv, None if a_is_t else TN).reshape(o_r.shape)

    return pl.pallas_call(
        body, name=name, grid=grid, in_specs=[a_spec, b_spec], out_specs=out_spec,
        out_shape=jax.ShapeDtypeStruct(out_shape, f32),
        compiler_params=_cp(("parallel",) * nt + ("arbitrary",), 56),
    )(a, b)


def attn_pre_bwd(dyb, os_, lses, ones):
    T = dyb.shape[0]
    tm = TM_FOLD

    def body(dy_r, o0, o1, o2, l0, l1, l2, ones_r, d0, d1, d2, f0, f1, f2, nat):
        o = [_unfold_in(nat, r, d) for r, (_, d) in zip((o0, o1, o2), GROUPS)]
        ls = [_unfold_in(nat, r, d) for r, (_, d) in zip((l0, l1, l2), GROUPS)]
        w = _group_weights(ls)
        dy = dy_r[...]
        t = dy * (w[0] * o[0] + w[1] * o[1] + w[2] * o[2])
        hi = t.astype(MX)
        lo = (t - hi.astype(f32)).astype(MX)
        c = _dot(hi, ones_r[...]) + _dot(lo, ones_r[...])
        for wg, do_o, df_o, (_, d) in zip(w, (d0, d1, d2), (f0, f1, f2), GROUPS):
            _fold_out(nat, wg * dy, do_o, d)
            _fold_out(nat, -wg * c, df_o, d)

    specs = _fold_specs(T, tm)
    return pl.pallas_call(
        body, name="attn_pre_bwd", grid=(T // tm,),
        in_specs=[pl.BlockSpec((tm, AO), lambda i: (i, 0))] + specs + specs + [pl.BlockSpec((AO, AO), lambda i: (0, 0))],
        out_specs=specs + specs,
        out_shape=[jax.ShapeDtypeStruct((d, T // d, AO), MX) for _, d in GROUPS]
        + [jax.ShapeDtypeStruct((d, T // d, AO), f32) for _, d in GROUPS],
        scratch_shapes=[pltpu.VMEM((AO // LANES, tm, LANES), f32)],
        compiler_params=_cp(("parallel",)),
    )(dyb, *os_, *lses, ones)


def _head_ones():
    i = jnp.arange(AO) // HD
    return (i[:, None] == i[None, :]).astype(MX)


def attn_bwd(qf, kf, vf, dof, lse, df, g, nb):
    T = qf.shape[0]

    def body(q_ref, k_ref, v_ref, do_ref, l_ref, d_ref, dq_ref, dk_ref, dv_ref):
        prev_m, cur_m = _window_masks()

        def head_col(ref, r0):
            v = ref[pl.ds(r0, BLK), :]
            return jnp.concatenate([v[:, 0:1], v[:, HD:HD + 1]], axis=0)

        def step(b, carry):
            dk_c, dv_c = carry
            r0 = pl.multiple_of(b * BLK, BLK)
            rp = pl.multiple_of(jnp.maximum(b - 1, 0) * BLK, BLK)
            qs, dos = _stack_heads(q_ref[pl.ds(r0, BLK), :]), _stack_heads(do_ref[pl.ds(r0, BLK), :])
            k2, v2 = _two_blocks(k_ref, b), _two_blocks(v_ref, b)
            valid = cur_m | (prev_m & ((b % nb) != 0))
            p = jnp.where(valid, jnp.exp(_dot(qs, k2, NT) - head_col(l_ref, r0)), 0.0)
            ds = (p * (_dot(dos, v2, NT) + head_col(d_ref, r0))).astype(MX)
            dq_ref[pl.ds(r0, BLK), :] = _unstack_heads(_dot(ds, k2)).astype(dq_ref.dtype)
            dk2 = _dot(ds, qs, TN)
            dv2 = _dot(p.astype(MX), dos, TN)
            dk_ref[pl.ds(rp, BLK), :] = (dk_c + dk2[:BLK]).astype(dk_ref.dtype)
            dv_ref[pl.ds(rp, BLK), :] = (dv_c + dv2[:BLK]).astype(dv_ref.dtype)
            return dk2[BLK:], dv2[BLK:]

        zero = jnp.zeros((BLK, LANES), f32)

        def two_steps(i, carry):
            return step(2 * i + 1, step(2 * i, carry))

        dk_c, dv_c = lax.fori_loop(0, T // BLK // 2, two_steps, (zero, zero))
        dk_ref[pl.ds(T - BLK, BLK), :] = dk_c.astype(dk_ref.dtype)
        dv_ref[pl.ds(T - BLK, BLK), :] = dv_c.astype(dv_ref.dtype)

    spec = pl.BlockSpec((T, LANES), lambda j: (0, j))
    return pl.pallas_call(
        body, name=f"attn_bwd{g}", grid=(AO // LANES,),
        in_specs=[spec] * 6, out_specs=[spec] * 3,
        out_shape=[jax.ShapeDtypeStruct((T, AO), MX)] * 3,
        compiler_params=_cp(("parallel",), 60),
    )(qf, kf, vf, dof, lse, df)


def unfold_rope_bwd(dqf, dkf, dvf, cos_t, sin_t, g, d):
    T = dqf.shape[0] * dqf.shape[1]
    tm = TM_FOLD

    def body(q_r, k_r, v_r, c_ref, s_ref, o_ref, nat):
        cos, sin = _tile4(c_ref[...]), _tile4(s_ref[...])
        for part, ref, scale in ((0, q_r, HD ** -0.5), (1, k_r, 1.0), (2, v_r, None)):
            x = _unfold_in(nat, ref, d)
            if scale is not None:
                x = (x * cos - _swap_halves(x) * sin) * scale
            o_ref[:, part * AO:(part + 1) * AO] = x.astype(o_ref.dtype)

    fold_spec = pl.BlockSpec((d, tm // d, AO), lambda i: (0, i, 0))
    tab = pl.BlockSpec((tm, LANES), lambda i: (i, 0))
    return pl.pallas_call(
        body, name=f"unfold_rope_bwd{g}", grid=(T // tm,),
        in_specs=[fold_spec] * 3 + [tab, tab],
        out_specs=pl.BlockSpec((tm, 3 * AO), lambda i: (i, 0)),
        out_shape=jax.ShapeDtypeStruct((T, 3 * AO), MX),
        scratch_shapes=[pltpu.VMEM((AO // LANES, tm, LANES), f32)],
        compiler_params=_cp(("parallel",)),
    )(dqf, dkf, dvf, cos_t, sin_t)


def conv_bwd(dya, proj, conv_w):
    T = dya.shape[0]
    tm = TM_AC
    last = T // tm - 1

    def body(dy_r, bch, hprev, dy_next, b_next, cw, d_o, dw_o, zs, ds):
        i = pl.program_id(0)
        pb = bch[...].astype(f32)
        bp, cp, hp = pb[:, :D], pb[:, D:2 * D], pb[:, 2 * D:]
        z = cp * hp
        hz = hprev[:, :D].astype(f32) * hprev[:, D:].astype(f32)
        zs[0:HALO, :] = jnp.where(i > 0, hz, 0.0)
        zs[HALO:HALO + tm, :] = z
        z2, z1 = zs[HALO - 2:HALO - 2 + tm, :], zs[HALO - 1:HALO - 1 + tm, :]
        cv = cw[0:1, :] * z2 + cw[1:2, :] * z1 + cw[2:3, :] * z
        dy = dy_r[...]
        dcv = dy * bp
        ds[0:tm, :] = dcv
        ds[tm:tm + HALO, :] = jnp.where(i < last, dy_next[...] * b_next[...].astype(f32), 0.0)
        dz = cw[2:3, :] * dcv + cw[1:2, :] * ds[1:1 + tm, :] + cw[0:1, :] * ds[2:2 + tm, :]
        d_o[:, :D] = (dy * cv).astype(d_o.dtype)
        d_o[:, D:2 * D] = (dz * hp).astype(d_o.dtype)
        d_o[:, 2 * D:] = (dz * cp).astype(d_o.dtype)

        @pl.when(i == 0)
        def _():
            dw_o[...] = jnp.zeros_like(dw_o)
        dw_o[0:1, :] += jnp.sum(dcv * z2, axis=0, keepdims=True)
        dw_o[1:2, :] += jnp.sum(dcv * z1, axis=0, keepdims=True)
        dw_o[2:3, :] += jnp.sum(dcv * z, axis=0, keepdims=True)

    nh = tm // HALO
    return pl.pallas_call(
        body, name="conv_bwd", grid=(T // tm,),
        in_specs=[pl.BlockSpec((tm, D), lambda i: (i, 0)), pl.BlockSpec((tm, 3 * D), lambda i: (i, 1)),
                  pl.BlockSpec((HALO, 2 * D), lambda i: (jnp.maximum(i * nh - 1, 0), 2)),
                  pl.BlockSpec((HALO, D), lambda i: (jnp.minimum((i + 1) * nh, T // HALO - 1), 0)),
                  pl.BlockSpec((HALO, D), lambda i: (jnp.minimum((i + 1) * nh, T // HALO - 1), 3)),
                  pl.BlockSpec((3, D), lambda i: (0, 0))],
        out_specs=[pl.BlockSpec((tm, 3 * D), lambda i: (i, 0)), pl.BlockSpec((3, D), lambda i: (0, 0))],
        out_shape=[jax.ShapeDtypeStruct((T, 3 * D), MX), jax.ShapeDtypeStruct((3, D), f32)],
        scratch_shapes=[pltpu.VMEM((HALO + tm, D), f32), pltpu.VMEM((tm + HALO, D), f32)],
        compiler_params=_cp(("arbitrary",)),
    )(dya, proj, proj, dya, proj, conv_w)


def gmlp_bwd(dyc, proj, wst, bsx, lg, lb):
    T = dyc.shape[0]
    tm = TM_AC
    last = T // tm - 1

    def body(dy_r, u0, u1, v0, v1, ws, bs, lg_r, lb_r, d_o, dws_o, dbs_o, dlg_o, dlb_o, bacc):
        i = pl.program_id(0)
        up = jnp.concatenate([u0[...], u1[...]], axis=1).astype(f32)
        vp = jnp.concatenate([v0[...], v1[...]], axis=1).astype(f32)
        u, vn, xhat, rstd, sp = _gmlp_fwd(up, vp, ws, bs, lg_r[...], lb_r[...])
        dy = dy_r[...]
        d_o[:, :D] = (dy * sp * _gelu_grad(up)).astype(d_o.dtype)
        dsp = dy * u
        dspb, vnb = dsp.astype(MX), vn.astype(MX)

        @pl.when(i == 0)
        def _():
            dws_o[...] = jnp.zeros_like(dws_o)
            bacc[...] = jnp.zeros_like(bacc)

        rows = []
        for c in range(tm // BLK):
            r = slice(c * BLK, (c + 1) * BLK)
            cols = []
            for g in range(8):
                cs = slice(g * BLK, (g + 1) * BLK)
                dws_o[g] += _dot(dspb[r, cs], vnb[r, cs], NT)
                bacc[g] += dsp[r, cs]
                cols.append(_dot(ws[g], dspb[r, cs], TN))
            rows.append(jnp.concatenate(cols, axis=1))
        dvn = jnp.concatenate(rows, axis=0)
        _acc_rows(dlg_o, i == 0, dvn * xhat)
        _acc_rows(dlb_o, i == 0, dvn)
        d_o[:, D:] = (_ln_bwd(dvn, xhat, rstd, lg_r[...]) * _gelu_grad(vp)).astype(d_o.dtype)

        @pl.when(i == last)
        def _():
            row = lax.broadcasted_iota(jnp.int32, (BLK, BLK), 0)
            col = lax.broadcasted_iota(jnp.int32, (BLK, BLK), 1)
            ones = jnp.ones((8, BLK), MX)
            for g in range(8):
                dws_o[g] = jnp.where(col <= row, dws_o[g], 0.0)
                a = bacc[g]
                hi = a.astype(MX)
                lo = (a - hi.astype(f32)).astype(MX)
                dbs_o[g:g + 1, :] = (_dot(ones, hi, NT) + _dot(ones, lo, NT))[0:1, :]

    full = lambda shape: pl.BlockSpec(shape, lambda i: (0,) * len(shape))
    return pl.pallas_call(
        body, name="gmlp_bwd", grid=(T // tm,),
        in_specs=[pl.BlockSpec((tm, D), lambda i: (i, 0)), *_uv_specs(), full((8, BLK, BLK)), full((8, BLK, BLK)),
                  full((1, D)), full((1, D))],
        out_specs=[pl.BlockSpec((tm, 2 * D), lambda i: (i, 0)), full((8, BLK, BLK)), full((8, BLK)), full((1, D)), full((1, D))],
        out_shape=[jax.ShapeDtypeStruct((T, 2 * D), MX), jax.ShapeDtypeStruct((8, BLK, BLK), f32),
                   jax.ShapeDtypeStruct((8, BLK), f32), jax.ShapeDtypeStruct((1, D), f32), jax.ShapeDtypeStruct((1, D), f32)],
        scratch_shapes=[pltpu.VMEM((8, BLK, BLK), f32)],
        compiler_params=_cp(("arbitrary",)),
    )(dyc, proj, proj, proj, proj, wst, bsx, lg, lb)


PART_TILES = (6, 6, 3, 3, 3, 4)
PART_START = (0, 6, 12, 15, 18, 21)
TJ = 512


def _part_specs(tm, rows_axis):
    specs = []
    for n, s in zip(PART_TILES, PART_START):
        def imap(*idx, n=n, s=s):
            i, j = idx[rows_axis], idx[1 - rows_axis]
            inside = (j >= s) & (j < s + n)
            return (jnp.where(inside, i, 0), jnp.clip(j - s, 0, n - 1))
        specs.append(pl.BlockSpec((tm, TJ), imap))
    return specs


def _with_part(j, refs, fn):
    for r, n, s in zip(refs, PART_TILES, PART_START):
        @pl.when((j >= s) & (j < s + n))
        def _():
            fn(r[...])


def dx_in(dr1, parts, w):
    T = dr1.shape[0]
    tm = min(1024, T)

    def body(dr_r, p0, p1, p2, p3, p4, p5, w_r, o_r):
        j = pl.program_id(1)

        @pl.when(j == 0)
        def _():
            o_r[...] = ALPHA * dr_r[...]

        def acc(tile):
            o_r[...] += _dot(tile, w_r[...], NT)
        _with_part(j, (p0, p1, p2, p3, p4, p5), acc)

    return pl.pallas_call(
        body, name="dx_in", grid=(T // tm, NIN // TJ),
        in_specs=[pl.BlockSpec((tm, D), lambda i, j: (i, 0))] + _part_specs(tm, 0) + [pl.BlockSpec((D, TJ), lambda i, j: (0, j))],
        out_specs=pl.BlockSpec((tm, D), lambda i, j: (i, 0)),
        out_shape=jax.ShapeDtypeStruct((T, D), f32),
        compiler_params=_cp(("parallel", "arbitrary"), 56),
    )(dr1, *parts, w)


def dw_in(x0t, parts):
    T = x0t.shape[1]
    tk = min(2048, T)

    def body(x_r, p0, p1, p2, p3, p4, p5, o_r):
        j, t = pl.program_id(0), pl.program_id(1)

        @pl.when(t == 0)
        def _():
            o_r[...] = jnp.zeros_like(o_r)

        def acc(tile):
            o_r[...] += _dot(x_r[...], tile)
        _with_part(j, (p0, p1, p2, p3, p4, p5), acc)

    return pl.pallas_call(
        body, name="dw_in", grid=(NIN // TJ, T // tk),
        in_specs=[pl.BlockSpec((D, tk), lambda j, t: (0, t))] + _part_specs(tk, 1),
        out_specs=pl.BlockSpec((D, TJ), lambda j, t: (0, j)),
        out_shape=jax.ShapeDtypeStruct((D, NIN), f32),
        compiler_params=_cp(("parallel", "arbitrary")),
    )(x0t, *parts)


def rope_tables(positions):
    half = HD // 2
    inv_freq = ROPE_THETA ** (-jnp.arange(half, dtype=f32) / half)
    ang = positions.astype(f32)[:, None] * inv_freq
    cos, sin = jnp.cos(ang), jnp.sin(ang)
    return jnp.tile(cos, (1, LANES // half)), jnp.tile(jnp.concatenate([-sin, sin], axis=1), (1, LANES // HD))


def _flat(a):
    return a.reshape(a.shape[0] * a.shape[1], a.shape[2])


def layer_fwd(x0, W, cos_t, sin_t):
    T = x0.shape[0]
    proj = mm_in(x0, W["w_in"])
    ya, yc = mix_ac_fwd(proj, W["conv_w"], W["wst"], W["bsx"], W["gmlp_ln_g"], W["gmlp_ln_b"])
    folded, os_, lses = [], [], []
    for g, (_, d) in enumerate(GROUPS):
        qf, kf, vf = fold_rope(proj, cos_t, sin_t, g, d)
        o, lse = attn_fwd(_flat(qf), _flat(kf), _flat(vf), g, T // d // BLK)
        folded.append((qf, kf, vf))
        os_.append(o.reshape(d, T // d, AO))
        lses.append(lse.reshape(d, T // d, AO))
    yb = combine_fwd(os_, lses)
    mabc, m, r1, x1 = mix_out_fwd(proj, ya, yb, yc, x0, W["p_a"], W["p_b"], W["p_c"], W["w_o"], W["ln1_g"], W["ln1_b"])
    gate, up, hh = ffn_up_fwd(x1, W["w_gate"], W["w_up"])
    r2, x2 = ffn_down_fwd(hh, W["w_down"], x1, W["ln2_g"], W["ln2_b"])
    saved = dict(x0=x0, proj=proj, ya=ya, yb=yb, yc=yc, folded=folded, os=os_, lses=lses, mabc=mabc, m=m, r1=r1,
                 x1=x1, gate=gate, up=up, hh=hh, r2=r2)
    return x2, saved


def layer_bwd(dx2, S, W, cos_t, sin_t, on_grads=None):
    T = dx2.shape[0]
    tk = min(2048, T)
    G = {}
    dr2, dgate, dup, G["ln2_g"], G["ln2_b"] = ffn_down_bwd(dx2, S["r2"], W["ln2_g"], W["w_down"], S["gate"], S["up"])
    blk_a = pl.BlockSpec((1, tk, FB), lambda k, t: (k, t, 0))
    row_b = pl.BlockSpec((tk, D), lambda k, t: (t, 0))
    G["w_down"] = tn_matmul("dw_down", S["hh"], dr2, blk_a, row_b, (NCHIP, FB, D),
                            pl.BlockSpec((1, FB, D), lambda k, t: (k, 0, 0)), (NCHIP, T // tk))
    x1t = transpose_cast(S["x1"])
    for nm, dv in (("w_gate", dgate), ("w_up", dup)):
        G[nm] = tn_matmul("d" + nm, x1t, dv, pl.BlockSpec((D, tk), lambda k, t: (0, t)), blk_a, (NCHIP, D, FB),
                          pl.BlockSpec((1, D, FB), lambda k, t: (k, 0, 0)), (NCHIP, T // tk), a_is_t=True)
    dr1, G["ln1_g"], G["ln1_b"] = ffn_up_bwd(dr2, dgate, dup, W["w_gate"], W["w_up"], S["r1"], W["ln1_g"])
    dmabc, dgates, dya, dyb, dyc = mix_out_bwd(dr1, S["proj"], S["mabc"], W["w_o"], W["p_a"], W["p_b"], W["p_c"])
    one = (1, T // tk)
    full_o = pl.BlockSpec((D, D), lambda k, t: (0, 0))
    G["w_o"] = tn_matmul("dw_o", S["m"], dr1, row_b, row_b, (D, D), full_o, one)
    G["p_a"] = tn_matmul("dp_a", S["ya"], dmabc, row_b, pl.BlockSpec((tk, D), lambda k, t: (t, 0)), (D, D), full_o, one)
    G["p_c"] = tn_matmul("dp_c", S["yc"], dmabc, row_b, pl.BlockSpec((tk, D), lambda k, t: (t, 2)), (D, D), full_o, one)
    G["p_b"] = tn_matmul("dp_b", S["yb"], dmabc, pl.BlockSpec((tk, AO), lambda k, t: (t, 0)),
                         pl.BlockSpec((tk, D // NCHIP), lambda k, t: (t, NCHIP + k)), (NCHIP, AO, D // NCHIP),
                         pl.BlockSpec((1, AO, D // NCHIP), lambda k, t: (k, 0, 0)), (NCHIP, T // tk))
    conv_w = W["conv_w"]
    if on_grads is not None:
        conv_w = conv_w + on_grads({n: G[n] for n in BIG if n != "w_in"})
    dbch, G["conv_w"] = conv_bwd(dya, S["proj"], conv_w)
    duv, G["w_s"], G["b_s"], G["gmlp_ln_g"], G["gmlp_ln_b"] = gmlp_bwd(
        dyc, S["proj"], W["wst"], W["bsx"], W["gmlp_ln_g"], W["gmlp_ln_b"])
    ones = _head_ones()
    if on_grads is not None:
        small = {n: G[n] for n in VECS + ("b_s", "w_s", "conv_w")}
        ones = ones + on_grads(small).astype(MX)
    pre = attn_pre_bwd(dyb, S["os"], S["lses"], ones)
    dqkv = []
    for g, (_, d) in enumerate(GROUPS):
        qf, kf, vf = S["folded"][g]
        dqf, dkf, dvf = attn_bwd(_flat(qf), _flat(kf), _flat(vf), _flat(pre[g]), _flat(S["lses"][g]), _flat(pre[3 + g]),
                                 g, T // d // BLK)
        shp = (d, T // d, AO)
        dqkv.append(unfold_rope_bwd(dqf.reshape(shp), dkf.reshape(shp), dvf.reshape(shp), cos_t, sin_t, g, d))
    parts = (dgates, dbch, *dqkv, duv)
    G["w_in"] = dw_in(transpose_cast(S["x0"]), parts)
    dx0 = dx_in(dr1, parts, W["w_in"])
    started = on_grads({"w_in": G["w_in"]}) if on_grads is not None else None
    return dx0, G, started


def prep_layer_weights(Wl):
    W = dict(Wl)
    tril = jnp.tril(jnp.ones((BLK, BLK), f32))
    W["wst"] = (Wl["w_s"] * tril[None]).astype(MX)
    W["bsx"] = jnp.broadcast_to(Wl["b_s"][:, :, None], (8, BLK, BLK))
    for n in ("gmlp_ln_g", "gmlp_ln_b", "ln1_g", "ln1_b", "ln2_g", "ln2_b"):
        W[n] = Wl[n].reshape(1, D)
    return W


def local_step(x, positions, target, layers, on_grads=None):
    cos_t, sin_t = rope_tables(positions)
    Ws = [prep_layer_weights(Wl) for Wl in layers]
    saved = []
    h = x
    for W in Ws:
        h, S = layer_fwd(h, W, cos_t, sin_t)
        saved.append(S)
    lsum, dh = loss_grad(h, target)
    if on_grads is not None:
        on_grads(len(Ws), {"loss": lsum})
    grads = [None] * len(Ws)
    started = None
    for l in reversed(range(len(Ws))):
        W = Ws[l]
        if started is not None:
            W = dict(W, ln2_g=W["ln2_g"] + started)
        hook = functools.partial(on_grads, l) if on_grads is not None else None
        dh, grads[l], started = layer_bwd(dh, saved[l], W, cos_t, sin_t, hook)
    return lsum, dh, grads


MESH = pl.DeviceIdType.MESH
ANY = pl.BlockSpec(memory_space=pl.ANY)
BIG = ("w_in", "w_gate", "w_up", "w_down", "p_a", "p_b", "p_c", "w_o")
NBIG = len(BIG)


def _place():
    x, y, c = lax.axis_index("x"), lax.axis_index("y"), lax.axis_index("c")
    return x, y, c, 2 * x + y


def _rcopy(src, dst, send, recv, dev):
    return pltpu.make_async_remote_copy(src_ref=src, dst_ref=dst, send_sem=send, recv_sem=recv, device_id=dev,
                                        device_id_type=MESH)


def _cols(ref, k, width):
    start = k * width if isinstance(k, int) else pl.multiple_of(k * width, LANES)
    return ref.at[:, pl.ds(start, width)]


CHUNK_BYTES = 1 << 20


def _pieces(shape, itemsize, nbytes=CHUNK_BYTES):
    rows, cols = shape[-2], shape[-1]
    per = max(16, nbytes // (cols * itemsize) // 16 * 16)
    out = []
    for lead in (range(shape[0]) if len(shape) == 3 else (None,)):
        for r in range(0, rows, per):
            sl = (pl.ds(r, min(per, rows - r)), slice(None))
            out.append(sl if lead is None else (lead,) + sl)
    return out


def _start_pieces(src, dst, make, nbytes=CHUNK_BYTES):
    for idx in _pieces(src.shape, jnp.dtype(src.dtype).itemsize, nbytes):
        make(src.at[idx], dst.at[idx]).start()


def gather_weights(shards):
    n = len(shards)

    def body(*refs):
        srcs, dsts = refs[:n], refs[n:2 * n]
        send, recv, own_send, own_recv = refs[2 * n:]
        x, y, c, k = _place()
        sib = (x, y, 1 - c)
        chips = [(1 - x, y), (x, 1 - y), (1 - x, 1 - y)]

        def slot(a, layer, pos):
            if a == 0:
                return _cols(dsts[0].at[layer], pos, WIN_SHARD)
            return dsts[a].at[layer, pos]

        def ici(a, j, src, dst):
            return _rcopy(src, dst, send.at[a, j], recv.at[a, j], (*chips[j], c))

        def d2d(a, j, src, dst):
            return _rcopy(src, dst, send.at[a, 3 + j], recv.at[a, 3 + j], sib)

        def own(a, layer, src, dst):
            return _rcopy(src, dst, own_send.at[a, layer], own_recv.at[a, layer], sib)

        for a in range(n):
            for j in range(3):
                _start_pieces(srcs[a].at[c], slot(a, c, k), functools.partial(ici, a, j))
        for a in range(n):
            for layer in range(DEPTH):
                _start_pieces(srcs[a].at[layer], slot(a, layer, k), functools.partial(own, a, layer))
        for a in range(n):
            for j, (cx, cy) in enumerate(chips):
                landed = slot(a, c, 2 * cx + cy)
                ici(a, j, landed, landed).wait_recv()
                _start_pieces(landed, landed, functools.partial(d2d, a, j))
        for a in range(n):
            for j, (cx, cy) in enumerate(chips):
                passed = slot(a, 1 - c, 2 * cx + cy)
                d2d(a, j, passed, passed).wait_recv()
                landed = slot(a, c, 2 * cx + cy)
                d2d(a, j, landed, landed).wait_send()
                ici(a, j, srcs[a].at[c], slot(a, c, k)).wait_send()
            for layer in range(DEPTH):
                own(a, layer, srcs[a].at[layer], slot(a, layer, k)).wait()

    outs = [jax.ShapeDtypeStruct((DEPTH, D, NIN), shards[0].dtype)]
    outs += [jax.ShapeDtypeStruct((DEPTH, NCHIP) + s.shape[1:], s.dtype) for s in shards[1:]]
    return pl.pallas_call(
        body, name="gather_weights", in_specs=[ANY] * n, out_specs=[ANY] * n, out_shape=outs,
        scratch_shapes=[pltpu.SemaphoreType.DMA((n, 6)), pltpu.SemaphoreType.DMA((n, 6)),
                        pltpu.SemaphoreType.DMA((n, DEPTH)), pltpu.SemaphoreType.DMA((n, DEPTH))],
    )(*shards)


def _half(ref, h):
    rows = ref.shape[-2] // 2
    start = pl.multiple_of(h * rows, 16)
    if len(ref.shape) == 2:
        return ref.at[pl.ds(start, rows), :]
    return ref.at[:, pl.ds(start, rows), :]


def rs_pair(tag, grads):
    n = len(grads)

    def body(*refs):
        g, theirs = refs[:n], refs[n:2 * n]
        send, recv = refs[2 * n:]
        x, y, c, _ = _place()

        def give(a, s, d):
            return _rcopy(s, d, send.at[a], recv.at[a], (x, y, 1 - c))

        for a in range(n):
            _start_pieces(_half(g[a], 1 - c), theirs[a], functools.partial(give, a))
        for a in range(n):
            give(a, _half(g[a], 1 - c), theirs[a]).wait()

    def hshape(s):
        return s[:-2] + (s[-2] // 2, s[-1])

    outs = [jax.ShapeDtypeStruct(hshape(g.shape), g.dtype) for g in grads]
    return pl.pallas_call(
        body, name=f"rs_pair{tag}", in_specs=[ANY] * n, out_specs=[ANY] * n, out_shape=outs,
        scratch_shapes=[pltpu.SemaphoreType.DMA((n,))] * 2,
    )(*grads)


HBM = pl.BlockSpec(memory_space=pltpu.HBM)
SEMS = pl.BlockSpec(memory_space=pltpu.SEMAPHORE)
EFFECT = pltpu.SideEffectType.DATAFLOW_SIDE_EFFECTING


def _chip_piece(ref, k):
    return _cols(ref, k, WIN_SHARD) if len(ref.shape) == 2 else ref.at[k]


def _chip_copy(a, k, src, dst, send, recv, me, c):
    return _rcopy(src, dst, send.at[a * NCHIP + k], recv.at[a * NCHIP + me], (k // 2, k % 2, c))


def rs_chips_start(tag, sums):
    n = len(sums)

    def pshape(s):
        return (NCHIP, s[0], WIN_SHARD) if len(s) == 2 else s

    def body(*refs):
        s, land = refs[:n], refs[n:2 * n]
        send, recv = refs[2 * n], refs[2 * n + 1]
        token = refs[-1]
        x, y, c, me = _place()
        for k in range(NCHIP):
            @pl.when(me != k)
            def _():
                for a in range(n):
                    _start_pieces(_chip_piece(s[a], k), land[a].at[me],
                                  lambda src, dst, a=a: _chip_copy(a, k, src, dst, send, recv, me, c))
        token[...] = jnp.zeros_like(token)

    lands = [lax.empty(pshape(v.shape), v.dtype) for v in sums]
    ops = [pltpu.with_memory_space_constraint(v, pltpu.HBM) for v in list(sums) + lands]
    sem = pltpu.SemaphoreType.DMA((n * NCHIP,))
    res = pl.pallas_call(
        body, name=f"rs_chips_start{tag}", in_specs=[HBM] * (2 * n),
        out_specs=[SEMS, SEMS] + [HBM] * (2 * n) + [pl.BlockSpec(memory_space=pltpu.VMEM)],
        out_shape=[sem, sem] + [pltpu.HBM(v.shape, v.dtype) for v in ops] + [jax.ShapeDtypeStruct((8, LANES), f32)],
        input_output_aliases={i: 2 + i for i in range(2 * n)},
        compiler_params=pltpu.CompilerParams(has_side_effects=EFFECT),
    )(*ops)
    return res[0], res[1], res[2:2 + n], res[2 + n:2 + 2 * n], res[-1]


def rs_chips_wait(tag, send, recv, sums, lands, after):
    n = len(sums)

    def body(*refs):
        s, land = refs[:n], refs[n:2 * n]
        send_r, recv_r = refs[2 * n], refs[2 * n + 1]
        x, y, c, me = _place()
        for k in range(NCHIP):
            @pl.when(me != k)
            def _():
                for a in range(n):
                    piece = _chip_piece(s[a], k)
                    _chip_copy(a, k, piece, land[a].at[me], send_r, recv_r, me, c).wait_send()
                    _rcopy(piece, land[a].at[k], send_r.at[a * NCHIP + k], recv_r.at[a * NCHIP + k],
                           (k // 2, k % 2, c)).wait_recv()

    ops = list(sums) + list(lands)
    res = pl.pallas_call(
        body, name=f"rs_chips_wait{tag}", in_specs=[HBM] * (2 * n) + [SEMS, SEMS] + [ANY] * len(after),
        out_specs=[HBM] * (2 * n), out_shape=[pltpu.HBM(v.shape, v.dtype) for v in ops],
        input_output_aliases={i: i for i in range(2 * n)},
        compiler_params=pltpu.CompilerParams(has_side_effects=EFFECT),
    )(*ops, send, recv, *after)
    return res[:n], res[n:]


def rs_join(tag, halves):
    n = len(halves)

    def body(*refs):
        h, other = refs[:n], refs[n:2 * n]
        send, recv = refs[2 * n:]
        x, y, c, _ = _place()

        def give(a, s, d):
            return _rcopy(s, d, send.at[a], recv.at[a], (x, y, 1 - c))

        for a in range(n):
            _start_pieces(h[a], other[a], functools.partial(give, a))
        for a in range(n):
            give(a, h[a], other[a]).wait()

    outs = [jax.ShapeDtypeStruct(v.shape, v.dtype) for v in halves]
    return pl.pallas_call(
        body, name=f"rs_join{tag}", in_specs=[ANY] * n, out_specs=[ANY] * n, out_shape=outs,
        scratch_shapes=[pltpu.SemaphoreType.DMA((n,))] * 2,
    )(*halves)


def _row_tile(rows, cols, itemsize=4, target=2 << 20):
    best = 8
    for t in range(8, rows + 1, 8):
        if rows % t == 0 and t * cols * itemsize <= target:
            best = t
    return best


GRAD_WIRE = jnp.bfloat16


def add_n(name, terms, out_dtype=f32):
    shape = terms[0].shape
    cols = shape[-1]
    rows = math.prod(shape[:-1])
    tr = _row_tile(rows, cols)

    def body(*refs):
        acc = refs[0][...]
        for r in refs[1:-1]:
            acc = acc + r[...]
        refs[-1][...] = acc.astype(out_dtype)

    tile = pl.BlockSpec((tr, cols), lambda i: (i, 0))
    out = pl.pallas_call(
        body, name=name, grid=(rows // tr,), in_specs=[tile] * len(terms), out_specs=tile,
        out_shape=jax.ShapeDtypeStruct((rows, cols), out_dtype), compiler_params=_cp(("parallel",)),
    )(*[t.reshape(rows, cols) for t in terms])
    return out.reshape(shape)


def add_chips(name, land, own):
    _, rows, cols = land.shape
    tr = _row_tile(rows, cols, target=1 << 20)

    def body(land_r, own_r, o_r):
        me = 2 * lax.axis_index("x") + lax.axis_index("y")
        for k in range(NCHIP):
            @pl.when(me == k)
            def _():
                acc = None
                for j in range(NCHIP):
                    t = (own_r[...] if j == k else land_r[j]).astype(f32)
                    acc = t if acc is None else acc + t
                o_r[...] = acc

    tile = pl.BlockSpec((tr, cols), lambda i: (i, 0))
    return pl.pallas_call(
        body, name=name, grid=(rows // tr,), in_specs=[pl.BlockSpec((NCHIP, tr, cols), lambda i: (0, i, 0)), tile],
        out_specs=tile, out_shape=jax.ShapeDtypeStruct((rows, cols), f32), compiler_params=_cp(("parallel",)),
    )(land, own)


def reduce_scatter_begin(tag, G):
    c = lax.axis_index("c")
    names = tuple(G)
    grads = [G[n] if G[n].ndim == 3 or n == "w_in" else G[n].reshape(NCHIP, D // NCHIP, D) for n in names]
    theirs = rs_pair(tag, grads)
    sums = []
    for n, g, t in zip(names, grads, theirs):
        rows = g.shape[-2] // 2
        mine = lax.dynamic_slice_in_dim(g, c * rows, rows, axis=g.ndim - 2)
        sums.append(add_n(f"rs_add_pair{tag}_{n}", [mine, t], GRAD_WIRE))
    send, recv, sums, lands, token = rs_chips_start(tag, sums)
    return (tag, names, send, recv, sums, lands), token[0, 0]


def reduce_scatter_finish(state, after):
    me = 2 * lax.axis_index("x") + lax.axis_index("y")
    tag, names, send, recv, sums, lands = state
    sums, landed = rs_chips_wait(tag, send, recv, sums, lands, after)
    halves = []
    for n, s, v in zip(names, sums, landed):
        own = lax.dynamic_slice_in_dim(s, me * WIN_SHARD, WIN_SHARD, axis=1) if s.ndim == 2 else \
            lax.dynamic_index_in_dim(s, me, 0, keepdims=False)
        halves.append(add_chips(f"rs_add_chips{tag}_{n}", v, own))
    return dict(zip(names, zip(halves, rs_join(tag, halves))))


NDEV = 8


def _small_copy(r, src, dst, send, recv, x, y, c):
    return _rcopy(src, dst, send.at[r - 1], recv.at[r - 1], (x ^ (r >> 2), y ^ ((r >> 1) & 1), c ^ (r & 1)))


def small_start(pack):
    def body(p, land, send, recv, p_thru, land_thru, token):
        x, y, c, _ = _place()
        me = 4 * x + 2 * y + c
        for r in range(1, NDEV):
            _start_pieces(p, land.at[me], lambda s, d, r=r: _small_copy(r, s, d, send, recv, x, y, c), 128 << 10)
        token[...] = jnp.zeros_like(token)

    ops = [pltpu.with_memory_space_constraint(v, pltpu.HBM) for v in (pack, lax.empty((NDEV,) + pack.shape, f32))]
    sem = pltpu.SemaphoreType.DMA((NDEV - 1,))
    return pl.pallas_call(
        body, name="small_start", in_specs=[HBM, HBM],
        out_specs=[SEMS, SEMS, HBM, HBM, pl.BlockSpec(memory_space=pltpu.VMEM)],
        out_shape=[sem, sem] + [pltpu.HBM(v.shape, v.dtype) for v in ops] + [jax.ShapeDtypeStruct((8, LANES), f32)],
        input_output_aliases={0: 2, 1: 3}, compiler_params=pltpu.CompilerParams(has_side_effects=EFFECT),
    )(*ops)


def small_wait(send, recv, pack, land, after):
    def body(p, land_r, send_r, recv_r, *rest):
        x, y, c, _ = _place()
        me = 4 * x + 2 * y + c
        for r in range(1, NDEV):
            _small_copy(r, p, land_r.at[me], send_r, recv_r, x, y, c).wait_send()
            src = 4 * (x ^ (r >> 2)) + 2 * (y ^ ((r >> 1) & 1)) + (c ^ (r & 1))
            _small_copy(r, p, land_r.at[src], send_r, recv_r, x, y, c).wait_recv()

    return pl.pallas_call(
        body, name="small_wait", in_specs=[HBM, HBM, SEMS, SEMS] + [ANY] * len(after), out_specs=[HBM, HBM],
        out_shape=[pltpu.HBM(pack.shape, f32), pltpu.HBM(land.shape, f32)], input_output_aliases={0: 0, 1: 1},
        compiler_params=pltpu.CompilerParams(has_side_effects=EFFECT),
    )(pack, land, send, recv, *after)


def small_sum(land, pack):
    def body(land_r, p_r, o_r):
        me = 4 * lax.axis_index("x") + 2 * lax.axis_index("y") + lax.axis_index("c")
        for k in range(NDEV):
            @pl.when(me == k)
            def _():
                acc = None
                for d in range(NDEV):
                    t = p_r[...] if d == k else land_r[d]
                    acc = t if acc is None else acc + t
                o_r[...] = acc

    vm = pl.BlockSpec(memory_space=pltpu.VMEM)
    return pl.pallas_call(
        body, name="small_sum", in_specs=[vm, vm], out_specs=vm, out_shape=jax.ShapeDtypeStruct(pack.shape, f32),
        compiler_params=pltpu.CompilerParams(vmem_limit_bytes=40 << 20),
    )(land, pack)


def _adamw_math(w, g, m, v):
    m = ADAM_B1 * m + (1.0 - ADAM_B1) * g
    v = ADAM_B2 * v + (1.0 - ADAM_B2) * (g * g)
    m_hat = m / (1.0 - ADAM_B1 ** ADAM_STEP)
    v_hat = v / (1.0 - ADAM_B2 ** ADAM_STEP)
    return -ADAM_LR * (m_hat / (jnp.sqrt(v_hat) + ADAM_EPS) + ADAM_WD * w), m, v


def adamw_big(name, halves, w, m, v):
    _, R, C = w.shape
    tr = _row_tile(R // 2, C, target=1 << 20)
    nt = R // 2 // tr

    def body(a0, b0, a1, b1, w_r, m_r, v_r, g_o, d_o, m_o, v_o):
        mine = pl.program_id(1) == lax.axis_index("c")
        g = jnp.where(pl.program_id(0) == 0, jnp.where(mine, a0[...], b0[...]), jnp.where(mine, a1[...], b1[...]))
        g_o[...] = g
        d_o[...], m_o[...], v_o[...] = _adamw_math(w_r[...], g, m_r[...], v_r[...])

    stk = pl.BlockSpec((None, tr, C), lambda l, h, i: (l, h * nt + i, 0))
    lay0 = pl.BlockSpec((tr, C), lambda l, h, i: (jnp.where(l == 0, i, nt - 1), 0))
    lay1 = pl.BlockSpec((tr, C), lambda l, h, i: (jnp.where(l == 0, 0, i), 0))
    return pl.pallas_call(
        body, name=name, grid=(DEPTH, 2, nt),
        in_specs=[lay0, lay0, lay1, lay1, stk, stk, stk],
        out_specs=[stk] * 4, out_shape=[jax.ShapeDtypeStruct(w.shape, f32)] * 4,
        compiler_params=_cp(("arbitrary", "arbitrary", "arbitrary")),
    )(*halves[0], *halves[1], w, m, v)


def adamw_small(name, g, w, m, v):
    def body(g_r, w_r, m_r, v_r, d_o, m_o, v_o):
        d_o[...], m_o[...], v_o[...] = _adamw_math(w_r[...], g_r[...], m_r[...], v_r[...])

    return pl.pallas_call(body, name=name, out_shape=[jax.ShapeDtypeStruct(w.shape, f32)] * 3)(g, w, m, v)


WEIGHTS = ("w_in", "conv_w", "gmlp_ln_g", "gmlp_ln_b", "w_s", "b_s", "p_a", "p_b", "p_c", "w_o", "ln1_g", "ln1_b",
           "w_gate", "w_up", "w_down", "ln2_g", "ln2_b")
VECS = ("ln1_g", "ln1_b", "ln2_g", "ln2_b", "gmlp_ln_g", "gmlp_ln_b")
ROWS_VEC, ROWS_BS, ROWS_WS, ROWS_CONV = D // LANES, 8, 8 * BLK, 3 * D // LANES
ROWS_LAYER = len(VECS) * ROWS_VEC + ROWS_BS + ROWS_WS + ROWS_CONV


def _pack_small(per_layer, tail):
    parts = []
    for P in per_layer:
        parts += [P[n].reshape(ROWS_VEC, LANES) for n in VECS]
        parts += [P["b_s"].reshape(ROWS_BS, LANES), P["w_s"].reshape(ROWS_WS, LANES), P["conv_w"].reshape(ROWS_CONV, LANES)]
    return jnp.concatenate(parts + [tail], axis=0)


def _unpack_small(pack):
    out = []
    for l in range(DEPTH):
        r = l * ROWS_LAYER
        P = {}
        for n in VECS:
            P[n] = pack[r:r + ROWS_VEC].reshape(D)
            r += ROWS_VEC
        P["b_s"] = pack[r:r + ROWS_BS].reshape(8, BLK)
        r += ROWS_BS
        P["w_s"] = pack[r:r + ROWS_WS].reshape(8, BLK, BLK)
        r += ROWS_WS
        P["conv_w"] = pack[r:r + ROWS_CONV].reshape(3, D)
        out.append(P)
    return out, pack[DEPTH * ROWS_LAYER:]


def kernel(x, positions, w_in, conv_w, gmlp_ln_g, gmlp_ln_b, w_s, b_s, p_a, p_b, p_c, w_o, ln1_g, ln1_b, w_gate, w_up, w_down, ln2_g, ln2_b, loss_target, m_w_in, m_conv_w, m_gmlp_ln_g, m_gmlp_ln_b, m_w_s, m_b_s, m_p_a, m_p_b, m_p_c, m_w_o, m_ln1_g, m_ln1_b, m_w_gate, m_w_up, m_w_down, m_ln2_g, m_ln2_b, v_w_in, v_conv_w, v_gmlp_ln_g, v_gmlp_ln_b, v_w_s, v_b_s, v_p_a, v_p_b, v_p_c, v_w_o, v_ln1_g, v_ln1_b, v_w_gate, v_w_up, v_w_down, v_ln2_g, v_ln2_b):
    Wt = dict(w_in=w_in, conv_w=conv_w, gmlp_ln_g=gmlp_ln_g, gmlp_ln_b=gmlp_ln_b, w_s=w_s, b_s=b_s, p_a=p_a, p_b=p_b,
              p_c=p_c, w_o=w_o, ln1_g=ln1_g, ln1_b=ln1_b, w_gate=w_gate, w_up=w_up, w_down=w_down, ln2_g=ln2_g, ln2_b=ln2_b)
    Mt = dict(w_in=m_w_in, conv_w=m_conv_w, gmlp_ln_g=m_gmlp_ln_g, gmlp_ln_b=m_gmlp_ln_b, w_s=m_w_s, b_s=m_b_s, p_a=m_p_a,
              p_b=m_p_b, p_c=m_p_c, w_o=m_w_o, ln1_g=m_ln1_g, ln1_b=m_ln1_b, w_gate=m_w_gate, w_up=m_w_up,
              w_down=m_w_down, ln2_g=m_ln2_g, ln2_b=m_ln2_b)
    Vt = dict(w_in=v_w_in, conv_w=v_conv_w, gmlp_ln_g=v_gmlp_ln_g, gmlp_ln_b=v_gmlp_ln_b, w_s=v_w_s, b_s=v_b_s, p_a=v_p_a,
              p_b=v_p_b, p_c=v_p_c, w_o=v_w_o, ln1_g=v_ln1_g, ln1_b=v_ln1_b, w_gate=v_w_gate, w_up=v_w_up,
              w_down=v_w_down, ln2_g=v_ln2_g, ln2_b=v_ln2_b)
    chip = 2 * lax.axis_index("x") + lax.axis_index("y")
    cw = D // NCHIP

    full = gather_weights([Wt[n].astype(MX) for n in BIG] + [conv_w])
    layers = []
    for l in range(DEPTH):
        Wl = dict(zip(BIG, (f[l] for f in full[:NBIG])))
        for n in ("p_a", "p_c", "w_o"):
            Wl[n] = Wl[n].reshape(D, D)
        Wl["conv_w"] = full[NBIG][l].transpose(1, 0, 2).reshape(3, D)
        for n in VECS + ("w_s", "b_s"):
            Wl[n] = Wt[n][l]
        layers.append(Wl)

    rs_state, rs_started, held = {}, {}, {}

    def start_exchange(l, g):
        if "loss" in g or "conv_w" in g:
            held[l] = g
            if l > 0:
                return jnp.zeros((), f32)
            pack = _pack_small([held[j] for j in range(DEPTH)], held[DEPTH]["loss"])
            *held["small"], token = small_start(pack)
            return token[0, 0]
        key = (l, "w_in" in g)
        rs_state[key], rs_started[key] = reduce_scatter_begin(f"{l}{'b' if key[1] else 'a'}", g)
        return rs_started[key]

    _, grad_x, _ = local_step(x[0], positions[0], loss_target[0], layers, start_exchange)

    last = jnp.zeros((8, LANES), f32) + rs_started[(0, True)]
    behind = [grad_x, last]
    red = [dict() for _ in range(DEPTH)]
    for key in ((1, False), (1, True), (0, False)):
        red[key[0]].update(reduce_scatter_finish(rs_state[key], behind))
    small, tail = _unpack_small(small_sum(*reversed(small_wait(*held["small"], behind))))
    loss = tail[0, 0]

    G, DW, NM, NV = {}, {}, {}, {}
    zc = jnp.zeros((3, D), f32)
    wp = _pack_small([{**{n: Wt[n][l] for n in VECS + ("b_s", "w_s")}, "conv_w": zc} for l in range(DEPTH)], jnp.zeros((8, LANES), f32))
    mp = _pack_small([{**{n: Mt[n][l] for n in VECS + ("b_s", "w_s")}, "conv_w": zc} for l in range(DEPTH)], jnp.zeros((8, LANES), f32))
    vp = _pack_small([{**{n: Vt[n][l] for n in VECS + ("b_s", "w_s")}, "conv_w": zc} for l in range(DEPTH)], jnp.ones((8, LANES), f32))
    gp = _pack_small(small, jnp.zeros((8, LANES), f32))
    outs = [_unpack_small(a)[0] for a in adamw_small("adamw_small", gp, wp, mp, vp)]
    for n in VECS + ("b_s", "w_s"):
        G[n] = jnp.stack([small[l][n] for l in range(DEPTH)])
        DW[n], NM[n], NV[n] = (jnp.stack([o[l][n] for l in range(DEPTH)]) for o in outs)
    gconv = jnp.stack([lax.dynamic_slice(small[l]["conv_w"], (0, chip * cw), (3, cw)) for l in range(DEPTH)])
    G["conv_w"] = gconv
    flat = lambda a: a.reshape(DEPTH * 3, cw)
    d, m2, v2 = adamw_small("adamw_conv", flat(gconv), flat(conv_w), flat(m_conv_w), flat(v_conv_w))
    DW["conv_w"], NM["conv_w"], NV["conv_w"] = (a.reshape(DEPTH, 3, cw) for a in (d, m2, v2))

    for n in BIG[1:]:
        G[n], DW[n], NM[n], NV[n] = adamw_big("adamw_" + n, (red[0][n], red[1][n]), Wt[n], Mt[n], Vt[n])
    done = [d, DW["ln2_b"], red[1]["w_in"][1]] + [DW[n] for n in BIG[1:]]
    red[0].update(reduce_scatter_finish(rs_state[(0, True)], done))
    G["w_in"], DW["w_in"], NM["w_in"], NV["w_in"] = adamw_big(
        "adamw_w_in", (red[0]["w_in"], red[1]["w_in"]), Wt["w_in"], Mt["w_in"], Vt["w_in"])

    return (loss, grad_x[None], *[G[n] for n in WEIGHTS], *[DW[n] for n in WEIGHTS], *[NM[n] for n in WEIGHTS],
            *[NV[n] for n in WEIGHTS])
```

```python
import functools
import math

import jax
import jax.numpy as jnp
from jax import lax
from jax.experimental import pallas as pl
from jax.experimental.pallas import tpu as pltpu

D = 1024
NIN = 12800
DFF = 2816
NCHIP = 4
FB = DFF // NCHIP
WIN_SHARD = NIN // NCHIP
DEPTH = 2
GROUPS = ((128, 1), (512, 4), (2048, 16))
HD = 64
BLK = 128
AO = 512
ALPHA = (2 * DEPTH) ** 0.25
EPS = 1e-5
ROPE_THETA = 10000.0
LANES = 128
NEG = -1e30

C_GATES, C_BCH, C_QKV, C_UV = 0, 3 * D, 6 * D, 6 * D + 9 * AO

MX = jnp.bfloat16
ACT = jnp.bfloat16

ADAM_LR, ADAM_B1, ADAM_B2, ADAM_EPS, ADAM_WD, ADAM_STEP = 0.001, 0.9, 0.999, 1e-08, 0.01, 10

f32 = jnp.float32
NT = (((1,), (1,)), ((), ()))
TN = (((0,), (0,)), ((), ()))


def _cp(sem, vmem_mb=48):
    return pltpu.CompilerParams(dimension_semantics=sem, vmem_limit_bytes=vmem_mb << 20)


def _dot(a, b, dims=None):
    if dims is None:
        return jnp.dot(a, b, preferred_element_type=f32)
    return lax.dot_general(a, b, dims, preferred_element_type=f32)


def _ln_stats(r):
    mu = jnp.mean(r, axis=-1, keepdims=True)
    xc = r - mu
    var = jnp.mean(xc * xc, axis=-1, keepdims=True)
    rstd = lax.rsqrt(var + EPS)
    return xc * rstd, rstd


def _ln_bwd(dy, xhat, rstd, g):
    dxh = dy * g
    return rstd * (dxh - jnp.mean(dxh, axis=-1, keepdims=True) - xhat * jnp.mean(dxh * xhat, axis=-1, keepdims=True))


def _gelu(x):
    return 0.5 * x * (1.0 + lax.erf(x * (1.0 / math.sqrt(2.0))))


def _gelu_grad(x):
    return 0.5 * (1.0 + lax.erf(x * (1.0 / math.sqrt(2.0)))) + x * jnp.exp(-0.5 * x * x) * (1.0 / math.sqrt(2.0 * math.pi))


def _sigmoid(x):
    return 0.5 * jnp.tanh(0.5 * x) + 0.5


def _acc_rows(o_ref, first, val):
    @pl.when(first)
    def _():
        o_ref[...] = jnp.zeros_like(o_ref)
    o_ref[...] += jnp.sum(val, axis=0, keepdims=True)


def mm_in(x, w):
    T = x.shape[0]
    tm, tn = min(1024, T), 1280

    def body(x_ref, w_ref, o_ref, xb):
        @pl.when(pl.program_id(1) == 0)
        def _():
            xb[...] = x_ref[...].astype(MX)
        o_ref[...] = _dot(xb[...], w_ref[...]).astype(o_ref.dtype)

    return pl.pallas_call(
        body, name="mm_in", grid=(T // tm, NIN // tn),
        in_specs=[pl.BlockSpec((tm, D), lambda i, j: (i, 0)), pl.BlockSpec((D, tn), lambda i, j: (0, j))],
        out_specs=pl.BlockSpec((tm, tn), lambda i, j: (i, j)),
        out_shape=jax.ShapeDtypeStruct((T, NIN), ACT),
        scratch_shapes=[pltpu.VMEM((tm, D), MX)],
        compiler_params=_cp(("parallel", "arbitrary")),
    )(x, w)


HALO = 16
TM_AC = 256


def _uv_specs():
    return [pl.BlockSpec((TM_AC, 512), functools.partial(lambda i, j: (i, j), j=C_UV // 512 + j)) for j in range(4)]


def _gmlp_fwd(up, vp, ws_ref, bs_ref, lg, lb):
    u = _gelu(up)
    xhat, rstd = _ln_stats(_gelu(vp))
    vn = xhat * lg + lb
    vnb = vn.astype(MX)
    rows = []
    for c in range(up.shape[0] // BLK):
        r = slice(c * BLK, (c + 1) * BLK)
        rows.append(jnp.concatenate(
            [_dot(ws_ref[g], vnb[r, g * BLK:(g + 1) * BLK]) + bs_ref[g] for g in range(8)], axis=1))
    return u, vn, xhat, rstd, jnp.concatenate(rows, axis=0)


def mix_ac_fwd(proj, conv_w, wst, bsx, lg, lb):
    T = proj.shape[0]
    tm = TM_AC

    def body(bch, halo, u0, u1, v0, v1, cw, ws, bs, lg_ref, lb_ref, ya, yc, zs):
        i = pl.program_id(0)
        pb = bch[...].astype(f32)
        z = pb[:, D:2 * D] * pb[:, 2 * D:]
        hz = halo[:, :D].astype(f32) * halo[:, D:].astype(f32)
        zs[0:HALO, :] = jnp.where(i > 0, hz, 0.0)
        zs[HALO:HALO + tm, :] = z
        cv = cw[0:1, :] * zs[HALO - 2:HALO - 2 + tm, :] + cw[1:2, :] * zs[HALO - 1:HALO - 1 + tm, :] + cw[2:3, :] * z
        ya[...] = (pb[:, :D] * cv).astype(ya.dtype)
        up = jnp.concatenate([u0[...], u1[...]], axis=1).astype(f32)
        vp = jnp.concatenate([v0[...], v1[...]], axis=1).astype(f32)
        u, _, _, _, sp = _gmlp_fwd(up, vp, ws, bs, lg_ref[...], lb_ref[...])
        yc[...] = (u * sp).astype(yc.dtype)

    full = lambda shape: pl.BlockSpec(shape, lambda i: (0,) * len(shape))
    return pl.pallas_call(
        body, name="mix_ac_fwd", grid=(T // tm,),
        in_specs=[pl.BlockSpec((tm, 3 * D), lambda i: (i, 1)),
                  pl.BlockSpec((HALO, 2 * D), lambda i: (jnp.maximum(i * (tm // HALO) - 1, 0), 2)),
                  *_uv_specs(), full((3, D)), full((8, BLK, BLK)), full((8, BLK, BLK)), full((1, D)), full((1, D))],
        out_specs=[pl.BlockSpec((tm, D), lambda i: (i, 0))] * 2,
        out_shape=[jax.ShapeDtypeStruct((T, D), MX)] * 2,
        scratch_shapes=[pltpu.VMEM((HALO + tm, D), f32)],
        compiler_params=_cp(("parallel",)),
    )(proj, proj, proj, proj, proj, proj, conv_w, wst, bsx, lg, lb)


def _swap_halves(x):
    lane = lax.broadcasted_iota(jnp.int32, x.shape, 1)
    return jnp.where((lane % HD) < HD // 2, pltpu.roll(x, x.shape[1] - HD // 2, 1), pltpu.roll(x, HD // 2, 1))


def _tile4(t):
    return jnp.concatenate([t] * (AO // LANES), axis=1)


TM_FOLD = 512


def _fold_out(nat, x, out_ref, d):
    if d == 1:
        out_ref[0] = x.astype(out_ref.dtype)
        return
    rows = x.shape[0] // d
    for j in range(AO // LANES):
        nat[j] = x[:, j * LANES:(j + 1) * LANES]
    for r in range(d):
        out_ref[r] = jnp.concatenate(
            [nat.at[j][pl.ds(r, rows, stride=d), :] for j in range(AO // LANES)], axis=1).astype(out_ref.dtype)


def _unfold_in(nat, in_ref, d):
    if d == 1:
        return in_ref[0].astype(f32)
    rows = in_ref.shape[1]
    for r in range(d):
        v = in_ref[r].astype(f32)
        for j in range(AO // LANES):
            nat.at[j][pl.ds(r, rows, stride=d), :] = v[:, j * LANES:(j + 1) * LANES]
    return jnp.concatenate([nat[j] for j in range(AO // LANES)], axis=1)


def fold_rope(proj, cos_t, sin_t, g, d):
    T = proj.shape[0]
    tm = TM_FOLD
    rows = tm // d

    def body(x_ref, c_ref, s_ref, q_o, k_o, v_o, nat):
        cos, sin = _tile4(c_ref[...]), _tile4(s_ref[...])
        for part, out, scale in ((0, q_o, HD ** -0.5), (1, k_o, 1.0), (2, v_o, None)):
            x = x_ref[:, part * AO:(part + 1) * AO].astype(f32)
            if scale is not None:
                x = (x * cos + _swap_halves(x) * sin) * scale
            _fold_out(nat, x, out, d)

    fold_spec = pl.BlockSpec((d, rows, AO), lambda i: (0, i, 0))
    return pl.pallas_call(
        body, name=f"fold_rope{g}", grid=(T // tm,),
        in_specs=[pl.BlockSpec((tm, 3 * AO), lambda i: (i, C_QKV // (3 * AO) + g)),
                  pl.BlockSpec((tm, LANES), lambda i: (i, 0)), pl.BlockSpec((tm, LANES), lambda i: (i, 0))],
        out_specs=[fold_spec] * 3,
        out_shape=[jax.ShapeDtypeStruct((d, T // d, AO), MX)] * 3,
        scratch_shapes=[pltpu.VMEM((AO // LANES, tm, LANES), f32)],
        compiler_params=_cp(("parallel",)),
    )(proj, cos_t, sin_t)


def _stack_heads(x):
    lane = lax.broadcasted_iota(jnp.int32, x.shape, 1)
    z = jnp.zeros_like(x)
    return jnp.concatenate([jnp.where(lane < HD, x, z), jnp.where(lane >= HD, x, z)], axis=0)


def _unstack_heads(y):
    lane = lax.broadcasted_iota(jnp.int32, (BLK, LANES), 1)
    return jnp.where(lane < HD, y[:BLK], y[BLK:])


def _window_masks():
    row = lax.broadcasted_iota(jnp.int32, (2 * BLK, 2 * BLK), 0) % BLK
    col = lax.broadcasted_iota(jnp.int32, (2 * BLK, 2 * BLK), 1)
    return (col < BLK) & (col >= row), (col >= BLK) & (col - BLK <= row)


def _two_blocks(ref, b):
    r0 = pl.multiple_of(b * BLK, BLK)
    rp = pl.multiple_of(jnp.maximum(b - 1, 0) * BLK, BLK)
    return jnp.concatenate([ref[pl.ds(rp, BLK), :], ref[pl.ds(r0, BLK), :]], axis=0)


def attn_fwd(qf, kf, vf, g, nb):
    T = qf.shape[0]

    def body(q_ref, k_ref, v_ref, o_ref, l_ref):
        prev_m, cur_m = _window_masks()

        def step(b, carry):
            r0 = pl.multiple_of(b * BLK, BLK)
            qs = _stack_heads(q_ref[pl.ds(r0, BLK), :])
            s = _dot(qs, _two_blocks(k_ref, b), NT)
            s = jnp.where(cur_m | (prev_m & ((b % nb) != 0)), s, NEG)
            m = jnp.max(s, axis=-1, keepdims=True)
            p = jnp.exp(s - m)
            l = jnp.sum(p, axis=-1, keepdims=True)
            o = _dot(p.astype(MX), _two_blocks(v_ref, b)) / l
            o_ref[pl.ds(r0, BLK), :] = _unstack_heads(o)
            l_ref[pl.ds(r0, BLK), :] = _unstack_heads(jnp.broadcast_to(m + jnp.log(l), (2 * BLK, LANES)))
            return carry

        lax.fori_loop(0, T // BLK, step, 0, unroll=4)

    spec = pl.BlockSpec((T, LANES), lambda j: (0, j))
    return pl.pallas_call(
        body, name=f"attn_fwd{g}", grid=(AO // LANES,),
        in_specs=[spec] * 3, out_specs=[spec] * 2,
        out_shape=[jax.ShapeDtypeStruct((T, AO), f32)] * 2,
        compiler_params=_cp(("parallel",), 56),
    )(qf, kf, vf)


def _group_weights(lses):
    m = jnp.maximum(jnp.maximum(lses[0], lses[1]), lses[2])
    e = [jnp.exp(l - m) for l in lses]
    inv = 1.0 / (e[0] + e[1] + e[2])
    return [x * inv for x in e]


def _fold_specs(T, tm):
    specs = []
    for _, d in GROUPS:
        specs.append(pl.BlockSpec((d, tm // d, AO), lambda i: (0, i, 0)))
    return specs


def combine_fwd(os_, lses):
    T = os_[0].shape[0] * os_[0].shape[1]
    tm = TM_FOLD

    def body(o0, o1, o2, l0, l1, l2, y_ref, nat):
        o = [_unfold_in(nat, r, d) for r, (_, d) in zip((o0, o1, o2), GROUPS)]
        ls = [_unfold_in(nat, r, d) for r, (_, d) in zip((l0, l1, l2), GROUPS)]
        w = _group_weights(ls)
        y_ref[...] = (w[0] * o[0] + w[1] * o[1] + w[2] * o[2]).astype(y_ref.dtype)

    specs = _fold_specs(T, tm)
    return pl.pallas_call(
        body, name="combine_fwd", grid=(T // tm,),
        in_specs=specs + specs, out_specs=pl.BlockSpec((tm, AO), lambda i: (i, 0)),
        out_shape=jax.ShapeDtypeStruct((T, AO), MX),
        scratch_shapes=[pltpu.VMEM((AO // LANES, tm, LANES), f32)],
        compiler_params=_cp(("parallel",)),
    )(*os_, *lses)


TM_MIX = 256


def mix_out_fwd(proj, ya, yb, yc, x0, pa, pb, pc, wo, g1, b1):
    T = x0.shape[0]
    tm = min(TM_MIX, T)

    def body(gt, ya_r, yb_r, yc_r, x0_r, pa_r, pb_r, pc_r, wo_r, g_r, b_r, mabc, m_o, r1_o, x1_o):
        ma = _dot(ya_r[...], pa_r[...])
        ybv = yb_r[...]
        mb = jnp.concatenate([_dot(ybv, pb_r[k]) for k in range(NCHIP)], axis=1)
        mc = _dot(yc_r[...], pc_r[...])
        m = jnp.zeros((tm, D), f32)
        for j, mm in enumerate((ma, mb, mc)):
            mabc[:, j * D:(j + 1) * D] = mm.astype(mabc.dtype)
            m = m + _sigmoid(gt[:, j * D:(j + 1) * D].astype(f32)) * mm
        mb16 = m.astype(MX)
        m_o[...] = mb16
        r1 = ALPHA * x0_r[...] + _dot(mb16, wo_r[...])
        r1_o[...] = r1
        xhat, _ = _ln_stats(r1)
        x1_o[...] = xhat * g_r[...] + b_r[...]

    full = lambda shape: pl.BlockSpec(shape, lambda i: (0,) * len(shape))
    tile = lambda w: pl.BlockSpec((tm, w), lambda i: (i, 0))
    return pl.pallas_call(
        body, name="mix_out_fwd", grid=(T // tm,),
        in_specs=[tile(3 * D), tile(D), tile(AO), tile(D), tile(D), full((D, D)), full((NCHIP, AO, D // NCHIP)),
                  full((D, D)), full((D, D)), full((1, D)), full((1, D))],
        out_specs=[tile(3 * D), tile(D), tile(D), tile(D)],
        out_shape=[jax.ShapeDtypeStruct((T, 3 * D), MX), jax.ShapeDtypeStruct((T, D), MX),
                   jax.ShapeDtypeStruct((T, D), f32), jax.ShapeDtypeStruct((T, D), f32)],
        compiler_params=_cp(("parallel",), 56),
    )(proj, ya, yb, yc, x0, pa, pb, pc, wo, g1, b1)


TM_FF = 512


def ffn_up_fwd(x1, wg, wu):
    T = x1.shape[0]
    tm = min(TM_FF, T)

    def body(x_r, wg_r, wu_r, g_o, u_o, h_o, xb):
        @pl.when(pl.program_id(1) == 0)
        def _():
            xb[...] = x_r[...].astype(MX)
        gate = _dot(xb[...], wg_r[0])
        up = _dot(xb[...], wu_r[0])
        g_o[0] = gate.astype(g_o.dtype)
        u_o[0] = up.astype(u_o.dtype)
        h_o[0] = (gate * _sigmoid(gate) * up).astype(h_o.dtype)

    wspec = pl.BlockSpec((1, D, FB), lambda i, k: (k, 0, 0))
    ospec = pl.BlockSpec((1, tm, FB), lambda i, k: (k, i, 0))
    return pl.pallas_call(
        body, name="ffn_up_fwd", grid=(T // tm, NCHIP),
        in_specs=[pl.BlockSpec((tm, D), lambda i, k: (i, 0)), wspec, wspec],
        out_specs=[ospec] * 3,
        out_shape=[jax.ShapeDtypeStruct((NCHIP, T, FB), ACT)] * 2 + [jax.ShapeDtypeStruct((NCHIP, T, FB), MX)],
        scratch_shapes=[pltpu.VMEM((tm, D), MX)],
        compiler_params=_cp(("parallel", "arbitrary")),
    )(x1, wg, wu)


def ffn_down_fwd(hh, wd, x1, g2, b2):
    T = x1.shape[0]
    tm = min(TM_FF, T)

    def body(h_r, w_r, x_r, g_r, b_r, r2_o, x2_o):
        r2 = ALPHA * x_r[...]
        for k in range(NCHIP):
            r2 = r2 + _dot(h_r[k], w_r[k])
        r2_o[...] = r2
        xhat, _ = _ln_stats(r2)
        x2_o[...] = xhat * g_r[...] + b_r[...]

    tile = pl.BlockSpec((tm, D), lambda i: (i, 0))
    vec = pl.BlockSpec((1, D), lambda i: (0, 0))
    return pl.pallas_call(
        body, name="ffn_down_fwd", grid=(T // tm,),
        in_specs=[pl.BlockSpec((NCHIP, tm, FB), lambda i: (0, i, 0)), pl.BlockSpec((NCHIP, FB, D), lambda i: (0, 0, 0)),
                  tile, vec, vec],
        out_specs=[tile, tile], out_shape=[jax.ShapeDtypeStruct((T, D), f32)] * 2,
        compiler_params=_cp(("parallel",)),
    )(hh, wd, x1, g2, b2)


def loss_grad(y, tgt):
    T = y.shape[0]
    tm = min(512, T)

    def body(y_r, t_r, l_o, dy_o):
        e = y_r[...] - t_r[...]
        dy_o[...] = e * (1.0 / D)

        @pl.when(pl.program_id(0) == 0)
        def _():
            l_o[...] = jnp.zeros_like(l_o)
        l_o[...] += (0.5 / D) * jnp.sum(e * e)

    tile = pl.BlockSpec((tm, D), lambda i: (i, 0))
    return pl.pallas_call(
        body, name="loss_grad", grid=(T // tm,),
        in_specs=[tile, tile], out_specs=[pl.BlockSpec((8, LANES), lambda i: (0, 0)), tile],
        out_shape=[jax.ShapeDtypeStruct((8, LANES), f32), jax.ShapeDtypeStruct((T, D), f32)],
        compiler_params=_cp(("arbitrary",)),
    )(y, tgt)


def ffn_down_bwd(dx2, r2, g2, wd, gate, up):
    T = dx2.shape[0]
    tm = min(TM_FF, T)

    def body(dx_r, r_r, g_r, w_r, ga_r, up_r, dr_o, dg_o, du_o, dlg_o, dlb_o, drb):
        i, k = pl.program_id(0), pl.program_id(1)

        @pl.when(k == 0)
        def _():
            xhat, rstd = _ln_stats(r_r[...])
            dx = dx_r[...]
            _acc_rows(dlg_o, i == 0, dx * xhat)
            _acc_rows(dlb_o, i == 0, dx)
            dr = _ln_bwd(dx, xhat, rstd, g_r[...])
            dr_o[...] = dr
            drb[...] = dr.astype(MX)

        dhh = _dot(drb[...], w_r[0], NT)
        gate_v, up_v = ga_r[0].astype(f32), up_r[0].astype(f32)
        sg = _sigmoid(gate_v)
        dg_o[0] = (dhh * up_v * sg * (1.0 + gate_v * (1.0 - sg))).astype(dg_o.dtype)
        du_o[0] = (dhh * gate_v * sg).astype(du_o.dtype)

    tile = pl.BlockSpec((tm, D), lambda i, k: (i, 0))
    vec = pl.BlockSpec((1, D), lambda i, k: (0, 0))
    blk = pl.BlockSpec((1, tm, FB), lambda i, k: (k, i, 0))
    return pl.pallas_call(
        body, name="ffn_down_bwd", grid=(T // tm, NCHIP),
        in_specs=[tile, tile, vec, pl.BlockSpec((1, FB, D), lambda i, k: (k, 0, 0)), blk, blk],
        out_specs=[tile, blk, blk, vec, vec],
        out_shape=[jax.ShapeDtypeStruct((T, D), f32)] + [jax.ShapeDtypeStruct((NCHIP, T, FB), MX)] * 2
        + [jax.ShapeDtypeStruct((1, D), f32)] * 2,
        scratch_shapes=[pltpu.VMEM((tm, D), MX)],
        compiler_params=_cp(("arbitrary", "arbitrary")),
    )(dx2, r2, g2, wd, gate, up)


def ffn_up_bwd(dr2, dgate, dup, wg, wu, r1, g1):
    T = dr2.shape[0]
    tm = min(TM_FF, T)

    def body(dr2_r, dg_r, du_r, wg_r, wu_r, r1_r, g_r, dr1_o, dlg_o, dlb_o, acc):
        i, k = pl.program_id(0), pl.program_id(1)

        @pl.when(k == 0)
        def _():
            acc[...] = ALPHA * dr2_r[...]
        acc[...] += _dot(dg_r[0], wg_r[0], NT) + _dot(du_r[0], wu_r[0], NT)

        @pl.when(k == NCHIP - 1)
        def _():
            dx = acc[...]
            xhat, rstd = _ln_stats(r1_r[...])
            _acc_rows(dlg_o, i == 0, dx * xhat)
            _acc_rows(dlb_o, i == 0, dx)
            dr1_o[...] = _ln_bwd(dx, xhat, rstd, g_r[...])

    tile = pl.BlockSpec((tm, D), lambda i, k: (i, 0))
    vec = pl.BlockSpec((1, D), lambda i, k: (0, 0))
    blk = pl.BlockSpec((1, tm, FB), lambda i, k: (k, i, 0))
    wspec = pl.BlockSpec((1, D, FB), lambda i, k: (k, 0, 0))
    return pl.pallas_call(
        body, name="ffn_up_bwd", grid=(T // tm, NCHIP),
        in_specs=[tile, blk, blk, wspec, wspec, tile, vec],
        out_specs=[tile, vec, vec],
        out_shape=[jax.ShapeDtypeStruct((T, D), f32)] + [jax.ShapeDtypeStruct((1, D), f32)] * 2,
        scratch_shapes=[pltpu.VMEM((tm, D), f32)],
        compiler_params=_cp(("arbitrary", "arbitrary")),
    )(dr2, dgate, dup, wg, wu, r1, g1)


def mix_out_bwd(dr1, proj, mabc, wo, pa, pb, pc):
    T = dr1.shape[0]
    tm = min(TM_MIX, T)

    def body(dr_r, gt, mabc_r, wo_r, pa_r, pb_r, pc_r, dmabc_o, dgt_o, dya_o, dyb_o, dyc_o):
        dm = _dot(dr_r[...].astype(MX), wo_r[...], NT)
        dmx = []
        for j in range(3):
            s = _sigmoid(gt[:, j * D:(j + 1) * D].astype(f32))
            v = (dm * s).astype(MX)
            dmx.append(v)
            dmabc_o[:, j * D:(j + 1) * D] = v
            dgt_o[:, j * D:(j + 1) * D] = (dm * mabc_r[:, j * D:(j + 1) * D].astype(f32) * s * (1.0 - s)).astype(dgt_o.dtype)
        dya_o[...] = _dot(dmx[0], pa_r[...], NT)
        dyb = jnp.zeros((tm, AO), f32)
        for k in range(NCHIP):
            dyb = dyb + _dot(dmx[1][:, k * (D // NCHIP):(k + 1) * (D // NCHIP)], pb_r[k], NT)
        dyb_o[...] = dyb
        dyc_o[...] = _dot(dmx[2], pc_r[...], NT)

    full = lambda shape: pl.BlockSpec(shape, lambda i: (0,) * len(shape))
    tile = lambda w: pl.BlockSpec((tm, w), lambda i: (i, 0))
    return pl.pallas_call(
        body, name="mix_out_bwd", grid=(T // tm,),
        in_specs=[tile(D), tile(3 * D), tile(3 * D), full((D, D)), full((D, D)), full((NCHIP, AO, D // NCHIP)), full((D, D))],
        out_specs=[tile(3 * D), tile(3 * D), tile(D), tile(AO), tile(D)],
        out_shape=[jax.ShapeDtypeStruct((T, 3 * D), MX), jax.ShapeDtypeStruct((T, 3 * D), MX),
                   jax.ShapeDtypeStruct((T, D), f32), jax.ShapeDtypeStruct((T, AO), f32), jax.ShapeDtypeStruct((T, D), f32)],
        compiler_params=_cp(("parallel",), 56),
    )(dr1, proj, mabc, wo, pa, pb, pc)


def transpose_cast(x):
    T = x.shape[0]
    tm = min(512, T)

    def body(x_r, o_r):
        o_r[...] = x_r[...].T.astype(o_r.dtype)

    return pl.pallas_call(
        body, name="transpose_cast", grid=(T // tm,),
        in_specs=[pl.BlockSpec((tm, D), lambda i: (i, 0))], out_specs=pl.BlockSpec((D, tm), lambda i: (0, i)),
        out_shape=jax.ShapeDtypeStruct((D, T), MX), compiler_params=_cp(("parallel",)),
    )(x)


def tn_matmul(name, a, b, a_spec, b_spec, out_shape, out_spec, grid, a_is_t=False):
    nt = len(grid) - 1

    def body(a_r, b_r, o_r):
        @pl.when(pl.program_id(nt) == 0)
        def _():
            o_r[...] = jnp.zeros_like(o_r)
        av = a_r[...].reshape(a_r.shape[-2:]).astype(MX)
        bv = b_r[...].reshape(b_r.shape[-2:]).astype(MX)
        o_r[...] += _dot(av, bv, None if a_is_t else TN).reshape(o_r.shape)

    return pl.pallas_call(
        body, name=name, grid=grid, in_specs=[a_spec, b_spec], out_specs=out_spec,
        out_shape=jax.ShapeDtypeStruct(out_shape, f32),
        compiler_params=_cp(("parallel",) * nt + ("arbitrary",), 56),
    )(a, b)


def attn_pre_bwd(dyb, os_, lses, ones):
    T = dyb.shape[0]
    tm = TM_FOLD

    def body(dy_r, o0, o1, o2, l0, l1, l2, ones_r, d0, d1, d2, f0, f1, f2, nat):
        o = [_unfold_in(nat, r, d) for r, (_, d) in zip((o0, o1, o2), GROUPS)]
        ls = [_unfold_in(nat, r, d) for r, (_, d) in zip((l0, l1, l2), GROUPS)]
        w = _group_weights(ls)
        dy = dy_r[...]
        t = dy * (w[0] * o[0] + w[1] * o[1] + w[2] * o[2])
        hi = t.astype(MX)
        lo = (t - hi.astype(f32)).astype(MX)
        c = _dot(hi, ones_r[...]) + _dot(lo, ones_r[...])
        for wg, do_o, df_o, (_, d) in zip(w, (d0, d1, d2), (f0, f1, f2), GROUPS):
            _fold_out(nat, wg * dy, do_o, d)
            _fold_out(nat, -wg * c, df_o, d)

    specs = _fold_specs(T, tm)
    return pl.pallas_call(
        body, name="attn_pre_bwd", grid=(T // tm,),
        in_specs=[pl.BlockSpec((tm, AO), lambda i: (i, 0))] + specs + specs + [pl.BlockSpec((AO, AO), lambda i: (0, 0))],
        out_specs=specs + specs,
        out_shape=[jax.ShapeDtypeStruct((d, T // d, AO), MX) for _, d in GROUPS]
        + [jax.ShapeDtypeStruct((d, T // d, AO), f32) for _, d in GROUPS],
        scratch_shapes=[pltpu.VMEM((AO // LANES, tm, LANES), f32)],
        compiler_params=_cp(("parallel",)),
    )(dyb, *os_, *lses, ones)


def _head_ones():
    i = jnp.arange(AO) // HD
    return (i[:, None] == i[None, :]).astype(MX)


def attn_bwd(qf, kf, vf, dof, lse, df, g, nb):
    T = qf.shape[0]

    def body(q_ref, k_ref, v_ref, do_ref, l_ref, d_ref, dq_ref, dk_ref, dv_ref):
        prev_m, cur_m = _window_masks()

        def head_col(ref, r0):
            v = ref[pl.ds(r0, BLK), :]
            return jnp.concatenate([v[:, 0:1], v[:, HD:HD + 1]], axis=0)

        def step(b, carry):
            dk_c, dv_c = carry
            r0 = pl.multiple_of(b * BLK, BLK)
            rp = pl.multiple_of(jnp.maximum(b - 1, 0) * BLK, BLK)
            qs, dos = _stack_heads(q_ref[pl.ds(r0, BLK), :]), _stack_heads(do_ref[pl.ds(r0, BLK), :])
            k2, v2 = _two_blocks(k_ref, b), _two_blocks(v_ref, b)
            valid = cur_m | (prev_m & ((b % nb) != 0))
            p = jnp.where(valid, jnp.exp(_dot(qs, k2, NT) - head_col(l_ref, r0)), 0.0)
            ds = (p * (_dot(dos, v2, NT) + head_col(d_ref, r0))).astype(MX)
            dq_ref[pl.ds(r0, BLK), :] = _unstack_heads(_dot(ds, k2)).astype(dq_ref.dtype)
            dk2 = _dot(ds, qs, TN)
            dv2 = _dot(p.astype(MX), dos, TN)
            dk_ref[pl.ds(rp, BLK), :] = (dk_c + dk2[:BLK]).astype(dk_ref.dtype)
            dv_ref[pl.ds(rp, BLK), :] = (dv_c + dv2[:BLK]).astype(dv_ref.dtype)
            return dk2[BLK:], dv2[BLK:]

        zero = jnp.zeros((BLK, LANES), f32)

        def two_steps(i, carry):
            return step(2 * i + 1, step(2 * i, carry))

        dk_c, dv_c = lax.fori_loop(0, T // BLK // 2, two_steps, (zero, zero))
        dk_ref[pl.ds(T - BLK, BLK), :] = dk_c.astype(dk_ref.dtype)
        dv_ref[pl.ds(T - BLK, BLK), :] = dv_c.astype(dv_ref.dtype)

    spec = pl.BlockSpec((T, LANES), lambda j: (0, j))
    return pl.pallas_call(
        body, name=f"attn_bwd{g}", grid=(AO // LANES,),
        in_specs=[spec] * 6, out_specs=[spec] * 3,
        out_shape=[jax.ShapeDtypeStruct((T, AO), MX)] * 3,
        compiler_params=_cp(("parallel",), 60),
    )(qf, kf, vf, dof, lse, df)


def unfold_rope_bwd(dqf, dkf, dvf, cos_t, sin_t, g, d):
    T = dqf.shape[0] * dqf.shape[1]
    tm = TM_FOLD

    def body(q_r, k_r, v_r, c_ref, s_ref, o_ref, nat):
        cos, sin = _tile4(c_ref[...]), _tile4(s_ref[...])
        for part, ref, scale in ((0, q_r, HD ** -0.5), (1, k_r, 1.0), (2, v_r, None)):
            x = _unfold_in(nat, ref, d)
            if scale is not None:
                x = (x * cos - _swap_halves(x) * sin) * scale
            o_ref[:, part * AO:(part + 1) * AO] = x.astype(o_ref.dtype)

    fold_spec = pl.BlockSpec((d, tm // d, AO), lambda i: (0, i, 0))
    tab = pl.BlockSpec((tm, LANES), lambda i: (i, 0))
    return pl.pallas_call(
        body, name=f"unfold_rope_bwd{g}", grid=(T // tm,),
        in_specs=[fold_spec] * 3 + [tab, tab],
        out_specs=pl.BlockSpec((tm, 3 * AO), lambda i: (i, 0)),
        out_shape=jax.ShapeDtypeStruct((T, 3 * AO), MX),
        scratch_shapes=[pltpu.VMEM((AO // LANES, tm, LANES), f32)],
        compiler_params=_cp(("parallel",)),
    )(dqf, dkf, dvf, cos_t, sin_t)


def conv_bwd(dya, proj, conv_w):
    T = dya.shape[0]
    tm = TM_AC
    last = T // tm - 1

    def body(dy_r, bch, hprev, dy_next, b_next, cw, d_o, dw_o, zs, ds):
        i = pl.program_id(0)
        pb = bch[...].astype(f32)
        bp, cp, hp = pb[:, :D], pb[:, D:2 * D], pb[:, 2 * D:]
        z = cp * hp
        hz = hprev[:, :D].astype(f32) * hprev[:, D:].astype(f32)
        zs[0:HALO, :] = jnp.where(i > 0, hz, 0.0)
        zs[HALO:HALO + tm, :] = z
        z2, z1 = zs[HALO - 2:HALO - 2 + tm, :], zs[HALO - 1:HALO - 1 + tm, :]
        cv = cw[0:1, :] * z2 + cw[1:2, :] * z1 + cw[2:3, :] * z
        dy = dy_r[...]
        dcv = dy * bp
        ds[0:tm, :] = dcv
        ds[tm:tm + HALO, :] = jnp.where(i < last, dy_next[...] * b_next[...].astype(f32), 0.0)
        dz = cw[2:3, :] * dcv + cw[1:2, :] * ds[1:1 + tm, :] + cw[0:1, :] * ds[2:2 + tm, :]
        d_o[:, :D] = (dy * cv).astype(d_o.dtype)
        d_o[:, D:2 * D] = (dz * hp).astype(d_o.dtype)
        d_o[:, 2 * D:] = (dz * cp).astype(d_o.dtype)

        @pl.when(i == 0)
        def _():
            dw_o[...] = jnp.zeros_like(dw_o)
        dw_o[0:1, :] += jnp.sum(dcv * z2, axis=0, keepdims=True)
        dw_o[1:2, :] += jnp.sum(dcv * z1, axis=0, keepdims=True)
        dw_o[2:3, :] += jnp.sum(dcv * z, axis=0, keepdims=True)

    nh = tm // HALO
    return pl.pallas_call(
        body, name="conv_bwd", grid=(T // tm,),
        in_specs=[pl.BlockSpec((tm, D), lambda i: (i, 0)), pl.BlockSpec((tm, 3 * D), lambda i: (i, 1)),
                  pl.BlockSpec((HALO, 2 * D), lambda i: (jnp.maximum(i * nh - 1, 0), 2)),
                  pl.BlockSpec((HALO, D), lambda i: (jnp.minimum((i + 1) * nh, T // HALO - 1), 0)),
                  pl.BlockSpec((HALO, D), lambda i: (jnp.minimum((i + 1) * nh, T // HALO - 1), 3)),
                  pl.BlockSpec((3, D), lambda i: (0, 0))],
        out_specs=[pl.BlockSpec((tm, 3 * D), lambda i: (i, 0)), pl.BlockSpec((3, D), lambda i: (0, 0))],
        out_shape=[jax.ShapeDtypeStruct((T, 3 * D), MX), jax.ShapeDtypeStruct((3, D), f32)],
        scratch_shapes=[pltpu.VMEM((HALO + tm, D), f32), pltpu.VMEM((tm + HALO, D), f32)],
        compiler_params=_cp(("arbitrary",)),
    )(dya, proj, proj, dya, proj, conv_w)


def gmlp_bwd(dyc, proj, wst, bsx, lg, lb):
    T = dyc.shape[0]
    tm = TM_AC
    last = T // tm - 1

    def body(dy_r, u0, u1, v0, v1, ws, bs, lg_r, lb_r, d_o, dws_o, dbs_o, dlg_o, dlb_o, bacc):
        i = pl.program_id(0)
        up = jnp.concatenate([u0[...], u1[...]], axis=1).astype(f32)
        vp = jnp.concatenate([v0[...], v1[...]], axis=1).astype(f32)
        u, vn, xhat, rstd, sp = _gmlp_fwd(up, vp, ws, bs, lg_r[...], lb_r[...])
        dy = dy_r[...]
        d_o[:, :D] = (dy * sp * _gelu_grad(up)).astype(d_o.dtype)
        dsp = dy * u
        dspb, vnb = dsp.astype(MX), vn.astype(MX)

        @pl.when(i == 0)
        def _():
            dws_o[...] = jnp.zeros_like(dws_o)
            bacc[...] = jnp.zeros_like(bacc)

        rows = []
        for c in range(tm // BLK):
            r = slice(c * BLK, (c + 1) * BLK)
            cols = []
            for g in range(8):
                cs = slice(g * BLK, (g + 1) * BLK)
                dws_o[g] += _dot(dspb[r, cs], vnb[r, cs], NT)
                bacc[g] += dsp[r, cs]
                cols.append(_dot(ws[g], dspb[r, cs], TN))
            rows.append(jnp.concatenate(cols, axis=1))
        dvn = jnp.concatenate(rows, axis=0)
        _acc_rows(dlg_o, i == 0, dvn * xhat)
        _acc_rows(dlb_o, i == 0, dvn)
        d_o[:, D:] = (_ln_bwd(dvn, xhat, rstd, lg_r[...]) * _gelu_grad(vp)).astype(d_o.dtype)

        @pl.when(i == last)
        def _():
            row = lax.broadcasted_iota(jnp.int32, (BLK, BLK), 0)
            col = lax.broadcasted_iota(jnp.int32, (BLK, BLK), 1)
            ones = jnp.ones((8, BLK), MX)
            for g in range(8):
                dws_o[g] = jnp.where(col <= row, dws_o[g], 0.0)
                a = bacc[g]
                hi = a.astype(MX)
                lo = (a - hi.astype(f32)).astype(MX)
                dbs_o[g:g + 1, :] = (_dot(ones, hi, NT) + _dot(ones, lo, NT))[0:1, :]

    full = lambda shape: pl.BlockSpec(shape, lambda i: (0,) * len(shape))
    return pl.pallas_call(
        body, name="gmlp_bwd", grid=(T // tm,),
        in_specs=[pl.BlockSpec((tm, D), lambda i: (i, 0)), *_uv_specs(), full((8, BLK, BLK)), full((8, BLK, BLK)),
                  full((1, D)), full((1, D))],
        out_specs=[pl.BlockSpec((tm, 2 * D), lambda i: (i, 0)), full((8, BLK, BLK)), full((8, BLK)), full((1, D)), full((1, D))],
        out_shape=[jax.ShapeDtypeStruct((T, 2 * D), MX), jax.ShapeDtypeStruct((8, BLK, BLK), f32),
                   jax.ShapeDtypeStruct((8, BLK), f32), jax.ShapeDtypeStruct((1, D), f32), jax.ShapeDtypeStruct((1, D), f32)],
        scratch_shapes=[pltpu.VMEM((8, BLK, BLK), f32)],
        compiler_params=_cp(("arbitrary",)),
    )(dyc, proj, proj, proj, proj, wst, bsx, lg, lb)


PART_TILES = (6, 6, 3, 3, 3, 4)
PART_START = (0, 6, 12, 15, 18, 21)
TJ = 512


def _part_specs(tm, rows_axis):
    specs = []
    for n, s in zip(PART_TILES, PART_START):
        def imap(*idx, n=n, s=s):
            i, j = idx[rows_axis], idx[1 - rows_axis]
            inside = (j >= s) & (j < s + n)
            return (jnp.where(inside, i, 0), jnp.clip(j - s, 0, n - 1))
        specs.append(pl.BlockSpec((tm, TJ), imap))
    return specs


def _with_part(j, refs, fn):
    for r, n, s in zip(refs, PART_TILES, PART_START):
        @pl.when((j >= s) & (j < s + n))
        def _():
            fn(r[...])


def dx_in(dr1, parts, w):
    T = dr1.shape[0]
    tm = min(1024, T)

    def body(dr_r, p0, p1, p2, p3, p4, p5, w_r, o_r):
        j = pl.program_id(1)

        @pl.when(j == 0)
        def _():
            o_r[...] = ALPHA * dr_r[...]

        def acc(tile):
            o_r[...] += _dot(tile, w_r[...], NT)
        _with_part(j, (p0, p1, p2, p3, p4, p5), acc)

    return pl.pallas_call(
        body, name="dx_in", grid=(T // tm, NIN // TJ),
        in_specs=[pl.BlockSpec((tm, D), lambda i, j: (i, 0))] + _part_specs(tm, 0) + [pl.BlockSpec((D, TJ), lambda i, j: (0, j))],
        out_specs=pl.BlockSpec((tm, D), lambda i, j: (i, 0)),
        out_shape=jax.ShapeDtypeStruct((T, D), f32),
        compiler_params=_cp(("parallel", "arbitrary"), 56),
    )(dr1, *parts, w)


def dw_in(x0t, parts):
    T = x0t.shape[1]
    tk = min(2048, T)

    def body(x_r, p0, p1, p2, p3, p4, p5, o_r):
        j, t = pl.program_id(0), pl.program_id(1)

        @pl.when(t == 0)
        def _():
            o_r[...] = jnp.zeros_like(o_r)

        def acc(tile):
            o_r[...] += _dot(x_r[...], tile)
        _with_part(j, (p0, p1, p2, p3, p4, p5), acc)

    return pl.pallas_call(
        body, name="dw_in", grid=(NIN // TJ, T // tk),
        in_specs=[pl.BlockSpec((D, tk), lambda j, t: (0, t))] + _part_specs(tk, 1),
        out_specs=pl.BlockSpec((D, TJ), lambda j, t: (0, j)),
        out_shape=jax.ShapeDtypeStruct((D, NIN), f32),
        compiler_params=_cp(("parallel", "arbitrary")),
    )(x0t, *parts)


def rope_tables(positions):
    half = HD // 2
    inv_freq = ROPE_THETA ** (-jnp.arange(half, dtype=f32) / half)
    ang = positions.astype(f32)[:, None] * inv_freq
    cos, sin = jnp.cos(ang), jnp.sin(ang)
    return jnp.tile(cos, (1, LANES // half)), jnp.tile(jnp.concatenate([-sin, sin], axis=1), (1, LANES // HD))


def _flat(a):
    return a.reshape(a.shape[0] * a.shape[1], a.shape[2])


def layer_fwd(x0, W, cos_t, sin_t):
    T = x0.shape[0]
    proj = mm_in(x0, W["w_in"])
    ya, yc = mix_ac_fwd(proj, W["conv_w"], W["wst"], W["bsx"], W["gmlp_ln_g"], W["gmlp_ln_b"])
    folded, os_, lses = [], [], []
    for g, (_, d) in enumerate(GROUPS):
        qf, kf, vf = fold_rope(proj, cos_t, sin_t, g, d)
        o, lse = attn_fwd(_flat(qf), _flat(kf), _flat(vf), g, T // d // BLK)
        folded.append((qf, kf, vf))
        os_.append(o.reshape(d, T // d, AO))
        lses.append(lse.reshape(d, T // d, AO))
    yb = combine_fwd(os_, lses)
    mabc, m, r1, x1 = mix_out_fwd(proj, ya, yb, yc, x0, W["p_a"], W["p_b"], W["p_c"], W["w_o"], W["ln1_g"], W["ln1_b"])
    gate, up, hh = ffn_up_fwd(x1, W["w_gate"], W["w_up"])
    r2, x2 = ffn_down_fwd(hh, W["w_down"], x1, W["ln2_g"], W["ln2_b"])
    saved = dict(x0=x0, proj=proj, ya=ya, yb=yb, yc=yc, folded=folded, os=os_, lses=lses, mabc=mabc, m=m, r1=r1,
                 x1=x1, gate=gate, up=up, hh=hh, r2=r2)
    return x2, saved


def layer_bwd(dx2, S, W, cos_t, sin_t, on_grads=None):
    T = dx2.shape[0]
    tk = min(2048, T)
    G = {}
    dr2, dgate, dup, G["ln2_g"], G["ln2_b"] = ffn_down_bwd(dx2, S["r2"], W["ln2_g"], W["w_down"], S["gate"], S["up"])
    blk_a = pl.BlockSpec((1, tk, FB), lambda k, t: (k, t, 0))
    row_b = pl.BlockSpec((tk, D), lambda k, t: (t, 0))
    G["w_down"] = tn_matmul("dw_down", S["hh"], dr2, blk_a, row_b, (NCHIP, FB, D),
                            pl.BlockSpec((1, FB, D), lambda k, t: (k, 0, 0)), (NCHIP, T // tk))
    x1t = transpose_cast(S["x1"])
    for nm, dv in (("w_gate", dgate), ("w_up", dup)):
        G[nm] = tn_matmul("d" + nm, x1t, dv, pl.BlockSpec((D, tk), lambda k, t: (0, t)), blk_a, (NCHIP, D, FB),
                          pl.BlockSpec((1, D, FB), lambda k, t: (k, 0, 0)), (NCHIP, T // tk), a_is_t=True)
    dr1, G["ln1_g"], G["ln1_b"] = ffn_up_bwd(dr2, dgate, dup, W["w_gate"], W["w_up"], S["r1"], W["ln1_g"])
    dmabc, dgates, dya, dyb, dyc = mix_out_bwd(dr1, S["proj"], S["mabc"], W["w_o"], W["p_a"], W["p_b"], W["p_c"])
    one = (1, T // tk)
    full_o = pl.BlockSpec((D, D), lambda k, t: (0, 0))
    G["w_o"] = tn_matmul("dw_o", S["m"], dr1, row_b, row_b, (D, D), full_o, one)
    G["p_a"] = tn_matmul("dp_a", S["ya"], dmabc, row_b, pl.BlockSpec((tk, D), lambda k, t: (t, 0)), (D, D), full_o, one)
    G["p_c"] = tn_matmul("dp_c", S["yc"], dmabc, row_b, pl.BlockSpec((tk, D), lambda k, t: (t, 2)), (D, D), full_o, one)
    G["p_b"] = tn_matmul("dp_b", S["yb"], dmabc, pl.BlockSpec((tk, AO), lambda k, t: (t, 0)),
                         pl.BlockSpec((tk, D // NCHIP), lambda k, t: (t, NCHIP + k)), (NCHIP, AO, D // NCHIP),
                         pl.BlockSpec((1, AO, D // NCHIP), lambda k, t: (k, 0, 0)), (NCHIP, T // tk))
    conv_w = W["conv_w"]
    if on_grads is not None:
        conv_w = conv_w + on_grads({n: G[n] for n in BIG if n != "w_in"})
    dbch, G["conv_w"] = conv_bwd(dya, S["proj"], conv_w)
    duv, G["w_s"], G["b_s"], G["gmlp_ln_g"], G["gmlp_ln_b"] = gmlp_bwd(
        dyc, S["proj"], W["wst"], W["bsx"], W["gmlp_ln_g"], W["gmlp_ln_b"])
    ones = _head_ones()
    if on_grads is not None:
        small = {n: G[n] for n in VECS + ("b_s", "w_s", "conv_w")}
        ones = ones + on_grads(small).astype(MX)
    pre = attn_pre_bwd(dyb, S["os"], S["lses"], ones)
    dqkv = []
    for g, (_, d) in enumerate(GROUPS):
        qf, kf, vf = S["folded"][g]
        dqf, dkf, dvf = attn_bwd(_flat(qf), _flat(kf), _flat(vf), _flat(pre[g]), _flat(S["lses"][g]), _flat(pre[3 + g]),
                                 g, T // d // BLK)
        shp = (d, T // d, AO)
        dqkv.append(unfold_rope_bwd(dqf.reshape(shp), dkf.reshape(shp), dvf.reshape(shp), cos_t, sin_t, g, d))
    parts = (dgates, dbch, *dqkv, duv)
    G["w_in"] = dw_in(transpose_cast(S["x0"]), parts)
    dx0 = dx_in(dr1, parts, W["w_in"])
    started = on_grads({"w_in": G["w_in"]}) if on_grads is not None else None
    return dx0, G, started


def prep_layer_weights(Wl):
    W = dict(Wl)
    tril = jnp.tril(jnp.ones((BLK, BLK), f32))
    W["wst"] = (Wl["w_s"] * tril[None]).astype(MX)
    W["bsx"] = jnp.broadcast_to(Wl["b_s"][:, :, None], (8, BLK, BLK))
    for n in ("gmlp_ln_g", "gmlp_ln_b", "ln1_g", "ln1_b", "ln2_g", "ln2_b"):
        W[n] = Wl[n].reshape(1, D)
    return W


def local_step(x, positions, target, layers, on_grads=None):
    cos_t, sin_t = rope_tables(positions)
    Ws, saved = [], []
    h = x
    for Wl in layers:
        Ws.append(prep_layer_weights(Wl(h) if callable(Wl) else Wl))
        h, S = layer_fwd(h, Ws[-1], cos_t, sin_t)
        saved.append(S)
    lsum, dh = loss_grad(h, target)
    if on_grads is not None:
        on_grads(len(Ws), {"loss": lsum})
    grads = [None] * len(Ws)
    started = None
    for l in reversed(range(len(Ws))):
        W = Ws[l]
        if started is not None:
            W = dict(W, ln2_g=W["ln2_g"] + started)
        hook = functools.partial(on_grads, l) if on_grads is not None else None
        dh, grads[l], started = layer_bwd(dh, saved[l], W, cos_t, sin_t, hook)
    return lsum, dh, grads


MESH = pl.DeviceIdType.MESH
ANY = pl.BlockSpec(memory_space=pl.ANY)
BIG = ("w_in", "w_gate", "w_up", "w_down", "p_a", "p_b", "p_c", "w_o")
NBIG = len(BIG)


def _place():
    x, y, c = lax.axis_index("x"), lax.axis_index("y"), lax.axis_index("c")
    return x, y, c, 2 * x + y


def _rcopy(src, dst, send, recv, dev):
    return pltpu.make_async_remote_copy(src_ref=src, dst_ref=dst, send_sem=send, recv_sem=recv, device_id=dev,
                                        device_id_type=MESH)


def _cols(ref, k, width):
    start = k * width if isinstance(k, int) else pl.multiple_of(k * width, LANES)
    return ref.at[:, pl.ds(start, width)]


CHUNK_BYTES = 1 << 20


def _pieces(shape, itemsize, nbytes=CHUNK_BYTES):
    rows, cols = shape[-2], shape[-1]
    per = max(16, nbytes // (cols * itemsize) // 16 * 16)
    out = []
    for lead in (range(shape[0]) if len(shape) == 3 else (None,)):
        for r in range(0, rows, per):
            sl = (pl.ds(r, min(per, rows - r)), slice(None))
            out.append(sl if lead is None else (lead,) + sl)
    return out


def _start_pieces(src, dst, make, nbytes=CHUNK_BYTES):
    for idx in _pieces(src.shape, jnp.dtype(src.dtype).itemsize, nbytes):
        make(src.at[idx], dst.at[idx]).start()


def gather_halves(shards):
    n = len(shards)

    def body(*refs):
        srcs, dsts = refs[:n], refs[n:2 * n]
        send, recv, own_send, own_recv = refs[2 * n:]
        x, y, c, k = _place()
        sib = (x, y, 1 - c)
        chips = [(1 - x, y), (x, 1 - y), (1 - x, 1 - y)]

        def slot(a, layer, pos):
            if a == 0:
                return _cols(dsts[0].at[layer], pos, WIN_SHARD)
            return dsts[a].at[pos, layer]

        def ici(a, j, src, dst):
            return _rcopy(src, dst, send.at[a, j], recv.at[a, j], (*chips[j], c))

        def d2d(a, j, src, dst):
            return _rcopy(src, dst, send.at[a, 3 + j], recv.at[a, 3 + j], sib)

        def own(a, layer, src, dst):
            return _rcopy(src, dst, own_send.at[a, layer], own_recv.at[a, layer], sib)

        for a in range(n):
            for j in range(3):
                _start_pieces(srcs[a].at[c], slot(a, c, k), functools.partial(ici, a, j))
        for a in range(n):
            for layer in range(DEPTH):
                _start_pieces(srcs[a].at[layer], slot(a, layer, k), functools.partial(own, a, layer))
        for a in range(n):
            for j, (cx, cy) in enumerate(chips):
                landed = slot(a, c, 2 * cx + cy)
                ici(a, j, landed, landed).wait_recv()
                _start_pieces(landed, landed, functools.partial(d2d, a, j))
        for a in range(n):
            for j, (cx, cy) in enumerate(chips):
                passed = slot(a, 1 - c, 2 * cx + cy)
                d2d(a, j, passed, passed).wait_recv()
                landed = slot(a, c, 2 * cx + cy)
                d2d(a, j, landed, landed).wait_send()
                ici(a, j, srcs[a].at[c], slot(a, c, k)).wait_send()
            for layer in range(DEPTH):
                own(a, layer, srcs[a].at[layer], slot(a, layer, k)).wait()

    outs = [jax.ShapeDtypeStruct((2, shards[0].shape[1], NIN), shards[0].dtype)]
    outs += [jax.ShapeDtypeStruct((NCHIP,) + s.shape, s.dtype) for s in shards[1:]]
    return pl.pallas_call(
        body, name="gather_halves", in_specs=[ANY] * n, out_specs=[ANY] * n, out_shape=outs,
        scratch_shapes=[pltpu.SemaphoreType.DMA((n, 6)), pltpu.SemaphoreType.DMA((n, 6)),
                        pltpu.SemaphoreType.DMA((n, DEPTH)), pltpu.SemaphoreType.DMA((n, DEPTH))],
    )(*shards)


def _gather_slot(dst, pos):
    return _cols(dst, pos, WIN_SHARD) if len(dst.shape) == 2 else dst.at[pos]


def _gather_copy(a, j, src, dst, send, recv, dev):
    return _rcopy(src, dst, send.at[a * NCHIP + j], recv.at[a * NCHIP + j], dev)


def gather_start(tag, shards, after):
    n = len(shards)

    def body(*refs):
        srcs, dsts = refs[:n], refs[n:2 * n]
        send, recv = refs[2 * n + len(after)], refs[2 * n + len(after) + 1]
        token = refs[-1]
        x, y, c, k = _place()
        peers = [(1 - x, y, c), (x, 1 - y, c), (1 - x, 1 - y, c), (x, y, 1 - c)]
        for a in range(n):
            for j, dev in enumerate(peers):
                _start_pieces(srcs[a], _gather_slot(dsts[a], k),
                              lambda s, d, a=a, j=j, dev=dev: _gather_copy(a, j, s, d, send, recv, dev))
        token[...] = jnp.zeros_like(token)

    gathered = [lax.empty((D, NIN) if i == 0 else (NCHIP,) + s.shape, s.dtype) for i, s in enumerate(shards)]
    ops = [pltpu.with_memory_space_constraint(v, pltpu.HBM) for v in list(shards) + gathered]
    sem = pltpu.SemaphoreType.DMA((n * NCHIP,))
    res = pl.pallas_call(
        body, name=f"gather_start{tag}", in_specs=[HBM] * (2 * n) + [ANY] * len(after),
        out_specs=[SEMS, SEMS] + [HBM] * (2 * n) + [pl.BlockSpec(memory_space=pltpu.VMEM)],
        out_shape=[sem, sem] + [pltpu.HBM(v.shape, v.dtype) for v in ops] + [jax.ShapeDtypeStruct((8, LANES), f32)],
        input_output_aliases={i: 2 + i for i in range(2 * n)},
        compiler_params=pltpu.CompilerParams(has_side_effects=EFFECT),
    )(*ops, *after)
    return res[0], res[1], res[2:2 + n], res[2 + n:2 + 2 * n], res[-1]


def gather_wait(tag, send, recv, shards, gathered, after):
    n = len(shards)

    def body(*refs):
        srcs, dsts = refs[:n], refs[n:2 * n]
        send_r, recv_r = refs[2 * n], refs[2 * n + 1]
        x, y, c, k = _place()
        peers = [(1 - x, y, c), (x, 1 - y, c), (1 - x, 1 - y, c), (x, y, 1 - c)]
        for a in range(n):
            for j, dev in enumerate(peers):
                _gather_copy(a, j, srcs[a], _gather_slot(dsts[a], k), send_r, recv_r, dev).wait_send()
                pos = 2 * dev[0] + dev[1]
                _gather_copy(a, j, srcs[a], _gather_slot(dsts[a], pos), send_r, recv_r, dev).wait_recv()

    ops = list(shards) + list(gathered)
    res = pl.pallas_call(
        body, name=f"gather_wait{tag}", in_specs=[HBM] * (2 * n) + [SEMS, SEMS] + [ANY] * len(after),
        out_specs=[HBM] * (2 * n), out_shape=[pltpu.HBM(v.shape, v.dtype) for v in ops],
        input_output_aliases={i: i for i in range(2 * n)},
        compiler_params=pltpu.CompilerParams(has_side_effects=EFFECT),
    )(*ops, send, recv, *after)
    return res[n:]


def _half(ref, h):
    rows = ref.shape[-2] // 2
    start = pl.multiple_of(h * rows, 16)
    if len(ref.shape) == 2:
        return ref.at[pl.ds(start, rows), :]
    return ref.at[:, pl.ds(start, rows), :]


def rs_pair(tag, grads):
    n = len(grads)

    def body(*refs):
        g, theirs = refs[:n], refs[n:2 * n]
        send, recv = refs[2 * n:]
        x, y, c, _ = _place()

        def give(a, s, d):
            return _rcopy(s, d, send.at[a], recv.at[a], (x, y, 1 - c))

        for a in range(n):
            _start_pieces(_half(g[a], 1 - c), theirs[a], functools.partial(give, a))
        for a in range(n):
            give(a, _half(g[a], 1 - c), theirs[a]).wait()

    def hshape(s):
        return s[:-2] + (s[-2] // 2, s[-1])

    outs = [jax.ShapeDtypeStruct(hshape(g.shape), g.dtype) for g in grads]
    return pl.pallas_call(
        body, name=f"rs_pair{tag}", in_specs=[ANY] * n, out_specs=[ANY] * n, out_shape=outs,
        scratch_shapes=[pltpu.SemaphoreType.DMA((n,))] * 2,
    )(*grads)


HBM = pl.BlockSpec(memory_space=pltpu.HBM)
SEMS = pl.BlockSpec(memory_space=pltpu.SEMAPHORE)
EFFECT = pltpu.SideEffectType.DATAFLOW_SIDE_EFFECTING


def _chip_piece(ref, k):
    return _cols(ref, k, WIN_SHARD) if len(ref.shape) == 2 else ref.at[k]


def _chip_copy(a, k, src, dst, send, recv, me, c):
    return _rcopy(src, dst, send.at[a * NCHIP + k], recv.at[a * NCHIP + me], (k // 2, k % 2, c))


def rs_chips_start(tag, sums):
    n = len(sums)

    def pshape(s):
        return (NCHIP, s[0], WIN_SHARD) if len(s) == 2 else s

    def body(*refs):
        s, land = refs[:n], refs[n:2 * n]
        send, recv = refs[2 * n], refs[2 * n + 1]
        token = refs[-1]
        x, y, c, me = _place()
        for k in range(NCHIP):
            @pl.when(me != k)
            def _():
                for a in range(n):
                    _start_pieces(_chip_piece(s[a], k), land[a].at[me],
                                  lambda src, dst, a=a: _chip_copy(a, k, src, dst, send, recv, me, c))
        token[...] = jnp.zeros_like(token)

    lands = [lax.empty(pshape(v.shape), v.dtype) for v in sums]
    ops = [pltpu.with_memory_space_constraint(v, pltpu.HBM) for v in list(sums) + lands]
    sem = pltpu.SemaphoreType.DMA((n * NCHIP,))
    res = pl.pallas_call(
        body, name=f"rs_chips_start{tag}", in_specs=[HBM] * (2 * n),
        out_specs=[SEMS, SEMS] + [HBM] * (2 * n) + [pl.BlockSpec(memory_space=pltpu.VMEM)],
        out_shape=[sem, sem] + [pltpu.HBM(v.shape, v.dtype) for v in ops] + [jax.ShapeDtypeStruct((8, LANES), f32)],
        input_output_aliases={i: 2 + i for i in range(2 * n)},
        compiler_params=pltpu.CompilerParams(has_side_effects=EFFECT),
    )(*ops)
    return res[0], res[1], res[2:2 + n], res[2 + n:2 + 2 * n], res[-1]


def rs_chips_wait(tag, send, recv, sums, lands, after):
    n = len(sums)

    def body(*refs):
        s, land = refs[:n], refs[n:2 * n]
        send_r, recv_r = refs[2 * n], refs[2 * n + 1]
        x, y, c, me = _place()
        for k in range(NCHIP):
            @pl.when(me != k)
            def _():
                for a in range(n):
                    piece = _chip_piece(s[a], k)
                    _chip_copy(a, k, piece, land[a].at[me], send_r, recv_r, me, c).wait_send()
                    _rcopy(piece, land[a].at[k], send_r.at[a * NCHIP + k], recv_r.at[a * NCHIP + k],
                           (k // 2, k % 2, c)).wait_recv()

    ops = list(sums) + list(lands)
    res = pl.pallas_call(
        body, name=f"rs_chips_wait{tag}", in_specs=[HBM] * (2 * n) + [SEMS, SEMS] + [ANY] * len(after),
        out_specs=[HBM] * (2 * n), out_shape=[pltpu.HBM(v.shape, v.dtype) for v in ops],
        input_output_aliases={i: i for i in range(2 * n)},
        compiler_params=pltpu.CompilerParams(has_side_effects=EFFECT),
    )(*ops, send, recv, *after)
    return res[:n], res[n:]


def rs_join(tag, halves):
    n = len(halves)

    def body(*refs):
        h, other = refs[:n], refs[n:2 * n]
        send, recv = refs[2 * n:]
        x, y, c, _ = _place()

        def give(a, s, d):
            return _rcopy(s, d, send.at[a], recv.at[a], (x, y, 1 - c))

        for a in range(n):
            _start_pieces(h[a], other[a], functools.partial(give, a))
        for a in range(n):
            give(a, h[a], other[a]).wait()

    outs = [jax.ShapeDtypeStruct(v.shape, v.dtype) for v in halves]
    return pl.pallas_call(
        body, name=f"rs_join{tag}", in_specs=[ANY] * n, out_specs=[ANY] * n, out_shape=outs,
        scratch_shapes=[pltpu.SemaphoreType.DMA((n,))] * 2,
    )(*halves)


def _row_tile(rows, cols, itemsize=4, target=2 << 20):
    best = 8
    for t in range(8, rows + 1, 8):
        if rows % t == 0 and t * cols * itemsize <= target:
            best = t
    return best


GRAD_WIRE = jnp.bfloat16


def add_n(name, terms, out_dtype=f32):
    shape = terms[0].shape
    cols = shape[-1]
    rows = math.prod(shape[:-1])
    tr = _row_tile(rows, cols)

    def body(*refs):
        acc = refs[0][...]
        for r in refs[1:-1]:
            acc = acc + r[...]
        refs[-1][...] = acc.astype(out_dtype)

    tile = pl.BlockSpec((tr, cols), lambda i: (i, 0))
    out = pl.pallas_call(
        body, name=name, grid=(rows // tr,), in_specs=[tile] * len(terms), out_specs=tile,
        out_shape=jax.ShapeDtypeStruct((rows, cols), out_dtype), compiler_params=_cp(("parallel",)),
    )(*[t.reshape(rows, cols) for t in terms])
    return out.reshape(shape)


def add_chips(name, land, own):
    _, rows, cols = land.shape
    tr = _row_tile(rows, cols, target=1 << 20)

    def body(land_r, own_r, o_r):
        me = 2 * lax.axis_index("x") + lax.axis_index("y")
        for k in range(NCHIP):
            @pl.when(me == k)
            def _():
                acc = None
                for j in range(NCHIP):
                    t = (own_r[...] if j == k else land_r[j]).astype(f32)
                    acc = t if acc is None else acc + t
                o_r[...] = acc

    tile = pl.BlockSpec((tr, cols), lambda i: (i, 0))
    return pl.pallas_call(
        body, name=name, grid=(rows // tr,), in_specs=[pl.BlockSpec((NCHIP, tr, cols), lambda i: (0, i, 0)), tile],
        out_specs=tile, out_shape=jax.ShapeDtypeStruct((rows, cols), f32), compiler_params=_cp(("parallel",)),
    )(land, own)


def reduce_scatter_begin(tag, G):
    c = lax.axis_index("c")
    names = tuple(G)
    grads = [G[n] if G[n].ndim == 3 or n == "w_in" else G[n].reshape(NCHIP, D // NCHIP, D) for n in names]
    theirs = rs_pair(tag, grads)
    sums = []
    for n, g, t in zip(names, grads, theirs):
        rows = g.shape[-2] // 2
        mine = lax.dynamic_slice_in_dim(g, c * rows, rows, axis=g.ndim - 2)
        sums.append(add_n(f"rs_add_pair{tag}_{n}", [mine, t], GRAD_WIRE))
    send, recv, sums, lands, token = rs_chips_start(tag, sums)
    return (tag, names, send, recv, sums, lands), token[0, 0]


def reduce_scatter_finish(state, after):
    me = 2 * lax.axis_index("x") + lax.axis_index("y")
    tag, names, send, recv, sums, lands = state
    sums, landed = rs_chips_wait(tag, send, recv, sums, lands, after)
    halves = []
    for n, s, v in zip(names, sums, landed):
        own = lax.dynamic_slice_in_dim(s, me * WIN_SHARD, WIN_SHARD, axis=1) if s.ndim == 2 else \
            lax.dynamic_index_in_dim(s, me, 0, keepdims=False)
        halves.append(add_chips(f"rs_add_chips{tag}_{n}", v, own))
    return dict(zip(names, zip(halves, rs_join(tag, halves))))


NDEV = 8


def _small_copy(r, src, dst, send, recv, x, y, c):
    return _rcopy(src, dst, send.at[r - 1], recv.at[r - 1], (x ^ (r >> 2), y ^ ((r >> 1) & 1), c ^ (r & 1)))


def small_start(pack):
    def body(p, land, send, recv, p_thru, land_thru, token):
        x, y, c, _ = _place()
        me = 4 * x + 2 * y + c
        for r in range(1, NDEV):
            _start_pieces(p, land.at[me], lambda s, d, r=r: _small_copy(r, s, d, send, recv, x, y, c), 128 << 10)
        token[...] = jnp.zeros_like(token)

    ops = [pltpu.with_memory_space_constraint(v, pltpu.HBM) for v in (pack, lax.empty((NDEV,) + pack.shape, f32))]
    sem = pltpu.SemaphoreType.DMA((NDEV - 1,))
    return pl.pallas_call(
        body, name="small_start", in_specs=[HBM, HBM],
        out_specs=[SEMS, SEMS, HBM, HBM, pl.BlockSpec(memory_space=pltpu.VMEM)],
        out_shape=[sem, sem] + [pltpu.HBM(v.shape, v.dtype) for v in ops] + [jax.ShapeDtypeStruct((8, LANES), f32)],
        input_output_aliases={0: 2, 1: 3}, compiler_params=pltpu.CompilerParams(has_side_effects=EFFECT),
    )(*ops)


def small_wait(send, recv, pack, land, after):
    def body(p, land_r, send_r, recv_r, *rest):
        x, y, c, _ = _place()
        me = 4 * x + 2 * y + c
        for r in range(1, NDEV):
            _small_copy(r, p, land_r.at[me], send_r, recv_r, x, y, c).wait_send()
            src = 4 * (x ^ (r >> 2)) + 2 * (y ^ ((r >> 1) & 1)) + (c ^ (r & 1))
            _small_copy(r, p, land_r.at[src], send_r, recv_r, x, y, c).wait_recv()

    return pl.pallas_call(
        body, name="small_wait", in_specs=[HBM, HBM, SEMS, SEMS] + [ANY] * len(after), out_specs=[HBM, HBM],
        out_shape=[pltpu.HBM(pack.shape, f32), pltpu.HBM(land.shape, f32)], input_output_aliases={0: 0, 1: 1},
        compiler_params=pltpu.CompilerParams(has_side_effects=EFFECT),
    )(pack, land, send, recv, *after)


def small_sum(land, pack):
    def body(land_r, p_r, o_r):
        me = 4 * lax.axis_index("x") + 2 * lax.axis_index("y") + lax.axis_index("c")
        for k in range(NDEV):
            @pl.when(me == k)
            def _():
                acc = None
                for d in range(NDEV):
                    t = p_r[...] if d == k else land_r[d]
                    acc = t if acc is None else acc + t
                o_r[...] = acc

    vm = pl.BlockSpec(memory_space=pltpu.VMEM)
    return pl.pallas_call(
        body, name="small_sum", in_specs=[vm, vm], out_specs=vm, out_shape=jax.ShapeDtypeStruct(pack.shape, f32),
        compiler_params=pltpu.CompilerParams(vmem_limit_bytes=40 << 20),
    )(land, pack)


def _adamw_math(w, g, m, v):
    m = ADAM_B1 * m + (1.0 - ADAM_B1) * g
    v = ADAM_B2 * v + (1.0 - ADAM_B2) * (g * g)
    m_hat = m / (1.0 - ADAM_B1 ** ADAM_STEP)
    v_hat = v / (1.0 - ADAM_B2 ** ADAM_STEP)
    return -ADAM_LR * (m_hat / (jnp.sqrt(v_hat) + ADAM_EPS) + ADAM_WD * w), m, v


def adamw_big(name, halves, w, m, v):
    _, R, C = w.shape
    tr = _row_tile(R // 2, C, target=1 << 20)
    nt = R // 2 // tr

    def body(a0, b0, a1, b1, w_r, m_r, v_r, g_o, d_o, m_o, v_o):
        mine = pl.program_id(1) == lax.axis_index("c")
        g = jnp.where(pl.program_id(0) == 0, jnp.where(mine, a0[...], b0[...]), jnp.where(mine, a1[...], b1[...]))
        g_o[...] = g
        d_o[...], m_o[...], v_o[...] = _adamw_math(w_r[...], g, m_r[...], v_r[...])

    stk = pl.BlockSpec((None, tr, C), lambda l, h, i: (l, h * nt + i, 0))
    lay0 = pl.BlockSpec((tr, C), lambda l, h, i: (jnp.where(l == 0, i, nt - 1), 0))
    lay1 = pl.BlockSpec((tr, C), lambda l, h, i: (jnp.where(l == 0, 0, i), 0))
    return pl.pallas_call(
        body, name=name, grid=(DEPTH, 2, nt),
        in_specs=[lay0, lay0, lay1, lay1, stk, stk, stk],
        out_specs=[stk] * 4, out_shape=[jax.ShapeDtypeStruct(w.shape, f32)] * 4,
        compiler_params=_cp(("arbitrary", "arbitrary", "arbitrary")),
    )(*halves[0], *halves[1], w, m, v)


def adamw_small(name, g, w, m, v):
    def body(g_r, w_r, m_r, v_r, d_o, m_o, v_o):
        d_o[...], m_o[...], v_o[...] = _adamw_math(w_r[...], g_r[...], m_r[...], v_r[...])

    return pl.pallas_call(body, name=name, out_shape=[jax.ShapeDtypeStruct(w.shape, f32)] * 3)(g, w, m, v)


WEIGHTS = ("w_in", "conv_w", "gmlp_ln_g", "gmlp_ln_b", "w_s", "b_s", "p_a", "p_b", "p_c", "w_o", "ln1_g", "ln1_b",
           "w_gate", "w_up", "w_down", "ln2_g", "ln2_b")
VECS = ("ln1_g", "ln1_b", "ln2_g", "ln2_b", "gmlp_ln_g", "gmlp_ln_b")
ROWS_VEC, ROWS_BS, ROWS_WS, ROWS_CONV = D // LANES, 8, 8 * BLK, 3 * D // LANES
ROWS_LAYER = len(VECS) * ROWS_VEC + ROWS_BS + ROWS_WS + ROWS_CONV


def _pack_small(per_layer, tail):
    parts = []
    for P in per_layer:
        parts += [P[n].reshape(ROWS_VEC, LANES) for n in VECS]
        parts += [P["b_s"].reshape(ROWS_BS, LANES), P["w_s"].reshape(ROWS_WS, LANES), P["conv_w"].reshape(ROWS_CONV, LANES)]
    return jnp.concatenate(parts + [tail], axis=0)


def _unpack_small(pack):
    out = []
    for l in range(DEPTH):
        r = l * ROWS_LAYER
        P = {}
        for n in VECS:
            P[n] = pack[r:r + ROWS_VEC].reshape(D)
            r += ROWS_VEC
        P["b_s"] = pack[r:r + ROWS_BS].reshape(8, BLK)
        r += ROWS_BS
        P["w_s"] = pack[r:r + ROWS_WS].reshape(8, BLK, BLK)
        r += ROWS_WS
        P["conv_w"] = pack[r:r + ROWS_CONV].reshape(3, D)
        out.append(P)
    return out, pack[DEPTH * ROWS_LAYER:]


def kernel(x, positions, w_in, conv_w, gmlp_ln_g, gmlp_ln_b, w_s, b_s, p_a, p_b, p_c, w_o, ln1_g, ln1_b, w_gate, w_up, w_down, ln2_g, ln2_b, loss_target, m_w_in, m_conv_w, m_gmlp_ln_g, m_gmlp_ln_b, m_w_s, m_b_s, m_p_a, m_p_b, m_p_c, m_w_o, m_ln1_g, m_ln1_b, m_w_gate, m_w_up, m_w_down, m_ln2_g, m_ln2_b, v_w_in, v_conv_w, v_gmlp_ln_g, v_gmlp_ln_b, v_w_s, v_b_s, v_p_a, v_p_b, v_p_c, v_w_o, v_ln1_g, v_ln1_b, v_w_gate, v_w_up, v_w_down, v_ln2_g, v_ln2_b):
    Wt = dict(w_in=w_in, conv_w=conv_w, gmlp_ln_g=gmlp_ln_g, gmlp_ln_b=gmlp_ln_b, w_s=w_s, b_s=b_s, p_a=p_a, p_b=p_b,
              p_c=p_c, w_o=w_o, ln1_g=ln1_g, ln1_b=ln1_b, w_gate=w_gate, w_up=w_up, w_down=w_down, ln2_g=ln2_g, ln2_b=ln2_b)
    Mt = dict(w_in=m_w_in, conv_w=m_conv_w, gmlp_ln_g=m_gmlp_ln_g, gmlp_ln_b=m_gmlp_ln_b, w_s=m_w_s, b_s=m_b_s, p_a=m_p_a,
              p_b=m_p_b, p_c=m_p_c, w_o=m_w_o, ln1_g=m_ln1_g, ln1_b=m_ln1_b, w_gate=m_w_gate, w_up=m_w_up,
              w_down=m_w_down, ln2_g=m_ln2_g, ln2_b=m_ln2_b)
    Vt = dict(w_in=v_w_in, conv_w=v_conv_w, gmlp_ln_g=v_gmlp_ln_g, gmlp_ln_b=v_gmlp_ln_b, w_s=v_w_s, b_s=v_b_s, p_a=v_p_a,
              p_b=v_p_b, p_c=v_p_c, w_o=v_w_o, ln1_g=v_ln1_g, ln1_b=v_ln1_b, w_gate=v_w_gate, w_up=v_w_up,
              w_down=v_w_down, ln2_g=v_ln2_g, ln2_b=v_ln2_b)
    chip = 2 * lax.axis_index("x") + lax.axis_index("y")
    cw = D // NCHIP

    def layer_weights(l, gathered, conv_all):
        Wl = dict(zip(BIG, gathered))
        for n in ("p_a", "p_c", "w_o"):
            Wl[n] = Wl[n].reshape(D, D)
        Wl["conv_w"] = conv_all[:, l].transpose(1, 0, 2).reshape(3, D)
        for n in VECS + ("w_s", "b_s"):
            Wl[n] = Wt[n][l]
        return Wl

    halves = [Wt[n][0].astype(MX).reshape(2, Wt[n].shape[1] // 2, Wt[n].shape[2]) for n in BIG]
    got = gather_halves(halves + [conv_w])
    conv_all = got[NBIG]
    g0 = [got[0].reshape(D, NIN)] + [a.reshape(NCHIP, 2 * a.shape[2], a.shape[3]) for a in got[1:NBIG]]
    send1, recv1, sh1, g1, coming = gather_start("1", [Wt[n][1].astype(MX) for n in BIG], [conv_all])
    W0 = layer_weights(0, g0, conv_all)
    W0["gmlp_ln_g"] = W0["gmlp_ln_g"] + coming[0, 0]

    def W1(h):
        return layer_weights(1, gather_wait("1", send1, recv1, sh1, g1, [h]), conv_all)

    layers = [W0, W1]

    rs_state, rs_started, held = {}, {}, {}

    def start_exchange(l, g):
        if "loss" in g or "conv_w" in g:
            held[l] = g
            if l > 0:
                return jnp.zeros((), f32)
            pack = _pack_small([held[j] for j in range(DEPTH)], held[DEPTH]["loss"])
            *held["small"], token = small_start(pack)
            return token[0, 0]
        key = (l, "w_in" in g)
        rs_state[key], rs_started[key] = reduce_scatter_begin(f"{l}{'b' if key[1] else 'a'}", g)
        return rs_started[key]

    _, grad_x, _ = local_step(x[0], positions[0], loss_target[0], layers, start_exchange)

    last = jnp.zeros((8, LANES), f32) + rs_started[(0, True)]
    behind = [grad_x, last]
    red = [dict() for _ in range(DEPTH)]
    for key in ((1, False), (1, True), (0, False)):
        red[key[0]].update(reduce_scatter_finish(rs_state[key], behind))
    small, tail = _unpack_small(small_sum(*reversed(small_wait(*held["small"], behind))))
    loss = tail[0, 0]

    G, DW, NM, NV = {}, {}, {}, {}
    zc = jnp.zeros((3, D), f32)
    wp = _pack_small([{**{n: Wt[n][l] for n in VECS + ("b_s", "w_s")}, "conv_w": zc} for l in range(DEPTH)], jnp.zeros((8, LANES), f32))
    mp = _pack_small([{**{n: Mt[n][l] for n in VECS + ("b_s", "w_s")}, "conv_w": zc} for l in range(DEPTH)], jnp.zeros((8, LANES), f32))
    vp = _pack_small([{**{n: Vt[n][l] for n in VECS + ("b_s", "w_s")}, "conv_w": zc} for l in range(DEPTH)], jnp.ones((8, LANES), f32))
    gp = _pack_small(small, jnp.zeros((8, LANES), f32))
    outs = [_unpack_small(a)[0] for a in adamw_small("adamw_small", gp, wp, mp, vp)]
    for n in VECS + ("b_s", "w_s"):
        G[n] = jnp.stack([small[l][n] for l in range(DEPTH)])
        DW[n], NM[n], NV[n] = (jnp.stack([o[l][n] for l in range(DEPTH)]) for o in outs)
    gconv = jnp.stack([lax.dynamic_slice(small[l]["conv_w"], (0, chip * cw), (3, cw)) for l in range(DEPTH)])
    G["conv_w"] = gconv
    flat = lambda a: a.reshape(DEPTH * 3, cw)
    d, m2, v2 = adamw_small("adamw_conv", flat(gconv), flat(conv_w), flat(m_conv_w), flat(v_conv_w))
    DW["conv_w"], NM["conv_w"], NV["conv_w"] = (a.reshape(DEPTH, 3, cw) for a in (d, m2, v2))

    for n in BIG[1:]:
        G[n], DW[n], NM[n], NV[n] = adamw_big("adamw_" + n, (red[0][n], red[1][n]), Wt[n], Mt[n], Vt[n])
    done = [d, DW["ln2_b"], red[1]["w_in"][1]] + [DW[n] for n in BIG[1:]]
    red[0].update(reduce_scatter_finish(rs_state[(0, True)], done))
    G["w_in"], DW["w_in"], NM["w_in"], NV["w_in"] = adamw_big(
        "adamw_w_in", (red[0]["w_in"], red[1]["w_in"]), Wt["w_in"], Mt["w_in"], Vt["w_in"])

    return (loss, grad_x[None], *[G[n] for n in WEIGHTS], *[DW[n] for n in WEIGHTS], *[NM[n] for n in WEIGHTS],
            *[NV[n] for n in WEIGHTS])
```

```python
import functools
import math

import jax
import jax.numpy as jnp
from jax import lax
from jax.experimental import pallas as pl
from jax.experimental.pallas import tpu as pltpu

D = 1024
NIN = 12800
DFF = 2816
NCHIP = 4
FB = DFF // NCHIP
WIN_SHARD = NIN // NCHIP
DEPTH = 2
GROUPS = ((128, 1), (512, 4), (2048, 16))
HD = 64
BLK = 128
AO = 512
ALPHA = (2 * DEPTH) ** 0.25
EPS = 1e-5
ROPE_THETA = 10000.0
LANES = 128
NEG = -1e30

C_GATES, C_BCH, C_QKV, C_UV = 0, 3 * D, 6 * D, 6 * D + 9 * AO

MX = jnp.bfloat16
ACT = jnp.bfloat16

ADAM_LR, ADAM_B1, ADAM_B2, ADAM_EPS, ADAM_WD, ADAM_STEP = 0.001, 0.9, 0.999, 1e-08, 0.01, 10

f32 = jnp.float32
NT = (((1,), (1,)), ((), ()))
TN = (((0,), (0,)), ((), ()))


def _cp(sem, vmem_mb=48):
    return pltpu.CompilerParams(dimension_semantics=sem, vmem_limit_bytes=vmem_mb << 20)


def _dot(a, b, dims=None):
    if dims is None:
        return jnp.dot(a, b, preferred_element_type=f32)
    return lax.dot_general(a, b, dims, preferred_element_type=f32)


def _ln_stats(r):
    mu = jnp.mean(r, axis=-1, keepdims=True)
    xc = r - mu
    var = jnp.mean(xc * xc, axis=-1, keepdims=True)
    rstd = lax.rsqrt(var + EPS)
    return xc * rstd, rstd


def _ln_bwd(dy, xhat, rstd, g):
    dxh = dy * g
    return rstd * (dxh - jnp.mean(dxh, axis=-1, keepdims=True) - xhat * jnp.mean(dxh * xhat, axis=-1, keepdims=True))


def _gelu(x):
    return 0.5 * x * (1.0 + lax.erf(x * (1.0 / math.sqrt(2.0))))


def _gelu_grad(x):
    return 0.5 * (1.0 + lax.erf(x * (1.0 / math.sqrt(2.0)))) + x * jnp.exp(-0.5 * x * x) * (1.0 / math.sqrt(2.0 * math.pi))


def _sigmoid(x):
    return 0.5 * jnp.tanh(0.5 * x) + 0.5


def _acc_rows(o_ref, first, val):
    @pl.when(first)
    def _():
        o_ref[...] = jnp.zeros_like(o_ref)
    o_ref[...] += jnp.sum(val, axis=0, keepdims=True)


def mm_in(x, w):
    T = x.shape[0]
    tm, tn = min(1024, T), 1280

    def body(x_ref, w_ref, o_ref, xb):
        @pl.when(pl.program_id(1) == 0)
        def _():
            xb[...] = x_ref[...].astype(MX)
        o_ref[...] = _dot(xb[...], w_ref[...]).astype(o_ref.dtype)

    return pl.pallas_call(
        body, name="mm_in", grid=(T // tm, NIN // tn),
        in_specs=[pl.BlockSpec((tm, D), lambda i, j: (i, 0)), pl.BlockSpec((D, tn), lambda i, j: (0, j))],
        out_specs=pl.BlockSpec((tm, tn), lambda i, j: (i, j)),
        out_shape=jax.ShapeDtypeStruct((T, NIN), ACT),
        scratch_shapes=[pltpu.VMEM((tm, D), MX)],
        compiler_params=_cp(("parallel", "arbitrary")),
    )(x, w)


HALO = 16
TM_AC = 256


def _uv_specs():
    return [pl.BlockSpec((TM_AC, 512), functools.partial(lambda i, j: (i, j), j=C_UV // 512 + j)) for j in range(4)]


def _gmlp_fwd(up, vp, ws_ref, bs_ref, lg, lb):
    u = _gelu(up)
    xhat, rstd = _ln_stats(_gelu(vp))
    vn = xhat * lg + lb
    vnb = vn.astype(MX)
    rows = []
    for c in range(up.shape[0] // BLK):
        r = slice(c * BLK, (c + 1) * BLK)
        rows.append(jnp.concatenate(
            [_dot(ws_ref[g], vnb[r, g * BLK:(g + 1) * BLK]) + bs_ref[g] for g in range(8)], axis=1))
    return u, vn, xhat, rstd, jnp.concatenate(rows, axis=0)


def mix_ac_fwd(proj, conv_w, wst, bsx, lg, lb):
    T = proj.shape[0]
    tm = TM_AC

    def body(bch, halo, u0, u1, v0, v1, cw, ws, bs, lg_ref, lb_ref, ya, yc, zs):
        i = pl.program_id(0)
        pb = bch[...].astype(f32)
        z = pb[:, D:2 * D] * pb[:, 2 * D:]
        hz = halo[:, :D].astype(f32) * halo[:, D:].astype(f32)
        zs[0:HALO, :] = jnp.where(i > 0, hz, 0.0)
        zs[HALO:HALO + tm, :] = z
        cv = cw[0:1, :] * zs[HALO - 2:HALO - 2 + tm, :] + cw[1:2, :] * zs[HALO - 1:HALO - 1 + tm, :] + cw[2:3, :] * z
        ya[...] = (pb[:, :D] * cv).astype(ya.dtype)
        up = jnp.concatenate([u0[...], u1[...]], axis=1).astype(f32)
        vp = jnp.concatenate([v0[...], v1[...]], axis=1).astype(f32)
        u, _, _, _, sp = _gmlp_fwd(up, vp, ws, bs, lg_ref[...], lb_ref[...])
        yc[...] = (u * sp).astype(yc.dtype)

    full = lambda shape: pl.BlockSpec(shape, lambda i: (0,) * len(shape))
    return pl.pallas_call(
        body, name="mix_ac_fwd", grid=(T // tm,),
        in_specs=[pl.BlockSpec((tm, 3 * D), lambda i: (i, 1)),
                  pl.BlockSpec((HALO, 2 * D), lambda i: (jnp.maximum(i * (tm // HALO) - 1, 0), 2)),
                  *_uv_specs(), full((3, D)), full((8, BLK, BLK)), full((8, BLK, BLK)), full((1, D)), full((1, D))],
        out_specs=[pl.BlockSpec((tm, D), lambda i: (i, 0))] * 2,
        out_shape=[jax.ShapeDtypeStruct((T, D), MX)] * 2,
        scratch_shapes=[pltpu.VMEM((HALO + tm, D), f32)],
        compiler_params=_cp(("parallel",)),
    )(proj, proj, proj, proj, proj, proj, conv_w, wst, bsx, lg, lb)


def _swap_halves(x):
    lane = lax.broadcasted_iota(jnp.int32, x.shape, 1)
    return jnp.where((lane % HD) < HD // 2, pltpu.roll(x, x.shape[1] - HD // 2, 1), pltpu.roll(x, HD // 2, 1))


def _tile4(t):
    return jnp.concatenate([t] * (AO // LANES), axis=1)


TM_FOLD = 512


def _fold_out(nat, x, out_ref, d):
    if d == 1:
        out_ref[0] = x.astype(out_ref.dtype)
        return
    rows = x.shape[0] // d
    for j in range(AO // LANES):
        nat[j] = x[:, j * LANES:(j + 1) * LANES]
    for r in range(d):
        out_ref[r] = jnp.concatenate(
            [nat.at[j][pl.ds(r, rows, stride=d), :] for j in range(AO // LANES)], axis=1).astype(out_ref.dtype)


def _unfold_in(nat, in_ref, d):
    if d == 1:
        return in_ref[0].astype(f32)
    rows = in_ref.shape[1]
    for r in range(d):
        v = in_ref[r].astype(f32)
        for j in range(AO // LANES):
            nat.at[j][pl.ds(r, rows, stride=d), :] = v[:, j * LANES:(j + 1) * LANES]
    return jnp.concatenate([nat[j] for j in range(AO // LANES)], axis=1)


def fold_rope(proj, cos_t, sin_t, g, d):
    T = proj.shape[0]
    tm = TM_FOLD
    rows = tm // d

    def body(x_ref, c_ref, s_ref, q_o, k_o, v_o, nat):
        cos, sin = _tile4(c_ref[...]), _tile4(s_ref[...])
        for part, out, scale in ((0, q_o, HD ** -0.5), (1, k_o, 1.0), (2, v_o, None)):
            x = x_ref[:, part * AO:(part + 1) * AO].astype(f32)
            if scale is not None:
                x = (x * cos + _swap_halves(x) * sin) * scale
            _fold_out(nat, x, out, d)

    fold_spec = pl.BlockSpec((d, rows, AO), lambda i: (0, i, 0))
    return pl.pallas_call(
        body, name=f"fold_rope{g}", grid=(T // tm,),
        in_specs=[pl.BlockSpec((tm, 3 * AO), lambda i: (i, C_QKV // (3 * AO) + g)),
                  pl.BlockSpec((tm, LANES), lambda i: (i, 0)), pl.BlockSpec((tm, LANES), lambda i: (i, 0))],
        out_specs=[fold_spec] * 3,
        out_shape=[jax.ShapeDtypeStruct((d, T // d, AO), MX)] * 3,
        scratch_shapes=[pltpu.VMEM((AO // LANES, tm, LANES), f32)],
        compiler_params=_cp(("parallel",)),
    )(proj, cos_t, sin_t)


def _stack_heads(x):
    lane = lax.broadcasted_iota(jnp.int32, x.shape, 1)
    z = jnp.zeros_like(x)
    return jnp.concatenate([jnp.where(lane < HD, x, z), jnp.where(lane >= HD, x, z)], axis=0)


def _unstack_heads(y):
    lane = lax.broadcasted_iota(jnp.int32, (BLK, LANES), 1)
    return jnp.where(lane < HD, y[:BLK], y[BLK:])


def _window_masks():
    row = lax.broadcasted_iota(jnp.int32, (2 * BLK, 2 * BLK), 0) % BLK
    col = lax.broadcasted_iota(jnp.int32, (2 * BLK, 2 * BLK), 1)
    return (col < BLK) & (col >= row), (col >= BLK) & (col - BLK <= row)


def _two_blocks(ref, b):
    r0 = pl.multiple_of(b * BLK, BLK)
    rp = pl.multiple_of(jnp.maximum(b - 1, 0) * BLK, BLK)
    return jnp.concatenate([ref[pl.ds(rp, BLK), :], ref[pl.ds(r0, BLK), :]], axis=0)


def attn_fwd(qf, kf, vf, g, nb):
    T = qf.shape[0]

    def body(q_ref, k_ref, v_ref, o_ref, l_ref):
        prev_m, cur_m = _window_masks()

        def step(b, carry):
            r0 = pl.multiple_of(b * BLK, BLK)
            qs = _stack_heads(q_ref[pl.ds(r0, BLK), :])
            s = _dot(qs, _two_blocks(k_ref, b), NT)
            s = jnp.where(cur_m | (prev_m & ((b % nb) != 0)), s, NEG)
            m = jnp.max(s, axis=-1, keepdims=True)
            p = jnp.exp(s - m)
            l = jnp.sum(p, axis=-1, keepdims=True)
            o = _dot(p.astype(MX), _two_blocks(v_ref, b)) / l
            o_ref[pl.ds(r0, BLK), :] = _unstack_heads(o)
            l_ref[pl.ds(r0, BLK), :] = _unstack_heads(jnp.broadcast_to(m + jnp.log(l), (2 * BLK, LANES)))
            return carry

        lax.fori_loop(0, T // BLK, step, 0, unroll=4)

    spec = pl.BlockSpec((T, LANES), lambda j: (0, j))
    return pl.pallas_call(
        body, name=f"attn_fwd{g}", grid=(AO // LANES,),
        in_specs=[spec] * 3, out_specs=[spec] * 2,
        out_shape=[jax.ShapeDtypeStruct((T, AO), f32)] * 2,
        compiler_params=_cp(("parallel",), 56),
    )(qf, kf, vf)


def _group_weights(lses):
    m = jnp.maximum(jnp.maximum(lses[0], lses[1]), lses[2])
    e = [jnp.exp(l - m) for l in lses]
    inv = 1.0 / (e[0] + e[1] + e[2])
    return [x * inv for x in e]


def _fold_specs(T, tm):
    specs = []
    for _, d in GROUPS:
        specs.append(pl.BlockSpec((d, tm // d, AO), lambda i: (0, i, 0)))
    return specs


def combine_fwd(os_, lses):
    T = os_[0].shape[0] * os_[0].shape[1]
    tm = TM_FOLD

    def body(o0, o1, o2, l0, l1, l2, y_ref, nat):
        o = [_unfold_in(nat, r, d) for r, (_, d) in zip((o0, o1, o2), GROUPS)]
        ls = [_unfold_in(nat, r, d) for r, (_, d) in zip((l0, l1, l2), GROUPS)]
        w = _group_weights(ls)
        y_ref[...] = (w[0] * o[0] + w[1] * o[1] + w[2] * o[2]).astype(y_ref.dtype)

    specs = _fold_specs(T, tm)
    return pl.pallas_call(
        body, name="combine_fwd", grid=(T // tm,),
        in_specs=specs + specs, out_specs=pl.BlockSpec((tm, AO), lambda i: (i, 0)),
        out_shape=jax.ShapeDtypeStruct((T, AO), MX),
        scratch_shapes=[pltpu.VMEM((AO // LANES, tm, LANES), f32)],
        compiler_params=_cp(("parallel",)),
    )(*os_, *lses)


TM_MIX = 256


def mix_out_fwd(proj, ya, yb, yc, x0, pa, pb, pc, wo, g1, b1):
    T = x0.shape[0]
    tm = min(TM_MIX, T)

    def body(gt, ya_r, yb_r, yc_r, x0_r, pa_r, pb_r, pc_r, wo_r, g_r, b_r, mabc, m_o, r1_o, x1_o):
        ma = _dot(ya_r[...], pa_r[...])
        ybv = yb_r[...]
        mb = jnp.concatenate([_dot(ybv, pb_r[k]) for k in range(NCHIP)], axis=1)
        mc = _dot(yc_r[...], pc_r[...])
        m = jnp.zeros((tm, D), f32)
        for j, mm in enumerate((ma, mb, mc)):
            mabc[:, j * D:(j + 1) * D] = mm.astype(mabc.dtype)
            m = m + _sigmoid(gt[:, j * D:(j + 1) * D].astype(f32)) * mm
        mb16 = m.astype(MX)
        m_o[...] = mb16
        r1 = ALPHA * x0_r[...] + _dot(mb16, wo_r[...])
        r1_o[...] = r1
        xhat, _ = _ln_stats(r1)
        x1_o[...] = xhat * g_r[...] + b_r[...]

    full = lambda shape: pl.BlockSpec(shape, lambda i: (0,) * len(shape))
    tile = lambda w: pl.BlockSpec((tm, w), lambda i: (i, 0))
    return pl.pallas_call(
        body, name="mix_out_fwd", grid=(T // tm,),
        in_specs=[tile(3 * D), tile(D), tile(AO), tile(D), tile(D), full((D, D)), full((NCHIP, AO, D // NCHIP)),
                  full((D, D)), full((D, D)), full((1, D)), full((1, D))],
        out_specs=[tile(3 * D), tile(D), tile(D), tile(D)],
        out_shape=[jax.ShapeDtypeStruct((T, 3 * D), MX), jax.ShapeDtypeStruct((T, D), MX),
                   jax.ShapeDtypeStruct((T, D), f32), jax.ShapeDtypeStruct((T, D), f32)],
        compiler_params=_cp(("parallel",), 56),
    )(proj, ya, yb, yc, x0, pa, pb, pc, wo, g1, b1)


TM_FF = 512
ROW_CHUNK = 64


def ffn_up_fwd(x1, wg, wu):
    T = x1.shape[0]
    tm = min(TM_FF, T)

    def body(x_r, wg_r, wu_r, g_o, u_o, h_o, xb):
        @pl.when(pl.program_id(1) == 0)
        def _():
            xb[...] = x_r[...].astype(MX)
        gate = _dot(xb[...], wg_r[0])
        up = _dot(xb[...], wu_r[0])
        g_o[0] = gate.astype(g_o.dtype)
        u_o[0] = up.astype(u_o.dtype)
        h_o[0] = (gate * _sigmoid(gate) * up).astype(h_o.dtype)

    wspec = pl.BlockSpec((1, D, FB), lambda i, k: (k, 0, 0))
    ospec = pl.BlockSpec((1, tm, FB), lambda i, k: (k, i, 0))
    return pl.pallas_call(
        body, name="ffn_up_fwd", grid=(T // tm, NCHIP),
        in_specs=[pl.BlockSpec((tm, D), lambda i, k: (i, 0)), wspec, wspec],
        out_specs=[ospec] * 3,
        out_shape=[jax.ShapeDtypeStruct((NCHIP, T, FB), ACT)] * 2 + [jax.ShapeDtypeStruct((NCHIP, T, FB), MX)],
        scratch_shapes=[pltpu.VMEM((tm, D), MX)],
        compiler_params=_cp(("parallel", "arbitrary")),
    )(x1, wg, wu)


def ffn_down_fwd(hh, wd, x1, g2, b2):
    T = x1.shape[0]
    tm = min(TM_FF, T)

    def body(h_r, w_r, x_r, g_r, b_r, r2_o, x2_o):
        r2 = ALPHA * x_r[...]
        for k in range(NCHIP):
            r2 = r2 + _dot(h_r[k], w_r[k])
        r2_o[...] = r2
        xhat, _ = _ln_stats(r2)
        x2_o[...] = xhat * g_r[...] + b_r[...]

    tile = pl.BlockSpec((tm, D), lambda i: (i, 0))
    vec = pl.BlockSpec((1, D), lambda i: (0, 0))
    return pl.pallas_call(
        body, name="ffn_down_fwd", grid=(T // tm,),
        in_specs=[pl.BlockSpec((NCHIP, tm, FB), lambda i: (0, i, 0)), pl.BlockSpec((NCHIP, FB, D), lambda i: (0, 0, 0)),
                  tile, vec, vec],
        out_specs=[tile, tile], out_shape=[jax.ShapeDtypeStruct((T, D), f32)] * 2,
        compiler_params=_cp(("parallel",)),
    )(hh, wd, x1, g2, b2)


def loss_grad(y, tgt):
    T = y.shape[0]
    tm = min(512, T)

    def body(y_r, t_r, l_o, dy_o):
        e = y_r[...] - t_r[...]
        dy_o[...] = e * (1.0 / D)

        @pl.when(pl.program_id(0) == 0)
        def _():
            l_o[...] = jnp.zeros_like(l_o)
        l_o[...] += (0.5 / D) * jnp.sum(e * e)

    tile = pl.BlockSpec((tm, D), lambda i: (i, 0))
    return pl.pallas_call(
        body, name="loss_grad", grid=(T // tm,),
        in_specs=[tile, tile], out_specs=[pl.BlockSpec((8, LANES), lambda i: (0, 0)), tile],
        out_shape=[jax.ShapeDtypeStruct((8, LANES), f32), jax.ShapeDtypeStruct((T, D), f32)],
        compiler_params=_cp(("arbitrary",)),
    )(y, tgt)


def ffn_down_bwd(dx2, r2, g2, wd, gate, up):
    T = dx2.shape[0]
    tm = min(TM_FF, T)

    def body(dx_r, r_r, g_r, w_r, ga_r, up_r, dr_o, dg_o, du_o, dlg_o, dlb_o, drb, hs):
        i, k = pl.program_id(0), pl.program_id(1)

        @pl.when(k == 0)
        def _():
            xhat, rstd = _ln_stats(r_r[...])
            dx = dx_r[...]
            _acc_rows(dlg_o, i == 0, dx * xhat)
            _acc_rows(dlb_o, i == 0, dx)
            dr = _ln_bwd(dx, xhat, rstd, g_r[...])
            dr_o[...] = dr
            drb[...] = dr.astype(MX)

        hs[...] = _dot(drb[...], w_r[0], NT)
        for r in range(0, tm, ROW_CHUNK):
            rows = pl.ds(r, ROW_CHUNK)
            dhh, gate_v, up_v = hs[rows, :], ga_r[0, rows, :].astype(f32), up_r[0, rows, :].astype(f32)
            sg = _sigmoid(gate_v)
            dg_o[0, rows, :] = (dhh * up_v * sg * (1.0 + gate_v * (1.0 - sg))).astype(dg_o.dtype)
            du_o[0, rows, :] = (dhh * gate_v * sg).astype(du_o.dtype)

    tile = pl.BlockSpec((tm, D), lambda i, k: (i, 0))
    vec = pl.BlockSpec((1, D), lambda i, k: (0, 0))
    blk = pl.BlockSpec((1, tm, FB), lambda i, k: (k, i, 0))
    return pl.pallas_call(
        body, name="ffn_down_bwd", grid=(T // tm, NCHIP),
        in_specs=[tile, tile, vec, pl.BlockSpec((1, FB, D), lambda i, k: (k, 0, 0)), blk, blk],
        out_specs=[tile, blk, blk, vec, vec],
        out_shape=[jax.ShapeDtypeStruct((T, D), f32)] + [jax.ShapeDtypeStruct((NCHIP, T, FB), MX)] * 2
        + [jax.ShapeDtypeStruct((1, D), f32)] * 2,
        scratch_shapes=[pltpu.VMEM((tm, D), MX), pltpu.VMEM((tm, FB), f32)],
        compiler_params=_cp(("arbitrary", "arbitrary")),
    )(dx2, r2, g2, wd, gate, up)


def ffn_up_bwd(dr2, dgate, dup, wg, wu, r1, g1):
    T = dr2.shape[0]
    tm = min(TM_FF, T)

    def body(dr2_r, dg_r, du_r, wg_r, wu_r, r1_r, g_r, dr1_o, dlg_o, dlb_o, acc):
        i, k = pl.program_id(0), pl.program_id(1)

        @pl.when(k == 0)
        def _():
            acc[...] = ALPHA * dr2_r[...]
        acc[...] += _dot(dg_r[0], wg_r[0], NT) + _dot(du_r[0], wu_r[0], NT)

        @pl.when(k == NCHIP - 1)
        def _():
            dx = acc[...]
            xhat, rstd = _ln_stats(r1_r[...])
            _acc_rows(dlg_o, i == 0, dx * xhat)
            _acc_rows(dlb_o, i == 0, dx)
            dr1_o[...] = _ln_bwd(dx, xhat, rstd, g_r[...])

    tile = pl.BlockSpec((tm, D), lambda i, k: (i, 0))
    vec = pl.BlockSpec((1, D), lambda i, k: (0, 0))
    blk = pl.BlockSpec((1, tm, FB), lambda i, k: (k, i, 0))
    wspec = pl.BlockSpec((1, D, FB), lambda i, k: (k, 0, 0))
    return pl.pallas_call(
        body, name="ffn_up_bwd", grid=(T // tm, NCHIP),
        in_specs=[tile, blk, blk, wspec, wspec, tile, vec],
        out_specs=[tile, vec, vec],
        out_shape=[jax.ShapeDtypeStruct((T, D), f32)] + [jax.ShapeDtypeStruct((1, D), f32)] * 2,
        scratch_shapes=[pltpu.VMEM((tm, D), f32)],
        compiler_params=_cp(("arbitrary", "arbitrary")),
    )(dr2, dgate, dup, wg, wu, r1, g1)


def mix_out_bwd(dr1, proj, mabc, wo, pa, pb, pc):
    T = dr1.shape[0]
    tm = min(TM_MIX, T)

    def body(dr_r, gt, mabc_r, wo_r, pa_r, pb_r, pc_r, dmabc_o, dgt_o, dya_o, dyb_o, dyc_o):
        dm = _dot(dr_r[...].astype(MX), wo_r[...], NT)
        dmx = []
        for j in range(3):
            s = _sigmoid(gt[:, j * D:(j + 1) * D].astype(f32))
            v = (dm * s).astype(MX)
            dmx.append(v)
            dmabc_o[:, j * D:(j + 1) * D] = v
            dgt_o[:, j * D:(j + 1) * D] = (dm * mabc_r[:, j * D:(j + 1) * D].astype(f32) * s * (1.0 - s)).astype(dgt_o.dtype)
        dya_o[...] = _dot(dmx[0], pa_r[...], NT)
        dyb = jnp.zeros((tm, AO), f32)
        for k in range(NCHIP):
            dyb = dyb + _dot(dmx[1][:, k * (D // NCHIP):(k + 1) * (D // NCHIP)], pb_r[k], NT)
        dyb_o[...] = dyb
        dyc_o[...] = _dot(dmx[2], pc_r[...], NT)

    full = lambda shape: pl.BlockSpec(shape, lambda i: (0,) * len(shape))
    tile = lambda w: pl.BlockSpec((tm, w), lambda i: (i, 0))
    return pl.pallas_call(
        body, name="mix_out_bwd", grid=(T // tm,),
        in_specs=[tile(D), tile(3 * D), tile(3 * D), full((D, D)), full((D, D)), full((NCHIP, AO, D // NCHIP)), full((D, D))],
        out_specs=[tile(3 * D), tile(3 * D), tile(D), tile(AO), tile(D)],
        out_shape=[jax.ShapeDtypeStruct((T, 3 * D), MX), jax.ShapeDtypeStruct((T, 3 * D), MX),
                   jax.ShapeDtypeStruct((T, D), f32), jax.ShapeDtypeStruct((T, AO), f32), jax.ShapeDtypeStruct((T, D), f32)],
        compiler_params=_cp(("parallel",), 56),
    )(dr1, proj, mabc, wo, pa, pb, pc)


def transpose_cast(x):
    T = x.shape[0]
    tm = min(512, T)

    def body(x_r, o_r):
        o_r[...] = x_r[...].T.astype(o_r.dtype)

    return pl.pallas_call(
        body, name="transpose_cast", grid=(T // tm,),
        in_specs=[pl.BlockSpec((tm, D), lambda i: (i, 0))], out_specs=pl.BlockSpec((D, tm), lambda i: (0, i)),
        out_shape=jax.ShapeDtypeStruct((D, T), MX), compiler_params=_cp(("parallel",)),
    )(x)


def tn_matmul(name, a, b, a_spec, b_spec, out_shape, out_spec, grid, a_is_t=False):
    nt = len(grid) - 1

    def body(a_r, b_r, o_r):
        @pl.when(pl.program_id(nt) == 0)
        def _():
            o_r[...] = jnp.zeros_like(o_r)
        av = a_r[...].reshape(a_r.shape[-2:]).astype(MX)
        bv = b_r[...].reshape(b_r.shape[-2:]).astype(MX)
        o_r[...] += _dot(av, bv, None if a_is_t else TN).reshape(o_r.shape)

    return pl.pallas_call(
        body, name=name, grid=grid, in_specs=[a_spec, b_spec], out_specs=out_spec,
        out_shape=jax.ShapeDtypeStruct(out_shape, f32),
        compiler_params=_cp(("parallel",) * nt + ("arbitrary",), 56),
    )(a, b)


def attn_pre_bwd(dyb, os_, lses, ones):
    T = dyb.shape[0]
    tm = TM_FOLD

    def body(dy_r, o0, o1, o2, l0, l1, l2, ones_r, d0, d1, d2, f0, f1, f2, nat):
        o = [_unfold_in(nat, r, d) for r, (_, d) in zip((o0, o1, o2), GROUPS)]
        ls = [_unfold_in(nat, r, d) for r, (_, d) in zip((l0, l1, l2), GROUPS)]
        w = _group_weights(ls)
        dy = dy_r[...]
        t = dy * (w[0] * o[0] + w[1] * o[1] + w[2] * o[2])
        hi = t.astype(MX)
        lo = (t - hi.astype(f32)).astype(MX)
        c = _dot(hi, ones_r[...]) + _dot(lo, ones_r[...])
        for wg, do_o, df_o, (_, d) in zip(w, (d0, d1, d2), (f0, f1, f2), GROUPS):
            _fold_out(nat, wg * dy, do_o, d)
            _fold_out(nat, -wg * c, df_o, d)

    specs = _fold_specs(T, tm)
    return pl.pallas_call(
        body, name="attn_pre_bwd", grid=(T // tm,),
        in_specs=[pl.BlockSpec((tm, AO), lambda i: (i, 0))] + specs + specs + [pl.BlockSpec((AO, AO), lambda i: (0, 0))],
        out_specs=specs + specs,
        out_shape=[jax.ShapeDtypeStruct((d, T // d, AO), MX) for _, d in GROUPS]
        + [jax.ShapeDtypeStruct((d, T // d, AO), f32) for _, d in GROUPS],
        scratch_shapes=[pltpu.VMEM((AO // LANES, tm, LANES), f32)],
        compiler_params=_cp(("parallel",)),
    )(dyb, *os_, *lses, ones)


def _head_ones():
    i = jnp.arange(AO) // HD
    return (i[:, None] == i[None, :]).astype(MX)


def attn_bwd(qf, kf, vf, dof, lse, df, g, nb):
    T = qf.shape[0]

    def body(q_ref, k_ref, v_ref, do_ref, l_ref, d_ref, dq_ref, dk_ref, dv_ref):
        prev_m, cur_m = _window_masks()

        def head_col(ref, r0):
            v = ref[pl.ds(r0, BLK), :]
            return jnp.concatenate([v[:, 0:1], v[:, HD:HD + 1]], axis=0)

        def step(b, carry):
            dk_c, dv_c = carry
            r0 = pl.multiple_of(b * BLK, BLK)
            rp = pl.multiple_of(jnp.maximum(b - 1, 0) * BLK, BLK)
            qs, dos = _stack_heads(q_ref[pl.ds(r0, BLK), :]), _stack_heads(do_ref[pl.ds(r0, BLK), :])
            k2, v2 = _two_blocks(k_ref, b), _two_blocks(v_ref, b)
            valid = cur_m | (prev_m & ((b % nb) != 0))
            p = jnp.where(valid, jnp.exp(_dot(qs, k2, NT) - head_col(l_ref, r0)), 0.0)
            ds = (p * (_dot(dos, v2, NT) + head_col(d_ref, r0))).astype(MX)
            dq_ref[pl.ds(r0, BLK), :] = _unstack_heads(_dot(ds, k2)).astype(dq_ref.dtype)
            dk2 = _dot(ds, qs, TN)
            dv2 = _dot(p.astype(MX), dos, TN)
            dk_ref[pl.ds(rp, BLK), :] = (dk_c + dk2[:BLK]).astype(dk_ref.dtype)
            dv_ref[pl.ds(rp, BLK), :] = (dv_c + dv2[:BLK]).astype(dv_ref.dtype)
            return dk2[BLK:], dv2[BLK:]

        zero = jnp.zeros((BLK, LANES), f32)

        def two_steps(i, carry):
            return step(2 * i + 1, step(2 * i, carry))

        dk_c, dv_c = lax.fori_loop(0, T // BLK // 2, two_steps, (zero, zero))
        dk_ref[pl.ds(T - BLK, BLK), :] = dk_c.astype(dk_ref.dtype)
        dv_ref[pl.ds(T - BLK, BLK), :] = dv_c.astype(dv_ref.dtype)

    spec = pl.BlockSpec((T, LANES), lambda j: (0, j))
    return pl.pallas_call(
        body, name=f"attn_bwd{g}", grid=(AO // LANES,),
        in_specs=[spec] * 6, out_specs=[spec] * 3,
        out_shape=[jax.ShapeDtypeStruct((T, AO), MX)] * 3,
        compiler_params=_cp(("parallel",), 60),
    )(qf, kf, vf, dof, lse, df)


def unfold_rope_bwd(dqf, dkf, dvf, cos_t, sin_t, g, d):
    T = dqf.shape[0] * dqf.shape[1]
    tm = TM_FOLD

    def body(q_r, k_r, v_r, c_ref, s_ref, o_ref, nat):
        cos, sin = _tile4(c_ref[...]), _tile4(s_ref[...])
        for part, ref, scale in ((0, q_r, HD ** -0.5), (1, k_r, 1.0), (2, v_r, None)):
            x = _unfold_in(nat, ref, d)
            if scale is not None:
                x = (x * cos - _swap_halves(x) * sin) * scale
            o_ref[:, part * AO:(part + 1) * AO] = x.astype(o_ref.dtype)

    fold_spec = pl.BlockSpec((d, tm // d, AO), lambda i: (0, i, 0))
    tab = pl.BlockSpec((tm, LANES), lambda i: (i, 0))
    return pl.pallas_call(
        body, name=f"unfold_rope_bwd{g}", grid=(T // tm,),
        in_specs=[fold_spec] * 3 + [tab, tab],
        out_specs=pl.BlockSpec((tm, 3 * AO), lambda i: (i, 0)),
        out_shape=jax.ShapeDtypeStruct((T, 3 * AO), MX),
        scratch_shapes=[pltpu.VMEM((AO // LANES, tm, LANES), f32)],
        compiler_params=_cp(("parallel",)),
    )(dqf, dkf, dvf, cos_t, sin_t)


def conv_bwd(dya, proj, conv_w):
    T = dya.shape[0]
    tm = TM_AC
    last = T // tm - 1

    def body(dy_r, bch, hprev, dy_next, b_next, cw, d_o, dw_o, zs, ds):
        i = pl.program_id(0)
        pb = bch[...].astype(f32)
        bp, cp, hp = pb[:, :D], pb[:, D:2 * D], pb[:, 2 * D:]
        z = cp * hp
        hz = hprev[:, :D].astype(f32) * hprev[:, D:].astype(f32)
        zs[0:HALO, :] = jnp.where(i > 0, hz, 0.0)
        zs[HALO:HALO + tm, :] = z
        z2, z1 = zs[HALO - 2:HALO - 2 + tm, :], zs[HALO - 1:HALO - 1 + tm, :]
        cv = cw[0:1, :] * z2 + cw[1:2, :] * z1 + cw[2:3, :] * z
        dy = dy_r[...]
        dcv = dy * bp
        ds[0:tm, :] = dcv
        ds[tm:tm + HALO, :] = jnp.where(i < last, dy_next[...] * b_next[...].astype(f32), 0.0)
        dz = cw[2:3, :] * dcv + cw[1:2, :] * ds[1:1 + tm, :] + cw[0:1, :] * ds[2:2 + tm, :]
        d_o[:, :D] = (dy * cv).astype(d_o.dtype)
        d_o[:, D:2 * D] = (dz * hp).astype(d_o.dtype)
        d_o[:, 2 * D:] = (dz * cp).astype(d_o.dtype)

        @pl.when(i == 0)
        def _():
            dw_o[...] = jnp.zeros_like(dw_o)
        dw_o[0:1, :] += jnp.sum(dcv * z2, axis=0, keepdims=True)
        dw_o[1:2, :] += jnp.sum(dcv * z1, axis=0, keepdims=True)
        dw_o[2:3, :] += jnp.sum(dcv * z, axis=0, keepdims=True)

    nh = tm // HALO
    return pl.pallas_call(
        body, name="conv_bwd", grid=(T // tm,),
        in_specs=[pl.BlockSpec((tm, D), lambda i: (i, 0)), pl.BlockSpec((tm, 3 * D), lambda i: (i, 1)),
                  pl.BlockSpec((HALO, 2 * D), lambda i: (jnp.maximum(i * nh - 1, 0), 2)),
                  pl.BlockSpec((HALO, D), lambda i: (jnp.minimum((i + 1) * nh, T // HALO - 1), 0)),
                  pl.BlockSpec((HALO, D), lambda i: (jnp.minimum((i + 1) * nh, T // HALO - 1), 3)),
                  pl.BlockSpec((3, D), lambda i: (0, 0))],
        out_specs=[pl.BlockSpec((tm, 3 * D), lambda i: (i, 0)), pl.BlockSpec((3, D), lambda i: (0, 0))],
        out_shape=[jax.ShapeDtypeStruct((T, 3 * D), MX), jax.ShapeDtypeStruct((3, D), f32)],
        scratch_shapes=[pltpu.VMEM((HALO + tm, D), f32), pltpu.VMEM((tm + HALO, D), f32)],
        compiler_params=_cp(("arbitrary",)),
    )(dya, proj, proj, dya, proj, conv_w)


def gmlp_bwd(dyc, proj, wst, bsx, lg, lb):
    T = dyc.shape[0]
    tm = TM_AC
    last = T // tm - 1

    def body(dy_r, u0, u1, v0, v1, ws, bs, lg_r, lb_r, d_o, dws_o, dbs_o, dlg_o, dlb_o, bacc):
        i = pl.program_id(0)
        up = jnp.concatenate([u0[...], u1[...]], axis=1).astype(f32)
        vp = jnp.concatenate([v0[...], v1[...]], axis=1).astype(f32)
        u, vn, xhat, rstd, sp = _gmlp_fwd(up, vp, ws, bs, lg_r[...], lb_r[...])
        dy = dy_r[...]
        d_o[:, :D] = (dy * sp * _gelu_grad(up)).astype(d_o.dtype)
        dsp = dy * u
        dspb, vnb = dsp.astype(MX), vn.astype(MX)

        @pl.when(i == 0)
        def _():
            dws_o[...] = jnp.zeros_like(dws_o)
            bacc[...] = jnp.zeros_like(bacc)

        rows = []
        for c in range(tm // BLK):
            r = slice(c * BLK, (c + 1) * BLK)
            cols = []
            for g in range(8):
                cs = slice(g * BLK, (g + 1) * BLK)
                dws_o[g] += _dot(dspb[r, cs], vnb[r, cs], NT)
                bacc[g] += dsp[r, cs]
                cols.append(_dot(ws[g], dspb[r, cs], TN))
            rows.append(jnp.concatenate(cols, axis=1))
        dvn = jnp.concatenate(rows, axis=0)
        _acc_rows(dlg_o, i == 0, dvn * xhat)
        _acc_rows(dlb_o, i == 0, dvn)
        d_o[:, D:] = (_ln_bwd(dvn, xhat, rstd, lg_r[...]) * _gelu_grad(vp)).astype(d_o.dtype)

        @pl.when(i == last)
        def _():
            row = lax.broadcasted_iota(jnp.int32, (BLK, BLK), 0)
            col = lax.broadcasted_iota(jnp.int32, (BLK, BLK), 1)
            ones = jnp.ones((8, BLK), MX)
            for g in range(8):
                dws_o[g] = jnp.where(col <= row, dws_o[g], 0.0)
                a = bacc[g]
                hi = a.astype(MX)
                lo = (a - hi.astype(f32)).astype(MX)
                dbs_o[g:g + 1, :] = (_dot(ones, hi, NT) + _dot(ones, lo, NT))[0:1, :]

    full = lambda shape: pl.BlockSpec(shape, lambda i: (0,) * len(shape))
    return pl.pallas_call(
        body, name="gmlp_bwd", grid=(T // tm,),
        in_specs=[pl.BlockSpec((tm, D), lambda i: (i, 0)), *_uv_specs(), full((8, BLK, BLK)), full((8, BLK, BLK)),
                  full((1, D)), full((1, D))],
        out_specs=[pl.BlockSpec((tm, 2 * D), lambda i: (i, 0)), full((8, BLK, BLK)), full((8, BLK)), full((1, D)), full((1, D))],
        out_shape=[jax.ShapeDtypeStruct((T, 2 * D), MX), jax.ShapeDtypeStruct((8, BLK, BLK), f32),
                   jax.ShapeDtypeStruct((8, BLK), f32), jax.ShapeDtypeStruct((1, D), f32), jax.ShapeDtypeStruct((1, D), f32)],
        scratch_shapes=[pltpu.VMEM((8, BLK, BLK), f32)],
        compiler_params=_cp(("arbitrary",)),
    )(dyc, proj, proj, proj, proj, wst, bsx, lg, lb)


PART_TILES = (6, 6, 3, 3, 3, 4)
PART_START = (0, 6, 12, 15, 18, 21)
TJ = 512


def _part_specs(tm, rows_axis):
    specs = []
    for n, s in zip(PART_TILES, PART_START):
        def imap(*idx, n=n, s=s):
            i, j = idx[rows_axis], idx[1 - rows_axis]
            inside = (j >= s) & (j < s + n)
            return (jnp.where(inside, i, 0), jnp.clip(j - s, 0, n - 1))
        specs.append(pl.BlockSpec((tm, TJ), imap))
    return specs


def _with_part(j, refs, fn):
    for r, n, s in zip(refs, PART_TILES, PART_START):
        @pl.when((j >= s) & (j < s + n))
        def _():
            fn(r[...])


def dx_in(dr1, parts, w, bias):
    T = dr1.shape[0]
    tm = min(1024, T)

    def body(dr_r, p0, p1, p2, p3, p4, p5, w_r, b_r, o_r):
        j = pl.program_id(1)

        @pl.when(j == 0)
        def _():
            o_r[...] = ALPHA * dr_r[...] + b_r[...]

        def acc(tile):
            o_r[...] += _dot(tile, w_r[...], NT)
        _with_part(j, (p0, p1, p2, p3, p4, p5), acc)

    return pl.pallas_call(
        body, name="dx_in", grid=(T // tm, NIN // TJ),
        in_specs=[pl.BlockSpec((tm, D), lambda i, j: (i, 0))] + _part_specs(tm, 0)
        + [pl.BlockSpec((D, TJ), lambda i, j: (0, j)), pl.BlockSpec((1, D), lambda i, j: (0, 0))],
        out_specs=pl.BlockSpec((tm, D), lambda i, j: (i, 0)),
        out_shape=jax.ShapeDtypeStruct((T, D), f32),
        compiler_params=_cp(("parallel", "arbitrary"), 56),
    )(dr1, *parts, w, bias)


def dw_in(x0t, parts):
    T = x0t.shape[1]
    tk = min(2048, T)

    def body(x_r, p0, p1, p2, p3, p4, p5, o_r):
        j, t = pl.program_id(0), pl.program_id(1)

        @pl.when(t == 0)
        def _():
            o_r[...] = jnp.zeros_like(o_r)

        def acc(tile):
            o_r[...] += _dot(x_r[...], tile)
        _with_part(j, (p0, p1, p2, p3, p4, p5), acc)

    return pl.pallas_call(
        body, name="dw_in", grid=(NIN // TJ, T // tk),
        in_specs=[pl.BlockSpec((D, tk), lambda j, t: (0, t))] + _part_specs(tk, 1),
        out_specs=pl.BlockSpec((D, TJ), lambda j, t: (0, j)),
        out_shape=jax.ShapeDtypeStruct((D, NIN), f32),
        compiler_params=_cp(("parallel", "arbitrary")),
    )(x0t, *parts)


def rope_tables(positions):
    half = HD // 2
    inv_freq = ROPE_THETA ** (-jnp.arange(half, dtype=f32) / half)
    ang = positions.astype(f32)[:, None] * inv_freq
    cos, sin = jnp.cos(ang), jnp.sin(ang)
    return jnp.tile(cos, (1, LANES // half)), jnp.tile(jnp.concatenate([-sin, sin], axis=1), (1, LANES // HD))


def _flat(a):
    return a.reshape(a.shape[0] * a.shape[1], a.shape[2])


def layer_fwd(x0, W, cos_t, sin_t):
    T = x0.shape[0]
    proj = mm_in(x0, W["w_in"])
    ya, yc = mix_ac_fwd(proj, W["conv_w"], W["wst"], W["bsx"], W["gmlp_ln_g"], W["gmlp_ln_b"])
    folded, os_, lses = [], [], []
    for g, (_, d) in enumerate(GROUPS):
        qf, kf, vf = fold_rope(proj, cos_t, sin_t, g, d)
        o, lse = attn_fwd(_flat(qf), _flat(kf), _flat(vf), g, T // d // BLK)
        folded.append((qf, kf, vf))
        os_.append(o.reshape(d, T // d, AO))
        lses.append(lse.reshape(d, T // d, AO))
    yb = combine_fwd(os_, lses)
    mabc, m, r1, x1 = mix_out_fwd(proj, ya, yb, yc, x0, W["p_a"], W["p_b"], W["p_c"], W["w_o"], W["ln1_g"], W["ln1_b"])
    gate, up, hh = ffn_up_fwd(x1, W["w_gate"], W["w_up"])
    r2, x2 = ffn_down_fwd(hh, W["w_down"], x1, W["ln2_g"], W["ln2_b"])
    saved = dict(x0=x0, proj=proj, ya=ya, yb=yb, yc=yc, folded=folded, os=os_, lses=lses, mabc=mabc, m=m, r1=r1,
                 x1=x1, gate=gate, up=up, hh=hh, r2=r2)
    return x2, saved


def layer_bwd(dx2, S, W, cos_t, sin_t, on_grads=None):
    T = dx2.shape[0]
    tk = min(2048, T)
    G = {}
    dr2, dgate, dup, G["ln2_g"], G["ln2_b"] = ffn_down_bwd(dx2, S["r2"], W["ln2_g"], W["w_down"], S["gate"], S["up"])
    blk_a = pl.BlockSpec((1, tk, FB), lambda k, t: (k, t, 0))
    row_b = pl.BlockSpec((tk, D), lambda k, t: (t, 0))
    G["w_down"] = tn_matmul("dw_down", S["hh"], dr2, blk_a, row_b, (NCHIP, FB, D),
                            pl.BlockSpec((1, FB, D), lambda k, t: (k, 0, 0)), (NCHIP, T // tk))
    for nm, dv in (("w_gate", dgate), ("w_up", dup)):
        G[nm] = tn_matmul("d" + nm, dv, S["x1"], blk_a, row_b, (NCHIP, FB, D),
                          pl.BlockSpec((1, FB, D), lambda k, t: (k, 0, 0)), (NCHIP, T // tk))
    dr1, G["ln1_g"], G["ln1_b"] = ffn_up_bwd(dr2, dgate, dup, W["w_gate"], W["w_up"], S["r1"], W["ln1_g"])
    dmabc, dgates, dya, dyb, dyc = mix_out_bwd(dr1, S["proj"], S["mabc"], W["w_o"], W["p_a"], W["p_b"], W["p_c"])
    one = (1, T // tk)
    full_o = pl.BlockSpec((D, D), lambda k, t: (0, 0))
    G["w_o"] = tn_matmul("dw_o", S["m"], dr1, row_b, row_b, (D, D), full_o, one)
    G["p_a"] = tn_matmul("dp_a", S["ya"], dmabc, row_b, pl.BlockSpec((tk, D), lambda k, t: (t, 0)), (D, D), full_o, one)
    G["p_c"] = tn_matmul("dp_c", S["yc"], dmabc, row_b, pl.BlockSpec((tk, D), lambda k, t: (t, 2)), (D, D), full_o, one)
    G["p_b"] = tn_matmul("dp_b", S["yb"], dmabc, pl.BlockSpec((tk, AO), lambda k, t: (t, 0)),
                         pl.BlockSpec((tk, D // NCHIP), lambda k, t: (t, NCHIP + k)), (NCHIP, AO, D // NCHIP),
                         pl.BlockSpec((1, AO, D // NCHIP), lambda k, t: (k, 0, 0)), (NCHIP, T // tk))
    conv_w = W["conv_w"]
    if on_grads is not None:
        conv_w = conv_w + on_grads({n: G[n] for n in BIG if n != "w_in"})
    dbch, G["conv_w"] = conv_bwd(dya, S["proj"], conv_w)
    duv, G["w_s"], G["b_s"], G["gmlp_ln_g"], G["gmlp_ln_b"] = gmlp_bwd(
        dyc, S["proj"], W["wst"], W["bsx"], W["gmlp_ln_g"], W["gmlp_ln_b"])
    ones = _head_ones()
    if on_grads is not None:
        small = {n: G[n] for n in VECS + ("b_s", "w_s", "conv_w")}
        ones = ones + on_grads(small).astype(MX)
    pre = attn_pre_bwd(dyb, S["os"], S["lses"], ones)
    dqkv = []
    for g, (_, d) in enumerate(GROUPS):
        qf, kf, vf = S["folded"][g]
        dqf, dkf, dvf = attn_bwd(_flat(qf), _flat(kf), _flat(vf), _flat(pre[g]), _flat(S["lses"][g]), _flat(pre[3 + g]),
                                 g, T // d // BLK)
        shp = (d, T // d, AO)
        dqkv.append(unfold_rope_bwd(dqf.reshape(shp), dkf.reshape(shp), dvf.reshape(shp), cos_t, sin_t, g, d))
    parts = (dgates, dbch, *dqkv, duv)
    G["w_in"] = dw_in(transpose_cast(S["x0"]), parts)
    bias = jnp.zeros((1, D), f32)
    if on_grads is not None:
        bias = bias + on_grads({"w_in": G["w_in"]})
    dx0 = dx_in(dr1, parts, W["w_in"], bias)
    started = on_grads({"dx": dx0}) if on_grads is not None else None
    return dx0, G, started


def prep_layer_weights(Wl):
    W = dict(Wl)
    tril = jnp.tril(jnp.ones((BLK, BLK), f32))
    W["wst"] = (Wl["w_s"] * tril[None]).astype(MX)
    W["bsx"] = jnp.broadcast_to(Wl["b_s"][:, :, None], (8, BLK, BLK))
    for n in ("gmlp_ln_g", "gmlp_ln_b", "ln1_g", "ln1_b", "ln2_g", "ln2_b"):
        W[n] = Wl[n].reshape(1, D)
    return W


def local_step(x, positions, target, layers, on_grads=None):
    cos_t, sin_t = rope_tables(positions)
    Ws, saved = [], []
    h = x
    for Wl in layers:
        Ws.append(prep_layer_weights(Wl(h) if callable(Wl) else Wl))
        h, S = layer_fwd(h, Ws[-1], cos_t, sin_t)
        saved.append(S)
    lsum, dh = loss_grad(h, target)
    if on_grads is not None:
        on_grads(len(Ws), {"loss": lsum})
    grads = [None] * len(Ws)
    started = None
    for l in reversed(range(len(Ws))):
        W = Ws[l]
        if started is not None:
            W = dict(W, ln2_g=W["ln2_g"] + started)
        hook = functools.partial(on_grads, l) if on_grads is not None else None
        dh, grads[l], started = layer_bwd(dh, saved[l], W, cos_t, sin_t, hook)
    return lsum, dh, grads


MESH = pl.DeviceIdType.MESH
ANY = pl.BlockSpec(memory_space=pl.ANY)
BIG = ("w_in", "w_gate", "w_up", "w_down", "p_a", "p_b", "p_c", "w_o")
NBIG = len(BIG)


def _place():
    x, y, c = lax.axis_index("x"), lax.axis_index("y"), lax.axis_index("c")
    return x, y, c, 2 * x + y


def _rcopy(src, dst, send, recv, dev):
    return pltpu.make_async_remote_copy(src_ref=src, dst_ref=dst, send_sem=send, recv_sem=recv, device_id=dev,
                                        device_id_type=MESH)


def _cols(ref, k, width):
    start = k * width if isinstance(k, int) else pl.multiple_of(k * width, LANES)
    return ref.at[:, pl.ds(start, width)]


CHUNK_BYTES = 1 << 20


def _pieces(shape, itemsize, nbytes=CHUNK_BYTES):
    rows, cols = shape[-2], shape[-1]
    per = max(16, nbytes // (cols * itemsize) // 16 * 16)
    out = []
    for lead in (range(shape[0]) if len(shape) == 3 else (None,)):
        for r in range(0, rows, per):
            sl = (pl.ds(r, min(per, rows - r)), slice(None))
            out.append(sl if lead is None else (lead,) + sl)
    return out


def _start_pieces(src, dst, make, nbytes=CHUNK_BYTES):
    for idx in _pieces(src.shape, jnp.dtype(src.dtype).itemsize, nbytes):
        make(src.at[idx], dst.at[idx]).start()


def gather_halves(shards):
    n = len(shards)

    def body(*refs):
        srcs, dsts = refs[:n], refs[n:2 * n]
        send, recv, own_send, own_recv = refs[2 * n:]
        x, y, c, k = _place()
        sib = (x, y, 1 - c)
        chips = [(1 - x, y), (x, 1 - y), (1 - x, 1 - y)]

        def slot(a, layer, pos):
            if a == 0:
                return _cols(dsts[0].at[layer], pos, WIN_SHARD)
            return dsts[a].at[pos, layer]

        def ici(a, j, src, dst):
            return _rcopy(src, dst, send.at[a, j], recv.at[a, j], (*chips[j], c))

        def d2d(a, j, src, dst):
            return _rcopy(src, dst, send.at[a, 3 + j], recv.at[a, 3 + j], sib)

        def own(a, layer, src, dst):
            return _rcopy(src, dst, own_send.at[a, layer], own_recv.at[a, layer], sib)

        for a in range(n):
            for j in range(3):
                _start_pieces(srcs[a].at[c], slot(a, c, k), functools.partial(ici, a, j))
        for a in range(n):
            for layer in range(DEPTH):
                _start_pieces(srcs[a].at[layer], slot(a, layer, k), functools.partial(own, a, layer))
        for a in range(n):
            for j, (cx, cy) in enumerate(chips):
                landed = slot(a, c, 2 * cx + cy)
                ici(a, j, landed, landed).wait_recv()
                _start_pieces(landed, landed, functools.partial(d2d, a, j))
        for a in range(n):
            for j, (cx, cy) in enumerate(chips):
                passed = slot(a, 1 - c, 2 * cx + cy)
                d2d(a, j, passed, passed).wait_recv()
                landed = slot(a, c, 2 * cx + cy)
                d2d(a, j, landed, landed).wait_send()
                ici(a, j, srcs[a].at[c], slot(a, c, k)).wait_send()
            for layer in range(DEPTH):
                own(a, layer, srcs[a].at[layer], slot(a, layer, k)).wait()

    outs = [jax.ShapeDtypeStruct((2, shards[0].shape[1], NIN), shards[0].dtype)]
    outs += [jax.ShapeDtypeStruct((NCHIP,) + s.shape, s.dtype) for s in shards[1:]]
    return pl.pallas_call(
        body, name="gather_halves", in_specs=[ANY] * n, out_specs=[ANY] * n, out_shape=outs,
        scratch_shapes=[pltpu.SemaphoreType.DMA((n, 6)), pltpu.SemaphoreType.DMA((n, 6)),
                        pltpu.SemaphoreType.DMA((n, DEPTH)), pltpu.SemaphoreType.DMA((n, DEPTH))],
    )(*shards)


def _gather_slot(dst, pos):
    return _cols(dst, pos, WIN_SHARD) if len(dst.shape) == 2 else dst.at[pos]


def _gather_copy(a, j, src, dst, send, recv, dev):
    return _rcopy(src, dst, send.at[a * NCHIP + j], recv.at[a * NCHIP + j], dev)


def gather_start(tag, shards, after):
    n = len(shards)

    def body(*refs):
        srcs, dsts = refs[:n], refs[n:2 * n]
        send, recv = refs[2 * n + len(after)], refs[2 * n + len(after) + 1]
        token = refs[-1]
        x, y, c, k = _place()
        peers = [(1 - x, y, c), (x, 1 - y, c), (1 - x, 1 - y, c), (x, y, 1 - c)]
        for a in range(n):
            for j, dev in enumerate(peers):
                _start_pieces(srcs[a], _gather_slot(dsts[a], k),
                              lambda s, d, a=a, j=j, dev=dev: _gather_copy(a, j, s, d, send, recv, dev))
        token[...] = jnp.zeros_like(token)

    gathered = [lax.empty((D, NIN) if i == 0 else (NCHIP,) + s.shape, s.dtype) for i, s in enumerate(shards)]
    ops = [pltpu.with_memory_space_constraint(v, pltpu.HBM) for v in list(shards) + gathered]
    sem = pltpu.SemaphoreType.DMA((n * NCHIP,))
    res = pl.pallas_call(
        body, name=f"gather_start{tag}", in_specs=[HBM] * (2 * n) + [ANY] * len(after),
        out_specs=[SEMS, SEMS] + [HBM] * (2 * n) + [pl.BlockSpec(memory_space=pltpu.VMEM)],
        out_shape=[sem, sem] + [pltpu.HBM(v.shape, v.dtype) for v in ops] + [jax.ShapeDtypeStruct((8, LANES), f32)],
        input_output_aliases={i: 2 + i for i in range(2 * n)},
        compiler_params=pltpu.CompilerParams(has_side_effects=EFFECT),
    )(*ops, *after)
    return res[0], res[1], res[2:2 + n], res[2 + n:2 + 2 * n], res[-1]


def gather_wait(tag, send, recv, shards, gathered, after):
    n = len(shards)

    def body(*refs):
        srcs, dsts = refs[:n], refs[n:2 * n]
        send_r, recv_r = refs[2 * n], refs[2 * n + 1]
        x, y, c, k = _place()
        peers = [(1 - x, y, c), (x, 1 - y, c), (1 - x, 1 - y, c), (x, y, 1 - c)]
        for a in range(n):
            for j, dev in enumerate(peers):
                _gather_copy(a, j, srcs[a], _gather_slot(dsts[a], k), send_r, recv_r, dev).wait_send()
                pos = 2 * dev[0] + dev[1]
                _gather_copy(a, j, srcs[a], _gather_slot(dsts[a], pos), send_r, recv_r, dev).wait_recv()

    ops = list(shards) + list(gathered)
    res = pl.pallas_call(
        body, name=f"gather_wait{tag}", in_specs=[HBM] * (2 * n) + [SEMS, SEMS] + [ANY] * len(after),
        out_specs=[HBM] * (2 * n), out_shape=[pltpu.HBM(v.shape, v.dtype) for v in ops],
        input_output_aliases={i: i for i in range(2 * n)},
        compiler_params=pltpu.CompilerParams(has_side_effects=EFFECT),
    )(*ops, send, recv, *after)
    return res[n:]


def _half(ref, h):
    rows = ref.shape[-2] // 2
    start = pl.multiple_of(h * rows, 16)
    if len(ref.shape) == 2:
        return ref.at[pl.ds(start, rows), :]
    return ref.at[:, pl.ds(start, rows), :]


HBM = pl.BlockSpec(memory_space=pltpu.HBM)
SEMS = pl.BlockSpec(memory_space=pltpu.SEMAPHORE)
EFFECT = pltpu.SideEffectType.DATAFLOW_SIDE_EFFECTING


def rs_pair_start(tag, grads):
    n = len(grads)

    def body(*refs):
        g, theirs = refs[:n], refs[n:2 * n]
        send, recv = refs[2 * n], refs[2 * n + 1]
        x, y, c, _ = _place()
        for a in range(n):
            _start_pieces(_half(g[a], 1 - c), theirs[a],
                          lambda s, d, a=a: _rcopy(s, d, send.at[a], recv.at[a], (x, y, 1 - c)))
        refs[-1][...] = jnp.zeros_like(refs[-1])

    lands = [lax.empty(g.shape[:-2] + (g.shape[-2] // 2, g.shape[-1]), g.dtype) for g in grads]
    ops = [pltpu.with_memory_space_constraint(v, pltpu.HBM) for v in list(grads) + lands]
    sem = pltpu.SemaphoreType.DMA((n,))
    res = pl.pallas_call(
        body, name=f"rs_pair_start{tag}", in_specs=[HBM] * (2 * n),
        out_specs=[SEMS, SEMS] + [HBM] * (2 * n) + [pl.BlockSpec(memory_space=pltpu.VMEM)],
        out_shape=[sem, sem] + [pltpu.HBM(v.shape, v.dtype) for v in ops] + [jax.ShapeDtypeStruct((8, LANES), f32)],
        input_output_aliases={i: 2 + i for i in range(2 * n)},
        compiler_params=pltpu.CompilerParams(has_side_effects=EFFECT),
    )(*ops)
    return res[0], res[1], res[2:2 + n], res[2 + n:2 + 2 * n], res[-1]


def rs_pair_wait(tag, send, recv, grads, theirs, after):
    n = len(grads)

    def body(*refs):
        g, land = refs[:n], refs[n:2 * n]
        send_r, recv_r = refs[2 * n], refs[2 * n + 1]
        x, y, c, _ = _place()
        for a in range(n):
            cp = _rcopy(_half(g[a], 1 - c), land[a], send_r.at[a], recv_r.at[a], (x, y, 1 - c))
            cp.wait_send()
            cp.wait_recv()

    ops = list(grads) + list(theirs)
    res = pl.pallas_call(
        body, name=f"rs_pair_wait{tag}", in_specs=[HBM] * (2 * n) + [SEMS, SEMS] + [ANY] * len(after),
        out_specs=[HBM] * (2 * n), out_shape=[pltpu.HBM(v.shape, v.dtype) for v in ops],
        input_output_aliases={i: i for i in range(2 * n)},
        compiler_params=pltpu.CompilerParams(has_side_effects=EFFECT),
    )(*ops, send, recv, *after)
    return res[:n], res[n:]


def _chip_piece(ref, k):
    return _cols(ref, k, WIN_SHARD) if len(ref.shape) == 2 else ref.at[k]


def _chip_copy(a, k, src, dst, send, recv, me, c):
    return _rcopy(src, dst, send.at[a * NCHIP + k], recv.at[a * NCHIP + me], (k // 2, k % 2, c))


def rs_chips_start(tag, sums):
    n = len(sums)

    def pshape(s):
        return (NCHIP, s[0], WIN_SHARD) if len(s) == 2 else s

    def body(*refs):
        s, land = refs[:n], refs[n:2 * n]
        send, recv = refs[2 * n], refs[2 * n + 1]
        token = refs[-1]
        x, y, c, me = _place()
        for k in range(NCHIP):
            @pl.when(me != k)
            def _():
                for a in range(n):
                    _start_pieces(_chip_piece(s[a], k), land[a].at[me],
                                  lambda src, dst, a=a: _chip_copy(a, k, src, dst, send, recv, me, c))
        token[...] = jnp.zeros_like(token)

    lands = [lax.empty(pshape(v.shape), v.dtype) for v in sums]
    ops = [pltpu.with_memory_space_constraint(v, pltpu.HBM) for v in list(sums) + lands]
    sem = pltpu.SemaphoreType.DMA((n * NCHIP,))
    res = pl.pallas_call(
        body, name=f"rs_chips_start{tag}", in_specs=[HBM] * (2 * n),
        out_specs=[SEMS, SEMS] + [HBM] * (2 * n) + [pl.BlockSpec(memory_space=pltpu.VMEM)],
        out_shape=[sem, sem] + [pltpu.HBM(v.shape, v.dtype) for v in ops] + [jax.ShapeDtypeStruct((8, LANES), f32)],
        input_output_aliases={i: 2 + i for i in range(2 * n)},
        compiler_params=pltpu.CompilerParams(has_side_effects=EFFECT),
    )(*ops)
    return res[0], res[1], res[2:2 + n], res[2 + n:2 + 2 * n], res[-1]


def rs_chips_wait(tag, send, recv, sums, lands, after):
    n = len(sums)

    def body(*refs):
        s, land = refs[:n], refs[n:2 * n]
        send_r, recv_r = refs[2 * n], refs[2 * n + 1]
        x, y, c, me = _place()
        for k in range(NCHIP):
            @pl.when(me != k)
            def _():
                for a in range(n):
                    piece = _chip_piece(s[a], k)
                    _chip_copy(a, k, piece, land[a].at[me], send_r, recv_r, me, c).wait_send()
                    _rcopy(piece, land[a].at[k], send_r.at[a * NCHIP + k], recv_r.at[a * NCHIP + k],
                           (k // 2, k % 2, c)).wait_recv()

    ops = list(sums) + list(lands)
    res = pl.pallas_call(
        body, name=f"rs_chips_wait{tag}", in_specs=[HBM] * (2 * n) + [SEMS, SEMS] + [ANY] * len(after),
        out_specs=[HBM] * (2 * n), out_shape=[pltpu.HBM(v.shape, v.dtype) for v in ops],
        input_output_aliases={i: i for i in range(2 * n)},
        compiler_params=pltpu.CompilerParams(has_side_effects=EFFECT),
    )(*ops, send, recv, *after)
    return res[:n], res[n:]


def rs_join(tag, halves):
    n = len(halves)

    def body(*refs):
        h, other = refs[:n], refs[n:2 * n]
        send, recv = refs[2 * n:]
        x, y, c, _ = _place()

        def give(a, s, d):
            return _rcopy(s, d, send.at[a], recv.at[a], (x, y, 1 - c))

        for a in range(n):
            _start_pieces(h[a], other[a], functools.partial(give, a))
        for a in range(n):
            give(a, h[a], other[a]).wait()

    outs = [jax.ShapeDtypeStruct(v.shape, v.dtype) for v in halves]
    return pl.pallas_call(
        body, name=f"rs_join{tag}", in_specs=[ANY] * n, out_specs=[ANY] * n, out_shape=outs,
        scratch_shapes=[pltpu.SemaphoreType.DMA((n,))] * 2,
    )(*halves)


def _row_tile(rows, cols, itemsize=4, target=2 << 20):
    best = 8
    for t in range(8, rows + 1, 8):
        if rows % t == 0 and t * cols * itemsize <= target:
            best = t
    return best


GRAD_WIRE = jnp.bfloat16


def add_n(name, terms, out_dtype=f32):
    shape = terms[0].shape
    cols = shape[-1]
    rows = math.prod(shape[:-1])
    tr = _row_tile(rows, cols)

    def body(*refs):
        acc = refs[0][...]
        for r in refs[1:-1]:
            acc = acc + r[...]
        refs[-1][...] = acc.astype(out_dtype)

    tile = pl.BlockSpec((tr, cols), lambda i: (i, 0))
    out = pl.pallas_call(
        body, name=name, grid=(rows // tr,), in_specs=[tile] * len(terms), out_specs=tile,
        out_shape=jax.ShapeDtypeStruct((rows, cols), out_dtype), compiler_params=_cp(("parallel",)),
    )(*[t.reshape(rows, cols) for t in terms])
    return out.reshape(shape)


def add_chips(name, land, own):
    _, rows, cols = land.shape
    tr = _row_tile(rows, cols, target=1 << 20)

    def body(land_r, own_r, o_r):
        me = 2 * lax.axis_index("x") + lax.axis_index("y")
        for k in range(NCHIP):
            @pl.when(me == k)
            def _():
                acc = None
                for j in range(NCHIP):
                    t = (own_r[...] if j == k else land_r[j]).astype(f32)
                    acc = t if acc is None else acc + t
                o_r[...] = acc

    tile = pl.BlockSpec((tr, cols), lambda i: (i, 0))
    return pl.pallas_call(
        body, name=name, grid=(rows // tr,), in_specs=[pl.BlockSpec((NCHIP, tr, cols), lambda i: (0, i, 0)), tile],
        out_specs=tile, out_shape=jax.ShapeDtypeStruct((rows, cols), f32), compiler_params=_cp(("parallel",)),
    )(land, own)


def reduce_scatter_pair(tag, G):
    names = tuple(G)
    grads = [G[n] if G[n].ndim == 3 or n == "w_in" else G[n].reshape(NCHIP, D // NCHIP, D) for n in names]
    send, recv, grads, theirs, token = rs_pair_start(tag, grads)
    return (tag, names, send, recv, grads, theirs), token[0, 0]


def reduce_scatter_chips(state, after):
    c = lax.axis_index("c")
    tag, names, send, recv, grads, theirs = state
    grads, theirs = rs_pair_wait(tag, send, recv, grads, theirs, after)
    sums = []
    for n, g, t in zip(names, grads, theirs):
        rows = g.shape[-2] // 2
        mine = lax.dynamic_slice_in_dim(g, c * rows, rows, axis=g.ndim - 2)
        sums.append(add_n(f"rs_add_pair{tag}_{n}", [mine, t], GRAD_WIRE))
    send, recv, sums, lands, token = rs_chips_start(tag, sums)
    return (tag, names, send, recv, sums, lands), token[0, 0]


def reduce_scatter_finish(state, after):
    me = 2 * lax.axis_index("x") + lax.axis_index("y")
    tag, names, send, recv, sums, lands = state
    sums, landed = rs_chips_wait(tag, send, recv, sums, lands, after)
    halves = []
    for n, s, v in zip(names, sums, landed):
        own = lax.dynamic_slice_in_dim(s, me * WIN_SHARD, WIN_SHARD, axis=1) if s.ndim == 2 else \
            lax.dynamic_index_in_dim(s, me, 0, keepdims=False)
        halves.append(add_chips(f"rs_add_chips{tag}_{n}", v, own))
    return dict(zip(names, zip(halves, rs_join(tag, halves))))


NDEV = 8


def _small_copy(r, src, dst, send, recv, x, y, c):
    return _rcopy(src, dst, send.at[r - 1], recv.at[r - 1], (x ^ (r >> 2), y ^ ((r >> 1) & 1), c ^ (r & 1)))


def small_start(pack):
    def body(p, land, send, recv, p_thru, land_thru, token):
        x, y, c, _ = _place()
        me = 4 * x + 2 * y + c
        for r in range(1, NDEV):
            _start_pieces(p, land.at[me], lambda s, d, r=r: _small_copy(r, s, d, send, recv, x, y, c), 128 << 10)
        token[...] = jnp.zeros_like(token)

    ops = [pltpu.with_memory_space_constraint(v, pltpu.HBM) for v in (pack, lax.empty((NDEV,) + pack.shape, f32))]
    sem = pltpu.SemaphoreType.DMA((NDEV - 1,))
    return pl.pallas_call(
        body, name="small_start", in_specs=[HBM, HBM],
        out_specs=[SEMS, SEMS, HBM, HBM, pl.BlockSpec(memory_space=pltpu.VMEM)],
        out_shape=[sem, sem] + [pltpu.HBM(v.shape, v.dtype) for v in ops] + [jax.ShapeDtypeStruct((8, LANES), f32)],
        input_output_aliases={0: 2, 1: 3}, compiler_params=pltpu.CompilerParams(has_side_effects=EFFECT),
    )(*ops)


def small_wait(send, recv, pack, land, after):
    def body(p, land_r, send_r, recv_r, *rest):
        x, y, c, _ = _place()
        me = 4 * x + 2 * y + c
        for r in range(1, NDEV):
            _small_copy(r, p, land_r.at[me], send_r, recv_r, x, y, c).wait_send()
            src = 4 * (x ^ (r >> 2)) + 2 * (y ^ ((r >> 1) & 1)) + (c ^ (r & 1))
            _small_copy(r, p, land_r.at[src], send_r, recv_r, x, y, c).wait_recv()

    return pl.pallas_call(
        body, name="small_wait", in_specs=[HBM, HBM, SEMS, SEMS] + [ANY] * len(after), out_specs=[HBM, HBM],
        out_shape=[pltpu.HBM(pack.shape, f32), pltpu.HBM(land.shape, f32)], input_output_aliases={0: 0, 1: 1},
        compiler_params=pltpu.CompilerParams(has_side_effects=EFFECT),
    )(pack, land, send, recv, *after)


def small_sum(land, pack):
    def body(land_r, p_r, o_r):
        me = 4 * lax.axis_index("x") + 2 * lax.axis_index("y") + lax.axis_index("c")
        for k in range(NDEV):
            @pl.when(me == k)
            def _():
                acc = None
                for d in range(NDEV):
                    t = p_r[...] if d == k else land_r[d]
                    acc = t if acc is None else acc + t
                o_r[...] = acc

    vm = pl.BlockSpec(memory_space=pltpu.VMEM)
    return pl.pallas_call(
        body, name="small_sum", in_specs=[vm, vm], out_specs=vm, out_shape=jax.ShapeDtypeStruct(pack.shape, f32),
        compiler_params=pltpu.CompilerParams(vmem_limit_bytes=40 << 20),
    )(land, pack)


def _adamw_math(w, g, m, v):
    m = ADAM_B1 * m + (1.0 - ADAM_B1) * g
    v = ADAM_B2 * v + (1.0 - ADAM_B2) * (g * g)
    m_hat = m / (1.0 - ADAM_B1 ** ADAM_STEP)
    v_hat = v / (1.0 - ADAM_B2 ** ADAM_STEP)
    return -ADAM_LR * (m_hat / (jnp.sqrt(v_hat) + ADAM_EPS) + ADAM_WD * w), m, v


def adamw_big(name, halves, w, m, v):
    _, R, C = w.shape
    tr = _row_tile(R // 2, C, target=1 << 20)
    nt = R // 2 // tr

    def body(a0, b0, a1, b1, w_r, m_r, v_r, g_o, d_o, m_o, v_o):
        mine = pl.program_id(1) == lax.axis_index("c")
        g = jnp.where(pl.program_id(0) == 0, jnp.where(mine, a0[...], b0[...]), jnp.where(mine, a1[...], b1[...]))
        g_o[...] = g
        d_o[...], m_o[...], v_o[...] = _adamw_math(w_r[...], g, m_r[...], v_r[...])

    stk = pl.BlockSpec((None, tr, C), lambda l, h, i: (l, h * nt + i, 0))
    lay0 = pl.BlockSpec((tr, C), lambda l, h, i: (jnp.where(l == 0, i, nt - 1), 0))
    lay1 = pl.BlockSpec((tr, C), lambda l, h, i: (jnp.where(l == 0, 0, i), 0))
    return pl.pallas_call(
        body, name=name, grid=(DEPTH, 2, nt),
        in_specs=[lay0, lay0, lay1, lay1, stk, stk, stk],
        out_specs=[stk] * 4, out_shape=[jax.ShapeDtypeStruct(w.shape, f32)] * 4,
        compiler_params=_cp(("arbitrary", "arbitrary", "arbitrary")),
    )(*halves[0], *halves[1], w, m, v)


def adamw_small(name, g, w, m, v):
    def body(g_r, w_r, m_r, v_r, d_o, m_o, v_o):
        d_o[...], m_o[...], v_o[...] = _adamw_math(w_r[...], g_r[...], m_r[...], v_r[...])

    return pl.pallas_call(body, name=name, out_shape=[jax.ShapeDtypeStruct(w.shape, f32)] * 3)(g, w, m, v)


WEIGHTS = ("w_in", "conv_w", "gmlp_ln_g", "gmlp_ln_b", "w_s", "b_s", "p_a", "p_b", "p_c", "w_o", "ln1_g", "ln1_b",
           "w_gate", "w_up", "w_down", "ln2_g", "ln2_b")
VECS = ("ln1_g", "ln1_b", "ln2_g", "ln2_b", "gmlp_ln_g", "gmlp_ln_b")
ROWS_VEC, ROWS_BS, ROWS_WS, ROWS_CONV = D // LANES, 8, 8 * BLK, 3 * D // LANES
ROWS_LAYER = len(VECS) * ROWS_VEC + ROWS_BS + ROWS_WS + ROWS_CONV


def _pack_small(per_layer, tail):
    parts = []
    for P in per_layer:
        parts += [P[n].reshape(ROWS_VEC, LANES) for n in VECS]
        parts += [P["b_s"].reshape(ROWS_BS, LANES), P["w_s"].reshape(ROWS_WS, LANES), P["conv_w"].reshape(ROWS_CONV, LANES)]
    return jnp.concatenate(parts + [tail], axis=0)


def _unpack_small(pack):
    out = []
    for l in range(DEPTH):
        r = l * ROWS_LAYER
        P = {}
        for n in VECS:
            P[n] = pack[r:r + ROWS_VEC].reshape(D)
            r += ROWS_VEC
        P["b_s"] = pack[r:r + ROWS_BS].reshape(8, BLK)
        r += ROWS_BS
        P["w_s"] = pack[r:r + ROWS_WS].reshape(8, BLK, BLK)
        r += ROWS_WS
        P["conv_w"] = pack[r:r + ROWS_CONV].reshape(3, D)
        out.append(P)
    return out, pack[DEPTH * ROWS_LAYER:]


def kernel(x, positions, w_in, conv_w, gmlp_ln_g, gmlp_ln_b, w_s, b_s, p_a, p_b, p_c, w_o, ln1_g, ln1_b, w_gate, w_up, w_down, ln2_g, ln2_b, loss_target, m_w_in, m_conv_w, m_gmlp_ln_g, m_gmlp_ln_b, m_w_s, m_b_s, m_p_a, m_p_b, m_p_c, m_w_o, m_ln1_g, m_ln1_b, m_w_gate, m_w_up, m_w_down, m_ln2_g, m_ln2_b, v_w_in, v_conv_w, v_gmlp_ln_g, v_gmlp_ln_b, v_w_s, v_b_s, v_p_a, v_p_b, v_p_c, v_w_o, v_ln1_g, v_ln1_b, v_w_gate, v_w_up, v_w_down, v_ln2_g, v_ln2_b):
    Wt = dict(w_in=w_in, conv_w=conv_w, gmlp_ln_g=gmlp_ln_g, gmlp_ln_b=gmlp_ln_b, w_s=w_s, b_s=b_s, p_a=p_a, p_b=p_b,
              p_c=p_c, w_o=w_o, ln1_g=ln1_g, ln1_b=ln1_b, w_gate=w_gate, w_up=w_up, w_down=w_down, ln2_g=ln2_g, ln2_b=ln2_b)
    Mt = dict(w_in=m_w_in, conv_w=m_conv_w, gmlp_ln_g=m_gmlp_ln_g, gmlp_ln_b=m_gmlp_ln_b, w_s=m_w_s, b_s=m_b_s, p_a=m_p_a,
              p_b=m_p_b, p_c=m_p_c, w_o=m_w_o, ln1_g=m_ln1_g, ln1_b=m_ln1_b, w_gate=m_w_gate, w_up=m_w_up,
              w_down=m_w_down, ln2_g=m_ln2_g, ln2_b=m_ln2_b)
    Vt = dict(w_in=v_w_in, conv_w=v_conv_w, gmlp_ln_g=v_gmlp_ln_g, gmlp_ln_b=v_gmlp_ln_b, w_s=v_w_s, b_s=v_b_s, p_a=v_p_a,
              p_b=v_p_b, p_c=v_p_c, w_o=v_w_o, ln1_g=v_ln1_g, ln1_b=v_ln1_b, w_gate=v_w_gate, w_up=v_w_up,
              w_down=v_w_down, ln2_g=v_ln2_g, ln2_b=v_ln2_b)
    chip = 2 * lax.axis_index("x") + lax.axis_index("y")
    cw = D // NCHIP

    def layer_weights(l, gathered, conv_all):
        Wl = dict(zip(BIG, gathered))
        for n in ("p_a", "p_c", "w_o"):
            Wl[n] = Wl[n].reshape(D, D)
        Wl["conv_w"] = conv_all[:, l].transpose(1, 0, 2).reshape(3, D)
        for n in VECS + ("w_s", "b_s"):
            Wl[n] = Wt[n][l]
        return Wl

    halves = [Wt[n][0].astype(MX).reshape(2, Wt[n].shape[1] // 2, Wt[n].shape[2]) for n in BIG]
    got = gather_halves(halves + [conv_w])
    conv_all = got[NBIG]
    g0 = [got[0].reshape(D, NIN)] + [a.reshape(NCHIP, 2 * a.shape[2], a.shape[3]) for a in got[1:NBIG]]
    send1, recv1, sh1, g1, coming = gather_start("1", [Wt[n][1].astype(MX) for n in BIG], [conv_all])
    W0 = layer_weights(0, g0, conv_all)
    W0["gmlp_ln_g"] = W0["gmlp_ln_g"] + coming[0, 0]

    def W1(h):
        return layer_weights(1, gather_wait("1", send1, recv1, sh1, g1, [h]), conv_all)

    layers = [W0, W1]

    rs_state, rs_started, held = {}, {}, {}

    def start_exchange(l, g):
        if "loss" in g:
            held[l] = g
            return None
        if "conv_w" in g:
            held[l] = g
            rs_state[(l, False)], started = reduce_scatter_chips(rs_state[(l, False)], [g["w_s"]])
            if l == 0:
                pack = _pack_small([held[j] for j in range(DEPTH)], held[DEPTH]["loss"])
                *held["small"], token = small_start(pack)
                started = started + token[0, 0]
            return started
        if "dx" in g:
            rs_state[(l, True)], rs_started[(l, True)] = reduce_scatter_chips(rs_state[(l, True)], [g["dx"]])
            return rs_started[(l, True)]
        key = (l, "w_in" in g)
        rs_state[key], started = reduce_scatter_pair(f"{l}{'b' if key[1] else 'a'}", g)
        return started

    _, grad_x, _ = local_step(x[0], positions[0], loss_target[0], layers, start_exchange)

    last = jnp.zeros((8, LANES), f32) + rs_started[(0, True)]
    behind = [grad_x, last]
    red = [dict() for _ in range(DEPTH)]
    for key in ((1, False), (1, True), (0, False)):
        red[key[0]].update(reduce_scatter_finish(rs_state[key], behind))
    small, tail = _unpack_small(small_sum(*reversed(small_wait(*held["small"], behind))))
    loss = tail[0, 0]

    G, DW, NM, NV = {}, {}, {}, {}
    zc = jnp.zeros((3, D), f32)
    wp = _pack_small([{**{n: Wt[n][l] for n in VECS + ("b_s", "w_s")}, "conv_w": zc} for l in range(DEPTH)], jnp.zeros((8, LANES), f32))
    mp = _pack_small([{**{n: Mt[n][l] for n in VECS + ("b_s", "w_s")}, "conv_w": zc} for l in range(DEPTH)], jnp.zeros((8, LANES), f32))
    vp = _pack_small([{**{n: Vt[n][l] for n in VECS + ("b_s", "w_s")}, "conv_w": zc} for l in range(DEPTH)], jnp.ones((8, LANES), f32))
    gp = _pack_small(small, jnp.zeros((8, LANES), f32))
    outs = [_unpack_small(a)[0] for a in adamw_small("adamw_small", gp, wp, mp, vp)]
    for n in VECS + ("b_s", "w_s"):
        G[n] = jnp.stack([small[l][n] for l in range(DEPTH)])
        DW[n], NM[n], NV[n] = (jnp.stack([o[l][n] for l in range(DEPTH)]) for o in outs)
    gconv = jnp.stack([lax.dynamic_slice(small[l]["conv_w"], (0, chip * cw), (3, cw)) for l in range(DEPTH)])
    G["conv_w"] = gconv
    flat = lambda a: a.reshape(DEPTH * 3, cw)
    d, m2, v2 = adamw_small("adamw_conv", flat(gconv), flat(conv_w), flat(m_conv_w), flat(v_conv_w))
    DW["conv_w"], NM["conv_w"], NV["conv_w"] = (a.reshape(DEPTH, 3, cw) for a in (d, m2, v2))

    updated = {}
    for n in BIG[1:]:
        tr = (lambda a: jnp.swapaxes(a, 1, 2)) if n in ("w_gate", "w_up") else (lambda a: a)
        updated[n] = adamw_big("adamw_" + n, (red[0][n], red[1][n]), tr(Wt[n]), tr(Mt[n]), tr(Vt[n]))
        G[n], DW[n], NM[n], NV[n] = map(tr, updated[n])
    done = [d, DW["ln2_b"], red[1]["w_in"][1]] + [updated[n][1] for n in BIG[1:]]
    red[0].update(reduce_scatter_finish(rs_state[(0, True)], done))
    G["w_in"], DW["w_in"], NM["w_in"], NV["w_in"] = adamw_big(
        "adamw_w_in", (red[0]["w_in"], red[1]["w_in"]), Wt["w_in"], Mt["w_in"], Vt["w_in"])

    return (loss, grad_x[None], *[G[n] for n in WEIGHTS], *[DW[n] for n in WEIGHTS], *[NM[n] for n in WEIGHTS],
            *[NV[n] for n in WEIGHTS])
```

```python
import functools
import math

import jax
import jax.numpy as jnp
from jax import lax
from jax.experimental import pallas as pl
from jax.experimental.pallas import tpu as pltpu

D = 1024
NIN = 12800
DFF = 2816
NCHIP = 4
FB = DFF // NCHIP
WIN_SHARD = NIN // NCHIP
DEPTH = 2
GROUPS = ((128, 1), (512, 4), (2048, 16))
HD = 64
BLK = 128
AO = 512
ALPHA = (2 * DEPTH) ** 0.25
EPS = 1e-5
ROPE_THETA = 10000.0
LANES = 128
NEG = -1e30

C_GATES, C_BCH, C_QKV, C_UV = 0, 3 * D, 6 * D, 6 * D + 9 * AO

MX = jnp.bfloat16
ACT = jnp.bfloat16

ADAM_LR, ADAM_B1, ADAM_B2, ADAM_EPS, ADAM_WD, ADAM_STEP = 0.001, 0.9, 0.999, 1e-08, 0.01, 10

f32 = jnp.float32
NT = (((1,), (1,)), ((), ()))
TN = (((0,), (0,)), ((), ()))


def _cp(sem, vmem_mb=48):
    return pltpu.CompilerParams(dimension_semantics=sem, vmem_limit_bytes=vmem_mb << 20)


def _dot(a, b, dims=None):
    if dims is None:
        return jnp.dot(a, b, preferred_element_type=f32)
    return lax.dot_general(a, b, dims, preferred_element_type=f32)


def _ln_stats(r):
    mu = jnp.mean(r, axis=-1, keepdims=True)
    xc = r - mu
    var = jnp.mean(xc * xc, axis=-1, keepdims=True)
    rstd = lax.rsqrt(var + EPS)
    return xc * rstd, rstd


def _ln_bwd(dy, xhat, rstd, g):
    dxh = dy * g
    return rstd * (dxh - jnp.mean(dxh, axis=-1, keepdims=True) - xhat * jnp.mean(dxh * xhat, axis=-1, keepdims=True))


def _gelu(x):
    return 0.5 * x * (1.0 + lax.erf(x * (1.0 / math.sqrt(2.0))))


def _gelu_grad(x):
    return 0.5 * (1.0 + lax.erf(x * (1.0 / math.sqrt(2.0)))) + x * jnp.exp(-0.5 * x * x) * (1.0 / math.sqrt(2.0 * math.pi))


def _sigmoid(x):
    return 0.5 * jnp.tanh(0.5 * x) + 0.5


def _acc_rows(o_ref, first, val):
    @pl.when(first)
    def _():
        o_ref[...] = jnp.zeros_like(o_ref)
    o_ref[...] += jnp.sum(val, axis=0, keepdims=True)


def mm_in(x, w, bias):
    T = x.shape[0]
    tm, tn = min(1024, T), 1280

    def body(x_ref, w_ref, b_ref, o_ref, xb):
        @pl.when(pl.program_id(1) == 0)
        def _():
            xb[...] = x_ref[...].astype(MX)
        o_ref[...] = (_dot(xb[...], w_ref[...]) + b_ref[...]).astype(o_ref.dtype)

    return pl.pallas_call(
        body, name="mm_in", grid=(T // tm, NIN // tn),
        in_specs=[pl.BlockSpec((tm, D), lambda i, j: (i, 0)), pl.BlockSpec((D, tn), lambda i, j: (0, j)),
                  pl.BlockSpec((1, tn), lambda i, j: (0, j))],
        out_specs=pl.BlockSpec((tm, tn), lambda i, j: (i, j)),
        out_shape=jax.ShapeDtypeStruct((T, NIN), ACT),
        scratch_shapes=[pltpu.VMEM((tm, D), MX)],
        compiler_params=_cp(("parallel", "arbitrary")),
    )(x, w, bias)


HALO = 16
TM_AC = 256


def _uv_specs():
    return [pl.BlockSpec((TM_AC, 512), functools.partial(lambda i, j: (i, j), j=C_UV // 512 + j)) for j in range(4)]


def _gmlp_fwd(up, vp, ws_ref, bs_ref, lg, lb):
    u = _gelu(up)
    xhat, rstd = _ln_stats(_gelu(vp))
    vn = xhat * lg + lb
    vnb = vn.astype(MX)
    rows = []
    for c in range(up.shape[0] // BLK):
        r = slice(c * BLK, (c + 1) * BLK)
        rows.append(jnp.concatenate(
            [_dot(ws_ref[g], vnb[r, g * BLK:(g + 1) * BLK]) + bs_ref[g] for g in range(8)], axis=1))
    return u, vn, xhat, rstd, jnp.concatenate(rows, axis=0)


def mix_ac_fwd(proj, conv_w, wst, bsx, lg, lb):
    T = proj.shape[0]
    tm = TM_AC

    def body(bch, halo, u0, u1, v0, v1, cw, ws, bs, lg_ref, lb_ref, ya, yc, zs):
        i = pl.program_id(0)
        pb = bch[...].astype(f32)
        z = pb[:, D:2 * D] * pb[:, 2 * D:]
        hz = halo[:, :D].astype(f32) * halo[:, D:].astype(f32)
        zs[0:HALO, :] = jnp.where(i > 0, hz, 0.0)
        zs[HALO:HALO + tm, :] = z
        cv = cw[0:1, :] * zs[HALO - 2:HALO - 2 + tm, :] + cw[1:2, :] * zs[HALO - 1:HALO - 1 + tm, :] + cw[2:3, :] * z
        ya[...] = (pb[:, :D] * cv).astype(ya.dtype)
        up = jnp.concatenate([u0[...], u1[...]], axis=1).astype(f32)
        vp = jnp.concatenate([v0[...], v1[...]], axis=1).astype(f32)
        u, _, _, _, sp = _gmlp_fwd(up, vp, ws, bs, lg_ref[...], lb_ref[...])
        yc[...] = (u * sp).astype(yc.dtype)

    full = lambda shape: pl.BlockSpec(shape, lambda i: (0,) * len(shape))
    return pl.pallas_call(
        body, name="mix_ac_fwd", grid=(T // tm,),
        in_specs=[pl.BlockSpec((tm, 3 * D), lambda i: (i, 1)),
                  pl.BlockSpec((HALO, 2 * D), lambda i: (jnp.maximum(i * (tm // HALO) - 1, 0), 2)),
                  *_uv_specs(), full((3, D)), full((8, BLK, BLK)), full((8, BLK, BLK)), full((1, D)), full((1, D))],
        out_specs=[pl.BlockSpec((tm, D), lambda i: (i, 0))] * 2,
        out_shape=[jax.ShapeDtypeStruct((T, D), MX)] * 2,
        scratch_shapes=[pltpu.VMEM((HALO + tm, D), f32)],
        compiler_params=_cp(("parallel",)),
    )(proj, proj, proj, proj, proj, proj, conv_w, wst, bsx, lg, lb)


def _swap_halves(x):
    lane = lax.broadcasted_iota(jnp.int32, x.shape, 1)
    return jnp.where((lane % HD) < HD // 2, pltpu.roll(x, x.shape[1] - HD // 2, 1), pltpu.roll(x, HD // 2, 1))


def _tile4(t):
    return jnp.concatenate([t] * (AO // LANES), axis=1)


TM_FOLD = 512


def _fold_out(nat, x, out_ref, d):
    if d == 1:
        out_ref[0] = x.astype(out_ref.dtype)
        return
    rows = x.shape[0] // d
    for j in range(AO // LANES):
        nat[j] = x[:, j * LANES:(j + 1) * LANES]
    for r in range(d):
        out_ref[r] = jnp.concatenate(
            [nat.at[j][pl.ds(r, rows, stride=d), :] for j in range(AO // LANES)], axis=1).astype(out_ref.dtype)


def _unfold_in(nat, in_ref, d):
    if d == 1:
        return in_ref[0].astype(f32)
    rows = in_ref.shape[1]
    for r in range(d):
        v = in_ref[r].astype(f32)
        for j in range(AO // LANES):
            nat.at[j][pl.ds(r, rows, stride=d), :] = v[:, j * LANES:(j + 1) * LANES]
    return jnp.concatenate([nat[j] for j in range(AO // LANES)], axis=1)


def fold_rope(proj, cos_t, sin_t, g, d):
    T = proj.shape[0]
    tm = TM_FOLD
    rows = tm // d

    def body(x_ref, c_ref, s_ref, q_o, k_o, v_o, nat):
        cos, sin = _tile4(c_ref[...]), _tile4(s_ref[...])
        for part, out, scale in ((0, q_o, HD ** -0.5), (1, k_o, 1.0), (2, v_o, None)):
            x = x_ref[:, part * AO:(part + 1) * AO].astype(f32)
            if scale is not None:
                x = (x * cos + _swap_halves(x) * sin) * scale
            _fold_out(nat, x, out, d)

    fold_spec = pl.BlockSpec((d, rows, AO), lambda i: (0, i, 0))
    return pl.pallas_call(
        body, name=f"fold_rope{g}", grid=(T // tm,),
        in_specs=[pl.BlockSpec((tm, 3 * AO), lambda i: (i, C_QKV // (3 * AO) + g)),
                  pl.BlockSpec((tm, LANES), lambda i: (i, 0)), pl.BlockSpec((tm, LANES), lambda i: (i, 0))],
        out_specs=[fold_spec] * 3,
        out_shape=[jax.ShapeDtypeStruct((d, T // d, AO), MX)] * 3,
        scratch_shapes=[pltpu.VMEM((AO // LANES, tm, LANES), f32)],
        compiler_params=_cp(("parallel",)),
    )(proj, cos_t, sin_t)


def _stack_heads(x):
    lane = lax.broadcasted_iota(jnp.int32, x.shape, 1)
    z = jnp.zeros_like(x)
    return jnp.concatenate([jnp.where(lane < HD, x, z), jnp.where(lane >= HD, x, z)], axis=0)


def _unstack_heads(y):
    lane = lax.broadcasted_iota(jnp.int32, (BLK, LANES), 1)
    return jnp.where(lane < HD, y[:BLK], y[BLK:])


def _window_masks():
    row = lax.broadcasted_iota(jnp.int32, (2 * BLK, 2 * BLK), 0) % BLK
    col = lax.broadcasted_iota(jnp.int32, (2 * BLK, 2 * BLK), 1)
    return (col < BLK) & (col >= row), (col >= BLK) & (col - BLK <= row)


def _two_blocks(ref, b):
    r0 = pl.multiple_of(b * BLK, BLK)
    rp = pl.multiple_of(jnp.maximum(b - 1, 0) * BLK, BLK)
    return jnp.concatenate([ref[pl.ds(rp, BLK), :], ref[pl.ds(r0, BLK), :]], axis=0)


def attn_fwd(qf, kf, vf, g, nb):
    T = qf.shape[0]

    def body(q_ref, k_ref, v_ref, o_ref, l_ref):
        prev_m, cur_m = _window_masks()

        def step(b, carry):
            r0 = pl.multiple_of(b * BLK, BLK)
            qs = _stack_heads(q_ref[pl.ds(r0, BLK), :])
            s = _dot(qs, _two_blocks(k_ref, b), NT)
            s = jnp.where(cur_m | (prev_m & ((b % nb) != 0)), s, NEG)
            m = jnp.max(s, axis=-1, keepdims=True)
            p = jnp.exp(s - m)
            l = jnp.sum(p, axis=-1, keepdims=True)
            o = _dot(p.astype(MX), _two_blocks(v_ref, b)) / l
            o_ref[pl.ds(r0, BLK), :] = _unstack_heads(o)
            l_ref[pl.ds(r0, BLK), :] = _unstack_heads(jnp.broadcast_to(m + jnp.log(l), (2 * BLK, LANES)))
            return carry

        lax.fori_loop(0, T // BLK, step, 0, unroll=4)

    spec = pl.BlockSpec((T, LANES), lambda j: (0, j))
    return pl.pallas_call(
        body, name=f"attn_fwd{g}", grid=(AO // LANES,),
        in_specs=[spec] * 3, out_specs=[spec] * 2,
        out_shape=[jax.ShapeDtypeStruct((T, AO), f32)] * 2,
        compiler_params=_cp(("parallel",), 56),
    )(qf, kf, vf)


def _group_weights(lses):
    m = jnp.maximum(jnp.maximum(lses[0], lses[1]), lses[2])
    e = [jnp.exp(l - m) for l in lses]
    inv = 1.0 / (e[0] + e[1] + e[2])
    return [x * inv for x in e]


def _fold_specs(T, tm):
    specs = []
    for _, d in GROUPS:
        specs.append(pl.BlockSpec((d, tm // d, AO), lambda i: (0, i, 0)))
    return specs


def combine_fwd(os_, lses):
    T = os_[0].shape[0] * os_[0].shape[1]
    tm = TM_FOLD

    def body(o0, o1, o2, l0, l1, l2, y_ref, nat):
        o = [_unfold_in(nat, r, d) for r, (_, d) in zip((o0, o1, o2), GROUPS)]
        ls = [_unfold_in(nat, r, d) for r, (_, d) in zip((l0, l1, l2), GROUPS)]
        w = _group_weights(ls)
        y_ref[...] = (w[0] * o[0] + w[1] * o[1] + w[2] * o[2]).astype(y_ref.dtype)

    specs = _fold_specs(T, tm)
    return pl.pallas_call(
        body, name="combine_fwd", grid=(T // tm,),
        in_specs=specs + specs, out_specs=pl.BlockSpec((tm, AO), lambda i: (i, 0)),
        out_shape=jax.ShapeDtypeStruct((T, AO), MX),
        scratch_shapes=[pltpu.VMEM((AO // LANES, tm, LANES), f32)],
        compiler_params=_cp(("parallel",)),
    )(*os_, *lses)


TM_MIX = 256


def mix_out_fwd(proj, ya, yb, yc, x0, pa, pb, pc, wo, g1, b1):
    T = x0.shape[0]
    tm = min(TM_MIX, T)

    def body(gt, ya_r, yb_r, yc_r, x0_r, pa_r, pb_r, pc_r, wo_r, g_r, b_r, mabc, m_o, r1_o, x1_o):
        ma = _dot(ya_r[...], pa_r[...])
        ybv = yb_r[...]
        mb = jnp.concatenate([_dot(ybv, pb_r[k]) for k in range(NCHIP)], axis=1)
        mc = _dot(yc_r[...], pc_r[...])
        m = jnp.zeros((tm, D), f32)
        for j, mm in enumerate((ma, mb, mc)):
            mabc[:, j * D:(j + 1) * D] = mm.astype(mabc.dtype)
            m = m + _sigmoid(gt[:, j * D:(j + 1) * D].astype(f32)) * mm
        mb16 = m.astype(MX)
        m_o[...] = mb16
        r1 = ALPHA * x0_r[...] + _dot(mb16, wo_r[...])
        r1_o[...] = r1
        xhat, _ = _ln_stats(r1)
        x1_o[...] = xhat * g_r[...] + b_r[...]

    full = lambda shape: pl.BlockSpec(shape, lambda i: (0,) * len(shape))
    tile = lambda w: pl.BlockSpec((tm, w), lambda i: (i, 0))
    return pl.pallas_call(
        body, name="mix_out_fwd", grid=(T // tm,),
        in_specs=[tile(3 * D), tile(D), tile(AO), tile(D), tile(D), full((D, D)), full((NCHIP, AO, D // NCHIP)),
                  full((D, D)), full((D, D)), full((1, D)), full((1, D))],
        out_specs=[tile(3 * D), tile(D), tile(D), tile(D)],
        out_shape=[jax.ShapeDtypeStruct((T, 3 * D), MX), jax.ShapeDtypeStruct((T, D), MX),
                   jax.ShapeDtypeStruct((T, D), f32), jax.ShapeDtypeStruct((T, D), f32)],
        compiler_params=_cp(("parallel",), 56),
    )(proj, ya, yb, yc, x0, pa, pb, pc, wo, g1, b1)


TM_FF = 512
TM_FFB = 256
ROW_CHUNK = 64


def ffn_up_fwd(x1, wg, wu):
    T = x1.shape[0]
    tm = min(TM_FF, T)

    def body(x_r, wg_r, wu_r, g_o, u_o, h_o, xb):
        @pl.when(pl.program_id(1) == 0)
        def _():
            xb[...] = x_r[...].astype(MX)
        gate = _dot(xb[...], wg_r[0])
        up = _dot(xb[...], wu_r[0])
        g_o[0] = gate.astype(g_o.dtype)
        u_o[0] = up.astype(u_o.dtype)
        h_o[0] = (gate * _sigmoid(gate) * up).astype(h_o.dtype)

    wspec = pl.BlockSpec((1, D, FB), lambda i, k: (k, 0, 0))
    ospec = pl.BlockSpec((1, tm, FB), lambda i, k: (k, i, 0))
    return pl.pallas_call(
        body, name="ffn_up_fwd", grid=(T // tm, NCHIP),
        in_specs=[pl.BlockSpec((tm, D), lambda i, k: (i, 0)), wspec, wspec],
        out_specs=[ospec] * 3,
        out_shape=[jax.ShapeDtypeStruct((NCHIP, T, FB), ACT)] * 2 + [jax.ShapeDtypeStruct((NCHIP, T, FB), MX)],
        scratch_shapes=[pltpu.VMEM((tm, D), MX)],
        compiler_params=_cp(("parallel", "arbitrary")),
    )(x1, wg, wu)


def ffn_down_fwd(hh, wd, x1, g2, b2):
    T = x1.shape[0]
    tm = min(TM_FF, T)

    def body(h_r, w_r, x_r, g_r, b_r, r2_o, x2_o):
        r2 = ALPHA * x_r[...]
        for k in range(NCHIP):
            r2 = r2 + _dot(h_r[k], w_r[k])
        r2_o[...] = r2
        xhat, _ = _ln_stats(r2)
        x2_o[...] = xhat * g_r[...] + b_r[...]

    tile = pl.BlockSpec((tm, D), lambda i: (i, 0))
    vec = pl.BlockSpec((1, D), lambda i: (0, 0))
    return pl.pallas_call(
        body, name="ffn_down_fwd", grid=(T // tm,),
        in_specs=[pl.BlockSpec((NCHIP, tm, FB), lambda i: (0, i, 0)), pl.BlockSpec((NCHIP, FB, D), lambda i: (0, 0, 0)),
                  tile, vec, vec],
        out_specs=[tile, tile], out_shape=[jax.ShapeDtypeStruct((T, D), f32)] * 2,
        compiler_params=_cp(("parallel",)),
    )(hh, wd, x1, g2, b2)


def loss_grad(y, tgt):
    T = y.shape[0]
    tm = min(512, T)

    def body(y_r, t_r, l_o, dy_o):
        e = y_r[...] - t_r[...]
        dy_o[...] = e * (1.0 / D)

        @pl.when(pl.program_id(0) == 0)
        def _():
            l_o[...] = jnp.zeros_like(l_o)
        l_o[...] += (0.5 / D) * jnp.sum(e * e)

    tile = pl.BlockSpec((tm, D), lambda i: (i, 0))
    return pl.pallas_call(
        body, name="loss_grad", grid=(T // tm,),
        in_specs=[tile, tile], out_specs=[pl.BlockSpec((8, LANES), lambda i: (0, 0)), tile],
        out_shape=[jax.ShapeDtypeStruct((8, LANES), f32), jax.ShapeDtypeStruct((T, D), f32)],
        compiler_params=_cp(("arbitrary",)),
    )(y, tgt)


def ffn_down_bwd(dx2, r2, g2, wd, gate, up):
    T = dx2.shape[0]
    tm = min(TM_FFB, T)

    def body(dx_r, r_r, g_r, w_r, ga_r, up_r, dr_o, dg_o, du_o, dlg_o, dlb_o, hs):
        i = pl.program_id(0)
        xhat, rstd = _ln_stats(r_r[...])
        dx = dx_r[...]
        _acc_rows(dlg_o, i == 0, dx * xhat)
        _acc_rows(dlb_o, i == 0, dx)
        dr = _ln_bwd(dx, xhat, rstd, g_r[...])
        dr_o[...] = dr
        drb = dr.astype(MX)
        for k in range(NCHIP):
            hs[...] = _dot(drb, w_r[k], NT)
            for r in range(0, tm, ROW_CHUNK):
                rows = pl.ds(r, ROW_CHUNK)
                dhh, gate_v, up_v = hs[rows, :], ga_r[k, rows, :].astype(f32), up_r[k, rows, :].astype(f32)
                sg = _sigmoid(gate_v)
                dg_o[k, rows, :] = (dhh * up_v * sg * (1.0 + gate_v * (1.0 - sg))).astype(dg_o.dtype)
                du_o[k, rows, :] = (dhh * gate_v * sg).astype(du_o.dtype)

    tile = pl.BlockSpec((tm, D), lambda i: (i, 0))
    vec = pl.BlockSpec((1, D), lambda i: (0, 0))
    blk = pl.BlockSpec((NCHIP, tm, FB), lambda i: (0, i, 0))
    return pl.pallas_call(
        body, name="ffn_down_bwd", grid=(T // tm,),
        in_specs=[tile, tile, vec, pl.BlockSpec((NCHIP, FB, D), lambda i: (0, 0, 0)), blk, blk],
        out_specs=[tile, blk, blk, vec, vec],
        out_shape=[jax.ShapeDtypeStruct((T, D), f32)] + [jax.ShapeDtypeStruct((NCHIP, T, FB), MX)] * 2
        + [jax.ShapeDtypeStruct((1, D), f32)] * 2,
        scratch_shapes=[pltpu.VMEM((tm, FB), f32)],
        compiler_params=_cp(("arbitrary",)),
    )(dx2, r2, g2, wd, gate, up)


def ffn_up_bwd(dr2, dgate, dup, wg, wu, r1, g1):
    T = dr2.shape[0]
    tm = min(TM_FFB, T)

    def body(dr2_r, dg_r, du_r, wg_r, wu_r, r1_r, g_r, dr1_o, dlg_o, dlb_o):
        i = pl.program_id(0)
        dx = ALPHA * dr2_r[...]
        for k in range(NCHIP):
            dx = dx + _dot(dg_r[k], wg_r[k], NT) + _dot(du_r[k], wu_r[k], NT)
        xhat, rstd = _ln_stats(r1_r[...])
        _acc_rows(dlg_o, i == 0, dx * xhat)
        _acc_rows(dlb_o, i == 0, dx)
        dr1_o[...] = _ln_bwd(dx, xhat, rstd, g_r[...])

    tile = pl.BlockSpec((tm, D), lambda i: (i, 0))
    vec = pl.BlockSpec((1, D), lambda i: (0, 0))
    blk = pl.BlockSpec((NCHIP, tm, FB), lambda i: (0, i, 0))
    wspec = pl.BlockSpec((NCHIP, D, FB), lambda i: (0, 0, 0))
    return pl.pallas_call(
        body, name="ffn_up_bwd", grid=(T // tm,),
        in_specs=[tile, blk, blk, wspec, wspec, tile, vec],
        out_specs=[tile, vec, vec],
        out_shape=[jax.ShapeDtypeStruct((T, D), f32)] + [jax.ShapeDtypeStruct((1, D), f32)] * 2,
        compiler_params=_cp(("arbitrary",)),
    )(dr2, dgate, dup, wg, wu, r1, g1)


def mix_out_bwd(dr1, proj, mabc, wo, pa, pb, pc):
    T = dr1.shape[0]
    tm = min(TM_MIX, T)

    def body(dr_r, gt, mabc_r, wo_r, pa_r, pb_r, pc_r, dmabc_o, dgt_o, dya_o, dyb_o, dyc_o):
        dm = _dot(dr_r[...].astype(MX), wo_r[...], NT)
        dmx = []
        for j in range(3):
            s = _sigmoid(gt[:, j * D:(j + 1) * D].astype(f32))
            v = (dm * s).astype(MX)
            dmx.append(v)
            dmabc_o[:, j * D:(j + 1) * D] = v
            dgt_o[:, j * D:(j + 1) * D] = (dm * mabc_r[:, j * D:(j + 1) * D].astype(f32) * s * (1.0 - s)).astype(dgt_o.dtype)
        dya_o[...] = _dot(dmx[0], pa_r[...], NT)
        dyb = jnp.zeros((tm, AO), f32)
        for k in range(NCHIP):
            dyb = dyb + _dot(dmx[1][:, k * (D // NCHIP):(k + 1) * (D // NCHIP)], pb_r[k], NT)
        dyb_o[...] = dyb
        dyc_o[...] = _dot(dmx[2], pc_r[...], NT)

    full = lambda shape: pl.BlockSpec(shape, lambda i: (0,) * len(shape))
    tile = lambda w: pl.BlockSpec((tm, w), lambda i: (i, 0))
    return pl.pallas_call(
        body, name="mix_out_bwd", grid=(T // tm,),
        in_specs=[tile(D), tile(3 * D), tile(3 * D), full((D, D)), full((D, D)), full((NCHIP, AO, D // NCHIP)), full((D, D))],
        out_specs=[tile(3 * D), tile(3 * D), tile(D), tile(AO), tile(D)],
        out_shape=[jax.ShapeDtypeStruct((T, 3 * D), MX), jax.ShapeDtypeStruct((T, 3 * D), MX),
                   jax.ShapeDtypeStruct((T, D), f32), jax.ShapeDtypeStruct((T, AO), f32), jax.ShapeDtypeStruct((T, D), f32)],
        compiler_params=_cp(("parallel",), 56),
    )(dr1, proj, mabc, wo, pa, pb, pc)


def transpose_cast(x):
    T = x.shape[0]
    tm = min(512, T)

    def body(x_r, o_r):
        o_r[...] = x_r[...].T.astype(o_r.dtype)

    return pl.pallas_call(
        body, name="transpose_cast", grid=(T // tm,),
        in_specs=[pl.BlockSpec((tm, D), lambda i: (i, 0))], out_specs=pl.BlockSpec((D, tm), lambda i: (0, i)),
        out_shape=jax.ShapeDtypeStruct((D, T), MX), compiler_params=_cp(("parallel",)),
    )(x)


def tn_matmul(name, a, b, a_spec, b_spec, out_shape, out_spec, grid, a_is_t=False):
    nt = len(grid) - 1

    def body(a_r, b_r, o_r):
        @pl.when(pl.program_id(nt) == 0)
        def _():
            o_r[...] = jnp.zeros_like(o_r)
        av = a_r[...].reshape(a_r.shape[-2:]).astype(MX)
        bv = b_r[...].reshape(b_r.shape[-2:]).astype(MX)
        o_r[...] += _dot(av, bv, None if a_is_t else TN).reshape(o_r.shape)

    return pl.pallas_call(
        body, name=name, grid=grid, in_specs=[a_spec, b_spec], out_specs=out_spec,
        out_shape=jax.ShapeDtypeStruct(out_shape, f32),
        compiler_params=_cp(("parallel",) * nt + ("arbitrary",), 56),
    )(a, b)


def attn_pre_bwd(dyb, os_, lses, ones):
    T = dyb.shape[0]
    tm = TM_FOLD

    def body(dy_r, o0, o1, o2, l0, l1, l2, ones_r, d0, d1, d2, f0, f1, f2, nat):
        o = [_unfold_in(nat, r, d) for r, (_, d) in zip((o0, o1, o2), GROUPS)]
        ls = [_unfold_in(nat, r, d) for r, (_, d) in zip((l0, l1, l2), GROUPS)]
        w = _group_weights(ls)
        dy = dy_r[...]
        t = dy * (w[0] * o[0] + w[1] * o[1] + w[2] * o[2])
        hi = t.astype(MX)
        lo = (t - hi.astype(f32)).astype(MX)
        c = _dot(hi, ones_r[...]) + _dot(lo, ones_r[...])
        for wg, do_o, df_o, (_, d) in zip(w, (d0, d1, d2), (f0, f1, f2), GROUPS):
            _fold_out(nat, wg * dy, do_o, d)
            _fold_out(nat, -wg * c, df_o, d)

    specs = _fold_specs(T, tm)
    return pl.pallas_call(
        body, name="attn_pre_bwd", grid=(T // tm,),
        in_specs=[pl.BlockSpec((tm, AO), lambda i: (i, 0))] + specs + specs + [pl.BlockSpec((AO, AO), lambda i: (0, 0))],
        out_specs=specs + specs,
        out_shape=[jax.ShapeDtypeStruct((d, T // d, AO), MX) for _, d in GROUPS]
        + [jax.ShapeDtypeStruct((d, T // d, AO), f32) for _, d in GROUPS],
        scratch_shapes=[pltpu.VMEM((AO // LANES, tm, LANES), f32)],
        compiler_params=_cp(("parallel",)),
    )(dyb, *os_, *lses, ones)


def _head_ones():
    i = jnp.arange(AO) // HD
    return (i[:, None] == i[None, :]).astype(MX)


def attn_bwd(qf, kf, vf, dof, lse, df, g, nb):
    T = qf.shape[0]

    def body(q_ref, k_ref, v_ref, do_ref, l_ref, d_ref, dq_ref, dk_ref, dv_ref):
        prev_m, cur_m = _window_masks()

        def head_col(ref, r0):
            v = ref[pl.ds(r0, BLK), :]
            return jnp.concatenate([v[:, 0:1], v[:, HD:HD + 1]], axis=0)

        def step(b, carry):
            dk_c, dv_c = carry
            r0 = pl.multiple_of(b * BLK, BLK)
            rp = pl.multiple_of(jnp.maximum(b - 1, 0) * BLK, BLK)
            qs, dos = _stack_heads(q_ref[pl.ds(r0, BLK), :]), _stack_heads(do_ref[pl.ds(r0, BLK), :])
            k2, v2 = _two_blocks(k_ref, b), _two_blocks(v_ref, b)
            valid = cur_m | (prev_m & ((b % nb) != 0))
            p = jnp.where(valid, jnp.exp(_dot(qs, k2, NT) - head_col(l_ref, r0)), 0.0)
            ds = (p * (_dot(dos, v2, NT) + head_col(d_ref, r0))).astype(MX)
            dq_ref[pl.ds(r0, BLK), :] = _unstack_heads(_dot(ds, k2)).astype(dq_ref.dtype)
            dk2 = _dot(ds, qs, TN)
            dv2 = _dot(p.astype(MX), dos, TN)
            dk_ref[pl.ds(rp, BLK), :] = (dk_c + dk2[:BLK]).astype(dk_ref.dtype)
            dv_ref[pl.ds(rp, BLK), :] = (dv_c + dv2[:BLK]).astype(dv_ref.dtype)
            return dk2[BLK:], dv2[BLK:]

        zero = jnp.zeros((BLK, LANES), f32)

        def two_steps(i, carry):
            return step(2 * i + 1, step(2 * i, carry))

        dk_c, dv_c = lax.fori_loop(0, T // BLK // 2, two_steps, (zero, zero))
        dk_ref[pl.ds(T - BLK, BLK), :] = dk_c.astype(dk_ref.dtype)
        dv_ref[pl.ds(T - BLK, BLK), :] = dv_c.astype(dv_ref.dtype)

    spec = pl.BlockSpec((T, LANES), lambda j: (0, j))
    return pl.pallas_call(
        body, name=f"attn_bwd{g}", grid=(AO // LANES,),
        in_specs=[spec] * 6, out_specs=[spec] * 3,
        out_shape=[jax.ShapeDtypeStruct((T, AO), MX)] * 3,
        compiler_params=_cp(("parallel",), 60),
    )(qf, kf, vf, dof, lse, df)


def unfold_rope_bwd(dqf, dkf, dvf, cos_t, sin_t, g, d):
    T = dqf.shape[0] * dqf.shape[1]
    tm = TM_FOLD

    def body(q_r, k_r, v_r, c_ref, s_ref, o_ref, nat):
        cos, sin = _tile4(c_ref[...]), _tile4(s_ref[...])
        for part, ref, scale in ((0, q_r, HD ** -0.5), (1, k_r, 1.0), (2, v_r, None)):
            x = _unfold_in(nat, ref, d)
            if scale is not None:
                x = (x * cos - _swap_halves(x) * sin) * scale
            o_ref[:, part * AO:(part + 1) * AO] = x.astype(o_ref.dtype)

    fold_spec = pl.BlockSpec((d, tm // d, AO), lambda i: (0, i, 0))
    tab = pl.BlockSpec((tm, LANES), lambda i: (i, 0))
    return pl.pallas_call(
        body, name=f"unfold_rope_bwd{g}", grid=(T // tm,),
        in_specs=[fold_spec] * 3 + [tab, tab],
        out_specs=pl.BlockSpec((tm, 3 * AO), lambda i: (i, 0)),
        out_shape=jax.ShapeDtypeStruct((T, 3 * AO), MX),
        scratch_shapes=[pltpu.VMEM((AO // LANES, tm, LANES), f32)],
        compiler_params=_cp(("parallel",)),
    )(dqf, dkf, dvf, cos_t, sin_t)


def conv_bwd(dya, proj, conv_w):
    T = dya.shape[0]
    tm = TM_AC
    last = T // tm - 1

    def body(dy_r, bch, hprev, dy_next, b_next, cw, d_o, dw_o, zs, ds):
        i = pl.program_id(0)
        pb = bch[...].astype(f32)
        bp, cp, hp = pb[:, :D], pb[:, D:2 * D], pb[:, 2 * D:]
        z = cp * hp
        hz = hprev[:, :D].astype(f32) * hprev[:, D:].astype(f32)
        zs[0:HALO, :] = jnp.where(i > 0, hz, 0.0)
        zs[HALO:HALO + tm, :] = z
        z2, z1 = zs[HALO - 2:HALO - 2 + tm, :], zs[HALO - 1:HALO - 1 + tm, :]
        cv = cw[0:1, :] * z2 + cw[1:2, :] * z1 + cw[2:3, :] * z
        dy = dy_r[...]
        dcv = dy * bp
        ds[0:tm, :] = dcv
        ds[tm:tm + HALO, :] = jnp.where(i < last, dy_next[...] * b_next[...].astype(f32), 0.0)
        dz = cw[2:3, :] * dcv + cw[1:2, :] * ds[1:1 + tm, :] + cw[0:1, :] * ds[2:2 + tm, :]
        d_o[:, :D] = (dy * cv).astype(d_o.dtype)
        d_o[:, D:2 * D] = (dz * hp).astype(d_o.dtype)
        d_o[:, 2 * D:] = (dz * cp).astype(d_o.dtype)

        @pl.when(i == 0)
        def _():
            dw_o[...] = jnp.zeros_like(dw_o)
        dw_o[0:1, :] += jnp.sum(dcv * z2, axis=0, keepdims=True)
        dw_o[1:2, :] += jnp.sum(dcv * z1, axis=0, keepdims=True)
        dw_o[2:3, :] += jnp.sum(dcv * z, axis=0, keepdims=True)

    nh = tm // HALO
    return pl.pallas_call(
        body, name="conv_bwd", grid=(T // tm,),
        in_specs=[pl.BlockSpec((tm, D), lambda i: (i, 0)), pl.BlockSpec((tm, 3 * D), lambda i: (i, 1)),
                  pl.BlockSpec((HALO, 2 * D), lambda i: (jnp.maximum(i * nh - 1, 0), 2)),
                  pl.BlockSpec((HALO, D), lambda i: (jnp.minimum((i + 1) * nh, T // HALO - 1), 0)),
                  pl.BlockSpec((HALO, D), lambda i: (jnp.minimum((i + 1) * nh, T // HALO - 1), 3)),
                  pl.BlockSpec((3, D), lambda i: (0, 0))],
        out_specs=[pl.BlockSpec((tm, 3 * D), lambda i: (i, 0)), pl.BlockSpec((3, D), lambda i: (0, 0))],
        out_shape=[jax.ShapeDtypeStruct((T, 3 * D), MX), jax.ShapeDtypeStruct((3, D), f32)],
        scratch_shapes=[pltpu.VMEM((HALO + tm, D), f32), pltpu.VMEM((tm + HALO, D), f32)],
        compiler_params=_cp(("arbitrary",)),
    )(dya, proj, proj, dya, proj, conv_w)


def gmlp_bwd(dyc, proj, wst, bsx, lg, lb):
    T = dyc.shape[0]
    tm = TM_AC
    last = T // tm - 1

    def body(dy_r, u0, u1, v0, v1, ws, bs, lg_r, lb_r, d_o, dws_o, dbs_o, dlg_o, dlb_o, bacc):
        i = pl.program_id(0)
        up = jnp.concatenate([u0[...], u1[...]], axis=1).astype(f32)
        vp = jnp.concatenate([v0[...], v1[...]], axis=1).astype(f32)
        u, vn, xhat, rstd, sp = _gmlp_fwd(up, vp, ws, bs, lg_r[...], lb_r[...])
        dy = dy_r[...]
        d_o[:, :D] = (dy * sp * _gelu_grad(up)).astype(d_o.dtype)
        dsp = dy * u
        dspb, vnb = dsp.astype(MX), vn.astype(MX)

        @pl.when(i == 0)
        def _():
            dws_o[...] = jnp.zeros_like(dws_o)
            bacc[...] = jnp.zeros_like(bacc)

        rows = []
        for c in range(tm // BLK):
            r = slice(c * BLK, (c + 1) * BLK)
            cols = []
            for g in range(8):
                cs = slice(g * BLK, (g + 1) * BLK)
                dws_o[g] += _dot(dspb[r, cs], vnb[r, cs], NT)
                bacc[g] += dsp[r, cs]
                cols.append(_dot(ws[g], dspb[r, cs], TN))
            rows.append(jnp.concatenate(cols, axis=1))
        dvn = jnp.concatenate(rows, axis=0)
        _acc_rows(dlg_o, i == 0, dvn * xhat)
        _acc_rows(dlb_o, i == 0, dvn)
        d_o[:, D:] = (_ln_bwd(dvn, xhat, rstd, lg_r[...]) * _gelu_grad(vp)).astype(d_o.dtype)

        @pl.when(i == last)
        def _():
            row = lax.broadcasted_iota(jnp.int32, (BLK, BLK), 0)
            col = lax.broadcasted_iota(jnp.int32, (BLK, BLK), 1)
            ones = jnp.ones((8, BLK), MX)
            for g in range(8):
                dws_o[g] = jnp.where(col <= row, dws_o[g], 0.0)
                a = bacc[g]
                hi = a.astype(MX)
                lo = (a - hi.astype(f32)).astype(MX)
                dbs_o[g:g + 1, :] = (_dot(ones, hi, NT) + _dot(ones, lo, NT))[0:1, :]

    full = lambda shape: pl.BlockSpec(shape, lambda i: (0,) * len(shape))
    return pl.pallas_call(
        body, name="gmlp_bwd", grid=(T // tm,),
        in_specs=[pl.BlockSpec((tm, D), lambda i: (i, 0)), *_uv_specs(), full((8, BLK, BLK)), full((8, BLK, BLK)),
                  full((1, D)), full((1, D))],
        out_specs=[pl.BlockSpec((tm, 2 * D), lambda i: (i, 0)), full((8, BLK, BLK)), full((8, BLK)), full((1, D)), full((1, D))],
        out_shape=[jax.ShapeDtypeStruct((T, 2 * D), MX), jax.ShapeDtypeStruct((8, BLK, BLK), f32),
                   jax.ShapeDtypeStruct((8, BLK), f32), jax.ShapeDtypeStruct((1, D), f32), jax.ShapeDtypeStruct((1, D), f32)],
        scratch_shapes=[pltpu.VMEM((8, BLK, BLK), f32)],
        compiler_params=_cp(("arbitrary",)),
    )(dyc, proj, proj, proj, proj, wst, bsx, lg, lb)


PART_TILES = (6, 6, 3, 3, 3, 4)
PART_START = (0, 6, 12, 15, 18, 21)
TJ = 512


def _part_specs(tm, rows_axis):
    specs = []
    for n, s in zip(PART_TILES, PART_START):
        def imap(*idx, n=n, s=s):
            i, j = idx[rows_axis], idx[1 - rows_axis]
            inside = (j >= s) & (j < s + n)
            return (jnp.where(inside, i, 0), jnp.clip(j - s, 0, n - 1))
        specs.append(pl.BlockSpec((tm, TJ), imap))
    return specs


def _with_part(j, refs, fn):
    for r, n, s in zip(refs, PART_TILES, PART_START):
        @pl.when((j >= s) & (j < s + n))
        def _():
            fn(r[...])


def dx_in(dr1, parts, w, bias):
    T = dr1.shape[0]
    tm = min(1024, T)

    def body(dr_r, p0, p1, p2, p3, p4, p5, w_r, b_r, o_r):
        j = pl.program_id(1)

        @pl.when(j == 0)
        def _():
            o_r[...] = ALPHA * dr_r[...] + b_r[...]

        def acc(tile):
            o_r[...] += _dot(tile, w_r[...], NT)
        _with_part(j, (p0, p1, p2, p3, p4, p5), acc)

    return pl.pallas_call(
        body, name="dx_in", grid=(T // tm, NIN // TJ),
        in_specs=[pl.BlockSpec((tm, D), lambda i, j: (i, 0))] + _part_specs(tm, 0)
        + [pl.BlockSpec((D, TJ), lambda i, j: (0, j)), pl.BlockSpec((1, D), lambda i, j: (0, 0))],
        out_specs=pl.BlockSpec((tm, D), lambda i, j: (i, 0)),
        out_shape=jax.ShapeDtypeStruct((T, D), f32),
        compiler_params=_cp(("parallel", "arbitrary"), 56),
    )(dr1, *parts, w, bias)


def dw_in(x0t, parts):
    T = x0t.shape[1]
    tk = min(2048, T)

    def body(x_r, p0, p1, p2, p3, p4, p5, o_r):
        j, t = pl.program_id(0), pl.program_id(1)

        @pl.when(t == 0)
        def _():
            o_r[...] = jnp.zeros_like(o_r)

        def acc(tile):
            o_r[...] += _dot(x_r[...], tile)
        _with_part(j, (p0, p1, p2, p3, p4, p5), acc)

    return pl.pallas_call(
        body, name="dw_in", grid=(NIN // TJ, T // tk),
        in_specs=[pl.BlockSpec((D, tk), lambda j, t: (0, t))] + _part_specs(tk, 1),
        out_specs=pl.BlockSpec((D, TJ), lambda j, t: (0, j)),
        out_shape=jax.ShapeDtypeStruct((D, NIN), f32),
        compiler_params=_cp(("parallel", "arbitrary")),
    )(x0t, *parts)


def rope_tables(positions):
    half = HD // 2
    inv_freq = ROPE_THETA ** (-jnp.arange(half, dtype=f32) / half)
    ang = positions.astype(f32)[:, None] * inv_freq
    cos, sin = jnp.cos(ang), jnp.sin(ang)
    return jnp.tile(cos, (1, LANES // half)), jnp.tile(jnp.concatenate([-sin, sin], axis=1), (1, LANES // HD))


def _flat(a):
    return a.reshape(a.shape[0] * a.shape[1], a.shape[2])


def layer_fwd(x0, W, cos_t, sin_t):
    T = x0.shape[0]
    proj = mm_in(x0, W["w_in"], W["in_bias"])
    ya, yc = mix_ac_fwd(proj, W["conv_w"], W["wst"], W["bsx"], W["gmlp_ln_g"], W["gmlp_ln_b"])
    folded, os_, lses = [], [], []
    for g, (_, d) in enumerate(GROUPS):
        qf, kf, vf = fold_rope(proj, cos_t, sin_t, g, d)
        o, lse = attn_fwd(_flat(qf), _flat(kf), _flat(vf), g, T // d // BLK)
        folded.append((qf, kf, vf))
        os_.append(o.reshape(d, T // d, AO))
        lses.append(lse.reshape(d, T // d, AO))
    yb = combine_fwd(os_, lses)
    mabc, m, r1, x1 = mix_out_fwd(proj, ya, yb, yc, x0, W["p_a"], W["p_b"], W["p_c"], W["w_o"], W["ln1_g"], W["ln1_b"])
    gate, up, hh = ffn_up_fwd(x1, W["w_gate"], W["w_up"])
    r2, x2 = ffn_down_fwd(hh, W["w_down"], x1, W["ln2_g"], W["ln2_b"])
    saved = dict(x0=x0, proj=proj, ya=ya, yb=yb, yc=yc, folded=folded, os=os_, lses=lses, mabc=mabc, m=m, r1=r1,
                 x1=x1, gate=gate, up=up, hh=hh, r2=r2)
    return x2, saved


def layer_bwd(dx2, S, W, cos_t, sin_t, on_grads=None):
    T = dx2.shape[0]
    tk = min(2048, T)
    G = {}
    dr2, dgate, dup, G["ln2_g"], G["ln2_b"] = ffn_down_bwd(dx2, S["r2"], W["ln2_g"], W["w_down"], S["gate"], S["up"])
    blk_a = pl.BlockSpec((1, tk, FB), lambda k, t: (k, t, 0))
    row_b = pl.BlockSpec((tk, D), lambda k, t: (t, 0))
    G["w_down"] = tn_matmul("dw_down", S["hh"], dr2, blk_a, row_b, (NCHIP, FB, D),
                            pl.BlockSpec((1, FB, D), lambda k, t: (k, 0, 0)), (NCHIP, T // tk))
    for nm, dv in (("w_gate", dgate), ("w_up", dup)):
        G[nm] = tn_matmul("d" + nm, dv, S["x1"], blk_a, row_b, (NCHIP, FB, D),
                          pl.BlockSpec((1, FB, D), lambda k, t: (k, 0, 0)), (NCHIP, T // tk))
    dr1, G["ln1_g"], G["ln1_b"] = ffn_up_bwd(dr2, dgate, dup, W["w_gate"], W["w_up"], S["r1"], W["ln1_g"])
    dmabc, dgates, dya, dyb, dyc = mix_out_bwd(dr1, S["proj"], S["mabc"], W["w_o"], W["p_a"], W["p_b"], W["p_c"])
    one = (1, T // tk)
    full_o = pl.BlockSpec((D, D), lambda k, t: (0, 0))
    G["w_o"] = tn_matmul("dw_o", S["m"], dr1, row_b, row_b, (D, D), full_o, one)
    G["p_a"] = tn_matmul("dp_a", S["ya"], dmabc, row_b, pl.BlockSpec((tk, D), lambda k, t: (t, 0)), (D, D), full_o, one)
    G["p_c"] = tn_matmul("dp_c", S["yc"], dmabc, row_b, pl.BlockSpec((tk, D), lambda k, t: (t, 2)), (D, D), full_o, one)
    G["p_b"] = tn_matmul("dp_b", S["yb"], dmabc, pl.BlockSpec((tk, AO), lambda k, t: (t, 0)),
                         pl.BlockSpec((tk, D // NCHIP), lambda k, t: (t, NCHIP + k)), (NCHIP, AO, D // NCHIP),
                         pl.BlockSpec((1, AO, D // NCHIP), lambda k, t: (k, 0, 0)), (NCHIP, T // tk))
    conv_w = W["conv_w"]
    if on_grads is not None:
        conv_w = conv_w + on_grads({n: G[n] for n in BIG if n != "w_in"})
    dbch, G["conv_w"] = conv_bwd(dya, S["proj"], conv_w)
    duv, G["w_s"], G["b_s"], G["gmlp_ln_g"], G["gmlp_ln_b"] = gmlp_bwd(
        dyc, S["proj"], W["wst"], W["bsx"], W["gmlp_ln_g"], W["gmlp_ln_b"])
    ones = _head_ones()
    if on_grads is not None:
        small = {n: G[n] for n in VECS + ("b_s", "w_s", "conv_w")}
        ones = ones + on_grads(small).astype(MX)
    pre = attn_pre_bwd(dyb, S["os"], S["lses"], ones)
    dqkv = []
    for g, (_, d) in enumerate(GROUPS):
        qf, kf, vf = S["folded"][g]
        dqf, dkf, dvf = attn_bwd(_flat(qf), _flat(kf), _flat(vf), _flat(pre[g]), _flat(S["lses"][g]), _flat(pre[3 + g]),
                                 g, T // d // BLK)
        shp = (d, T // d, AO)
        dqkv.append(unfold_rope_bwd(dqf.reshape(shp), dkf.reshape(shp), dvf.reshape(shp), cos_t, sin_t, g, d))
    parts = (dgates, dbch, *dqkv, duv)
    G["w_in"] = dw_in(transpose_cast(S["x0"]), parts)
    bias = jnp.zeros((1, D), f32)
    if on_grads is not None:
        bias = bias + on_grads({"w_in": G["w_in"]})
    dx0 = dx_in(dr1, parts, W["w_in"], bias)
    started = on_grads({"dx": dx0}) if on_grads is not None else None
    return dx0, G, started


def prep_layer_weights(Wl):
    W = dict(Wl)
    tril = jnp.tril(jnp.ones((BLK, BLK), f32))
    W["wst"] = (Wl["w_s"] * tril[None]).astype(MX)
    W["bsx"] = jnp.broadcast_to(Wl["b_s"][:, :, None], (8, BLK, BLK))
    for n in ("gmlp_ln_g", "gmlp_ln_b", "ln1_g", "ln1_b", "ln2_g", "ln2_b"):
        W[n] = Wl[n].reshape(1, D)
    W["in_bias"] = jnp.zeros((1, NIN), f32) + Wl.get("after", 0.0)
    return W


def local_step(x, positions, target, layers, on_grads=None):
    cos_t, sin_t = rope_tables(positions)
    Ws, saved = [], []
    h = x
    for Wl in layers:
        Ws.append(prep_layer_weights(Wl(h) if callable(Wl) else Wl))
        h, S = layer_fwd(h, Ws[-1], cos_t, sin_t)
        saved.append(S)
    lsum, dh = loss_grad(h, target)
    if on_grads is not None:
        on_grads(len(Ws), {"loss": lsum})
    grads = [None] * len(Ws)
    started = None
    for l in reversed(range(len(Ws))):
        W = Ws[l]
        if started is not None:
            W = dict(W, ln2_g=W["ln2_g"] + started)
        hook = functools.partial(on_grads, l) if on_grads is not None else None
        dh, grads[l], started = layer_bwd(dh, saved[l], W, cos_t, sin_t, hook)
    return lsum, dh, grads


MESH = pl.DeviceIdType.MESH
ANY = pl.BlockSpec(memory_space=pl.ANY)
BIG = ("w_in", "w_gate", "w_up", "w_down", "p_a", "p_b", "p_c", "w_o")
NBIG = len(BIG)


def _place():
    x, y, c = lax.axis_index("x"), lax.axis_index("y"), lax.axis_index("c")
    return x, y, c, 2 * x + y


def _rcopy(src, dst, send, recv, dev):
    return pltpu.make_async_remote_copy(src_ref=src, dst_ref=dst, send_sem=send, recv_sem=recv, device_id=dev,
                                        device_id_type=MESH)


def _cols(ref, k, width):
    start = k * width if isinstance(k, int) else pl.multiple_of(k * width, LANES)
    return ref.at[:, pl.ds(start, width)]


CHUNK_BYTES = 1 << 20


def _pieces(shape, itemsize, nbytes=CHUNK_BYTES):
    rows, cols = shape[-2], shape[-1]
    per = max(16, nbytes // (cols * itemsize) // 16 * 16)
    out = []
    for lead in (range(shape[0]) if len(shape) == 3 else (None,)):
        for r in range(0, rows, per):
            sl = (pl.ds(r, min(per, rows - r)), slice(None))
            out.append(sl if lead is None else (lead,) + sl)
    return out


def _start_pieces(src, dst, make, nbytes=CHUNK_BYTES):
    for idx in _pieces(src.shape, jnp.dtype(src.dtype).itemsize, nbytes):
        make(src.at[idx], dst.at[idx]).start()


def gather_halves(shards):
    n = len(shards)

    def body(*refs):
        srcs, dsts = refs[:n], refs[n:2 * n]
        send, recv, own_send, own_recv = refs[2 * n:]
        x, y, c, k = _place()
        sib = (x, y, 1 - c)
        chips = [(1 - x, y), (x, 1 - y), (1 - x, 1 - y)]

        def slot(a, layer, pos):
            if a == 0:
                return _cols(dsts[0].at[layer], pos, WIN_SHARD)
            return dsts[a].at[pos, layer]

        def ici(a, j, src, dst):
            return _rcopy(src, dst, send.at[a, j], recv.at[a, j], (*chips[j], c))

        def d2d(a, j, src, dst):
            return _rcopy(src, dst, send.at[a, 3 + j], recv.at[a, 3 + j], sib)

        def own(a, layer, src, dst):
            return _rcopy(src, dst, own_send.at[a, layer], own_recv.at[a, layer], sib)

        for a in range(n):
            for j in range(3):
                _start_pieces(srcs[a].at[c], slot(a, c, k), functools.partial(ici, a, j))
        for a in range(n):
            for layer in range(DEPTH):
                _start_pieces(srcs[a].at[layer], slot(a, layer, k), functools.partial(own, a, layer))
        for a in range(n):
            for j, (cx, cy) in enumerate(chips):
                landed = slot(a, c, 2 * cx + cy)
                ici(a, j, landed, landed).wait_recv()
                _start_pieces(landed, landed, functools.partial(d2d, a, j))
        for a in range(n):
            for j, (cx, cy) in enumerate(chips):
                passed = slot(a, 1 - c, 2 * cx + cy)
                d2d(a, j, passed, passed).wait_recv()
                landed = slot(a, c, 2 * cx + cy)
                d2d(a, j, landed, landed).wait_send()
                ici(a, j, srcs[a].at[c], slot(a, c, k)).wait_send()
            for layer in range(DEPTH):
                own(a, layer, srcs[a].at[layer], slot(a, layer, k)).wait()

    outs = [jax.ShapeDtypeStruct((2, shards[0].shape[1], NIN), shards[0].dtype)]
    outs += [jax.ShapeDtypeStruct((NCHIP,) + s.shape, s.dtype) for s in shards[1:]]
    return pl.pallas_call(
        body, name="gather_halves", in_specs=[ANY] * n, out_specs=[ANY] * n, out_shape=outs,
        scratch_shapes=[pltpu.SemaphoreType.DMA((n, 6)), pltpu.SemaphoreType.DMA((n, 6)),
                        pltpu.SemaphoreType.DMA((n, DEPTH)), pltpu.SemaphoreType.DMA((n, DEPTH))],
    )(*shards)


def _gather_slot(dst, pos):
    return _cols(dst, pos, WIN_SHARD) if len(dst.shape) == 2 else dst.at[pos]


def _gather_copy(a, j, src, dst, send, recv, dev):
    return _rcopy(src, dst, send.at[a * NCHIP + j], recv.at[a * NCHIP + j], dev)


def gather_start(tag, shards, after):
    n = len(shards)

    def body(*refs):
        srcs, dsts = refs[:n], refs[n:2 * n]
        send, recv = refs[2 * n + len(after)], refs[2 * n + len(after) + 1]
        token = refs[-1]
        x, y, c, k = _place()
        peers = [(1 - x, y, c), (x, 1 - y, c), (1 - x, 1 - y, c), (x, y, 1 - c)]
        for a in range(n):
            for j, dev in enumerate(peers):
                _start_pieces(srcs[a], _gather_slot(dsts[a], k),
                              lambda s, d, a=a, j=j, dev=dev: _gather_copy(a, j, s, d, send, recv, dev))
        token[...] = jnp.zeros_like(token)

    gathered = [lax.empty((D, NIN) if i == 0 else (NCHIP,) + s.shape, s.dtype) for i, s in enumerate(shards)]
    ops = [pltpu.with_memory_space_constraint(v, pltpu.HBM) for v in list(shards) + gathered]
    sem = pltpu.SemaphoreType.DMA((n * NCHIP,))
    res = pl.pallas_call(
        body, name=f"gather_start{tag}", in_specs=[HBM] * (2 * n) + [ANY] * len(after),
        out_specs=[SEMS, SEMS] + [HBM] * (2 * n) + [pl.BlockSpec(memory_space=pltpu.VMEM)],
        out_shape=[sem, sem] + [pltpu.HBM(v.shape, v.dtype) for v in ops] + [jax.ShapeDtypeStruct((8, LANES), f32)],
        input_output_aliases={i: 2 + i for i in range(2 * n)},
        compiler_params=pltpu.CompilerParams(has_side_effects=EFFECT),
    )(*ops, *after)
    return res[0], res[1], res[2:2 + n], res[2 + n:2 + 2 * n], res[-1]


def gather_wait(tag, send, recv, shards, gathered, after):
    n = len(shards)

    def body(*refs):
        srcs, dsts = refs[:n], refs[n:2 * n]
        send_r, recv_r = refs[2 * n], refs[2 * n + 1]
        x, y, c, k = _place()
        peers = [(1 - x, y, c), (x, 1 - y, c), (1 - x, 1 - y, c), (x, y, 1 - c)]
        for a in range(n):
            for j, dev in enumerate(peers):
                _gather_copy(a, j, srcs[a], _gather_slot(dsts[a], k), send_r, recv_r, dev).wait_send()
                pos = 2 * dev[0] + dev[1]
                _gather_copy(a, j, srcs[a], _gather_slot(dsts[a], pos), send_r, recv_r, dev).wait_recv()

    ops = list(shards) + list(gathered)
    res = pl.pallas_call(
        body, name=f"gather_wait{tag}", in_specs=[HBM] * (2 * n) + [SEMS, SEMS] + [ANY] * len(after),
        out_specs=[HBM] * (2 * n), out_shape=[pltpu.HBM(v.shape, v.dtype) for v in ops],
        input_output_aliases={i: i for i in range(2 * n)},
        compiler_params=pltpu.CompilerParams(has_side_effects=EFFECT),
    )(*ops, send, recv, *after)
    return res[n:]


def _half(ref, h):
    rows = ref.shape[-2] // 2
    start = pl.multiple_of(h * rows, 16)
    if len(ref.shape) == 2:
        return ref.at[pl.ds(start, rows), :]
    return ref.at[:, pl.ds(start, rows), :]


HBM = pl.BlockSpec(memory_space=pltpu.HBM)
SEMS = pl.BlockSpec(memory_space=pltpu.SEMAPHORE)
EFFECT = pltpu.SideEffectType.DATAFLOW_SIDE_EFFECTING


def rs_pair_start(tag, grads):
    n = len(grads)

    def body(*refs):
        g, theirs = refs[:n], refs[n:2 * n]
        send, recv = refs[2 * n], refs[2 * n + 1]
        x, y, c, _ = _place()
        for a in range(n):
            _start_pieces(_half(g[a], 1 - c), theirs[a],
                          lambda s, d, a=a: _rcopy(s, d, send.at[a], recv.at[a], (x, y, 1 - c)))
        refs[-1][...] = jnp.zeros_like(refs[-1])

    lands = [lax.empty(g.shape[:-2] + (g.shape[-2] // 2, g.shape[-1]), g.dtype) for g in grads]
    ops = [pltpu.with_memory_space_constraint(v, pltpu.HBM) for v in list(grads) + lands]
    sem = pltpu.SemaphoreType.DMA((n,))
    res = pl.pallas_call(
        body, name=f"rs_pair_start{tag}", in_specs=[HBM] * (2 * n),
        out_specs=[SEMS, SEMS] + [HBM] * (2 * n) + [pl.BlockSpec(memory_space=pltpu.VMEM)],
        out_shape=[sem, sem] + [pltpu.HBM(v.shape, v.dtype) for v in ops] + [jax.ShapeDtypeStruct((8, LANES), f32)],
        input_output_aliases={i: 2 + i for i in range(2 * n)},
        compiler_params=pltpu.CompilerParams(has_side_effects=EFFECT),
    )(*ops)
    return res[0], res[1], res[2:2 + n], res[2 + n:2 + 2 * n], res[-1]


def rs_pair_wait(tag, send, recv, grads, theirs, after):
    n = len(grads)

    def body(*refs):
        g, land = refs[:n], refs[n:2 * n]
        send_r, recv_r = refs[2 * n], refs[2 * n + 1]
        x, y, c, _ = _place()
        for a in range(n):
            cp = _rcopy(_half(g[a], 1 - c), land[a], send_r.at[a], recv_r.at[a], (x, y, 1 - c))
            cp.wait_send()
            cp.wait_recv()

    ops = list(grads) + list(theirs)
    res = pl.pallas_call(
        body, name=f"rs_pair_wait{tag}", in_specs=[HBM] * (2 * n) + [SEMS, SEMS] + [ANY] * len(after),
        out_specs=[HBM] * (2 * n), out_shape=[pltpu.HBM(v.shape, v.dtype) for v in ops],
        input_output_aliases={i: i for i in range(2 * n)},
        compiler_params=pltpu.CompilerParams(has_side_effects=EFFECT),
    )(*ops, send, recv, *after)
    return res[:n], res[n:]


def _chip_piece(ref, k):
    return _cols(ref, k, WIN_SHARD) if len(ref.shape) == 2 else ref.at[k]


def _chip_copy(a, k, src, dst, send, recv, me, c):
    return _rcopy(src, dst, send.at[a * NCHIP + k], recv.at[a * NCHIP + me], (k // 2, k % 2, c))


def rs_chips_start(tag, sums):
    n = len(sums)

    def pshape(s):
        return (NCHIP, s[0], WIN_SHARD) if len(s) == 2 else s

    def body(*refs):
        s, land = refs[:n], refs[n:2 * n]
        send, recv = refs[2 * n], refs[2 * n + 1]
        token = refs[-1]
        x, y, c, me = _place()
        for k in range(NCHIP):
            @pl.when(me != k)
            def _():
                for a in range(n):
                    _start_pieces(_chip_piece(s[a], k), land[a].at[me],
                                  lambda src, dst, a=a: _chip_copy(a, k, src, dst, send, recv, me, c))
        token[...] = jnp.zeros_like(token)

    lands = [lax.empty(pshape(v.shape), v.dtype) for v in sums]
    ops = [pltpu.with_memory_space_constraint(v, pltpu.HBM) for v in list(sums) + lands]
    sem = pltpu.SemaphoreType.DMA((n * NCHIP,))
    res = pl.pallas_call(
        body, name=f"rs_chips_start{tag}", in_specs=[HBM] * (2 * n),
        out_specs=[SEMS, SEMS] + [HBM] * (2 * n) + [pl.BlockSpec(memory_space=pltpu.VMEM)],
        out_shape=[sem, sem] + [pltpu.HBM(v.shape, v.dtype) for v in ops] + [jax.ShapeDtypeStruct((8, LANES), f32)],
        input_output_aliases={i: 2 + i for i in range(2 * n)},
        compiler_params=pltpu.CompilerParams(has_side_effects=EFFECT),
    )(*ops)
    return res[0], res[1], res[2:2 + n], res[2 + n:2 + 2 * n], res[-1]


def rs_chips_wait(tag, send, recv, sums, lands, after):
    n = len(sums)

    def body(*refs):
        s, land = refs[:n], refs[n:2 * n]
        send_r, recv_r = refs[2 * n], refs[2 * n + 1]
        x, y, c, me = _place()
        for k in range(NCHIP):
            @pl.when(me != k)
            def _():
                for a in range(n):
                    piece = _chip_piece(s[a], k)
                    _chip_copy(a, k, piece, land[a].at[me], send_r, recv_r, me, c).wait_send()
                    _rcopy(piece, land[a].at[k], send_r.at[a * NCHIP + k], recv_r.at[a * NCHIP + k],
                           (k // 2, k % 2, c)).wait_recv()

    ops = list(sums) + list(lands)
    res = pl.pallas_call(
        body, name=f"rs_chips_wait{tag}", in_specs=[HBM] * (2 * n) + [SEMS, SEMS] + [ANY] * len(after),
        out_specs=[HBM] * (2 * n), out_shape=[pltpu.HBM(v.shape, v.dtype) for v in ops],
        input_output_aliases={i: i for i in range(2 * n)},
        compiler_params=pltpu.CompilerParams(has_side_effects=EFFECT),
    )(*ops, send, recv, *after)
    return res[:n], res[n:]


def rs_join(tag, halves):
    n = len(halves)

    def body(*refs):
        h, other = refs[:n], refs[n:2 * n]
        send, recv = refs[2 * n:]
        x, y, c, _ = _place()

        def give(a, s, d):
            return _rcopy(s, d, send.at[a], recv.at[a], (x, y, 1 - c))

        for a in range(n):
            _start_pieces(h[a], other[a], functools.partial(give, a))
        for a in range(n):
            give(a, h[a], other[a]).wait()

    outs = [jax.ShapeDtypeStruct(v.shape, v.dtype) for v in halves]
    return pl.pallas_call(
        body, name=f"rs_join{tag}", in_specs=[ANY] * n, out_specs=[ANY] * n, out_shape=outs,
        scratch_shapes=[pltpu.SemaphoreType.DMA((n,))] * 2,
    )(*halves)


def _row_tile(rows, cols, itemsize=4, target=2 << 20):
    best = 8
    for t in range(8, rows + 1, 8):
        if rows % t == 0 and t * cols * itemsize <= target:
            best = t
    return best


GRAD_WIRE = jnp.bfloat16


def add_n(name, terms, out_dtype=f32):
    shape = terms[0].shape
    cols = shape[-1]
    rows = math.prod(shape[:-1])
    tr = _row_tile(rows, cols)

    def body(*refs):
        acc = refs[0][...]
        for r in refs[1:-1]:
            acc = acc + r[...]
        refs[-1][...] = acc.astype(out_dtype)

    tile = pl.BlockSpec((tr, cols), lambda i: (i, 0))
    out = pl.pallas_call(
        body, name=name, grid=(rows // tr,), in_specs=[tile] * len(terms), out_specs=tile,
        out_shape=jax.ShapeDtypeStruct((rows, cols), out_dtype), compiler_params=_cp(("parallel",)),
    )(*[t.reshape(rows, cols) for t in terms])
    return out.reshape(shape)


def add_chips(name, land, own):
    _, rows, cols = land.shape
    tr = _row_tile(rows, cols, target=1 << 20)

    def body(land_r, own_r, o_r):
        me = 2 * lax.axis_index("x") + lax.axis_index("y")
        for k in range(NCHIP):
            @pl.when(me == k)
            def _():
                acc = None
                for j in range(NCHIP):
                    t = (own_r[...] if j == k else land_r[j]).astype(f32)
                    acc = t if acc is None else acc + t
                o_r[...] = acc

    tile = pl.BlockSpec((tr, cols), lambda i: (i, 0))
    return pl.pallas_call(
        body, name=name, grid=(rows // tr,), in_specs=[pl.BlockSpec((NCHIP, tr, cols), lambda i: (0, i, 0)), tile],
        out_specs=tile, out_shape=jax.ShapeDtypeStruct((rows, cols), f32), compiler_params=_cp(("parallel",)),
    )(land, own)


def reduce_scatter_pair(tag, G):
    names = tuple(G)
    grads = [G[n] if G[n].ndim == 3 or n == "w_in" else G[n].reshape(NCHIP, D // NCHIP, D) for n in names]
    send, recv, grads, theirs, token = rs_pair_start(tag, grads)
    return (tag, names, send, recv, grads, theirs), token[0, 0]


def reduce_scatter_chips(state, after):
    c = lax.axis_index("c")
    tag, names, send, recv, grads, theirs = state
    grads, theirs = rs_pair_wait(tag, send, recv, grads, theirs, after)
    sums = []
    for n, g, t in zip(names, grads, theirs):
        rows = g.shape[-2] // 2
        mine = lax.dynamic_slice_in_dim(g, c * rows, rows, axis=g.ndim - 2)
        sums.append(add_n(f"rs_add_pair{tag}_{n}", [mine, t], GRAD_WIRE))
    send, recv, sums, lands, token = rs_chips_start(tag, sums)
    return (tag, names, send, recv, sums, lands), token[0, 0]


def reduce_scatter_finish(state, after):
    me = 2 * lax.axis_index("x") + lax.axis_index("y")
    tag, names, send, recv, sums, lands = state
    sums, landed = rs_chips_wait(tag, send, recv, sums, lands, after)
    halves = []
    for n, s, v in zip(names, sums, landed):
        own = lax.dynamic_slice_in_dim(s, me * WIN_SHARD, WIN_SHARD, axis=1) if s.ndim == 2 else \
            lax.dynamic_index_in_dim(s, me, 0, keepdims=False)
        halves.append(add_chips(f"rs_add_chips{tag}_{n}", v, own))
    return dict(zip(names, zip(halves, rs_join(tag, halves))))


NDEV = 8


def _small_copy(r, src, dst, send, recv, x, y, c):
    return _rcopy(src, dst, send.at[r - 1], recv.at[r - 1], (x ^ (r >> 2), y ^ ((r >> 1) & 1), c ^ (r & 1)))


def small_start(pack):
    def body(p, land, send, recv, p_thru, land_thru, token):
        x, y, c, _ = _place()
        me = 4 * x + 2 * y + c
        for r in range(1, NDEV):
            _start_pieces(p, land.at[me], lambda s, d, r=r: _small_copy(r, s, d, send, recv, x, y, c), 128 << 10)
        token[...] = jnp.zeros_like(token)

    ops = [pltpu.with_memory_space_constraint(v, pltpu.HBM) for v in (pack, lax.empty((NDEV,) + pack.shape, f32))]
    sem = pltpu.SemaphoreType.DMA((NDEV - 1,))
    return pl.pallas_call(
        body, name="small_start", in_specs=[HBM, HBM],
        out_specs=[SEMS, SEMS, HBM, HBM, pl.BlockSpec(memory_space=pltpu.VMEM)],
        out_shape=[sem, sem] + [pltpu.HBM(v.shape, v.dtype) for v in ops] + [jax.ShapeDtypeStruct((8, LANES), f32)],
        input_output_aliases={0: 2, 1: 3}, compiler_params=pltpu.CompilerParams(has_side_effects=EFFECT),
    )(*ops)


def small_wait(send, recv, pack, land, after):
    def body(p, land_r, send_r, recv_r, *rest):
        x, y, c, _ = _place()
        me = 4 * x + 2 * y + c
        for r in range(1, NDEV):
            _small_copy(r, p, land_r.at[me], send_r, recv_r, x, y, c).wait_send()
            src = 4 * (x ^ (r >> 2)) + 2 * (y ^ ((r >> 1) & 1)) + (c ^ (r & 1))
            _small_copy(r, p, land_r.at[src], send_r, recv_r, x, y, c).wait_recv()

    return pl.pallas_call(
        body, name="small_wait", in_specs=[HBM, HBM, SEMS, SEMS] + [ANY] * len(after), out_specs=[HBM, HBM],
        out_shape=[pltpu.HBM(pack.shape, f32), pltpu.HBM(land.shape, f32)], input_output_aliases={0: 0, 1: 1},
        compiler_params=pltpu.CompilerParams(has_side_effects=EFFECT),
    )(pack, land, send, recv, *after)


def small_sum(land, pack):
    def body(land_r, p_r, o_r):
        me = 4 * lax.axis_index("x") + 2 * lax.axis_index("y") + lax.axis_index("c")
        for k in range(NDEV):
            @pl.when(me == k)
            def _():
                acc = None
                for d in range(NDEV):
                    t = p_r[...] if d == k else land_r[d]
                    acc = t if acc is None else acc + t
                o_r[...] = acc

    vm = pl.BlockSpec(memory_space=pltpu.VMEM)
    return pl.pallas_call(
        body, name="small_sum", in_specs=[vm, vm], out_specs=vm, out_shape=jax.ShapeDtypeStruct(pack.shape, f32),
        compiler_params=pltpu.CompilerParams(vmem_limit_bytes=40 << 20),
    )(land, pack)


def _adamw_math(w, g, m, v):
    m = ADAM_B1 * m + (1.0 - ADAM_B1) * g
    v = ADAM_B2 * v + (1.0 - ADAM_B2) * (g * g)
    m_hat = m / (1.0 - ADAM_B1 ** ADAM_STEP)
    v_hat = v / (1.0 - ADAM_B2 ** ADAM_STEP)
    return -ADAM_LR * (m_hat / (jnp.sqrt(v_hat) + ADAM_EPS) + ADAM_WD * w), m, v


def adamw_big(name, halves, w, m, v):
    _, R, C = w.shape
    tr = _row_tile(R // 2, C, target=1 << 20)
    nt = R // 2 // tr

    def body(a0, b0, a1, b1, w_r, m_r, v_r, g_o, d_o, m_o, v_o):
        mine = pl.program_id(1) == lax.axis_index("c")
        g = jnp.where(pl.program_id(0) == 0, jnp.where(mine, a0[...], b0[...]), jnp.where(mine, a1[...], b1[...]))
        g_o[...] = g
        d_o[...], m_o[...], v_o[...] = _adamw_math(w_r[...], g, m_r[...], v_r[...])

    stk = pl.BlockSpec((None, tr, C), lambda l, h, i: (l, h * nt + i, 0))
    lay0 = pl.BlockSpec((tr, C), lambda l, h, i: (jnp.where(l == 0, i, nt - 1), 0))
    lay1 = pl.BlockSpec((tr, C), lambda l, h, i: (jnp.where(l == 0, 0, i), 0))
    return pl.pallas_call(
        body, name=name, grid=(DEPTH, 2, nt),
        in_specs=[lay0, lay0, lay1, lay1, stk, stk, stk],
        out_specs=[stk] * 4, out_shape=[jax.ShapeDtypeStruct(w.shape, f32)] * 4,
        compiler_params=_cp(("arbitrary", "arbitrary", "arbitrary")),
    )(*halves[0], *halves[1], w, m, v)


def adamw_small(name, g, w, m, v):
    def body(g_r, w_r, m_r, v_r, d_o, m_o, v_o):
        d_o[...], m_o[...], v_o[...] = _adamw_math(w_r[...], g_r[...], m_r[...], v_r[...])

    return pl.pallas_call(body, name=name, out_shape=[jax.ShapeDtypeStruct(w.shape, f32)] * 3)(g, w, m, v)


WEIGHTS = ("w_in", "conv_w", "gmlp_ln_g", "gmlp_ln_b", "w_s", "b_s", "p_a", "p_b", "p_c", "w_o", "ln1_g", "ln1_b",
           "w_gate", "w_up", "w_down", "ln2_g", "ln2_b")
VECS = ("ln1_g", "ln1_b", "ln2_g", "ln2_b", "gmlp_ln_g", "gmlp_ln_b")
ROWS_VEC, ROWS_BS, ROWS_WS, ROWS_CONV = D // LANES, 8, 8 * BLK, 3 * D // LANES
ROWS_LAYER = len(VECS) * ROWS_VEC + ROWS_BS + ROWS_WS + ROWS_CONV


def _pack_small(per_layer, tail):
    parts = []
    for P in per_layer:
        parts += [P[n].reshape(ROWS_VEC, LANES) for n in VECS]
        parts += [P["b_s"].reshape(ROWS_BS, LANES), P["w_s"].reshape(ROWS_WS, LANES), P["conv_w"].reshape(ROWS_CONV, LANES)]
    return jnp.concatenate(parts + [tail], axis=0)


def _unpack_small(pack):
    out = []
    for l in range(DEPTH):
        r = l * ROWS_LAYER
        P = {}
        for n in VECS:
            P[n] = pack[r:r + ROWS_VEC].reshape(D)
            r += ROWS_VEC
        P["b_s"] = pack[r:r + ROWS_BS].reshape(8, BLK)
        r += ROWS_BS
        P["w_s"] = pack[r:r + ROWS_WS].reshape(8, BLK, BLK)
        r += ROWS_WS
        P["conv_w"] = pack[r:r + ROWS_CONV].reshape(3, D)
        out.append(P)
    return out, pack[DEPTH * ROWS_LAYER:]


def kernel(x, positions, w_in, conv_w, gmlp_ln_g, gmlp_ln_b, w_s, b_s, p_a, p_b, p_c, w_o, ln1_g, ln1_b, w_gate, w_up, w_down, ln2_g, ln2_b, loss_target, m_w_in, m_conv_w, m_gmlp_ln_g, m_gmlp_ln_b, m_w_s, m_b_s, m_p_a, m_p_b, m_p_c, m_w_o, m_ln1_g, m_ln1_b, m_w_gate, m_w_up, m_w_down, m_ln2_g, m_ln2_b, v_w_in, v_conv_w, v_gmlp_ln_g, v_gmlp_ln_b, v_w_s, v_b_s, v_p_a, v_p_b, v_p_c, v_w_o, v_ln1_g, v_ln1_b, v_w_gate, v_w_up, v_w_down, v_ln2_g, v_ln2_b):
    Wt = dict(w_in=w_in, conv_w=conv_w, gmlp_ln_g=gmlp_ln_g, gmlp_ln_b=gmlp_ln_b, w_s=w_s, b_s=b_s, p_a=p_a, p_b=p_b,
              p_c=p_c, w_o=w_o, ln1_g=ln1_g, ln1_b=ln1_b, w_gate=w_gate, w_up=w_up, w_down=w_down, ln2_g=ln2_g, ln2_b=ln2_b)
    Mt = dict(w_in=m_w_in, conv_w=m_conv_w, gmlp_ln_g=m_gmlp_ln_g, gmlp_ln_b=m_gmlp_ln_b, w_s=m_w_s, b_s=m_b_s, p_a=m_p_a,
              p_b=m_p_b, p_c=m_p_c, w_o=m_w_o, ln1_g=m_ln1_g, ln1_b=m_ln1_b, w_gate=m_w_gate, w_up=m_w_up,
              w_down=m_w_down, ln2_g=m_ln2_g, ln2_b=m_ln2_b)
    Vt = dict(w_in=v_w_in, conv_w=v_conv_w, gmlp_ln_g=v_gmlp_ln_g, gmlp_ln_b=v_gmlp_ln_b, w_s=v_w_s, b_s=v_b_s, p_a=v_p_a,
              p_b=v_p_b, p_c=v_p_c, w_o=v_w_o, ln1_g=v_ln1_g, ln1_b=v_ln1_b, w_gate=v_w_gate, w_up=v_w_up,
              w_down=v_w_down, ln2_g=v_ln2_g, ln2_b=v_ln2_b)
    chip = 2 * lax.axis_index("x") + lax.axis_index("y")
    cw = D // NCHIP

    def layer_weights(l, gathered, conv_all):
        Wl = dict(zip(BIG, gathered))
        for n in ("p_a", "p_c", "w_o"):
            Wl[n] = Wl[n].reshape(D, D)
        Wl["conv_w"] = conv_all[:, l].transpose(1, 0, 2).reshape(3, D)
        for n in VECS + ("w_s", "b_s"):
            Wl[n] = Wt[n][l]
        return Wl

    halves = [Wt[n][0].astype(MX).reshape(2, Wt[n].shape[1] // 2, Wt[n].shape[2]) for n in BIG]
    got = gather_halves(halves + [conv_w])
    conv_all = got[NBIG]
    g0 = [got[0].reshape(D, NIN)] + [a.reshape(NCHIP, 2 * a.shape[2], a.shape[3]) for a in got[1:NBIG]]
    send1, recv1, sh1, g1, coming = gather_start("1", [Wt[n][1].astype(MX) for n in BIG], [conv_all])
    W0 = layer_weights(0, g0, conv_all)
    W0["after"] = coming[0, 0]

    def W1(h):
        return layer_weights(1, gather_wait("1", send1, recv1, sh1, g1, [h]), conv_all)

    layers = [W0, W1]

    rs_state, rs_started, held = {}, {}, {}

    def start_exchange(l, g):
        if "loss" in g:
            held[l] = g
            return None
        if "conv_w" in g:
            held[l] = g
            rs_state[(l, False)], started = reduce_scatter_chips(rs_state[(l, False)], [g["w_s"], g["conv_w"]])
            if l == 0:
                pack = _pack_small([held[j] for j in range(DEPTH)], held[DEPTH]["loss"])
                *held["small"], token = small_start(pack)
                started = started + token[0, 0]
            return started
        if "dx" in g:
            rs_state[(l, True)], rs_started[(l, True)] = reduce_scatter_chips(rs_state[(l, True)], [g["dx"]])
            return rs_started[(l, True)]
        key = (l, "w_in" in g)
        rs_state[key], started = reduce_scatter_pair(f"{l}{'b' if key[1] else 'a'}", g)
        return started

    _, grad_x, _ = local_step(x[0], positions[0], loss_target[0], layers, start_exchange)

    last = jnp.zeros((8, LANES), f32) + rs_started[(0, True)]
    behind = [grad_x, last]
    red = [dict() for _ in range(DEPTH)]
    for key in ((1, False), (1, True), (0, False)):
        red[key[0]].update(reduce_scatter_finish(rs_state[key], behind))
    small, tail = _unpack_small(small_sum(*reversed(small_wait(*held["small"], behind))))
    loss = tail[0, 0]

    G, DW, NM, NV = {}, {}, {}, {}
    zc = jnp.zeros((3, D), f32)
    wp = _pack_small([{**{n: Wt[n][l] for n in VECS + ("b_s", "w_s")}, "conv_w": zc} for l in range(DEPTH)], jnp.zeros((8, LANES), f32))
    mp = _pack_small([{**{n: Mt[n][l] for n in VECS + ("b_s", "w_s")}, "conv_w": zc} for l in range(DEPTH)], jnp.zeros((8, LANES), f32))
    vp = _pack_small([{**{n: Vt[n][l] for n in VECS + ("b_s", "w_s")}, "conv_w": zc} for l in range(DEPTH)], jnp.ones((8, LANES), f32))
    gp = _pack_small(small, jnp.zeros((8, LANES), f32))
    outs = [_unpack_small(a)[0] for a in adamw_small("adamw_small", gp, wp, mp, vp)]
    for n in VECS + ("b_s", "w_s"):
        G[n] = jnp.stack([small[l][n] for l in range(DEPTH)])
        DW[n], NM[n], NV[n] = (jnp.stack([o[l][n] for l in range(DEPTH)]) for o in outs)
    gconv = jnp.stack([lax.dynamic_slice(small[l]["conv_w"], (0, chip * cw), (3, cw)) for l in range(DEPTH)])
    G["conv_w"] = gconv
    flat = lambda a: a.reshape(DEPTH * 3, cw)
    d, m2, v2 = adamw_small("adamw_conv", flat(gconv), flat(conv_w), flat(m_conv_w), flat(v_conv_w))
    DW["conv_w"], NM["conv_w"], NV["conv_w"] = (a.reshape(DEPTH, 3, cw) for a in (d, m2, v2))

    updated = {}
    for n in BIG[1:]:
        tr = (lambda a: jnp.swapaxes(a, 1, 2)) if n in ("w_gate", "w_up") else (lambda a: a)
        updated[n] = adamw_big("adamw_" + n, (red[0][n], red[1][n]), tr(Wt[n]), tr(Mt[n]), tr(Vt[n]))
        G[n], DW[n], NM[n], NV[n] = map(tr, updated[n])
    done = [d, DW["ln2_b"], red[1]["w_in"][1]] + [updated[n][1] for n in BIG[1:]]
    red[0].update(reduce_scatter_finish(rs_state[(0, True)], done))
    G["w_in"], DW["w_in"], NM["w_in"], NV["w_in"] = adamw_big(
        "adamw_w_in", (red[0]["w_in"], red[1]["w_in"]), Wt["w_in"], Mt["w_in"], Vt["w_in"])

    return (loss, grad_x[None], *[G[n] for n in WEIGHTS], *[DW[n] for n in WEIGHTS], *[NM[n] for n in WEIGHTS],
            *[NV[n] for n in WEIGHTS])
```

```python
import functools
import math

import jax
import jax.numpy as jnp
from jax import lax
from jax.experimental import pallas as pl
from jax.experimental.pallas import tpu as pltpu

D = 1024
NIN = 12800
DFF = 2816
NCHIP = 4
FB = DFF // NCHIP
WIN_SHARD = NIN // NCHIP
DEPTH = 2
GROUPS = ((128, 1), (512, 4), (2048, 16))
HD = 64
BLK = 128
AO = 512
ALPHA = (2 * DEPTH) ** 0.25
EPS = 1e-5
ROPE_THETA = 10000.0
LANES = 128
NEG = -1e30

C_GATES, C_BCH, C_QKV, C_UV = 0, 3 * D, 6 * D, 6 * D + 9 * AO

MX = jnp.bfloat16
ACT = jnp.bfloat16

ADAM_LR, ADAM_B1, ADAM_B2, ADAM_EPS, ADAM_WD, ADAM_STEP = 0.001, 0.9, 0.999, 1e-08, 0.01, 10

f32 = jnp.float32
NT = (((1,), (1,)), ((), ()))
TN = (((0,), (0,)), ((), ()))


def _cp(sem, vmem_mb=48):
    return pltpu.CompilerParams(dimension_semantics=sem, vmem_limit_bytes=vmem_mb << 20)


def _dot(a, b, dims=None):
    if dims is None:
        return jnp.dot(a, b, preferred_element_type=f32)
    return lax.dot_general(a, b, dims, preferred_element_type=f32)


def _ln_stats(r):
    mu = jnp.mean(r, axis=-1, keepdims=True)
    xc = r - mu
    var = jnp.mean(xc * xc, axis=-1, keepdims=True)
    rstd = lax.rsqrt(var + EPS)
    return xc * rstd, rstd


def _ln_bwd(dy, xhat, rstd, g):
    dxh = dy * g
    return rstd * (dxh - jnp.mean(dxh, axis=-1, keepdims=True) - xhat * jnp.mean(dxh * xhat, axis=-1, keepdims=True))


def _gelu(x):
    return 0.5 * x * (1.0 + lax.erf(x * (1.0 / math.sqrt(2.0))))


def _gelu_grad(x):
    return 0.5 * (1.0 + lax.erf(x * (1.0 / math.sqrt(2.0)))) + x * jnp.exp(-0.5 * x * x) * (1.0 / math.sqrt(2.0 * math.pi))


def _sigmoid(x):
    return 0.5 * jnp.tanh(0.5 * x) + 0.5


def _acc_rows(o_ref, first, val):
    @pl.when(first)
    def _():
        o_ref[...] = jnp.zeros_like(o_ref)
    o_ref[...] += jnp.sum(val, axis=0, keepdims=True)


def mm_in(x, w, bias):
    T = x.shape[0]
    tm, tn = min(1024, T), 1280

    def body(x_ref, w_ref, b_ref, o_ref, xb):
        @pl.when(pl.program_id(1) == 0)
        def _():
            xb[...] = x_ref[...].astype(MX)
        o_ref[...] = (_dot(xb[...], w_ref[...]) + b_ref[...]).astype(o_ref.dtype)

    return pl.pallas_call(
        body, name="mm_in", grid=(T // tm, NIN // tn),
        in_specs=[pl.BlockSpec((tm, D), lambda i, j: (i, 0)), pl.BlockSpec((D, tn), lambda i, j: (0, j)),
                  pl.BlockSpec((1, tn), lambda i, j: (0, j))],
        out_specs=pl.BlockSpec((tm, tn), lambda i, j: (i, j)),
        out_shape=jax.ShapeDtypeStruct((T, NIN), ACT),
        scratch_shapes=[pltpu.VMEM((tm, D), MX)],
        compiler_params=_cp(("parallel", "arbitrary")),
    )(x, w, bias)


HALO = 16
TM_AC = 256


def _uv_specs():
    return [pl.BlockSpec((TM_AC, 512), functools.partial(lambda i, j: (i, j), j=C_UV // 512 + j)) for j in range(4)]


def _gmlp_fwd(up, vp, ws_ref, bs_ref, lg, lb):
    u = _gelu(up)
    xhat, rstd = _ln_stats(_gelu(vp))
    vn = xhat * lg + lb
    vnb = vn.astype(MX)
    rows = []
    for c in range(up.shape[0] // BLK):
        r = slice(c * BLK, (c + 1) * BLK)
        rows.append(jnp.concatenate(
            [_dot(ws_ref[g], vnb[r, g * BLK:(g + 1) * BLK]) + bs_ref[g] for g in range(8)], axis=1))
    return u, vn, xhat, rstd, jnp.concatenate(rows, axis=0)


def mix_ac_fwd(proj, conv_w, wst, bsx, lg, lb):
    T = proj.shape[0]
    tm = TM_AC

    def body(bch, halo, u0, u1, v0, v1, cw, ws, bs, lg_ref, lb_ref, ya, yc, zs):
        i = pl.program_id(0)
        pb = bch[...].astype(f32)
        z = pb[:, D:2 * D] * pb[:, 2 * D:]
        hz = halo[:, :D].astype(f32) * halo[:, D:].astype(f32)
        zs[0:HALO, :] = jnp.where(i > 0, hz, 0.0)
        zs[HALO:HALO + tm, :] = z
        cv = cw[0:1, :] * zs[HALO - 2:HALO - 2 + tm, :] + cw[1:2, :] * zs[HALO - 1:HALO - 1 + tm, :] + cw[2:3, :] * z
        ya[...] = (pb[:, :D] * cv).astype(ya.dtype)
        up = jnp.concatenate([u0[...], u1[...]], axis=1).astype(f32)
        vp = jnp.concatenate([v0[...], v1[...]], axis=1).astype(f32)
        u, _, _, _, sp = _gmlp_fwd(up, vp, ws, bs, lg_ref[...], lb_ref[...])
        yc[...] = (u * sp).astype(yc.dtype)

    full = lambda shape: pl.BlockSpec(shape, lambda i: (0,) * len(shape))
    return pl.pallas_call(
        body, name="mix_ac_fwd", grid=(T // tm,),
        in_specs=[pl.BlockSpec((tm, 3 * D), lambda i: (i, 1)),
                  pl.BlockSpec((HALO, 2 * D), lambda i: (jnp.maximum(i * (tm // HALO) - 1, 0), 2)),
                  *_uv_specs(), full((3, D)), full((8, BLK, BLK)), full((8, BLK, BLK)), full((1, D)), full((1, D))],
        out_specs=[pl.BlockSpec((tm, D), lambda i: (i, 0))] * 2,
        out_shape=[jax.ShapeDtypeStruct((T, D), MX)] * 2,
        scratch_shapes=[pltpu.VMEM((HALO + tm, D), f32)],
        compiler_params=_cp(("parallel",)),
    )(proj, proj, proj, proj, proj, proj, conv_w, wst, bsx, lg, lb)


def _swap_halves(x):
    lane = lax.broadcasted_iota(jnp.int32, x.shape, 1)
    return jnp.where((lane % HD) < HD // 2, pltpu.roll(x, x.shape[1] - HD // 2, 1), pltpu.roll(x, HD // 2, 1))


def _tile4(t):
    return jnp.concatenate([t] * (AO // LANES), axis=1)


TM_FOLD = 512


def _fold_out(nat, x, out_ref, d):
    if d == 1:
        out_ref[0] = x.astype(out_ref.dtype)
        return
    rows = x.shape[0] // d
    for j in range(AO // LANES):
        nat[j] = x[:, j * LANES:(j + 1) * LANES]
    for r in range(d):
        out_ref[r] = jnp.concatenate(
            [nat.at[j][pl.ds(r, rows, stride=d), :] for j in range(AO // LANES)], axis=1).astype(out_ref.dtype)


def _unfold_in(nat, in_ref, d):
    if d == 1:
        return in_ref[0].astype(f32)
    rows = in_ref.shape[1]
    for r in range(d):
        v = in_ref[r].astype(f32)
        for j in range(AO // LANES):
            nat.at[j][pl.ds(r, rows, stride=d), :] = v[:, j * LANES:(j + 1) * LANES]
    return jnp.concatenate([nat[j] for j in range(AO // LANES)], axis=1)


def fold_rope(proj, cos_t, sin_t, g, d):
    T = proj.shape[0]
    tm = TM_FOLD
    rows = tm // d

    def body(x_ref, c_ref, s_ref, q_o, k_o, v_o, nat):
        cos, sin = _tile4(c_ref[...]), _tile4(s_ref[...])
        for part, out, scale in ((0, q_o, HD ** -0.5), (1, k_o, 1.0), (2, v_o, None)):
            x = x_ref[:, part * AO:(part + 1) * AO].astype(f32)
            if scale is not None:
                x = (x * cos + _swap_halves(x) * sin) * scale
            _fold_out(nat, x, out, d)

    fold_spec = pl.BlockSpec((d, rows, AO), lambda i: (0, i, 0))
    return pl.pallas_call(
        body, name=f"fold_rope{g}", grid=(T // tm,),
        in_specs=[pl.BlockSpec((tm, 3 * AO), lambda i: (i, C_QKV // (3 * AO) + g)),
                  pl.BlockSpec((tm, LANES), lambda i: (i, 0)), pl.BlockSpec((tm, LANES), lambda i: (i, 0))],
        out_specs=[fold_spec] * 3,
        out_shape=[jax.ShapeDtypeStruct((d, T // d, AO), MX)] * 3,
        scratch_shapes=[pltpu.VMEM((AO // LANES, tm, LANES), f32)],
        compiler_params=_cp(("parallel",)),
    )(proj, cos_t, sin_t)


def _stack_heads(x):
    lane = lax.broadcasted_iota(jnp.int32, x.shape, 1)
    z = jnp.zeros_like(x)
    return jnp.concatenate([jnp.where(lane < HD, x, z), jnp.where(lane >= HD, x, z)], axis=0)


def _unstack_heads(y):
    lane = lax.broadcasted_iota(jnp.int32, (BLK, LANES), 1)
    return jnp.where(lane < HD, y[:BLK], y[BLK:])


def _window_masks():
    row = lax.broadcasted_iota(jnp.int32, (2 * BLK, 2 * BLK), 0) % BLK
    col = lax.broadcasted_iota(jnp.int32, (2 * BLK, 2 * BLK), 1)
    return (col < BLK) & (col >= row), (col >= BLK) & (col - BLK <= row)


def _two_blocks(ref, b):
    r0 = pl.multiple_of(b * BLK, BLK)
    rp = pl.multiple_of(jnp.maximum(b - 1, 0) * BLK, BLK)
    return jnp.concatenate([ref[pl.ds(rp, BLK), :], ref[pl.ds(r0, BLK), :]], axis=0)


def attn_fwd(qf, kf, vf, g, nb):
    T = qf.shape[0]

    def body(q_ref, k_ref, v_ref, o_ref, l_ref):
        prev_m, cur_m = _window_masks()

        def step(b, carry):
            r0 = pl.multiple_of(b * BLK, BLK)
            qs = _stack_heads(q_ref[pl.ds(r0, BLK), :])
            s = _dot(qs, _two_blocks(k_ref, b), NT)
            s = jnp.where(cur_m | (prev_m & ((b % nb) != 0)), s, NEG)
            m = jnp.max(s, axis=-1, keepdims=True)
            p = jnp.exp(s - m)
            l = jnp.sum(p, axis=-1, keepdims=True)
            o = _dot(p.astype(MX), _two_blocks(v_ref, b)) / l
            o_ref[pl.ds(r0, BLK), :] = _unstack_heads(o)
            l_ref[pl.ds(r0, BLK), :] = _unstack_heads(jnp.broadcast_to(m + jnp.log(l), (2 * BLK, LANES)))
            return carry

        lax.fori_loop(0, T // BLK, step, 0, unroll=4)

    spec = pl.BlockSpec((T, LANES), lambda j: (0, j))
    return pl.pallas_call(
        body, name=f"attn_fwd{g}", grid=(AO // LANES,),
        in_specs=[spec] * 3, out_specs=[spec] * 2,
        out_shape=[jax.ShapeDtypeStruct((T, AO), f32)] * 2,
        compiler_params=_cp(("parallel",), 56),
    )(qf, kf, vf)


def _group_weights(lses):
    m = jnp.maximum(jnp.maximum(lses[0], lses[1]), lses[2])
    e = [jnp.exp(l - m) for l in lses]
    inv = 1.0 / (e[0] + e[1] + e[2])
    return [x * inv for x in e]


def _fold_specs(T, tm):
    specs = []
    for _, d in GROUPS:
        specs.append(pl.BlockSpec((d, tm // d, AO), lambda i: (0, i, 0)))
    return specs


def combine_fwd(os_, lses):
    T = os_[0].shape[0] * os_[0].shape[1]
    tm = TM_FOLD

    def body(o0, o1, o2, l0, l1, l2, y_ref, nat):
        o = [_unfold_in(nat, r, d) for r, (_, d) in zip((o0, o1, o2), GROUPS)]
        ls = [_unfold_in(nat, r, d) for r, (_, d) in zip((l0, l1, l2), GROUPS)]
        w = _group_weights(ls)
        y_ref[...] = (w[0] * o[0] + w[1] * o[1] + w[2] * o[2]).astype(y_ref.dtype)

    specs = _fold_specs(T, tm)
    return pl.pallas_call(
        body, name="combine_fwd", grid=(T // tm,),
        in_specs=specs + specs, out_specs=pl.BlockSpec((tm, AO), lambda i: (i, 0)),
        out_shape=jax.ShapeDtypeStruct((T, AO), MX),
        scratch_shapes=[pltpu.VMEM((AO // LANES, tm, LANES), f32)],
        compiler_params=_cp(("parallel",)),
    )(*os_, *lses)


TM_MIX = 256


def mix_out_fwd(proj, ya, yb, yc, x0, pa, pb, pc, wo, g1, b1):
    T = x0.shape[0]
    tm = min(TM_MIX, T)

    def body(gt, ya_r, yb_r, yc_r, x0_r, pa_r, pb_r, pc_r, wo_r, g_r, b_r, mabc, m_o, r1_o, x1_o):
        ma = _dot(ya_r[...], pa_r[...])
        ybv = yb_r[...]
        mb = jnp.concatenate([_dot(ybv, pb_r[k]) for k in range(NCHIP)], axis=1)
        mc = _dot(yc_r[...], pc_r[...])
        m = jnp.zeros((tm, D), f32)
        for j, mm in enumerate((ma, mb, mc)):
            mabc[:, j * D:(j + 1) * D] = mm.astype(mabc.dtype)
            m = m + _sigmoid(gt[:, j * D:(j + 1) * D].astype(f32)) * mm
        mb16 = m.astype(MX)
        m_o[...] = mb16
        r1 = ALPHA * x0_r[...] + _dot(mb16, wo_r[...])
        r1_o[...] = r1
        xhat, _ = _ln_stats(r1)
        x1_o[...] = xhat * g_r[...] + b_r[...]

    full = lambda shape: pl.BlockSpec(shape, lambda i: (0,) * len(shape))
    tile = lambda w: pl.BlockSpec((tm, w), lambda i: (i, 0))
    return pl.pallas_call(
        body, name="mix_out_fwd", grid=(T // tm,),
        in_specs=[tile(3 * D), tile(D), tile(AO), tile(D), tile(D), full((D, D)), full((NCHIP, AO, D // NCHIP)),
                  full((D, D)), full((D, D)), full((1, D)), full((1, D))],
        out_specs=[tile(3 * D), tile(D), tile(D), tile(D)],
        out_shape=[jax.ShapeDtypeStruct((T, 3 * D), MX), jax.ShapeDtypeStruct((T, D), MX),
                   jax.ShapeDtypeStruct((T, D), f32), jax.ShapeDtypeStruct((T, D), f32)],
        compiler_params=_cp(("parallel",), 56),
    )(proj, ya, yb, yc, x0, pa, pb, pc, wo, g1, b1)


TM_FF = 512
TM_FFB = 256
ROW_CHUNK = 64


def ffn_up_fwd(x1, wg, wu):
    T = x1.shape[0]
    tm = min(TM_FFB, T)

    def body(x_r, wg_r, wu_r, g_o, u_o, h_o, gs, us):
        xb = x_r[...].astype(MX)
        for k in range(NCHIP):
            gs[...] = _dot(xb, wg_r[k])
            us[...] = _dot(xb, wu_r[k])
            for r in range(0, tm, ROW_CHUNK):
                rows = pl.ds(r, ROW_CHUNK)
                gate, up = gs[rows, :], us[rows, :]
                g_o[k, rows, :] = gate.astype(g_o.dtype)
                u_o[k, rows, :] = up.astype(u_o.dtype)
                h_o[k, rows, :] = (gate * _sigmoid(gate) * up).astype(h_o.dtype)

    wspec = pl.BlockSpec((NCHIP, D, FB), lambda i: (0, 0, 0))
    ospec = pl.BlockSpec((NCHIP, tm, FB), lambda i: (0, i, 0))
    return pl.pallas_call(
        body, name="ffn_up_fwd", grid=(T // tm,),
        in_specs=[pl.BlockSpec((tm, D), lambda i: (i, 0)), wspec, wspec],
        out_specs=[ospec] * 3,
        out_shape=[jax.ShapeDtypeStruct((NCHIP, T, FB), ACT)] * 2 + [jax.ShapeDtypeStruct((NCHIP, T, FB), MX)],
        scratch_shapes=[pltpu.VMEM((tm, FB), f32)] * 2,
        compiler_params=_cp(("parallel",)),
    )(x1, wg, wu)


def ffn_down_fwd(hh, wd, x1, g2, b2):
    T = x1.shape[0]
    tm = min(TM_FF, T)

    def body(h_r, w_r, x_r, g_r, b_r, r2_o, x2_o):
        r2 = ALPHA * x_r[...]
        for k in range(NCHIP):
            r2 = r2 + _dot(h_r[k], w_r[k])
        r2_o[...] = r2
        xhat, _ = _ln_stats(r2)
        x2_o[...] = xhat * g_r[...] + b_r[...]

    tile = pl.BlockSpec((tm, D), lambda i: (i, 0))
    vec = pl.BlockSpec((1, D), lambda i: (0, 0))
    return pl.pallas_call(
        body, name="ffn_down_fwd", grid=(T // tm,),
        in_specs=[pl.BlockSpec((NCHIP, tm, FB), lambda i: (0, i, 0)), pl.BlockSpec((NCHIP, FB, D), lambda i: (0, 0, 0)),
                  tile, vec, vec],
        out_specs=[tile, tile], out_shape=[jax.ShapeDtypeStruct((T, D), f32)] * 2,
        compiler_params=_cp(("parallel",)),
    )(hh, wd, x1, g2, b2)


def loss_grad(y, tgt):
    T = y.shape[0]
    tm = min(512, T)

    def body(y_r, t_r, l_o, dy_o):
        e = y_r[...] - t_r[...]
        dy_o[...] = e * (1.0 / D)

        @pl.when(pl.program_id(0) == 0)
        def _():
            l_o[...] = jnp.zeros_like(l_o)
        l_o[...] += (0.5 / D) * jnp.sum(e * e)

    tile = pl.BlockSpec((tm, D), lambda i: (i, 0))
    return pl.pallas_call(
        body, name="loss_grad", grid=(T // tm,),
        in_specs=[tile, tile], out_specs=[pl.BlockSpec((8, LANES), lambda i: (0, 0)), tile],
        out_shape=[jax.ShapeDtypeStruct((8, LANES), f32), jax.ShapeDtypeStruct((T, D), f32)],
        compiler_params=_cp(("arbitrary",)),
    )(y, tgt)


def ffn_down_bwd(dx2, r2, g2, wd, gate, up):
    T = dx2.shape[0]
    tm = min(TM_FFB, T)

    def body(dx_r, r_r, g_r, w_r, ga_r, up_r, dr_o, dg_o, du_o, dlg_o, dlb_o, hs):
        i = pl.program_id(0)
        xhat, rstd = _ln_stats(r_r[...])
        dx = dx_r[...]
        _acc_rows(dlg_o, i == 0, dx * xhat)
        _acc_rows(dlb_o, i == 0, dx)
        dr = _ln_bwd(dx, xhat, rstd, g_r[...])
        dr_o[...] = dr
        drb = dr.astype(MX)
        for k in range(NCHIP):
            hs[...] = _dot(drb, w_r[k], NT)
            for r in range(0, tm, ROW_CHUNK):
                rows = pl.ds(r, ROW_CHUNK)
                dhh, gate_v, up_v = hs[rows, :], ga_r[k, rows, :].astype(f32), up_r[k, rows, :].astype(f32)
                sg = _sigmoid(gate_v)
                dg_o[k, rows, :] = (dhh * up_v * sg * (1.0 + gate_v * (1.0 - sg))).astype(dg_o.dtype)
                du_o[k, rows, :] = (dhh * gate_v * sg).astype(du_o.dtype)

    tile = pl.BlockSpec((tm, D), lambda i: (i, 0))
    vec = pl.BlockSpec((1, D), lambda i: (0, 0))
    blk = pl.BlockSpec((NCHIP, tm, FB), lambda i: (0, i, 0))
    return pl.pallas_call(
        body, name="ffn_down_bwd", grid=(T // tm,),
        in_specs=[tile, tile, vec, pl.BlockSpec((NCHIP, FB, D), lambda i: (0, 0, 0)), blk, blk],
        out_specs=[tile, blk, blk, vec, vec],
        out_shape=[jax.ShapeDtypeStruct((T, D), f32)] + [jax.ShapeDtypeStruct((NCHIP, T, FB), MX)] * 2
        + [jax.ShapeDtypeStruct((1, D), f32)] * 2,
        scratch_shapes=[pltpu.VMEM((tm, FB), f32)],
        compiler_params=_cp(("arbitrary",)),
    )(dx2, r2, g2, wd, gate, up)


def ffn_up_bwd(dr2, dgate, dup, wg, wu, r1, g1):
    T = dr2.shape[0]
    tm = min(TM_FFB, T)

    def body(dr2_r, dg_r, du_r, wg_r, wu_r, r1_r, g_r, dr1_o, dlg_o, dlb_o):
        i = pl.program_id(0)
        dx = ALPHA * dr2_r[...]
        for k in range(NCHIP):
            dx = dx + _dot(dg_r[k], wg_r[k], NT) + _dot(du_r[k], wu_r[k], NT)
        xhat, rstd = _ln_stats(r1_r[...])
        _acc_rows(dlg_o, i == 0, dx * xhat)
        _acc_rows(dlb_o, i == 0, dx)
        dr1_o[...] = _ln_bwd(dx, xhat, rstd, g_r[...])

    tile = pl.BlockSpec((tm, D), lambda i: (i, 0))
    vec = pl.BlockSpec((1, D), lambda i: (0, 0))
    blk = pl.BlockSpec((NCHIP, tm, FB), lambda i: (0, i, 0))
    wspec = pl.BlockSpec((NCHIP, D, FB), lambda i: (0, 0, 0))
    return pl.pallas_call(
        body, name="ffn_up_bwd", grid=(T // tm,),
        in_specs=[tile, blk, blk, wspec, wspec, tile, vec],
        out_specs=[tile, vec, vec],
        out_shape=[jax.ShapeDtypeStruct((T, D), f32)] + [jax.ShapeDtypeStruct((1, D), f32)] * 2,
        compiler_params=_cp(("arbitrary",)),
    )(dr2, dgate, dup, wg, wu, r1, g1)


def mix_out_bwd(dr1, proj, mabc, wo, pa, pb, pc):
    T = dr1.shape[0]
    tm = min(TM_MIX, T)

    def body(dr_r, gt, mabc_r, wo_r, pa_r, pb_r, pc_r, dmabc_o, dgt_o, dya_o, dyb_o, dyc_o):
        dm = _dot(dr_r[...].astype(MX), wo_r[...], NT)
        dmx = []
        for j in range(3):
            s = _sigmoid(gt[:, j * D:(j + 1) * D].astype(f32))
            v = (dm * s).astype(MX)
            dmx.append(v)
            dmabc_o[:, j * D:(j + 1) * D] = v
            dgt_o[:, j * D:(j + 1) * D] = (dm * mabc_r[:, j * D:(j + 1) * D].astype(f32) * s * (1.0 - s)).astype(dgt_o.dtype)
        dya_o[...] = _dot(dmx[0], pa_r[...], NT)
        dyb = jnp.zeros((tm, AO), f32)
        for k in range(NCHIP):
            dyb = dyb + _dot(dmx[1][:, k * (D // NCHIP):(k + 1) * (D // NCHIP)], pb_r[k], NT)
        dyb_o[...] = dyb
        dyc_o[...] = _dot(dmx[2], pc_r[...], NT)

    full = lambda shape: pl.BlockSpec(shape, lambda i: (0,) * len(shape))
    tile = lambda w: pl.BlockSpec((tm, w), lambda i: (i, 0))
    return pl.pallas_call(
        body, name="mix_out_bwd", grid=(T // tm,),
        in_specs=[tile(D), tile(3 * D), tile(3 * D), full((D, D)), full((D, D)), full((NCHIP, AO, D // NCHIP)), full((D, D))],
        out_specs=[tile(3 * D), tile(3 * D), tile(D), tile(AO), tile(D)],
        out_shape=[jax.ShapeDtypeStruct((T, 3 * D), MX), jax.ShapeDtypeStruct((T, 3 * D), MX),
                   jax.ShapeDtypeStruct((T, D), f32), jax.ShapeDtypeStruct((T, AO), f32), jax.ShapeDtypeStruct((T, D), f32)],
        compiler_params=_cp(("parallel",), 56),
    )(dr1, proj, mabc, wo, pa, pb, pc)


def transpose_cast(x):
    T = x.shape[0]
    tm = min(512, T)

    def body(x_r, o_r):
        o_r[...] = x_r[...].T.astype(o_r.dtype)

    return pl.pallas_call(
        body, name="transpose_cast", grid=(T // tm,),
        in_specs=[pl.BlockSpec((tm, D), lambda i: (i, 0))], out_specs=pl.BlockSpec((D, tm), lambda i: (0, i)),
        out_shape=jax.ShapeDtypeStruct((D, T), MX), compiler_params=_cp(("parallel",)),
    )(x)


def tn_matmul(name, a, b, a_spec, b_spec, out_shape, out_spec, grid, a_is_t=False):
    nt = len(grid) - 1

    def body(a_r, b_r, o_r):
        @pl.when(pl.program_id(nt) == 0)
        def _():
            o_r[...] = jnp.zeros_like(o_r)
        av = a_r[...].reshape(a_r.shape[-2:]).astype(MX)
        bv = b_r[...].reshape(b_r.shape[-2:]).astype(MX)
        o_r[...] += _dot(av, bv, None if a_is_t else TN).reshape(o_r.shape)

    return pl.pallas_call(
        body, name=name, grid=grid, in_specs=[a_spec, b_spec], out_specs=out_spec,
        out_shape=jax.ShapeDtypeStruct(out_shape, f32),
        compiler_params=_cp(("parallel",) * nt + ("arbitrary",), 56),
    )(a, b)


def attn_pre_bwd(dyb, os_, lses, ones):
    T = dyb.shape[0]
    tm = TM_FOLD

    def body(dy_r, o0, o1, o2, l0, l1, l2, ones_r, d0, d1, d2, f0, f1, f2, nat):
        o = [_unfold_in(nat, r, d) for r, (_, d) in zip((o0, o1, o2), GROUPS)]
        ls = [_unfold_in(nat, r, d) for r, (_, d) in zip((l0, l1, l2), GROUPS)]
        w = _group_weights(ls)
        dy = dy_r[...]
        t = dy * (w[0] * o[0] + w[1] * o[1] + w[2] * o[2])
        hi = t.astype(MX)
        lo = (t - hi.astype(f32)).astype(MX)
        c = _dot(hi, ones_r[...]) + _dot(lo, ones_r[...])
        for wg, do_o, df_o, (_, d) in zip(w, (d0, d1, d2), (f0, f1, f2), GROUPS):
            _fold_out(nat, wg * dy, do_o, d)
            _fold_out(nat, -wg * c, df_o, d)

    specs = _fold_specs(T, tm)
    return pl.pallas_call(
        body, name="attn_pre_bwd", grid=(T // tm,),
        in_specs=[pl.BlockSpec((tm, AO), lambda i: (i, 0))] + specs + specs + [pl.BlockSpec((AO, AO), lambda i: (0, 0))],
        out_specs=specs + specs,
        out_shape=[jax.ShapeDtypeStruct((d, T // d, AO), MX) for _, d in GROUPS]
        + [jax.ShapeDtypeStruct((d, T // d, AO), f32) for _, d in GROUPS],
        scratch_shapes=[pltpu.VMEM((AO // LANES, tm, LANES), f32)],
        compiler_params=_cp(("parallel",)),
    )(dyb, *os_, *lses, ones)


def _head_ones():
    i = jnp.arange(AO) // HD
    return (i[:, None] == i[None, :]).astype(MX)


BWD_BLOCKS = 4


def attn_bwd(qf, kf, vf, dof, lse, df, g, nb):
    T = qf.shape[0]

    def body(q_ref, k_ref, v_ref, do_ref, l_ref, d_ref, dq_ref, dk_ref, dv_ref):
        prev_m, cur_m = _window_masks()

        def head_col(ref, r0):
            v = ref[pl.ds(r0, BLK), :]
            return jnp.concatenate([v[:, 0:1], v[:, HD:HD + 1]], axis=0)

        def step(b, carry):
            dk_c, dv_c = carry
            r0 = pl.multiple_of(b * BLK, BLK)
            rp = pl.multiple_of(jnp.maximum(b - 1, 0) * BLK, BLK)
            qs, dos = _stack_heads(q_ref[pl.ds(r0, BLK), :]), _stack_heads(do_ref[pl.ds(r0, BLK), :])
            k2, v2 = _two_blocks(k_ref, b), _two_blocks(v_ref, b)
            valid = cur_m | (prev_m & ((b % nb) != 0))
            p = jnp.where(valid, jnp.exp(_dot(qs, k2, NT) - head_col(l_ref, r0)), 0.0)
            ds = (p * (_dot(dos, v2, NT) + head_col(d_ref, r0))).astype(MX)
            dq_ref[pl.ds(r0, BLK), :] = _unstack_heads(_dot(ds, k2)).astype(dq_ref.dtype)
            dk2 = _dot(ds, qs, TN)
            dv2 = _dot(p.astype(MX), dos, TN)
            dk_ref[pl.ds(rp, BLK), :] = (dk_c + dk2[:BLK]).astype(dk_ref.dtype)
            dv_ref[pl.ds(rp, BLK), :] = (dv_c + dv2[:BLK]).astype(dv_ref.dtype)
            return dk2[BLK:], dv2[BLK:]

        zero = jnp.zeros((BLK, LANES), f32)

        def steps(i, carry):
            for j in range(BWD_BLOCKS):
                carry = step(BWD_BLOCKS * i + j, carry)
            return carry

        dk_c, dv_c = lax.fori_loop(0, T // BLK // BWD_BLOCKS, steps, (zero, zero))
        dk_ref[pl.ds(T - BLK, BLK), :] = dk_c.astype(dk_ref.dtype)
        dv_ref[pl.ds(T - BLK, BLK), :] = dv_c.astype(dv_ref.dtype)

    spec = pl.BlockSpec((T, LANES), lambda j: (0, j))
    return pl.pallas_call(
        body, name=f"attn_bwd{g}", grid=(AO // LANES,),
        in_specs=[spec] * 6, out_specs=[spec] * 3,
        out_shape=[jax.ShapeDtypeStruct((T, AO), MX)] * 3,
        compiler_params=_cp(("parallel",), 60),
    )(qf, kf, vf, dof, lse, df)


def unfold_rope_bwd(dqf, dkf, dvf, cos_t, sin_t, g, d):
    T = dqf.shape[0] * dqf.shape[1]
    tm = TM_FOLD

    def body(q_r, k_r, v_r, c_ref, s_ref, o_ref, nat):
        cos, sin = _tile4(c_ref[...]), _tile4(s_ref[...])
        for part, ref, scale in ((0, q_r, HD ** -0.5), (1, k_r, 1.0), (2, v_r, None)):
            x = _unfold_in(nat, ref, d)
            if scale is not None:
                x = (x * cos - _swap_halves(x) * sin) * scale
            o_ref[:, part * AO:(part + 1) * AO] = x.astype(o_ref.dtype)

    fold_spec = pl.BlockSpec((d, tm // d, AO), lambda i: (0, i, 0))
    tab = pl.BlockSpec((tm, LANES), lambda i: (i, 0))
    return pl.pallas_call(
        body, name=f"unfold_rope_bwd{g}", grid=(T // tm,),
        in_specs=[fold_spec] * 3 + [tab, tab],
        out_specs=pl.BlockSpec((tm, 3 * AO), lambda i: (i, 0)),
        out_shape=jax.ShapeDtypeStruct((T, 3 * AO), MX),
        scratch_shapes=[pltpu.VMEM((AO // LANES, tm, LANES), f32)],
        compiler_params=_cp(("parallel",)),
    )(dqf, dkf, dvf, cos_t, sin_t)


def conv_bwd(dya, proj, conv_w):
    T = dya.shape[0]
    tm = TM_AC
    last = T // tm - 1

    def body(dy_r, bch, hprev, dy_next, b_next, cw, d_o, dw_o, zs, ds):
        i = pl.program_id(0)
        pb = bch[...].astype(f32)
        bp, cp, hp = pb[:, :D], pb[:, D:2 * D], pb[:, 2 * D:]
        z = cp * hp
        hz = hprev[:, :D].astype(f32) * hprev[:, D:].astype(f32)
        zs[0:HALO, :] = jnp.where(i > 0, hz, 0.0)
        zs[HALO:HALO + tm, :] = z
        z2, z1 = zs[HALO - 2:HALO - 2 + tm, :], zs[HALO - 1:HALO - 1 + tm, :]
        cv = cw[0:1, :] * z2 + cw[1:2, :] * z1 + cw[2:3, :] * z
        dy = dy_r[...]
        dcv = dy * bp
        ds[0:tm, :] = dcv
        ds[tm:tm + HALO, :] = jnp.where(i < last, dy_next[...] * b_next[...].astype(f32), 0.0)
        dz = cw[2:3, :] * dcv + cw[1:2, :] * ds[1:1 + tm, :] + cw[0:1, :] * ds[2:2 + tm, :]
        d_o[:, :D] = (dy * cv).astype(d_o.dtype)
        d_o[:, D:2 * D] = (dz * hp).astype(d_o.dtype)
        d_o[:, 2 * D:] = (dz * cp).astype(d_o.dtype)

        @pl.when(i == 0)
        def _():
            dw_o[...] = jnp.zeros_like(dw_o)
        dw_o[0:1, :] += jnp.sum(dcv * z2, axis=0, keepdims=True)
        dw_o[1:2, :] += jnp.sum(dcv * z1, axis=0, keepdims=True)
        dw_o[2:3, :] += jnp.sum(dcv * z, axis=0, keepdims=True)

    nh = tm // HALO
    return pl.pallas_call(
        body, name="conv_bwd", grid=(T // tm,),
        in_specs=[pl.BlockSpec((tm, D), lambda i: (i, 0)), pl.BlockSpec((tm, 3 * D), lambda i: (i, 1)),
                  pl.BlockSpec((HALO, 2 * D), lambda i: (jnp.maximum(i * nh - 1, 0), 2)),
                  pl.BlockSpec((HALO, D), lambda i: (jnp.minimum((i + 1) * nh, T // HALO - 1), 0)),
                  pl.BlockSpec((HALO, D), lambda i: (jnp.minimum((i + 1) * nh, T // HALO - 1), 3)),
                  pl.BlockSpec((3, D), lambda i: (0, 0))],
        out_specs=[pl.BlockSpec((tm, 3 * D), lambda i: (i, 0)), pl.BlockSpec((3, D), lambda i: (0, 0))],
        out_shape=[jax.ShapeDtypeStruct((T, 3 * D), MX), jax.ShapeDtypeStruct((3, D), f32)],
        scratch_shapes=[pltpu.VMEM((HALO + tm, D), f32), pltpu.VMEM((tm + HALO, D), f32)],
        compiler_params=_cp(("arbitrary",)),
    )(dya, proj, proj, dya, proj, conv_w)


def gmlp_bwd(dyc, proj, wst, bsx, lg, lb):
    T = dyc.shape[0]
    tm = TM_AC
    last = T // tm - 1

    def body(dy_r, u0, u1, v0, v1, ws, bs, lg_r, lb_r, d_o, dws_o, dbs_o, dlg_o, dlb_o, bacc):
        i = pl.program_id(0)
        up = jnp.concatenate([u0[...], u1[...]], axis=1).astype(f32)
        vp = jnp.concatenate([v0[...], v1[...]], axis=1).astype(f32)
        u, vn, xhat, rstd, sp = _gmlp_fwd(up, vp, ws, bs, lg_r[...], lb_r[...])
        dy = dy_r[...]
        d_o[:, :D] = (dy * sp * _gelu_grad(up)).astype(d_o.dtype)
        dsp = dy * u
        dspb, vnb = dsp.astype(MX), vn.astype(MX)

        @pl.when(i == 0)
        def _():
            dws_o[...] = jnp.zeros_like(dws_o)
            bacc[...] = jnp.zeros_like(bacc)

        rows = []
        for c in range(tm // BLK):
            r = slice(c * BLK, (c + 1) * BLK)
            cols = []
            for g in range(8):
                cs = slice(g * BLK, (g + 1) * BLK)
                dws_o[g] += _dot(dspb[r, cs], vnb[r, cs], NT)
                bacc[g] += dsp[r, cs]
                cols.append(_dot(ws[g], dspb[r, cs], TN))
            rows.append(jnp.concatenate(cols, axis=1))
        dvn = jnp.concatenate(rows, axis=0)
        _acc_rows(dlg_o, i == 0, dvn * xhat)
        _acc_rows(dlb_o, i == 0, dvn)
        d_o[:, D:] = (_ln_bwd(dvn, xhat, rstd, lg_r[...]) * _gelu_grad(vp)).astype(d_o.dtype)

        @pl.when(i == last)
        def _():
            row = lax.broadcasted_iota(jnp.int32, (BLK, BLK), 0)
            col = lax.broadcasted_iota(jnp.int32, (BLK, BLK), 1)
            ones = jnp.ones((8, BLK), MX)
            for g in range(8):
                dws_o[g] = jnp.where(col <= row, dws_o[g], 0.0)
                a = bacc[g]
                hi = a.astype(MX)
                lo = (a - hi.astype(f32)).astype(MX)
                dbs_o[g:g + 1, :] = (_dot(ones, hi, NT) + _dot(ones, lo, NT))[0:1, :]

    full = lambda shape: pl.BlockSpec(shape, lambda i: (0,) * len(shape))
    return pl.pallas_call(
        body, name="gmlp_bwd", grid=(T // tm,),
        in_specs=[pl.BlockSpec((tm, D), lambda i: (i, 0)), *_uv_specs(), full((8, BLK, BLK)), full((8, BLK, BLK)),
                  full((1, D)), full((1, D))],
        out_specs=[pl.BlockSpec((tm, 2 * D), lambda i: (i, 0)), full((8, BLK, BLK)), full((8, BLK)), full((1, D)), full((1, D))],
        out_shape=[jax.ShapeDtypeStruct((T, 2 * D), MX), jax.ShapeDtypeStruct((8, BLK, BLK), f32),
                   jax.ShapeDtypeStruct((8, BLK), f32), jax.ShapeDtypeStruct((1, D), f32), jax.ShapeDtypeStruct((1, D), f32)],
        scratch_shapes=[pltpu.VMEM((8, BLK, BLK), f32)],
        compiler_params=_cp(("arbitrary",)),
    )(dyc, proj, proj, proj, proj, wst, bsx, lg, lb)


PART_TILES = (6, 6, 3, 3, 3, 4)
PART_START = (0, 6, 12, 15, 18, 21)
TJ = 512


def _part_specs(tm, rows_axis):
    specs = []
    for n, s in zip(PART_TILES, PART_START):
        def imap(*idx, n=n, s=s):
            i, j = idx[rows_axis], idx[1 - rows_axis]
            inside = (j >= s) & (j < s + n)
            return (jnp.where(inside, i, 0), jnp.clip(j - s, 0, n - 1))
        specs.append(pl.BlockSpec((tm, TJ), imap))
    return specs


def _with_part(j, refs, fn):
    for r, n, s in zip(refs, PART_TILES, PART_START):
        @pl.when((j >= s) & (j < s + n))
        def _():
            fn(r[...])


def dx_in(dr1, parts, w, bias):
    T = dr1.shape[0]
    tm = min(1024, T)

    def body(dr_r, p0, p1, p2, p3, p4, p5, w_r, b_r, o_r):
        j = pl.program_id(1)

        @pl.when(j == 0)
        def _():
            o_r[...] = ALPHA * dr_r[...] + b_r[...]

        def acc(tile):
            o_r[...] += _dot(tile, w_r[...], NT)
        _with_part(j, (p0, p1, p2, p3, p4, p5), acc)

    return pl.pallas_call(
        body, name="dx_in", grid=(T // tm, NIN // TJ),
        in_specs=[pl.BlockSpec((tm, D), lambda i, j: (i, 0))] + _part_specs(tm, 0)
        + [pl.BlockSpec((D, TJ), lambda i, j: (0, j)), pl.BlockSpec((1, D), lambda i, j: (0, 0))],
        out_specs=pl.BlockSpec((tm, D), lambda i, j: (i, 0)),
        out_shape=jax.ShapeDtypeStruct((T, D), f32),
        compiler_params=_cp(("parallel", "arbitrary"), 56),
    )(dr1, *parts, w, bias)


def dw_in(x0t, parts):
    T = x0t.shape[1]
    tk = min(2048, T)

    def body(x_r, p0, p1, p2, p3, p4, p5, o_r):
        j, t = pl.program_id(0), pl.program_id(1)

        @pl.when(t == 0)
        def _():
            o_r[...] = jnp.zeros_like(o_r)

        def acc(tile):
            o_r[...] += _dot(x_r[...], tile)
        _with_part(j, (p0, p1, p2, p3, p4, p5), acc)

    return pl.pallas_call(
        body, name="dw_in", grid=(NIN // TJ, T // tk),
        in_specs=[pl.BlockSpec((D, tk), lambda j, t: (0, t))] + _part_specs(tk, 1),
        out_specs=pl.BlockSpec((D, TJ), lambda j, t: (0, j)),
        out_shape=jax.ShapeDtypeStruct((D, NIN), f32),
        compiler_params=_cp(("parallel", "arbitrary")),
    )(x0t, *parts)


def rope_tables(positions):
    half = HD // 2
    inv_freq = ROPE_THETA ** (-jnp.arange(half, dtype=f32) / half)
    ang = positions.astype(f32)[:, None] * inv_freq
    cos, sin = jnp.cos(ang), jnp.sin(ang)
    return jnp.tile(cos, (1, LANES // half)), jnp.tile(jnp.concatenate([-sin, sin], axis=1), (1, LANES // HD))


def _flat(a):
    return a.reshape(a.shape[0] * a.shape[1], a.shape[2])


def layer_fwd(x0, W, cos_t, sin_t):
    T = x0.shape[0]
    proj = mm_in(x0, W["w_in"], W["in_bias"])
    ya, yc = mix_ac_fwd(proj, W["conv_w"], W["wst"], W["bsx"], W["gmlp_ln_g"], W["gmlp_ln_b"])
    folded, os_, lses = [], [], []
    for g, (_, d) in enumerate(GROUPS):
        qf, kf, vf = fold_rope(proj, cos_t, sin_t, g, d)
        o, lse = attn_fwd(_flat(qf), _flat(kf), _flat(vf), g, T // d // BLK)
        folded.append((qf, kf, vf))
        os_.append(o.reshape(d, T // d, AO))
        lses.append(lse.reshape(d, T // d, AO))
    yb = combine_fwd(os_, lses)
    mabc, m, r1, x1 = mix_out_fwd(proj, ya, yb, yc, x0, W["p_a"], W["p_b"], W["p_c"], W["w_o"], W["ln1_g"], W["ln1_b"])
    gate, up, hh = ffn_up_fwd(x1, W["w_gate"], W["w_up"])
    r2, x2 = ffn_down_fwd(hh, W["w_down"], x1, W["ln2_g"], W["ln2_b"])
    saved = dict(x0=x0, proj=proj, ya=ya, yb=yb, yc=yc, folded=folded, os=os_, lses=lses, mabc=mabc, m=m, r1=r1,
                 x1=x1, gate=gate, up=up, hh=hh, r2=r2)
    return x2, saved


def layer_bwd(dx2, S, W, cos_t, sin_t, on_grads=None):
    T = dx2.shape[0]
    tk = min(2048, T)
    G = {}
    dr2, dgate, dup, G["ln2_g"], G["ln2_b"] = ffn_down_bwd(dx2, S["r2"], W["ln2_g"], W["w_down"], S["gate"], S["up"])
    blk_a = pl.BlockSpec((1, tk, FB), lambda k, t: (k, t, 0))
    row_b = pl.BlockSpec((tk, D), lambda k, t: (t, 0))
    G["w_down"] = tn_matmul("dw_down", S["hh"], dr2, blk_a, row_b, (NCHIP, FB, D),
                            pl.BlockSpec((1, FB, D), lambda k, t: (k, 0, 0)), (NCHIP, T // tk))
    for nm, dv in (("w_gate", dgate), ("w_up", dup)):
        G[nm] = tn_matmul("d" + nm, dv, S["x1"], blk_a, row_b, (NCHIP, FB, D),
                          pl.BlockSpec((1, FB, D), lambda k, t: (k, 0, 0)), (NCHIP, T // tk))
    dr1, G["ln1_g"], G["ln1_b"] = ffn_up_bwd(dr2, dgate, dup, W["w_gate"], W["w_up"], S["r1"], W["ln1_g"])
    dmabc, dgates, dya, dyb, dyc = mix_out_bwd(dr1, S["proj"], S["mabc"], W["w_o"], W["p_a"], W["p_b"], W["p_c"])
    one = (1, T // tk)
    full_o = pl.BlockSpec((D, D), lambda k, t: (0, 0))
    G["w_o"] = tn_matmul("dw_o", S["m"], dr1, row_b, row_b, (D, D), full_o, one)
    G["p_a"] = tn_matmul("dp_a", S["ya"], dmabc, row_b, pl.BlockSpec((tk, D), lambda k, t: (t, 0)), (D, D), full_o, one)
    G["p_c"] = tn_matmul("dp_c", S["yc"], dmabc, row_b, pl.BlockSpec((tk, D), lambda k, t: (t, 2)), (D, D), full_o, one)
    G["p_b"] = tn_matmul("dp_b", S["yb"], dmabc, pl.BlockSpec((tk, AO), lambda k, t: (t, 0)),
                         pl.BlockSpec((tk, D // NCHIP), lambda k, t: (t, NCHIP + k)), (NCHIP, AO, D // NCHIP),
                         pl.BlockSpec((1, AO, D // NCHIP), lambda k, t: (k, 0, 0)), (NCHIP, T // tk))
    conv_w = W["conv_w"]
    if on_grads is not None:
        conv_w = conv_w + on_grads({n: G[n] for n in BIG if n != "w_in"})
    dbch, G["conv_w"] = conv_bwd(dya, S["proj"], conv_w)
    duv, G["w_s"], G["b_s"], G["gmlp_ln_g"], G["gmlp_ln_b"] = gmlp_bwd(
        dyc, S["proj"], W["wst"], W["bsx"], W["gmlp_ln_g"], W["gmlp_ln_b"])
    ones = _head_ones()
    if on_grads is not None:
        small = {n: G[n] for n in VECS + ("b_s", "w_s", "conv_w")}
        ones = ones + on_grads(small).astype(MX)
    pre = attn_pre_bwd(dyb, S["os"], S["lses"], ones)
    dqkv = []
    for g, (_, d) in enumerate(GROUPS):
        qf, kf, vf = S["folded"][g]
        dqf, dkf, dvf = attn_bwd(_flat(qf), _flat(kf), _flat(vf), _flat(pre[g]), _flat(S["lses"][g]), _flat(pre[3 + g]),
                                 g, T // d // BLK)
        shp = (d, T // d, AO)
        dqkv.append(unfold_rope_bwd(dqf.reshape(shp), dkf.reshape(shp), dvf.reshape(shp), cos_t, sin_t, g, d))
    parts = (dgates, dbch, *dqkv, duv)
    G["w_in"] = dw_in(transpose_cast(S["x0"]), parts)
    bias = jnp.zeros((1, D), f32)
    if on_grads is not None:
        bias = bias + on_grads({"w_in": G["w_in"]})
    dx0 = dx_in(dr1, parts, W["w_in"], bias)
    started = on_grads({"dx": dx0}) if on_grads is not None else None
    return dx0, G, started


def prep_layer_weights(Wl):
    W = dict(Wl)
    tril = jnp.tril(jnp.ones((BLK, BLK), f32))
    W["wst"] = (Wl["w_s"] * tril[None]).astype(MX)
    W["bsx"] = jnp.broadcast_to(Wl["b_s"][:, :, None], (8, BLK, BLK))
    for n in ("gmlp_ln_g", "gmlp_ln_b", "ln1_g", "ln1_b", "ln2_g", "ln2_b"):
        W[n] = Wl[n].reshape(1, D)
    W["in_bias"] = jnp.zeros((1, NIN), f32) + Wl.get("after", 0.0)
    return W


def local_step(x, positions, target, layers, on_grads=None):
    cos_t, sin_t = rope_tables(positions)
    Ws, saved = [], []
    h = x
    for Wl in layers:
        Ws.append(prep_layer_weights(Wl(h) if callable(Wl) else Wl))
        h, S = layer_fwd(h, Ws[-1], cos_t, sin_t)
        saved.append(S)
    lsum, dh = loss_grad(h, target)
    if on_grads is not None:
        on_grads(len(Ws), {"loss": lsum})
    grads = [None] * len(Ws)
    started = None
    for l in reversed(range(len(Ws))):
        W = Ws[l]
        if started is not None:
            W = dict(W, ln2_g=W["ln2_g"] + started)
        hook = functools.partial(on_grads, l) if on_grads is not None else None
        dh, grads[l], started = layer_bwd(dh, saved[l], W, cos_t, sin_t, hook)
    return lsum, dh, grads


MESH = pl.DeviceIdType.MESH
ANY = pl.BlockSpec(memory_space=pl.ANY)
BIG = ("w_in", "w_gate", "w_up", "w_down", "p_a", "p_b", "p_c", "w_o")
NBIG = len(BIG)


def _place():
    x, y, c = lax.axis_index("x"), lax.axis_index("y"), lax.axis_index("c")
    return x, y, c, 2 * x + y


def _rcopy(src, dst, send, recv, dev):
    return pltpu.make_async_remote_copy(src_ref=src, dst_ref=dst, send_sem=send, recv_sem=recv, device_id=dev,
                                        device_id_type=MESH)


def _cols(ref, k, width):
    start = k * width if isinstance(k, int) else pl.multiple_of(k * width, LANES)
    return ref.at[:, pl.ds(start, width)]


CHUNK_BYTES = 1 << 20


def _pieces(shape, itemsize, nbytes=CHUNK_BYTES):
    rows, cols = shape[-2], shape[-1]
    per = max(16, nbytes // (cols * itemsize) // 16 * 16)
    out = []
    for lead in (range(shape[0]) if len(shape) == 3 else (None,)):
        for r in range(0, rows, per):
            sl = (pl.ds(r, min(per, rows - r)), slice(None))
            out.append(sl if lead is None else (lead,) + sl)
    return out


def _start_pieces(src, dst, make, nbytes=CHUNK_BYTES):
    for idx in _pieces(src.shape, jnp.dtype(src.dtype).itemsize, nbytes):
        make(src.at[idx], dst.at[idx]).start()


def gather_halves(shards):
    n = len(shards)

    def body(*refs):
        srcs, dsts = refs[:n], refs[n:2 * n]
        send, recv, own_send, own_recv = refs[2 * n:]
        x, y, c, k = _place()
        sib = (x, y, 1 - c)
        chips = [(1 - x, y), (x, 1 - y), (1 - x, 1 - y)]

        def slot(a, layer, pos):
            if a == 0:
                return _cols(dsts[0].at[layer], pos, WIN_SHARD)
            return dsts[a].at[pos, layer]

        def ici(a, j, src, dst):
            return _rcopy(src, dst, send.at[a, j], recv.at[a, j], (*chips[j], c))

        def d2d(a, j, src, dst):
            return _rcopy(src, dst, send.at[a, 3 + j], recv.at[a, 3 + j], sib)

        def own(a, layer, src, dst):
            return _rcopy(src, dst, own_send.at[a, layer], own_recv.at[a, layer], sib)

        for a in range(n):
            for j in range(3):
                _start_pieces(srcs[a].at[c], slot(a, c, k), functools.partial(ici, a, j))
        for a in range(n):
            for layer in range(DEPTH):
                _start_pieces(srcs[a].at[layer], slot(a, layer, k), functools.partial(own, a, layer))
        for a in range(n):
            for j, (cx, cy) in enumerate(chips):
                landed = slot(a, c, 2 * cx + cy)
                ici(a, j, landed, landed).wait_recv()
                _start_pieces(landed, landed, functools.partial(d2d, a, j))
        for a in range(n):
            for j, (cx, cy) in enumerate(chips):
                passed = slot(a, 1 - c, 2 * cx + cy)
                d2d(a, j, passed, passed).wait_recv()
                landed = slot(a, c, 2 * cx + cy)
                d2d(a, j, landed, landed).wait_send()
                ici(a, j, srcs[a].at[c], slot(a, c, k)).wait_send()
            for layer in range(DEPTH):
                own(a, layer, srcs[a].at[layer], slot(a, layer, k)).wait()

    outs = [jax.ShapeDtypeStruct((2, shards[0].shape[1], NIN), shards[0].dtype)]
    outs += [jax.ShapeDtypeStruct((NCHIP,) + s.shape, s.dtype) for s in shards[1:]]
    return pl.pallas_call(
        body, name="gather_halves", in_specs=[ANY] * n, out_specs=[ANY] * n, out_shape=outs,
        scratch_shapes=[pltpu.SemaphoreType.DMA((n, 6)), pltpu.SemaphoreType.DMA((n, 6)),
                        pltpu.SemaphoreType.DMA((n, DEPTH)), pltpu.SemaphoreType.DMA((n, DEPTH))],
    )(*shards)


def _gather_slot(dst, pos):
    return _cols(dst, pos, WIN_SHARD) if len(dst.shape) == 2 else dst.at[pos]


def _gather_copy(a, j, src, dst, send, recv, dev):
    return _rcopy(src, dst, send.at[a * NCHIP + j], recv.at[a * NCHIP + j], dev)


def gather_start(tag, shards, after):
    n = len(shards)

    def body(*refs):
        srcs, dsts = refs[:n], refs[n:2 * n]
        send, recv = refs[2 * n + len(after)], refs[2 * n + len(after) + 1]
        token = refs[-1]
        x, y, c, k = _place()
        peers = [(1 - x, y, c), (x, 1 - y, c), (1 - x, 1 - y, c), (x, y, 1 - c)]
        for a in range(n):
            for j, dev in enumerate(peers):
                _start_pieces(srcs[a], _gather_slot(dsts[a], k),
                              lambda s, d, a=a, j=j, dev=dev: _gather_copy(a, j, s, d, send, recv, dev))
        token[...] = jnp.zeros_like(token)

    gathered = [lax.empty((D, NIN) if i == 0 else (NCHIP,) + s.shape, s.dtype) for i, s in enumerate(shards)]
    ops = [pltpu.with_memory_space_constraint(v, pltpu.HBM) for v in list(shards) + gathered]
    sem = pltpu.SemaphoreType.DMA((n * NCHIP,))
    res = pl.pallas_call(
        body, name=f"gather_start{tag}", in_specs=[HBM] * (2 * n) + [ANY] * len(after),
        out_specs=[SEMS, SEMS] + [HBM] * (2 * n) + [pl.BlockSpec(memory_space=pltpu.VMEM)],
        out_shape=[sem, sem] + [pltpu.HBM(v.shape, v.dtype) for v in ops] + [jax.ShapeDtypeStruct((8, LANES), f32)],
        input_output_aliases={i: 2 + i for i in range(2 * n)},
        compiler_params=pltpu.CompilerParams(has_side_effects=EFFECT),
    )(*ops, *after)
    return res[0], res[1], res[2:2 + n], res[2 + n:2 + 2 * n], res[-1]


def gather_wait(tag, send, recv, shards, gathered, after):
    n = len(shards)

    def body(*refs):
        srcs, dsts = refs[:n], refs[n:2 * n]
        send_r, recv_r = refs[2 * n], refs[2 * n + 1]
        x, y, c, k = _place()
        peers = [(1 - x, y, c), (x, 1 - y, c), (1 - x, 1 - y, c), (x, y, 1 - c)]
        for a in range(n):
            for j, dev in enumerate(peers):
                _gather_copy(a, j, srcs[a], _gather_slot(dsts[a], k), send_r, recv_r, dev).wait_send()
                pos = 2 * dev[0] + dev[1]
                _gather_copy(a, j, srcs[a], _gather_slot(dsts[a], pos), send_r, recv_r, dev).wait_recv()

    ops = list(shards) + list(gathered)
    res = pl.pallas_call(
        body, name=f"gather_wait{tag}", in_specs=[HBM] * (2 * n) + [SEMS, SEMS] + [ANY] * len(after),
        out_specs=[HBM] * (2 * n), out_shape=[pltpu.HBM(v.shape, v.dtype) for v in ops],
        input_output_aliases={i: i for i in range(2 * n)},
        compiler_params=pltpu.CompilerParams(has_side_effects=EFFECT),
    )(*ops, send, recv, *after)
    return res[n:]


def _half(ref, h):
    rows = ref.shape[-2] // 2
    start = pl.multiple_of(h * rows, 16)
    if len(ref.shape) == 2:
        return ref.at[pl.ds(start, rows), :]
    return ref.at[:, pl.ds(start, rows), :]


HBM = pl.BlockSpec(memory_space=pltpu.HBM)
SEMS = pl.BlockSpec(memory_space=pltpu.SEMAPHORE)
EFFECT = pltpu.SideEffectType.DATAFLOW_SIDE_EFFECTING


def rs_pair_start(tag, grads):
    n = len(grads)

    def body(*refs):
        g, theirs = refs[:n], refs[n:2 * n]
        send, recv = refs[2 * n], refs[2 * n + 1]
        x, y, c, _ = _place()
        for a in range(n):
            _start_pieces(_half(g[a], 1 - c), theirs[a],
                          lambda s, d, a=a: _rcopy(s, d, send.at[a], recv.at[a], (x, y, 1 - c)))
        refs[-1][...] = jnp.zeros_like(refs[-1])

    lands = [lax.empty(g.shape[:-2] + (g.shape[-2] // 2, g.shape[-1]), g.dtype) for g in grads]
    ops = [pltpu.with_memory_space_constraint(v, pltpu.HBM) for v in list(grads) + lands]
    sem = pltpu.SemaphoreType.DMA((n,))
    res = pl.pallas_call(
        body, name=f"rs_pair_start{tag}", in_specs=[HBM] * (2 * n),
        out_specs=[SEMS, SEMS] + [HBM] * (2 * n) + [pl.BlockSpec(memory_space=pltpu.VMEM)],
        out_shape=[sem, sem] + [pltpu.HBM(v.shape, v.dtype) for v in ops] + [jax.ShapeDtypeStruct((8, LANES), f32)],
        input_output_aliases={i: 2 + i for i in range(2 * n)},
        compiler_params=pltpu.CompilerParams(has_side_effects=EFFECT),
    )(*ops)
    return res[0], res[1], res[2:2 + n], res[2 + n:2 + 2 * n], res[-1]


def rs_pair_wait(tag, send, recv, grads, theirs, after):
    n = len(grads)

    def body(*refs):
        g, land = refs[:n], refs[n:2 * n]
        send_r, recv_r = refs[2 * n], refs[2 * n + 1]
        x, y, c, _ = _place()
        for a in range(n):
            cp = _rcopy(_half(g[a], 1 - c), land[a], send_r.at[a], recv_r.at[a], (x, y, 1 - c))
            cp.wait_send()
            cp.wait_recv()

    ops = list(grads) + list(theirs)
    res = pl.pallas_call(
        body, name=f"rs_pair_wait{tag}", in_specs=[HBM] * (2 * n) + [SEMS, SEMS] + [ANY] * len(after),
        out_specs=[HBM] * (2 * n), out_shape=[pltpu.HBM(v.shape, v.dtype) for v in ops],
        input_output_aliases={i: i for i in range(2 * n)},
        compiler_params=pltpu.CompilerParams(has_side_effects=EFFECT),
    )(*ops, send, recv, *after)
    return res[:n], res[n:]


def _chip_piece(ref, k):
    return _cols(ref, k, WIN_SHARD) if len(ref.shape) == 2 else ref.at[k]


def _chip_copy(a, k, src, dst, send, recv, me, c):
    return _rcopy(src, dst, send.at[a * NCHIP + k], recv.at[a * NCHIP + me], (k // 2, k % 2, c))


def rs_chips_start(tag, sums):
    n = len(sums)

    def pshape(s):
        return (NCHIP, s[0], WIN_SHARD) if len(s) == 2 else s

    def body(*refs):
        s, land = refs[:n], refs[n:2 * n]
        send, recv = refs[2 * n], refs[2 * n + 1]
        token = refs[-1]
        x, y, c, me = _place()
        for k in range(NCHIP):
            @pl.when(me != k)
            def _():
                for a in range(n):
                    _start_pieces(_chip_piece(s[a], k), land[a].at[me],
                                  lambda src, dst, a=a: _chip_copy(a, k, src, dst, send, recv, me, c))
        token[...] = jnp.zeros_like(token)

    lands = [lax.empty(pshape(v.shape), v.dtype) for v in sums]
    ops = [pltpu.with_memory_space_constraint(v, pltpu.HBM) for v in list(sums) + lands]
    sem = pltpu.SemaphoreType.DMA((n * NCHIP,))
    res = pl.pallas_call(
        body, name=f"rs_chips_start{tag}", in_specs=[HBM] * (2 * n),
        out_specs=[SEMS, SEMS] + [HBM] * (2 * n) + [pl.BlockSpec(memory_space=pltpu.VMEM)],
        out_shape=[sem, sem] + [pltpu.HBM(v.shape, v.dtype) for v in ops] + [jax.ShapeDtypeStruct((8, LANES), f32)],
        input_output_aliases={i: 2 + i for i in range(2 * n)},
        compiler_params=pltpu.CompilerParams(has_side_effects=EFFECT),
    )(*ops)
    return res[0], res[1], res[2:2 + n], res[2 + n:2 + 2 * n], res[-1]


def rs_chips_wait(tag, send, recv, sums, lands, after):
    n = len(sums)

    def body(*refs):
        s, land = refs[:n], refs[n:2 * n]
        send_r, recv_r = refs[2 * n], refs[2 * n + 1]
        x, y, c, me = _place()
        for k in range(NCHIP):
            @pl.when(me != k)
            def _():
                for a in range(n):
                    piece = _chip_piece(s[a], k)
                    _chip_copy(a, k, piece, land[a].at[me], send_r, recv_r, me, c).wait_send()
                    _rcopy(piece, land[a].at[k], send_r.at[a * NCHIP + k], recv_r.at[a * NCHIP + k],
                           (k // 2, k % 2, c)).wait_recv()

    ops = list(sums) + list(lands)
    res = pl.pallas_call(
        body, name=f"rs_chips_wait{tag}", in_specs=[HBM] * (2 * n) + [SEMS, SEMS] + [ANY] * len(after),
        out_specs=[HBM] * (2 * n), out_shape=[pltpu.HBM(v.shape, v.dtype) for v in ops],
        input_output_aliases={i: i for i in range(2 * n)},
        compiler_params=pltpu.CompilerParams(has_side_effects=EFFECT),
    )(*ops, send, recv, *after)
    return res[:n], res[n:]


def rs_join(tag, halves):
    n = len(halves)

    def body(*refs):
        h, other = refs[:n], refs[n:2 * n]
        send, recv = refs[2 * n:]
        x, y, c, _ = _place()

        def give(a, s, d):
            return _rcopy(s, d, send.at[a], recv.at[a], (x, y, 1 - c))

        for a in range(n):
            _start_pieces(h[a], other[a], functools.partial(give, a))
        for a in range(n):
            give(a, h[a], other[a]).wait()

    outs = [jax.ShapeDtypeStruct(v.shape, v.dtype) for v in halves]
    return pl.pallas_call(
        body, name=f"rs_join{tag}", in_specs=[ANY] * n, out_specs=[ANY] * n, out_shape=outs,
        scratch_shapes=[pltpu.SemaphoreType.DMA((n,))] * 2,
    )(*halves)


def _row_tile(rows, cols, itemsize=4, target=2 << 20):
    best = 8
    for t in range(8, rows + 1, 8):
        if rows % t == 0 and t * cols * itemsize <= target:
            best = t
    return best


GRAD_WIRE = jnp.bfloat16


def add_n(name, terms, out_dtype=f32):
    shape = terms[0].shape
    cols = shape[-1]
    rows = math.prod(shape[:-1])
    tr = _row_tile(rows, cols)

    def body(*refs):
        acc = refs[0][...]
        for r in refs[1:-1]:
            acc = acc + r[...]
        refs[-1][...] = acc.astype(out_dtype)

    tile = pl.BlockSpec((tr, cols), lambda i: (i, 0))
    out = pl.pallas_call(
        body, name=name, grid=(rows // tr,), in_specs=[tile] * len(terms), out_specs=tile,
        out_shape=jax.ShapeDtypeStruct((rows, cols), out_dtype), compiler_params=_cp(("parallel",)),
    )(*[t.reshape(rows, cols) for t in terms])
    return out.reshape(shape)


def add_chips(name, land, own):
    _, rows, cols = land.shape
    tr = _row_tile(rows, cols, target=1 << 20)

    def body(land_r, own_r, o_r):
        me = 2 * lax.axis_index("x") + lax.axis_index("y")
        for k in range(NCHIP):
            @pl.when(me == k)
            def _():
                acc = None
                for j in range(NCHIP):
                    t = (own_r[...] if j == k else land_r[j]).astype(f32)
                    acc = t if acc is None else acc + t
                o_r[...] = acc

    tile = pl.BlockSpec((tr, cols), lambda i: (i, 0))
    return pl.pallas_call(
        body, name=name, grid=(rows // tr,), in_specs=[pl.BlockSpec((NCHIP, tr, cols), lambda i: (0, i, 0)), tile],
        out_specs=tile, out_shape=jax.ShapeDtypeStruct((rows, cols), f32), compiler_params=_cp(("parallel",)),
    )(land, own)


def reduce_scatter_pair(tag, G):
    names = tuple(G)
    grads = [G[n] if G[n].ndim == 3 or n == "w_in" else G[n].reshape(NCHIP, D // NCHIP, D) for n in names]
    send, recv, grads, theirs, token = rs_pair_start(tag, grads)
    return (tag, names, send, recv, grads, theirs), token[0, 0]


def reduce_scatter_chips(state, after):
    c = lax.axis_index("c")
    tag, names, send, recv, grads, theirs = state
    grads, theirs = rs_pair_wait(tag, send, recv, grads, theirs, after)
    sums = []
    for n, g, t in zip(names, grads, theirs):
        rows = g.shape[-2] // 2
        mine = lax.dynamic_slice_in_dim(g, c * rows, rows, axis=g.ndim - 2)
        sums.append(add_n(f"rs_add_pair{tag}_{n}", [mine, t], GRAD_WIRE))
    send, recv, sums, lands, token = rs_chips_start(tag, sums)
    return (tag, names, send, recv, sums, lands), token[0, 0]


def reduce_scatter_finish(state, after):
    me = 2 * lax.axis_index("x") + lax.axis_index("y")
    tag, names, send, recv, sums, lands = state
    sums, landed = rs_chips_wait(tag, send, recv, sums, lands, after)
    halves = []
    for n, s, v in zip(names, sums, landed):
        own = lax.dynamic_slice_in_dim(s, me * WIN_SHARD, WIN_SHARD, axis=1) if s.ndim == 2 else \
            lax.dynamic_index_in_dim(s, me, 0, keepdims=False)
        halves.append(add_chips(f"rs_add_chips{tag}_{n}", v, own))
    return dict(zip(names, zip(halves, rs_join(tag, halves))))


NDEV = 8


def _small_copy(r, src, dst, send, recv, x, y, c):
    return _rcopy(src, dst, send.at[r - 1], recv.at[r - 1], (x ^ (r >> 2), y ^ ((r >> 1) & 1), c ^ (r & 1)))


def small_start(pack):
    def body(p, land, send, recv, p_thru, land_thru, token):
        x, y, c, _ = _place()
        me = 4 * x + 2 * y + c
        for r in range(1, NDEV):
            _start_pieces(p, land.at[me], lambda s, d, r=r: _small_copy(r, s, d, send, recv, x, y, c), 128 << 10)
        token[...] = jnp.zeros_like(token)

    ops = [pltpu.with_memory_space_constraint(v, pltpu.HBM) for v in (pack, lax.empty((NDEV,) + pack.shape, f32))]
    sem = pltpu.SemaphoreType.DMA((NDEV - 1,))
    return pl.pallas_call(
        body, name="small_start", in_specs=[HBM, HBM],
        out_specs=[SEMS, SEMS, HBM, HBM, pl.BlockSpec(memory_space=pltpu.VMEM)],
        out_shape=[sem, sem] + [pltpu.HBM(v.shape, v.dtype) for v in ops] + [jax.ShapeDtypeStruct((8, LANES), f32)],
        input_output_aliases={0: 2, 1: 3}, compiler_params=pltpu.CompilerParams(has_side_effects=EFFECT),
    )(*ops)


def small_wait(send, recv, pack, land, after):
    def body(p, land_r, send_r, recv_r, *rest):
        x, y, c, _ = _place()
        me = 4 * x + 2 * y + c
        for r in range(1, NDEV):
            _small_copy(r, p, land_r.at[me], send_r, recv_r, x, y, c).wait_send()
            src = 4 * (x ^ (r >> 2)) + 2 * (y ^ ((r >> 1) & 1)) + (c ^ (r & 1))
            _small_copy(r, p, land_r.at[src], send_r, recv_r, x, y, c).wait_recv()

    return pl.pallas_call(
        body, name="small_wait", in_specs=[HBM, HBM, SEMS, SEMS] + [ANY] * len(after), out_specs=[HBM, HBM],
        out_shape=[pltpu.HBM(pack.shape, f32), pltpu.HBM(land.shape, f32)], input_output_aliases={0: 0, 1: 1},
        compiler_params=pltpu.CompilerParams(has_side_effects=EFFECT),
    )(pack, land, send, recv, *after)


def small_sum(land, pack):
    def body(land_r, p_r, o_r):
        me = 4 * lax.axis_index("x") + 2 * lax.axis_index("y") + lax.axis_index("c")
        for k in range(NDEV):
            @pl.when(me == k)
            def _():
                acc = None
                for d in range(NDEV):
                    t = p_r[...] if d == k else land_r[d]
                    acc = t if acc is None else acc + t
                o_r[...] = acc

    vm = pl.BlockSpec(memory_space=pltpu.VMEM)
    return pl.pallas_call(
        body, name="small_sum", in_specs=[vm, vm], out_specs=vm, out_shape=jax.ShapeDtypeStruct(pack.shape, f32),
        compiler_params=pltpu.CompilerParams(vmem_limit_bytes=40 << 20),
    )(land, pack)


def _adamw_math(w, g, m, v):
    m = ADAM_B1 * m + (1.0 - ADAM_B1) * g
    v = ADAM_B2 * v + (1.0 - ADAM_B2) * (g * g)
    m_hat = m / (1.0 - ADAM_B1 ** ADAM_STEP)
    v_hat = v / (1.0 - ADAM_B2 ** ADAM_STEP)
    return -ADAM_LR * (m_hat / (jnp.sqrt(v_hat) + ADAM_EPS) + ADAM_WD * w), m, v


def adamw_big(name, halves, w, m, v):
    _, R, C = w.shape
    tr = _row_tile(R // 2, C, target=1 << 20)
    nt = R // 2 // tr

    def body(a0, b0, a1, b1, w_r, m_r, v_r, g_o, d_o, m_o, v_o):
        mine = pl.program_id(1) == lax.axis_index("c")
        g = jnp.where(pl.program_id(0) == 0, jnp.where(mine, a0[...], b0[...]), jnp.where(mine, a1[...], b1[...]))
        g_o[...] = g
        d_o[...], m_o[...], v_o[...] = _adamw_math(w_r[...], g, m_r[...], v_r[...])

    stk = pl.BlockSpec((None, tr, C), lambda l, h, i: (l, h * nt + i, 0))
    lay0 = pl.BlockSpec((tr, C), lambda l, h, i: (jnp.where(l == 0, i, nt - 1), 0))
    lay1 = pl.BlockSpec((tr, C), lambda l, h, i: (jnp.where(l == 0, 0, i), 0))
    return pl.pallas_call(
        body, name=name, grid=(DEPTH, 2, nt),
        in_specs=[lay0, lay0, lay1, lay1, stk, stk, stk],
        out_specs=[stk] * 4, out_shape=[jax.ShapeDtypeStruct(w.shape, f32)] * 4,
        compiler_params=_cp(("arbitrary", "arbitrary", "arbitrary")),
    )(*halves[0], *halves[1], w, m, v)


def adamw_small(name, g, w, m, v):
    def body(g_r, w_r, m_r, v_r, d_o, m_o, v_o):
        d_o[...], m_o[...], v_o[...] = _adamw_math(w_r[...], g_r[...], m_r[...], v_r[...])

    return pl.pallas_call(body, name=name, out_shape=[jax.ShapeDtypeStruct(w.shape, f32)] * 3)(g, w, m, v)


WEIGHTS = ("w_in", "conv_w", "gmlp_ln_g", "gmlp_ln_b", "w_s", "b_s", "p_a", "p_b", "p_c", "w_o", "ln1_g", "ln1_b",
           "w_gate", "w_up", "w_down", "ln2_g", "ln2_b")
VECS = ("ln1_g", "ln1_b", "ln2_g", "ln2_b", "gmlp_ln_g", "gmlp_ln_b")
ROWS_VEC, ROWS_BS, ROWS_WS, ROWS_CONV = D // LANES, 8, 8 * BLK, 3 * D // LANES
ROWS_LAYER = len(VECS) * ROWS_VEC + ROWS_BS + ROWS_WS + ROWS_CONV


def _pack_small(per_layer, tail):
    parts = []
    for P in per_layer:
        parts += [P[n].reshape(ROWS_VEC, LANES) for n in VECS]
        parts += [P["b_s"].reshape(ROWS_BS, LANES), P["w_s"].reshape(ROWS_WS, LANES), P["conv_w"].reshape(ROWS_CONV, LANES)]
    return jnp.concatenate(parts + [tail], axis=0)


def _unpack_small(pack):
    out = []
    for l in range(DEPTH):
        r = l * ROWS_LAYER
        P = {}
        for n in VECS:
            P[n] = pack[r:r + ROWS_VEC].reshape(D)
            r += ROWS_VEC
        P["b_s"] = pack[r:r + ROWS_BS].reshape(8, BLK)
        r += ROWS_BS
        P["w_s"] = pack[r:r + ROWS_WS].reshape(8, BLK, BLK)
        r += ROWS_WS
        P["conv_w"] = pack[r:r + ROWS_CONV].reshape(3, D)
        out.append(P)
    return out, pack[DEPTH * ROWS_LAYER:]


def kernel(x, positions, w_in, conv_w, gmlp_ln_g, gmlp_ln_b, w_s, b_s, p_a, p_b, p_c, w_o, ln1_g, ln1_b, w_gate, w_up, w_down, ln2_g, ln2_b, loss_target, m_w_in, m_conv_w, m_gmlp_ln_g, m_gmlp_ln_b, m_w_s, m_b_s, m_p_a, m_p_b, m_p_c, m_w_o, m_ln1_g, m_ln1_b, m_w_gate, m_w_up, m_w_down, m_ln2_g, m_ln2_b, v_w_in, v_conv_w, v_gmlp_ln_g, v_gmlp_ln_b, v_w_s, v_b_s, v_p_a, v_p_b, v_p_c, v_w_o, v_ln1_g, v_ln1_b, v_w_gate, v_w_up, v_w_down, v_ln2_g, v_ln2_b):
    Wt = dict(w_in=w_in, conv_w=conv_w, gmlp_ln_g=gmlp_ln_g, gmlp_ln_b=gmlp_ln_b, w_s=w_s, b_s=b_s, p_a=p_a, p_b=p_b,
              p_c=p_c, w_o=w_o, ln1_g=ln1_g, ln1_b=ln1_b, w_gate=w_gate, w_up=w_up, w_down=w_down, ln2_g=ln2_g, ln2_b=ln2_b)
    Mt = dict(w_in=m_w_in, conv_w=m_conv_w, gmlp_ln_g=m_gmlp_ln_g, gmlp_ln_b=m_gmlp_ln_b, w_s=m_w_s, b_s=m_b_s, p_a=m_p_a,
              p_b=m_p_b, p_c=m_p_c, w_o=m_w_o, ln1_g=m_ln1_g, ln1_b=m_ln1_b, w_gate=m_w_gate, w_up=m_w_up,
              w_down=m_w_down, ln2_g=m_ln2_g, ln2_b=m_ln2_b)
    Vt = dict(w_in=v_w_in, conv_w=v_conv_w, gmlp_ln_g=v_gmlp_ln_g, gmlp_ln_b=v_gmlp_ln_b, w_s=v_w_s, b_s=v_b_s, p_a=v_p_a,
              p_b=v_p_b, p_c=v_p_c, w_o=v_w_o, ln1_g=v_ln1_g, ln1_b=v_ln1_b, w_gate=v_w_gate, w_up=v_w_up,
              w_down=v_w_down, ln2_g=v_ln2_g, ln2_b=v_ln2_b)
    chip = 2 * lax.axis_index("x") + lax.axis_index("y")
    cw = D // NCHIP

    def layer_weights(l, gathered, conv_all):
        Wl = dict(zip(BIG, gathered))
        for n in ("p_a", "p_c", "w_o"):
            Wl[n] = Wl[n].reshape(D, D)
        Wl["conv_w"] = conv_all[:, l].transpose(1, 0, 2).reshape(3, D)
        for n in VECS + ("w_s", "b_s"):
            Wl[n] = Wt[n][l]
        return Wl

    halves = [Wt[n][0].astype(MX).reshape(2, Wt[n].shape[1] // 2, Wt[n].shape[2]) for n in BIG]
    got = gather_halves(halves + [conv_w])
    conv_all = got[NBIG]
    g0 = [got[0].reshape(D, NIN)] + [a.reshape(NCHIP, 2 * a.shape[2], a.shape[3]) for a in got[1:NBIG]]
    send1, recv1, sh1, g1, coming = gather_start("1", [Wt[n][1].astype(MX) for n in BIG], [conv_all])
    W0 = layer_weights(0, g0, conv_all)
    W0["after"] = coming[0, 0]

    def W1(h):
        return layer_weights(1, gather_wait("1", send1, recv1, sh1, g1, [h]), conv_all)

    layers = [W0, W1]

    rs_state, rs_started, held = {}, {}, {}

    def start_exchange(l, g):
        if "loss" in g:
            held[l] = g
            return None
        if "conv_w" in g:
            held[l] = g
            rs_state[(l, False)], started = reduce_scatter_chips(rs_state[(l, False)], [g["w_s"], g["conv_w"]])
            if l == 0:
                pack = _pack_small([held[j] for j in range(DEPTH)], held[DEPTH]["loss"])
                *held["small"], token = small_start(pack)
                started = started + token[0, 0]
            return started
        if "dx" in g:
            rs_state[(l, True)], rs_started[(l, True)] = reduce_scatter_chips(rs_state[(l, True)], [g["dx"]])
            return rs_started[(l, True)]
        key = (l, "w_in" in g)
        rs_state[key], started = reduce_scatter_pair(f"{l}{'b' if key[1] else 'a'}", g)
        return started

    _, grad_x, _ = local_step(x[0], positions[0], loss_target[0], layers, start_exchange)

    last = jnp.zeros((8, LANES), f32) + rs_started[(0, True)]
    behind = [grad_x, last]
    red = [dict() for _ in range(DEPTH)]
    for key in ((1, False), (1, True), (0, False)):
        red[key[0]].update(reduce_scatter_finish(rs_state[key], behind))
    small, tail = _unpack_small(small_sum(*reversed(small_wait(*held["small"], behind))))
    loss = tail[0, 0]

    G, DW, NM, NV = {}, {}, {}, {}
    zc = jnp.zeros((3, D), f32)
    wp = _pack_small([{**{n: Wt[n][l] for n in VECS + ("b_s", "w_s")}, "conv_w": zc} for l in range(DEPTH)], jnp.zeros((8, LANES), f32))
    mp = _pack_small([{**{n: Mt[n][l] for n in VECS + ("b_s", "w_s")}, "conv_w": zc} for l in range(DEPTH)], jnp.zeros((8, LANES), f32))
    vp = _pack_small([{**{n: Vt[n][l] for n in VECS + ("b_s", "w_s")}, "conv_w": zc} for l in range(DEPTH)], jnp.ones((8, LANES), f32))
    gp = _pack_small(small, jnp.zeros((8, LANES), f32))
    outs = [_unpack_small(a)[0] for a in adamw_small("adamw_small", gp, wp, mp, vp)]
    for n in VECS + ("b_s", "w_s"):
        G[n] = jnp.stack([small[l][n] for l in range(DEPTH)])
        DW[n], NM[n], NV[n] = (jnp.stack([o[l][n] for l in range(DEPTH)]) for o in outs)
    gconv = jnp.stack([lax.dynamic_slice(small[l]["conv_w"], (0, chip * cw), (3, cw)) for l in range(DEPTH)])
    G["conv_w"] = gconv
    flat = lambda a: a.reshape(DEPTH * 3, cw)
    d, m2, v2 = adamw_small("adamw_conv", flat(gconv), flat(conv_w), flat(m_conv_w), flat(v_conv_w))
    DW["conv_w"], NM["conv_w"], NV["conv_w"] = (a.reshape(DEPTH, 3, cw) for a in (d, m2, v2))

    updated = {}
    for n in BIG[1:]:
        tr = (lambda a: jnp.swapaxes(a, 1, 2)) if n in ("w_gate", "w_up") else (lambda a: a)
        updated[n] = adamw_big("adamw_" + n, (red[0][n], red[1][n]), tr(Wt[n]), tr(Mt[n]), tr(Vt[n]))
        G[n], DW[n], NM[n], NV[n] = map(tr, updated[n])
    done = [d, DW["ln2_b"], red[1]["w_in"][1]] + [updated[n][1] for n in BIG[1:]]
    red[0].update(reduce_scatter_finish(rs_state[(0, True)], done))
    G["w_in"], DW["w_in"], NM["w_in"], NV["w_in"] = adamw_big(
        "adamw_w_in", (red[0]["w_in"], red[1]["w_in"]), Wt["w_in"], Mt["w_in"], Vt["w_in"])

    return (loss, grad_x[None], *[G[n] for n in WEIGHTS], *[DW[n] for n in WEIGHTS], *[NM[n] for n in WEIGHTS],
            *[NV[n] for n in WEIGHTS])
```

```python
import functools
import math

import jax
import jax.numpy as jnp
from jax import lax
from jax.experimental import pallas as pl
from jax.experimental.pallas import tpu as pltpu

D = 1024
NIN = 12800
DFF = 2816
NCHIP = 4
FB = DFF // NCHIP
WIN_SHARD = NIN // NCHIP
DEPTH = 2
GROUPS = ((128, 1), (512, 4), (2048, 16))
HD = 64
BLK = 128
AO = 512
ALPHA = (2 * DEPTH) ** 0.25
EPS = 1e-5
ROPE_THETA = 10000.0
LANES = 128
NEG = -1e30

C_GATES, C_BCH, C_QKV, C_UV = 0, 3 * D, 6 * D, 6 * D + 9 * AO

MX = jnp.bfloat16
ACT = jnp.bfloat16

ADAM_LR, ADAM_B1, ADAM_B2, ADAM_EPS, ADAM_WD, ADAM_STEP = 0.001, 0.9, 0.999, 1e-08, 0.01, 10

f32 = jnp.float32
NT = (((1,), (1,)), ((), ()))
TN = (((0,), (0,)), ((), ()))


def _cp(sem, vmem_mb=48):
    return pltpu.CompilerParams(dimension_semantics=sem, vmem_limit_bytes=vmem_mb << 20)


def _dot(a, b, dims=None):
    if dims is None:
        return jnp.dot(a, b, preferred_element_type=f32)
    return lax.dot_general(a, b, dims, preferred_element_type=f32)


def _ln_stats(r):
    mu = jnp.mean(r, axis=-1, keepdims=True)
    xc = r - mu
    var = jnp.mean(xc * xc, axis=-1, keepdims=True)
    rstd = lax.rsqrt(var + EPS)
    return xc * rstd, rstd


def _ln_bwd(dy, xhat, rstd, g):
    dxh = dy * g
    return rstd * (dxh - jnp.mean(dxh, axis=-1, keepdims=True) - xhat * jnp.mean(dxh * xhat, axis=-1, keepdims=True))


def _gelu(x):
    return 0.5 * x * (1.0 + lax.erf(x * (1.0 / math.sqrt(2.0))))


def _gelu_and_grad(x):
    cdf = 0.5 * (1.0 + lax.erf(x * (1.0 / math.sqrt(2.0))))
    return x * cdf, cdf + x * jnp.exp(-0.5 * x * x) * (1.0 / math.sqrt(2.0 * math.pi))


def _sigmoid(x):
    return 0.5 * jnp.tanh(0.5 * x) + 0.5


def _acc_rows(o_ref, first, val):
    @pl.when(first)
    def _():
        o_ref[...] = jnp.zeros_like(o_ref)
    o_ref[...] += jnp.sum(val, axis=0, keepdims=True)


def mm_in(x, w, bias):
    T = x.shape[0]
    tm, tn = min(1024, T), 1280

    def body(x_ref, w_ref, b_ref, o_ref, xb):
        @pl.when(pl.program_id(1) == 0)
        def _():
            xb[...] = x_ref[...].astype(MX)
        o_ref[...] = (_dot(xb[...], w_ref[...]) + b_ref[...]).astype(o_ref.dtype)

    return pl.pallas_call(
        body, name="mm_in", grid=(T // tm, NIN // tn),
        in_specs=[pl.BlockSpec((tm, D), lambda i, j: (i, 0)), pl.BlockSpec((D, tn), lambda i, j: (0, j)),
                  pl.BlockSpec((1, tn), lambda i, j: (0, j))],
        out_specs=pl.BlockSpec((tm, tn), lambda i, j: (i, j)),
        out_shape=jax.ShapeDtypeStruct((T, NIN), ACT),
        scratch_shapes=[pltpu.VMEM((tm, D), MX)],
        compiler_params=_cp(("parallel", "arbitrary")),
    )(x, w, bias)


HALO = 16
TM_AC = 256


def _uv_specs():
    return [pl.BlockSpec((TM_AC, 512), functools.partial(lambda i, j: (i, j), j=C_UV // 512 + j)) for j in range(4)]


def _gmlp_fwd(up, vp, ws_ref, bs_ref, lg, lb, u=None, gv=None):
    u = _gelu(up) if u is None else u
    xhat, rstd = _ln_stats(_gelu(vp) if gv is None else gv)
    vn = xhat * lg + lb
    vnb = vn.astype(MX)
    rows = []
    for c in range(up.shape[0] // BLK):
        r = slice(c * BLK, (c + 1) * BLK)
        rows.append(jnp.concatenate(
            [_dot(ws_ref[g], vnb[r, g * BLK:(g + 1) * BLK]) + bs_ref[g] for g in range(8)], axis=1))
    return u, vn, xhat, rstd, jnp.concatenate(rows, axis=0)


def mix_ac_fwd(proj, conv_w, wst, bsx, lg, lb):
    T = proj.shape[0]
    tm = TM_AC

    def body(bch, halo, u0, u1, v0, v1, cw, ws, bs, lg_ref, lb_ref, ya, yc, zs):
        i = pl.program_id(0)
        pb = bch[...].astype(f32)
        z = pb[:, D:2 * D] * pb[:, 2 * D:]
        hz = halo[:, :D].astype(f32) * halo[:, D:].astype(f32)
        zs[0:HALO, :] = jnp.where(i > 0, hz, 0.0)
        zs[HALO:HALO + tm, :] = z
        cv = cw[0:1, :] * zs[HALO - 2:HALO - 2 + tm, :] + cw[1:2, :] * zs[HALO - 1:HALO - 1 + tm, :] + cw[2:3, :] * z
        ya[...] = (pb[:, :D] * cv).astype(ya.dtype)
        up = jnp.concatenate([u0[...], u1[...]], axis=1).astype(f32)
        vp = jnp.concatenate([v0[...], v1[...]], axis=1).astype(f32)
        u, _, _, _, sp = _gmlp_fwd(up, vp, ws, bs, lg_ref[...], lb_ref[...])
        yc[...] = (u * sp).astype(yc.dtype)

    full = lambda shape: pl.BlockSpec(shape, lambda i: (0,) * len(shape))
    return pl.pallas_call(
        body, name="mix_ac_fwd", grid=(T // tm,),
        in_specs=[pl.BlockSpec((tm, 3 * D), lambda i: (i, 1)),
                  pl.BlockSpec((HALO, 2 * D), lambda i: (jnp.maximum(i * (tm // HALO) - 1, 0), 2)),
                  *_uv_specs(), full((3, D)), full((8, BLK, BLK)), full((8, BLK, BLK)), full((1, D)), full((1, D))],
        out_specs=[pl.BlockSpec((tm, D), lambda i: (i, 0))] * 2,
        out_shape=[jax.ShapeDtypeStruct((T, D), MX)] * 2,
        scratch_shapes=[pltpu.VMEM((HALO + tm, D), f32)],
        compiler_params=_cp(("parallel",)),
    )(proj, proj, proj, proj, proj, proj, conv_w, wst, bsx, lg, lb)


def _swap_halves(x):
    lane = lax.broadcasted_iota(jnp.int32, x.shape, 1)
    return jnp.where((lane % HD) < HD // 2, pltpu.roll(x, x.shape[1] - HD // 2, 1), pltpu.roll(x, HD // 2, 1))


def _tile4(t):
    return jnp.concatenate([t] * (AO // LANES), axis=1)


TM_FOLD = 512


def _fold_out(nat, x, out_ref, d):
    if d == 1:
        out_ref[0] = x.astype(out_ref.dtype)
        return
    rows = x.shape[0] // d
    for j in range(AO // LANES):
        nat[j] = x[:, j * LANES:(j + 1) * LANES]
    for r in range(d):
        out_ref[r] = jnp.concatenate(
            [nat.at[j][pl.ds(r, rows, stride=d), :] for j in range(AO // LANES)], axis=1).astype(out_ref.dtype)


def _unfold_in(nat, in_ref, d):
    if d == 1:
        return in_ref[0].astype(f32)
    rows = in_ref.shape[1]
    for r in range(d):
        v = in_ref[r].astype(f32)
        for j in range(AO // LANES):
            nat.at[j][pl.ds(r, rows, stride=d), :] = v[:, j * LANES:(j + 1) * LANES]
    return jnp.concatenate([nat[j] for j in range(AO // LANES)], axis=1)


def fold_rope(proj, cos_t, sin_t, g, d):
    T = proj.shape[0]
    tm = TM_FOLD
    rows = tm // d

    def body(x_ref, c_ref, s_ref, q_o, k_o, v_o, nat):
        cos, sin = _tile4(c_ref[...]), _tile4(s_ref[...])
        for part, out, scale in ((0, q_o, HD ** -0.5), (1, k_o, 1.0), (2, v_o, None)):
            x = x_ref[:, part * AO:(part + 1) * AO].astype(f32)
            if scale is not None:
                x = (x * cos + _swap_halves(x) * sin) * scale
            _fold_out(nat, x, out, d)

    fold_spec = pl.BlockSpec((d, rows, AO), lambda i: (0, i, 0))
    return pl.pallas_call(
        body, name=f"fold_rope{g}", grid=(T // tm,),
        in_specs=[pl.BlockSpec((tm, 3 * AO), lambda i: (i, C_QKV // (3 * AO) + g)),
                  pl.BlockSpec((tm, LANES), lambda i: (i, 0)), pl.BlockSpec((tm, LANES), lambda i: (i, 0))],
        out_specs=[fold_spec] * 3,
        out_shape=[jax.ShapeDtypeStruct((d, T // d, AO), MX)] * 3,
        scratch_shapes=[pltpu.VMEM((AO // LANES, tm, LANES), f32)],
        compiler_params=_cp(("parallel",)),
    )(proj, cos_t, sin_t)


def _stack_heads(x):
    lane = lax.broadcasted_iota(jnp.int32, x.shape, 1)
    z = jnp.zeros_like(x)
    return jnp.concatenate([jnp.where(lane < HD, x, z), jnp.where(lane >= HD, x, z)], axis=0)


def _unstack_heads(y):
    lane = lax.broadcasted_iota(jnp.int32, (BLK, LANES), 1)
    return jnp.where(lane < HD, y[:BLK], y[BLK:])


def _window_masks():
    row = lax.broadcasted_iota(jnp.int32, (2 * BLK, 2 * BLK), 0) % BLK
    col = lax.broadcasted_iota(jnp.int32, (2 * BLK, 2 * BLK), 1)
    return (col < BLK) & (col >= row), (col >= BLK) & (col - BLK <= row)


def _two_blocks(ref, b):
    r0 = pl.multiple_of(b * BLK, BLK)
    rp = pl.multiple_of(jnp.maximum(b - 1, 0) * BLK, BLK)
    return jnp.concatenate([ref[pl.ds(rp, BLK), :], ref[pl.ds(r0, BLK), :]], axis=0)


def attn_fwd(qf, kf, vf, g, nb):
    T = qf.shape[0]

    def body(q_ref, k_ref, v_ref, o_ref, l_ref):
        prev_m, cur_m = _window_masks()

        def step(b, carry):
            r0 = pl.multiple_of(b * BLK, BLK)
            qs = _stack_heads(q_ref[pl.ds(r0, BLK), :])
            s = _dot(qs, _two_blocks(k_ref, b), NT)
            s = jnp.where(cur_m | (prev_m & ((b % nb) != 0)), s, NEG)
            m = jnp.max(s, axis=-1, keepdims=True)
            p = jnp.exp(s - m)
            l = jnp.sum(p, axis=-1, keepdims=True)
            o = _dot(p.astype(MX), _two_blocks(v_ref, b)) / l
            o_ref[pl.ds(r0, BLK), :] = _unstack_heads(o)
            l_ref[pl.ds(r0, BLK), :] = _unstack_heads(jnp.broadcast_to(m + jnp.log(l), (2 * BLK, LANES)))
            return carry

        lax.fori_loop(0, T // BLK, step, 0, unroll=4)

    spec = pl.BlockSpec((T, LANES), lambda j: (0, j))
    return pl.pallas_call(
        body, name=f"attn_fwd{g}", grid=(AO // LANES,),
        in_specs=[spec] * 3, out_specs=[spec] * 2,
        out_shape=[jax.ShapeDtypeStruct((T, AO), f32)] * 2,
        compiler_params=_cp(("parallel",), 56),
    )(qf, kf, vf)


def _group_weights(lses):
    m = jnp.maximum(jnp.maximum(lses[0], lses[1]), lses[2])
    e = [jnp.exp(l - m) for l in lses]
    inv = 1.0 / (e[0] + e[1] + e[2])
    return [x * inv for x in e]


def _fold_specs(T, tm):
    specs = []
    for _, d in GROUPS:
        specs.append(pl.BlockSpec((d, tm // d, AO), lambda i: (0, i, 0)))
    return specs


def combine_fwd(os_, lses):
    T = os_[0].shape[0] * os_[0].shape[1]
    tm = TM_FOLD

    def body(o0, o1, o2, l0, l1, l2, y_ref, nat):
        o = [_unfold_in(nat, r, d) for r, (_, d) in zip((o0, o1, o2), GROUPS)]
        ls = [_unfold_in(nat, r, d) for r, (_, d) in zip((l0, l1, l2), GROUPS)]
        w = _group_weights(ls)
        y_ref[...] = (w[0] * o[0] + w[1] * o[1] + w[2] * o[2]).astype(y_ref.dtype)

    specs = _fold_specs(T, tm)
    return pl.pallas_call(
        body, name="combine_fwd", grid=(T // tm,),
        in_specs=specs + specs, out_specs=pl.BlockSpec((tm, AO), lambda i: (i, 0)),
        out_shape=jax.ShapeDtypeStruct((T, AO), MX),
        scratch_shapes=[pltpu.VMEM((AO // LANES, tm, LANES), f32)],
        compiler_params=_cp(("parallel",)),
    )(*os_, *lses)


TM_MIX = 256


def mix_out_fwd(proj, ya, yb, yc, x0, pa, pb, pc, wo, g1, b1):
    T = x0.shape[0]
    tm = min(TM_MIX, T)

    def body(gt, ya_r, yb_r, yc_r, x0_r, pa_r, pb_r, pc_r, wo_r, g_r, b_r, mabc, m_o, r1_o, x1_o):
        ma = _dot(ya_r[...], pa_r[...])
        ybv = yb_r[...]
        mb = jnp.concatenate([_dot(ybv, pb_r[k]) for k in range(NCHIP)], axis=1)
        mc = _dot(yc_r[...], pc_r[...])
        m = jnp.zeros((tm, D), f32)
        for j, mm in enumerate((ma, mb, mc)):
            mabc[:, j * D:(j + 1) * D] = mm.astype(mabc.dtype)
            m = m + _sigmoid(gt[:, j * D:(j + 1) * D].astype(f32)) * mm
        mb16 = m.astype(MX)
        m_o[...] = mb16
        r1 = ALPHA * x0_r[...] + _dot(mb16, wo_r[...])
        r1_o[...] = r1
        xhat, _ = _ln_stats(r1)
        x1_o[...] = xhat * g_r[...] + b_r[...]

    full = lambda shape: pl.BlockSpec(shape, lambda i: (0,) * len(shape))
    tile = lambda w: pl.BlockSpec((tm, w), lambda i: (i, 0))
    return pl.pallas_call(
        body, name="mix_out_fwd", grid=(T // tm,),
        in_specs=[tile(3 * D), tile(D), tile(AO), tile(D), tile(D), full((D, D)), full((NCHIP, AO, D // NCHIP)),
                  full((D, D)), full((D, D)), full((1, D)), full((1, D))],
        out_specs=[tile(3 * D), tile(D), tile(D), tile(D)],
        out_shape=[jax.ShapeDtypeStruct((T, 3 * D), MX), jax.ShapeDtypeStruct((T, D), MX),
                   jax.ShapeDtypeStruct((T, D), f32), jax.ShapeDtypeStruct((T, D), f32)],
        compiler_params=_cp(("parallel",), 56),
    )(proj, ya, yb, yc, x0, pa, pb, pc, wo, g1, b1)


TM_FF = 512
TM_FFB = 256
ROW_CHUNK = 64


def ffn_up_fwd(x1, wg, wu):
    T = x1.shape[0]
    tm = min(TM_FFB, T)

    def body(x_r, wg_r, wu_r, g_o, u_o, h_o, gs, us):
        xb = x_r[...].astype(MX)
        for k in range(NCHIP):
            gs[...] = _dot(xb, wg_r[k])
            us[...] = _dot(xb, wu_r[k])
            for r in range(0, tm, ROW_CHUNK):
                rows = pl.ds(r, ROW_CHUNK)
                gate, up = gs[rows, :], us[rows, :]
                g_o[k, rows, :] = gate.astype(g_o.dtype)
                u_o[k, rows, :] = up.astype(u_o.dtype)
                h_o[k, rows, :] = (gate * _sigmoid(gate) * up).astype(h_o.dtype)

    wspec = pl.BlockSpec((NCHIP, D, FB), lambda i: (0, 0, 0))
    ospec = pl.BlockSpec((NCHIP, tm, FB), lambda i: (0, i, 0))
    return pl.pallas_call(
        body, name="ffn_up_fwd", grid=(T // tm,),
        in_specs=[pl.BlockSpec((tm, D), lambda i: (i, 0)), wspec, wspec],
        out_specs=[ospec] * 3,
        out_shape=[jax.ShapeDtypeStruct((NCHIP, T, FB), ACT)] * 2 + [jax.ShapeDtypeStruct((NCHIP, T, FB), MX)],
        scratch_shapes=[pltpu.VMEM((tm, FB), f32)] * 2,
        compiler_params=_cp(("parallel",)),
    )(x1, wg, wu)


def ffn_down_fwd(hh, wd, x1, g2, b2):
    T = x1.shape[0]
    tm = min(TM_FF, T)

    def body(h_r, w_r, x_r, g_r, b_r, r2_o, x2_o):
        r2 = ALPHA * x_r[...]
        for k in range(NCHIP):
            r2 = r2 + _dot(h_r[k], w_r[k])
        r2_o[...] = r2
        xhat, _ = _ln_stats(r2)
        x2_o[...] = xhat * g_r[...] + b_r[...]

    tile = pl.BlockSpec((tm, D), lambda i: (i, 0))
    vec = pl.BlockSpec((1, D), lambda i: (0, 0))
    return pl.pallas_call(
        body, name="ffn_down_fwd", grid=(T // tm,),
        in_specs=[pl.BlockSpec((NCHIP, tm, FB), lambda i: (0, i, 0)), pl.BlockSpec((NCHIP, FB, D), lambda i: (0, 0, 0)),
                  tile, vec, vec],
        out_specs=[tile, tile], out_shape=[jax.ShapeDtypeStruct((T, D), f32)] * 2,
        compiler_params=_cp(("parallel",)),
    )(hh, wd, x1, g2, b2)


def loss_grad(y, tgt):
    T = y.shape[0]
    tm = min(512, T)

    def body(y_r, t_r, l_o, dy_o):
        e = y_r[...] - t_r[...]
        dy_o[...] = e * (1.0 / D)

        @pl.when(pl.program_id(0) == 0)
        def _():
            l_o[...] = jnp.zeros_like(l_o)
        l_o[...] += (0.5 / D) * jnp.sum(e * e)

    tile = pl.BlockSpec((tm, D), lambda i: (i, 0))
    return pl.pallas_call(
        body, name="loss_grad", grid=(T // tm,),
        in_specs=[tile, tile], out_specs=[pl.BlockSpec((8, LANES), lambda i: (0, 0)), tile],
        out_shape=[jax.ShapeDtypeStruct((8, LANES), f32), jax.ShapeDtypeStruct((T, D), f32)],
        compiler_params=_cp(("arbitrary",)),
    )(y, tgt)


def ffn_down_bwd(dx2, r2, g2, wd, gate, up):
    T = dx2.shape[0]
    tm = min(TM_FFB, T)

    def body(dx_r, r_r, g_r, w_r, ga_r, up_r, dr_o, dg_o, du_o, dlg_o, dlb_o, hs):
        i = pl.program_id(0)
        xhat, rstd = _ln_stats(r_r[...])
        dx = dx_r[...]
        _acc_rows(dlg_o, i == 0, dx * xhat)
        _acc_rows(dlb_o, i == 0, dx)
        dr = _ln_bwd(dx, xhat, rstd, g_r[...])
        dr_o[...] = dr
        drb = dr.astype(MX)
        for k in range(NCHIP):
            hs[...] = _dot(drb, w_r[k], NT)
            for r in range(0, tm, ROW_CHUNK):
                rows = pl.ds(r, ROW_CHUNK)
                dhh, gate_v, up_v = hs[rows, :], ga_r[k, rows, :].astype(f32), up_r[k, rows, :].astype(f32)
                sg = _sigmoid(gate_v)
                dg_o[k, rows, :] = (dhh * up_v * sg * (1.0 + gate_v * (1.0 - sg))).astype(dg_o.dtype)
                du_o[k, rows, :] = (dhh * gate_v * sg).astype(du_o.dtype)

    tile = pl.BlockSpec((tm, D), lambda i: (i, 0))
    vec = pl.BlockSpec((1, D), lambda i: (0, 0))
    blk = pl.BlockSpec((NCHIP, tm, FB), lambda i: (0, i, 0))
    return pl.pallas_call(
        body, name="ffn_down_bwd", grid=(T // tm,),
        in_specs=[tile, tile, vec, pl.BlockSpec((NCHIP, FB, D), lambda i: (0, 0, 0)), blk, blk],
        out_specs=[tile, blk, blk, vec, vec],
        out_shape=[jax.ShapeDtypeStruct((T, D), f32)] + [jax.ShapeDtypeStruct((NCHIP, T, FB), MX)] * 2
        + [jax.ShapeDtypeStruct((1, D), f32)] * 2,
        scratch_shapes=[pltpu.VMEM((tm, FB), f32)],
        compiler_params=_cp(("arbitrary",)),
    )(dx2, r2, g2, wd, gate, up)


def ffn_up_bwd(dr2, dgate, dup, wg, wu, r1, g1):
    T = dr2.shape[0]
    tm = min(TM_FFB, T)

    def body(dr2_r, dg_r, du_r, wg_r, wu_r, r1_r, g_r, dr1_o, dlg_o, dlb_o):
        i = pl.program_id(0)
        dx = ALPHA * dr2_r[...]
        for k in range(NCHIP):
            dx = dx + _dot(dg_r[k], wg_r[k], NT) + _dot(du_r[k], wu_r[k], NT)
        xhat, rstd = _ln_stats(r1_r[...])
        _acc_rows(dlg_o, i == 0, dx * xhat)
        _acc_rows(dlb_o, i == 0, dx)
        dr1_o[...] = _ln_bwd(dx, xhat, rstd, g_r[...])

    tile = pl.BlockSpec((tm, D), lambda i: (i, 0))
    vec = pl.BlockSpec((1, D), lambda i: (0, 0))
    blk = pl.BlockSpec((NCHIP, tm, FB), lambda i: (0, i, 0))
    wspec = pl.BlockSpec((NCHIP, D, FB), lambda i: (0, 0, 0))
    return pl.pallas_call(
        body, name="ffn_up_bwd", grid=(T // tm,),
        in_specs=[tile, blk, blk, wspec, wspec, tile, vec],
        out_specs=[tile, vec, vec],
        out_shape=[jax.ShapeDtypeStruct((T, D), f32)] + [jax.ShapeDtypeStruct((1, D), f32)] * 2,
        compiler_params=_cp(("arbitrary",)),
    )(dr2, dgate, dup, wg, wu, r1, g1)


def mix_out_bwd(dr1, proj, mabc, wo, pa, pb, pc):
    T = dr1.shape[0]
    tm = min(TM_MIX, T)

    def body(dr_r, gt, mabc_r, wo_r, pa_r, pb_r, pc_r, dmabc_o, dgt_o, dya_o, dyb_o, dyc_o):
        dm = _dot(dr_r[...].astype(MX), wo_r[...], NT)
        dmx = []
        for j in range(3):
            s = _sigmoid(gt[:, j * D:(j + 1) * D].astype(f32))
            v = (dm * s).astype(MX)
            dmx.append(v)
            dmabc_o[:, j * D:(j + 1) * D] = v
            dgt_o[:, j * D:(j + 1) * D] = (dm * mabc_r[:, j * D:(j + 1) * D].astype(f32) * s * (1.0 - s)).astype(dgt_o.dtype)
        dya_o[...] = _dot(dmx[0], pa_r[...], NT)
        dyb = jnp.zeros((tm, AO), f32)
        for k in range(NCHIP):
            dyb = dyb + _dot(dmx[1][:, k * (D // NCHIP):(k + 1) * (D // NCHIP)], pb_r[k], NT)
        dyb_o[...] = dyb
        dyc_o[...] = _dot(dmx[2], pc_r[...], NT)

    full = lambda shape: pl.BlockSpec(shape, lambda i: (0,) * len(shape))
    tile = lambda w: pl.BlockSpec((tm, w), lambda i: (i, 0))
    return pl.pallas_call(
        body, name="mix_out_bwd", grid=(T // tm,),
        in_specs=[tile(D), tile(3 * D), tile(3 * D), full((D, D)), full((D, D)), full((NCHIP, AO, D // NCHIP)), full((D, D))],
        out_specs=[tile(3 * D), tile(3 * D), tile(D), tile(AO), tile(D)],
        out_shape=[jax.ShapeDtypeStruct((T, 3 * D), MX), jax.ShapeDtypeStruct((T, 3 * D), MX),
                   jax.ShapeDtypeStruct((T, D), f32), jax.ShapeDtypeStruct((T, AO), f32), jax.ShapeDtypeStruct((T, D), f32)],
        compiler_params=_cp(("parallel",), 56),
    )(dr1, proj, mabc, wo, pa, pb, pc)


def transpose_cast(x):
    T = x.shape[0]
    tm = min(512, T)

    def body(x_r, o_r):
        o_r[...] = x_r[...].T.astype(o_r.dtype)

    return pl.pallas_call(
        body, name="transpose_cast", grid=(T // tm,),
        in_specs=[pl.BlockSpec((tm, D), lambda i: (i, 0))], out_specs=pl.BlockSpec((D, tm), lambda i: (0, i)),
        out_shape=jax.ShapeDtypeStruct((D, T), MX), compiler_params=_cp(("parallel",)),
    )(x)


def tn_matmul(name, a, b, a_spec, b_spec, out_shape, out_spec, grid, a_is_t=False):
    nt = len(grid) - 1

    def body(a_r, b_r, o_r):
        @pl.when(pl.program_id(nt) == 0)
        def _():
            o_r[...] = jnp.zeros_like(o_r)
        av = a_r[...].reshape(a_r.shape[-2:]).astype(MX)
        bv = b_r[...].reshape(b_r.shape[-2:]).astype(MX)
        o_r[...] += _dot(av, bv, None if a_is_t else TN).reshape(o_r.shape)

    return pl.pallas_call(
        body, name=name, grid=grid, in_specs=[a_spec, b_spec], out_specs=out_spec,
        out_shape=jax.ShapeDtypeStruct(out_shape, f32),
        compiler_params=_cp(("parallel",) * nt + ("arbitrary",), 56),
    )(a, b)


def attn_pre_bwd(dyb, os_, lses, ones):
    T = dyb.shape[0]
    tm = TM_FOLD

    def body(dy_r, o0, o1, o2, l0, l1, l2, ones_r, d0, d1, d2, f0, f1, f2, nat):
        o = [_unfold_in(nat, r, d) for r, (_, d) in zip((o0, o1, o2), GROUPS)]
        ls = [_unfold_in(nat, r, d) for r, (_, d) in zip((l0, l1, l2), GROUPS)]
        w = _group_weights(ls)
        dy = dy_r[...]
        t = dy * (w[0] * o[0] + w[1] * o[1] + w[2] * o[2])
        hi = t.astype(MX)
        lo = (t - hi.astype(f32)).astype(MX)
        c = _dot(hi, ones_r[...]) + _dot(lo, ones_r[...])
        for wg, do_o, df_o, (_, d) in zip(w, (d0, d1, d2), (f0, f1, f2), GROUPS):
            _fold_out(nat, wg * dy, do_o, d)
            _fold_out(nat, -wg * c, df_o, d)

    specs = _fold_specs(T, tm)
    return pl.pallas_call(
        body, name="attn_pre_bwd", grid=(T // tm,),
        in_specs=[pl.BlockSpec((tm, AO), lambda i: (i, 0))] + specs + specs + [pl.BlockSpec((AO, AO), lambda i: (0, 0))],
        out_specs=specs + specs,
        out_shape=[jax.ShapeDtypeStruct((d, T // d, AO), MX) for _, d in GROUPS]
        + [jax.ShapeDtypeStruct((d, T // d, AO), f32) for _, d in GROUPS],
        scratch_shapes=[pltpu.VMEM((AO // LANES, tm, LANES), f32)],
        compiler_params=_cp(("parallel",)),
    )(dyb, *os_, *lses, ones)


def _head_ones():
    i = jnp.arange(AO) // HD
    return (i[:, None] == i[None, :]).astype(MX)


BWD_BLOCKS = 4


def attn_bwd(qf, kf, vf, dof, lse, df, g, nb):
    T = qf.shape[0]

    def body(q_ref, k_ref, v_ref, do_ref, l_ref, d_ref, dq_ref, dk_ref, dv_ref):
        prev_m, cur_m = _window_masks()

        def head_col(ref, r0):
            v = ref[pl.ds(r0, BLK), :]
            return jnp.concatenate([v[:, 0:1], v[:, HD:HD + 1]], axis=0)

        def step(b, carry):
            dk_c, dv_c = carry
            r0 = pl.multiple_of(b * BLK, BLK)
            rp = pl.multiple_of(jnp.maximum(b - 1, 0) * BLK, BLK)
            qs, dos = _stack_heads(q_ref[pl.ds(r0, BLK), :]), _stack_heads(do_ref[pl.ds(r0, BLK), :])
            k2, v2 = _two_blocks(k_ref, b), _two_blocks(v_ref, b)
            valid = cur_m | (prev_m & ((b % nb) != 0))
            p = jnp.where(valid, jnp.exp(_dot(qs, k2, NT) - head_col(l_ref, r0)), 0.0)
            ds = (p * (_dot(dos, v2, NT) + head_col(d_ref, r0))).astype(MX)
            dq_ref[pl.ds(r0, BLK), :] = _unstack_heads(_dot(ds, k2)).astype(dq_ref.dtype)
            dk2 = _dot(ds, qs, TN)
            dv2 = _dot(p.astype(MX), dos, TN)
            dk_ref[pl.ds(rp, BLK), :] = (dk_c + dk2[:BLK]).astype(dk_ref.dtype)
            dv_ref[pl.ds(rp, BLK), :] = (dv_c + dv2[:BLK]).astype(dv_ref.dtype)
            return dk2[BLK:], dv2[BLK:]

        zero = jnp.zeros((BLK, LANES), f32)

        def steps(i, carry):
            for j in range(BWD_BLOCKS):
                carry = step(BWD_BLOCKS * i + j, carry)
            return carry

        dk_c, dv_c = lax.fori_loop(0, T // BLK // BWD_BLOCKS, steps, (zero, zero))
        dk_ref[pl.ds(T - BLK, BLK), :] = dk_c.astype(dk_ref.dtype)
        dv_ref[pl.ds(T - BLK, BLK), :] = dv_c.astype(dv_ref.dtype)

    spec = pl.BlockSpec((T, LANES), lambda j: (0, j))
    return pl.pallas_call(
        body, name=f"attn_bwd{g}", grid=(AO // LANES,),
        in_specs=[spec] * 6, out_specs=[spec] * 3,
        out_shape=[jax.ShapeDtypeStruct((T, AO), MX)] * 3,
        compiler_params=_cp(("parallel",), 60),
    )(qf, kf, vf, dof, lse, df)


def unfold_rope_bwd(dqf, dkf, dvf, cos_t, sin_t, g, d):
    T = dqf.shape[0] * dqf.shape[1]
    tm = TM_FOLD

    def body(q_r, k_r, v_r, c_ref, s_ref, o_ref, nat):
        cos, sin = _tile4(c_ref[...]), _tile4(s_ref[...])
        for part, ref, scale in ((0, q_r, HD ** -0.5), (1, k_r, 1.0), (2, v_r, None)):
            x = _unfold_in(nat, ref, d)
            if scale is not None:
                x = (x * cos - _swap_halves(x) * sin) * scale
            o_ref[:, part * AO:(part + 1) * AO] = x.astype(o_ref.dtype)

    fold_spec = pl.BlockSpec((d, tm // d, AO), lambda i: (0, i, 0))
    tab = pl.BlockSpec((tm, LANES), lambda i: (i, 0))
    return pl.pallas_call(
        body, name=f"unfold_rope_bwd{g}", grid=(T // tm,),
        in_specs=[fold_spec] * 3 + [tab, tab],
        out_specs=pl.BlockSpec((tm, 3 * AO), lambda i: (i, 0)),
        out_shape=jax.ShapeDtypeStruct((T, 3 * AO), MX),
        scratch_shapes=[pltpu.VMEM((AO // LANES, tm, LANES), f32)],
        compiler_params=_cp(("parallel",)),
    )(dqf, dkf, dvf, cos_t, sin_t)


CONV_CHUNK = 32


def conv_bwd(dya, proj, conv_w):
    T = dya.shape[0]
    tm = TM_AC
    last = T // tm - 1

    def body(dy_r, bch, hprev, dy_next, b_next, cw, d_o, dw_o, zs, ds):
        i = pl.program_id(0)
        ch = CONV_CHUNK
        hz = hprev[:, :D].astype(f32) * hprev[:, D:].astype(f32)
        zs[0:HALO, :] = jnp.where(i > 0, hz, 0.0)
        ds[tm:tm + HALO, :] = jnp.where(i < last, dy_next[...] * b_next[...].astype(f32), 0.0)
        for r in range(0, tm, ch):
            zs[HALO + r:HALO + r + ch, :] = bch[r:r + ch, D:2 * D].astype(f32) * bch[r:r + ch, 2 * D:].astype(f32)
            ds[r:r + ch, :] = dy_r[r:r + ch, :] * bch[r:r + ch, :D].astype(f32)

        @pl.when(i == 0)
        def _():
            dw_o[...] = jnp.zeros_like(dw_o)

        sums = [jnp.zeros((1, D), f32) for _ in range(3)]
        for r in range(0, tm, ch):
            z2, z1, z = (zs[HALO + r - s:HALO + r - s + ch, :] for s in (2, 1, 0))
            dcv, d1, d2 = (ds[r + s:r + s + ch, :] for s in (0, 1, 2))
            cv = cw[0:1, :] * z2 + cw[1:2, :] * z1 + cw[2:3, :] * z
            dz = cw[2:3, :] * dcv + cw[1:2, :] * d1 + cw[0:1, :] * d2
            d_o[r:r + ch, :D] = (dy_r[r:r + ch, :] * cv).astype(d_o.dtype)
            d_o[r:r + ch, D:2 * D] = (dz * bch[r:r + ch, 2 * D:].astype(f32)).astype(d_o.dtype)
            d_o[r:r + ch, 2 * D:] = (dz * bch[r:r + ch, D:2 * D].astype(f32)).astype(d_o.dtype)
            for k, zz in enumerate((z2, z1, z)):
                sums[k] = sums[k] + jnp.sum(dcv * zz, axis=0, keepdims=True)
        for k in range(3):
            dw_o[k:k + 1, :] += sums[k]

    nh = tm // HALO
    return pl.pallas_call(
        body, name="conv_bwd", grid=(T // tm,),
        in_specs=[pl.BlockSpec((tm, D), lambda i: (i, 0)), pl.BlockSpec((tm, 3 * D), lambda i: (i, 1)),
                  pl.BlockSpec((HALO, 2 * D), lambda i: (jnp.maximum(i * nh - 1, 0), 2)),
                  pl.BlockSpec((HALO, D), lambda i: (jnp.minimum((i + 1) * nh, T // HALO - 1), 0)),
                  pl.BlockSpec((HALO, D), lambda i: (jnp.minimum((i + 1) * nh, T // HALO - 1), 3)),
                  pl.BlockSpec((3, D), lambda i: (0, 0))],
        out_specs=[pl.BlockSpec((tm, 3 * D), lambda i: (i, 0)), pl.BlockSpec((3, D), lambda i: (0, 0))],
        out_shape=[jax.ShapeDtypeStruct((T, 3 * D), MX), jax.ShapeDtypeStruct((3, D), f32)],
        scratch_shapes=[pltpu.VMEM((HALO + tm, D), f32), pltpu.VMEM((tm + HALO, D), f32)],
        compiler_params=_cp(("arbitrary",)),
    )(dya, proj, proj, dya, proj, conv_w)


def gmlp_bwd(dyc, proj, wst, bsx, lg, lb):
    T = dyc.shape[0]
    tm = TM_AC
    last = T // tm - 1

    def body(dy_r, u0, u1, v0, v1, ws, bs, lg_r, lb_r, d_o, dws_o, dbs_o, dlg_o, dlb_o, bacc):
        i = pl.program_id(0)
        up = jnp.concatenate([u0[...], u1[...]], axis=1).astype(f32)
        vp = jnp.concatenate([v0[...], v1[...]], axis=1).astype(f32)
        u, du = _gelu_and_grad(up)
        gv, dgv = _gelu_and_grad(vp)
        u, vn, xhat, rstd, sp = _gmlp_fwd(up, vp, ws, bs, lg_r[...], lb_r[...], u, gv)
        dy = dy_r[...]
        d_o[:, :D] = (dy * sp * du).astype(d_o.dtype)
        dsp = dy * u
        dspb, vnb = dsp.astype(MX), vn.astype(MX)

        @pl.when(i == 0)
        def _():
            dws_o[...] = jnp.zeros_like(dws_o)
            bacc[...] = jnp.zeros_like(bacc)

        rows = []
        for c in range(tm // BLK):
            r = slice(c * BLK, (c + 1) * BLK)
            cols = []
            for g in range(8):
                cs = slice(g * BLK, (g + 1) * BLK)
                dws_o[g] += _dot(dspb[r, cs], vnb[r, cs], NT)
                bacc[g] += dsp[r, cs]
                cols.append(_dot(ws[g], dspb[r, cs], TN))
            rows.append(jnp.concatenate(cols, axis=1))
        dvn = jnp.concatenate(rows, axis=0)
        _acc_rows(dlg_o, i == 0, dvn * xhat)
        _acc_rows(dlb_o, i == 0, dvn)
        d_o[:, D:] = (_ln_bwd(dvn, xhat, rstd, lg_r[...]) * dgv).astype(d_o.dtype)

        @pl.when(i == last)
        def _():
            row = lax.broadcasted_iota(jnp.int32, (BLK, BLK), 0)
            col = lax.broadcasted_iota(jnp.int32, (BLK, BLK), 1)
            ones = jnp.ones((8, BLK), MX)
            for g in range(8):
                dws_o[g] = jnp.where(col <= row, dws_o[g], 0.0)
                a = bacc[g]
                hi = a.astype(MX)
                lo = (a - hi.astype(f32)).astype(MX)
                dbs_o[g:g + 1, :] = (_dot(ones, hi, NT) + _dot(ones, lo, NT))[0:1, :]

    full = lambda shape: pl.BlockSpec(shape, lambda i: (0,) * len(shape))
    return pl.pallas_call(
        body, name="gmlp_bwd", grid=(T // tm,),
        in_specs=[pl.BlockSpec((tm, D), lambda i: (i, 0)), *_uv_specs(), full((8, BLK, BLK)), full((8, BLK, BLK)),
                  full((1, D)), full((1, D))],
        out_specs=[pl.BlockSpec((tm, 2 * D), lambda i: (i, 0)), full((8, BLK, BLK)), full((8, BLK)), full((1, D)), full((1, D))],
        out_shape=[jax.ShapeDtypeStruct((T, 2 * D), MX), jax.ShapeDtypeStruct((8, BLK, BLK), f32),
                   jax.ShapeDtypeStruct((8, BLK), f32), jax.ShapeDtypeStruct((1, D), f32), jax.ShapeDtypeStruct((1, D), f32)],
        scratch_shapes=[pltpu.VMEM((8, BLK, BLK), f32)],
        compiler_params=_cp(("arbitrary",)),
    )(dyc, proj, proj, proj, proj, wst, bsx, lg, lb)


PART_TILES = (6, 6, 3, 3, 3, 4)
PART_START = (0, 6, 12, 15, 18, 21)
TJ = 512


def _part_specs(tm, rows_axis):
    specs = []
    for n, s in zip(PART_TILES, PART_START):
        def imap(*idx, n=n, s=s):
            i, j = idx[rows_axis], idx[1 - rows_axis]
            inside = (j >= s) & (j < s + n)
            return (jnp.where(inside, i, 0), jnp.clip(j - s, 0, n - 1))
        specs.append(pl.BlockSpec((tm, TJ), imap))
    return specs


def _with_part(j, refs, fn):
    for r, n, s in zip(refs, PART_TILES, PART_START):
        @pl.when((j >= s) & (j < s + n))
        def _():
            fn(r[...])


def dx_in(dr1, parts, w, bias):
    T = dr1.shape[0]
    tm = min(2048, T)

    def body(dr_r, p0, p1, p2, p3, p4, p5, w_r, b_r, o_r):
        j = pl.program_id(1)

        @pl.when(j == 0)
        def _():
            o_r[...] = ALPHA * dr_r[...] + b_r[...]

        def acc(tile):
            o_r[...] += _dot(tile, w_r[...], NT)
        _with_part(j, (p0, p1, p2, p3, p4, p5), acc)

    once = dict(pipeline_mode=pl.Buffered(1))
    return pl.pallas_call(
        body, name="dx_in", grid=(T // tm, NIN // TJ),
        in_specs=[pl.BlockSpec((tm, D), lambda i, j: (i, 0), **once)] + _part_specs(tm, 0)
        + [pl.BlockSpec((D, TJ), lambda i, j: (0, j)), pl.BlockSpec((1, D), lambda i, j: (0, 0))],
        out_specs=pl.BlockSpec((tm, D), lambda i, j: (i, 0), **once),
        out_shape=jax.ShapeDtypeStruct((T, D), f32),
        compiler_params=_cp(("parallel", "arbitrary"), 56),
    )(dr1, *parts, w, bias)


def dw_in(x0t, parts):
    T = x0t.shape[1]
    tk = min(2048, T)

    def body(x_r, p0, p1, p2, p3, p4, p5, o_r):
        j, t = pl.program_id(0), pl.program_id(1)

        @pl.when(t == 0)
        def _():
            o_r[...] = jnp.zeros_like(o_r)

        def acc(tile):
            o_r[...] += _dot(x_r[...], tile)
        _with_part(j, (p0, p1, p2, p3, p4, p5), acc)

    return pl.pallas_call(
        body, name="dw_in", grid=(NIN // TJ, T // tk),
        in_specs=[pl.BlockSpec((D, tk), lambda j, t: (0, t))] + _part_specs(tk, 1),
        out_specs=pl.BlockSpec((D, TJ), lambda j, t: (0, j)),
        out_shape=jax.ShapeDtypeStruct((D, NIN), f32),
        compiler_params=_cp(("parallel", "arbitrary")),
    )(x0t, *parts)


def rope_tables(positions):
    half = HD // 2
    inv_freq = ROPE_THETA ** (-jnp.arange(half, dtype=f32) / half)
    ang = positions.astype(f32)[:, None] * inv_freq
    cos, sin = jnp.cos(ang), jnp.sin(ang)
    return jnp.tile(cos, (1, LANES // half)), jnp.tile(jnp.concatenate([-sin, sin], axis=1), (1, LANES // HD))


def _flat(a):
    return a.reshape(a.shape[0] * a.shape[1], a.shape[2])


def layer_fwd(x0, W, cos_t, sin_t):
    T = x0.shape[0]
    proj = mm_in(x0, W["w_in"], W["in_bias"])
    ya, yc = mix_ac_fwd(proj, W["conv_w"], W["wst"], W["bsx"], W["gmlp_ln_g"], W["gmlp_ln_b"])
    folded, os_, lses = [], [], []
    for g, (_, d) in enumerate(GROUPS):
        qf, kf, vf = fold_rope(proj, cos_t, sin_t, g, d)
        o, lse = attn_fwd(_flat(qf), _flat(kf), _flat(vf), g, T // d // BLK)
        folded.append((qf, kf, vf))
        os_.append(o.reshape(d, T // d, AO))
        lses.append(lse.reshape(d, T // d, AO))
    yb = combine_fwd(os_, lses)
    mabc, m, r1, x1 = mix_out_fwd(proj, ya, yb, yc, x0, W["p_a"], W["p_b"], W["p_c"], W["w_o"], W["ln1_g"], W["ln1_b"])
    gate, up, hh = ffn_up_fwd(x1, W["w_gate"], W["w_up"])
    r2, x2 = ffn_down_fwd(hh, W["w_down"], x1, W["ln2_g"], W["ln2_b"])
    saved = dict(x0=x0, proj=proj, ya=ya, yb=yb, yc=yc, folded=folded, os=os_, lses=lses, mabc=mabc, m=m, r1=r1,
                 x1=x1, gate=gate, up=up, hh=hh, r2=r2)
    return x2, saved


def layer_bwd(dx2, S, W, cos_t, sin_t, on_grads=None):
    T = dx2.shape[0]
    tk = min(2048, T)
    G = {}
    dr2, dgate, dup, G["ln2_g"], G["ln2_b"] = ffn_down_bwd(dx2, S["r2"], W["ln2_g"], W["w_down"], S["gate"], S["up"])
    blk_a = pl.BlockSpec((1, tk, FB), lambda k, t: (k, t, 0))
    row_b = pl.BlockSpec((tk, D), lambda k, t: (t, 0))
    G["w_down"] = tn_matmul("dw_down", S["hh"], dr2, blk_a, row_b, (NCHIP, FB, D),
                            pl.BlockSpec((1, FB, D), lambda k, t: (k, 0, 0)), (NCHIP, T // tk))
    for nm, dv in (("w_gate", dgate), ("w_up", dup)):
        G[nm] = tn_matmul("d" + nm, dv, S["x1"], blk_a, row_b, (NCHIP, FB, D),
                          pl.BlockSpec((1, FB, D), lambda k, t: (k, 0, 0)), (NCHIP, T // tk))
    dr1, G["ln1_g"], G["ln1_b"] = ffn_up_bwd(dr2, dgate, dup, W["w_gate"], W["w_up"], S["r1"], W["ln1_g"])
    dmabc, dgates, dya, dyb, dyc = mix_out_bwd(dr1, S["proj"], S["mabc"], W["w_o"], W["p_a"], W["p_b"], W["p_c"])
    one = (1, T // tk)
    full_o = pl.BlockSpec((D, D), lambda k, t: (0, 0))
    G["w_o"] = tn_matmul("dw_o", S["m"], dr1, row_b, row_b, (D, D), full_o, one)
    G["p_a"] = tn_matmul("dp_a", S["ya"], dmabc, row_b, pl.BlockSpec((tk, D), lambda k, t: (t, 0)), (D, D), full_o, one)
    G["p_c"] = tn_matmul("dp_c", S["yc"], dmabc, row_b, pl.BlockSpec((tk, D), lambda k, t: (t, 2)), (D, D), full_o, one)
    G["p_b"] = tn_matmul("dp_b", S["yb"], dmabc, pl.BlockSpec((tk, AO), lambda k, t: (t, 0)),
                         pl.BlockSpec((tk, D // NCHIP), lambda k, t: (t, NCHIP + k)), (NCHIP, AO, D // NCHIP),
                         pl.BlockSpec((1, AO, D // NCHIP), lambda k, t: (k, 0, 0)), (NCHIP, T // tk))
    conv_w = W["conv_w"]
    if on_grads is not None:
        conv_w = conv_w + on_grads({n: G[n] for n in BIG if n != "w_in"})
    dbch, G["conv_w"] = conv_bwd(dya, S["proj"], conv_w)
    duv, G["w_s"], G["b_s"], G["gmlp_ln_g"], G["gmlp_ln_b"] = gmlp_bwd(
        dyc, S["proj"], W["wst"], W["bsx"], W["gmlp_ln_g"], W["gmlp_ln_b"])
    ones = _head_ones()
    if on_grads is not None:
        small = {n: G[n] for n in VECS + ("b_s", "w_s", "conv_w")}
        ones = ones + on_grads(small).astype(MX)
    pre = attn_pre_bwd(dyb, S["os"], S["lses"], ones)
    dqkv = []
    for g, (_, d) in enumerate(GROUPS):
        qf, kf, vf = S["folded"][g]
        dqf, dkf, dvf = attn_bwd(_flat(qf), _flat(kf), _flat(vf), _flat(pre[g]), _flat(S["lses"][g]), _flat(pre[3 + g]),
                                 g, T // d // BLK)
        shp = (d, T // d, AO)
        dqkv.append(unfold_rope_bwd(dqf.reshape(shp), dkf.reshape(shp), dvf.reshape(shp), cos_t, sin_t, g, d))
    parts = (dgates, dbch, *dqkv, duv)
    G["w_in"] = dw_in(transpose_cast(S["x0"]), parts)
    bias = jnp.zeros((1, D), f32)
    if on_grads is not None:
        bias = bias + on_grads({"w_in": G["w_in"]})
    dx0 = dx_in(dr1, parts, W["w_in"], bias)
    started = on_grads({"dx": dx0}) if on_grads is not None else None
    return dx0, G, started


def prep_layer_weights(Wl):
    W = dict(Wl)
    tril = jnp.tril(jnp.ones((BLK, BLK), f32))
    W["wst"] = (Wl["w_s"] * tril[None]).astype(MX)
    W["bsx"] = jnp.broadcast_to(Wl["b_s"][:, :, None], (8, BLK, BLK))
    for n in ("gmlp_ln_g", "gmlp_ln_b", "ln1_g", "ln1_b", "ln2_g", "ln2_b"):
        W[n] = Wl[n].reshape(1, D)
    W["in_bias"] = jnp.zeros((1, NIN), f32) + Wl.get("after", 0.0)
    return W


def local_step(x, positions, target, layers, on_grads=None):
    cos_t, sin_t = rope_tables(positions)
    Ws, saved = [], []
    h = x
    for Wl in layers:
        Ws.append(prep_layer_weights(Wl(h) if callable(Wl) else Wl))
        h, S = layer_fwd(h, Ws[-1], cos_t, sin_t)
        saved.append(S)
    lsum, dh = loss_grad(h, target)
    if on_grads is not None:
        on_grads(len(Ws), {"loss": lsum})
    grads = [None] * len(Ws)
    started = None
    for l in reversed(range(len(Ws))):
        W = Ws[l]
        if started is not None:
            W = dict(W, ln2_g=W["ln2_g"] + started)
        hook = functools.partial(on_grads, l) if on_grads is not None else None
        dh, grads[l], started = layer_bwd(dh, saved[l], W, cos_t, sin_t, hook)
    return lsum, dh, grads


MESH = pl.DeviceIdType.MESH
ANY = pl.BlockSpec(memory_space=pl.ANY)
BIG = ("w_in", "w_gate", "w_up", "w_down", "p_a", "p_b", "p_c", "w_o")
NBIG = len(BIG)


def _place():
    x, y, c = lax.axis_index("x"), lax.axis_index("y"), lax.axis_index("c")
    return x, y, c, 2 * x + y


def _rcopy(src, dst, send, recv, dev):
    return pltpu.make_async_remote_copy(src_ref=src, dst_ref=dst, send_sem=send, recv_sem=recv, device_id=dev,
                                        device_id_type=MESH)


def _cols(ref, k, width):
    start = k * width if isinstance(k, int) else pl.multiple_of(k * width, LANES)
    return ref.at[:, pl.ds(start, width)]


CHUNK_BYTES = 1 << 20


def _pieces(shape, itemsize, nbytes=CHUNK_BYTES):
    rows, cols = shape[-2], shape[-1]
    per = max(16, nbytes // (cols * itemsize) // 16 * 16)
    out = []
    for lead in (range(shape[0]) if len(shape) == 3 else (None,)):
        for r in range(0, rows, per):
            sl = (pl.ds(r, min(per, rows - r)), slice(None))
            out.append(sl if lead is None else (lead,) + sl)
    return out


def _start_pieces(src, dst, make, nbytes=CHUNK_BYTES):
    for idx in _pieces(src.shape, jnp.dtype(src.dtype).itemsize, nbytes):
        make(src.at[idx], dst.at[idx]).start()


def gather_halves(shards):
    n = len(shards)

    def body(*refs):
        srcs, dsts = refs[:n], refs[n:2 * n]
        send, recv, own_send, own_recv = refs[2 * n:]
        x, y, c, k = _place()
        sib = (x, y, 1 - c)
        chips = [(1 - x, y), (x, 1 - y), (1 - x, 1 - y)]

        def slot(a, layer, pos):
            if a == 0:
                return _cols(dsts[0].at[layer], pos, WIN_SHARD)
            return dsts[a].at[pos, layer]

        def ici(a, j, src, dst):
            return _rcopy(src, dst, send.at[a, j], recv.at[a, j], (*chips[j], c))

        def d2d(a, j, src, dst):
            return _rcopy(src, dst, send.at[a, 3 + j], recv.at[a, 3 + j], sib)

        def own(a, layer, src, dst):
            return _rcopy(src, dst, own_send.at[a, layer], own_recv.at[a, layer], sib)

        for a in range(n):
            for j in range(3):
                _start_pieces(srcs[a].at[c], slot(a, c, k), functools.partial(ici, a, j))
        for a in range(n):
            for layer in range(DEPTH):
                _start_pieces(srcs[a].at[layer], slot(a, layer, k), functools.partial(own, a, layer))
        for a in range(n):
            for j, (cx, cy) in enumerate(chips):
                landed = slot(a, c, 2 * cx + cy)
                ici(a, j, landed, landed).wait_recv()
                _start_pieces(landed, landed, functools.partial(d2d, a, j))
        for a in range(n):
            for j, (cx, cy) in enumerate(chips):
                passed = slot(a, 1 - c, 2 * cx + cy)
                d2d(a, j, passed, passed).wait_recv()
                landed = slot(a, c, 2 * cx + cy)
                d2d(a, j, landed, landed).wait_send()
                ici(a, j, srcs[a].at[c], slot(a, c, k)).wait_send()
            for layer in range(DEPTH):
                own(a, layer, srcs[a].at[layer], slot(a, layer, k)).wait()

    outs = [jax.ShapeDtypeStruct((2, shards[0].shape[1], NIN), shards[0].dtype)]
    outs += [jax.ShapeDtypeStruct((NCHIP,) + s.shape, s.dtype) for s in shards[1:]]
    return pl.pallas_call(
        body, name="gather_halves", in_specs=[ANY] * n, out_specs=[ANY] * n, out_shape=outs,
        scratch_shapes=[pltpu.SemaphoreType.DMA((n, 6)), pltpu.SemaphoreType.DMA((n, 6)),
                        pltpu.SemaphoreType.DMA((n, DEPTH)), pltpu.SemaphoreType.DMA((n, DEPTH))],
    )(*shards)


def _gather_slot(dst, pos):
    return _cols(dst, pos, WIN_SHARD) if len(dst.shape) == 2 else dst.at[pos]


def _gather_copy(a, j, src, dst, send, recv, dev):
    return _rcopy(src, dst, send.at[a * NCHIP + j], recv.at[a * NCHIP + j], dev)


def gather_start(tag, shards, after):
    n = len(shards)

    def body(*refs):
        srcs, dsts = refs[:n], refs[n:2 * n]
        send, recv = refs[2 * n + len(after)], refs[2 * n + len(after) + 1]
        token = refs[-1]
        x, y, c, k = _place()
        peers = [(1 - x, y, c), (x, 1 - y, c), (1 - x, 1 - y, c), (x, y, 1 - c)]
        for a in range(n):
            for j, dev in enumerate(peers):
                _start_pieces(srcs[a], _gather_slot(dsts[a], k),
                              lambda s, d, a=a, j=j, dev=dev: _gather_copy(a, j, s, d, send, recv, dev))
        token[...] = jnp.zeros_like(token)

    gathered = [lax.empty((D, NIN) if i == 0 else (NCHIP,) + s.shape, s.dtype) for i, s in enumerate(shards)]
    ops = [pltpu.with_memory_space_constraint(v, pltpu.HBM) for v in list(shards) + gathered]
    sem = pltpu.SemaphoreType.DMA((n * NCHIP,))
    res = pl.pallas_call(
        body, name=f"gather_start{tag}", in_specs=[HBM] * (2 * n) + [ANY] * len(after),
        out_specs=[SEMS, SEMS] + [HBM] * (2 * n) + [pl.BlockSpec(memory_space=pltpu.VMEM)],
        out_shape=[sem, sem] + [pltpu.HBM(v.shape, v.dtype) for v in ops] + [jax.ShapeDtypeStruct((8, LANES), f32)],
        input_output_aliases={i: 2 + i for i in range(2 * n)},
        compiler_params=pltpu.CompilerParams(has_side_effects=EFFECT),
    )(*ops, *after)
    return res[0], res[1], res[2:2 + n], res[2 + n:2 + 2 * n], res[-1]


def gather_wait(tag, send, recv, shards, gathered, after):
    n = len(shards)

    def body(*refs):
        srcs, dsts = refs[:n], refs[n:2 * n]
        send_r, recv_r = refs[2 * n], refs[2 * n + 1]
        x, y, c, k = _place()
        peers = [(1 - x, y, c), (x, 1 - y, c), (1 - x, 1 - y, c), (x, y, 1 - c)]
        for a in range(n):
            for j, dev in enumerate(peers):
                _gather_copy(a, j, srcs[a], _gather_slot(dsts[a], k), send_r, recv_r, dev).wait_send()
                pos = 2 * dev[0] + dev[1]
                _gather_copy(a, j, srcs[a], _gather_slot(dsts[a], pos), send_r, recv_r, dev).wait_recv()

    ops = list(shards) + list(gathered)
    res = pl.pallas_call(
        body, name=f"gather_wait{tag}", in_specs=[HBM] * (2 * n) + [SEMS, SEMS] + [ANY] * len(after),
        out_specs=[HBM] * (2 * n), out_shape=[pltpu.HBM(v.shape, v.dtype) for v in ops],
        input_output_aliases={i: i for i in range(2 * n)},
        compiler_params=pltpu.CompilerParams(has_side_effects=EFFECT),
    )(*ops, send, recv, *after)
    return res[n:]


def _half(ref, h):
    rows = ref.shape[-2] // 2
    start = pl.multiple_of(h * rows, 16)
    if len(ref.shape) == 2:
        return ref.at[pl.ds(start, rows), :]
    return ref.at[:, pl.ds(start, rows), :]


HBM = pl.BlockSpec(memory_space=pltpu.HBM)
SEMS = pl.BlockSpec(memory_space=pltpu.SEMAPHORE)
EFFECT = pltpu.SideEffectType.DATAFLOW_SIDE_EFFECTING


def rs_pair_start(tag, grads):
    n = len(grads)

    def body(*refs):
        g, theirs = refs[:n], refs[n:2 * n]
        send, recv = refs[2 * n], refs[2 * n + 1]
        x, y, c, _ = _place()
        for a in range(n):
            _start_pieces(_half(g[a], 1 - c), theirs[a],
                          lambda s, d, a=a: _rcopy(s, d, send.at[a], recv.at[a], (x, y, 1 - c)))
        refs[-1][...] = jnp.zeros_like(refs[-1])

    lands = [lax.empty(g.shape[:-2] + (g.shape[-2] // 2, g.shape[-1]), g.dtype) for g in grads]
    ops = [pltpu.with_memory_space_constraint(v, pltpu.HBM) for v in list(grads) + lands]
    sem = pltpu.SemaphoreType.DMA((n,))
    res = pl.pallas_call(
        body, name=f"rs_pair_start{tag}", in_specs=[HBM] * (2 * n),
        out_specs=[SEMS, SEMS] + [HBM] * (2 * n) + [pl.BlockSpec(memory_space=pltpu.VMEM)],
        out_shape=[sem, sem] + [pltpu.HBM(v.shape, v.dtype) for v in ops] + [jax.ShapeDtypeStruct((8, LANES), f32)],
        input_output_aliases={i: 2 + i for i in range(2 * n)},
        compiler_params=pltpu.CompilerParams(has_side_effects=EFFECT),
    )(*ops)
    return res[0], res[1], res[2:2 + n], res[2 + n:2 + 2 * n], res[-1]


def rs_pair_wait(tag, send, recv, grads, theirs, after):
    n = len(grads)

    def body(*refs):
        g, land = refs[:n], refs[n:2 * n]
        send_r, recv_r = refs[2 * n], refs[2 * n + 1]
        x, y, c, _ = _place()
        for a in range(n):
            cp = _rcopy(_half(g[a], 1 - c), land[a], send_r.at[a], recv_r.at[a], (x, y, 1 - c))
            cp.wait_send()
            cp.wait_recv()

    ops = list(grads) + list(theirs)
    res = pl.pallas_call(
        body, name=f"rs_pair_wait{tag}", in_specs=[HBM] * (2 * n) + [SEMS, SEMS] + [ANY] * len(after),
        out_specs=[HBM] * (2 * n), out_shape=[pltpu.HBM(v.shape, v.dtype) for v in ops],
        input_output_aliases={i: i for i in range(2 * n)},
        compiler_params=pltpu.CompilerParams(has_side_effects=EFFECT),
    )(*ops, send, recv, *after)
    return res[:n], res[n:]


def _chip_piece(ref, k):
    return _cols(ref, k, WIN_SHARD) if len(ref.shape) == 2 else ref.at[k]


def _chip_copy(a, k, src, dst, send, recv, me, c):
    return _rcopy(src, dst, send.at[a * NCHIP + k], recv.at[a * NCHIP + me], (k // 2, k % 2, c))


def rs_chips_start(tag, sums):
    n = len(sums)

    def pshape(s):
        return (NCHIP, s[0], WIN_SHARD) if len(s) == 2 else s

    def body(*refs):
        s, land = refs[:n], refs[n:2 * n]
        send, recv = refs[2 * n], refs[2 * n + 1]
        token = refs[-1]
        x, y, c, me = _place()
        for k in range(NCHIP):
            @pl.when(me != k)
            def _():
                for a in range(n):
                    _start_pieces(_chip_piece(s[a], k), land[a].at[me],
                                  lambda src, dst, a=a: _chip_copy(a, k, src, dst, send, recv, me, c))
        token[...] = jnp.zeros_like(token)

    lands = [lax.empty(pshape(v.shape), v.dtype) for v in sums]
    ops = [pltpu.with_memory_space_constraint(v, pltpu.HBM) for v in list(sums) + lands]
    sem = pltpu.SemaphoreType.DMA((n * NCHIP,))
    res = pl.pallas_call(
        body, name=f"rs_chips_start{tag}", in_specs=[HBM] * (2 * n),
        out_specs=[SEMS, SEMS] + [HBM] * (2 * n) + [pl.BlockSpec(memory_space=pltpu.VMEM)],
        out_shape=[sem, sem] + [pltpu.HBM(v.shape, v.dtype) for v in ops] + [jax.ShapeDtypeStruct((8, LANES), f32)],
        input_output_aliases={i: 2 + i for i in range(2 * n)},
        compiler_params=pltpu.CompilerParams(has_side_effects=EFFECT),
    )(*ops)
    return res[0], res[1], res[2:2 + n], res[2 + n:2 + 2 * n], res[-1]


def rs_chips_wait(tag, send, recv, sums, lands, after):
    n = len(sums)

    def body(*refs):
        s, land = refs[:n], refs[n:2 * n]
        send_r, recv_r = refs[2 * n], refs[2 * n + 1]
        x, y, c, me = _place()
        for k in range(NCHIP):
            @pl.when(me != k)
            def _():
                for a in range(n):
                    piece = _chip_piece(s[a], k)
                    _chip_copy(a, k, piece, land[a].at[me], send_r, recv_r, me, c).wait_send()
                    _rcopy(piece, land[a].at[k], send_r.at[a * NCHIP + k], recv_r.at[a * NCHIP + k],
                           (k // 2, k % 2, c)).wait_recv()

    ops = list(sums) + list(lands)
    res = pl.pallas_call(
        body, name=f"rs_chips_wait{tag}", in_specs=[HBM] * (2 * n) + [SEMS, SEMS] + [ANY] * len(after),
        out_specs=[HBM] * (2 * n), out_shape=[pltpu.HBM(v.shape, v.dtype) for v in ops],
        input_output_aliases={i: i for i in range(2 * n)},
        compiler_params=pltpu.CompilerParams(has_side_effects=EFFECT),
    )(*ops, send, recv, *after)
    return res[:n], res[n:]


def rs_join(tag, halves):
    n = len(halves)

    def body(*refs):
        h, other = refs[:n], refs[n:2 * n]
        send, recv = refs[2 * n:]
        x, y, c, _ = _place()

        def give(a, s, d):
            return _rcopy(s, d, send.at[a], recv.at[a], (x, y, 1 - c))

        for a in range(n):
            _start_pieces(h[a], other[a], functools.partial(give, a))
        for a in range(n):
            give(a, h[a], other[a]).wait()

    outs = [jax.ShapeDtypeStruct(v.shape, v.dtype) for v in halves]
    return pl.pallas_call(
        body, name=f"rs_join{tag}", in_specs=[ANY] * n, out_specs=[ANY] * n, out_shape=outs,
        scratch_shapes=[pltpu.SemaphoreType.DMA((n,))] * 2,
    )(*halves)


def _row_tile(rows, cols, itemsize=4, target=2 << 20):
    best = 8
    for t in range(8, rows + 1, 8):
        if rows % t == 0 and t * cols * itemsize <= target:
            best = t
    return best


GRAD_WIRE = jnp.bfloat16


def add_n(name, terms, out_dtype=f32):
    shape = terms[0].shape
    cols = shape[-1]
    rows = math.prod(shape[:-1])
    tr = _row_tile(rows, cols)

    def body(*refs):
        acc = refs[0][...]
        for r in refs[1:-1]:
            acc = acc + r[...]
        refs[-1][...] = acc.astype(out_dtype)

    tile = pl.BlockSpec((tr, cols), lambda i: (i, 0))
    out = pl.pallas_call(
        body, name=name, grid=(rows // tr,), in_specs=[tile] * len(terms), out_specs=tile,
        out_shape=jax.ShapeDtypeStruct((rows, cols), out_dtype), compiler_params=_cp(("parallel",)),
    )(*[t.reshape(rows, cols) for t in terms])
    return out.reshape(shape)


def add_chips(name, land, own):
    _, rows, cols = land.shape
    tr = _row_tile(rows, cols, target=1 << 20)

    def body(land_r, own_r, o_r):
        me = 2 * lax.axis_index("x") + lax.axis_index("y")
        for k in range(NCHIP):
            @pl.when(me == k)
            def _():
                acc = None
                for j in range(NCHIP):
                    t = (own_r[...] if j == k else land_r[j]).astype(f32)
                    acc = t if acc is None else acc + t
                o_r[...] = acc

    tile = pl.BlockSpec((tr, cols), lambda i: (i, 0))
    return pl.pallas_call(
        body, name=name, grid=(rows // tr,), in_specs=[pl.BlockSpec((NCHIP, tr, cols), lambda i: (0, i, 0)), tile],
        out_specs=tile, out_shape=jax.ShapeDtypeStruct((rows, cols), f32), compiler_params=_cp(("parallel",)),
    )(land, own)


def reduce_scatter_pair(tag, G):
    names = tuple(G)
    grads = [G[n] if G[n].ndim == 3 or n == "w_in" else G[n].reshape(NCHIP, D // NCHIP, D) for n in names]
    send, recv, grads, theirs, token = rs_pair_start(tag, grads)
    return (tag, names, send, recv, grads, theirs), token[0, 0]


def reduce_scatter_chips(state, after):
    c = lax.axis_index("c")
    tag, names, send, recv, grads, theirs = state
    grads, theirs = rs_pair_wait(tag, send, recv, grads, theirs, after)
    sums = []
    for n, g, t in zip(names, grads, theirs):
        rows = g.shape[-2] // 2
        mine = lax.dynamic_slice_in_dim(g, c * rows, rows, axis=g.ndim - 2)
        sums.append(add_n(f"rs_add_pair{tag}_{n}", [mine, t], GRAD_WIRE))
    send, recv, sums, lands, token = rs_chips_start(tag, sums)
    return (tag, names, send, recv, sums, lands), token[0, 0]


def reduce_scatter_finish(state, after):
    me = 2 * lax.axis_index("x") + lax.axis_index("y")
    tag, names, send, recv, sums, lands = state
    sums, landed = rs_chips_wait(tag, send, recv, sums, lands, after)
    halves = []
    for n, s, v in zip(names, sums, landed):
        own = lax.dynamic_slice_in_dim(s, me * WIN_SHARD, WIN_SHARD, axis=1) if s.ndim == 2 else \
            lax.dynamic_index_in_dim(s, me, 0, keepdims=False)
        halves.append(add_chips(f"rs_add_chips{tag}_{n}", v, own))
    return dict(zip(names, zip(halves, rs_join(tag, halves))))


NDEV = 8


def _small_copy(r, src, dst, send, recv, x, y, c):
    return _rcopy(src, dst, send.at[r - 1], recv.at[r - 1], (x ^ (r >> 2), y ^ ((r >> 1) & 1), c ^ (r & 1)))


def small_start(pack):
    def body(p, land, send, recv, p_thru, land_thru, token):
        x, y, c, _ = _place()
        me = 4 * x + 2 * y + c
        for r in range(1, NDEV):
            _start_pieces(p, land.at[me], lambda s, d, r=r: _small_copy(r, s, d, send, recv, x, y, c), 128 << 10)
        token[...] = jnp.zeros_like(token)

    ops = [pltpu.with_memory_space_constraint(v, pltpu.HBM) for v in (pack, lax.empty((NDEV,) + pack.shape, f32))]
    sem = pltpu.SemaphoreType.DMA((NDEV - 1,))
    return pl.pallas_call(
        body, name="small_start", in_specs=[HBM, HBM],
        out_specs=[SEMS, SEMS, HBM, HBM, pl.BlockSpec(memory_space=pltpu.VMEM)],
        out_shape=[sem, sem] + [pltpu.HBM(v.shape, v.dtype) for v in ops] + [jax.ShapeDtypeStruct((8, LANES), f32)],
        input_output_aliases={0: 2, 1: 3}, compiler_params=pltpu.CompilerParams(has_side_effects=EFFECT),
    )(*ops)


def small_wait(send, recv, pack, land, after):
    def body(p, land_r, send_r, recv_r, *rest):
        x, y, c, _ = _place()
        me = 4 * x + 2 * y + c
        for r in range(1, NDEV):
            _small_copy(r, p, land_r.at[me], send_r, recv_r, x, y, c).wait_send()
            src = 4 * (x ^ (r >> 2)) + 2 * (y ^ ((r >> 1) & 1)) + (c ^ (r & 1))
            _small_copy(r, p, land_r.at[src], send_r, recv_r, x, y, c).wait_recv()

    return pl.pallas_call(
        body, name="small_wait", in_specs=[HBM, HBM, SEMS, SEMS] + [ANY] * len(after), out_specs=[HBM, HBM],
        out_shape=[pltpu.HBM(pack.shape, f32), pltpu.HBM(land.shape, f32)], input_output_aliases={0: 0, 1: 1},
        compiler_params=pltpu.CompilerParams(has_side_effects=EFFECT),
    )(pack, land, send, recv, *after)


def small_sum(land, pack):
    def body(land_r, p_r, o_r):
        me = 4 * lax.axis_index("x") + 2 * lax.axis_index("y") + lax.axis_index("c")
        for k in range(NDEV):
            @pl.when(me == k)
            def _():
                acc = None
                for d in range(NDEV):
                    t = p_r[...] if d == k else land_r[d]
                    acc = t if acc is None else acc + t
                o_r[...] = acc

    vm = pl.BlockSpec(memory_space=pltpu.VMEM)
    return pl.pallas_call(
        body, name="small_sum", in_specs=[vm, vm], out_specs=vm, out_shape=jax.ShapeDtypeStruct(pack.shape, f32),
        compiler_params=pltpu.CompilerParams(vmem_limit_bytes=40 << 20),
    )(land, pack)


def _adamw_math(w, g, m, v):
    m = ADAM_B1 * m + (1.0 - ADAM_B1) * g
    v = ADAM_B2 * v + (1.0 - ADAM_B2) * (g * g)
    m_hat = m / (1.0 - ADAM_B1 ** ADAM_STEP)
    v_hat = v / (1.0 - ADAM_B2 ** ADAM_STEP)
    return -ADAM_LR * (m_hat / (jnp.sqrt(v_hat) + ADAM_EPS) + ADAM_WD * w), m, v


def adamw_big(name, halves, w, m, v):
    _, R, C = w.shape
    tr = _row_tile(R // 2, C, target=1 << 20)
    nt = R // 2 // tr

    def body(a0, b0, a1, b1, w_r, m_r, v_r, g_o, d_o, m_o, v_o):
        mine = pl.program_id(1) == lax.axis_index("c")
        g = jnp.where(pl.program_id(0) == 0, jnp.where(mine, a0[...], b0[...]), jnp.where(mine, a1[...], b1[...]))
        g_o[...] = g
        d_o[...], m_o[...], v_o[...] = _adamw_math(w_r[...], g, m_r[...], v_r[...])

    stk = pl.BlockSpec((None, tr, C), lambda l, h, i: (l, h * nt + i, 0))
    lay0 = pl.BlockSpec((tr, C), lambda l, h, i: (jnp.where(l == 0, i, nt - 1), 0))
    lay1 = pl.BlockSpec((tr, C), lambda l, h, i: (jnp.where(l == 0, 0, i), 0))
    return pl.pallas_call(
        body, name=name, grid=(DEPTH, 2, nt),
        in_specs=[lay0, lay0, lay1, lay1, stk, stk, stk],
        out_specs=[stk] * 4, out_shape=[jax.ShapeDtypeStruct(w.shape, f32)] * 4,
        compiler_params=_cp(("arbitrary", "arbitrary", "arbitrary")),
    )(*halves[0], *halves[1], w, m, v)


def adamw_small(name, g, w, m, v):
    def body(g_r, w_r, m_r, v_r, d_o, m_o, v_o):
        d_o[...], m_o[...], v_o[...] = _adamw_math(w_r[...], g_r[...], m_r[...], v_r[...])

    return pl.pallas_call(body, name=name, out_shape=[jax.ShapeDtypeStruct(w.shape, f32)] * 3)(g, w, m, v)


WEIGHTS = ("w_in", "conv_w", "gmlp_ln_g", "gmlp_ln_b", "w_s", "b_s", "p_a", "p_b", "p_c", "w_o", "ln1_g", "ln1_b",
           "w_gate", "w_up", "w_down", "ln2_g", "ln2_b")
VECS = ("ln1_g", "ln1_b", "ln2_g", "ln2_b", "gmlp_ln_g", "gmlp_ln_b")
ROWS_VEC, ROWS_BS, ROWS_WS, ROWS_CONV = D // LANES, 8, 8 * BLK, 3 * D // LANES
ROWS_LAYER = len(VECS) * ROWS_VEC + ROWS_BS + ROWS_WS + ROWS_CONV


def _pack_small(per_layer, tail):
    parts = []
    for P in per_layer:
        parts += [P[n].reshape(ROWS_VEC, LANES) for n in VECS]
        parts += [P["b_s"].reshape(ROWS_BS, LANES), P["w_s"].reshape(ROWS_WS, LANES), P["conv_w"].reshape(ROWS_CONV, LANES)]
    return jnp.concatenate(parts + [tail], axis=0)


def _unpack_small(pack):
    out = []
    for l in range(DEPTH):
        r = l * ROWS_LAYER
        P = {}
        for n in VECS:
            P[n] = pack[r:r + ROWS_VEC].reshape(D)
            r += ROWS_VEC
        P["b_s"] = pack[r:r + ROWS_BS].reshape(8, BLK)
        r += ROWS_BS
        P["w_s"] = pack[r:r + ROWS_WS].reshape(8, BLK, BLK)
        r += ROWS_WS
        P["conv_w"] = pack[r:r + ROWS_CONV].reshape(3, D)
        out.append(P)
    return out, pack[DEPTH * ROWS_LAYER:]


def kernel(x, positions, w_in, conv_w, gmlp_ln_g, gmlp_ln_b, w_s, b_s, p_a, p_b, p_c, w_o, ln1_g, ln1_b, w_gate, w_up, w_down, ln2_g, ln2_b, loss_target, m_w_in, m_conv_w, m_gmlp_ln_g, m_gmlp_ln_b, m_w_s, m_b_s, m_p_a, m_p_b, m_p_c, m_w_o, m_ln1_g, m_ln1_b, m_w_gate, m_w_up, m_w_down, m_ln2_g, m_ln2_b, v_w_in, v_conv_w, v_gmlp_ln_g, v_gmlp_ln_b, v_w_s, v_b_s, v_p_a, v_p_b, v_p_c, v_w_o, v_ln1_g, v_ln1_b, v_w_gate, v_w_up, v_w_down, v_ln2_g, v_ln2_b):
    Wt = dict(w_in=w_in, conv_w=conv_w, gmlp_ln_g=gmlp_ln_g, gmlp_ln_b=gmlp_ln_b, w_s=w_s, b_s=b_s, p_a=p_a, p_b=p_b,
              p_c=p_c, w_o=w_o, ln1_g=ln1_g, ln1_b=ln1_b, w_gate=w_gate, w_up=w_up, w_down=w_down, ln2_g=ln2_g, ln2_b=ln2_b)
    Mt = dict(w_in=m_w_in, conv_w=m_conv_w, gmlp_ln_g=m_gmlp_ln_g, gmlp_ln_b=m_gmlp_ln_b, w_s=m_w_s, b_s=m_b_s, p_a=m_p_a,
              p_b=m_p_b, p_c=m_p_c, w_o=m_w_o, ln1_g=m_ln1_g, ln1_b=m_ln1_b, w_gate=m_w_gate, w_up=m_w_up,
              w_down=m_w_down, ln2_g=m_ln2_g, ln2_b=m_ln2_b)
    Vt = dict(w_in=v_w_in, conv_w=v_conv_w, gmlp_ln_g=v_gmlp_ln_g, gmlp_ln_b=v_gmlp_ln_b, w_s=v_w_s, b_s=v_b_s, p_a=v_p_a,
              p_b=v_p_b, p_c=v_p_c, w_o=v_w_o, ln1_g=v_ln1_g, ln1_b=v_ln1_b, w_gate=v_w_gate, w_up=v_w_up,
              w_down=v_w_down, ln2_g=v_ln2_g, ln2_b=v_ln2_b)
    chip = 2 * lax.axis_index("x") + lax.axis_index("y")
    cw = D // NCHIP

    def layer_weights(l, gathered, conv_all):
        Wl = dict(zip(BIG, gathered))
        for n in ("p_a", "p_c", "w_o"):
            Wl[n] = Wl[n].reshape(D, D)
        Wl["conv_w"] = conv_all[:, l].transpose(1, 0, 2).reshape(3, D)
        for n in VECS + ("w_s", "b_s"):
            Wl[n] = Wt[n][l]
        return Wl

    halves = [Wt[n][0].astype(MX).reshape(2, Wt[n].shape[1] // 2, Wt[n].shape[2]) for n in BIG]
    got = gather_halves(halves + [conv_w])
    conv_all = got[NBIG]
    g0 = [got[0].reshape(D, NIN)] + [a.reshape(NCHIP, 2 * a.shape[2], a.shape[3]) for a in got[1:NBIG]]
    send1, recv1, sh1, g1, coming = gather_start("1", [Wt[n][1].astype(MX) for n in BIG], [conv_all])
    W0 = layer_weights(0, g0, conv_all)
    W0["after"] = coming[0, 0]

    def W1(h):
        return layer_weights(1, gather_wait("1", send1, recv1, sh1, g1, [h]), conv_all)

    layers = [W0, W1]

    rs_state, rs_started, held = {}, {}, {}

    def start_exchange(l, g):
        if "loss" in g:
            held[l] = g
            return None
        if "conv_w" in g:
            held[l] = g
            rs_state[(l, False)], started = reduce_scatter_chips(rs_state[(l, False)], [g["w_s"], g["conv_w"]])
            if l == 0:
                pack = _pack_small([held[j] for j in range(DEPTH)], held[DEPTH]["loss"])
                *held["small"], token = small_start(pack)
                started = started + token[0, 0]
            return started
        if "dx" in g:
            rs_state[(l, True)], rs_started[(l, True)] = reduce_scatter_chips(rs_state[(l, True)], [g["dx"]])
            return rs_started[(l, True)]
        key = (l, "w_in" in g)
        rs_state[key], started = reduce_scatter_pair(f"{l}{'b' if key[1] else 'a'}", g)
        return started

    _, grad_x, _ = local_step(x[0], positions[0], loss_target[0], layers, start_exchange)

    last = jnp.zeros((8, LANES), f32) + rs_started[(0, True)]
    behind = [grad_x, last]
    red = [dict() for _ in range(DEPTH)]
    for key in ((1, False), (1, True), (0, False)):
        red[key[0]].update(reduce_scatter_finish(rs_state[key], behind))
    small, tail = _unpack_small(small_sum(*reversed(small_wait(*held["small"], behind))))
    loss = tail[0, 0]

    G, DW, NM, NV = {}, {}, {}, {}
    zc = jnp.zeros((3, D), f32)
    wp = _pack_small([{**{n: Wt[n][l] for n in VECS + ("b_s", "w_s")}, "conv_w": zc} for l in range(DEPTH)], jnp.zeros((8, LANES), f32))
    mp = _pack_small([{**{n: Mt[n][l] for n in VECS + ("b_s", "w_s")}, "conv_w": zc} for l in range(DEPTH)], jnp.zeros((8, LANES), f32))
    vp = _pack_small([{**{n: Vt[n][l] for n in VECS + ("b_s", "w_s")}, "conv_w": zc} for l in range(DEPTH)], jnp.ones((8, LANES), f32))
    gp = _pack_small(small, jnp.zeros((8, LANES), f32))
    outs = [_unpack_small(a)[0] for a in adamw_small("adamw_small", gp, wp, mp, vp)]
    for n in VECS + ("b_s", "w_s"):
        G[n] = jnp.stack([small[l][n] for l in range(DEPTH)])
        DW[n], NM[n], NV[n] = (jnp.stack([o[l][n] for l in range(DEPTH)]) for o in outs)
    gconv = jnp.stack([lax.dynamic_slice(small[l]["conv_w"], (0, chip * cw), (3, cw)) for l in range(DEPTH)])
    G["conv_w"] = gconv
    flat = lambda a: a.reshape(DEPTH * 3, cw)
    d, m2, v2 = adamw_small("adamw_conv", flat(gconv), flat(conv_w), flat(m_conv_w), flat(v_conv_w))
    DW["conv_w"], NM["conv_w"], NV["conv_w"] = (a.reshape(DEPTH, 3, cw) for a in (d, m2, v2))

    updated = {}
    for n in BIG[1:]:
        tr = (lambda a: jnp.swapaxes(a, 1, 2)) if n in ("w_gate", "w_up") else (lambda a: a)
        updated[n] = adamw_big("adamw_" + n, (red[0][n], red[1][n]), tr(Wt[n]), tr(Mt[n]), tr(Vt[n]))
        G[n], DW[n], NM[n], NV[n] = map(tr, updated[n])
    done = [d, DW["ln2_b"], red[1]["w_in"][1]] + [updated[n][1] for n in BIG[1:]]
    red[0].update(reduce_scatter_finish(rs_state[(0, True)], done))
    G["w_in"], DW["w_in"], NM["w_in"], NV["w_in"] = adamw_big(
        "adamw_w_in", (red[0]["w_in"], red[1]["w_in"]), Wt["w_in"], Mt["w_in"], Vt["w_in"])

    return (loss, grad_x[None], *[G[n] for n in WEIGHTS], *[DW[n] for n in WEIGHTS], *[NM[n] for n in WEIGHTS],
            *[NV[n] for n in WEIGHTS])
```

```python
import functools
import math

import jax
import jax.numpy as jnp
from jax import lax
from jax.experimental import pallas as pl
from jax.experimental.pallas import tpu as pltpu

D = 1024
NIN = 12800
DFF = 2816
NCHIP = 4
FB = DFF // NCHIP
WIN_SHARD = NIN // NCHIP
DEPTH = 2
GROUPS = ((128, 1), (512, 4), (2048, 16))
HD = 64
BLK = 128
AO = 512
ALPHA = (2 * DEPTH) ** 0.25
EPS = 1e-5
ROPE_THETA = 10000.0
LANES = 128
NEG = -1e30

C_GATES, C_BCH, C_QKV, C_UV = 0, 3 * D, 6 * D, 6 * D + 9 * AO

MX = jnp.bfloat16
ACT = jnp.bfloat16

ADAM_LR, ADAM_B1, ADAM_B2, ADAM_EPS, ADAM_WD, ADAM_STEP = 0.001, 0.9, 0.999, 1e-08, 0.01, 10

f32 = jnp.float32
NT = (((1,), (1,)), ((), ()))
TN = (((0,), (0,)), ((), ()))


def _cp(sem, vmem_mb=48):
    return pltpu.CompilerParams(dimension_semantics=sem, vmem_limit_bytes=vmem_mb << 20)


def _dot(a, b, dims=None):
    if dims is None:
        return jnp.dot(a, b, preferred_element_type=f32)
    return lax.dot_general(a, b, dims, preferred_element_type=f32)


def _ln_stats(r):
    mu = jnp.mean(r, axis=-1, keepdims=True)
    xc = r - mu
    var = jnp.mean(xc * xc, axis=-1, keepdims=True)
    rstd = lax.rsqrt(var + EPS)
    return xc * rstd, rstd


def _ln_bwd(dy, xhat, rstd, g):
    dxh = dy * g
    return rstd * (dxh - jnp.mean(dxh, axis=-1, keepdims=True) - xhat * jnp.mean(dxh * xhat, axis=-1, keepdims=True))


def _gelu(x):
    return 0.5 * x * (1.0 + lax.erf(x * (1.0 / math.sqrt(2.0))))


def _gelu_and_grad(x):
    cdf = 0.5 * (1.0 + lax.erf(x * (1.0 / math.sqrt(2.0))))
    return x * cdf, cdf + x * jnp.exp(-0.5 * x * x) * (1.0 / math.sqrt(2.0 * math.pi))


def _sigmoid(x):
    return 0.5 * jnp.tanh(0.5 * x) + 0.5


def _acc_rows(o_ref, first, val):
    @pl.when(first)
    def _():
        o_ref[...] = jnp.zeros_like(o_ref)
    o_ref[...] += jnp.sum(val, axis=0, keepdims=True)


def mm_in(x, w, bias):
    T = x.shape[0]
    tm, tn = min(2048, T), 1280

    def body(x_ref, w_ref, b_ref, o_ref, xb):
        @pl.when(pl.program_id(1) == 0)
        def _():
            xb[...] = x_ref[...].astype(MX)
        o_ref[...] = (_dot(xb[...], w_ref[...]) + b_ref[...]).astype(o_ref.dtype)

    return pl.pallas_call(
        body, name="mm_in", grid=(T // tm, NIN // tn),
        in_specs=[pl.BlockSpec((tm, D), lambda i, j: (i, 0), pipeline_mode=pl.Buffered(1)),
                  pl.BlockSpec((D, tn), lambda i, j: (0, j)), pl.BlockSpec((1, tn), lambda i, j: (0, j))],
        out_specs=pl.BlockSpec((tm, tn), lambda i, j: (i, j)),
        out_shape=jax.ShapeDtypeStruct((T, NIN), ACT),
        scratch_shapes=[pltpu.VMEM((tm, D), MX)],
        compiler_params=_cp(("parallel", "arbitrary")),
    )(x, w, bias)


HALO = 16
TM_AC = 256


def _uv_specs():
    return [pl.BlockSpec((TM_AC, 512), functools.partial(lambda i, j: (i, j), j=C_UV // 512 + j)) for j in range(4)]


def _gmlp_fwd(up, vp, ws_ref, bs_ref, lg, lb, u=None, gv=None):
    u = _gelu(up) if u is None else u
    xhat, rstd = _ln_stats(_gelu(vp) if gv is None else gv)
    vn = xhat * lg + lb
    vnb = vn.astype(MX)
    rows = []
    for c in range(up.shape[0] // BLK):
        r = slice(c * BLK, (c + 1) * BLK)
        rows.append(jnp.concatenate(
            [_dot(ws_ref[g], vnb[r, g * BLK:(g + 1) * BLK]) + bs_ref[g] for g in range(8)], axis=1))
    return u, vn, xhat, rstd, jnp.concatenate(rows, axis=0)


def mix_ac_fwd(proj, conv_w, wst, bsx, lg, lb):
    T = proj.shape[0]
    tm = TM_AC

    def body(bch, halo, u0, u1, v0, v1, cw, ws, bs, lg_ref, lb_ref, ya, yc, zs):
        i = pl.program_id(0)
        pb = bch[...].astype(f32)
        z = pb[:, D:2 * D] * pb[:, 2 * D:]
        hz = halo[:, :D].astype(f32) * halo[:, D:].astype(f32)
        zs[0:HALO, :] = jnp.where(i > 0, hz, 0.0)
        zs[HALO:HALO + tm, :] = z
        cv = cw[0:1, :] * zs[HALO - 2:HALO - 2 + tm, :] + cw[1:2, :] * zs[HALO - 1:HALO - 1 + tm, :] + cw[2:3, :] * z
        ya[...] = (pb[:, :D] * cv).astype(ya.dtype)
        up = jnp.concatenate([u0[...], u1[...]], axis=1).astype(f32)
        vp = jnp.concatenate([v0[...], v1[...]], axis=1).astype(f32)
        u, _, _, _, sp = _gmlp_fwd(up, vp, ws, bs, lg_ref[...], lb_ref[...])
        yc[...] = (u * sp).astype(yc.dtype)

    full = lambda shape: pl.BlockSpec(shape, lambda i: (0,) * len(shape))
    return pl.pallas_call(
        body, name="mix_ac_fwd", grid=(T // tm,),
        in_specs=[pl.BlockSpec((tm, 3 * D), lambda i: (i, 1)),
                  pl.BlockSpec((HALO, 2 * D), lambda i: (jnp.maximum(i * (tm // HALO) - 1, 0), 2)),
                  *_uv_specs(), full((3, D)), full((8, BLK, BLK)), full((8, BLK, BLK)), full((1, D)), full((1, D))],
        out_specs=[pl.BlockSpec((tm, D), lambda i: (i, 0))] * 2,
        out_shape=[jax.ShapeDtypeStruct((T, D), MX)] * 2,
        scratch_shapes=[pltpu.VMEM((HALO + tm, D), f32)],
        compiler_params=_cp(("parallel",)),
    )(proj, proj, proj, proj, proj, proj, conv_w, wst, bsx, lg, lb)


def _swap_halves(x):
    lane = lax.broadcasted_iota(jnp.int32, x.shape, 1)
    return jnp.where((lane % HD) < HD // 2, pltpu.roll(x, x.shape[1] - HD // 2, 1), pltpu.roll(x, HD // 2, 1))


def _tile4(t):
    return jnp.concatenate([t] * (AO // LANES), axis=1)


TM_FOLD = 512


def _fold_out(nat, x, out_ref, d):
    if d == 1:
        out_ref[0] = x.astype(out_ref.dtype)
        return
    rows = x.shape[0] // d
    for j in range(AO // LANES):
        nat[j] = x[:, j * LANES:(j + 1) * LANES]
    for r in range(d):
        out_ref[r] = jnp.concatenate(
            [nat.at[j][pl.ds(r, rows, stride=d), :] for j in range(AO // LANES)], axis=1).astype(out_ref.dtype)


def _unfold_in(nat, in_ref, d):
    if d == 1:
        return in_ref[0].astype(f32)
    rows = in_ref.shape[1]
    for r in range(d):
        v = in_ref[r].astype(f32)
        for j in range(AO // LANES):
            nat.at[j][pl.ds(r, rows, stride=d), :] = v[:, j * LANES:(j + 1) * LANES]
    return jnp.concatenate([nat[j] for j in range(AO // LANES)], axis=1)


def fold_rope(proj, cos_t, sin_t, g, d):
    T = proj.shape[0]
    tm = TM_FOLD
    rows = tm // d

    def body(x_ref, c_ref, s_ref, q_o, k_o, v_o, nat):
        cos, sin = _tile4(c_ref[...]), _tile4(s_ref[...])
        for part, out, scale in ((0, q_o, HD ** -0.5), (1, k_o, 1.0), (2, v_o, None)):
            x = x_ref[:, part * AO:(part + 1) * AO].astype(f32)
            if scale is not None:
                x = (x * cos + _swap_halves(x) * sin) * scale
            _fold_out(nat, x, out, d)

    fold_spec = pl.BlockSpec((d, rows, AO), lambda i: (0, i, 0))
    return pl.pallas_call(
        body, name=f"fold_rope{g}", grid=(T // tm,),
        in_specs=[pl.BlockSpec((tm, 3 * AO), lambda i: (i, C_QKV // (3 * AO) + g)),
                  pl.BlockSpec((tm, LANES), lambda i: (i, 0)), pl.BlockSpec((tm, LANES), lambda i: (i, 0))],
        out_specs=[fold_spec] * 3,
        out_shape=[jax.ShapeDtypeStruct((d, T // d, AO), MX)] * 3,
        scratch_shapes=[pltpu.VMEM((AO // LANES, tm, LANES), f32)],
        compiler_params=_cp(("parallel",)),
    )(proj, cos_t, sin_t)


def _stack_heads(x):
    lane = lax.broadcasted_iota(jnp.int32, x.shape, 1)
    z = jnp.zeros_like(x)
    return jnp.concatenate([jnp.where(lane < HD, x, z), jnp.where(lane >= HD, x, z)], axis=0)


def _unstack_heads(y):
    lane = lax.broadcasted_iota(jnp.int32, (BLK, LANES), 1)
    return jnp.where(lane < HD, y[:BLK], y[BLK:])


def _window_masks():
    row = lax.broadcasted_iota(jnp.int32, (2 * BLK, 2 * BLK), 0) % BLK
    col = lax.broadcasted_iota(jnp.int32, (2 * BLK, 2 * BLK), 1)
    return (col < BLK) & (col >= row), (col >= BLK) & (col - BLK <= row)


def _two_blocks(ref, b):
    r0 = pl.multiple_of(b * BLK, BLK)
    rp = pl.multiple_of(jnp.maximum(b - 1, 0) * BLK, BLK)
    return jnp.concatenate([ref[pl.ds(rp, BLK), :], ref[pl.ds(r0, BLK), :]], axis=0)


def attn_fwd(qf, kf, vf, g, nb):
    T = qf.shape[0]

    def body(q_ref, k_ref, v_ref, o_ref, l_ref):
        prev_m, cur_m = _window_masks()

        def step(b, carry):
            r0 = pl.multiple_of(b * BLK, BLK)
            qs = _stack_heads(q_ref[pl.ds(r0, BLK), :])
            s = _dot(qs, _two_blocks(k_ref, b), NT)
            s = jnp.where(cur_m | (prev_m & ((b % nb) != 0)), s, NEG)
            m = jnp.max(s, axis=-1, keepdims=True)
            p = jnp.exp(s - m)
            l = jnp.sum(p, axis=-1, keepdims=True)
            o = _dot(p.astype(MX), _two_blocks(v_ref, b)) / l
            o_ref[pl.ds(r0, BLK), :] = _unstack_heads(o)
            l_ref[pl.ds(r0, BLK), :] = _unstack_heads(jnp.broadcast_to(m + jnp.log(l), (2 * BLK, LANES)))
            return carry

        lax.fori_loop(0, T // BLK, step, 0, unroll=4)

    spec = pl.BlockSpec((T, LANES), lambda j: (0, j))
    return pl.pallas_call(
        body, name=f"attn_fwd{g}", grid=(AO // LANES,),
        in_specs=[spec] * 3, out_specs=[spec] * 2,
        out_shape=[jax.ShapeDtypeStruct((T, AO), f32)] * 2,
        compiler_params=_cp(("parallel",), 56),
    )(qf, kf, vf)


def _group_weights(lses):
    m = jnp.maximum(jnp.maximum(lses[0], lses[1]), lses[2])
    e = [jnp.exp(l - m) for l in lses]
    inv = 1.0 / (e[0] + e[1] + e[2])
    return [x * inv for x in e]


def _fold_specs(T, tm):
    specs = []
    for _, d in GROUPS:
        specs.append(pl.BlockSpec((d, tm // d, AO), lambda i: (0, i, 0)))
    return specs


def combine_fwd(os_, lses):
    T = os_[0].shape[0] * os_[0].shape[1]
    tm = TM_FOLD

    def body(o0, o1, o2, l0, l1, l2, y_ref, nat):
        o = [_unfold_in(nat, r, d) for r, (_, d) in zip((o0, o1, o2), GROUPS)]
        ls = [_unfold_in(nat, r, d) for r, (_, d) in zip((l0, l1, l2), GROUPS)]
        w = _group_weights(ls)
        y_ref[...] = (w[0] * o[0] + w[1] * o[1] + w[2] * o[2]).astype(y_ref.dtype)

    specs = _fold_specs(T, tm)
    return pl.pallas_call(
        body, name="combine_fwd", grid=(T // tm,),
        in_specs=specs + specs, out_specs=pl.BlockSpec((tm, AO), lambda i: (i, 0)),
        out_shape=jax.ShapeDtypeStruct((T, AO), MX),
        scratch_shapes=[pltpu.VMEM((AO // LANES, tm, LANES), f32)],
        compiler_params=_cp(("parallel",)),
    )(*os_, *lses)


TM_MIX = 256


def mix_out_fwd(proj, ya, yb, yc, x0, pa, pb, pc, wo, g1, b1):
    T = x0.shape[0]
    tm = min(TM_MIX, T)

    def body(gt, ya_r, yb_r, yc_r, x0_r, pa_r, pb_r, pc_r, wo_r, g_r, b_r, mabc, m_o, r1_o, x1_o):
        ma = _dot(ya_r[...], pa_r[...])
        ybv = yb_r[...]
        mb = jnp.concatenate([_dot(ybv, pb_r[k]) for k in range(NCHIP)], axis=1)
        mc = _dot(yc_r[...], pc_r[...])
        m = jnp.zeros((tm, D), f32)
        for j, mm in enumerate((ma, mb, mc)):
            mabc[:, j * D:(j + 1) * D] = mm.astype(mabc.dtype)
            m = m + _sigmoid(gt[:, j * D:(j + 1) * D].astype(f32)) * mm
        mb16 = m.astype(MX)
        m_o[...] = mb16
        r1 = ALPHA * x0_r[...] + _dot(mb16, wo_r[...])
        r1_o[...] = r1
        xhat, _ = _ln_stats(r1)
        x1_o[...] = xhat * g_r[...] + b_r[...]

    full = lambda shape: pl.BlockSpec(shape, lambda i: (0,) * len(shape))
    tile = lambda w: pl.BlockSpec((tm, w), lambda i: (i, 0))
    return pl.pallas_call(
        body, name="mix_out_fwd", grid=(T // tm,),
        in_specs=[tile(3 * D), tile(D), tile(AO), tile(D), tile(D), full((D, D)), full((NCHIP, AO, D // NCHIP)),
                  full((D, D)), full((D, D)), full((1, D)), full((1, D))],
        out_specs=[tile(3 * D), tile(D), tile(D), tile(D)],
        out_shape=[jax.ShapeDtypeStruct((T, 3 * D), MX), jax.ShapeDtypeStruct((T, D), MX),
                   jax.ShapeDtypeStruct((T, D), f32), jax.ShapeDtypeStruct((T, D), f32)],
        compiler_params=_cp(("parallel",), 56),
    )(proj, ya, yb, yc, x0, pa, pb, pc, wo, g1, b1)


TM_FF = 512
TM_FFB = 256
ROW_CHUNK = 64


def ffn_up_fwd(x1, wg, wu):
    T = x1.shape[0]
    tm = min(TM_FFB, T)

    def body(x_r, wg_r, wu_r, g_o, u_o, h_o, gs, us):
        xb = x_r[...].astype(MX)
        for k in range(NCHIP):
            gs[...] = _dot(xb, wg_r[k])
            us[...] = _dot(xb, wu_r[k])
            for r in range(0, tm, ROW_CHUNK):
                rows = pl.ds(r, ROW_CHUNK)
                gate, up = gs[rows, :], us[rows, :]
                g_o[k, rows, :] = gate.astype(g_o.dtype)
                u_o[k, rows, :] = up.astype(u_o.dtype)
                h_o[k, rows, :] = (gate * _sigmoid(gate) * up).astype(h_o.dtype)

    wspec = pl.BlockSpec((NCHIP, D, FB), lambda i: (0, 0, 0))
    ospec = pl.BlockSpec((NCHIP, tm, FB), lambda i: (0, i, 0))
    return pl.pallas_call(
        body, name="ffn_up_fwd", grid=(T // tm,),
        in_specs=[pl.BlockSpec((tm, D), lambda i: (i, 0)), wspec, wspec],
        out_specs=[ospec] * 3,
        out_shape=[jax.ShapeDtypeStruct((NCHIP, T, FB), ACT)] * 2 + [jax.ShapeDtypeStruct((NCHIP, T, FB), MX)],
        scratch_shapes=[pltpu.VMEM((tm, FB), f32)] * 2,
        compiler_params=_cp(("parallel",)),
    )(x1, wg, wu)


def ffn_down_fwd(hh, wd, x1, g2, b2):
    T = x1.shape[0]
    tm = min(TM_FF, T)

    def body(h_r, w_r, x_r, g_r, b_r, r2_o, x2_o):
        r2 = ALPHA * x_r[...]
        for k in range(NCHIP):
            r2 = r2 + _dot(h_r[k], w_r[k])
        r2_o[...] = r2
        xhat, _ = _ln_stats(r2)
        x2_o[...] = xhat * g_r[...] + b_r[...]

    tile = pl.BlockSpec((tm, D), lambda i: (i, 0))
    vec = pl.BlockSpec((1, D), lambda i: (0, 0))
    return pl.pallas_call(
        body, name="ffn_down_fwd", grid=(T // tm,),
        in_specs=[pl.BlockSpec((NCHIP, tm, FB), lambda i: (0, i, 0)), pl.BlockSpec((NCHIP, FB, D), lambda i: (0, 0, 0)),
                  tile, vec, vec],
        out_specs=[tile, tile], out_shape=[jax.ShapeDtypeStruct((T, D), f32)] * 2,
        compiler_params=_cp(("parallel",)),
    )(hh, wd, x1, g2, b2)


def loss_grad(y, tgt):
    T = y.shape[0]
    tm = min(512, T)

    def body(y_r, t_r, l_o, dy_o):
        e = y_r[...] - t_r[...]
        dy_o[...] = e * (1.0 / D)

        @pl.when(pl.program_id(0) == 0)
        def _():
            l_o[...] = jnp.zeros_like(l_o)
        l_o[...] += (0.5 / D) * jnp.sum(e * e)

    tile = pl.BlockSpec((tm, D), lambda i: (i, 0))
    return pl.pallas_call(
        body, name="loss_grad", grid=(T // tm,),
        in_specs=[tile, tile], out_specs=[pl.BlockSpec((8, LANES), lambda i: (0, 0)), tile],
        out_shape=[jax.ShapeDtypeStruct((8, LANES), f32), jax.ShapeDtypeStruct((T, D), f32)],
        compiler_params=_cp(("arbitrary",)),
    )(y, tgt)


def ffn_down_bwd(dx2, r2, g2, wd, gate, up):
    T = dx2.shape[0]
    tm = min(TM_FFB, T)

    def body(dx_r, r_r, g_r, w_r, ga_r, up_r, dr_o, dg_o, du_o, dlg_o, dlb_o, hs):
        i = pl.program_id(0)
        xhat, rstd = _ln_stats(r_r[...])
        dx = dx_r[...]
        _acc_rows(dlg_o, i == 0, dx * xhat)
        _acc_rows(dlb_o, i == 0, dx)
        dr = _ln_bwd(dx, xhat, rstd, g_r[...])
        dr_o[...] = dr
        drb = dr.astype(MX)
        for k in range(NCHIP):
            hs[...] = _dot(drb, w_r[k], NT)
            for r in range(0, tm, ROW_CHUNK):
                rows = pl.ds(r, ROW_CHUNK)
                dhh, gate_v, up_v = hs[rows, :], ga_r[k, rows, :].astype(f32), up_r[k, rows, :].astype(f32)
                sg = _sigmoid(gate_v)
                dg_o[k, rows, :] = (dhh * up_v * sg * (1.0 + gate_v * (1.0 - sg))).astype(dg_o.dtype)
                du_o[k, rows, :] = (dhh * gate_v * sg).astype(du_o.dtype)

    tile = pl.BlockSpec((tm, D), lambda i: (i, 0))
    vec = pl.BlockSpec((1, D), lambda i: (0, 0))
    blk = pl.BlockSpec((NCHIP, tm, FB), lambda i: (0, i, 0))
    return pl.pallas_call(
        body, name="ffn_down_bwd", grid=(T // tm,),
        in_specs=[tile, tile, vec, pl.BlockSpec((NCHIP, FB, D), lambda i: (0, 0, 0)), blk, blk],
        out_specs=[tile, blk, blk, vec, vec],
        out_shape=[jax.ShapeDtypeStruct((T, D), f32)] + [jax.ShapeDtypeStruct((NCHIP, T, FB), MX)] * 2
        + [jax.ShapeDtypeStruct((1, D), f32)] * 2,
        scratch_shapes=[pltpu.VMEM((tm, FB), f32)],
        compiler_params=_cp(("arbitrary",)),
    )(dx2, r2, g2, wd, gate, up)


def ffn_up_bwd(dr2, dgate, dup, wg, wu, r1, g1):
    T = dr2.shape[0]
    tm = min(TM_FFB, T)

    def body(dr2_r, dg_r, du_r, wg_r, wu_r, r1_r, g_r, dr1_o, dlg_o, dlb_o):
        i = pl.program_id(0)
        dx = ALPHA * dr2_r[...]
        for k in range(NCHIP):
            dx = dx + _dot(dg_r[k], wg_r[k], NT) + _dot(du_r[k], wu_r[k], NT)
        xhat, rstd = _ln_stats(r1_r[...])
        _acc_rows(dlg_o, i == 0, dx * xhat)
        _acc_rows(dlb_o, i == 0, dx)
        dr1_o[...] = _ln_bwd(dx, xhat, rstd, g_r[...])

    tile = pl.BlockSpec((tm, D), lambda i: (i, 0))
    vec = pl.BlockSpec((1, D), lambda i: (0, 0))
    blk = pl.BlockSpec((NCHIP, tm, FB), lambda i: (0, i, 0))
    wspec = pl.BlockSpec((NCHIP, D, FB), lambda i: (0, 0, 0))
    return pl.pallas_call(
        body, name="ffn_up_bwd", grid=(T // tm,),
        in_specs=[tile, blk, blk, wspec, wspec, tile, vec],
        out_specs=[tile, vec, vec],
        out_shape=[jax.ShapeDtypeStruct((T, D), f32)] + [jax.ShapeDtypeStruct((1, D), f32)] * 2,
        compiler_params=_cp(("arbitrary",)),
    )(dr2, dgate, dup, wg, wu, r1, g1)


def mix_out_bwd(dr1, proj, mabc, wo, pa, pb, pc):
    T = dr1.shape[0]
    tm = min(TM_MIX, T)

    def body(dr_r, gt, mabc_r, wo_r, pa_r, pb_r, pc_r, dmabc_o, dgt_o, dya_o, dyb_o, dyc_o):
        dm = _dot(dr_r[...].astype(MX), wo_r[...], NT)
        dmx = []
        for j in range(3):
            s = _sigmoid(gt[:, j * D:(j + 1) * D].astype(f32))
            v = (dm * s).astype(MX)
            dmx.append(v)
            dmabc_o[:, j * D:(j + 1) * D] = v
            dgt_o[:, j * D:(j + 1) * D] = (dm * mabc_r[:, j * D:(j + 1) * D].astype(f32) * s * (1.0 - s)).astype(dgt_o.dtype)
        dya_o[...] = _dot(dmx[0], pa_r[...], NT)
        dyb = jnp.zeros((tm, AO), f32)
        for k in range(NCHIP):
            dyb = dyb + _dot(dmx[1][:, k * (D // NCHIP):(k + 1) * (D // NCHIP)], pb_r[k], NT)
        dyb_o[...] = dyb
        dyc_o[...] = _dot(dmx[2], pc_r[...], NT)

    full = lambda shape: pl.BlockSpec(shape, lambda i: (0,) * len(shape))
    tile = lambda w: pl.BlockSpec((tm, w), lambda i: (i, 0))
    return pl.pallas_call(
        body, name="mix_out_bwd", grid=(T // tm,),
        in_specs=[tile(D), tile(3 * D), tile(3 * D), full((D, D)), full((D, D)), full((NCHIP, AO, D // NCHIP)), full((D, D))],
        out_specs=[tile(3 * D), tile(3 * D), tile(D), tile(AO), tile(D)],
        out_shape=[jax.ShapeDtypeStruct((T, 3 * D), MX), jax.ShapeDtypeStruct((T, 3 * D), MX),
                   jax.ShapeDtypeStruct((T, D), f32), jax.ShapeDtypeStruct((T, AO), f32), jax.ShapeDtypeStruct((T, D), f32)],
        compiler_params=_cp(("parallel",), 56),
    )(dr1, proj, mabc, wo, pa, pb, pc)


def transpose_cast(x):
    T = x.shape[0]
    tm = min(512, T)

    def body(x_r, o_r):
        o_r[...] = x_r[...].T.astype(o_r.dtype)

    return pl.pallas_call(
        body, name="transpose_cast", grid=(T // tm,),
        in_specs=[pl.BlockSpec((tm, D), lambda i: (i, 0))], out_specs=pl.BlockSpec((D, tm), lambda i: (0, i)),
        out_shape=jax.ShapeDtypeStruct((D, T), MX), compiler_params=_cp(("parallel",)),
    )(x)


def tn_matmul(name, a, b, a_spec, b_spec, out_shape, out_spec, grid, a_is_t=False):
    nt = len(grid) - 1

    def body(a_r, b_r, o_r):
        @pl.when(pl.program_id(nt) == 0)
        def _():
            o_r[...] = jnp.zeros_like(o_r)
        av = a_r[...].reshape(a_r.shape[-2:]).astype(MX)
        bv = b_r[...].reshape(b_r.shape[-2:]).astype(MX)
        o_r[...] += _dot(av, bv, None if a_is_t else TN).reshape(o_r.shape)

    return pl.pallas_call(
        body, name=name, grid=grid, in_specs=[a_spec, b_spec], out_specs=out_spec,
        out_shape=jax.ShapeDtypeStruct(out_shape, f32),
        compiler_params=_cp(("parallel",) * nt + ("arbitrary",), 56),
    )(a, b)


def attn_pre_bwd(dyb, os_, lses, ones):
    T = dyb.shape[0]
    tm = TM_FOLD

    def body(dy_r, o0, o1, o2, l0, l1, l2, ones_r, d0, d1, d2, f0, f1, f2, nat):
        o = [_unfold_in(nat, r, d) for r, (_, d) in zip((o0, o1, o2), GROUPS)]
        ls = [_unfold_in(nat, r, d) for r, (_, d) in zip((l0, l1, l2), GROUPS)]
        w = _group_weights(ls)
        dy = dy_r[...]
        t = dy * (w[0] * o[0] + w[1] * o[1] + w[2] * o[2])
        hi = t.astype(MX)
        lo = (t - hi.astype(f32)).astype(MX)
        c = _dot(hi, ones_r[...]) + _dot(lo, ones_r[...])
        for wg, do_o, df_o, (_, d) in zip(w, (d0, d1, d2), (f0, f1, f2), GROUPS):
            _fold_out(nat, wg * dy, do_o, d)
            _fold_out(nat, -wg * c, df_o, d)

    specs = _fold_specs(T, tm)
    return pl.pallas_call(
        body, name="attn_pre_bwd", grid=(T // tm,),
        in_specs=[pl.BlockSpec((tm, AO), lambda i: (i, 0))] + specs + specs + [pl.BlockSpec((AO, AO), lambda i: (0, 0))],
        out_specs=specs + specs,
        out_shape=[jax.ShapeDtypeStruct((d, T // d, AO), MX) for _, d in GROUPS]
        + [jax.ShapeDtypeStruct((d, T // d, AO), f32) for _, d in GROUPS],
        scratch_shapes=[pltpu.VMEM((AO // LANES, tm, LANES), f32)],
        compiler_params=_cp(("parallel",)),
    )(dyb, *os_, *lses, ones)


def _head_ones():
    i = jnp.arange(AO) // HD
    return (i[:, None] == i[None, :]).astype(MX)


BWD_BLOCKS = 4


def attn_bwd(qf, kf, vf, dof, lse, df, g, nb):
    T = qf.shape[0]

    def body(q_ref, k_ref, v_ref, do_ref, l_ref, d_ref, dq_ref, dk_ref, dv_ref):
        prev_m, cur_m = _window_masks()

        def head_col(ref, r0):
            v = ref[pl.ds(r0, BLK), :]
            return jnp.concatenate([v[:, 0:1], v[:, HD:HD + 1]], axis=0)

        def step(b, carry):
            dk_c, dv_c = carry
            r0 = pl.multiple_of(b * BLK, BLK)
            rp = pl.multiple_of(jnp.maximum(b - 1, 0) * BLK, BLK)
            qs, dos = _stack_heads(q_ref[pl.ds(r0, BLK), :]), _stack_heads(do_ref[pl.ds(r0, BLK), :])
            k2, v2 = _two_blocks(k_ref, b), _two_blocks(v_ref, b)
            valid = cur_m | (prev_m & ((b % nb) != 0))
            p = jnp.where(valid, jnp.exp(_dot(qs, k2, NT) - head_col(l_ref, r0)), 0.0)
            ds = (p * (_dot(dos, v2, NT) + head_col(d_ref, r0))).astype(MX)
            dq_ref[pl.ds(r0, BLK), :] = _unstack_heads(_dot(ds, k2)).astype(dq_ref.dtype)
            dk2 = _dot(ds, qs, TN)
            dv2 = _dot(p.astype(MX), dos, TN)
            dk_ref[pl.ds(rp, BLK), :] = (dk_c + dk2[:BLK]).astype(dk_ref.dtype)
            dv_ref[pl.ds(rp, BLK), :] = (dv_c + dv2[:BLK]).astype(dv_ref.dtype)
            return dk2[BLK:], dv2[BLK:]

        zero = jnp.zeros((BLK, LANES), f32)

        def steps(i, carry):
            for j in range(BWD_BLOCKS):
                carry = step(BWD_BLOCKS * i + j, carry)
            return carry

        dk_c, dv_c = lax.fori_loop(0, T // BLK // BWD_BLOCKS, steps, (zero, zero))
        dk_ref[pl.ds(T - BLK, BLK), :] = dk_c.astype(dk_ref.dtype)
        dv_ref[pl.ds(T - BLK, BLK), :] = dv_c.astype(dv_ref.dtype)

    spec = pl.BlockSpec((T, LANES), lambda j: (0, j))
    return pl.pallas_call(
        body, name=f"attn_bwd{g}", grid=(AO // LANES,),
        in_specs=[spec] * 6, out_specs=[spec] * 3,
        out_shape=[jax.ShapeDtypeStruct((T, AO), MX)] * 3,
        compiler_params=_cp(("parallel",), 60),
    )(qf, kf, vf, dof, lse, df)


def unfold_rope_bwd(dqf, dkf, dvf, cos_t, sin_t, g, d):
    T = dqf.shape[0] * dqf.shape[1]
    tm = TM_FOLD

    def body(q_r, k_r, v_r, c_ref, s_ref, o_ref, nat):
        cos, sin = _tile4(c_ref[...]), _tile4(s_ref[...])
        for part, ref, scale in ((0, q_r, HD ** -0.5), (1, k_r, 1.0), (2, v_r, None)):
            x = _unfold_in(nat, ref, d)
            if scale is not None:
                x = (x * cos - _swap_halves(x) * sin) * scale
            o_ref[:, part * AO:(part + 1) * AO] = x.astype(o_ref.dtype)

    fold_spec = pl.BlockSpec((d, tm // d, AO), lambda i: (0, i, 0))
    tab = pl.BlockSpec((tm, LANES), lambda i: (i, 0))
    return pl.pallas_call(
        body, name=f"unfold_rope_bwd{g}", grid=(T // tm,),
        in_specs=[fold_spec] * 3 + [tab, tab],
        out_specs=pl.BlockSpec((tm, 3 * AO), lambda i: (i, 0)),
        out_shape=jax.ShapeDtypeStruct((T, 3 * AO), MX),
        scratch_shapes=[pltpu.VMEM((AO // LANES, tm, LANES), f32)],
        compiler_params=_cp(("parallel",)),
    )(dqf, dkf, dvf, cos_t, sin_t)


CONV_CHUNK = 32


def conv_bwd(dya, proj, conv_w):
    T = dya.shape[0]
    tm = TM_AC
    last = T // tm - 1

    def body(dy_r, bch, hprev, dy_next, b_next, cw, d_o, dw_o, zs, ds):
        i = pl.program_id(0)
        ch = CONV_CHUNK
        hz = hprev[:, :D].astype(f32) * hprev[:, D:].astype(f32)
        zs[0:HALO, :] = jnp.where(i > 0, hz, 0.0)
        ds[tm:tm + HALO, :] = jnp.where(i < last, dy_next[...] * b_next[...].astype(f32), 0.0)
        for r in range(0, tm, ch):
            zs[HALO + r:HALO + r + ch, :] = bch[r:r + ch, D:2 * D].astype(f32) * bch[r:r + ch, 2 * D:].astype(f32)
            ds[r:r + ch, :] = dy_r[r:r + ch, :] * bch[r:r + ch, :D].astype(f32)

        @pl.when(i == 0)
        def _():
            dw_o[...] = jnp.zeros_like(dw_o)

        sums = [jnp.zeros((1, D), f32) for _ in range(3)]
        for r in range(0, tm, ch):
            z2, z1, z = (zs[HALO + r - s:HALO + r - s + ch, :] for s in (2, 1, 0))
            dcv, d1, d2 = (ds[r + s:r + s + ch, :] for s in (0, 1, 2))
            cv = cw[0:1, :] * z2 + cw[1:2, :] * z1 + cw[2:3, :] * z
            dz = cw[2:3, :] * dcv + cw[1:2, :] * d1 + cw[0:1, :] * d2
            d_o[r:r + ch, :D] = (dy_r[r:r + ch, :] * cv).astype(d_o.dtype)
            d_o[r:r + ch, D:2 * D] = (dz * bch[r:r + ch, 2 * D:].astype(f32)).astype(d_o.dtype)
            d_o[r:r + ch, 2 * D:] = (dz * bch[r:r + ch, D:2 * D].astype(f32)).astype(d_o.dtype)
            for k, zz in enumerate((z2, z1, z)):
                sums[k] = sums[k] + jnp.sum(dcv * zz, axis=0, keepdims=True)
        for k in range(3):
            dw_o[k:k + 1, :] += sums[k]

    nh = tm // HALO
    return pl.pallas_call(
        body, name="conv_bwd", grid=(T // tm,),
        in_specs=[pl.BlockSpec((tm, D), lambda i: (i, 0)), pl.BlockSpec((tm, 3 * D), lambda i: (i, 1)),
                  pl.BlockSpec((HALO, 2 * D), lambda i: (jnp.maximum(i * nh - 1, 0), 2)),
                  pl.BlockSpec((HALO, D), lambda i: (jnp.minimum((i + 1) * nh, T // HALO - 1), 0)),
                  pl.BlockSpec((HALO, D), lambda i: (jnp.minimum((i + 1) * nh, T // HALO - 1), 3)),
                  pl.BlockSpec((3, D), lambda i: (0, 0))],
        out_specs=[pl.BlockSpec((tm, 3 * D), lambda i: (i, 0)), pl.BlockSpec((3, D), lambda i: (0, 0))],
        out_shape=[jax.ShapeDtypeStruct((T, 3 * D), MX), jax.ShapeDtypeStruct((3, D), f32)],
        scratch_shapes=[pltpu.VMEM((HALO + tm, D), f32), pltpu.VMEM((tm + HALO, D), f32)],
        compiler_params=_cp(("arbitrary",)),
    )(dya, proj, proj, dya, proj, conv_w)


def gmlp_bwd(dyc, proj, wst, bsx, lg, lb):
    T = dyc.shape[0]
    tm = TM_AC
    last = T // tm - 1

    def body(dy_r, u0, u1, v0, v1, ws, bs, lg_r, lb_r, d_o, dws_o, dbs_o, dlg_o, dlb_o, bacc):
        i = pl.program_id(0)
        up = jnp.concatenate([u0[...], u1[...]], axis=1).astype(f32)
        vp = jnp.concatenate([v0[...], v1[...]], axis=1).astype(f32)
        u, du = _gelu_and_grad(up)
        gv, dgv = _gelu_and_grad(vp)
        u, vn, xhat, rstd, sp = _gmlp_fwd(up, vp, ws, bs, lg_r[...], lb_r[...], u, gv)
        dy = dy_r[...]
        d_o[:, :D] = (dy * sp * du).astype(d_o.dtype)
        dsp = dy * u
        dspb, vnb = dsp.astype(MX), vn.astype(MX)

        @pl.when(i == 0)
        def _():
            dws_o[...] = jnp.zeros_like(dws_o)
            bacc[...] = jnp.zeros_like(bacc)

        rows = []
        for c in range(tm // BLK):
            r = slice(c * BLK, (c + 1) * BLK)
            cols = []
            for g in range(8):
                cs = slice(g * BLK, (g + 1) * BLK)
                dws_o[g] += _dot(dspb[r, cs], vnb[r, cs], NT)
                bacc[g] += dsp[r, cs]
                cols.append(_dot(ws[g], dspb[r, cs], TN))
            rows.append(jnp.concatenate(cols, axis=1))
        dvn = jnp.concatenate(rows, axis=0)
        _acc_rows(dlg_o, i == 0, dvn * xhat)
        _acc_rows(dlb_o, i == 0, dvn)
        d_o[:, D:] = (_ln_bwd(dvn, xhat, rstd, lg_r[...]) * dgv).astype(d_o.dtype)

        @pl.when(i == last)
        def _():
            row = lax.broadcasted_iota(jnp.int32, (BLK, BLK), 0)
            col = lax.broadcasted_iota(jnp.int32, (BLK, BLK), 1)
            ones = jnp.ones((8, BLK), MX)
            for g in range(8):
                dws_o[g] = jnp.where(col <= row, dws_o[g], 0.0)
                a = bacc[g]
                hi = a.astype(MX)
                lo = (a - hi.astype(f32)).astype(MX)
                dbs_o[g:g + 1, :] = (_dot(ones, hi, NT) + _dot(ones, lo, NT))[0:1, :]

    full = lambda shape: pl.BlockSpec(shape, lambda i: (0,) * len(shape))
    return pl.pallas_call(
        body, name="gmlp_bwd", grid=(T // tm,),
        in_specs=[pl.BlockSpec((tm, D), lambda i: (i, 0)), *_uv_specs(), full((8, BLK, BLK)), full((8, BLK, BLK)),
                  full((1, D)), full((1, D))],
        out_specs=[pl.BlockSpec((tm, 2 * D), lambda i: (i, 0)), full((8, BLK, BLK)), full((8, BLK)), full((1, D)), full((1, D))],
        out_shape=[jax.ShapeDtypeStruct((T, 2 * D), MX), jax.ShapeDtypeStruct((8, BLK, BLK), f32),
                   jax.ShapeDtypeStruct((8, BLK), f32), jax.ShapeDtypeStruct((1, D), f32), jax.ShapeDtypeStruct((1, D), f32)],
        scratch_shapes=[pltpu.VMEM((8, BLK, BLK), f32)],
        compiler_params=_cp(("arbitrary",)),
    )(dyc, proj, proj, proj, proj, wst, bsx, lg, lb)


PART_TILES = (6, 6, 3, 3, 3, 4)
PART_START = (0, 6, 12, 15, 18, 21)
TJ = 512


def _part_specs(tm, rows_axis):
    specs = []
    for n, s in zip(PART_TILES, PART_START):
        def imap(*idx, n=n, s=s):
            i, j = idx[rows_axis], idx[1 - rows_axis]
            inside = (j >= s) & (j < s + n)
            return (jnp.where(inside, i, 0), jnp.clip(j - s, 0, n - 1))
        specs.append(pl.BlockSpec((tm, TJ), imap))
    return specs


def _with_part(j, refs, fn):
    for r, n, s in zip(refs, PART_TILES, PART_START):
        @pl.when((j >= s) & (j < s + n))
        def _():
            fn(r[...])


def dx_in(dr1, parts, w, bias):
    T = dr1.shape[0]
    tm = min(2048, T)

    def body(dr_r, p0, p1, p2, p3, p4, p5, w_r, b_r, o_r):
        j = pl.program_id(1)

        @pl.when(j == 0)
        def _():
            o_r[...] = ALPHA * dr_r[...] + b_r[...]

        def acc(tile):
            o_r[...] += _dot(tile, w_r[...], NT)
        _with_part(j, (p0, p1, p2, p3, p4, p5), acc)

    once = dict(pipeline_mode=pl.Buffered(1))
    return pl.pallas_call(
        body, name="dx_in", grid=(T // tm, NIN // TJ),
        in_specs=[pl.BlockSpec((tm, D), lambda i, j: (i, 0), **once)] + _part_specs(tm, 0)
        + [pl.BlockSpec((D, TJ), lambda i, j: (0, j)), pl.BlockSpec((1, D), lambda i, j: (0, 0))],
        out_specs=pl.BlockSpec((tm, D), lambda i, j: (i, 0), **once),
        out_shape=jax.ShapeDtypeStruct((T, D), f32),
        compiler_params=_cp(("parallel", "arbitrary"), 56),
    )(dr1, *parts, w, bias)


def dw_in(x0t, parts):
    T = x0t.shape[1]
    tk = min(2048, T)

    def body(x_r, p0, p1, p2, p3, p4, p5, o_r):
        j, t = pl.program_id(0), pl.program_id(1)

        @pl.when(t == 0)
        def _():
            o_r[...] = jnp.zeros_like(o_r)

        def acc(tile):
            o_r[...] += _dot(x_r[...], tile)
        _with_part(j, (p0, p1, p2, p3, p4, p5), acc)

    return pl.pallas_call(
        body, name="dw_in", grid=(NIN // TJ, T // tk),
        in_specs=[pl.BlockSpec((D, tk), lambda j, t: (0, t))] + _part_specs(tk, 1),
        out_specs=pl.BlockSpec((D, TJ), lambda j, t: (0, j)),
        out_shape=jax.ShapeDtypeStruct((D, NIN), f32),
        compiler_params=_cp(("parallel", "arbitrary")),
    )(x0t, *parts)


def rope_tables(positions):
    half = HD // 2
    inv_freq = ROPE_THETA ** (-jnp.arange(half, dtype=f32) / half)
    ang = positions.astype(f32)[:, None] * inv_freq
    cos, sin = jnp.cos(ang), jnp.sin(ang)
    return jnp.tile(cos, (1, LANES // half)), jnp.tile(jnp.concatenate([-sin, sin], axis=1), (1, LANES // HD))


def _flat(a):
    return a.reshape(a.shape[0] * a.shape[1], a.shape[2])


def layer_fwd(x0, W, cos_t, sin_t):
    T = x0.shape[0]
    proj = mm_in(x0, W["w_in"], W["in_bias"])
    ya, yc = mix_ac_fwd(proj, W["conv_w"], W["wst"], W["bsx"], W["gmlp_ln_g"], W["gmlp_ln_b"])
    folded, os_, lses = [], [], []
    for g, (_, d) in enumerate(GROUPS):
        qf, kf, vf = fold_rope(proj, cos_t, sin_t, g, d)
        o, lse = attn_fwd(_flat(qf), _flat(kf), _flat(vf), g, T // d // BLK)
        folded.append((qf, kf, vf))
        os_.append(o.reshape(d, T // d, AO))
        lses.append(lse.reshape(d, T // d, AO))
    yb = combine_fwd(os_, lses)
    if "late" in W:
        W = {**W, **W["late"](yb)}
    mabc, m, r1, x1 = mix_out_fwd(proj, ya, yb, yc, x0, W["p_a"], W["p_b"], W["p_c"], W["w_o"], W["ln1_g"], W["ln1_b"])
    gate, up, hh = ffn_up_fwd(x1, W["w_gate"], W["w_up"])
    r2, x2 = ffn_down_fwd(hh, W["w_down"], x1, W["ln2_g"], W["ln2_b"])
    saved = dict(x0=x0, proj=proj, ya=ya, yb=yb, yc=yc, folded=folded, os=os_, lses=lses, mabc=mabc, m=m, r1=r1,
                 x1=x1, gate=gate, up=up, hh=hh, r2=r2)
    return x2, saved, W


def layer_bwd(dx2, S, W, cos_t, sin_t, on_grads=None):
    T = dx2.shape[0]
    tk = min(2048, T)
    G = {}
    dr2, dgate, dup, G["ln2_g"], G["ln2_b"] = ffn_down_bwd(dx2, S["r2"], W["ln2_g"], W["w_down"], S["gate"], S["up"])
    blk_a = pl.BlockSpec((1, tk, FB), lambda k, t: (k, t, 0))
    row_b = pl.BlockSpec((tk, D), lambda k, t: (t, 0))
    G["w_down"] = tn_matmul("dw_down", S["hh"], dr2, blk_a, row_b, (NCHIP, FB, D),
                            pl.BlockSpec((1, FB, D), lambda k, t: (k, 0, 0)), (NCHIP, T // tk))
    for nm, dv in (("w_gate", dgate), ("w_up", dup)):
        G[nm] = tn_matmul("d" + nm, dv, S["x1"], blk_a, row_b, (NCHIP, FB, D),
                          pl.BlockSpec((1, FB, D), lambda k, t: (k, 0, 0)), (NCHIP, T // tk))
    dr1, G["ln1_g"], G["ln1_b"] = ffn_up_bwd(dr2, dgate, dup, W["w_gate"], W["w_up"], S["r1"], W["ln1_g"])
    dmabc, dgates, dya, dyb, dyc = mix_out_bwd(dr1, S["proj"], S["mabc"], W["w_o"], W["p_a"], W["p_b"], W["p_c"])
    one = (1, T // tk)
    full_o = pl.BlockSpec((D, D), lambda k, t: (0, 0))
    G["w_o"] = tn_matmul("dw_o", S["m"], dr1, row_b, row_b, (D, D), full_o, one)
    G["p_a"] = tn_matmul("dp_a", S["ya"], dmabc, row_b, pl.BlockSpec((tk, D), lambda k, t: (t, 0)), (D, D), full_o, one)
    G["p_c"] = tn_matmul("dp_c", S["yc"], dmabc, row_b, pl.BlockSpec((tk, D), lambda k, t: (t, 2)), (D, D), full_o, one)
    G["p_b"] = tn_matmul("dp_b", S["yb"], dmabc, pl.BlockSpec((tk, AO), lambda k, t: (t, 0)),
                         pl.BlockSpec((tk, D // NCHIP), lambda k, t: (t, NCHIP + k)), (NCHIP, AO, D // NCHIP),
                         pl.BlockSpec((1, AO, D // NCHIP), lambda k, t: (k, 0, 0)), (NCHIP, T // tk))
    conv_w = W["conv_w"]
    if on_grads is not None:
        conv_w = conv_w + on_grads({n: G[n] for n in BIG if n != "w_in"})
    dbch, G["conv_w"] = conv_bwd(dya, S["proj"], conv_w)
    duv, G["w_s"], G["b_s"], G["gmlp_ln_g"], G["gmlp_ln_b"] = gmlp_bwd(
        dyc, S["proj"], W["wst"], W["bsx"], W["gmlp_ln_g"], W["gmlp_ln_b"])
    ones = _head_ones()
    if on_grads is not None:
        small = {n: G[n] for n in VECS + ("b_s", "w_s", "conv_w")}
        ones = ones + on_grads(small).astype(MX)
    pre = attn_pre_bwd(dyb, S["os"], S["lses"], ones)
    dqkv = []
    for g, (_, d) in enumerate(GROUPS):
        qf, kf, vf = S["folded"][g]
        dqf, dkf, dvf = attn_bwd(_flat(qf), _flat(kf), _flat(vf), _flat(pre[g]), _flat(S["lses"][g]), _flat(pre[3 + g]),
                                 g, T // d // BLK)
        shp = (d, T // d, AO)
        dqkv.append(unfold_rope_bwd(dqf.reshape(shp), dkf.reshape(shp), dvf.reshape(shp), cos_t, sin_t, g, d))
    parts = (dgates, dbch, *dqkv, duv)
    G["w_in"] = dw_in(transpose_cast(S["x0"]), parts)
    bias = jnp.zeros((1, D), f32)
    if on_grads is not None:
        bias = bias + on_grads({"w_in": G["w_in"]})
    dx0 = dx_in(dr1, parts, W["w_in"], bias)
    started = on_grads({"dx": dx0}) if on_grads is not None else None
    return dx0, G, started


def prep_layer_weights(Wl):
    W = dict(Wl)
    tril = jnp.tril(jnp.ones((BLK, BLK), f32))
    W["wst"] = (Wl["w_s"] * tril[None]).astype(MX)
    W["bsx"] = jnp.broadcast_to(Wl["b_s"][:, :, None], (8, BLK, BLK))
    for n in ("gmlp_ln_g", "gmlp_ln_b", "ln1_g", "ln1_b", "ln2_g", "ln2_b"):
        W[n] = Wl[n].reshape(1, D)
    W["in_bias"] = jnp.zeros((1, NIN), f32) + Wl.get("after", 0.0)
    return W


def local_step(x, positions, target, layers, on_grads=None):
    cos_t, sin_t = rope_tables(positions)
    Ws, saved = [], []
    h = x
    for Wl in layers:
        h, S, W = layer_fwd(h, prep_layer_weights(Wl(h) if callable(Wl) else Wl), cos_t, sin_t)
        Ws.append(W)
        saved.append(S)
    lsum, dh = loss_grad(h, target)
    if on_grads is not None:
        on_grads(len(Ws), {"loss": lsum})
    grads = [None] * len(Ws)
    started = None
    for l in reversed(range(len(Ws))):
        W = Ws[l]
        if started is not None:
            W = dict(W, ln2_g=W["ln2_g"] + started)
        hook = functools.partial(on_grads, l) if on_grads is not None else None
        dh, grads[l], started = layer_bwd(dh, saved[l], W, cos_t, sin_t, hook)
    return lsum, dh, grads


MESH = pl.DeviceIdType.MESH
ANY = pl.BlockSpec(memory_space=pl.ANY)
BIG = ("w_in", "w_gate", "w_up", "w_down", "p_a", "p_b", "p_c", "w_o")
NBIG = len(BIG)


def _place():
    x, y, c = lax.axis_index("x"), lax.axis_index("y"), lax.axis_index("c")
    return x, y, c, 2 * x + y


def _rcopy(src, dst, send, recv, dev):
    return pltpu.make_async_remote_copy(src_ref=src, dst_ref=dst, send_sem=send, recv_sem=recv, device_id=dev,
                                        device_id_type=MESH)


def _cols(ref, k, width):
    start = k * width if isinstance(k, int) else pl.multiple_of(k * width, LANES)
    return ref.at[:, pl.ds(start, width)]


CHUNK_BYTES = 1 << 20


def _pieces(shape, itemsize, nbytes=CHUNK_BYTES):
    rows, cols = shape[-2], shape[-1]
    per = max(16, nbytes // (cols * itemsize) // 16 * 16)
    out = []
    for lead in (range(shape[0]) if len(shape) == 3 else (None,)):
        for r in range(0, rows, per):
            sl = (pl.ds(r, min(per, rows - r)), slice(None))
            out.append(sl if lead is None else (lead,) + sl)
    return out


def _start_pieces(src, dst, make, nbytes=CHUNK_BYTES):
    for idx in _pieces(src.shape, jnp.dtype(src.dtype).itemsize, nbytes):
        make(src.at[idx], dst.at[idx]).start()


def gather_halves(shards):
    n = len(shards)

    def body(*refs):
        srcs, dsts = refs[:n], refs[n:2 * n]
        send, recv, own_send, own_recv = refs[2 * n:]
        x, y, c, k = _place()
        sib = (x, y, 1 - c)
        chips = [(1 - x, y), (x, 1 - y), (1 - x, 1 - y)]

        def slot(a, layer, pos):
            if a == 0:
                return _cols(dsts[0].at[layer], pos, WIN_SHARD)
            return dsts[a].at[pos, layer]

        def ici(a, j, src, dst):
            return _rcopy(src, dst, send.at[a, j], recv.at[a, j], (*chips[j], c))

        def d2d(a, j, src, dst):
            return _rcopy(src, dst, send.at[a, 3 + j], recv.at[a, 3 + j], sib)

        def own(a, layer, src, dst):
            return _rcopy(src, dst, own_send.at[a, layer], own_recv.at[a, layer], sib)

        for a in range(n):
            for j in range(3):
                _start_pieces(srcs[a].at[c], slot(a, c, k), functools.partial(ici, a, j))
        for a in range(n):
            for layer in range(DEPTH):
                _start_pieces(srcs[a].at[layer], slot(a, layer, k), functools.partial(own, a, layer))
        for a in range(n):
            for j, (cx, cy) in enumerate(chips):
                landed = slot(a, c, 2 * cx + cy)
                ici(a, j, landed, landed).wait_recv()
                _start_pieces(landed, landed, functools.partial(d2d, a, j))
        for a in range(n):
            for j, (cx, cy) in enumerate(chips):
                passed = slot(a, 1 - c, 2 * cx + cy)
                d2d(a, j, passed, passed).wait_recv()
                landed = slot(a, c, 2 * cx + cy)
                d2d(a, j, landed, landed).wait_send()
                ici(a, j, srcs[a].at[c], slot(a, c, k)).wait_send()
            for layer in range(DEPTH):
                own(a, layer, srcs[a].at[layer], slot(a, layer, k)).wait()

    outs = [jax.ShapeDtypeStruct((2, shards[0].shape[1], NIN), shards[0].dtype)]
    outs += [jax.ShapeDtypeStruct((NCHIP,) + s.shape, s.dtype) for s in shards[1:]]
    return pl.pallas_call(
        body, name="gather_halves", in_specs=[ANY] * n, out_specs=[ANY] * n, out_shape=outs,
        scratch_shapes=[pltpu.SemaphoreType.DMA((n, 6)), pltpu.SemaphoreType.DMA((n, 6)),
                        pltpu.SemaphoreType.DMA((n, DEPTH)), pltpu.SemaphoreType.DMA((n, DEPTH))],
    )(*shards)


def _gather_slot(dst, pos):
    return _cols(dst, pos, WIN_SHARD) if len(dst.shape) == 2 else dst.at[pos]


def _gather_copy(a, j, src, dst, send, recv, dev):
    return _rcopy(src, dst, send.at[a * NCHIP + j], recv.at[a * NCHIP + j], dev)


def gather_start(tag, shards, after):
    n = len(shards)

    def body(*refs):
        srcs, dsts = refs[:n], refs[n:2 * n]
        send, recv = refs[2 * n + len(after)], refs[2 * n + len(after) + 1]
        token = refs[-1]
        x, y, c, k = _place()
        peers = [(1 - x, y, c), (x, 1 - y, c), (1 - x, 1 - y, c), (x, y, 1 - c)]
        for a in range(n):
            for j, dev in enumerate(peers):
                _start_pieces(srcs[a], _gather_slot(dsts[a], k),
                              lambda s, d, a=a, j=j, dev=dev: _gather_copy(a, j, s, d, send, recv, dev))
        token[...] = jnp.zeros_like(token)

    gathered = [lax.empty((D, NIN) if s.shape == (D, WIN_SHARD) else (NCHIP,) + s.shape, s.dtype) for s in shards]
    ops = [pltpu.with_memory_space_constraint(v, pltpu.HBM) for v in list(shards) + gathered]
    sem = pltpu.SemaphoreType.DMA((n * NCHIP,))
    res = pl.pallas_call(
        body, name=f"gather_start{tag}", in_specs=[HBM] * (2 * n) + [ANY] * len(after),
        out_specs=[SEMS, SEMS] + [HBM] * (2 * n) + [pl.BlockSpec(memory_space=pltpu.VMEM)],
        out_shape=[sem, sem] + [pltpu.HBM(v.shape, v.dtype) for v in ops] + [jax.ShapeDtypeStruct((8, LANES), f32)],
        input_output_aliases={i: 2 + i for i in range(2 * n)},
        compiler_params=pltpu.CompilerParams(has_side_effects=EFFECT),
    )(*ops, *after)
    return res[0], res[1], res[2:2 + n], res[2 + n:2 + 2 * n], res[-1]


def gather_wait(tag, send, recv, shards, gathered, after):
    n = len(shards)

    def body(*refs):
        srcs, dsts = refs[:n], refs[n:2 * n]
        send_r, recv_r = refs[2 * n], refs[2 * n + 1]
        x, y, c, k = _place()
        peers = [(1 - x, y, c), (x, 1 - y, c), (1 - x, 1 - y, c), (x, y, 1 - c)]
        for a in range(n):
            for j, dev in enumerate(peers):
                _gather_copy(a, j, srcs[a], _gather_slot(dsts[a], k), send_r, recv_r, dev).wait_send()
                pos = 2 * dev[0] + dev[1]
                _gather_copy(a, j, srcs[a], _gather_slot(dsts[a], pos), send_r, recv_r, dev).wait_recv()

    ops = list(shards) + list(gathered)
    res = pl.pallas_call(
        body, name=f"gather_wait{tag}", in_specs=[HBM] * (2 * n) + [SEMS, SEMS] + [ANY] * len(after),
        out_specs=[HBM] * (2 * n), out_shape=[pltpu.HBM(v.shape, v.dtype) for v in ops],
        input_output_aliases={i: i for i in range(2 * n)},
        compiler_params=pltpu.CompilerParams(has_side_effects=EFFECT),
    )(*ops, send, recv, *after)
    return res[n:]


def _half(ref, h):
    rows = ref.shape[-2] // 2
    start = pl.multiple_of(h * rows, 16)
    if len(ref.shape) == 2:
        return ref.at[pl.ds(start, rows), :]
    return ref.at[:, pl.ds(start, rows), :]


HBM = pl.BlockSpec(memory_space=pltpu.HBM)
SEMS = pl.BlockSpec(memory_space=pltpu.SEMAPHORE)
EFFECT = pltpu.SideEffectType.DATAFLOW_SIDE_EFFECTING


def rs_pair_start(tag, grads):
    n = len(grads)

    def body(*refs):
        g, theirs = refs[:n], refs[n:2 * n]
        send, recv = refs[2 * n], refs[2 * n + 1]
        x, y, c, _ = _place()
        for a in range(n):
            _start_pieces(_half(g[a], 1 - c), theirs[a],
                          lambda s, d, a=a: _rcopy(s, d, send.at[a], recv.at[a], (x, y, 1 - c)))
        refs[-1][...] = jnp.zeros_like(refs[-1])

    lands = [lax.empty(g.shape[:-2] + (g.shape[-2] // 2, g.shape[-1]), g.dtype) for g in grads]
    ops = [pltpu.with_memory_space_constraint(v, pltpu.HBM) for v in list(grads) + lands]
    sem = pltpu.SemaphoreType.DMA((n,))
    res = pl.pallas_call(
        body, name=f"rs_pair_start{tag}", in_specs=[HBM] * (2 * n),
        out_specs=[SEMS, SEMS] + [HBM] * (2 * n) + [pl.BlockSpec(memory_space=pltpu.VMEM)],
        out_shape=[sem, sem] + [pltpu.HBM(v.shape, v.dtype) for v in ops] + [jax.ShapeDtypeStruct((8, LANES), f32)],
        input_output_aliases={i: 2 + i for i in range(2 * n)},
        compiler_params=pltpu.CompilerParams(has_side_effects=EFFECT),
    )(*ops)
    return res[0], res[1], res[2:2 + n], res[2 + n:2 + 2 * n], res[-1]


def rs_pair_wait(tag, send, recv, grads, theirs, after):
    n = len(grads)

    def body(*refs):
        g, land = refs[:n], refs[n:2 * n]
        send_r, recv_r = refs[2 * n], refs[2 * n + 1]
        x, y, c, _ = _place()
        for a in range(n):
            cp = _rcopy(_half(g[a], 1 - c), land[a], send_r.at[a], recv_r.at[a], (x, y, 1 - c))
            cp.wait_send()
            cp.wait_recv()

    ops = list(grads) + list(theirs)
    res = pl.pallas_call(
        body, name=f"rs_pair_wait{tag}", in_specs=[HBM] * (2 * n) + [SEMS, SEMS] + [ANY] * len(after),
        out_specs=[HBM] * (2 * n), out_shape=[pltpu.HBM(v.shape, v.dtype) for v in ops],
        input_output_aliases={i: i for i in range(2 * n)},
        compiler_params=pltpu.CompilerParams(has_side_effects=EFFECT),
    )(*ops, send, recv, *after)
    return res[:n], res[n:]


def _chip_piece(ref, k):
    return _cols(ref, k, WIN_SHARD) if len(ref.shape) == 2 else ref.at[k]


def _chip_copy(a, k, src, dst, send, recv, me, c):
    return _rcopy(src, dst, send.at[a * NCHIP + k], recv.at[a * NCHIP + me], (k // 2, k % 2, c))


def rs_chips_start(tag, sums):
    n = len(sums)

    def pshape(s):
        return (NCHIP, s[0], WIN_SHARD) if len(s) == 2 else s

    def body(*refs):
        s, land = refs[:n], refs[n:2 * n]
        send, recv = refs[2 * n], refs[2 * n + 1]
        token = refs[-1]
        x, y, c, me = _place()
        for k in range(NCHIP):
            @pl.when(me != k)
            def _():
                for a in range(n):
                    _start_pieces(_chip_piece(s[a], k), land[a].at[me],
                                  lambda src, dst, a=a: _chip_copy(a, k, src, dst, send, recv, me, c))
        token[...] = jnp.zeros_like(token)

    lands = [lax.empty(pshape(v.shape), v.dtype) for v in sums]
    ops = [pltpu.with_memory_space_constraint(v, pltpu.HBM) for v in list(sums) + lands]
    sem = pltpu.SemaphoreType.DMA((n * NCHIP,))
    res = pl.pallas_call(
        body, name=f"rs_chips_start{tag}", in_specs=[HBM] * (2 * n),
        out_specs=[SEMS, SEMS] + [HBM] * (2 * n) + [pl.BlockSpec(memory_space=pltpu.VMEM)],
        out_shape=[sem, sem] + [pltpu.HBM(v.shape, v.dtype) for v in ops] + [jax.ShapeDtypeStruct((8, LANES), f32)],
        input_output_aliases={i: 2 + i for i in range(2 * n)},
        compiler_params=pltpu.CompilerParams(has_side_effects=EFFECT),
    )(*ops)
    return res[0], res[1], res[2:2 + n], res[2 + n:2 + 2 * n], res[-1]


def rs_chips_wait(tag, send, recv, sums, lands, after):
    n = len(sums)

    def body(*refs):
        s, land = refs[:n], refs[n:2 * n]
        send_r, recv_r = refs[2 * n], refs[2 * n + 1]
        x, y, c, me = _place()
        for k in range(NCHIP):
            @pl.when(me != k)
            def _():
                for a in range(n):
                    piece = _chip_piece(s[a], k)
                    _chip_copy(a, k, piece, land[a].at[me], send_r, recv_r, me, c).wait_send()
                    _rcopy(piece, land[a].at[k], send_r.at[a * NCHIP + k], recv_r.at[a * NCHIP + k],
                           (k // 2, k % 2, c)).wait_recv()

    ops = list(sums) + list(lands)
    res = pl.pallas_call(
        body, name=f"rs_chips_wait{tag}", in_specs=[HBM] * (2 * n) + [SEMS, SEMS] + [ANY] * len(after),
        out_specs=[HBM] * (2 * n), out_shape=[pltpu.HBM(v.shape, v.dtype) for v in ops],
        input_output_aliases={i: i for i in range(2 * n)},
        compiler_params=pltpu.CompilerParams(has_side_effects=EFFECT),
    )(*ops, send, recv, *after)
    return res[:n], res[n:]


def rs_join(tag, halves):
    n = len(halves)

    def body(*refs):
        h, other = refs[:n], refs[n:2 * n]
        send, recv = refs[2 * n:]
        x, y, c, _ = _place()

        def give(a, s, d):
            return _rcopy(s, d, send.at[a], recv.at[a], (x, y, 1 - c))

        for a in range(n):
            _start_pieces(h[a], other[a], functools.partial(give, a))
        for a in range(n):
            give(a, h[a], other[a]).wait()

    outs = [jax.ShapeDtypeStruct(v.shape, v.dtype) for v in halves]
    return pl.pallas_call(
        body, name=f"rs_join{tag}", in_specs=[ANY] * n, out_specs=[ANY] * n, out_shape=outs,
        scratch_shapes=[pltpu.SemaphoreType.DMA((n,))] * 2,
    )(*halves)


def _row_tile(rows, cols, itemsize=4, target=2 << 20):
    best = 8
    for t in range(8, rows + 1, 8):
        if rows % t == 0 and t * cols * itemsize <= target:
            best = t
    return best


GRAD_WIRE = jnp.bfloat16


def add_n(name, terms, out_dtype=f32):
    shape = terms[0].shape
    cols = shape[-1]
    rows = math.prod(shape[:-1])
    tr = _row_tile(rows, cols)

    def body(*refs):
        acc = refs[0][...]
        for r in refs[1:-1]:
            acc = acc + r[...]
        refs[-1][...] = acc.astype(out_dtype)

    tile = pl.BlockSpec((tr, cols), lambda i: (i, 0))
    out = pl.pallas_call(
        body, name=name, grid=(rows // tr,), in_specs=[tile] * len(terms), out_specs=tile,
        out_shape=jax.ShapeDtypeStruct((rows, cols), out_dtype), compiler_params=_cp(("parallel",)),
    )(*[t.reshape(rows, cols) for t in terms])
    return out.reshape(shape)


def add_chips(name, land, own):
    _, rows, cols = land.shape
    tr = _row_tile(rows, cols, target=1 << 20)

    def body(land_r, own_r, o_r):
        me = 2 * lax.axis_index("x") + lax.axis_index("y")
        for k in range(NCHIP):
            @pl.when(me == k)
            def _():
                acc = None
                for j in range(NCHIP):
                    t = (own_r[...] if j == k else land_r[j]).astype(f32)
                    acc = t if acc is None else acc + t
                o_r[...] = acc

    tile = pl.BlockSpec((tr, cols), lambda i: (i, 0))
    return pl.pallas_call(
        body, name=name, grid=(rows // tr,), in_specs=[pl.BlockSpec((NCHIP, tr, cols), lambda i: (0, i, 0)), tile],
        out_specs=tile, out_shape=jax.ShapeDtypeStruct((rows, cols), f32), compiler_params=_cp(("parallel",)),
    )(land, own)


def reduce_scatter_pair(tag, G):
    names = tuple(G)
    grads = [G[n] if G[n].ndim == 3 or n == "w_in" else G[n].reshape(NCHIP, D // NCHIP, D) for n in names]
    send, recv, grads, theirs, token = rs_pair_start(tag, grads)
    return (tag, names, send, recv, grads, theirs), token[0, 0]


def reduce_scatter_chips(state, after):
    c = lax.axis_index("c")
    tag, names, send, recv, grads, theirs = state
    grads, theirs = rs_pair_wait(tag, send, recv, grads, theirs, after)
    sums = []
    for n, g, t in zip(names, grads, theirs):
        rows = g.shape[-2] // 2
        mine = lax.dynamic_slice_in_dim(g, c * rows, rows, axis=g.ndim - 2)
        sums.append(add_n(f"rs_add_pair{tag}_{n}", [mine, t], GRAD_WIRE))
    send, recv, sums, lands, token = rs_chips_start(tag, sums)
    return (tag, names, send, recv, sums, lands), token[0, 0]


def reduce_scatter_finish(state, after):
    me = 2 * lax.axis_index("x") + lax.axis_index("y")
    tag, names, send, recv, sums, lands = state
    sums, landed = rs_chips_wait(tag, send, recv, sums, lands, after)
    halves = []
    for n, s, v in zip(names, sums, landed):
        own = lax.dynamic_slice_in_dim(s, me * WIN_SHARD, WIN_SHARD, axis=1) if s.ndim == 2 else \
            lax.dynamic_index_in_dim(s, me, 0, keepdims=False)
        halves.append(add_chips(f"rs_add_chips{tag}_{n}", v, own))
    return dict(zip(names, zip(halves, rs_join(tag, halves))))


NDEV = 8


def _small_copy(r, src, dst, send, recv, x, y, c):
    return _rcopy(src, dst, send.at[r - 1], recv.at[r - 1], (x ^ (r >> 2), y ^ ((r >> 1) & 1), c ^ (r & 1)))


def small_start(pack):
    def body(p, land, send, recv, p_thru, land_thru, token):
        x, y, c, _ = _place()
        me = 4 * x + 2 * y + c
        for r in range(1, NDEV):
            _start_pieces(p, land.at[me], lambda s, d, r=r: _small_copy(r, s, d, send, recv, x, y, c), 128 << 10)
        token[...] = jnp.zeros_like(token)

    ops = [pltpu.with_memory_space_constraint(v, pltpu.HBM) for v in (pack, lax.empty((NDEV,) + pack.shape, f32))]
    sem = pltpu.SemaphoreType.DMA((NDEV - 1,))
    return pl.pallas_call(
        body, name="small_start", in_specs=[HBM, HBM],
        out_specs=[SEMS, SEMS, HBM, HBM, pl.BlockSpec(memory_space=pltpu.VMEM)],
        out_shape=[sem, sem] + [pltpu.HBM(v.shape, v.dtype) for v in ops] + [jax.ShapeDtypeStruct((8, LANES), f32)],
        input_output_aliases={0: 2, 1: 3}, compiler_params=pltpu.CompilerParams(has_side_effects=EFFECT),
    )(*ops)


def small_wait(send, recv, pack, land, after):
    def body(p, land_r, send_r, recv_r, *rest):
        x, y, c, _ = _place()
        me = 4 * x + 2 * y + c
        for r in range(1, NDEV):
            _small_copy(r, p, land_r.at[me], send_r, recv_r, x, y, c).wait_send()
            src = 4 * (x ^ (r >> 2)) + 2 * (y ^ ((r >> 1) & 1)) + (c ^ (r & 1))
            _small_copy(r, p, land_r.at[src], send_r, recv_r, x, y, c).wait_recv()

    return pl.pallas_call(
        body, name="small_wait", in_specs=[HBM, HBM, SEMS, SEMS] + [ANY] * len(after), out_specs=[HBM, HBM],
        out_shape=[pltpu.HBM(pack.shape, f32), pltpu.HBM(land.shape, f32)], input_output_aliases={0: 0, 1: 1},
        compiler_params=pltpu.CompilerParams(has_side_effects=EFFECT),
    )(pack, land, send, recv, *after)


def small_sum(land, pack):
    def body(land_r, p_r, o_r):
        me = 4 * lax.axis_index("x") + 2 * lax.axis_index("y") + lax.axis_index("c")
        for k in range(NDEV):
            @pl.when(me == k)
            def _():
                acc = None
                for d in range(NDEV):
                    t = p_r[...] if d == k else land_r[d]
                    acc = t if acc is None else acc + t
                o_r[...] = acc

    vm = pl.BlockSpec(memory_space=pltpu.VMEM)
    return pl.pallas_call(
        body, name="small_sum", in_specs=[vm, vm], out_specs=vm, out_shape=jax.ShapeDtypeStruct(pack.shape, f32),
        compiler_params=pltpu.CompilerParams(vmem_limit_bytes=40 << 20),
    )(land, pack)


def _adamw_math(w, g, m, v):
    m = ADAM_B1 * m + (1.0 - ADAM_B1) * g
    v = ADAM_B2 * v + (1.0 - ADAM_B2) * (g * g)
    m_hat = m / (1.0 - ADAM_B1 ** ADAM_STEP)
    v_hat = v / (1.0 - ADAM_B2 ** ADAM_STEP)
    return -ADAM_LR * (m_hat / (jnp.sqrt(v_hat) + ADAM_EPS) + ADAM_WD * w), m, v


def adamw_big(name, halves, w, m, v):
    _, R, C = w.shape
    tr = _row_tile(R // 2, C, target=1 << 20)
    nt = R // 2 // tr

    def body(a0, b0, a1, b1, w_r, m_r, v_r, g_o, d_o, m_o, v_o):
        mine = pl.program_id(1) == lax.axis_index("c")
        g = jnp.where(pl.program_id(0) == 0, jnp.where(mine, a0[...], b0[...]), jnp.where(mine, a1[...], b1[...]))
        g_o[...] = g
        d_o[...], m_o[...], v_o[...] = _adamw_math(w_r[...], g, m_r[...], v_r[...])

    stk = pl.BlockSpec((None, tr, C), lambda l, h, i: (l, h * nt + i, 0))
    lay0 = pl.BlockSpec((tr, C), lambda l, h, i: (jnp.where(l == 0, i, nt - 1), 0))
    lay1 = pl.BlockSpec((tr, C), lambda l, h, i: (jnp.where(l == 0, 0, i), 0))
    return pl.pallas_call(
        body, name=name, grid=(DEPTH, 2, nt),
        in_specs=[lay0, lay0, lay1, lay1, stk, stk, stk],
        out_specs=[stk] * 4, out_shape=[jax.ShapeDtypeStruct(w.shape, f32)] * 4,
        compiler_params=_cp(("arbitrary", "arbitrary", "arbitrary")),
    )(*halves[0], *halves[1], w, m, v)


def adamw_small(name, g, w, m, v):
    def body(g_r, w_r, m_r, v_r, d_o, m_o, v_o):
        d_o[...], m_o[...], v_o[...] = _adamw_math(w_r[...], g_r[...], m_r[...], v_r[...])

    return pl.pallas_call(body, name=name, out_shape=[jax.ShapeDtypeStruct(w.shape, f32)] * 3)(g, w, m, v)


WEIGHTS = ("w_in", "conv_w", "gmlp_ln_g", "gmlp_ln_b", "w_s", "b_s", "p_a", "p_b", "p_c", "w_o", "ln1_g", "ln1_b",
           "w_gate", "w_up", "w_down", "ln2_g", "ln2_b")
VECS = ("ln1_g", "ln1_b", "ln2_g", "ln2_b", "gmlp_ln_g", "gmlp_ln_b")
ROWS_VEC, ROWS_BS, ROWS_WS, ROWS_CONV = D // LANES, 8, 8 * BLK, 3 * D // LANES
ROWS_LAYER = len(VECS) * ROWS_VEC + ROWS_BS + ROWS_WS + ROWS_CONV


def _pack_small(per_layer, tail):
    parts = []
    for P in per_layer:
        parts += [P[n].reshape(ROWS_VEC, LANES) for n in VECS]
        parts += [P["b_s"].reshape(ROWS_BS, LANES), P["w_s"].reshape(ROWS_WS, LANES), P["conv_w"].reshape(ROWS_CONV, LANES)]
    return jnp.concatenate(parts + [tail], axis=0)


def _unpack_small(pack):
    out = []
    for l in range(DEPTH):
        r = l * ROWS_LAYER
        P = {}
        for n in VECS:
            P[n] = pack[r:r + ROWS_VEC].reshape(D)
            r += ROWS_VEC
        P["b_s"] = pack[r:r + ROWS_BS].reshape(8, BLK)
        r += ROWS_BS
        P["w_s"] = pack[r:r + ROWS_WS].reshape(8, BLK, BLK)
        r += ROWS_WS
        P["conv_w"] = pack[r:r + ROWS_CONV].reshape(3, D)
        out.append(P)
    return out, pack[DEPTH * ROWS_LAYER:]


def kernel(x, positions, w_in, conv_w, gmlp_ln_g, gmlp_ln_b, w_s, b_s, p_a, p_b, p_c, w_o, ln1_g, ln1_b, w_gate, w_up, w_down, ln2_g, ln2_b, loss_target, m_w_in, m_conv_w, m_gmlp_ln_g, m_gmlp_ln_b, m_w_s, m_b_s, m_p_a, m_p_b, m_p_c, m_w_o, m_ln1_g, m_ln1_b, m_w_gate, m_w_up, m_w_down, m_ln2_g, m_ln2_b, v_w_in, v_conv_w, v_gmlp_ln_g, v_gmlp_ln_b, v_w_s, v_b_s, v_p_a, v_p_b, v_p_c, v_w_o, v_ln1_g, v_ln1_b, v_w_gate, v_w_up, v_w_down, v_ln2_g, v_ln2_b):
    Wt = dict(w_in=w_in, conv_w=conv_w, gmlp_ln_g=gmlp_ln_g, gmlp_ln_b=gmlp_ln_b, w_s=w_s, b_s=b_s, p_a=p_a, p_b=p_b,
              p_c=p_c, w_o=w_o, ln1_g=ln1_g, ln1_b=ln1_b, w_gate=w_gate, w_up=w_up, w_down=w_down, ln2_g=ln2_g, ln2_b=ln2_b)
    Mt = dict(w_in=m_w_in, conv_w=m_conv_w, gmlp_ln_g=m_gmlp_ln_g, gmlp_ln_b=m_gmlp_ln_b, w_s=m_w_s, b_s=m_b_s, p_a=m_p_a,
              p_b=m_p_b, p_c=m_p_c, w_o=m_w_o, ln1_g=m_ln1_g, ln1_b=m_ln1_b, w_gate=m_w_gate, w_up=m_w_up,
              w_down=m_w_down, ln2_g=m_ln2_g, ln2_b=m_ln2_b)
    Vt = dict(w_in=v_w_in, conv_w=v_conv_w, gmlp_ln_g=v_gmlp_ln_g, gmlp_ln_b=v_gmlp_ln_b, w_s=v_w_s, b_s=v_b_s, p_a=v_p_a,
              p_b=v_p_b, p_c=v_p_c, w_o=v_w_o, ln1_g=v_ln1_g, ln1_b=v_ln1_b, w_gate=v_w_gate, w_up=v_w_up,
              w_down=v_w_down, ln2_g=v_ln2_g, ln2_b=v_ln2_b)
    chip = 2 * lax.axis_index("x") + lax.axis_index("y")
    cw = D // NCHIP

    def gathered_weights(names, arrays):
        Wl = dict(zip(names, arrays))
        for n in ("p_a", "p_c", "w_o"):
            Wl[n] = Wl[n].reshape(D, D)
        return Wl

    def small_weights(l, conv_all):
        Wl = {n: Wt[n][l] for n in VECS + ("w_s", "b_s")}
        Wl["conv_w"] = conv_all[:, l].transpose(1, 0, 2).reshape(3, D)
        return Wl

    w_in0, conv_all = gather_halves([Wt["w_in"][0].astype(MX).reshape(2, D // 2, WIN_SHARD), conv_w])
    rest = BIG[1:]
    *late0, coming0 = gather_start("0", [Wt[n][0].astype(MX) for n in rest], [conv_all])
    *late1, coming1 = gather_start("1", [Wt[n][1].astype(MX) for n in BIG], [conv_all, coming0])
    W0 = dict(small_weights(0, conv_all), w_in=w_in0.reshape(D, NIN), after=coming1[0, 0],
              late=lambda y: gathered_weights(rest, gather_wait("0", *late0, [y])))

    def W1(h):
        return dict(small_weights(1, conv_all), **gathered_weights(BIG, gather_wait("1", *late1, [h])))

    layers = [W0, W1]

    rs_state, rs_started, held = {}, {}, {}

    def start_exchange(l, g):
        if "loss" in g:
            held[l] = g
            return None
        if "conv_w" in g:
            held[l] = g
            rs_state[(l, False)], started = reduce_scatter_chips(rs_state[(l, False)], [g["w_s"], g["conv_w"]])
            if l == 0:
                pack = _pack_small([held[j] for j in range(DEPTH)], held[DEPTH]["loss"])
                *held["small"], token = small_start(pack)
                started = started + token[0, 0]
            return started
        if "dx" in g:
            rs_state[(l, True)], rs_started[(l, True)] = reduce_scatter_chips(rs_state[(l, True)], [g["dx"]])
            return rs_started[(l, True)]
        key = (l, "w_in" in g)
        rs_state[key], started = reduce_scatter_pair(f"{l}{'b' if key[1] else 'a'}", g)
        return started

    _, grad_x, _ = local_step(x[0], positions[0], loss_target[0], layers, start_exchange)

    last = jnp.zeros((8, LANES), f32) + rs_started[(0, True)]
    behind = [grad_x, last]
    red = [dict() for _ in range(DEPTH)]
    for key in ((1, False), (1, True), (0, False)):
        red[key[0]].update(reduce_scatter_finish(rs_state[key], behind))
    small, tail = _unpack_small(small_sum(*reversed(small_wait(*held["small"], behind))))
    loss = tail[0, 0]

    G, DW, NM, NV = {}, {}, {}, {}
    zc = jnp.zeros((3, D), f32)
    wp = _pack_small([{**{n: Wt[n][l] for n in VECS + ("b_s", "w_s")}, "conv_w": zc} for l in range(DEPTH)], jnp.zeros((8, LANES), f32))
    mp = _pack_small([{**{n: Mt[n][l] for n in VECS + ("b_s", "w_s")}, "conv_w": zc} for l in range(DEPTH)], jnp.zeros((8, LANES), f32))
    vp = _pack_small([{**{n: Vt[n][l] for n in VECS + ("b_s", "w_s")}, "conv_w": zc} for l in range(DEPTH)], jnp.ones((8, LANES), f32))
    gp = _pack_small(small, jnp.zeros((8, LANES), f32))
    outs = [_unpack_small(a)[0] for a in adamw_small("adamw_small", gp, wp, mp, vp)]
    for n in VECS + ("b_s", "w_s"):
        G[n] = jnp.stack([small[l][n] for l in range(DEPTH)])
        DW[n], NM[n], NV[n] = (jnp.stack([o[l][n] for l in range(DEPTH)]) for o in outs)
    gconv = jnp.stack([lax.dynamic_slice(small[l]["conv_w"], (0, chip * cw), (3, cw)) for l in range(DEPTH)])
    G["conv_w"] = gconv
    flat = lambda a: a.reshape(DEPTH * 3, cw)
    d, m2, v2 = adamw_small("adamw_conv", flat(gconv), flat(conv_w), flat(m_conv_w), flat(v_conv_w))
    DW["conv_w"], NM["conv_w"], NV["conv_w"] = (a.reshape(DEPTH, 3, cw) for a in (d, m2, v2))

    updated = {}
    for n in BIG[1:]:
        tr = (lambda a: jnp.swapaxes(a, 1, 2)) if n in ("w_gate", "w_up") else (lambda a: a)
        updated[n] = adamw_big("adamw_" + n, (red[0][n], red[1][n]), tr(Wt[n]), tr(Mt[n]), tr(Vt[n]))
        G[n], DW[n], NM[n], NV[n] = map(tr, updated[n])
    done = [d, DW["ln2_b"], red[1]["w_in"][1]] + [updated[n][1] for n in BIG[1:]]
    red[0].update(reduce_scatter_finish(rs_state[(0, True)], done))
    G["w_in"], DW["w_in"], NM["w_in"], NV["w_in"] = adamw_big(
        "adamw_w_in", (red[0]["w_in"], red[1]["w_in"]), Wt["w_in"], Mt["w_in"], Vt["w_in"])

    return (loss, grad_x[None], *[G[n] for n in WEIGHTS], *[DW[n] for n in WEIGHTS], *[NM[n] for n in WEIGHTS],
            *[NV[n] for n in WEIGHTS])
```

```python
import functools
import math

import jax
import jax.numpy as jnp
from jax import lax
from jax.experimental import pallas as pl
from jax.experimental.pallas import tpu as pltpu

D = 1024
NIN = 12800
DFF = 2816
NCHIP = 4
FB = DFF // NCHIP
WIN_SHARD = NIN // NCHIP
DEPTH = 2
GROUPS = ((128, 1), (512, 4), (2048, 16))
HD = 64
BLK = 128
AO = 512
ALPHA = (2 * DEPTH) ** 0.25
EPS = 1e-5
ROPE_THETA = 10000.0
LANES = 128
NEG = -1e30

C_GATES, C_BCH, C_QKV, C_UV = 0, 3 * D, 6 * D, 6 * D + 9 * AO

MX = jnp.bfloat16
ACT = jnp.bfloat16

ADAM_LR, ADAM_B1, ADAM_B2, ADAM_EPS, ADAM_WD, ADAM_STEP = 0.001, 0.9, 0.999, 1e-08, 0.01, 10

f32 = jnp.float32
NT = (((1,), (1,)), ((), ()))
TN = (((0,), (0,)), ((), ()))


def _cp(sem, vmem_mb=48):
    return pltpu.CompilerParams(dimension_semantics=sem, vmem_limit_bytes=vmem_mb << 20)


def _dot(a, b, dims=None):
    if dims is None:
        return jnp.dot(a, b, preferred_element_type=f32)
    return lax.dot_general(a, b, dims, preferred_element_type=f32)


def _ln_stats(r):
    mu = jnp.mean(r, axis=-1, keepdims=True)
    xc = r - mu
    var = jnp.mean(xc * xc, axis=-1, keepdims=True)
    rstd = lax.rsqrt(var + EPS)
    return xc * rstd, rstd


def _ln_bwd(dy, xhat, rstd, g):
    dxh = dy * g
    return rstd * (dxh - jnp.mean(dxh, axis=-1, keepdims=True) - xhat * jnp.mean(dxh * xhat, axis=-1, keepdims=True))


def _gelu(x):
    return 0.5 * x * (1.0 + lax.erf(x * (1.0 / math.sqrt(2.0))))


def _gelu_and_grad(x):
    cdf = 0.5 * (1.0 + lax.erf(x * (1.0 / math.sqrt(2.0))))
    return x * cdf, cdf + x * jnp.exp(-0.5 * x * x) * (1.0 / math.sqrt(2.0 * math.pi))


def _sigmoid(x):
    return 0.5 * jnp.tanh(0.5 * x) + 0.5


def _acc_rows(o_ref, first, val):
    @pl.when(first)
    def _():
        o_ref[...] = jnp.zeros_like(o_ref)
    o_ref[...] += jnp.sum(val, axis=0, keepdims=True)


def mm_in(x, w, bias):
    T = x.shape[0]
    tm, tn = min(2048, T), 1280

    def body(x_ref, w_ref, b_ref, o_ref, xb):
        @pl.when(pl.program_id(1) == 0)
        def _():
            xb[...] = x_ref[...].astype(MX)
        o_ref[...] = (_dot(xb[...], w_ref[...]) + b_ref[...]).astype(o_ref.dtype)

    return pl.pallas_call(
        body, name="mm_in", grid=(T // tm, NIN // tn),
        in_specs=[pl.BlockSpec((tm, D), lambda i, j: (i, 0), pipeline_mode=pl.Buffered(1)),
                  pl.BlockSpec((D, tn), lambda i, j: (0, j)), pl.BlockSpec((1, tn), lambda i, j: (0, j))],
        out_specs=pl.BlockSpec((tm, tn), lambda i, j: (i, j)),
        out_shape=jax.ShapeDtypeStruct((T, NIN), ACT),
        scratch_shapes=[pltpu.VMEM((tm, D), MX)],
        compiler_params=_cp(("parallel", "arbitrary")),
    )(x, w, bias)


HALO = 16
TM_AC = 256


def _uv_specs():
    return [pl.BlockSpec((TM_AC, 512), functools.partial(lambda i, j: (i, j), j=C_UV // 512 + j)) for j in range(4)]


def _gmlp_fwd(up, vp, ws_ref, bs_ref, lg, lb, u=None, gv=None):
    u = _gelu(up) if u is None else u
    xhat, rstd = _ln_stats(_gelu(vp) if gv is None else gv)
    vn = xhat * lg + lb
    vnb = vn.astype(MX)
    rows = []
    for c in range(up.shape[0] // BLK):
        r = slice(c * BLK, (c + 1) * BLK)
        rows.append(jnp.concatenate(
            [_dot(ws_ref[g], vnb[r, g * BLK:(g + 1) * BLK]) + bs_ref[g] for g in range(8)], axis=1))
    return u, vn, xhat, rstd, jnp.concatenate(rows, axis=0)


def mix_ac_fwd(proj, conv_w, wst, bsx, lg, lb):
    T = proj.shape[0]
    tm = TM_AC

    def body(bch, halo, u0, u1, v0, v1, cw, ws, bs, lg_ref, lb_ref, ya, yc, zs):
        i = pl.program_id(0)
        pb = bch[...].astype(f32)
        z = pb[:, D:2 * D] * pb[:, 2 * D:]
        hz = halo[:, :D].astype(f32) * halo[:, D:].astype(f32)
        zs[0:HALO, :] = jnp.where(i > 0, hz, 0.0)
        zs[HALO:HALO + tm, :] = z
        cv = cw[0:1, :] * zs[HALO - 2:HALO - 2 + tm, :] + cw[1:2, :] * zs[HALO - 1:HALO - 1 + tm, :] + cw[2:3, :] * z
        ya[...] = (pb[:, :D] * cv).astype(ya.dtype)
        up = jnp.concatenate([u0[...], u1[...]], axis=1).astype(f32)
        vp = jnp.concatenate([v0[...], v1[...]], axis=1).astype(f32)
        u, _, _, _, sp = _gmlp_fwd(up, vp, ws, bs, lg_ref[...], lb_ref[...])
        yc[...] = (u * sp).astype(yc.dtype)

    full = lambda shape: pl.BlockSpec(shape, lambda i: (0,) * len(shape))
    return pl.pallas_call(
        body, name="mix_ac_fwd", grid=(T // tm,),
        in_specs=[pl.BlockSpec((tm, 3 * D), lambda i: (i, 1)),
                  pl.BlockSpec((HALO, 2 * D), lambda i: (jnp.maximum(i * (tm // HALO) - 1, 0), 2)),
                  *_uv_specs(), full((3, D)), full((8, BLK, BLK)), full((8, BLK, BLK)), full((1, D)), full((1, D))],
        out_specs=[pl.BlockSpec((tm, D), lambda i: (i, 0))] * 2,
        out_shape=[jax.ShapeDtypeStruct((T, D), MX)] * 2,
        scratch_shapes=[pltpu.VMEM((HALO + tm, D), f32)],
        compiler_params=_cp(("parallel",)),
    )(proj, proj, proj, proj, proj, proj, conv_w, wst, bsx, lg, lb)


def _swap_halves(x):
    lane = lax.broadcasted_iota(jnp.int32, x.shape, 1)
    return jnp.where((lane % HD) < HD // 2, pltpu.roll(x, x.shape[1] - HD // 2, 1), pltpu.roll(x, HD // 2, 1))


def _tile4(t):
    return jnp.concatenate([t] * (AO // LANES), axis=1)


TM_FOLD = 512


def _fold_out(nat, x, out_ref, d):
    if d == 1:
        out_ref[0] = x.astype(out_ref.dtype)
        return
    rows = x.shape[0] // d
    for j in range(AO // LANES):
        nat[j] = x[:, j * LANES:(j + 1) * LANES]
    for r in range(d):
        out_ref[r] = jnp.concatenate(
            [nat.at[j][pl.ds(r, rows, stride=d), :] for j in range(AO // LANES)], axis=1).astype(out_ref.dtype)


def _unfold_in(nat, in_ref, d):
    if d == 1:
        return in_ref[0].astype(f32)
    rows = in_ref.shape[1]
    for r in range(d):
        v = in_ref[r].astype(f32)
        for j in range(AO // LANES):
            nat.at[j][pl.ds(r, rows, stride=d), :] = v[:, j * LANES:(j + 1) * LANES]
    return jnp.concatenate([nat[j] for j in range(AO // LANES)], axis=1)


def fold_rope(proj, cos_t, sin_t, g, d):
    T = proj.shape[0]
    tm = TM_FOLD
    rows = tm // d

    def body(x_ref, c_ref, s_ref, q_o, k_o, v_o, nat):
        cos, sin = _tile4(c_ref[...]), _tile4(s_ref[...])
        for part, out, scale in ((0, q_o, HD ** -0.5), (1, k_o, 1.0), (2, v_o, None)):
            x = x_ref[:, part * AO:(part + 1) * AO].astype(f32)
            if scale is not None:
                x = (x * cos + _swap_halves(x) * sin) * scale
            _fold_out(nat, x, out, d)

    fold_spec = pl.BlockSpec((d, rows, AO), lambda i: (0, i, 0))
    return pl.pallas_call(
        body, name=f"fold_rope{g}", grid=(T // tm,),
        in_specs=[pl.BlockSpec((tm, 3 * AO), lambda i: (i, C_QKV // (3 * AO) + g)),
                  pl.BlockSpec((tm, LANES), lambda i: (i, 0)), pl.BlockSpec((tm, LANES), lambda i: (i, 0))],
        out_specs=[fold_spec] * 3,
        out_shape=[jax.ShapeDtypeStruct((d, T // d, AO), MX)] * 3,
        scratch_shapes=[pltpu.VMEM((AO // LANES, tm, LANES), f32)],
        compiler_params=_cp(("parallel",)),
    )(proj, cos_t, sin_t)


def _stack_heads(x):
    lane = lax.broadcasted_iota(jnp.int32, x.shape, 1)
    z = jnp.zeros_like(x)
    return jnp.concatenate([jnp.where(lane < HD, x, z), jnp.where(lane >= HD, x, z)], axis=0)


def _unstack_heads(y):
    lane = lax.broadcasted_iota(jnp.int32, (BLK, LANES), 1)
    return jnp.where(lane < HD, y[:BLK], y[BLK:])


def _window_masks():
    row = lax.broadcasted_iota(jnp.int32, (2 * BLK, 2 * BLK), 0) % BLK
    col = lax.broadcasted_iota(jnp.int32, (2 * BLK, 2 * BLK), 1)
    return (col < BLK) & (col >= row), (col >= BLK) & (col - BLK <= row)


def _two_blocks(ref, b):
    r0 = pl.multiple_of(b * BLK, BLK)
    rp = pl.multiple_of(jnp.maximum(b - 1, 0) * BLK, BLK)
    return jnp.concatenate([ref[pl.ds(rp, BLK), :], ref[pl.ds(r0, BLK), :]], axis=0)


def _merge_masks():
    row = lax.broadcasted_iota(jnp.int32, (2 * BLK, BLK), 0) % BLK
    col = lax.broadcasted_iota(jnp.int32, (2 * BLK, BLK), 1)
    return col <= row, col == row


def attn_fwd(qf, kf, vf, g, nb):
    T = qf.shape[0]

    def body(q_ref, k_ref, v_ref, o_ref, l_ref):
        cur_m, own_m = _merge_masks()

        def step(b, carry):
            r0 = pl.multiple_of(b * BLK, BLK)
            rp = pl.multiple_of(jnp.maximum(b - 1, 0) * BLK, BLK)
            qs = _stack_heads(q_ref[pl.ds(r0, BLK), :])
            vc, vp = v_ref[pl.ds(r0, BLK), :], v_ref[pl.ds(rp, BLK), :]
            sp = jnp.where((b % nb) != 0, _dot(qs, k_ref[pl.ds(rp, BLK), :], NT), NEG)
            s = jnp.where(cur_m, _dot(qs, k_ref[pl.ds(r0, BLK), :], NT), sp)
            s_own = jnp.sum(jnp.where(own_m, sp, 0.0), axis=-1, keepdims=True)
            m = jnp.maximum(jnp.max(s, axis=-1, keepdims=True), s_own)
            p, p_own = jnp.exp(s - m), jnp.exp(s_own - m)
            l = jnp.sum(p, axis=-1, keepdims=True) + p_own
            pb = p.astype(MX)
            zero = jnp.zeros_like(pb)
            o = _dot(jnp.where(cur_m, pb, zero), vc) + _dot(jnp.where(cur_m, zero, pb), vp)
            o = (o + p_own * jnp.concatenate([vp, vp], axis=0).astype(f32)) / l
            o_ref[pl.ds(r0, BLK), :] = _unstack_heads(o)
            l_ref[pl.ds(r0, BLK), :] = _unstack_heads(jnp.broadcast_to(m + jnp.log(l), (2 * BLK, LANES)))
            return carry

        lax.fori_loop(0, T // BLK, step, 0, unroll=8)

    spec = pl.BlockSpec((T, LANES), lambda j: (0, j))
    return pl.pallas_call(
        body, name=f"attn_fwd{g}", grid=(AO // LANES,),
        in_specs=[spec] * 3, out_specs=[spec] * 2,
        out_shape=[jax.ShapeDtypeStruct((T, AO), f32)] * 2,
        compiler_params=_cp(("parallel",), 56),
    )(qf, kf, vf)


def _group_weights(lses):
    m = jnp.maximum(jnp.maximum(lses[0], lses[1]), lses[2])
    e = [jnp.exp(l - m) for l in lses]
    inv = 1.0 / (e[0] + e[1] + e[2])
    return [x * inv for x in e]


def _fold_specs(T, tm):
    specs = []
    for _, d in GROUPS:
        specs.append(pl.BlockSpec((d, tm // d, AO), lambda i: (0, i, 0)))
    return specs


def combine_fwd(os_, lses):
    T = os_[0].shape[0] * os_[0].shape[1]
    tm = TM_FOLD

    def body(o0, o1, o2, l0, l1, l2, y_ref, nat):
        o = [_unfold_in(nat, r, d) for r, (_, d) in zip((o0, o1, o2), GROUPS)]
        ls = [_unfold_in(nat, r, d) for r, (_, d) in zip((l0, l1, l2), GROUPS)]
        w = _group_weights(ls)
        y_ref[...] = (w[0] * o[0] + w[1] * o[1] + w[2] * o[2]).astype(y_ref.dtype)

    specs = _fold_specs(T, tm)
    return pl.pallas_call(
        body, name="combine_fwd", grid=(T // tm,),
        in_specs=specs + specs, out_specs=pl.BlockSpec((tm, AO), lambda i: (i, 0)),
        out_shape=jax.ShapeDtypeStruct((T, AO), MX),
        scratch_shapes=[pltpu.VMEM((AO // LANES, tm, LANES), f32)],
        compiler_params=_cp(("parallel",)),
    )(*os_, *lses)


TM_MIX = 256


def mix_out_fwd(proj, ya, yb, yc, x0, pa, pb, pc, wo, g1, b1):
    T = x0.shape[0]
    tm = min(TM_MIX, T)

    def body(gt, ya_r, yb_r, yc_r, x0_r, pa_r, pb_r, pc_r, wo_r, g_r, b_r, mabc, m_o, r1_o, x1_o):
        ma = _dot(ya_r[...], pa_r[...])
        ybv = yb_r[...]
        mb = jnp.concatenate([_dot(ybv, pb_r[k]) for k in range(NCHIP)], axis=1)
        mc = _dot(yc_r[...], pc_r[...])
        m = jnp.zeros((tm, D), f32)
        for j, mm in enumerate((ma, mb, mc)):
            mabc[:, j * D:(j + 1) * D] = mm.astype(mabc.dtype)
            m = m + _sigmoid(gt[:, j * D:(j + 1) * D].astype(f32)) * mm
        mb16 = m.astype(MX)
        m_o[...] = mb16
        r1 = ALPHA * x0_r[...] + _dot(mb16, wo_r[...])
        r1_o[...] = r1
        xhat, _ = _ln_stats(r1)
        x1_o[...] = xhat * g_r[...] + b_r[...]

    full = lambda shape: pl.BlockSpec(shape, lambda i: (0,) * len(shape))
    tile = lambda w: pl.BlockSpec((tm, w), lambda i: (i, 0))
    return pl.pallas_call(
        body, name="mix_out_fwd", grid=(T // tm,),
        in_specs=[tile(3 * D), tile(D), tile(AO), tile(D), tile(D), full((D, D)), full((NCHIP, AO, D // NCHIP)),
                  full((D, D)), full((D, D)), full((1, D)), full((1, D))],
        out_specs=[tile(3 * D), tile(D), tile(D), tile(D)],
        out_shape=[jax.ShapeDtypeStruct((T, 3 * D), MX), jax.ShapeDtypeStruct((T, D), MX),
                   jax.ShapeDtypeStruct((T, D), f32), jax.ShapeDtypeStruct((T, D), f32)],
        compiler_params=_cp(("parallel",), 56),
    )(proj, ya, yb, yc, x0, pa, pb, pc, wo, g1, b1)


TM_FF = 512
TM_FFB = 256
ROW_CHUNK = 64


def ffn_up_fwd(x1, wg, wu):
    T = x1.shape[0]
    tm = min(TM_FFB, T)

    def body(x_r, wg_r, wu_r, g_o, u_o, h_o, gs, us):
        xb = x_r[...].astype(MX)
        for k in range(NCHIP):
            gs[...] = _dot(xb, wg_r[k])
            us[...] = _dot(xb, wu_r[k])
            for r in range(0, tm, ROW_CHUNK):
                rows = pl.ds(r, ROW_CHUNK)
                gate, up = gs[rows, :], us[rows, :]
                g_o[k, rows, :] = gate.astype(g_o.dtype)
                u_o[k, rows, :] = up.astype(u_o.dtype)
                h_o[k, rows, :] = (gate * _sigmoid(gate) * up).astype(h_o.dtype)

    wspec = pl.BlockSpec((NCHIP, D, FB), lambda i: (0, 0, 0))
    ospec = pl.BlockSpec((NCHIP, tm, FB), lambda i: (0, i, 0))
    return pl.pallas_call(
        body, name="ffn_up_fwd", grid=(T // tm,),
        in_specs=[pl.BlockSpec((tm, D), lambda i: (i, 0)), wspec, wspec],
        out_specs=[ospec] * 3,
        out_shape=[jax.ShapeDtypeStruct((NCHIP, T, FB), ACT)] * 2 + [jax.ShapeDtypeStruct((NCHIP, T, FB), MX)],
        scratch_shapes=[pltpu.VMEM((tm, FB), f32)] * 2,
        compiler_params=_cp(("parallel",)),
    )(x1, wg, wu)


def ffn_down_fwd(hh, wd, x1, g2, b2):
    T = x1.shape[0]
    tm = min(TM_FF, T)

    def body(h_r, w_r, x_r, g_r, b_r, r2_o, x2_o):
        r2 = ALPHA * x_r[...]
        for k in range(NCHIP):
            r2 = r2 + _dot(h_r[k], w_r[k])
        r2_o[...] = r2
        xhat, _ = _ln_stats(r2)
        x2_o[...] = xhat * g_r[...] + b_r[...]

    tile = pl.BlockSpec((tm, D), lambda i: (i, 0))
    vec = pl.BlockSpec((1, D), lambda i: (0, 0))
    return pl.pallas_call(
        body, name="ffn_down_fwd", grid=(T // tm,),
        in_specs=[pl.BlockSpec((NCHIP, tm, FB), lambda i: (0, i, 0)), pl.BlockSpec((NCHIP, FB, D), lambda i: (0, 0, 0)),
                  tile, vec, vec],
        out_specs=[tile, tile], out_shape=[jax.ShapeDtypeStruct((T, D), f32)] * 2,
        compiler_params=_cp(("parallel",)),
    )(hh, wd, x1, g2, b2)


def loss_grad(y, tgt):
    T = y.shape[0]
    tm = min(512, T)

    def body(y_r, t_r, l_o, dy_o):
        e = y_r[...] - t_r[...]
        dy_o[...] = e * (1.0 / D)

        @pl.when(pl.program_id(0) == 0)
        def _():
            l_o[...] = jnp.zeros_like(l_o)
        l_o[...] += (0.5 / D) * jnp.sum(e * e)

    tile = pl.BlockSpec((tm, D), lambda i: (i, 0))
    return pl.pallas_call(
        body, name="loss_grad", grid=(T // tm,),
        in_specs=[tile, tile], out_specs=[pl.BlockSpec((8, LANES), lambda i: (0, 0)), tile],
        out_shape=[jax.ShapeDtypeStruct((8, LANES), f32), jax.ShapeDtypeStruct((T, D), f32)],
        compiler_params=_cp(("arbitrary",)),
    )(y, tgt)


def ffn_down_bwd(dx2, r2, g2, wd, gate, up):
    T = dx2.shape[0]
    tm = min(TM_FFB, T)

    def body(dx_r, r_r, g_r, w_r, ga_r, up_r, dr_o, dg_o, du_o, dlg_o, dlb_o, hs):
        i = pl.program_id(0)
        xhat, rstd = _ln_stats(r_r[...])
        dx = dx_r[...]
        _acc_rows(dlg_o, i == 0, dx * xhat)
        _acc_rows(dlb_o, i == 0, dx)
        dr = _ln_bwd(dx, xhat, rstd, g_r[...])
        dr_o[...] = dr
        drb = dr.astype(MX)
        for k in range(NCHIP):
            hs[...] = _dot(drb, w_r[k], NT)
            for r in range(0, tm, ROW_CHUNK):
                rows = pl.ds(r, ROW_CHUNK)
                dhh, gate_v, up_v = hs[rows, :], ga_r[k, rows, :].astype(f32), up_r[k, rows, :].astype(f32)
                sg = _sigmoid(gate_v)
                dg_o[k, rows, :] = (dhh * up_v * sg * (1.0 + gate_v * (1.0 - sg))).astype(dg_o.dtype)
                du_o[k, rows, :] = (dhh * gate_v * sg).astype(du_o.dtype)

    tile = pl.BlockSpec((tm, D), lambda i: (i, 0))
    vec = pl.BlockSpec((1, D), lambda i: (0, 0))
    blk = pl.BlockSpec((NCHIP, tm, FB), lambda i: (0, i, 0))
    return pl.pallas_call(
        body, name="ffn_down_bwd", grid=(T // tm,),
        in_specs=[tile, tile, vec, pl.BlockSpec((NCHIP, FB, D), lambda i: (0, 0, 0)), blk, blk],
        out_specs=[tile, blk, blk, vec, vec],
        out_shape=[jax.ShapeDtypeStruct((T, D), f32)] + [jax.ShapeDtypeStruct((NCHIP, T, FB), MX)] * 2
        + [jax.ShapeDtypeStruct((1, D), f32)] * 2,
        scratch_shapes=[pltpu.VMEM((tm, FB), f32)],
        compiler_params=_cp(("arbitrary",)),
    )(dx2, r2, g2, wd, gate, up)


def ffn_up_bwd(dr2, dgate, dup, wg, wu, r1, g1):
    T = dr2.shape[0]
    tm = min(TM_FFB, T)

    def body(dr2_r, dg_r, du_r, wg_r, wu_r, r1_r, g_r, dr1_o, dlg_o, dlb_o):
        i = pl.program_id(0)
        dx = ALPHA * dr2_r[...]
        for k in range(NCHIP):
            dx = dx + _dot(dg_r[k], wg_r[k], NT) + _dot(du_r[k], wu_r[k], NT)
        xhat, rstd = _ln_stats(r1_r[...])
        _acc_rows(dlg_o, i == 0, dx * xhat)
        _acc_rows(dlb_o, i == 0, dx)
        dr1_o[...] = _ln_bwd(dx, xhat, rstd, g_r[...])

    tile = pl.BlockSpec((tm, D), lambda i: (i, 0))
    vec = pl.BlockSpec((1, D), lambda i: (0, 0))
    blk = pl.BlockSpec((NCHIP, tm, FB), lambda i: (0, i, 0))
    wspec = pl.BlockSpec((NCHIP, D, FB), lambda i: (0, 0, 0))
    return pl.pallas_call(
        body, name="ffn_up_bwd", grid=(T // tm,),
        in_specs=[tile, blk, blk, wspec, wspec, tile, vec],
        out_specs=[tile, vec, vec],
        out_shape=[jax.ShapeDtypeStruct((T, D), f32)] + [jax.ShapeDtypeStruct((1, D), f32)] * 2,
        compiler_params=_cp(("arbitrary",)),
    )(dr2, dgate, dup, wg, wu, r1, g1)


def mix_out_bwd(dr1, proj, mabc, wo, pa, pb, pc):
    T = dr1.shape[0]
    tm = min(TM_MIX, T)

    def body(dr_r, gt, mabc_r, wo_r, pa_r, pb_r, pc_r, dmabc_o, dgt_o, dya_o, dyb_o, dyc_o):
        dm = _dot(dr_r[...].astype(MX), wo_r[...], NT)
        dmx = []
        for j in range(3):
            s = _sigmoid(gt[:, j * D:(j + 1) * D].astype(f32))
            v = (dm * s).astype(MX)
            dmx.append(v)
            dmabc_o[:, j * D:(j + 1) * D] = v
            dgt_o[:, j * D:(j + 1) * D] = (dm * mabc_r[:, j * D:(j + 1) * D].astype(f32) * s * (1.0 - s)).astype(dgt_o.dtype)
        dya_o[...] = _dot(dmx[0], pa_r[...], NT)
        dyb = jnp.zeros((tm, AO), f32)
        for k in range(NCHIP):
            dyb = dyb + _dot(dmx[1][:, k * (D // NCHIP):(k + 1) * (D // NCHIP)], pb_r[k], NT)
        dyb_o[...] = dyb
        dyc_o[...] = _dot(dmx[2], pc_r[...], NT)

    full = lambda shape: pl.BlockSpec(shape, lambda i: (0,) * len(shape))
    tile = lambda w: pl.BlockSpec((tm, w), lambda i: (i, 0))
    return pl.pallas_call(
        body, name="mix_out_bwd", grid=(T // tm,),
        in_specs=[tile(D), tile(3 * D), tile(3 * D), full((D, D)), full((D, D)), full((NCHIP, AO, D // NCHIP)), full((D, D))],
        out_specs=[tile(3 * D), tile(3 * D), tile(D), tile(AO), tile(D)],
        out_shape=[jax.ShapeDtypeStruct((T, 3 * D), MX), jax.ShapeDtypeStruct((T, 3 * D), MX),
                   jax.ShapeDtypeStruct((T, D), f32), jax.ShapeDtypeStruct((T, AO), f32), jax.ShapeDtypeStruct((T, D), f32)],
        compiler_params=_cp(("parallel",), 56),
    )(dr1, proj, mabc, wo, pa, pb, pc)


def transpose_cast(x):
    T = x.shape[0]
    tm = min(512, T)

    def body(x_r, o_r):
        o_r[...] = x_r[...].T.astype(o_r.dtype)

    return pl.pallas_call(
        body, name="transpose_cast", grid=(T // tm,),
        in_specs=[pl.BlockSpec((tm, D), lambda i: (i, 0))], out_specs=pl.BlockSpec((D, tm), lambda i: (0, i)),
        out_shape=jax.ShapeDtypeStruct((D, T), MX), compiler_params=_cp(("parallel",)),
    )(x)


def tn_matmul(name, a, b, a_spec, b_spec, out_shape, out_spec, grid, a_is_t=False):
    nt = len(grid) - 1

    def body(a_r, b_r, o_r):
        @pl.when(pl.program_id(nt) == 0)
        def _():
            o_r[...] = jnp.zeros_like(o_r)
        av = a_r[...].reshape(a_r.shape[-2:]).astype(MX)
        bv = b_r[...].reshape(b_r.shape[-2:]).astype(MX)
        o_r[...] += _dot(av, bv, None if a_is_t else TN).reshape(o_r.shape)

    return pl.pallas_call(
        body, name=name, grid=grid, in_specs=[a_spec, b_spec], out_specs=out_spec,
        out_shape=jax.ShapeDtypeStruct(out_shape, f32),
        compiler_params=_cp(("parallel",) * nt + ("arbitrary",), 56),
    )(a, b)


def attn_pre_bwd(dyb, os_, lses, ones):
    T = dyb.shape[0]
    tm = TM_FOLD

    def body(dy_r, o0, o1, o2, l0, l1, l2, ones_r, d0, d1, d2, f0, f1, f2, nat):
        o = [_unfold_in(nat, r, d) for r, (_, d) in zip((o0, o1, o2), GROUPS)]
        ls = [_unfold_in(nat, r, d) for r, (_, d) in zip((l0, l1, l2), GROUPS)]
        w = _group_weights(ls)
        dy = dy_r[...]
        t = dy * (w[0] * o[0] + w[1] * o[1] + w[2] * o[2])
        hi = t.astype(MX)
        lo = (t - hi.astype(f32)).astype(MX)
        c = _dot(hi, ones_r[...]) + _dot(lo, ones_r[...])
        for wg, do_o, df_o, (_, d) in zip(w, (d0, d1, d2), (f0, f1, f2), GROUPS):
            _fold_out(nat, wg * dy, do_o, d)
            _fold_out(nat, -wg * c, df_o, d)

    specs = _fold_specs(T, tm)
    return pl.pallas_call(
        body, name="attn_pre_bwd", grid=(T // tm,),
        in_specs=[pl.BlockSpec((tm, AO), lambda i: (i, 0))] + specs + specs + [pl.BlockSpec((AO, AO), lambda i: (0, 0))],
        out_specs=specs + specs,
        out_shape=[jax.ShapeDtypeStruct((d, T // d, AO), MX) for _, d in GROUPS]
        + [jax.ShapeDtypeStruct((d, T // d, AO), f32) for _, d in GROUPS],
        scratch_shapes=[pltpu.VMEM((AO // LANES, tm, LANES), f32)],
        compiler_params=_cp(("parallel",)),
    )(dyb, *os_, *lses, ones)


def _head_ones():
    i = jnp.arange(AO) // HD
    return (i[:, None] == i[None, :]).astype(MX)


BWD_BLOCKS = 4


def attn_bwd(qf, kf, vf, dof, lse, df, g, nb):
    T = qf.shape[0]

    def body(q_ref, k_ref, v_ref, do_ref, l_ref, d_ref, dq_ref, dk_ref, dv_ref):
        prev_m, cur_m = _window_masks()

        def head_col(ref, r0):
            v = ref[pl.ds(r0, BLK), :]
            return jnp.concatenate([v[:, 0:1], v[:, HD:HD + 1]], axis=0)

        def step(b, carry):
            dk_c, dv_c = carry
            r0 = pl.multiple_of(b * BLK, BLK)
            rp = pl.multiple_of(jnp.maximum(b - 1, 0) * BLK, BLK)
            qs, dos = _stack_heads(q_ref[pl.ds(r0, BLK), :]), _stack_heads(do_ref[pl.ds(r0, BLK), :])
            k2, v2 = _two_blocks(k_ref, b), _two_blocks(v_ref, b)
            valid = cur_m | (prev_m & ((b % nb) != 0))
            p = jnp.where(valid, jnp.exp(_dot(qs, k2, NT) - head_col(l_ref, r0)), 0.0)
            ds = (p * (_dot(dos, v2, NT) + head_col(d_ref, r0))).astype(MX)
            dq_ref[pl.ds(r0, BLK), :] = _unstack_heads(_dot(ds, k2)).astype(dq_ref.dtype)
            dk2 = _dot(ds, qs, TN)
            dv2 = _dot(p.astype(MX), dos, TN)
            dk_ref[pl.ds(rp, BLK), :] = (dk_c + dk2[:BLK]).astype(dk_ref.dtype)
            dv_ref[pl.ds(rp, BLK), :] = (dv_c + dv2[:BLK]).astype(dv_ref.dtype)
            return dk2[BLK:], dv2[BLK:]

        zero = jnp.zeros((BLK, LANES), f32)

        def steps(i, carry):
            for j in range(BWD_BLOCKS):
                carry = step(BWD_BLOCKS * i + j, carry)
            return carry

        dk_c, dv_c = lax.fori_loop(0, T // BLK // BWD_BLOCKS, steps, (zero, zero))
        dk_ref[pl.ds(T - BLK, BLK), :] = dk_c.astype(dk_ref.dtype)
        dv_ref[pl.ds(T - BLK, BLK), :] = dv_c.astype(dv_ref.dtype)

    spec = pl.BlockSpec((T, LANES), lambda j: (0, j))
    return pl.pallas_call(
        body, name=f"attn_bwd{g}", grid=(AO // LANES,),
        in_specs=[spec] * 6, out_specs=[spec] * 3,
        out_shape=[jax.ShapeDtypeStruct((T, AO), MX)] * 3,
        compiler_params=_cp(("parallel",), 60),
    )(qf, kf, vf, dof, lse, df)


def unfold_rope_bwd(dqf, dkf, dvf, cos_t, sin_t, g, d):
    T = dqf.shape[0] * dqf.shape[1]
    tm = TM_FOLD

    def body(q_r, k_r, v_r, c_ref, s_ref, o_ref, nat):
        cos, sin = _tile4(c_ref[...]), _tile4(s_ref[...])
        for part, ref, scale in ((0, q_r, HD ** -0.5), (1, k_r, 1.0), (2, v_r, None)):
            x = _unfold_in(nat, ref, d)
            if scale is not None:
                x = (x * cos - _swap_halves(x) * sin) * scale
            o_ref[:, part * AO:(part + 1) * AO] = x.astype(o_ref.dtype)

    fold_spec = pl.BlockSpec((d, tm // d, AO), lambda i: (0, i, 0))
    tab = pl.BlockSpec((tm, LANES), lambda i: (i, 0))
    return pl.pallas_call(
        body, name=f"unfold_rope_bwd{g}", grid=(T // tm,),
        in_specs=[fold_spec] * 3 + [tab, tab],
        out_specs=pl.BlockSpec((tm, 3 * AO), lambda i: (i, 0)),
        out_shape=jax.ShapeDtypeStruct((T, 3 * AO), MX),
        scratch_shapes=[pltpu.VMEM((AO // LANES, tm, LANES), f32)],
        compiler_params=_cp(("parallel",)),
    )(dqf, dkf, dvf, cos_t, sin_t)


CONV_CHUNK = 32


def conv_bwd(dya, proj, conv_w):
    T = dya.shape[0]
    tm = TM_AC
    last = T // tm - 1

    def body(dy_r, bch, hprev, dy_next, b_next, cw, d_o, dw_o, zs, ds):
        i = pl.program_id(0)
        ch = CONV_CHUNK
        hz = hprev[:, :D].astype(f32) * hprev[:, D:].astype(f32)
        zs[0:HALO, :] = jnp.where(i > 0, hz, 0.0)
        ds[tm:tm + HALO, :] = jnp.where(i < last, dy_next[...] * b_next[...].astype(f32), 0.0)
        for r in range(0, tm, ch):
            zs[HALO + r:HALO + r + ch, :] = bch[r:r + ch, D:2 * D].astype(f32) * bch[r:r + ch, 2 * D:].astype(f32)
            ds[r:r + ch, :] = dy_r[r:r + ch, :] * bch[r:r + ch, :D].astype(f32)

        @pl.when(i == 0)
        def _():
            dw_o[...] = jnp.zeros_like(dw_o)

        sums = [jnp.zeros((1, D), f32) for _ in range(3)]
        for r in range(0, tm, ch):
            z2, z1, z = (zs[HALO + r - s:HALO + r - s + ch, :] for s in (2, 1, 0))
            dcv, d1, d2 = (ds[r + s:r + s + ch, :] for s in (0, 1, 2))
            cv = cw[0:1, :] * z2 + cw[1:2, :] * z1 + cw[2:3, :] * z
            dz = cw[2:3, :] * dcv + cw[1:2, :] * d1 + cw[0:1, :] * d2
            d_o[r:r + ch, :D] = (dy_r[r:r + ch, :] * cv).astype(d_o.dtype)
            d_o[r:r + ch, D:2 * D] = (dz * bch[r:r + ch, 2 * D:].astype(f32)).astype(d_o.dtype)
            d_o[r:r + ch, 2 * D:] = (dz * bch[r:r + ch, D:2 * D].astype(f32)).astype(d_o.dtype)
            for k, zz in enumerate((z2, z1, z)):
                sums[k] = sums[k] + jnp.sum(dcv * zz, axis=0, keepdims=True)
        for k in range(3):
            dw_o[k:k + 1, :] += sums[k]

    nh = tm // HALO
    return pl.pallas_call(
        body, name="conv_bwd", grid=(T // tm,),
        in_specs=[pl.BlockSpec((tm, D), lambda i: (i, 0)), pl.BlockSpec((tm, 3 * D), lambda i: (i, 1)),
                  pl.BlockSpec((HALO, 2 * D), lambda i: (jnp.maximum(i * nh - 1, 0), 2)),
                  pl.BlockSpec((HALO, D), lambda i: (jnp.minimum((i + 1) * nh, T // HALO - 1), 0)),
                  pl.BlockSpec((HALO, D), lambda i: (jnp.minimum((i + 1) * nh, T // HALO - 1), 3)),
                  pl.BlockSpec((3, D), lambda i: (0, 0))],
        out_specs=[pl.BlockSpec((tm, 3 * D), lambda i: (i, 0)), pl.BlockSpec((3, D), lambda i: (0, 0))],
        out_shape=[jax.ShapeDtypeStruct((T, 3 * D), MX), jax.ShapeDtypeStruct((3, D), f32)],
        scratch_shapes=[pltpu.VMEM((HALO + tm, D), f32), pltpu.VMEM((tm + HALO, D), f32)],
        compiler_params=_cp(("arbitrary",)),
    )(dya, proj, proj, dya, proj, conv_w)


def gmlp_bwd(dyc, proj, wst, bsx, lg, lb):
    T = dyc.shape[0]
    tm = TM_AC
    last = T // tm - 1

    def body(dy_r, u0, u1, v0, v1, ws, bs, lg_r, lb_r, d_o, dws_o, dbs_o, dlg_o, dlb_o, bacc):
        i = pl.program_id(0)
        up = jnp.concatenate([u0[...], u1[...]], axis=1).astype(f32)
        vp = jnp.concatenate([v0[...], v1[...]], axis=1).astype(f32)
        u, du = _gelu_and_grad(up)
        gv, dgv = _gelu_and_grad(vp)
        u, vn, xhat, rstd, sp = _gmlp_fwd(up, vp, ws, bs, lg_r[...], lb_r[...], u, gv)
        dy = dy_r[...]
        d_o[:, :D] = (dy * sp * du).astype(d_o.dtype)
        dsp = dy * u
        dspb, vnb = dsp.astype(MX), vn.astype(MX)

        @pl.when(i == 0)
        def _():
            dws_o[...] = jnp.zeros_like(dws_o)
            bacc[...] = jnp.zeros_like(bacc)

        rows = []
        for c in range(tm // BLK):
            r = slice(c * BLK, (c + 1) * BLK)
            cols = []
            for g in range(8):
                cs = slice(g * BLK, (g + 1) * BLK)
                dws_o[g] += _dot(dspb[r, cs], vnb[r, cs], NT)
                bacc[g] += dsp[r, cs]
                cols.append(_dot(ws[g], dspb[r, cs], TN))
            rows.append(jnp.concatenate(cols, axis=1))
        dvn = jnp.concatenate(rows, axis=0)
        _acc_rows(dlg_o, i == 0, dvn * xhat)
        _acc_rows(dlb_o, i == 0, dvn)
        d_o[:, D:] = (_ln_bwd(dvn, xhat, rstd, lg_r[...]) * dgv).astype(d_o.dtype)

        @pl.when(i == last)
        def _():
            row = lax.broadcasted_iota(jnp.int32, (BLK, BLK), 0)
            col = lax.broadcasted_iota(jnp.int32, (BLK, BLK), 1)
            ones = jnp.ones((8, BLK), MX)
            for g in range(8):
                dws_o[g] = jnp.where(col <= row, dws_o[g], 0.0)
                a = bacc[g]
                hi = a.astype(MX)
                lo = (a - hi.astype(f32)).astype(MX)
                dbs_o[g:g + 1, :] = (_dot(ones, hi, NT) + _dot(ones, lo, NT))[0:1, :]

    full = lambda shape: pl.BlockSpec(shape, lambda i: (0,) * len(shape))
    return pl.pallas_call(
        body, name="gmlp_bwd", grid=(T // tm,),
        in_specs=[pl.BlockSpec((tm, D), lambda i: (i, 0)), *_uv_specs(), full((8, BLK, BLK)), full((8, BLK, BLK)),
                  full((1, D)), full((1, D))],
        out_specs=[pl.BlockSpec((tm, 2 * D), lambda i: (i, 0)), full((8, BLK, BLK)), full((8, BLK)), full((1, D)), full((1, D))],
        out_shape=[jax.ShapeDtypeStruct((T, 2 * D), MX), jax.ShapeDtypeStruct((8, BLK, BLK), f32),
                   jax.ShapeDtypeStruct((8, BLK), f32), jax.ShapeDtypeStruct((1, D), f32), jax.ShapeDtypeStruct((1, D), f32)],
        scratch_shapes=[pltpu.VMEM((8, BLK, BLK), f32)],
        compiler_params=_cp(("arbitrary",)),
    )(dyc, proj, proj, proj, proj, wst, bsx, lg, lb)


PART_TILES = (6, 6, 3, 3, 3, 4)
PART_START = (0, 6, 12, 15, 18, 21)
TJ = 512


def _part_specs(tm, rows_axis):
    specs = []
    for n, s in zip(PART_TILES, PART_START):
        def imap(*idx, n=n, s=s):
            i, j = idx[rows_axis], idx[1 - rows_axis]
            inside = (j >= s) & (j < s + n)
            return (jnp.where(inside, i, 0), jnp.clip(j - s, 0, n - 1))
        specs.append(pl.BlockSpec((tm, TJ), imap))
    return specs


def _with_part(j, refs, fn):
    for r, n, s in zip(refs, PART_TILES, PART_START):
        @pl.when((j >= s) & (j < s + n))
        def _():
            fn(r[...])


def dx_in(dr1, parts, w, bias):
    T = dr1.shape[0]
    tm = min(2048, T)

    def body(dr_r, p0, p1, p2, p3, p4, p5, w_r, b_r, o_r):
        j = pl.program_id(1)

        @pl.when(j == 0)
        def _():
            o_r[...] = ALPHA * dr_r[...] + b_r[...]

        def acc(tile):
            o_r[...] += _dot(tile, w_r[...], NT)
        _with_part(j, (p0, p1, p2, p3, p4, p5), acc)

    once = dict(pipeline_mode=pl.Buffered(1))
    return pl.pallas_call(
        body, name="dx_in", grid=(T // tm, NIN // TJ),
        in_specs=[pl.BlockSpec((tm, D), lambda i, j: (i, 0), **once)] + _part_specs(tm, 0)
        + [pl.BlockSpec((D, TJ), lambda i, j: (0, j)), pl.BlockSpec((1, D), lambda i, j: (0, 0))],
        out_specs=pl.BlockSpec((tm, D), lambda i, j: (i, 0), **once),
        out_shape=jax.ShapeDtypeStruct((T, D), f32),
        compiler_params=_cp(("parallel", "arbitrary"), 56),
    )(dr1, *parts, w, bias)


def dw_in(x0t, parts):
    T = x0t.shape[1]
    tk = min(2048, T)

    def body(x_r, p0, p1, p2, p3, p4, p5, o_r):
        j, t = pl.program_id(0), pl.program_id(1)

        @pl.when(t == 0)
        def _():
            o_r[...] = jnp.zeros_like(o_r)

        def acc(tile):
            o_r[...] += _dot(x_r[...], tile)
        _with_part(j, (p0, p1, p2, p3, p4, p5), acc)

    return pl.pallas_call(
        body, name="dw_in", grid=(NIN // TJ, T // tk),
        in_specs=[pl.BlockSpec((D, tk), lambda j, t: (0, t))] + _part_specs(tk, 1),
        out_specs=pl.BlockSpec((D, TJ), lambda j, t: (0, j)),
        out_shape=jax.ShapeDtypeStruct((D, NIN), f32),
        compiler_params=_cp(("parallel", "arbitrary")),
    )(x0t, *parts)


def rope_tables(positions):
    half = HD // 2
    inv_freq = ROPE_THETA ** (-jnp.arange(half, dtype=f32) / half)
    ang = positions.astype(f32)[:, None] * inv_freq
    cos, sin = jnp.cos(ang), jnp.sin(ang)
    return jnp.tile(cos, (1, LANES // half)), jnp.tile(jnp.concatenate([-sin, sin], axis=1), (1, LANES // HD))


def _flat(a):
    return a.reshape(a.shape[0] * a.shape[1], a.shape[2])


def layer_fwd(x0, W, cos_t, sin_t):
    T = x0.shape[0]
    proj = mm_in(x0, W["w_in"], W["in_bias"])
    ya, yc = mix_ac_fwd(proj, W["conv_w"], W["wst"], W["bsx"], W["gmlp_ln_g"], W["gmlp_ln_b"])
    folded, os_, lses = [], [], []
    for g, (_, d) in enumerate(GROUPS):
        qf, kf, vf = fold_rope(proj, cos_t, sin_t, g, d)
        o, lse = attn_fwd(_flat(qf), _flat(kf), _flat(vf), g, T // d // BLK)
        folded.append((qf, kf, vf))
        os_.append(o.reshape(d, T // d, AO))
        lses.append(lse.reshape(d, T // d, AO))
    yb = combine_fwd(os_, lses)
    if "late" in W:
        W = {**W, **W["late"](yb)}
    mabc, m, r1, x1 = mix_out_fwd(proj, ya, yb, yc, x0, W["p_a"], W["p_b"], W["p_c"], W["w_o"], W["ln1_g"], W["ln1_b"])
    gate, up, hh = ffn_up_fwd(x1, W["w_gate"], W["w_up"])
    r2, x2 = ffn_down_fwd(hh, W["w_down"], x1, W["ln2_g"], W["ln2_b"])
    saved = dict(x0=x0, proj=proj, ya=ya, yb=yb, yc=yc, folded=folded, os=os_, lses=lses, mabc=mabc, m=m, r1=r1,
                 x1=x1, gate=gate, up=up, hh=hh, r2=r2)
    return x2, saved, W


def layer_bwd(dx2, S, W, cos_t, sin_t, on_grads=None):
    T = dx2.shape[0]
    tk = min(2048, T)
    G = {}
    dr2, dgate, dup, G["ln2_g"], G["ln2_b"] = ffn_down_bwd(dx2, S["r2"], W["ln2_g"], W["w_down"], S["gate"], S["up"])
    blk_a = pl.BlockSpec((1, tk, FB), lambda k, t: (k, t, 0))
    row_b = pl.BlockSpec((tk, D), lambda k, t: (t, 0))
    G["w_down"] = tn_matmul("dw_down", S["hh"], dr2, blk_a, row_b, (NCHIP, FB, D),
                            pl.BlockSpec((1, FB, D), lambda k, t: (k, 0, 0)), (NCHIP, T // tk))
    for nm, dv in (("w_gate", dgate), ("w_up", dup)):
        G[nm] = tn_matmul("d" + nm, dv, S["x1"], blk_a, row_b, (NCHIP, FB, D),
                          pl.BlockSpec((1, FB, D), lambda k, t: (k, 0, 0)), (NCHIP, T // tk))
    dr1, G["ln1_g"], G["ln1_b"] = ffn_up_bwd(dr2, dgate, dup, W["w_gate"], W["w_up"], S["r1"], W["ln1_g"])
    dmabc, dgates, dya, dyb, dyc = mix_out_bwd(dr1, S["proj"], S["mabc"], W["w_o"], W["p_a"], W["p_b"], W["p_c"])
    one = (1, T // tk)
    full_o = pl.BlockSpec((D, D), lambda k, t: (0, 0))
    G["w_o"] = tn_matmul("dw_o", S["m"], dr1, row_b, row_b, (D, D), full_o, one)
    G["p_a"] = tn_matmul("dp_a", S["ya"], dmabc, row_b, pl.BlockSpec((tk, D), lambda k, t: (t, 0)), (D, D), full_o, one)
    G["p_c"] = tn_matmul("dp_c", S["yc"], dmabc, row_b, pl.BlockSpec((tk, D), lambda k, t: (t, 2)), (D, D), full_o, one)
    G["p_b"] = tn_matmul("dp_b", S["yb"], dmabc, pl.BlockSpec((tk, AO), lambda k, t: (t, 0)),
                         pl.BlockSpec((tk, D // NCHIP), lambda k, t: (t, NCHIP + k)), (NCHIP, AO, D // NCHIP),
                         pl.BlockSpec((1, AO, D // NCHIP), lambda k, t: (k, 0, 0)), (NCHIP, T // tk))
    conv_w = W["conv_w"]
    if on_grads is not None:
        conv_w = conv_w + on_grads({n: G[n] for n in BIG if n != "w_in"})
    dbch, G["conv_w"] = conv_bwd(dya, S["proj"], conv_w)
    duv, G["w_s"], G["b_s"], G["gmlp_ln_g"], G["gmlp_ln_b"] = gmlp_bwd(
        dyc, S["proj"], W["wst"], W["bsx"], W["gmlp_ln_g"], W["gmlp_ln_b"])
    ones = _head_ones()
    if on_grads is not None:
        small = {n: G[n] for n in VECS + ("b_s", "w_s", "conv_w")}
        ones = ones + on_grads(small).astype(MX)
    pre = attn_pre_bwd(dyb, S["os"], S["lses"], ones)
    dqkv = []
    for g, (_, d) in enumerate(GROUPS):
        qf, kf, vf = S["folded"][g]
        dqf, dkf, dvf = attn_bwd(_flat(qf), _flat(kf), _flat(vf), _flat(pre[g]), _flat(S["lses"][g]), _flat(pre[3 + g]),
                                 g, T // d // BLK)
        shp = (d, T // d, AO)
        dqkv.append(unfold_rope_bwd(dqf.reshape(shp), dkf.reshape(shp), dvf.reshape(shp), cos_t, sin_t, g, d))
    parts = (dgates, dbch, *dqkv, duv)
    G["w_in"] = dw_in(transpose_cast(S["x0"]), parts)
    bias = jnp.zeros((1, D), f32)
    if on_grads is not None:
        bias = bias + on_grads({"w_in": G["w_in"]})
    dx0 = dx_in(dr1, parts, W["w_in"], bias)
    started = on_grads({"dx": dx0}) if on_grads is not None else None
    return dx0, G, started


def prep_layer_weights(Wl):
    W = dict(Wl)
    tril = jnp.tril(jnp.ones((BLK, BLK), f32))
    W["wst"] = (Wl["w_s"] * tril[None]).astype(MX)
    W["bsx"] = jnp.broadcast_to(Wl["b_s"][:, :, None], (8, BLK, BLK))
    for n in ("gmlp_ln_g", "gmlp_ln_b", "ln1_g", "ln1_b", "ln2_g", "ln2_b"):
        W[n] = Wl[n].reshape(1, D)
    W["in_bias"] = jnp.zeros((1, NIN), f32) + Wl.get("after", 0.0)
    return W


def local_step(x, positions, target, layers, on_grads=None):
    cos_t, sin_t = rope_tables(positions)
    Ws, saved = [], []
    h = x
    for Wl in layers:
        h, S, W = layer_fwd(h, prep_layer_weights(Wl(h) if callable(Wl) else Wl), cos_t, sin_t)
        Ws.append(W)
        saved.append(S)
    lsum, dh = loss_grad(h, target)
    if on_grads is not None:
        on_grads(len(Ws), {"loss": lsum})
    grads = [None] * len(Ws)
    started = None
    for l in reversed(range(len(Ws))):
        W = Ws[l]
        if started is not None:
            W = dict(W, ln2_g=W["ln2_g"] + started)
        hook = functools.partial(on_grads, l) if on_grads is not None else None
        dh, grads[l], started = layer_bwd(dh, saved[l], W, cos_t, sin_t, hook)
    return lsum, dh, grads


MESH = pl.DeviceIdType.MESH
ANY = pl.BlockSpec(memory_space=pl.ANY)
BIG = ("w_in", "w_gate", "w_up", "w_down", "p_a", "p_b", "p_c", "w_o")
NBIG = len(BIG)


def _place():
    x, y, c = lax.axis_index("x"), lax.axis_index("y"), lax.axis_index("c")
    return x, y, c, 2 * x + y


def _rcopy(src, dst, send, recv, dev):
    return pltpu.make_async_remote_copy(src_ref=src, dst_ref=dst, send_sem=send, recv_sem=recv, device_id=dev,
                                        device_id_type=MESH)


def _cols(ref, k, width):
    start = k * width if isinstance(k, int) else pl.multiple_of(k * width, LANES)
    return ref.at[:, pl.ds(start, width)]


CHUNK_BYTES = 1 << 20


def _pieces(shape, itemsize, nbytes=CHUNK_BYTES):
    rows, cols = shape[-2], shape[-1]
    per = max(16, nbytes // (cols * itemsize) // 16 * 16)
    out = []
    for lead in (range(shape[0]) if len(shape) == 3 else (None,)):
        for r in range(0, rows, per):
            sl = (pl.ds(r, min(per, rows - r)), slice(None))
            out.append(sl if lead is None else (lead,) + sl)
    return out


def _start_pieces(src, dst, make, nbytes=CHUNK_BYTES):
    for idx in _pieces(src.shape, jnp.dtype(src.dtype).itemsize, nbytes):
        make(src.at[idx], dst.at[idx]).start()


def gather_halves(shards):
    n = len(shards)

    def body(*refs):
        srcs, dsts = refs[:n], refs[n:2 * n]
        send, recv, own_send, own_recv = refs[2 * n:]
        x, y, c, k = _place()
        sib = (x, y, 1 - c)
        chips = [(1 - x, y), (x, 1 - y), (1 - x, 1 - y)]

        def slot(a, layer, pos):
            if a == 0:
                return _cols(dsts[0].at[layer], pos, WIN_SHARD)
            return dsts[a].at[pos, layer]

        def ici(a, j, src, dst):
            return _rcopy(src, dst, send.at[a, j], recv.at[a, j], (*chips[j], c))

        def d2d(a, j, src, dst):
            return _rcopy(src, dst, send.at[a, 3 + j], recv.at[a, 3 + j], sib)

        def own(a, layer, src, dst):
            return _rcopy(src, dst, own_send.at[a, layer], own_recv.at[a, layer], sib)

        for a in range(n):
            for j in range(3):
                _start_pieces(srcs[a].at[c], slot(a, c, k), functools.partial(ici, a, j))
        for a in range(n):
            for layer in range(DEPTH):
                _start_pieces(srcs[a].at[layer], slot(a, layer, k), functools.partial(own, a, layer))
        for a in range(n):
            for j, (cx, cy) in enumerate(chips):
                landed = slot(a, c, 2 * cx + cy)
                ici(a, j, landed, landed).wait_recv()
                _start_pieces(landed, landed, functools.partial(d2d, a, j))
        for a in range(n):
            for j, (cx, cy) in enumerate(chips):
                passed = slot(a, 1 - c, 2 * cx + cy)
                d2d(a, j, passed, passed).wait_recv()
                landed = slot(a, c, 2 * cx + cy)
                d2d(a, j, landed, landed).wait_send()
                ici(a, j, srcs[a].at[c], slot(a, c, k)).wait_send()
            for layer in range(DEPTH):
                own(a, layer, srcs[a].at[layer], slot(a, layer, k)).wait()

    outs = [jax.ShapeDtypeStruct((2, shards[0].shape[1], NIN), shards[0].dtype)]
    outs += [jax.ShapeDtypeStruct((NCHIP,) + s.shape, s.dtype) for s in shards[1:]]
    return pl.pallas_call(
        body, name="gather_halves", in_specs=[ANY] * n, out_specs=[ANY] * n, out_shape=outs,
        scratch_shapes=[pltpu.SemaphoreType.DMA((n, 6)), pltpu.SemaphoreType.DMA((n, 6)),
                        pltpu.SemaphoreType.DMA((n, DEPTH)), pltpu.SemaphoreType.DMA((n, DEPTH))],
    )(*shards)


def _gather_slot(dst, pos):
    return _cols(dst, pos, WIN_SHARD) if len(dst.shape) == 2 else dst.at[pos]


def _gather_copy(a, j, src, dst, send, recv, dev):
    return _rcopy(src, dst, send.at[a * NCHIP + j], recv.at[a * NCHIP + j], dev)


def gather_start(tag, shards, after):
    n = len(shards)

    def body(*refs):
        srcs, dsts = refs[:n], refs[n:2 * n]
        send, recv = refs[2 * n + len(after)], refs[2 * n + len(after) + 1]
        token = refs[-1]
        x, y, c, k = _place()
        peers = [(1 - x, y, c), (x, 1 - y, c), (1 - x, 1 - y, c), (x, y, 1 - c)]
        for a in range(n):
            for j, dev in enumerate(peers):
                _start_pieces(srcs[a], _gather_slot(dsts[a], k),
                              lambda s, d, a=a, j=j, dev=dev: _gather_copy(a, j, s, d, send, recv, dev))
        token[...] = jnp.zeros_like(token)

    gathered = [lax.empty((D, NIN) if s.shape == (D, WIN_SHARD) else (NCHIP,) + s.shape, s.dtype) for s in shards]
    ops = [pltpu.with_memory_space_constraint(v, pltpu.HBM) for v in list(shards) + gathered]
    sem = pltpu.SemaphoreType.DMA((n * NCHIP,))
    res = pl.pallas_call(
        body, name=f"gather_start{tag}", in_specs=[HBM] * (2 * n) + [ANY] * len(after),
        out_specs=[SEMS, SEMS] + [HBM] * (2 * n) + [pl.BlockSpec(memory_space=pltpu.VMEM)],
        out_shape=[sem, sem] + [pltpu.HBM(v.shape, v.dtype) for v in ops] + [jax.ShapeDtypeStruct((8, LANES), f32)],
        input_output_aliases={i: 2 + i for i in range(2 * n)},
        compiler_params=pltpu.CompilerParams(has_side_effects=EFFECT),
    )(*ops, *after)
    return res[0], res[1], res[2:2 + n], res[2 + n:2 + 2 * n], res[-1]


def gather_wait(tag, send, recv, shards, gathered, after):
    n = len(shards)

    def body(*refs):
        srcs, dsts = refs[:n], refs[n:2 * n]
        send_r, recv_r = refs[2 * n], refs[2 * n + 1]
        x, y, c, k = _place()
        peers = [(1 - x, y, c), (x, 1 - y, c), (1 - x, 1 - y, c), (x, y, 1 - c)]
        for a in range(n):
            for j, dev in enumerate(peers):
                _gather_copy(a, j, srcs[a], _gather_slot(dsts[a], k), send_r, recv_r, dev).wait_send()
                pos = 2 * dev[0] + dev[1]
                _gather_copy(a, j, srcs[a], _gather_slot(dsts[a], pos), send_r, recv_r, dev).wait_recv()

    ops = list(shards) + list(gathered)
    res = pl.pallas_call(
        body, name=f"gather_wait{tag}", in_specs=[HBM] * (2 * n) + [SEMS, SEMS] + [ANY] * len(after),
        out_specs=[HBM] * (2 * n), out_shape=[pltpu.HBM(v.shape, v.dtype) for v in ops],
        input_output_aliases={i: i for i in range(2 * n)},
        compiler_params=pltpu.CompilerParams(has_side_effects=EFFECT),
    )(*ops, send, recv, *after)
    return res[n:]


def _half(ref, h):
    rows = ref.shape[-2] // 2
    start = pl.multiple_of(h * rows, 16)
    if len(ref.shape) == 2:
        return ref.at[pl.ds(start, rows), :]
    return ref.at[:, pl.ds(start, rows), :]


HBM = pl.BlockSpec(memory_space=pltpu.HBM)
SEMS = pl.BlockSpec(memory_space=pltpu.SEMAPHORE)
EFFECT = pltpu.SideEffectType.DATAFLOW_SIDE_EFFECTING


def rs_pair_start(tag, grads):
    n = len(grads)

    def body(*refs):
        g, theirs = refs[:n], refs[n:2 * n]
        send, recv = refs[2 * n], refs[2 * n + 1]
        x, y, c, _ = _place()
        for a in range(n):
            _start_pieces(_half(g[a], 1 - c), theirs[a],
                          lambda s, d, a=a: _rcopy(s, d, send.at[a], recv.at[a], (x, y, 1 - c)))
        refs[-1][...] = jnp.zeros_like(refs[-1])

    lands = [lax.empty(g.shape[:-2] + (g.shape[-2] // 2, g.shape[-1]), g.dtype) for g in grads]
    ops = [pltpu.with_memory_space_constraint(v, pltpu.HBM) for v in list(grads) + lands]
    sem = pltpu.SemaphoreType.DMA((n,))
    res = pl.pallas_call(
        body, name=f"rs_pair_start{tag}", in_specs=[HBM] * (2 * n),
        out_specs=[SEMS, SEMS] + [HBM] * (2 * n) + [pl.BlockSpec(memory_space=pltpu.VMEM)],
        out_shape=[sem, sem] + [pltpu.HBM(v.shape, v.dtype) for v in ops] + [jax.ShapeDtypeStruct((8, LANES), f32)],
        input_output_aliases={i: 2 + i for i in range(2 * n)},
        compiler_params=pltpu.CompilerParams(has_side_effects=EFFECT),
    )(*ops)
    return res[0], res[1], res[2:2 + n], res[2 + n:2 + 2 * n], res[-1]


def rs_pair_wait(tag, send, recv, grads, theirs, after):
    n = len(grads)

    def body(*refs):
        g, land = refs[:n], refs[n:2 * n]
        send_r, recv_r = refs[2 * n], refs[2 * n + 1]
        x, y, c, _ = _place()
        for a in range(n):
            cp = _rcopy(_half(g[a], 1 - c), land[a], send_r.at[a], recv_r.at[a], (x, y, 1 - c))
            cp.wait_send()
            cp.wait_recv()

    ops = list(grads) + list(theirs)
    res = pl.pallas_call(
        body, name=f"rs_pair_wait{tag}", in_specs=[HBM] * (2 * n) + [SEMS, SEMS] + [ANY] * len(after),
        out_specs=[HBM] * (2 * n), out_shape=[pltpu.HBM(v.shape, v.dtype) for v in ops],
        input_output_aliases={i: i for i in range(2 * n)},
        compiler_params=pltpu.CompilerParams(has_side_effects=EFFECT),
    )(*ops, send, recv, *after)
    return res[:n], res[n:]


def _chip_piece(ref, k):
    return _cols(ref, k, WIN_SHARD) if len(ref.shape) == 2 else ref.at[k]


def _chip_copy(a, k, src, dst, send, recv, me, c):
    return _rcopy(src, dst, send.at[a * NCHIP + k], recv.at[a * NCHIP + me], (k // 2, k % 2, c))


def rs_chips_start(tag, sums):
    n = len(sums)

    def pshape(s):
        return (NCHIP, s[0], WIN_SHARD) if len(s) == 2 else s

    def body(*refs):
        s, land = refs[:n], refs[n:2 * n]
        send, recv = refs[2 * n], refs[2 * n + 1]
        token = refs[-1]
        x, y, c, me = _place()
        for k in range(NCHIP):
            @pl.when(me != k)
            def _():
                for a in range(n):
                    _start_pieces(_chip_piece(s[a], k), land[a].at[me],
                                  lambda src, dst, a=a: _chip_copy(a, k, src, dst, send, recv, me, c))
        token[...] = jnp.zeros_like(token)

    lands = [lax.empty(pshape(v.shape), v.dtype) for v in sums]
    ops = [pltpu.with_memory_space_constraint(v, pltpu.HBM) for v in list(sums) + lands]
    sem = pltpu.SemaphoreType.DMA((n * NCHIP,))
    res = pl.pallas_call(
        body, name=f"rs_chips_start{tag}", in_specs=[HBM] * (2 * n),
        out_specs=[SEMS, SEMS] + [HBM] * (2 * n) + [pl.BlockSpec(memory_space=pltpu.VMEM)],
        out_shape=[sem, sem] + [pltpu.HBM(v.shape, v.dtype) for v in ops] + [jax.ShapeDtypeStruct((8, LANES), f32)],
        input_output_aliases={i: 2 + i for i in range(2 * n)},
        compiler_params=pltpu.CompilerParams(has_side_effects=EFFECT),
    )(*ops)
    return res[0], res[1], res[2:2 + n], res[2 + n:2 + 2 * n], res[-1]


def rs_chips_wait(tag, send, recv, sums, lands, after):
    n = len(sums)

    def body(*refs):
        s, land = refs[:n], refs[n:2 * n]
        send_r, recv_r = refs[2 * n], refs[2 * n + 1]
        x, y, c, me = _place()
        for k in range(NCHIP):
            @pl.when(me != k)
            def _():
                for a in range(n):
                    piece = _chip_piece(s[a], k)
                    _chip_copy(a, k, piece, land[a].at[me], send_r, recv_r, me, c).wait_send()
                    _rcopy(piece, land[a].at[k], send_r.at[a * NCHIP + k], recv_r.at[a * NCHIP + k],
                           (k // 2, k % 2, c)).wait_recv()

    ops = list(sums) + list(lands)
    res = pl.pallas_call(
        body, name=f"rs_chips_wait{tag}", in_specs=[HBM] * (2 * n) + [SEMS, SEMS] + [ANY] * len(after),
        out_specs=[HBM] * (2 * n), out_shape=[pltpu.HBM(v.shape, v.dtype) for v in ops],
        input_output_aliases={i: i for i in range(2 * n)},
        compiler_params=pltpu.CompilerParams(has_side_effects=EFFECT),
    )(*ops, send, recv, *after)
    return res[:n], res[n:]


def rs_join(tag, halves):
    n = len(halves)

    def body(*refs):
        h, other = refs[:n], refs[n:2 * n]
        send, recv = refs[2 * n:]
        x, y, c, _ = _place()

        def give(a, s, d):
            return _rcopy(s, d, send.at[a], recv.at[a], (x, y, 1 - c))

        for a in range(n):
            _start_pieces(h[a], other[a], functools.partial(give, a))
        for a in range(n):
            give(a, h[a], other[a]).wait()

    outs = [jax.ShapeDtypeStruct(v.shape, v.dtype) for v in halves]
    return pl.pallas_call(
        body, name=f"rs_join{tag}", in_specs=[ANY] * n, out_specs=[ANY] * n, out_shape=outs,
        scratch_shapes=[pltpu.SemaphoreType.DMA((n,))] * 2,
    )(*halves)


def _row_tile(rows, cols, itemsize=4, target=2 << 20):
    best = 8
    for t in range(8, rows + 1, 8):
        if rows % t == 0 and t * cols * itemsize <= target:
            best = t
    return best


GRAD_WIRE = jnp.bfloat16


def add_n(name, terms, out_dtype=f32):
    shape = terms[0].shape
    cols = shape[-1]
    rows = math.prod(shape[:-1])
    tr = _row_tile(rows, cols)

    def body(*refs):
        acc = refs[0][...]
        for r in refs[1:-1]:
            acc = acc + r[...]
        refs[-1][...] = acc.astype(out_dtype)

    tile = pl.BlockSpec((tr, cols), lambda i: (i, 0))
    out = pl.pallas_call(
        body, name=name, grid=(rows // tr,), in_specs=[tile] * len(terms), out_specs=tile,
        out_shape=jax.ShapeDtypeStruct((rows, cols), out_dtype), compiler_params=_cp(("parallel",)),
    )(*[t.reshape(rows, cols) for t in terms])
    return out.reshape(shape)


def add_chips(name, land, own):
    _, rows, cols = land.shape
    tr = _row_tile(rows, cols, target=1 << 20)

    def body(land_r, own_r, o_r):
        me = 2 * lax.axis_index("x") + lax.axis_index("y")
        for k in range(NCHIP):
            @pl.when(me == k)
            def _():
                acc = None
                for j in range(NCHIP):
                    t = (own_r[...] if j == k else land_r[j]).astype(f32)
                    acc = t if acc is None else acc + t
                o_r[...] = acc

    tile = pl.BlockSpec((tr, cols), lambda i: (i, 0))
    return pl.pallas_call(
        body, name=name, grid=(rows // tr,), in_specs=[pl.BlockSpec((NCHIP, tr, cols), lambda i: (0, i, 0)), tile],
        out_specs=tile, out_shape=jax.ShapeDtypeStruct((rows, cols), f32), compiler_params=_cp(("parallel",)),
    )(land, own)


def reduce_scatter_pair(tag, G):
    names = tuple(G)
    grads = [G[n] if G[n].ndim == 3 or n == "w_in" else G[n].reshape(NCHIP, D // NCHIP, D) for n in names]
    send, recv, grads, theirs, token = rs_pair_start(tag, grads)
    return (tag, names, send, recv, grads, theirs), token[0, 0]


def reduce_scatter_chips(state, after):
    c = lax.axis_index("c")
    tag, names, send, recv, grads, theirs = state
    grads, theirs = rs_pair_wait(tag, send, recv, grads, theirs, after)
    sums = []
    for n, g, t in zip(names, grads, theirs):
        rows = g.shape[-2] // 2
        mine = lax.dynamic_slice_in_dim(g, c * rows, rows, axis=g.ndim - 2)
        sums.append(add_n(f"rs_add_pair{tag}_{n}", [mine, t], GRAD_WIRE))
    send, recv, sums, lands, token = rs_chips_start(tag, sums)
    return (tag, names, send, recv, sums, lands), token[0, 0]


def reduce_scatter_finish(state, after):
    me = 2 * lax.axis_index("x") + lax.axis_index("y")
    tag, names, send, recv, sums, lands = state
    sums, landed = rs_chips_wait(tag, send, recv, sums, lands, after)
    halves = []
    for n, s, v in zip(names, sums, landed):
        own = lax.dynamic_slice_in_dim(s, me * WIN_SHARD, WIN_SHARD, axis=1) if s.ndim == 2 else \
            lax.dynamic_index_in_dim(s, me, 0, keepdims=False)
        halves.append(add_chips(f"rs_add_chips{tag}_{n}", v, own))
    return dict(zip(names, zip(halves, rs_join(tag, halves))))


NDEV = 8


def _small_copy(r, src, dst, send, recv, x, y, c):
    return _rcopy(src, dst, send.at[r - 1], recv.at[r - 1], (x ^ (r >> 2), y ^ ((r >> 1) & 1), c ^ (r & 1)))


def small_start(pack):
    def body(p, land, send, recv, p_thru, land_thru, token):
        x, y, c, _ = _place()
        me = 4 * x + 2 * y + c
        for r in range(1, NDEV):
            _start_pieces(p, land.at[me], lambda s, d, r=r: _small_copy(r, s, d, send, recv, x, y, c), 128 << 10)
        token[...] = jnp.zeros_like(token)

    ops = [pltpu.with_memory_space_constraint(v, pltpu.HBM) for v in (pack, lax.empty((NDEV,) + pack.shape, f32))]
    sem = pltpu.SemaphoreType.DMA((NDEV - 1,))
    return pl.pallas_call(
        body, name="small_start", in_specs=[HBM, HBM],
        out_specs=[SEMS, SEMS, HBM, HBM, pl.BlockSpec(memory_space=pltpu.VMEM)],
        out_shape=[sem, sem] + [pltpu.HBM(v.shape, v.dtype) for v in ops] + [jax.ShapeDtypeStruct((8, LANES), f32)],
        input_output_aliases={0: 2, 1: 3}, compiler_params=pltpu.CompilerParams(has_side_effects=EFFECT),
    )(*ops)


def small_wait(send, recv, pack, land, after):
    def body(p, land_r, send_r, recv_r, *rest):
        x, y, c, _ = _place()
        me = 4 * x + 2 * y + c
        for r in range(1, NDEV):
            _small_copy(r, p, land_r.at[me], send_r, recv_r, x, y, c).wait_send()
            src = 4 * (x ^ (r >> 2)) + 2 * (y ^ ((r >> 1) & 1)) + (c ^ (r & 1))
            _small_copy(r, p, land_r.at[src], send_r, recv_r, x, y, c).wait_recv()

    return pl.pallas_call(
        body, name="small_wait", in_specs=[HBM, HBM, SEMS, SEMS] + [ANY] * len(after), out_specs=[HBM, HBM],
        out_shape=[pltpu.HBM(pack.shape, f32), pltpu.HBM(land.shape, f32)], input_output_aliases={0: 0, 1: 1},
        compiler_params=pltpu.CompilerParams(has_side_effects=EFFECT),
    )(pack, land, send, recv, *after)


def small_sum(land, pack):
    def body(land_r, p_r, o_r):
        me = 4 * lax.axis_index("x") + 2 * lax.axis_index("y") + lax.axis_index("c")
        for k in range(NDEV):
            @pl.when(me == k)
            def _():
                acc = None
                for d in range(NDEV):
                    t = p_r[...] if d == k else land_r[d]
                    acc = t if acc is None else acc + t
                o_r[...] = acc

    vm = pl.BlockSpec(memory_space=pltpu.VMEM)
    return pl.pallas_call(
        body, name="small_sum", in_specs=[vm, vm], out_specs=vm, out_shape=jax.ShapeDtypeStruct(pack.shape, f32),
        compiler_params=pltpu.CompilerParams(vmem_limit_bytes=40 << 20),
    )(land, pack)


def _adamw_math(w, g, m, v):
    m = ADAM_B1 * m + (1.0 - ADAM_B1) * g
    v = ADAM_B2 * v + (1.0 - ADAM_B2) * (g * g)
    m_hat = m / (1.0 - ADAM_B1 ** ADAM_STEP)
    v_hat = v / (1.0 - ADAM_B2 ** ADAM_STEP)
    return -ADAM_LR * (m_hat / (jnp.sqrt(v_hat) + ADAM_EPS) + ADAM_WD * w), m, v


def adamw_big(name, halves, w, m, v):
    _, R, C = w.shape
    tr = _row_tile(R // 2, C, target=1 << 20)
    nt = R // 2 // tr

    def body(a0, b0, a1, b1, w_r, m_r, v_r, g_o, d_o, m_o, v_o):
        mine = pl.program_id(1) == lax.axis_index("c")
        g = jnp.where(pl.program_id(0) == 0, jnp.where(mine, a0[...], b0[...]), jnp.where(mine, a1[...], b1[...]))
        g_o[...] = g
        d_o[...], m_o[...], v_o[...] = _adamw_math(w_r[...], g, m_r[...], v_r[...])

    stk = pl.BlockSpec((None, tr, C), lambda l, h, i: (l, h * nt + i, 0))
    lay0 = pl.BlockSpec((tr, C), lambda l, h, i: (jnp.where(l == 0, i, nt - 1), 0))
    lay1 = pl.BlockSpec((tr, C), lambda l, h, i: (jnp.where(l == 0, 0, i), 0))
    return pl.pallas_call(
        body, name=name, grid=(DEPTH, 2, nt),
        in_specs=[lay0, lay0, lay1, lay1, stk, stk, stk],
        out_specs=[stk] * 4, out_shape=[jax.ShapeDtypeStruct(w.shape, f32)] * 4,
        compiler_params=_cp(("arbitrary", "arbitrary", "arbitrary")),
    )(*halves[0], *halves[1], w, m, v)


def adamw_small(name, g, w, m, v):
    def body(g_r, w_r, m_r, v_r, d_o, m_o, v_o):
        d_o[...], m_o[...], v_o[...] = _adamw_math(w_r[...], g_r[...], m_r[...], v_r[...])

    return pl.pallas_call(body, name=name, out_shape=[jax.ShapeDtypeStruct(w.shape, f32)] * 3)(g, w, m, v)


WEIGHTS = ("w_in", "conv_w", "gmlp_ln_g", "gmlp_ln_b", "w_s", "b_s", "p_a", "p_b", "p_c", "w_o", "ln1_g", "ln1_b",
           "w_gate", "w_up", "w_down", "ln2_g", "ln2_b")
VECS = ("ln1_g", "ln1_b", "ln2_g", "ln2_b", "gmlp_ln_g", "gmlp_ln_b")
ROWS_VEC, ROWS_BS, ROWS_WS, ROWS_CONV = D // LANES, 8, 8 * BLK, 3 * D // LANES
ROWS_LAYER = len(VECS) * ROWS_VEC + ROWS_BS + ROWS_WS + ROWS_CONV


def _pack_small(per_layer, tail):
    parts = []
    for P in per_layer:
        parts += [P[n].reshape(ROWS_VEC, LANES) for n in VECS]
        parts += [P["b_s"].reshape(ROWS_BS, LANES), P["w_s"].reshape(ROWS_WS, LANES), P["conv_w"].reshape(ROWS_CONV, LANES)]
    return jnp.concatenate(parts + [tail], axis=0)


def _unpack_small(pack):
    out = []
    for l in range(DEPTH):
        r = l * ROWS_LAYER
        P = {}
        for n in VECS:
            P[n] = pack[r:r + ROWS_VEC].reshape(D)
            r += ROWS_VEC
        P["b_s"] = pack[r:r + ROWS_BS].reshape(8, BLK)
        r += ROWS_BS
        P["w_s"] = pack[r:r + ROWS_WS].reshape(8, BLK, BLK)
        r += ROWS_WS
        P["conv_w"] = pack[r:r + ROWS_CONV].reshape(3, D)
        out.append(P)
    return out, pack[DEPTH * ROWS_LAYER:]


def kernel(x, positions, w_in, conv_w, gmlp_ln_g, gmlp_ln_b, w_s, b_s, p_a, p_b, p_c, w_o, ln1_g, ln1_b, w_gate, w_up, w_down, ln2_g, ln2_b, loss_target, m_w_in, m_conv_w, m_gmlp_ln_g, m_gmlp_ln_b, m_w_s, m_b_s, m_p_a, m_p_b, m_p_c, m_w_o, m_ln1_g, m_ln1_b, m_w_gate, m_w_up, m_w_down, m_ln2_g, m_ln2_b, v_w_in, v_conv_w, v_gmlp_ln_g, v_gmlp_ln_b, v_w_s, v_b_s, v_p_a, v_p_b, v_p_c, v_w_o, v_ln1_g, v_ln1_b, v_w_gate, v_w_up, v_w_down, v_ln2_g, v_ln2_b):
    Wt = dict(w_in=w_in, conv_w=conv_w, gmlp_ln_g=gmlp_ln_g, gmlp_ln_b=gmlp_ln_b, w_s=w_s, b_s=b_s, p_a=p_a, p_b=p_b,
              p_c=p_c, w_o=w_o, ln1_g=ln1_g, ln1_b=ln1_b, w_gate=w_gate, w_up=w_up, w_down=w_down, ln2_g=ln2_g, ln2_b=ln2_b)
    Mt = dict(w_in=m_w_in, conv_w=m_conv_w, gmlp_ln_g=m_gmlp_ln_g, gmlp_ln_b=m_gmlp_ln_b, w_s=m_w_s, b_s=m_b_s, p_a=m_p_a,
              p_b=m_p_b, p_c=m_p_c, w_o=m_w_o, ln1_g=m_ln1_g, ln1_b=m_ln1_b, w_gate=m_w_gate, w_up=m_w_up,
              w_down=m_w_down, ln2_g=m_ln2_g, ln2_b=m_ln2_b)
    Vt = dict(w_in=v_w_in, conv_w=v_conv_w, gmlp_ln_g=v_gmlp_ln_g, gmlp_ln_b=v_gmlp_ln_b, w_s=v_w_s, b_s=v_b_s, p_a=v_p_a,
              p_b=v_p_b, p_c=v_p_c, w_o=v_w_o, ln1_g=v_ln1_g, ln1_b=v_ln1_b, w_gate=v_w_gate, w_up=v_w_up,
              w_down=v_w_down, ln2_g=v_ln2_g, ln2_b=v_ln2_b)
    chip = 2 * lax.axis_index("x") + lax.axis_index("y")
    cw = D // NCHIP

    def gathered_weights(names, arrays):
        Wl = dict(zip(names, arrays))
        for n in ("p_a", "p_c", "w_o"):
            Wl[n] = Wl[n].reshape(D, D)
        return Wl

    def small_weights(l, conv_all):
        Wl = {n: Wt[n][l] for n in VECS + ("w_s", "b_s")}
        Wl["conv_w"] = conv_all[:, l].transpose(1, 0, 2).reshape(3, D)
        return Wl

    w_in0, conv_all = gather_halves([Wt["w_in"][0].astype(MX).reshape(2, D // 2, WIN_SHARD), conv_w])
    rest = BIG[1:]
    *late0, coming0 = gather_start("0", [Wt[n][0].astype(MX) for n in rest], [conv_all])
    *late1, coming1 = gather_start("1", [Wt[n][1].astype(MX) for n in BIG], [conv_all, coming0])
    W0 = dict(small_weights(0, conv_all), w_in=w_in0.reshape(D, NIN), after=coming1[0, 0],
              late=lambda y: gathered_weights(rest, gather_wait("0", *late0, [y])))

    def W1(h):
        return dict(small_weights(1, conv_all), **gathered_weights(BIG, gather_wait("1", *late1, [h])))

    layers = [W0, W1]

    rs_state, rs_started, held = {}, {}, {}

    def start_exchange(l, g):
        if "loss" in g:
            held[l] = g
            return None
        if "conv_w" in g:
            held[l] = g
            rs_state[(l, False)], started = reduce_scatter_chips(rs_state[(l, False)], [g["w_s"], g["conv_w"]])
            if l == 0:
                pack = _pack_small([held[j] for j in range(DEPTH)], held[DEPTH]["loss"])
                *held["small"], token = small_start(pack)
                started = started + token[0, 0]
            return started
        if "dx" in g:
            rs_state[(l, True)], rs_started[(l, True)] = reduce_scatter_chips(rs_state[(l, True)], [g["dx"]])
            return rs_started[(l, True)]
        key = (l, "w_in" in g)
        rs_state[key], started = reduce_scatter_pair(f"{l}{'b' if key[1] else 'a'}", g)
        return started

    _, grad_x, _ = local_step(x[0], positions[0], loss_target[0], layers, start_exchange)

    last = jnp.zeros((8, LANES), f32) + rs_started[(0, True)]
    behind = [grad_x, last]
    red = [dict() for _ in range(DEPTH)]
    for key in ((1, False), (1, True), (0, False)):
        red[key[0]].update(reduce_scatter_finish(rs_state[key], behind))
    small, tail = _unpack_small(small_sum(*reversed(small_wait(*held["small"], behind))))
    loss = tail[0, 0]

    G, DW, NM, NV = {}, {}, {}, {}
    zc = jnp.zeros((3, D), f32)
    wp = _pack_small([{**{n: Wt[n][l] for n in VECS + ("b_s", "w_s")}, "conv_w": zc} for l in range(DEPTH)], jnp.zeros((8, LANES), f32))
    mp = _pack_small([{**{n: Mt[n][l] for n in VECS + ("b_s", "w_s")}, "conv_w": zc} for l in range(DEPTH)], jnp.zeros((8, LANES), f32))
    vp = _pack_small([{**{n: Vt[n][l] for n in VECS + ("b_s", "w_s")}, "conv_w": zc} for l in range(DEPTH)], jnp.ones((8, LANES), f32))
    gp = _pack_small(small, jnp.zeros((8, LANES), f32))
    outs = [_unpack_small(a)[0] for a in adamw_small("adamw_small", gp, wp, mp, vp)]
    for n in VECS + ("b_s", "w_s"):
        G[n] = jnp.stack([small[l][n] for l in range(DEPTH)])
        DW[n], NM[n], NV[n] = (jnp.stack([o[l][n] for l in range(DEPTH)]) for o in outs)
    gconv = jnp.stack([lax.dynamic_slice(small[l]["conv_w"], (0, chip * cw), (3, cw)) for l in range(DEPTH)])
    G["conv_w"] = gconv
    flat = lambda a: a.reshape(DEPTH * 3, cw)
    d, m2, v2 = adamw_small("adamw_conv", flat(gconv), flat(conv_w), flat(m_conv_w), flat(v_conv_w))
    DW["conv_w"], NM["conv_w"], NV["conv_w"] = (a.reshape(DEPTH, 3, cw) for a in (d, m2, v2))

    updated = {}
    for n in BIG[1:]:
        tr = (lambda a: jnp.swapaxes(a, 1, 2)) if n in ("w_gate", "w_up") else (lambda a: a)
        updated[n] = adamw_big("adamw_" + n, (red[0][n], red[1][n]), tr(Wt[n]), tr(Mt[n]), tr(Vt[n]))
        G[n], DW[n], NM[n], NV[n] = map(tr, updated[n])
    done = [d, DW["ln2_b"], red[1]["w_in"][1]] + [updated[n][1] for n in BIG[1:]]
    red[0].update(reduce_scatter_finish(rs_state[(0, True)], done))
    G["w_in"], DW["w_in"], NM["w_in"], NV["w_in"] = adamw_big(
        "adamw_w_in", (red[0]["w_in"], red[1]["w_in"]), Wt["w_in"], Mt["w_in"], Vt["w_in"])

    return (loss, grad_x[None], *[G[n] for n in WEIGHTS], *[DW[n] for n in WEIGHTS], *[NM[n] for n in WEIGHTS],
            *[NV[n] for n in WEIGHTS])
```

```python
import functools
import math

import jax
import jax.numpy as jnp
from jax import lax
from jax.experimental import pallas as pl
from jax.experimental.pallas import tpu as pltpu

D = 1024
NIN = 12800
DFF = 2816
NCHIP = 4
FB = DFF // NCHIP
WIN_SHARD = NIN // NCHIP
DEPTH = 2
GROUPS = ((128, 1), (512, 4), (2048, 16))
HD = 64
BLK = 128
AO = 512
ALPHA = (2 * DEPTH) ** 0.25
EPS = 1e-5
ROPE_THETA = 10000.0
LANES = 128
NEG = -1e30

C_GATES, C_BCH, C_QKV, C_UV = 0, 3 * D, 6 * D, 6 * D + 9 * AO

MX = jnp.bfloat16
ACT = jnp.bfloat16

ADAM_LR, ADAM_B1, ADAM_B2, ADAM_EPS, ADAM_WD, ADAM_STEP = 0.001, 0.9, 0.999, 1e-08, 0.01, 10

f32 = jnp.float32
NT = (((1,), (1,)), ((), ()))
TN = (((0,), (0,)), ((), ()))


def _cp(sem, vmem_mb=48):
    return pltpu.CompilerParams(dimension_semantics=sem, vmem_limit_bytes=vmem_mb << 20)


def _dot(a, b, dims=None):
    if dims is None:
        return jnp.dot(a, b, preferred_element_type=f32)
    return lax.dot_general(a, b, dims, preferred_element_type=f32)


def _ln_stats(r):
    mu = jnp.mean(r, axis=-1, keepdims=True)
    xc = r - mu
    var = jnp.mean(xc * xc, axis=-1, keepdims=True)
    rstd = lax.rsqrt(var + EPS)
    return xc * rstd, rstd


def _ln_bwd(dy, xhat, rstd, g):
    dxh = dy * g
    return rstd * (dxh - jnp.mean(dxh, axis=-1, keepdims=True) - xhat * jnp.mean(dxh * xhat, axis=-1, keepdims=True))


def _gelu(x):
    return 0.5 * x * (1.0 + lax.erf(x * (1.0 / math.sqrt(2.0))))


def _gelu_and_grad(x):
    cdf = 0.5 * (1.0 + lax.erf(x * (1.0 / math.sqrt(2.0))))
    return x * cdf, cdf + x * jnp.exp(-0.5 * x * x) * (1.0 / math.sqrt(2.0 * math.pi))


def _sigmoid(x):
    return 0.5 * jnp.tanh(0.5 * x) + 0.5


def _acc_rows(o_ref, first, val):
    @pl.when(first)
    def _():
        o_ref[...] = jnp.zeros_like(o_ref)
    o_ref[...] += jnp.sum(val, axis=0, keepdims=True)


def mm_in(x, w, bias):
    T = x.shape[0]
    tm, tn = min(2048, T), 1280

    def body(x_ref, w_ref, b_ref, o_ref, xb):
        @pl.when(pl.program_id(1) == 0)
        def _():
            xb[...] = x_ref[...].astype(MX)
        o_ref[...] = (_dot(xb[...], w_ref[...]) + b_ref[...]).astype(o_ref.dtype)

    return pl.pallas_call(
        body, name="mm_in", grid=(T // tm, NIN // tn),
        in_specs=[pl.BlockSpec((tm, D), lambda i, j: (i, 0), pipeline_mode=pl.Buffered(1)),
                  pl.BlockSpec((D, tn), lambda i, j: (0, j)), pl.BlockSpec((1, tn), lambda i, j: (0, j))],
        out_specs=pl.BlockSpec((tm, tn), lambda i, j: (i, j)),
        out_shape=jax.ShapeDtypeStruct((T, NIN), ACT),
        scratch_shapes=[pltpu.VMEM((tm, D), MX)],
        compiler_params=_cp(("parallel", "arbitrary")),
    )(x, w, bias)


HALO = 16
TM_AC = 256


def _uv_specs():
    return [pl.BlockSpec((TM_AC, 512), functools.partial(lambda i, j: (i, j), j=C_UV // 512 + j)) for j in range(4)]


def _gmlp_fwd(up, vp, ws_ref, bs_ref, lg, lb, u=None, gv=None):
    u = _gelu(up) if u is None else u
    xhat, rstd = _ln_stats(_gelu(vp) if gv is None else gv)
    vn = xhat * lg + lb
    vnb = vn.astype(MX)
    rows = []
    for c in range(up.shape[0] // BLK):
        r = slice(c * BLK, (c + 1) * BLK)
        rows.append(jnp.concatenate(
            [_dot(ws_ref[g], vnb[r, g * BLK:(g + 1) * BLK]) + bs_ref[g] for g in range(8)], axis=1))
    return u, vn, xhat, rstd, jnp.concatenate(rows, axis=0)


def mix_ac_fwd(proj, conv_w, wst, bsx, lg, lb):
    T = proj.shape[0]
    tm = TM_AC

    def body(bch, halo, u0, u1, v0, v1, cw, ws, bs, lg_ref, lb_ref, ya, yc, zs):
        i = pl.program_id(0)
        pb = bch[...].astype(f32)
        z = pb[:, D:2 * D] * pb[:, 2 * D:]
        hz = halo[:, :D].astype(f32) * halo[:, D:].astype(f32)
        zs[0:HALO, :] = jnp.where(i > 0, hz, 0.0)
        zs[HALO:HALO + tm, :] = z
        cv = cw[0:1, :] * zs[HALO - 2:HALO - 2 + tm, :] + cw[1:2, :] * zs[HALO - 1:HALO - 1 + tm, :] + cw[2:3, :] * z
        ya[...] = (pb[:, :D] * cv).astype(ya.dtype)
        up = jnp.concatenate([u0[...], u1[...]], axis=1).astype(f32)
        vp = jnp.concatenate([v0[...], v1[...]], axis=1).astype(f32)
        u, _, _, _, sp = _gmlp_fwd(up, vp, ws, bs, lg_ref[...], lb_ref[...])
        yc[...] = (u * sp).astype(yc.dtype)

    full = lambda shape: pl.BlockSpec(shape, lambda i: (0,) * len(shape))
    return pl.pallas_call(
        body, name="mix_ac_fwd", grid=(T // tm,),
        in_specs=[pl.BlockSpec((tm, 3 * D), lambda i: (i, 1)),
                  pl.BlockSpec((HALO, 2 * D), lambda i: (jnp.maximum(i * (tm // HALO) - 1, 0), 2)),
                  *_uv_specs(), full((3, D)), full((8, BLK, BLK)), full((8, BLK, BLK)), full((1, D)), full((1, D))],
        out_specs=[pl.BlockSpec((tm, D), lambda i: (i, 0))] * 2,
        out_shape=[jax.ShapeDtypeStruct((T, D), MX)] * 2,
        scratch_shapes=[pltpu.VMEM((HALO + tm, D), f32)],
        compiler_params=_cp(("parallel",)),
    )(proj, proj, proj, proj, proj, proj, conv_w, wst, bsx, lg, lb)


def _swap_halves(x):
    lane = lax.broadcasted_iota(jnp.int32, x.shape, 1)
    return jnp.where((lane % HD) < HD // 2, pltpu.roll(x, x.shape[1] - HD // 2, 1), pltpu.roll(x, HD // 2, 1))


def _tile4(t):
    return jnp.concatenate([t] * (AO // LANES), axis=1)


TM_FOLD = 512


def _fold_out(nat, x, out_ref, d):
    if d == 1:
        out_ref[0] = x.astype(out_ref.dtype)
        return
    rows = x.shape[0] // d
    for j in range(AO // LANES):
        nat[j] = x[:, j * LANES:(j + 1) * LANES]
    for r in range(d):
        out_ref[r] = jnp.concatenate(
            [nat.at[j][pl.ds(r, rows, stride=d), :] for j in range(AO // LANES)], axis=1).astype(out_ref.dtype)


def _unfold_in(nat, in_ref, d):
    if d == 1:
        return in_ref[0].astype(f32)
    rows = in_ref.shape[1]
    for r in range(d):
        v = in_ref[r].astype(f32)
        for j in range(AO // LANES):
            nat.at[j][pl.ds(r, rows, stride=d), :] = v[:, j * LANES:(j + 1) * LANES]
    return jnp.concatenate([nat[j] for j in range(AO // LANES)], axis=1)


def fold_rope(proj, cos_t, sin_t, g, d):
    T = proj.shape[0]
    tm = TM_FOLD
    rows = tm // d

    def body(x_ref, c_ref, s_ref, q_o, k_o, v_o, nat):
        cos, sin = _tile4(c_ref[...]), _tile4(s_ref[...])
        for part, out, scale in ((0, q_o, HD ** -0.5), (1, k_o, 1.0), (2, v_o, None)):
            x = x_ref[:, part * AO:(part + 1) * AO].astype(f32)
            if scale is not None:
                x = (x * cos + _swap_halves(x) * sin) * scale
            _fold_out(nat, x, out, d)

    fold_spec = pl.BlockSpec((d, rows, AO), lambda i: (0, i, 0))
    return pl.pallas_call(
        body, name=f"fold_rope{g}", grid=(T // tm,),
        in_specs=[pl.BlockSpec((tm, 3 * AO), lambda i: (i, C_QKV // (3 * AO) + g)),
                  pl.BlockSpec((tm, LANES), lambda i: (i, 0)), pl.BlockSpec((tm, LANES), lambda i: (i, 0))],
        out_specs=[fold_spec] * 3,
        out_shape=[jax.ShapeDtypeStruct((d, T // d, AO), MX)] * 3,
        scratch_shapes=[pltpu.VMEM((AO // LANES, tm, LANES), f32)],
        compiler_params=_cp(("parallel",)),
    )(proj, cos_t, sin_t)


def _stack_heads(x):
    lane = lax.broadcasted_iota(jnp.int32, x.shape, 1)
    z = jnp.zeros_like(x)
    return jnp.concatenate([jnp.where(lane < HD, x, z), jnp.where(lane >= HD, x, z)], axis=0)


def _unstack_heads(y):
    lane = lax.broadcasted_iota(jnp.int32, (BLK, LANES), 1)
    return jnp.where(lane < HD, y[:BLK], y[BLK:])


def _window_masks():
    row = lax.broadcasted_iota(jnp.int32, (2 * BLK, 2 * BLK), 0) % BLK
    col = lax.broadcasted_iota(jnp.int32, (2 * BLK, 2 * BLK), 1)
    return (col < BLK) & (col >= row), (col >= BLK) & (col - BLK <= row)


def _two_blocks(ref, b):
    r0 = pl.multiple_of(b * BLK, BLK)
    rp = pl.multiple_of(jnp.maximum(b - 1, 0) * BLK, BLK)
    return jnp.concatenate([ref[pl.ds(rp, BLK), :], ref[pl.ds(r0, BLK), :]], axis=0)


def _merge_masks():
    row = lax.broadcasted_iota(jnp.int32, (2 * BLK, BLK), 0) % BLK
    col = lax.broadcasted_iota(jnp.int32, (2 * BLK, BLK), 1)
    return col <= row, col == row


def attn_fwd(qf, kf, vf, g, nb):
    T = qf.shape[0]

    def body(q_ref, k_ref, v_ref, o_ref, l_ref):
        cur_m, own_m = _merge_masks()

        def step(b, carry):
            r0 = pl.multiple_of(b * BLK, BLK)
            rp = pl.multiple_of(jnp.maximum(b - 1, 0) * BLK, BLK)
            qs = _stack_heads(q_ref[pl.ds(r0, BLK), :])
            vc, vp = v_ref[pl.ds(r0, BLK), :], v_ref[pl.ds(rp, BLK), :]
            sp = jnp.where((b % nb) != 0, _dot(qs, k_ref[pl.ds(rp, BLK), :], NT), NEG)
            s = jnp.where(cur_m, _dot(qs, k_ref[pl.ds(r0, BLK), :], NT), sp)
            s_own = jnp.sum(jnp.where(own_m, sp, 0.0), axis=-1, keepdims=True)
            m = jnp.maximum(jnp.max(s, axis=-1, keepdims=True), s_own)
            p, p_own = jnp.exp(s - m), jnp.exp(s_own - m)
            l = jnp.sum(p, axis=-1, keepdims=True) + p_own
            pb = p.astype(MX)
            zero = jnp.zeros_like(pb)
            o = _dot(jnp.where(cur_m, pb, zero), vc) + _dot(jnp.where(cur_m, zero, pb), vp)
            o = (o + p_own * jnp.concatenate([vp, vp], axis=0).astype(f32)) / l
            o_ref[pl.ds(r0, BLK), :] = _unstack_heads(o)
            l_ref[pl.ds(r0, BLK), :] = _unstack_heads(jnp.broadcast_to(m + jnp.log(l), (2 * BLK, LANES)))
            return carry

        lax.fori_loop(0, T // BLK, step, 0, unroll=8)

    spec = pl.BlockSpec((T, LANES), lambda j: (0, j))
    return pl.pallas_call(
        body, name=f"attn_fwd{g}", grid=(AO // LANES,),
        in_specs=[spec] * 3, out_specs=[spec] * 2,
        out_shape=[jax.ShapeDtypeStruct((T, AO), f32)] * 2,
        compiler_params=_cp(("parallel",), 56),
    )(qf, kf, vf)


def _group_weights(lses):
    m = jnp.maximum(jnp.maximum(lses[0], lses[1]), lses[2])
    e = [jnp.exp(l - m) for l in lses]
    inv = 1.0 / (e[0] + e[1] + e[2])
    return [x * inv for x in e]


def _fold_specs(T, tm):
    specs = []
    for _, d in GROUPS:
        specs.append(pl.BlockSpec((d, tm // d, AO), lambda i: (0, i, 0)))
    return specs


def combine_fwd(os_, lses):
    T = os_[0].shape[0] * os_[0].shape[1]
    tm = TM_FOLD

    def body(o0, o1, o2, l0, l1, l2, y_ref, nat):
        o = [_unfold_in(nat, r, d) for r, (_, d) in zip((o0, o1, o2), GROUPS)]
        ls = [_unfold_in(nat, r, d) for r, (_, d) in zip((l0, l1, l2), GROUPS)]
        w = _group_weights(ls)
        y_ref[...] = (w[0] * o[0] + w[1] * o[1] + w[2] * o[2]).astype(y_ref.dtype)

    specs = _fold_specs(T, tm)
    return pl.pallas_call(
        body, name="combine_fwd", grid=(T // tm,),
        in_specs=specs + specs, out_specs=pl.BlockSpec((tm, AO), lambda i: (i, 0)),
        out_shape=jax.ShapeDtypeStruct((T, AO), MX),
        scratch_shapes=[pltpu.VMEM((AO // LANES, tm, LANES), f32)],
        compiler_params=_cp(("parallel",)),
    )(*os_, *lses)


TM_MIX = 256


def mix_out_fwd(proj, ya, yb, yc, x0, pa, pb, pc, wo, g1, b1):
    T = x0.shape[0]
    tm = min(TM_MIX, T)

    def body(gt, ya_r, yb_r, yc_r, x0_r, pa_r, pb_r, pc_r, wo_r, g_r, b_r, mabc, m_o, r1_o, x1_o):
        ma = _dot(ya_r[...], pa_r[...])
        ybv = yb_r[...]
        mb = jnp.concatenate([_dot(ybv, pb_r[k]) for k in range(NCHIP)], axis=1)
        mc = _dot(yc_r[...], pc_r[...])
        m = jnp.zeros((tm, D), f32)
        for j, mm in enumerate((ma, mb, mc)):
            mabc[:, j * D:(j + 1) * D] = mm.astype(mabc.dtype)
            m = m + _sigmoid(gt[:, j * D:(j + 1) * D].astype(f32)) * mm
        mb16 = m.astype(MX)
        m_o[...] = mb16
        r1 = ALPHA * x0_r[...] + _dot(mb16, wo_r[...])
        r1_o[...] = r1
        xhat, _ = _ln_stats(r1)
        x1_o[...] = xhat * g_r[...] + b_r[...]

    full = lambda shape: pl.BlockSpec(shape, lambda i: (0,) * len(shape))
    tile = lambda w: pl.BlockSpec((tm, w), lambda i: (i, 0))
    return pl.pallas_call(
        body, name="mix_out_fwd", grid=(T // tm,),
        in_specs=[tile(3 * D), tile(D), tile(AO), tile(D), tile(D), full((D, D)), full((NCHIP, AO, D // NCHIP)),
                  full((D, D)), full((D, D)), full((1, D)), full((1, D))],
        out_specs=[tile(3 * D), tile(D), tile(D), tile(D)],
        out_shape=[jax.ShapeDtypeStruct((T, 3 * D), MX), jax.ShapeDtypeStruct((T, D), MX),
                   jax.ShapeDtypeStruct((T, D), f32), jax.ShapeDtypeStruct((T, D), f32)],
        compiler_params=_cp(("parallel",), 56),
    )(proj, ya, yb, yc, x0, pa, pb, pc, wo, g1, b1)


TM_FF = 512
TM_FFB = 256
ROW_CHUNK = 64


def ffn_up_fwd(x1, wg, wu):
    T = x1.shape[0]
    tm = min(TM_FFB, T)

    def body(x_r, wg_r, wu_r, g_o, u_o, h_o, gs, us):
        xb = x_r[...].astype(MX)
        for k in range(NCHIP):
            gs[...] = _dot(xb, wg_r[k])
            us[...] = _dot(xb, wu_r[k])
            for r in range(0, tm, ROW_CHUNK):
                rows = pl.ds(r, ROW_CHUNK)
                gate, up = gs[rows, :], us[rows, :]
                g_o[k, rows, :] = gate.astype(g_o.dtype)
                u_o[k, rows, :] = up.astype(u_o.dtype)
                h_o[k, rows, :] = (gate * _sigmoid(gate) * up).astype(h_o.dtype)

    wspec = pl.BlockSpec((NCHIP, D, FB), lambda i: (0, 0, 0))
    ospec = pl.BlockSpec((NCHIP, tm, FB), lambda i: (0, i, 0))
    return pl.pallas_call(
        body, name="ffn_up_fwd", grid=(T // tm,),
        in_specs=[pl.BlockSpec((tm, D), lambda i: (i, 0)), wspec, wspec],
        out_specs=[ospec] * 3,
        out_shape=[jax.ShapeDtypeStruct((NCHIP, T, FB), ACT)] * 2 + [jax.ShapeDtypeStruct((NCHIP, T, FB), MX)],
        scratch_shapes=[pltpu.VMEM((tm, FB), f32)] * 2,
        compiler_params=_cp(("parallel",)),
    )(x1, wg, wu)


def ffn_down_fwd(hh, wd, x1, g2, b2):
    T = x1.shape[0]
    tm = min(TM_FF, T)

    def body(h_r, w_r, x_r, g_r, b_r, r2_o, x2_o):
        r2 = ALPHA * x_r[...]
        for k in range(NCHIP):
            r2 = r2 + _dot(h_r[k], w_r[k])
        r2_o[...] = r2
        xhat, _ = _ln_stats(r2)
        x2_o[...] = xhat * g_r[...] + b_r[...]

    tile = pl.BlockSpec((tm, D), lambda i: (i, 0))
    vec = pl.BlockSpec((1, D), lambda i: (0, 0))
    return pl.pallas_call(
        body, name="ffn_down_fwd", grid=(T // tm,),
        in_specs=[pl.BlockSpec((NCHIP, tm, FB), lambda i: (0, i, 0)), pl.BlockSpec((NCHIP, FB, D), lambda i: (0, 0, 0)),
                  tile, vec, vec],
        out_specs=[tile, tile], out_shape=[jax.ShapeDtypeStruct((T, D), f32)] * 2,
        compiler_params=_cp(("parallel",)),
    )(hh, wd, x1, g2, b2)


def loss_grad(y, tgt):
    T = y.shape[0]
    tm = min(512, T)

    def body(y_r, t_r, l_o, dy_o):
        e = y_r[...] - t_r[...]
        dy_o[...] = e * (1.0 / D)

        @pl.when(pl.program_id(0) == 0)
        def _():
            l_o[...] = jnp.zeros_like(l_o)
        l_o[...] += (0.5 / D) * jnp.sum(e * e)

    tile = pl.BlockSpec((tm, D), lambda i: (i, 0))
    return pl.pallas_call(
        body, name="loss_grad", grid=(T // tm,),
        in_specs=[tile, tile], out_specs=[pl.BlockSpec((8, LANES), lambda i: (0, 0)), tile],
        out_shape=[jax.ShapeDtypeStruct((8, LANES), f32), jax.ShapeDtypeStruct((T, D), f32)],
        compiler_params=_cp(("arbitrary",)),
    )(y, tgt)


def ffn_down_bwd(dx2, r2, g2, wd, gate, up):
    T = dx2.shape[0]
    tm = min(TM_FFB, T)

    def body(dx_r, r_r, g_r, w_r, ga_r, up_r, dr_o, dg_o, du_o, dlg_o, dlb_o, hs):
        i = pl.program_id(0)
        xhat, rstd = _ln_stats(r_r[...])
        dx = dx_r[...]
        _acc_rows(dlg_o, i == 0, dx * xhat)
        _acc_rows(dlb_o, i == 0, dx)
        dr = _ln_bwd(dx, xhat, rstd, g_r[...])
        dr_o[...] = dr
        drb = dr.astype(MX)
        for k in range(NCHIP):
            hs[...] = _dot(drb, w_r[k], NT)
            for r in range(0, tm, ROW_CHUNK):
                rows = pl.ds(r, ROW_CHUNK)
                dhh, gate_v, up_v = hs[rows, :], ga_r[k, rows, :].astype(f32), up_r[k, rows, :].astype(f32)
                sg = _sigmoid(gate_v)
                dg_o[k, rows, :] = (dhh * up_v * sg * (1.0 + gate_v * (1.0 - sg))).astype(dg_o.dtype)
                du_o[k, rows, :] = (dhh * gate_v * sg).astype(du_o.dtype)

    tile = pl.BlockSpec((tm, D), lambda i: (i, 0))
    vec = pl.BlockSpec((1, D), lambda i: (0, 0))
    blk = pl.BlockSpec((NCHIP, tm, FB), lambda i: (0, i, 0))
    return pl.pallas_call(
        body, name="ffn_down_bwd", grid=(T // tm,),
        in_specs=[tile, tile, vec, pl.BlockSpec((NCHIP, FB, D), lambda i: (0, 0, 0)), blk, blk],
        out_specs=[tile, blk, blk, vec, vec],
        out_shape=[jax.ShapeDtypeStruct((T, D), f32)] + [jax.ShapeDtypeStruct((NCHIP, T, FB), MX)] * 2
        + [jax.ShapeDtypeStruct((1, D), f32)] * 2,
        scratch_shapes=[pltpu.VMEM((tm, FB), f32)],
        compiler_params=_cp(("arbitrary",)),
    )(dx2, r2, g2, wd, gate, up)


def ffn_up_bwd(dr2, dgate, dup, wg, wu, r1, g1):
    T = dr2.shape[0]
    tm = min(TM_FFB, T)

    def body(dr2_r, dg_r, du_r, wg_r, wu_r, r1_r, g_r, dr1_o, dlg_o, dlb_o):
        i = pl.program_id(0)
        dx = ALPHA * dr2_r[...]
        for k in range(NCHIP):
            dx = dx + _dot(dg_r[k], wg_r[k], NT) + _dot(du_r[k], wu_r[k], NT)
        xhat, rstd = _ln_stats(r1_r[...])
        _acc_rows(dlg_o, i == 0, dx * xhat)
        _acc_rows(dlb_o, i == 0, dx)
        dr1_o[...] = _ln_bwd(dx, xhat, rstd, g_r[...])

    tile = pl.BlockSpec((tm, D), lambda i: (i, 0))
    vec = pl.BlockSpec((1, D), lambda i: (0, 0))
    blk = pl.BlockSpec((NCHIP, tm, FB), lambda i: (0, i, 0))
    wspec = pl.BlockSpec((NCHIP, D, FB), lambda i: (0, 0, 0))
    return pl.pallas_call(
        body, name="ffn_up_bwd", grid=(T // tm,),
        in_specs=[tile, blk, blk, wspec, wspec, tile, vec],
        out_specs=[tile, vec, vec],
        out_shape=[jax.ShapeDtypeStruct((T, D), f32)] + [jax.ShapeDtypeStruct((1, D), f32)] * 2,
        compiler_params=_cp(("arbitrary",)),
    )(dr2, dgate, dup, wg, wu, r1, g1)


def mix_out_bwd(dr1, proj, mabc, wo, pa, pb, pc):
    T = dr1.shape[0]
    tm = min(TM_MIX, T)

    def body(dr_r, gt, mabc_r, wo_r, pa_r, pb_r, pc_r, dmabc_o, dgt_o, dya_o, dyb_o, dyc_o):
        dm = _dot(dr_r[...].astype(MX), wo_r[...], NT)
        dmx = []
        for j in range(3):
            s = _sigmoid(gt[:, j * D:(j + 1) * D].astype(f32))
            v = (dm * s).astype(MX)
            dmx.append(v)
            dmabc_o[:, j * D:(j + 1) * D] = v
            dgt_o[:, j * D:(j + 1) * D] = (dm * mabc_r[:, j * D:(j + 1) * D].astype(f32) * s * (1.0 - s)).astype(dgt_o.dtype)
        dya_o[...] = _dot(dmx[0], pa_r[...], NT)
        dyb = jnp.zeros((tm, AO), f32)
        for k in range(NCHIP):
            dyb = dyb + _dot(dmx[1][:, k * (D // NCHIP):(k + 1) * (D // NCHIP)], pb_r[k], NT)
        dyb_o[...] = dyb
        dyc_o[...] = _dot(dmx[2], pc_r[...], NT)

    full = lambda shape: pl.BlockSpec(shape, lambda i: (0,) * len(shape))
    tile = lambda w: pl.BlockSpec((tm, w), lambda i: (i, 0))
    return pl.pallas_call(
        body, name="mix_out_bwd", grid=(T // tm,),
        in_specs=[tile(D), tile(3 * D), tile(3 * D), full((D, D)), full((D, D)), full((NCHIP, AO, D // NCHIP)), full((D, D))],
        out_specs=[tile(3 * D), tile(3 * D), tile(D), tile(AO), tile(D)],
        out_shape=[jax.ShapeDtypeStruct((T, 3 * D), MX), jax.ShapeDtypeStruct((T, 3 * D), MX),
                   jax.ShapeDtypeStruct((T, D), f32), jax.ShapeDtypeStruct((T, AO), f32), jax.ShapeDtypeStruct((T, D), f32)],
        compiler_params=_cp(("parallel",), 56),
    )(dr1, proj, mabc, wo, pa, pb, pc)


def transpose_cast(x):
    T = x.shape[0]
    tm = min(512, T)

    def body(x_r, o_r):
        o_r[...] = x_r[...].T.astype(o_r.dtype)

    return pl.pallas_call(
        body, name="transpose_cast", grid=(T // tm,),
        in_specs=[pl.BlockSpec((tm, D), lambda i: (i, 0))], out_specs=pl.BlockSpec((D, tm), lambda i: (0, i)),
        out_shape=jax.ShapeDtypeStruct((D, T), MX), compiler_params=_cp(("parallel",)),
    )(x)


def tn_matmul(name, a, b, a_spec, b_spec, out_shape, out_spec, grid):
    nt = len(grid) - 1

    def body(a_r, b_r, o_r):
        @pl.when(pl.program_id(nt) == 0)
        def _():
            o_r[...] = jnp.zeros_like(o_r)
        av = a_r[...].reshape(a_r.shape[-2:]).astype(MX)
        bv = b_r[...].reshape(b_r.shape[-2:]).astype(MX)
        o_r[...] += _dot(av, bv, TN).reshape(o_r.shape)

    return pl.pallas_call(
        body, name=name, grid=grid, in_specs=[a_spec, b_spec], out_specs=out_spec,
        out_shape=jax.ShapeDtypeStruct(out_shape, f32),
        compiler_params=_cp(("parallel",) * nt + ("arbitrary",), 56),
    )(a, b)


def attn_pre_bwd(dyb, os_, lses, ones):
    T = dyb.shape[0]
    tm = TM_FOLD

    def body(dy_r, o0, o1, o2, l0, l1, l2, ones_r, d0, d1, d2, f0, f1, f2, nat):
        o = [_unfold_in(nat, r, d) for r, (_, d) in zip((o0, o1, o2), GROUPS)]
        ls = [_unfold_in(nat, r, d) for r, (_, d) in zip((l0, l1, l2), GROUPS)]
        w = _group_weights(ls)
        dy = dy_r[...]
        t = dy * (w[0] * o[0] + w[1] * o[1] + w[2] * o[2])
        hi = t.astype(MX)
        lo = (t - hi.astype(f32)).astype(MX)
        c = _dot(hi, ones_r[...]) + _dot(lo, ones_r[...])
        for wg, do_o, df_o, (_, d) in zip(w, (d0, d1, d2), (f0, f1, f2), GROUPS):
            _fold_out(nat, wg * dy, do_o, d)
            _fold_out(nat, -wg * c, df_o, d)

    specs = _fold_specs(T, tm)
    return pl.pallas_call(
        body, name="attn_pre_bwd", grid=(T // tm,),
        in_specs=[pl.BlockSpec((tm, AO), lambda i: (i, 0))] + specs + specs + [pl.BlockSpec((AO, AO), lambda i: (0, 0))],
        out_specs=specs + specs,
        out_shape=[jax.ShapeDtypeStruct((d, T // d, AO), MX) for _, d in GROUPS]
        + [jax.ShapeDtypeStruct((d, T // d, AO), f32) for _, d in GROUPS],
        scratch_shapes=[pltpu.VMEM((AO // LANES, tm, LANES), f32)],
        compiler_params=_cp(("parallel",)),
    )(dyb, *os_, *lses, ones)


def _head_ones():
    i = jnp.arange(AO) // HD
    return (i[:, None] == i[None, :]).astype(MX)


BWD_BLOCKS = 4


def attn_bwd(qf, kf, vf, dof, lse, df, g, nb):
    T = qf.shape[0]

    def body(q_ref, k_ref, v_ref, do_ref, l_ref, d_ref, dq_ref, dk_ref, dv_ref):
        prev_m, cur_m = _window_masks()

        def head_col(ref, r0):
            v = ref[pl.ds(r0, BLK), :]
            return jnp.concatenate([v[:, 0:1], v[:, HD:HD + 1]], axis=0)

        def step(b, carry):
            dk_c, dv_c = carry
            r0 = pl.multiple_of(b * BLK, BLK)
            rp = pl.multiple_of(jnp.maximum(b - 1, 0) * BLK, BLK)
            qs, dos = _stack_heads(q_ref[pl.ds(r0, BLK), :]), _stack_heads(do_ref[pl.ds(r0, BLK), :])
            k2, v2 = _two_blocks(k_ref, b), _two_blocks(v_ref, b)
            valid = cur_m | (prev_m & ((b % nb) != 0))
            p = jnp.where(valid, jnp.exp(_dot(qs, k2, NT) - head_col(l_ref, r0)), 0.0)
            ds = (p * (_dot(dos, v2, NT) + head_col(d_ref, r0))).astype(MX)
            dq_ref[pl.ds(r0, BLK), :] = _unstack_heads(_dot(ds, k2)).astype(dq_ref.dtype)
            dk2 = _dot(ds, qs, TN)
            dv2 = _dot(p.astype(MX), dos, TN)
            dk_ref[pl.ds(rp, BLK), :] = (dk_c + dk2[:BLK]).astype(dk_ref.dtype)
            dv_ref[pl.ds(rp, BLK), :] = (dv_c + dv2[:BLK]).astype(dv_ref.dtype)
            return dk2[BLK:], dv2[BLK:]

        zero = jnp.zeros((BLK, LANES), f32)

        def steps(i, carry):
            for j in range(BWD_BLOCKS):
                carry = step(BWD_BLOCKS * i + j, carry)
            return carry

        dk_c, dv_c = lax.fori_loop(0, T // BLK // BWD_BLOCKS, steps, (zero, zero))
        dk_ref[pl.ds(T - BLK, BLK), :] = dk_c.astype(dk_ref.dtype)
        dv_ref[pl.ds(T - BLK, BLK), :] = dv_c.astype(dv_ref.dtype)

    spec = pl.BlockSpec((T, LANES), lambda j: (0, j))
    return pl.pallas_call(
        body, name=f"attn_bwd{g}", grid=(AO // LANES,),
        in_specs=[spec] * 6, out_specs=[spec] * 3,
        out_shape=[jax.ShapeDtypeStruct((T, AO), MX)] * 3,
        compiler_params=_cp(("parallel",), 60),
    )(qf, kf, vf, dof, lse, df)


def unfold_rope_bwd(dqf, dkf, dvf, cos_t, sin_t, g, d):
    T = dqf.shape[0] * dqf.shape[1]
    tm = TM_FOLD

    def body(q_r, k_r, v_r, c_ref, s_ref, o_ref, nat):
        cos, sin = _tile4(c_ref[...]), _tile4(s_ref[...])
        for part, ref, scale in ((0, q_r, HD ** -0.5), (1, k_r, 1.0), (2, v_r, None)):
            x = _unfold_in(nat, ref, d)
            if scale is not None:
                x = (x * cos - _swap_halves(x) * sin) * scale
            o_ref[:, part * AO:(part + 1) * AO] = x.astype(o_ref.dtype)

    fold_spec = pl.BlockSpec((d, tm // d, AO), lambda i: (0, i, 0))
    tab = pl.BlockSpec((tm, LANES), lambda i: (i, 0))
    return pl.pallas_call(
        body, name=f"unfold_rope_bwd{g}", grid=(T // tm,),
        in_specs=[fold_spec] * 3 + [tab, tab],
        out_specs=pl.BlockSpec((tm, 3 * AO), lambda i: (i, 0)),
        out_shape=jax.ShapeDtypeStruct((T, 3 * AO), MX),
        scratch_shapes=[pltpu.VMEM((AO // LANES, tm, LANES), f32)],
        compiler_params=_cp(("parallel",)),
    )(dqf, dkf, dvf, cos_t, sin_t)


CONV_CHUNK = 32


def conv_bwd(dya, proj, conv_w):
    T = dya.shape[0]
    tm = TM_AC
    last = T // tm - 1

    def body(dy_r, bch, hprev, dy_next, b_next, cw, d_o, dw_o, zs, ds):
        i = pl.program_id(0)
        ch = CONV_CHUNK
        hz = hprev[:, :D].astype(f32) * hprev[:, D:].astype(f32)
        zs[0:HALO, :] = jnp.where(i > 0, hz, 0.0)
        ds[tm:tm + HALO, :] = jnp.where(i < last, dy_next[...] * b_next[...].astype(f32), 0.0)
        for r in range(0, tm, ch):
            zs[HALO + r:HALO + r + ch, :] = bch[r:r + ch, D:2 * D].astype(f32) * bch[r:r + ch, 2 * D:].astype(f32)
            ds[r:r + ch, :] = dy_r[r:r + ch, :] * bch[r:r + ch, :D].astype(f32)

        @pl.when(i == 0)
        def _():
            dw_o[...] = jnp.zeros_like(dw_o)

        sums = [jnp.zeros((1, D), f32) for _ in range(3)]
        for r in range(0, tm, ch):
            z2, z1, z = (zs[HALO + r - s:HALO + r - s + ch, :] for s in (2, 1, 0))
            dcv, d1, d2 = (ds[r + s:r + s + ch, :] for s in (0, 1, 2))
            cv = cw[0:1, :] * z2 + cw[1:2, :] * z1 + cw[2:3, :] * z
            dz = cw[2:3, :] * dcv + cw[1:2, :] * d1 + cw[0:1, :] * d2
            d_o[r:r + ch, :D] = (dy_r[r:r + ch, :] * cv).astype(d_o.dtype)
            d_o[r:r + ch, D:2 * D] = (dz * bch[r:r + ch, 2 * D:].astype(f32)).astype(d_o.dtype)
            d_o[r:r + ch, 2 * D:] = (dz * bch[r:r + ch, D:2 * D].astype(f32)).astype(d_o.dtype)
            for k, zz in enumerate((z2, z1, z)):
                sums[k] = sums[k] + jnp.sum(dcv * zz, axis=0, keepdims=True)
        for k in range(3):
            dw_o[k:k + 1, :] += sums[k]

    nh = tm // HALO
    return pl.pallas_call(
        body, name="conv_bwd", grid=(T // tm,),
        in_specs=[pl.BlockSpec((tm, D), lambda i: (i, 0)), pl.BlockSpec((tm, 3 * D), lambda i: (i, 1)),
                  pl.BlockSpec((HALO, 2 * D), lambda i: (jnp.maximum(i * nh - 1, 0), 2)),
                  pl.BlockSpec((HALO, D), lambda i: (jnp.minimum((i + 1) * nh, T // HALO - 1), 0)),
                  pl.BlockSpec((HALO, D), lambda i: (jnp.minimum((i + 1) * nh, T // HALO - 1), 3)),
                  pl.BlockSpec((3, D), lambda i: (0, 0))],
        out_specs=[pl.BlockSpec((tm, 3 * D), lambda i: (i, 0)), pl.BlockSpec((3, D), lambda i: (0, 0))],
        out_shape=[jax.ShapeDtypeStruct((T, 3 * D), MX), jax.ShapeDtypeStruct((3, D), f32)],
        scratch_shapes=[pltpu.VMEM((HALO + tm, D), f32), pltpu.VMEM((tm + HALO, D), f32)],
        compiler_params=_cp(("arbitrary",)),
    )(dya, proj, proj, dya, proj, conv_w)


def gmlp_bwd(dyc, proj, wst, bsx, lg, lb):
    T = dyc.shape[0]
    tm = TM_AC
    last = T // tm - 1

    def body(dy_r, u0, u1, v0, v1, ws, bs, lg_r, lb_r, d_o, dws_o, dbs_o, dlg_o, dlb_o, bacc):
        i = pl.program_id(0)
        up = jnp.concatenate([u0[...], u1[...]], axis=1).astype(f32)
        vp = jnp.concatenate([v0[...], v1[...]], axis=1).astype(f32)
        u, du = _gelu_and_grad(up)
        gv, dgv = _gelu_and_grad(vp)
        u, vn, xhat, rstd, sp = _gmlp_fwd(up, vp, ws, bs, lg_r[...], lb_r[...], u, gv)
        dy = dy_r[...]
        d_o[:, :D] = (dy * sp * du).astype(d_o.dtype)
        dsp = dy * u
        dspb, vnb = dsp.astype(MX), vn.astype(MX)

        @pl.when(i == 0)
        def _():
            dws_o[...] = jnp.zeros_like(dws_o)
            bacc[...] = jnp.zeros_like(bacc)

        rows = []
        for c in range(tm // BLK):
            r = slice(c * BLK, (c + 1) * BLK)
            cols = []
            for g in range(8):
                cs = slice(g * BLK, (g + 1) * BLK)
                dws_o[g] += _dot(dspb[r, cs], vnb[r, cs], NT)
                bacc[g] += dsp[r, cs]
                cols.append(_dot(ws[g], dspb[r, cs], TN))
            rows.append(jnp.concatenate(cols, axis=1))
        dvn = jnp.concatenate(rows, axis=0)
        _acc_rows(dlg_o, i == 0, dvn * xhat)
        _acc_rows(dlb_o, i == 0, dvn)
        d_o[:, D:] = (_ln_bwd(dvn, xhat, rstd, lg_r[...]) * dgv).astype(d_o.dtype)

        @pl.when(i == last)
        def _():
            row = lax.broadcasted_iota(jnp.int32, (BLK, BLK), 0)
            col = lax.broadcasted_iota(jnp.int32, (BLK, BLK), 1)
            ones = jnp.ones((8, BLK), MX)
            for g in range(8):
                dws_o[g] = jnp.where(col <= row, dws_o[g], 0.0)
                a = bacc[g]
                hi = a.astype(MX)
                lo = (a - hi.astype(f32)).astype(MX)
                dbs_o[g:g + 1, :] = (_dot(ones, hi, NT) + _dot(ones, lo, NT))[0:1, :]

    full = lambda shape: pl.BlockSpec(shape, lambda i: (0,) * len(shape))
    return pl.pallas_call(
        body, name="gmlp_bwd", grid=(T // tm,),
        in_specs=[pl.BlockSpec((tm, D), lambda i: (i, 0)), *_uv_specs(), full((8, BLK, BLK)), full((8, BLK, BLK)),
                  full((1, D)), full((1, D))],
        out_specs=[pl.BlockSpec((tm, 2 * D), lambda i: (i, 0)), full((8, BLK, BLK)), full((8, BLK)), full((1, D)), full((1, D))],
        out_shape=[jax.ShapeDtypeStruct((T, 2 * D), MX), jax.ShapeDtypeStruct((8, BLK, BLK), f32),
                   jax.ShapeDtypeStruct((8, BLK), f32), jax.ShapeDtypeStruct((1, D), f32), jax.ShapeDtypeStruct((1, D), f32)],
        scratch_shapes=[pltpu.VMEM((8, BLK, BLK), f32)],
        compiler_params=_cp(("arbitrary",)),
    )(dyc, proj, proj, proj, proj, wst, bsx, lg, lb)


PART_TILES = (6, 6, 3, 3, 3, 4)
PART_START = (0, 6, 12, 15, 18, 21)
TJ = 512


def _part_specs(tm, rows_axis):
    specs = []
    for n, s in zip(PART_TILES, PART_START):
        def imap(*idx, n=n, s=s):
            i, j = idx[rows_axis], idx[1 - rows_axis]
            inside = (j >= s) & (j < s + n)
            return (jnp.where(inside, i, 0), jnp.clip(j - s, 0, n - 1))
        specs.append(pl.BlockSpec((tm, TJ), imap))
    return specs


def _with_part(j, refs, fn):
    for r, n, s in zip(refs, PART_TILES, PART_START):
        @pl.when((j >= s) & (j < s + n))
        def _():
            fn(r[...])


def dx_in(dr1, parts, w, bias):
    T = dr1.shape[0]
    tm = min(2048, T)

    def body(dr_r, p0, p1, p2, p3, p4, p5, w_r, b_r, o_r):
        j = pl.program_id(1)

        @pl.when(j == 0)
        def _():
            o_r[...] = ALPHA * dr_r[...] + b_r[...]

        def acc(tile):
            o_r[...] += _dot(tile, w_r[...], NT)
        _with_part(j, (p0, p1, p2, p3, p4, p5), acc)

    once = dict(pipeline_mode=pl.Buffered(1))
    return pl.pallas_call(
        body, name="dx_in", grid=(T // tm, NIN // TJ),
        in_specs=[pl.BlockSpec((tm, D), lambda i, j: (i, 0), **once)] + _part_specs(tm, 0)
        + [pl.BlockSpec((D, TJ), lambda i, j: (0, j)), pl.BlockSpec((1, D), lambda i, j: (0, 0))],
        out_specs=pl.BlockSpec((tm, D), lambda i, j: (i, 0), **once),
        out_shape=jax.ShapeDtypeStruct((T, D), f32),
        compiler_params=_cp(("parallel", "arbitrary"), 56),
    )(dr1, *parts, w, bias)


def dw_in(x0t, parts):
    T = x0t.shape[1]
    tk = min(2048, T)

    def body(x_r, p0, p1, p2, p3, p4, p5, o_r):
        j, t = pl.program_id(0), pl.program_id(1)

        @pl.when(t == 0)
        def _():
            o_r[...] = jnp.zeros_like(o_r)

        def acc(tile):
            o_r[...] += _dot(x_r[:, pl.ds(pl.multiple_of(t * tk, tk), tk)], tile)
        _with_part(j, (p0, p1, p2, p3, p4, p5), acc)

    return pl.pallas_call(
        body, name="dw_in", grid=(NIN // TJ, T // tk),
        in_specs=[pl.BlockSpec((D, T), lambda j, t: (0, 0), pipeline_mode=pl.Buffered(1))] + _part_specs(tk, 1),
        out_specs=pl.BlockSpec((D, TJ), lambda j, t: (0, j)),
        out_shape=jax.ShapeDtypeStruct((D, NIN), f32),
        compiler_params=_cp(("parallel", "arbitrary"), 56),
    )(x0t, *parts)


def rope_tables(positions):
    half = HD // 2
    inv_freq = ROPE_THETA ** (-jnp.arange(half, dtype=f32) / half)
    ang = positions.astype(f32)[:, None] * inv_freq
    cos, sin = jnp.cos(ang), jnp.sin(ang)
    return jnp.tile(cos, (1, LANES // half)), jnp.tile(jnp.concatenate([-sin, sin], axis=1), (1, LANES // HD))


def _flat(a):
    return a.reshape(a.shape[0] * a.shape[1], a.shape[2])


def layer_fwd(x0, W, cos_t, sin_t):
    T = x0.shape[0]
    proj = mm_in(x0, W["w_in"], W["in_bias"])
    ya, yc = mix_ac_fwd(proj, W["conv_w"], W["wst"], W["bsx"], W["gmlp_ln_g"], W["gmlp_ln_b"])
    folded, os_, lses = [], [], []
    for g, (_, d) in enumerate(GROUPS):
        qf, kf, vf = fold_rope(proj, cos_t, sin_t, g, d)
        o, lse = attn_fwd(_flat(qf), _flat(kf), _flat(vf), g, T // d // BLK)
        folded.append((qf, kf, vf))
        os_.append(o.reshape(d, T // d, AO))
        lses.append(lse.reshape(d, T // d, AO))
    yb = combine_fwd(os_, lses)
    if "late" in W:
        W = {**W, **W["late"](yb)}
    mabc, m, r1, x1 = mix_out_fwd(proj, ya, yb, yc, x0, W["p_a"], W["p_b"], W["p_c"], W["w_o"], W["ln1_g"], W["ln1_b"])
    gate, up, hh = ffn_up_fwd(x1, W["w_gate"], W["w_up"])
    r2, x2 = ffn_down_fwd(hh, W["w_down"], x1, W["ln2_g"], W["ln2_b"])
    saved = dict(x0=x0, proj=proj, ya=ya, yb=yb, yc=yc, folded=folded, os=os_, lses=lses, mabc=mabc, m=m, r1=r1,
                 x1=x1, gate=gate, up=up, hh=hh, r2=r2)
    return x2, saved, W


def layer_bwd(dx2, S, W, cos_t, sin_t, on_grads=None):
    T = dx2.shape[0]
    tk = min(2048, T)
    G = {}
    dr2, dgate, dup, G["ln2_g"], G["ln2_b"] = ffn_down_bwd(dx2, S["r2"], W["ln2_g"], W["w_down"], S["gate"], S["up"])
    blk_a = pl.BlockSpec((1, tk, FB), lambda k, t: (k, t, 0))
    row_b = pl.BlockSpec((tk, D), lambda k, t: (t, 0))
    G["w_down"] = tn_matmul("dw_down", S["hh"], dr2, blk_a, row_b, (NCHIP, FB, D),
                            pl.BlockSpec((1, FB, D), lambda k, t: (k, 0, 0)), (NCHIP, T // tk))
    for nm, dv in (("w_gate", dgate), ("w_up", dup)):
        G[nm] = tn_matmul("d" + nm, dv, S["x1"], blk_a, row_b, (NCHIP, FB, D),
                          pl.BlockSpec((1, FB, D), lambda k, t: (k, 0, 0)), (NCHIP, T // tk))
    dr1, G["ln1_g"], G["ln1_b"] = ffn_up_bwd(dr2, dgate, dup, W["w_gate"], W["w_up"], S["r1"], W["ln1_g"])
    dmabc, dgates, dya, dyb, dyc = mix_out_bwd(dr1, S["proj"], S["mabc"], W["w_o"], W["p_a"], W["p_b"], W["p_c"])
    one = (1, T // tk)
    full_o = pl.BlockSpec((D, D), lambda k, t: (0, 0))
    G["w_o"] = tn_matmul("dw_o", S["m"], dr1, row_b, row_b, (D, D), full_o, one)
    G["p_a"] = tn_matmul("dp_a", S["ya"], dmabc, row_b, pl.BlockSpec((tk, D), lambda k, t: (t, 0)), (D, D), full_o, one)
    G["p_c"] = tn_matmul("dp_c", S["yc"], dmabc, row_b, pl.BlockSpec((tk, D), lambda k, t: (t, 2)), (D, D), full_o, one)
    G["p_b"] = tn_matmul("dp_b", S["yb"], dmabc, pl.BlockSpec((tk, AO), lambda k, t: (t, 0)),
                         pl.BlockSpec((tk, D // NCHIP), lambda k, t: (t, NCHIP + k)), (NCHIP, AO, D // NCHIP),
                         pl.BlockSpec((1, AO, D // NCHIP), lambda k, t: (k, 0, 0)), (NCHIP, T // tk))
    conv_w = W["conv_w"]
    if on_grads is not None:
        conv_w = conv_w + on_grads({n: G[n] for n in BIG if n != "w_in"})
    dbch, G["conv_w"] = conv_bwd(dya, S["proj"], conv_w)
    duv, G["w_s"], G["b_s"], G["gmlp_ln_g"], G["gmlp_ln_b"] = gmlp_bwd(
        dyc, S["proj"], W["wst"], W["bsx"], W["gmlp_ln_g"], W["gmlp_ln_b"])
    ones = _head_ones()
    if on_grads is not None:
        small = {n: G[n] for n in VECS + ("b_s", "w_s", "conv_w")}
        ones = ones + on_grads(small).astype(MX)
    pre = attn_pre_bwd(dyb, S["os"], S["lses"], ones)
    dqkv = []
    for g, (_, d) in enumerate(GROUPS):
        qf, kf, vf = S["folded"][g]
        dqf, dkf, dvf = attn_bwd(_flat(qf), _flat(kf), _flat(vf), _flat(pre[g]), _flat(S["lses"][g]), _flat(pre[3 + g]),
                                 g, T // d // BLK)
        shp = (d, T // d, AO)
        dqkv.append(unfold_rope_bwd(dqf.reshape(shp), dkf.reshape(shp), dvf.reshape(shp), cos_t, sin_t, g, d))
    parts = (dgates, dbch, *dqkv, duv)
    G["w_in"] = dw_in(transpose_cast(S["x0"]), parts)
    bias = jnp.zeros((1, D), f32)
    if on_grads is not None:
        bias = bias + on_grads({"w_in": G["w_in"]})
    dx0 = dx_in(dr1, parts, W["w_in"], bias)
    started = on_grads({"dx": dx0}) if on_grads is not None else None
    return dx0, G, started


def prep_layer_weights(Wl):
    W = dict(Wl)
    tril = jnp.tril(jnp.ones((BLK, BLK), f32))
    W["wst"] = (Wl["w_s"] * tril[None]).astype(MX)
    W["bsx"] = jnp.broadcast_to(Wl["b_s"][:, :, None], (8, BLK, BLK))
    for n in ("gmlp_ln_g", "gmlp_ln_b", "ln1_g", "ln1_b", "ln2_g", "ln2_b"):
        W[n] = Wl[n].reshape(1, D)
    W["in_bias"] = jnp.zeros((1, NIN), f32) + Wl.get("after", 0.0)
    return W


def local_step(x, positions, target, layers, on_grads=None):
    cos_t, sin_t = rope_tables(positions)
    Ws, saved = [], []
    h = x
    for Wl in layers:
        h, S, W = layer_fwd(h, prep_layer_weights(Wl(h) if callable(Wl) else Wl), cos_t, sin_t)
        Ws.append(W)
        saved.append(S)
    lsum, dh = loss_grad(h, target)
    if on_grads is not None:
        on_grads(len(Ws), {"loss": lsum})
    grads = [None] * len(Ws)
    started = None
    for l in reversed(range(len(Ws))):
        W = Ws[l]
        if started is not None:
            W = dict(W, ln2_g=W["ln2_g"] + started)
        hook = functools.partial(on_grads, l) if on_grads is not None else None
        dh, grads[l], started = layer_bwd(dh, saved[l], W, cos_t, sin_t, hook)
    return lsum, dh, grads


MESH = pl.DeviceIdType.MESH
ANY = pl.BlockSpec(memory_space=pl.ANY)
BIG = ("w_in", "w_gate", "w_up", "w_down", "p_a", "p_b", "p_c", "w_o")
NBIG = len(BIG)


def _place():
    x, y, c = lax.axis_index("x"), lax.axis_index("y"), lax.axis_index("c")
    return x, y, c, 2 * x + y


def _rcopy(src, dst, send, recv, dev):
    return pltpu.make_async_remote_copy(src_ref=src, dst_ref=dst, send_sem=send, recv_sem=recv, device_id=dev,
                                        device_id_type=MESH)


def _cols(ref, k, width):
    start = k * width if isinstance(k, int) else pl.multiple_of(k * width, LANES)
    return ref.at[:, pl.ds(start, width)]


CHUNK_BYTES = 1 << 20


def _pieces(shape, itemsize, nbytes=CHUNK_BYTES):
    rows, cols = shape[-2], shape[-1]
    per = max(16, nbytes // (cols * itemsize) // 16 * 16)
    out = []
    for lead in (range(shape[0]) if len(shape) == 3 else (None,)):
        for r in range(0, rows, per):
            sl = (pl.ds(r, min(per, rows - r)), slice(None))
            out.append(sl if lead is None else (lead,) + sl)
    return out


def _start_pieces(src, dst, make, nbytes=CHUNK_BYTES):
    for idx in _pieces(src.shape, jnp.dtype(src.dtype).itemsize, nbytes):
        make(src.at[idx], dst.at[idx]).start()


def gather_halves(shards):
    n = len(shards)

    def body(*refs):
        srcs, dsts = refs[:n], refs[n:2 * n]
        send, recv, own_send, own_recv = refs[2 * n:]
        x, y, c, k = _place()
        sib = (x, y, 1 - c)
        chips = [(1 - x, y), (x, 1 - y), (1 - x, 1 - y)]

        def slot(a, layer, pos):
            if a == 0:
                return _cols(dsts[0].at[layer], pos, WIN_SHARD)
            return dsts[a].at[pos, layer]

        def ici(a, j, src, dst):
            return _rcopy(src, dst, send.at[a, j], recv.at[a, j], (*chips[j], c))

        def d2d(a, j, src, dst):
            return _rcopy(src, dst, send.at[a, 3 + j], recv.at[a, 3 + j], sib)

        def own(a, layer, src, dst):
            return _rcopy(src, dst, own_send.at[a, layer], own_recv.at[a, layer], sib)

        for a in range(n):
            for j in range(3):
                _start_pieces(srcs[a].at[c], slot(a, c, k), functools.partial(ici, a, j))
        for a in range(n):
            for layer in range(DEPTH):
                _start_pieces(srcs[a].at[layer], slot(a, layer, k), functools.partial(own, a, layer))
        for a in range(n):
            for j, (cx, cy) in enumerate(chips):
                landed = slot(a, c, 2 * cx + cy)
                ici(a, j, landed, landed).wait_recv()
                _start_pieces(landed, landed, functools.partial(d2d, a, j))
        for a in range(n):
            for j, (cx, cy) in enumerate(chips):
                passed = slot(a, 1 - c, 2 * cx + cy)
                d2d(a, j, passed, passed).wait_recv()
                landed = slot(a, c, 2 * cx + cy)
                d2d(a, j, landed, landed).wait_send()
                ici(a, j, srcs[a].at[c], slot(a, c, k)).wait_send()
            for layer in range(DEPTH):
                own(a, layer, srcs[a].at[layer], slot(a, layer, k)).wait()

    outs = [jax.ShapeDtypeStruct((2, shards[0].shape[1], NIN), shards[0].dtype)]
    outs += [jax.ShapeDtypeStruct((NCHIP,) + s.shape, s.dtype) for s in shards[1:]]
    return pl.pallas_call(
        body, name="gather_halves", in_specs=[ANY] * n, out_specs=[ANY] * n, out_shape=outs,
        scratch_shapes=[pltpu.SemaphoreType.DMA((n, 6)), pltpu.SemaphoreType.DMA((n, 6)),
                        pltpu.SemaphoreType.DMA((n, DEPTH)), pltpu.SemaphoreType.DMA((n, DEPTH))],
    )(*shards)


def _gather_slot(dst, pos):
    return _cols(dst, pos, WIN_SHARD) if len(dst.shape) == 2 else dst.at[pos]


def _gather_copy(a, j, src, dst, send, recv, dev):
    return _rcopy(src, dst, send.at[a * NCHIP + j], recv.at[a * NCHIP + j], dev)


def gather_start(tag, shards, after):
    n = len(shards)

    def body(*refs):
        srcs, dsts = refs[:n], refs[n:2 * n]
        send, recv = refs[2 * n + len(after)], refs[2 * n + len(after) + 1]
        token = refs[-1]
        x, y, c, k = _place()
        peers = [(1 - x, y, c), (x, 1 - y, c), (1 - x, 1 - y, c), (x, y, 1 - c)]
        for a in range(n):
            for j, dev in enumerate(peers):
                _start_pieces(srcs[a], _gather_slot(dsts[a], k),
                              lambda s, d, a=a, j=j, dev=dev: _gather_copy(a, j, s, d, send, recv, dev))
        token[...] = jnp.zeros_like(token)

    gathered = [lax.empty((D, NIN) if s.shape == (D, WIN_SHARD) else (NCHIP,) + s.shape, s.dtype) for s in shards]
    ops = [pltpu.with_memory_space_constraint(v, pltpu.HBM) for v in list(shards) + gathered]
    sem = pltpu.SemaphoreType.DMA((n * NCHIP,))
    res = pl.pallas_call(
        body, name=f"gather_start{tag}", in_specs=[HBM] * (2 * n) + [ANY] * len(after),
        out_specs=[SEMS, SEMS] + [HBM] * (2 * n) + [pl.BlockSpec(memory_space=pltpu.VMEM)],
        out_shape=[sem, sem] + [pltpu.HBM(v.shape, v.dtype) for v in ops] + [jax.ShapeDtypeStruct((8, LANES), f32)],
        input_output_aliases={i: 2 + i for i in range(2 * n)},
        compiler_params=pltpu.CompilerParams(has_side_effects=EFFECT),
    )(*ops, *after)
    return res[0], res[1], res[2:2 + n], res[2 + n:2 + 2 * n], res[-1]


def gather_wait(tag, send, recv, shards, gathered, after):
    n = len(shards)

    def body(*refs):
        srcs, dsts = refs[:n], refs[n:2 * n]
        send_r, recv_r = refs[2 * n], refs[2 * n + 1]
        x, y, c, k = _place()
        peers = [(1 - x, y, c), (x, 1 - y, c), (1 - x, 1 - y, c), (x, y, 1 - c)]
        for a in range(n):
            for j, dev in enumerate(peers):
                _gather_copy(a, j, srcs[a], _gather_slot(dsts[a], k), send_r, recv_r, dev).wait_send()
                pos = 2 * dev[0] + dev[1]
                _gather_copy(a, j, srcs[a], _gather_slot(dsts[a], pos), send_r, recv_r, dev).wait_recv()

    ops = list(shards) + list(gathered)
    res = pl.pallas_call(
        body, name=f"gather_wait{tag}", in_specs=[HBM] * (2 * n) + [SEMS, SEMS] + [ANY] * len(after),
        out_specs=[HBM] * (2 * n), out_shape=[pltpu.HBM(v.shape, v.dtype) for v in ops],
        input_output_aliases={i: i for i in range(2 * n)},
        compiler_params=pltpu.CompilerParams(has_side_effects=EFFECT),
    )(*ops, send, recv, *after)
    return res[n:]


def _half(ref, h):
    rows = ref.shape[-2] // 2
    start = pl.multiple_of(h * rows, 16)
    if len(ref.shape) == 2:
        return ref.at[pl.ds(start, rows), :]
    return ref.at[:, pl.ds(start, rows), :]


HBM = pl.BlockSpec(memory_space=pltpu.HBM)
SEMS = pl.BlockSpec(memory_space=pltpu.SEMAPHORE)
EFFECT = pltpu.SideEffectType.DATAFLOW_SIDE_EFFECTING


def rs_pair_start(tag, grads):
    n = len(grads)

    def body(*refs):
        g, theirs = refs[:n], refs[n:2 * n]
        send, recv = refs[2 * n], refs[2 * n + 1]
        x, y, c, _ = _place()
        for a in range(n):
            _start_pieces(_half(g[a], 1 - c), theirs[a],
                          lambda s, d, a=a: _rcopy(s, d, send.at[a], recv.at[a], (x, y, 1 - c)))
        refs[-1][...] = jnp.zeros_like(refs[-1])

    lands = [lax.empty(g.shape[:-2] + (g.shape[-2] // 2, g.shape[-1]), g.dtype) for g in grads]
    ops = [pltpu.with_memory_space_constraint(v, pltpu.HBM) for v in list(grads) + lands]
    sem = pltpu.SemaphoreType.DMA((n,))
    res = pl.pallas_call(
        body, name=f"rs_pair_start{tag}", in_specs=[HBM] * (2 * n),
        out_specs=[SEMS, SEMS] + [HBM] * (2 * n) + [pl.BlockSpec(memory_space=pltpu.VMEM)],
        out_shape=[sem, sem] + [pltpu.HBM(v.shape, v.dtype) for v in ops] + [jax.ShapeDtypeStruct((8, LANES), f32)],
        input_output_aliases={i: 2 + i for i in range(2 * n)},
        compiler_params=pltpu.CompilerParams(has_side_effects=EFFECT),
    )(*ops)
    return res[0], res[1], res[2:2 + n], res[2 + n:2 + 2 * n], res[-1]


def rs_pair_wait(tag, send, recv, grads, theirs, after):
    n = len(grads)

    def body(*refs):
        g, land = refs[:n], refs[n:2 * n]
        send_r, recv_r = refs[2 * n], refs[2 * n + 1]
        x, y, c, _ = _place()
        for a in range(n):
            cp = _rcopy(_half(g[a], 1 - c), land[a], send_r.at[a], recv_r.at[a], (x, y, 1 - c))
            cp.wait_send()
            cp.wait_recv()

    ops = list(grads) + list(theirs)
    res = pl.pallas_call(
        body, name=f"rs_pair_wait{tag}", in_specs=[HBM] * (2 * n) + [SEMS, SEMS] + [ANY] * len(after),
        out_specs=[HBM] * (2 * n), out_shape=[pltpu.HBM(v.shape, v.dtype) for v in ops],
        input_output_aliases={i: i for i in range(2 * n)},
        compiler_params=pltpu.CompilerParams(has_side_effects=EFFECT),
    )(*ops, send, recv, *after)
    return res[:n], res[n:]


def _chip_piece(ref, k):
    return _cols(ref, k, WIN_SHARD) if len(ref.shape) == 2 else ref.at[k]


def _chip_copy(a, k, src, dst, send, recv, me, c):
    return _rcopy(src, dst, send.at[a * NCHIP + k], recv.at[a * NCHIP + me], (k // 2, k % 2, c))


def rs_chips_start(tag, sums):
    n = len(sums)

    def pshape(s):
        return (NCHIP, s[0], WIN_SHARD) if len(s) == 2 else s

    def body(*refs):
        s, land = refs[:n], refs[n:2 * n]
        send, recv = refs[2 * n], refs[2 * n + 1]
        token = refs[-1]
        x, y, c, me = _place()
        for k in range(NCHIP):
            @pl.when(me != k)
            def _():
                for a in range(n):
                    _start_pieces(_chip_piece(s[a], k), land[a].at[me],
                                  lambda src, dst, a=a: _chip_copy(a, k, src, dst, send, recv, me, c))
        token[...] = jnp.zeros_like(token)

    lands = [lax.empty(pshape(v.shape), v.dtype) for v in sums]
    ops = [pltpu.with_memory_space_constraint(v, pltpu.HBM) for v in list(sums) + lands]
    sem = pltpu.SemaphoreType.DMA((n * NCHIP,))
    res = pl.pallas_call(
        body, name=f"rs_chips_start{tag}", in_specs=[HBM] * (2 * n),
        out_specs=[SEMS, SEMS] + [HBM] * (2 * n) + [pl.BlockSpec(memory_space=pltpu.VMEM)],
        out_shape=[sem, sem] + [pltpu.HBM(v.shape, v.dtype) for v in ops] + [jax.ShapeDtypeStruct((8, LANES), f32)],
        input_output_aliases={i: 2 + i for i in range(2 * n)},
        compiler_params=pltpu.CompilerParams(has_side_effects=EFFECT),
    )(*ops)
    return res[0], res[1], res[2:2 + n], res[2 + n:2 + 2 * n], res[-1]


def rs_chips_wait(tag, send, recv, sums, lands, after):
    n = len(sums)

    def body(*refs):
        s, land = refs[:n], refs[n:2 * n]
        send_r, recv_r = refs[2 * n], refs[2 * n + 1]
        x, y, c, me = _place()
        for k in range(NCHIP):
            @pl.when(me != k)
            def _():
                for a in range(n):
                    piece = _chip_piece(s[a], k)
                    _chip_copy(a, k, piece, land[a].at[me], send_r, recv_r, me, c).wait_send()
                    _rcopy(piece, land[a].at[k], send_r.at[a * NCHIP + k], recv_r.at[a * NCHIP + k],
                           (k // 2, k % 2, c)).wait_recv()

    ops = list(sums) + list(lands)
    res = pl.pallas_call(
        body, name=f"rs_chips_wait{tag}", in_specs=[HBM] * (2 * n) + [SEMS, SEMS] + [ANY] * len(after),
        out_specs=[HBM] * (2 * n), out_shape=[pltpu.HBM(v.shape, v.dtype) for v in ops],
        input_output_aliases={i: i for i in range(2 * n)},
        compiler_params=pltpu.CompilerParams(has_side_effects=EFFECT),
    )(*ops, send, recv, *after)
    return res[:n], res[n:]


def rs_join(tag, halves):
    n = len(halves)

    def body(*refs):
        h, other = refs[:n], refs[n:2 * n]
        send, recv = refs[2 * n:]
        x, y, c, _ = _place()

        def give(a, s, d):
            return _rcopy(s, d, send.at[a], recv.at[a], (x, y, 1 - c))

        for a in range(n):
            _start_pieces(h[a], other[a], functools.partial(give, a))
        for a in range(n):
            give(a, h[a], other[a]).wait()

    outs = [jax.ShapeDtypeStruct(v.shape, v.dtype) for v in halves]
    return pl.pallas_call(
        body, name=f"rs_join{tag}", in_specs=[ANY] * n, out_specs=[ANY] * n, out_shape=outs,
        scratch_shapes=[pltpu.SemaphoreType.DMA((n,))] * 2,
    )(*halves)


def _row_tile(rows, cols, itemsize=4, target=2 << 20):
    best = 8
    for t in range(8, rows + 1, 8):
        if rows % t == 0 and t * cols * itemsize <= target:
            best = t
    return best


GRAD_WIRE = jnp.bfloat16


def add_half(name, g, t, c):
    cols, half = t.shape[-1], t.shape[-2]
    nblk = 1 if t.ndim == 2 else t.shape[0]
    tr = _row_tile(half, cols)
    per = half // tr

    def body(c_ref, g_r, t_r, o_r):
        o_r[...] = (g_r[...] + t_r[...]).astype(o_r.dtype)

    tile_t = pl.BlockSpec((tr, cols), lambda i, c_ref: (i, 0))
    tile_g = pl.BlockSpec((tr, cols), lambda i, c_ref: ((i // per) * 2 * per + c_ref[0] * per + i % per, 0))
    out = pl.pallas_call(
        body, name=name, out_shape=jax.ShapeDtypeStruct((nblk * half, cols), GRAD_WIRE),
        grid_spec=pltpu.PrefetchScalarGridSpec(num_scalar_prefetch=1, grid=(nblk * per,), in_specs=[tile_g, tile_t],
                                               out_specs=tile_t),
        compiler_params=_cp(("parallel",)),
    )(c.reshape(1).astype(jnp.int32), g.reshape(nblk * 2 * half, cols), t.reshape(nblk * half, cols))
    return out.reshape(t.shape)


def add_chips(name, land, own):
    _, rows, cols = land.shape
    tr = _row_tile(rows, cols, target=1 << 20)

    def body(land_r, own_r, o_r):
        me = 2 * lax.axis_index("x") + lax.axis_index("y")
        for k in range(NCHIP):
            @pl.when(me == k)
            def _():
                acc = None
                for j in range(NCHIP):
                    t = (own_r[...] if j == k else land_r[j]).astype(f32)
                    acc = t if acc is None else acc + t
                o_r[...] = acc

    tile = pl.BlockSpec((tr, cols), lambda i: (i, 0))
    return pl.pallas_call(
        body, name=name, grid=(rows // tr,), in_specs=[pl.BlockSpec((NCHIP, tr, cols), lambda i: (0, i, 0)), tile],
        out_specs=tile, out_shape=jax.ShapeDtypeStruct((rows, cols), f32), compiler_params=_cp(("parallel",)),
    )(land, own)


def reduce_scatter_pair(tag, G):
    names = tuple(G)
    grads = [G[n] if G[n].ndim == 3 or n == "w_in" else G[n].reshape(NCHIP, D // NCHIP, D) for n in names]
    send, recv, grads, theirs, token = rs_pair_start(tag, grads)
    return (tag, names, send, recv, grads, theirs), token[0, 0]


def reduce_scatter_chips(state, after):
    c = lax.axis_index("c")
    tag, names, send, recv, grads, theirs = state
    grads, theirs = rs_pair_wait(tag, send, recv, grads, theirs, after)
    sums = [add_half(f"rs_add_pair{tag}_{n}", g, t, c) for n, g, t in zip(names, grads, theirs)]
    send, recv, sums, lands, token = rs_chips_start(tag, sums)
    return (tag, names, send, recv, sums, lands), token[0, 0]


def reduce_scatter_finish(state, after):
    me = 2 * lax.axis_index("x") + lax.axis_index("y")
    tag, names, send, recv, sums, lands = state
    sums, landed = rs_chips_wait(tag, send, recv, sums, lands, after)
    halves = []
    for n, s, v in zip(names, sums, landed):
        own = lax.dynamic_slice_in_dim(s, me * WIN_SHARD, WIN_SHARD, axis=1) if s.ndim == 2 else \
            lax.dynamic_index_in_dim(s, me, 0, keepdims=False)
        halves.append(add_chips(f"rs_add_chips{tag}_{n}", v, own))
    return dict(zip(names, zip(halves, rs_join(tag, halves))))


NDEV = 8


def _small_copy(r, src, dst, send, recv, x, y, c):
    return _rcopy(src, dst, send.at[r - 1], recv.at[r - 1], (x ^ (r >> 2), y ^ ((r >> 1) & 1), c ^ (r & 1)))


def small_start(pack):
    def body(p, land, send, recv, p_thru, land_thru, token):
        x, y, c, _ = _place()
        me = 4 * x + 2 * y + c
        for r in range(1, NDEV):
            _start_pieces(p, land.at[me], lambda s, d, r=r: _small_copy(r, s, d, send, recv, x, y, c), 128 << 10)
        token[...] = jnp.zeros_like(token)

    ops = [pltpu.with_memory_space_constraint(v, pltpu.HBM) for v in (pack, lax.empty((NDEV,) + pack.shape, f32))]
    sem = pltpu.SemaphoreType.DMA((NDEV - 1,))
    return pl.pallas_call(
        body, name="small_start", in_specs=[HBM, HBM],
        out_specs=[SEMS, SEMS, HBM, HBM, pl.BlockSpec(memory_space=pltpu.VMEM)],
        out_shape=[sem, sem] + [pltpu.HBM(v.shape, v.dtype) for v in ops] + [jax.ShapeDtypeStruct((8, LANES), f32)],
        input_output_aliases={0: 2, 1: 3}, compiler_params=pltpu.CompilerParams(has_side_effects=EFFECT),
    )(*ops)


def small_wait(send, recv, pack, land, after):
    def body(p, land_r, send_r, recv_r, *rest):
        x, y, c, _ = _place()
        me = 4 * x + 2 * y + c
        for r in range(1, NDEV):
            _small_copy(r, p, land_r.at[me], send_r, recv_r, x, y, c).wait_send()
            src = 4 * (x ^ (r >> 2)) + 2 * (y ^ ((r >> 1) & 1)) + (c ^ (r & 1))
            _small_copy(r, p, land_r.at[src], send_r, recv_r, x, y, c).wait_recv()

    return pl.pallas_call(
        body, name="small_wait", in_specs=[HBM, HBM, SEMS, SEMS] + [ANY] * len(after), out_specs=[HBM, HBM],
        out_shape=[pltpu.HBM(pack.shape, f32), pltpu.HBM(land.shape, f32)], input_output_aliases={0: 0, 1: 1},
        compiler_params=pltpu.CompilerParams(has_side_effects=EFFECT),
    )(pack, land, send, recv, *after)


def small_sum(land, pack):
    def body(land_r, p_r, o_r):
        me = 4 * lax.axis_index("x") + 2 * lax.axis_index("y") + lax.axis_index("c")
        for k in range(NDEV):
            @pl.when(me == k)
            def _():
                acc = None
                for d in range(NDEV):
                    t = p_r[...] if d == k else land_r[d]
                    acc = t if acc is None else acc + t
                o_r[...] = acc

    vm = pl.BlockSpec(memory_space=pltpu.VMEM)
    return pl.pallas_call(
        body, name="small_sum", in_specs=[vm, vm], out_specs=vm, out_shape=jax.ShapeDtypeStruct(pack.shape, f32),
        compiler_params=pltpu.CompilerParams(vmem_limit_bytes=40 << 20),
    )(land, pack)


def _adamw_math(w, g, m, v):
    m = ADAM_B1 * m + (1.0 - ADAM_B1) * g
    v = ADAM_B2 * v + (1.0 - ADAM_B2) * (g * g)
    m_hat = m / (1.0 - ADAM_B1 ** ADAM_STEP)
    v_hat = v / (1.0 - ADAM_B2 ** ADAM_STEP)
    return -ADAM_LR * (m_hat / (jnp.sqrt(v_hat) + ADAM_EPS) + ADAM_WD * w), m, v


def adamw_big(name, halves, w, m, v):
    _, R, C = w.shape
    tr = _row_tile(R // 2, C, target=1 << 20)
    nt = R // 2 // tr

    def body(a0, b0, a1, b1, w_r, m_r, v_r, g_o, d_o, m_o, v_o):
        mine = pl.program_id(1) == lax.axis_index("c")
        g = jnp.where(pl.program_id(0) == 0, jnp.where(mine, a0[...], b0[...]), jnp.where(mine, a1[...], b1[...]))
        g_o[...] = g
        d_o[...], m_o[...], v_o[...] = _adamw_math(w_r[...], g, m_r[...], v_r[...])

    stk = pl.BlockSpec((None, tr, C), lambda l, h, i: (l, h * nt + i, 0))
    lay0 = pl.BlockSpec((tr, C), lambda l, h, i: (jnp.where(l == 0, i, nt - 1), 0))
    lay1 = pl.BlockSpec((tr, C), lambda l, h, i: (jnp.where(l == 0, 0, i), 0))
    return pl.pallas_call(
        body, name=name, grid=(DEPTH, 2, nt),
        in_specs=[lay0, lay0, lay1, lay1, stk, stk, stk],
        out_specs=[stk] * 4, out_shape=[jax.ShapeDtypeStruct(w.shape, f32)] * 4,
        compiler_params=_cp(("arbitrary", "arbitrary", "arbitrary")),
    )(*halves[0], *halves[1], w, m, v)


def adamw_small(name, g, w, m, v):
    def body(g_r, w_r, m_r, v_r, d_o, m_o, v_o):
        d_o[...], m_o[...], v_o[...] = _adamw_math(w_r[...], g_r[...], m_r[...], v_r[...])

    return pl.pallas_call(body, name=name, out_shape=[jax.ShapeDtypeStruct(w.shape, f32)] * 3)(g, w, m, v)


WEIGHTS = ("w_in", "conv_w", "gmlp_ln_g", "gmlp_ln_b", "w_s", "b_s", "p_a", "p_b", "p_c", "w_o", "ln1_g", "ln1_b",
           "w_gate", "w_up", "w_down", "ln2_g", "ln2_b")
VECS = ("ln1_g", "ln1_b", "ln2_g", "ln2_b", "gmlp_ln_g", "gmlp_ln_b")
ROWS_VEC, ROWS_BS, ROWS_WS, ROWS_CONV = D // LANES, 8, 8 * BLK, 3 * D // LANES
ROWS_LAYER = len(VECS) * ROWS_VEC + ROWS_BS + ROWS_WS + ROWS_CONV


def _pack_small(per_layer, tail):
    parts = []
    for P in per_layer:
        parts += [P[n].reshape(ROWS_VEC, LANES) for n in VECS]
        parts += [P["b_s"].reshape(ROWS_BS, LANES), P["w_s"].reshape(ROWS_WS, LANES), P["conv_w"].reshape(ROWS_CONV, LANES)]
    return jnp.concatenate(parts + [tail], axis=0)


def _unpack_small(pack):
    out = []
    for l in range(DEPTH):
        r = l * ROWS_LAYER
        P = {}
        for n in VECS:
            P[n] = pack[r:r + ROWS_VEC].reshape(D)
            r += ROWS_VEC
        P["b_s"] = pack[r:r + ROWS_BS].reshape(8, BLK)
        r += ROWS_BS
        P["w_s"] = pack[r:r + ROWS_WS].reshape(8, BLK, BLK)
        r += ROWS_WS
        P["conv_w"] = pack[r:r + ROWS_CONV].reshape(3, D)
        out.append(P)
    return out, pack[DEPTH * ROWS_LAYER:]


def kernel(x, positions, w_in, conv_w, gmlp_ln_g, gmlp_ln_b, w_s, b_s, p_a, p_b, p_c, w_o, ln1_g, ln1_b, w_gate, w_up, w_down, ln2_g, ln2_b, loss_target, m_w_in, m_conv_w, m_gmlp_ln_g, m_gmlp_ln_b, m_w_s, m_b_s, m_p_a, m_p_b, m_p_c, m_w_o, m_ln1_g, m_ln1_b, m_w_gate, m_w_up, m_w_down, m_ln2_g, m_ln2_b, v_w_in, v_conv_w, v_gmlp_ln_g, v_gmlp_ln_b, v_w_s, v_b_s, v_p_a, v_p_b, v_p_c, v_w_o, v_ln1_g, v_ln1_b, v_w_gate, v_w_up, v_w_down, v_ln2_g, v_ln2_b):
    Wt = dict(w_in=w_in, conv_w=conv_w, gmlp_ln_g=gmlp_ln_g, gmlp_ln_b=gmlp_ln_b, w_s=w_s, b_s=b_s, p_a=p_a, p_b=p_b,
              p_c=p_c, w_o=w_o, ln1_g=ln1_g, ln1_b=ln1_b, w_gate=w_gate, w_up=w_up, w_down=w_down, ln2_g=ln2_g, ln2_b=ln2_b)
    Mt = dict(w_in=m_w_in, conv_w=m_conv_w, gmlp_ln_g=m_gmlp_ln_g, gmlp_ln_b=m_gmlp_ln_b, w_s=m_w_s, b_s=m_b_s, p_a=m_p_a,
              p_b=m_p_b, p_c=m_p_c, w_o=m_w_o, ln1_g=m_ln1_g, ln1_b=m_ln1_b, w_gate=m_w_gate, w_up=m_w_up,
              w_down=m_w_down, ln2_g=m_ln2_g, ln2_b=m_ln2_b)
    Vt = dict(w_in=v_w_in, conv_w=v_conv_w, gmlp_ln_g=v_gmlp_ln_g, gmlp_ln_b=v_gmlp_ln_b, w_s=v_w_s, b_s=v_b_s, p_a=v_p_a,
              p_b=v_p_b, p_c=v_p_c, w_o=v_w_o, ln1_g=v_ln1_g, ln1_b=v_ln1_b, w_gate=v_w_gate, w_up=v_w_up,
              w_down=v_w_down, ln2_g=v_ln2_g, ln2_b=v_ln2_b)
    chip = 2 * lax.axis_index("x") + lax.axis_index("y")
    cw = D // NCHIP

    def gathered_weights(names, arrays):
        Wl = dict(zip(names, arrays))
        for n in ("p_a", "p_c", "w_o"):
            Wl[n] = Wl[n].reshape(D, D)
        return Wl

    def small_weights(l, conv_all):
        Wl = {n: Wt[n][l] for n in VECS + ("w_s", "b_s")}
        Wl["conv_w"] = conv_all[:, l].transpose(1, 0, 2).reshape(3, D)
        return Wl

    w_in0, conv_all = gather_halves([Wt["w_in"][0].astype(MX).reshape(2, D // 2, WIN_SHARD), conv_w])
    rest = BIG[1:]
    *late0, coming0 = gather_start("0", [Wt[n][0].astype(MX) for n in rest], [conv_all])
    *late1, coming1 = gather_start("1", [Wt[n][1].astype(MX) for n in BIG], [conv_all, coming0])
    W0 = dict(small_weights(0, conv_all), w_in=w_in0.reshape(D, NIN), after=coming1[0, 0],
              late=lambda y: gathered_weights(rest, gather_wait("0", *late0, [y])))

    def W1(h):
        return dict(small_weights(1, conv_all), **gathered_weights(BIG, gather_wait("1", *late1, [h])))

    layers = [W0, W1]

    rs_state, rs_started, held = {}, {}, {}

    def start_exchange(l, g):
        if "loss" in g:
            held[l] = g
            return None
        if "conv_w" in g:
            held[l] = g
            rs_state[(l, False)], started = reduce_scatter_chips(rs_state[(l, False)], [g["w_s"], g["conv_w"]])
            if l == 0:
                pack = _pack_small([held[j] for j in range(DEPTH)], held[DEPTH]["loss"])
                *held["small"], token = small_start(pack)
                started = started + token[0, 0]
            return started
        if "dx" in g:
            rs_state[(l, True)], rs_started[(l, True)] = reduce_scatter_chips(rs_state[(l, True)], [g["dx"]])
            return rs_started[(l, True)]
        key = (l, "w_in" in g)
        rs_state[key], started = reduce_scatter_pair(f"{l}{'b' if key[1] else 'a'}", g)
        return started

    _, grad_x, _ = local_step(x[0], positions[0], loss_target[0], layers, start_exchange)

    last = jnp.zeros((8, LANES), f32) + rs_started[(0, True)]
    behind = [grad_x, last]
    red = [dict() for _ in range(DEPTH)]
    for key in ((1, False), (1, True), (0, False)):
        red[key[0]].update(reduce_scatter_finish(rs_state[key], behind))
    small, tail = _unpack_small(small_sum(*reversed(small_wait(*held["small"], behind))))
    loss = tail[0, 0]

    G, DW, NM, NV = {}, {}, {}, {}
    zc = jnp.zeros((3, D), f32)
    wp = _pack_small([{**{n: Wt[n][l] for n in VECS + ("b_s", "w_s")}, "conv_w": zc} for l in range(DEPTH)], jnp.zeros((8, LANES), f32))
    mp = _pack_small([{**{n: Mt[n][l] for n in VECS + ("b_s", "w_s")}, "conv_w": zc} for l in range(DEPTH)], jnp.zeros((8, LANES), f32))
    vp = _pack_small([{**{n: Vt[n][l] for n in VECS + ("b_s", "w_s")}, "conv_w": zc} for l in range(DEPTH)], jnp.ones((8, LANES), f32))
    gp = _pack_small(small, jnp.zeros((8, LANES), f32))
    outs = [_unpack_small(a)[0] for a in adamw_small("adamw_small", gp, wp, mp, vp)]
    for n in VECS + ("b_s", "w_s"):
        G[n] = jnp.stack([small[l][n] for l in range(DEPTH)])
        DW[n], NM[n], NV[n] = (jnp.stack([o[l][n] for l in range(DEPTH)]) for o in outs)
    gconv = jnp.stack([lax.dynamic_slice(small[l]["conv_w"], (0, chip * cw), (3, cw)) for l in range(DEPTH)])
    G["conv_w"] = gconv
    flat = lambda a: a.reshape(DEPTH * 3, cw)
    d, m2, v2 = adamw_small("adamw_conv", flat(gconv), flat(conv_w), flat(m_conv_w), flat(v_conv_w))
    DW["conv_w"], NM["conv_w"], NV["conv_w"] = (a.reshape(DEPTH, 3, cw) for a in (d, m2, v2))

    updated = {}
    for n in BIG[1:]:
        tr = (lambda a: jnp.swapaxes(a, 1, 2)) if n in ("w_gate", "w_up") else (lambda a: a)
        updated[n] = adamw_big("adamw_" + n, (red[0][n], red[1][n]), tr(Wt[n]), tr(Mt[n]), tr(Vt[n]))
        G[n], DW[n], NM[n], NV[n] = map(tr, updated[n])
    done = [d, DW["ln2_b"], red[1]["w_in"][1]] + [updated[n][1] for n in BIG[1:]]
    red[0].update(reduce_scatter_finish(rs_state[(0, True)], done))
    G["w_in"], DW["w_in"], NM["w_in"], NV["w_in"] = adamw_big(
        "adamw_w_in", (red[0]["w_in"], red[1]["w_in"]), Wt["w_in"], Mt["w_in"], Vt["w_in"])

    return (loss, grad_x[None], *[G[n] for n in WEIGHTS], *[DW[n] for n in WEIGHTS], *[NM[n] for n in WEIGHTS],
            *[NV[n] for n in WEIGHTS])
```

```python
import functools
import math

import jax
import jax.numpy as jnp
from jax import lax
from jax.experimental import pallas as pl
from jax.experimental.pallas import tpu as pltpu

D = 1024
NIN = 12800
DFF = 2816
NCHIP = 4
FB = DFF // NCHIP
WIN_SHARD = NIN // NCHIP
DEPTH = 2
GROUPS = ((128, 1), (512, 4), (2048, 16))
HD = 64
BLK = 128
AO = 512
ALPHA = (2 * DEPTH) ** 0.25
EPS = 1e-5
ROPE_THETA = 10000.0
LANES = 128
NEG = -1e30

C_GATES, C_BCH, C_QKV, C_UV = 0, 3 * D, 6 * D, 6 * D + 9 * AO

MX = jnp.bfloat16
ACT = jnp.bfloat16

ADAM_LR, ADAM_B1, ADAM_B2, ADAM_EPS, ADAM_WD, ADAM_STEP = 0.001, 0.9, 0.999, 1e-08, 0.01, 10

f32 = jnp.float32
NT = (((1,), (1,)), ((), ()))
TN = (((0,), (0,)), ((), ()))


def _cp(sem, vmem_mb=48):
    return pltpu.CompilerParams(dimension_semantics=sem, vmem_limit_bytes=vmem_mb << 20)


def _dot(a, b, dims=None):
    if dims is None:
        return jnp.dot(a, b, preferred_element_type=f32)
    return lax.dot_general(a, b, dims, preferred_element_type=f32)


def _ln_stats(r):
    mu = jnp.mean(r, axis=-1, keepdims=True)
    xc = r - mu
    var = jnp.mean(xc * xc, axis=-1, keepdims=True)
    rstd = lax.rsqrt(var + EPS)
    return xc * rstd, rstd


def _ln_bwd(dy, xhat, rstd, g):
    dxh = dy * g
    return rstd * (dxh - jnp.mean(dxh, axis=-1, keepdims=True) - xhat * jnp.mean(dxh * xhat, axis=-1, keepdims=True))


def _gelu(x):
    return 0.5 * x * (1.0 + lax.erf(x * (1.0 / math.sqrt(2.0))))


def _gelu_and_grad(x):
    cdf = 0.5 * (1.0 + lax.erf(x * (1.0 / math.sqrt(2.0))))
    return x * cdf, cdf + x * jnp.exp(-0.5 * x * x) * (1.0 / math.sqrt(2.0 * math.pi))


def _sigmoid(x):
    return 0.5 * jnp.tanh(0.5 * x) + 0.5


def _acc_rows(o_ref, first, val):
    @pl.when(first)
    def _():
        o_ref[...] = jnp.zeros_like(o_ref)
    o_ref[...] += jnp.sum(val, axis=0, keepdims=True)


def mm_in(x, w, bias):
    T = x.shape[0]
    tm, tn = min(2048, T), 1280

    def body(x_ref, w_ref, b_ref, o_ref, xb):
        @pl.when(pl.program_id(1) == 0)
        def _():
            xb[...] = x_ref[...].astype(MX)
        o_ref[...] = (_dot(xb[...], w_ref[...]) + b_ref[...]).astype(o_ref.dtype)

    return pl.pallas_call(
        body, name="mm_in", grid=(T // tm, NIN // tn),
        in_specs=[pl.BlockSpec((tm, D), lambda i, j: (i, 0), pipeline_mode=pl.Buffered(1)),
                  pl.BlockSpec((D, tn), lambda i, j: (0, j)), pl.BlockSpec((1, tn), lambda i, j: (0, j))],
        out_specs=pl.BlockSpec((tm, tn), lambda i, j: (i, j)),
        out_shape=jax.ShapeDtypeStruct((T, NIN), ACT),
        scratch_shapes=[pltpu.VMEM((tm, D), MX)],
        compiler_params=_cp(("parallel", "arbitrary")),
    )(x, w, bias)


HALO = 16
TM_AC = 256


def _uv_specs():
    return [pl.BlockSpec((TM_AC, 512), functools.partial(lambda i, j: (i, j), j=C_UV // 512 + j)) for j in range(4)]


def _gmlp_fwd(up, vp, ws_ref, bs_ref, lg, lb, u=None, gv=None):
    u = _gelu(up) if u is None else u
    xhat, rstd = _ln_stats(_gelu(vp) if gv is None else gv)
    vn = xhat * lg + lb
    vnb = vn.astype(MX)
    rows = []
    for c in range(up.shape[0] // BLK):
        r = slice(c * BLK, (c + 1) * BLK)
        rows.append(jnp.concatenate(
            [_dot(ws_ref[g], vnb[r, g * BLK:(g + 1) * BLK]) + bs_ref[g] for g in range(8)], axis=1))
    return u, vn, xhat, rstd, jnp.concatenate(rows, axis=0)


def mix_ac_fwd(proj, conv_w, wst, bsx, lg, lb):
    T = proj.shape[0]
    tm = TM_AC

    def body(bch, halo, u0, u1, v0, v1, cw, ws, bs, lg_ref, lb_ref, ya, yc, zs):
        i = pl.program_id(0)
        pb = bch[...].astype(f32)
        z = pb[:, D:2 * D] * pb[:, 2 * D:]
        hz = halo[:, :D].astype(f32) * halo[:, D:].astype(f32)
        zs[0:HALO, :] = jnp.where(i > 0, hz, 0.0)
        zs[HALO:HALO + tm, :] = z
        cv = cw[0:1, :] * zs[HALO - 2:HALO - 2 + tm, :] + cw[1:2, :] * zs[HALO - 1:HALO - 1 + tm, :] + cw[2:3, :] * z
        ya[...] = (pb[:, :D] * cv).astype(ya.dtype)
        up = jnp.concatenate([u0[...], u1[...]], axis=1).astype(f32)
        vp = jnp.concatenate([v0[...], v1[...]], axis=1).astype(f32)
        u, _, _, _, sp = _gmlp_fwd(up, vp, ws, bs, lg_ref[...], lb_ref[...])
        yc[...] = (u * sp).astype(yc.dtype)

    full = lambda shape: pl.BlockSpec(shape, lambda i: (0,) * len(shape))
    return pl.pallas_call(
        body, name="mix_ac_fwd", grid=(T // tm,),
        in_specs=[pl.BlockSpec((tm, 3 * D), lambda i: (i, 1)),
                  pl.BlockSpec((HALO, 2 * D), lambda i: (jnp.maximum(i * (tm // HALO) - 1, 0), 2)),
                  *_uv_specs(), full((3, D)), full((8, BLK, BLK)), full((8, BLK, BLK)), full((1, D)), full((1, D))],
        out_specs=[pl.BlockSpec((tm, D), lambda i: (i, 0))] * 2,
        out_shape=[jax.ShapeDtypeStruct((T, D), MX)] * 2,
        scratch_shapes=[pltpu.VMEM((HALO + tm, D), f32)],
        compiler_params=_cp(("parallel",)),
    )(proj, proj, proj, proj, proj, proj, conv_w, wst, bsx, lg, lb)


def _swap_halves(x):
    lane = lax.broadcasted_iota(jnp.int32, x.shape, 1)
    return jnp.where((lane % HD) < HD // 2, pltpu.roll(x, x.shape[1] - HD // 2, 1), pltpu.roll(x, HD // 2, 1))


def _tile4(t):
    return jnp.concatenate([t] * (AO // LANES), axis=1)


TM_FOLD = 512


def _fold_out(nat, x, out_ref, d):
    if d == 1:
        out_ref[0] = x.astype(out_ref.dtype)
        return
    rows = x.shape[0] // d
    for j in range(AO // LANES):
        nat[j] = x[:, j * LANES:(j + 1) * LANES]
    for r in range(d):
        out_ref[r] = jnp.concatenate(
            [nat.at[j][pl.ds(r, rows, stride=d), :] for j in range(AO // LANES)], axis=1).astype(out_ref.dtype)


def _unfold_in(nat, in_ref, d):
    if d == 1:
        return in_ref[0].astype(f32)
    rows = in_ref.shape[1]
    for r in range(d):
        v = in_ref[r].astype(f32)
        for j in range(AO // LANES):
            nat.at[j][pl.ds(r, rows, stride=d), :] = v[:, j * LANES:(j + 1) * LANES]
    return jnp.concatenate([nat[j] for j in range(AO // LANES)], axis=1)


def fold_rope(proj, cos_t, sin_t, g, d):
    T = proj.shape[0]
    tm = TM_FOLD
    rows = tm // d

    def body(x_ref, c_ref, s_ref, q_o, k_o, v_o, nat):
        cos, sin = _tile4(c_ref[...]), _tile4(s_ref[...])
        for part, out, scale in ((0, q_o, HD ** -0.5), (1, k_o, 1.0), (2, v_o, None)):
            x = x_ref[:, part * AO:(part + 1) * AO].astype(f32)
            if scale is not None:
                x = (x * cos + _swap_halves(x) * sin) * scale
            _fold_out(nat, x, out, d)

    fold_spec = pl.BlockSpec((d, rows, AO), lambda i: (0, i, 0))
    return pl.pallas_call(
        body, name=f"fold_rope{g}", grid=(T // tm,),
        in_specs=[pl.BlockSpec((tm, 3 * AO), lambda i: (i, C_QKV // (3 * AO) + g)),
                  pl.BlockSpec((tm, LANES), lambda i: (i, 0)), pl.BlockSpec((tm, LANES), lambda i: (i, 0))],
        out_specs=[fold_spec] * 3,
        out_shape=[jax.ShapeDtypeStruct((d, T // d, AO), MX)] * 3,
        scratch_shapes=[pltpu.VMEM((AO // LANES, tm, LANES), f32)],
        compiler_params=_cp(("parallel",)),
    )(proj, cos_t, sin_t)


def _stack_heads(x):
    lane = lax.broadcasted_iota(jnp.int32, x.shape, 1)
    z = jnp.zeros_like(x)
    return jnp.concatenate([jnp.where(lane < HD, x, z), jnp.where(lane >= HD, x, z)], axis=0)


def _unstack_heads(y):
    lane = lax.broadcasted_iota(jnp.int32, (BLK, LANES), 1)
    return jnp.where(lane < HD, y[:BLK], y[BLK:])


def _window_masks():
    row = lax.broadcasted_iota(jnp.int32, (2 * BLK, 2 * BLK), 0) % BLK
    col = lax.broadcasted_iota(jnp.int32, (2 * BLK, 2 * BLK), 1)
    return (col < BLK) & (col >= row), (col >= BLK) & (col - BLK <= row)


def _two_blocks(ref, b):
    r0 = pl.multiple_of(b * BLK, BLK)
    rp = pl.multiple_of(jnp.maximum(b - 1, 0) * BLK, BLK)
    return jnp.concatenate([ref[pl.ds(rp, BLK), :], ref[pl.ds(r0, BLK), :]], axis=0)


def _merge_masks():
    row = lax.broadcasted_iota(jnp.int32, (2 * BLK, BLK), 0) % BLK
    col = lax.broadcasted_iota(jnp.int32, (2 * BLK, BLK), 1)
    return col <= row, col == row


def attn_fwd(qf, kf, vf, g, nb):
    T = qf.shape[0]

    def body(q_ref, k_ref, v_ref, o_ref, l_ref):
        cur_m, own_m = _merge_masks()

        def step(b, carry):
            r0 = pl.multiple_of(b * BLK, BLK)
            rp = pl.multiple_of(jnp.maximum(b - 1, 0) * BLK, BLK)
            qs = _stack_heads(q_ref[pl.ds(r0, BLK), :])
            vc, vp = v_ref[pl.ds(r0, BLK), :], v_ref[pl.ds(rp, BLK), :]
            sp = jnp.where((b % nb) != 0, _dot(qs, k_ref[pl.ds(rp, BLK), :], NT), NEG)
            s = jnp.where(cur_m, _dot(qs, k_ref[pl.ds(r0, BLK), :], NT), sp)
            s_own = jnp.sum(jnp.where(own_m, sp, 0.0), axis=-1, keepdims=True)
            m = jnp.maximum(jnp.max(s, axis=-1, keepdims=True), s_own)
            p, p_own = jnp.exp(s - m), jnp.exp(s_own - m)
            l = jnp.sum(p, axis=-1, keepdims=True) + p_own
            pb = p.astype(MX)
            zero = jnp.zeros_like(pb)
            o = _dot(jnp.where(cur_m, pb, zero), vc) + _dot(jnp.where(cur_m, zero, pb), vp)
            o = (o + p_own * jnp.concatenate([vp, vp], axis=0).astype(f32)) / l
            o_ref[pl.ds(r0, BLK), :] = _unstack_heads(o).astype(o_ref.dtype)
            l_ref[pl.ds(r0, BLK), :] = _unstack_heads(jnp.broadcast_to(m + jnp.log(l), (2 * BLK, LANES)))
            return carry

        lax.fori_loop(0, T // BLK, step, 0, unroll=8)

    spec = pl.BlockSpec((T, LANES), lambda j: (0, j))
    return pl.pallas_call(
        body, name=f"attn_fwd{g}", grid=(AO // LANES,),
        in_specs=[spec] * 3, out_specs=[spec] * 2,
        out_shape=[jax.ShapeDtypeStruct((T, AO), ACT), jax.ShapeDtypeStruct((T, AO), f32)],
        compiler_params=_cp(("parallel",), 56),
    )(qf, kf, vf)


def _group_weights(lses):
    m = jnp.maximum(jnp.maximum(lses[0], lses[1]), lses[2])
    e = [jnp.exp(l - m) for l in lses]
    inv = 1.0 / (e[0] + e[1] + e[2])
    return [x * inv for x in e]


def _fold_specs(T, tm):
    specs = []
    for _, d in GROUPS:
        specs.append(pl.BlockSpec((d, tm // d, AO), lambda i: (0, i, 0)))
    return specs


def combine_fwd(os_, lses):
    T = os_[0].shape[0] * os_[0].shape[1]
    tm = TM_FOLD

    def body(o0, o1, o2, l0, l1, l2, y_ref, nat):
        o = [_unfold_in(nat, r, d) for r, (_, d) in zip((o0, o1, o2), GROUPS)]
        ls = [_unfold_in(nat, r, d) for r, (_, d) in zip((l0, l1, l2), GROUPS)]
        w = _group_weights(ls)
        y_ref[...] = (w[0] * o[0] + w[1] * o[1] + w[2] * o[2]).astype(y_ref.dtype)

    specs = _fold_specs(T, tm)
    return pl.pallas_call(
        body, name="combine_fwd", grid=(T // tm,),
        in_specs=specs + specs, out_specs=pl.BlockSpec((tm, AO), lambda i: (i, 0)),
        out_shape=jax.ShapeDtypeStruct((T, AO), MX),
        scratch_shapes=[pltpu.VMEM((AO // LANES, tm, LANES), f32)],
        compiler_params=_cp(("parallel",)),
    )(*os_, *lses)


TM_MIX = 256


def mix_out_fwd(proj, ya, yb, yc, x0, pa, pb, pc, wo, g1, b1):
    T = x0.shape[0]
    tm = min(TM_MIX, T)

    def body(gt, ya_r, yb_r, yc_r, x0_r, pa_r, pb_r, pc_r, wo_r, g_r, b_r, mabc, m_o, r1_o, x1_o):
        ma = _dot(ya_r[...], pa_r[...])
        ybv = yb_r[...]
        mb = jnp.concatenate([_dot(ybv, pb_r[k]) for k in range(NCHIP)], axis=1)
        mc = _dot(yc_r[...], pc_r[...])
        m = jnp.zeros((tm, D), f32)
        for j, mm in enumerate((ma, mb, mc)):
            mabc[:, j * D:(j + 1) * D] = mm.astype(mabc.dtype)
            m = m + _sigmoid(gt[:, j * D:(j + 1) * D].astype(f32)) * mm
        mb16 = m.astype(MX)
        m_o[...] = mb16
        r1 = ALPHA * x0_r[...] + _dot(mb16, wo_r[...])
        r1_o[...] = r1
        xhat, _ = _ln_stats(r1)
        x1_o[...] = xhat * g_r[...] + b_r[...]

    full = lambda shape: pl.BlockSpec(shape, lambda i: (0,) * len(shape))
    tile = lambda w: pl.BlockSpec((tm, w), lambda i: (i, 0))
    return pl.pallas_call(
        body, name="mix_out_fwd", grid=(T // tm,),
        in_specs=[tile(3 * D), tile(D), tile(AO), tile(D), tile(D), full((D, D)), full((NCHIP, AO, D // NCHIP)),
                  full((D, D)), full((D, D)), full((1, D)), full((1, D))],
        out_specs=[tile(3 * D), tile(D), tile(D), tile(D)],
        out_shape=[jax.ShapeDtypeStruct((T, 3 * D), MX), jax.ShapeDtypeStruct((T, D), MX),
                   jax.ShapeDtypeStruct((T, D), f32), jax.ShapeDtypeStruct((T, D), f32)],
        compiler_params=_cp(("parallel",), 56),
    )(proj, ya, yb, yc, x0, pa, pb, pc, wo, g1, b1)


TM_FF = 512
TM_FFB = 256
ROW_CHUNK = 64


def ffn_up_fwd(x1, wg, wu):
    T = x1.shape[0]
    tm = min(TM_FFB, T)

    def body(x_r, wg_r, wu_r, g_o, u_o, h_o, gs, us):
        xb = x_r[...].astype(MX)
        for k in range(NCHIP):
            gs[...] = _dot(xb, wg_r[k])
            us[...] = _dot(xb, wu_r[k])
            for r in range(0, tm, ROW_CHUNK):
                rows = pl.ds(r, ROW_CHUNK)
                gate, up = gs[rows, :], us[rows, :]
                g_o[k, rows, :] = gate.astype(g_o.dtype)
                u_o[k, rows, :] = up.astype(u_o.dtype)
                h_o[k, rows, :] = (gate * _sigmoid(gate) * up).astype(h_o.dtype)

    wspec = pl.BlockSpec((NCHIP, D, FB), lambda i: (0, 0, 0))
    ospec = pl.BlockSpec((NCHIP, tm, FB), lambda i: (0, i, 0))
    return pl.pallas_call(
        body, name="ffn_up_fwd", grid=(T // tm,),
        in_specs=[pl.BlockSpec((tm, D), lambda i: (i, 0)), wspec, wspec],
        out_specs=[ospec] * 3,
        out_shape=[jax.ShapeDtypeStruct((NCHIP, T, FB), ACT)] * 2 + [jax.ShapeDtypeStruct((NCHIP, T, FB), MX)],
        scratch_shapes=[pltpu.VMEM((tm, FB), f32)] * 2,
        compiler_params=_cp(("parallel",)),
    )(x1, wg, wu)


def ffn_down_fwd(hh, wd, x1, g2, b2):
    T = x1.shape[0]
    tm = min(TM_FF, T)

    def body(h_r, w_r, x_r, g_r, b_r, r2_o, x2_o):
        r2 = ALPHA * x_r[...]
        for k in range(NCHIP):
            r2 = r2 + _dot(h_r[k], w_r[k])
        r2_o[...] = r2
        xhat, _ = _ln_stats(r2)
        x2_o[...] = xhat * g_r[...] + b_r[...]

    tile = pl.BlockSpec((tm, D), lambda i: (i, 0))
    vec = pl.BlockSpec((1, D), lambda i: (0, 0))
    return pl.pallas_call(
        body, name="ffn_down_fwd", grid=(T // tm,),
        in_specs=[pl.BlockSpec((NCHIP, tm, FB), lambda i: (0, i, 0)), pl.BlockSpec((NCHIP, FB, D), lambda i: (0, 0, 0)),
                  tile, vec, vec],
        out_specs=[tile, tile], out_shape=[jax.ShapeDtypeStruct((T, D), f32)] * 2,
        compiler_params=_cp(("parallel",)),
    )(hh, wd, x1, g2, b2)


def loss_grad(y, tgt):
    T = y.shape[0]
    tm = min(512, T)

    def body(y_r, t_r, l_o, dy_o):
        e = y_r[...] - t_r[...]
        dy_o[...] = e * (1.0 / D)

        @pl.when(pl.program_id(0) == 0)
        def _():
            l_o[...] = jnp.zeros_like(l_o)
        l_o[...] += (0.5 / D) * jnp.sum(e * e)

    tile = pl.BlockSpec((tm, D), lambda i: (i, 0))
    return pl.pallas_call(
        body, name="loss_grad", grid=(T // tm,),
        in_specs=[tile, tile], out_specs=[pl.BlockSpec((8, LANES), lambda i: (0, 0)), tile],
        out_shape=[jax.ShapeDtypeStruct((8, LANES), f32), jax.ShapeDtypeStruct((T, D), f32)],
        compiler_params=_cp(("arbitrary",)),
    )(y, tgt)


def ffn_down_bwd(dx2, r2, g2, wd, gate, up):
    T = dx2.shape[0]
    tm = min(TM_FFB, T)

    def body(dx_r, r_r, g_r, w_r, ga_r, up_r, dr_o, dg_o, du_o, dlg_o, dlb_o, hs):
        i = pl.program_id(0)
        xhat, rstd = _ln_stats(r_r[...])
        dx = dx_r[...]
        _acc_rows(dlg_o, i == 0, dx * xhat)
        _acc_rows(dlb_o, i == 0, dx)
        dr = _ln_bwd(dx, xhat, rstd, g_r[...])
        dr_o[...] = dr
        drb = dr.astype(MX)
        for k in range(NCHIP):
            hs[...] = _dot(drb, w_r[k], NT)
            for r in range(0, tm, ROW_CHUNK):
                rows = pl.ds(r, ROW_CHUNK)
                dhh, gate_v, up_v = hs[rows, :], ga_r[k, rows, :].astype(f32), up_r[k, rows, :].astype(f32)
                sg = _sigmoid(gate_v)
                dg_o[k, rows, :] = (dhh * up_v * sg * (1.0 + gate_v * (1.0 - sg))).astype(dg_o.dtype)
                du_o[k, rows, :] = (dhh * gate_v * sg).astype(du_o.dtype)

    tile = pl.BlockSpec((tm, D), lambda i: (i, 0))
    vec = pl.BlockSpec((1, D), lambda i: (0, 0))
    blk = pl.BlockSpec((NCHIP, tm, FB), lambda i: (0, i, 0))
    return pl.pallas_call(
        body, name="ffn_down_bwd", grid=(T // tm,),
        in_specs=[tile, tile, vec, pl.BlockSpec((NCHIP, FB, D), lambda i: (0, 0, 0)), blk, blk],
        out_specs=[tile, blk, blk, vec, vec],
        out_shape=[jax.ShapeDtypeStruct((T, D), f32)] + [jax.ShapeDtypeStruct((NCHIP, T, FB), MX)] * 2
        + [jax.ShapeDtypeStruct((1, D), f32)] * 2,
        scratch_shapes=[pltpu.VMEM((tm, FB), f32)],
        compiler_params=_cp(("arbitrary",)),
    )(dx2, r2, g2, wd, gate, up)


def ffn_up_bwd(dr2, dgate, dup, wg, wu, r1, g1):
    T = dr2.shape[0]
    tm = min(TM_FFB, T)

    def body(dr2_r, dg_r, du_r, wg_r, wu_r, r1_r, g_r, dr1_o, dlg_o, dlb_o):
        i = pl.program_id(0)
        dx = ALPHA * dr2_r[...]
        for k in range(NCHIP):
            dx = dx + _dot(dg_r[k], wg_r[k], NT) + _dot(du_r[k], wu_r[k], NT)
        xhat, rstd = _ln_stats(r1_r[...])
        _acc_rows(dlg_o, i == 0, dx * xhat)
        _acc_rows(dlb_o, i == 0, dx)
        dr1_o[...] = _ln_bwd(dx, xhat, rstd, g_r[...])

    tile = pl.BlockSpec((tm, D), lambda i: (i, 0))
    vec = pl.BlockSpec((1, D), lambda i: (0, 0))
    blk = pl.BlockSpec((NCHIP, tm, FB), lambda i: (0, i, 0))
    wspec = pl.BlockSpec((NCHIP, D, FB), lambda i: (0, 0, 0))
    return pl.pallas_call(
        body, name="ffn_up_bwd", grid=(T // tm,),
        in_specs=[tile, blk, blk, wspec, wspec, tile, vec],
        out_specs=[tile, vec, vec],
        out_shape=[jax.ShapeDtypeStruct((T, D), f32)] + [jax.ShapeDtypeStruct((1, D), f32)] * 2,
        compiler_params=_cp(("arbitrary",)),
    )(dr2, dgate, dup, wg, wu, r1, g1)


def mix_out_bwd(dr1, proj, mabc, wo, pa, pb, pc):
    T = dr1.shape[0]
    tm = min(TM_MIX, T)

    def body(dr_r, gt, mabc_r, wo_r, pa_r, pb_r, pc_r, dmabc_o, dgt_o, dya_o, dyb_o, dyc_o):
        dm = _dot(dr_r[...].astype(MX), wo_r[...], NT)
        dmx = []
        for j in range(3):
            s = _sigmoid(gt[:, j * D:(j + 1) * D].astype(f32))
            v = (dm * s).astype(MX)
            dmx.append(v)
            dmabc_o[:, j * D:(j + 1) * D] = v
            dgt_o[:, j * D:(j + 1) * D] = (dm * mabc_r[:, j * D:(j + 1) * D].astype(f32) * s * (1.0 - s)).astype(dgt_o.dtype)
        dya_o[...] = _dot(dmx[0], pa_r[...], NT).astype(dya_o.dtype)
        dyb = jnp.zeros((tm, AO), f32)
        for k in range(NCHIP):
            dyb = dyb + _dot(dmx[1][:, k * (D // NCHIP):(k + 1) * (D // NCHIP)], pb_r[k], NT)
        dyb_o[...] = dyb.astype(dyb_o.dtype)
        dyc_o[...] = _dot(dmx[2], pc_r[...], NT).astype(dyc_o.dtype)

    full = lambda shape: pl.BlockSpec(shape, lambda i: (0,) * len(shape))
    tile = lambda w: pl.BlockSpec((tm, w), lambda i: (i, 0))
    return pl.pallas_call(
        body, name="mix_out_bwd", grid=(T // tm,),
        in_specs=[tile(D), tile(3 * D), tile(3 * D), full((D, D)), full((D, D)), full((NCHIP, AO, D // NCHIP)), full((D, D))],
        out_specs=[tile(3 * D), tile(3 * D), tile(D), tile(AO), tile(D)],
        out_shape=[jax.ShapeDtypeStruct((T, 3 * D), MX), jax.ShapeDtypeStruct((T, 3 * D), MX),
                   jax.ShapeDtypeStruct((T, D), ACT), jax.ShapeDtypeStruct((T, AO), ACT), jax.ShapeDtypeStruct((T, D), ACT)],
        compiler_params=_cp(("parallel",), 56),
    )(dr1, proj, mabc, wo, pa, pb, pc)


def transpose_cast(x):
    T = x.shape[0]
    tm = min(512, T)

    def body(x_r, o_r):
        o_r[...] = x_r[...].T.astype(o_r.dtype)

    return pl.pallas_call(
        body, name="transpose_cast", grid=(T // tm,),
        in_specs=[pl.BlockSpec((tm, D), lambda i: (i, 0))], out_specs=pl.BlockSpec((D, tm), lambda i: (0, i)),
        out_shape=jax.ShapeDtypeStruct((D, T), MX), compiler_params=_cp(("parallel",)),
    )(x)


def tn_matmul(name, a, b, a_spec, b_spec, out_shape, out_spec, grid):
    nt = len(grid) - 1

    def body(a_r, b_r, o_r):
        @pl.when(pl.program_id(nt) == 0)
        def _():
            o_r[...] = jnp.zeros_like(o_r)
        av = a_r[...].reshape(a_r.shape[-2:]).astype(MX)
        bv = b_r[...].reshape(b_r.shape[-2:]).astype(MX)
        o_r[...] += _dot(av, bv, TN).reshape(o_r.shape)

    return pl.pallas_call(
        body, name=name, grid=grid, in_specs=[a_spec, b_spec], out_specs=out_spec,
        out_shape=jax.ShapeDtypeStruct(out_shape, f32),
        compiler_params=_cp(("parallel",) * nt + ("arbitrary",), 56),
    )(a, b)


def attn_pre_bwd(dyb, os_, lses, ones):
    T = dyb.shape[0]
    tm = TM_FOLD

    def body(dy_r, o0, o1, o2, l0, l1, l2, ones_r, d0, d1, d2, f0, f1, f2, nat):
        o = [_unfold_in(nat, r, d) for r, (_, d) in zip((o0, o1, o2), GROUPS)]
        ls = [_unfold_in(nat, r, d) for r, (_, d) in zip((l0, l1, l2), GROUPS)]
        w = _group_weights(ls)
        dy = dy_r[...].astype(f32)
        t = dy * (w[0] * o[0] + w[1] * o[1] + w[2] * o[2])
        hi = t.astype(MX)
        lo = (t - hi.astype(f32)).astype(MX)
        c = _dot(hi, ones_r[...]) + _dot(lo, ones_r[...])
        for wg, do_o, df_o, (_, d) in zip(w, (d0, d1, d2), (f0, f1, f2), GROUPS):
            _fold_out(nat, wg * dy, do_o, d)
            _fold_out(nat, -wg * c, df_o, d)

    specs = _fold_specs(T, tm)
    return pl.pallas_call(
        body, name="attn_pre_bwd", grid=(T // tm,),
        in_specs=[pl.BlockSpec((tm, AO), lambda i: (i, 0))] + specs + specs + [pl.BlockSpec((AO, AO), lambda i: (0, 0))],
        out_specs=specs + specs,
        out_shape=[jax.ShapeDtypeStruct((d, T // d, AO), MX) for _, d in GROUPS]
        + [jax.ShapeDtypeStruct((d, T // d, AO), f32) for _, d in GROUPS],
        scratch_shapes=[pltpu.VMEM((AO // LANES, tm, LANES), f32)],
        compiler_params=_cp(("parallel",)),
    )(dyb, *os_, *lses, ones)


def _head_ones():
    i = jnp.arange(AO) // HD
    return (i[:, None] == i[None, :]).astype(MX)


BWD_BLOCKS = 4


def attn_bwd(qf, kf, vf, dof, lse, df, g, nb):
    T = qf.shape[0]

    def body(q_ref, k_ref, v_ref, do_ref, l_ref, d_ref, dq_ref, dk_ref, dv_ref):
        prev_m, cur_m = _window_masks()

        def head_col(ref, r0):
            v = ref[pl.ds(r0, BLK), :]
            return jnp.concatenate([v[:, 0:1], v[:, HD:HD + 1]], axis=0)

        def step(b, carry):
            dk_c, dv_c = carry
            r0 = pl.multiple_of(b * BLK, BLK)
            rp = pl.multiple_of(jnp.maximum(b - 1, 0) * BLK, BLK)
            qs, dos = _stack_heads(q_ref[pl.ds(r0, BLK), :]), _stack_heads(do_ref[pl.ds(r0, BLK), :])
            k2, v2 = _two_blocks(k_ref, b), _two_blocks(v_ref, b)
            valid = cur_m | (prev_m & ((b % nb) != 0))
            p = jnp.where(valid, jnp.exp(_dot(qs, k2, NT) - head_col(l_ref, r0)), 0.0)
            ds = (p * (_dot(dos, v2, NT) + head_col(d_ref, r0))).astype(MX)
            dq_ref[pl.ds(r0, BLK), :] = _unstack_heads(_dot(ds, k2)).astype(dq_ref.dtype)
            dk2 = _dot(ds, qs, TN)
            dv2 = _dot(p.astype(MX), dos, TN)
            dk_ref[pl.ds(rp, BLK), :] = (dk_c + dk2[:BLK]).astype(dk_ref.dtype)
            dv_ref[pl.ds(rp, BLK), :] = (dv_c + dv2[:BLK]).astype(dv_ref.dtype)
            return dk2[BLK:], dv2[BLK:]

        zero = jnp.zeros((BLK, LANES), f32)

        def steps(i, carry):
            for j in range(BWD_BLOCKS):
                carry = step(BWD_BLOCKS * i + j, carry)
            return carry

        dk_c, dv_c = lax.fori_loop(0, T // BLK // BWD_BLOCKS, steps, (zero, zero))
        dk_ref[pl.ds(T - BLK, BLK), :] = dk_c.astype(dk_ref.dtype)
        dv_ref[pl.ds(T - BLK, BLK), :] = dv_c.astype(dv_ref.dtype)

    spec = pl.BlockSpec((T, LANES), lambda j: (0, j))
    return pl.pallas_call(
        body, name=f"attn_bwd{g}", grid=(AO // LANES,),
        in_specs=[spec] * 6, out_specs=[spec] * 3,
        out_shape=[jax.ShapeDtypeStruct((T, AO), MX)] * 3,
        compiler_params=_cp(("parallel",), 60),
    )(qf, kf, vf, dof, lse, df)


def unfold_rope_bwd(dqf, dkf, dvf, cos_t, sin_t, g, d):
    T = dqf.shape[0] * dqf.shape[1]
    tm = TM_FOLD

    def body(q_r, k_r, v_r, c_ref, s_ref, o_ref, nat):
        cos, sin = _tile4(c_ref[...]), _tile4(s_ref[...])
        for part, ref, scale in ((0, q_r, HD ** -0.5), (1, k_r, 1.0), (2, v_r, None)):
            x = _unfold_in(nat, ref, d)
            if scale is not None:
                x = (x * cos - _swap_halves(x) * sin) * scale
            o_ref[:, part * AO:(part + 1) * AO] = x.astype(o_ref.dtype)

    fold_spec = pl.BlockSpec((d, tm // d, AO), lambda i: (0, i, 0))
    tab = pl.BlockSpec((tm, LANES), lambda i: (i, 0))
    return pl.pallas_call(
        body, name=f"unfold_rope_bwd{g}", grid=(T // tm,),
        in_specs=[fold_spec] * 3 + [tab, tab],
        out_specs=pl.BlockSpec((tm, 3 * AO), lambda i: (i, 0)),
        out_shape=jax.ShapeDtypeStruct((T, 3 * AO), MX),
        scratch_shapes=[pltpu.VMEM((AO // LANES, tm, LANES), f32)],
        compiler_params=_cp(("parallel",)),
    )(dqf, dkf, dvf, cos_t, sin_t)


CONV_CHUNK = 32


def conv_bwd(dya, proj, conv_w):
    T = dya.shape[0]
    tm = TM_AC
    last = T // tm - 1

    def body(dy_r, bch, hprev, dy_next, b_next, cw, d_o, dw_o, zs, ds):
        i = pl.program_id(0)
        ch = CONV_CHUNK
        hz = hprev[:, :D].astype(f32) * hprev[:, D:].astype(f32)
        zs[0:HALO, :] = jnp.where(i > 0, hz, 0.0)
        ds[tm:tm + HALO, :] = jnp.where(i < last, dy_next[...].astype(f32) * b_next[...].astype(f32), 0.0)
        for r in range(0, tm, ch):
            zs[HALO + r:HALO + r + ch, :] = bch[r:r + ch, D:2 * D].astype(f32) * bch[r:r + ch, 2 * D:].astype(f32)
            ds[r:r + ch, :] = dy_r[r:r + ch, :].astype(f32) * bch[r:r + ch, :D].astype(f32)

        @pl.when(i == 0)
        def _():
            dw_o[...] = jnp.zeros_like(dw_o)

        sums = [jnp.zeros((1, D), f32) for _ in range(3)]
        for r in range(0, tm, ch):
            z2, z1, z = (zs[HALO + r - s:HALO + r - s + ch, :] for s in (2, 1, 0))
            dcv, d1, d2 = (ds[r + s:r + s + ch, :] for s in (0, 1, 2))
            cv = cw[0:1, :] * z2 + cw[1:2, :] * z1 + cw[2:3, :] * z
            dz = cw[2:3, :] * dcv + cw[1:2, :] * d1 + cw[0:1, :] * d2
            d_o[r:r + ch, :D] = (dy_r[r:r + ch, :].astype(f32) * cv).astype(d_o.dtype)
            d_o[r:r + ch, D:2 * D] = (dz * bch[r:r + ch, 2 * D:].astype(f32)).astype(d_o.dtype)
            d_o[r:r + ch, 2 * D:] = (dz * bch[r:r + ch, D:2 * D].astype(f32)).astype(d_o.dtype)
            for k, zz in enumerate((z2, z1, z)):
                sums[k] = sums[k] + jnp.sum(dcv * zz, axis=0, keepdims=True)
        for k in range(3):
            dw_o[k:k + 1, :] += sums[k]

    nh = tm // HALO
    return pl.pallas_call(
        body, name="conv_bwd", grid=(T // tm,),
        in_specs=[pl.BlockSpec((tm, D), lambda i: (i, 0)), pl.BlockSpec((tm, 3 * D), lambda i: (i, 1)),
                  pl.BlockSpec((HALO, 2 * D), lambda i: (jnp.maximum(i * nh - 1, 0), 2)),
                  pl.BlockSpec((HALO, D), lambda i: (jnp.minimum((i + 1) * nh, T // HALO - 1), 0)),
                  pl.BlockSpec((HALO, D), lambda i: (jnp.minimum((i + 1) * nh, T // HALO - 1), 3)),
                  pl.BlockSpec((3, D), lambda i: (0, 0))],
        out_specs=[pl.BlockSpec((tm, 3 * D), lambda i: (i, 0)), pl.BlockSpec((3, D), lambda i: (0, 0))],
        out_shape=[jax.ShapeDtypeStruct((T, 3 * D), MX), jax.ShapeDtypeStruct((3, D), f32)],
        scratch_shapes=[pltpu.VMEM((HALO + tm, D), f32), pltpu.VMEM((tm + HALO, D), f32)],
        compiler_params=_cp(("arbitrary",)),
    )(dya, proj, proj, dya, proj, conv_w)


def gmlp_bwd(dyc, proj, wst, bsx, lg, lb):
    T = dyc.shape[0]
    tm = TM_AC
    last = T // tm - 1

    def body(dy_r, u0, u1, v0, v1, ws, bs, lg_r, lb_r, d_o, dws_o, dbs_o, dlg_o, dlb_o, bacc):
        i = pl.program_id(0)
        up = jnp.concatenate([u0[...], u1[...]], axis=1).astype(f32)
        vp = jnp.concatenate([v0[...], v1[...]], axis=1).astype(f32)
        u, du = _gelu_and_grad(up)
        gv, dgv = _gelu_and_grad(vp)
        u, vn, xhat, rstd, sp = _gmlp_fwd(up, vp, ws, bs, lg_r[...], lb_r[...], u, gv)
        dy = dy_r[...].astype(f32)
        d_o[:, :D] = (dy * sp * du).astype(d_o.dtype)
        dsp = dy * u
        dspb, vnb = dsp.astype(MX), vn.astype(MX)

        @pl.when(i == 0)
        def _():
            dws_o[...] = jnp.zeros_like(dws_o)
            bacc[...] = jnp.zeros_like(bacc)

        rows = []
        for c in range(tm // BLK):
            r = slice(c * BLK, (c + 1) * BLK)
            cols = []
            for g in range(8):
                cs = slice(g * BLK, (g + 1) * BLK)
                dws_o[g] += _dot(dspb[r, cs], vnb[r, cs], NT)
                bacc[g] += dsp[r, cs]
                cols.append(_dot(ws[g], dspb[r, cs], TN))
            rows.append(jnp.concatenate(cols, axis=1))
        dvn = jnp.concatenate(rows, axis=0)
        _acc_rows(dlg_o, i == 0, dvn * xhat)
        _acc_rows(dlb_o, i == 0, dvn)
        d_o[:, D:] = (_ln_bwd(dvn, xhat, rstd, lg_r[...]) * dgv).astype(d_o.dtype)

        @pl.when(i == last)
        def _():
            row = lax.broadcasted_iota(jnp.int32, (BLK, BLK), 0)
            col = lax.broadcasted_iota(jnp.int32, (BLK, BLK), 1)
            ones = jnp.ones((8, BLK), MX)
            for g in range(8):
                dws_o[g] = jnp.where(col <= row, dws_o[g], 0.0)
                a = bacc[g]
                hi = a.astype(MX)
                lo = (a - hi.astype(f32)).astype(MX)
                dbs_o[g:g + 1, :] = (_dot(ones, hi, NT) + _dot(ones, lo, NT))[0:1, :]

    full = lambda shape: pl.BlockSpec(shape, lambda i: (0,) * len(shape))
    return pl.pallas_call(
        body, name="gmlp_bwd", grid=(T // tm,),
        in_specs=[pl.BlockSpec((tm, D), lambda i: (i, 0)), *_uv_specs(), full((8, BLK, BLK)), full((8, BLK, BLK)),
                  full((1, D)), full((1, D))],
        out_specs=[pl.BlockSpec((tm, 2 * D), lambda i: (i, 0)), full((8, BLK, BLK)), full((8, BLK)), full((1, D)), full((1, D))],
        out_shape=[jax.ShapeDtypeStruct((T, 2 * D), MX), jax.ShapeDtypeStruct((8, BLK, BLK), f32),
                   jax.ShapeDtypeStruct((8, BLK), f32), jax.ShapeDtypeStruct((1, D), f32), jax.ShapeDtypeStruct((1, D), f32)],
        scratch_shapes=[pltpu.VMEM((8, BLK, BLK), f32)],
        compiler_params=_cp(("arbitrary",)),
    )(dyc, proj, proj, proj, proj, wst, bsx, lg, lb)


PART_TILES = (6, 6, 3, 3, 3, 4)
PART_START = (0, 6, 12, 15, 18, 21)
TJ = 512


def _part_specs(tm, rows_axis):
    specs = []
    for n, s in zip(PART_TILES, PART_START):
        def imap(*idx, n=n, s=s):
            i, j = idx[rows_axis], idx[1 - rows_axis]
            inside = (j >= s) & (j < s + n)
            return (jnp.where(inside, i, 0), jnp.clip(j - s, 0, n - 1))
        specs.append(pl.BlockSpec((tm, TJ), imap))
    return specs


def _with_part(j, refs, fn):
    for r, n, s in zip(refs, PART_TILES, PART_START):
        @pl.when((j >= s) & (j < s + n))
        def _():
            fn(r[...])


def dx_in(dr1, parts, w, bias):
    T = dr1.shape[0]
    tm = min(2048, T)

    def body(dr_r, p0, p1, p2, p3, p4, p5, w_r, b_r, o_r):
        j = pl.program_id(1)

        @pl.when(j == 0)
        def _():
            o_r[...] = ALPHA * dr_r[...] + b_r[...]

        def acc(tile):
            o_r[...] += _dot(tile, w_r[...], NT)
        _with_part(j, (p0, p1, p2, p3, p4, p5), acc)

    once = dict(pipeline_mode=pl.Buffered(1))
    return pl.pallas_call(
        body, name="dx_in", grid=(T // tm, NIN // TJ),
        in_specs=[pl.BlockSpec((tm, D), lambda i, j: (i, 0), **once)] + _part_specs(tm, 0)
        + [pl.BlockSpec((D, TJ), lambda i, j: (0, j)), pl.BlockSpec((1, D), lambda i, j: (0, 0))],
        out_specs=pl.BlockSpec((tm, D), lambda i, j: (i, 0), **once),
        out_shape=jax.ShapeDtypeStruct((T, D), f32),
        compiler_params=_cp(("parallel", "arbitrary"), 56),
    )(dr1, *parts, w, bias)


def dw_in(x0t, parts):
    T = x0t.shape[1]
    tk = min(2048, T)

    def body(x_r, p0, p1, p2, p3, p4, p5, o_r):
        j, t = pl.program_id(0), pl.program_id(1)

        @pl.when(t == 0)
        def _():
            o_r[...] = jnp.zeros_like(o_r)

        def acc(tile):
            o_r[...] += _dot(x_r[:, pl.ds(pl.multiple_of(t * tk, tk), tk)], tile)
        _with_part(j, (p0, p1, p2, p3, p4, p5), acc)

    return pl.pallas_call(
        body, name="dw_in", grid=(NIN // TJ, T // tk),
        in_specs=[pl.BlockSpec((D, T), lambda j, t: (0, 0), pipeline_mode=pl.Buffered(1))] + _part_specs(tk, 1),
        out_specs=pl.BlockSpec((D, TJ), lambda j, t: (0, j)),
        out_shape=jax.ShapeDtypeStruct((D, NIN), f32),
        compiler_params=_cp(("parallel", "arbitrary"), 56),
    )(x0t, *parts)


def rope_tables(positions):
    half = HD // 2
    inv_freq = ROPE_THETA ** (-jnp.arange(half, dtype=f32) / half)
    ang = positions.astype(f32)[:, None] * inv_freq
    cos, sin = jnp.cos(ang), jnp.sin(ang)
    return jnp.tile(cos, (1, LANES // half)), jnp.tile(jnp.concatenate([-sin, sin], axis=1), (1, LANES // HD))


def _flat(a):
    return a.reshape(a.shape[0] * a.shape[1], a.shape[2])


def layer_fwd(x0, W, cos_t, sin_t):
    T = x0.shape[0]
    proj = mm_in(x0, W["w_in"], W["in_bias"])
    ya, yc = mix_ac_fwd(proj, W["conv_w"], W["wst"], W["bsx"], W["gmlp_ln_g"], W["gmlp_ln_b"])
    folded, os_, lses = [], [], []
    for g, (_, d) in enumerate(GROUPS):
        qf, kf, vf = fold_rope(proj, cos_t, sin_t, g, d)
        o, lse = attn_fwd(_flat(qf), _flat(kf), _flat(vf), g, T // d // BLK)
        folded.append((qf, kf, vf))
        os_.append(o.reshape(d, T // d, AO))
        lses.append(lse.reshape(d, T // d, AO))
    yb = combine_fwd(os_, lses)
    if "late" in W:
        W = {**W, **W["late"](yb)}
    mabc, m, r1, x1 = mix_out_fwd(proj, ya, yb, yc, x0, W["p_a"], W["p_b"], W["p_c"], W["w_o"], W["ln1_g"], W["ln1_b"])
    gate, up, hh = ffn_up_fwd(x1, W["w_gate"], W["w_up"])
    r2, x2 = ffn_down_fwd(hh, W["w_down"], x1, W["ln2_g"], W["ln2_b"])
    saved = dict(x0=x0, proj=proj, ya=ya, yb=yb, yc=yc, folded=folded, os=os_, lses=lses, mabc=mabc, m=m, r1=r1,
                 x1=x1, gate=gate, up=up, hh=hh, r2=r2)
    return x2, saved, W


def layer_bwd(dx2, S, W, cos_t, sin_t, on_grads=None):
    T = dx2.shape[0]
    tk = min(2048, T)
    G = {}
    dr2, dgate, dup, G["ln2_g"], G["ln2_b"] = ffn_down_bwd(dx2, S["r2"], W["ln2_g"], W["w_down"], S["gate"], S["up"])
    blk_a = pl.BlockSpec((1, tk, FB), lambda k, t: (k, t, 0))
    row_b = pl.BlockSpec((tk, D), lambda k, t: (t, 0))
    G["w_down"] = tn_matmul("dw_down", S["hh"], dr2, blk_a, row_b, (NCHIP, FB, D),
                            pl.BlockSpec((1, FB, D), lambda k, t: (k, 0, 0)), (NCHIP, T // tk))
    for nm, dv in (("w_gate", dgate), ("w_up", dup)):
        G[nm] = tn_matmul("d" + nm, dv, S["x1"], blk_a, row_b, (NCHIP, FB, D),
                          pl.BlockSpec((1, FB, D), lambda k, t: (k, 0, 0)), (NCHIP, T // tk))
    dr1, G["ln1_g"], G["ln1_b"] = ffn_up_bwd(dr2, dgate, dup, W["w_gate"], W["w_up"], S["r1"], W["ln1_g"])
    dmabc, dgates, dya, dyb, dyc = mix_out_bwd(dr1, S["proj"], S["mabc"], W["w_o"], W["p_a"], W["p_b"], W["p_c"])
    one = (1, T // tk)
    full_o = pl.BlockSpec((D, D), lambda k, t: (0, 0))
    G["w_o"] = tn_matmul("dw_o", S["m"], dr1, row_b, row_b, (D, D), full_o, one)
    G["p_a"] = tn_matmul("dp_a", S["ya"], dmabc, row_b, pl.BlockSpec((tk, D), lambda k, t: (t, 0)), (D, D), full_o, one)
    G["p_c"] = tn_matmul("dp_c", S["yc"], dmabc, row_b, pl.BlockSpec((tk, D), lambda k, t: (t, 2)), (D, D), full_o, one)
    G["p_b"] = tn_matmul("dp_b", S["yb"], dmabc, pl.BlockSpec((tk, AO), lambda k, t: (t, 0)),
                         pl.BlockSpec((tk, D // NCHIP), lambda k, t: (t, NCHIP + k)), (NCHIP, AO, D // NCHIP),
                         pl.BlockSpec((1, AO, D // NCHIP), lambda k, t: (k, 0, 0)), (NCHIP, T // tk))
    conv_w = W["conv_w"]
    if on_grads is not None:
        conv_w = conv_w + on_grads({n: G[n] for n in BIG if n != "w_in"})
    dbch, G["conv_w"] = conv_bwd(dya, S["proj"], conv_w)
    duv, G["w_s"], G["b_s"], G["gmlp_ln_g"], G["gmlp_ln_b"] = gmlp_bwd(
        dyc, S["proj"], W["wst"], W["bsx"], W["gmlp_ln_g"], W["gmlp_ln_b"])
    ones = _head_ones()
    if on_grads is not None:
        small = {n: G[n] for n in VECS + ("b_s", "w_s", "conv_w")}
        ones = ones + on_grads(small).astype(MX)
    pre = attn_pre_bwd(dyb, S["os"], S["lses"], ones)
    dqkv = []
    for g, (_, d) in enumerate(GROUPS):
        qf, kf, vf = S["folded"][g]
        dqf, dkf, dvf = attn_bwd(_flat(qf), _flat(kf), _flat(vf), _flat(pre[g]), _flat(S["lses"][g]), _flat(pre[3 + g]),
                                 g, T // d // BLK)
        shp = (d, T // d, AO)
        dqkv.append(unfold_rope_bwd(dqf.reshape(shp), dkf.reshape(shp), dvf.reshape(shp), cos_t, sin_t, g, d))
    parts = (dgates, dbch, *dqkv, duv)
    G["w_in"] = dw_in(transpose_cast(S["x0"]), parts)
    bias = jnp.zeros((1, D), f32)
    if on_grads is not None:
        bias = bias + on_grads({"w_in": G["w_in"]})
    dx0 = dx_in(dr1, parts, W["w_in"], bias)
    started = on_grads({"dx": dx0}) if on_grads is not None else None
    return dx0, G, started


def prep_layer_weights(Wl):
    W = dict(Wl)
    tril = jnp.tril(jnp.ones((BLK, BLK), f32))
    W["wst"] = (Wl["w_s"] * tril[None]).astype(MX)
    W["bsx"] = jnp.broadcast_to(Wl["b_s"][:, :, None], (8, BLK, BLK))
    for n in ("gmlp_ln_g", "gmlp_ln_b", "ln1_g", "ln1_b", "ln2_g", "ln2_b"):
        W[n] = Wl[n].reshape(1, D)
    W["in_bias"] = jnp.zeros((1, NIN), f32) + Wl.get("after", 0.0)
    return W


def local_step(x, positions, target, layers, on_grads=None):
    cos_t, sin_t = rope_tables(positions)
    Ws, saved = [], []
    h = x
    for Wl in layers:
        h, S, W = layer_fwd(h, prep_layer_weights(Wl(h) if callable(Wl) else Wl), cos_t, sin_t)
        Ws.append(W)
        saved.append(S)
    lsum, dh = loss_grad(h, target)
    if on_grads is not None:
        on_grads(len(Ws), {"loss": lsum})
    grads = [None] * len(Ws)
    started = None
    for l in reversed(range(len(Ws))):
        W = Ws[l]
        if started is not None:
            W = dict(W, ln2_g=W["ln2_g"] + started)
        hook = functools.partial(on_grads, l) if on_grads is not None else None
        dh, grads[l], started = layer_bwd(dh, saved[l], W, cos_t, sin_t, hook)
    return lsum, dh, grads


MESH = pl.DeviceIdType.MESH
ANY = pl.BlockSpec(memory_space=pl.ANY)
BIG = ("w_in", "w_gate", "w_up", "w_down", "p_a", "p_b", "p_c", "w_o")
NBIG = len(BIG)


def _place():
    x, y, c = lax.axis_index("x"), lax.axis_index("y"), lax.axis_index("c")
    return x, y, c, 2 * x + y


def _rcopy(src, dst, send, recv, dev):
    return pltpu.make_async_remote_copy(src_ref=src, dst_ref=dst, send_sem=send, recv_sem=recv, device_id=dev,
                                        device_id_type=MESH)


def _cols(ref, k, width):
    start = k * width if isinstance(k, int) else pl.multiple_of(k * width, LANES)
    return ref.at[:, pl.ds(start, width)]


CHUNK_BYTES = 1 << 20


def _pieces(shape, itemsize, nbytes=CHUNK_BYTES):
    rows, cols = shape[-2], shape[-1]
    per = max(16, nbytes // (cols * itemsize) // 16 * 16)
    out = []
    for lead in (range(shape[0]) if len(shape) == 3 else (None,)):
        for r in range(0, rows, per):
            sl = (pl.ds(r, min(per, rows - r)), slice(None))
            out.append(sl if lead is None else (lead,) + sl)
    return out


def _start_pieces(src, dst, make, nbytes=CHUNK_BYTES):
    for idx in _pieces(src.shape, jnp.dtype(src.dtype).itemsize, nbytes):
        make(src.at[idx], dst.at[idx]).start()


def gather_halves(shards):
    n = len(shards)

    def body(*refs):
        srcs, dsts = refs[:n], refs[n:2 * n]
        send, recv, own_send, own_recv = refs[2 * n:]
        x, y, c, k = _place()
        sib = (x, y, 1 - c)
        chips = [(1 - x, y), (x, 1 - y), (1 - x, 1 - y)]

        def slot(a, layer, pos):
            if a == 0:
                return _cols(dsts[0].at[layer], pos, WIN_SHARD)
            return dsts[a].at[pos, layer]

        def ici(a, j, src, dst):
            return _rcopy(src, dst, send.at[a, j], recv.at[a, j], (*chips[j], c))

        def d2d(a, j, src, dst):
            return _rcopy(src, dst, send.at[a, 3 + j], recv.at[a, 3 + j], sib)

        def own(a, layer, src, dst):
            return _rcopy(src, dst, own_send.at[a, layer], own_recv.at[a, layer], sib)

        for a in range(n):
            for j in range(3):
                _start_pieces(srcs[a].at[c], slot(a, c, k), functools.partial(ici, a, j))
        for a in range(n):
            for layer in range(DEPTH):
                _start_pieces(srcs[a].at[layer], slot(a, layer, k), functools.partial(own, a, layer))
        for a in range(n):
            for j, (cx, cy) in enumerate(chips):
                landed = slot(a, c, 2 * cx + cy)
                ici(a, j, landed, landed).wait_recv()
                _start_pieces(landed, landed, functools.partial(d2d, a, j))
        for a in range(n):
            for j, (cx, cy) in enumerate(chips):
                passed = slot(a, 1 - c, 2 * cx + cy)
                d2d(a, j, passed, passed).wait_recv()
                landed = slot(a, c, 2 * cx + cy)
                d2d(a, j, landed, landed).wait_send()
                ici(a, j, srcs[a].at[c], slot(a, c, k)).wait_send()
            for layer in range(DEPTH):
                own(a, layer, srcs[a].at[layer], slot(a, layer, k)).wait()

    outs = [jax.ShapeDtypeStruct((2, shards[0].shape[1], NIN), shards[0].dtype)]
    outs += [jax.ShapeDtypeStruct((NCHIP,) + s.shape, s.dtype) for s in shards[1:]]
    return pl.pallas_call(
        body, name="gather_halves", in_specs=[ANY] * n, out_specs=[ANY] * n, out_shape=outs,
        scratch_shapes=[pltpu.SemaphoreType.DMA((n, 6)), pltpu.SemaphoreType.DMA((n, 6)),
                        pltpu.SemaphoreType.DMA((n, DEPTH)), pltpu.SemaphoreType.DMA((n, DEPTH))],
    )(*shards)


def _gather_slot(dst, pos):
    return _cols(dst, pos, WIN_SHARD) if len(dst.shape) == 2 else dst.at[pos]


def _gather_copy(a, j, src, dst, send, recv, dev):
    return _rcopy(src, dst, send.at[a * NCHIP + j], recv.at[a * NCHIP + j], dev)


def gather_start(tag, shards, after):
    n = len(shards)

    def body(*refs):
        srcs, dsts = refs[:n], refs[n:2 * n]
        send, recv = refs[2 * n + len(after)], refs[2 * n + len(after) + 1]
        token = refs[-1]
        x, y, c, k = _place()
        peers = [(1 - x, y, c), (x, 1 - y, c), (1 - x, 1 - y, c), (x, y, 1 - c)]
        for a in range(n):
            for j, dev in enumerate(peers):
                _start_pieces(srcs[a], _gather_slot(dsts[a], k),
                              lambda s, d, a=a, j=j, dev=dev: _gather_copy(a, j, s, d, send, recv, dev))
        token[...] = jnp.zeros_like(token)

    gathered = [lax.empty((D, NIN) if s.shape == (D, WIN_SHARD) else (NCHIP,) + s.shape, s.dtype) for s in shards]
    ops = [pltpu.with_memory_space_constraint(v, pltpu.HBM) for v in list(shards) + gathered]
    sem = pltpu.SemaphoreType.DMA((n * NCHIP,))
    res = pl.pallas_call(
        body, name=f"gather_start{tag}", in_specs=[HBM] * (2 * n) + [ANY] * len(after),
        out_specs=[SEMS, SEMS] + [HBM] * (2 * n) + [pl.BlockSpec(memory_space=pltpu.VMEM)],
        out_shape=[sem, sem] + [pltpu.HBM(v.shape, v.dtype) for v in ops] + [jax.ShapeDtypeStruct((8, LANES), f32)],
        input_output_aliases={i: 2 + i for i in range(2 * n)},
        compiler_params=pltpu.CompilerParams(has_side_effects=EFFECT),
    )(*ops, *after)
    return res[0], res[1], res[2:2 + n], res[2 + n:2 + 2 * n], res[-1]


def gather_wait(tag, send, recv, shards, gathered, after):
    n = len(shards)

    def body(*refs):
        srcs, dsts = refs[:n], refs[n:2 * n]
        send_r, recv_r = refs[2 * n], refs[2 * n + 1]
        x, y, c, k = _place()
        peers = [(1 - x, y, c), (x, 1 - y, c), (1 - x, 1 - y, c), (x, y, 1 - c)]
        for a in range(n):
            for j, dev in enumerate(peers):
                _gather_copy(a, j, srcs[a], _gather_slot(dsts[a], k), send_r, recv_r, dev).wait_send()
                pos = 2 * dev[0] + dev[1]
                _gather_copy(a, j, srcs[a], _gather_slot(dsts[a], pos), send_r, recv_r, dev).wait_recv()

    ops = list(shards) + list(gathered)
    res = pl.pallas_call(
        body, name=f"gather_wait{tag}", in_specs=[HBM] * (2 * n) + [SEMS, SEMS] + [ANY] * len(after),
        out_specs=[HBM] * (2 * n), out_shape=[pltpu.HBM(v.shape, v.dtype) for v in ops],
        input_output_aliases={i: i for i in range(2 * n)},
        compiler_params=pltpu.CompilerParams(has_side_effects=EFFECT),
    )(*ops, send, recv, *after)
    return res[n:]


def _half(ref, h):
    rows = ref.shape[-2] // 2
    start = pl.multiple_of(h * rows, 16)
    if len(ref.shape) == 2:
        return ref.at[pl.ds(start, rows), :]
    return ref.at[:, pl.ds(start, rows), :]


HBM = pl.BlockSpec(memory_space=pltpu.HBM)
SEMS = pl.BlockSpec(memory_space=pltpu.SEMAPHORE)
EFFECT = pltpu.SideEffectType.DATAFLOW_SIDE_EFFECTING


def rs_pair_start(tag, grads):
    n = len(grads)

    def body(*refs):
        g, theirs = refs[:n], refs[n:2 * n]
        send, recv = refs[2 * n], refs[2 * n + 1]
        x, y, c, _ = _place()
        for a in range(n):
            _start_pieces(_half(g[a], 1 - c), theirs[a],
                          lambda s, d, a=a: _rcopy(s, d, send.at[a], recv.at[a], (x, y, 1 - c)))
        refs[-1][...] = jnp.zeros_like(refs[-1])

    lands = [lax.empty(g.shape[:-2] + (g.shape[-2] // 2, g.shape[-1]), g.dtype) for g in grads]
    ops = [pltpu.with_memory_space_constraint(v, pltpu.HBM) for v in list(grads) + lands]
    sem = pltpu.SemaphoreType.DMA((n,))
    res = pl.pallas_call(
        body, name=f"rs_pair_start{tag}", in_specs=[HBM] * (2 * n),
        out_specs=[SEMS, SEMS] + [HBM] * (2 * n) + [pl.BlockSpec(memory_space=pltpu.VMEM)],
        out_shape=[sem, sem] + [pltpu.HBM(v.shape, v.dtype) for v in ops] + [jax.ShapeDtypeStruct((8, LANES), f32)],
        input_output_aliases={i: 2 + i for i in range(2 * n)},
        compiler_params=pltpu.CompilerParams(has_side_effects=EFFECT),
    )(*ops)
    return res[0], res[1], res[2:2 + n], res[2 + n:2 + 2 * n], res[-1]


def rs_pair_wait(tag, send, recv, grads, theirs, after):
    n = len(grads)

    def body(*refs):
        g, land = refs[:n], refs[n:2 * n]
        send_r, recv_r = refs[2 * n], refs[2 * n + 1]
        x, y, c, _ = _place()
        for a in range(n):
            cp = _rcopy(_half(g[a], 1 - c), land[a], send_r.at[a], recv_r.at[a], (x, y, 1 - c))
            cp.wait_send()
            cp.wait_recv()

    ops = list(grads) + list(theirs)
    res = pl.pallas_call(
        body, name=f"rs_pair_wait{tag}", in_specs=[HBM] * (2 * n) + [SEMS, SEMS] + [ANY] * len(after),
        out_specs=[HBM] * (2 * n), out_shape=[pltpu.HBM(v.shape, v.dtype) for v in ops],
        input_output_aliases={i: i for i in range(2 * n)},
        compiler_params=pltpu.CompilerParams(has_side_effects=EFFECT),
    )(*ops, send, recv, *after)
    return res[:n], res[n:]


def _chip_piece(ref, k):
    return _cols(ref, k, WIN_SHARD) if len(ref.shape) == 2 else ref.at[k]


def _chip_copy(a, k, src, dst, send, recv, me, c):
    return _rcopy(src, dst, send.at[a * NCHIP + k], recv.at[a * NCHIP + me], (k // 2, k % 2, c))


def rs_chips_start(tag, sums):
    n = len(sums)

    def pshape(s):
        return (NCHIP, s[0], WIN_SHARD) if len(s) == 2 else s

    def body(*refs):
        s, land = refs[:n], refs[n:2 * n]
        send, recv = refs[2 * n], refs[2 * n + 1]
        token = refs[-1]
        x, y, c, me = _place()
        for k in range(NCHIP):
            @pl.when(me != k)
            def _():
                for a in range(n):
                    _start_pieces(_chip_piece(s[a], k), land[a].at[me],
                                  lambda src, dst, a=a: _chip_copy(a, k, src, dst, send, recv, me, c))
        token[...] = jnp.zeros_like(token)

    lands = [lax.empty(pshape(v.shape), v.dtype) for v in sums]
    ops = [pltpu.with_memory_space_constraint(v, pltpu.HBM) for v in list(sums) + lands]
    sem = pltpu.SemaphoreType.DMA((n * NCHIP,))
    res = pl.pallas_call(
        body, name=f"rs_chips_start{tag}", in_specs=[HBM] * (2 * n),
        out_specs=[SEMS, SEMS] + [HBM] * (2 * n) + [pl.BlockSpec(memory_space=pltpu.VMEM)],
        out_shape=[sem, sem] + [pltpu.HBM(v.shape, v.dtype) for v in ops] + [jax.ShapeDtypeStruct((8, LANES), f32)],
        input_output_aliases={i: 2 + i for i in range(2 * n)},
        compiler_params=pltpu.CompilerParams(has_side_effects=EFFECT),
    )(*ops)
    return res[0], res[1], res[2:2 + n], res[2 + n:2 + 2 * n], res[-1]


def rs_chips_wait(tag, send, recv, sums, lands, after):
    n = len(sums)

    def body(*refs):
        s, land = refs[:n], refs[n:2 * n]
        send_r, recv_r = refs[2 * n], refs[2 * n + 1]
        x, y, c, me = _place()
        for k in range(NCHIP):
            @pl.when(me != k)
            def _():
                for a in range(n):
                    piece = _chip_piece(s[a], k)
                    _chip_copy(a, k, piece, land[a].at[me], send_r, recv_r, me, c).wait_send()
                    _rcopy(piece, land[a].at[k], send_r.at[a * NCHIP + k], recv_r.at[a * NCHIP + k],
                           (k // 2, k % 2, c)).wait_recv()

    ops = list(sums) + list(lands)
    res = pl.pallas_call(
        body, name=f"rs_chips_wait{tag}", in_specs=[HBM] * (2 * n) + [SEMS, SEMS] + [ANY] * len(after),
        out_specs=[HBM] * (2 * n), out_shape=[pltpu.HBM(v.shape, v.dtype) for v in ops],
        input_output_aliases={i: i for i in range(2 * n)},
        compiler_params=pltpu.CompilerParams(has_side_effects=EFFECT),
    )(*ops, send, recv, *after)
    return res[:n], res[n:]


def rs_join(tag, halves):
    n = len(halves)

    def body(*refs):
        h, other = refs[:n], refs[n:2 * n]
        send, recv = refs[2 * n:]
        x, y, c, _ = _place()

        def give(a, s, d):
            return _rcopy(s, d, send.at[a], recv.at[a], (x, y, 1 - c))

        for a in range(n):
            _start_pieces(h[a], other[a], functools.partial(give, a))
        for a in range(n):
            give(a, h[a], other[a]).wait()

    outs = [jax.ShapeDtypeStruct(v.shape, v.dtype) for v in halves]
    return pl.pallas_call(
        body, name=f"rs_join{tag}", in_specs=[ANY] * n, out_specs=[ANY] * n, out_shape=outs,
        scratch_shapes=[pltpu.SemaphoreType.DMA((n,))] * 2,
    )(*halves)


def _row_tile(rows, cols, itemsize=4, target=2 << 20):
    best = 8
    for t in range(8, rows + 1, 8):
        if rows % t == 0 and t * cols * itemsize <= target:
            best = t
    return best


GRAD_WIRE = jnp.bfloat16


def add_half(name, g, t, c):
    cols, half = t.shape[-1], t.shape[-2]
    nblk = 1 if t.ndim == 2 else t.shape[0]
    tr = _row_tile(half, cols)
    per = half // tr

    def body(c_ref, g_r, t_r, o_r):
        o_r[...] = (g_r[...] + t_r[...]).astype(o_r.dtype)

    tile_t = pl.BlockSpec((tr, cols), lambda i, c_ref: (i, 0))
    tile_g = pl.BlockSpec((tr, cols), lambda i, c_ref: ((i // per) * 2 * per + c_ref[0] * per + i % per, 0))
    out = pl.pallas_call(
        body, name=name, out_shape=jax.ShapeDtypeStruct((nblk * half, cols), GRAD_WIRE),
        grid_spec=pltpu.PrefetchScalarGridSpec(num_scalar_prefetch=1, grid=(nblk * per,), in_specs=[tile_g, tile_t],
                                               out_specs=tile_t),
        compiler_params=_cp(("parallel",)),
    )(c.reshape(1).astype(jnp.int32), g.reshape(nblk * 2 * half, cols), t.reshape(nblk * half, cols))
    return out.reshape(t.shape)


def add_chips(name, land, own):
    _, rows, cols = land.shape
    tr = _row_tile(rows, cols, target=1 << 20)

    def body(land_r, own_r, o_r):
        me = 2 * lax.axis_index("x") + lax.axis_index("y")
        for k in range(NCHIP):
            @pl.when(me == k)
            def _():
                acc = None
                for j in range(NCHIP):
                    t = (own_r[...] if j == k else land_r[j]).astype(f32)
                    acc = t if acc is None else acc + t
                o_r[...] = acc

    tile = pl.BlockSpec((tr, cols), lambda i: (i, 0))
    return pl.pallas_call(
        body, name=name, grid=(rows // tr,), in_specs=[pl.BlockSpec((NCHIP, tr, cols), lambda i: (0, i, 0)), tile],
        out_specs=tile, out_shape=jax.ShapeDtypeStruct((rows, cols), f32), compiler_params=_cp(("parallel",)),
    )(land, own)


def reduce_scatter_pair(tag, G):
    names = tuple(G)
    grads = [G[n] if G[n].ndim == 3 or n == "w_in" else G[n].reshape(NCHIP, D // NCHIP, D) for n in names]
    send, recv, grads, theirs, token = rs_pair_start(tag, grads)
    return (tag, names, send, recv, grads, theirs), token[0, 0]


def reduce_scatter_chips(state, after):
    c = lax.axis_index("c")
    tag, names, send, recv, grads, theirs = state
    grads, theirs = rs_pair_wait(tag, send, recv, grads, theirs, after)
    sums = [add_half(f"rs_add_pair{tag}_{n}", g, t, c) for n, g, t in zip(names, grads, theirs)]
    send, recv, sums, lands, token = rs_chips_start(tag, sums)
    return (tag, names, send, recv, sums, lands), token[0, 0]


def reduce_scatter_finish(state, after):
    me = 2 * lax.axis_index("x") + lax.axis_index("y")
    tag, names, send, recv, sums, lands = state
    sums, landed = rs_chips_wait(tag, send, recv, sums, lands, after)
    halves = []
    for n, s, v in zip(names, sums, landed):
        own = lax.dynamic_slice_in_dim(s, me * WIN_SHARD, WIN_SHARD, axis=1) if s.ndim == 2 else \
            lax.dynamic_index_in_dim(s, me, 0, keepdims=False)
        halves.append(add_chips(f"rs_add_chips{tag}_{n}", v, own))
    return dict(zip(names, zip(halves, rs_join(tag, halves))))


NDEV = 8


def _small_copy(r, src, dst, send, recv, x, y, c):
    return _rcopy(src, dst, send.at[r - 1], recv.at[r - 1], (x ^ (r >> 2), y ^ ((r >> 1) & 1), c ^ (r & 1)))


def small_start(pack):
    def body(p, land, send, recv, p_thru, land_thru, token):
        x, y, c, _ = _place()
        me = 4 * x + 2 * y + c
        for r in range(1, NDEV):
            _start_pieces(p, land.at[me], lambda s, d, r=r: _small_copy(r, s, d, send, recv, x, y, c), 128 << 10)
        token[...] = jnp.zeros_like(token)

    ops = [pltpu.with_memory_space_constraint(v, pltpu.HBM) for v in (pack, lax.empty((NDEV,) + pack.shape, f32))]
    sem = pltpu.SemaphoreType.DMA((NDEV - 1,))
    return pl.pallas_call(
        body, name="small_start", in_specs=[HBM, HBM],
        out_specs=[SEMS, SEMS, HBM, HBM, pl.BlockSpec(memory_space=pltpu.VMEM)],
        out_shape=[sem, sem] + [pltpu.HBM(v.shape, v.dtype) for v in ops] + [jax.ShapeDtypeStruct((8, LANES), f32)],
        input_output_aliases={0: 2, 1: 3}, compiler_params=pltpu.CompilerParams(has_side_effects=EFFECT),
    )(*ops)


def small_wait(send, recv, pack, land, after):
    def body(p, land_r, send_r, recv_r, *rest):
        x, y, c, _ = _place()
        me = 4 * x + 2 * y + c
        for r in range(1, NDEV):
            _small_copy(r, p, land_r.at[me], send_r, recv_r, x, y, c).wait_send()
            src = 4 * (x ^ (r >> 2)) + 2 * (y ^ ((r >> 1) & 1)) + (c ^ (r & 1))
            _small_copy(r, p, land_r.at[src], send_r, recv_r, x, y, c).wait_recv()

    return pl.pallas_call(
        body, name="small_wait", in_specs=[HBM, HBM, SEMS, SEMS] + [ANY] * len(after), out_specs=[HBM, HBM],
        out_shape=[pltpu.HBM(pack.shape, f32), pltpu.HBM(land.shape, f32)], input_output_aliases={0: 0, 1: 1},
        compiler_params=pltpu.CompilerParams(has_side_effects=EFFECT),
    )(pack, land, send, recv, *after)


def small_sum(land, pack):
    def body(land_r, p_r, o_r):
        me = 4 * lax.axis_index("x") + 2 * lax.axis_index("y") + lax.axis_index("c")
        for k in range(NDEV):
            @pl.when(me == k)
            def _():
                acc = None
                for d in range(NDEV):
                    t = p_r[...] if d == k else land_r[d]
                    acc = t if acc is None else acc + t
                o_r[...] = acc

    vm = pl.BlockSpec(memory_space=pltpu.VMEM)
    return pl.pallas_call(
        body, name="small_sum", in_specs=[vm, vm], out_specs=vm, out_shape=jax.ShapeDtypeStruct(pack.shape, f32),
        compiler_params=pltpu.CompilerParams(vmem_limit_bytes=40 << 20),
    )(land, pack)


def _adamw_math(w, g, m, v):
    m = ADAM_B1 * m + (1.0 - ADAM_B1) * g
    v = ADAM_B2 * v + (1.0 - ADAM_B2) * (g * g)
    m_hat = m / (1.0 - ADAM_B1 ** ADAM_STEP)
    v_hat = v / (1.0 - ADAM_B2 ** ADAM_STEP)
    return -ADAM_LR * (m_hat / (jnp.sqrt(v_hat) + ADAM_EPS) + ADAM_WD * w), m, v


def adamw_big(name, halves, w, m, v):
    _, R, C = w.shape
    tr = _row_tile(R // 2, C, target=1 << 20)
    nt = R // 2 // tr

    def body(a0, b0, a1, b1, w_r, m_r, v_r, g_o, d_o, m_o, v_o):
        mine = pl.program_id(1) == lax.axis_index("c")
        g = jnp.where(pl.program_id(0) == 0, jnp.where(mine, a0[...], b0[...]), jnp.where(mine, a1[...], b1[...]))
        g_o[...] = g
        d_o[...], m_o[...], v_o[...] = _adamw_math(w_r[...], g, m_r[...], v_r[...])

    stk = pl.BlockSpec((None, tr, C), lambda l, h, i: (l, h * nt + i, 0))
    lay0 = pl.BlockSpec((tr, C), lambda l, h, i: (jnp.where(l == 0, i, nt - 1), 0))
    lay1 = pl.BlockSpec((tr, C), lambda l, h, i: (jnp.where(l == 0, 0, i), 0))
    return pl.pallas_call(
        body, name=name, grid=(DEPTH, 2, nt),
        in_specs=[lay0, lay0, lay1, lay1, stk, stk, stk],
        out_specs=[stk] * 4, out_shape=[jax.ShapeDtypeStruct(w.shape, f32)] * 4,
        compiler_params=_cp(("arbitrary", "arbitrary", "arbitrary")),
    )(*halves[0], *halves[1], w, m, v)


def adamw_small(name, g, w, m, v):
    def body(g_r, w_r, m_r, v_r, d_o, m_o, v_o):
        d_o[...], m_o[...], v_o[...] = _adamw_math(w_r[...], g_r[...], m_r[...], v_r[...])

    return pl.pallas_call(body, name=name, out_shape=[jax.ShapeDtypeStruct(w.shape, f32)] * 3)(g, w, m, v)


WEIGHTS = ("w_in", "conv_w", "gmlp_ln_g", "gmlp_ln_b", "w_s", "b_s", "p_a", "p_b", "p_c", "w_o", "ln1_g", "ln1_b",
           "w_gate", "w_up", "w_down", "ln2_g", "ln2_b")
VECS = ("ln1_g", "ln1_b", "ln2_g", "ln2_b", "gmlp_ln_g", "gmlp_ln_b")
ROWS_VEC, ROWS_BS, ROWS_WS, ROWS_CONV = D // LANES, 8, 8 * BLK, 3 * D // LANES
ROWS_LAYER = len(VECS) * ROWS_VEC + ROWS_BS + ROWS_WS + ROWS_CONV


def _pack_small(per_layer, tail):
    parts = []
    for P in per_layer:
        parts += [P[n].reshape(ROWS_VEC, LANES) for n in VECS]
        parts += [P["b_s"].reshape(ROWS_BS, LANES), P["w_s"].reshape(ROWS_WS, LANES), P["conv_w"].reshape(ROWS_CONV, LANES)]
    return jnp.concatenate(parts + [tail], axis=0)


def _unpack_small(pack):
    out = []
    for l in range(DEPTH):
        r = l * ROWS_LAYER
        P = {}
        for n in VECS:
            P[n] = pack[r:r + ROWS_VEC].reshape(D)
            r += ROWS_VEC
        P["b_s"] = pack[r:r + ROWS_BS].reshape(8, BLK)
        r += ROWS_BS
        P["w_s"] = pack[r:r + ROWS_WS].reshape(8, BLK, BLK)
        r += ROWS_WS
        P["conv_w"] = pack[r:r + ROWS_CONV].reshape(3, D)
        out.append(P)
    return out, pack[DEPTH * ROWS_LAYER:]


def kernel(x, positions, w_in, conv_w, gmlp_ln_g, gmlp_ln_b, w_s, b_s, p_a, p_b, p_c, w_o, ln1_g, ln1_b, w_gate, w_up, w_down, ln2_g, ln2_b, loss_target, m_w_in, m_conv_w, m_gmlp_ln_g, m_gmlp_ln_b, m_w_s, m_b_s, m_p_a, m_p_b, m_p_c, m_w_o, m_ln1_g, m_ln1_b, m_w_gate, m_w_up, m_w_down, m_ln2_g, m_ln2_b, v_w_in, v_conv_w, v_gmlp_ln_g, v_gmlp_ln_b, v_w_s, v_b_s, v_p_a, v_p_b, v_p_c, v_w_o, v_ln1_g, v_ln1_b, v_w_gate, v_w_up, v_w_down, v_ln2_g, v_ln2_b):
    Wt = dict(w_in=w_in, conv_w=conv_w, gmlp_ln_g=gmlp_ln_g, gmlp_ln_b=gmlp_ln_b, w_s=w_s, b_s=b_s, p_a=p_a, p_b=p_b,
              p_c=p_c, w_o=w_o, ln1_g=ln1_g, ln1_b=ln1_b, w_gate=w_gate, w_up=w_up, w_down=w_down, ln2_g=ln2_g, ln2_b=ln2_b)
    Mt = dict(w_in=m_w_in, conv_w=m_conv_w, gmlp_ln_g=m_gmlp_ln_g, gmlp_ln_b=m_gmlp_ln_b, w_s=m_w_s, b_s=m_b_s, p_a=m_p_a,
              p_b=m_p_b, p_c=m_p_c, w_o=m_w_o, ln1_g=m_ln1_g, ln1_b=m_ln1_b, w_gate=m_w_gate, w_up=m_w_up,
              w_down=m_w_down, ln2_g=m_ln2_g, ln2_b=m_ln2_b)
    Vt = dict(w_in=v_w_in, conv_w=v_conv_w, gmlp_ln_g=v_gmlp_ln_g, gmlp_ln_b=v_gmlp_ln_b, w_s=v_w_s, b_s=v_b_s, p_a=v_p_a,
              p_b=v_p_b, p_c=v_p_c, w_o=v_w_o, ln1_g=v_ln1_g, ln1_b=v_ln1_b, w_gate=v_w_gate, w_up=v_w_up,
              w_down=v_w_down, ln2_g=v_ln2_g, ln2_b=v_ln2_b)
    chip = 2 * lax.axis_index("x") + lax.axis_index("y")
    cw = D // NCHIP

    def gathered_weights(names, arrays):
        Wl = dict(zip(names, arrays))
        for n in ("p_a", "p_c", "w_o"):
            Wl[n] = Wl[n].reshape(D, D)
        return Wl

    def small_weights(l, conv_all):
        Wl = {n: Wt[n][l] for n in VECS + ("w_s", "b_s")}
        Wl["conv_w"] = conv_all[:, l].transpose(1, 0, 2).reshape(3, D)
        return Wl

    w_in0, conv_all = gather_halves([Wt["w_in"][0].astype(MX).reshape(2, D // 2, WIN_SHARD), conv_w])
    rest = BIG[1:]
    *late0, coming0 = gather_start("0", [Wt[n][0].astype(MX) for n in rest], [conv_all])
    *late1, coming1 = gather_start("1", [Wt[n][1].astype(MX) for n in BIG], [conv_all, coming0])
    W0 = dict(small_weights(0, conv_all), w_in=w_in0.reshape(D, NIN), after=coming1[0, 0],
              late=lambda y: gathered_weights(rest, gather_wait("0", *late0, [y])))

    def W1(h):
        return dict(small_weights(1, conv_all), **gathered_weights(BIG, gather_wait("1", *late1, [h])))

    layers = [W0, W1]

    rs_state, rs_started, held = {}, {}, {}

    def start_exchange(l, g):
        if "loss" in g:
            held[l] = g
            return None
        if "conv_w" in g:
            held[l] = g
            rs_state[(l, False)], started = reduce_scatter_chips(rs_state[(l, False)], [g["w_s"], g["conv_w"]])
            if l == 0:
                pack = _pack_small([held[j] for j in range(DEPTH)], held[DEPTH]["loss"])
                *held["small"], token = small_start(pack)
                started = started + token[0, 0]
            return started
        if "dx" in g:
            rs_state[(l, True)], rs_started[(l, True)] = reduce_scatter_chips(rs_state[(l, True)], [g["dx"]])
            return rs_started[(l, True)]
        key = (l, "w_in" in g)
        rs_state[key], started = reduce_scatter_pair(f"{l}{'b' if key[1] else 'a'}", g)
        return started

    _, grad_x, _ = local_step(x[0], positions[0], loss_target[0], layers, start_exchange)

    last = jnp.zeros((8, LANES), f32) + rs_started[(0, True)]
    behind = [grad_x, last]
    red = [dict() for _ in range(DEPTH)]
    for key in ((1, False), (1, True), (0, False)):
        red[key[0]].update(reduce_scatter_finish(rs_state[key], behind))
    small, tail = _unpack_small(small_sum(*reversed(small_wait(*held["small"], behind))))
    loss = tail[0, 0]

    G, DW, NM, NV = {}, {}, {}, {}
    zc = jnp.zeros((3, D), f32)
    wp = _pack_small([{**{n: Wt[n][l] for n in VECS + ("b_s", "w_s")}, "conv_w": zc} for l in range(DEPTH)], jnp.zeros((8, LANES), f32))
    mp = _pack_small([{**{n: Mt[n][l] for n in VECS + ("b_s", "w_s")}, "conv_w": zc} for l in range(DEPTH)], jnp.zeros((8, LANES), f32))
    vp = _pack_small([{**{n: Vt[n][l] for n in VECS + ("b_s", "w_s")}, "conv_w": zc} for l in range(DEPTH)], jnp.ones((8, LANES), f32))
    gp = _pack_small(small, jnp.zeros((8, LANES), f32))
    outs = [_unpack_small(a)[0] for a in adamw_small("adamw_small", gp, wp, mp, vp)]
    for n in VECS + ("b_s", "w_s"):
        G[n] = jnp.stack([small[l][n] for l in range(DEPTH)])
        DW[n], NM[n], NV[n] = (jnp.stack([o[l][n] for l in range(DEPTH)]) for o in outs)
    gconv = jnp.stack([lax.dynamic_slice(small[l]["conv_w"], (0, chip * cw), (3, cw)) for l in range(DEPTH)])
    G["conv_w"] = gconv
    flat = lambda a: a.reshape(DEPTH * 3, cw)
    d, m2, v2 = adamw_small("adamw_conv", flat(gconv), flat(conv_w), flat(m_conv_w), flat(v_conv_w))
    DW["conv_w"], NM["conv_w"], NV["conv_w"] = (a.reshape(DEPTH, 3, cw) for a in (d, m2, v2))

    updated = {}
    for n in BIG[1:]:
        tr = (lambda a: jnp.swapaxes(a, 1, 2)) if n in ("w_gate", "w_up") else (lambda a: a)
        updated[n] = adamw_big("adamw_" + n, (red[0][n], red[1][n]), tr(Wt[n]), tr(Mt[n]), tr(Vt[n]))
        G[n], DW[n], NM[n], NV[n] = map(tr, updated[n])
    done = [d, DW["ln2_b"], red[1]["w_in"][1]] + [updated[n][1] for n in BIG[1:]]
    red[0].update(reduce_scatter_finish(rs_state[(0, True)], done))
    G["w_in"], DW["w_in"], NM["w_in"], NV["w_in"] = adamw_big(
        "adamw_w_in", (red[0]["w_in"], red[1]["w_in"]), Wt["w_in"], Mt["w_in"], Vt["w_in"])

    return (loss, grad_x[None], *[G[n] for n in WEIGHTS], *[DW[n] for n in WEIGHTS], *[NM[n] for n in WEIGHTS],
            *[NV[n] for n in WEIGHTS])
```

```python
import functools
import math

import jax
import jax.numpy as jnp
from jax import lax
from jax.experimental import pallas as pl
from jax.experimental.pallas import tpu as pltpu

D = 1024
NIN = 12800
DFF = 2816
NCHIP = 4
FB = DFF // NCHIP
WIN_SHARD = NIN // NCHIP
DEPTH = 2
GROUPS = ((128, 1), (512, 4), (2048, 16))
HD = 64
BLK = 128
AO = 512
ALPHA = (2 * DEPTH) ** 0.25
EPS = 1e-5
ROPE_THETA = 10000.0
LANES = 128
NEG = -1e30

C_GATES, C_BCH, C_QKV, C_UV = 0, 3 * D, 6 * D, 6 * D + 9 * AO

MX = jnp.bfloat16
ACT = jnp.bfloat16

ADAM_LR, ADAM_B1, ADAM_B2, ADAM_EPS, ADAM_WD, ADAM_STEP = 0.001, 0.9, 0.999, 1e-08, 0.01, 10

f32 = jnp.float32
NT = (((1,), (1,)), ((), ()))
TN = (((0,), (0,)), ((), ()))


def _cp(sem, vmem_mb=48):
    return pltpu.CompilerParams(dimension_semantics=sem, vmem_limit_bytes=vmem_mb << 20)


def _dot(a, b, dims=None):
    if dims is None:
        return jnp.dot(a, b, preferred_element_type=f32)
    return lax.dot_general(a, b, dims, preferred_element_type=f32)


def _ln_stats(r):
    mu = jnp.mean(r, axis=-1, keepdims=True)
    xc = r - mu
    var = jnp.mean(xc * xc, axis=-1, keepdims=True)
    rstd = lax.rsqrt(var + EPS)
    return xc * rstd, rstd


def _ln_bwd(dy, xhat, rstd, g):
    dxh = dy * g
    return rstd * (dxh - jnp.mean(dxh, axis=-1, keepdims=True) - xhat * jnp.mean(dxh * xhat, axis=-1, keepdims=True))


def _gelu(x):
    return 0.5 * x * (1.0 + lax.erf(x * (1.0 / math.sqrt(2.0))))


def _gelu_and_grad(x):
    cdf = 0.5 * (1.0 + lax.erf(x * (1.0 / math.sqrt(2.0))))
    return x * cdf, cdf + x * jnp.exp(-0.5 * x * x) * (1.0 / math.sqrt(2.0 * math.pi))


def _sigmoid(x):
    return 0.5 * jnp.tanh(0.5 * x) + 0.5


def _acc_rows(o_ref, first, val):
    @pl.when(first)
    def _():
        o_ref[...] = jnp.zeros_like(o_ref)
    o_ref[...] += jnp.sum(val, axis=0, keepdims=True)


def mm_in(x, w, bias):
    T = x.shape[0]
    tm, tn = min(2048, T), 1280

    def body(x_ref, w_ref, b_ref, o_ref, xb):
        @pl.when(pl.program_id(1) == 0)
        def _():
            xb[...] = x_ref[...].astype(MX)
        o_ref[...] = (_dot(xb[...], w_ref[...]) + b_ref[...]).astype(o_ref.dtype)

    return pl.pallas_call(
        body, name="mm_in", grid=(T // tm, NIN // tn),
        in_specs=[pl.BlockSpec((tm, D), lambda i, j: (i, 0), pipeline_mode=pl.Buffered(1)),
                  pl.BlockSpec((D, tn), lambda i, j: (0, j)), pl.BlockSpec((1, tn), lambda i, j: (0, j))],
        out_specs=pl.BlockSpec((tm, tn), lambda i, j: (i, j)),
        out_shape=jax.ShapeDtypeStruct((T, NIN), ACT),
        scratch_shapes=[pltpu.VMEM((tm, D), MX)],
        compiler_params=_cp(("parallel", "arbitrary")),
    )(x, w, bias)


HALO = 16
TM_AC = 256


def _uv_specs():
    return [pl.BlockSpec((TM_AC, 512), functools.partial(lambda i, j: (i, j), j=C_UV // 512 + j)) for j in range(4)]


def _gmlp_fwd(up, vp, ws_ref, bs_ref, lg, lb, u=None, gv=None):
    u = _gelu(up) if u is None else u
    xhat, rstd = _ln_stats(_gelu(vp) if gv is None else gv)
    vn = xhat * lg + lb
    vnb = vn.astype(MX)
    rows = []
    for c in range(up.shape[0] // BLK):
        r = slice(c * BLK, (c + 1) * BLK)
        rows.append(jnp.concatenate(
            [_dot(ws_ref[g], vnb[r, g * BLK:(g + 1) * BLK]) + bs_ref[g] for g in range(8)], axis=1))
    return u, vn, xhat, rstd, jnp.concatenate(rows, axis=0)


def mix_ac_fwd(proj, conv_w, wst, bsx, lg, lb):
    T = proj.shape[0]
    tm = TM_AC

    def body(bch, halo, u0, u1, v0, v1, cw, ws, bs, lg_ref, lb_ref, ya, yc, zs):
        i = pl.program_id(0)
        pb = bch[...].astype(f32)
        z = pb[:, D:2 * D] * pb[:, 2 * D:]
        hz = halo[:, :D].astype(f32) * halo[:, D:].astype(f32)
        zs[0:HALO, :] = jnp.where(i > 0, hz, 0.0)
        zs[HALO:HALO + tm, :] = z
        cv = cw[0:1, :] * zs[HALO - 2:HALO - 2 + tm, :] + cw[1:2, :] * zs[HALO - 1:HALO - 1 + tm, :] + cw[2:3, :] * z
        ya[...] = (pb[:, :D] * cv).astype(ya.dtype)
        up = jnp.concatenate([u0[...], u1[...]], axis=1).astype(f32)
        vp = jnp.concatenate([v0[...], v1[...]], axis=1).astype(f32)
        u, _, _, _, sp = _gmlp_fwd(up, vp, ws, bs, lg_ref[...], lb_ref[...])
        yc[...] = (u * sp).astype(yc.dtype)

    full = lambda shape: pl.BlockSpec(shape, lambda i: (0,) * len(shape))
    return pl.pallas_call(
        body, name="mix_ac_fwd", grid=(T // tm,),
        in_specs=[pl.BlockSpec((tm, 3 * D), lambda i: (i, 1)),
                  pl.BlockSpec((HALO, 2 * D), lambda i: (jnp.maximum(i * (tm // HALO) - 1, 0), 2)),
                  *_uv_specs(), full((3, D)), full((8, BLK, BLK)), full((8, BLK, BLK)), full((1, D)), full((1, D))],
        out_specs=[pl.BlockSpec((tm, D), lambda i: (i, 0))] * 2,
        out_shape=[jax.ShapeDtypeStruct((T, D), MX)] * 2,
        scratch_shapes=[pltpu.VMEM((HALO + tm, D), f32)],
        compiler_params=_cp(("parallel",)),
    )(proj, proj, proj, proj, proj, proj, conv_w, wst, bsx, lg, lb)


def _swap_halves(x):
    lane = lax.broadcasted_iota(jnp.int32, x.shape, 1)
    return jnp.where((lane % HD) < HD // 2, pltpu.roll(x, x.shape[1] - HD // 2, 1), pltpu.roll(x, HD // 2, 1))


def _tile4(t):
    return jnp.concatenate([t] * (AO // LANES), axis=1)


TM_FOLD = 512


def _fold_out(nat, x, out_ref, d):
    if d == 1:
        out_ref[0] = x.astype(out_ref.dtype)
        return
    rows = x.shape[0] // d
    for j in range(AO // LANES):
        nat[j] = x[:, j * LANES:(j + 1) * LANES]
    for r in range(d):
        out_ref[r] = jnp.concatenate(
            [nat.at[j][pl.ds(r, rows, stride=d), :] for j in range(AO // LANES)], axis=1).astype(out_ref.dtype)


def _unfold_in(nat, in_ref, d):
    if d == 1:
        return in_ref[0].astype(f32)
    rows = in_ref.shape[1]
    for r in range(d):
        v = in_ref[r].astype(f32)
        for j in range(AO // LANES):
            nat.at[j][pl.ds(r, rows, stride=d), :] = v[:, j * LANES:(j + 1) * LANES]
    return jnp.concatenate([nat[j] for j in range(AO // LANES)], axis=1)


def fold_rope(proj, cos_t, sin_t, g, d):
    T = proj.shape[0]
    tm = TM_FOLD
    rows = tm // d

    def body(x_ref, c_ref, s_ref, q_o, k_o, v_o, nat):
        cos, sin = _tile4(c_ref[...]), _tile4(s_ref[...])
        for part, out, scale in ((0, q_o, HD ** -0.5), (1, k_o, 1.0), (2, v_o, None)):
            x = x_ref[:, part * AO:(part + 1) * AO].astype(f32)
            if scale is not None:
                x = (x * cos + _swap_halves(x) * sin) * scale
            _fold_out(nat, x, out, d)

    fold_spec = pl.BlockSpec((d, rows, AO), lambda i: (0, i, 0))
    return pl.pallas_call(
        body, name=f"fold_rope{g}", grid=(T // tm,),
        in_specs=[pl.BlockSpec((tm, 3 * AO), lambda i: (i, C_QKV // (3 * AO) + g)),
                  pl.BlockSpec((tm, LANES), lambda i: (i, 0)), pl.BlockSpec((tm, LANES), lambda i: (i, 0))],
        out_specs=[fold_spec] * 3,
        out_shape=[jax.ShapeDtypeStruct((d, T // d, AO), MX)] * 3,
        scratch_shapes=[pltpu.VMEM((AO // LANES, tm, LANES), f32)],
        compiler_params=_cp(("parallel",)),
    )(proj, cos_t, sin_t)


def _stack_heads(x):
    lane = lax.broadcasted_iota(jnp.int32, x.shape, 1)
    z = jnp.zeros_like(x)
    return jnp.concatenate([jnp.where(lane < HD, x, z), jnp.where(lane >= HD, x, z)], axis=0)


def _unstack_heads(y):
    lane = lax.broadcasted_iota(jnp.int32, (BLK, LANES), 1)
    return jnp.where(lane < HD, y[:BLK], y[BLK:])


def _window_masks():
    row = lax.broadcasted_iota(jnp.int32, (2 * BLK, 2 * BLK), 0) % BLK
    col = lax.broadcasted_iota(jnp.int32, (2 * BLK, 2 * BLK), 1)
    return (col < BLK) & (col >= row), (col >= BLK) & (col - BLK <= row)


def _two_blocks(ref, b):
    r0 = pl.multiple_of(b * BLK, BLK)
    rp = pl.multiple_of(jnp.maximum(b - 1, 0) * BLK, BLK)
    return jnp.concatenate([ref[pl.ds(rp, BLK), :], ref[pl.ds(r0, BLK), :]], axis=0)


def _merge_masks():
    row = lax.broadcasted_iota(jnp.int32, (2 * BLK, BLK), 0) % BLK
    col = lax.broadcasted_iota(jnp.int32, (2 * BLK, BLK), 1)
    return col <= row, col == row


def attn_fwd(qf, kf, vf, g, nb):
    T = qf.shape[0]

    def body(q_ref, k_ref, v_ref, o_ref, l_ref):
        cur_m, own_m = _merge_masks()

        def step(b, carry):
            r0 = pl.multiple_of(b * BLK, BLK)
            rp = pl.multiple_of(jnp.maximum(b - 1, 0) * BLK, BLK)
            qs = _stack_heads(q_ref[pl.ds(r0, BLK), :])
            vc, vp = v_ref[pl.ds(r0, BLK), :], v_ref[pl.ds(rp, BLK), :]
            sp = jnp.where((b % nb) != 0, _dot(qs, k_ref[pl.ds(rp, BLK), :], NT), NEG)
            s = jnp.where(cur_m, _dot(qs, k_ref[pl.ds(r0, BLK), :], NT), sp)
            s_own = jnp.sum(jnp.where(own_m, sp, 0.0), axis=-1, keepdims=True)
            m = jnp.maximum(jnp.max(s, axis=-1, keepdims=True), s_own)
            p, p_own = jnp.exp(s - m), jnp.exp(s_own - m)
            l = jnp.sum(p, axis=-1, keepdims=True) + p_own
            pb = p.astype(MX)
            zero = jnp.zeros_like(pb)
            o = _dot(jnp.where(cur_m, pb, zero), vc) + _dot(jnp.where(cur_m, zero, pb), vp)
            o = (o + p_own * jnp.concatenate([vp, vp], axis=0).astype(f32)) / l
            o_ref[pl.ds(r0, BLK), :] = _unstack_heads(o).astype(o_ref.dtype)
            l_ref[pl.ds(r0, BLK), :] = _unstack_heads(jnp.broadcast_to(m + jnp.log(l), (2 * BLK, LANES)))
            return carry

        lax.fori_loop(0, T // BLK, step, 0, unroll=8)

    spec = pl.BlockSpec((T, LANES), lambda j: (0, j))
    return pl.pallas_call(
        body, name=f"attn_fwd{g}", grid=(AO // LANES,),
        in_specs=[spec] * 3, out_specs=[spec] * 2,
        out_shape=[jax.ShapeDtypeStruct((T, AO), ACT), jax.ShapeDtypeStruct((T, AO), f32)],
        compiler_params=_cp(("parallel",), 56),
    )(qf, kf, vf)


def _group_weights(lses):
    m = jnp.maximum(jnp.maximum(lses[0], lses[1]), lses[2])
    e = [jnp.exp(l - m) for l in lses]
    inv = 1.0 / (e[0] + e[1] + e[2])
    return [x * inv for x in e]


def _fold_specs(T, tm):
    specs = []
    for _, d in GROUPS:
        specs.append(pl.BlockSpec((d, tm // d, AO), lambda i: (0, i, 0)))
    return specs


def combine_fwd(os_, lses):
    T = os_[0].shape[0] * os_[0].shape[1]
    tm = TM_FOLD

    def body(o0, o1, o2, l0, l1, l2, y_ref, nat):
        o = [_unfold_in(nat, r, d) for r, (_, d) in zip((o0, o1, o2), GROUPS)]
        ls = [_unfold_in(nat, r, d) for r, (_, d) in zip((l0, l1, l2), GROUPS)]
        w = _group_weights(ls)
        y_ref[...] = (w[0] * o[0] + w[1] * o[1] + w[2] * o[2]).astype(y_ref.dtype)

    specs = _fold_specs(T, tm)
    return pl.pallas_call(
        body, name="combine_fwd", grid=(T // tm,),
        in_specs=specs + specs, out_specs=pl.BlockSpec((tm, AO), lambda i: (i, 0)),
        out_shape=jax.ShapeDtypeStruct((T, AO), MX),
        scratch_shapes=[pltpu.VMEM((AO // LANES, tm, LANES), f32)],
        compiler_params=_cp(("parallel",)),
    )(*os_, *lses)


TM_MIX = 256


def mix_out_fwd(proj, ya, yb, yc, x0, pa, pb, pc, wo, g1, b1):
    T = x0.shape[0]
    tm = min(TM_MIX, T)

    def body(gt, ya_r, yb_r, yc_r, x0_r, pa_r, pb_r, pc_r, wo_r, g_r, b_r, mabc, m_o, r1_o, x1_o, x1b_o):
        ma = _dot(ya_r[...], pa_r[...])
        ybv = yb_r[...]
        mb = jnp.concatenate([_dot(ybv, pb_r[k]) for k in range(NCHIP)], axis=1)
        mc = _dot(yc_r[...], pc_r[...])
        m = jnp.zeros((tm, D), f32)
        for j, mm in enumerate((ma, mb, mc)):
            mabc[:, j * D:(j + 1) * D] = mm.astype(mabc.dtype)
            m = m + _sigmoid(gt[:, j * D:(j + 1) * D].astype(f32)) * mm
        mb16 = m.astype(MX)
        m_o[...] = mb16
        r1 = ALPHA * x0_r[...] + _dot(mb16, wo_r[...])
        r1_o[...] = r1
        xhat, _ = _ln_stats(r1)
        x1 = xhat * g_r[...] + b_r[...]
        x1_o[...] = x1
        x1b_o[...] = x1.astype(MX)

    full = lambda shape: pl.BlockSpec(shape, lambda i: (0,) * len(shape))
    tile = lambda w: pl.BlockSpec((tm, w), lambda i: (i, 0))
    return pl.pallas_call(
        body, name="mix_out_fwd", grid=(T // tm,),
        in_specs=[tile(3 * D), tile(D), tile(AO), tile(D), tile(D), full((D, D)), full((NCHIP, AO, D // NCHIP)),
                  full((D, D)), full((D, D)), full((1, D)), full((1, D))],
        out_specs=[tile(3 * D), tile(D), tile(D), tile(D), tile(D)],
        out_shape=[jax.ShapeDtypeStruct((T, 3 * D), MX), jax.ShapeDtypeStruct((T, D), MX),
                   jax.ShapeDtypeStruct((T, D), f32), jax.ShapeDtypeStruct((T, D), f32), jax.ShapeDtypeStruct((T, D), MX)],
        compiler_params=_cp(("parallel",), 56),
    )(proj, ya, yb, yc, x0, pa, pb, pc, wo, g1, b1)


TM_FF = 512
TM_FFB = 256
ROW_CHUNK = 64


def ffn_up_fwd(x1, wg, wu):
    T = x1.shape[0]
    tm = min(TM_FFB, T)

    def body(x_r, wg_r, wu_r, g_o, u_o, h_o, gs, us):
        xb = x_r[...].astype(MX)
        for k in range(NCHIP):
            gs[...] = _dot(xb, wg_r[k])
            us[...] = _dot(xb, wu_r[k])
            for r in range(0, tm, ROW_CHUNK):
                rows = pl.ds(r, ROW_CHUNK)
                gate, up = gs[rows, :], us[rows, :]
                g_o[k, rows, :] = gate.astype(g_o.dtype)
                u_o[k, rows, :] = up.astype(u_o.dtype)
                h_o[k, rows, :] = (gate * _sigmoid(gate) * up).astype(h_o.dtype)

    wspec = pl.BlockSpec((NCHIP, D, FB), lambda i: (0, 0, 0))
    ospec = pl.BlockSpec((NCHIP, tm, FB), lambda i: (0, i, 0))
    return pl.pallas_call(
        body, name="ffn_up_fwd", grid=(T // tm,),
        in_specs=[pl.BlockSpec((tm, D), lambda i: (i, 0)), wspec, wspec],
        out_specs=[ospec] * 3,
        out_shape=[jax.ShapeDtypeStruct((NCHIP, T, FB), ACT)] * 2 + [jax.ShapeDtypeStruct((NCHIP, T, FB), MX)],
        scratch_shapes=[pltpu.VMEM((tm, FB), f32)] * 2,
        compiler_params=_cp(("parallel",)),
    )(x1, wg, wu)


def ffn_down_fwd(hh, wd, x1, g2, b2):
    T = x1.shape[0]
    tm = min(TM_FF, T)

    def body(h_r, w_r, x_r, g_r, b_r, r2_o, x2_o):
        r2 = ALPHA * x_r[...]
        for k in range(NCHIP):
            r2 = r2 + _dot(h_r[k], w_r[k])
        r2_o[...] = r2
        xhat, _ = _ln_stats(r2)
        x2_o[...] = xhat * g_r[...] + b_r[...]

    tile = pl.BlockSpec((tm, D), lambda i: (i, 0))
    vec = pl.BlockSpec((1, D), lambda i: (0, 0))
    return pl.pallas_call(
        body, name="ffn_down_fwd", grid=(T // tm,),
        in_specs=[pl.BlockSpec((NCHIP, tm, FB), lambda i: (0, i, 0)), pl.BlockSpec((NCHIP, FB, D), lambda i: (0, 0, 0)),
                  tile, vec, vec],
        out_specs=[tile, tile], out_shape=[jax.ShapeDtypeStruct((T, D), f32)] * 2,
        compiler_params=_cp(("parallel",)),
    )(hh, wd, x1, g2, b2)


def loss_grad(y, tgt):
    T = y.shape[0]
    tm = min(512, T)

    def body(y_r, t_r, l_o, dy_o):
        e = y_r[...] - t_r[...]
        dy_o[...] = e * (1.0 / D)

        @pl.when(pl.program_id(0) == 0)
        def _():
            l_o[...] = jnp.zeros_like(l_o)
        l_o[...] += (0.5 / D) * jnp.sum(e * e)

    tile = pl.BlockSpec((tm, D), lambda i: (i, 0))
    return pl.pallas_call(
        body, name="loss_grad", grid=(T // tm,),
        in_specs=[tile, tile], out_specs=[pl.BlockSpec((8, LANES), lambda i: (0, 0)), tile],
        out_shape=[jax.ShapeDtypeStruct((8, LANES), f32), jax.ShapeDtypeStruct((T, D), f32)],
        compiler_params=_cp(("arbitrary",)),
    )(y, tgt)


def ffn_down_bwd(dx2, r2, g2, wd, gate, up):
    T = dx2.shape[0]
    tm = min(TM_FFB, T)

    def body(dx_r, r_r, g_r, w_r, ga_r, up_r, dr_o, drb_o, dg_o, du_o, dlg_o, dlb_o, hs):
        i = pl.program_id(0)
        xhat, rstd = _ln_stats(r_r[...])
        dx = dx_r[...]
        _acc_rows(dlg_o, i == 0, dx * xhat)
        _acc_rows(dlb_o, i == 0, dx)
        dr = _ln_bwd(dx, xhat, rstd, g_r[...])
        dr_o[...] = dr
        drb = dr.astype(MX)
        drb_o[...] = drb
        for k in range(NCHIP):
            hs[...] = _dot(drb, w_r[k], NT)
            for r in range(0, tm, ROW_CHUNK):
                rows = pl.ds(r, ROW_CHUNK)
                dhh, gate_v, up_v = hs[rows, :], ga_r[k, rows, :].astype(f32), up_r[k, rows, :].astype(f32)
                sg = _sigmoid(gate_v)
                dg_o[k, rows, :] = (dhh * up_v * sg * (1.0 + gate_v * (1.0 - sg))).astype(dg_o.dtype)
                du_o[k, rows, :] = (dhh * gate_v * sg).astype(du_o.dtype)

    tile = pl.BlockSpec((tm, D), lambda i: (i, 0))
    vec = pl.BlockSpec((1, D), lambda i: (0, 0))
    blk = pl.BlockSpec((NCHIP, tm, FB), lambda i: (0, i, 0))
    return pl.pallas_call(
        body, name="ffn_down_bwd", grid=(T // tm,),
        in_specs=[tile, tile, vec, pl.BlockSpec((NCHIP, FB, D), lambda i: (0, 0, 0)), blk, blk],
        out_specs=[tile, tile, blk, blk, vec, vec],
        out_shape=[jax.ShapeDtypeStruct((T, D), f32), jax.ShapeDtypeStruct((T, D), MX)]
        + [jax.ShapeDtypeStruct((NCHIP, T, FB), MX)] * 2 + [jax.ShapeDtypeStruct((1, D), f32)] * 2,
        scratch_shapes=[pltpu.VMEM((tm, FB), f32)],
        compiler_params=_cp(("arbitrary",)),
    )(dx2, r2, g2, wd, gate, up)


def ffn_up_bwd(dr2, dgate, dup, wg, wu, r1, g1):
    T = dr2.shape[0]
    tm = min(TM_FFB, T)

    def body(dr2_r, dg_r, du_r, wg_r, wu_r, r1_r, g_r, dr1_o, dr1b_o, dlg_o, dlb_o):
        i = pl.program_id(0)
        dx = ALPHA * dr2_r[...]
        for k in range(NCHIP):
            dx = dx + _dot(dg_r[k], wg_r[k], NT) + _dot(du_r[k], wu_r[k], NT)
        xhat, rstd = _ln_stats(r1_r[...])
        _acc_rows(dlg_o, i == 0, dx * xhat)
        _acc_rows(dlb_o, i == 0, dx)
        dr1 = _ln_bwd(dx, xhat, rstd, g_r[...])
        dr1_o[...] = dr1
        dr1b_o[...] = dr1.astype(MX)

    tile = pl.BlockSpec((tm, D), lambda i: (i, 0))
    vec = pl.BlockSpec((1, D), lambda i: (0, 0))
    blk = pl.BlockSpec((NCHIP, tm, FB), lambda i: (0, i, 0))
    wspec = pl.BlockSpec((NCHIP, D, FB), lambda i: (0, 0, 0))
    return pl.pallas_call(
        body, name="ffn_up_bwd", grid=(T // tm,),
        in_specs=[tile, blk, blk, wspec, wspec, tile, vec],
        out_specs=[tile, tile, vec, vec],
        out_shape=[jax.ShapeDtypeStruct((T, D), f32), jax.ShapeDtypeStruct((T, D), MX)]
        + [jax.ShapeDtypeStruct((1, D), f32)] * 2,
        compiler_params=_cp(("arbitrary",)),
    )(dr2, dgate, dup, wg, wu, r1, g1)


def mix_out_bwd(dr1, proj, mabc, wo, pa, pb, pc):
    T = dr1.shape[0]
    tm = min(TM_MIX, T)

    def body(dr_r, gt, mabc_r, wo_r, pa_r, pb_r, pc_r, dmabc_o, dgt_o, dya_o, dyb_o, dyc_o):
        dm = _dot(dr_r[...].astype(MX), wo_r[...], NT)
        dmx = []
        for j in range(3):
            s = _sigmoid(gt[:, j * D:(j + 1) * D].astype(f32))
            v = (dm * s).astype(MX)
            dmx.append(v)
            dmabc_o[:, j * D:(j + 1) * D] = v
            dgt_o[:, j * D:(j + 1) * D] = (dm * mabc_r[:, j * D:(j + 1) * D].astype(f32) * s * (1.0 - s)).astype(dgt_o.dtype)
        dya_o[...] = _dot(dmx[0], pa_r[...], NT).astype(dya_o.dtype)
        dyb = jnp.zeros((tm, AO), f32)
        for k in range(NCHIP):
            dyb = dyb + _dot(dmx[1][:, k * (D // NCHIP):(k + 1) * (D // NCHIP)], pb_r[k], NT)
        dyb_o[...] = dyb.astype(dyb_o.dtype)
        dyc_o[...] = _dot(dmx[2], pc_r[...], NT).astype(dyc_o.dtype)

    full = lambda shape: pl.BlockSpec(shape, lambda i: (0,) * len(shape))
    tile = lambda w: pl.BlockSpec((tm, w), lambda i: (i, 0))
    return pl.pallas_call(
        body, name="mix_out_bwd", grid=(T // tm,),
        in_specs=[tile(D), tile(3 * D), tile(3 * D), full((D, D)), full((D, D)), full((NCHIP, AO, D // NCHIP)), full((D, D))],
        out_specs=[tile(3 * D), tile(3 * D), tile(D), tile(AO), tile(D)],
        out_shape=[jax.ShapeDtypeStruct((T, 3 * D), MX), jax.ShapeDtypeStruct((T, 3 * D), MX),
                   jax.ShapeDtypeStruct((T, D), ACT), jax.ShapeDtypeStruct((T, AO), ACT), jax.ShapeDtypeStruct((T, D), ACT)],
        compiler_params=_cp(("parallel",), 56),
    )(dr1, proj, mabc, wo, pa, pb, pc)


def transpose_cast(x):
    T = x.shape[0]
    tm = min(512, T)

    def body(x_r, o_r):
        o_r[...] = x_r[...].T.astype(o_r.dtype)

    return pl.pallas_call(
        body, name="transpose_cast", grid=(T // tm,),
        in_specs=[pl.BlockSpec((tm, D), lambda i: (i, 0))], out_specs=pl.BlockSpec((D, tm), lambda i: (0, i)),
        out_shape=jax.ShapeDtypeStruct((D, T), MX), compiler_params=_cp(("parallel",)),
    )(x)


def tn_matmul(name, a, b, a_spec, b_spec, out_shape, out_spec, grid):
    nt = len(grid) - 1

    def body(a_r, b_r, o_r):
        @pl.when(pl.program_id(nt) == 0)
        def _():
            o_r[...] = jnp.zeros_like(o_r)
        av = a_r[...].reshape(a_r.shape[-2:]).astype(MX)
        bv = b_r[...].reshape(b_r.shape[-2:]).astype(MX)
        o_r[...] += _dot(av, bv, TN).reshape(o_r.shape)

    return pl.pallas_call(
        body, name=name, grid=grid, in_specs=[a_spec, b_spec], out_specs=out_spec,
        out_shape=jax.ShapeDtypeStruct(out_shape, f32),
        compiler_params=_cp(("parallel",) * nt + ("arbitrary",), 56),
    )(a, b)


def attn_pre_bwd(dyb, os_, lses, ones):
    T = dyb.shape[0]
    tm = TM_FOLD

    def body(dy_r, o0, o1, o2, l0, l1, l2, ones_r, d0, d1, d2, f0, f1, f2, nat):
        o = [_unfold_in(nat, r, d) for r, (_, d) in zip((o0, o1, o2), GROUPS)]
        ls = [_unfold_in(nat, r, d) for r, (_, d) in zip((l0, l1, l2), GROUPS)]
        w = _group_weights(ls)
        dy = dy_r[...].astype(f32)
        t = dy * (w[0] * o[0] + w[1] * o[1] + w[2] * o[2])
        hi = t.astype(MX)
        lo = (t - hi.astype(f32)).astype(MX)
        c = _dot(hi, ones_r[...]) + _dot(lo, ones_r[...])
        for wg, do_o, df_o, (_, d) in zip(w, (d0, d1, d2), (f0, f1, f2), GROUPS):
            _fold_out(nat, wg * dy, do_o, d)
            _fold_out(nat, -wg * c, df_o, d)

    specs = _fold_specs(T, tm)
    return pl.pallas_call(
        body, name="attn_pre_bwd", grid=(T // tm,),
        in_specs=[pl.BlockSpec((tm, AO), lambda i: (i, 0))] + specs + specs + [pl.BlockSpec((AO, AO), lambda i: (0, 0))],
        out_specs=specs + specs,
        out_shape=[jax.ShapeDtypeStruct((d, T // d, AO), MX) for _, d in GROUPS]
        + [jax.ShapeDtypeStruct((d, T // d, AO), f32) for _, d in GROUPS],
        scratch_shapes=[pltpu.VMEM((AO // LANES, tm, LANES), f32)],
        compiler_params=_cp(("parallel",)),
    )(dyb, *os_, *lses, ones)


def _head_ones():
    i = jnp.arange(AO) // HD
    return (i[:, None] == i[None, :]).astype(MX)


BWD_BLOCKS = 4


def attn_bwd(qf, kf, vf, dof, lse, df, g, nb):
    T = qf.shape[0]

    def body(q_ref, k_ref, v_ref, do_ref, l_ref, d_ref, dq_ref, dk_ref, dv_ref):
        prev_m, cur_m = _window_masks()

        def head_col(ref, r0):
            v = ref[pl.ds(r0, BLK), :]
            return jnp.concatenate([v[:, 0:1], v[:, HD:HD + 1]], axis=0)

        def step(b, carry):
            dk_c, dv_c = carry
            r0 = pl.multiple_of(b * BLK, BLK)
            rp = pl.multiple_of(jnp.maximum(b - 1, 0) * BLK, BLK)
            qs, dos = _stack_heads(q_ref[pl.ds(r0, BLK), :]), _stack_heads(do_ref[pl.ds(r0, BLK), :])
            k2, v2 = _two_blocks(k_ref, b), _two_blocks(v_ref, b)
            valid = cur_m | (prev_m & ((b % nb) != 0))
            p = jnp.where(valid, jnp.exp(_dot(qs, k2, NT) - head_col(l_ref, r0)), 0.0)
            ds = (p * (_dot(dos, v2, NT) + head_col(d_ref, r0))).astype(MX)
            dq_ref[pl.ds(r0, BLK), :] = _unstack_heads(_dot(ds, k2)).astype(dq_ref.dtype)
            dk2 = _dot(ds, qs, TN)
            dv2 = _dot(p.astype(MX), dos, TN)
            dk_ref[pl.ds(rp, BLK), :] = (dk_c + dk2[:BLK]).astype(dk_ref.dtype)
            dv_ref[pl.ds(rp, BLK), :] = (dv_c + dv2[:BLK]).astype(dv_ref.dtype)
            return dk2[BLK:], dv2[BLK:]

        zero = jnp.zeros((BLK, LANES), f32)

        def steps(i, carry):
            for j in range(BWD_BLOCKS):
                carry = step(BWD_BLOCKS * i + j, carry)
            return carry

        dk_c, dv_c = lax.fori_loop(0, T // BLK // BWD_BLOCKS, steps, (zero, zero))
        dk_ref[pl.ds(T - BLK, BLK), :] = dk_c.astype(dk_ref.dtype)
        dv_ref[pl.ds(T - BLK, BLK), :] = dv_c.astype(dv_ref.dtype)

    spec = pl.BlockSpec((T, LANES), lambda j: (0, j))
    return pl.pallas_call(
        body, name=f"attn_bwd{g}", grid=(AO // LANES,),
        in_specs=[spec] * 6, out_specs=[spec] * 3,
        out_shape=[jax.ShapeDtypeStruct((T, AO), MX)] * 3,
        compiler_params=_cp(("parallel",), 60),
    )(qf, kf, vf, dof, lse, df)


def unfold_rope_bwd(dqf, dkf, dvf, cos_t, sin_t, g, d):
    T = dqf.shape[0] * dqf.shape[1]
    tm = TM_FOLD

    def body(q_r, k_r, v_r, c_ref, s_ref, o_ref, nat):
        cos, sin = _tile4(c_ref[...]), _tile4(s_ref[...])
        for part, ref, scale in ((0, q_r, HD ** -0.5), (1, k_r, 1.0), (2, v_r, None)):
            x = _unfold_in(nat, ref, d)
            if scale is not None:
                x = (x * cos - _swap_halves(x) * sin) * scale
            o_ref[:, part * AO:(part + 1) * AO] = x.astype(o_ref.dtype)

    fold_spec = pl.BlockSpec((d, tm // d, AO), lambda i: (0, i, 0))
    tab = pl.BlockSpec((tm, LANES), lambda i: (i, 0))
    return pl.pallas_call(
        body, name=f"unfold_rope_bwd{g}", grid=(T // tm,),
        in_specs=[fold_spec] * 3 + [tab, tab],
        out_specs=pl.BlockSpec((tm, 3 * AO), lambda i: (i, 0)),
        out_shape=jax.ShapeDtypeStruct((T, 3 * AO), MX),
        scratch_shapes=[pltpu.VMEM((AO // LANES, tm, LANES), f32)],
        compiler_params=_cp(("parallel",)),
    )(dqf, dkf, dvf, cos_t, sin_t)


CONV_CHUNK = 32


def conv_bwd(dya, proj, conv_w):
    T = dya.shape[0]
    tm = TM_AC
    last = T // tm - 1

    def body(dy_r, bch, hprev, dy_next, b_next, cw, d_o, dw_o, zs, ds):
        i = pl.program_id(0)
        ch = CONV_CHUNK
        hz = hprev[:, :D].astype(f32) * hprev[:, D:].astype(f32)
        zs[0:HALO, :] = jnp.where(i > 0, hz, 0.0)
        ds[tm:tm + HALO, :] = jnp.where(i < last, dy_next[...].astype(f32) * b_next[...].astype(f32), 0.0)
        for r in range(0, tm, ch):
            zs[HALO + r:HALO + r + ch, :] = bch[r:r + ch, D:2 * D].astype(f32) * bch[r:r + ch, 2 * D:].astype(f32)
            ds[r:r + ch, :] = dy_r[r:r + ch, :].astype(f32) * bch[r:r + ch, :D].astype(f32)

        @pl.when(i == 0)
        def _():
            dw_o[...] = jnp.zeros_like(dw_o)

        sums = [jnp.zeros((1, D), f32) for _ in range(3)]
        for r in range(0, tm, ch):
            z2, z1, z = (zs[HALO + r - s:HALO + r - s + ch, :] for s in (2, 1, 0))
            dcv, d1, d2 = (ds[r + s:r + s + ch, :] for s in (0, 1, 2))
            cv = cw[0:1, :] * z2 + cw[1:2, :] * z1 + cw[2:3, :] * z
            dz = cw[2:3, :] * dcv + cw[1:2, :] * d1 + cw[0:1, :] * d2
            d_o[r:r + ch, :D] = (dy_r[r:r + ch, :].astype(f32) * cv).astype(d_o.dtype)
            d_o[r:r + ch, D:2 * D] = (dz * bch[r:r + ch, 2 * D:].astype(f32)).astype(d_o.dtype)
            d_o[r:r + ch, 2 * D:] = (dz * bch[r:r + ch, D:2 * D].astype(f32)).astype(d_o.dtype)
            for k, zz in enumerate((z2, z1, z)):
                sums[k] = sums[k] + jnp.sum(dcv * zz, axis=0, keepdims=True)
        for k in range(3):
            dw_o[k:k + 1, :] += sums[k]

    nh = tm // HALO
    return pl.pallas_call(
        body, name="conv_bwd", grid=(T // tm,),
        in_specs=[pl.BlockSpec((tm, D), lambda i: (i, 0)), pl.BlockSpec((tm, 3 * D), lambda i: (i, 1)),
                  pl.BlockSpec((HALO, 2 * D), lambda i: (jnp.maximum(i * nh - 1, 0), 2)),
                  pl.BlockSpec((HALO, D), lambda i: (jnp.minimum((i + 1) * nh, T // HALO - 1), 0)),
                  pl.BlockSpec((HALO, D), lambda i: (jnp.minimum((i + 1) * nh, T // HALO - 1), 3)),
                  pl.BlockSpec((3, D), lambda i: (0, 0))],
        out_specs=[pl.BlockSpec((tm, 3 * D), lambda i: (i, 0)), pl.BlockSpec((3, D), lambda i: (0, 0))],
        out_shape=[jax.ShapeDtypeStruct((T, 3 * D), MX), jax.ShapeDtypeStruct((3, D), f32)],
        scratch_shapes=[pltpu.VMEM((HALO + tm, D), f32), pltpu.VMEM((tm + HALO, D), f32)],
        compiler_params=_cp(("arbitrary",)),
    )(dya, proj, proj, dya, proj, conv_w)


def gmlp_bwd(dyc, proj, wst, bsx, lg, lb):
    T = dyc.shape[0]
    tm = TM_AC
    last = T // tm - 1

    def body(dy_r, u0, u1, v0, v1, ws, bs, lg_r, lb_r, d_o, dws_o, dbs_o, dlg_o, dlb_o, bacc):
        i = pl.program_id(0)
        up = jnp.concatenate([u0[...], u1[...]], axis=1).astype(f32)
        vp = jnp.concatenate([v0[...], v1[...]], axis=1).astype(f32)
        u, du = _gelu_and_grad(up)
        gv, dgv = _gelu_and_grad(vp)
        u, vn, xhat, rstd, sp = _gmlp_fwd(up, vp, ws, bs, lg_r[...], lb_r[...], u, gv)
        dy = dy_r[...].astype(f32)
        d_o[:, :D] = (dy * sp * du).astype(d_o.dtype)
        dsp = dy * u
        dspb, vnb = dsp.astype(MX), vn.astype(MX)

        @pl.when(i == 0)
        def _():
            dws_o[...] = jnp.zeros_like(dws_o)
            bacc[...] = jnp.zeros_like(bacc)

        rows = []
        for c in range(tm // BLK):
            r = slice(c * BLK, (c + 1) * BLK)
            cols = []
            for g in range(8):
                cs = slice(g * BLK, (g + 1) * BLK)
                dws_o[g] += _dot(dspb[r, cs], vnb[r, cs], NT)
                bacc[g] += dsp[r, cs]
                cols.append(_dot(ws[g], dspb[r, cs], TN))
            rows.append(jnp.concatenate(cols, axis=1))
        dvn = jnp.concatenate(rows, axis=0)
        _acc_rows(dlg_o, i == 0, dvn * xhat)
        _acc_rows(dlb_o, i == 0, dvn)
        d_o[:, D:] = (_ln_bwd(dvn, xhat, rstd, lg_r[...]) * dgv).astype(d_o.dtype)

        @pl.when(i == last)
        def _():
            row = lax.broadcasted_iota(jnp.int32, (BLK, BLK), 0)
            col = lax.broadcasted_iota(jnp.int32, (BLK, BLK), 1)
            ones = jnp.ones((8, BLK), MX)
            for g in range(8):
                dws_o[g] = jnp.where(col <= row, dws_o[g], 0.0)
                a = bacc[g]
                hi = a.astype(MX)
                lo = (a - hi.astype(f32)).astype(MX)
                dbs_o[g:g + 1, :] = (_dot(ones, hi, NT) + _dot(ones, lo, NT))[0:1, :]

    full = lambda shape: pl.BlockSpec(shape, lambda i: (0,) * len(shape))
    return pl.pallas_call(
        body, name="gmlp_bwd", grid=(T // tm,),
        in_specs=[pl.BlockSpec((tm, D), lambda i: (i, 0)), *_uv_specs(), full((8, BLK, BLK)), full((8, BLK, BLK)),
                  full((1, D)), full((1, D))],
        out_specs=[pl.BlockSpec((tm, 2 * D), lambda i: (i, 0)), full((8, BLK, BLK)), full((8, BLK)), full((1, D)), full((1, D))],
        out_shape=[jax.ShapeDtypeStruct((T, 2 * D), MX), jax.ShapeDtypeStruct((8, BLK, BLK), f32),
                   jax.ShapeDtypeStruct((8, BLK), f32), jax.ShapeDtypeStruct((1, D), f32), jax.ShapeDtypeStruct((1, D), f32)],
        scratch_shapes=[pltpu.VMEM((8, BLK, BLK), f32)],
        compiler_params=_cp(("arbitrary",)),
    )(dyc, proj, proj, proj, proj, wst, bsx, lg, lb)


PART_TILES = (6, 6, 3, 3, 3, 4)
PART_START = (0, 6, 12, 15, 18, 21)
TJ = 512


def _part_specs(tm, rows_axis):
    specs = []
    for n, s in zip(PART_TILES, PART_START):
        def imap(*idx, n=n, s=s):
            i, j = idx[rows_axis], idx[1 - rows_axis]
            inside = (j >= s) & (j < s + n)
            return (jnp.where(inside, i, 0), jnp.clip(j - s, 0, n - 1))
        specs.append(pl.BlockSpec((tm, TJ), imap))
    return specs


def _with_part(j, refs, fn):
    for r, n, s in zip(refs, PART_TILES, PART_START):
        @pl.when((j >= s) & (j < s + n))
        def _():
            fn(r[...])


def dx_in(dr1, parts, w, bias):
    T = dr1.shape[0]
    tm = min(2048, T)

    def body(dr_r, p0, p1, p2, p3, p4, p5, w_r, b_r, o_r):
        j = pl.program_id(1)

        @pl.when(j == 0)
        def _():
            o_r[...] = ALPHA * dr_r[...] + b_r[...]

        def acc(tile):
            o_r[...] += _dot(tile, w_r[...], NT)
        _with_part(j, (p0, p1, p2, p3, p4, p5), acc)

    once = dict(pipeline_mode=pl.Buffered(1))
    return pl.pallas_call(
        body, name="dx_in", grid=(T // tm, NIN // TJ),
        in_specs=[pl.BlockSpec((tm, D), lambda i, j: (i, 0), **once)] + _part_specs(tm, 0)
        + [pl.BlockSpec((D, TJ), lambda i, j: (0, j)), pl.BlockSpec((1, D), lambda i, j: (0, 0))],
        out_specs=pl.BlockSpec((tm, D), lambda i, j: (i, 0), **once),
        out_shape=jax.ShapeDtypeStruct((T, D), f32),
        compiler_params=_cp(("parallel", "arbitrary"), 56),
    )(dr1, *parts, w, bias)


def dw_in(x0t, parts):
    T = x0t.shape[1]
    tk = min(2048, T)

    def body(x_r, p0, p1, p2, p3, p4, p5, o_r):
        j, t = pl.program_id(0), pl.program_id(1)

        @pl.when(t == 0)
        def _():
            o_r[...] = jnp.zeros_like(o_r)

        def acc(tile):
            o_r[...] += _dot(x_r[:, pl.ds(pl.multiple_of(t * tk, tk), tk)], tile)
        _with_part(j, (p0, p1, p2, p3, p4, p5), acc)

    return pl.pallas_call(
        body, name="dw_in", grid=(NIN // TJ, T // tk),
        in_specs=[pl.BlockSpec((D, T), lambda j, t: (0, 0), pipeline_mode=pl.Buffered(1))] + _part_specs(tk, 1),
        out_specs=pl.BlockSpec((D, TJ), lambda j, t: (0, j)),
        out_shape=jax.ShapeDtypeStruct((D, NIN), f32),
        compiler_params=_cp(("parallel", "arbitrary"), 56),
    )(x0t, *parts)


def rope_tables(positions):
    half = HD // 2
    inv_freq = ROPE_THETA ** (-jnp.arange(half, dtype=f32) / half)
    ang = positions.astype(f32)[:, None] * inv_freq
    cos, sin = jnp.cos(ang), jnp.sin(ang)
    return jnp.tile(cos, (1, LANES // half)), jnp.tile(jnp.concatenate([-sin, sin], axis=1), (1, LANES // HD))


def _flat(a):
    return a.reshape(a.shape[0] * a.shape[1], a.shape[2])


def layer_fwd(x0, W, cos_t, sin_t):
    T = x0.shape[0]
    proj = mm_in(x0, W["w_in"], W["in_bias"])
    ya, yc = mix_ac_fwd(proj, W["conv_w"], W["wst"], W["bsx"], W["gmlp_ln_g"], W["gmlp_ln_b"])
    folded, os_, lses = [], [], []
    for g, (_, d) in enumerate(GROUPS):
        qf, kf, vf = fold_rope(proj, cos_t, sin_t, g, d)
        o, lse = attn_fwd(_flat(qf), _flat(kf), _flat(vf), g, T // d // BLK)
        folded.append((qf, kf, vf))
        os_.append(o.reshape(d, T // d, AO))
        lses.append(lse.reshape(d, T // d, AO))
    yb = combine_fwd(os_, lses)
    if "late" in W:
        W = {**W, **W["late"](yb)}
    mabc, m, r1, x1, x1b = mix_out_fwd(proj, ya, yb, yc, x0, W["p_a"], W["p_b"], W["p_c"], W["w_o"], W["ln1_g"], W["ln1_b"])
    gate, up, hh = ffn_up_fwd(x1b, W["w_gate"], W["w_up"])
    r2, x2 = ffn_down_fwd(hh, W["w_down"], x1, W["ln2_g"], W["ln2_b"])
    saved = dict(x0=x0, proj=proj, ya=ya, yb=yb, yc=yc, folded=folded, os=os_, lses=lses, mabc=mabc, m=m, r1=r1,
                 x1b=x1b, gate=gate, up=up, hh=hh, r2=r2)
    return x2, saved, W


def layer_bwd(dx2, S, W, cos_t, sin_t, on_grads=None):
    T = dx2.shape[0]
    tk = min(2048, T)
    G = {}
    dr2, dr2b, dgate, dup, G["ln2_g"], G["ln2_b"] = ffn_down_bwd(dx2, S["r2"], W["ln2_g"], W["w_down"], S["gate"], S["up"])
    blk_a = pl.BlockSpec((1, tk, FB), lambda k, t: (k, t, 0))
    row_b = pl.BlockSpec((tk, D), lambda k, t: (t, 0))
    G["w_down"] = tn_matmul("dw_down", S["hh"], dr2b, blk_a, row_b, (NCHIP, FB, D),
                            pl.BlockSpec((1, FB, D), lambda k, t: (k, 0, 0)), (NCHIP, T // tk))
    for nm, dv in (("w_gate", dgate), ("w_up", dup)):
        G[nm] = tn_matmul("d" + nm, dv, S["x1b"], blk_a, row_b, (NCHIP, FB, D),
                          pl.BlockSpec((1, FB, D), lambda k, t: (k, 0, 0)), (NCHIP, T // tk))
    dr1, dr1b, G["ln1_g"], G["ln1_b"] = ffn_up_bwd(dr2, dgate, dup, W["w_gate"], W["w_up"], S["r1"], W["ln1_g"])
    dmabc, dgates, dya, dyb, dyc = mix_out_bwd(dr1b, S["proj"], S["mabc"], W["w_o"], W["p_a"], W["p_b"], W["p_c"])
    one = (1, T // tk)
    full_o = pl.BlockSpec((D, D), lambda k, t: (0, 0))
    G["w_o"] = tn_matmul("dw_o", S["m"], dr1b, row_b, row_b, (D, D), full_o, one)
    G["p_a"] = tn_matmul("dp_a", S["ya"], dmabc, row_b, pl.BlockSpec((tk, D), lambda k, t: (t, 0)), (D, D), full_o, one)
    G["p_c"] = tn_matmul("dp_c", S["yc"], dmabc, row_b, pl.BlockSpec((tk, D), lambda k, t: (t, 2)), (D, D), full_o, one)
    G["p_b"] = tn_matmul("dp_b", S["yb"], dmabc, pl.BlockSpec((tk, AO), lambda k, t: (t, 0)),
                         pl.BlockSpec((tk, D // NCHIP), lambda k, t: (t, NCHIP + k)), (NCHIP, AO, D // NCHIP),
                         pl.BlockSpec((1, AO, D // NCHIP), lambda k, t: (k, 0, 0)), (NCHIP, T // tk))
    conv_w = W["conv_w"]
    if on_grads is not None:
        conv_w = conv_w + on_grads({n: G[n] for n in BIG if n != "w_in"})
    dbch, G["conv_w"] = conv_bwd(dya, S["proj"], conv_w)
    duv, G["w_s"], G["b_s"], G["gmlp_ln_g"], G["gmlp_ln_b"] = gmlp_bwd(
        dyc, S["proj"], W["wst"], W["bsx"], W["gmlp_ln_g"], W["gmlp_ln_b"])
    ones = _head_ones()
    if on_grads is not None:
        small = {n: G[n] for n in VECS + ("b_s", "w_s", "conv_w")}
        ones = ones + on_grads(small).astype(MX)
    pre = attn_pre_bwd(dyb, S["os"], S["lses"], ones)
    dqkv = []
    for g, (_, d) in enumerate(GROUPS):
        qf, kf, vf = S["folded"][g]
        dqf, dkf, dvf = attn_bwd(_flat(qf), _flat(kf), _flat(vf), _flat(pre[g]), _flat(S["lses"][g]), _flat(pre[3 + g]),
                                 g, T // d // BLK)
        shp = (d, T // d, AO)
        dqkv.append(unfold_rope_bwd(dqf.reshape(shp), dkf.reshape(shp), dvf.reshape(shp), cos_t, sin_t, g, d))
    parts = (dgates, dbch, *dqkv, duv)
    G["w_in"] = dw_in(transpose_cast(S["x0"]), parts)
    bias = jnp.zeros((1, D), f32)
    if on_grads is not None:
        bias = bias + on_grads({"w_in": G["w_in"]})
    dx0 = dx_in(dr1, parts, W["w_in"], bias)
    started = on_grads({"dx": dx0}) if on_grads is not None else None
    return dx0, G, started


def prep_layer_weights(Wl):
    W = dict(Wl)
    tril = jnp.tril(jnp.ones((BLK, BLK), f32))
    W["wst"] = (Wl["w_s"] * tril[None]).astype(MX)
    W["bsx"] = jnp.broadcast_to(Wl["b_s"][:, :, None], (8, BLK, BLK))
    for n in ("gmlp_ln_g", "gmlp_ln_b", "ln1_g", "ln1_b", "ln2_g", "ln2_b"):
        W[n] = Wl[n].reshape(1, D)
    W["in_bias"] = jnp.zeros((1, NIN), f32) + Wl.get("after", 0.0)
    return W


def local_step(x, positions, target, layers, on_grads=None):
    cos_t, sin_t = rope_tables(positions)
    Ws, saved = [], []
    h = x
    for Wl in layers:
        h, S, W = layer_fwd(h, prep_layer_weights(Wl(h) if callable(Wl) else Wl), cos_t, sin_t)
        Ws.append(W)
        saved.append(S)
    lsum, dh = loss_grad(h, target)
    if on_grads is not None:
        on_grads(len(Ws), {"loss": lsum})
    grads = [None] * len(Ws)
    started = None
    for l in reversed(range(len(Ws))):
        W = Ws[l]
        if started is not None:
            W = dict(W, ln2_g=W["ln2_g"] + started)
        hook = functools.partial(on_grads, l) if on_grads is not None else None
        dh, grads[l], started = layer_bwd(dh, saved[l], W, cos_t, sin_t, hook)
    return lsum, dh, grads


MESH = pl.DeviceIdType.MESH
ANY = pl.BlockSpec(memory_space=pl.ANY)
BIG = ("w_in", "w_gate", "w_up", "w_down", "p_a", "p_b", "p_c", "w_o")
NBIG = len(BIG)


def _place():
    x, y, c = lax.axis_index("x"), lax.axis_index("y"), lax.axis_index("c")
    return x, y, c, 2 * x + y


def _rcopy(src, dst, send, recv, dev):
    return pltpu.make_async_remote_copy(src_ref=src, dst_ref=dst, send_sem=send, recv_sem=recv, device_id=dev,
                                        device_id_type=MESH)


def _cols(ref, k, width):
    start = k * width if isinstance(k, int) else pl.multiple_of(k * width, LANES)
    return ref.at[:, pl.ds(start, width)]


CHUNK_BYTES = 1 << 20


def _pieces(shape, itemsize, nbytes=CHUNK_BYTES):
    rows, cols = shape[-2], shape[-1]
    per = max(16, nbytes // (cols * itemsize) // 16 * 16)
    out = []
    for lead in (range(shape[0]) if len(shape) == 3 else (None,)):
        for r in range(0, rows, per):
            sl = (pl.ds(r, min(per, rows - r)), slice(None))
            out.append(sl if lead is None else (lead,) + sl)
    return out


def _start_pieces(src, dst, make, nbytes=CHUNK_BYTES):
    for idx in _pieces(src.shape, jnp.dtype(src.dtype).itemsize, nbytes):
        make(src.at[idx], dst.at[idx]).start()


def gather_halves(shards):
    n = len(shards)

    def body(*refs):
        srcs, dsts = refs[:n], refs[n:2 * n]
        send, recv, own_send, own_recv = refs[2 * n:]
        x, y, c, k = _place()
        sib = (x, y, 1 - c)
        chips = [(1 - x, y), (x, 1 - y), (1 - x, 1 - y)]

        def slot(a, layer, pos):
            if a == 0:
                return _cols(dsts[0].at[layer], pos, WIN_SHARD)
            return dsts[a].at[pos, layer]

        def ici(a, j, src, dst):
            return _rcopy(src, dst, send.at[a, j], recv.at[a, j], (*chips[j], c))

        def d2d(a, j, src, dst):
            return _rcopy(src, dst, send.at[a, 3 + j], recv.at[a, 3 + j], sib)

        def own(a, layer, src, dst):
            return _rcopy(src, dst, own_send.at[a, layer], own_recv.at[a, layer], sib)

        for a in range(n):
            for j in range(3):
                _start_pieces(srcs[a].at[c], slot(a, c, k), functools.partial(ici, a, j))
        for a in range(n):
            for layer in range(DEPTH):
                _start_pieces(srcs[a].at[layer], slot(a, layer, k), functools.partial(own, a, layer))
        for a in range(n):
            for j, (cx, cy) in enumerate(chips):
                landed = slot(a, c, 2 * cx + cy)
                ici(a, j, landed, landed).wait_recv()
                _start_pieces(landed, landed, functools.partial(d2d, a, j))
        for a in range(n):
            for j, (cx, cy) in enumerate(chips):
                passed = slot(a, 1 - c, 2 * cx + cy)
                d2d(a, j, passed, passed).wait_recv()
                landed = slot(a, c, 2 * cx + cy)
                d2d(a, j, landed, landed).wait_send()
                ici(a, j, srcs[a].at[c], slot(a, c, k)).wait_send()
            for layer in range(DEPTH):
                own(a, layer, srcs[a].at[layer], slot(a, layer, k)).wait()

    outs = [jax.ShapeDtypeStruct((2, shards[0].shape[1], NIN), shards[0].dtype)]
    outs += [jax.ShapeDtypeStruct((NCHIP,) + s.shape, s.dtype) for s in shards[1:]]
    return pl.pallas_call(
        body, name="gather_halves", in_specs=[ANY] * n, out_specs=[ANY] * n, out_shape=outs,
        scratch_shapes=[pltpu.SemaphoreType.DMA((n, 6)), pltpu.SemaphoreType.DMA((n, 6)),
                        pltpu.SemaphoreType.DMA((n, DEPTH)), pltpu.SemaphoreType.DMA((n, DEPTH))],
    )(*shards)


def _gather_slot(dst, pos):
    return _cols(dst, pos, WIN_SHARD) if len(dst.shape) == 2 else dst.at[pos]


def _gather_copy(a, j, src, dst, send, recv, dev):
    return _rcopy(src, dst, send.at[a * NCHIP + j], recv.at[a * NCHIP + j], dev)


def gather_start(tag, shards, after):
    n = len(shards)

    def body(*refs):
        srcs, dsts = refs[:n], refs[n:2 * n]
        send, recv = refs[2 * n + len(after)], refs[2 * n + len(after) + 1]
        token = refs[-1]
        x, y, c, k = _place()
        peers = [(1 - x, y, c), (x, 1 - y, c), (1 - x, 1 - y, c), (x, y, 1 - c)]
        for a in range(n):
            for j, dev in enumerate(peers):
                _start_pieces(srcs[a], _gather_slot(dsts[a], k),
                              lambda s, d, a=a, j=j, dev=dev: _gather_copy(a, j, s, d, send, recv, dev))
        token[...] = jnp.zeros_like(token)

    gathered = [lax.empty((D, NIN) if s.shape == (D, WIN_SHARD) else (NCHIP,) + s.shape, s.dtype) for s in shards]
    ops = [pltpu.with_memory_space_constraint(v, pltpu.HBM) for v in list(shards) + gathered]
    sem = pltpu.SemaphoreType.DMA((n * NCHIP,))
    res = pl.pallas_call(
        body, name=f"gather_start{tag}", in_specs=[HBM] * (2 * n) + [ANY] * len(after),
        out_specs=[SEMS, SEMS] + [HBM] * (2 * n) + [pl.BlockSpec(memory_space=pltpu.VMEM)],
        out_shape=[sem, sem] + [pltpu.HBM(v.shape, v.dtype) for v in ops] + [jax.ShapeDtypeStruct((8, LANES), f32)],
        input_output_aliases={i: 2 + i for i in range(2 * n)},
        compiler_params=pltpu.CompilerParams(has_side_effects=EFFECT),
    )(*ops, *after)
    return res[0], res[1], res[2:2 + n], res[2 + n:2 + 2 * n], res[-1]


def gather_wait(tag, send, recv, shards, gathered, after):
    n = len(shards)

    def body(*refs):
        srcs, dsts = refs[:n], refs[n:2 * n]
        send_r, recv_r = refs[2 * n], refs[2 * n + 1]
        x, y, c, k = _place()
        peers = [(1 - x, y, c), (x, 1 - y, c), (1 - x, 1 - y, c), (x, y, 1 - c)]
        for a in range(n):
            for j, dev in enumerate(peers):
                _gather_copy(a, j, srcs[a], _gather_slot(dsts[a], k), send_r, recv_r, dev).wait_send()
                pos = 2 * dev[0] + dev[1]
                _gather_copy(a, j, srcs[a], _gather_slot(dsts[a], pos), send_r, recv_r, dev).wait_recv()

    ops = list(shards) + list(gathered)
    res = pl.pallas_call(
        body, name=f"gather_wait{tag}", in_specs=[HBM] * (2 * n) + [SEMS, SEMS] + [ANY] * len(after),
        out_specs=[HBM] * (2 * n), out_shape=[pltpu.HBM(v.shape, v.dtype) for v in ops],
        input_output_aliases={i: i for i in range(2 * n)},
        compiler_params=pltpu.CompilerParams(has_side_effects=EFFECT),
    )(*ops, send, recv, *after)
    return res[n:]


def _half(ref, h):
    rows = ref.shape[-2] // 2
    start = pl.multiple_of(h * rows, 16)
    if len(ref.shape) == 2:
        return ref.at[pl.ds(start, rows), :]
    return ref.at[:, pl.ds(start, rows), :]


HBM = pl.BlockSpec(memory_space=pltpu.HBM)
SEMS = pl.BlockSpec(memory_space=pltpu.SEMAPHORE)
EFFECT = pltpu.SideEffectType.DATAFLOW_SIDE_EFFECTING


def rs_pair_start(tag, grads):
    n = len(grads)

    def body(*refs):
        g, theirs = refs[:n], refs[n:2 * n]
        send, recv = refs[2 * n], refs[2 * n + 1]
        x, y, c, _ = _place()
        for a in range(n):
            _start_pieces(_half(g[a], 1 - c), theirs[a],
                          lambda s, d, a=a: _rcopy(s, d, send.at[a], recv.at[a], (x, y, 1 - c)))
        refs[-1][...] = jnp.zeros_like(refs[-1])

    lands = [lax.empty(g.shape[:-2] + (g.shape[-2] // 2, g.shape[-1]), g.dtype) for g in grads]
    ops = [pltpu.with_memory_space_constraint(v, pltpu.HBM) for v in list(grads) + lands]
    sem = pltpu.SemaphoreType.DMA((n,))
    res = pl.pallas_call(
        body, name=f"rs_pair_start{tag}", in_specs=[HBM] * (2 * n),
        out_specs=[SEMS, SEMS] + [HBM] * (2 * n) + [pl.BlockSpec(memory_space=pltpu.VMEM)],
        out_shape=[sem, sem] + [pltpu.HBM(v.shape, v.dtype) for v in ops] + [jax.ShapeDtypeStruct((8, LANES), f32)],
        input_output_aliases={i: 2 + i for i in range(2 * n)},
        compiler_params=pltpu.CompilerParams(has_side_effects=EFFECT),
    )(*ops)
    return res[0], res[1], res[2:2 + n], res[2 + n:2 + 2 * n], res[-1]


def rs_pair_wait(tag, send, recv, grads, theirs, after):
    n = len(grads)

    def body(*refs):
        g, land = refs[:n], refs[n:2 * n]
        send_r, recv_r = refs[2 * n], refs[2 * n + 1]
        x, y, c, _ = _place()
        for a in range(n):
            cp = _rcopy(_half(g[a], 1 - c), land[a], send_r.at[a], recv_r.at[a], (x, y, 1 - c))
            cp.wait_send()
            cp.wait_recv()

    ops = list(grads) + list(theirs)
    res = pl.pallas_call(
        body, name=f"rs_pair_wait{tag}", in_specs=[HBM] * (2 * n) + [SEMS, SEMS] + [ANY] * len(after),
        out_specs=[HBM] * (2 * n), out_shape=[pltpu.HBM(v.shape, v.dtype) for v in ops],
        input_output_aliases={i: i for i in range(2 * n)},
        compiler_params=pltpu.CompilerParams(has_side_effects=EFFECT),
    )(*ops, send, recv, *after)
    return res[:n], res[n:]


def _chip_piece(ref, k):
    return _cols(ref, k, WIN_SHARD) if len(ref.shape) == 2 else ref.at[k]


def _chip_copy(a, k, src, dst, send, recv, me, c):
    return _rcopy(src, dst, send.at[a * NCHIP + k], recv.at[a * NCHIP + me], (k // 2, k % 2, c))


def rs_chips_start(tag, sums):
    n = len(sums)

    def pshape(s):
        return (NCHIP, s[0], WIN_SHARD) if len(s) == 2 else s

    def body(*refs):
        s, land = refs[:n], refs[n:2 * n]
        send, recv = refs[2 * n], refs[2 * n + 1]
        token = refs[-1]
        x, y, c, me = _place()
        for k in range(NCHIP):
            @pl.when(me != k)
            def _():
                for a in range(n):
                    _start_pieces(_chip_piece(s[a], k), land[a].at[me],
                                  lambda src, dst, a=a: _chip_copy(a, k, src, dst, send, recv, me, c))
        token[...] = jnp.zeros_like(token)

    lands = [lax.empty(pshape(v.shape), v.dtype) for v in sums]
    ops = [pltpu.with_memory_space_constraint(v, pltpu.HBM) for v in list(sums) + lands]
    sem = pltpu.SemaphoreType.DMA((n * NCHIP,))
    res = pl.pallas_call(
        body, name=f"rs_chips_start{tag}", in_specs=[HBM] * (2 * n),
        out_specs=[SEMS, SEMS] + [HBM] * (2 * n) + [pl.BlockSpec(memory_space=pltpu.VMEM)],
        out_shape=[sem, sem] + [pltpu.HBM(v.shape, v.dtype) for v in ops] + [jax.ShapeDtypeStruct((8, LANES), f32)],
        input_output_aliases={i: 2 + i for i in range(2 * n)},
        compiler_params=pltpu.CompilerParams(has_side_effects=EFFECT),
    )(*ops)
    return res[0], res[1], res[2:2 + n], res[2 + n:2 + 2 * n], res[-1]


def rs_chips_wait(tag, send, recv, sums, lands, after):
    n = len(sums)

    def body(*refs):
        s, land = refs[:n], refs[n:2 * n]
        send_r, recv_r = refs[2 * n], refs[2 * n + 1]
        x, y, c, me = _place()
        for k in range(NCHIP):
            @pl.when(me != k)
            def _():
                for a in range(n):
                    piece = _chip_piece(s[a], k)
                    _chip_copy(a, k, piece, land[a].at[me], send_r, recv_r, me, c).wait_send()
                    _rcopy(piece, land[a].at[k], send_r.at[a * NCHIP + k], recv_r.at[a * NCHIP + k],
                           (k // 2, k % 2, c)).wait_recv()

    ops = list(sums) + list(lands)
    res = pl.pallas_call(
        body, name=f"rs_chips_wait{tag}", in_specs=[HBM] * (2 * n) + [SEMS, SEMS] + [ANY] * len(after),
        out_specs=[HBM] * (2 * n), out_shape=[pltpu.HBM(v.shape, v.dtype) for v in ops],
        input_output_aliases={i: i for i in range(2 * n)},
        compiler_params=pltpu.CompilerParams(has_side_effects=EFFECT),
    )(*ops, send, recv, *after)
    return res[:n], res[n:]


def rs_join(tag, halves):
    n = len(halves)

    def body(*refs):
        h, other = refs[:n], refs[n:2 * n]
        send, recv = refs[2 * n:]
        x, y, c, _ = _place()

        def give(a, s, d):
            return _rcopy(s, d, send.at[a], recv.at[a], (x, y, 1 - c))

        for a in range(n):
            _start_pieces(h[a], other[a], functools.partial(give, a))
        for a in range(n):
            give(a, h[a], other[a]).wait()

    outs = [jax.ShapeDtypeStruct(v.shape, v.dtype) for v in halves]
    return pl.pallas_call(
        body, name=f"rs_join{tag}", in_specs=[ANY] * n, out_specs=[ANY] * n, out_shape=outs,
        scratch_shapes=[pltpu.SemaphoreType.DMA((n,))] * 2,
    )(*halves)


def _row_tile(rows, cols, itemsize=4, target=2 << 20):
    best = 8
    for t in range(8, rows + 1, 8):
        if rows % t == 0 and t * cols * itemsize <= target:
            best = t
    return best


GRAD_WIRE = jnp.bfloat16


def add_half(name, g, t, c):
    cols, half = t.shape[-1], t.shape[-2]
    nblk = 1 if t.ndim == 2 else t.shape[0]
    tr = _row_tile(half, cols)
    per = half // tr

    def body(c_ref, g_r, t_r, o_r):
        o_r[...] = (g_r[...] + t_r[...]).astype(o_r.dtype)

    tile_t = pl.BlockSpec((tr, cols), lambda i, c_ref: (i, 0))
    tile_g = pl.BlockSpec((tr, cols), lambda i, c_ref: ((i // per) * 2 * per + c_ref[0] * per + i % per, 0))
    out = pl.pallas_call(
        body, name=name, out_shape=jax.ShapeDtypeStruct((nblk * half, cols), GRAD_WIRE),
        grid_spec=pltpu.PrefetchScalarGridSpec(num_scalar_prefetch=1, grid=(nblk * per,), in_specs=[tile_g, tile_t],
                                               out_specs=tile_t),
        compiler_params=_cp(("parallel",)),
    )(c.reshape(1).astype(jnp.int32), g.reshape(nblk * 2 * half, cols), t.reshape(nblk * half, cols))
    return out.reshape(t.shape)


def add_chips(name, land, own):
    _, rows, cols = land.shape
    tr = _row_tile(rows, cols, target=1 << 20)

    def body(land_r, own_r, o_r):
        me = 2 * lax.axis_index("x") + lax.axis_index("y")
        for k in range(NCHIP):
            @pl.when(me == k)
            def _():
                acc = None
                for j in range(NCHIP):
                    t = (own_r[...] if j == k else land_r[j]).astype(f32)
                    acc = t if acc is None else acc + t
                o_r[...] = acc

    tile = pl.BlockSpec((tr, cols), lambda i: (i, 0))
    return pl.pallas_call(
        body, name=name, grid=(rows // tr,), in_specs=[pl.BlockSpec((NCHIP, tr, cols), lambda i: (0, i, 0)), tile],
        out_specs=tile, out_shape=jax.ShapeDtypeStruct((rows, cols), f32), compiler_params=_cp(("parallel",)),
    )(land, own)


def reduce_scatter_pair(tag, G):
    names = tuple(G)
    grads = [G[n] if G[n].ndim == 3 or n == "w_in" else G[n].reshape(NCHIP, D // NCHIP, D) for n in names]
    send, recv, grads, theirs, token = rs_pair_start(tag, grads)
    return (tag, names, send, recv, grads, theirs), token[0, 0]


def reduce_scatter_chips(state, after):
    c = lax.axis_index("c")
    tag, names, send, recv, grads, theirs = state
    grads, theirs = rs_pair_wait(tag, send, recv, grads, theirs, after)
    sums = [add_half(f"rs_add_pair{tag}_{n}", g, t, c) for n, g, t in zip(names, grads, theirs)]
    send, recv, sums, lands, token = rs_chips_start(tag, sums)
    return (tag, names, send, recv, sums, lands), token[0, 0]


def reduce_scatter_finish(state, after):
    me = 2 * lax.axis_index("x") + lax.axis_index("y")
    tag, names, send, recv, sums, lands = state
    sums, landed = rs_chips_wait(tag, send, recv, sums, lands, after)
    halves = []
    for n, s, v in zip(names, sums, landed):
        own = lax.dynamic_slice_in_dim(s, me * WIN_SHARD, WIN_SHARD, axis=1) if s.ndim == 2 else \
            lax.dynamic_index_in_dim(s, me, 0, keepdims=False)
        halves.append(add_chips(f"rs_add_chips{tag}_{n}", v, own))
    return dict(zip(names, zip(halves, rs_join(tag, halves))))


NDEV = 8


def _small_copy(r, src, dst, send, recv, x, y, c):
    return _rcopy(src, dst, send.at[r - 1], recv.at[r - 1], (x ^ (r >> 2), y ^ ((r >> 1) & 1), c ^ (r & 1)))


def small_start(pack):
    def body(p, land, send, recv, p_thru, land_thru, token):
        x, y, c, _ = _place()
        me = 4 * x + 2 * y + c
        for r in range(1, NDEV):
            _start_pieces(p, land.at[me], lambda s, d, r=r: _small_copy(r, s, d, send, recv, x, y, c), 128 << 10)
        token[...] = jnp.zeros_like(token)

    ops = [pltpu.with_memory_space_constraint(v, pltpu.HBM) for v in (pack, lax.empty((NDEV,) + pack.shape, f32))]
    sem = pltpu.SemaphoreType.DMA((NDEV - 1,))
    return pl.pallas_call(
        body, name="small_start", in_specs=[HBM, HBM],
        out_specs=[SEMS, SEMS, HBM, HBM, pl.BlockSpec(memory_space=pltpu.VMEM)],
        out_shape=[sem, sem] + [pltpu.HBM(v.shape, v.dtype) for v in ops] + [jax.ShapeDtypeStruct((8, LANES), f32)],
        input_output_aliases={0: 2, 1: 3}, compiler_params=pltpu.CompilerParams(has_side_effects=EFFECT),
    )(*ops)


def small_wait(send, recv, pack, land, after):
    def body(p, land_r, send_r, recv_r, *rest):
        x, y, c, _ = _place()
        me = 4 * x + 2 * y + c
        for r in range(1, NDEV):
            _small_copy(r, p, land_r.at[me], send_r, recv_r, x, y, c).wait_send()
            src = 4 * (x ^ (r >> 2)) + 2 * (y ^ ((r >> 1) & 1)) + (c ^ (r & 1))
            _small_copy(r, p, land_r.at[src], send_r, recv_r, x, y, c).wait_recv()

    return pl.pallas_call(
        body, name="small_wait", in_specs=[HBM, HBM, SEMS, SEMS] + [ANY] * len(after), out_specs=[HBM, HBM],
        out_shape=[pltpu.HBM(pack.shape, f32), pltpu.HBM(land.shape, f32)], input_output_aliases={0: 0, 1: 1},
        compiler_params=pltpu.CompilerParams(has_side_effects=EFFECT),
    )(pack, land, send, recv, *after)


def small_sum(land, pack):
    def body(land_r, p_r, o_r):
        me = 4 * lax.axis_index("x") + 2 * lax.axis_index("y") + lax.axis_index("c")
        for k in range(NDEV):
            @pl.when(me == k)
            def _():
                acc = None
                for d in range(NDEV):
                    t = p_r[...] if d == k else land_r[d]
                    acc = t if acc is None else acc + t
                o_r[...] = acc

    vm = pl.BlockSpec(memory_space=pltpu.VMEM)
    return pl.pallas_call(
        body, name="small_sum", in_specs=[vm, vm], out_specs=vm, out_shape=jax.ShapeDtypeStruct(pack.shape, f32),
        compiler_params=pltpu.CompilerParams(vmem_limit_bytes=40 << 20),
    )(land, pack)


def _adamw_math(w, g, m, v):
    m = ADAM_B1 * m + (1.0 - ADAM_B1) * g
    v = ADAM_B2 * v + (1.0 - ADAM_B2) * (g * g)
    m_hat = m / (1.0 - ADAM_B1 ** ADAM_STEP)
    v_hat = v / (1.0 - ADAM_B2 ** ADAM_STEP)
    return -ADAM_LR * (m_hat / (jnp.sqrt(v_hat) + ADAM_EPS) + ADAM_WD * w), m, v


def adamw_big(name, halves, w, m, v):
    _, R, C = w.shape
    tr = _row_tile(R // 2, C, target=1 << 20)
    nt = R // 2 // tr

    def body(a0, b0, a1, b1, w_r, m_r, v_r, g_o, d_o, m_o, v_o):
        mine = pl.program_id(1) == lax.axis_index("c")
        g = jnp.where(pl.program_id(0) == 0, jnp.where(mine, a0[...], b0[...]), jnp.where(mine, a1[...], b1[...]))
        g_o[...] = g
        d_o[...], m_o[...], v_o[...] = _adamw_math(w_r[...], g, m_r[...], v_r[...])

    stk = pl.BlockSpec((None, tr, C), lambda l, h, i: (l, h * nt + i, 0))
    lay0 = pl.BlockSpec((tr, C), lambda l, h, i: (jnp.where(l == 0, i, nt - 1), 0))
    lay1 = pl.BlockSpec((tr, C), lambda l, h, i: (jnp.where(l == 0, 0, i), 0))
    return pl.pallas_call(
        body, name=name, grid=(DEPTH, 2, nt),
        in_specs=[lay0, lay0, lay1, lay1, stk, stk, stk],
        out_specs=[stk] * 4, out_shape=[jax.ShapeDtypeStruct(w.shape, f32)] * 4,
        compiler_params=_cp(("arbitrary", "arbitrary", "arbitrary")),
    )(*halves[0], *halves[1], w, m, v)


def adamw_small(name, g, w, m, v):
    def body(g_r, w_r, m_r, v_r, d_o, m_o, v_o):
        d_o[...], m_o[...], v_o[...] = _adamw_math(w_r[...], g_r[...], m_r[...], v_r[...])

    return pl.pallas_call(body, name=name, out_shape=[jax.ShapeDtypeStruct(w.shape, f32)] * 3)(g, w, m, v)


WEIGHTS = ("w_in", "conv_w", "gmlp_ln_g", "gmlp_ln_b", "w_s", "b_s", "p_a", "p_b", "p_c", "w_o", "ln1_g", "ln1_b",
           "w_gate", "w_up", "w_down", "ln2_g", "ln2_b")
VECS = ("ln1_g", "ln1_b", "ln2_g", "ln2_b", "gmlp_ln_g", "gmlp_ln_b")
ROWS_VEC, ROWS_BS, ROWS_WS, ROWS_CONV = D // LANES, 8, 8 * BLK, 3 * D // LANES
ROWS_LAYER = len(VECS) * ROWS_VEC + ROWS_BS + ROWS_WS + ROWS_CONV


def _pack_small(per_layer, tail):
    parts = []
    for P in per_layer:
        parts += [P[n].reshape(ROWS_VEC, LANES) for n in VECS]
        parts += [P["b_s"].reshape(ROWS_BS, LANES), P["w_s"].reshape(ROWS_WS, LANES), P["conv_w"].reshape(ROWS_CONV, LANES)]
    return jnp.concatenate(parts + [tail], axis=0)


def _unpack_small(pack):
    out = []
    for l in range(DEPTH):
        r = l * ROWS_LAYER
        P = {}
        for n in VECS:
            P[n] = pack[r:r + ROWS_VEC].reshape(D)
            r += ROWS_VEC
        P["b_s"] = pack[r:r + ROWS_BS].reshape(8, BLK)
        r += ROWS_BS
        P["w_s"] = pack[r:r + ROWS_WS].reshape(8, BLK, BLK)
        r += ROWS_WS
        P["conv_w"] = pack[r:r + ROWS_CONV].reshape(3, D)
        out.append(P)
    return out, pack[DEPTH * ROWS_LAYER:]


def kernel(x, positions, w_in, conv_w, gmlp_ln_g, gmlp_ln_b, w_s, b_s, p_a, p_b, p_c, w_o, ln1_g, ln1_b, w_gate, w_up, w_down, ln2_g, ln2_b, loss_target, m_w_in, m_conv_w, m_gmlp_ln_g, m_gmlp_ln_b, m_w_s, m_b_s, m_p_a, m_p_b, m_p_c, m_w_o, m_ln1_g, m_ln1_b, m_w_gate, m_w_up, m_w_down, m_ln2_g, m_ln2_b, v_w_in, v_conv_w, v_gmlp_ln_g, v_gmlp_ln_b, v_w_s, v_b_s, v_p_a, v_p_b, v_p_c, v_w_o, v_ln1_g, v_ln1_b, v_w_gate, v_w_up, v_w_down, v_ln2_g, v_ln2_b):
    Wt = dict(w_in=w_in, conv_w=conv_w, gmlp_ln_g=gmlp_ln_g, gmlp_ln_b=gmlp_ln_b, w_s=w_s, b_s=b_s, p_a=p_a, p_b=p_b,
              p_c=p_c, w_o=w_o, ln1_g=ln1_g, ln1_b=ln1_b, w_gate=w_gate, w_up=w_up, w_down=w_down, ln2_g=ln2_g, ln2_b=ln2_b)
    Mt = dict(w_in=m_w_in, conv_w=m_conv_w, gmlp_ln_g=m_gmlp_ln_g, gmlp_ln_b=m_gmlp_ln_b, w_s=m_w_s, b_s=m_b_s, p_a=m_p_a,
              p_b=m_p_b, p_c=m_p_c, w_o=m_w_o, ln1_g=m_ln1_g, ln1_b=m_ln1_b, w_gate=m_w_gate, w_up=m_w_up,
              w_down=m_w_down, ln2_g=m_ln2_g, ln2_b=m_ln2_b)
    Vt = dict(w_in=v_w_in, conv_w=v_conv_w, gmlp_ln_g=v_gmlp_ln_g, gmlp_ln_b=v_gmlp_ln_b, w_s=v_w_s, b_s=v_b_s, p_a=v_p_a,
              p_b=v_p_b, p_c=v_p_c, w_o=v_w_o, ln1_g=v_ln1_g, ln1_b=v_ln1_b, w_gate=v_w_gate, w_up=v_w_up,
              w_down=v_w_down, ln2_g=v_ln2_g, ln2_b=v_ln2_b)
    chip = 2 * lax.axis_index("x") + lax.axis_index("y")
    cw = D // NCHIP

    def gathered_weights(names, arrays):
        Wl = dict(zip(names, arrays))
        for n in ("p_a", "p_c", "w_o"):
            Wl[n] = Wl[n].reshape(D, D)
        return Wl

    def small_weights(l, conv_all):
        Wl = {n: Wt[n][l] for n in VECS + ("w_s", "b_s")}
        Wl["conv_w"] = conv_all[:, l].transpose(1, 0, 2).reshape(3, D)
        return Wl

    w_in0, conv_all = gather_halves([Wt["w_in"][0].astype(MX).reshape(2, D // 2, WIN_SHARD), conv_w])
    rest = BIG[1:]
    *late0, coming0 = gather_start("0", [Wt[n][0].astype(MX) for n in rest], [conv_all])
    *late1, coming1 = gather_start("1", [Wt[n][1].astype(MX) for n in BIG], [conv_all, coming0])
    W0 = dict(small_weights(0, conv_all), w_in=w_in0.reshape(D, NIN), after=coming1[0, 0],
              late=lambda y: gathered_weights(rest, gather_wait("0", *late0, [y])))

    def W1(h):
        return dict(small_weights(1, conv_all), **gathered_weights(BIG, gather_wait("1", *late1, [h])))

    layers = [W0, W1]

    rs_state, rs_started, held = {}, {}, {}

    def start_exchange(l, g):
        if "loss" in g:
            held[l] = g
            return None
        if "conv_w" in g:
            held[l] = g
            rs_state[(l, False)], started = reduce_scatter_chips(rs_state[(l, False)], [g["w_s"], g["conv_w"]])
            if l == 0:
                pack = _pack_small([held[j] for j in range(DEPTH)], held[DEPTH]["loss"])
                *held["small"], token = small_start(pack)
                started = started + token[0, 0]
            return started
        if "dx" in g:
            rs_state[(l, True)], rs_started[(l, True)] = reduce_scatter_chips(rs_state[(l, True)], [g["dx"]])
            return rs_started[(l, True)]
        key = (l, "w_in" in g)
        rs_state[key], started = reduce_scatter_pair(f"{l}{'b' if key[1] else 'a'}", g)
        return started

    _, grad_x, _ = local_step(x[0], positions[0], loss_target[0], layers, start_exchange)

    last = jnp.zeros((8, LANES), f32) + rs_started[(0, True)]
    behind = [grad_x, last]
    red = [dict() for _ in range(DEPTH)]
    for key in ((1, False), (1, True), (0, False)):
        red[key[0]].update(reduce_scatter_finish(rs_state[key], behind))
    small, tail = _unpack_small(small_sum(*reversed(small_wait(*held["small"], behind))))
    loss = tail[0, 0]

    G, DW, NM, NV = {}, {}, {}, {}
    zc = jnp.zeros((3, D), f32)
    wp = _pack_small([{**{n: Wt[n][l] for n in VECS + ("b_s", "w_s")}, "conv_w": zc} for l in range(DEPTH)], jnp.zeros((8, LANES), f32))
    mp = _pack_small([{**{n: Mt[n][l] for n in VECS + ("b_s", "w_s")}, "conv_w": zc} for l in range(DEPTH)], jnp.zeros((8, LANES), f32))
    vp = _pack_small([{**{n: Vt[n][l] for n in VECS + ("b_s", "w_s")}, "conv_w": zc} for l in range(DEPTH)], jnp.ones((8, LANES), f32))
    gp = _pack_small(small, jnp.zeros((8, LANES), f32))
    outs = [_unpack_small(a)[0] for a in adamw_small("adamw_small", gp, wp, mp, vp)]
    for n in VECS + ("b_s", "w_s"):
        G[n] = jnp.stack([small[l][n] for l in range(DEPTH)])
        DW[n], NM[n], NV[n] = (jnp.stack([o[l][n] for l in range(DEPTH)]) for o in outs)
    gconv = jnp.stack([lax.dynamic_slice(small[l]["conv_w"], (0, chip * cw), (3, cw)) for l in range(DEPTH)])
    G["conv_w"] = gconv
    flat = lambda a: a.reshape(DEPTH * 3, cw)
    d, m2, v2 = adamw_small("adamw_conv", flat(gconv), flat(conv_w), flat(m_conv_w), flat(v_conv_w))
    DW["conv_w"], NM["conv_w"], NV["conv_w"] = (a.reshape(DEPTH, 3, cw) for a in (d, m2, v2))

    updated = {}
    for n in BIG[1:]:
        tr = (lambda a: jnp.swapaxes(a, 1, 2)) if n in ("w_gate", "w_up") else (lambda a: a)
        updated[n] = adamw_big("adamw_" + n, (red[0][n], red[1][n]), tr(Wt[n]), tr(Mt[n]), tr(Vt[n]))
        G[n], DW[n], NM[n], NV[n] = map(tr, updated[n])
    done = [d, DW["ln2_b"], red[1]["w_in"][1]] + [updated[n][1] for n in BIG[1:]]
    red[0].update(reduce_scatter_finish(rs_state[(0, True)], done))
    G["w_in"], DW["w_in"], NM["w_in"], NV["w_in"] = adamw_big(
        "adamw_w_in", (red[0]["w_in"], red[1]["w_in"]), Wt["w_in"], Mt["w_in"], Vt["w_in"])

    return (loss, grad_x[None], *[G[n] for n in WEIGHTS], *[DW[n] for n in WEIGHTS], *[NM[n] for n in WEIGHTS],
            *[NV[n] for n in WEIGHTS])
```

```python
import functools
import math

import jax
import jax.numpy as jnp
from jax import lax
from jax.experimental import pallas as pl
from jax.experimental.pallas import tpu as pltpu

D = 1024
NIN = 12800
DFF = 2816
NCHIP = 4
FB = DFF // NCHIP
WIN_SHARD = NIN // NCHIP
DEPTH = 2
GROUPS = ((128, 1), (512, 4), (2048, 16))
HD = 64
BLK = 128
AO = 512
ALPHA = (2 * DEPTH) ** 0.25
EPS = 1e-5
ROPE_THETA = 10000.0
LANES = 128
NEG = -1e30

C_GATES, C_BCH, C_QKV, C_UV = 0, 3 * D, 6 * D, 6 * D + 9 * AO

MX = jnp.bfloat16
ACT = jnp.bfloat16

ADAM_LR, ADAM_B1, ADAM_B2, ADAM_EPS, ADAM_WD, ADAM_STEP = 0.001, 0.9, 0.999, 1e-08, 0.01, 10

f32 = jnp.float32
NT = (((1,), (1,)), ((), ()))
TN = (((0,), (0,)), ((), ()))


def _cp(sem, vmem_mb=48):
    return pltpu.CompilerParams(dimension_semantics=sem, vmem_limit_bytes=vmem_mb << 20)


def _dot(a, b, dims=None):
    if dims is None:
        return jnp.dot(a, b, preferred_element_type=f32)
    return lax.dot_general(a, b, dims, preferred_element_type=f32)


def _ln_stats(r):
    mu = jnp.mean(r, axis=-1, keepdims=True)
    xc = r - mu
    var = jnp.mean(xc * xc, axis=-1, keepdims=True)
    rstd = lax.rsqrt(var + EPS)
    return xc * rstd, rstd


def _ln_bwd(dy, xhat, rstd, g):
    dxh = dy * g
    return rstd * (dxh - jnp.mean(dxh, axis=-1, keepdims=True) - xhat * jnp.mean(dxh * xhat, axis=-1, keepdims=True))


def _gelu(x):
    return 0.5 * x * (1.0 + lax.erf(x * (1.0 / math.sqrt(2.0))))


def _gelu_and_grad(x):
    cdf = 0.5 * (1.0 + lax.erf(x * (1.0 / math.sqrt(2.0))))
    return x * cdf, cdf + x * jnp.exp(-0.5 * x * x) * (1.0 / math.sqrt(2.0 * math.pi))


def _sigmoid(x):
    return 0.5 * jnp.tanh(0.5 * x) + 0.5


def _acc_rows(o_ref, first, val):
    @pl.when(first)
    def _():
        o_ref[...] = jnp.zeros_like(o_ref)
    o_ref[...] += jnp.sum(val, axis=0, keepdims=True)


def mm_in(x, w, bias):
    T = x.shape[0]
    tm, tn = min(2048, T), 1280

    def body(x_ref, w_ref, b_ref, o_ref, xb):
        @pl.when(pl.program_id(1) == 0)
        def _():
            xb[...] = x_ref[...].astype(MX)
        o_ref[...] = (_dot(xb[...], w_ref[...]) + b_ref[...]).astype(o_ref.dtype)

    return pl.pallas_call(
        body, name="mm_in", grid=(T // tm, NIN // tn),
        in_specs=[pl.BlockSpec((tm, D), lambda i, j: (i, 0), pipeline_mode=pl.Buffered(1)),
                  pl.BlockSpec((D, tn), lambda i, j: (0, j)), pl.BlockSpec((1, tn), lambda i, j: (0, j))],
        out_specs=pl.BlockSpec((tm, tn), lambda i, j: (i, j)),
        out_shape=jax.ShapeDtypeStruct((T, NIN), ACT),
        scratch_shapes=[pltpu.VMEM((tm, D), MX)],
        compiler_params=_cp(("parallel", "arbitrary")),
    )(x, w, bias)


HALO = 16
TM_AC = 256


def _uv_specs():
    return [pl.BlockSpec((TM_AC, 512), functools.partial(lambda i, j: (i, j), j=C_UV // 512 + j)) for j in range(4)]


def _gmlp_fwd(up, vp, ws_ref, bs_ref, lg, lb, u=None, gv=None):
    u = _gelu(up) if u is None else u
    xhat, rstd = _ln_stats(_gelu(vp) if gv is None else gv)
    vn = xhat * lg + lb
    vnb = vn.astype(MX)
    rows = []
    for c in range(up.shape[0] // BLK):
        r = slice(c * BLK, (c + 1) * BLK)
        rows.append(jnp.concatenate(
            [_dot(ws_ref[g], vnb[r, g * BLK:(g + 1) * BLK]) + bs_ref[g] for g in range(8)], axis=1))
    return u, vn, xhat, rstd, jnp.concatenate(rows, axis=0)


def mix_ac_fwd(proj, conv_w, wst, bsx, lg, lb):
    T = proj.shape[0]
    tm = TM_AC

    def body(bch, halo, u0, u1, v0, v1, cw, ws, bs, lg_ref, lb_ref, ya, yc, zs):
        i = pl.program_id(0)
        pb = bch[...].astype(f32)
        z = pb[:, D:2 * D] * pb[:, 2 * D:]
        hz = halo[:, :D].astype(f32) * halo[:, D:].astype(f32)
        zs[0:HALO, :] = jnp.where(i > 0, hz, 0.0)
        zs[HALO:HALO + tm, :] = z
        cv = cw[0:1, :] * zs[HALO - 2:HALO - 2 + tm, :] + cw[1:2, :] * zs[HALO - 1:HALO - 1 + tm, :] + cw[2:3, :] * z
        ya[...] = (pb[:, :D] * cv).astype(ya.dtype)
        up = jnp.concatenate([u0[...], u1[...]], axis=1).astype(f32)
        vp = jnp.concatenate([v0[...], v1[...]], axis=1).astype(f32)
        u, _, _, _, sp = _gmlp_fwd(up, vp, ws, bs, lg_ref[...], lb_ref[...])
        yc[...] = (u * sp).astype(yc.dtype)

    full = lambda shape: pl.BlockSpec(shape, lambda i: (0,) * len(shape))
    return pl.pallas_call(
        body, name="mix_ac_fwd", grid=(T // tm,),
        in_specs=[pl.BlockSpec((tm, 3 * D), lambda i: (i, 1)),
                  pl.BlockSpec((HALO, 2 * D), lambda i: (jnp.maximum(i * (tm // HALO) - 1, 0), 2)),
                  *_uv_specs(), full((3, D)), full((8, BLK, BLK)), full((8, BLK, BLK)), full((1, D)), full((1, D))],
        out_specs=[pl.BlockSpec((tm, D), lambda i: (i, 0))] * 2,
        out_shape=[jax.ShapeDtypeStruct((T, D), MX)] * 2,
        scratch_shapes=[pltpu.VMEM((HALO + tm, D), f32)],
        compiler_params=_cp(("parallel",)),
    )(proj, proj, proj, proj, proj, proj, conv_w, wst, bsx, lg, lb)


def _swap_halves(x):
    lane = lax.broadcasted_iota(jnp.int32, x.shape, 1)
    return jnp.where((lane % HD) < HD // 2, pltpu.roll(x, x.shape[1] - HD // 2, 1), pltpu.roll(x, HD // 2, 1))


def _tile4(t):
    return jnp.concatenate([t] * (AO // LANES), axis=1)


TM_FOLD = 512


def _fold_out(nat, x, out_ref, d):
    if d == 1:
        out_ref[0] = x.astype(out_ref.dtype)
        return
    rows = x.shape[0] // d
    for j in range(AO // LANES):
        nat[j] = x[:, j * LANES:(j + 1) * LANES]
    for r in range(d):
        out_ref[r] = jnp.concatenate(
            [nat.at[j][pl.ds(r, rows, stride=d), :] for j in range(AO // LANES)], axis=1).astype(out_ref.dtype)


def _unfold_in(nat, in_ref, d):
    if d == 1:
        return in_ref[0].astype(f32)
    rows = in_ref.shape[1]
    for r in range(d):
        v = in_ref[r].astype(f32)
        for j in range(AO // LANES):
            nat.at[j][pl.ds(r, rows, stride=d), :] = v[:, j * LANES:(j + 1) * LANES]
    return jnp.concatenate([nat[j] for j in range(AO // LANES)], axis=1)


def fold_rope(proj, cos_t, sin_t, g, d):
    T = proj.shape[0]
    tm = TM_FOLD
    rows = tm // d

    def body(x_ref, c_ref, s_ref, q_o, k_o, v_o, nat):
        cos, sin = _tile4(c_ref[...]), _tile4(s_ref[...])
        for part, out, scale in ((0, q_o, HD ** -0.5), (1, k_o, 1.0), (2, v_o, None)):
            x = x_ref[:, part * AO:(part + 1) * AO].astype(f32)
            if scale is not None:
                x = (x * cos + _swap_halves(x) * sin) * scale
            _fold_out(nat, x, out, d)

    fold_spec = pl.BlockSpec((d, rows, AO), lambda i: (0, i, 0))
    return pl.pallas_call(
        body, name=f"fold_rope{g}", grid=(T // tm,),
        in_specs=[pl.BlockSpec((tm, 3 * AO), lambda i: (i, C_QKV // (3 * AO) + g)),
                  pl.BlockSpec((tm, LANES), lambda i: (i, 0)), pl.BlockSpec((tm, LANES), lambda i: (i, 0))],
        out_specs=[fold_spec] * 3,
        out_shape=[jax.ShapeDtypeStruct((d, T // d, AO), MX)] * 3,
        scratch_shapes=[pltpu.VMEM((AO // LANES, tm, LANES), f32)],
        compiler_params=_cp(("parallel",)),
    )(proj, cos_t, sin_t)


def _stack_heads(x):
    lane = lax.broadcasted_iota(jnp.int32, x.shape, 1)
    z = jnp.zeros_like(x)
    return jnp.concatenate([jnp.where(lane < HD, x, z), jnp.where(lane >= HD, x, z)], axis=0)


def _unstack_heads(y):
    lane = lax.broadcasted_iota(jnp.int32, (BLK, LANES), 1)
    return jnp.where(lane < HD, y[:BLK], y[BLK:])


def _window_masks():
    row = lax.broadcasted_iota(jnp.int32, (2 * BLK, 2 * BLK), 0) % BLK
    col = lax.broadcasted_iota(jnp.int32, (2 * BLK, 2 * BLK), 1)
    return (col < BLK) & (col >= row), (col >= BLK) & (col - BLK <= row)


def _two_blocks(ref, b):
    r0 = pl.multiple_of(b * BLK, BLK)
    rp = pl.multiple_of(jnp.maximum(b - 1, 0) * BLK, BLK)
    return jnp.concatenate([ref[pl.ds(rp, BLK), :], ref[pl.ds(r0, BLK), :]], axis=0)


def _merge_masks():
    row = lax.broadcasted_iota(jnp.int32, (2 * BLK, BLK), 0) % BLK
    col = lax.broadcasted_iota(jnp.int32, (2 * BLK, BLK), 1)
    return col <= row, col == row


def attn_fwd(qf, kf, vf, g, nb):
    T = qf.shape[0]

    def body(q_ref, k_ref, v_ref, o_ref, l_ref):
        cur_m, own_m = _merge_masks()

        def step(b, carry):
            r0 = pl.multiple_of(b * BLK, BLK)
            rp = pl.multiple_of(jnp.maximum(b - 1, 0) * BLK, BLK)
            qs = _stack_heads(q_ref[pl.ds(r0, BLK), :])
            vc, vp = v_ref[pl.ds(r0, BLK), :], v_ref[pl.ds(rp, BLK), :]
            sp = jnp.where((b % nb) != 0, _dot(qs, k_ref[pl.ds(rp, BLK), :], NT), NEG)
            s = jnp.where(cur_m, _dot(qs, k_ref[pl.ds(r0, BLK), :], NT), sp)
            s_own = jnp.sum(jnp.where(own_m, sp, 0.0), axis=-1, keepdims=True)
            m = jnp.maximum(jnp.max(s, axis=-1, keepdims=True), s_own)
            p, p_own = jnp.exp(s - m), jnp.exp(s_own - m)
            l = jnp.sum(p, axis=-1, keepdims=True) + p_own
            pb = p.astype(MX)
            zero = jnp.zeros_like(pb)
            o = _dot(jnp.where(cur_m, pb, zero), vc) + _dot(jnp.where(cur_m, zero, pb), vp)
            o = (o + p_own * jnp.concatenate([vp, vp], axis=0).astype(f32)) / l
            o_ref[pl.ds(r0, BLK), :] = _unstack_heads(o).astype(o_ref.dtype)
            l_ref[pl.ds(r0, BLK), :] = _unstack_heads(jnp.broadcast_to(m + jnp.log(l), (2 * BLK, LANES)))
            return carry

        lax.fori_loop(0, T // BLK, step, 0, unroll=8)

    spec = pl.BlockSpec((T, LANES), lambda j: (0, j))
    return pl.pallas_call(
        body, name=f"attn_fwd{g}", grid=(AO // LANES,),
        in_specs=[spec] * 3, out_specs=[spec] * 2,
        out_shape=[jax.ShapeDtypeStruct((T, AO), ACT), jax.ShapeDtypeStruct((T, AO), f32)],
        compiler_params=_cp(("parallel",), 56),
    )(qf, kf, vf)


def _group_weights(lses):
    m = jnp.maximum(jnp.maximum(lses[0], lses[1]), lses[2])
    e = [jnp.exp(l - m) for l in lses]
    inv = 1.0 / (e[0] + e[1] + e[2])
    return [x * inv for x in e]


def _fold_specs(T, tm):
    specs = []
    for _, d in GROUPS:
        specs.append(pl.BlockSpec((d, tm // d, AO), lambda i: (0, i, 0)))
    return specs


def combine_fwd(os_, lses):
    T = os_[0].shape[0] * os_[0].shape[1]
    tm = TM_FOLD

    def body(o0, o1, o2, l0, l1, l2, y_ref, nat):
        o = [_unfold_in(nat, r, d) for r, (_, d) in zip((o0, o1, o2), GROUPS)]
        ls = [_unfold_in(nat, r, d) for r, (_, d) in zip((l0, l1, l2), GROUPS)]
        w = _group_weights(ls)
        y_ref[...] = (w[0] * o[0] + w[1] * o[1] + w[2] * o[2]).astype(y_ref.dtype)

    specs = _fold_specs(T, tm)
    return pl.pallas_call(
        body, name="combine_fwd", grid=(T // tm,),
        in_specs=specs + specs, out_specs=pl.BlockSpec((tm, AO), lambda i: (i, 0)),
        out_shape=jax.ShapeDtypeStruct((T, AO), MX),
        scratch_shapes=[pltpu.VMEM((AO // LANES, tm, LANES), f32)],
        compiler_params=_cp(("parallel",)),
    )(*os_, *lses)


TM_MIX = 256


def mix_out_fwd(proj, ya, yb, yc, x0, pa, pb, pc, wo, g1, b1):
    T = x0.shape[0]
    tm = min(TM_MIX, T)

    def body(gt, ya_r, yb_r, yc_r, x0_r, pa_r, pb_r, pc_r, wo_r, g_r, b_r, mabc, m_o, r1_o, x1_o, x1b_o):
        ma = _dot(ya_r[...], pa_r[...])
        ybv = yb_r[...]
        mb = jnp.concatenate([_dot(ybv, pb_r[k]) for k in range(NCHIP)], axis=1)
        mc = _dot(yc_r[...], pc_r[...])
        m = jnp.zeros((tm, D), f32)
        for j, mm in enumerate((ma, mb, mc)):
            mabc[:, j * D:(j + 1) * D] = mm.astype(mabc.dtype)
            m = m + _sigmoid(gt[:, j * D:(j + 1) * D].astype(f32)) * mm
        mb16 = m.astype(MX)
        m_o[...] = mb16
        r1 = ALPHA * x0_r[...] + _dot(mb16, wo_r[...])
        r1_o[...] = r1
        xhat, _ = _ln_stats(r1)
        x1 = xhat * g_r[...] + b_r[...]
        x1_o[...] = x1
        x1b_o[...] = x1.astype(MX)

    full = lambda shape: pl.BlockSpec(shape, lambda i: (0,) * len(shape))
    tile = lambda w: pl.BlockSpec((tm, w), lambda i: (i, 0))
    return pl.pallas_call(
        body, name="mix_out_fwd", grid=(T // tm,),
        in_specs=[tile(3 * D), tile(D), tile(AO), tile(D), tile(D), full((D, D)), full((NCHIP, AO, D // NCHIP)),
                  full((D, D)), full((D, D)), full((1, D)), full((1, D))],
        out_specs=[tile(3 * D), tile(D), tile(D), tile(D), tile(D)],
        out_shape=[jax.ShapeDtypeStruct((T, 3 * D), MX), jax.ShapeDtypeStruct((T, D), MX),
                   jax.ShapeDtypeStruct((T, D), f32), jax.ShapeDtypeStruct((T, D), f32), jax.ShapeDtypeStruct((T, D), MX)],
        compiler_params=_cp(("parallel",), 56),
    )(proj, ya, yb, yc, x0, pa, pb, pc, wo, g1, b1)


TM_FF = 512
TM_FFB = 256
ROW_CHUNK = 64


def ffn_up_fwd(x1, wg, wu):
    T = x1.shape[0]
    tm = min(TM_FFB, T)

    def body(x_r, wg_r, wu_r, g_o, u_o, h_o, gs, us):
        xb = x_r[...].astype(MX)
        for k in range(NCHIP):
            gs[...] = _dot(xb, wg_r[k])
            us[...] = _dot(xb, wu_r[k])
            for r in range(0, tm, ROW_CHUNK):
                rows = pl.ds(r, ROW_CHUNK)
                gate, up = gs[rows, :], us[rows, :]
                g_o[k, rows, :] = gate.astype(g_o.dtype)
                u_o[k, rows, :] = up.astype(u_o.dtype)
                h_o[k, rows, :] = (gate * _sigmoid(gate) * up).astype(h_o.dtype)

    wspec = pl.BlockSpec((NCHIP, D, FB), lambda i: (0, 0, 0))
    ospec = pl.BlockSpec((NCHIP, tm, FB), lambda i: (0, i, 0))
    return pl.pallas_call(
        body, name="ffn_up_fwd", grid=(T // tm,),
        in_specs=[pl.BlockSpec((tm, D), lambda i: (i, 0)), wspec, wspec],
        out_specs=[ospec] * 3,
        out_shape=[jax.ShapeDtypeStruct((NCHIP, T, FB), ACT)] * 2 + [jax.ShapeDtypeStruct((NCHIP, T, FB), MX)],
        scratch_shapes=[pltpu.VMEM((tm, FB), f32)] * 2,
        compiler_params=_cp(("parallel",)),
    )(x1, wg, wu)


def ffn_down_fwd(hh, wd, x1, g2, b2):
    T = x1.shape[0]
    tm = min(TM_FF, T)

    def body(h_r, w_r, x_r, g_r, b_r, r2_o, x2_o):
        r2 = ALPHA * x_r[...]
        for k in range(NCHIP):
            r2 = r2 + _dot(h_r[k], w_r[k])
        r2_o[...] = r2
        xhat, _ = _ln_stats(r2)
        x2_o[...] = xhat * g_r[...] + b_r[...]

    tile = pl.BlockSpec((tm, D), lambda i: (i, 0))
    vec = pl.BlockSpec((1, D), lambda i: (0, 0))
    return pl.pallas_call(
        body, name="ffn_down_fwd", grid=(T // tm,),
        in_specs=[pl.BlockSpec((NCHIP, tm, FB), lambda i: (0, i, 0)), pl.BlockSpec((NCHIP, FB, D), lambda i: (0, 0, 0)),
                  tile, vec, vec],
        out_specs=[tile, tile], out_shape=[jax.ShapeDtypeStruct((T, D), f32)] * 2,
        compiler_params=_cp(("parallel",)),
    )(hh, wd, x1, g2, b2)


def loss_grad(y, tgt):
    T = y.shape[0]
    tm = min(512, T)

    def body(y_r, t_r, l_o, dy_o):
        e = y_r[...] - t_r[...]
        dy_o[...] = e * (1.0 / D)

        @pl.when(pl.program_id(0) == 0)
        def _():
            l_o[...] = jnp.zeros_like(l_o)
        l_o[...] += (0.5 / D) * jnp.sum(e * e)

    tile = pl.BlockSpec((tm, D), lambda i: (i, 0))
    return pl.pallas_call(
        body, name="loss_grad", grid=(T // tm,),
        in_specs=[tile, tile], out_specs=[pl.BlockSpec((8, LANES), lambda i: (0, 0)), tile],
        out_shape=[jax.ShapeDtypeStruct((8, LANES), f32), jax.ShapeDtypeStruct((T, D), f32)],
        compiler_params=_cp(("arbitrary",)),
    )(y, tgt)


def ffn_down_bwd(dx2, r2, g2, wd, gate, up):
    T = dx2.shape[0]
    tm = min(TM_FFB, T)

    def body(dx_r, r_r, g_r, w_r, ga_r, up_r, dr_o, drb_o, dg_o, du_o, dlg_o, dlb_o, hs):
        i = pl.program_id(0)
        xhat, rstd = _ln_stats(r_r[...])
        dx = dx_r[...]
        _acc_rows(dlg_o, i == 0, dx * xhat)
        _acc_rows(dlb_o, i == 0, dx)
        dr = _ln_bwd(dx, xhat, rstd, g_r[...])
        dr_o[...] = dr
        drb = dr.astype(MX)
        drb_o[...] = drb
        for k in range(NCHIP):
            hs[...] = _dot(drb, w_r[k], NT)
            for r in range(0, tm, ROW_CHUNK):
                rows = pl.ds(r, ROW_CHUNK)
                dhh, gate_v, up_v = hs[rows, :], ga_r[k, rows, :].astype(f32), up_r[k, rows, :].astype(f32)
                sg = _sigmoid(gate_v)
                dg_o[k, rows, :] = (dhh * up_v * sg * (1.0 + gate_v * (1.0 - sg))).astype(dg_o.dtype)
                du_o[k, rows, :] = (dhh * gate_v * sg).astype(du_o.dtype)

    tile = pl.BlockSpec((tm, D), lambda i: (i, 0))
    vec = pl.BlockSpec((1, D), lambda i: (0, 0))
    blk = pl.BlockSpec((NCHIP, tm, FB), lambda i: (0, i, 0))
    return pl.pallas_call(
        body, name="ffn_down_bwd", grid=(T // tm,),
        in_specs=[tile, tile, vec, pl.BlockSpec((NCHIP, FB, D), lambda i: (0, 0, 0)), blk, blk],
        out_specs=[tile, tile, blk, blk, vec, vec],
        out_shape=[jax.ShapeDtypeStruct((T, D), f32), jax.ShapeDtypeStruct((T, D), MX)]
        + [jax.ShapeDtypeStruct((NCHIP, T, FB), MX)] * 2 + [jax.ShapeDtypeStruct((1, D), f32)] * 2,
        scratch_shapes=[pltpu.VMEM((tm, FB), f32)],
        compiler_params=_cp(("arbitrary",)),
    )(dx2, r2, g2, wd, gate, up)


def ffn_up_bwd(dr2, dgate, dup, wg, wu, r1, g1):
    T = dr2.shape[0]
    tm = min(TM_FFB, T)

    def body(dr2_r, dg_r, du_r, wg_r, wu_r, r1_r, g_r, dr1_o, dr1b_o, dlg_o, dlb_o):
        i = pl.program_id(0)
        dx = ALPHA * dr2_r[...]
        for k in range(NCHIP):
            dx = dx + _dot(dg_r[k], wg_r[k], NT) + _dot(du_r[k], wu_r[k], NT)
        xhat, rstd = _ln_stats(r1_r[...])
        _acc_rows(dlg_o, i == 0, dx * xhat)
        _acc_rows(dlb_o, i == 0, dx)
        dr1 = _ln_bwd(dx, xhat, rstd, g_r[...])
        dr1_o[...] = dr1
        dr1b_o[...] = dr1.astype(MX)

    tile = pl.BlockSpec((tm, D), lambda i: (i, 0))
    vec = pl.BlockSpec((1, D), lambda i: (0, 0))
    blk = pl.BlockSpec((NCHIP, tm, FB), lambda i: (0, i, 0))
    wspec = pl.BlockSpec((NCHIP, D, FB), lambda i: (0, 0, 0))
    return pl.pallas_call(
        body, name="ffn_up_bwd", grid=(T // tm,),
        in_specs=[tile, blk, blk, wspec, wspec, tile, vec],
        out_specs=[tile, tile, vec, vec],
        out_shape=[jax.ShapeDtypeStruct((T, D), f32), jax.ShapeDtypeStruct((T, D), MX)]
        + [jax.ShapeDtypeStruct((1, D), f32)] * 2,
        compiler_params=_cp(("arbitrary",)),
    )(dr2, dgate, dup, wg, wu, r1, g1)


def mix_out_bwd(dr1, proj, mabc, wo, pa, pb, pc):
    T = dr1.shape[0]
    tm = min(TM_MIX, T)

    def body(dr_r, gt, mabc_r, wo_r, pa_r, pb_r, pc_r, dmabc_o, dgt_o, dya_o, dyb_o, dyc_o):
        dm = _dot(dr_r[...].astype(MX), wo_r[...], NT)
        dmx = []
        for j in range(3):
            s = _sigmoid(gt[:, j * D:(j + 1) * D].astype(f32))
            v = (dm * s).astype(MX)
            dmx.append(v)
            dmabc_o[:, j * D:(j + 1) * D] = v
            dgt_o[:, j * D:(j + 1) * D] = (dm * mabc_r[:, j * D:(j + 1) * D].astype(f32) * s * (1.0 - s)).astype(dgt_o.dtype)
        dya_o[...] = _dot(dmx[0], pa_r[...], NT).astype(dya_o.dtype)
        dyb = jnp.zeros((tm, AO), f32)
        for k in range(NCHIP):
            dyb = dyb + _dot(dmx[1][:, k * (D // NCHIP):(k + 1) * (D // NCHIP)], pb_r[k], NT)
        dyb_o[...] = dyb.astype(dyb_o.dtype)
        dyc_o[...] = _dot(dmx[2], pc_r[...], NT).astype(dyc_o.dtype)

    full = lambda shape: pl.BlockSpec(shape, lambda i: (0,) * len(shape))
    tile = lambda w: pl.BlockSpec((tm, w), lambda i: (i, 0))
    return pl.pallas_call(
        body, name="mix_out_bwd", grid=(T // tm,),
        in_specs=[tile(D), tile(3 * D), tile(3 * D), full((D, D)), full((D, D)), full((NCHIP, AO, D // NCHIP)), full((D, D))],
        out_specs=[tile(3 * D), tile(3 * D), tile(D), tile(AO), tile(D)],
        out_shape=[jax.ShapeDtypeStruct((T, 3 * D), MX), jax.ShapeDtypeStruct((T, 3 * D), MX),
                   jax.ShapeDtypeStruct((T, D), ACT), jax.ShapeDtypeStruct((T, AO), ACT), jax.ShapeDtypeStruct((T, D), ACT)],
        compiler_params=_cp(("parallel",), 56),
    )(dr1, proj, mabc, wo, pa, pb, pc)


def transpose_cast(x):
    T = x.shape[0]
    tm = min(512, T)

    def body(x_r, o_r):
        o_r[...] = x_r[...].T.astype(o_r.dtype)

    return pl.pallas_call(
        body, name="transpose_cast", grid=(T // tm,),
        in_specs=[pl.BlockSpec((tm, D), lambda i: (i, 0))], out_specs=pl.BlockSpec((D, tm), lambda i: (0, i)),
        out_shape=jax.ShapeDtypeStruct((D, T), MX), compiler_params=_cp(("parallel",)),
    )(x)


def tn_matmul(name, a, b, a_spec, b_spec, out_shape, out_spec, grid):
    nt = len(grid) - 1

    def body(a_r, b_r, o_r):
        @pl.when(pl.program_id(nt) == 0)
        def _():
            o_r[...] = jnp.zeros_like(o_r)
        av = a_r[...].reshape(a_r.shape[-2:]).astype(MX)
        bv = b_r[...].reshape(b_r.shape[-2:]).astype(MX)
        o_r[...] += _dot(av, bv, TN).reshape(o_r.shape)

    return pl.pallas_call(
        body, name=name, grid=grid, in_specs=[a_spec, b_spec], out_specs=out_spec,
        out_shape=jax.ShapeDtypeStruct(out_shape, f32),
        compiler_params=_cp(("parallel",) * nt + ("arbitrary",), 56),
    )(a, b)


def attn_pre_bwd(dyb, os_, lses, ones):
    T = dyb.shape[0]
    tm = TM_FOLD

    def body(dy_r, o0, o1, o2, l0, l1, l2, ones_r, d0, d1, d2, f0, f1, f2, nat):
        o = [_unfold_in(nat, r, d) for r, (_, d) in zip((o0, o1, o2), GROUPS)]
        ls = [_unfold_in(nat, r, d) for r, (_, d) in zip((l0, l1, l2), GROUPS)]
        w = _group_weights(ls)
        dy = dy_r[...].astype(f32)
        t = dy * (w[0] * o[0] + w[1] * o[1] + w[2] * o[2])
        hi = t.astype(MX)
        lo = (t - hi.astype(f32)).astype(MX)
        c = _dot(hi, ones_r[...]) + _dot(lo, ones_r[...])
        for wg, do_o, df_o, (_, d) in zip(w, (d0, d1, d2), (f0, f1, f2), GROUPS):
            _fold_out(nat, wg * dy, do_o, d)
            _fold_out(nat, -wg * c, df_o, d)

    specs = _fold_specs(T, tm)
    return pl.pallas_call(
        body, name="attn_pre_bwd", grid=(T // tm,),
        in_specs=[pl.BlockSpec((tm, AO), lambda i: (i, 0))] + specs + specs + [pl.BlockSpec((AO, AO), lambda i: (0, 0))],
        out_specs=specs + specs,
        out_shape=[jax.ShapeDtypeStruct((d, T // d, AO), MX) for _, d in GROUPS]
        + [jax.ShapeDtypeStruct((d, T // d, AO), f32) for _, d in GROUPS],
        scratch_shapes=[pltpu.VMEM((AO // LANES, tm, LANES), f32)],
        compiler_params=_cp(("parallel",)),
    )(dyb, *os_, *lses, ones)


def _head_ones():
    i = jnp.arange(AO) // HD
    return (i[:, None] == i[None, :]).astype(MX)


BWD_BLOCKS = 8


def attn_bwd(qf, kf, vf, dof, lse, df, g, nb):
    T = qf.shape[0]

    def body(q_ref, k_ref, v_ref, do_ref, l_ref, d_ref, dq_ref, dk_ref, dv_ref):
        prev_m, cur_m = _window_masks()

        def head_col(ref, r0):
            v = ref[pl.ds(r0, BLK), :]
            return jnp.concatenate([v[:, 0:1], v[:, HD:HD + 1]], axis=0)

        def step(b, carry):
            dk_c, dv_c = carry
            r0 = pl.multiple_of(b * BLK, BLK)
            rp = pl.multiple_of(jnp.maximum(b - 1, 0) * BLK, BLK)
            qs, dos = _stack_heads(q_ref[pl.ds(r0, BLK), :]), _stack_heads(do_ref[pl.ds(r0, BLK), :])
            k2, v2 = _two_blocks(k_ref, b), _two_blocks(v_ref, b)
            valid = cur_m | (prev_m & ((b % nb) != 0))
            p = jnp.where(valid, jnp.exp(_dot(qs, k2, NT) - head_col(l_ref, r0)), 0.0)
            ds = (p * (_dot(dos, v2, NT) + head_col(d_ref, r0))).astype(MX)
            dq_ref[pl.ds(r0, BLK), :] = _unstack_heads(_dot(ds, k2)).astype(dq_ref.dtype)
            dk2 = _dot(ds, qs, TN)
            dv2 = _dot(p.astype(MX), dos, TN)
            dk_ref[pl.ds(rp, BLK), :] = (dk_c + dk2[:BLK]).astype(dk_ref.dtype)
            dv_ref[pl.ds(rp, BLK), :] = (dv_c + dv2[:BLK]).astype(dv_ref.dtype)
            return dk2[BLK:], dv2[BLK:]

        zero = jnp.zeros((BLK, LANES), f32)

        def steps(i, carry):
            for j in range(BWD_BLOCKS):
                carry = step(BWD_BLOCKS * i + j, carry)
            return carry

        dk_c, dv_c = lax.fori_loop(0, T // BLK // BWD_BLOCKS, steps, (zero, zero))
        dk_ref[pl.ds(T - BLK, BLK), :] = dk_c.astype(dk_ref.dtype)
        dv_ref[pl.ds(T - BLK, BLK), :] = dv_c.astype(dv_ref.dtype)

    spec = pl.BlockSpec((T, LANES), lambda j: (0, j))
    return pl.pallas_call(
        body, name=f"attn_bwd{g}", grid=(AO // LANES,),
        in_specs=[spec] * 6, out_specs=[spec] * 3,
        out_shape=[jax.ShapeDtypeStruct((T, AO), MX)] * 3,
        compiler_params=_cp(("parallel",), 60),
    )(qf, kf, vf, dof, lse, df)


def unfold_rope_bwd(dqf, dkf, dvf, cos_t, sin_t, g, d):
    T = dqf.shape[0] * dqf.shape[1]
    tm = TM_FOLD

    def body(q_r, k_r, v_r, c_ref, s_ref, o_ref, nat):
        cos, sin = _tile4(c_ref[...]), _tile4(s_ref[...])
        for part, ref, scale in ((0, q_r, HD ** -0.5), (1, k_r, 1.0), (2, v_r, None)):
            x = _unfold_in(nat, ref, d)
            if scale is not None:
                x = (x * cos - _swap_halves(x) * sin) * scale
            o_ref[:, part * AO:(part + 1) * AO] = x.astype(o_ref.dtype)

    fold_spec = pl.BlockSpec((d, tm // d, AO), lambda i: (0, i, 0))
    tab = pl.BlockSpec((tm, LANES), lambda i: (i, 0))
    return pl.pallas_call(
        body, name=f"unfold_rope_bwd{g}", grid=(T // tm,),
        in_specs=[fold_spec] * 3 + [tab, tab],
        out_specs=pl.BlockSpec((tm, 3 * AO), lambda i: (i, 0)),
        out_shape=jax.ShapeDtypeStruct((T, 3 * AO), MX),
        scratch_shapes=[pltpu.VMEM((AO // LANES, tm, LANES), f32)],
        compiler_params=_cp(("parallel",)),
    )(dqf, dkf, dvf, cos_t, sin_t)


CONV_CHUNK = 32


def conv_bwd(dya, proj, conv_w):
    T = dya.shape[0]
    tm = TM_AC
    last = T // tm - 1

    def body(dy_r, bch, hprev, dy_next, b_next, cw, d_o, dw_o, zs, ds):
        i = pl.program_id(0)
        ch = CONV_CHUNK
        hz = hprev[:, :D].astype(f32) * hprev[:, D:].astype(f32)
        zs[0:HALO, :] = jnp.where(i > 0, hz, 0.0)
        ds[tm:tm + HALO, :] = jnp.where(i < last, dy_next[...].astype(f32) * b_next[...].astype(f32), 0.0)
        for r in range(0, tm, ch):
            zs[HALO + r:HALO + r + ch, :] = bch[r:r + ch, D:2 * D].astype(f32) * bch[r:r + ch, 2 * D:].astype(f32)
            ds[r:r + ch, :] = dy_r[r:r + ch, :].astype(f32) * bch[r:r + ch, :D].astype(f32)

        @pl.when(i == 0)
        def _():
            dw_o[...] = jnp.zeros_like(dw_o)

        sums = [jnp.zeros((1, D), f32) for _ in range(3)]
        for r in range(0, tm, ch):
            z2, z1, z = (zs[HALO + r - s:HALO + r - s + ch, :] for s in (2, 1, 0))
            dcv, d1, d2 = (ds[r + s:r + s + ch, :] for s in (0, 1, 2))
            cv = cw[0:1, :] * z2 + cw[1:2, :] * z1 + cw[2:3, :] * z
            dz = cw[2:3, :] * dcv + cw[1:2, :] * d1 + cw[0:1, :] * d2
            d_o[r:r + ch, :D] = (dy_r[r:r + ch, :].astype(f32) * cv).astype(d_o.dtype)
            d_o[r:r + ch, D:2 * D] = (dz * bch[r:r + ch, 2 * D:].astype(f32)).astype(d_o.dtype)
            d_o[r:r + ch, 2 * D:] = (dz * bch[r:r + ch, D:2 * D].astype(f32)).astype(d_o.dtype)
            for k, zz in enumerate((z2, z1, z)):
                sums[k] = sums[k] + jnp.sum(dcv * zz, axis=0, keepdims=True)
        for k in range(3):
            dw_o[k:k + 1, :] += sums[k]

    nh = tm // HALO
    return pl.pallas_call(
        body, name="conv_bwd", grid=(T // tm,),
        in_specs=[pl.BlockSpec((tm, D), lambda i: (i, 0)), pl.BlockSpec((tm, 3 * D), lambda i: (i, 1)),
                  pl.BlockSpec((HALO, 2 * D), lambda i: (jnp.maximum(i * nh - 1, 0), 2)),
                  pl.BlockSpec((HALO, D), lambda i: (jnp.minimum((i + 1) * nh, T // HALO - 1), 0)),
                  pl.BlockSpec((HALO, D), lambda i: (jnp.minimum((i + 1) * nh, T // HALO - 1), 3)),
                  pl.BlockSpec((3, D), lambda i: (0, 0))],
        out_specs=[pl.BlockSpec((tm, 3 * D), lambda i: (i, 0)), pl.BlockSpec((3, D), lambda i: (0, 0))],
        out_shape=[jax.ShapeDtypeStruct((T, 3 * D), MX), jax.ShapeDtypeStruct((3, D), f32)],
        scratch_shapes=[pltpu.VMEM((HALO + tm, D), f32), pltpu.VMEM((tm + HALO, D), f32)],
        compiler_params=_cp(("arbitrary",)),
    )(dya, proj, proj, dya, proj, conv_w)


def gmlp_bwd(dyc, proj, wst, bsx, lg, lb):
    T = dyc.shape[0]
    tm = TM_AC
    last = T // tm - 1

    def body(dy_r, u0, u1, v0, v1, ws, bs, lg_r, lb_r, d_o, dws_o, dbs_o, dlg_o, dlb_o, bacc):
        i = pl.program_id(0)
        up = jnp.concatenate([u0[...], u1[...]], axis=1).astype(f32)
        vp = jnp.concatenate([v0[...], v1[...]], axis=1).astype(f32)
        u, du = _gelu_and_grad(up)
        gv, dgv = _gelu_and_grad(vp)
        u, vn, xhat, rstd, sp = _gmlp_fwd(up, vp, ws, bs, lg_r[...], lb_r[...], u, gv)
        dy = dy_r[...].astype(f32)
        d_o[:, :D] = (dy * sp * du).astype(d_o.dtype)
        dsp = dy * u
        dspb, vnb = dsp.astype(MX), vn.astype(MX)

        @pl.when(i == 0)
        def _():
            dws_o[...] = jnp.zeros_like(dws_o)
            bacc[...] = jnp.zeros_like(bacc)

        rows = []
        for c in range(tm // BLK):
            r = slice(c * BLK, (c + 1) * BLK)
            cols = []
            for g in range(8):
                cs = slice(g * BLK, (g + 1) * BLK)
                dws_o[g] += _dot(dspb[r, cs], vnb[r, cs], NT)
                bacc[g] += dsp[r, cs]
                cols.append(_dot(ws[g], dspb[r, cs], TN))
            rows.append(jnp.concatenate(cols, axis=1))
        dvn = jnp.concatenate(rows, axis=0)
        _acc_rows(dlg_o, i == 0, dvn * xhat)
        _acc_rows(dlb_o, i == 0, dvn)
        d_o[:, D:] = (_ln_bwd(dvn, xhat, rstd, lg_r[...]) * dgv).astype(d_o.dtype)

        @pl.when(i == last)
        def _():
            row = lax.broadcasted_iota(jnp.int32, (BLK, BLK), 0)
            col = lax.broadcasted_iota(jnp.int32, (BLK, BLK), 1)
            ones = jnp.ones((8, BLK), MX)
            for g in range(8):
                dws_o[g] = jnp.where(col <= row, dws_o[g], 0.0)
                a = bacc[g]
                hi = a.astype(MX)
                lo = (a - hi.astype(f32)).astype(MX)
                dbs_o[g:g + 1, :] = (_dot(ones, hi, NT) + _dot(ones, lo, NT))[0:1, :]

    full = lambda shape: pl.BlockSpec(shape, lambda i: (0,) * len(shape))
    return pl.pallas_call(
        body, name="gmlp_bwd", grid=(T // tm,),
        in_specs=[pl.BlockSpec((tm, D), lambda i: (i, 0)), *_uv_specs(), full((8, BLK, BLK)), full((8, BLK, BLK)),
                  full((1, D)), full((1, D))],
        out_specs=[pl.BlockSpec((tm, 2 * D), lambda i: (i, 0)), full((8, BLK, BLK)), full((8, BLK)), full((1, D)), full((1, D))],
        out_shape=[jax.ShapeDtypeStruct((T, 2 * D), MX), jax.ShapeDtypeStruct((8, BLK, BLK), f32),
                   jax.ShapeDtypeStruct((8, BLK), f32), jax.ShapeDtypeStruct((1, D), f32), jax.ShapeDtypeStruct((1, D), f32)],
        scratch_shapes=[pltpu.VMEM((8, BLK, BLK), f32)],
        compiler_params=_cp(("arbitrary",)),
    )(dyc, proj, proj, proj, proj, wst, bsx, lg, lb)


PART_TILES = (6, 6, 3, 3, 3, 4)
PART_START = (0, 6, 12, 15, 18, 21)
TJ = 512


def _part_specs(tm, rows_axis):
    specs = []
    for n, s in zip(PART_TILES, PART_START):
        def imap(*idx, n=n, s=s):
            i, j = idx[rows_axis], idx[1 - rows_axis]
            inside = (j >= s) & (j < s + n)
            return (jnp.where(inside, i, 0), jnp.clip(j - s, 0, n - 1))
        specs.append(pl.BlockSpec((tm, TJ), imap))
    return specs


def _with_part(j, refs, fn):
    for r, n, s in zip(refs, PART_TILES, PART_START):
        @pl.when((j >= s) & (j < s + n))
        def _():
            fn(r[...])


def dx_in(dr1, parts, w, bias):
    T = dr1.shape[0]
    tm = min(2048, T)

    def body(dr_r, p0, p1, p2, p3, p4, p5, w_r, b_r, o_r):
        j = pl.program_id(1)

        @pl.when(j == 0)
        def _():
            o_r[...] = ALPHA * dr_r[...] + b_r[...]

        def acc(tile):
            o_r[...] += _dot(tile, w_r[...], NT)
        _with_part(j, (p0, p1, p2, p3, p4, p5), acc)

    once = dict(pipeline_mode=pl.Buffered(1))
    return pl.pallas_call(
        body, name="dx_in", grid=(T // tm, NIN // TJ),
        in_specs=[pl.BlockSpec((tm, D), lambda i, j: (i, 0), **once)] + _part_specs(tm, 0)
        + [pl.BlockSpec((D, TJ), lambda i, j: (0, j)), pl.BlockSpec((1, D), lambda i, j: (0, 0))],
        out_specs=pl.BlockSpec((tm, D), lambda i, j: (i, 0), **once),
        out_shape=jax.ShapeDtypeStruct((T, D), f32),
        compiler_params=_cp(("parallel", "arbitrary"), 56),
    )(dr1, *parts, w, bias)


def dw_in(x0t, parts):
    T = x0t.shape[1]
    tk = min(2048, T)

    def body(x_r, p0, p1, p2, p3, p4, p5, o_r):
        j, t = pl.program_id(0), pl.program_id(1)

        @pl.when(t == 0)
        def _():
            o_r[...] = jnp.zeros_like(o_r)

        def acc(tile):
            o_r[...] += _dot(x_r[:, pl.ds(pl.multiple_of(t * tk, tk), tk)], tile)
        _with_part(j, (p0, p1, p2, p3, p4, p5), acc)

    return pl.pallas_call(
        body, name="dw_in", grid=(NIN // TJ, T // tk),
        in_specs=[pl.BlockSpec((D, T), lambda j, t: (0, 0), pipeline_mode=pl.Buffered(1))] + _part_specs(tk, 1),
        out_specs=pl.BlockSpec((D, TJ), lambda j, t: (0, j)),
        out_shape=jax.ShapeDtypeStruct((D, NIN), f32),
        compiler_params=_cp(("parallel", "arbitrary"), 56),
    )(x0t, *parts)


def rope_tables(positions):
    half = HD // 2
    inv_freq = ROPE_THETA ** (-jnp.arange(half, dtype=f32) / half)
    ang = positions.astype(f32)[:, None] * inv_freq
    cos, sin = jnp.cos(ang), jnp.sin(ang)
    return jnp.tile(cos, (1, LANES // half)), jnp.tile(jnp.concatenate([-sin, sin], axis=1), (1, LANES // HD))


def _flat(a):
    return a.reshape(a.shape[0] * a.shape[1], a.shape[2])


def layer_fwd(x0, W, cos_t, sin_t):
    T = x0.shape[0]
    proj = mm_in(x0, W["w_in"], W["in_bias"])
    ya, yc = mix_ac_fwd(proj, W["conv_w"], W["wst"], W["bsx"], W["gmlp_ln_g"], W["gmlp_ln_b"])
    folded, os_, lses = [], [], []
    for g, (_, d) in enumerate(GROUPS):
        qf, kf, vf = fold_rope(proj, cos_t, sin_t, g, d)
        o, lse = attn_fwd(_flat(qf), _flat(kf), _flat(vf), g, T // d // BLK)
        folded.append((qf, kf, vf))
        os_.append(o.reshape(d, T // d, AO))
        lses.append(lse.reshape(d, T // d, AO))
    yb = combine_fwd(os_, lses)
    if "late" in W:
        W = {**W, **W["late"](yb)}
    mabc, m, r1, x1, x1b = mix_out_fwd(proj, ya, yb, yc, x0, W["p_a"], W["p_b"], W["p_c"], W["w_o"], W["ln1_g"], W["ln1_b"])
    gate, up, hh = ffn_up_fwd(x1b, W["w_gate"], W["w_up"])
    r2, x2 = ffn_down_fwd(hh, W["w_down"], x1, W["ln2_g"], W["ln2_b"])
    saved = dict(x0=x0, proj=proj, ya=ya, yb=yb, yc=yc, folded=folded, os=os_, lses=lses, mabc=mabc, m=m, r1=r1,
                 x1b=x1b, gate=gate, up=up, hh=hh, r2=r2)
    return x2, saved, W


def layer_bwd(dx2, S, W, cos_t, sin_t, on_grads=None):
    T = dx2.shape[0]
    tk = min(2048, T)
    G = {}
    dr2, dr2b, dgate, dup, G["ln2_g"], G["ln2_b"] = ffn_down_bwd(dx2, S["r2"], W["ln2_g"], W["w_down"], S["gate"], S["up"])
    blk_a = pl.BlockSpec((1, tk, FB), lambda k, t: (k, t, 0))
    row_b = pl.BlockSpec((tk, D), lambda k, t: (t, 0))
    G["w_down"] = tn_matmul("dw_down", S["hh"], dr2b, blk_a, row_b, (NCHIP, FB, D),
                            pl.BlockSpec((1, FB, D), lambda k, t: (k, 0, 0)), (NCHIP, T // tk))
    for nm, dv in (("w_gate", dgate), ("w_up", dup)):
        G[nm] = tn_matmul("d" + nm, dv, S["x1b"], blk_a, row_b, (NCHIP, FB, D),
                          pl.BlockSpec((1, FB, D), lambda k, t: (k, 0, 0)), (NCHIP, T // tk))
    dr1, dr1b, G["ln1_g"], G["ln1_b"] = ffn_up_bwd(dr2, dgate, dup, W["w_gate"], W["w_up"], S["r1"], W["ln1_g"])
    dmabc, dgates, dya, dyb, dyc = mix_out_bwd(dr1b, S["proj"], S["mabc"], W["w_o"], W["p_a"], W["p_b"], W["p_c"])
    one = (1, T // tk)
    full_o = pl.BlockSpec((D, D), lambda k, t: (0, 0))
    G["w_o"] = tn_matmul("dw_o", S["m"], dr1b, row_b, row_b, (D, D), full_o, one)
    G["p_a"] = tn_matmul("dp_a", S["ya"], dmabc, row_b, pl.BlockSpec((tk, D), lambda k, t: (t, 0)), (D, D), full_o, one)
    G["p_c"] = tn_matmul("dp_c", S["yc"], dmabc, row_b, pl.BlockSpec((tk, D), lambda k, t: (t, 2)), (D, D), full_o, one)
    G["p_b"] = tn_matmul("dp_b", S["yb"], dmabc, pl.BlockSpec((tk, AO), lambda k, t: (t, 0)),
                         pl.BlockSpec((tk, D // NCHIP), lambda k, t: (t, NCHIP + k)), (NCHIP, AO, D // NCHIP),
                         pl.BlockSpec((1, AO, D // NCHIP), lambda k, t: (k, 0, 0)), (NCHIP, T // tk))
    conv_w = W["conv_w"]
    if on_grads is not None:
        conv_w = conv_w + on_grads({n: G[n] for n in BIG if n != "w_in"})
    dbch, G["conv_w"] = conv_bwd(dya, S["proj"], conv_w)
    duv, G["w_s"], G["b_s"], G["gmlp_ln_g"], G["gmlp_ln_b"] = gmlp_bwd(
        dyc, S["proj"], W["wst"], W["bsx"], W["gmlp_ln_g"], W["gmlp_ln_b"])
    ones = _head_ones()
    if on_grads is not None:
        small = {n: G[n] for n in VECS + ("b_s", "w_s", "conv_w")}
        ones = ones + on_grads(small).astype(MX)
    pre = attn_pre_bwd(dyb, S["os"], S["lses"], ones)
    dqkv = []
    for g, (_, d) in enumerate(GROUPS):
        qf, kf, vf = S["folded"][g]
        dqf, dkf, dvf = attn_bwd(_flat(qf), _flat(kf), _flat(vf), _flat(pre[g]), _flat(S["lses"][g]), _flat(pre[3 + g]),
                                 g, T // d // BLK)
        shp = (d, T // d, AO)
        dqkv.append(unfold_rope_bwd(dqf.reshape(shp), dkf.reshape(shp), dvf.reshape(shp), cos_t, sin_t, g, d))
    parts = (dgates, dbch, *dqkv, duv)
    G["w_in"] = dw_in(transpose_cast(S["x0"]), parts)
    bias = jnp.zeros((1, D), f32)
    if on_grads is not None:
        bias = bias + on_grads({"w_in": G["w_in"]})
    dx0 = dx_in(dr1, parts, W["w_in"], bias)
    started = on_grads({"dx": dx0}) if on_grads is not None else None
    return dx0, G, started


def prep_layer_weights(Wl):
    W = dict(Wl)
    tril = jnp.tril(jnp.ones((BLK, BLK), f32))
    W["wst"] = (Wl["w_s"] * tril[None]).astype(MX)
    W["bsx"] = jnp.broadcast_to(Wl["b_s"][:, :, None], (8, BLK, BLK))
    for n in ("gmlp_ln_g", "gmlp_ln_b", "ln1_g", "ln1_b", "ln2_g", "ln2_b"):
        W[n] = Wl[n].reshape(1, D)
    W["in_bias"] = jnp.zeros((1, NIN), f32) + Wl.get("after", 0.0)
    return W


def local_step(x, positions, target, layers, on_grads=None):
    cos_t, sin_t = rope_tables(positions)
    Ws, saved = [], []
    h = x
    for Wl in layers:
        h, S, W = layer_fwd(h, prep_layer_weights(Wl(h) if callable(Wl) else Wl), cos_t, sin_t)
        Ws.append(W)
        saved.append(S)
    lsum, dh = loss_grad(h, target)
    if on_grads is not None:
        on_grads(len(Ws), {"loss": lsum})
    grads = [None] * len(Ws)
    started = None
    for l in reversed(range(len(Ws))):
        W = Ws[l]
        if started is not None:
            W = dict(W, ln2_g=W["ln2_g"] + started)
        hook = functools.partial(on_grads, l) if on_grads is not None else None
        dh, grads[l], started = layer_bwd(dh, saved[l], W, cos_t, sin_t, hook)
    return lsum, dh, grads


MESH = pl.DeviceIdType.MESH
ANY = pl.BlockSpec(memory_space=pl.ANY)
BIG = ("w_in", "w_gate", "w_up", "w_down", "p_a", "p_b", "p_c", "w_o")
NBIG = len(BIG)


def _place():
    x, y, c = lax.axis_index("x"), lax.axis_index("y"), lax.axis_index("c")
    return x, y, c, 2 * x + y


def _rcopy(src, dst, send, recv, dev):
    return pltpu.make_async_remote_copy(src_ref=src, dst_ref=dst, send_sem=send, recv_sem=recv, device_id=dev,
                                        device_id_type=MESH)


def _cols(ref, k, width):
    start = k * width if isinstance(k, int) else pl.multiple_of(k * width, LANES)
    return ref.at[:, pl.ds(start, width)]


CHUNK_BYTES = 1 << 20


def _pieces(shape, itemsize, nbytes=CHUNK_BYTES):
    rows, cols = shape[-2], shape[-1]
    per = max(16, nbytes // (cols * itemsize) // 16 * 16)
    out = []
    for lead in (range(shape[0]) if len(shape) == 3 else (None,)):
        for r in range(0, rows, per):
            sl = (pl.ds(r, min(per, rows - r)), slice(None))
            out.append(sl if lead is None else (lead,) + sl)
    return out


def _start_pieces(src, dst, make, nbytes=CHUNK_BYTES):
    for idx in _pieces(src.shape, jnp.dtype(src.dtype).itemsize, nbytes):
        make(src.at[idx], dst.at[idx]).start()


def gather_halves(shards):
    n = len(shards)

    def body(*refs):
        srcs, dsts = refs[:n], refs[n:2 * n]
        send, recv, own_send, own_recv = refs[2 * n:]
        x, y, c, k = _place()
        sib = (x, y, 1 - c)
        chips = [(1 - x, y), (x, 1 - y), (1 - x, 1 - y)]

        def slot(a, layer, pos):
            if a == 0:
                return _cols(dsts[0].at[layer], pos, WIN_SHARD)
            return dsts[a].at[pos, layer]

        def ici(a, j, src, dst):
            return _rcopy(src, dst, send.at[a, j], recv.at[a, j], (*chips[j], c))

        def d2d(a, j, src, dst):
            return _rcopy(src, dst, send.at[a, 3 + j], recv.at[a, 3 + j], sib)

        def own(a, layer, src, dst):
            return _rcopy(src, dst, own_send.at[a, layer], own_recv.at[a, layer], sib)

        for a in range(n):
            for j in range(3):
                _start_pieces(srcs[a].at[c], slot(a, c, k), functools.partial(ici, a, j))
        for a in range(n):
            for layer in range(DEPTH):
                _start_pieces(srcs[a].at[layer], slot(a, layer, k), functools.partial(own, a, layer))
        for a in range(n):
            for j, (cx, cy) in enumerate(chips):
                landed = slot(a, c, 2 * cx + cy)
                ici(a, j, landed, landed).wait_recv()
                _start_pieces(landed, landed, functools.partial(d2d, a, j))
        for a in range(n):
            for j, (cx, cy) in enumerate(chips):
                passed = slot(a, 1 - c, 2 * cx + cy)
                d2d(a, j, passed, passed).wait_recv()
                landed = slot(a, c, 2 * cx + cy)
                d2d(a, j, landed, landed).wait_send()
                ici(a, j, srcs[a].at[c], slot(a, c, k)).wait_send()
            for layer in range(DEPTH):
                own(a, layer, srcs[a].at[layer], slot(a, layer, k)).wait()

    outs = [jax.ShapeDtypeStruct((2, shards[0].shape[1], NIN), shards[0].dtype)]
    outs += [jax.ShapeDtypeStruct((NCHIP,) + s.shape, s.dtype) for s in shards[1:]]
    return pl.pallas_call(
        body, name="gather_halves", in_specs=[ANY] * n, out_specs=[ANY] * n, out_shape=outs,
        scratch_shapes=[pltpu.SemaphoreType.DMA((n, 6)), pltpu.SemaphoreType.DMA((n, 6)),
                        pltpu.SemaphoreType.DMA((n, DEPTH)), pltpu.SemaphoreType.DMA((n, DEPTH))],
    )(*shards)


def _gather_slot(dst, pos):
    return _cols(dst, pos, WIN_SHARD) if len(dst.shape) == 2 else dst.at[pos]


def _gather_copy(a, j, src, dst, send, recv, dev):
    return _rcopy(src, dst, send.at[a * NCHIP + j], recv.at[a * NCHIP + j], dev)


def gather_start(tag, shards, after):
    n = len(shards)

    def body(*refs):
        srcs, dsts = refs[:n], refs[n:2 * n]
        send, recv = refs[2 * n + len(after)], refs[2 * n + len(after) + 1]
        token = refs[-1]
        x, y, c, k = _place()
        peers = [(1 - x, y, c), (x, 1 - y, c), (1 - x, 1 - y, c), (x, y, 1 - c)]
        for a in range(n):
            for j, dev in enumerate(peers):
                _start_pieces(srcs[a], _gather_slot(dsts[a], k),
                              lambda s, d, a=a, j=j, dev=dev: _gather_copy(a, j, s, d, send, recv, dev))
        token[...] = jnp.zeros_like(token)

    gathered = [lax.empty((D, NIN) if s.shape == (D, WIN_SHARD) else (NCHIP,) + s.shape, s.dtype) for s in shards]
    ops = [pltpu.with_memory_space_constraint(v, pltpu.HBM) for v in list(shards) + gathered]
    sem = pltpu.SemaphoreType.DMA((n * NCHIP,))
    res = pl.pallas_call(
        body, name=f"gather_start{tag}", in_specs=[HBM] * (2 * n) + [ANY] * len(after),
        out_specs=[SEMS, SEMS] + [HBM] * (2 * n) + [pl.BlockSpec(memory_space=pltpu.VMEM)],
        out_shape=[sem, sem] + [pltpu.HBM(v.shape, v.dtype) for v in ops] + [jax.ShapeDtypeStruct((8, LANES), f32)],
        input_output_aliases={i: 2 + i for i in range(2 * n)},
        compiler_params=pltpu.CompilerParams(has_side_effects=EFFECT),
    )(*ops, *after)
    return res[0], res[1], res[2:2 + n], res[2 + n:2 + 2 * n], res[-1]


def gather_wait(tag, send, recv, shards, gathered, after):
    n = len(shards)

    def body(*refs):
        srcs, dsts = refs[:n], refs[n:2 * n]
        send_r, recv_r = refs[2 * n], refs[2 * n + 1]
        x, y, c, k = _place()
        peers = [(1 - x, y, c), (x, 1 - y, c), (1 - x, 1 - y, c), (x, y, 1 - c)]
        for a in range(n):
            for j, dev in enumerate(peers):
                _gather_copy(a, j, srcs[a], _gather_slot(dsts[a], k), send_r, recv_r, dev).wait_send()
                pos = 2 * dev[0] + dev[1]
                _gather_copy(a, j, srcs[a], _gather_slot(dsts[a], pos), send_r, recv_r, dev).wait_recv()

    ops = list(shards) + list(gathered)
    res = pl.pallas_call(
        body, name=f"gather_wait{tag}", in_specs=[HBM] * (2 * n) + [SEMS, SEMS] + [ANY] * len(after),
        out_specs=[HBM] * (2 * n), out_shape=[pltpu.HBM(v.shape, v.dtype) for v in ops],
        input_output_aliases={i: i for i in range(2 * n)},
        compiler_params=pltpu.CompilerParams(has_side_effects=EFFECT),
    )(*ops, send, recv, *after)
    return res[n:]


def _half(ref, h):
    rows = ref.shape[-2] // 2
    start = pl.multiple_of(h * rows, 16)
    if len(ref.shape) == 2:
        return ref.at[pl.ds(start, rows), :]
    return ref.at[:, pl.ds(start, rows), :]


HBM = pl.BlockSpec(memory_space=pltpu.HBM)
SEMS = pl.BlockSpec(memory_space=pltpu.SEMAPHORE)
EFFECT = pltpu.SideEffectType.DATAFLOW_SIDE_EFFECTING


def rs_pair_start(tag, grads, halves=True):
    n = len(grads)

    def body(*refs):
        g, theirs = refs[:n], refs[n:2 * n]
        send, recv = refs[2 * n], refs[2 * n + 1]
        x, y, c, _ = _place()
        for a in range(n):
            _start_pieces(_half(g[a], 1 - c) if halves else g[a], theirs[a],
                          lambda s, d, a=a: _rcopy(s, d, send.at[a], recv.at[a], (x, y, 1 - c)))
        refs[-1][...] = jnp.zeros_like(refs[-1])

    lands = [lax.empty(g.shape[:-2] + (g.shape[-2] // 2 if halves else g.shape[-2], g.shape[-1]), g.dtype) for g in grads]
    ops = [pltpu.with_memory_space_constraint(v, pltpu.HBM) for v in list(grads) + lands]
    sem = pltpu.SemaphoreType.DMA((n,))
    res = pl.pallas_call(
        body, name=f"rs_pair_start{tag}", in_specs=[HBM] * (2 * n),
        out_specs=[SEMS, SEMS] + [HBM] * (2 * n) + [pl.BlockSpec(memory_space=pltpu.VMEM)],
        out_shape=[sem, sem] + [pltpu.HBM(v.shape, v.dtype) for v in ops] + [jax.ShapeDtypeStruct((8, LANES), f32)],
        input_output_aliases={i: 2 + i for i in range(2 * n)},
        compiler_params=pltpu.CompilerParams(has_side_effects=EFFECT),
    )(*ops)
    return res[0], res[1], res[2:2 + n], res[2 + n:2 + 2 * n], res[-1]


def rs_pair_wait(tag, send, recv, grads, theirs, after, halves=True):
    n = len(grads)

    def body(*refs):
        g, land = refs[:n], refs[n:2 * n]
        send_r, recv_r = refs[2 * n], refs[2 * n + 1]
        x, y, c, _ = _place()
        for a in range(n):
            cp = _rcopy(_half(g[a], 1 - c) if halves else g[a], land[a], send_r.at[a], recv_r.at[a], (x, y, 1 - c))
            cp.wait_send()
            cp.wait_recv()

    ops = list(grads) + list(theirs)
    res = pl.pallas_call(
        body, name=f"rs_pair_wait{tag}", in_specs=[HBM] * (2 * n) + [SEMS, SEMS] + [ANY] * len(after),
        out_specs=[HBM] * (2 * n), out_shape=[pltpu.HBM(v.shape, v.dtype) for v in ops],
        input_output_aliases={i: i for i in range(2 * n)},
        compiler_params=pltpu.CompilerParams(has_side_effects=EFFECT),
    )(*ops, send, recv, *after)
    return res[:n], res[n:]


def _chip_piece(ref, k):
    return _cols(ref, k, WIN_SHARD) if len(ref.shape) == 2 else ref.at[k]


def _chip_copy(a, k, src, dst, send, recv, me, c):
    return _rcopy(src, dst, send.at[a * NCHIP + k], recv.at[a * NCHIP + me], (k // 2, k % 2, c))


def rs_chips_start(tag, sums):
    n = len(sums)

    def pshape(s):
        return (NCHIP, s[0], WIN_SHARD) if len(s) == 2 else s

    def body(*refs):
        s, land = refs[:n], refs[n:2 * n]
        send, recv = refs[2 * n], refs[2 * n + 1]
        token = refs[-1]
        x, y, c, me = _place()
        for k in range(NCHIP):
            @pl.when(me != k)
            def _():
                for a in range(n):
                    _start_pieces(_chip_piece(s[a], k), land[a].at[me],
                                  lambda src, dst, a=a: _chip_copy(a, k, src, dst, send, recv, me, c))
        token[...] = jnp.zeros_like(token)

    lands = [lax.empty(pshape(v.shape), v.dtype) for v in sums]
    ops = [pltpu.with_memory_space_constraint(v, pltpu.HBM) for v in list(sums) + lands]
    sem = pltpu.SemaphoreType.DMA((n * NCHIP,))
    res = pl.pallas_call(
        body, name=f"rs_chips_start{tag}", in_specs=[HBM] * (2 * n),
        out_specs=[SEMS, SEMS] + [HBM] * (2 * n) + [pl.BlockSpec(memory_space=pltpu.VMEM)],
        out_shape=[sem, sem] + [pltpu.HBM(v.shape, v.dtype) for v in ops] + [jax.ShapeDtypeStruct((8, LANES), f32)],
        input_output_aliases={i: 2 + i for i in range(2 * n)},
        compiler_params=pltpu.CompilerParams(has_side_effects=EFFECT),
    )(*ops)
    return res[0], res[1], res[2:2 + n], res[2 + n:2 + 2 * n], res[-1]


def rs_chips_wait(tag, send, recv, sums, lands, after):
    n = len(sums)

    def body(*refs):
        s, land = refs[:n], refs[n:2 * n]
        send_r, recv_r = refs[2 * n], refs[2 * n + 1]
        x, y, c, me = _place()
        for k in range(NCHIP):
            @pl.when(me != k)
            def _():
                for a in range(n):
                    piece = _chip_piece(s[a], k)
                    _chip_copy(a, k, piece, land[a].at[me], send_r, recv_r, me, c).wait_send()
                    _rcopy(piece, land[a].at[k], send_r.at[a * NCHIP + k], recv_r.at[a * NCHIP + k],
                           (k // 2, k % 2, c)).wait_recv()

    ops = list(sums) + list(lands)
    res = pl.pallas_call(
        body, name=f"rs_chips_wait{tag}", in_specs=[HBM] * (2 * n) + [SEMS, SEMS] + [ANY] * len(after),
        out_specs=[HBM] * (2 * n), out_shape=[pltpu.HBM(v.shape, v.dtype) for v in ops],
        input_output_aliases={i: i for i in range(2 * n)},
        compiler_params=pltpu.CompilerParams(has_side_effects=EFFECT),
    )(*ops, send, recv, *after)
    return res[:n], res[n:]


def _row_tile(rows, cols, itemsize=4, target=2 << 20):
    best = 8
    for t in range(8, rows + 1, 8):
        if rows % t == 0 and t * cols * itemsize <= target:
            best = t
    return best


GRAD_WIRE = jnp.bfloat16


def add_half(name, g, t, c):
    cols, half = t.shape[-1], t.shape[-2]
    nblk = 1 if t.ndim == 2 else t.shape[0]
    tr = _row_tile(half, cols)
    per = half // tr

    def body(c_ref, g_r, t_r, o_r):
        o_r[...] = (g_r[...] + t_r[...]).astype(o_r.dtype)

    tile_t = pl.BlockSpec((tr, cols), lambda i, c_ref: (i, 0))
    tile_g = pl.BlockSpec((tr, cols), lambda i, c_ref: ((i // per) * 2 * per + c_ref[0] * per + i % per, 0))
    out = pl.pallas_call(
        body, name=name, out_shape=jax.ShapeDtypeStruct((nblk * half, cols), GRAD_WIRE),
        grid_spec=pltpu.PrefetchScalarGridSpec(num_scalar_prefetch=1, grid=(nblk * per,), in_specs=[tile_g, tile_t],
                                               out_specs=tile_t),
        compiler_params=_cp(("parallel",)),
    )(c.reshape(1).astype(jnp.int32), g.reshape(nblk * 2 * half, cols), t.reshape(nblk * half, cols))
    return out.reshape(t.shape)


def add_chips(name, land, own):
    _, rows, cols = land.shape
    tr = _row_tile(rows, cols, target=1 << 20)

    def body(land_r, own_r, o_r):
        me = 2 * lax.axis_index("x") + lax.axis_index("y")
        for k in range(NCHIP):
            @pl.when(me == k)
            def _():
                acc = None
                for j in range(NCHIP):
                    t = (own_r[...] if j == k else land_r[j]).astype(f32)
                    acc = t if acc is None else acc + t
                o_r[...] = acc

    tile = pl.BlockSpec((tr, cols), lambda i: (i, 0))
    return pl.pallas_call(
        body, name=name, grid=(rows // tr,), in_specs=[pl.BlockSpec((NCHIP, tr, cols), lambda i: (0, i, 0)), tile],
        out_specs=tile, out_shape=jax.ShapeDtypeStruct((rows, cols), f32), compiler_params=_cp(("parallel",)),
    )(land, own)


def reduce_scatter_pair(tag, G):
    names = tuple(G)
    grads = [G[n] if G[n].ndim == 3 or n == "w_in" else G[n].reshape(NCHIP, D // NCHIP, D) for n in names]
    send, recv, grads, theirs, token = rs_pair_start(tag, grads)
    return (tag, names, send, recv, grads, theirs), token[0, 0]


def reduce_scatter_chips(state, after):
    c = lax.axis_index("c")
    tag, names, send, recv, grads, theirs = state
    grads, theirs = rs_pair_wait(tag, send, recv, grads, theirs, after)
    sums = [add_half(f"rs_add_pair{tag}_{n}", g, t, c) for n, g, t in zip(names, grads, theirs)]
    send, recv, sums, lands, token = rs_chips_start(tag, sums)
    return (tag, names, send, recv, sums, lands), token[0, 0]


def reduce_scatter_finish(state, after):
    me = 2 * lax.axis_index("x") + lax.axis_index("y")
    tag, names, send, recv, sums, lands = state
    sums, landed = rs_chips_wait(tag, send, recv, sums, lands, after)
    halves = []
    for n, s, v in zip(names, sums, landed):
        own = lax.dynamic_slice_in_dim(s, me * WIN_SHARD, WIN_SHARD, axis=1) if s.ndim == 2 else \
            lax.dynamic_index_in_dim(s, me, 0, keepdims=False)
        halves.append(add_chips(f"rs_add_chips{tag}_{n}", v, own))
    send, recv, halves, others, _ = rs_pair_start("_join" + tag, halves, halves=False)
    return tag, names, send, recv, halves, others


def reduce_scatter_join(state, after):
    tag, names, send, recv, halves, others = state
    halves, others = rs_pair_wait("_join" + tag, send, recv, halves, others, after, halves=False)
    return dict(zip(names, zip(halves, others)))


NDEV = 8


def _small_copy(r, src, dst, send, recv, x, y, c):
    return _rcopy(src, dst, send.at[r - 1], recv.at[r - 1], (x ^ (r >> 2), y ^ ((r >> 1) & 1), c ^ (r & 1)))


def small_start(pack):
    def body(p, land, send, recv, p_thru, land_thru, token):
        x, y, c, _ = _place()
        me = 4 * x + 2 * y + c
        for r in range(1, NDEV):
            _start_pieces(p, land.at[me], lambda s, d, r=r: _small_copy(r, s, d, send, recv, x, y, c), 128 << 10)
        token[...] = jnp.zeros_like(token)

    ops = [pltpu.with_memory_space_constraint(v, pltpu.HBM) for v in (pack, lax.empty((NDEV,) + pack.shape, f32))]
    sem = pltpu.SemaphoreType.DMA((NDEV - 1,))
    return pl.pallas_call(
        body, name="small_start", in_specs=[HBM, HBM],
        out_specs=[SEMS, SEMS, HBM, HBM, pl.BlockSpec(memory_space=pltpu.VMEM)],
        out_shape=[sem, sem] + [pltpu.HBM(v.shape, v.dtype) for v in ops] + [jax.ShapeDtypeStruct((8, LANES), f32)],
        input_output_aliases={0: 2, 1: 3}, compiler_params=pltpu.CompilerParams(has_side_effects=EFFECT),
    )(*ops)


def small_wait(send, recv, pack, land, after):
    def body(p, land_r, send_r, recv_r, *rest):
        x, y, c, _ = _place()
        me = 4 * x + 2 * y + c
        for r in range(1, NDEV):
            _small_copy(r, p, land_r.at[me], send_r, recv_r, x, y, c).wait_send()
            src = 4 * (x ^ (r >> 2)) + 2 * (y ^ ((r >> 1) & 1)) + (c ^ (r & 1))
            _small_copy(r, p, land_r.at[src], send_r, recv_r, x, y, c).wait_recv()

    return pl.pallas_call(
        body, name="small_wait", in_specs=[HBM, HBM, SEMS, SEMS] + [ANY] * len(after), out_specs=[HBM, HBM],
        out_shape=[pltpu.HBM(pack.shape, f32), pltpu.HBM(land.shape, f32)], input_output_aliases={0: 0, 1: 1},
        compiler_params=pltpu.CompilerParams(has_side_effects=EFFECT),
    )(pack, land, send, recv, *after)


def small_sum(land, pack):
    def body(land_r, p_r, o_r):
        me = 4 * lax.axis_index("x") + 2 * lax.axis_index("y") + lax.axis_index("c")
        for k in range(NDEV):
            @pl.when(me == k)
            def _():
                acc = None
                for d in range(NDEV):
                    t = p_r[...] if d == k else land_r[d]
                    acc = t if acc is None else acc + t
                o_r[...] = acc

    vm = pl.BlockSpec(memory_space=pltpu.VMEM)
    return pl.pallas_call(
        body, name="small_sum", in_specs=[vm, vm], out_specs=vm, out_shape=jax.ShapeDtypeStruct(pack.shape, f32),
        compiler_params=pltpu.CompilerParams(vmem_limit_bytes=40 << 20),
    )(land, pack)


def _adamw_math(w, g, m, v):
    m = ADAM_B1 * m + (1.0 - ADAM_B1) * g
    v = ADAM_B2 * v + (1.0 - ADAM_B2) * (g * g)
    m_hat = m / (1.0 - ADAM_B1 ** ADAM_STEP)
    v_hat = v / (1.0 - ADAM_B2 ** ADAM_STEP)
    return -ADAM_LR * (m_hat / (jnp.sqrt(v_hat) + ADAM_EPS) + ADAM_WD * w), m, v


def adamw_big(name, halves, w, m, v):
    _, R, C = w.shape
    tr = _row_tile(R // 2, C, target=1 << 20)
    nt = R // 2 // tr

    def body(a0, b0, a1, b1, w_r, m_r, v_r, g_o, d_o, m_o, v_o):
        mine = pl.program_id(1) == lax.axis_index("c")
        g = jnp.where(pl.program_id(0) == 0, jnp.where(mine, a0[...], b0[...]), jnp.where(mine, a1[...], b1[...]))
        g_o[...] = g
        d_o[...], m_o[...], v_o[...] = _adamw_math(w_r[...], g, m_r[...], v_r[...])

    stk = pl.BlockSpec((None, tr, C), lambda l, h, i: (l, h * nt + i, 0))
    lay0 = pl.BlockSpec((tr, C), lambda l, h, i: (jnp.where(l == 0, i, nt - 1), 0))
    lay1 = pl.BlockSpec((tr, C), lambda l, h, i: (jnp.where(l == 0, 0, i), 0))
    return pl.pallas_call(
        body, name=name, grid=(DEPTH, 2, nt),
        in_specs=[lay0, lay0, lay1, lay1, stk, stk, stk],
        out_specs=[stk] * 4, out_shape=[jax.ShapeDtypeStruct(w.shape, f32)] * 4,
        compiler_params=_cp(("arbitrary", "arbitrary", "arbitrary")),
    )(*halves[0], *halves[1], w, m, v)


def adamw_small(name, g, w, m, v):
    def body(g_r, w_r, m_r, v_r, d_o, m_o, v_o):
        d_o[...], m_o[...], v_o[...] = _adamw_math(w_r[...], g_r[...], m_r[...], v_r[...])

    return pl.pallas_call(body, name=name, out_shape=[jax.ShapeDtypeStruct(w.shape, f32)] * 3)(g, w, m, v)


WEIGHTS = ("w_in", "conv_w", "gmlp_ln_g", "gmlp_ln_b", "w_s", "b_s", "p_a", "p_b", "p_c", "w_o", "ln1_g", "ln1_b",
           "w_gate", "w_up", "w_down", "ln2_g", "ln2_b")
VECS = ("ln1_g", "ln1_b", "ln2_g", "ln2_b", "gmlp_ln_g", "gmlp_ln_b")
ROWS_VEC, ROWS_BS, ROWS_WS, ROWS_CONV = D // LANES, 8, 8 * BLK, 3 * D // LANES
ROWS_LAYER = len(VECS) * ROWS_VEC + ROWS_BS + ROWS_WS + ROWS_CONV


def _pack_small(per_layer, tail):
    parts = []
    for P in per_layer:
        parts += [P[n].reshape(ROWS_VEC, LANES) for n in VECS]
        parts += [P["b_s"].reshape(ROWS_BS, LANES), P["w_s"].reshape(ROWS_WS, LANES), P["conv_w"].reshape(ROWS_CONV, LANES)]
    return jnp.concatenate(parts + [tail], axis=0)


def _unpack_small(pack):
    out = []
    for l in range(DEPTH):
        r = l * ROWS_LAYER
        P = {}
        for n in VECS:
            P[n] = pack[r:r + ROWS_VEC].reshape(D)
            r += ROWS_VEC
        P["b_s"] = pack[r:r + ROWS_BS].reshape(8, BLK)
        r += ROWS_BS
        P["w_s"] = pack[r:r + ROWS_WS].reshape(8, BLK, BLK)
        r += ROWS_WS
        P["conv_w"] = pack[r:r + ROWS_CONV].reshape(3, D)
        out.append(P)
    return out, pack[DEPTH * ROWS_LAYER:]


def kernel(x, positions, w_in, conv_w, gmlp_ln_g, gmlp_ln_b, w_s, b_s, p_a, p_b, p_c, w_o, ln1_g, ln1_b, w_gate, w_up, w_down, ln2_g, ln2_b, loss_target, m_w_in, m_conv_w, m_gmlp_ln_g, m_gmlp_ln_b, m_w_s, m_b_s, m_p_a, m_p_b, m_p_c, m_w_o, m_ln1_g, m_ln1_b, m_w_gate, m_w_up, m_w_down, m_ln2_g, m_ln2_b, v_w_in, v_conv_w, v_gmlp_ln_g, v_gmlp_ln_b, v_w_s, v_b_s, v_p_a, v_p_b, v_p_c, v_w_o, v_ln1_g, v_ln1_b, v_w_gate, v_w_up, v_w_down, v_ln2_g, v_ln2_b):
    Wt = dict(w_in=w_in, conv_w=conv_w, gmlp_ln_g=gmlp_ln_g, gmlp_ln_b=gmlp_ln_b, w_s=w_s, b_s=b_s, p_a=p_a, p_b=p_b,
              p_c=p_c, w_o=w_o, ln1_g=ln1_g, ln1_b=ln1_b, w_gate=w_gate, w_up=w_up, w_down=w_down, ln2_g=ln2_g, ln2_b=ln2_b)
    Mt = dict(w_in=m_w_in, conv_w=m_conv_w, gmlp_ln_g=m_gmlp_ln_g, gmlp_ln_b=m_gmlp_ln_b, w_s=m_w_s, b_s=m_b_s, p_a=m_p_a,
              p_b=m_p_b, p_c=m_p_c, w_o=m_w_o, ln1_g=m_ln1_g, ln1_b=m_ln1_b, w_gate=m_w_gate, w_up=m_w_up,
              w_down=m_w_down, ln2_g=m_ln2_g, ln2_b=m_ln2_b)
    Vt = dict(w_in=v_w_in, conv_w=v_conv_w, gmlp_ln_g=v_gmlp_ln_g, gmlp_ln_b=v_gmlp_ln_b, w_s=v_w_s, b_s=v_b_s, p_a=v_p_a,
              p_b=v_p_b, p_c=v_p_c, w_o=v_w_o, ln1_g=v_ln1_g, ln1_b=v_ln1_b, w_gate=v_w_gate, w_up=v_w_up,
              w_down=v_w_down, ln2_g=v_ln2_g, ln2_b=v_ln2_b)
    chip = 2 * lax.axis_index("x") + lax.axis_index("y")
    cw = D // NCHIP

    def gathered_weights(names, arrays):
        Wl = dict(zip(names, arrays))
        for n in ("p_a", "p_c", "w_o"):
            Wl[n] = Wl[n].reshape(D, D)
        return Wl

    def small_weights(l, conv_all):
        Wl = {n: Wt[n][l] for n in VECS + ("w_s", "b_s")}
        Wl["conv_w"] = conv_all[:, l].transpose(1, 0, 2).reshape(3, D)
        return Wl

    w_in0, conv_all = gather_halves([Wt["w_in"][0].astype(MX).reshape(2, D // 2, WIN_SHARD), conv_w])
    rest = BIG[1:]
    *late0, coming0 = gather_start("0", [Wt[n][0].astype(MX) for n in rest], [conv_all])
    *late1, coming1 = gather_start("1", [Wt[n][1].astype(MX) for n in BIG], [conv_all, coming0])
    W0 = dict(small_weights(0, conv_all), w_in=w_in0.reshape(D, NIN), after=coming1[0, 0],
              late=lambda y: gathered_weights(rest, gather_wait("0", *late0, [y])))

    def W1(h):
        return dict(small_weights(1, conv_all), **gathered_weights(BIG, gather_wait("1", *late1, [h])))

    layers = [W0, W1]

    rs_state, rs_started, held = {}, {}, {}

    def start_exchange(l, g):
        if "loss" in g:
            held[l] = g
            return None
        if "conv_w" in g:
            held[l] = g
            rs_state[(l, False)], started = reduce_scatter_chips(rs_state[(l, False)], [g["w_s"], g["conv_w"]])
            if l == 0:
                pack = _pack_small([held[j] for j in range(DEPTH)], held[DEPTH]["loss"])
                *held["small"], token = small_start(pack)
                started = started + token[0, 0]
            return started
        if "dx" in g:
            rs_state[(l, True)], rs_started[(l, True)] = reduce_scatter_chips(rs_state[(l, True)], [g["dx"]])
            return rs_started[(l, True)]
        key = (l, "w_in" in g)
        rs_state[key], started = reduce_scatter_pair(f"{l}{'b' if key[1] else 'a'}", g)
        return started

    _, grad_x, _ = local_step(x[0], positions[0], loss_target[0], layers, start_exchange)

    last = jnp.zeros((8, LANES), f32) + rs_started[(0, True)]
    behind = [grad_x, last]
    red = [dict() for _ in range(DEPTH)]
    swaps = {key: reduce_scatter_finish(rs_state[key], behind) for key in ((1, False), (1, True), (0, False))}
    small, tail = _unpack_small(small_sum(*reversed(small_wait(*held["small"], behind))))
    loss = tail[0, 0]

    G, DW, NM, NV = {}, {}, {}, {}
    zc = jnp.zeros((3, D), f32)
    wp = _pack_small([{**{n: Wt[n][l] for n in VECS + ("b_s", "w_s")}, "conv_w": zc} for l in range(DEPTH)], jnp.zeros((8, LANES), f32))
    mp = _pack_small([{**{n: Mt[n][l] for n in VECS + ("b_s", "w_s")}, "conv_w": zc} for l in range(DEPTH)], jnp.zeros((8, LANES), f32))
    vp = _pack_small([{**{n: Vt[n][l] for n in VECS + ("b_s", "w_s")}, "conv_w": zc} for l in range(DEPTH)], jnp.ones((8, LANES), f32))
    gp = _pack_small(small, jnp.zeros((8, LANES), f32))
    outs = [_unpack_small(a)[0] for a in adamw_small("adamw_small", gp, wp, mp, vp)]
    for n in VECS + ("b_s", "w_s"):
        G[n] = jnp.stack([small[l][n] for l in range(DEPTH)])
        DW[n], NM[n], NV[n] = (jnp.stack([o[l][n] for l in range(DEPTH)]) for o in outs)
    gconv = jnp.stack([lax.dynamic_slice(small[l]["conv_w"], (0, chip * cw), (3, cw)) for l in range(DEPTH)])
    G["conv_w"] = gconv
    flat = lambda a: a.reshape(DEPTH * 3, cw)
    d, m2, v2 = adamw_small("adamw_conv", flat(gconv), flat(conv_w), flat(m_conv_w), flat(v_conv_w))
    DW["conv_w"], NM["conv_w"], NV["conv_w"] = (a.reshape(DEPTH, 3, cw) for a in (d, m2, v2))

    for key in swaps:
        red[key[0]].update(reduce_scatter_join(swaps[key], [d, DW["ln2_b"]]))
    updated = {}
    for n in BIG[1:]:
        tr = (lambda a: jnp.swapaxes(a, 1, 2)) if n in ("w_gate", "w_up") else (lambda a: a)
        updated[n] = adamw_big("adamw_" + n, (red[0][n], red[1][n]), tr(Wt[n]), tr(Mt[n]), tr(Vt[n]))
        G[n], DW[n], NM[n], NV[n] = map(tr, updated[n])
    done = [d, DW["ln2_b"], red[1]["w_in"][1]] + [updated[n][1] for n in BIG[1:]]
    red[0].update(reduce_scatter_join(reduce_scatter_finish(rs_state[(0, True)], done), [updated["w_o"][1]]))
    G["w_in"], DW["w_in"], NM["w_in"], NV["w_in"] = adamw_big(
        "adamw_w_in", (red[0]["w_in"], red[1]["w_in"]), Wt["w_in"], Mt["w_in"], Vt["w_in"])

    return (loss, grad_x[None], *[G[n] for n in WEIGHTS], *[DW[n] for n in WEIGHTS], *[NM[n] for n in WEIGHTS],
            *[NV[n] for n in WEIGHTS])
```

```python
import functools
import math

import jax
import jax.numpy as jnp
from jax import lax
from jax.experimental import pallas as pl
from jax.experimental.pallas import tpu as pltpu

D = 1024
NIN = 12800
DFF = 2816
NCHIP = 4
FB = DFF // NCHIP
WIN_SHARD = NIN // NCHIP
DEPTH = 2
GROUPS = ((128, 1), (512, 4), (2048, 16))
HD = 64
BLK = 128
AO = 512
ALPHA = (2 * DEPTH) ** 0.25
EPS = 1e-5
ROPE_THETA = 10000.0
LANES = 128
NEG = -1e30

C_GATES, C_BCH, C_QKV, C_UV = 0, 3 * D, 6 * D, 6 * D + 9 * AO

MX = jnp.bfloat16
ACT = jnp.bfloat16

ADAM_LR, ADAM_B1, ADAM_B2, ADAM_EPS, ADAM_WD, ADAM_STEP = 0.001, 0.9, 0.999, 1e-08, 0.01, 10

f32 = jnp.float32
NT = (((1,), (1,)), ((), ()))
TN = (((0,), (0,)), ((), ()))


def _cp(sem, vmem_mb=48):
    return pltpu.CompilerParams(dimension_semantics=sem, vmem_limit_bytes=vmem_mb << 20)


def _dot(a, b, dims=None):
    if dims is None:
        return jnp.dot(a, b, preferred_element_type=f32)
    return lax.dot_general(a, b, dims, preferred_element_type=f32)


def _ln_stats(r):
    mu = jnp.mean(r, axis=-1, keepdims=True)
    xc = r - mu
    var = jnp.mean(xc * xc, axis=-1, keepdims=True)
    rstd = lax.rsqrt(var + EPS)
    return xc * rstd, rstd


def _ln_bwd(dy, xhat, rstd, g):
    dxh = dy * g
    return rstd * (dxh - jnp.mean(dxh, axis=-1, keepdims=True) - xhat * jnp.mean(dxh * xhat, axis=-1, keepdims=True))


def _gelu(x):
    return 0.5 * x * (1.0 + lax.erf(x * (1.0 / math.sqrt(2.0))))


def _gelu_and_grad(x):
    cdf = 0.5 * (1.0 + lax.erf(x * (1.0 / math.sqrt(2.0))))
    return x * cdf, cdf + x * jnp.exp(-0.5 * x * x) * (1.0 / math.sqrt(2.0 * math.pi))


def _sigmoid(x):
    return 0.5 * jnp.tanh(0.5 * x) + 0.5


def _acc_rows(o_ref, first, val):
    @pl.when(first)
    def _():
        o_ref[...] = jnp.zeros_like(o_ref)
    o_ref[...] += jnp.sum(val, axis=0, keepdims=True)


def mm_in(x, w, bias):
    T = x.shape[0]
    tm, tn = min(2048, T), 1280

    def body(x_ref, w_ref, b_ref, o_ref, xb):
        @pl.when(pl.program_id(1) == 0)
        def _():
            xb[...] = x_ref[...].astype(MX)
        o_ref[...] = (_dot(xb[...], w_ref[...]) + b_ref[...]).astype(o_ref.dtype)

    return pl.pallas_call(
        body, name="mm_in", grid=(T // tm, NIN // tn),
        in_specs=[pl.BlockSpec((tm, D), lambda i, j: (i, 0), pipeline_mode=pl.Buffered(1)),
                  pl.BlockSpec((D, tn), lambda i, j: (0, j)), pl.BlockSpec((1, tn), lambda i, j: (0, j))],
        out_specs=pl.BlockSpec((tm, tn), lambda i, j: (i, j)),
        out_shape=jax.ShapeDtypeStruct((T, NIN), ACT),
        scratch_shapes=[pltpu.VMEM((tm, D), MX)],
        compiler_params=_cp(("parallel", "arbitrary")),
    )(x, w, bias)


HALO = 16
TM_AC = 256


def _uv_specs():
    return [pl.BlockSpec((TM_AC, 512), functools.partial(lambda i, j: (i, j), j=C_UV // 512 + j)) for j in range(4)]


def _gmlp_fwd(up, vp, ws_ref, bs_ref, lg, lb, u=None, gv=None):
    u = _gelu(up) if u is None else u
    xhat, rstd = _ln_stats(_gelu(vp) if gv is None else gv)
    vn = xhat * lg + lb
    vnb = vn.astype(MX)
    rows = []
    for c in range(up.shape[0] // BLK):
        r = slice(c * BLK, (c + 1) * BLK)
        rows.append(jnp.concatenate(
            [_dot(ws_ref[g], vnb[r, g * BLK:(g + 1) * BLK]) + bs_ref[g] for g in range(8)], axis=1))
    return u, vn, xhat, rstd, jnp.concatenate(rows, axis=0)


def mix_ac_fwd(proj, conv_w, wst, bsx, lg, lb):
    T = proj.shape[0]
    tm = TM_AC

    def body(bch, halo, u0, u1, v0, v1, cw, ws, bs, lg_ref, lb_ref, ya, yc, zs):
        i = pl.program_id(0)
        pb = bch[...].astype(f32)
        z = pb[:, D:2 * D] * pb[:, 2 * D:]
        hz = halo[:, :D].astype(f32) * halo[:, D:].astype(f32)
        zs[0:HALO, :] = jnp.where(i > 0, hz, 0.0)
        zs[HALO:HALO + tm, :] = z
        cv = cw[0:1, :] * zs[HALO - 2:HALO - 2 + tm, :] + cw[1:2, :] * zs[HALO - 1:HALO - 1 + tm, :] + cw[2:3, :] * z
        ya[...] = (pb[:, :D] * cv).astype(ya.dtype)
        up = jnp.concatenate([u0[...], u1[...]], axis=1).astype(f32)
        vp = jnp.concatenate([v0[...], v1[...]], axis=1).astype(f32)
        u, _, _, _, sp = _gmlp_fwd(up, vp, ws, bs, lg_ref[...], lb_ref[...])
        yc[...] = (u * sp).astype(yc.dtype)

    full = lambda shape: pl.BlockSpec(shape, lambda i: (0,) * len(shape))
    return pl.pallas_call(
        body, name="mix_ac_fwd", grid=(T // tm,),
        in_specs=[pl.BlockSpec((tm, 3 * D), lambda i: (i, 1)),
                  pl.BlockSpec((HALO, 2 * D), lambda i: (jnp.maximum(i * (tm // HALO) - 1, 0), 2)),
                  *_uv_specs(), full((3, D)), full((8, BLK, BLK)), full((8, BLK, BLK)), full((1, D)), full((1, D))],
        out_specs=[pl.BlockSpec((tm, D), lambda i: (i, 0))] * 2,
        out_shape=[jax.ShapeDtypeStruct((T, D), MX)] * 2,
        scratch_shapes=[pltpu.VMEM((HALO + tm, D), f32)],
        compiler_params=_cp(("parallel",)),
    )(proj, proj, proj, proj, proj, proj, conv_w, wst, bsx, lg, lb)


def _swap_halves(x):
    lane = lax.broadcasted_iota(jnp.int32, x.shape, 1)
    return jnp.where((lane % HD) < HD // 2, pltpu.roll(x, x.shape[1] - HD // 2, 1), pltpu.roll(x, HD // 2, 1))


def _tile4(t):
    return jnp.concatenate([t] * (AO // LANES), axis=1)


TM_FOLD = 1024


def _fold_out(nat, x, out_ref, d):
    if d == 1:
        out_ref[0] = x.astype(out_ref.dtype)
        return
    rows = x.shape[0] // d
    for j in range(AO // LANES):
        nat[j] = x[:, j * LANES:(j + 1) * LANES]
    for r in range(d):
        out_ref[r] = jnp.concatenate(
            [nat.at[j][pl.ds(r, rows, stride=d), :] for j in range(AO // LANES)], axis=1).astype(out_ref.dtype)


def _unfold_in(nat, in_ref, d):
    if d == 1:
        return in_ref[0].astype(f32)
    rows = in_ref.shape[1]
    for r in range(d):
        v = in_ref[r].astype(f32)
        for j in range(AO // LANES):
            nat.at[j][pl.ds(r, rows, stride=d), :] = v[:, j * LANES:(j + 1) * LANES]
    return jnp.concatenate([nat[j] for j in range(AO // LANES)], axis=1)


def fold_rope(proj, cos_t, sin_t, g, d):
    T = proj.shape[0]
    tm = TM_FOLD
    rows = tm // d

    def body(x_ref, c_ref, s_ref, q_o, k_o, v_o, nat):
        cos, sin = _tile4(c_ref[...]), _tile4(s_ref[...])
        for part, out, scale in ((0, q_o, HD ** -0.5), (1, k_o, 1.0), (2, v_o, None)):
            x = x_ref[:, part * AO:(part + 1) * AO].astype(f32)
            if scale is not None:
                x = (x * cos + _swap_halves(x) * sin) * scale
            _fold_out(nat, x, out, d)

    fold_spec = pl.BlockSpec((d, rows, AO), lambda i: (0, i, 0))
    return pl.pallas_call(
        body, name=f"fold_rope{g}", grid=(T // tm,),
        in_specs=[pl.BlockSpec((tm, 3 * AO), lambda i: (i, C_QKV // (3 * AO) + g)),
                  pl.BlockSpec((tm, LANES), lambda i: (i, 0)), pl.BlockSpec((tm, LANES), lambda i: (i, 0))],
        out_specs=[fold_spec] * 3,
        out_shape=[jax.ShapeDtypeStruct((d, T // d, AO), MX)] * 3,
        scratch_shapes=[pltpu.VMEM((AO // LANES, tm, LANES), f32)],
        compiler_params=_cp(("parallel",)),
    )(proj, cos_t, sin_t)


def _stack_heads(x):
    lane = lax.broadcasted_iota(jnp.int32, x.shape, 1)
    z = jnp.zeros_like(x)
    return jnp.concatenate([jnp.where(lane < HD, x, z), jnp.where(lane >= HD, x, z)], axis=0)


def _unstack_heads(y):
    lane = lax.broadcasted_iota(jnp.int32, (BLK, LANES), 1)
    return jnp.where(lane < HD, y[:BLK], y[BLK:])


def _window_masks():
    row = lax.broadcasted_iota(jnp.int32, (2 * BLK, 2 * BLK), 0) % BLK
    col = lax.broadcasted_iota(jnp.int32, (2 * BLK, 2 * BLK), 1)
    return (col < BLK) & (col >= row), (col >= BLK) & (col - BLK <= row)


def _two_blocks(ref, b):
    r0 = pl.multiple_of(b * BLK, BLK)
    rp = pl.multiple_of(jnp.maximum(b - 1, 0) * BLK, BLK)
    return jnp.concatenate([ref[pl.ds(rp, BLK), :], ref[pl.ds(r0, BLK), :]], axis=0)


def _merge_masks():
    row = lax.broadcasted_iota(jnp.int32, (2 * BLK, BLK), 0) % BLK
    col = lax.broadcasted_iota(jnp.int32, (2 * BLK, BLK), 1)
    return col <= row, col == row


def attn_fwd(qf, kf, vf, g, nb):
    T = qf.shape[0]

    def body(q_ref, k_ref, v_ref, o_ref, l_ref):
        cur_m, own_m = _merge_masks()

        def step(b, carry):
            r0 = pl.multiple_of(b * BLK, BLK)
            rp = pl.multiple_of(jnp.maximum(b - 1, 0) * BLK, BLK)
            qs = _stack_heads(q_ref[pl.ds(r0, BLK), :])
            vc, vp = v_ref[pl.ds(r0, BLK), :], v_ref[pl.ds(rp, BLK), :]
            sp = jnp.where((b % nb) != 0, _dot(qs, k_ref[pl.ds(rp, BLK), :], NT), NEG)
            s = jnp.where(cur_m, _dot(qs, k_ref[pl.ds(r0, BLK), :], NT), sp)
            s_own = jnp.sum(jnp.where(own_m, sp, 0.0), axis=-1, keepdims=True)
            m = jnp.maximum(jnp.max(s, axis=-1, keepdims=True), s_own)
            p, p_own = jnp.exp(s - m), jnp.exp(s_own - m)
            l = jnp.sum(p, axis=-1, keepdims=True) + p_own
            pb = p.astype(MX)
            zero = jnp.zeros_like(pb)
            o = _dot(jnp.where(cur_m, pb, zero), vc) + _dot(jnp.where(cur_m, zero, pb), vp)
            o = (o + p_own * jnp.concatenate([vp, vp], axis=0).astype(f32)) / l
            o_ref[pl.ds(r0, BLK), :] = _unstack_heads(o).astype(o_ref.dtype)
            l_ref[pl.ds(r0, BLK), :] = _unstack_heads(jnp.broadcast_to(m + jnp.log(l), (2 * BLK, LANES)))
            return carry

        lax.fori_loop(0, T // BLK, step, 0, unroll=8)

    spec = pl.BlockSpec((T, LANES), lambda j: (0, j))
    return pl.pallas_call(
        body, name=f"attn_fwd{g}", grid=(AO // LANES,),
        in_specs=[spec] * 3, out_specs=[spec] * 2,
        out_shape=[jax.ShapeDtypeStruct((T, AO), ACT), jax.ShapeDtypeStruct((T, AO), f32)],
        compiler_params=_cp(("parallel",), 56),
    )(qf, kf, vf)


def _group_weights(lses):
    m = jnp.maximum(jnp.maximum(lses[0], lses[1]), lses[2])
    e = [jnp.exp(l - m) for l in lses]
    inv = 1.0 / (e[0] + e[1] + e[2])
    return [x * inv for x in e]


def _fold_specs(T, tm):
    specs = []
    for _, d in GROUPS:
        specs.append(pl.BlockSpec((d, tm // d, AO), lambda i: (0, i, 0)))
    return specs


def combine_fwd(os_, lses):
    T = os_[0].shape[0] * os_[0].shape[1]
    tm = TM_FOLD

    def body(o0, o1, o2, l0, l1, l2, y_ref, nat):
        o = [_unfold_in(nat, r, d) for r, (_, d) in zip((o0, o1, o2), GROUPS)]
        ls = [_unfold_in(nat, r, d) for r, (_, d) in zip((l0, l1, l2), GROUPS)]
        w = _group_weights(ls)
        y_ref[...] = (w[0] * o[0] + w[1] * o[1] + w[2] * o[2]).astype(y_ref.dtype)

    specs = _fold_specs(T, tm)
    return pl.pallas_call(
        body, name="combine_fwd", grid=(T // tm,),
        in_specs=specs + specs, out_specs=pl.BlockSpec((tm, AO), lambda i: (i, 0)),
        out_shape=jax.ShapeDtypeStruct((T, AO), MX),
        scratch_shapes=[pltpu.VMEM((AO // LANES, tm, LANES), f32)],
        compiler_params=_cp(("parallel",)),
    )(*os_, *lses)


TM_MIX = 256


def mix_out_fwd(proj, ya, yb, yc, x0, pa, pb, pc, wo, g1, b1):
    T = x0.shape[0]
    tm = min(TM_MIX, T)

    def body(gt, ya_r, yb_r, yc_r, x0_r, pa_r, pb_r, pc_r, wo_r, g_r, b_r, mabc, m_o, r1_o, x1_o, x1b_o):
        ma = _dot(ya_r[...], pa_r[...])
        ybv = yb_r[...]
        mb = jnp.concatenate([_dot(ybv, pb_r[k]) for k in range(NCHIP)], axis=1)
        mc = _dot(yc_r[...], pc_r[...])
        m = jnp.zeros((tm, D), f32)
        for j, mm in enumerate((ma, mb, mc)):
            mabc[:, j * D:(j + 1) * D] = mm.astype(mabc.dtype)
            m = m + _sigmoid(gt[:, j * D:(j + 1) * D].astype(f32)) * mm
        mb16 = m.astype(MX)
        m_o[...] = mb16
        r1 = ALPHA * x0_r[...] + _dot(mb16, wo_r[...])
        r1_o[...] = r1
        xhat, _ = _ln_stats(r1)
        x1 = xhat * g_r[...] + b_r[...]
        x1_o[...] = x1
        x1b_o[...] = x1.astype(MX)

    full = lambda shape: pl.BlockSpec(shape, lambda i: (0,) * len(shape))
    tile = lambda w: pl.BlockSpec((tm, w), lambda i: (i, 0))
    return pl.pallas_call(
        body, name="mix_out_fwd", grid=(T // tm,),
        in_specs=[tile(3 * D), tile(D), tile(AO), tile(D), tile(D), full((D, D)), full((NCHIP, AO, D // NCHIP)),
                  full((D, D)), full((D, D)), full((1, D)), full((1, D))],
        out_specs=[tile(3 * D), tile(D), tile(D), tile(D), tile(D)],
        out_shape=[jax.ShapeDtypeStruct((T, 3 * D), MX), jax.ShapeDtypeStruct((T, D), MX),
                   jax.ShapeDtypeStruct((T, D), f32), jax.ShapeDtypeStruct((T, D), f32), jax.ShapeDtypeStruct((T, D), MX)],
        compiler_params=_cp(("parallel",), 56),
    )(proj, ya, yb, yc, x0, pa, pb, pc, wo, g1, b1)


TM_FF = 512
TM_FFB = 256
ROW_CHUNK = 64


def ffn_up_fwd(x1, wg, wu):
    T = x1.shape[0]
    tm = min(TM_FFB, T)

    def body(x_r, wg_r, wu_r, g_o, u_o, h_o, gs, us):
        xb = x_r[...].astype(MX)
        for k in range(NCHIP):
            gs[...] = _dot(xb, wg_r[k])
            us[...] = _dot(xb, wu_r[k])
            for r in range(0, tm, ROW_CHUNK):
                rows = pl.ds(r, ROW_CHUNK)
                gate, up = gs[rows, :], us[rows, :]
                g_o[k, rows, :] = gate.astype(g_o.dtype)
                u_o[k, rows, :] = up.astype(u_o.dtype)
                h_o[k, rows, :] = (gate * _sigmoid(gate) * up).astype(h_o.dtype)

    wspec = pl.BlockSpec((NCHIP, D, FB), lambda i: (0, 0, 0))
    ospec = pl.BlockSpec((NCHIP, tm, FB), lambda i: (0, i, 0))
    return pl.pallas_call(
        body, name="ffn_up_fwd", grid=(T // tm,),
        in_specs=[pl.BlockSpec((tm, D), lambda i: (i, 0)), wspec, wspec],
        out_specs=[ospec] * 3,
        out_shape=[jax.ShapeDtypeStruct((NCHIP, T, FB), ACT)] * 2 + [jax.ShapeDtypeStruct((NCHIP, T, FB), MX)],
        scratch_shapes=[pltpu.VMEM((tm, FB), f32)] * 2,
        compiler_params=_cp(("parallel",)),
    )(x1, wg, wu)


def ffn_down_fwd(hh, wd, x1, g2, b2):
    T = x1.shape[0]
    tm = min(TM_FF, T)

    def body(h_r, w_r, x_r, g_r, b_r, r2_o, x2_o):
        r2 = ALPHA * x_r[...]
        for k in range(NCHIP):
            r2 = r2 + _dot(h_r[k], w_r[k])
        r2_o[...] = r2
        xhat, _ = _ln_stats(r2)
        x2_o[...] = xhat * g_r[...] + b_r[...]

    tile = pl.BlockSpec((tm, D), lambda i: (i, 0))
    vec = pl.BlockSpec((1, D), lambda i: (0, 0))
    return pl.pallas_call(
        body, name="ffn_down_fwd", grid=(T // tm,),
        in_specs=[pl.BlockSpec((NCHIP, tm, FB), lambda i: (0, i, 0)), pl.BlockSpec((NCHIP, FB, D), lambda i: (0, 0, 0)),
                  tile, vec, vec],
        out_specs=[tile, tile], out_shape=[jax.ShapeDtypeStruct((T, D), f32)] * 2,
        compiler_params=_cp(("parallel",)),
    )(hh, wd, x1, g2, b2)


def loss_grad(y, tgt):
    T = y.shape[0]
    tm = min(512, T)

    def body(y_r, t_r, l_o, dy_o):
        e = y_r[...] - t_r[...]
        dy_o[...] = e * (1.0 / D)

        @pl.when(pl.program_id(0) == 0)
        def _():
            l_o[...] = jnp.zeros_like(l_o)
        l_o[...] += (0.5 / D) * jnp.sum(e * e)

    tile = pl.BlockSpec((tm, D), lambda i: (i, 0))
    return pl.pallas_call(
        body, name="loss_grad", grid=(T // tm,),
        in_specs=[tile, tile], out_specs=[pl.BlockSpec((8, LANES), lambda i: (0, 0)), tile],
        out_shape=[jax.ShapeDtypeStruct((8, LANES), f32), jax.ShapeDtypeStruct((T, D), f32)],
        compiler_params=_cp(("arbitrary",)),
    )(y, tgt)


def ffn_down_bwd(dx2, r2, g2, wd, gate, up):
    T = dx2.shape[0]
    tm = min(TM_FFB, T)

    def body(dx_r, r_r, g_r, w_r, ga_r, up_r, dr_o, drb_o, dg_o, du_o, dlg_o, dlb_o, hs):
        i = pl.program_id(0)
        xhat, rstd = _ln_stats(r_r[...])
        dx = dx_r[...]
        _acc_rows(dlg_o, i == 0, dx * xhat)
        _acc_rows(dlb_o, i == 0, dx)
        dr = _ln_bwd(dx, xhat, rstd, g_r[...])
        dr_o[...] = dr
        drb = dr.astype(MX)
        drb_o[...] = drb
        for k in range(NCHIP):
            hs[...] = _dot(drb, w_r[k], NT)
            for r in range(0, tm, ROW_CHUNK):
                rows = pl.ds(r, ROW_CHUNK)
                dhh, gate_v, up_v = hs[rows, :], ga_r[k, rows, :].astype(f32), up_r[k, rows, :].astype(f32)
                sg = _sigmoid(gate_v)
                dg_o[k, rows, :] = (dhh * up_v * sg * (1.0 + gate_v * (1.0 - sg))).astype(dg_o.dtype)
                du_o[k, rows, :] = (dhh * gate_v * sg).astype(du_o.dtype)

    tile = pl.BlockSpec((tm, D), lambda i: (i, 0))
    vec = pl.BlockSpec((1, D), lambda i: (0, 0))
    blk = pl.BlockSpec((NCHIP, tm, FB), lambda i: (0, i, 0))
    return pl.pallas_call(
        body, name="ffn_down_bwd", grid=(T // tm,),
        in_specs=[tile, tile, vec, pl.BlockSpec((NCHIP, FB, D), lambda i: (0, 0, 0)), blk, blk],
        out_specs=[tile, tile, blk, blk, vec, vec],
        out_shape=[jax.ShapeDtypeStruct((T, D), f32), jax.ShapeDtypeStruct((T, D), MX)]
        + [jax.ShapeDtypeStruct((NCHIP, T, FB), MX)] * 2 + [jax.ShapeDtypeStruct((1, D), f32)] * 2,
        scratch_shapes=[pltpu.VMEM((tm, FB), f32)],
        compiler_params=_cp(("arbitrary",)),
    )(dx2, r2, g2, wd, gate, up)


def ffn_up_bwd(dr2, dgate, dup, wg, wu, r1, g1):
    T = dr2.shape[0]
    tm = min(TM_FFB, T)

    def body(dr2_r, dg_r, du_r, wg_r, wu_r, r1_r, g_r, dr1_o, dr1b_o, dlg_o, dlb_o):
        i = pl.program_id(0)
        dx = ALPHA * dr2_r[...]
        for k in range(NCHIP):
            dx = dx + _dot(dg_r[k], wg_r[k], NT) + _dot(du_r[k], wu_r[k], NT)
        xhat, rstd = _ln_stats(r1_r[...])
        _acc_rows(dlg_o, i == 0, dx * xhat)
        _acc_rows(dlb_o, i == 0, dx)
        dr1 = _ln_bwd(dx, xhat, rstd, g_r[...])
        dr1_o[...] = dr1
        dr1b_o[...] = dr1.astype(MX)

    tile = pl.BlockSpec((tm, D), lambda i: (i, 0))
    vec = pl.BlockSpec((1, D), lambda i: (0, 0))
    blk = pl.BlockSpec((NCHIP, tm, FB), lambda i: (0, i, 0))
    wspec = pl.BlockSpec((NCHIP, D, FB), lambda i: (0, 0, 0))
    return pl.pallas_call(
        body, name="ffn_up_bwd", grid=(T // tm,),
        in_specs=[tile, blk, blk, wspec, wspec, tile, vec],
        out_specs=[tile, tile, vec, vec],
        out_shape=[jax.ShapeDtypeStruct((T, D), f32), jax.ShapeDtypeStruct((T, D), MX)]
        + [jax.ShapeDtypeStruct((1, D), f32)] * 2,
        compiler_params=_cp(("arbitrary",)),
    )(dr2, dgate, dup, wg, wu, r1, g1)


def mix_out_bwd(dr1, proj, mabc, wo, pa, pb, pc):
    T = dr1.shape[0]
    tm = min(TM_MIX, T)

    def body(dr_r, gt, mabc_r, wo_r, pa_r, pb_r, pc_r, dmabc_o, dgt_o, dya_o, dyb_o, dyc_o):
        dm = _dot(dr_r[...].astype(MX), wo_r[...], NT)
        dmx = []
        for j in range(3):
            s = _sigmoid(gt[:, j * D:(j + 1) * D].astype(f32))
            v = (dm * s).astype(MX)
            dmx.append(v)
            dmabc_o[:, j * D:(j + 1) * D] = v
            dgt_o[:, j * D:(j + 1) * D] = (dm * mabc_r[:, j * D:(j + 1) * D].astype(f32) * s * (1.0 - s)).astype(dgt_o.dtype)
        dya_o[...] = _dot(dmx[0], pa_r[...], NT).astype(dya_o.dtype)
        dyb = jnp.zeros((tm, AO), f32)
        for k in range(NCHIP):
            dyb = dyb + _dot(dmx[1][:, k * (D // NCHIP):(k + 1) * (D // NCHIP)], pb_r[k], NT)
        dyb_o[...] = dyb.astype(dyb_o.dtype)
        dyc_o[...] = _dot(dmx[2], pc_r[...], NT).astype(dyc_o.dtype)

    full = lambda shape: pl.BlockSpec(shape, lambda i: (0,) * len(shape))
    tile = lambda w: pl.BlockSpec((tm, w), lambda i: (i, 0))
    return pl.pallas_call(
        body, name="mix_out_bwd", grid=(T // tm,),
        in_specs=[tile(D), tile(3 * D), tile(3 * D), full((D, D)), full((D, D)), full((NCHIP, AO, D // NCHIP)), full((D, D))],
        out_specs=[tile(3 * D), tile(3 * D), tile(D), tile(AO), tile(D)],
        out_shape=[jax.ShapeDtypeStruct((T, 3 * D), MX), jax.ShapeDtypeStruct((T, 3 * D), MX),
                   jax.ShapeDtypeStruct((T, D), ACT), jax.ShapeDtypeStruct((T, AO), ACT), jax.ShapeDtypeStruct((T, D), ACT)],
        compiler_params=_cp(("parallel",), 56),
    )(dr1, proj, mabc, wo, pa, pb, pc)


def transpose_cast(x):
    T = x.shape[0]
    tm = min(512, T)

    def body(x_r, o_r):
        o_r[...] = x_r[...].T.astype(o_r.dtype)

    return pl.pallas_call(
        body, name="transpose_cast", grid=(T // tm,),
        in_specs=[pl.BlockSpec((tm, D), lambda i: (i, 0))], out_specs=pl.BlockSpec((D, tm), lambda i: (0, i)),
        out_shape=jax.ShapeDtypeStruct((D, T), MX), compiler_params=_cp(("parallel",)),
    )(x)


def tn_matmul(name, a, b, a_spec, b_spec, out_shape, out_spec, grid):
    nt = len(grid) - 1

    def body(a_r, b_r, o_r):
        @pl.when(pl.program_id(nt) == 0)
        def _():
            o_r[...] = jnp.zeros_like(o_r)
        av = a_r[...].reshape(a_r.shape[-2:]).astype(MX)
        bv = b_r[...].reshape(b_r.shape[-2:]).astype(MX)
        o_r[...] += _dot(av, bv, TN).reshape(o_r.shape)

    return pl.pallas_call(
        body, name=name, grid=grid, in_specs=[a_spec, b_spec], out_specs=out_spec,
        out_shape=jax.ShapeDtypeStruct(out_shape, f32),
        compiler_params=_cp(("parallel",) * nt + ("arbitrary",), 56),
    )(a, b)


def attn_pre_bwd(dyb, os_, lses, ones):
    T = dyb.shape[0]
    tm = TM_FOLD

    def body(dy_r, o0, o1, o2, l0, l1, l2, ones_r, d0, d1, d2, f0, f1, f2, nat):
        o = [_unfold_in(nat, r, d) for r, (_, d) in zip((o0, o1, o2), GROUPS)]
        ls = [_unfold_in(nat, r, d) for r, (_, d) in zip((l0, l1, l2), GROUPS)]
        w = _group_weights(ls)
        dy = dy_r[...].astype(f32)
        t = dy * (w[0] * o[0] + w[1] * o[1] + w[2] * o[2])
        hi = t.astype(MX)
        lo = (t - hi.astype(f32)).astype(MX)
        c = _dot(hi, ones_r[...]) + _dot(lo, ones_r[...])
        for wg, do_o, df_o, (_, d) in zip(w, (d0, d1, d2), (f0, f1, f2), GROUPS):
            _fold_out(nat, wg * dy, do_o, d)
            _fold_out(nat, -wg * c, df_o, d)

    specs = _fold_specs(T, tm)
    return pl.pallas_call(
        body, name="attn_pre_bwd", grid=(T // tm,),
        in_specs=[pl.BlockSpec((tm, AO), lambda i: (i, 0))] + specs + specs + [pl.BlockSpec((AO, AO), lambda i: (0, 0))],
        out_specs=specs + specs,
        out_shape=[jax.ShapeDtypeStruct((d, T // d, AO), MX) for _, d in GROUPS]
        + [jax.ShapeDtypeStruct((d, T // d, AO), f32) for _, d in GROUPS],
        scratch_shapes=[pltpu.VMEM((AO // LANES, tm, LANES), f32)],
        compiler_params=_cp(("parallel",), 56),
    )(dyb, *os_, *lses, ones)


def _head_ones():
    i = jnp.arange(AO) // HD
    return (i[:, None] == i[None, :]).astype(MX)


BWD_BLOCKS = 8


def attn_bwd(qf, kf, vf, dof, lse, df, g, nb):
    T = qf.shape[0]

    def body(q_ref, k_ref, v_ref, do_ref, l_ref, d_ref, dq_ref, dk_ref, dv_ref):
        prev_m, cur_m = _window_masks()

        def head_col(ref, r0):
            v = ref[pl.ds(r0, BLK), :]
            return jnp.concatenate([v[:, 0:1], v[:, HD:HD + 1]], axis=0)

        def step(b, carry):
            dk_c, dv_c = carry
            r0 = pl.multiple_of(b * BLK, BLK)
            rp = pl.multiple_of(jnp.maximum(b - 1, 0) * BLK, BLK)
            qs, dos = _stack_heads(q_ref[pl.ds(r0, BLK), :]), _stack_heads(do_ref[pl.ds(r0, BLK), :])
            k2, v2 = _two_blocks(k_ref, b), _two_blocks(v_ref, b)
            valid = cur_m | (prev_m & ((b % nb) != 0))
            p = jnp.where(valid, jnp.exp(_dot(qs, k2, NT) - head_col(l_ref, r0)), 0.0)
            ds = (p * (_dot(dos, v2, NT) + head_col(d_ref, r0))).astype(MX)
            dq_ref[pl.ds(r0, BLK), :] = _unstack_heads(_dot(ds, k2)).astype(dq_ref.dtype)
            dk2 = _dot(ds, qs, TN)
            dv2 = _dot(p.astype(MX), dos, TN)
            dk_ref[pl.ds(rp, BLK), :] = (dk_c + dk2[:BLK]).astype(dk_ref.dtype)
            dv_ref[pl.ds(rp, BLK), :] = (dv_c + dv2[:BLK]).astype(dv_ref.dtype)
            return dk2[BLK:], dv2[BLK:]

        zero = jnp.zeros((BLK, LANES), f32)

        def steps(i, carry):
            for j in range(BWD_BLOCKS):
                carry = step(BWD_BLOCKS * i + j, carry)
            return carry

        dk_c, dv_c = lax.fori_loop(0, T // BLK // BWD_BLOCKS, steps, (zero, zero))
        dk_ref[pl.ds(T - BLK, BLK), :] = dk_c.astype(dk_ref.dtype)
        dv_ref[pl.ds(T - BLK, BLK), :] = dv_c.astype(dv_ref.dtype)

    spec = pl.BlockSpec((T, LANES), lambda j: (0, j))
    return pl.pallas_call(
        body, name=f"attn_bwd{g}", grid=(AO // LANES,),
        in_specs=[spec] * 6, out_specs=[spec] * 3,
        out_shape=[jax.ShapeDtypeStruct((T, AO), MX)] * 3,
        compiler_params=_cp(("parallel",), 60),
    )(qf, kf, vf, dof, lse, df)


def unfold_rope_bwd(dqf, dkf, dvf, cos_t, sin_t, g, d):
    T = dqf.shape[0] * dqf.shape[1]
    tm = TM_FOLD

    def body(q_r, k_r, v_r, c_ref, s_ref, o_ref, nat):
        cos, sin = _tile4(c_ref[...]), _tile4(s_ref[...])
        for part, ref, scale in ((0, q_r, HD ** -0.5), (1, k_r, 1.0), (2, v_r, None)):
            x = _unfold_in(nat, ref, d)
            if scale is not None:
                x = (x * cos - _swap_halves(x) * sin) * scale
            o_ref[:, part * AO:(part + 1) * AO] = x.astype(o_ref.dtype)

    fold_spec = pl.BlockSpec((d, tm // d, AO), lambda i: (0, i, 0))
    tab = pl.BlockSpec((tm, LANES), lambda i: (i, 0))
    return pl.pallas_call(
        body, name=f"unfold_rope_bwd{g}", grid=(T // tm,),
        in_specs=[fold_spec] * 3 + [tab, tab],
        out_specs=pl.BlockSpec((tm, 3 * AO), lambda i: (i, 0)),
        out_shape=jax.ShapeDtypeStruct((T, 3 * AO), MX),
        scratch_shapes=[pltpu.VMEM((AO // LANES, tm, LANES), f32)],
        compiler_params=_cp(("parallel",)),
    )(dqf, dkf, dvf, cos_t, sin_t)


CONV_CHUNK = 32


def conv_bwd(dya, proj, conv_w):
    T = dya.shape[0]
    tm = TM_AC
    last = T // tm - 1

    def body(dy_r, bch, hprev, dy_next, b_next, cw, d_o, dw_o, zs, ds):
        i = pl.program_id(0)
        ch = CONV_CHUNK
        hz = hprev[:, :D].astype(f32) * hprev[:, D:].astype(f32)
        zs[0:HALO, :] = jnp.where(i > 0, hz, 0.0)
        ds[tm:tm + HALO, :] = jnp.where(i < last, dy_next[...].astype(f32) * b_next[...].astype(f32), 0.0)
        for r in range(0, tm, ch):
            zs[HALO + r:HALO + r + ch, :] = bch[r:r + ch, D:2 * D].astype(f32) * bch[r:r + ch, 2 * D:].astype(f32)
            ds[r:r + ch, :] = dy_r[r:r + ch, :].astype(f32) * bch[r:r + ch, :D].astype(f32)

        @pl.when(i == 0)
        def _():
            dw_o[...] = jnp.zeros_like(dw_o)

        sums = [jnp.zeros((1, D), f32) for _ in range(3)]
        for r in range(0, tm, ch):
            z2, z1, z = (zs[HALO + r - s:HALO + r - s + ch, :] for s in (2, 1, 0))
            dcv, d1, d2 = (ds[r + s:r + s + ch, :] for s in (0, 1, 2))
            cv = cw[0:1, :] * z2 + cw[1:2, :] * z1 + cw[2:3, :] * z
            dz = cw[2:3, :] * dcv + cw[1:2, :] * d1 + cw[0:1, :] * d2
            d_o[r:r + ch, :D] = (dy_r[r:r + ch, :].astype(f32) * cv).astype(d_o.dtype)
            d_o[r:r + ch, D:2 * D] = (dz * bch[r:r + ch, 2 * D:].astype(f32)).astype(d_o.dtype)
            d_o[r:r + ch, 2 * D:] = (dz * bch[r:r + ch, D:2 * D].astype(f32)).astype(d_o.dtype)
            for k, zz in enumerate((z2, z1, z)):
                sums[k] = sums[k] + jnp.sum(dcv * zz, axis=0, keepdims=True)
        for k in range(3):
            dw_o[k:k + 1, :] += sums[k]

    nh = tm // HALO
    return pl.pallas_call(
        body, name="conv_bwd", grid=(T // tm,),
        in_specs=[pl.BlockSpec((tm, D), lambda i: (i, 0)), pl.BlockSpec((tm, 3 * D), lambda i: (i, 1)),
                  pl.BlockSpec((HALO, 2 * D), lambda i: (jnp.maximum(i * nh - 1, 0), 2)),
                  pl.BlockSpec((HALO, D), lambda i: (jnp.minimum((i + 1) * nh, T // HALO - 1), 0)),
                  pl.BlockSpec((HALO, D), lambda i: (jnp.minimum((i + 1) * nh, T // HALO - 1), 3)),
                  pl.BlockSpec((3, D), lambda i: (0, 0))],
        out_specs=[pl.BlockSpec((tm, 3 * D), lambda i: (i, 0)), pl.BlockSpec((3, D), lambda i: (0, 0))],
        out_shape=[jax.ShapeDtypeStruct((T, 3 * D), MX), jax.ShapeDtypeStruct((3, D), f32)],
        scratch_shapes=[pltpu.VMEM((HALO + tm, D), f32), pltpu.VMEM((tm + HALO, D), f32)],
        compiler_params=_cp(("arbitrary",)),
    )(dya, proj, proj, dya, proj, conv_w)


def gmlp_bwd(dyc, proj, wst, bsx, lg, lb):
    T = dyc.shape[0]
    tm = TM_AC
    last = T // tm - 1

    def body(dy_r, u0, u1, v0, v1, ws, bs, lg_r, lb_r, d_o, dws_o, dbs_o, dlg_o, dlb_o, bacc):
        i = pl.program_id(0)
        up = jnp.concatenate([u0[...], u1[...]], axis=1).astype(f32)
        vp = jnp.concatenate([v0[...], v1[...]], axis=1).astype(f32)
        u, du = _gelu_and_grad(up)
        gv, dgv = _gelu_and_grad(vp)
        u, vn, xhat, rstd, sp = _gmlp_fwd(up, vp, ws, bs, lg_r[...], lb_r[...], u, gv)
        dy = dy_r[...].astype(f32)
        d_o[:, :D] = (dy * sp * du).astype(d_o.dtype)
        dsp = dy * u
        dspb, vnb = dsp.astype(MX), vn.astype(MX)

        @pl.when(i == 0)
        def _():
            dws_o[...] = jnp.zeros_like(dws_o)
            bacc[...] = jnp.zeros_like(bacc)

        rows = []
        for c in range(tm // BLK):
            r = slice(c * BLK, (c + 1) * BLK)
            cols = []
            for g in range(8):
                cs = slice(g * BLK, (g + 1) * BLK)
                dws_o[g] += _dot(dspb[r, cs], vnb[r, cs], NT)
                bacc[g] += dsp[r, cs]
                cols.append(_dot(ws[g], dspb[r, cs], TN))
            rows.append(jnp.concatenate(cols, axis=1))
        dvn = jnp.concatenate(rows, axis=0)
        _acc_rows(dlg_o, i == 0, dvn * xhat)
        _acc_rows(dlb_o, i == 0, dvn)
        d_o[:, D:] = (_ln_bwd(dvn, xhat, rstd, lg_r[...]) * dgv).astype(d_o.dtype)

        @pl.when(i == last)
        def _():
            row = lax.broadcasted_iota(jnp.int32, (BLK, BLK), 0)
            col = lax.broadcasted_iota(jnp.int32, (BLK, BLK), 1)
            ones = jnp.ones((8, BLK), MX)
            for g in range(8):
                dws_o[g] = jnp.where(col <= row, dws_o[g], 0.0)
                a = bacc[g]
                hi = a.astype(MX)
                lo = (a - hi.astype(f32)).astype(MX)
                dbs_o[g:g + 1, :] = (_dot(ones, hi, NT) + _dot(ones, lo, NT))[0:1, :]

    full = lambda shape: pl.BlockSpec(shape, lambda i: (0,) * len(shape))
    return pl.pallas_call(
        body, name="gmlp_bwd", grid=(T // tm,),
        in_specs=[pl.BlockSpec((tm, D), lambda i: (i, 0)), *_uv_specs(), full((8, BLK, BLK)), full((8, BLK, BLK)),
                  full((1, D)), full((1, D))],
        out_specs=[pl.BlockSpec((tm, 2 * D), lambda i: (i, 0)), full((8, BLK, BLK)), full((8, BLK)), full((1, D)), full((1, D))],
        out_shape=[jax.ShapeDtypeStruct((T, 2 * D), MX), jax.ShapeDtypeStruct((8, BLK, BLK), f32),
                   jax.ShapeDtypeStruct((8, BLK), f32), jax.ShapeDtypeStruct((1, D), f32), jax.ShapeDtypeStruct((1, D), f32)],
        scratch_shapes=[pltpu.VMEM((8, BLK, BLK), f32)],
        compiler_params=_cp(("arbitrary",)),
    )(dyc, proj, proj, proj, proj, wst, bsx, lg, lb)


PART_TILES = (6, 6, 3, 3, 3, 4)
PART_START = (0, 6, 12, 15, 18, 21)
TJ = 512


def _part_specs(tm, rows_axis):
    specs = []
    for n, s in zip(PART_TILES, PART_START):
        def imap(*idx, n=n, s=s):
            i, j = idx[rows_axis], idx[1 - rows_axis]
            inside = (j >= s) & (j < s + n)
            return (jnp.where(inside, i, 0), jnp.clip(j - s, 0, n - 1))
        specs.append(pl.BlockSpec((tm, TJ), imap))
    return specs


def _with_part(j, refs, fn):
    for r, n, s in zip(refs, PART_TILES, PART_START):
        @pl.when((j >= s) & (j < s + n))
        def _():
            fn(r[...])


def dx_in(dr1, parts, w, bias):
    T = dr1.shape[0]
    tm = min(2048, T)

    def body(dr_r, p0, p1, p2, p3, p4, p5, w_r, b_r, o_r):
        j = pl.program_id(1)

        @pl.when(j == 0)
        def _():
            o_r[...] = ALPHA * dr_r[...] + b_r[...]

        def acc(tile):
            o_r[...] += _dot(tile, w_r[...], NT)
        _with_part(j, (p0, p1, p2, p3, p4, p5), acc)

    once = dict(pipeline_mode=pl.Buffered(1))
    return pl.pallas_call(
        body, name="dx_in", grid=(T // tm, NIN // TJ),
        in_specs=[pl.BlockSpec((tm, D), lambda i, j: (i, 0), **once)] + _part_specs(tm, 0)
        + [pl.BlockSpec((D, TJ), lambda i, j: (0, j)), pl.BlockSpec((1, D), lambda i, j: (0, 0))],
        out_specs=pl.BlockSpec((tm, D), lambda i, j: (i, 0), **once),
        out_shape=jax.ShapeDtypeStruct((T, D), f32),
        compiler_params=_cp(("parallel", "arbitrary"), 56),
    )(dr1, *parts, w, bias)


def dw_in(x0t, parts):
    T = x0t.shape[1]
    tk = min(2048, T)

    def body(x_r, p0, p1, p2, p3, p4, p5, o_r):
        j, t = pl.program_id(0), pl.program_id(1)

        @pl.when(t == 0)
        def _():
            o_r[...] = jnp.zeros_like(o_r)

        def acc(tile):
            o_r[...] += _dot(x_r[:, pl.ds(pl.multiple_of(t * tk, tk), tk)], tile)
        _with_part(j, (p0, p1, p2, p3, p4, p5), acc)

    return pl.pallas_call(
        body, name="dw_in", grid=(NIN // TJ, T // tk),
        in_specs=[pl.BlockSpec((D, T), lambda j, t: (0, 0), pipeline_mode=pl.Buffered(1))] + _part_specs(tk, 1),
        out_specs=pl.BlockSpec((D, TJ), lambda j, t: (0, j)),
        out_shape=jax.ShapeDtypeStruct((D, NIN), f32),
        compiler_params=_cp(("parallel", "arbitrary"), 56),
    )(x0t, *parts)


def rope_tables(positions):
    half = HD // 2
    inv_freq = ROPE_THETA ** (-jnp.arange(half, dtype=f32) / half)
    ang = positions.astype(f32)[:, None] * inv_freq
    cos, sin = jnp.cos(ang), jnp.sin(ang)
    return jnp.tile(cos, (1, LANES // half)), jnp.tile(jnp.concatenate([-sin, sin], axis=1), (1, LANES // HD))


def _flat(a):
    return a.reshape(a.shape[0] * a.shape[1], a.shape[2])


def layer_fwd(x0, W, cos_t, sin_t):
    T = x0.shape[0]
    proj = mm_in(x0, W["w_in"], W["in_bias"])
    ya, yc = mix_ac_fwd(proj, W["conv_w"], W["wst"], W["bsx"], W["gmlp_ln_g"], W["gmlp_ln_b"])
    folded, os_, lses = [], [], []
    for g, (_, d) in enumerate(GROUPS):
        qf, kf, vf = fold_rope(proj, cos_t, sin_t, g, d)
        o, lse = attn_fwd(_flat(qf), _flat(kf), _flat(vf), g, T // d // BLK)
        folded.append((qf, kf, vf))
        os_.append(o.reshape(d, T // d, AO))
        lses.append(lse.reshape(d, T // d, AO))
    yb = combine_fwd(os_, lses)
    if "late" in W:
        W = {**W, **W["late"](yb)}
    mabc, m, r1, x1, x1b = mix_out_fwd(proj, ya, yb, yc, x0, W["p_a"], W["p_b"], W["p_c"], W["w_o"], W["ln1_g"], W["ln1_b"])
    gate, up, hh = ffn_up_fwd(x1b, W["w_gate"], W["w_up"])
    r2, x2 = ffn_down_fwd(hh, W["w_down"], x1, W["ln2_g"], W["ln2_b"])
    saved = dict(x0=x0, proj=proj, ya=ya, yb=yb, yc=yc, folded=folded, os=os_, lses=lses, mabc=mabc, m=m, r1=r1,
                 x1b=x1b, gate=gate, up=up, hh=hh, r2=r2)
    return x2, saved, W


def layer_bwd(dx2, S, W, cos_t, sin_t, on_grads=None):
    T = dx2.shape[0]
    tk = min(2048, T)
    G = {}
    dr2, dr2b, dgate, dup, G["ln2_g"], G["ln2_b"] = ffn_down_bwd(dx2, S["r2"], W["ln2_g"], W["w_down"], S["gate"], S["up"])
    blk_a = pl.BlockSpec((1, tk, FB), lambda k, t: (k, t, 0))
    row_b = pl.BlockSpec((tk, D), lambda k, t: (t, 0))
    G["w_down"] = tn_matmul("dw_down", S["hh"], dr2b, blk_a, row_b, (NCHIP, FB, D),
                            pl.BlockSpec((1, FB, D), lambda k, t: (k, 0, 0)), (NCHIP, T // tk))
    for nm, dv in (("w_gate", dgate), ("w_up", dup)):
        G[nm] = tn_matmul("d" + nm, dv, S["x1b"], blk_a, row_b, (NCHIP, FB, D),
                          pl.BlockSpec((1, FB, D), lambda k, t: (k, 0, 0)), (NCHIP, T // tk))
    dr1, dr1b, G["ln1_g"], G["ln1_b"] = ffn_up_bwd(dr2, dgate, dup, W["w_gate"], W["w_up"], S["r1"], W["ln1_g"])
    dmabc, dgates, dya, dyb, dyc = mix_out_bwd(dr1b, S["proj"], S["mabc"], W["w_o"], W["p_a"], W["p_b"], W["p_c"])
    one = (1, T // tk)
    full_o = pl.BlockSpec((D, D), lambda k, t: (0, 0))
    G["w_o"] = tn_matmul("dw_o", S["m"], dr1b, row_b, row_b, (D, D), full_o, one)
    G["p_a"] = tn_matmul("dp_a", S["ya"], dmabc, row_b, pl.BlockSpec((tk, D), lambda k, t: (t, 0)), (D, D), full_o, one)
    G["p_c"] = tn_matmul("dp_c", S["yc"], dmabc, row_b, pl.BlockSpec((tk, D), lambda k, t: (t, 2)), (D, D), full_o, one)
    G["p_b"] = tn_matmul("dp_b", S["yb"], dmabc, pl.BlockSpec((tk, AO), lambda k, t: (t, 0)),
                         pl.BlockSpec((tk, D // NCHIP), lambda k, t: (t, NCHIP + k)), (NCHIP, AO, D // NCHIP),
                         pl.BlockSpec((1, AO, D // NCHIP), lambda k, t: (k, 0, 0)), (NCHIP, T // tk))
    conv_w = W["conv_w"]
    if on_grads is not None:
        conv_w = conv_w + on_grads({n: G[n] for n in BIG if n != "w_in"})
    dbch, G["conv_w"] = conv_bwd(dya, S["proj"], conv_w)
    duv, G["w_s"], G["b_s"], G["gmlp_ln_g"], G["gmlp_ln_b"] = gmlp_bwd(
        dyc, S["proj"], W["wst"], W["bsx"], W["gmlp_ln_g"], W["gmlp_ln_b"])
    ones = _head_ones()
    if on_grads is not None:
        small = {n: G[n] for n in VECS + ("b_s", "w_s", "conv_w")}
        ones = ones + on_grads(small).astype(MX)
    pre = attn_pre_bwd(dyb, S["os"], S["lses"], ones)
    dqkv = []
    for g, (_, d) in enumerate(GROUPS):
        qf, kf, vf = S["folded"][g]
        dqf, dkf, dvf = attn_bwd(_flat(qf), _flat(kf), _flat(vf), _flat(pre[g]), _flat(S["lses"][g]), _flat(pre[3 + g]),
                                 g, T // d // BLK)
        shp = (d, T // d, AO)
        dqkv.append(unfold_rope_bwd(dqf.reshape(shp), dkf.reshape(shp), dvf.reshape(shp), cos_t, sin_t, g, d))
    parts = (dgates, dbch, *dqkv, duv)
    G["w_in"] = dw_in(transpose_cast(S["x0"]), parts)
    bias = jnp.zeros((1, D), f32)
    if on_grads is not None:
        bias = bias + on_grads({"w_in": G["w_in"]})
    dx0 = dx_in(dr1, parts, W["w_in"], bias)
    started = on_grads({"dx": dx0}) if on_grads is not None else None
    return dx0, G, started


def prep_layer_weights(Wl):
    W = dict(Wl)
    tril = jnp.tril(jnp.ones((BLK, BLK), f32))
    W["wst"] = (Wl["w_s"] * tril[None]).astype(MX)
    W["bsx"] = jnp.broadcast_to(Wl["b_s"][:, :, None], (8, BLK, BLK))
    for n in ("gmlp_ln_g", "gmlp_ln_b", "ln1_g", "ln1_b", "ln2_g", "ln2_b"):
        W[n] = Wl[n].reshape(1, D)
    W["in_bias"] = jnp.zeros((1, NIN), f32) + Wl.get("after", 0.0)
    return W


def local_step(x, positions, target, layers, on_grads=None):
    cos_t, sin_t = rope_tables(positions)
    Ws, saved = [], []
    h = x
    for Wl in layers:
        h, S, W = layer_fwd(h, prep_layer_weights(Wl(h) if callable(Wl) else Wl), cos_t, sin_t)
        Ws.append(W)
        saved.append(S)
    lsum, dh = loss_grad(h, target)
    if on_grads is not None:
        on_grads(len(Ws), {"loss": lsum})
    grads = [None] * len(Ws)
    started = None
    for l in reversed(range(len(Ws))):
        W = Ws[l]
        if started is not None:
            W = dict(W, ln2_g=W["ln2_g"] + started)
        hook = functools.partial(on_grads, l) if on_grads is not None else None
        dh, grads[l], started = layer_bwd(dh, saved[l], W, cos_t, sin_t, hook)
    return lsum, dh, grads


MESH = pl.DeviceIdType.MESH
ANY = pl.BlockSpec(memory_space=pl.ANY)
BIG = ("w_in", "w_gate", "w_up", "w_down", "p_a", "p_b", "p_c", "w_o")
NBIG = len(BIG)


def _place():
    x, y, c = lax.axis_index("x"), lax.axis_index("y"), lax.axis_index("c")
    return x, y, c, 2 * x + y


def _rcopy(src, dst, send, recv, dev):
    return pltpu.make_async_remote_copy(src_ref=src, dst_ref=dst, send_sem=send, recv_sem=recv, device_id=dev,
                                        device_id_type=MESH)


def _cols(ref, k, width):
    start = k * width if isinstance(k, int) else pl.multiple_of(k * width, LANES)
    return ref.at[:, pl.ds(start, width)]


CHUNK_BYTES = 1 << 20


def _pieces(shape, itemsize, nbytes=CHUNK_BYTES):
    rows, cols = shape[-2], shape[-1]
    per = max(16, nbytes // (cols * itemsize) // 16 * 16)
    out = []
    for lead in (range(shape[0]) if len(shape) == 3 else (None,)):
        for r in range(0, rows, per):
            sl = (pl.ds(r, min(per, rows - r)), slice(None))
            out.append(sl if lead is None else (lead,) + sl)
    return out


def _start_pieces(src, dst, make, nbytes=CHUNK_BYTES):
    for idx in _pieces(src.shape, jnp.dtype(src.dtype).itemsize, nbytes):
        make(src.at[idx], dst.at[idx]).start()


def gather_halves(shards):
    n = len(shards)

    def body(*refs):
        srcs, dsts = refs[:n], refs[n:2 * n]
        send, recv, own_send, own_recv = refs[2 * n:]
        x, y, c, k = _place()
        sib = (x, y, 1 - c)
        chips = [(1 - x, y), (x, 1 - y), (1 - x, 1 - y)]

        def slot(a, layer, pos):
            if a == 0:
                return _cols(dsts[0].at[layer], pos, WIN_SHARD)
            return dsts[a].at[pos, layer]

        def ici(a, j, src, dst):
            return _rcopy(src, dst, send.at[a, j], recv.at[a, j], (*chips[j], c))

        def d2d(a, j, src, dst):
            return _rcopy(src, dst, send.at[a, 3 + j], recv.at[a, 3 + j], sib)

        def own(a, layer, src, dst):
            return _rcopy(src, dst, own_send.at[a, layer], own_recv.at[a, layer], sib)

        for a in range(n):
            for j in range(3):
                _start_pieces(srcs[a].at[c], slot(a, c, k), functools.partial(ici, a, j))
        for a in range(n):
            for layer in range(DEPTH):
                _start_pieces(srcs[a].at[layer], slot(a, layer, k), functools.partial(own, a, layer))
        for a in range(n):
            for j, (cx, cy) in enumerate(chips):
                landed = slot(a, c, 2 * cx + cy)
                ici(a, j, landed, landed).wait_recv()
                _start_pieces(landed, landed, functools.partial(d2d, a, j))
        for a in range(n):
            for j, (cx, cy) in enumerate(chips):
                passed = slot(a, 1 - c, 2 * cx + cy)
                d2d(a, j, passed, passed).wait_recv()
                landed = slot(a, c, 2 * cx + cy)
                d2d(a, j, landed, landed).wait_send()
                ici(a, j, srcs[a].at[c], slot(a, c, k)).wait_send()
            for layer in range(DEPTH):
                own(a, layer, srcs[a].at[layer], slot(a, layer, k)).wait()

    outs = [jax.ShapeDtypeStruct((2, shards[0].shape[1], NIN), shards[0].dtype)]
    outs += [jax.ShapeDtypeStruct((NCHIP,) + s.shape, s.dtype) for s in shards[1:]]
    return pl.pallas_call(
        body, name="gather_halves", in_specs=[ANY] * n, out_specs=[ANY] * n, out_shape=outs,
        scratch_shapes=[pltpu.SemaphoreType.DMA((n, 6)), pltpu.SemaphoreType.DMA((n, 6)),
                        pltpu.SemaphoreType.DMA((n, DEPTH)), pltpu.SemaphoreType.DMA((n, DEPTH))],
    )(*shards)


def _gather_slot(dst, pos):
    return _cols(dst, pos, WIN_SHARD) if len(dst.shape) == 2 else dst.at[pos]


def _gather_copy(a, j, src, dst, send, recv, dev):
    return _rcopy(src, dst, send.at[a * NCHIP + j], recv.at[a * NCHIP + j], dev)


def gather_start(tag, shards, after):
    n = len(shards)

    def body(*refs):
        srcs, dsts = refs[:n], refs[n:2 * n]
        send, recv = refs[2 * n + len(after)], refs[2 * n + len(after) + 1]
        token = refs[-1]
        x, y, c, k = _place()
        peers = [(1 - x, y, c), (x, 1 - y, c), (1 - x, 1 - y, c), (x, y, 1 - c)]
        for a in range(n):
            for j, dev in enumerate(peers):
                _start_pieces(srcs[a], _gather_slot(dsts[a], k),
                              lambda s, d, a=a, j=j, dev=dev: _gather_copy(a, j, s, d, send, recv, dev))
        token[...] = jnp.zeros_like(token)

    gathered = [lax.empty((D, NIN) if s.shape == (D, WIN_SHARD) else (NCHIP,) + s.shape, s.dtype) for s in shards]
    ops = [pltpu.with_memory_space_constraint(v, pltpu.HBM) for v in list(shards) + gathered]
    sem = pltpu.SemaphoreType.DMA((n * NCHIP,))
    res = pl.pallas_call(
        body, name=f"gather_start{tag}", in_specs=[HBM] * (2 * n) + [ANY] * len(after),
        out_specs=[SEMS, SEMS] + [HBM] * (2 * n) + [pl.BlockSpec(memory_space=pltpu.VMEM)],
        out_shape=[sem, sem] + [pltpu.HBM(v.shape, v.dtype) for v in ops] + [jax.ShapeDtypeStruct((8, LANES), f32)],
        input_output_aliases={i: 2 + i for i in range(2 * n)},
        compiler_params=pltpu.CompilerParams(has_side_effects=EFFECT),
    )(*ops, *after)
    return res[0], res[1], res[2:2 + n], res[2 + n:2 + 2 * n], res[-1]


def gather_wait(tag, send, recv, shards, gathered, after):
    n = len(shards)

    def body(*refs):
        srcs, dsts = refs[:n], refs[n:2 * n]
        send_r, recv_r = refs[2 * n], refs[2 * n + 1]
        x, y, c, k = _place()
        peers = [(1 - x, y, c), (x, 1 - y, c), (1 - x, 1 - y, c), (x, y, 1 - c)]
        for a in range(n):
            for j, dev in enumerate(peers):
                _gather_copy(a, j, srcs[a], _gather_slot(dsts[a], k), send_r, recv_r, dev).wait_send()
                pos = 2 * dev[0] + dev[1]
                _gather_copy(a, j, srcs[a], _gather_slot(dsts[a], pos), send_r, recv_r, dev).wait_recv()

    ops = list(shards) + list(gathered)
    res = pl.pallas_call(
        body, name=f"gather_wait{tag}", in_specs=[HBM] * (2 * n) + [SEMS, SEMS] + [ANY] * len(after),
        out_specs=[HBM] * (2 * n), out_shape=[pltpu.HBM(v.shape, v.dtype) for v in ops],
        input_output_aliases={i: i for i in range(2 * n)},
        compiler_params=pltpu.CompilerParams(has_side_effects=EFFECT),
    )(*ops, send, recv, *after)
    return res[n:]


def _half(ref, h):
    rows = ref.shape[-2] // 2
    start = pl.multiple_of(h * rows, 16)
    if len(ref.shape) == 2:
        return ref.at[pl.ds(start, rows), :]
    return ref.at[:, pl.ds(start, rows), :]


HBM = pl.BlockSpec(memory_space=pltpu.HBM)
SEMS = pl.BlockSpec(memory_space=pltpu.SEMAPHORE)
EFFECT = pltpu.SideEffectType.DATAFLOW_SIDE_EFFECTING


def rs_pair_start(tag, grads, halves=True):
    n = len(grads)

    def body(*refs):
        g, theirs = refs[:n], refs[n:2 * n]
        send, recv = refs[2 * n], refs[2 * n + 1]
        x, y, c, _ = _place()
        for a in range(n):
            _start_pieces(_half(g[a], 1 - c) if halves else g[a], theirs[a],
                          lambda s, d, a=a: _rcopy(s, d, send.at[a], recv.at[a], (x, y, 1 - c)))
        refs[-1][...] = jnp.zeros_like(refs[-1])

    lands = [lax.empty(g.shape[:-2] + (g.shape[-2] // 2 if halves else g.shape[-2], g.shape[-1]), g.dtype) for g in grads]
    ops = [pltpu.with_memory_space_constraint(v, pltpu.HBM) for v in list(grads) + lands]
    sem = pltpu.SemaphoreType.DMA((n,))
    res = pl.pallas_call(
        body, name=f"rs_pair_start{tag}", in_specs=[HBM] * (2 * n),
        out_specs=[SEMS, SEMS] + [HBM] * (2 * n) + [pl.BlockSpec(memory_space=pltpu.VMEM)],
        out_shape=[sem, sem] + [pltpu.HBM(v.shape, v.dtype) for v in ops] + [jax.ShapeDtypeStruct((8, LANES), f32)],
        input_output_aliases={i: 2 + i for i in range(2 * n)},
        compiler_params=pltpu.CompilerParams(has_side_effects=EFFECT),
    )(*ops)
    return res[0], res[1], res[2:2 + n], res[2 + n:2 + 2 * n], res[-1]


def rs_pair_wait(tag, send, recv, grads, theirs, after, halves=True):
    n = len(grads)

    def body(*refs):
        g, land = refs[:n], refs[n:2 * n]
        send_r, recv_r = refs[2 * n], refs[2 * n + 1]
        x, y, c, _ = _place()
        for a in range(n):
            cp = _rcopy(_half(g[a], 1 - c) if halves else g[a], land[a], send_r.at[a], recv_r.at[a], (x, y, 1 - c))
            cp.wait_send()
            cp.wait_recv()

    ops = list(grads) + list(theirs)
    res = pl.pallas_call(
        body, name=f"rs_pair_wait{tag}", in_specs=[HBM] * (2 * n) + [SEMS, SEMS] + [ANY] * len(after),
        out_specs=[HBM] * (2 * n), out_shape=[pltpu.HBM(v.shape, v.dtype) for v in ops],
        input_output_aliases={i: i for i in range(2 * n)},
        compiler_params=pltpu.CompilerParams(has_side_effects=EFFECT),
    )(*ops, send, recv, *after)
    return res[:n], res[n:]


def _chip_piece(ref, k):
    return _cols(ref, k, WIN_SHARD) if len(ref.shape) == 2 else ref.at[k]


def _chip_copy(a, k, src, dst, send, recv, me, c):
    return _rcopy(src, dst, send.at[a * NCHIP + k], recv.at[a * NCHIP + me], (k // 2, k % 2, c))


def rs_chips_start(tag, sums):
    n = len(sums)

    def pshape(s):
        return (NCHIP, s[0], WIN_SHARD) if len(s) == 2 else s

    def body(*refs):
        s, land = refs[:n], refs[n:2 * n]
        send, recv = refs[2 * n], refs[2 * n + 1]
        token = refs[-1]
        x, y, c, me = _place()
        for k in range(NCHIP):
            @pl.when(me != k)
            def _():
                for a in range(n):
                    _start_pieces(_chip_piece(s[a], k), land[a].at[me],
                                  lambda src, dst, a=a: _chip_copy(a, k, src, dst, send, recv, me, c))
        token[...] = jnp.zeros_like(token)

    lands = [lax.empty(pshape(v.shape), v.dtype) for v in sums]
    ops = [pltpu.with_memory_space_constraint(v, pltpu.HBM) for v in list(sums) + lands]
    sem = pltpu.SemaphoreType.DMA((n * NCHIP,))
    res = pl.pallas_call(
        body, name=f"rs_chips_start{tag}", in_specs=[HBM] * (2 * n),
        out_specs=[SEMS, SEMS] + [HBM] * (2 * n) + [pl.BlockSpec(memory_space=pltpu.VMEM)],
        out_shape=[sem, sem] + [pltpu.HBM(v.shape, v.dtype) for v in ops] + [jax.ShapeDtypeStruct((8, LANES), f32)],
        input_output_aliases={i: 2 + i for i in range(2 * n)},
        compiler_params=pltpu.CompilerParams(has_side_effects=EFFECT),
    )(*ops)
    return res[0], res[1], res[2:2 + n], res[2 + n:2 + 2 * n], res[-1]


def rs_chips_wait(tag, send, recv, sums, lands, after):
    n = len(sums)

    def body(*refs):
        s, land = refs[:n], refs[n:2 * n]
        send_r, recv_r = refs[2 * n], refs[2 * n + 1]
        x, y, c, me = _place()
        for k in range(NCHIP):
            @pl.when(me != k)
            def _():
                for a in range(n):
                    piece = _chip_piece(s[a], k)
                    _chip_copy(a, k, piece, land[a].at[me], send_r, recv_r, me, c).wait_send()
                    _rcopy(piece, land[a].at[k], send_r.at[a * NCHIP + k], recv_r.at[a * NCHIP + k],
                           (k // 2, k % 2, c)).wait_recv()

    ops = list(sums) + list(lands)
    res = pl.pallas_call(
        body, name=f"rs_chips_wait{tag}", in_specs=[HBM] * (2 * n) + [SEMS, SEMS] + [ANY] * len(after),
        out_specs=[HBM] * (2 * n), out_shape=[pltpu.HBM(v.shape, v.dtype) for v in ops],
        input_output_aliases={i: i for i in range(2 * n)},
        compiler_params=pltpu.CompilerParams(has_side_effects=EFFECT),
    )(*ops, send, recv, *after)
    return res[:n], res[n:]


def _row_tile(rows, cols, itemsize=4, target=2 << 20):
    best = 8
    for t in range(8, rows + 1, 8):
        if rows % t == 0 and t * cols * itemsize <= target:
            best = t
    return best


GRAD_WIRE = jnp.bfloat16


def add_half(name, g, t, c):
    cols, half = t.shape[-1], t.shape[-2]
    nblk = 1 if t.ndim == 2 else t.shape[0]
    tr = _row_tile(half, cols)
    per = half // tr

    def body(c_ref, g_r, t_r, o_r):
        o_r[...] = (g_r[...] + t_r[...]).astype(o_r.dtype)

    tile_t = pl.BlockSpec((tr, cols), lambda i, c_ref: (i, 0))
    tile_g = pl.BlockSpec((tr, cols), lambda i, c_ref: ((i // per) * 2 * per + c_ref[0] * per + i % per, 0))
    out = pl.pallas_call(
        body, name=name, out_shape=jax.ShapeDtypeStruct((nblk * half, cols), GRAD_WIRE),
        grid_spec=pltpu.PrefetchScalarGridSpec(num_scalar_prefetch=1, grid=(nblk * per,), in_specs=[tile_g, tile_t],
                                               out_specs=tile_t),
        compiler_params=_cp(("parallel",)),
    )(c.reshape(1).astype(jnp.int32), g.reshape(nblk * 2 * half, cols), t.reshape(nblk * half, cols))
    return out.reshape(t.shape)


def add_chips(name, land, own):
    _, rows, cols = land.shape
    tr = _row_tile(rows, cols, target=1 << 20)

    def body(land_r, own_r, o_r):
        me = 2 * lax.axis_index("x") + lax.axis_index("y")
        for k in range(NCHIP):
            @pl.when(me == k)
            def _():
                acc = None
                for j in range(NCHIP):
                    t = (own_r[...] if j == k else land_r[j]).astype(f32)
                    acc = t if acc is None else acc + t
                o_r[...] = acc

    tile = pl.BlockSpec((tr, cols), lambda i: (i, 0))
    return pl.pallas_call(
        body, name=name, grid=(rows // tr,), in_specs=[pl.BlockSpec((NCHIP, tr, cols), lambda i: (0, i, 0)), tile],
        out_specs=tile, out_shape=jax.ShapeDtypeStruct((rows, cols), f32), compiler_params=_cp(("parallel",)),
    )(land, own)


def reduce_scatter_pair(tag, G):
    names = tuple(G)
    grads = [G[n] if G[n].ndim == 3 or n == "w_in" else G[n].reshape(NCHIP, D // NCHIP, D) for n in names]
    send, recv, grads, theirs, token = rs_pair_start(tag, grads)
    return (tag, names, send, recv, grads, theirs), token[0, 0]


def reduce_scatter_chips(state, after):
    c = lax.axis_index("c")
    tag, names, send, recv, grads, theirs = state
    grads, theirs = rs_pair_wait(tag, send, recv, grads, theirs, after)
    sums = [add_half(f"rs_add_pair{tag}_{n}", g, t, c) for n, g, t in zip(names, grads, theirs)]
    send, recv, sums, lands, token = rs_chips_start(tag, sums)
    return (tag, names, send, recv, sums, lands), token[0, 0]


def reduce_scatter_finish(state, after):
    me = 2 * lax.axis_index("x") + lax.axis_index("y")
    tag, names, send, recv, sums, lands = state
    sums, landed = rs_chips_wait(tag, send, recv, sums, lands, after)
    halves = []
    for n, s, v in zip(names, sums, landed):
        own = lax.dynamic_slice_in_dim(s, me * WIN_SHARD, WIN_SHARD, axis=1) if s.ndim == 2 else \
            lax.dynamic_index_in_dim(s, me, 0, keepdims=False)
        halves.append(add_chips(f"rs_add_chips{tag}_{n}", v, own))
    send, recv, halves, others, _ = rs_pair_start("_join" + tag, halves, halves=False)
    return tag, names, send, recv, halves, others


def reduce_scatter_join(state, after):
    tag, names, send, recv, halves, others = state
    halves, others = rs_pair_wait("_join" + tag, send, recv, halves, others, after, halves=False)
    return dict(zip(names, zip(halves, others)))


NDEV = 8


def _small_copy(r, src, dst, send, recv, x, y, c):
    return _rcopy(src, dst, send.at[r - 1], recv.at[r - 1], (x ^ (r >> 2), y ^ ((r >> 1) & 1), c ^ (r & 1)))


def small_start(pack):
    def body(p, land, send, recv, p_thru, land_thru, token):
        x, y, c, _ = _place()
        me = 4 * x + 2 * y + c
        for r in range(1, NDEV):
            _start_pieces(p, land.at[me], lambda s, d, r=r: _small_copy(r, s, d, send, recv, x, y, c), 128 << 10)
        token[...] = jnp.zeros_like(token)

    ops = [pltpu.with_memory_space_constraint(v, pltpu.HBM) for v in (pack, lax.empty((NDEV,) + pack.shape, f32))]
    sem = pltpu.SemaphoreType.DMA((NDEV - 1,))
    return pl.pallas_call(
        body, name="small_start", in_specs=[HBM, HBM],
        out_specs=[SEMS, SEMS, HBM, HBM, pl.BlockSpec(memory_space=pltpu.VMEM)],
        out_shape=[sem, sem] + [pltpu.HBM(v.shape, v.dtype) for v in ops] + [jax.ShapeDtypeStruct((8, LANES), f32)],
        input_output_aliases={0: 2, 1: 3}, compiler_params=pltpu.CompilerParams(has_side_effects=EFFECT),
    )(*ops)


def small_wait(send, recv, pack, land, after):
    def body(p, land_r, send_r, recv_r, *rest):
        x, y, c, _ = _place()
        me = 4 * x + 2 * y + c
        for r in range(1, NDEV):
            _small_copy(r, p, land_r.at[me], send_r, recv_r, x, y, c).wait_send()
            src = 4 * (x ^ (r >> 2)) + 2 * (y ^ ((r >> 1) & 1)) + (c ^ (r & 1))
            _small_copy(r, p, land_r.at[src], send_r, recv_r, x, y, c).wait_recv()

    return pl.pallas_call(
        body, name="small_wait", in_specs=[HBM, HBM, SEMS, SEMS] + [ANY] * len(after), out_specs=[HBM, HBM],
        out_shape=[pltpu.HBM(pack.shape, f32), pltpu.HBM(land.shape, f32)], input_output_aliases={0: 0, 1: 1},
        compiler_params=pltpu.CompilerParams(has_side_effects=EFFECT),
    )(pack, land, send, recv, *after)


def small_sum(land, pack):
    def body(land_r, p_r, o_r):
        me = 4 * lax.axis_index("x") + 2 * lax.axis_index("y") + lax.axis_index("c")
        for k in range(NDEV):
            @pl.when(me == k)
            def _():
                acc = None
                for d in range(NDEV):
                    t = p_r[...] if d == k else land_r[d]
                    acc = t if acc is None else acc + t
                o_r[...] = acc

    vm = pl.BlockSpec(memory_space=pltpu.VMEM)
    return pl.pallas_call(
        body, name="small_sum", in_specs=[vm, vm], out_specs=vm, out_shape=jax.ShapeDtypeStruct(pack.shape, f32),
        compiler_params=pltpu.CompilerParams(vmem_limit_bytes=40 << 20),
    )(land, pack)


def _adamw_math(w, g, m, v):
    m = ADAM_B1 * m + (1.0 - ADAM_B1) * g
    v = ADAM_B2 * v + (1.0 - ADAM_B2) * (g * g)
    m_hat = m / (1.0 - ADAM_B1 ** ADAM_STEP)
    v_hat = v / (1.0 - ADAM_B2 ** ADAM_STEP)
    return -ADAM_LR * (m_hat / (jnp.sqrt(v_hat) + ADAM_EPS) + ADAM_WD * w), m, v


def adamw_big(name, halves, w, m, v):
    _, R, C = w.shape
    tr = _row_tile(R // 2, C, target=1 << 20)
    nt = R // 2 // tr

    def body(a0, b0, a1, b1, w_r, m_r, v_r, g_o, d_o, m_o, v_o):
        mine = pl.program_id(1) == lax.axis_index("c")
        g = jnp.where(pl.program_id(0) == 0, jnp.where(mine, a0[...], b0[...]), jnp.where(mine, a1[...], b1[...]))
        g_o[...] = g
        d_o[...], m_o[...], v_o[...] = _adamw_math(w_r[...], g, m_r[...], v_r[...])

    stk = pl.BlockSpec((None, tr, C), lambda l, h, i: (l, h * nt + i, 0))
    lay0 = pl.BlockSpec((tr, C), lambda l, h, i: (jnp.where(l == 0, i, nt - 1), 0))
    lay1 = pl.BlockSpec((tr, C), lambda l, h, i: (jnp.where(l == 0, 0, i), 0))
    return pl.pallas_call(
        body, name=name, grid=(DEPTH, 2, nt),
        in_specs=[lay0, lay0, lay1, lay1, stk, stk, stk],
        out_specs=[stk] * 4, out_shape=[jax.ShapeDtypeStruct(w.shape, f32)] * 4,
        compiler_params=_cp(("arbitrary", "arbitrary", "arbitrary")),
    )(*halves[0], *halves[1], w, m, v)


def adamw_small(name, g, w, m, v):
    def body(g_r, w_r, m_r, v_r, d_o, m_o, v_o):
        d_o[...], m_o[...], v_o[...] = _adamw_math(w_r[...], g_r[...], m_r[...], v_r[...])

    return pl.pallas_call(body, name=name, out_shape=[jax.ShapeDtypeStruct(w.shape, f32)] * 3)(g, w, m, v)


WEIGHTS = ("w_in", "conv_w", "gmlp_ln_g", "gmlp_ln_b", "w_s", "b_s", "p_a", "p_b", "p_c", "w_o", "ln1_g", "ln1_b",
           "w_gate", "w_up", "w_down", "ln2_g", "ln2_b")
VECS = ("ln1_g", "ln1_b", "ln2_g", "ln2_b", "gmlp_ln_g", "gmlp_ln_b")
ROWS_VEC, ROWS_BS, ROWS_WS, ROWS_CONV = D // LANES, 8, 8 * BLK, 3 * D // LANES
ROWS_LAYER = len(VECS) * ROWS_VEC + ROWS_BS + ROWS_WS + ROWS_CONV


def _pack_small(per_layer, tail):
    parts = []
    for P in per_layer:
        parts += [P[n].reshape(ROWS_VEC, LANES) for n in VECS]
        parts += [P["b_s"].reshape(ROWS_BS, LANES), P["w_s"].reshape(ROWS_WS, LANES), P["conv_w"].reshape(ROWS_CONV, LANES)]
    return jnp.concatenate(parts + [tail], axis=0)


def _unpack_small(pack):
    out = []
    for l in range(DEPTH):
        r = l * ROWS_LAYER
        P = {}
        for n in VECS:
            P[n] = pack[r:r + ROWS_VEC].reshape(D)
            r += ROWS_VEC
        P["b_s"] = pack[r:r + ROWS_BS].reshape(8, BLK)
        r += ROWS_BS
        P["w_s"] = pack[r:r + ROWS_WS].reshape(8, BLK, BLK)
        r += ROWS_WS
        P["conv_w"] = pack[r:r + ROWS_CONV].reshape(3, D)
        out.append(P)
    return out, pack[DEPTH * ROWS_LAYER:]


def kernel(x, positions, w_in, conv_w, gmlp_ln_g, gmlp_ln_b, w_s, b_s, p_a, p_b, p_c, w_o, ln1_g, ln1_b, w_gate, w_up, w_down, ln2_g, ln2_b, loss_target, m_w_in, m_conv_w, m_gmlp_ln_g, m_gmlp_ln_b, m_w_s, m_b_s, m_p_a, m_p_b, m_p_c, m_w_o, m_ln1_g, m_ln1_b, m_w_gate, m_w_up, m_w_down, m_ln2_g, m_ln2_b, v_w_in, v_conv_w, v_gmlp_ln_g, v_gmlp_ln_b, v_w_s, v_b_s, v_p_a, v_p_b, v_p_c, v_w_o, v_ln1_g, v_ln1_b, v_w_gate, v_w_up, v_w_down, v_ln2_g, v_ln2_b):
    Wt = dict(w_in=w_in, conv_w=conv_w, gmlp_ln_g=gmlp_ln_g, gmlp_ln_b=gmlp_ln_b, w_s=w_s, b_s=b_s, p_a=p_a, p_b=p_b,
              p_c=p_c, w_o=w_o, ln1_g=ln1_g, ln1_b=ln1_b, w_gate=w_gate, w_up=w_up, w_down=w_down, ln2_g=ln2_g, ln2_b=ln2_b)
    Mt = dict(w_in=m_w_in, conv_w=m_conv_w, gmlp_ln_g=m_gmlp_ln_g, gmlp_ln_b=m_gmlp_ln_b, w_s=m_w_s, b_s=m_b_s, p_a=m_p_a,
              p_b=m_p_b, p_c=m_p_c, w_o=m_w_o, ln1_g=m_ln1_g, ln1_b=m_ln1_b, w_gate=m_w_gate, w_up=m_w_up,
              w_down=m_w_down, ln2_g=m_ln2_g, ln2_b=m_ln2_b)
    Vt = dict(w_in=v_w_in, conv_w=v_conv_w, gmlp_ln_g=v_gmlp_ln_g, gmlp_ln_b=v_gmlp_ln_b, w_s=v_w_s, b_s=v_b_s, p_a=v_p_a,
              p_b=v_p_b, p_c=v_p_c, w_o=v_w_o, ln1_g=v_ln1_g, ln1_b=v_ln1_b, w_gate=v_w_gate, w_up=v_w_up,
              w_down=v_w_down, ln2_g=v_ln2_g, ln2_b=v_ln2_b)
    chip = 2 * lax.axis_index("x") + lax.axis_index("y")
    cw = D // NCHIP

    def gathered_weights(names, arrays):
        Wl = dict(zip(names, arrays))
        for n in ("p_a", "p_c", "w_o"):
            Wl[n] = Wl[n].reshape(D, D)
        return Wl

    def small_weights(l, conv_all):
        Wl = {n: Wt[n][l] for n in VECS + ("w_s", "b_s")}
        Wl["conv_w"] = conv_all[:, l].transpose(1, 0, 2).reshape(3, D)
        return Wl

    w_in0, conv_all = gather_halves([Wt["w_in"][0].astype(MX).reshape(2, D // 2, WIN_SHARD), conv_w])
    rest = BIG[1:]
    *late0, coming0 = gather_start("0", [Wt[n][0].astype(MX) for n in rest], [conv_all])
    *late1, coming1 = gather_start("1", [Wt[n][1].astype(MX) for n in BIG], [conv_all, coming0])
    W0 = dict(small_weights(0, conv_all), w_in=w_in0.reshape(D, NIN), after=coming1[0, 0],
              late=lambda y: gathered_weights(rest, gather_wait("0", *late0, [y])))

    def W1(h):
        return dict(small_weights(1, conv_all), **gathered_weights(BIG, gather_wait("1", *late1, [h])))

    layers = [W0, W1]

    rs_state, rs_started, held = {}, {}, {}

    def start_exchange(l, g):
        if "loss" in g:
            held[l] = g
            return None
        if "conv_w" in g:
            held[l] = g
            rs_state[(l, False)], started = reduce_scatter_chips(rs_state[(l, False)], [g["w_s"], g["conv_w"]])
            if l == 0:
                pack = _pack_small([held[j] for j in range(DEPTH)], held[DEPTH]["loss"])
                *held["small"], token = small_start(pack)
                started = started + token[0, 0]
            return started
        if "dx" in g:
            rs_state[(l, True)], rs_started[(l, True)] = reduce_scatter_chips(rs_state[(l, True)], [g["dx"]])
            return rs_started[(l, True)]
        key = (l, "w_in" in g)
        rs_state[key], started = reduce_scatter_pair(f"{l}{'b' if key[1] else 'a'}", g)
        return started

    _, grad_x, _ = local_step(x[0], positions[0], loss_target[0], layers, start_exchange)

    last = jnp.zeros((8, LANES), f32) + rs_started[(0, True)]
    behind = [grad_x, last]
    red = [dict() for _ in range(DEPTH)]
    swaps = {key: reduce_scatter_finish(rs_state[key], behind) for key in ((1, False), (1, True), (0, False))}
    small, tail = _unpack_small(small_sum(*reversed(small_wait(*held["small"], behind))))
    loss = tail[0, 0]

    G, DW, NM, NV = {}, {}, {}, {}
    zc = jnp.zeros((3, D), f32)
    wp = _pack_small([{**{n: Wt[n][l] for n in VECS + ("b_s", "w_s")}, "conv_w": zc} for l in range(DEPTH)], jnp.zeros((8, LANES), f32))
    mp = _pack_small([{**{n: Mt[n][l] for n in VECS + ("b_s", "w_s")}, "conv_w": zc} for l in range(DEPTH)], jnp.zeros((8, LANES), f32))
    vp = _pack_small([{**{n: Vt[n][l] for n in VECS + ("b_s", "w_s")}, "conv_w": zc} for l in range(DEPTH)], jnp.ones((8, LANES), f32))
    gp = _pack_small(small, jnp.zeros((8, LANES), f32))
    outs = [_unpack_small(a)[0] for a in adamw_small("adamw_small", gp, wp, mp, vp)]
    for n in VECS + ("b_s", "w_s"):
        G[n] = jnp.stack([small[l][n] for l in range(DEPTH)])
        DW[n], NM[n], NV[n] = (jnp.stack([o[l][n] for l in range(DEPTH)]) for o in outs)
    gconv = jnp.stack([lax.dynamic_slice(small[l]["conv_w"], (0, chip * cw), (3, cw)) for l in range(DEPTH)])
    G["conv_w"] = gconv
    flat = lambda a: a.reshape(DEPTH * 3, cw)
    d, m2, v2 = adamw_small("adamw_conv", flat(gconv), flat(conv_w), flat(m_conv_w), flat(v_conv_w))
    DW["conv_w"], NM["conv_w"], NV["conv_w"] = (a.reshape(DEPTH, 3, cw) for a in (d, m2, v2))

    for key in swaps:
        red[key[0]].update(reduce_scatter_join(swaps[key], [d, DW["ln2_b"]]))
    updated = {}
    for n in BIG[1:]:
        tr = (lambda a: jnp.swapaxes(a, 1, 2)) if n in ("w_gate", "w_up") else (lambda a: a)
        updated[n] = adamw_big("adamw_" + n, (red[0][n], red[1][n]), tr(Wt[n]), tr(Mt[n]), tr(Vt[n]))
        G[n], DW[n], NM[n], NV[n] = map(tr, updated[n])
    done = [d, DW["ln2_b"], red[1]["w_in"][1]] + [updated[n][1] for n in BIG[1:]]
    red[0].update(reduce_scatter_join(reduce_scatter_finish(rs_state[(0, True)], done), [updated["w_o"][1]]))
    G["w_in"], DW["w_in"], NM["w_in"], NV["w_in"] = adamw_big(
        "adamw_w_in", (red[0]["w_in"], red[1]["w_in"]), Wt["w_in"], Mt["w_in"], Vt["w_in"])

    return (loss, grad_x[None], *[G[n] for n in WEIGHTS], *[DW[n] for n in WEIGHTS], *[NM[n] for n in WEIGHTS],
            *[NV[n] for n in WEIGHTS])
```

```python
import functools
import math

import jax
import jax.numpy as jnp
from jax import lax
from jax.experimental import pallas as pl
from jax.experimental.pallas import tpu as pltpu

D = 1024
NIN = 12800
DFF = 2816
NCHIP = 4
FB = DFF // NCHIP
WIN_SHARD = NIN // NCHIP
DEPTH = 2
GROUPS = ((128, 1), (512, 4), (2048, 16))
HD = 64
BLK = 128
AO = 512
ALPHA = (2 * DEPTH) ** 0.25
EPS = 1e-5
ROPE_THETA = 10000.0
LANES = 128
NEG = -1e30

C_GATES, C_BCH, C_QKV, C_UV = 0, 3 * D, 6 * D, 6 * D + 9 * AO

MX = jnp.bfloat16
ACT = jnp.bfloat16

ADAM_LR, ADAM_B1, ADAM_B2, ADAM_EPS, ADAM_WD, ADAM_STEP = 0.001, 0.9, 0.999, 1e-08, 0.01, 10

f32 = jnp.float32
NT = (((1,), (1,)), ((), ()))
TN = (((0,), (0,)), ((), ()))


def _cp(sem, vmem_mb=48):
    return pltpu.CompilerParams(dimension_semantics=sem, vmem_limit_bytes=vmem_mb << 20)


def _dot(a, b, dims=None):
    if dims is None:
        return jnp.dot(a, b, preferred_element_type=f32)
    return lax.dot_general(a, b, dims, preferred_element_type=f32)


def _ln_stats(r):
    mu = jnp.mean(r, axis=-1, keepdims=True)
    xc = r - mu
    var = jnp.mean(xc * xc, axis=-1, keepdims=True)
    rstd = lax.rsqrt(var + EPS)
    return xc * rstd, rstd


def _ln_bwd(dy, xhat, rstd, g):
    dxh = dy * g
    return rstd * (dxh - jnp.mean(dxh, axis=-1, keepdims=True) - xhat * jnp.mean(dxh * xhat, axis=-1, keepdims=True))


def _gelu(x):
    return 0.5 * x * (1.0 + lax.erf(x * (1.0 / math.sqrt(2.0))))


def _gelu_and_grad(x):
    cdf = 0.5 * (1.0 + lax.erf(x * (1.0 / math.sqrt(2.0))))
    return x * cdf, cdf + x * jnp.exp(-0.5 * x * x) * (1.0 / math.sqrt(2.0 * math.pi))


def _sigmoid(x):
    return 0.5 * jnp.tanh(0.5 * x) + 0.5


def _acc_rows(o_ref, first, val):
    @pl.when(first)
    def _():
        o_ref[...] = jnp.zeros_like(o_ref)
    o_ref[...] += jnp.sum(val, axis=0, keepdims=True)


def mm_in(x, w, bias):
    T = x.shape[0]
    tm, tn = min(2048, T), 1280

    def body(x_ref, w_ref, b_ref, o_ref, xb):
        @pl.when(pl.program_id(1) == 0)
        def _():
            xb[...] = x_ref[...].astype(MX)
        o_ref[...] = (_dot(xb[...], w_ref[...]) + b_ref[...]).astype(o_ref.dtype)

    return pl.pallas_call(
        body, name="mm_in", grid=(T // tm, NIN // tn),
        in_specs=[pl.BlockSpec((tm, D), lambda i, j: (i, 0), pipeline_mode=pl.Buffered(1)),
                  pl.BlockSpec((D, tn), lambda i, j: (0, j)), pl.BlockSpec((1, tn), lambda i, j: (0, j))],
        out_specs=pl.BlockSpec((tm, tn), lambda i, j: (i, j)),
        out_shape=jax.ShapeDtypeStruct((T, NIN), ACT),
        scratch_shapes=[pltpu.VMEM((tm, D), MX)],
        compiler_params=_cp(("parallel", "arbitrary")),
    )(x, w, bias)


HALO = 16
TM_AC = 256


def _uv_specs():
    return [pl.BlockSpec((TM_AC, 512), functools.partial(lambda i, j: (i, j), j=C_UV // 512 + j)) for j in range(4)]


def _gmlp_fwd(up, vp, ws_ref, bs_ref, lg, lb, u=None, gv=None):
    u = _gelu(up) if u is None else u
    xhat, rstd = _ln_stats(_gelu(vp) if gv is None else gv)
    vn = xhat * lg + lb
    vnb = vn.astype(MX)
    rows = []
    for c in range(up.shape[0] // BLK):
        r = slice(c * BLK, (c + 1) * BLK)
        rows.append(jnp.concatenate(
            [_dot(ws_ref[g], vnb[r, g * BLK:(g + 1) * BLK]) + bs_ref[g] for g in range(8)], axis=1))
    return u, vn, xhat, rstd, jnp.concatenate(rows, axis=0)


def mix_ac_fwd(proj, conv_w, wst, bsx, lg, lb):
    T = proj.shape[0]
    tm = TM_AC

    def body(bch, halo, u0, u1, v0, v1, cw, ws, bs, lg_ref, lb_ref, ya, yc, zs):
        i = pl.program_id(0)
        pb = bch[...].astype(f32)
        z = pb[:, D:2 * D] * pb[:, 2 * D:]
        hz = halo[:, :D].astype(f32) * halo[:, D:].astype(f32)
        zs[0:HALO, :] = jnp.where(i > 0, hz, 0.0)
        zs[HALO:HALO + tm, :] = z
        cv = cw[0:1, :] * zs[HALO - 2:HALO - 2 + tm, :] + cw[1:2, :] * zs[HALO - 1:HALO - 1 + tm, :] + cw[2:3, :] * z
        ya[...] = (pb[:, :D] * cv).astype(ya.dtype)
        up = jnp.concatenate([u0[...], u1[...]], axis=1).astype(f32)
        vp = jnp.concatenate([v0[...], v1[...]], axis=1).astype(f32)
        u, _, _, _, sp = _gmlp_fwd(up, vp, ws, bs, lg_ref[...], lb_ref[...])
        yc[...] = (u * sp).astype(yc.dtype)

    full = lambda shape: pl.BlockSpec(shape, lambda i: (0,) * len(shape))
    return pl.pallas_call(
        body, name="mix_ac_fwd", grid=(T // tm,),
        in_specs=[pl.BlockSpec((tm, 3 * D), lambda i: (i, 1)),
                  pl.BlockSpec((HALO, 2 * D), lambda i: (jnp.maximum(i * (tm // HALO) - 1, 0), 2)),
                  *_uv_specs(), full((3, D)), full((8, BLK, BLK)), full((8, BLK, BLK)), full((1, D)), full((1, D))],
        out_specs=[pl.BlockSpec((tm, D), lambda i: (i, 0))] * 2,
        out_shape=[jax.ShapeDtypeStruct((T, D), MX)] * 2,
        scratch_shapes=[pltpu.VMEM((HALO + tm, D), f32)],
        compiler_params=_cp(("parallel",)),
    )(proj, proj, proj, proj, proj, proj, conv_w, wst, bsx, lg, lb)


def _swap_halves(x):
    lane = lax.broadcasted_iota(jnp.int32, x.shape, 1)
    return jnp.where((lane % HD) < HD // 2, pltpu.roll(x, x.shape[1] - HD // 2, 1), pltpu.roll(x, HD // 2, 1))


def _tile4(t):
    return jnp.concatenate([t] * (AO // LANES), axis=1)


TM_FOLD = 1024


def _fold_out(nat, x, out_ref, d):
    if d == 1:
        out_ref[0] = x.astype(out_ref.dtype)
        return
    rows = x.shape[0] // d
    for j in range(AO // LANES):
        nat[j] = x[:, j * LANES:(j + 1) * LANES]
    for r in range(d):
        out_ref[r] = jnp.concatenate(
            [nat.at[j][pl.ds(r, rows, stride=d), :] for j in range(AO // LANES)], axis=1).astype(out_ref.dtype)


def _unfold_in(nat, in_ref, d):
    if d == 1:
        return in_ref[0].astype(f32)
    rows = in_ref.shape[1]
    for r in range(d):
        v = in_ref[r].astype(f32)
        for j in range(AO // LANES):
            nat.at[j][pl.ds(r, rows, stride=d), :] = v[:, j * LANES:(j + 1) * LANES]
    return jnp.concatenate([nat[j] for j in range(AO // LANES)], axis=1)


def fold_rope(proj, cos_t, sin_t, g, d):
    T = proj.shape[0]
    tm = TM_FOLD
    rows = tm // d

    def body(x_ref, c_ref, s_ref, q_o, k_o, v_o, nat):
        cos, sin = _tile4(c_ref[...]), _tile4(s_ref[...])
        for part, out, scale in ((0, q_o, HD ** -0.5), (1, k_o, 1.0), (2, v_o, None)):
            x = x_ref[:, part * AO:(part + 1) * AO].astype(f32)
            if scale is not None:
                x = (x * cos + _swap_halves(x) * sin) * scale
            _fold_out(nat, x, out, d)

    fold_spec = pl.BlockSpec((d, rows, AO), lambda i: (0, i, 0))
    return pl.pallas_call(
        body, name=f"fold_rope{g}", grid=(T // tm,),
        in_specs=[pl.BlockSpec((tm, 3 * AO), lambda i: (i, C_QKV // (3 * AO) + g)),
                  pl.BlockSpec((tm, LANES), lambda i: (i, 0)), pl.BlockSpec((tm, LANES), lambda i: (i, 0))],
        out_specs=[fold_spec] * 3,
        out_shape=[jax.ShapeDtypeStruct((d, T // d, AO), MX)] * 3,
        scratch_shapes=[pltpu.VMEM((AO // LANES, tm, LANES), f32)],
        compiler_params=_cp(("parallel",)),
    )(proj, cos_t, sin_t)


def _stack_heads(x):
    lane = lax.broadcasted_iota(jnp.int32, x.shape, 1)
    z = jnp.zeros_like(x)
    return jnp.concatenate([jnp.where(lane < HD, x, z), jnp.where(lane >= HD, x, z)], axis=0)


def _unstack_heads(y):
    lane = lax.broadcasted_iota(jnp.int32, (BLK, LANES), 1)
    return jnp.where(lane < HD, y[:BLK], y[BLK:])


def _window_masks():
    row = lax.broadcasted_iota(jnp.int32, (2 * BLK, 2 * BLK), 0) % BLK
    col = lax.broadcasted_iota(jnp.int32, (2 * BLK, 2 * BLK), 1)
    return (col < BLK) & (col >= row), (col >= BLK) & (col - BLK <= row)


def _two_blocks(ref, b):
    r0 = pl.multiple_of(b * BLK, BLK)
    rp = pl.multiple_of(jnp.maximum(b - 1, 0) * BLK, BLK)
    return jnp.concatenate([ref[pl.ds(rp, BLK), :], ref[pl.ds(r0, BLK), :]], axis=0)


def _merge_masks():
    row = lax.broadcasted_iota(jnp.int32, (2 * BLK, BLK), 0) % BLK
    col = lax.broadcasted_iota(jnp.int32, (2 * BLK, BLK), 1)
    return col <= row, col == row


def attn_fwd(qf, kf, vf, g, nb):
    T = qf.shape[0]

    def body(q_ref, k_ref, v_ref, o_ref, l_ref):
        cur_m, own_m = _merge_masks()

        def step(b, carry):
            r0 = pl.multiple_of(b * BLK, BLK)
            rp = pl.multiple_of(jnp.maximum(b - 1, 0) * BLK, BLK)
            qs = _stack_heads(q_ref[pl.ds(r0, BLK), :])
            vc, vp = v_ref[pl.ds(r0, BLK), :], v_ref[pl.ds(rp, BLK), :]
            sp = jnp.where((b % nb) != 0, _dot(qs, k_ref[pl.ds(rp, BLK), :], NT), NEG)
            s = jnp.where(cur_m, _dot(qs, k_ref[pl.ds(r0, BLK), :], NT), sp)
            s_own = jnp.sum(jnp.where(own_m, sp, 0.0), axis=-1, keepdims=True)
            m = jnp.maximum(jnp.max(s, axis=-1, keepdims=True), s_own)
            p, p_own = jnp.exp(s - m), jnp.exp(s_own - m)
            l = jnp.sum(p, axis=-1, keepdims=True) + p_own
            pb = p.astype(MX)
            zero = jnp.zeros_like(pb)
            o = _dot(jnp.where(cur_m, pb, zero), vc) + _dot(jnp.where(cur_m, zero, pb), vp)
            o = (o + p_own * jnp.concatenate([vp, vp], axis=0).astype(f32)) / l
            o_ref[pl.ds(r0, BLK), :] = _unstack_heads(o).astype(o_ref.dtype)
            l_ref[pl.ds(r0, BLK), :] = _unstack_heads(jnp.broadcast_to(m + jnp.log(l), (2 * BLK, LANES)))
            return carry

        lax.fori_loop(0, T // BLK, step, 0, unroll=8)

    spec = pl.BlockSpec((T, LANES), lambda j: (0, j))
    return pl.pallas_call(
        body, name=f"attn_fwd{g}", grid=(AO // LANES,),
        in_specs=[spec] * 3, out_specs=[spec] * 2,
        out_shape=[jax.ShapeDtypeStruct((T, AO), ACT), jax.ShapeDtypeStruct((T, AO), f32)],
        compiler_params=_cp(("parallel",), 56),
    )(qf, kf, vf)


def _group_weights(lses):
    m = jnp.maximum(jnp.maximum(lses[0], lses[1]), lses[2])
    e = [jnp.exp(l - m) for l in lses]
    inv = 1.0 / (e[0] + e[1] + e[2])
    return [x * inv for x in e]


def _fold_specs(T, tm):
    specs = []
    for _, d in GROUPS:
        specs.append(pl.BlockSpec((d, tm // d, AO), lambda i: (0, i, 0)))
    return specs


def combine_fwd(os_, lses):
    T = os_[0].shape[0] * os_[0].shape[1]
    tm = TM_FOLD

    def body(o0, o1, o2, l0, l1, l2, y_ref, nat):
        o = [_unfold_in(nat, r, d) for r, (_, d) in zip((o0, o1, o2), GROUPS)]
        ls = [_unfold_in(nat, r, d) for r, (_, d) in zip((l0, l1, l2), GROUPS)]
        w = _group_weights(ls)
        y_ref[...] = (w[0] * o[0] + w[1] * o[1] + w[2] * o[2]).astype(y_ref.dtype)

    specs = _fold_specs(T, tm)
    return pl.pallas_call(
        body, name="combine_fwd", grid=(T // tm,),
        in_specs=specs + specs, out_specs=pl.BlockSpec((tm, AO), lambda i: (i, 0)),
        out_shape=jax.ShapeDtypeStruct((T, AO), MX),
        scratch_shapes=[pltpu.VMEM((AO // LANES, tm, LANES), f32)],
        compiler_params=_cp(("parallel",)),
    )(*os_, *lses)


TM_MIX = 512


def mix_out_fwd(proj, ya, yb, yc, x0, pa, pb, pc, wo, g1, b1):
    T = x0.shape[0]
    tm = min(TM_MIX, T)

    def body(gt, ya_r, yb_r, yc_r, x0_r, pa_r, pb_r, pc_r, wo_r, g_r, b_r, mabc, m_o, r1_o, x1_o, x1b_o):
        ma = _dot(ya_r[...], pa_r[...])
        ybv = yb_r[...]
        mb = jnp.concatenate([_dot(ybv, pb_r[k]) for k in range(NCHIP)], axis=1)
        mc = _dot(yc_r[...], pc_r[...])
        m = jnp.zeros((tm, D), f32)
        for j, mm in enumerate((ma, mb, mc)):
            mabc[:, j * D:(j + 1) * D] = mm.astype(mabc.dtype)
            m = m + _sigmoid(gt[:, j * D:(j + 1) * D].astype(f32)) * mm
        mb16 = m.astype(MX)
        m_o[...] = mb16
        r1 = ALPHA * x0_r[...] + _dot(mb16, wo_r[...])
        r1_o[...] = r1
        xhat, _ = _ln_stats(r1)
        x1 = xhat * g_r[...] + b_r[...]
        x1_o[...] = x1
        x1b_o[...] = x1.astype(MX)

    full = lambda shape: pl.BlockSpec(shape, lambda i: (0,) * len(shape), pipeline_mode=pl.Buffered(1))
    tile = lambda w: pl.BlockSpec((tm, w), lambda i: (i, 0))
    return pl.pallas_call(
        body, name="mix_out_fwd", grid=(T // tm,),
        in_specs=[tile(3 * D), tile(D), tile(AO), tile(D), tile(D), full((D, D)), full((NCHIP, AO, D // NCHIP)),
                  full((D, D)), full((D, D)), full((1, D)), full((1, D))],
        out_specs=[tile(3 * D), tile(D), tile(D), tile(D), tile(D)],
        out_shape=[jax.ShapeDtypeStruct((T, 3 * D), MX), jax.ShapeDtypeStruct((T, D), MX),
                   jax.ShapeDtypeStruct((T, D), f32), jax.ShapeDtypeStruct((T, D), f32), jax.ShapeDtypeStruct((T, D), MX)],
        compiler_params=_cp(("parallel",), 56),
    )(proj, ya, yb, yc, x0, pa, pb, pc, wo, g1, b1)


TM_FF = 512
TM_FFB = 512
ROW_CHUNK = 64


def ffn_up_fwd(x1, wg, wu):
    T = x1.shape[0]
    tm = min(TM_FFB, T)

    def body(x_r, wg_r, wu_r, g_o, u_o, h_o, gs, us):
        xb = x_r[...].astype(MX)
        for k in range(NCHIP):
            gs[...] = _dot(xb, wg_r[k])
            us[...] = _dot(xb, wu_r[k])
            for r in range(0, tm, ROW_CHUNK):
                rows = pl.ds(r, ROW_CHUNK)
                gate, up = gs[rows, :], us[rows, :]
                g_o[k, rows, :] = gate.astype(g_o.dtype)
                u_o[k, rows, :] = up.astype(u_o.dtype)
                h_o[k, rows, :] = (gate * _sigmoid(gate) * up).astype(h_o.dtype)

    wspec = pl.BlockSpec((NCHIP, D, FB), lambda i: (0, 0, 0), pipeline_mode=pl.Buffered(1))
    ospec = pl.BlockSpec((NCHIP, tm, FB), lambda i: (0, i, 0))
    return pl.pallas_call(
        body, name="ffn_up_fwd", grid=(T // tm,),
        in_specs=[pl.BlockSpec((tm, D), lambda i: (i, 0)), wspec, wspec],
        out_specs=[ospec] * 3,
        out_shape=[jax.ShapeDtypeStruct((NCHIP, T, FB), ACT)] * 2 + [jax.ShapeDtypeStruct((NCHIP, T, FB), MX)],
        scratch_shapes=[pltpu.VMEM((tm, FB), f32)] * 2,
        compiler_params=_cp(("parallel",)),
    )(x1, wg, wu)


def ffn_down_fwd(hh, wd, x1, g2, b2):
    T = x1.shape[0]
    tm = min(TM_FF, T)

    def body(h_r, w_r, x_r, g_r, b_r, r2_o, x2_o):
        r2 = ALPHA * x_r[...]
        for k in range(NCHIP):
            r2 = r2 + _dot(h_r[k], w_r[k])
        r2_o[...] = r2
        xhat, _ = _ln_stats(r2)
        x2_o[...] = xhat * g_r[...] + b_r[...]

    tile = pl.BlockSpec((tm, D), lambda i: (i, 0))
    vec = pl.BlockSpec((1, D), lambda i: (0, 0))
    return pl.pallas_call(
        body, name="ffn_down_fwd", grid=(T // tm,),
        in_specs=[pl.BlockSpec((NCHIP, tm, FB), lambda i: (0, i, 0)), pl.BlockSpec((NCHIP, FB, D), lambda i: (0, 0, 0), pipeline_mode=pl.Buffered(1)),
                  tile, vec, vec],
        out_specs=[tile, tile], out_shape=[jax.ShapeDtypeStruct((T, D), f32)] * 2,
        compiler_params=_cp(("parallel",)),
    )(hh, wd, x1, g2, b2)


def loss_grad(y, tgt):
    T = y.shape[0]
    tm = min(512, T)

    def body(y_r, t_r, l_o, dy_o):
        e = y_r[...] - t_r[...]
        dy_o[...] = e * (1.0 / D)

        @pl.when(pl.program_id(0) == 0)
        def _():
            l_o[...] = jnp.zeros_like(l_o)
        l_o[...] += (0.5 / D) * jnp.sum(e * e)

    tile = pl.BlockSpec((tm, D), lambda i: (i, 0))
    return pl.pallas_call(
        body, name="loss_grad", grid=(T // tm,),
        in_specs=[tile, tile], out_specs=[pl.BlockSpec((8, LANES), lambda i: (0, 0)), tile],
        out_shape=[jax.ShapeDtypeStruct((8, LANES), f32), jax.ShapeDtypeStruct((T, D), f32)],
        compiler_params=_cp(("arbitrary",)),
    )(y, tgt)


def ffn_down_bwd(dx2, r2, g2, wd, gate, up):
    T = dx2.shape[0]
    tm = min(TM_FFB, T)

    def body(dx_r, r_r, g_r, w_r, ga_r, up_r, dr_o, drb_o, dg_o, du_o, dlg_o, dlb_o, hs):
        i = pl.program_id(0)
        xhat, rstd = _ln_stats(r_r[...])
        dx = dx_r[...]
        _acc_rows(dlg_o, i == 0, dx * xhat)
        _acc_rows(dlb_o, i == 0, dx)
        dr = _ln_bwd(dx, xhat, rstd, g_r[...])
        dr_o[...] = dr
        drb = dr.astype(MX)
        drb_o[...] = drb
        for k in range(NCHIP):
            hs[...] = _dot(drb, w_r[k], NT)
            for r in range(0, tm, ROW_CHUNK):
                rows = pl.ds(r, ROW_CHUNK)
                dhh, gate_v, up_v = hs[rows, :], ga_r[k, rows, :].astype(f32), up_r[k, rows, :].astype(f32)
                sg = _sigmoid(gate_v)
                dg_o[k, rows, :] = (dhh * up_v * sg * (1.0 + gate_v * (1.0 - sg))).astype(dg_o.dtype)
                du_o[k, rows, :] = (dhh * gate_v * sg).astype(du_o.dtype)

    tile = pl.BlockSpec((tm, D), lambda i: (i, 0))
    vec = pl.BlockSpec((1, D), lambda i: (0, 0))
    blk = pl.BlockSpec((NCHIP, tm, FB), lambda i: (0, i, 0))
    return pl.pallas_call(
        body, name="ffn_down_bwd", grid=(T // tm,),
        in_specs=[tile, tile, vec, pl.BlockSpec((NCHIP, FB, D), lambda i: (0, 0, 0), pipeline_mode=pl.Buffered(1)), blk, blk],
        out_specs=[tile, tile, blk, blk, vec, vec],
        out_shape=[jax.ShapeDtypeStruct((T, D), f32), jax.ShapeDtypeStruct((T, D), MX)]
        + [jax.ShapeDtypeStruct((NCHIP, T, FB), MX)] * 2 + [jax.ShapeDtypeStruct((1, D), f32)] * 2,
        scratch_shapes=[pltpu.VMEM((tm, FB), f32)],
        compiler_params=_cp(("arbitrary",), 58),
    )(dx2, r2, g2, wd, gate, up)


def ffn_up_bwd(dr2, dgate, dup, wg, wu, r1, g1):
    T = dr2.shape[0]
    tm = min(TM_FFB, T)

    def body(dr2_r, dg_r, du_r, wg_r, wu_r, r1_r, g_r, dr1_o, dr1b_o, dlg_o, dlb_o):
        i = pl.program_id(0)
        dx = ALPHA * dr2_r[...]
        for k in range(NCHIP):
            dx = dx + _dot(dg_r[k], wg_r[k], NT) + _dot(du_r[k], wu_r[k], NT)
        xhat, rstd = _ln_stats(r1_r[...])
        _acc_rows(dlg_o, i == 0, dx * xhat)
        _acc_rows(dlb_o, i == 0, dx)
        dr1 = _ln_bwd(dx, xhat, rstd, g_r[...])
        dr1_o[...] = dr1
        dr1b_o[...] = dr1.astype(MX)

    tile = pl.BlockSpec((tm, D), lambda i: (i, 0))
    vec = pl.BlockSpec((1, D), lambda i: (0, 0))
    blk = pl.BlockSpec((NCHIP, tm, FB), lambda i: (0, i, 0))
    wspec = pl.BlockSpec((NCHIP, D, FB), lambda i: (0, 0, 0), pipeline_mode=pl.Buffered(1))
    return pl.pallas_call(
        body, name="ffn_up_bwd", grid=(T // tm,),
        in_specs=[tile, blk, blk, wspec, wspec, tile, vec],
        out_specs=[tile, tile, vec, vec],
        out_shape=[jax.ShapeDtypeStruct((T, D), f32), jax.ShapeDtypeStruct((T, D), MX)]
        + [jax.ShapeDtypeStruct((1, D), f32)] * 2,
        compiler_params=_cp(("arbitrary",), 58),
    )(dr2, dgate, dup, wg, wu, r1, g1)


def mix_out_bwd(dr1, proj, mabc, wo, pa, pb, pc):
    T = dr1.shape[0]
    tm = min(TM_MIX, T)

    def body(dr_r, gt, mabc_r, wo_r, pa_r, pb_r, pc_r, dmabc_o, dgt_o, dya_o, dyb_o, dyc_o):
        dm = _dot(dr_r[...].astype(MX), wo_r[...], NT)
        dmx = []
        for j in range(3):
            s = _sigmoid(gt[:, j * D:(j + 1) * D].astype(f32))
            v = (dm * s).astype(MX)
            dmx.append(v)
            dmabc_o[:, j * D:(j + 1) * D] = v
            dgt_o[:, j * D:(j + 1) * D] = (dm * mabc_r[:, j * D:(j + 1) * D].astype(f32) * s * (1.0 - s)).astype(dgt_o.dtype)
        dya_o[...] = _dot(dmx[0], pa_r[...], NT).astype(dya_o.dtype)
        dyb = jnp.zeros((tm, AO), f32)
        for k in range(NCHIP):
            dyb = dyb + _dot(dmx[1][:, k * (D // NCHIP):(k + 1) * (D // NCHIP)], pb_r[k], NT)
        dyb_o[...] = dyb.astype(dyb_o.dtype)
        dyc_o[...] = _dot(dmx[2], pc_r[...], NT).astype(dyc_o.dtype)

    full = lambda shape: pl.BlockSpec(shape, lambda i: (0,) * len(shape), pipeline_mode=pl.Buffered(1))
    tile = lambda w: pl.BlockSpec((tm, w), lambda i: (i, 0))
    return pl.pallas_call(
        body, name="mix_out_bwd", grid=(T // tm,),
        in_specs=[tile(D), tile(3 * D), tile(3 * D), full((D, D)), full((D, D)), full((NCHIP, AO, D // NCHIP)), full((D, D))],
        out_specs=[tile(3 * D), tile(3 * D), tile(D), tile(AO), tile(D)],
        out_shape=[jax.ShapeDtypeStruct((T, 3 * D), MX), jax.ShapeDtypeStruct((T, 3 * D), MX),
                   jax.ShapeDtypeStruct((T, D), ACT), jax.ShapeDtypeStruct((T, AO), ACT), jax.ShapeDtypeStruct((T, D), ACT)],
        compiler_params=_cp(("parallel",), 56),
    )(dr1, proj, mabc, wo, pa, pb, pc)


def transpose_cast(x):
    T = x.shape[0]
    tm = min(512, T)

    def body(x_r, o_r):
        o_r[...] = x_r[...].T.astype(o_r.dtype)

    return pl.pallas_call(
        body, name="transpose_cast", grid=(T // tm,),
        in_specs=[pl.BlockSpec((tm, D), lambda i: (i, 0))], out_specs=pl.BlockSpec((D, tm), lambda i: (0, i)),
        out_shape=jax.ShapeDtypeStruct((D, T), MX), compiler_params=_cp(("parallel",)),
    )(x)


def tn_matmul(name, a, b, a_spec, b_spec, out_shape, out_spec, grid):
    nt = len(grid) - 1

    def body(a_r, b_r, o_r):
        @pl.when(pl.program_id(nt) == 0)
        def _():
            o_r[...] = jnp.zeros_like(o_r)
        av = a_r[...].reshape(a_r.shape[-2:]).astype(MX)
        bv = b_r[...].reshape(b_r.shape[-2:]).astype(MX)
        o_r[...] += _dot(av, bv, TN).reshape(o_r.shape)

    return pl.pallas_call(
        body, name=name, grid=grid, in_specs=[a_spec, b_spec], out_specs=out_spec,
        out_shape=jax.ShapeDtypeStruct(out_shape, f32),
        compiler_params=_cp(("parallel",) * nt + ("arbitrary",), 56),
    )(a, b)


def attn_pre_bwd(dyb, os_, lses, ones):
    T = dyb.shape[0]
    tm = TM_FOLD

    def body(dy_r, o0, o1, o2, l0, l1, l2, ones_r, d0, d1, d2, f0, f1, f2, nat):
        o = [_unfold_in(nat, r, d) for r, (_, d) in zip((o0, o1, o2), GROUPS)]
        ls = [_unfold_in(nat, r, d) for r, (_, d) in zip((l0, l1, l2), GROUPS)]
        w = _group_weights(ls)
        dy = dy_r[...].astype(f32)
        t = dy * (w[0] * o[0] + w[1] * o[1] + w[2] * o[2])
        hi = t.astype(MX)
        lo = (t - hi.astype(f32)).astype(MX)
        c = _dot(hi, ones_r[...]) + _dot(lo, ones_r[...])
        for wg, do_o, df_o, (_, d) in zip(w, (d0, d1, d2), (f0, f1, f2), GROUPS):
            _fold_out(nat, wg * dy, do_o, d)
            _fold_out(nat, -wg * c, df_o, d)

    specs = _fold_specs(T, tm)
    return pl.pallas_call(
        body, name="attn_pre_bwd", grid=(T // tm,),
        in_specs=[pl.BlockSpec((tm, AO), lambda i: (i, 0))] + specs + specs + [pl.BlockSpec((AO, AO), lambda i: (0, 0))],
        out_specs=specs + specs,
        out_shape=[jax.ShapeDtypeStruct((d, T // d, AO), MX) for _, d in GROUPS]
        + [jax.ShapeDtypeStruct((d, T // d, AO), f32) for _, d in GROUPS],
        scratch_shapes=[pltpu.VMEM((AO // LANES, tm, LANES), f32)],
        compiler_params=_cp(("parallel",), 56),
    )(dyb, *os_, *lses, ones)


def _head_ones():
    i = jnp.arange(AO) // HD
    return (i[:, None] == i[None, :]).astype(MX)


BWD_BLOCKS = 8


def attn_bwd(qf, kf, vf, dof, lse, df, g, nb):
    T = qf.shape[0]

    def body(q_ref, k_ref, v_ref, do_ref, l_ref, d_ref, dq_ref, dk_ref, dv_ref):
        prev_m, cur_m = _window_masks()

        def head_col(ref, r0):
            v = ref[pl.ds(r0, BLK), :]
            return jnp.concatenate([v[:, 0:1], v[:, HD:HD + 1]], axis=0)

        def step(b, carry):
            dk_c, dv_c = carry
            r0 = pl.multiple_of(b * BLK, BLK)
            rp = pl.multiple_of(jnp.maximum(b - 1, 0) * BLK, BLK)
            qs, dos = _stack_heads(q_ref[pl.ds(r0, BLK), :]), _stack_heads(do_ref[pl.ds(r0, BLK), :])
            k2, v2 = _two_blocks(k_ref, b), _two_blocks(v_ref, b)
            valid = cur_m | (prev_m & ((b % nb) != 0))
            p = jnp.where(valid, jnp.exp(_dot(qs, k2, NT) - head_col(l_ref, r0)), 0.0)
            ds = (p * (_dot(dos, v2, NT) + head_col(d_ref, r0))).astype(MX)
            dq_ref[pl.ds(r0, BLK), :] = _unstack_heads(_dot(ds, k2)).astype(dq_ref.dtype)
            dk2 = _dot(ds, qs, TN)
            dv2 = _dot(p.astype(MX), dos, TN)
            dk_ref[pl.ds(rp, BLK), :] = (dk_c + dk2[:BLK]).astype(dk_ref.dtype)
            dv_ref[pl.ds(rp, BLK), :] = (dv_c + dv2[:BLK]).astype(dv_ref.dtype)
            return dk2[BLK:], dv2[BLK:]

        zero = jnp.zeros((BLK, LANES), f32)

        def steps(i, carry):
            for j in range(BWD_BLOCKS):
                carry = step(BWD_BLOCKS * i + j, carry)
            return carry

        dk_c, dv_c = lax.fori_loop(0, T // BLK // BWD_BLOCKS, steps, (zero, zero))
        dk_ref[pl.ds(T - BLK, BLK), :] = dk_c.astype(dk_ref.dtype)
        dv_ref[pl.ds(T - BLK, BLK), :] = dv_c.astype(dv_ref.dtype)

    spec = pl.BlockSpec((T, LANES), lambda j: (0, j))
    return pl.pallas_call(
        body, name=f"attn_bwd{g}", grid=(AO // LANES,),
        in_specs=[spec] * 6, out_specs=[spec] * 3,
        out_shape=[jax.ShapeDtypeStruct((T, AO), MX)] * 3,
        compiler_params=_cp(("parallel",), 60),
    )(qf, kf, vf, dof, lse, df)


def unfold_rope_bwd(dqf, dkf, dvf, cos_t, sin_t, g, d):
    T = dqf.shape[0] * dqf.shape[1]
    tm = TM_FOLD

    def body(q_r, k_r, v_r, c_ref, s_ref, o_ref, nat):
        cos, sin = _tile4(c_ref[...]), _tile4(s_ref[...])
        for part, ref, scale in ((0, q_r, HD ** -0.5), (1, k_r, 1.0), (2, v_r, None)):
            x = _unfold_in(nat, ref, d)
            if scale is not None:
                x = (x * cos - _swap_halves(x) * sin) * scale
            o_ref[:, part * AO:(part + 1) * AO] = x.astype(o_ref.dtype)

    fold_spec = pl.BlockSpec((d, tm // d, AO), lambda i: (0, i, 0))
    tab = pl.BlockSpec((tm, LANES), lambda i: (i, 0))
    return pl.pallas_call(
        body, name=f"unfold_rope_bwd{g}", grid=(T // tm,),
        in_specs=[fold_spec] * 3 + [tab, tab],
        out_specs=pl.BlockSpec((tm, 3 * AO), lambda i: (i, 0)),
        out_shape=jax.ShapeDtypeStruct((T, 3 * AO), MX),
        scratch_shapes=[pltpu.VMEM((AO // LANES, tm, LANES), f32)],
        compiler_params=_cp(("parallel",)),
    )(dqf, dkf, dvf, cos_t, sin_t)


CONV_CHUNK = 32


def conv_bwd(dya, proj, conv_w):
    T = dya.shape[0]
    tm = TM_AC
    last = T // tm - 1

    def body(dy_r, bch, hprev, dy_next, b_next, cw, d_o, dw_o, zs, ds):
        i = pl.program_id(0)
        ch = CONV_CHUNK
        hz = hprev[:, :D].astype(f32) * hprev[:, D:].astype(f32)
        zs[0:HALO, :] = jnp.where(i > 0, hz, 0.0)
        ds[tm:tm + HALO, :] = jnp.where(i < last, dy_next[...].astype(f32) * b_next[...].astype(f32), 0.0)
        for r in range(0, tm, ch):
            zs[HALO + r:HALO + r + ch, :] = bch[r:r + ch, D:2 * D].astype(f32) * bch[r:r + ch, 2 * D:].astype(f32)
            ds[r:r + ch, :] = dy_r[r:r + ch, :].astype(f32) * bch[r:r + ch, :D].astype(f32)

        @pl.when(i == 0)
        def _():
            dw_o[...] = jnp.zeros_like(dw_o)

        sums = [jnp.zeros((1, D), f32) for _ in range(3)]
        for r in range(0, tm, ch):
            z2, z1, z = (zs[HALO + r - s:HALO + r - s + ch, :] for s in (2, 1, 0))
            dcv, d1, d2 = (ds[r + s:r + s + ch, :] for s in (0, 1, 2))
            cv = cw[0:1, :] * z2 + cw[1:2, :] * z1 + cw[2:3, :] * z
            dz = cw[2:3, :] * dcv + cw[1:2, :] * d1 + cw[0:1, :] * d2
            d_o[r:r + ch, :D] = (dy_r[r:r + ch, :].astype(f32) * cv).astype(d_o.dtype)
            d_o[r:r + ch, D:2 * D] = (dz * bch[r:r + ch, 2 * D:].astype(f32)).astype(d_o.dtype)
            d_o[r:r + ch, 2 * D:] = (dz * bch[r:r + ch, D:2 * D].astype(f32)).astype(d_o.dtype)
            for k, zz in enumerate((z2, z1, z)):
                sums[k] = sums[k] + jnp.sum(dcv * zz, axis=0, keepdims=True)
        for k in range(3):
            dw_o[k:k + 1, :] += sums[k]

    nh = tm // HALO
    return pl.pallas_call(
        body, name="conv_bwd", grid=(T // tm,),
        in_specs=[pl.BlockSpec((tm, D), lambda i: (i, 0)), pl.BlockSpec((tm, 3 * D), lambda i: (i, 1)),
                  pl.BlockSpec((HALO, 2 * D), lambda i: (jnp.maximum(i * nh - 1, 0), 2)),
                  pl.BlockSpec((HALO, D), lambda i: (jnp.minimum((i + 1) * nh, T // HALO - 1), 0)),
                  pl.BlockSpec((HALO, D), lambda i: (jnp.minimum((i + 1) * nh, T // HALO - 1), 3)),
                  pl.BlockSpec((3, D), lambda i: (0, 0))],
        out_specs=[pl.BlockSpec((tm, 3 * D), lambda i: (i, 0)), pl.BlockSpec((3, D), lambda i: (0, 0))],
        out_shape=[jax.ShapeDtypeStruct((T, 3 * D), MX), jax.ShapeDtypeStruct((3, D), f32)],
        scratch_shapes=[pltpu.VMEM((HALO + tm, D), f32), pltpu.VMEM((tm + HALO, D), f32)],
        compiler_params=_cp(("arbitrary",)),
    )(dya, proj, proj, dya, proj, conv_w)


def gmlp_bwd(dyc, proj, wst, bsx, lg, lb):
    T = dyc.shape[0]
    tm = TM_AC
    last = T // tm - 1

    def body(dy_r, u0, u1, v0, v1, ws, bs, lg_r, lb_r, d_o, dws_o, dbs_o, dlg_o, dlb_o, bacc):
        i = pl.program_id(0)
        up = jnp.concatenate([u0[...], u1[...]], axis=1).astype(f32)
        vp = jnp.concatenate([v0[...], v1[...]], axis=1).astype(f32)
        u, du = _gelu_and_grad(up)
        gv, dgv = _gelu_and_grad(vp)
        u, vn, xhat, rstd, sp = _gmlp_fwd(up, vp, ws, bs, lg_r[...], lb_r[...], u, gv)
        dy = dy_r[...].astype(f32)
        d_o[:, :D] = (dy * sp * du).astype(d_o.dtype)
        dsp = dy * u
        dspb, vnb = dsp.astype(MX), vn.astype(MX)

        @pl.when(i == 0)
        def _():
            dws_o[...] = jnp.zeros_like(dws_o)
            bacc[...] = jnp.zeros_like(bacc)

        rows = []
        for c in range(tm // BLK):
            r = slice(c * BLK, (c + 1) * BLK)
            cols = []
            for g in range(8):
                cs = slice(g * BLK, (g + 1) * BLK)
                dws_o[g] += _dot(dspb[r, cs], vnb[r, cs], NT)
                bacc[g] += dsp[r, cs]
                cols.append(_dot(ws[g], dspb[r, cs], TN))
            rows.append(jnp.concatenate(cols, axis=1))
        dvn = jnp.concatenate(rows, axis=0)
        _acc_rows(dlg_o, i == 0, dvn * xhat)
        _acc_rows(dlb_o, i == 0, dvn)
        d_o[:, D:] = (_ln_bwd(dvn, xhat, rstd, lg_r[...]) * dgv).astype(d_o.dtype)

        @pl.when(i == last)
        def _():
            row = lax.broadcasted_iota(jnp.int32, (BLK, BLK), 0)
            col = lax.broadcasted_iota(jnp.int32, (BLK, BLK), 1)
            ones = jnp.ones((8, BLK), MX)
            for g in range(8):
                dws_o[g] = jnp.where(col <= row, dws_o[g], 0.0)
                a = bacc[g]
                hi = a.astype(MX)
                lo = (a - hi.astype(f32)).astype(MX)
                dbs_o[g:g + 1, :] = (_dot(ones, hi, NT) + _dot(ones, lo, NT))[0:1, :]

    full = lambda shape: pl.BlockSpec(shape, lambda i: (0,) * len(shape))
    return pl.pallas_call(
        body, name="gmlp_bwd", grid=(T // tm,),
        in_specs=[pl.BlockSpec((tm, D), lambda i: (i, 0)), *_uv_specs(), full((8, BLK, BLK)), full((8, BLK, BLK)),
                  full((1, D)), full((1, D))],
        out_specs=[pl.BlockSpec((tm, 2 * D), lambda i: (i, 0)), full((8, BLK, BLK)), full((8, BLK)), full((1, D)), full((1, D))],
        out_shape=[jax.ShapeDtypeStruct((T, 2 * D), MX), jax.ShapeDtypeStruct((8, BLK, BLK), f32),
                   jax.ShapeDtypeStruct((8, BLK), f32), jax.ShapeDtypeStruct((1, D), f32), jax.ShapeDtypeStruct((1, D), f32)],
        scratch_shapes=[pltpu.VMEM((8, BLK, BLK), f32)],
        compiler_params=_cp(("arbitrary",)),
    )(dyc, proj, proj, proj, proj, wst, bsx, lg, lb)


PART_TILES = (6, 6, 3, 3, 3, 4)
PART_START = (0, 6, 12, 15, 18, 21)
TJ = 512


def _part_specs(tm, rows_axis):
    specs = []
    for n, s in zip(PART_TILES, PART_START):
        def imap(*idx, n=n, s=s):
            i, j = idx[rows_axis], idx[1 - rows_axis]
            inside = (j >= s) & (j < s + n)
            return (jnp.where(inside, i, 0), jnp.clip(j - s, 0, n - 1))
        specs.append(pl.BlockSpec((tm, TJ), imap))
    return specs


def _with_part(j, refs, fn):
    for r, n, s in zip(refs, PART_TILES, PART_START):
        @pl.when((j >= s) & (j < s + n))
        def _():
            fn(r[...])


def dx_in(dr1, parts, w, bias):
    T = dr1.shape[0]
    tm = min(2048, T)

    def body(dr_r, p0, p1, p2, p3, p4, p5, w_r, b_r, o_r):
        j = pl.program_id(1)

        @pl.when(j == 0)
        def _():
            o_r[...] = ALPHA * dr_r[...] + b_r[...]

        def acc(tile):
            o_r[...] += _dot(tile, w_r[...], NT)
        _with_part(j, (p0, p1, p2, p3, p4, p5), acc)

    once = dict(pipeline_mode=pl.Buffered(1))
    return pl.pallas_call(
        body, name="dx_in", grid=(T // tm, NIN // TJ),
        in_specs=[pl.BlockSpec((tm, D), lambda i, j: (i, 0), **once)] + _part_specs(tm, 0)
        + [pl.BlockSpec((D, TJ), lambda i, j: (0, j)), pl.BlockSpec((1, D), lambda i, j: (0, 0))],
        out_specs=pl.BlockSpec((tm, D), lambda i, j: (i, 0), **once),
        out_shape=jax.ShapeDtypeStruct((T, D), f32),
        compiler_params=_cp(("parallel", "arbitrary"), 56),
    )(dr1, *parts, w, bias)


def dw_in(x0t, parts):
    T = x0t.shape[1]
    tk = min(2048, T)

    def body(x_r, p0, p1, p2, p3, p4, p5, o_r):
        j, t = pl.program_id(0), pl.program_id(1)

        @pl.when(t == 0)
        def _():
            o_r[...] = jnp.zeros_like(o_r)

        def acc(tile):
            o_r[...] += _dot(x_r[:, pl.ds(pl.multiple_of(t * tk, tk), tk)], tile)
        _with_part(j, (p0, p1, p2, p3, p4, p5), acc)

    return pl.pallas_call(
        body, name="dw_in", grid=(NIN // TJ, T // tk),
        in_specs=[pl.BlockSpec((D, T), lambda j, t: (0, 0), pipeline_mode=pl.Buffered(1))] + _part_specs(tk, 1),
        out_specs=pl.BlockSpec((D, TJ), lambda j, t: (0, j)),
        out_shape=jax.ShapeDtypeStruct((D, NIN), f32),
        compiler_params=_cp(("parallel", "arbitrary"), 56),
    )(x0t, *parts)


def rope_tables(positions):
    half = HD // 2
    inv_freq = ROPE_THETA ** (-jnp.arange(half, dtype=f32) / half)
    ang = positions.astype(f32)[:, None] * inv_freq
    cos, sin = jnp.cos(ang), jnp.sin(ang)
    return jnp.tile(cos, (1, LANES // half)), jnp.tile(jnp.concatenate([-sin, sin], axis=1), (1, LANES // HD))


def _flat(a):
    return a.reshape(a.shape[0] * a.shape[1], a.shape[2])


def layer_fwd(x0, W, cos_t, sin_t):
    T = x0.shape[0]
    proj = mm_in(x0, W["w_in"], W["in_bias"])
    ya, yc = mix_ac_fwd(proj, W["conv_w"], W["wst"], W["bsx"], W["gmlp_ln_g"], W["gmlp_ln_b"])
    folded, os_, lses = [], [], []
    for g, (_, d) in enumerate(GROUPS):
        qf, kf, vf = fold_rope(proj, cos_t, sin_t, g, d)
        o, lse = attn_fwd(_flat(qf), _flat(kf), _flat(vf), g, T // d // BLK)
        folded.append((qf, kf, vf))
        os_.append(o.reshape(d, T // d, AO))
        lses.append(lse.reshape(d, T // d, AO))
    yb = combine_fwd(os_, lses)
    if "late" in W:
        W = {**W, **W["late"](yb)}
    mabc, m, r1, x1, x1b = mix_out_fwd(proj, ya, yb, yc, x0, W["p_a"], W["p_b"], W["p_c"], W["w_o"], W["ln1_g"], W["ln1_b"])
    gate, up, hh = ffn_up_fwd(x1b, W["w_gate"], W["w_up"])
    r2, x2 = ffn_down_fwd(hh, W["w_down"], x1, W["ln2_g"], W["ln2_b"])
    saved = dict(x0=x0, proj=proj, ya=ya, yb=yb, yc=yc, folded=folded, os=os_, lses=lses, mabc=mabc, m=m, r1=r1,
                 x1b=x1b, gate=gate, up=up, hh=hh, r2=r2)
    return x2, saved, W


def layer_bwd(dx2, S, W, cos_t, sin_t, on_grads=None):
    T = dx2.shape[0]
    tk = min(2048, T)
    G = {}
    dr2, dr2b, dgate, dup, G["ln2_g"], G["ln2_b"] = ffn_down_bwd(dx2, S["r2"], W["ln2_g"], W["w_down"], S["gate"], S["up"])
    blk_a = pl.BlockSpec((1, tk, FB), lambda k, t: (k, t, 0))
    row_b = pl.BlockSpec((tk, D), lambda k, t: (t, 0))
    G["w_down"] = tn_matmul("dw_down", S["hh"], dr2b, blk_a, row_b, (NCHIP, FB, D),
                            pl.BlockSpec((1, FB, D), lambda k, t: (k, 0, 0)), (NCHIP, T // tk))
    for nm, dv in (("w_gate", dgate), ("w_up", dup)):
        G[nm] = tn_matmul("d" + nm, dv, S["x1b"], blk_a, row_b, (NCHIP, FB, D),
                          pl.BlockSpec((1, FB, D), lambda k, t: (k, 0, 0)), (NCHIP, T // tk))
    dr1, dr1b, G["ln1_g"], G["ln1_b"] = ffn_up_bwd(dr2, dgate, dup, W["w_gate"], W["w_up"], S["r1"], W["ln1_g"])
    dmabc, dgates, dya, dyb, dyc = mix_out_bwd(dr1b, S["proj"], S["mabc"], W["w_o"], W["p_a"], W["p_b"], W["p_c"])
    one = (1, T // tk)
    full_o = pl.BlockSpec((D, D), lambda k, t: (0, 0))
    G["w_o"] = tn_matmul("dw_o", S["m"], dr1b, row_b, row_b, (D, D), full_o, one)
    G["p_a"] = tn_matmul("dp_a", S["ya"], dmabc, row_b, pl.BlockSpec((tk, D), lambda k, t: (t, 0)), (D, D), full_o, one)
    G["p_c"] = tn_matmul("dp_c", S["yc"], dmabc, row_b, pl.BlockSpec((tk, D), lambda k, t: (t, 2)), (D, D), full_o, one)
    G["p_b"] = tn_matmul("dp_b", S["yb"], dmabc, pl.BlockSpec((tk, AO), lambda k, t: (t, 0)),
                         pl.BlockSpec((tk, D // NCHIP), lambda k, t: (t, NCHIP + k)), (NCHIP, AO, D // NCHIP),
                         pl.BlockSpec((1, AO, D // NCHIP), lambda k, t: (k, 0, 0)), (NCHIP, T // tk))
    conv_w = W["conv_w"]
    if on_grads is not None:
        conv_w = conv_w + on_grads({n: G[n] for n in BIG if n != "w_in"})
    dbch, G["conv_w"] = conv_bwd(dya, S["proj"], conv_w)
    duv, G["w_s"], G["b_s"], G["gmlp_ln_g"], G["gmlp_ln_b"] = gmlp_bwd(
        dyc, S["proj"], W["wst"], W["bsx"], W["gmlp_ln_g"], W["gmlp_ln_b"])
    ones = _head_ones()
    if on_grads is not None:
        small = {n: G[n] for n in VECS + ("b_s", "w_s", "conv_w")}
        ones = ones + on_grads(small).astype(MX)
    pre = attn_pre_bwd(dyb, S["os"], S["lses"], ones)
    dqkv = []
    for g, (_, d) in enumerate(GROUPS):
        qf, kf, vf = S["folded"][g]
        dqf, dkf, dvf = attn_bwd(_flat(qf), _flat(kf), _flat(vf), _flat(pre[g]), _flat(S["lses"][g]), _flat(pre[3 + g]),
                                 g, T // d // BLK)
        shp = (d, T // d, AO)
        dqkv.append(unfold_rope_bwd(dqf.reshape(shp), dkf.reshape(shp), dvf.reshape(shp), cos_t, sin_t, g, d))
    parts = (dgates, dbch, *dqkv, duv)
    G["w_in"] = dw_in(transpose_cast(S["x0"]), parts)
    bias = jnp.zeros((1, D), f32)
    if on_grads is not None:
        bias = bias + on_grads({"w_in": G["w_in"]})
    dx0 = dx_in(dr1, parts, W["w_in"], bias)
    started = on_grads({"dx": dx0}) if on_grads is not None else None
    return dx0, G, started


def prep_layer_weights(Wl):
    W = dict(Wl)
    tril = jnp.tril(jnp.ones((BLK, BLK), f32))
    W["wst"] = (Wl["w_s"] * tril[None]).astype(MX)
    W["bsx"] = jnp.broadcast_to(Wl["b_s"][:, :, None], (8, BLK, BLK))
    for n in ("gmlp_ln_g", "gmlp_ln_b", "ln1_g", "ln1_b", "ln2_g", "ln2_b"):
        W[n] = Wl[n].reshape(1, D)
    W["in_bias"] = jnp.zeros((1, NIN), f32) + Wl.get("after", 0.0)
    return W


def local_step(x, positions, target, layers, on_grads=None):
    cos_t, sin_t = rope_tables(positions)
    Ws, saved = [], []
    h = x
    for Wl in layers:
        h, S, W = layer_fwd(h, prep_layer_weights(Wl(h) if callable(Wl) else Wl), cos_t, sin_t)
        Ws.append(W)
        saved.append(S)
    lsum, dh = loss_grad(h, target)
    if on_grads is not None:
        on_grads(len(Ws), {"loss": lsum})
    grads = [None] * len(Ws)
    started = None
    for l in reversed(range(len(Ws))):
        W = Ws[l]
        if started is not None:
            W = dict(W, ln2_g=W["ln2_g"] + started)
        hook = functools.partial(on_grads, l) if on_grads is not None else None
        dh, grads[l], started = layer_bwd(dh, saved[l], W, cos_t, sin_t, hook)
    return lsum, dh, grads


MESH = pl.DeviceIdType.MESH
ANY = pl.BlockSpec(memory_space=pl.ANY)
BIG = ("w_in", "w_gate", "w_up", "w_down", "p_a", "p_b", "p_c", "w_o")
NBIG = len(BIG)


def _place():
    x, y, c = lax.axis_index("x"), lax.axis_index("y"), lax.axis_index("c")
    return x, y, c, 2 * x + y


def _rcopy(src, dst, send, recv, dev):
    return pltpu.make_async_remote_copy(src_ref=src, dst_ref=dst, send_sem=send, recv_sem=recv, device_id=dev,
                                        device_id_type=MESH)


def _cols(ref, k, width):
    start = k * width if isinstance(k, int) else pl.multiple_of(k * width, LANES)
    return ref.at[:, pl.ds(start, width)]


CHUNK_BYTES = 1 << 20


def _pieces(shape, itemsize, nbytes=CHUNK_BYTES):
    rows, cols = shape[-2], shape[-1]
    per = max(16, nbytes // (cols * itemsize) // 16 * 16)
    out = []
    for lead in (range(shape[0]) if len(shape) == 3 else (None,)):
        for r in range(0, rows, per):
            sl = (pl.ds(r, min(per, rows - r)), slice(None))
            out.append(sl if lead is None else (lead,) + sl)
    return out


def _start_pieces(src, dst, make, nbytes=CHUNK_BYTES):
    for idx in _pieces(src.shape, jnp.dtype(src.dtype).itemsize, nbytes):
        make(src.at[idx], dst.at[idx]).start()


def gather_halves(shards):
    n = len(shards)

    def body(*refs):
        srcs, dsts = refs[:n], refs[n:2 * n]
        send, recv, own_send, own_recv = refs[2 * n:]
        x, y, c, k = _place()
        sib = (x, y, 1 - c)
        chips = [(1 - x, y), (x, 1 - y), (1 - x, 1 - y)]

        def slot(a, layer, pos):
            if a == 0:
                return _cols(dsts[0].at[layer], pos, WIN_SHARD)
            return dsts[a].at[pos, layer]

        def ici(a, j, src, dst):
            return _rcopy(src, dst, send.at[a, j], recv.at[a, j], (*chips[j], c))

        def d2d(a, j, src, dst):
            return _rcopy(src, dst, send.at[a, 3 + j], recv.at[a, 3 + j], sib)

        def own(a, layer, src, dst):
            return _rcopy(src, dst, own_send.at[a, layer], own_recv.at[a, layer], sib)

        for a in range(n):
            for j in range(3):
                _start_pieces(srcs[a].at[c], slot(a, c, k), functools.partial(ici, a, j))
        for a in range(n):
            for layer in range(DEPTH):
                _start_pieces(srcs[a].at[layer], slot(a, layer, k), functools.partial(own, a, layer))
        for a in range(n):
            for j, (cx, cy) in enumerate(chips):
                landed = slot(a, c, 2 * cx + cy)
                ici(a, j, landed, landed).wait_recv()
                _start_pieces(landed, landed, functools.partial(d2d, a, j))
        for a in range(n):
            for j, (cx, cy) in enumerate(chips):
                passed = slot(a, 1 - c, 2 * cx + cy)
                d2d(a, j, passed, passed).wait_recv()
                landed = slot(a, c, 2 * cx + cy)
                d2d(a, j, landed, landed).wait_send()
                ici(a, j, srcs[a].at[c], slot(a, c, k)).wait_send()
            for layer in range(DEPTH):
                own(a, layer, srcs[a].at[layer], slot(a, layer, k)).wait()

    outs = [jax.ShapeDtypeStruct((2, shards[0].shape[1], NIN), shards[0].dtype)]
    outs += [jax.ShapeDtypeStruct((NCHIP,) + s.shape, s.dtype) for s in shards[1:]]
    return pl.pallas_call(
        body, name="gather_halves", in_specs=[ANY] * n, out_specs=[ANY] * n, out_shape=outs,
        scratch_shapes=[pltpu.SemaphoreType.DMA((n, 6)), pltpu.SemaphoreType.DMA((n, 6)),
                        pltpu.SemaphoreType.DMA((n, DEPTH)), pltpu.SemaphoreType.DMA((n, DEPTH))],
    )(*shards)


def _gather_slot(dst, pos):
    return _cols(dst, pos, WIN_SHARD) if len(dst.shape) == 2 else dst.at[pos]


def _gather_copy(a, j, src, dst, send, recv, dev):
    return _rcopy(src, dst, send.at[a * NCHIP + j], recv.at[a * NCHIP + j], dev)


def gather_start(tag, shards, after):
    n = len(shards)

    def body(*refs):
        srcs, dsts = refs[:n], refs[n:2 * n]
        send, recv = refs[2 * n + len(after)], refs[2 * n + len(after) + 1]
        token = refs[-1]
        x, y, c, k = _place()
        peers = [(1 - x, y, c), (x, 1 - y, c), (1 - x, 1 - y, c), (x, y, 1 - c)]
        for a in range(n):
            for j, dev in enumerate(peers):
                _start_pieces(srcs[a], _gather_slot(dsts[a], k),
                              lambda s, d, a=a, j=j, dev=dev: _gather_copy(a, j, s, d, send, recv, dev))
        token[...] = jnp.zeros_like(token)

    gathered = [lax.empty((D, NIN) if s.shape == (D, WIN_SHARD) else (NCHIP,) + s.shape, s.dtype) for s in shards]
    ops = [pltpu.with_memory_space_constraint(v, pltpu.HBM) for v in list(shards) + gathered]
    sem = pltpu.SemaphoreType.DMA((n * NCHIP,))
    res = pl.pallas_call(
        body, name=f"gather_start{tag}", in_specs=[HBM] * (2 * n) + [ANY] * len(after),
        out_specs=[SEMS, SEMS] + [HBM] * (2 * n) + [pl.BlockSpec(memory_space=pltpu.VMEM)],
        out_shape=[sem, sem] + [pltpu.HBM(v.shape, v.dtype) for v in ops] + [jax.ShapeDtypeStruct((8, LANES), f32)],
        input_output_aliases={i: 2 + i for i in range(2 * n)},
        compiler_params=pltpu.CompilerParams(has_side_effects=EFFECT),
    )(*ops, *after)
    return res[0], res[1], res[2:2 + n], res[2 + n:2 + 2 * n], res[-1]


def gather_wait(tag, send, recv, shards, gathered, after):
    n = len(shards)

    def body(*refs):
        srcs, dsts = refs[:n], refs[n:2 * n]
        send_r, recv_r = refs[2 * n], refs[2 * n + 1]
        x, y, c, k = _place()
        peers = [(1 - x, y, c), (x, 1 - y, c), (1 - x, 1 - y, c), (x, y, 1 - c)]
        for a in range(n):
            for j, dev in enumerate(peers):
                _gather_copy(a, j, srcs[a], _gather_slot(dsts[a], k), send_r, recv_r, dev).wait_send()
                pos = 2 * dev[0] + dev[1]
                _gather_copy(a, j, srcs[a], _gather_slot(dsts[a], pos), send_r, recv_r, dev).wait_recv()

    ops = list(shards) + list(gathered)
    res = pl.pallas_call(
        body, name=f"gather_wait{tag}", in_specs=[HBM] * (2 * n) + [SEMS, SEMS] + [ANY] * len(after),
        out_specs=[HBM] * (2 * n), out_shape=[pltpu.HBM(v.shape, v.dtype) for v in ops],
        input_output_aliases={i: i for i in range(2 * n)},
        compiler_params=pltpu.CompilerParams(has_side_effects=EFFECT),
    )(*ops, send, recv, *after)
    return res[n:]


def _half(ref, h):
    rows = ref.shape[-2] // 2
    start = pl.multiple_of(h * rows, 16)
    if len(ref.shape) == 2:
        return ref.at[pl.ds(start, rows), :]
    return ref.at[:, pl.ds(start, rows), :]


HBM = pl.BlockSpec(memory_space=pltpu.HBM)
SEMS = pl.BlockSpec(memory_space=pltpu.SEMAPHORE)
EFFECT = pltpu.SideEffectType.DATAFLOW_SIDE_EFFECTING


def rs_pair_start(tag, grads, halves=True):
    n = len(grads)

    def body(*refs):
        g, theirs = refs[:n], refs[n:2 * n]
        send, recv = refs[2 * n], refs[2 * n + 1]
        x, y, c, _ = _place()
        for a in range(n):
            _start_pieces(_half(g[a], 1 - c) if halves else g[a], theirs[a],
                          lambda s, d, a=a: _rcopy(s, d, send.at[a], recv.at[a], (x, y, 1 - c)))
        refs[-1][...] = jnp.zeros_like(refs[-1])

    lands = [lax.empty(g.shape[:-2] + (g.shape[-2] // 2 if halves else g.shape[-2], g.shape[-1]), g.dtype) for g in grads]
    ops = [pltpu.with_memory_space_constraint(v, pltpu.HBM) for v in list(grads) + lands]
    sem = pltpu.SemaphoreType.DMA((n,))
    res = pl.pallas_call(
        body, name=f"rs_pair_start{tag}", in_specs=[HBM] * (2 * n),
        out_specs=[SEMS, SEMS] + [HBM] * (2 * n) + [pl.BlockSpec(memory_space=pltpu.VMEM)],
        out_shape=[sem, sem] + [pltpu.HBM(v.shape, v.dtype) for v in ops] + [jax.ShapeDtypeStruct((8, LANES), f32)],
        input_output_aliases={i: 2 + i for i in range(2 * n)},
        compiler_params=pltpu.CompilerParams(has_side_effects=EFFECT),
    )(*ops)
    return res[0], res[1], res[2:2 + n], res[2 + n:2 + 2 * n], res[-1]


def rs_pair_wait(tag, send, recv, grads, theirs, after, halves=True):
    n = len(grads)

    def body(*refs):
        g, land = refs[:n], refs[n:2 * n]
        send_r, recv_r = refs[2 * n], refs[2 * n + 1]
        x, y, c, _ = _place()
        for a in range(n):
            cp = _rcopy(_half(g[a], 1 - c) if halves else g[a], land[a], send_r.at[a], recv_r.at[a], (x, y, 1 - c))
            cp.wait_send()
            cp.wait_recv()

    ops = list(grads) + list(theirs)
    res = pl.pallas_call(
        body, name=f"rs_pair_wait{tag}", in_specs=[HBM] * (2 * n) + [SEMS, SEMS] + [ANY] * len(after),
        out_specs=[HBM] * (2 * n), out_shape=[pltpu.HBM(v.shape, v.dtype) for v in ops],
        input_output_aliases={i: i for i in range(2 * n)},
        compiler_params=pltpu.CompilerParams(has_side_effects=EFFECT),
    )(*ops, send, recv, *after)
    return res[:n], res[n:]


def _chip_piece(ref, k):
    return _cols(ref, k, WIN_SHARD) if len(ref.shape) == 2 else ref.at[k]


def _chip_copy(a, k, src, dst, send, recv, me, c):
    return _rcopy(src, dst, send.at[a * NCHIP + k], recv.at[a * NCHIP + me], (k // 2, k % 2, c))


def rs_chips_start(tag, sums):
    n = len(sums)

    def pshape(s):
        return (NCHIP, s[0], WIN_SHARD) if len(s) == 2 else s

    def body(*refs):
        s, land = refs[:n], refs[n:2 * n]
        send, recv = refs[2 * n], refs[2 * n + 1]
        token = refs[-1]
        x, y, c, me = _place()
        for k in range(NCHIP):
            @pl.when(me != k)
            def _():
                for a in range(n):
                    _start_pieces(_chip_piece(s[a], k), land[a].at[me],
                                  lambda src, dst, a=a: _chip_copy(a, k, src, dst, send, recv, me, c))
        token[...] = jnp.zeros_like(token)

    lands = [lax.empty(pshape(v.shape), v.dtype) for v in sums]
    ops = [pltpu.with_memory_space_constraint(v, pltpu.HBM) for v in list(sums) + lands]
    sem = pltpu.SemaphoreType.DMA((n * NCHIP,))
    res = pl.pallas_call(
        body, name=f"rs_chips_start{tag}", in_specs=[HBM] * (2 * n),
        out_specs=[SEMS, SEMS] + [HBM] * (2 * n) + [pl.BlockSpec(memory_space=pltpu.VMEM)],
        out_shape=[sem, sem] + [pltpu.HBM(v.shape, v.dtype) for v in ops] + [jax.ShapeDtypeStruct((8, LANES), f32)],
        input_output_aliases={i: 2 + i for i in range(2 * n)},
        compiler_params=pltpu.CompilerParams(has_side_effects=EFFECT),
    )(*ops)
    return res[0], res[1], res[2:2 + n], res[2 + n:2 + 2 * n], res[-1]


def rs_chips_wait(tag, send, recv, sums, lands, after):
    n = len(sums)

    def body(*refs):
        s, land = refs[:n], refs[n:2 * n]
        send_r, recv_r = refs[2 * n], refs[2 * n + 1]
        x, y, c, me = _place()
        for k in range(NCHIP):
            @pl.when(me != k)
            def _():
                for a in range(n):
                    piece = _chip_piece(s[a], k)
                    _chip_copy(a, k, piece, land[a].at[me], send_r, recv_r, me, c).wait_send()
                    _rcopy(piece, land[a].at[k], send_r.at[a * NCHIP + k], recv_r.at[a * NCHIP + k],
                           (k // 2, k % 2, c)).wait_recv()

    ops = list(sums) + list(lands)
    res = pl.pallas_call(
        body, name=f"rs_chips_wait{tag}", in_specs=[HBM] * (2 * n) + [SEMS, SEMS] + [ANY] * len(after),
        out_specs=[HBM] * (2 * n), out_shape=[pltpu.HBM(v.shape, v.dtype) for v in ops],
        input_output_aliases={i: i for i in range(2 * n)},
        compiler_params=pltpu.CompilerParams(has_side_effects=EFFECT),
    )(*ops, send, recv, *after)
    return res[:n], res[n:]


def _row_tile(rows, cols, itemsize=4, target=2 << 20):
    best = 8
    for t in range(8, rows + 1, 8):
        if rows % t == 0 and t * cols * itemsize <= target:
            best = t
    return best


GRAD_WIRE = jnp.bfloat16


def add_half(name, g, t, c):
    cols, half = t.shape[-1], t.shape[-2]
    nblk = 1 if t.ndim == 2 else t.shape[0]
    tr = _row_tile(half, cols)
    per = half // tr

    def body(c_ref, g_r, t_r, o_r):
        o_r[...] = (g_r[...] + t_r[...]).astype(o_r.dtype)

    tile_t = pl.BlockSpec((tr, cols), lambda i, c_ref: (i, 0))
    tile_g = pl.BlockSpec((tr, cols), lambda i, c_ref: ((i // per) * 2 * per + c_ref[0] * per + i % per, 0))
    out = pl.pallas_call(
        body, name=name, out_shape=jax.ShapeDtypeStruct((nblk * half, cols), GRAD_WIRE),
        grid_spec=pltpu.PrefetchScalarGridSpec(num_scalar_prefetch=1, grid=(nblk * per,), in_specs=[tile_g, tile_t],
                                               out_specs=tile_t),
        compiler_params=_cp(("parallel",)),
    )(c.reshape(1).astype(jnp.int32), g.reshape(nblk * 2 * half, cols), t.reshape(nblk * half, cols))
    return out.reshape(t.shape)


def add_chips(name, land, own):
    _, rows, cols = land.shape
    tr = _row_tile(rows, cols, target=1 << 20)

    def body(land_r, own_r, o_r):
        me = 2 * lax.axis_index("x") + lax.axis_index("y")
        for k in range(NCHIP):
            @pl.when(me == k)
            def _():
                acc = None
                for j in range(NCHIP):
                    t = (own_r[...] if j == k else land_r[j]).astype(f32)
                    acc = t if acc is None else acc + t
                o_r[...] = acc

    tile = pl.BlockSpec((tr, cols), lambda i: (i, 0))
    return pl.pallas_call(
        body, name=name, grid=(rows // tr,), in_specs=[pl.BlockSpec((NCHIP, tr, cols), lambda i: (0, i, 0)), tile],
        out_specs=tile, out_shape=jax.ShapeDtypeStruct((rows, cols), f32), compiler_params=_cp(("parallel",)),
    )(land, own)


def reduce_scatter_pair(tag, G):
    names = tuple(G)
    grads = [G[n] if G[n].ndim == 3 or n == "w_in" else G[n].reshape(NCHIP, D // NCHIP, D) for n in names]
    send, recv, grads, theirs, token = rs_pair_start(tag, grads)
    return (tag, names, send, recv, grads, theirs), token[0, 0]


def reduce_scatter_chips(state, after):
    c = lax.axis_index("c")
    tag, names, send, recv, grads, theirs = state
    grads, theirs = rs_pair_wait(tag, send, recv, grads, theirs, after)
    sums = [add_half(f"rs_add_pair{tag}_{n}", g, t, c) for n, g, t in zip(names, grads, theirs)]
    send, recv, sums, lands, token = rs_chips_start(tag, sums)
    return (tag, names, send, recv, sums, lands), token[0, 0]


def reduce_scatter_finish(state, after):
    me = 2 * lax.axis_index("x") + lax.axis_index("y")
    tag, names, send, recv, sums, lands = state
    sums, landed = rs_chips_wait(tag, send, recv, sums, lands, after)
    halves = []
    for n, s, v in zip(names, sums, landed):
        own = lax.dynamic_slice_in_dim(s, me * WIN_SHARD, WIN_SHARD, axis=1) if s.ndim == 2 else \
            lax.dynamic_index_in_dim(s, me, 0, keepdims=False)
        halves.append(add_chips(f"rs_add_chips{tag}_{n}", v, own))
    send, recv, halves, others, _ = rs_pair_start("_join" + tag, halves, halves=False)
    return tag, names, send, recv, halves, others


def reduce_scatter_join(state, after):
    tag, names, send, recv, halves, others = state
    halves, others = rs_pair_wait("_join" + tag, send, recv, halves, others, after, halves=False)
    return dict(zip(names, zip(halves, others)))


NDEV = 8


def _small_copy(r, src, dst, send, recv, x, y, c):
    return _rcopy(src, dst, send.at[r - 1], recv.at[r - 1], (x ^ (r >> 2), y ^ ((r >> 1) & 1), c ^ (r & 1)))


def small_start(pack):
    def body(p, land, send, recv, p_thru, land_thru, token):
        x, y, c, _ = _place()
        me = 4 * x + 2 * y + c
        for r in range(1, NDEV):
            _start_pieces(p, land.at[me], lambda s, d, r=r: _small_copy(r, s, d, send, recv, x, y, c), 128 << 10)
        token[...] = jnp.zeros_like(token)

    ops = [pltpu.with_memory_space_constraint(v, pltpu.HBM) for v in (pack, lax.empty((NDEV,) + pack.shape, f32))]
    sem = pltpu.SemaphoreType.DMA((NDEV - 1,))
    return pl.pallas_call(
        body, name="small_start", in_specs=[HBM, HBM],
        out_specs=[SEMS, SEMS, HBM, HBM, pl.BlockSpec(memory_space=pltpu.VMEM)],
        out_shape=[sem, sem] + [pltpu.HBM(v.shape, v.dtype) for v in ops] + [jax.ShapeDtypeStruct((8, LANES), f32)],
        input_output_aliases={0: 2, 1: 3}, compiler_params=pltpu.CompilerParams(has_side_effects=EFFECT),
    )(*ops)


def small_wait(send, recv, pack, land, after):
    def body(p, land_r, send_r, recv_r, *rest):
        x, y, c, _ = _place()
        me = 4 * x + 2 * y + c
        for r in range(1, NDEV):
            _small_copy(r, p, land_r.at[me], send_r, recv_r, x, y, c).wait_send()
            src = 4 * (x ^ (r >> 2)) + 2 * (y ^ ((r >> 1) & 1)) + (c ^ (r & 1))
            _small_copy(r, p, land_r.at[src], send_r, recv_r, x, y, c).wait_recv()

    return pl.pallas_call(
        body, name="small_wait", in_specs=[HBM, HBM, SEMS, SEMS] + [ANY] * len(after), out_specs=[HBM, HBM],
        out_shape=[pltpu.HBM(pack.shape, f32), pltpu.HBM(land.shape, f32)], input_output_aliases={0: 0, 1: 1},
        compiler_params=pltpu.CompilerParams(has_side_effects=EFFECT),
    )(pack, land, send, recv, *after)


def small_sum(land, pack):
    def body(land_r, p_r, o_r):
        me = 4 * lax.axis_index("x") + 2 * lax.axis_index("y") + lax.axis_index("c")
        for k in range(NDEV):
            @pl.when(me == k)
            def _():
                acc = None
                for d in range(NDEV):
                    t = p_r[...] if d == k else land_r[d]
                    acc = t if acc is None else acc + t
                o_r[...] = acc

    vm = pl.BlockSpec(memory_space=pltpu.VMEM)
    return pl.pallas_call(
        body, name="small_sum", in_specs=[vm, vm], out_specs=vm, out_shape=jax.ShapeDtypeStruct(pack.shape, f32),
        compiler_params=pltpu.CompilerParams(vmem_limit_bytes=40 << 20),
    )(land, pack)


def _adamw_math(w, g, m, v):
    m = ADAM_B1 * m + (1.0 - ADAM_B1) * g
    v = ADAM_B2 * v + (1.0 - ADAM_B2) * (g * g)
    m_hat = m / (1.0 - ADAM_B1 ** ADAM_STEP)
    v_hat = v / (1.0 - ADAM_B2 ** ADAM_STEP)
    return -ADAM_LR * (m_hat / (jnp.sqrt(v_hat) + ADAM_EPS) + ADAM_WD * w), m, v


def adamw_big(name, halves, w, m, v):
    _, R, C = w.shape
    tr = _row_tile(R // 2, C, target=1 << 20)
    nt = R // 2 // tr

    def body(a0, b0, a1, b1, w_r, m_r, v_r, g_o, d_o, m_o, v_o):
        mine = pl.program_id(1) == lax.axis_index("c")
        g = jnp.where(pl.program_id(0) == 0, jnp.where(mine, a0[...], b0[...]), jnp.where(mine, a1[...], b1[...]))
        g_o[...] = g
        d_o[...], m_o[...], v_o[...] = _adamw_math(w_r[...], g, m_r[...], v_r[...])

    stk = pl.BlockSpec((None, tr, C), lambda l, h, i: (l, h * nt + i, 0))
    lay0 = pl.BlockSpec((tr, C), lambda l, h, i: (jnp.where(l == 0, i, nt - 1), 0))
    lay1 = pl.BlockSpec((tr, C), lambda l, h, i: (jnp.where(l == 0, 0, i), 0))
    return pl.pallas_call(
        body, name=name, grid=(DEPTH, 2, nt),
        in_specs=[lay0, lay0, lay1, lay1, stk, stk, stk],
        out_specs=[stk] * 4, out_shape=[jax.ShapeDtypeStruct(w.shape, f32)] * 4,
        compiler_params=_cp(("arbitrary", "arbitrary", "arbitrary")),
    )(*halves[0], *halves[1], w, m, v)


def adamw_small(name, g, w, m, v):
    def body(g_r, w_r, m_r, v_r, d_o, m_o, v_o):
        d_o[...], m_o[...], v_o[...] = _adamw_math(w_r[...], g_r[...], m_r[...], v_r[...])

    return pl.pallas_call(body, name=name, out_shape=[jax.ShapeDtypeStruct(w.shape, f32)] * 3)(g, w, m, v)


WEIGHTS = ("w_in", "conv_w", "gmlp_ln_g", "gmlp_ln_b", "w_s", "b_s", "p_a", "p_b", "p_c", "w_o", "ln1_g", "ln1_b",
           "w_gate", "w_up", "w_down", "ln2_g", "ln2_b")
VECS = ("ln1_g", "ln1_b", "ln2_g", "ln2_b", "gmlp_ln_g", "gmlp_ln_b")
ROWS_VEC, ROWS_BS, ROWS_WS, ROWS_CONV = D // LANES, 8, 8 * BLK, 3 * D // LANES
ROWS_LAYER = len(VECS) * ROWS_VEC + ROWS_BS + ROWS_WS + ROWS_CONV


def _pack_small(per_layer, tail):
    parts = []
    for P in per_layer:
        parts += [P[n].reshape(ROWS_VEC, LANES) for n in VECS]
        parts += [P["b_s"].reshape(ROWS_BS, LANES), P["w_s"].reshape(ROWS_WS, LANES), P["conv_w"].reshape(ROWS_CONV, LANES)]
    return jnp.concatenate(parts + [tail], axis=0)


def _unpack_small(pack):
    out = []
    for l in range(DEPTH):
        r = l * ROWS_LAYER
        P = {}
        for n in VECS:
            P[n] = pack[r:r + ROWS_VEC].reshape(D)
            r += ROWS_VEC
        P["b_s"] = pack[r:r + ROWS_BS].reshape(8, BLK)
        r += ROWS_BS
        P["w_s"] = pack[r:r + ROWS_WS].reshape(8, BLK, BLK)
        r += ROWS_WS
        P["conv_w"] = pack[r:r + ROWS_CONV].reshape(3, D)
        out.append(P)
    return out, pack[DEPTH * ROWS_LAYER:]


def kernel(x, positions, w_in, conv_w, gmlp_ln_g, gmlp_ln_b, w_s, b_s, p_a, p_b, p_c, w_o, ln1_g, ln1_b, w_gate, w_up, w_down, ln2_g, ln2_b, loss_target, m_w_in, m_conv_w, m_gmlp_ln_g, m_gmlp_ln_b, m_w_s, m_b_s, m_p_a, m_p_b, m_p_c, m_w_o, m_ln1_g, m_ln1_b, m_w_gate, m_w_up, m_w_down, m_ln2_g, m_ln2_b, v_w_in, v_conv_w, v_gmlp_ln_g, v_gmlp_ln_b, v_w_s, v_b_s, v_p_a, v_p_b, v_p_c, v_w_o, v_ln1_g, v_ln1_b, v_w_gate, v_w_up, v_w_down, v_ln2_g, v_ln2_b):
    Wt = dict(w_in=w_in, conv_w=conv_w, gmlp_ln_g=gmlp_ln_g, gmlp_ln_b=gmlp_ln_b, w_s=w_s, b_s=b_s, p_a=p_a, p_b=p_b,
              p_c=p_c, w_o=w_o, ln1_g=ln1_g, ln1_b=ln1_b, w_gate=w_gate, w_up=w_up, w_down=w_down, ln2_g=ln2_g, ln2_b=ln2_b)
    Mt = dict(w_in=m_w_in, conv_w=m_conv_w, gmlp_ln_g=m_gmlp_ln_g, gmlp_ln_b=m_gmlp_ln_b, w_s=m_w_s, b_s=m_b_s, p_a=m_p_a,
              p_b=m_p_b, p_c=m_p_c, w_o=m_w_o, ln1_g=m_ln1_g, ln1_b=m_ln1_b, w_gate=m_w_gate, w_up=m_w_up,
              w_down=m_w_down, ln2_g=m_ln2_g, ln2_b=m_ln2_b)
    Vt = dict(w_in=v_w_in, conv_w=v_conv_w, gmlp_ln_g=v_gmlp_ln_g, gmlp_ln_b=v_gmlp_ln_b, w_s=v_w_s, b_s=v_b_s, p_a=v_p_a,
              p_b=v_p_b, p_c=v_p_c, w_o=v_w_o, ln1_g=v_ln1_g, ln1_b=v_ln1_b, w_gate=v_w_gate, w_up=v_w_up,
              w_down=v_w_down, ln2_g=v_ln2_g, ln2_b=v_ln2_b)
    chip = 2 * lax.axis_index("x") + lax.axis_index("y")
    cw = D // NCHIP

    def gathered_weights(names, arrays):
        Wl = dict(zip(names, arrays))
        for n in ("p_a", "p_c", "w_o"):
            Wl[n] = Wl[n].reshape(D, D)
        return Wl

    def small_weights(l, conv_all):
        Wl = {n: Wt[n][l] for n in VECS + ("w_s", "b_s")}
        Wl["conv_w"] = conv_all[:, l].transpose(1, 0, 2).reshape(3, D)
        return Wl

    w_in0, conv_all = gather_halves([Wt["w_in"][0].astype(MX).reshape(2, D // 2, WIN_SHARD), conv_w])
    rest = BIG[1:]
    *late0, coming0 = gather_start("0", [Wt[n][0].astype(MX) for n in rest], [conv_all])
    *late1, coming1 = gather_start("1", [Wt[n][1].astype(MX) for n in BIG], [conv_all, coming0])
    W0 = dict(small_weights(0, conv_all), w_in=w_in0.reshape(D, NIN), after=coming1[0, 0],
              late=lambda y: gathered_weights(rest, gather_wait("0", *late0, [y])))

    def W1(h):
        return dict(small_weights(1, conv_all), **gathered_weights(BIG, gather_wait("1", *late1, [h])))

    layers = [W0, W1]

    rs_state, rs_started, held = {}, {}, {}

    def start_exchange(l, g):
        if "loss" in g:
            held[l] = g
            return None
        if "conv_w" in g:
            held[l] = g
            rs_state[(l, False)], started = reduce_scatter_chips(rs_state[(l, False)], [g["w_s"], g["conv_w"]])
            if l == 0:
                pack = _pack_small([held[j] for j in range(DEPTH)], held[DEPTH]["loss"])
                *held["small"], token = small_start(pack)
                started = started + token[0, 0]
            return started
        if "dx" in g:
            rs_state[(l, True)], rs_started[(l, True)] = reduce_scatter_chips(rs_state[(l, True)], [g["dx"]])
            return rs_started[(l, True)]
        key = (l, "w_in" in g)
        rs_state[key], started = reduce_scatter_pair(f"{l}{'b' if key[1] else 'a'}", g)
        return started

    _, grad_x, _ = local_step(x[0], positions[0], loss_target[0], layers, start_exchange)

    last = jnp.zeros((8, LANES), f32) + rs_started[(0, True)]
    behind = [grad_x, last]
    red = [dict() for _ in range(DEPTH)]
    swaps = {key: reduce_scatter_finish(rs_state[key], behind) for key in ((1, False), (1, True), (0, False))}
    small, tail = _unpack_small(small_sum(*reversed(small_wait(*held["small"], behind))))
    loss = tail[0, 0]

    G, DW, NM, NV = {}, {}, {}, {}
    zc = jnp.zeros((3, D), f32)
    wp = _pack_small([{**{n: Wt[n][l] for n in VECS + ("b_s", "w_s")}, "conv_w": zc} for l in range(DEPTH)], jnp.zeros((8, LANES), f32))
    mp = _pack_small([{**{n: Mt[n][l] for n in VECS + ("b_s", "w_s")}, "conv_w": zc} for l in range(DEPTH)], jnp.zeros((8, LANES), f32))
    vp = _pack_small([{**{n: Vt[n][l] for n in VECS + ("b_s", "w_s")}, "conv_w": zc} for l in range(DEPTH)], jnp.ones((8, LANES), f32))
    gp = _pack_small(small, jnp.zeros((8, LANES), f32))
    outs = [_unpack_small(a)[0] for a in adamw_small("adamw_small", gp, wp, mp, vp)]
    for n in VECS + ("b_s", "w_s"):
        G[n] = jnp.stack([small[l][n] for l in range(DEPTH)])
        DW[n], NM[n], NV[n] = (jnp.stack([o[l][n] for l in range(DEPTH)]) for o in outs)
    gconv = jnp.stack([lax.dynamic_slice(small[l]["conv_w"], (0, chip * cw), (3, cw)) for l in range(DEPTH)])
    G["conv_w"] = gconv
    flat = lambda a: a.reshape(DEPTH * 3, cw)
    d, m2, v2 = adamw_small("adamw_conv", flat(gconv), flat(conv_w), flat(m_conv_w), flat(v_conv_w))
    DW["conv_w"], NM["conv_w"], NV["conv_w"] = (a.reshape(DEPTH, 3, cw) for a in (d, m2, v2))

    for key in swaps:
        red[key[0]].update(reduce_scatter_join(swaps[key], [d, DW["ln2_b"]]))
    updated = {}
    for n in BIG[1:]:
        tr = (lambda a: jnp.swapaxes(a, 1, 2)) if n in ("w_gate", "w_up") else (lambda a: a)
        updated[n] = adamw_big("adamw_" + n, (red[0][n], red[1][n]), tr(Wt[n]), tr(Mt[n]), tr(Vt[n]))
        G[n], DW[n], NM[n], NV[n] = map(tr, updated[n])
    done = [d, DW["ln2_b"], red[1]["w_in"][1]] + [updated[n][1] for n in BIG[1:]]
    red[0].update(reduce_scatter_join(reduce_scatter_finish(rs_state[(0, True)], done), [updated["w_o"][1]]))
    G["w_in"], DW["w_in"], NM["w_in"], NV["w_in"] = adamw_big(
        "adamw_w_in", (red[0]["w_in"], red[1]["w_in"]), Wt["w_in"], Mt["w_in"], Vt["w_in"])

    return (loss, grad_x[None], *[G[n] for n in WEIGHTS], *[DW[n] for n in WEIGHTS], *[NM[n] for n in WEIGHTS],
            *[NV[n] for n in WEIGHTS])
```

```python
import functools
import math

import jax
import jax.numpy as jnp
from jax import lax
from jax.experimental import pallas as pl
from jax.experimental.pallas import tpu as pltpu

D = 1024
NIN = 12800
DFF = 2816
NCHIP = 4
FB = DFF // NCHIP
WIN_SHARD = NIN // NCHIP
DEPTH = 2
GROUPS = ((128, 1), (512, 4), (2048, 16))
HD = 64
BLK = 128
AO = 512
ALPHA = (2 * DEPTH) ** 0.25
EPS = 1e-5
ROPE_THETA = 10000.0
LANES = 128
NEG = -1e30

C_GATES, C_BCH, C_QKV, C_UV = 0, 3 * D, 6 * D, 6 * D + 9 * AO

MX = jnp.bfloat16
ACT = jnp.bfloat16

ADAM_LR, ADAM_B1, ADAM_B2, ADAM_EPS, ADAM_WD, ADAM_STEP = 0.001, 0.9, 0.999, 1e-08, 0.01, 10

f32 = jnp.float32
NT = (((1,), (1,)), ((), ()))
TN = (((0,), (0,)), ((), ()))


def _cp(sem, vmem_mb=48):
    return pltpu.CompilerParams(dimension_semantics=sem, vmem_limit_bytes=vmem_mb << 20)


def _dot(a, b, dims=None):
    if dims is None:
        return jnp.dot(a, b, preferred_element_type=f32)
    return lax.dot_general(a, b, dims, preferred_element_type=f32)


def _ln_stats(r):
    mu = jnp.mean(r, axis=-1, keepdims=True)
    xc = r - mu
    var = jnp.mean(xc * xc, axis=-1, keepdims=True)
    rstd = lax.rsqrt(var + EPS)
    return xc * rstd, rstd


def _ln_bwd(dy, xhat, rstd, g):
    dxh = dy * g
    return rstd * (dxh - jnp.mean(dxh, axis=-1, keepdims=True) - xhat * jnp.mean(dxh * xhat, axis=-1, keepdims=True))


def _gelu(x):
    return 0.5 * x * (1.0 + lax.erf(x * (1.0 / math.sqrt(2.0))))


def _gelu_and_grad(x):
    cdf = 0.5 * (1.0 + lax.erf(x * (1.0 / math.sqrt(2.0))))
    return x * cdf, cdf + x * jnp.exp(-0.5 * x * x) * (1.0 / math.sqrt(2.0 * math.pi))


def _sigmoid(x):
    return 0.5 * jnp.tanh(0.5 * x) + 0.5


def _acc_rows(o_ref, first, val):
    @pl.when(first)
    def _():
        o_ref[...] = jnp.zeros_like(o_ref)
    o_ref[...] += jnp.sum(val, axis=0, keepdims=True)


def mm_in(x, w, bias):
    T = x.shape[0]
    tm, tn = min(2048, T), 1280

    def body(x_ref, w_ref, b_ref, o_ref, xb):
        @pl.when(pl.program_id(1) == 0)
        def _():
            xb[...] = x_ref[...].astype(MX)
        o_ref[...] = (_dot(xb[...], w_ref[...]) + b_ref[...]).astype(o_ref.dtype)

    return pl.pallas_call(
        body, name="mm_in", grid=(T // tm, NIN // tn),
        in_specs=[pl.BlockSpec((tm, D), lambda i, j: (i, 0), pipeline_mode=pl.Buffered(1)),
                  pl.BlockSpec((D, tn), lambda i, j: (0, j)), pl.BlockSpec((1, tn), lambda i, j: (0, j))],
        out_specs=pl.BlockSpec((tm, tn), lambda i, j: (i, j)),
        out_shape=jax.ShapeDtypeStruct((T, NIN), ACT),
        scratch_shapes=[pltpu.VMEM((tm, D), MX)],
        compiler_params=_cp(("parallel", "arbitrary")),
    )(x, w, bias)


HALO = 16
TM_AC = 512


def _uv_specs():
    return [pl.BlockSpec((TM_AC, 512), functools.partial(lambda i, j: (i, j), j=C_UV // 512 + j)) for j in range(4)]


def _gmlp_fwd(up, vp, ws_ref, bs_ref, lg, lb, u=None, gv=None):
    u = _gelu(up) if u is None else u
    xhat, rstd = _ln_stats(_gelu(vp) if gv is None else gv)
    vn = xhat * lg + lb
    vnb = vn.astype(MX)
    rows = []
    for c in range(up.shape[0] // BLK):
        r = slice(c * BLK, (c + 1) * BLK)
        rows.append(jnp.concatenate(
            [_dot(ws_ref[g], vnb[r, g * BLK:(g + 1) * BLK]) + bs_ref[g] for g in range(8)], axis=1))
    return u, vn, xhat, rstd, jnp.concatenate(rows, axis=0)


def mix_ac_fwd(proj, conv_w, wst, bsx, lg, lb):
    T = proj.shape[0]
    tm = TM_AC

    def body(bch, halo, u0, u1, v0, v1, cw, ws, bs, lg_ref, lb_ref, ya, yc, zs):
        i = pl.program_id(0)
        pb = bch[...].astype(f32)
        z = pb[:, D:2 * D] * pb[:, 2 * D:]
        hz = halo[:, :D].astype(f32) * halo[:, D:].astype(f32)
        zs[0:HALO, :] = jnp.where(i > 0, hz, 0.0)
        zs[HALO:HALO + tm, :] = z
        cv = cw[0:1, :] * zs[HALO - 2:HALO - 2 + tm, :] + cw[1:2, :] * zs[HALO - 1:HALO - 1 + tm, :] + cw[2:3, :] * z
        ya[...] = (pb[:, :D] * cv).astype(ya.dtype)
        up = jnp.concatenate([u0[...], u1[...]], axis=1).astype(f32)
        vp = jnp.concatenate([v0[...], v1[...]], axis=1).astype(f32)
        u, _, _, _, sp = _gmlp_fwd(up, vp, ws, bs, lg_ref[...], lb_ref[...])
        yc[...] = (u * sp).astype(yc.dtype)

    full = lambda shape: pl.BlockSpec(shape, lambda i: (0,) * len(shape))
    return pl.pallas_call(
        body, name="mix_ac_fwd", grid=(T // tm,),
        in_specs=[pl.BlockSpec((tm, 3 * D), lambda i: (i, 1)),
                  pl.BlockSpec((HALO, 2 * D), lambda i: (jnp.maximum(i * (tm // HALO) - 1, 0), 2)),
                  *_uv_specs(), full((3, D)), full((8, BLK, BLK)), full((8, BLK, BLK)), full((1, D)), full((1, D))],
        out_specs=[pl.BlockSpec((tm, D), lambda i: (i, 0))] * 2,
        out_shape=[jax.ShapeDtypeStruct((T, D), MX)] * 2,
        scratch_shapes=[pltpu.VMEM((HALO + tm, D), f32)],
        compiler_params=_cp(("parallel",)),
    )(proj, proj, proj, proj, proj, proj, conv_w, wst, bsx, lg, lb)


def _swap_halves(x):
    lane = lax.broadcasted_iota(jnp.int32, x.shape, 1)
    return jnp.where((lane % HD) < HD // 2, pltpu.roll(x, x.shape[1] - HD // 2, 1), pltpu.roll(x, HD // 2, 1))


def _tile4(t):
    return jnp.concatenate([t] * (AO // LANES), axis=1)


TM_FOLD = 1024


def _fold_out(nat, x, out_ref, d):
    if d == 1:
        out_ref[0] = x.astype(out_ref.dtype)
        return
    rows = x.shape[0] // d
    for j in range(AO // LANES):
        nat[j] = x[:, j * LANES:(j + 1) * LANES]
    for r in range(d):
        out_ref[r] = jnp.concatenate(
            [nat.at[j][pl.ds(r, rows, stride=d), :] for j in range(AO // LANES)], axis=1).astype(out_ref.dtype)


def _unfold_in(nat, in_ref, d):
    if d == 1:
        return in_ref[0].astype(f32)
    rows = in_ref.shape[1]
    for r in range(d):
        v = in_ref[r].astype(f32)
        for j in range(AO // LANES):
            nat.at[j][pl.ds(r, rows, stride=d), :] = v[:, j * LANES:(j + 1) * LANES]
    return jnp.concatenate([nat[j] for j in range(AO // LANES)], axis=1)


def fold_rope(proj, cos_t, sin_t, g, d):
    T = proj.shape[0]
    tm = TM_FOLD
    rows = tm // d

    def body(x_ref, c_ref, s_ref, q_o, k_o, v_o, nat):
        cos, sin = _tile4(c_ref[...]), _tile4(s_ref[...])
        for part, out, scale in ((0, q_o, HD ** -0.5), (1, k_o, 1.0), (2, v_o, None)):
            x = x_ref[:, part * AO:(part + 1) * AO].astype(f32)
            if scale is not None:
                x = (x * cos + _swap_halves(x) * sin) * scale
            _fold_out(nat, x, out, d)

    fold_spec = pl.BlockSpec((d, rows, AO), lambda i: (0, i, 0))
    return pl.pallas_call(
        body, name=f"fold_rope{g}", grid=(T // tm,),
        in_specs=[pl.BlockSpec((tm, 3 * AO), lambda i: (i, C_QKV // (3 * AO) + g)),
                  pl.BlockSpec((tm, LANES), lambda i: (i, 0)), pl.BlockSpec((tm, LANES), lambda i: (i, 0))],
        out_specs=[fold_spec] * 3,
        out_shape=[jax.ShapeDtypeStruct((d, T // d, AO), MX)] * 3,
        scratch_shapes=[pltpu.VMEM((AO // LANES, tm, LANES), f32)],
        compiler_params=_cp(("parallel",)),
    )(proj, cos_t, sin_t)


def _stack_heads(x):
    lane = lax.broadcasted_iota(jnp.int32, x.shape, 1)
    z = jnp.zeros_like(x)
    return jnp.concatenate([jnp.where(lane < HD, x, z), jnp.where(lane >= HD, x, z)], axis=0)


def _unstack_heads(y):
    lane = lax.broadcasted_iota(jnp.int32, (BLK, LANES), 1)
    return jnp.where(lane < HD, y[:BLK], y[BLK:])


def _window_masks():
    row = lax.broadcasted_iota(jnp.int32, (2 * BLK, 2 * BLK), 0) % BLK
    col = lax.broadcasted_iota(jnp.int32, (2 * BLK, 2 * BLK), 1)
    return (col < BLK) & (col >= row), (col >= BLK) & (col - BLK <= row)


def _two_blocks(ref, b):
    r0 = pl.multiple_of(b * BLK, BLK)
    rp = pl.multiple_of(jnp.maximum(b - 1, 0) * BLK, BLK)
    return jnp.concatenate([ref[pl.ds(rp, BLK), :], ref[pl.ds(r0, BLK), :]], axis=0)


def _merge_masks():
    row = lax.broadcasted_iota(jnp.int32, (2 * BLK, BLK), 0) % BLK
    col = lax.broadcasted_iota(jnp.int32, (2 * BLK, BLK), 1)
    return col <= row, col == row


def attn_fwd(qf, kf, vf, g, nb):
    T = qf.shape[0]

    def body(q_ref, k_ref, v_ref, o_ref, l_ref):
        cur_m, own_m = _merge_masks()

        def step(b, carry):
            r0 = pl.multiple_of(b * BLK, BLK)
            rp = pl.multiple_of(jnp.maximum(b - 1, 0) * BLK, BLK)
            qs = _stack_heads(q_ref[pl.ds(r0, BLK), :])
            vc, vp = v_ref[pl.ds(r0, BLK), :], v_ref[pl.ds(rp, BLK), :]
            sp = jnp.where((b % nb) != 0, _dot(qs, k_ref[pl.ds(rp, BLK), :], NT), NEG)
            s = jnp.where(cur_m, _dot(qs, k_ref[pl.ds(r0, BLK), :], NT), sp)
            s_own = jnp.sum(jnp.where(own_m, sp, 0.0), axis=-1, keepdims=True)
            m = jnp.maximum(jnp.max(s, axis=-1, keepdims=True), s_own)
            p, p_own = jnp.exp(s - m), jnp.exp(s_own - m)
            l = jnp.sum(p, axis=-1, keepdims=True) + p_own
            pb = p.astype(MX)
            zero = jnp.zeros_like(pb)
            o = _dot(jnp.where(cur_m, pb, zero), vc) + _dot(jnp.where(cur_m, zero, pb), vp)
            o = (o + p_own * jnp.concatenate([vp, vp], axis=0).astype(f32)) / l
            o_ref[pl.ds(r0, BLK), :] = _unstack_heads(o).astype(o_ref.dtype)
            l_ref[pl.ds(r0, BLK), :] = _unstack_heads(jnp.broadcast_to(m + jnp.log(l), (2 * BLK, LANES)))
            return carry

        lax.fori_loop(0, T // BLK, step, 0, unroll=8)

    spec = pl.BlockSpec((T, LANES), lambda j: (0, j))
    return pl.pallas_call(
        body, name=f"attn_fwd{g}", grid=(AO // LANES,),
        in_specs=[spec] * 3, out_specs=[spec] * 2,
        out_shape=[jax.ShapeDtypeStruct((T, AO), ACT), jax.ShapeDtypeStruct((T, AO), f32)],
        compiler_params=_cp(("parallel",), 56),
    )(qf, kf, vf)


def _group_weights(lses):
    m = jnp.maximum(jnp.maximum(lses[0], lses[1]), lses[2])
    e = [jnp.exp(l - m) for l in lses]
    inv = 1.0 / (e[0] + e[1] + e[2])
    return [x * inv for x in e]


def _fold_specs(T, tm):
    specs = []
    for _, d in GROUPS:
        specs.append(pl.BlockSpec((d, tm // d, AO), lambda i: (0, i, 0)))
    return specs


def combine_fwd(os_, lses):
    T = os_[0].shape[0] * os_[0].shape[1]
    tm = TM_FOLD

    def body(o0, o1, o2, l0, l1, l2, y_ref, nat):
        o = [_unfold_in(nat, r, d) for r, (_, d) in zip((o0, o1, o2), GROUPS)]
        ls = [_unfold_in(nat, r, d) for r, (_, d) in zip((l0, l1, l2), GROUPS)]
        w = _group_weights(ls)
        y_ref[...] = (w[0] * o[0] + w[1] * o[1] + w[2] * o[2]).astype(y_ref.dtype)

    specs = _fold_specs(T, tm)
    return pl.pallas_call(
        body, name="combine_fwd", grid=(T // tm,),
        in_specs=specs + specs, out_specs=pl.BlockSpec((tm, AO), lambda i: (i, 0)),
        out_shape=jax.ShapeDtypeStruct((T, AO), MX),
        scratch_shapes=[pltpu.VMEM((AO // LANES, tm, LANES), f32)],
        compiler_params=_cp(("parallel",)),
    )(*os_, *lses)


TM_MIX = 512


def mix_out_fwd(proj, ya, yb, yc, x0, pa, pb, pc, wo, g1, b1):
    T = x0.shape[0]
    tm = min(TM_MIX, T)

    def body(gt, ya_r, yb_r, yc_r, x0_r, pa_r, pb_r, pc_r, wo_r, g_r, b_r, mabc, m_o, r1_o, x1_o, x1b_o):
        ma = _dot(ya_r[...], pa_r[...])
        ybv = yb_r[...]
        mb = jnp.concatenate([_dot(ybv, pb_r[k]) for k in range(NCHIP)], axis=1)
        mc = _dot(yc_r[...], pc_r[...])
        m = jnp.zeros((tm, D), f32)
        for j, mm in enumerate((ma, mb, mc)):
            mabc[:, j * D:(j + 1) * D] = mm.astype(mabc.dtype)
            m = m + _sigmoid(gt[:, j * D:(j + 1) * D].astype(f32)) * mm
        mb16 = m.astype(MX)
        m_o[...] = mb16
        r1 = ALPHA * x0_r[...] + _dot(mb16, wo_r[...])
        r1_o[...] = r1
        xhat, _ = _ln_stats(r1)
        x1 = xhat * g_r[...] + b_r[...]
        x1_o[...] = x1
        x1b_o[...] = x1.astype(MX)

    full = lambda shape: pl.BlockSpec(shape, lambda i: (0,) * len(shape), pipeline_mode=pl.Buffered(1))
    tile = lambda w: pl.BlockSpec((tm, w), lambda i: (i, 0))
    return pl.pallas_call(
        body, name="mix_out_fwd", grid=(T // tm,),
        in_specs=[tile(3 * D), tile(D), tile(AO), tile(D), tile(D), full((D, D)), full((NCHIP, AO, D // NCHIP)),
                  full((D, D)), full((D, D)), full((1, D)), full((1, D))],
        out_specs=[tile(3 * D), tile(D), tile(D), tile(D), tile(D)],
        out_shape=[jax.ShapeDtypeStruct((T, 3 * D), MX), jax.ShapeDtypeStruct((T, D), MX),
                   jax.ShapeDtypeStruct((T, D), f32), jax.ShapeDtypeStruct((T, D), f32), jax.ShapeDtypeStruct((T, D), MX)],
        compiler_params=_cp(("parallel",), 56),
    )(proj, ya, yb, yc, x0, pa, pb, pc, wo, g1, b1)


TM_FF = 512
TM_FFB = 512
ROW_CHUNK = 64


def ffn_up_fwd(x1, wg, wu):
    T = x1.shape[0]
    tm = min(TM_FFB, T)

    def body(x_r, wg_r, wu_r, g_o, u_o, h_o, gs, us):
        xb = x_r[...].astype(MX)
        for k in range(NCHIP):
            gs[...] = _dot(xb, wg_r[k])
            us[...] = _dot(xb, wu_r[k])
            for r in range(0, tm, ROW_CHUNK):
                rows = pl.ds(r, ROW_CHUNK)
                gate, up = gs[rows, :], us[rows, :]
                g_o[k, rows, :] = gate.astype(g_o.dtype)
                u_o[k, rows, :] = up.astype(u_o.dtype)
                h_o[k, rows, :] = (gate * _sigmoid(gate) * up).astype(h_o.dtype)

    wspec = pl.BlockSpec((NCHIP, D, FB), lambda i: (0, 0, 0), pipeline_mode=pl.Buffered(1))
    ospec = pl.BlockSpec((NCHIP, tm, FB), lambda i: (0, i, 0))
    return pl.pallas_call(
        body, name="ffn_up_fwd", grid=(T // tm,),
        in_specs=[pl.BlockSpec((tm, D), lambda i: (i, 0)), wspec, wspec],
        out_specs=[ospec] * 3,
        out_shape=[jax.ShapeDtypeStruct((NCHIP, T, FB), ACT)] * 2 + [jax.ShapeDtypeStruct((NCHIP, T, FB), MX)],
        scratch_shapes=[pltpu.VMEM((tm, FB), f32)] * 2,
        compiler_params=_cp(("parallel",)),
    )(x1, wg, wu)


def ffn_down_fwd(hh, wd, x1, g2, b2):
    T = x1.shape[0]
    tm = min(TM_FF, T)

    def body(h_r, w_r, x_r, g_r, b_r, r2_o, x2_o):
        r2 = ALPHA * x_r[...]
        for k in range(NCHIP):
            r2 = r2 + _dot(h_r[k], w_r[k])
        r2_o[...] = r2
        xhat, _ = _ln_stats(r2)
        x2_o[...] = xhat * g_r[...] + b_r[...]

    tile = pl.BlockSpec((tm, D), lambda i: (i, 0))
    vec = pl.BlockSpec((1, D), lambda i: (0, 0))
    return pl.pallas_call(
        body, name="ffn_down_fwd", grid=(T // tm,),
        in_specs=[pl.BlockSpec((NCHIP, tm, FB), lambda i: (0, i, 0)), pl.BlockSpec((NCHIP, FB, D), lambda i: (0, 0, 0), pipeline_mode=pl.Buffered(1)),
                  tile, vec, vec],
        out_specs=[tile, tile], out_shape=[jax.ShapeDtypeStruct((T, D), f32)] * 2,
        compiler_params=_cp(("parallel",)),
    )(hh, wd, x1, g2, b2)


def loss_grad(y, tgt):
    T = y.shape[0]
    tm = min(512, T)

    def body(y_r, t_r, l_o, dy_o):
        e = y_r[...] - t_r[...]
        dy_o[...] = e * (1.0 / D)

        @pl.when(pl.program_id(0) == 0)
        def _():
            l_o[...] = jnp.zeros_like(l_o)
        l_o[...] += (0.5 / D) * jnp.sum(e * e)

    tile = pl.BlockSpec((tm, D), lambda i: (i, 0))
    return pl.pallas_call(
        body, name="loss_grad", grid=(T // tm,),
        in_specs=[tile, tile], out_specs=[pl.BlockSpec((8, LANES), lambda i: (0, 0)), tile],
        out_shape=[jax.ShapeDtypeStruct((8, LANES), f32), jax.ShapeDtypeStruct((T, D), f32)],
        compiler_params=_cp(("arbitrary",)),
    )(y, tgt)


def ffn_down_bwd(dx2, r2, g2, wd, gate, up):
    T = dx2.shape[0]
    tm = min(TM_FFB, T)

    def body(dx_r, r_r, g_r, w_r, ga_r, up_r, dr_o, drb_o, dg_o, du_o, dlg_o, dlb_o, hs):
        i = pl.program_id(0)
        xhat, rstd = _ln_stats(r_r[...])
        dx = dx_r[...]
        _acc_rows(dlg_o, i == 0, dx * xhat)
        _acc_rows(dlb_o, i == 0, dx)
        dr = _ln_bwd(dx, xhat, rstd, g_r[...])
        dr_o[...] = dr
        drb = dr.astype(MX)
        drb_o[...] = drb
        for k in range(NCHIP):
            hs[...] = _dot(drb, w_r[k], NT)
            for r in range(0, tm, ROW_CHUNK):
                rows = pl.ds(r, ROW_CHUNK)
                dhh, gate_v, up_v = hs[rows, :], ga_r[k, rows, :].astype(f32), up_r[k, rows, :].astype(f32)
                sg = _sigmoid(gate_v)
                dg_o[k, rows, :] = (dhh * up_v * sg * (1.0 + gate_v * (1.0 - sg))).astype(dg_o.dtype)
                du_o[k, rows, :] = (dhh * gate_v * sg).astype(du_o.dtype)

    tile = pl.BlockSpec((tm, D), lambda i: (i, 0))
    vec = pl.BlockSpec((1, D), lambda i: (0, 0))
    blk = pl.BlockSpec((NCHIP, tm, FB), lambda i: (0, i, 0))
    return pl.pallas_call(
        body, name="ffn_down_bwd", grid=(T // tm,),
        in_specs=[tile, tile, vec, pl.BlockSpec((NCHIP, FB, D), lambda i: (0, 0, 0), pipeline_mode=pl.Buffered(1)), blk, blk],
        out_specs=[tile, tile, blk, blk, vec, vec],
        out_shape=[jax.ShapeDtypeStruct((T, D), f32), jax.ShapeDtypeStruct((T, D), MX)]
        + [jax.ShapeDtypeStruct((NCHIP, T, FB), MX)] * 2 + [jax.ShapeDtypeStruct((1, D), f32)] * 2,
        scratch_shapes=[pltpu.VMEM((tm, FB), f32)],
        compiler_params=_cp(("arbitrary",), 58),
    )(dx2, r2, g2, wd, gate, up)


def ffn_up_bwd(dr2, dgate, dup, wg, wu, r1, g1):
    T = dr2.shape[0]
    tm = min(TM_FFB, T)

    def body(dr2_r, dg_r, du_r, wg_r, wu_r, r1_r, g_r, dr1_o, dr1b_o, dlg_o, dlb_o):
        i = pl.program_id(0)
        dx = ALPHA * dr2_r[...]
        for k in range(NCHIP):
            dx = dx + _dot(dg_r[k], wg_r[k], NT) + _dot(du_r[k], wu_r[k], NT)
        xhat, rstd = _ln_stats(r1_r[...])
        _acc_rows(dlg_o, i == 0, dx * xhat)
        _acc_rows(dlb_o, i == 0, dx)
        dr1 = _ln_bwd(dx, xhat, rstd, g_r[...])
        dr1_o[...] = dr1
        dr1b_o[...] = dr1.astype(MX)

    tile = pl.BlockSpec((tm, D), lambda i: (i, 0))
    vec = pl.BlockSpec((1, D), lambda i: (0, 0))
    blk = pl.BlockSpec((NCHIP, tm, FB), lambda i: (0, i, 0))
    wspec = pl.BlockSpec((NCHIP, D, FB), lambda i: (0, 0, 0), pipeline_mode=pl.Buffered(1))
    return pl.pallas_call(
        body, name="ffn_up_bwd", grid=(T // tm,),
        in_specs=[tile, blk, blk, wspec, wspec, tile, vec],
        out_specs=[tile, tile, vec, vec],
        out_shape=[jax.ShapeDtypeStruct((T, D), f32), jax.ShapeDtypeStruct((T, D), MX)]
        + [jax.ShapeDtypeStruct((1, D), f32)] * 2,
        compiler_params=_cp(("arbitrary",), 58),
    )(dr2, dgate, dup, wg, wu, r1, g1)


def mix_out_bwd(dr1, proj, mabc, wo, pa, pb, pc):
    T = dr1.shape[0]
    tm = min(TM_MIX, T)

    def body(dr_r, gt, mabc_r, wo_r, pa_r, pb_r, pc_r, dmabc_o, dgt_o, dya_o, dyb_o, dyc_o):
        dm = _dot(dr_r[...].astype(MX), wo_r[...], NT)
        dmx = []
        for j in range(3):
            s = _sigmoid(gt[:, j * D:(j + 1) * D].astype(f32))
            v = (dm * s).astype(MX)
            dmx.append(v)
            dmabc_o[:, j * D:(j + 1) * D] = v
            dgt_o[:, j * D:(j + 1) * D] = (dm * mabc_r[:, j * D:(j + 1) * D].astype(f32) * s * (1.0 - s)).astype(dgt_o.dtype)
        dya_o[...] = _dot(dmx[0], pa_r[...], NT).astype(dya_o.dtype)
        dyb = jnp.zeros((tm, AO), f32)
        for k in range(NCHIP):
            dyb = dyb + _dot(dmx[1][:, k * (D // NCHIP):(k + 1) * (D // NCHIP)], pb_r[k], NT)
        dyb_o[...] = dyb.astype(dyb_o.dtype)
        dyc_o[...] = _dot(dmx[2], pc_r[...], NT).astype(dyc_o.dtype)

    full = lambda shape: pl.BlockSpec(shape, lambda i: (0,) * len(shape), pipeline_mode=pl.Buffered(1))
    tile = lambda w: pl.BlockSpec((tm, w), lambda i: (i, 0))
    return pl.pallas_call(
        body, name="mix_out_bwd", grid=(T // tm,),
        in_specs=[tile(D), tile(3 * D), tile(3 * D), full((D, D)), full((D, D)), full((NCHIP, AO, D // NCHIP)), full((D, D))],
        out_specs=[tile(3 * D), tile(3 * D), tile(D), tile(AO), tile(D)],
        out_shape=[jax.ShapeDtypeStruct((T, 3 * D), MX), jax.ShapeDtypeStruct((T, 3 * D), MX),
                   jax.ShapeDtypeStruct((T, D), ACT), jax.ShapeDtypeStruct((T, AO), ACT), jax.ShapeDtypeStruct((T, D), ACT)],
        compiler_params=_cp(("parallel",), 56),
    )(dr1, proj, mabc, wo, pa, pb, pc)


def transpose_cast(x):
    T = x.shape[0]
    tm = min(512, T)

    def body(x_r, o_r):
        o_r[...] = x_r[...].T.astype(o_r.dtype)

    return pl.pallas_call(
        body, name="transpose_cast", grid=(T // tm,),
        in_specs=[pl.BlockSpec((tm, D), lambda i: (i, 0))], out_specs=pl.BlockSpec((D, tm), lambda i: (0, i)),
        out_shape=jax.ShapeDtypeStruct((D, T), MX), compiler_params=_cp(("parallel",)),
    )(x)


def tn_matmul(name, a, b, a_spec, b_spec, out_shape, out_spec, grid):
    nt = len(grid) - 1

    def body(a_r, b_r, o_r):
        @pl.when(pl.program_id(nt) == 0)
        def _():
            o_r[...] = jnp.zeros_like(o_r)
        av = a_r[...].reshape(a_r.shape[-2:]).astype(MX)
        bv = b_r[...].reshape(b_r.shape[-2:]).astype(MX)
        o_r[...] += _dot(av, bv, TN).reshape(o_r.shape)

    return pl.pallas_call(
        body, name=name, grid=grid, in_specs=[a_spec, b_spec], out_specs=out_spec,
        out_shape=jax.ShapeDtypeStruct(out_shape, f32),
        compiler_params=_cp(("parallel",) * nt + ("arbitrary",), 56),
    )(a, b)


def attn_pre_bwd(dyb, os_, lses, ones):
    T = dyb.shape[0]
    tm = TM_FOLD

    def body(dy_r, o0, o1, o2, l0, l1, l2, ones_r, d0, d1, d2, f0, f1, f2, nat):
        o = [_unfold_in(nat, r, d) for r, (_, d) in zip((o0, o1, o2), GROUPS)]
        ls = [_unfold_in(nat, r, d) for r, (_, d) in zip((l0, l1, l2), GROUPS)]
        w = _group_weights(ls)
        dy = dy_r[...].astype(f32)
        t = dy * (w[0] * o[0] + w[1] * o[1] + w[2] * o[2])
        hi = t.astype(MX)
        lo = (t - hi.astype(f32)).astype(MX)
        c = _dot(hi, ones_r[...]) + _dot(lo, ones_r[...])
        for wg, do_o, df_o, (_, d) in zip(w, (d0, d1, d2), (f0, f1, f2), GROUPS):
            _fold_out(nat, wg * dy, do_o, d)
            _fold_out(nat, -wg * c, df_o, d)

    specs = _fold_specs(T, tm)
    return pl.pallas_call(
        body, name="attn_pre_bwd", grid=(T // tm,),
        in_specs=[pl.BlockSpec((tm, AO), lambda i: (i, 0))] + specs + specs + [pl.BlockSpec((AO, AO), lambda i: (0, 0))],
        out_specs=specs + specs,
        out_shape=[jax.ShapeDtypeStruct((d, T // d, AO), MX) for _, d in GROUPS]
        + [jax.ShapeDtypeStruct((d, T // d, AO), f32) for _, d in GROUPS],
        scratch_shapes=[pltpu.VMEM((AO // LANES, tm, LANES), f32)],
        compiler_params=_cp(("parallel",), 56),
    )(dyb, *os_, *lses, ones)


def _head_ones():
    i = jnp.arange(AO) // HD
    return (i[:, None] == i[None, :]).astype(MX)


BWD_BLOCKS = 8


def attn_bwd(qf, kf, vf, dof, lse, df, g, nb):
    T = qf.shape[0]

    def body(q_ref, k_ref, v_ref, do_ref, l_ref, d_ref, dq_ref, dk_ref, dv_ref):
        prev_m, cur_m = _window_masks()

        def head_col(ref, r0):
            v = ref[pl.ds(r0, BLK), :]
            return jnp.concatenate([v[:, 0:1], v[:, HD:HD + 1]], axis=0)

        def step(b, carry):
            dk_c, dv_c = carry
            r0 = pl.multiple_of(b * BLK, BLK)
            rp = pl.multiple_of(jnp.maximum(b - 1, 0) * BLK, BLK)
            qs, dos = _stack_heads(q_ref[pl.ds(r0, BLK), :]), _stack_heads(do_ref[pl.ds(r0, BLK), :])
            k2, v2 = _two_blocks(k_ref, b), _two_blocks(v_ref, b)
            valid = cur_m | (prev_m & ((b % nb) != 0))
            p = jnp.where(valid, jnp.exp(_dot(qs, k2, NT) - head_col(l_ref, r0)), 0.0)
            ds = (p * (_dot(dos, v2, NT) + head_col(d_ref, r0))).astype(MX)
            dq_ref[pl.ds(r0, BLK), :] = _unstack_heads(_dot(ds, k2)).astype(dq_ref.dtype)
            dk2 = _dot(ds, qs, TN)
            dv2 = _dot(p.astype(MX), dos, TN)
            dk_ref[pl.ds(rp, BLK), :] = (dk_c + dk2[:BLK]).astype(dk_ref.dtype)
            dv_ref[pl.ds(rp, BLK), :] = (dv_c + dv2[:BLK]).astype(dv_ref.dtype)
            return dk2[BLK:], dv2[BLK:]

        zero = jnp.zeros((BLK, LANES), f32)

        def steps(i, carry):
            for j in range(BWD_BLOCKS):
                carry = step(BWD_BLOCKS * i + j, carry)
            return carry

        dk_c, dv_c = lax.fori_loop(0, T // BLK // BWD_BLOCKS, steps, (zero, zero))
        dk_ref[pl.ds(T - BLK, BLK), :] = dk_c.astype(dk_ref.dtype)
        dv_ref[pl.ds(T - BLK, BLK), :] = dv_c.astype(dv_ref.dtype)

    spec = pl.BlockSpec((T, LANES), lambda j: (0, j))
    return pl.pallas_call(
        body, name=f"attn_bwd{g}", grid=(AO // LANES,),
        in_specs=[spec] * 6, out_specs=[spec] * 3,
        out_shape=[jax.ShapeDtypeStruct((T, AO), MX)] * 3,
        compiler_params=_cp(("parallel",), 60),
    )(qf, kf, vf, dof, lse, df)


def unfold_rope_bwd(dqf, dkf, dvf, cos_t, sin_t, g, d):
    T = dqf.shape[0] * dqf.shape[1]
    tm = TM_FOLD

    def body(q_r, k_r, v_r, c_ref, s_ref, o_ref, nat):
        cos, sin = _tile4(c_ref[...]), _tile4(s_ref[...])
        for part, ref, scale in ((0, q_r, HD ** -0.5), (1, k_r, 1.0), (2, v_r, None)):
            x = _unfold_in(nat, ref, d)
            if scale is not None:
                x = (x * cos - _swap_halves(x) * sin) * scale
            o_ref[:, part * AO:(part + 1) * AO] = x.astype(o_ref.dtype)

    fold_spec = pl.BlockSpec((d, tm // d, AO), lambda i: (0, i, 0))
    tab = pl.BlockSpec((tm, LANES), lambda i: (i, 0))
    return pl.pallas_call(
        body, name=f"unfold_rope_bwd{g}", grid=(T // tm,),
        in_specs=[fold_spec] * 3 + [tab, tab],
        out_specs=pl.BlockSpec((tm, 3 * AO), lambda i: (i, 0)),
        out_shape=jax.ShapeDtypeStruct((T, 3 * AO), MX),
        scratch_shapes=[pltpu.VMEM((AO // LANES, tm, LANES), f32)],
        compiler_params=_cp(("parallel",)),
    )(dqf, dkf, dvf, cos_t, sin_t)


CONV_CHUNK = 32


def conv_bwd(dya, proj, conv_w):
    T = dya.shape[0]
    tm = TM_AC
    last = T // tm - 1

    def body(dy_r, bch, hprev, dy_next, b_next, cw, d_o, dw_o, zs, ds):
        i = pl.program_id(0)
        ch = CONV_CHUNK
        hz = hprev[:, :D].astype(f32) * hprev[:, D:].astype(f32)
        zs[0:HALO, :] = jnp.where(i > 0, hz, 0.0)
        ds[tm:tm + HALO, :] = jnp.where(i < last, dy_next[...].astype(f32) * b_next[...].astype(f32), 0.0)
        for r in range(0, tm, ch):
            zs[HALO + r:HALO + r + ch, :] = bch[r:r + ch, D:2 * D].astype(f32) * bch[r:r + ch, 2 * D:].astype(f32)
            ds[r:r + ch, :] = dy_r[r:r + ch, :].astype(f32) * bch[r:r + ch, :D].astype(f32)

        @pl.when(i == 0)
        def _():
            dw_o[...] = jnp.zeros_like(dw_o)

        sums = [jnp.zeros((1, D), f32) for _ in range(3)]
        for r in range(0, tm, ch):
            z2, z1, z = (zs[HALO + r - s:HALO + r - s + ch, :] for s in (2, 1, 0))
            dcv, d1, d2 = (ds[r + s:r + s + ch, :] for s in (0, 1, 2))
            cv = cw[0:1, :] * z2 + cw[1:2, :] * z1 + cw[2:3, :] * z
            dz = cw[2:3, :] * dcv + cw[1:2, :] * d1 + cw[0:1, :] * d2
            d_o[r:r + ch, :D] = (dy_r[r:r + ch, :].astype(f32) * cv).astype(d_o.dtype)
            d_o[r:r + ch, D:2 * D] = (dz * bch[r:r + ch, 2 * D:].astype(f32)).astype(d_o.dtype)
            d_o[r:r + ch, 2 * D:] = (dz * bch[r:r + ch, D:2 * D].astype(f32)).astype(d_o.dtype)
            for k, zz in enumerate((z2, z1, z)):
                sums[k] = sums[k] + jnp.sum(dcv * zz, axis=0, keepdims=True)
        for k in range(3):
            dw_o[k:k + 1, :] += sums[k]

    nh = tm // HALO
    return pl.pallas_call(
        body, name="conv_bwd", grid=(T // tm,),
        in_specs=[pl.BlockSpec((tm, D), lambda i: (i, 0)), pl.BlockSpec((tm, 3 * D), lambda i: (i, 1)),
                  pl.BlockSpec((HALO, 2 * D), lambda i: (jnp.maximum(i * nh - 1, 0), 2)),
                  pl.BlockSpec((HALO, D), lambda i: (jnp.minimum((i + 1) * nh, T // HALO - 1), 0)),
                  pl.BlockSpec((HALO, D), lambda i: (jnp.minimum((i + 1) * nh, T // HALO - 1), 3)),
                  pl.BlockSpec((3, D), lambda i: (0, 0))],
        out_specs=[pl.BlockSpec((tm, 3 * D), lambda i: (i, 0)), pl.BlockSpec((3, D), lambda i: (0, 0))],
        out_shape=[jax.ShapeDtypeStruct((T, 3 * D), MX), jax.ShapeDtypeStruct((3, D), f32)],
        scratch_shapes=[pltpu.VMEM((HALO + tm, D), f32), pltpu.VMEM((tm + HALO, D), f32)],
        compiler_params=_cp(("arbitrary",)),
    )(dya, proj, proj, dya, proj, conv_w)


def gmlp_bwd(dyc, proj, wst, bsx, lg, lb):
    T = dyc.shape[0]
    tm = TM_AC
    last = T // tm - 1

    def body(dy_r, u0, u1, v0, v1, ws, bs, lg_r, lb_r, d_o, dws_o, dbs_o, dlg_o, dlb_o, bacc):
        i = pl.program_id(0)
        up = jnp.concatenate([u0[...], u1[...]], axis=1).astype(f32)
        vp = jnp.concatenate([v0[...], v1[...]], axis=1).astype(f32)
        u, du = _gelu_and_grad(up)
        gv, dgv = _gelu_and_grad(vp)
        u, vn, xhat, rstd, sp = _gmlp_fwd(up, vp, ws, bs, lg_r[...], lb_r[...], u, gv)
        dy = dy_r[...].astype(f32)
        d_o[:, :D] = (dy * sp * du).astype(d_o.dtype)
        dsp = dy * u
        dspb, vnb = dsp.astype(MX), vn.astype(MX)

        @pl.when(i == 0)
        def _():
            dws_o[...] = jnp.zeros_like(dws_o)
            bacc[...] = jnp.zeros_like(bacc)

        rows = []
        for c in range(tm // BLK):
            r = slice(c * BLK, (c + 1) * BLK)
            cols = []
            for g in range(8):
                cs = slice(g * BLK, (g + 1) * BLK)
                dws_o[g] += _dot(dspb[r, cs], vnb[r, cs], NT)
                bacc[g] += dsp[r, cs]
                cols.append(_dot(ws[g], dspb[r, cs], TN))
            rows.append(jnp.concatenate(cols, axis=1))
        dvn = jnp.concatenate(rows, axis=0)
        _acc_rows(dlg_o, i == 0, dvn * xhat)
        _acc_rows(dlb_o, i == 0, dvn)
        d_o[:, D:] = (_ln_bwd(dvn, xhat, rstd, lg_r[...]) * dgv).astype(d_o.dtype)

        @pl.when(i == last)
        def _():
            row = lax.broadcasted_iota(jnp.int32, (BLK, BLK), 0)
            col = lax.broadcasted_iota(jnp.int32, (BLK, BLK), 1)
            ones = jnp.ones((8, BLK), MX)
            for g in range(8):
                dws_o[g] = jnp.where(col <= row, dws_o[g], 0.0)
                a = bacc[g]
                hi = a.astype(MX)
                lo = (a - hi.astype(f32)).astype(MX)
                dbs_o[g:g + 1, :] = (_dot(ones, hi, NT) + _dot(ones, lo, NT))[0:1, :]

    full = lambda shape: pl.BlockSpec(shape, lambda i: (0,) * len(shape))
    return pl.pallas_call(
        body, name="gmlp_bwd", grid=(T // tm,),
        in_specs=[pl.BlockSpec((tm, D), lambda i: (i, 0)), *_uv_specs(), full((8, BLK, BLK)), full((8, BLK, BLK)),
                  full((1, D)), full((1, D))],
        out_specs=[pl.BlockSpec((tm, 2 * D), lambda i: (i, 0)), full((8, BLK, BLK)), full((8, BLK)), full((1, D)), full((1, D))],
        out_shape=[jax.ShapeDtypeStruct((T, 2 * D), MX), jax.ShapeDtypeStruct((8, BLK, BLK), f32),
                   jax.ShapeDtypeStruct((8, BLK), f32), jax.ShapeDtypeStruct((1, D), f32), jax.ShapeDtypeStruct((1, D), f32)],
        scratch_shapes=[pltpu.VMEM((8, BLK, BLK), f32)],
        compiler_params=_cp(("arbitrary",)),
    )(dyc, proj, proj, proj, proj, wst, bsx, lg, lb)


PART_TILES = (6, 6, 3, 3, 3, 4)
PART_START = (0, 6, 12, 15, 18, 21)
TJ = 512


def _part_specs(tm, rows_axis):
    specs = []
    for n, s in zip(PART_TILES, PART_START):
        def imap(*idx, n=n, s=s):
            i, j = idx[rows_axis], idx[1 - rows_axis]
            inside = (j >= s) & (j < s + n)
            return (jnp.where(inside, i, 0), jnp.clip(j - s, 0, n - 1))
        specs.append(pl.BlockSpec((tm, TJ), imap))
    return specs


def _with_part(j, refs, fn):
    for r, n, s in zip(refs, PART_TILES, PART_START):
        @pl.when((j >= s) & (j < s + n))
        def _():
            fn(r[...])


def dx_in(dr1, parts, w, bias):
    T = dr1.shape[0]
    tm = min(2048, T)

    def body(dr_r, p0, p1, p2, p3, p4, p5, w_r, b_r, o_r):
        j = pl.program_id(1)

        @pl.when(j == 0)
        def _():
            o_r[...] = ALPHA * dr_r[...] + b_r[...]

        def acc(tile):
            o_r[...] += _dot(tile, w_r[...], NT)
        _with_part(j, (p0, p1, p2, p3, p4, p5), acc)

    once = dict(pipeline_mode=pl.Buffered(1))
    return pl.pallas_call(
        body, name="dx_in", grid=(T // tm, NIN // TJ),
        in_specs=[pl.BlockSpec((tm, D), lambda i, j: (i, 0), **once)] + _part_specs(tm, 0)
        + [pl.BlockSpec((D, TJ), lambda i, j: (0, j)), pl.BlockSpec((1, D), lambda i, j: (0, 0))],
        out_specs=pl.BlockSpec((tm, D), lambda i, j: (i, 0), **once),
        out_shape=jax.ShapeDtypeStruct((T, D), f32),
        compiler_params=_cp(("parallel", "arbitrary"), 56),
    )(dr1, *parts, w, bias)


def dw_in(x0t, parts):
    T = x0t.shape[1]
    tk = min(2048, T)

    def body(x_r, p0, p1, p2, p3, p4, p5, o_r):
        j, t = pl.program_id(0), pl.program_id(1)

        @pl.when(t == 0)
        def _():
            o_r[...] = jnp.zeros_like(o_r)

        def acc(tile):
            o_r[...] += _dot(x_r[:, pl.ds(pl.multiple_of(t * tk, tk), tk)], tile)
        _with_part(j, (p0, p1, p2, p3, p4, p5), acc)

    return pl.pallas_call(
        body, name="dw_in", grid=(NIN // TJ, T // tk),
        in_specs=[pl.BlockSpec((D, T), lambda j, t: (0, 0), pipeline_mode=pl.Buffered(1))] + _part_specs(tk, 1),
        out_specs=pl.BlockSpec((D, TJ), lambda j, t: (0, j)),
        out_shape=jax.ShapeDtypeStruct((D, NIN), f32),
        compiler_params=_cp(("parallel", "arbitrary"), 56),
    )(x0t, *parts)


def rope_tables(positions):
    half = HD // 2
    inv_freq = ROPE_THETA ** (-jnp.arange(half, dtype=f32) / half)
    ang = positions.astype(f32)[:, None] * inv_freq
    cos, sin = jnp.cos(ang), jnp.sin(ang)
    return jnp.tile(cos, (1, LANES // half)), jnp.tile(jnp.concatenate([-sin, sin], axis=1), (1, LANES // HD))


def _flat(a):
    return a.reshape(a.shape[0] * a.shape[1], a.shape[2])


def layer_fwd(x0, W, cos_t, sin_t):
    T = x0.shape[0]
    proj = mm_in(x0, W["w_in"], W["in_bias"])
    ya, yc = mix_ac_fwd(proj, W["conv_w"], W["wst"], W["bsx"], W["gmlp_ln_g"], W["gmlp_ln_b"])
    folded, os_, lses = [], [], []
    for g, (_, d) in enumerate(GROUPS):
        qf, kf, vf = fold_rope(proj, cos_t, sin_t, g, d)
        o, lse = attn_fwd(_flat(qf), _flat(kf), _flat(vf), g, T // d // BLK)
        folded.append((qf, kf, vf))
        os_.append(o.reshape(d, T // d, AO))
        lses.append(lse.reshape(d, T // d, AO))
    yb = combine_fwd(os_, lses)
    if "late" in W:
        W = {**W, **W["late"](yb)}
    mabc, m, r1, x1, x1b = mix_out_fwd(proj, ya, yb, yc, x0, W["p_a"], W["p_b"], W["p_c"], W["w_o"], W["ln1_g"], W["ln1_b"])
    gate, up, hh = ffn_up_fwd(x1b, W["w_gate"], W["w_up"])
    r2, x2 = ffn_down_fwd(hh, W["w_down"], x1, W["ln2_g"], W["ln2_b"])
    saved = dict(x0=x0, proj=proj, ya=ya, yb=yb, yc=yc, folded=folded, os=os_, lses=lses, mabc=mabc, m=m, r1=r1,
                 x1b=x1b, gate=gate, up=up, hh=hh, r2=r2)
    return x2, saved, W


def layer_bwd(dx2, S, W, cos_t, sin_t, on_grads=None):
    T = dx2.shape[0]
    tk = min(4096, T)
    G = {}
    dr2, dr2b, dgate, dup, G["ln2_g"], G["ln2_b"] = ffn_down_bwd(dx2, S["r2"], W["ln2_g"], W["w_down"], S["gate"], S["up"])
    blk_a = pl.BlockSpec((1, tk, FB), lambda k, t: (k, t, 0))
    row_b = pl.BlockSpec((tk, D), lambda k, t: (t, 0))
    G["w_down"] = tn_matmul("dw_down", S["hh"], dr2b, blk_a, row_b, (NCHIP, FB, D),
                            pl.BlockSpec((1, FB, D), lambda k, t: (k, 0, 0)), (NCHIP, T // tk))
    for nm, dv in (("w_gate", dgate), ("w_up", dup)):
        G[nm] = tn_matmul("d" + nm, dv, S["x1b"], blk_a, row_b, (NCHIP, FB, D),
                          pl.BlockSpec((1, FB, D), lambda k, t: (k, 0, 0)), (NCHIP, T // tk))
    dr1, dr1b, G["ln1_g"], G["ln1_b"] = ffn_up_bwd(dr2, dgate, dup, W["w_gate"], W["w_up"], S["r1"], W["ln1_g"])
    dmabc, dgates, dya, dyb, dyc = mix_out_bwd(dr1b, S["proj"], S["mabc"], W["w_o"], W["p_a"], W["p_b"], W["p_c"])
    one = (1, T // tk)
    full_o = pl.BlockSpec((D, D), lambda k, t: (0, 0))
    G["w_o"] = tn_matmul("dw_o", S["m"], dr1b, row_b, row_b, (D, D), full_o, one)
    G["p_a"] = tn_matmul("dp_a", S["ya"], dmabc, row_b, pl.BlockSpec((tk, D), lambda k, t: (t, 0)), (D, D), full_o, one)
    G["p_c"] = tn_matmul("dp_c", S["yc"], dmabc, row_b, pl.BlockSpec((tk, D), lambda k, t: (t, 2)), (D, D), full_o, one)
    G["p_b"] = tn_matmul("dp_b", S["yb"], dmabc, pl.BlockSpec((tk, AO), lambda k, t: (t, 0)),
                         pl.BlockSpec((tk, D // NCHIP), lambda k, t: (t, NCHIP + k)), (NCHIP, AO, D // NCHIP),
                         pl.BlockSpec((1, AO, D // NCHIP), lambda k, t: (k, 0, 0)), (NCHIP, T // tk))
    conv_w = W["conv_w"]
    if on_grads is not None:
        conv_w = conv_w + on_grads({n: G[n] for n in BIG if n != "w_in"})
    dbch, G["conv_w"] = conv_bwd(dya, S["proj"], conv_w)
    duv, G["w_s"], G["b_s"], G["gmlp_ln_g"], G["gmlp_ln_b"] = gmlp_bwd(
        dyc, S["proj"], W["wst"], W["bsx"], W["gmlp_ln_g"], W["gmlp_ln_b"])
    ones = _head_ones()
    if on_grads is not None:
        small = {n: G[n] for n in VECS + ("b_s", "w_s", "conv_w")}
        ones = ones + on_grads(small).astype(MX)
    pre = attn_pre_bwd(dyb, S["os"], S["lses"], ones)
    dqkv = []
    for g, (_, d) in enumerate(GROUPS):
        qf, kf, vf = S["folded"][g]
        dqf, dkf, dvf = attn_bwd(_flat(qf), _flat(kf), _flat(vf), _flat(pre[g]), _flat(S["lses"][g]), _flat(pre[3 + g]),
                                 g, T // d // BLK)
        shp = (d, T // d, AO)
        dqkv.append(unfold_rope_bwd(dqf.reshape(shp), dkf.reshape(shp), dvf.reshape(shp), cos_t, sin_t, g, d))
    parts = (dgates, dbch, *dqkv, duv)
    G["w_in"] = dw_in(transpose_cast(S["x0"]), parts)
    bias = jnp.zeros((1, D), f32)
    if on_grads is not None:
        bias = bias + on_grads({"w_in": G["w_in"]})
    dx0 = dx_in(dr1, parts, W["w_in"], bias)
    started = on_grads({"dx": dx0}) if on_grads is not None else None
    return dx0, G, started


def prep_layer_weights(Wl):
    W = dict(Wl)
    tril = jnp.tril(jnp.ones((BLK, BLK), f32))
    W["wst"] = (Wl["w_s"] * tril[None]).astype(MX)
    W["bsx"] = jnp.broadcast_to(Wl["b_s"][:, :, None], (8, BLK, BLK))
    for n in ("gmlp_ln_g", "gmlp_ln_b", "ln1_g", "ln1_b", "ln2_g", "ln2_b"):
        W[n] = Wl[n].reshape(1, D)
    W["in_bias"] = jnp.zeros((1, NIN), f32) + Wl.get("after", 0.0)
    return W


def local_step(x, positions, target, layers, on_grads=None):
    cos_t, sin_t = rope_tables(positions)
    Ws, saved = [], []
    h = x
    for Wl in layers:
        h, S, W = layer_fwd(h, prep_layer_weights(Wl(h) if callable(Wl) else Wl), cos_t, sin_t)
        Ws.append(W)
        saved.append(S)
    lsum, dh = loss_grad(h, target)
    if on_grads is not None:
        on_grads(len(Ws), {"loss": lsum})
    grads = [None] * len(Ws)
    started = None
    for l in reversed(range(len(Ws))):
        W = Ws[l]
        if started is not None:
            W = dict(W, ln2_g=W["ln2_g"] + started)
        hook = functools.partial(on_grads, l) if on_grads is not None else None
        dh, grads[l], started = layer_bwd(dh, saved[l], W, cos_t, sin_t, hook)
    return lsum, dh, grads


MESH = pl.DeviceIdType.MESH
ANY = pl.BlockSpec(memory_space=pl.ANY)
BIG = ("w_in", "w_gate", "w_up", "w_down", "p_a", "p_b", "p_c", "w_o")
NBIG = len(BIG)


def _place():
    x, y, c = lax.axis_index("x"), lax.axis_index("y"), lax.axis_index("c")
    return x, y, c, 2 * x + y


def _rcopy(src, dst, send, recv, dev):
    return pltpu.make_async_remote_copy(src_ref=src, dst_ref=dst, send_sem=send, recv_sem=recv, device_id=dev,
                                        device_id_type=MESH)


def _cols(ref, k, width):
    start = k * width if isinstance(k, int) else pl.multiple_of(k * width, LANES)
    return ref.at[:, pl.ds(start, width)]


CHUNK_BYTES = 1 << 20


def _pieces(shape, itemsize, nbytes=CHUNK_BYTES):
    rows, cols = shape[-2], shape[-1]
    per = max(16, nbytes // (cols * itemsize) // 16 * 16)
    out = []
    for lead in (range(shape[0]) if len(shape) == 3 else (None,)):
        for r in range(0, rows, per):
            sl = (pl.ds(r, min(per, rows - r)), slice(None))
            out.append(sl if lead is None else (lead,) + sl)
    return out


def _start_pieces(src, dst, make, nbytes=CHUNK_BYTES):
    for idx in _pieces(src.shape, jnp.dtype(src.dtype).itemsize, nbytes):
        make(src.at[idx], dst.at[idx]).start()


def gather_halves(shards):
    n = len(shards)

    def body(*refs):
        srcs, dsts = refs[:n], refs[n:2 * n]
        send, recv, own_send, own_recv = refs[2 * n:]
        x, y, c, k = _place()
        sib = (x, y, 1 - c)
        chips = [(1 - x, y), (x, 1 - y), (1 - x, 1 - y)]

        def slot(a, layer, pos):
            if a == 0:
                return _cols(dsts[0].at[layer], pos, WIN_SHARD)
            return dsts[a].at[pos, layer]

        def ici(a, j, src, dst):
            return _rcopy(src, dst, send.at[a, j], recv.at[a, j], (*chips[j], c))

        def d2d(a, j, src, dst):
            return _rcopy(src, dst, send.at[a, 3 + j], recv.at[a, 3 + j], sib)

        def own(a, layer, src, dst):
            return _rcopy(src, dst, own_send.at[a, layer], own_recv.at[a, layer], sib)

        for a in range(n):
            for j in range(3):
                _start_pieces(srcs[a].at[c], slot(a, c, k), functools.partial(ici, a, j))
        for a in range(n):
            for layer in range(DEPTH):
                _start_pieces(srcs[a].at[layer], slot(a, layer, k), functools.partial(own, a, layer))
        for a in range(n):
            for j, (cx, cy) in enumerate(chips):
                landed = slot(a, c, 2 * cx + cy)
                ici(a, j, landed, landed).wait_recv()
                _start_pieces(landed, landed, functools.partial(d2d, a, j))
        for a in range(n):
            for j, (cx, cy) in enumerate(chips):
                passed = slot(a, 1 - c, 2 * cx + cy)
                d2d(a, j, passed, passed).wait_recv()
                landed = slot(a, c, 2 * cx + cy)
                d2d(a, j, landed, landed).wait_send()
                ici(a, j, srcs[a].at[c], slot(a, c, k)).wait_send()
            for layer in range(DEPTH):
                own(a, layer, srcs[a].at[layer], slot(a, layer, k)).wait()

    outs = [jax.ShapeDtypeStruct((2, shards[0].shape[1], NIN), shards[0].dtype)]
    outs += [jax.ShapeDtypeStruct((NCHIP,) + s.shape, s.dtype) for s in shards[1:]]
    return pl.pallas_call(
        body, name="gather_halves", in_specs=[ANY] * n, out_specs=[ANY] * n, out_shape=outs,
        scratch_shapes=[pltpu.SemaphoreType.DMA((n, 6)), pltpu.SemaphoreType.DMA((n, 6)),
                        pltpu.SemaphoreType.DMA((n, DEPTH)), pltpu.SemaphoreType.DMA((n, DEPTH))],
    )(*shards)


def _gather_slot(dst, pos):
    return _cols(dst, pos, WIN_SHARD) if len(dst.shape) == 2 else dst.at[pos]


def _gather_copy(a, j, src, dst, send, recv, dev):
    return _rcopy(src, dst, send.at[a * NCHIP + j], recv.at[a * NCHIP + j], dev)


def gather_start(tag, shards, after):
    n = len(shards)

    def body(*refs):
        srcs, dsts = refs[:n], refs[n:2 * n]
        send, recv = refs[2 * n + len(after)], refs[2 * n + len(after) + 1]
        token = refs[-1]
        x, y, c, k = _place()
        peers = [(1 - x, y, c), (x, 1 - y, c), (1 - x, 1 - y, c), (x, y, 1 - c)]
        for a in range(n):
            for j, dev in enumerate(peers):
                _start_pieces(srcs[a], _gather_slot(dsts[a], k),
                              lambda s, d, a=a, j=j, dev=dev: _gather_copy(a, j, s, d, send, recv, dev))
        token[...] = jnp.zeros_like(token)

    gathered = [lax.empty((D, NIN) if s.shape == (D, WIN_SHARD) else (NCHIP,) + s.shape, s.dtype) for s in shards]
    ops = [pltpu.with_memory_space_constraint(v, pltpu.HBM) for v in list(shards) + gathered]
    sem = pltpu.SemaphoreType.DMA((n * NCHIP,))
    res = pl.pallas_call(
        body, name=f"gather_start{tag}", in_specs=[HBM] * (2 * n) + [ANY] * len(after),
        out_specs=[SEMS, SEMS] + [HBM] * (2 * n) + [pl.BlockSpec(memory_space=pltpu.VMEM)],
        out_shape=[sem, sem] + [pltpu.HBM(v.shape, v.dtype) for v in ops] + [jax.ShapeDtypeStruct((8, LANES), f32)],
        input_output_aliases={i: 2 + i for i in range(2 * n)},
        compiler_params=pltpu.CompilerParams(has_side_effects=EFFECT),
    )(*ops, *after)
    return res[0], res[1], res[2:2 + n], res[2 + n:2 + 2 * n], res[-1]


def gather_wait(tag, send, recv, shards, gathered, after):
    n = len(shards)

    def body(*refs):
        srcs, dsts = refs[:n], refs[n:2 * n]
        send_r, recv_r = refs[2 * n], refs[2 * n + 1]
        x, y, c, k = _place()
        peers = [(1 - x, y, c), (x, 1 - y, c), (1 - x, 1 - y, c), (x, y, 1 - c)]
        for a in range(n):
            for j, dev in enumerate(peers):
                _gather_copy(a, j, srcs[a], _gather_slot(dsts[a], k), send_r, recv_r, dev).wait_send()
                pos = 2 * dev[0] + dev[1]
                _gather_copy(a, j, srcs[a], _gather_slot(dsts[a], pos), send_r, recv_r, dev).wait_recv()

    ops = list(shards) + list(gathered)
    res = pl.pallas_call(
        body, name=f"gather_wait{tag}", in_specs=[HBM] * (2 * n) + [SEMS, SEMS] + [ANY] * len(after),
        out_specs=[HBM] * (2 * n), out_shape=[pltpu.HBM(v.shape, v.dtype) for v in ops],
        input_output_aliases={i: i for i in range(2 * n)},
        compiler_params=pltpu.CompilerParams(has_side_effects=EFFECT),
    )(*ops, send, recv, *after)
    return res[n:]


def _half(ref, h):
    rows = ref.shape[-2] // 2
    start = pl.multiple_of(h * rows, 16)
    if len(ref.shape) == 2:
        return ref.at[pl.ds(start, rows), :]
    return ref.at[:, pl.ds(start, rows), :]


HBM = pl.BlockSpec(memory_space=pltpu.HBM)
SEMS = pl.BlockSpec(memory_space=pltpu.SEMAPHORE)
EFFECT = pltpu.SideEffectType.DATAFLOW_SIDE_EFFECTING


def rs_pair_start(tag, grads, halves=True):
    n = len(grads)

    def body(*refs):
        g, theirs = refs[:n], refs[n:2 * n]
        send, recv = refs[2 * n], refs[2 * n + 1]
        x, y, c, _ = _place()
        for a in range(n):
            _start_pieces(_half(g[a], 1 - c) if halves else g[a], theirs[a],
                          lambda s, d, a=a: _rcopy(s, d, send.at[a], recv.at[a], (x, y, 1 - c)))
        refs[-1][...] = jnp.zeros_like(refs[-1])

    lands = [lax.empty(g.shape[:-2] + (g.shape[-2] // 2 if halves else g.shape[-2], g.shape[-1]), g.dtype) for g in grads]
    ops = [pltpu.with_memory_space_constraint(v, pltpu.HBM) for v in list(grads) + lands]
    sem = pltpu.SemaphoreType.DMA((n,))
    res = pl.pallas_call(
        body, name=f"rs_pair_start{tag}", in_specs=[HBM] * (2 * n),
        out_specs=[SEMS, SEMS] + [HBM] * (2 * n) + [pl.BlockSpec(memory_space=pltpu.VMEM)],
        out_shape=[sem, sem] + [pltpu.HBM(v.shape, v.dtype) for v in ops] + [jax.ShapeDtypeStruct((8, LANES), f32)],
        input_output_aliases={i: 2 + i for i in range(2 * n)},
        compiler_params=pltpu.CompilerParams(has_side_effects=EFFECT),
    )(*ops)
    return res[0], res[1], res[2:2 + n], res[2 + n:2 + 2 * n], res[-1]


def rs_pair_wait(tag, send, recv, grads, theirs, after, halves=True):
    n = len(grads)

    def body(*refs):
        g, land = refs[:n], refs[n:2 * n]
        send_r, recv_r = refs[2 * n], refs[2 * n + 1]
        x, y, c, _ = _place()
        for a in range(n):
            cp = _rcopy(_half(g[a], 1 - c) if halves else g[a], land[a], send_r.at[a], recv_r.at[a], (x, y, 1 - c))
            cp.wait_send()
            cp.wait_recv()

    ops = list(grads) + list(theirs)
    res = pl.pallas_call(
        body, name=f"rs_pair_wait{tag}", in_specs=[HBM] * (2 * n) + [SEMS, SEMS] + [ANY] * len(after),
        out_specs=[HBM] * (2 * n), out_shape=[pltpu.HBM(v.shape, v.dtype) for v in ops],
        input_output_aliases={i: i for i in range(2 * n)},
        compiler_params=pltpu.CompilerParams(has_side_effects=EFFECT),
    )(*ops, send, recv, *after)
    return res[:n], res[n:]


def _chip_piece(ref, k):
    return _cols(ref, k, WIN_SHARD) if len(ref.shape) == 2 else ref.at[k]


def _chip_copy(a, k, src, dst, send, recv, me, c):
    return _rcopy(src, dst, send.at[a * NCHIP + k], recv.at[a * NCHIP + me], (k // 2, k % 2, c))


def rs_chips_start(tag, sums):
    n = len(sums)

    def pshape(s):
        return (NCHIP, s[0], WIN_SHARD) if len(s) == 2 else s

    def body(*refs):
        s, land = refs[:n], refs[n:2 * n]
        send, recv = refs[2 * n], refs[2 * n + 1]
        token = refs[-1]
        x, y, c, me = _place()
        for k in range(NCHIP):
            @pl.when(me != k)
            def _():
                for a in range(n):
                    _start_pieces(_chip_piece(s[a], k), land[a].at[me],
                                  lambda src, dst, a=a: _chip_copy(a, k, src, dst, send, recv, me, c))
        token[...] = jnp.zeros_like(token)

    lands = [lax.empty(pshape(v.shape), v.dtype) for v in sums]
    ops = [pltpu.with_memory_space_constraint(v, pltpu.HBM) for v in list(sums) + lands]
    sem = pltpu.SemaphoreType.DMA((n * NCHIP,))
    res = pl.pallas_call(
        body, name=f"rs_chips_start{tag}", in_specs=[HBM] * (2 * n),
        out_specs=[SEMS, SEMS] + [HBM] * (2 * n) + [pl.BlockSpec(memory_space=pltpu.VMEM)],
        out_shape=[sem, sem] + [pltpu.HBM(v.shape, v.dtype) for v in ops] + [jax.ShapeDtypeStruct((8, LANES), f32)],
        input_output_aliases={i: 2 + i for i in range(2 * n)},
        compiler_params=pltpu.CompilerParams(has_side_effects=EFFECT),
    )(*ops)
    return res[0], res[1], res[2:2 + n], res[2 + n:2 + 2 * n], res[-1]


def rs_chips_wait(tag, send, recv, sums, lands, after):
    n = len(sums)

    def body(*refs):
        s, land = refs[:n], refs[n:2 * n]
        send_r, recv_r = refs[2 * n], refs[2 * n + 1]
        x, y, c, me = _place()
        for k in range(NCHIP):
            @pl.when(me != k)
            def _():
                for a in range(n):
                    piece = _chip_piece(s[a], k)
                    _chip_copy(a, k, piece, land[a].at[me], send_r, recv_r, me, c).wait_send()
                    _rcopy(piece, land[a].at[k], send_r.at[a * NCHIP + k], recv_r.at[a * NCHIP + k],
                           (k // 2, k % 2, c)).wait_recv()

    ops = list(sums) + list(lands)
    res = pl.pallas_call(
        body, name=f"rs_chips_wait{tag}", in_specs=[HBM] * (2 * n) + [SEMS, SEMS] + [ANY] * len(after),
        out_specs=[HBM] * (2 * n), out_shape=[pltpu.HBM(v.shape, v.dtype) for v in ops],
        input_output_aliases={i: i for i in range(2 * n)},
        compiler_params=pltpu.CompilerParams(has_side_effects=EFFECT),
    )(*ops, send, recv, *after)
    return res[:n], res[n:]


def _row_tile(rows, cols, itemsize=4, target=2 << 20):
    best = 8
    for t in range(8, rows + 1, 8):
        if rows % t == 0 and t * cols * itemsize <= target:
            best = t
    return best


GRAD_WIRE = jnp.bfloat16


def add_half(name, g, t, c):
    cols, half = t.shape[-1], t.shape[-2]
    nblk = 1 if t.ndim == 2 else t.shape[0]
    tr = _row_tile(half, cols)
    per = half // tr

    def body(c_ref, g_r, t_r, o_r):
        o_r[...] = (g_r[...] + t_r[...]).astype(o_r.dtype)

    tile_t = pl.BlockSpec((tr, cols), lambda i, c_ref: (i, 0))
    tile_g = pl.BlockSpec((tr, cols), lambda i, c_ref: ((i // per) * 2 * per + c_ref[0] * per + i % per, 0))
    out = pl.pallas_call(
        body, name=name, out_shape=jax.ShapeDtypeStruct((nblk * half, cols), GRAD_WIRE),
        grid_spec=pltpu.PrefetchScalarGridSpec(num_scalar_prefetch=1, grid=(nblk * per,), in_specs=[tile_g, tile_t],
                                               out_specs=tile_t),
        compiler_params=_cp(("parallel",)),
    )(c.reshape(1).astype(jnp.int32), g.reshape(nblk * 2 * half, cols), t.reshape(nblk * half, cols))
    return out.reshape(t.shape)


def add_chips(name, land, own):
    _, rows, cols = land.shape
    tr = _row_tile(rows, cols, target=1 << 20)

    def body(land_r, own_r, o_r):
        me = 2 * lax.axis_index("x") + lax.axis_index("y")
        for k in range(NCHIP):
            @pl.when(me == k)
            def _():
                acc = None
                for j in range(NCHIP):
                    t = (own_r[...] if j == k else land_r[j]).astype(f32)
                    acc = t if acc is None else acc + t
                o_r[...] = acc

    tile = pl.BlockSpec((tr, cols), lambda i: (i, 0))
    return pl.pallas_call(
        body, name=name, grid=(rows // tr,), in_specs=[pl.BlockSpec((NCHIP, tr, cols), lambda i: (0, i, 0)), tile],
        out_specs=tile, out_shape=jax.ShapeDtypeStruct((rows, cols), f32), compiler_params=_cp(("parallel",)),
    )(land, own)


def reduce_scatter_pair(tag, G):
    names = tuple(G)
    grads = [G[n] if G[n].ndim == 3 or n == "w_in" else G[n].reshape(NCHIP, D // NCHIP, D) for n in names]
    send, recv, grads, theirs, token = rs_pair_start(tag, grads)
    return (tag, names, send, recv, grads, theirs), token[0, 0]


def reduce_scatter_chips(state, after):
    c = lax.axis_index("c")
    tag, names, send, recv, grads, theirs = state
    grads, theirs = rs_pair_wait(tag, send, recv, grads, theirs, after)
    sums = [add_half(f"rs_add_pair{tag}_{n}", g, t, c) for n, g, t in zip(names, grads, theirs)]
    send, recv, sums, lands, token = rs_chips_start(tag, sums)
    return (tag, names, send, recv, sums, lands), token[0, 0]


def reduce_scatter_finish(state, after):
    me = 2 * lax.axis_index("x") + lax.axis_index("y")
    tag, names, send, recv, sums, lands = state
    sums, landed = rs_chips_wait(tag, send, recv, sums, lands, after)
    halves = []
    for n, s, v in zip(names, sums, landed):
        own = lax.dynamic_slice_in_dim(s, me * WIN_SHARD, WIN_SHARD, axis=1) if s.ndim == 2 else \
            lax.dynamic_index_in_dim(s, me, 0, keepdims=False)
        halves.append(add_chips(f"rs_add_chips{tag}_{n}", v, own))
    send, recv, halves, others, _ = rs_pair_start("_join" + tag, halves, halves=False)
    return tag, names, send, recv, halves, others


def reduce_scatter_join(state, after):
    tag, names, send, recv, halves, others = state
    halves, others = rs_pair_wait("_join" + tag, send, recv, halves, others, after, halves=False)
    return dict(zip(names, zip(halves, others)))


NDEV = 8


def _small_copy(r, src, dst, send, recv, x, y, c):
    return _rcopy(src, dst, send.at[r - 1], recv.at[r - 1], (x ^ (r >> 2), y ^ ((r >> 1) & 1), c ^ (r & 1)))


def small_start(pack):
    def body(p, land, send, recv, p_thru, land_thru, token):
        x, y, c, _ = _place()
        me = 4 * x + 2 * y + c
        for r in range(1, NDEV):
            _start_pieces(p, land.at[me], lambda s, d, r=r: _small_copy(r, s, d, send, recv, x, y, c), 128 << 10)
        token[...] = jnp.zeros_like(token)

    ops = [pltpu.with_memory_space_constraint(v, pltpu.HBM) for v in (pack, lax.empty((NDEV,) + pack.shape, f32))]
    sem = pltpu.SemaphoreType.DMA((NDEV - 1,))
    return pl.pallas_call(
        body, name="small_start", in_specs=[HBM, HBM],
        out_specs=[SEMS, SEMS, HBM, HBM, pl.BlockSpec(memory_space=pltpu.VMEM)],
        out_shape=[sem, sem] + [pltpu.HBM(v.shape, v.dtype) for v in ops] + [jax.ShapeDtypeStruct((8, LANES), f32)],
        input_output_aliases={0: 2, 1: 3}, compiler_params=pltpu.CompilerParams(has_side_effects=EFFECT),
    )(*ops)


def small_wait(send, recv, pack, land, after):
    def body(p, land_r, send_r, recv_r, *rest):
        x, y, c, _ = _place()
        me = 4 * x + 2 * y + c
        for r in range(1, NDEV):
            _small_copy(r, p, land_r.at[me], send_r, recv_r, x, y, c).wait_send()
            src = 4 * (x ^ (r >> 2)) + 2 * (y ^ ((r >> 1) & 1)) + (c ^ (r & 1))
            _small_copy(r, p, land_r.at[src], send_r, recv_r, x, y, c).wait_recv()

    return pl.pallas_call(
        body, name="small_wait", in_specs=[HBM, HBM, SEMS, SEMS] + [ANY] * len(after), out_specs=[HBM, HBM],
        out_shape=[pltpu.HBM(pack.shape, f32), pltpu.HBM(land.shape, f32)], input_output_aliases={0: 0, 1: 1},
        compiler_params=pltpu.CompilerParams(has_side_effects=EFFECT),
    )(pack, land, send, recv, *after)


def small_sum(land, pack):
    def body(land_r, p_r, o_r):
        me = 4 * lax.axis_index("x") + 2 * lax.axis_index("y") + lax.axis_index("c")
        for k in range(NDEV):
            @pl.when(me == k)
            def _():
                acc = None
                for d in range(NDEV):
                    t = p_r[...] if d == k else land_r[d]
                    acc = t if acc is None else acc + t
                o_r[...] = acc

    vm = pl.BlockSpec(memory_space=pltpu.VMEM)
    return pl.pallas_call(
        body, name="small_sum", in_specs=[vm, vm], out_specs=vm, out_shape=jax.ShapeDtypeStruct(pack.shape, f32),
        compiler_params=pltpu.CompilerParams(vmem_limit_bytes=40 << 20),
    )(land, pack)


def _adamw_math(w, g, m, v):
    m = ADAM_B1 * m + (1.0 - ADAM_B1) * g
    v = ADAM_B2 * v + (1.0 - ADAM_B2) * (g * g)
    m_hat = m / (1.0 - ADAM_B1 ** ADAM_STEP)
    v_hat = v / (1.0 - ADAM_B2 ** ADAM_STEP)
    return -ADAM_LR * (m_hat / (jnp.sqrt(v_hat) + ADAM_EPS) + ADAM_WD * w), m, v


def adamw_big(name, halves, w, m, v):
    _, R, C = w.shape
    tr = _row_tile(R // 2, C, target=1 << 20)
    nt = R // 2 // tr

    def body(a0, b0, a1, b1, w_r, m_r, v_r, g_o, d_o, m_o, v_o):
        mine = pl.program_id(1) == lax.axis_index("c")
        g = jnp.where(pl.program_id(0) == 0, jnp.where(mine, a0[...], b0[...]), jnp.where(mine, a1[...], b1[...]))
        g_o[...] = g
        d_o[...], m_o[...], v_o[...] = _adamw_math(w_r[...], g, m_r[...], v_r[...])

    stk = pl.BlockSpec((None, tr, C), lambda l, h, i: (l, h * nt + i, 0))
    lay0 = pl.BlockSpec((tr, C), lambda l, h, i: (jnp.where(l == 0, i, nt - 1), 0))
    lay1 = pl.BlockSpec((tr, C), lambda l, h, i: (jnp.where(l == 0, 0, i), 0))
    return pl.pallas_call(
        body, name=name, grid=(DEPTH, 2, nt),
        in_specs=[lay0, lay0, lay1, lay1, stk, stk, stk],
        out_specs=[stk] * 4, out_shape=[jax.ShapeDtypeStruct(w.shape, f32)] * 4,
        compiler_params=_cp(("arbitrary", "arbitrary", "arbitrary")),
    )(*halves[0], *halves[1], w, m, v)


def adamw_small(name, g, w, m, v):
    def body(g_r, w_r, m_r, v_r, d_o, m_o, v_o):
        d_o[...], m_o[...], v_o[...] = _adamw_math(w_r[...], g_r[...], m_r[...], v_r[...])

    return pl.pallas_call(body, name=name, out_shape=[jax.ShapeDtypeStruct(w.shape, f32)] * 3)(g, w, m, v)


WEIGHTS = ("w_in", "conv_w", "gmlp_ln_g", "gmlp_ln_b", "w_s", "b_s", "p_a", "p_b", "p_c", "w_o", "ln1_g", "ln1_b",
           "w_gate", "w_up", "w_down", "ln2_g", "ln2_b")
VECS = ("ln1_g", "ln1_b", "ln2_g", "ln2_b", "gmlp_ln_g", "gmlp_ln_b")
ROWS_VEC, ROWS_BS, ROWS_WS, ROWS_CONV = D // LANES, 8, 8 * BLK, 3 * D // LANES
ROWS_LAYER = len(VECS) * ROWS_VEC + ROWS_BS + ROWS_WS + ROWS_CONV


def _pack_small(per_layer, tail):
    parts = []
    for P in per_layer:
        parts += [P[n].reshape(ROWS_VEC, LANES) for n in VECS]
        parts += [P["b_s"].reshape(ROWS_BS, LANES), P["w_s"].reshape(ROWS_WS, LANES), P["conv_w"].reshape(ROWS_CONV, LANES)]
    return jnp.concatenate(parts + [tail], axis=0)


def _unpack_small(pack):
    out = []
    for l in range(DEPTH):
        r = l * ROWS_LAYER
        P = {}
        for n in VECS:
            P[n] = pack[r:r + ROWS_VEC].reshape(D)
            r += ROWS_VEC
        P["b_s"] = pack[r:r + ROWS_BS].reshape(8, BLK)
        r += ROWS_BS
        P["w_s"] = pack[r:r + ROWS_WS].reshape(8, BLK, BLK)
        r += ROWS_WS
        P["conv_w"] = pack[r:r + ROWS_CONV].reshape(3, D)
        out.append(P)
    return out, pack[DEPTH * ROWS_LAYER:]


def kernel(x, positions, w_in, conv_w, gmlp_ln_g, gmlp_ln_b, w_s, b_s, p_a, p_b, p_c, w_o, ln1_g, ln1_b, w_gate, w_up, w_down, ln2_g, ln2_b, loss_target, m_w_in, m_conv_w, m_gmlp_ln_g, m_gmlp_ln_b, m_w_s, m_b_s, m_p_a, m_p_b, m_p_c, m_w_o, m_ln1_g, m_ln1_b, m_w_gate, m_w_up, m_w_down, m_ln2_g, m_ln2_b, v_w_in, v_conv_w, v_gmlp_ln_g, v_gmlp_ln_b, v_w_s, v_b_s, v_p_a, v_p_b, v_p_c, v_w_o, v_ln1_g, v_ln1_b, v_w_gate, v_w_up, v_w_down, v_ln2_g, v_ln2_b):
    Wt = dict(w_in=w_in, conv_w=conv_w, gmlp_ln_g=gmlp_ln_g, gmlp_ln_b=gmlp_ln_b, w_s=w_s, b_s=b_s, p_a=p_a, p_b=p_b,
              p_c=p_c, w_o=w_o, ln1_g=ln1_g, ln1_b=ln1_b, w_gate=w_gate, w_up=w_up, w_down=w_down, ln2_g=ln2_g, ln2_b=ln2_b)
    Mt = dict(w_in=m_w_in, conv_w=m_conv_w, gmlp_ln_g=m_gmlp_ln_g, gmlp_ln_b=m_gmlp_ln_b, w_s=m_w_s, b_s=m_b_s, p_a=m_p_a,
              p_b=m_p_b, p_c=m_p_c, w_o=m_w_o, ln1_g=m_ln1_g, ln1_b=m_ln1_b, w_gate=m_w_gate, w_up=m_w_up,
              w_down=m_w_down, ln2_g=m_ln2_g, ln2_b=m_ln2_b)
    Vt = dict(w_in=v_w_in, conv_w=v_conv_w, gmlp_ln_g=v_gmlp_ln_g, gmlp_ln_b=v_gmlp_ln_b, w_s=v_w_s, b_s=v_b_s, p_a=v_p_a,
              p_b=v_p_b, p_c=v_p_c, w_o=v_w_o, ln1_g=v_ln1_g, ln1_b=v_ln1_b, w_gate=v_w_gate, w_up=v_w_up,
              w_down=v_w_down, ln2_g=v_ln2_g, ln2_b=v_ln2_b)
    chip = 2 * lax.axis_index("x") + lax.axis_index("y")
    cw = D // NCHIP

    def gathered_weights(names, arrays):
        Wl = dict(zip(names, arrays))
        for n in ("p_a", "p_c", "w_o"):
            Wl[n] = Wl[n].reshape(D, D)
        return Wl

    def small_weights(l, conv_all):
        Wl = {n: Wt[n][l] for n in VECS + ("w_s", "b_s")}
        Wl["conv_w"] = conv_all[:, l].transpose(1, 0, 2).reshape(3, D)
        return Wl

    w_in0, conv_all = gather_halves([Wt["w_in"][0].astype(MX).reshape(2, D // 2, WIN_SHARD), conv_w])
    rest = BIG[1:]
    *late0, coming0 = gather_start("0", [Wt[n][0].astype(MX) for n in rest], [conv_all])
    *late1, coming1 = gather_start("1", [Wt[n][1].astype(MX) for n in BIG], [conv_all, coming0])
    W0 = dict(small_weights(0, conv_all), w_in=w_in0.reshape(D, NIN), after=coming1[0, 0],
              late=lambda y: gathered_weights(rest, gather_wait("0", *late0, [y])))

    def W1(h):
        return dict(small_weights(1, conv_all), **gathered_weights(BIG, gather_wait("1", *late1, [h])))

    layers = [W0, W1]

    rs_state, rs_started, held = {}, {}, {}

    def start_exchange(l, g):
        if "loss" in g:
            held[l] = g
            return None
        if "conv_w" in g:
            held[l] = g
            rs_state[(l, False)], started = reduce_scatter_chips(rs_state[(l, False)], [g["w_s"], g["conv_w"]])
            if l == 0:
                pack = _pack_small([held[j] for j in range(DEPTH)], held[DEPTH]["loss"])
                *held["small"], token = small_start(pack)
                started = started + token[0, 0]
            return started
        if "dx" in g:
            rs_state[(l, True)], rs_started[(l, True)] = reduce_scatter_chips(rs_state[(l, True)], [g["dx"]])
            return rs_started[(l, True)]
        key = (l, "w_in" in g)
        rs_state[key], started = reduce_scatter_pair(f"{l}{'b' if key[1] else 'a'}", g)
        return started

    _, grad_x, _ = local_step(x[0], positions[0], loss_target[0], layers, start_exchange)

    last = jnp.zeros((8, LANES), f32) + rs_started[(0, True)]
    behind = [grad_x, last]
    red = [dict() for _ in range(DEPTH)]
    swaps = {key: reduce_scatter_finish(rs_state[key], behind) for key in ((1, False), (1, True), (0, False))}
    small, tail = _unpack_small(small_sum(*reversed(small_wait(*held["small"], behind))))
    loss = tail[0, 0]

    G, DW, NM, NV = {}, {}, {}, {}
    zc = jnp.zeros((3, D), f32)
    wp = _pack_small([{**{n: Wt[n][l] for n in VECS + ("b_s", "w_s")}, "conv_w": zc} for l in range(DEPTH)], jnp.zeros((8, LANES), f32))
    mp = _pack_small([{**{n: Mt[n][l] for n in VECS + ("b_s", "w_s")}, "conv_w": zc} for l in range(DEPTH)], jnp.zeros((8, LANES), f32))
    vp = _pack_small([{**{n: Vt[n][l] for n in VECS + ("b_s", "w_s")}, "conv_w": zc} for l in range(DEPTH)], jnp.ones((8, LANES), f32))
    gp = _pack_small(small, jnp.zeros((8, LANES), f32))
    outs = [_unpack_small(a)[0] for a in adamw_small("adamw_small", gp, wp, mp, vp)]
    for n in VECS + ("b_s", "w_s"):
        G[n] = jnp.stack([small[l][n] for l in range(DEPTH)])
        DW[n], NM[n], NV[n] = (jnp.stack([o[l][n] for l in range(DEPTH)]) for o in outs)
    gconv = jnp.stack([lax.dynamic_slice(small[l]["conv_w"], (0, chip * cw), (3, cw)) for l in range(DEPTH)])
    G["conv_w"] = gconv
    flat = lambda a: a.reshape(DEPTH * 3, cw)
    d, m2, v2 = adamw_small("adamw_conv", flat(gconv), flat(conv_w), flat(m_conv_w), flat(v_conv_w))
    DW["conv_w"], NM["conv_w"], NV["conv_w"] = (a.reshape(DEPTH, 3, cw) for a in (d, m2, v2))

    for key in swaps:
        red[key[0]].update(reduce_scatter_join(swaps[key], [d, DW["ln2_b"]]))
    updated = {}
    for n in BIG[1:]:
        tr = (lambda a: jnp.swapaxes(a, 1, 2)) if n in ("w_gate", "w_up") else (lambda a: a)
        updated[n] = adamw_big("adamw_" + n, (red[0][n], red[1][n]), tr(Wt[n]), tr(Mt[n]), tr(Vt[n]))
        G[n], DW[n], NM[n], NV[n] = map(tr, updated[n])
    done = [d, DW["ln2_b"], red[1]["w_in"][1]] + [updated[n][1] for n in BIG[1:]]
    red[0].update(reduce_scatter_join(reduce_scatter_finish(rs_state[(0, True)], done), [updated["w_o"][1]]))
    G["w_in"], DW["w_in"], NM["w_in"], NV["w_in"] = adamw_big(
        "adamw_w_in", (red[0]["w_in"], red[1]["w_in"]), Wt["w_in"], Mt["w_in"], Vt["w_in"])

    return (loss, grad_x[None], *[G[n] for n in WEIGHTS], *[DW[n] for n in WEIGHTS], *[NM[n] for n in WEIGHTS],
            *[NV[n] for n in WEIGHTS])
```

```python
import functools
import math

import jax
import jax.numpy as jnp
from jax import lax
from jax.experimental import pallas as pl
from jax.experimental.pallas import tpu as pltpu

D = 1024
NIN = 12800
DFF = 2816
NCHIP = 4
FB = DFF // NCHIP
WIN_SHARD = NIN // NCHIP
DEPTH = 2
GROUPS = ((128, 1), (512, 4), (2048, 16))
HD = 64
BLK = 128
AO = 512
ALPHA = (2 * DEPTH) ** 0.25
EPS = 1e-5
ROPE_THETA = 10000.0
LANES = 128
NEG = -1e30

C_GATES, C_BCH, C_QKV, C_UV = 0, 3 * D, 6 * D, 6 * D + 9 * AO

MX = jnp.bfloat16
ACT = jnp.bfloat16

ADAM_LR, ADAM_B1, ADAM_B2, ADAM_EPS, ADAM_WD, ADAM_STEP = 0.001, 0.9, 0.999, 1e-08, 0.01, 10

f32 = jnp.float32
NT = (((1,), (1,)), ((), ()))
TN = (((0,), (0,)), ((), ()))


def _cp(sem, vmem_mb=48):
    return pltpu.CompilerParams(dimension_semantics=sem, vmem_limit_bytes=vmem_mb << 20)


def _dot(a, b, dims=None):
    if dims is None:
        return jnp.dot(a, b, preferred_element_type=f32)
    return lax.dot_general(a, b, dims, preferred_element_type=f32)


def _ln_stats(r):
    mu = jnp.mean(r, axis=-1, keepdims=True)
    xc = r - mu
    var = jnp.mean(xc * xc, axis=-1, keepdims=True)
    rstd = lax.rsqrt(var + EPS)
    return xc * rstd, rstd


def _ln_bwd(dy, xhat, rstd, g):
    dxh = dy * g
    return rstd * (dxh - jnp.mean(dxh, axis=-1, keepdims=True) - xhat * jnp.mean(dxh * xhat, axis=-1, keepdims=True))


def _gelu(x):
    return 0.5 * x * (1.0 + lax.erf(x * (1.0 / math.sqrt(2.0))))


def _gelu_and_grad(x):
    cdf = 0.5 * (1.0 + lax.erf(x * (1.0 / math.sqrt(2.0))))
    return x * cdf, cdf + x * jnp.exp(-0.5 * x * x) * (1.0 / math.sqrt(2.0 * math.pi))


def _sigmoid(x):
    return 0.5 * jnp.tanh(0.5 * x) + 0.5


def _acc_rows(o_ref, first, val):
    @pl.when(first)
    def _():
        o_ref[...] = jnp.zeros_like(o_ref)
    o_ref[...] += jnp.sum(val, axis=0, keepdims=True)


def mm_in(x, w, bias):
    T = x.shape[0]
    tm, tn = min(2048, T), 1280

    def body(x_ref, w_ref, b_ref, o_ref, xb):
        @pl.when(pl.program_id(1) == 0)
        def _():
            xb[...] = x_ref[...].astype(MX)
        o_ref[...] = (_dot(xb[...], w_ref[...]) + b_ref[...]).astype(o_ref.dtype)

    return pl.pallas_call(
        body, name="mm_in", grid=(T // tm, NIN // tn),
        in_specs=[pl.BlockSpec((tm, D), lambda i, j: (i, 0), pipeline_mode=pl.Buffered(1)),
                  pl.BlockSpec((D, tn), lambda i, j: (0, j)), pl.BlockSpec((1, tn), lambda i, j: (0, j))],
        out_specs=pl.BlockSpec((tm, tn), lambda i, j: (i, j)),
        out_shape=jax.ShapeDtypeStruct((T, NIN), ACT),
        scratch_shapes=[pltpu.VMEM((tm, D), MX)],
        compiler_params=_cp(("parallel", "arbitrary")),
    )(x, w, bias)


HALO = 16
TM_AC = 512


def _uv_specs():
    return [pl.BlockSpec((TM_AC, 512), functools.partial(lambda i, j: (i, j), j=C_UV // 512 + j)) for j in range(4)]


def _gmlp_fwd(up, vp, ws_ref, bs_ref, lg, lb, u=None, gv=None):
    u = _gelu(up) if u is None else u
    xhat, rstd = _ln_stats(_gelu(vp) if gv is None else gv)
    vn = xhat * lg + lb
    vnb = vn.astype(MX)
    rows = []
    for c in range(up.shape[0] // BLK):
        r = slice(c * BLK, (c + 1) * BLK)
        rows.append(jnp.concatenate(
            [_dot(ws_ref[g], vnb[r, g * BLK:(g + 1) * BLK]) + bs_ref[g] for g in range(8)], axis=1))
    return u, vn, xhat, rstd, jnp.concatenate(rows, axis=0)


def mix_ac_fwd(proj, conv_w, wst, bsx, lg, lb):
    T = proj.shape[0]
    tm = TM_AC

    def body(bch, halo, u0, u1, v0, v1, cw, ws, bs, lg_ref, lb_ref, ya, yc, zs):
        i = pl.program_id(0)
        pb = bch[...].astype(f32)
        z = pb[:, D:2 * D] * pb[:, 2 * D:]
        hz = halo[:, :D].astype(f32) * halo[:, D:].astype(f32)
        zs[0:HALO, :] = jnp.where(i > 0, hz, 0.0)
        zs[HALO:HALO + tm, :] = z
        cv = cw[0:1, :] * zs[HALO - 2:HALO - 2 + tm, :] + cw[1:2, :] * zs[HALO - 1:HALO - 1 + tm, :] + cw[2:3, :] * z
        ya[...] = (pb[:, :D] * cv).astype(ya.dtype)
        up = jnp.concatenate([u0[...], u1[...]], axis=1).astype(f32)
        vp = jnp.concatenate([v0[...], v1[...]], axis=1).astype(f32)
        u, _, _, _, sp = _gmlp_fwd(up, vp, ws, bs, lg_ref[...], lb_ref[...])
        yc[...] = (u * sp).astype(yc.dtype)

    full = lambda shape: pl.BlockSpec(shape, lambda i: (0,) * len(shape))
    return pl.pallas_call(
        body, name="mix_ac_fwd", grid=(T // tm,),
        in_specs=[pl.BlockSpec((tm, 3 * D), lambda i: (i, 1)),
                  pl.BlockSpec((HALO, 2 * D), lambda i: (jnp.maximum(i * (tm // HALO) - 1, 0), 2)),
                  *_uv_specs(), full((3, D)), full((8, BLK, BLK)), full((8, BLK, BLK)), full((1, D)), full((1, D))],
        out_specs=[pl.BlockSpec((tm, D), lambda i: (i, 0))] * 2,
        out_shape=[jax.ShapeDtypeStruct((T, D), MX)] * 2,
        scratch_shapes=[pltpu.VMEM((HALO + tm, D), f32)],
        compiler_params=_cp(("parallel",)),
    )(proj, proj, proj, proj, proj, proj, conv_w, wst, bsx, lg, lb)


def _swap_halves(x):
    lane = lax.broadcasted_iota(jnp.int32, x.shape, 1)
    return jnp.where((lane % HD) < HD // 2, pltpu.roll(x, x.shape[1] - HD // 2, 1), pltpu.roll(x, HD // 2, 1))


def _tile4(t):
    return jnp.concatenate([t] * (AO // LANES), axis=1)


TM_FOLD = 1024


def _fold_out(nat, x, out_ref, d):
    if d == 1:
        out_ref[0] = x.astype(out_ref.dtype)
        return
    rows = x.shape[0] // d
    for j in range(AO // LANES):
        nat[j] = x[:, j * LANES:(j + 1) * LANES]
    for r in range(d):
        out_ref[r] = jnp.concatenate(
            [nat.at[j][pl.ds(r, rows, stride=d), :] for j in range(AO // LANES)], axis=1).astype(out_ref.dtype)


def _unfold_in(nat, in_ref, d):
    if d == 1:
        return in_ref[0].astype(f32)
    rows = in_ref.shape[1]
    for r in range(d):
        v = in_ref[r].astype(f32)
        for j in range(AO // LANES):
            nat.at[j][pl.ds(r, rows, stride=d), :] = v[:, j * LANES:(j + 1) * LANES]
    return jnp.concatenate([nat[j] for j in range(AO // LANES)], axis=1)


def fold_rope(proj, cos_t, sin_t, g, d):
    T = proj.shape[0]
    tm = TM_FOLD
    rows = tm // d

    def body(x_ref, c_ref, s_ref, q_o, k_o, v_o, nat):
        cos, sin = _tile4(c_ref[...]), _tile4(s_ref[...])
        for part, out, scale in ((0, q_o, HD ** -0.5), (1, k_o, 1.0), (2, v_o, None)):
            x = x_ref[:, part * AO:(part + 1) * AO].astype(f32)
            if scale is not None:
                x = (x * cos + _swap_halves(x) * sin) * scale
            _fold_out(nat, x, out, d)

    fold_spec = pl.BlockSpec((d, rows, AO), lambda i: (0, i, 0))
    return pl.pallas_call(
        body, name=f"fold_rope{g}", grid=(T // tm,),
        in_specs=[pl.BlockSpec((tm, 3 * AO), lambda i: (i, C_QKV // (3 * AO) + g)),
                  pl.BlockSpec((tm, LANES), lambda i: (i, 0)), pl.BlockSpec((tm, LANES), lambda i: (i, 0))],
        out_specs=[fold_spec] * 3,
        out_shape=[jax.ShapeDtypeStruct((d, T // d, AO), MX)] * 3,
        scratch_shapes=[pltpu.VMEM((AO // LANES, tm, LANES), f32)],
        compiler_params=_cp(("parallel",)),
    )(proj, cos_t, sin_t)


def _stack_heads(x):
    lane = lax.broadcasted_iota(jnp.int32, x.shape, 1)
    z = jnp.zeros_like(x)
    return jnp.concatenate([jnp.where(lane < HD, x, z), jnp.where(lane >= HD, x, z)], axis=0)


def _unstack_heads(y):
    lane = lax.broadcasted_iota(jnp.int32, (BLK, LANES), 1)
    return jnp.where(lane < HD, y[:BLK], y[BLK:])


def _window_masks():
    row = lax.broadcasted_iota(jnp.int32, (2 * BLK, 2 * BLK), 0) % BLK
    col = lax.broadcasted_iota(jnp.int32, (2 * BLK, 2 * BLK), 1)
    return (col < BLK) & (col >= row), (col >= BLK) & (col - BLK <= row)


def _two_blocks(ref, b):
    r0 = pl.multiple_of(b * BLK, BLK)
    rp = pl.multiple_of(jnp.maximum(b - 1, 0) * BLK, BLK)
    return jnp.concatenate([ref[pl.ds(rp, BLK), :], ref[pl.ds(r0, BLK), :]], axis=0)


def _merge_masks():
    row = lax.broadcasted_iota(jnp.int32, (2 * BLK, BLK), 0) % BLK
    col = lax.broadcasted_iota(jnp.int32, (2 * BLK, BLK), 1)
    return col <= row, col == row


def attn_fwd(qf, kf, vf, g, nb):
    T = qf.shape[0]

    def body(q_ref, k_ref, v_ref, o_ref, l_ref):
        cur_m, own_m = _merge_masks()

        def step(b, carry):
            r0 = pl.multiple_of(b * BLK, BLK)
            rp = pl.multiple_of(jnp.maximum(b - 1, 0) * BLK, BLK)
            qs = _stack_heads(q_ref[pl.ds(r0, BLK), :])
            vc, vp = v_ref[pl.ds(r0, BLK), :], v_ref[pl.ds(rp, BLK), :]
            sp = jnp.where((b % nb) != 0, _dot(qs, k_ref[pl.ds(rp, BLK), :], NT), NEG)
            s = jnp.where(cur_m, _dot(qs, k_ref[pl.ds(r0, BLK), :], NT), sp)
            s_own = jnp.sum(jnp.where(own_m, sp, 0.0), axis=-1, keepdims=True)
            m = jnp.maximum(jnp.max(s, axis=-1, keepdims=True), s_own)
            p, p_own = jnp.exp(s - m), jnp.exp(s_own - m)
            l = jnp.sum(p, axis=-1, keepdims=True) + p_own
            pb = p.astype(MX)
            zero = jnp.zeros_like(pb)
            o = _dot(jnp.where(cur_m, pb, zero), vc) + _dot(jnp.where(cur_m, zero, pb), vp)
            o = (o + p_own * jnp.concatenate([vp, vp], axis=0).astype(f32)) / l
            o_ref[pl.ds(r0, BLK), :] = _unstack_heads(o).astype(o_ref.dtype)
            l_ref[pl.ds(r0, BLK), :] = _unstack_heads(jnp.broadcast_to(m + jnp.log(l), (2 * BLK, LANES)))
            return carry

        lax.fori_loop(0, T // BLK, step, 0, unroll=8)

    spec = pl.BlockSpec((T, LANES), lambda j: (0, j))
    return pl.pallas_call(
        body, name=f"attn_fwd{g}", grid=(AO // LANES,),
        in_specs=[spec] * 3, out_specs=[spec] * 2,
        out_shape=[jax.ShapeDtypeStruct((T, AO), ACT), jax.ShapeDtypeStruct((T, AO), f32)],
        compiler_params=_cp(("parallel",), 56),
    )(qf, kf, vf)


def _group_weights(lses):
    m = jnp.maximum(jnp.maximum(lses[0], lses[1]), lses[2])
    e = [jnp.exp(l - m) for l in lses]
    inv = 1.0 / (e[0] + e[1] + e[2])
    return [x * inv for x in e]


def _fold_specs(T, tm):
    specs = []
    for _, d in GROUPS:
        specs.append(pl.BlockSpec((d, tm // d, AO), lambda i: (0, i, 0)))
    return specs


def combine_fwd(os_, lses):
    T = os_[0].shape[0] * os_[0].shape[1]
    tm = TM_FOLD

    def body(o0, o1, o2, l0, l1, l2, y_ref, nat):
        o = [_unfold_in(nat, r, d) for r, (_, d) in zip((o0, o1, o2), GROUPS)]
        ls = [_unfold_in(nat, r, d) for r, (_, d) in zip((l0, l1, l2), GROUPS)]
        w = _group_weights(ls)
        y_ref[...] = (w[0] * o[0] + w[1] * o[1] + w[2] * o[2]).astype(y_ref.dtype)

    specs = _fold_specs(T, tm)
    return pl.pallas_call(
        body, name="combine_fwd", grid=(T // tm,),
        in_specs=specs + specs, out_specs=pl.BlockSpec((tm, AO), lambda i: (i, 0)),
        out_shape=jax.ShapeDtypeStruct((T, AO), MX),
        scratch_shapes=[pltpu.VMEM((AO // LANES, tm, LANES), f32)],
        compiler_params=_cp(("parallel",)),
    )(*os_, *lses)


TM_MIX = 512


def mix_out_fwd(proj, ya, yb, yc, x0, pa, pb, pc, wo, g1, b1):
    T = x0.shape[0]
    tm = min(TM_MIX, T)

    def body(gt, ya_r, yb_r, yc_r, x0_r, pa_r, pb_r, pc_r, wo_r, g_r, b_r, mabc, m_o, r1_o, x1_o, x1b_o):
        ma = _dot(ya_r[...], pa_r[...])
        ybv = yb_r[...]
        mb = jnp.concatenate([_dot(ybv, pb_r[k]) for k in range(NCHIP)], axis=1)
        mc = _dot(yc_r[...], pc_r[...])
        m = jnp.zeros((tm, D), f32)
        for j, mm in enumerate((ma, mb, mc)):
            mabc[:, j * D:(j + 1) * D] = mm.astype(mabc.dtype)
            m = m + _sigmoid(gt[:, j * D:(j + 1) * D].astype(f32)) * mm
        mb16 = m.astype(MX)
        m_o[...] = mb16
        r1 = ALPHA * x0_r[...] + _dot(mb16, wo_r[...])
        r1_o[...] = r1
        xhat, _ = _ln_stats(r1)
        x1 = xhat * g_r[...] + b_r[...]
        x1_o[...] = x1
        x1b_o[...] = x1.astype(MX)

    full = lambda shape: pl.BlockSpec(shape, lambda i: (0,) * len(shape), pipeline_mode=pl.Buffered(1))
    tile = lambda w: pl.BlockSpec((tm, w), lambda i: (i, 0))
    return pl.pallas_call(
        body, name="mix_out_fwd", grid=(T // tm,),
        in_specs=[tile(3 * D), tile(D), tile(AO), tile(D), tile(D), full((D, D)), full((NCHIP, AO, D // NCHIP)),
                  full((D, D)), full((D, D)), full((1, D)), full((1, D))],
        out_specs=[tile(3 * D), tile(D), tile(D), tile(D), tile(D)],
        out_shape=[jax.ShapeDtypeStruct((T, 3 * D), MX), jax.ShapeDtypeStruct((T, D), MX),
                   jax.ShapeDtypeStruct((T, D), f32), jax.ShapeDtypeStruct((T, D), f32), jax.ShapeDtypeStruct((T, D), MX)],
        compiler_params=_cp(("parallel",), 56),
    )(proj, ya, yb, yc, x0, pa, pb, pc, wo, g1, b1)


TM_FF = 512
TM_FFB = 512
ROW_CHUNK = 64


def ffn_up_fwd(x1, wg, wu):
    T = x1.shape[0]
    tm = min(TM_FFB, T)

    def body(x_r, wg_r, wu_r, g_o, u_o, h_o, gs, us):
        xb = x_r[...].astype(MX)
        for k in range(NCHIP):
            gs[...] = _dot(xb, wg_r[k])
            us[...] = _dot(xb, wu_r[k])
            for r in range(0, tm, ROW_CHUNK):
                rows = pl.ds(r, ROW_CHUNK)
                gate, up = gs[rows, :], us[rows, :]
                g_o[k, rows, :] = gate.astype(g_o.dtype)
                u_o[k, rows, :] = up.astype(u_o.dtype)
                h_o[k, rows, :] = (gate * _sigmoid(gate) * up).astype(h_o.dtype)

    wspec = pl.BlockSpec((NCHIP, D, FB), lambda i: (0, 0, 0), pipeline_mode=pl.Buffered(1))
    ospec = pl.BlockSpec((NCHIP, tm, FB), lambda i: (0, i, 0))
    return pl.pallas_call(
        body, name="ffn_up_fwd", grid=(T // tm,),
        in_specs=[pl.BlockSpec((tm, D), lambda i: (i, 0)), wspec, wspec],
        out_specs=[ospec] * 3,
        out_shape=[jax.ShapeDtypeStruct((NCHIP, T, FB), ACT)] * 2 + [jax.ShapeDtypeStruct((NCHIP, T, FB), MX)],
        scratch_shapes=[pltpu.VMEM((tm, FB), f32)] * 2,
        compiler_params=_cp(("parallel",)),
    )(x1, wg, wu)


def ffn_down_fwd(hh, wd, x1, g2, b2):
    T = x1.shape[0]
    tm = min(TM_FF, T)

    def body(h_r, w_r, x_r, g_r, b_r, r2_o, x2_o):
        r2 = ALPHA * x_r[...]
        for k in range(NCHIP):
            r2 = r2 + _dot(h_r[k], w_r[k])
        r2_o[...] = r2
        xhat, _ = _ln_stats(r2)
        x2_o[...] = xhat * g_r[...] + b_r[...]

    tile = pl.BlockSpec((tm, D), lambda i: (i, 0))
    vec = pl.BlockSpec((1, D), lambda i: (0, 0))
    return pl.pallas_call(
        body, name="ffn_down_fwd", grid=(T // tm,),
        in_specs=[pl.BlockSpec((NCHIP, tm, FB), lambda i: (0, i, 0)), pl.BlockSpec((NCHIP, FB, D), lambda i: (0, 0, 0), pipeline_mode=pl.Buffered(1)),
                  tile, vec, vec],
        out_specs=[tile, tile], out_shape=[jax.ShapeDtypeStruct((T, D), f32)] * 2,
        compiler_params=_cp(("parallel",)),
    )(hh, wd, x1, g2, b2)


def loss_grad(y, tgt):
    T = y.shape[0]
    tm = min(512, T)

    def body(y_r, t_r, l_o, dy_o):
        e = y_r[...] - t_r[...]
        dy_o[...] = e * (1.0 / D)

        @pl.when(pl.program_id(0) == 0)
        def _():
            l_o[...] = jnp.zeros_like(l_o)
        l_o[...] += (0.5 / D) * jnp.sum(e * e)

    tile = pl.BlockSpec((tm, D), lambda i: (i, 0))
    return pl.pallas_call(
        body, name="loss_grad", grid=(T // tm,),
        in_specs=[tile, tile], out_specs=[pl.BlockSpec((8, LANES), lambda i: (0, 0)), tile],
        out_shape=[jax.ShapeDtypeStruct((8, LANES), f32), jax.ShapeDtypeStruct((T, D), f32)],
        compiler_params=_cp(("arbitrary",)),
    )(y, tgt)


def ffn_down_bwd(dx2, r2, g2, wd, gate, up):
    T = dx2.shape[0]
    tm = min(TM_FFB, T)

    def body(dx_r, r_r, g_r, w_r, ga_r, up_r, dr_o, drb_o, dg_o, du_o, dlg_o, dlb_o, hs):
        i = pl.program_id(0)
        xhat, rstd = _ln_stats(r_r[...])
        dx = dx_r[...]
        _acc_rows(dlg_o, i == 0, dx * xhat)
        _acc_rows(dlb_o, i == 0, dx)
        dr = _ln_bwd(dx, xhat, rstd, g_r[...])
        dr_o[...] = dr
        drb = dr.astype(MX)
        drb_o[...] = drb
        for k in range(NCHIP):
            hs[...] = _dot(drb, w_r[k], NT)
            for r in range(0, tm, ROW_CHUNK):
                rows = pl.ds(r, ROW_CHUNK)
                dhh, gate_v, up_v = hs[rows, :], ga_r[k, rows, :].astype(f32), up_r[k, rows, :].astype(f32)
                sg = _sigmoid(gate_v)
                dg_o[k, rows, :] = (dhh * up_v * sg * (1.0 + gate_v * (1.0 - sg))).astype(dg_o.dtype)
                du_o[k, rows, :] = (dhh * gate_v * sg).astype(du_o.dtype)

    tile = pl.BlockSpec((tm, D), lambda i: (i, 0))
    vec = pl.BlockSpec((1, D), lambda i: (0, 0))
    blk = pl.BlockSpec((NCHIP, tm, FB), lambda i: (0, i, 0))
    return pl.pallas_call(
        body, name="ffn_down_bwd", grid=(T // tm,),
        in_specs=[tile, tile, vec, pl.BlockSpec((NCHIP, FB, D), lambda i: (0, 0, 0), pipeline_mode=pl.Buffered(1)), blk, blk],
        out_specs=[tile, tile, blk, blk, vec, vec],
        out_shape=[jax.ShapeDtypeStruct((T, D), f32), jax.ShapeDtypeStruct((T, D), MX)]
        + [jax.ShapeDtypeStruct((NCHIP, T, FB), MX)] * 2 + [jax.ShapeDtypeStruct((1, D), f32)] * 2,
        scratch_shapes=[pltpu.VMEM((tm, FB), f32)],
        compiler_params=_cp(("arbitrary",), 58),
    )(dx2, r2, g2, wd, gate, up)


def ffn_up_bwd(dr2, dgate, dup, wg, wu, r1, g1):
    T = dr2.shape[0]
    tm = min(TM_FFB, T)

    def body(dr2_r, dg_r, du_r, wg_r, wu_r, r1_r, g_r, dr1_o, dr1b_o, dlg_o, dlb_o):
        i = pl.program_id(0)
        dx = ALPHA * dr2_r[...]
        for k in range(NCHIP):
            dx = dx + _dot(dg_r[k], wg_r[k], NT) + _dot(du_r[k], wu_r[k], NT)
        xhat, rstd = _ln_stats(r1_r[...])
        _acc_rows(dlg_o, i == 0, dx * xhat)
        _acc_rows(dlb_o, i == 0, dx)
        dr1 = _ln_bwd(dx, xhat, rstd, g_r[...])
        dr1_o[...] = dr1
        dr1b_o[...] = dr1.astype(MX)

    tile = pl.BlockSpec((tm, D), lambda i: (i, 0))
    vec = pl.BlockSpec((1, D), lambda i: (0, 0))
    blk = pl.BlockSpec((NCHIP, tm, FB), lambda i: (0, i, 0))
    wspec = pl.BlockSpec((NCHIP, D, FB), lambda i: (0, 0, 0), pipeline_mode=pl.Buffered(1))
    return pl.pallas_call(
        body, name="ffn_up_bwd", grid=(T // tm,),
        in_specs=[tile, blk, blk, wspec, wspec, tile, vec],
        out_specs=[tile, tile, vec, vec],
        out_shape=[jax.ShapeDtypeStruct((T, D), f32), jax.ShapeDtypeStruct((T, D), MX)]
        + [jax.ShapeDtypeStruct((1, D), f32)] * 2,
        compiler_params=_cp(("arbitrary",), 58),
    )(dr2, dgate, dup, wg, wu, r1, g1)


TM_MIXB = 256


def mix_out_bwd(dr1, proj, mabc, wo, pa, pb, pc, ya, yb, yc, m):
    T = dr1.shape[0]
    tm = min(TM_MIXB, T)
    cb = D // NCHIP

    def body(dr_r, gt, mabc_r, wo_r, pa_r, pb_r, pc_r, ya_r, yb_r, yc_r, m_r,
             dgt_o, dya_o, dyb_o, dyc_o, dwo_o, dpa_o, dpb_o, dpc_o):
        @pl.when(pl.program_id(0) == 0)
        def _():
            for o in (dwo_o, dpa_o, dpb_o, dpc_o):
                o[...] = jnp.zeros_like(o)

        dr = dr_r[...].astype(MX)
        dm = _dot(dr, wo_r[...], NT)
        dmx = []
        for j in range(3):
            s = _sigmoid(gt[:, j * D:(j + 1) * D].astype(f32))
            dmx.append((dm * s).astype(MX))
            dgt_o[:, j * D:(j + 1) * D] = (dm * mabc_r[:, j * D:(j + 1) * D].astype(f32) * s * (1.0 - s)).astype(dgt_o.dtype)
        dya_o[...] = _dot(dmx[0], pa_r[...], NT).astype(dya_o.dtype)
        dyb = jnp.zeros((tm, AO), f32)
        for k in range(NCHIP):
            dyb = dyb + _dot(dmx[1][:, k * cb:(k + 1) * cb], pb_r[k], NT)
            dpb_o[k] += _dot(yb_r[...], dmx[1][:, k * cb:(k + 1) * cb], TN)
        dyb_o[...] = dyb.astype(dyb_o.dtype)
        dyc_o[...] = _dot(dmx[2], pc_r[...], NT).astype(dyc_o.dtype)
        dwo_o[...] += _dot(m_r[...], dr, TN)
        dpa_o[...] += _dot(ya_r[...], dmx[0], TN)
        dpc_o[...] += _dot(yc_r[...], dmx[2], TN)

    full = lambda shape: pl.BlockSpec(shape, lambda i: (0,) * len(shape), pipeline_mode=pl.Buffered(1))
    tile = lambda w: pl.BlockSpec((tm, w), lambda i: (i, 0))
    return pl.pallas_call(
        body, name="mix_out_bwd", grid=(T // tm,),
        in_specs=[tile(D), tile(3 * D), tile(3 * D), full((D, D)), full((D, D)), full((NCHIP, AO, cb)), full((D, D)),
                  tile(D), tile(AO), tile(D), tile(D)],
        out_specs=[tile(3 * D), tile(D), tile(AO), tile(D), full((D, D)), full((D, D)), full((NCHIP, AO, cb)), full((D, D))],
        out_shape=[jax.ShapeDtypeStruct((T, 3 * D), MX), jax.ShapeDtypeStruct((T, D), ACT),
                   jax.ShapeDtypeStruct((T, AO), ACT), jax.ShapeDtypeStruct((T, D), ACT), jax.ShapeDtypeStruct((D, D), f32),
                   jax.ShapeDtypeStruct((D, D), f32), jax.ShapeDtypeStruct((NCHIP, AO, cb), f32), jax.ShapeDtypeStruct((D, D), f32)],
        compiler_params=_cp(("arbitrary",), 58),
    )(dr1, proj, mabc, wo, pa, pb, pc, ya, yb, yc, m)


def transpose_cast(x):
    T = x.shape[0]
    tm = min(512, T)

    def body(x_r, o_r):
        o_r[...] = x_r[...].T.astype(o_r.dtype)

    return pl.pallas_call(
        body, name="transpose_cast", grid=(T // tm,),
        in_specs=[pl.BlockSpec((tm, D), lambda i: (i, 0))], out_specs=pl.BlockSpec((D, tm), lambda i: (0, i)),
        out_shape=jax.ShapeDtypeStruct((D, T), MX), compiler_params=_cp(("parallel",)),
    )(x)


def tn_matmul(name, a, b, a_spec, b_spec, out_shape, out_spec, grid):
    nt = len(grid) - 1

    def body(a_r, b_r, o_r):
        @pl.when(pl.program_id(nt) == 0)
        def _():
            o_r[...] = jnp.zeros_like(o_r)
        av = a_r[...].reshape(a_r.shape[-2:]).astype(MX)
        bv = b_r[...].reshape(b_r.shape[-2:]).astype(MX)
        o_r[...] += _dot(av, bv, TN).reshape(o_r.shape)

    return pl.pallas_call(
        body, name=name, grid=grid, in_specs=[a_spec, b_spec], out_specs=out_spec,
        out_shape=jax.ShapeDtypeStruct(out_shape, f32),
        compiler_params=_cp(("parallel",) * nt + ("arbitrary",), 56),
    )(a, b)


def attn_pre_bwd(dyb, os_, lses, ones):
    T = dyb.shape[0]
    tm = TM_FOLD

    def body(dy_r, o0, o1, o2, l0, l1, l2, ones_r, d0, d1, d2, f0, f1, f2, nat):
        o = [_unfold_in(nat, r, d) for r, (_, d) in zip((o0, o1, o2), GROUPS)]
        ls = [_unfold_in(nat, r, d) for r, (_, d) in zip((l0, l1, l2), GROUPS)]
        w = _group_weights(ls)
        dy = dy_r[...].astype(f32)
        t = dy * (w[0] * o[0] + w[1] * o[1] + w[2] * o[2])
        hi = t.astype(MX)
        lo = (t - hi.astype(f32)).astype(MX)
        c = _dot(hi, ones_r[...]) + _dot(lo, ones_r[...])
        for wg, do_o, df_o, (_, d) in zip(w, (d0, d1, d2), (f0, f1, f2), GROUPS):
            _fold_out(nat, wg * dy, do_o, d)
            _fold_out(nat, -wg * c, df_o, d)

    specs = _fold_specs(T, tm)
    return pl.pallas_call(
        body, name="attn_pre_bwd", grid=(T // tm,),
        in_specs=[pl.BlockSpec((tm, AO), lambda i: (i, 0))] + specs + specs + [pl.BlockSpec((AO, AO), lambda i: (0, 0))],
        out_specs=specs + specs,
        out_shape=[jax.ShapeDtypeStruct((d, T // d, AO), MX) for _, d in GROUPS]
        + [jax.ShapeDtypeStruct((d, T // d, AO), f32) for _, d in GROUPS],
        scratch_shapes=[pltpu.VMEM((AO // LANES, tm, LANES), f32)],
        compiler_params=_cp(("parallel",), 56),
    )(dyb, *os_, *lses, ones)


def _head_ones():
    i = jnp.arange(AO) // HD
    return (i[:, None] == i[None, :]).astype(MX)


BWD_BLOCKS = 8


def attn_bwd(qf, kf, vf, dof, lse, df, g, nb):
    T = qf.shape[0]

    def body(q_ref, k_ref, v_ref, do_ref, l_ref, d_ref, dq_ref, dk_ref, dv_ref):
        prev_m, cur_m = _window_masks()

        def head_col(ref, r0):
            v = ref[pl.ds(r0, BLK), :]
            return jnp.concatenate([v[:, 0:1], v[:, HD:HD + 1]], axis=0)

        def step(b, carry):
            dk_c, dv_c = carry
            r0 = pl.multiple_of(b * BLK, BLK)
            rp = pl.multiple_of(jnp.maximum(b - 1, 0) * BLK, BLK)
            qs, dos = _stack_heads(q_ref[pl.ds(r0, BLK), :]), _stack_heads(do_ref[pl.ds(r0, BLK), :])
            k2, v2 = _two_blocks(k_ref, b), _two_blocks(v_ref, b)
            valid = cur_m | (prev_m & ((b % nb) != 0))
            p = jnp.where(valid, jnp.exp(_dot(qs, k2, NT) - head_col(l_ref, r0)), 0.0)
            ds = (p * (_dot(dos, v2, NT) + head_col(d_ref, r0))).astype(MX)
            dq_ref[pl.ds(r0, BLK), :] = _unstack_heads(_dot(ds, k2)).astype(dq_ref.dtype)
            dk2 = _dot(ds, qs, TN)
            dv2 = _dot(p.astype(MX), dos, TN)
            dk_ref[pl.ds(rp, BLK), :] = (dk_c + dk2[:BLK]).astype(dk_ref.dtype)
            dv_ref[pl.ds(rp, BLK), :] = (dv_c + dv2[:BLK]).astype(dv_ref.dtype)
            return dk2[BLK:], dv2[BLK:]

        zero = jnp.zeros((BLK, LANES), f32)

        def steps(i, carry):
            for j in range(BWD_BLOCKS):
                carry = step(BWD_BLOCKS * i + j, carry)
            return carry

        dk_c, dv_c = lax.fori_loop(0, T // BLK // BWD_BLOCKS, steps, (zero, zero))
        dk_ref[pl.ds(T - BLK, BLK), :] = dk_c.astype(dk_ref.dtype)
        dv_ref[pl.ds(T - BLK, BLK), :] = dv_c.astype(dv_ref.dtype)

    spec = pl.BlockSpec((T, LANES), lambda j: (0, j))
    return pl.pallas_call(
        body, name=f"attn_bwd{g}", grid=(AO // LANES,),
        in_specs=[spec] * 6, out_specs=[spec] * 3,
        out_shape=[jax.ShapeDtypeStruct((T, AO), MX)] * 3,
        compiler_params=_cp(("parallel",), 60),
    )(qf, kf, vf, dof, lse, df)


def unfold_rope_bwd(dqf, dkf, dvf, cos_t, sin_t, g, d):
    T = dqf.shape[0] * dqf.shape[1]
    tm = TM_FOLD

    def body(q_r, k_r, v_r, c_ref, s_ref, o_ref, nat):
        cos, sin = _tile4(c_ref[...]), _tile4(s_ref[...])
        for part, ref, scale in ((0, q_r, HD ** -0.5), (1, k_r, 1.0), (2, v_r, None)):
            x = _unfold_in(nat, ref, d)
            if scale is not None:
                x = (x * cos - _swap_halves(x) * sin) * scale
            o_ref[:, part * AO:(part + 1) * AO] = x.astype(o_ref.dtype)

    fold_spec = pl.BlockSpec((d, tm // d, AO), lambda i: (0, i, 0))
    tab = pl.BlockSpec((tm, LANES), lambda i: (i, 0))
    return pl.pallas_call(
        body, name=f"unfold_rope_bwd{g}", grid=(T // tm,),
        in_specs=[fold_spec] * 3 + [tab, tab],
        out_specs=pl.BlockSpec((tm, 3 * AO), lambda i: (i, 0)),
        out_shape=jax.ShapeDtypeStruct((T, 3 * AO), MX),
        scratch_shapes=[pltpu.VMEM((AO // LANES, tm, LANES), f32)],
        compiler_params=_cp(("parallel",)),
    )(dqf, dkf, dvf, cos_t, sin_t)


CONV_CHUNK = 32


def conv_bwd(dya, proj, conv_w):
    T = dya.shape[0]
    tm = TM_AC
    last = T // tm - 1

    def body(dy_r, bch, hprev, dy_next, b_next, cw, d_o, dw_o, zs, ds):
        i = pl.program_id(0)
        ch = CONV_CHUNK
        hz = hprev[:, :D].astype(f32) * hprev[:, D:].astype(f32)
        zs[0:HALO, :] = jnp.where(i > 0, hz, 0.0)
        ds[tm:tm + HALO, :] = jnp.where(i < last, dy_next[...].astype(f32) * b_next[...].astype(f32), 0.0)
        for r in range(0, tm, ch):
            zs[HALO + r:HALO + r + ch, :] = bch[r:r + ch, D:2 * D].astype(f32) * bch[r:r + ch, 2 * D:].astype(f32)
            ds[r:r + ch, :] = dy_r[r:r + ch, :].astype(f32) * bch[r:r + ch, :D].astype(f32)

        @pl.when(i == 0)
        def _():
            dw_o[...] = jnp.zeros_like(dw_o)

        sums = [jnp.zeros((1, D), f32) for _ in range(3)]
        for r in range(0, tm, ch):
            z2, z1, z = (zs[HALO + r - s:HALO + r - s + ch, :] for s in (2, 1, 0))
            dcv, d1, d2 = (ds[r + s:r + s + ch, :] for s in (0, 1, 2))
            cv = cw[0:1, :] * z2 + cw[1:2, :] * z1 + cw[2:3, :] * z
            dz = cw[2:3, :] * dcv + cw[1:2, :] * d1 + cw[0:1, :] * d2
            d_o[r:r + ch, :D] = (dy_r[r:r + ch, :].astype(f32) * cv).astype(d_o.dtype)
            d_o[r:r + ch, D:2 * D] = (dz * bch[r:r + ch, 2 * D:].astype(f32)).astype(d_o.dtype)
            d_o[r:r + ch, 2 * D:] = (dz * bch[r:r + ch, D:2 * D].astype(f32)).astype(d_o.dtype)
            for k, zz in enumerate((z2, z1, z)):
                sums[k] = sums[k] + jnp.sum(dcv * zz, axis=0, keepdims=True)
        for k in range(3):
            dw_o[k:k + 1, :] += sums[k]

    nh = tm // HALO
    return pl.pallas_call(
        body, name="conv_bwd", grid=(T // tm,),
        in_specs=[pl.BlockSpec((tm, D), lambda i: (i, 0)), pl.BlockSpec((tm, 3 * D), lambda i: (i, 1)),
                  pl.BlockSpec((HALO, 2 * D), lambda i: (jnp.maximum(i * nh - 1, 0), 2)),
                  pl.BlockSpec((HALO, D), lambda i: (jnp.minimum((i + 1) * nh, T // HALO - 1), 0)),
                  pl.BlockSpec((HALO, D), lambda i: (jnp.minimum((i + 1) * nh, T // HALO - 1), 3)),
                  pl.BlockSpec((3, D), lambda i: (0, 0))],
        out_specs=[pl.BlockSpec((tm, 3 * D), lambda i: (i, 0)), pl.BlockSpec((3, D), lambda i: (0, 0))],
        out_shape=[jax.ShapeDtypeStruct((T, 3 * D), MX), jax.ShapeDtypeStruct((3, D), f32)],
        scratch_shapes=[pltpu.VMEM((HALO + tm, D), f32), pltpu.VMEM((tm + HALO, D), f32)],
        compiler_params=_cp(("arbitrary",)),
    )(dya, proj, proj, dya, proj, conv_w)


def gmlp_bwd(dyc, proj, wst, bsx, lg, lb):
    T = dyc.shape[0]
    tm = TM_AC
    last = T // tm - 1

    def body(dy_r, u0, u1, v0, v1, ws, bs, lg_r, lb_r, d_o, dws_o, dbs_o, dlg_o, dlb_o, bacc):
        i = pl.program_id(0)
        up = jnp.concatenate([u0[...], u1[...]], axis=1).astype(f32)
        vp = jnp.concatenate([v0[...], v1[...]], axis=1).astype(f32)
        u, du = _gelu_and_grad(up)
        gv, dgv = _gelu_and_grad(vp)
        u, vn, xhat, rstd, sp = _gmlp_fwd(up, vp, ws, bs, lg_r[...], lb_r[...], u, gv)
        dy = dy_r[...].astype(f32)
        d_o[:, :D] = (dy * sp * du).astype(d_o.dtype)
        dsp = dy * u
        dspb, vnb = dsp.astype(MX), vn.astype(MX)

        @pl.when(i == 0)
        def _():
            dws_o[...] = jnp.zeros_like(dws_o)
            bacc[...] = jnp.zeros_like(bacc)

        rows = []
        for c in range(tm // BLK):
            r = slice(c * BLK, (c + 1) * BLK)
            cols = []
            for g in range(8):
                cs = slice(g * BLK, (g + 1) * BLK)
                dws_o[g] += _dot(dspb[r, cs], vnb[r, cs], NT)
                bacc[g] += dsp[r, cs]
                cols.append(_dot(ws[g], dspb[r, cs], TN))
            rows.append(jnp.concatenate(cols, axis=1))
        dvn = jnp.concatenate(rows, axis=0)
        _acc_rows(dlg_o, i == 0, dvn * xhat)
        _acc_rows(dlb_o, i == 0, dvn)
        d_o[:, D:] = (_ln_bwd(dvn, xhat, rstd, lg_r[...]) * dgv).astype(d_o.dtype)

        @pl.when(i == last)
        def _():
            row = lax.broadcasted_iota(jnp.int32, (BLK, BLK), 0)
            col = lax.broadcasted_iota(jnp.int32, (BLK, BLK), 1)
            ones = jnp.ones((8, BLK), MX)
            for g in range(8):
                dws_o[g] = jnp.where(col <= row, dws_o[g], 0.0)
                a = bacc[g]
                hi = a.astype(MX)
                lo = (a - hi.astype(f32)).astype(MX)
                dbs_o[g:g + 1, :] = (_dot(ones, hi, NT) + _dot(ones, lo, NT))[0:1, :]

    full = lambda shape: pl.BlockSpec(shape, lambda i: (0,) * len(shape))
    return pl.pallas_call(
        body, name="gmlp_bwd", grid=(T // tm,),
        in_specs=[pl.BlockSpec((tm, D), lambda i: (i, 0)), *_uv_specs(), full((8, BLK, BLK)), full((8, BLK, BLK)),
                  full((1, D)), full((1, D))],
        out_specs=[pl.BlockSpec((tm, 2 * D), lambda i: (i, 0)), full((8, BLK, BLK)), full((8, BLK)), full((1, D)), full((1, D))],
        out_shape=[jax.ShapeDtypeStruct((T, 2 * D), MX), jax.ShapeDtypeStruct((8, BLK, BLK), f32),
                   jax.ShapeDtypeStruct((8, BLK), f32), jax.ShapeDtypeStruct((1, D), f32), jax.ShapeDtypeStruct((1, D), f32)],
        scratch_shapes=[pltpu.VMEM((8, BLK, BLK), f32)],
        compiler_params=_cp(("arbitrary",)),
    )(dyc, proj, proj, proj, proj, wst, bsx, lg, lb)


PART_TILES = (6, 6, 3, 3, 3, 4)
PART_START = (0, 6, 12, 15, 18, 21)
TJ = 512


def _part_specs(tm, rows_axis):
    specs = []
    for n, s in zip(PART_TILES, PART_START):
        def imap(*idx, n=n, s=s):
            i, j = idx[rows_axis], idx[1 - rows_axis]
            inside = (j >= s) & (j < s + n)
            return (jnp.where(inside, i, 0), jnp.clip(j - s, 0, n - 1))
        specs.append(pl.BlockSpec((tm, TJ), imap))
    return specs


def _with_part(j, refs, fn):
    for r, n, s in zip(refs, PART_TILES, PART_START):
        @pl.when((j >= s) & (j < s + n))
        def _():
            fn(r[...])


def dx_in(dr1, parts, w, bias):
    T = dr1.shape[0]
    tm = min(2048, T)

    def body(dr_r, p0, p1, p2, p3, p4, p5, w_r, b_r, o_r):
        j = pl.program_id(1)

        @pl.when(j == 0)
        def _():
            o_r[...] = ALPHA * dr_r[...] + b_r[...]

        def acc(tile):
            o_r[...] += _dot(tile, w_r[...], NT)
        _with_part(j, (p0, p1, p2, p3, p4, p5), acc)

    once = dict(pipeline_mode=pl.Buffered(1))
    return pl.pallas_call(
        body, name="dx_in", grid=(T // tm, NIN // TJ),
        in_specs=[pl.BlockSpec((tm, D), lambda i, j: (i, 0), **once)] + _part_specs(tm, 0)
        + [pl.BlockSpec((D, TJ), lambda i, j: (0, j)), pl.BlockSpec((1, D), lambda i, j: (0, 0))],
        out_specs=pl.BlockSpec((tm, D), lambda i, j: (i, 0), **once),
        out_shape=jax.ShapeDtypeStruct((T, D), f32),
        compiler_params=_cp(("parallel", "arbitrary"), 56),
    )(dr1, *parts, w, bias)


def dw_in(x0t, parts):
    T = x0t.shape[1]
    tk = min(2048, T)

    def body(x_r, p0, p1, p2, p3, p4, p5, o_r):
        j, t = pl.program_id(0), pl.program_id(1)

        @pl.when(t == 0)
        def _():
            o_r[...] = jnp.zeros_like(o_r)

        def acc(tile):
            o_r[...] += _dot(x_r[:, pl.ds(pl.multiple_of(t * tk, tk), tk)], tile)
        _with_part(j, (p0, p1, p2, p3, p4, p5), acc)

    return pl.pallas_call(
        body, name="dw_in", grid=(NIN // TJ, T // tk),
        in_specs=[pl.BlockSpec((D, T), lambda j, t: (0, 0), pipeline_mode=pl.Buffered(1))] + _part_specs(tk, 1),
        out_specs=pl.BlockSpec((D, TJ), lambda j, t: (0, j)),
        out_shape=jax.ShapeDtypeStruct((D, NIN), f32),
        compiler_params=_cp(("parallel", "arbitrary"), 56),
    )(x0t, *parts)


def rope_tables(positions):
    half = HD // 2
    inv_freq = ROPE_THETA ** (-jnp.arange(half, dtype=f32) / half)
    ang = positions.astype(f32)[:, None] * inv_freq
    cos, sin = jnp.cos(ang), jnp.sin(ang)
    return jnp.tile(cos, (1, LANES // half)), jnp.tile(jnp.concatenate([-sin, sin], axis=1), (1, LANES // HD))


def _flat(a):
    return a.reshape(a.shape[0] * a.shape[1], a.shape[2])


def layer_fwd(x0, W, cos_t, sin_t):
    T = x0.shape[0]
    proj = mm_in(x0, W["w_in"], W["in_bias"])
    ya, yc = mix_ac_fwd(proj, W["conv_w"], W["wst"], W["bsx"], W["gmlp_ln_g"], W["gmlp_ln_b"])
    folded, os_, lses = [], [], []
    for g, (_, d) in enumerate(GROUPS):
        qf, kf, vf = fold_rope(proj, cos_t, sin_t, g, d)
        o, lse = attn_fwd(_flat(qf), _flat(kf), _flat(vf), g, T // d // BLK)
        folded.append((qf, kf, vf))
        os_.append(o.reshape(d, T // d, AO))
        lses.append(lse.reshape(d, T // d, AO))
    yb = combine_fwd(os_, lses)
    if "late" in W:
        W = {**W, **W["late"](yb)}
    mabc, m, r1, x1, x1b = mix_out_fwd(proj, ya, yb, yc, x0, W["p_a"], W["p_b"], W["p_c"], W["w_o"], W["ln1_g"], W["ln1_b"])
    gate, up, hh = ffn_up_fwd(x1b, W["w_gate"], W["w_up"])
    r2, x2 = ffn_down_fwd(hh, W["w_down"], x1, W["ln2_g"], W["ln2_b"])
    saved = dict(x0=x0, proj=proj, ya=ya, yb=yb, yc=yc, folded=folded, os=os_, lses=lses, mabc=mabc, m=m, r1=r1,
                 x1b=x1b, gate=gate, up=up, hh=hh, r2=r2)
    return x2, saved, W


def layer_bwd(dx2, S, W, cos_t, sin_t, on_grads=None):
    T = dx2.shape[0]
    tk = min(4096, T)
    G = {}
    dr2, dr2b, dgate, dup, G["ln2_g"], G["ln2_b"] = ffn_down_bwd(dx2, S["r2"], W["ln2_g"], W["w_down"], S["gate"], S["up"])
    blk_a = pl.BlockSpec((1, tk, FB), lambda k, t: (k, t, 0))
    row_b = pl.BlockSpec((tk, D), lambda k, t: (t, 0))
    G["w_down"] = tn_matmul("dw_down", S["hh"], dr2b, blk_a, row_b, (NCHIP, FB, D),
                            pl.BlockSpec((1, FB, D), lambda k, t: (k, 0, 0)), (NCHIP, T // tk))
    for nm, dv in (("w_gate", dgate), ("w_up", dup)):
        G[nm] = tn_matmul("d" + nm, dv, S["x1b"], blk_a, row_b, (NCHIP, FB, D),
                          pl.BlockSpec((1, FB, D), lambda k, t: (k, 0, 0)), (NCHIP, T // tk))
    dr1, dr1b, G["ln1_g"], G["ln1_b"] = ffn_up_bwd(dr2, dgate, dup, W["w_gate"], W["w_up"], S["r1"], W["ln1_g"])
    dgates, dya, dyb, dyc, G["w_o"], G["p_a"], G["p_b"], G["p_c"] = mix_out_bwd(
        dr1b, S["proj"], S["mabc"], W["w_o"], W["p_a"], W["p_b"], W["p_c"], S["ya"], S["yb"], S["yc"], S["m"])
    conv_w = W["conv_w"]
    if on_grads is not None:
        conv_w = conv_w + on_grads({n: G[n] for n in BIG if n != "w_in"})
    dbch, G["conv_w"] = conv_bwd(dya, S["proj"], conv_w)
    duv, G["w_s"], G["b_s"], G["gmlp_ln_g"], G["gmlp_ln_b"] = gmlp_bwd(
        dyc, S["proj"], W["wst"], W["bsx"], W["gmlp_ln_g"], W["gmlp_ln_b"])
    ones = _head_ones()
    if on_grads is not None:
        small = {n: G[n] for n in VECS + ("b_s", "w_s", "conv_w")}
        ones = ones + on_grads(small).astype(MX)
    pre = attn_pre_bwd(dyb, S["os"], S["lses"], ones)
    dqkv = []
    for g, (_, d) in enumerate(GROUPS):
        qf, kf, vf = S["folded"][g]
        dqf, dkf, dvf = attn_bwd(_flat(qf), _flat(kf), _flat(vf), _flat(pre[g]), _flat(S["lses"][g]), _flat(pre[3 + g]),
                                 g, T // d // BLK)
        shp = (d, T // d, AO)
        dqkv.append(unfold_rope_bwd(dqf.reshape(shp), dkf.reshape(shp), dvf.reshape(shp), cos_t, sin_t, g, d))
    parts = (dgates, dbch, *dqkv, duv)
    G["w_in"] = dw_in(transpose_cast(S["x0"]), parts)
    bias = jnp.zeros((1, D), f32)
    if on_grads is not None:
        bias = bias + on_grads({"w_in": G["w_in"]})
    dx0 = dx_in(dr1, parts, W["w_in"], bias)
    started = on_grads({"dx": dx0}) if on_grads is not None else None
    return dx0, G, started


def prep_layer_weights(Wl):
    W = dict(Wl)
    tril = jnp.tril(jnp.ones((BLK, BLK), f32))
    W["wst"] = (Wl["w_s"] * tril[None]).astype(MX)
    W["bsx"] = jnp.broadcast_to(Wl["b_s"][:, :, None], (8, BLK, BLK))
    for n in ("gmlp_ln_g", "gmlp_ln_b", "ln1_g", "ln1_b", "ln2_g", "ln2_b"):
        W[n] = Wl[n].reshape(1, D)
    W["in_bias"] = jnp.zeros((1, NIN), f32) + Wl.get("after", 0.0)
    return W


def local_step(x, positions, target, layers, on_grads=None):
    cos_t, sin_t = rope_tables(positions)
    Ws, saved = [], []
    h = x
    for Wl in layers:
        h, S, W = layer_fwd(h, prep_layer_weights(Wl(h) if callable(Wl) else Wl), cos_t, sin_t)
        Ws.append(W)
        saved.append(S)
    lsum, dh = loss_grad(h, target)
    if on_grads is not None:
        on_grads(len(Ws), {"loss": lsum})
    grads = [None] * len(Ws)
    started = None
    for l in reversed(range(len(Ws))):
        W = Ws[l]
        if started is not None:
            W = dict(W, ln2_g=W["ln2_g"] + started)
        hook = functools.partial(on_grads, l) if on_grads is not None else None
        dh, grads[l], started = layer_bwd(dh, saved[l], W, cos_t, sin_t, hook)
    return lsum, dh, grads


MESH = pl.DeviceIdType.MESH
ANY = pl.BlockSpec(memory_space=pl.ANY)
BIG = ("w_in", "w_gate", "w_up", "w_down", "p_a", "p_b", "p_c", "w_o")
NBIG = len(BIG)


def _place():
    x, y, c = lax.axis_index("x"), lax.axis_index("y"), lax.axis_index("c")
    return x, y, c, 2 * x + y


def _rcopy(src, dst, send, recv, dev):
    return pltpu.make_async_remote_copy(src_ref=src, dst_ref=dst, send_sem=send, recv_sem=recv, device_id=dev,
                                        device_id_type=MESH)


def _cols(ref, k, width):
    start = k * width if isinstance(k, int) else pl.multiple_of(k * width, LANES)
    return ref.at[:, pl.ds(start, width)]


CHUNK_BYTES = 1 << 20


def _pieces(shape, itemsize, nbytes=CHUNK_BYTES):
    rows, cols = shape[-2], shape[-1]
    per = max(16, nbytes // (cols * itemsize) // 16 * 16)
    out = []
    for lead in (range(shape[0]) if len(shape) == 3 else (None,)):
        for r in range(0, rows, per):
            sl = (pl.ds(r, min(per, rows - r)), slice(None))
            out.append(sl if lead is None else (lead,) + sl)
    return out


def _start_pieces(src, dst, make, nbytes=CHUNK_BYTES):
    for idx in _pieces(src.shape, jnp.dtype(src.dtype).itemsize, nbytes):
        make(src.at[idx], dst.at[idx]).start()


def gather_halves(shards):
    n = len(shards)

    def body(*refs):
        srcs, dsts = refs[:n], refs[n:2 * n]
        send, recv, own_send, own_recv = refs[2 * n:]
        x, y, c, k = _place()
        sib = (x, y, 1 - c)
        chips = [(1 - x, y), (x, 1 - y), (1 - x, 1 - y)]

        def slot(a, layer, pos):
            if a == 0:
                return _cols(dsts[0].at[layer], pos, WIN_SHARD)
            return dsts[a].at[pos, layer]

        def ici(a, j, src, dst):
            return _rcopy(src, dst, send.at[a, j], recv.at[a, j], (*chips[j], c))

        def d2d(a, j, src, dst):
            return _rcopy(src, dst, send.at[a, 3 + j], recv.at[a, 3 + j], sib)

        def own(a, layer, src, dst):
            return _rcopy(src, dst, own_send.at[a, layer], own_recv.at[a, layer], sib)

        for a in range(n):
            for j in range(3):
                _start_pieces(srcs[a].at[c], slot(a, c, k), functools.partial(ici, a, j))
        for a in range(n):
            for layer in range(DEPTH):
                _start_pieces(srcs[a].at[layer], slot(a, layer, k), functools.partial(own, a, layer))
        for a in range(n):
            for j, (cx, cy) in enumerate(chips):
                landed = slot(a, c, 2 * cx + cy)
                ici(a, j, landed, landed).wait_recv()
                _start_pieces(landed, landed, functools.partial(d2d, a, j))
        for a in range(n):
            for j, (cx, cy) in enumerate(chips):
                passed = slot(a, 1 - c, 2 * cx + cy)
                d2d(a, j, passed, passed).wait_recv()
                landed = slot(a, c, 2 * cx + cy)
                d2d(a, j, landed, landed).wait_send()
                ici(a, j, srcs[a].at[c], slot(a, c, k)).wait_send()
            for layer in range(DEPTH):
                own(a, layer, srcs[a].at[layer], slot(a, layer, k)).wait()

    outs = [jax.ShapeDtypeStruct((2, shards[0].shape[1], NIN), shards[0].dtype)]
    outs += [jax.ShapeDtypeStruct((NCHIP,) + s.shape, s.dtype) for s in shards[1:]]
    return pl.pallas_call(
        body, name="gather_halves", in_specs=[ANY] * n, out_specs=[ANY] * n, out_shape=outs,
        scratch_shapes=[pltpu.SemaphoreType.DMA((n, 6)), pltpu.SemaphoreType.DMA((n, 6)),
                        pltpu.SemaphoreType.DMA((n, DEPTH)), pltpu.SemaphoreType.DMA((n, DEPTH))],
    )(*shards)


def _gather_slot(dst, pos):
    return _cols(dst, pos, WIN_SHARD) if len(dst.shape) == 2 else dst.at[pos]


def _gather_copy(a, j, src, dst, send, recv, dev):
    return _rcopy(src, dst, send.at[a * NCHIP + j], recv.at[a * NCHIP + j], dev)


def gather_start(tag, shards, after):
    n = len(shards)

    def body(*refs):
        srcs, dsts = refs[:n], refs[n:2 * n]
        send, recv = refs[2 * n + len(after)], refs[2 * n + len(after) + 1]
        token = refs[-1]
        x, y, c, k = _place()
        peers = [(1 - x, y, c), (x, 1 - y, c), (1 - x, 1 - y, c), (x, y, 1 - c)]
        for a in range(n):
            for j, dev in enumerate(peers):
                _start_pieces(srcs[a], _gather_slot(dsts[a], k),
                              lambda s, d, a=a, j=j, dev=dev: _gather_copy(a, j, s, d, send, recv, dev))
        token[...] = jnp.zeros_like(token)

    gathered = [lax.empty((D, NIN) if s.shape == (D, WIN_SHARD) else (NCHIP,) + s.shape, s.dtype) for s in shards]
    ops = [pltpu.with_memory_space_constraint(v, pltpu.HBM) for v in list(shards) + gathered]
    sem = pltpu.SemaphoreType.DMA((n * NCHIP,))
    res = pl.pallas_call(
        body, name=f"gather_start{tag}", in_specs=[HBM] * (2 * n) + [ANY] * len(after),
        out_specs=[SEMS, SEMS] + [HBM] * (2 * n) + [pl.BlockSpec(memory_space=pltpu.VMEM)],
        out_shape=[sem, sem] + [pltpu.HBM(v.shape, v.dtype) for v in ops] + [jax.ShapeDtypeStruct((8, LANES), f32)],
        input_output_aliases={i: 2 + i for i in range(2 * n)},
        compiler_params=pltpu.CompilerParams(has_side_effects=EFFECT),
    )(*ops, *after)
    return res[0], res[1], res[2:2 + n], res[2 + n:2 + 2 * n], res[-1]


def gather_wait(tag, send, recv, shards, gathered, after):
    n = len(shards)

    def body(*refs):
        srcs, dsts = refs[:n], refs[n:2 * n]
        send_r, recv_r = refs[2 * n], refs[2 * n + 1]
        x, y, c, k = _place()
        peers = [(1 - x, y, c), (x, 1 - y, c), (1 - x, 1 - y, c), (x, y, 1 - c)]
        for a in range(n):
            for j, dev in enumerate(peers):
                _gather_copy(a, j, srcs[a], _gather_slot(dsts[a], k), send_r, recv_r, dev).wait_send()
                pos = 2 * dev[0] + dev[1]
                _gather_copy(a, j, srcs[a], _gather_slot(dsts[a], pos), send_r, recv_r, dev).wait_recv()

    ops = list(shards) + list(gathered)
    res = pl.pallas_call(
        body, name=f"gather_wait{tag}", in_specs=[HBM] * (2 * n) + [SEMS, SEMS] + [ANY] * len(after),
        out_specs=[HBM] * (2 * n), out_shape=[pltpu.HBM(v.shape, v.dtype) for v in ops],
        input_output_aliases={i: i for i in range(2 * n)},
        compiler_params=pltpu.CompilerParams(has_side_effects=EFFECT),
    )(*ops, send, recv, *after)
    return res[n:]


def _half(ref, h):
    rows = ref.shape[-2] // 2
    start = pl.multiple_of(h * rows, 16)
    if len(ref.shape) == 2:
        return ref.at[pl.ds(start, rows), :]
    return ref.at[:, pl.ds(start, rows), :]


HBM = pl.BlockSpec(memory_space=pltpu.HBM)
SEMS = pl.BlockSpec(memory_space=pltpu.SEMAPHORE)
EFFECT = pltpu.SideEffectType.DATAFLOW_SIDE_EFFECTING


def rs_pair_start(tag, grads, halves=True):
    n = len(grads)

    def body(*refs):
        g, theirs = refs[:n], refs[n:2 * n]
        send, recv = refs[2 * n], refs[2 * n + 1]
        x, y, c, _ = _place()
        for a in range(n):
            _start_pieces(_half(g[a], 1 - c) if halves else g[a], theirs[a],
                          lambda s, d, a=a: _rcopy(s, d, send.at[a], recv.at[a], (x, y, 1 - c)))
        refs[-1][...] = jnp.zeros_like(refs[-1])

    lands = [lax.empty(g.shape[:-2] + (g.shape[-2] // 2 if halves else g.shape[-2], g.shape[-1]), g.dtype) for g in grads]
    ops = [pltpu.with_memory_space_constraint(v, pltpu.HBM) for v in list(grads) + lands]
    sem = pltpu.SemaphoreType.DMA((n,))
    res = pl.pallas_call(
        body, name=f"rs_pair_start{tag}", in_specs=[HBM] * (2 * n),
        out_specs=[SEMS, SEMS] + [HBM] * (2 * n) + [pl.BlockSpec(memory_space=pltpu.VMEM)],
        out_shape=[sem, sem] + [pltpu.HBM(v.shape, v.dtype) for v in ops] + [jax.ShapeDtypeStruct((8, LANES), f32)],
        input_output_aliases={i: 2 + i for i in range(2 * n)},
        compiler_params=pltpu.CompilerParams(has_side_effects=EFFECT),
    )(*ops)
    return res[0], res[1], res[2:2 + n], res[2 + n:2 + 2 * n], res[-1]


def rs_pair_wait(tag, send, recv, grads, theirs, after, halves=True):
    n = len(grads)

    def body(*refs):
        g, land = refs[:n], refs[n:2 * n]
        send_r, recv_r = refs[2 * n], refs[2 * n + 1]
        x, y, c, _ = _place()
        for a in range(n):
            cp = _rcopy(_half(g[a], 1 - c) if halves else g[a], land[a], send_r.at[a], recv_r.at[a], (x, y, 1 - c))
            cp.wait_send()
            cp.wait_recv()

    ops = list(grads) + list(theirs)
    res = pl.pallas_call(
        body, name=f"rs_pair_wait{tag}", in_specs=[HBM] * (2 * n) + [SEMS, SEMS] + [ANY] * len(after),
        out_specs=[HBM] * (2 * n), out_shape=[pltpu.HBM(v.shape, v.dtype) for v in ops],
        input_output_aliases={i: i for i in range(2 * n)},
        compiler_params=pltpu.CompilerParams(has_side_effects=EFFECT),
    )(*ops, send, recv, *after)
    return res[:n], res[n:]


def _chip_piece(ref, k):
    return _cols(ref, k, WIN_SHARD) if len(ref.shape) == 2 else ref.at[k]


def _chip_copy(a, k, src, dst, send, recv, me, c):
    return _rcopy(src, dst, send.at[a * NCHIP + k], recv.at[a * NCHIP + me], (k // 2, k % 2, c))


def rs_chips_start(tag, sums):
    n = len(sums)

    def pshape(s):
        return (NCHIP, s[0], WIN_SHARD) if len(s) == 2 else s

    def body(*refs):
        s, land = refs[:n], refs[n:2 * n]
        send, recv = refs[2 * n], refs[2 * n + 1]
        token = refs[-1]
        x, y, c, me = _place()
        for k in range(NCHIP):
            @pl.when(me != k)
            def _():
                for a in range(n):
                    _start_pieces(_chip_piece(s[a], k), land[a].at[me],
                                  lambda src, dst, a=a: _chip_copy(a, k, src, dst, send, recv, me, c))
        token[...] = jnp.zeros_like(token)

    lands = [lax.empty(pshape(v.shape), v.dtype) for v in sums]
    ops = [pltpu.with_memory_space_constraint(v, pltpu.HBM) for v in list(sums) + lands]
    sem = pltpu.SemaphoreType.DMA((n * NCHIP,))
    res = pl.pallas_call(
        body, name=f"rs_chips_start{tag}", in_specs=[HBM] * (2 * n),
        out_specs=[SEMS, SEMS] + [HBM] * (2 * n) + [pl.BlockSpec(memory_space=pltpu.VMEM)],
        out_shape=[sem, sem] + [pltpu.HBM(v.shape, v.dtype) for v in ops] + [jax.ShapeDtypeStruct((8, LANES), f32)],
        input_output_aliases={i: 2 + i for i in range(2 * n)},
        compiler_params=pltpu.CompilerParams(has_side_effects=EFFECT),
    )(*ops)
    return res[0], res[1], res[2:2 + n], res[2 + n:2 + 2 * n], res[-1]


def rs_chips_wait(tag, send, recv, sums, lands, after):
    n = len(sums)

    def body(*refs):
        s, land = refs[:n], refs[n:2 * n]
        send_r, recv_r = refs[2 * n], refs[2 * n + 1]
        x, y, c, me = _place()
        for k in range(NCHIP):
            @pl.when(me != k)
            def _():
                for a in range(n):
                    piece = _chip_piece(s[a], k)
                    _chip_copy(a, k, piece, land[a].at[me], send_r, recv_r, me, c).wait_send()
                    _rcopy(piece, land[a].at[k], send_r.at[a * NCHIP + k], recv_r.at[a * NCHIP + k],
                           (k // 2, k % 2, c)).wait_recv()

    ops = list(sums) + list(lands)
    res = pl.pallas_call(
        body, name=f"rs_chips_wait{tag}", in_specs=[HBM] * (2 * n) + [SEMS, SEMS] + [ANY] * len(after),
        out_specs=[HBM] * (2 * n), out_shape=[pltpu.HBM(v.shape, v.dtype) for v in ops],
        input_output_aliases={i: i for i in range(2 * n)},
        compiler_params=pltpu.CompilerParams(has_side_effects=EFFECT),
    )(*ops, send, recv, *after)
    return res[:n], res[n:]


def _row_tile(rows, cols, itemsize=4, target=2 << 20):
    best = 8
    for t in range(8, rows + 1, 8):
        if rows % t == 0 and t * cols * itemsize <= target:
            best = t
    return best


GRAD_WIRE = jnp.bfloat16


def add_half(name, g, t, c):
    cols, half = t.shape[-1], t.shape[-2]
    nblk = 1 if t.ndim == 2 else t.shape[0]
    tr = _row_tile(half, cols)
    per = half // tr

    def body(c_ref, g_r, t_r, o_r):
        o_r[...] = (g_r[...] + t_r[...]).astype(o_r.dtype)

    tile_t = pl.BlockSpec((tr, cols), lambda i, c_ref: (i, 0))
    tile_g = pl.BlockSpec((tr, cols), lambda i, c_ref: ((i // per) * 2 * per + c_ref[0] * per + i % per, 0))
    out = pl.pallas_call(
        body, name=name, out_shape=jax.ShapeDtypeStruct((nblk * half, cols), GRAD_WIRE),
        grid_spec=pltpu.PrefetchScalarGridSpec(num_scalar_prefetch=1, grid=(nblk * per,), in_specs=[tile_g, tile_t],
                                               out_specs=tile_t),
        compiler_params=_cp(("parallel",)),
    )(c.reshape(1).astype(jnp.int32), g.reshape(nblk * 2 * half, cols), t.reshape(nblk * half, cols))
    return out.reshape(t.shape)


def add_chips(name, land, own):
    _, rows, cols = land.shape
    tr = _row_tile(rows, cols, target=1 << 20)

    def body(land_r, own_r, o_r):
        me = 2 * lax.axis_index("x") + lax.axis_index("y")
        for k in range(NCHIP):
            @pl.when(me == k)
            def _():
                acc = None
                for j in range(NCHIP):
                    t = (own_r[...] if j == k else land_r[j]).astype(f32)
                    acc = t if acc is None else acc + t
                o_r[...] = acc

    tile = pl.BlockSpec((tr, cols), lambda i: (i, 0))
    return pl.pallas_call(
        body, name=name, grid=(rows // tr,), in_specs=[pl.BlockSpec((NCHIP, tr, cols), lambda i: (0, i, 0)), tile],
        out_specs=tile, out_shape=jax.ShapeDtypeStruct((rows, cols), f32), compiler_params=_cp(("parallel",)),
    )(land, own)


def reduce_scatter_pair(tag, G):
    names = tuple(G)
    grads = [G[n] if G[n].ndim == 3 or n == "w_in" else G[n].reshape(NCHIP, D // NCHIP, D) for n in names]
    send, recv, grads, theirs, token = rs_pair_start(tag, grads)
    return (tag, names, send, recv, grads, theirs), token[0, 0]


def reduce_scatter_chips(state, after):
    c = lax.axis_index("c")
    tag, names, send, recv, grads, theirs = state
    grads, theirs = rs_pair_wait(tag, send, recv, grads, theirs, after)
    sums = [add_half(f"rs_add_pair{tag}_{n}", g, t, c) for n, g, t in zip(names, grads, theirs)]
    send, recv, sums, lands, token = rs_chips_start(tag, sums)
    return (tag, names, send, recv, sums, lands), token[0, 0]


def reduce_scatter_finish(state, after):
    me = 2 * lax.axis_index("x") + lax.axis_index("y")
    tag, names, send, recv, sums, lands = state
    sums, landed = rs_chips_wait(tag, send, recv, sums, lands, after)
    halves = []
    for n, s, v in zip(names, sums, landed):
        own = lax.dynamic_slice_in_dim(s, me * WIN_SHARD, WIN_SHARD, axis=1) if s.ndim == 2 else \
            lax.dynamic_index_in_dim(s, me, 0, keepdims=False)
        halves.append(add_chips(f"rs_add_chips{tag}_{n}", v, own))
    send, recv, halves, others, _ = rs_pair_start("_join" + tag, halves, halves=False)
    return tag, names, send, recv, halves, others


def reduce_scatter_join(state, after):
    tag, names, send, recv, halves, others = state
    halves, others = rs_pair_wait("_join" + tag, send, recv, halves, others, after, halves=False)
    return dict(zip(names, zip(halves, others)))


NDEV = 8


def _small_copy(r, src, dst, send, recv, x, y, c):
    return _rcopy(src, dst, send.at[r - 1], recv.at[r - 1], (x ^ (r >> 2), y ^ ((r >> 1) & 1), c ^ (r & 1)))


def small_start(pack):
    def body(p, land, send, recv, p_thru, land_thru, token):
        x, y, c, _ = _place()
        me = 4 * x + 2 * y + c
        for r in range(1, NDEV):
            _start_pieces(p, land.at[me], lambda s, d, r=r: _small_copy(r, s, d, send, recv, x, y, c), 128 << 10)
        token[...] = jnp.zeros_like(token)

    ops = [pltpu.with_memory_space_constraint(v, pltpu.HBM) for v in (pack, lax.empty((NDEV,) + pack.shape, f32))]
    sem = pltpu.SemaphoreType.DMA((NDEV - 1,))
    return pl.pallas_call(
        body, name="small_start", in_specs=[HBM, HBM],
        out_specs=[SEMS, SEMS, HBM, HBM, pl.BlockSpec(memory_space=pltpu.VMEM)],
        out_shape=[sem, sem] + [pltpu.HBM(v.shape, v.dtype) for v in ops] + [jax.ShapeDtypeStruct((8, LANES), f32)],
        input_output_aliases={0: 2, 1: 3}, compiler_params=pltpu.CompilerParams(has_side_effects=EFFECT),
    )(*ops)


def small_wait(send, recv, pack, land, after):
    def body(p, land_r, send_r, recv_r, *rest):
        x, y, c, _ = _place()
        me = 4 * x + 2 * y + c
        for r in range(1, NDEV):
            _small_copy(r, p, land_r.at[me], send_r, recv_r, x, y, c).wait_send()
            src = 4 * (x ^ (r >> 2)) + 2 * (y ^ ((r >> 1) & 1)) + (c ^ (r & 1))
            _small_copy(r, p, land_r.at[src], send_r, recv_r, x, y, c).wait_recv()

    return pl.pallas_call(
        body, name="small_wait", in_specs=[HBM, HBM, SEMS, SEMS] + [ANY] * len(after), out_specs=[HBM, HBM],
        out_shape=[pltpu.HBM(pack.shape, f32), pltpu.HBM(land.shape, f32)], input_output_aliases={0: 0, 1: 1},
        compiler_params=pltpu.CompilerParams(has_side_effects=EFFECT),
    )(pack, land, send, recv, *after)


def small_sum(land, pack):
    def body(land_r, p_r, o_r):
        me = 4 * lax.axis_index("x") + 2 * lax.axis_index("y") + lax.axis_index("c")
        for k in range(NDEV):
            @pl.when(me == k)
            def _():
                acc = None
                for d in range(NDEV):
                    t = p_r[...] if d == k else land_r[d]
                    acc = t if acc is None else acc + t
                o_r[...] = acc

    vm = pl.BlockSpec(memory_space=pltpu.VMEM)
    return pl.pallas_call(
        body, name="small_sum", in_specs=[vm, vm], out_specs=vm, out_shape=jax.ShapeDtypeStruct(pack.shape, f32),
        compiler_params=pltpu.CompilerParams(vmem_limit_bytes=40 << 20),
    )(land, pack)


def _adamw_math(w, g, m, v):
    m = ADAM_B1 * m + (1.0 - ADAM_B1) * g
    v = ADAM_B2 * v + (1.0 - ADAM_B2) * (g * g)
    m_hat = m / (1.0 - ADAM_B1 ** ADAM_STEP)
    v_hat = v / (1.0 - ADAM_B2 ** ADAM_STEP)
    return -ADAM_LR * (m_hat / (jnp.sqrt(v_hat) + ADAM_EPS) + ADAM_WD * w), m, v


def adamw_big(name, halves, w, m, v):
    _, R, C = w.shape
    tr = _row_tile(R // 2, C, target=1 << 20)
    nt = R // 2 // tr

    def body(a0, b0, a1, b1, w_r, m_r, v_r, g_o, d_o, m_o, v_o):
        mine = pl.program_id(1) == lax.axis_index("c")
        g = jnp.where(pl.program_id(0) == 0, jnp.where(mine, a0[...], b0[...]), jnp.where(mine, a1[...], b1[...]))
        g_o[...] = g
        d_o[...], m_o[...], v_o[...] = _adamw_math(w_r[...], g, m_r[...], v_r[...])

    stk = pl.BlockSpec((None, tr, C), lambda l, h, i: (l, h * nt + i, 0))
    lay0 = pl.BlockSpec((tr, C), lambda l, h, i: (jnp.where(l == 0, i, nt - 1), 0))
    lay1 = pl.BlockSpec((tr, C), lambda l, h, i: (jnp.where(l == 0, 0, i), 0))
    return pl.pallas_call(
        body, name=name, grid=(DEPTH, 2, nt),
        in_specs=[lay0, lay0, lay1, lay1, stk, stk, stk],
        out_specs=[stk] * 4, out_shape=[jax.ShapeDtypeStruct(w.shape, f32)] * 4,
        compiler_params=_cp(("arbitrary", "arbitrary", "arbitrary")),
    )(*halves[0], *halves[1], w, m, v)


def adamw_small(name, g, w, m, v):
    def body(g_r, w_r, m_r, v_r, d_o, m_o, v_o):
        d_o[...], m_o[...], v_o[...] = _adamw_math(w_r[...], g_r[...], m_r[...], v_r[...])

    return pl.pallas_call(body, name=name, out_shape=[jax.ShapeDtypeStruct(w.shape, f32)] * 3)(g, w, m, v)


WEIGHTS = ("w_in", "conv_w", "gmlp_ln_g", "gmlp_ln_b", "w_s", "b_s", "p_a", "p_b", "p_c", "w_o", "ln1_g", "ln1_b",
           "w_gate", "w_up", "w_down", "ln2_g", "ln2_b")
VECS = ("ln1_g", "ln1_b", "ln2_g", "ln2_b", "gmlp_ln_g", "gmlp_ln_b")
ROWS_VEC, ROWS_BS, ROWS_WS, ROWS_CONV = D // LANES, 8, 8 * BLK, 3 * D // LANES
ROWS_LAYER = len(VECS) * ROWS_VEC + ROWS_BS + ROWS_WS + ROWS_CONV


def _pack_small(per_layer, tail):
    parts = []
    for P in per_layer:
        parts += [P[n].reshape(ROWS_VEC, LANES) for n in VECS]
        parts += [P["b_s"].reshape(ROWS_BS, LANES), P["w_s"].reshape(ROWS_WS, LANES), P["conv_w"].reshape(ROWS_CONV, LANES)]
    return jnp.concatenate(parts + [tail], axis=0)


def _unpack_small(pack):
    out = []
    for l in range(DEPTH):
        r = l * ROWS_LAYER
        P = {}
        for n in VECS:
            P[n] = pack[r:r + ROWS_VEC].reshape(D)
            r += ROWS_VEC
        P["b_s"] = pack[r:r + ROWS_BS].reshape(8, BLK)
        r += ROWS_BS
        P["w_s"] = pack[r:r + ROWS_WS].reshape(8, BLK, BLK)
        r += ROWS_WS
        P["conv_w"] = pack[r:r + ROWS_CONV].reshape(3, D)
        out.append(P)
    return out, pack[DEPTH * ROWS_LAYER:]


def kernel(x, positions, w_in, conv_w, gmlp_ln_g, gmlp_ln_b, w_s, b_s, p_a, p_b, p_c, w_o, ln1_g, ln1_b, w_gate, w_up, w_down, ln2_g, ln2_b, loss_target, m_w_in, m_conv_w, m_gmlp_ln_g, m_gmlp_ln_b, m_w_s, m_b_s, m_p_a, m_p_b, m_p_c, m_w_o, m_ln1_g, m_ln1_b, m_w_gate, m_w_up, m_w_down, m_ln2_g, m_ln2_b, v_w_in, v_conv_w, v_gmlp_ln_g, v_gmlp_ln_b, v_w_s, v_b_s, v_p_a, v_p_b, v_p_c, v_w_o, v_ln1_g, v_ln1_b, v_w_gate, v_w_up, v_w_down, v_ln2_g, v_ln2_b):
    Wt = dict(w_in=w_in, conv_w=conv_w, gmlp_ln_g=gmlp_ln_g, gmlp_ln_b=gmlp_ln_b, w_s=w_s, b_s=b_s, p_a=p_a, p_b=p_b,
              p_c=p_c, w_o=w_o, ln1_g=ln1_g, ln1_b=ln1_b, w_gate=w_gate, w_up=w_up, w_down=w_down, ln2_g=ln2_g, ln2_b=ln2_b)
    Mt = dict(w_in=m_w_in, conv_w=m_conv_w, gmlp_ln_g=m_gmlp_ln_g, gmlp_ln_b=m_gmlp_ln_b, w_s=m_w_s, b_s=m_b_s, p_a=m_p_a,
              p_b=m_p_b, p_c=m_p_c, w_o=m_w_o, ln1_g=m_ln1_g, ln1_b=m_ln1_b, w_gate=m_w_gate, w_up=m_w_up,
              w_down=m_w_down, ln2_g=m_ln2_g, ln2_b=m_ln2_b)
    Vt = dict(w_in=v_w_in, conv_w=v_conv_w, gmlp_ln_g=v_gmlp_ln_g, gmlp_ln_b=v_gmlp_ln_b, w_s=v_w_s, b_s=v_b_s, p_a=v_p_a,
              p_b=v_p_b, p_c=v_p_c, w_o=v_w_o, ln1_g=v_ln1_g, ln1_b=v_ln1_b, w_gate=v_w_gate, w_up=v_w_up,
              w_down=v_w_down, ln2_g=v_ln2_g, ln2_b=v_ln2_b)
    chip = 2 * lax.axis_index("x") + lax.axis_index("y")
    cw = D // NCHIP

    def gathered_weights(names, arrays):
        Wl = dict(zip(names, arrays))
        for n in ("p_a", "p_c", "w_o"):
            Wl[n] = Wl[n].reshape(D, D)
        return Wl

    def small_weights(l, conv_all):
        Wl = {n: Wt[n][l] for n in VECS + ("w_s", "b_s")}
        Wl["conv_w"] = conv_all[:, l].transpose(1, 0, 2).reshape(3, D)
        return Wl

    w_in0, conv_all = gather_halves([Wt["w_in"][0].astype(MX).reshape(2, D // 2, WIN_SHARD), conv_w])
    rest = BIG[1:]
    *late0, coming0 = gather_start("0", [Wt[n][0].astype(MX) for n in rest], [conv_all])
    *late1, coming1 = gather_start("1", [Wt[n][1].astype(MX) for n in BIG], [conv_all, coming0])
    W0 = dict(small_weights(0, conv_all), w_in=w_in0.reshape(D, NIN), after=coming1[0, 0],
              late=lambda y: gathered_weights(rest, gather_wait("0", *late0, [y])))

    def W1(h):
        return dict(small_weights(1, conv_all), **gathered_weights(BIG, gather_wait("1", *late1, [h])))

    layers = [W0, W1]

    rs_state, rs_started, held = {}, {}, {}

    def start_exchange(l, g):
        if "loss" in g:
            held[l] = g
            return None
        if "conv_w" in g:
            held[l] = g
            rs_state[(l, False)], started = reduce_scatter_chips(rs_state[(l, False)], [g["w_s"], g["conv_w"]])
            if l == 0:
                pack = _pack_small([held[j] for j in range(DEPTH)], held[DEPTH]["loss"])
                *held["small"], token = small_start(pack)
                started = started + token[0, 0]
            return started
        if "dx" in g:
            rs_state[(l, True)], rs_started[(l, True)] = reduce_scatter_chips(rs_state[(l, True)], [g["dx"]])
            return rs_started[(l, True)]
        key = (l, "w_in" in g)
        rs_state[key], started = reduce_scatter_pair(f"{l}{'b' if key[1] else 'a'}", g)
        return started

    _, grad_x, _ = local_step(x[0], positions[0], loss_target[0], layers, start_exchange)

    last = jnp.zeros((8, LANES), f32) + rs_started[(0, True)]
    behind = [grad_x, last]
    red = [dict() for _ in range(DEPTH)]
    swaps = {key: reduce_scatter_finish(rs_state[key], behind) for key in ((1, False), (1, True), (0, False))}
    small, tail = _unpack_small(small_sum(*reversed(small_wait(*held["small"], behind))))
    loss = tail[0, 0]

    G, DW, NM, NV = {}, {}, {}, {}
    zc = jnp.zeros((3, D), f32)
    wp = _pack_small([{**{n: Wt[n][l] for n in VECS + ("b_s", "w_s")}, "conv_w": zc} for l in range(DEPTH)], jnp.zeros((8, LANES), f32))
    mp = _pack_small([{**{n: Mt[n][l] for n in VECS + ("b_s", "w_s")}, "conv_w": zc} for l in range(DEPTH)], jnp.zeros((8, LANES), f32))
    vp = _pack_small([{**{n: Vt[n][l] for n in VECS + ("b_s", "w_s")}, "conv_w": zc} for l in range(DEPTH)], jnp.ones((8, LANES), f32))
    gp = _pack_small(small, jnp.zeros((8, LANES), f32))
    outs = [_unpack_small(a)[0] for a in adamw_small("adamw_small", gp, wp, mp, vp)]
    for n in VECS + ("b_s", "w_s"):
        G[n] = jnp.stack([small[l][n] for l in range(DEPTH)])
        DW[n], NM[n], NV[n] = (jnp.stack([o[l][n] for l in range(DEPTH)]) for o in outs)
    gconv = jnp.stack([lax.dynamic_slice(small[l]["conv_w"], (0, chip * cw), (3, cw)) for l in range(DEPTH)])
    G["conv_w"] = gconv
    flat = lambda a: a.reshape(DEPTH * 3, cw)
    d, m2, v2 = adamw_small("adamw_conv", flat(gconv), flat(conv_w), flat(m_conv_w), flat(v_conv_w))
    DW["conv_w"], NM["conv_w"], NV["conv_w"] = (a.reshape(DEPTH, 3, cw) for a in (d, m2, v2))

    for key in swaps:
        red[key[0]].update(reduce_scatter_join(swaps[key], [d, DW["ln2_b"]]))
    updated = {}
    for n in BIG[1:]:
        tr = (lambda a: jnp.swapaxes(a, 1, 2)) if n in ("w_gate", "w_up") else (lambda a: a)
        updated[n] = adamw_big("adamw_" + n, (red[0][n], red[1][n]), tr(Wt[n]), tr(Mt[n]), tr(Vt[n]))
        G[n], DW[n], NM[n], NV[n] = map(tr, updated[n])
    done = [d, DW["ln2_b"], red[1]["w_in"][1]] + [updated[n][1] for n in BIG[1:]]
    red[0].update(reduce_scatter_join(reduce_scatter_finish(rs_state[(0, True)], done), [updated["w_o"][1]]))
    G["w_in"], DW["w_in"], NM["w_in"], NV["w_in"] = adamw_big(
        "adamw_w_in", (red[0]["w_in"], red[1]["w_in"]), Wt["w_in"], Mt["w_in"], Vt["w_in"])

    return (loss, grad_x[None], *[G[n] for n in WEIGHTS], *[DW[n] for n in WEIGHTS], *[NM[n] for n in WEIGHTS],
            *[NV[n] for n in WEIGHTS])
```

```python
import functools
import math

import jax
import jax.numpy as jnp
from jax import lax
from jax.experimental import pallas as pl
from jax.experimental.pallas import tpu as pltpu

D = 1024
NIN = 12800
DFF = 2816
NCHIP = 4
FB = DFF // NCHIP
WIN_SHARD = NIN // NCHIP
DEPTH = 2
GROUPS = ((128, 1), (512, 4), (2048, 16))
HD = 64
BLK = 128
AO = 512
ALPHA = (2 * DEPTH) ** 0.25
EPS = 1e-5
ROPE_THETA = 10000.0
LANES = 128
NEG = -1e30

C_GATES, C_BCH, C_QKV, C_UV = 0, 3 * D, 6 * D, 6 * D + 9 * AO

MX = jnp.bfloat16
ACT = jnp.bfloat16

ADAM_LR, ADAM_B1, ADAM_B2, ADAM_EPS, ADAM_WD, ADAM_STEP = 0.001, 0.9, 0.999, 1e-08, 0.01, 10

f32 = jnp.float32
NT = (((1,), (1,)), ((), ()))
TN = (((0,), (0,)), ((), ()))


def _cp(sem, vmem_mb=48):
    return pltpu.CompilerParams(dimension_semantics=sem, vmem_limit_bytes=vmem_mb << 20)


def _dot(a, b, dims=None):
    if dims is None:
        return jnp.dot(a, b, preferred_element_type=f32)
    return lax.dot_general(a, b, dims, preferred_element_type=f32)


def _ln_stats(r):
    mu = jnp.mean(r, axis=-1, keepdims=True)
    xc = r - mu
    var = jnp.mean(xc * xc, axis=-1, keepdims=True)
    rstd = lax.rsqrt(var + EPS)
    return xc * rstd, rstd


def _ln_bwd(dy, xhat, rstd, g):
    dxh = dy * g
    return rstd * (dxh - jnp.mean(dxh, axis=-1, keepdims=True) - xhat * jnp.mean(dxh * xhat, axis=-1, keepdims=True))


def _gelu(x):
    return 0.5 * x * (1.0 + lax.erf(x * (1.0 / math.sqrt(2.0))))


def _gelu_and_grad(x):
    cdf = 0.5 * (1.0 + lax.erf(x * (1.0 / math.sqrt(2.0))))
    return x * cdf, cdf + x * jnp.exp(-0.5 * x * x) * (1.0 / math.sqrt(2.0 * math.pi))


def _sigmoid(x):
    return 0.5 * jnp.tanh(0.5 * x) + 0.5


def _acc_rows(o_ref, first, val):
    @pl.when(first)
    def _():
        o_ref[...] = jnp.zeros_like(o_ref)
    o_ref[...] += jnp.sum(val, axis=0, keepdims=True)


def mm_in(x, w, bias):
    T = x.shape[0]
    tm, tn = min(2048, T), 1280

    def body(x_ref, w_ref, b_ref, o_ref, xb):
        @pl.when(pl.program_id(1) == 0)
        def _():
            xb[...] = x_ref[...].astype(MX)
        o_ref[...] = (_dot(xb[...], w_ref[...]) + b_ref[...]).astype(o_ref.dtype)

    return pl.pallas_call(
        body, name="mm_in", grid=(T // tm, NIN // tn),
        in_specs=[pl.BlockSpec((tm, D), lambda i, j: (i, 0), pipeline_mode=pl.Buffered(1)),
                  pl.BlockSpec((D, tn), lambda i, j: (0, j)), pl.BlockSpec((1, tn), lambda i, j: (0, j))],
        out_specs=pl.BlockSpec((tm, tn), lambda i, j: (i, j)),
        out_shape=jax.ShapeDtypeStruct((T, NIN), ACT),
        scratch_shapes=[pltpu.VMEM((tm, D), MX)],
        compiler_params=_cp(("parallel", "arbitrary")),
    )(x, w, bias)


HALO = 16
TM_AC = 512


def _uv_specs():
    return [pl.BlockSpec((TM_AC, 512), functools.partial(lambda i, j: (i, j), j=C_UV // 512 + j)) for j in range(4)]


def _gmlp_fwd(up, vp, ws_ref, bs_ref, lg, lb, u=None, gv=None):
    u = _gelu(up) if u is None else u
    xhat, rstd = _ln_stats(_gelu(vp) if gv is None else gv)
    vn = xhat * lg + lb
    vnb = vn.astype(MX)
    rows = []
    for c in range(up.shape[0] // BLK):
        r = slice(c * BLK, (c + 1) * BLK)
        rows.append(jnp.concatenate(
            [_dot(ws_ref[g], vnb[r, g * BLK:(g + 1) * BLK]) + bs_ref[g] for g in range(8)], axis=1))
    return u, vn, xhat, rstd, jnp.concatenate(rows, axis=0)


def mix_ac_fwd(proj, conv_w, wst, bsx, lg, lb):
    T = proj.shape[0]
    tm = TM_AC

    def body(bch, halo, u0, u1, v0, v1, cw, ws, bs, lg_ref, lb_ref, ya, yc, zs):
        i = pl.program_id(0)
        pb = bch[...].astype(f32)
        z = pb[:, D:2 * D] * pb[:, 2 * D:]
        hz = halo[:, :D].astype(f32) * halo[:, D:].astype(f32)
        zs[0:HALO, :] = jnp.where(i > 0, hz, 0.0)
        zs[HALO:HALO + tm, :] = z
        cv = cw[0:1, :] * zs[HALO - 2:HALO - 2 + tm, :] + cw[1:2, :] * zs[HALO - 1:HALO - 1 + tm, :] + cw[2:3, :] * z
        ya[...] = (pb[:, :D] * cv).astype(ya.dtype)
        up = jnp.concatenate([u0[...], u1[...]], axis=1).astype(f32)
        vp = jnp.concatenate([v0[...], v1[...]], axis=1).astype(f32)
        u, _, _, _, sp = _gmlp_fwd(up, vp, ws, bs, lg_ref[...], lb_ref[...])
        yc[...] = (u * sp).astype(yc.dtype)

    full = lambda shape: pl.BlockSpec(shape, lambda i: (0,) * len(shape))
    return pl.pallas_call(
        body, name="mix_ac_fwd", grid=(T // tm,),
        in_specs=[pl.BlockSpec((tm, 3 * D), lambda i: (i, 1)),
                  pl.BlockSpec((HALO, 2 * D), lambda i: (jnp.maximum(i * (tm // HALO) - 1, 0), 2)),
                  *_uv_specs(), full((3, D)), full((8, BLK, BLK)), full((8, BLK, BLK)), full((1, D)), full((1, D))],
        out_specs=[pl.BlockSpec((tm, D), lambda i: (i, 0))] * 2,
        out_shape=[jax.ShapeDtypeStruct((T, D), MX)] * 2,
        scratch_shapes=[pltpu.VMEM((HALO + tm, D), f32)],
        compiler_params=_cp(("parallel",)),
    )(proj, proj, proj, proj, proj, proj, conv_w, wst, bsx, lg, lb)


def _swap_halves(x):
    lane = lax.broadcasted_iota(jnp.int32, x.shape, 1)
    return jnp.where((lane % HD) < HD // 2, pltpu.roll(x, x.shape[1] - HD // 2, 1), pltpu.roll(x, HD // 2, 1))


def _tile4(t):
    return jnp.concatenate([t] * (AO // LANES), axis=1)


TM_FOLD = 1024


def _fold_out(nat, x, out_ref, d):
    if d == 1:
        out_ref[0] = x.astype(out_ref.dtype)
        return
    rows = x.shape[0] // d
    for j in range(AO // LANES):
        nat[j] = x[:, j * LANES:(j + 1) * LANES]
    for r in range(d):
        out_ref[r] = jnp.concatenate(
            [nat.at[j][pl.ds(r, rows, stride=d), :] for j in range(AO // LANES)], axis=1).astype(out_ref.dtype)


def _unfold_in(nat, in_ref, d):
    if d == 1:
        return in_ref[0].astype(f32)
    rows = in_ref.shape[1]
    for r in range(d):
        v = in_ref[r].astype(f32)
        for j in range(AO // LANES):
            nat.at[j][pl.ds(r, rows, stride=d), :] = v[:, j * LANES:(j + 1) * LANES]
    return jnp.concatenate([nat[j] for j in range(AO // LANES)], axis=1)


def fold_rope(proj, cos_t, sin_t, g, d):
    T = proj.shape[0]
    tm = TM_FOLD
    rows = tm // d

    def body(x_ref, c_ref, s_ref, q_o, k_o, v_o, nat):
        cos, sin = _tile4(c_ref[...]), _tile4(s_ref[...])
        for part, out, scale in ((0, q_o, HD ** -0.5), (1, k_o, 1.0), (2, v_o, None)):
            x = x_ref[:, part * AO:(part + 1) * AO].astype(f32)
            if scale is not None:
                x = (x * cos + _swap_halves(x) * sin) * scale
            _fold_out(nat, x, out, d)

    fold_spec = pl.BlockSpec((d, rows, AO), lambda i: (0, i, 0))
    return pl.pallas_call(
        body, name=f"fold_rope{g}", grid=(T // tm,),
        in_specs=[pl.BlockSpec((tm, 3 * AO), lambda i: (i, C_QKV // (3 * AO) + g)),
                  pl.BlockSpec((tm, LANES), lambda i: (i, 0)), pl.BlockSpec((tm, LANES), lambda i: (i, 0))],
        out_specs=[fold_spec] * 3,
        out_shape=[jax.ShapeDtypeStruct((d, T // d, AO), MX)] * 3,
        scratch_shapes=[pltpu.VMEM((AO // LANES, tm, LANES), f32)],
        compiler_params=_cp(("parallel",)),
    )(proj, cos_t, sin_t)


def _stack_heads(x):
    lane = lax.broadcasted_iota(jnp.int32, x.shape, 1)
    z = jnp.zeros_like(x)
    return jnp.concatenate([jnp.where(lane < HD, x, z), jnp.where(lane >= HD, x, z)], axis=0)


def _unstack_heads(y):
    lane = lax.broadcasted_iota(jnp.int32, (BLK, LANES), 1)
    return jnp.where(lane < HD, y[:BLK], y[BLK:])


def _window_masks():
    row = lax.broadcasted_iota(jnp.int32, (2 * BLK, 2 * BLK), 0) % BLK
    col = lax.broadcasted_iota(jnp.int32, (2 * BLK, 2 * BLK), 1)
    return (col < BLK) & (col >= row), (col >= BLK) & (col - BLK <= row)


def _two_blocks(ref, b):
    r0 = pl.multiple_of(b * BLK, BLK)
    rp = pl.multiple_of(jnp.maximum(b - 1, 0) * BLK, BLK)
    return jnp.concatenate([ref[pl.ds(rp, BLK), :], ref[pl.ds(r0, BLK), :]], axis=0)


def _merge_masks():
    row = lax.broadcasted_iota(jnp.int32, (2 * BLK, BLK), 0) % BLK
    col = lax.broadcasted_iota(jnp.int32, (2 * BLK, BLK), 1)
    return col <= row, col == row


def attn_fwd(qf, kf, vf, g, nb):
    T = qf.shape[0]

    def body(q_ref, k_ref, v_ref, o_ref, l_ref):
        cur_m, own_m = _merge_masks()

        def step(b, carry):
            r0 = pl.multiple_of(b * BLK, BLK)
            rp = pl.multiple_of(jnp.maximum(b - 1, 0) * BLK, BLK)
            qs = _stack_heads(q_ref[pl.ds(r0, BLK), :])
            vc, vp = v_ref[pl.ds(r0, BLK), :], v_ref[pl.ds(rp, BLK), :]
            sp = jnp.where((b % nb) != 0, _dot(qs, k_ref[pl.ds(rp, BLK), :], NT), NEG)
            s = jnp.where(cur_m, _dot(qs, k_ref[pl.ds(r0, BLK), :], NT), sp)
            s_own = jnp.sum(jnp.where(own_m, sp, 0.0), axis=-1, keepdims=True)
            m = jnp.maximum(jnp.max(s, axis=-1, keepdims=True), s_own)
            p, p_own = jnp.exp(s - m), jnp.exp(s_own - m)
            l = jnp.sum(p, axis=-1, keepdims=True) + p_own
            pb = p.astype(MX)
            zero = jnp.zeros_like(pb)
            o = _dot(jnp.where(cur_m, pb, zero), vc) + _dot(jnp.where(cur_m, zero, pb), vp)
            o = (o + p_own * jnp.concatenate([vp, vp], axis=0).astype(f32)) / l
            o_ref[pl.ds(r0, BLK), :] = _unstack_heads(o).astype(o_ref.dtype)
            l_ref[pl.ds(r0, BLK), :] = _unstack_heads(jnp.broadcast_to(m + jnp.log(l), (2 * BLK, LANES)))
            return carry

        lax.fori_loop(0, T // BLK, step, 0, unroll=8)

    spec = pl.BlockSpec((T, LANES), lambda j: (0, j))
    return pl.pallas_call(
        body, name=f"attn_fwd{g}", grid=(AO // LANES,),
        in_specs=[spec] * 3, out_specs=[spec] * 2,
        out_shape=[jax.ShapeDtypeStruct((T, AO), ACT), jax.ShapeDtypeStruct((T, AO), f32)],
        compiler_params=_cp(("parallel",), 56),
    )(qf, kf, vf)


def _group_weights(lses):
    m = jnp.maximum(jnp.maximum(lses[0], lses[1]), lses[2])
    e = [jnp.exp(l - m) for l in lses]
    inv = 1.0 / (e[0] + e[1] + e[2])
    return [x * inv for x in e]


def _fold_specs(T, tm):
    specs = []
    for _, d in GROUPS:
        specs.append(pl.BlockSpec((d, tm // d, AO), lambda i: (0, i, 0)))
    return specs


def combine_fwd(os_, lses):
    T = os_[0].shape[0] * os_[0].shape[1]
    tm = TM_FOLD

    def body(o0, o1, o2, l0, l1, l2, y_ref, nat):
        o = [_unfold_in(nat, r, d) for r, (_, d) in zip((o0, o1, o2), GROUPS)]
        ls = [_unfold_in(nat, r, d) for r, (_, d) in zip((l0, l1, l2), GROUPS)]
        w = _group_weights(ls)
        y_ref[...] = (w[0] * o[0] + w[1] * o[1] + w[2] * o[2]).astype(y_ref.dtype)

    specs = _fold_specs(T, tm)
    return pl.pallas_call(
        body, name="combine_fwd", grid=(T // tm,),
        in_specs=specs + specs, out_specs=pl.BlockSpec((tm, AO), lambda i: (i, 0)),
        out_shape=jax.ShapeDtypeStruct((T, AO), MX),
        scratch_shapes=[pltpu.VMEM((AO // LANES, tm, LANES), f32)],
        compiler_params=_cp(("parallel",)),
    )(*os_, *lses)


TM_MIX = 512


def mix_out_fwd(proj, ya, yb, yc, x0, pa, pb, pc, wo, g1, b1):
    T = x0.shape[0]
    tm = min(TM_MIX, T)

    def body(gt, ya_r, yb_r, yc_r, x0_r, pa_r, pb_r, pc_r, wo_r, g_r, b_r, mabc, m_o, r1_o, x1_o, x1b_o):
        ma = _dot(ya_r[...], pa_r[...])
        ybv = yb_r[...]
        mb = jnp.concatenate([_dot(ybv, pb_r[k]) for k in range(NCHIP)], axis=1)
        mc = _dot(yc_r[...], pc_r[...])
        m = jnp.zeros((tm, D), f32)
        for j, mm in enumerate((ma, mb, mc)):
            mabc[:, j * D:(j + 1) * D] = mm.astype(mabc.dtype)
            m = m + _sigmoid(gt[:, j * D:(j + 1) * D].astype(f32)) * mm
        mb16 = m.astype(MX)
        m_o[...] = mb16
        r1 = ALPHA * x0_r[...] + _dot(mb16, wo_r[...])
        r1_o[...] = r1
        xhat, _ = _ln_stats(r1)
        x1 = xhat * g_r[...] + b_r[...]
        x1_o[...] = x1
        x1b_o[...] = x1.astype(MX)

    full = lambda shape: pl.BlockSpec(shape, lambda i: (0,) * len(shape), pipeline_mode=pl.Buffered(1))
    tile = lambda w: pl.BlockSpec((tm, w), lambda i: (i, 0))
    return pl.pallas_call(
        body, name="mix_out_fwd", grid=(T // tm,),
        in_specs=[tile(3 * D), tile(D), tile(AO), tile(D), tile(D), full((D, D)), full((NCHIP, AO, D // NCHIP)),
                  full((D, D)), full((D, D)), full((1, D)), full((1, D))],
        out_specs=[tile(3 * D), tile(D), tile(D), tile(D), tile(D)],
        out_shape=[jax.ShapeDtypeStruct((T, 3 * D), MX), jax.ShapeDtypeStruct((T, D), MX),
                   jax.ShapeDtypeStruct((T, D), f32), jax.ShapeDtypeStruct((T, D), f32), jax.ShapeDtypeStruct((T, D), MX)],
        compiler_params=_cp(("parallel",), 56),
    )(proj, ya, yb, yc, x0, pa, pb, pc, wo, g1, b1)


TM_FF = 512
TM_FFB = 512
ROW_CHUNK = 64


TM_FFW = 256


def ffn_fwd(x1b, x1, wg, wu, wd, g2, b2):
    T = x1.shape[0]
    tm = min(TM_FFW, T)

    def body(xb_r, x_r, wg_r, wu_r, wd_r, g_r, b_r, g_o, u_o, h_o, r2_o, x2_o, gs, us):
        xb = xb_r[...]
        r2 = ALPHA * x_r[...]
        for k in range(NCHIP):
            gs[...] = _dot(xb, wg_r[k])
            us[...] = _dot(xb, wu_r[k])
            for r in range(0, tm, ROW_CHUNK):
                rows = pl.ds(r, ROW_CHUNK)
                gate, up = gs[rows, :], us[rows, :]
                g_o[k, rows, :] = gate.astype(g_o.dtype)
                u_o[k, rows, :] = up.astype(u_o.dtype)
                h_o[k, rows, :] = (gate * _sigmoid(gate) * up).astype(h_o.dtype)
            r2 = r2 + _dot(h_o[k], wd_r[k])
        r2_o[...] = r2
        xhat, _ = _ln_stats(r2)
        x2_o[...] = xhat * g_r[...] + b_r[...]

    once = dict(pipeline_mode=pl.Buffered(1))
    wspec = pl.BlockSpec((NCHIP, D, FB), lambda i: (0, 0, 0), **once)
    ospec = pl.BlockSpec((NCHIP, tm, FB), lambda i: (0, i, 0))
    tile = pl.BlockSpec((tm, D), lambda i: (i, 0))
    vec = pl.BlockSpec((1, D), lambda i: (0, 0))
    return pl.pallas_call(
        body, name="ffn_fwd", grid=(T // tm,),
        in_specs=[tile, tile, wspec, wspec, pl.BlockSpec((NCHIP, FB, D), lambda i: (0, 0, 0), **once), vec, vec],
        out_specs=[ospec] * 3 + [tile, tile],
        out_shape=[jax.ShapeDtypeStruct((NCHIP, T, FB), ACT)] * 2 + [jax.ShapeDtypeStruct((NCHIP, T, FB), MX)]
        + [jax.ShapeDtypeStruct((T, D), f32)] * 2,
        scratch_shapes=[pltpu.VMEM((tm, FB), f32)] * 2,
        compiler_params=_cp(("parallel",), 56),
    )(x1b, x1, wg, wu, wd, g2, b2)


def loss_grad(y, tgt):
    T = y.shape[0]
    tm = min(512, T)

    def body(y_r, t_r, l_o, dy_o):
        e = y_r[...] - t_r[...]
        dy_o[...] = e * (1.0 / D)

        @pl.when(pl.program_id(0) == 0)
        def _():
            l_o[...] = jnp.zeros_like(l_o)
        l_o[...] += (0.5 / D) * jnp.sum(e * e)

    tile = pl.BlockSpec((tm, D), lambda i: (i, 0))
    return pl.pallas_call(
        body, name="loss_grad", grid=(T // tm,),
        in_specs=[tile, tile], out_specs=[pl.BlockSpec((8, LANES), lambda i: (0, 0)), tile],
        out_shape=[jax.ShapeDtypeStruct((8, LANES), f32), jax.ShapeDtypeStruct((T, D), f32)],
        compiler_params=_cp(("arbitrary",)),
    )(y, tgt)


def ffn_down_bwd(dx2, r2, g2, wd, gate, up):
    T = dx2.shape[0]
    tm = min(TM_FFB, T)

    def body(dx_r, r_r, g_r, w_r, ga_r, up_r, dr_o, drb_o, dg_o, du_o, dlg_o, dlb_o, hs):
        i = pl.program_id(0)
        xhat, rstd = _ln_stats(r_r[...])
        dx = dx_r[...]
        _acc_rows(dlg_o, i == 0, dx * xhat)
        _acc_rows(dlb_o, i == 0, dx)
        dr = _ln_bwd(dx, xhat, rstd, g_r[...])
        dr_o[...] = dr
        drb = dr.astype(MX)
        drb_o[...] = drb
        for k in range(NCHIP):
            hs[...] = _dot(drb, w_r[k], NT)
            for r in range(0, tm, ROW_CHUNK):
                rows = pl.ds(r, ROW_CHUNK)
                dhh, gate_v, up_v = hs[rows, :], ga_r[k, rows, :].astype(f32), up_r[k, rows, :].astype(f32)
                sg = _sigmoid(gate_v)
                dg_o[k, rows, :] = (dhh * up_v * sg * (1.0 + gate_v * (1.0 - sg))).astype(dg_o.dtype)
                du_o[k, rows, :] = (dhh * gate_v * sg).astype(du_o.dtype)

    tile = pl.BlockSpec((tm, D), lambda i: (i, 0))
    vec = pl.BlockSpec((1, D), lambda i: (0, 0))
    blk = pl.BlockSpec((NCHIP, tm, FB), lambda i: (0, i, 0))
    return pl.pallas_call(
        body, name="ffn_down_bwd", grid=(T // tm,),
        in_specs=[tile, tile, vec, pl.BlockSpec((NCHIP, FB, D), lambda i: (0, 0, 0), pipeline_mode=pl.Buffered(1)), blk, blk],
        out_specs=[tile, tile, blk, blk, vec, vec],
        out_shape=[jax.ShapeDtypeStruct((T, D), f32), jax.ShapeDtypeStruct((T, D), MX)]
        + [jax.ShapeDtypeStruct((NCHIP, T, FB), MX)] * 2 + [jax.ShapeDtypeStruct((1, D), f32)] * 2,
        scratch_shapes=[pltpu.VMEM((tm, FB), f32)],
        compiler_params=_cp(("arbitrary",), 58),
    )(dx2, r2, g2, wd, gate, up)


def ffn_up_bwd(dr2, dgate, dup, wg, wu, r1, g1):
    T = dr2.shape[0]
    tm = min(TM_FFB, T)

    def body(dr2_r, dg_r, du_r, wg_r, wu_r, r1_r, g_r, dr1_o, dr1b_o, dlg_o, dlb_o):
        i = pl.program_id(0)
        dx = ALPHA * dr2_r[...]
        for k in range(NCHIP):
            dx = dx + _dot(dg_r[k], wg_r[k], NT) + _dot(du_r[k], wu_r[k], NT)
        xhat, rstd = _ln_stats(r1_r[...])
        _acc_rows(dlg_o, i == 0, dx * xhat)
        _acc_rows(dlb_o, i == 0, dx)
        dr1 = _ln_bwd(dx, xhat, rstd, g_r[...])
        dr1_o[...] = dr1
        dr1b_o[...] = dr1.astype(MX)

    tile = pl.BlockSpec((tm, D), lambda i: (i, 0))
    vec = pl.BlockSpec((1, D), lambda i: (0, 0))
    blk = pl.BlockSpec((NCHIP, tm, FB), lambda i: (0, i, 0))
    wspec = pl.BlockSpec((NCHIP, D, FB), lambda i: (0, 0, 0), pipeline_mode=pl.Buffered(1))
    return pl.pallas_call(
        body, name="ffn_up_bwd", grid=(T // tm,),
        in_specs=[tile, blk, blk, wspec, wspec, tile, vec],
        out_specs=[tile, tile, vec, vec],
        out_shape=[jax.ShapeDtypeStruct((T, D), f32), jax.ShapeDtypeStruct((T, D), MX)]
        + [jax.ShapeDtypeStruct((1, D), f32)] * 2,
        compiler_params=_cp(("arbitrary",), 58),
    )(dr2, dgate, dup, wg, wu, r1, g1)


TM_MIXB = 256


def mix_out_bwd(dr1, proj, mabc, wo, pa, pb, pc, ya, yb, yc, m):
    T = dr1.shape[0]
    tm = min(TM_MIXB, T)
    cb = D // NCHIP

    def body(dr_r, gt, mabc_r, wo_r, pa_r, pb_r, pc_r, ya_r, yb_r, yc_r, m_r,
             dgt_o, dya_o, dyb_o, dyc_o, dwo_o, dpa_o, dpb_o, dpc_o):
        @pl.when(pl.program_id(0) == 0)
        def _():
            for o in (dwo_o, dpa_o, dpb_o, dpc_o):
                o[...] = jnp.zeros_like(o)

        dr = dr_r[...].astype(MX)
        dm = _dot(dr, wo_r[...], NT)
        dmx = []
        for j in range(3):
            s = _sigmoid(gt[:, j * D:(j + 1) * D].astype(f32))
            dmx.append((dm * s).astype(MX))
            dgt_o[:, j * D:(j + 1) * D] = (dm * mabc_r[:, j * D:(j + 1) * D].astype(f32) * s * (1.0 - s)).astype(dgt_o.dtype)
        dya_o[...] = _dot(dmx[0], pa_r[...], NT).astype(dya_o.dtype)
        dyb = jnp.zeros((tm, AO), f32)
        for k in range(NCHIP):
            dyb = dyb + _dot(dmx[1][:, k * cb:(k + 1) * cb], pb_r[k], NT)
            dpb_o[k] += _dot(yb_r[...], dmx[1][:, k * cb:(k + 1) * cb], TN)
        dyb_o[...] = dyb.astype(dyb_o.dtype)
        dyc_o[...] = _dot(dmx[2], pc_r[...], NT).astype(dyc_o.dtype)
        dwo_o[...] += _dot(m_r[...], dr, TN)
        dpa_o[...] += _dot(ya_r[...], dmx[0], TN)
        dpc_o[...] += _dot(yc_r[...], dmx[2], TN)

    full = lambda shape: pl.BlockSpec(shape, lambda i: (0,) * len(shape), pipeline_mode=pl.Buffered(1))
    tile = lambda w: pl.BlockSpec((tm, w), lambda i: (i, 0))
    return pl.pallas_call(
        body, name="mix_out_bwd", grid=(T // tm,),
        in_specs=[tile(D), tile(3 * D), tile(3 * D), full((D, D)), full((D, D)), full((NCHIP, AO, cb)), full((D, D)),
                  tile(D), tile(AO), tile(D), tile(D)],
        out_specs=[tile(3 * D), tile(D), tile(AO), tile(D), full((D, D)), full((D, D)), full((NCHIP, AO, cb)), full((D, D))],
        out_shape=[jax.ShapeDtypeStruct((T, 3 * D), MX), jax.ShapeDtypeStruct((T, D), ACT),
                   jax.ShapeDtypeStruct((T, AO), ACT), jax.ShapeDtypeStruct((T, D), ACT), jax.ShapeDtypeStruct((D, D), f32),
                   jax.ShapeDtypeStruct((D, D), f32), jax.ShapeDtypeStruct((NCHIP, AO, cb), f32), jax.ShapeDtypeStruct((D, D), f32)],
        compiler_params=_cp(("arbitrary",), 58),
    )(dr1, proj, mabc, wo, pa, pb, pc, ya, yb, yc, m)


def transpose_cast(x):
    T = x.shape[0]
    tm = min(512, T)

    def body(x_r, o_r):
        o_r[...] = x_r[...].T.astype(o_r.dtype)

    return pl.pallas_call(
        body, name="transpose_cast", grid=(T // tm,),
        in_specs=[pl.BlockSpec((tm, D), lambda i: (i, 0))], out_specs=pl.BlockSpec((D, tm), lambda i: (0, i)),
        out_shape=jax.ShapeDtypeStruct((D, T), MX), compiler_params=_cp(("parallel",)),
    )(x)


def tn_matmul(name, a, b, a_spec, b_spec, out_shape, out_spec, grid):
    nt = len(grid) - 1

    def body(a_r, b_r, o_r):
        @pl.when(pl.program_id(nt) == 0)
        def _():
            o_r[...] = jnp.zeros_like(o_r)
        av = a_r[...].reshape(a_r.shape[-2:]).astype(MX)
        bv = b_r[...].reshape(b_r.shape[-2:]).astype(MX)
        o_r[...] += _dot(av, bv, TN).reshape(o_r.shape)

    return pl.pallas_call(
        body, name=name, grid=grid, in_specs=[a_spec, b_spec], out_specs=out_spec,
        out_shape=jax.ShapeDtypeStruct(out_shape, f32),
        compiler_params=_cp(("parallel",) * nt + ("arbitrary",), 56),
    )(a, b)


def attn_pre_bwd(dyb, os_, lses, ones):
    T = dyb.shape[0]
    tm = TM_FOLD

    def body(dy_r, o0, o1, o2, l0, l1, l2, ones_r, d0, d1, d2, f0, f1, f2, nat):
        o = [_unfold_in(nat, r, d) for r, (_, d) in zip((o0, o1, o2), GROUPS)]
        ls = [_unfold_in(nat, r, d) for r, (_, d) in zip((l0, l1, l2), GROUPS)]
        w = _group_weights(ls)
        dy = dy_r[...].astype(f32)
        t = dy * (w[0] * o[0] + w[1] * o[1] + w[2] * o[2])
        hi = t.astype(MX)
        lo = (t - hi.astype(f32)).astype(MX)
        c = _dot(hi, ones_r[...]) + _dot(lo, ones_r[...])
        for wg, do_o, df_o, (_, d) in zip(w, (d0, d1, d2), (f0, f1, f2), GROUPS):
            _fold_out(nat, wg * dy, do_o, d)
            _fold_out(nat, -wg * c, df_o, d)

    specs = _fold_specs(T, tm)
    return pl.pallas_call(
        body, name="attn_pre_bwd", grid=(T // tm,),
        in_specs=[pl.BlockSpec((tm, AO), lambda i: (i, 0))] + specs + specs + [pl.BlockSpec((AO, AO), lambda i: (0, 0))],
        out_specs=specs + specs,
        out_shape=[jax.ShapeDtypeStruct((d, T // d, AO), MX) for _, d in GROUPS]
        + [jax.ShapeDtypeStruct((d, T // d, AO), f32) for _, d in GROUPS],
        scratch_shapes=[pltpu.VMEM((AO // LANES, tm, LANES), f32)],
        compiler_params=_cp(("parallel",), 56),
    )(dyb, *os_, *lses, ones)


def _head_ones():
    i = jnp.arange(AO) // HD
    return (i[:, None] == i[None, :]).astype(MX)


BWD_BLOCKS = 8


def attn_bwd(qf, kf, vf, dof, lse, df, g, nb):
    T = qf.shape[0]

    def body(q_ref, k_ref, v_ref, do_ref, l_ref, d_ref, dq_ref, dk_ref, dv_ref):
        prev_m, cur_m = _window_masks()

        def head_col(ref, r0):
            v = ref[pl.ds(r0, BLK), :]
            return jnp.concatenate([v[:, 0:1], v[:, HD:HD + 1]], axis=0)

        def step(b, carry):
            dk_c, dv_c = carry
            r0 = pl.multiple_of(b * BLK, BLK)
            rp = pl.multiple_of(jnp.maximum(b - 1, 0) * BLK, BLK)
            qs, dos = _stack_heads(q_ref[pl.ds(r0, BLK), :]), _stack_heads(do_ref[pl.ds(r0, BLK), :])
            k2, v2 = _two_blocks(k_ref, b), _two_blocks(v_ref, b)
            valid = cur_m | (prev_m & ((b % nb) != 0))
            p = jnp.where(valid, jnp.exp(_dot(qs, k2, NT) - head_col(l_ref, r0)), 0.0)
            ds = (p * (_dot(dos, v2, NT) + head_col(d_ref, r0))).astype(MX)
            dq_ref[pl.ds(r0, BLK), :] = _unstack_heads(_dot(ds, k2)).astype(dq_ref.dtype)
            dk2 = _dot(ds, qs, TN)
            dv2 = _dot(p.astype(MX), dos, TN)
            dk_ref[pl.ds(rp, BLK), :] = (dk_c + dk2[:BLK]).astype(dk_ref.dtype)
            dv_ref[pl.ds(rp, BLK), :] = (dv_c + dv2[:BLK]).astype(dv_ref.dtype)
            return dk2[BLK:], dv2[BLK:]

        zero = jnp.zeros((BLK, LANES), f32)

        def steps(i, carry):
            for j in range(BWD_BLOCKS):
                carry = step(BWD_BLOCKS * i + j, carry)
            return carry

        dk_c, dv_c = lax.fori_loop(0, T // BLK // BWD_BLOCKS, steps, (zero, zero))
        dk_ref[pl.ds(T - BLK, BLK), :] = dk_c.astype(dk_ref.dtype)
        dv_ref[pl.ds(T - BLK, BLK), :] = dv_c.astype(dv_ref.dtype)

    spec = pl.BlockSpec((T, LANES), lambda j: (0, j))
    return pl.pallas_call(
        body, name=f"attn_bwd{g}", grid=(AO // LANES,),
        in_specs=[spec] * 6, out_specs=[spec] * 3,
        out_shape=[jax.ShapeDtypeStruct((T, AO), MX)] * 3,
        compiler_params=_cp(("parallel",), 60),
    )(qf, kf, vf, dof, lse, df)


def unfold_rope_bwd(dqf, dkf, dvf, cos_t, sin_t, g, d):
    T = dqf.shape[0] * dqf.shape[1]
    tm = TM_FOLD

    def body(q_r, k_r, v_r, c_ref, s_ref, o_ref, nat):
        cos, sin = _tile4(c_ref[...]), _tile4(s_ref[...])
        for part, ref, scale in ((0, q_r, HD ** -0.5), (1, k_r, 1.0), (2, v_r, None)):
            x = _unfold_in(nat, ref, d)
            if scale is not None:
                x = (x * cos - _swap_halves(x) * sin) * scale
            o_ref[:, part * AO:(part + 1) * AO] = x.astype(o_ref.dtype)

    fold_spec = pl.BlockSpec((d, tm // d, AO), lambda i: (0, i, 0))
    tab = pl.BlockSpec((tm, LANES), lambda i: (i, 0))
    return pl.pallas_call(
        body, name=f"unfold_rope_bwd{g}", grid=(T // tm,),
        in_specs=[fold_spec] * 3 + [tab, tab],
        out_specs=pl.BlockSpec((tm, 3 * AO), lambda i: (i, 0)),
        out_shape=jax.ShapeDtypeStruct((T, 3 * AO), MX),
        scratch_shapes=[pltpu.VMEM((AO // LANES, tm, LANES), f32)],
        compiler_params=_cp(("parallel",)),
    )(dqf, dkf, dvf, cos_t, sin_t)


CONV_CHUNK = 32


def conv_bwd(dya, proj, conv_w):
    T = dya.shape[0]
    tm = TM_AC
    last = T // tm - 1

    def body(dy_r, bch, hprev, dy_next, b_next, cw, d_o, dw_o, zs, ds):
        i = pl.program_id(0)
        ch = CONV_CHUNK
        hz = hprev[:, :D].astype(f32) * hprev[:, D:].astype(f32)
        zs[0:HALO, :] = jnp.where(i > 0, hz, 0.0)
        ds[tm:tm + HALO, :] = jnp.where(i < last, dy_next[...].astype(f32) * b_next[...].astype(f32), 0.0)
        for r in range(0, tm, ch):
            zs[HALO + r:HALO + r + ch, :] = bch[r:r + ch, D:2 * D].astype(f32) * bch[r:r + ch, 2 * D:].astype(f32)
            ds[r:r + ch, :] = dy_r[r:r + ch, :].astype(f32) * bch[r:r + ch, :D].astype(f32)

        @pl.when(i == 0)
        def _():
            dw_o[...] = jnp.zeros_like(dw_o)

        sums = [jnp.zeros((1, D), f32) for _ in range(3)]
        for r in range(0, tm, ch):
            z2, z1, z = (zs[HALO + r - s:HALO + r - s + ch, :] for s in (2, 1, 0))
            dcv, d1, d2 = (ds[r + s:r + s + ch, :] for s in (0, 1, 2))
            cv = cw[0:1, :] * z2 + cw[1:2, :] * z1 + cw[2:3, :] * z
            dz = cw[2:3, :] * dcv + cw[1:2, :] * d1 + cw[0:1, :] * d2
            d_o[r:r + ch, :D] = (dy_r[r:r + ch, :].astype(f32) * cv).astype(d_o.dtype)
            d_o[r:r + ch, D:2 * D] = (dz * bch[r:r + ch, 2 * D:].astype(f32)).astype(d_o.dtype)
            d_o[r:r + ch, 2 * D:] = (dz * bch[r:r + ch, D:2 * D].astype(f32)).astype(d_o.dtype)
            for k, zz in enumerate((z2, z1, z)):
                sums[k] = sums[k] + jnp.sum(dcv * zz, axis=0, keepdims=True)
        for k in range(3):
            dw_o[k:k + 1, :] += sums[k]

    nh = tm // HALO
    return pl.pallas_call(
        body, name="conv_bwd", grid=(T // tm,),
        in_specs=[pl.BlockSpec((tm, D), lambda i: (i, 0)), pl.BlockSpec((tm, 3 * D), lambda i: (i, 1)),
                  pl.BlockSpec((HALO, 2 * D), lambda i: (jnp.maximum(i * nh - 1, 0), 2)),
                  pl.BlockSpec((HALO, D), lambda i: (jnp.minimum((i + 1) * nh, T // HALO - 1), 0)),
                  pl.BlockSpec((HALO, D), lambda i: (jnp.minimum((i + 1) * nh, T // HALO - 1), 3)),
                  pl.BlockSpec((3, D), lambda i: (0, 0))],
        out_specs=[pl.BlockSpec((tm, 3 * D), lambda i: (i, 0)), pl.BlockSpec((3, D), lambda i: (0, 0))],
        out_shape=[jax.ShapeDtypeStruct((T, 3 * D), MX), jax.ShapeDtypeStruct((3, D), f32)],
        scratch_shapes=[pltpu.VMEM((HALO + tm, D), f32), pltpu.VMEM((tm + HALO, D), f32)],
        compiler_params=_cp(("arbitrary",)),
    )(dya, proj, proj, dya, proj, conv_w)


def gmlp_bwd(dyc, proj, wst, bsx, lg, lb):
    T = dyc.shape[0]
    tm = TM_AC
    last = T // tm - 1

    def body(dy_r, u0, u1, v0, v1, ws, bs, lg_r, lb_r, d_o, dws_o, dbs_o, dlg_o, dlb_o, bacc):
        i = pl.program_id(0)
        up = jnp.concatenate([u0[...], u1[...]], axis=1).astype(f32)
        vp = jnp.concatenate([v0[...], v1[...]], axis=1).astype(f32)
        u, du = _gelu_and_grad(up)
        gv, dgv = _gelu_and_grad(vp)
        u, vn, xhat, rstd, sp = _gmlp_fwd(up, vp, ws, bs, lg_r[...], lb_r[...], u, gv)
        dy = dy_r[...].astype(f32)
        d_o[:, :D] = (dy * sp * du).astype(d_o.dtype)
        dsp = dy * u
        dspb, vnb = dsp.astype(MX), vn.astype(MX)

        @pl.when(i == 0)
        def _():
            dws_o[...] = jnp.zeros_like(dws_o)
            bacc[...] = jnp.zeros_like(bacc)

        rows = []
        for c in range(tm // BLK):
            r = slice(c * BLK, (c + 1) * BLK)
            cols = []
            for g in range(8):
                cs = slice(g * BLK, (g + 1) * BLK)
                dws_o[g] += _dot(dspb[r, cs], vnb[r, cs], NT)
                bacc[g] += dsp[r, cs]
                cols.append(_dot(ws[g], dspb[r, cs], TN))
            rows.append(jnp.concatenate(cols, axis=1))
        dvn = jnp.concatenate(rows, axis=0)
        _acc_rows(dlg_o, i == 0, dvn * xhat)
        _acc_rows(dlb_o, i == 0, dvn)
        d_o[:, D:] = (_ln_bwd(dvn, xhat, rstd, lg_r[...]) * dgv).astype(d_o.dtype)

        @pl.when(i == last)
        def _():
            row = lax.broadcasted_iota(jnp.int32, (BLK, BLK), 0)
            col = lax.broadcasted_iota(jnp.int32, (BLK, BLK), 1)
            ones = jnp.ones((8, BLK), MX)
            for g in range(8):
                dws_o[g] = jnp.where(col <= row, dws_o[g], 0.0)
                a = bacc[g]
                hi = a.astype(MX)
                lo = (a - hi.astype(f32)).astype(MX)
                dbs_o[g:g + 1, :] = (_dot(ones, hi, NT) + _dot(ones, lo, NT))[0:1, :]

    full = lambda shape: pl.BlockSpec(shape, lambda i: (0,) * len(shape))
    return pl.pallas_call(
        body, name="gmlp_bwd", grid=(T // tm,),
        in_specs=[pl.BlockSpec((tm, D), lambda i: (i, 0)), *_uv_specs(), full((8, BLK, BLK)), full((8, BLK, BLK)),
                  full((1, D)), full((1, D))],
        out_specs=[pl.BlockSpec((tm, 2 * D), lambda i: (i, 0)), full((8, BLK, BLK)), full((8, BLK)), full((1, D)), full((1, D))],
        out_shape=[jax.ShapeDtypeStruct((T, 2 * D), MX), jax.ShapeDtypeStruct((8, BLK, BLK), f32),
                   jax.ShapeDtypeStruct((8, BLK), f32), jax.ShapeDtypeStruct((1, D), f32), jax.ShapeDtypeStruct((1, D), f32)],
        scratch_shapes=[pltpu.VMEM((8, BLK, BLK), f32)],
        compiler_params=_cp(("arbitrary",)),
    )(dyc, proj, proj, proj, proj, wst, bsx, lg, lb)


PART_TILES = (6, 6, 3, 3, 3, 4)
PART_START = (0, 6, 12, 15, 18, 21)
TJ = 512


def _part_specs(tm, rows_axis):
    specs = []
    for n, s in zip(PART_TILES, PART_START):
        def imap(*idx, n=n, s=s):
            i, j = idx[rows_axis], idx[1 - rows_axis]
            inside = (j >= s) & (j < s + n)
            return (jnp.where(inside, i, 0), jnp.clip(j - s, 0, n - 1))
        specs.append(pl.BlockSpec((tm, TJ), imap))
    return specs


def _with_part(j, refs, fn):
    for r, n, s in zip(refs, PART_TILES, PART_START):
        @pl.when((j >= s) & (j < s + n))
        def _():
            fn(r[...])


def dx_in(dr1, parts, w, bias):
    T = dr1.shape[0]
    tm = min(2048, T)

    def body(dr_r, p0, p1, p2, p3, p4, p5, w_r, b_r, o_r):
        j = pl.program_id(1)

        @pl.when(j == 0)
        def _():
            o_r[...] = ALPHA * dr_r[...] + b_r[...]

        def acc(tile):
            o_r[...] += _dot(tile, w_r[...], NT)
        _with_part(j, (p0, p1, p2, p3, p4, p5), acc)

    once = dict(pipeline_mode=pl.Buffered(1))
    return pl.pallas_call(
        body, name="dx_in", grid=(T // tm, NIN // TJ),
        in_specs=[pl.BlockSpec((tm, D), lambda i, j: (i, 0), **once)] + _part_specs(tm, 0)
        + [pl.BlockSpec((D, TJ), lambda i, j: (0, j)), pl.BlockSpec((1, D), lambda i, j: (0, 0))],
        out_specs=pl.BlockSpec((tm, D), lambda i, j: (i, 0), **once),
        out_shape=jax.ShapeDtypeStruct((T, D), f32),
        compiler_params=_cp(("parallel", "arbitrary"), 56),
    )(dr1, *parts, w, bias)


def dw_in(x0t, parts):
    T = x0t.shape[1]
    tk = min(2048, T)

    def body(x_r, p0, p1, p2, p3, p4, p5, o_r):
        j, t = pl.program_id(0), pl.program_id(1)

        @pl.when(t == 0)
        def _():
            o_r[...] = jnp.zeros_like(o_r)

        def acc(tile):
            o_r[...] += _dot(x_r[:, pl.ds(pl.multiple_of(t * tk, tk), tk)], tile)
        _with_part(j, (p0, p1, p2, p3, p4, p5), acc)

    return pl.pallas_call(
        body, name="dw_in", grid=(NIN // TJ, T // tk),
        in_specs=[pl.BlockSpec((D, T), lambda j, t: (0, 0), pipeline_mode=pl.Buffered(1))] + _part_specs(tk, 1),
        out_specs=pl.BlockSpec((D, TJ), lambda j, t: (0, j)),
        out_shape=jax.ShapeDtypeStruct((D, NIN), f32),
        compiler_params=_cp(("parallel", "arbitrary"), 56),
    )(x0t, *parts)


def rope_tables(positions):
    half = HD // 2
    inv_freq = ROPE_THETA ** (-jnp.arange(half, dtype=f32) / half)
    ang = positions.astype(f32)[:, None] * inv_freq
    cos, sin = jnp.cos(ang), jnp.sin(ang)
    return jnp.tile(cos, (1, LANES // half)), jnp.tile(jnp.concatenate([-sin, sin], axis=1), (1, LANES // HD))


def _flat(a):
    return a.reshape(a.shape[0] * a.shape[1], a.shape[2])


def layer_fwd(x0, W, cos_t, sin_t):
    T = x0.shape[0]
    proj = mm_in(x0, W["w_in"], W["in_bias"])
    ya, yc = mix_ac_fwd(proj, W["conv_w"], W["wst"], W["bsx"], W["gmlp_ln_g"], W["gmlp_ln_b"])
    folded, os_, lses = [], [], []
    for g, (_, d) in enumerate(GROUPS):
        qf, kf, vf = fold_rope(proj, cos_t, sin_t, g, d)
        o, lse = attn_fwd(_flat(qf), _flat(kf), _flat(vf), g, T // d // BLK)
        folded.append((qf, kf, vf))
        os_.append(o.reshape(d, T // d, AO))
        lses.append(lse.reshape(d, T // d, AO))
    yb = combine_fwd(os_, lses)
    if "late" in W:
        W = {**W, **W["late"](yb)}
    mabc, m, r1, x1, x1b = mix_out_fwd(proj, ya, yb, yc, x0, W["p_a"], W["p_b"], W["p_c"], W["w_o"], W["ln1_g"], W["ln1_b"])
    gate, up, hh, r2, x2 = ffn_fwd(x1b, x1, W["w_gate"], W["w_up"], W["w_down"], W["ln2_g"], W["ln2_b"])
    saved = dict(x0=x0, proj=proj, ya=ya, yb=yb, yc=yc, folded=folded, os=os_, lses=lses, mabc=mabc, m=m, r1=r1,
                 x1b=x1b, gate=gate, up=up, hh=hh, r2=r2)
    return x2, saved, W


def layer_bwd(dx2, S, W, cos_t, sin_t, on_grads=None):
    T = dx2.shape[0]
    tk = min(4096, T)
    G = {}
    dr2, dr2b, dgate, dup, G["ln2_g"], G["ln2_b"] = ffn_down_bwd(dx2, S["r2"], W["ln2_g"], W["w_down"], S["gate"], S["up"])
    blk_a = pl.BlockSpec((1, tk, FB), lambda k, t: (k, t, 0))
    row_b = pl.BlockSpec((tk, D), lambda k, t: (t, 0))
    G["w_down"] = tn_matmul("dw_down", S["hh"], dr2b, blk_a, row_b, (NCHIP, FB, D),
                            pl.BlockSpec((1, FB, D), lambda k, t: (k, 0, 0)), (NCHIP, T // tk))
    for nm, dv in (("w_gate", dgate), ("w_up", dup)):
        G[nm] = tn_matmul("d" + nm, dv, S["x1b"], blk_a, row_b, (NCHIP, FB, D),
                          pl.BlockSpec((1, FB, D), lambda k, t: (k, 0, 0)), (NCHIP, T // tk))
    dr1, dr1b, G["ln1_g"], G["ln1_b"] = ffn_up_bwd(dr2, dgate, dup, W["w_gate"], W["w_up"], S["r1"], W["ln1_g"])
    dgates, dya, dyb, dyc, G["w_o"], G["p_a"], G["p_b"], G["p_c"] = mix_out_bwd(
        dr1b, S["proj"], S["mabc"], W["w_o"], W["p_a"], W["p_b"], W["p_c"], S["ya"], S["yb"], S["yc"], S["m"])
    conv_w = W["conv_w"]
    if on_grads is not None:
        conv_w = conv_w + on_grads({n: G[n] for n in BIG if n != "w_in"})
    dbch, G["conv_w"] = conv_bwd(dya, S["proj"], conv_w)
    duv, G["w_s"], G["b_s"], G["gmlp_ln_g"], G["gmlp_ln_b"] = gmlp_bwd(
        dyc, S["proj"], W["wst"], W["bsx"], W["gmlp_ln_g"], W["gmlp_ln_b"])
    ones = _head_ones()
    if on_grads is not None:
        small = {n: G[n] for n in VECS + ("b_s", "w_s", "conv_w")}
        ones = ones + on_grads(small).astype(MX)
    pre = attn_pre_bwd(dyb, S["os"], S["lses"], ones)
    dqkv = []
    for g, (_, d) in enumerate(GROUPS):
        qf, kf, vf = S["folded"][g]
        dqf, dkf, dvf = attn_bwd(_flat(qf), _flat(kf), _flat(vf), _flat(pre[g]), _flat(S["lses"][g]), _flat(pre[3 + g]),
                                 g, T // d // BLK)
        shp = (d, T // d, AO)
        dqkv.append(unfold_rope_bwd(dqf.reshape(shp), dkf.reshape(shp), dvf.reshape(shp), cos_t, sin_t, g, d))
    parts = (dgates, dbch, *dqkv, duv)
    G["w_in"] = dw_in(transpose_cast(S["x0"]), parts)
    bias = jnp.zeros((1, D), f32)
    if on_grads is not None:
        bias = bias + on_grads({"w_in": G["w_in"]})
    dx0 = dx_in(dr1, parts, W["w_in"], bias)
    started = on_grads({"dx": dx0}) if on_grads is not None else None
    return dx0, G, started


def prep_layer_weights(Wl):
    W = dict(Wl)
    tril = jnp.tril(jnp.ones((BLK, BLK), f32))
    W["wst"] = (Wl["w_s"] * tril[None]).astype(MX)
    W["bsx"] = jnp.broadcast_to(Wl["b_s"][:, :, None], (8, BLK, BLK))
    for n in ("gmlp_ln_g", "gmlp_ln_b", "ln1_g", "ln1_b", "ln2_g", "ln2_b"):
        W[n] = Wl[n].reshape(1, D)
    W["in_bias"] = jnp.zeros((1, NIN), f32) + Wl.get("after", 0.0)
    return W


def local_step(x, positions, target, layers, on_grads=None):
    cos_t, sin_t = rope_tables(positions)
    Ws, saved = [], []
    h = x
    for Wl in layers:
        h, S, W = layer_fwd(h, prep_layer_weights(Wl(h) if callable(Wl) else Wl), cos_t, sin_t)
        Ws.append(W)
        saved.append(S)
    lsum, dh = loss_grad(h, target)
    if on_grads is not None:
        on_grads(len(Ws), {"loss": lsum})
    grads = [None] * len(Ws)
    started = None
    for l in reversed(range(len(Ws))):
        W = Ws[l]
        if started is not None:
            W = dict(W, ln2_g=W["ln2_g"] + started)
        hook = functools.partial(on_grads, l) if on_grads is not None else None
        dh, grads[l], started = layer_bwd(dh, saved[l], W, cos_t, sin_t, hook)
    return lsum, dh, grads


MESH = pl.DeviceIdType.MESH
ANY = pl.BlockSpec(memory_space=pl.ANY)
BIG = ("w_in", "w_gate", "w_up", "w_down", "p_a", "p_b", "p_c", "w_o")
NBIG = len(BIG)


def _place():
    x, y, c = lax.axis_index("x"), lax.axis_index("y"), lax.axis_index("c")
    return x, y, c, 2 * x + y


def _rcopy(src, dst, send, recv, dev):
    return pltpu.make_async_remote_copy(src_ref=src, dst_ref=dst, send_sem=send, recv_sem=recv, device_id=dev,
                                        device_id_type=MESH)


def _cols(ref, k, width):
    start = k * width if isinstance(k, int) else pl.multiple_of(k * width, LANES)
    return ref.at[:, pl.ds(start, width)]


CHUNK_BYTES = 1 << 20


def _pieces(shape, itemsize, nbytes=CHUNK_BYTES):
    rows, cols = shape[-2], shape[-1]
    per = max(16, nbytes // (cols * itemsize) // 16 * 16)
    out = []
    for lead in (range(shape[0]) if len(shape) == 3 else (None,)):
        for r in range(0, rows, per):
            sl = (pl.ds(r, min(per, rows - r)), slice(None))
            out.append(sl if lead is None else (lead,) + sl)
    return out


def _start_pieces(src, dst, make, nbytes=CHUNK_BYTES):
    for idx in _pieces(src.shape, jnp.dtype(src.dtype).itemsize, nbytes):
        make(src.at[idx], dst.at[idx]).start()


def gather_halves(shards):
    n = len(shards)

    def body(*refs):
        srcs, dsts = refs[:n], refs[n:2 * n]
        send, recv, own_send, own_recv = refs[2 * n:]
        x, y, c, k = _place()
        sib = (x, y, 1 - c)
        chips = [(1 - x, y), (x, 1 - y), (1 - x, 1 - y)]

        def slot(a, layer, pos):
            if a == 0:
                return _cols(dsts[0].at[layer], pos, WIN_SHARD)
            return dsts[a].at[pos, layer]

        def ici(a, j, src, dst):
            return _rcopy(src, dst, send.at[a, j], recv.at[a, j], (*chips[j], c))

        def d2d(a, j, src, dst):
            return _rcopy(src, dst, send.at[a, 3 + j], recv.at[a, 3 + j], sib)

        def own(a, layer, src, dst):
            return _rcopy(src, dst, own_send.at[a, layer], own_recv.at[a, layer], sib)

        for a in range(n):
            for j in range(3):
                _start_pieces(srcs[a].at[c], slot(a, c, k), functools.partial(ici, a, j))
        for a in range(n):
            for layer in range(DEPTH):
                _start_pieces(srcs[a].at[layer], slot(a, layer, k), functools.partial(own, a, layer))
        for a in range(n):
            for j, (cx, cy) in enumerate(chips):
                landed = slot(a, c, 2 * cx + cy)
                ici(a, j, landed, landed).wait_recv()
                _start_pieces(landed, landed, functools.partial(d2d, a, j))
        for a in range(n):
            for j, (cx, cy) in enumerate(chips):
                passed = slot(a, 1 - c, 2 * cx + cy)
                d2d(a, j, passed, passed).wait_recv()
                landed = slot(a, c, 2 * cx + cy)
                d2d(a, j, landed, landed).wait_send()
                ici(a, j, srcs[a].at[c], slot(a, c, k)).wait_send()
            for layer in range(DEPTH):
                own(a, layer, srcs[a].at[layer], slot(a, layer, k)).wait()

    outs = [jax.ShapeDtypeStruct((2, shards[0].shape[1], NIN), shards[0].dtype)]
    outs += [jax.ShapeDtypeStruct((NCHIP,) + s.shape, s.dtype) for s in shards[1:]]
    return pl.pallas_call(
        body, name="gather_halves", in_specs=[ANY] * n, out_specs=[ANY] * n, out_shape=outs,
        scratch_shapes=[pltpu.SemaphoreType.DMA((n, 6)), pltpu.SemaphoreType.DMA((n, 6)),
                        pltpu.SemaphoreType.DMA((n, DEPTH)), pltpu.SemaphoreType.DMA((n, DEPTH))],
    )(*shards)


def _gather_slot(dst, pos):
    return _cols(dst, pos, WIN_SHARD) if len(dst.shape) == 2 else dst.at[pos]


def _gather_copy(a, j, src, dst, send, recv, dev):
    return _rcopy(src, dst, send.at[a * NCHIP + j], recv.at[a * NCHIP + j], dev)


def gather_start(tag, shards, after):
    n = len(shards)

    def body(*refs):
        srcs, dsts = refs[:n], refs[n:2 * n]
        send, recv = refs[2 * n + len(after)], refs[2 * n + len(after) + 1]
        token = refs[-1]
        x, y, c, k = _place()
        peers = [(1 - x, y, c), (x, 1 - y, c), (1 - x, 1 - y, c), (x, y, 1 - c)]
        for a in range(n):
            for j, dev in enumerate(peers):
                _start_pieces(srcs[a], _gather_slot(dsts[a], k),
                              lambda s, d, a=a, j=j, dev=dev: _gather_copy(a, j, s, d, send, recv, dev))
        token[...] = jnp.zeros_like(token)

    gathered = [lax.empty((D, NIN) if s.shape == (D, WIN_SHARD) else (NCHIP,) + s.shape, s.dtype) for s in shards]
    ops = [pltpu.with_memory_space_constraint(v, pltpu.HBM) for v in list(shards) + gathered]
    sem = pltpu.SemaphoreType.DMA((n * NCHIP,))
    res = pl.pallas_call(
        body, name=f"gather_start{tag}", in_specs=[HBM] * (2 * n) + [ANY] * len(after),
        out_specs=[SEMS, SEMS] + [HBM] * (2 * n) + [pl.BlockSpec(memory_space=pltpu.VMEM)],
        out_shape=[sem, sem] + [pltpu.HBM(v.shape, v.dtype) for v in ops] + [jax.ShapeDtypeStruct((8, LANES), f32)],
        input_output_aliases={i: 2 + i for i in range(2 * n)},
        compiler_params=pltpu.CompilerParams(has_side_effects=EFFECT),
    )(*ops, *after)
    return res[0], res[1], res[2:2 + n], res[2 + n:2 + 2 * n], res[-1]


def gather_wait(tag, send, recv, shards, gathered, after):
    n = len(shards)

    def body(*refs):
        srcs, dsts = refs[:n], refs[n:2 * n]
        send_r, recv_r = refs[2 * n], refs[2 * n + 1]
        x, y, c, k = _place()
        peers = [(1 - x, y, c), (x, 1 - y, c), (1 - x, 1 - y, c), (x, y, 1 - c)]
        for a in range(n):
            for j, dev in enumerate(peers):
                _gather_copy(a, j, srcs[a], _gather_slot(dsts[a], k), send_r, recv_r, dev).wait_send()
                pos = 2 * dev[0] + dev[1]
                _gather_copy(a, j, srcs[a], _gather_slot(dsts[a], pos), send_r, recv_r, dev).wait_recv()

    ops = list(shards) + list(gathered)
    res = pl.pallas_call(
        body, name=f"gather_wait{tag}", in_specs=[HBM] * (2 * n) + [SEMS, SEMS] + [ANY] * len(after),
        out_specs=[HBM] * (2 * n), out_shape=[pltpu.HBM(v.shape, v.dtype) for v in ops],
        input_output_aliases={i: i for i in range(2 * n)},
        compiler_params=pltpu.CompilerParams(has_side_effects=EFFECT),
    )(*ops, send, recv, *after)
    return res[n:]


def _half(ref, h):
    rows = ref.shape[-2] // 2
    start = pl.multiple_of(h * rows, 16)
    if len(ref.shape) == 2:
        return ref.at[pl.ds(start, rows), :]
    return ref.at[:, pl.ds(start, rows), :]


HBM = pl.BlockSpec(memory_space=pltpu.HBM)
SEMS = pl.BlockSpec(memory_space=pltpu.SEMAPHORE)
EFFECT = pltpu.SideEffectType.DATAFLOW_SIDE_EFFECTING


def rs_pair_start(tag, grads, halves=True):
    n = len(grads)

    def body(*refs):
        g, theirs = refs[:n], refs[n:2 * n]
        send, recv = refs[2 * n], refs[2 * n + 1]
        x, y, c, _ = _place()
        for a in range(n):
            _start_pieces(_half(g[a], 1 - c) if halves else g[a], theirs[a],
                          lambda s, d, a=a: _rcopy(s, d, send.at[a], recv.at[a], (x, y, 1 - c)))
        refs[-1][...] = jnp.zeros_like(refs[-1])

    lands = [lax.empty(g.shape[:-2] + (g.shape[-2] // 2 if halves else g.shape[-2], g.shape[-1]), g.dtype) for g in grads]
    ops = [pltpu.with_memory_space_constraint(v, pltpu.HBM) for v in list(grads) + lands]
    sem = pltpu.SemaphoreType.DMA((n,))
    res = pl.pallas_call(
        body, name=f"rs_pair_start{tag}", in_specs=[HBM] * (2 * n),
        out_specs=[SEMS, SEMS] + [HBM] * (2 * n) + [pl.BlockSpec(memory_space=pltpu.VMEM)],
        out_shape=[sem, sem] + [pltpu.HBM(v.shape, v.dtype) for v in ops] + [jax.ShapeDtypeStruct((8, LANES), f32)],
        input_output_aliases={i: 2 + i for i in range(2 * n)},
        compiler_params=pltpu.CompilerParams(has_side_effects=EFFECT),
    )(*ops)
    return res[0], res[1], res[2:2 + n], res[2 + n:2 + 2 * n], res[-1]


def rs_pair_wait(tag, send, recv, grads, theirs, after, halves=True):
    n = len(grads)

    def body(*refs):
        g, land = refs[:n], refs[n:2 * n]
        send_r, recv_r = refs[2 * n], refs[2 * n + 1]
        x, y, c, _ = _place()
        for a in range(n):
            cp = _rcopy(_half(g[a], 1 - c) if halves else g[a], land[a], send_r.at[a], recv_r.at[a], (x, y, 1 - c))
            cp.wait_send()
            cp.wait_recv()

    ops = list(grads) + list(theirs)
    res = pl.pallas_call(
        body, name=f"rs_pair_wait{tag}", in_specs=[HBM] * (2 * n) + [SEMS, SEMS] + [ANY] * len(after),
        out_specs=[HBM] * (2 * n), out_shape=[pltpu.HBM(v.shape, v.dtype) for v in ops],
        input_output_aliases={i: i for i in range(2 * n)},
        compiler_params=pltpu.CompilerParams(has_side_effects=EFFECT),
    )(*ops, send, recv, *after)
    return res[:n], res[n:]


def _chip_piece(ref, k):
    return _cols(ref, k, WIN_SHARD) if len(ref.shape) == 2 else ref.at[k]


def _chip_copy(a, k, src, dst, send, recv, me, c):
    return _rcopy(src, dst, send.at[a * NCHIP + k], recv.at[a * NCHIP + me], (k // 2, k % 2, c))


def rs_chips_start(tag, sums):
    n = len(sums)

    def pshape(s):
        return (NCHIP, s[0], WIN_SHARD) if len(s) == 2 else s

    def body(*refs):
        s, land = refs[:n], refs[n:2 * n]
        send, recv = refs[2 * n], refs[2 * n + 1]
        token = refs[-1]
        x, y, c, me = _place()
        for k in range(NCHIP):
            @pl.when(me != k)
            def _():
                for a in range(n):
                    _start_pieces(_chip_piece(s[a], k), land[a].at[me],
                                  lambda src, dst, a=a: _chip_copy(a, k, src, dst, send, recv, me, c))
        token[...] = jnp.zeros_like(token)

    lands = [lax.empty(pshape(v.shape), v.dtype) for v in sums]
    ops = [pltpu.with_memory_space_constraint(v, pltpu.HBM) for v in list(sums) + lands]
    sem = pltpu.SemaphoreType.DMA((n * NCHIP,))
    res = pl.pallas_call(
        body, name=f"rs_chips_start{tag}", in_specs=[HBM] * (2 * n),
        out_specs=[SEMS, SEMS] + [HBM] * (2 * n) + [pl.BlockSpec(memory_space=pltpu.VMEM)],
        out_shape=[sem, sem] + [pltpu.HBM(v.shape, v.dtype) for v in ops] + [jax.ShapeDtypeStruct((8, LANES), f32)],
        input_output_aliases={i: 2 + i for i in range(2 * n)},
        compiler_params=pltpu.CompilerParams(has_side_effects=EFFECT),
    )(*ops)
    return res[0], res[1], res[2:2 + n], res[2 + n:2 + 2 * n], res[-1]


def rs_chips_wait(tag, send, recv, sums, lands, after):
    n = len(sums)

    def body(*refs):
        s, land = refs[:n], refs[n:2 * n]
        send_r, recv_r = refs[2 * n], refs[2 * n + 1]
        x, y, c, me = _place()
        for k in range(NCHIP):
            @pl.when(me != k)
            def _():
                for a in range(n):
                    piece = _chip_piece(s[a], k)
                    _chip_copy(a, k, piece, land[a].at[me], send_r, recv_r, me, c).wait_send()
                    _rcopy(piece, land[a].at[k], send_r.at[a * NCHIP + k], recv_r.at[a * NCHIP + k],
                           (k // 2, k % 2, c)).wait_recv()

    ops = list(sums) + list(lands)
    res = pl.pallas_call(
        body, name=f"rs_chips_wait{tag}", in_specs=[HBM] * (2 * n) + [SEMS, SEMS] + [ANY] * len(after),
        out_specs=[HBM] * (2 * n), out_shape=[pltpu.HBM(v.shape, v.dtype) for v in ops],
        input_output_aliases={i: i for i in range(2 * n)},
        compiler_params=pltpu.CompilerParams(has_side_effects=EFFECT),
    )(*ops, send, recv, *after)
    return res[:n], res[n:]


def _row_tile(rows, cols, itemsize=4, target=2 << 20):
    best = 8
    for t in range(8, rows + 1, 8):
        if rows % t == 0 and t * cols * itemsize <= target:
            best = t
    return best


GRAD_WIRE = jnp.bfloat16


def add_half(name, g, t, c):
    cols, half = t.shape[-1], t.shape[-2]
    nblk = 1 if t.ndim == 2 else t.shape[0]
    tr = _row_tile(half, cols)
    per = half // tr

    def body(c_ref, g_r, t_r, o_r):
        o_r[...] = (g_r[...] + t_r[...]).astype(o_r.dtype)

    tile_t = pl.BlockSpec((tr, cols), lambda i, c_ref: (i, 0))
    tile_g = pl.BlockSpec((tr, cols), lambda i, c_ref: ((i // per) * 2 * per + c_ref[0] * per + i % per, 0))
    out = pl.pallas_call(
        body, name=name, out_shape=jax.ShapeDtypeStruct((nblk * half, cols), GRAD_WIRE),
        grid_spec=pltpu.PrefetchScalarGridSpec(num_scalar_prefetch=1, grid=(nblk * per,), in_specs=[tile_g, tile_t],
                                               out_specs=tile_t),
        compiler_params=_cp(("parallel",)),
    )(c.reshape(1).astype(jnp.int32), g.reshape(nblk * 2 * half, cols), t.reshape(nblk * half, cols))
    return out.reshape(t.shape)


def add_chips(name, land, own):
    _, rows, cols = land.shape
    tr = _row_tile(rows, cols, target=1 << 20)

    def body(land_r, own_r, o_r):
        me = 2 * lax.axis_index("x") + lax.axis_index("y")
        for k in range(NCHIP):
            @pl.when(me == k)
            def _():
                acc = None
                for j in range(NCHIP):
                    t = (own_r[...] if j == k else land_r[j]).astype(f32)
                    acc = t if acc is None else acc + t
                o_r[...] = acc

    tile = pl.BlockSpec((tr, cols), lambda i: (i, 0))
    return pl.pallas_call(
        body, name=name, grid=(rows // tr,), in_specs=[pl.BlockSpec((NCHIP, tr, cols), lambda i: (0, i, 0)), tile],
        out_specs=tile, out_shape=jax.ShapeDtypeStruct((rows, cols), f32), compiler_params=_cp(("parallel",)),
    )(land, own)


def reduce_scatter_pair(tag, G):
    names = tuple(G)
    grads = [G[n] if G[n].ndim == 3 or n == "w_in" else G[n].reshape(NCHIP, D // NCHIP, D) for n in names]
    send, recv, grads, theirs, token = rs_pair_start(tag, grads)
    return (tag, names, send, recv, grads, theirs), token[0, 0]


def reduce_scatter_chips(state, after):
    c = lax.axis_index("c")
    tag, names, send, recv, grads, theirs = state
    grads, theirs = rs_pair_wait(tag, send, recv, grads, theirs, after)
    sums = [add_half(f"rs_add_pair{tag}_{n}", g, t, c) for n, g, t in zip(names, grads, theirs)]
    send, recv, sums, lands, token = rs_chips_start(tag, sums)
    return (tag, names, send, recv, sums, lands), token[0, 0]


def reduce_scatter_finish(state, after):
    me = 2 * lax.axis_index("x") + lax.axis_index("y")
    tag, names, send, recv, sums, lands = state
    sums, landed = rs_chips_wait(tag, send, recv, sums, lands, after)
    halves = []
    for n, s, v in zip(names, sums, landed):
        own = lax.dynamic_slice_in_dim(s, me * WIN_SHARD, WIN_SHARD, axis=1) if s.ndim == 2 else \
            lax.dynamic_index_in_dim(s, me, 0, keepdims=False)
        halves.append(add_chips(f"rs_add_chips{tag}_{n}", v, own))
    send, recv, halves, others, _ = rs_pair_start("_join" + tag, halves, halves=False)
    return tag, names, send, recv, halves, others


def reduce_scatter_join(state, after):
    tag, names, send, recv, halves, others = state
    halves, others = rs_pair_wait("_join" + tag, send, recv, halves, others, after, halves=False)
    return dict(zip(names, zip(halves, others)))


NDEV = 8


def _small_copy(r, src, dst, send, recv, x, y, c):
    return _rcopy(src, dst, send.at[r - 1], recv.at[r - 1], (x ^ (r >> 2), y ^ ((r >> 1) & 1), c ^ (r & 1)))


def small_start(pack):
    def body(p, land, send, recv, p_thru, land_thru, token):
        x, y, c, _ = _place()
        me = 4 * x + 2 * y + c
        for r in range(1, NDEV):
            _start_pieces(p, land.at[me], lambda s, d, r=r: _small_copy(r, s, d, send, recv, x, y, c), 128 << 10)
        token[...] = jnp.zeros_like(token)

    ops = [pltpu.with_memory_space_constraint(v, pltpu.HBM) for v in (pack, lax.empty((NDEV,) + pack.shape, f32))]
    sem = pltpu.SemaphoreType.DMA((NDEV - 1,))
    return pl.pallas_call(
        body, name="small_start", in_specs=[HBM, HBM],
        out_specs=[SEMS, SEMS, HBM, HBM, pl.BlockSpec(memory_space=pltpu.VMEM)],
        out_shape=[sem, sem] + [pltpu.HBM(v.shape, v.dtype) for v in ops] + [jax.ShapeDtypeStruct((8, LANES), f32)],
        input_output_aliases={0: 2, 1: 3}, compiler_params=pltpu.CompilerParams(has_side_effects=EFFECT),
    )(*ops)


def small_wait(send, recv, pack, land, after):
    def body(p, land_r, send_r, recv_r, *rest):
        x, y, c, _ = _place()
        me = 4 * x + 2 * y + c
        for r in range(1, NDEV):
            _small_copy(r, p, land_r.at[me], send_r, recv_r, x, y, c).wait_send()
            src = 4 * (x ^ (r >> 2)) + 2 * (y ^ ((r >> 1) & 1)) + (c ^ (r & 1))
            _small_copy(r, p, land_r.at[src], send_r, recv_r, x, y, c).wait_recv()

    return pl.pallas_call(
        body, name="small_wait", in_specs=[HBM, HBM, SEMS, SEMS] + [ANY] * len(after), out_specs=[HBM, HBM],
        out_shape=[pltpu.HBM(pack.shape, f32), pltpu.HBM(land.shape, f32)], input_output_aliases={0: 0, 1: 1},
        compiler_params=pltpu.CompilerParams(has_side_effects=EFFECT),
    )(pack, land, send, recv, *after)


def small_sum(land, pack):
    def body(land_r, p_r, o_r):
        me = 4 * lax.axis_index("x") + 2 * lax.axis_index("y") + lax.axis_index("c")
        for k in range(NDEV):
            @pl.when(me == k)
            def _():
                acc = None
                for d in range(NDEV):
                    t = p_r[...] if d == k else land_r[d]
                    acc = t if acc is None else acc + t
                o_r[...] = acc

    vm = pl.BlockSpec(memory_space=pltpu.VMEM)
    return pl.pallas_call(
        body, name="small_sum", in_specs=[vm, vm], out_specs=vm, out_shape=jax.ShapeDtypeStruct(pack.shape, f32),
        compiler_params=pltpu.CompilerParams(vmem_limit_bytes=40 << 20),
    )(land, pack)


def _adamw_math(w, g, m, v):
    m = ADAM_B1 * m + (1.0 - ADAM_B1) * g
    v = ADAM_B2 * v + (1.0 - ADAM_B2) * (g * g)
    m_hat = m / (1.0 - ADAM_B1 ** ADAM_STEP)
    v_hat = v / (1.0 - ADAM_B2 ** ADAM_STEP)
    return -ADAM_LR * (m_hat / (jnp.sqrt(v_hat) + ADAM_EPS) + ADAM_WD * w), m, v


def adamw_big(name, halves, w, m, v):
    _, R, C = w.shape
    tr = _row_tile(R // 2, C, target=1 << 20)
    nt = R // 2 // tr

    def body(a0, b0, a1, b1, w_r, m_r, v_r, g_o, d_o, m_o, v_o):
        mine = pl.program_id(1) == lax.axis_index("c")
        g = jnp.where(pl.program_id(0) == 0, jnp.where(mine, a0[...], b0[...]), jnp.where(mine, a1[...], b1[...]))
        g_o[...] = g
        d_o[...], m_o[...], v_o[...] = _adamw_math(w_r[...], g, m_r[...], v_r[...])

    stk = pl.BlockSpec((None, tr, C), lambda l, h, i: (l, h * nt + i, 0))
    lay0 = pl.BlockSpec((tr, C), lambda l, h, i: (jnp.where(l == 0, i, nt - 1), 0))
    lay1 = pl.BlockSpec((tr, C), lambda l, h, i: (jnp.where(l == 0, 0, i), 0))
    return pl.pallas_call(
        body, name=name, grid=(DEPTH, 2, nt),
        in_specs=[lay0, lay0, lay1, lay1, stk, stk, stk],
        out_specs=[stk] * 4, out_shape=[jax.ShapeDtypeStruct(w.shape, f32)] * 4,
        compiler_params=_cp(("arbitrary", "arbitrary", "arbitrary")),
    )(*halves[0], *halves[1], w, m, v)


def adamw_small(name, g, w, m, v):
    def body(g_r, w_r, m_r, v_r, d_o, m_o, v_o):
        d_o[...], m_o[...], v_o[...] = _adamw_math(w_r[...], g_r[...], m_r[...], v_r[...])

    return pl.pallas_call(body, name=name, out_shape=[jax.ShapeDtypeStruct(w.shape, f32)] * 3)(g, w, m, v)


WEIGHTS = ("w_in", "conv_w", "gmlp_ln_g", "gmlp_ln_b", "w_s", "b_s", "p_a", "p_b", "p_c", "w_o", "ln1_g", "ln1_b",
           "w_gate", "w_up", "w_down", "ln2_g", "ln2_b")
VECS = ("ln1_g", "ln1_b", "ln2_g", "ln2_b", "gmlp_ln_g", "gmlp_ln_b")
ROWS_VEC, ROWS_BS, ROWS_WS, ROWS_CONV = D // LANES, 8, 8 * BLK, 3 * D // LANES
ROWS_LAYER = len(VECS) * ROWS_VEC + ROWS_BS + ROWS_WS + ROWS_CONV


def _pack_small(per_layer, tail):
    parts = []
    for P in per_layer:
        parts += [P[n].reshape(ROWS_VEC, LANES) for n in VECS]
        parts += [P["b_s"].reshape(ROWS_BS, LANES), P["w_s"].reshape(ROWS_WS, LANES), P["conv_w"].reshape(ROWS_CONV, LANES)]
    return jnp.concatenate(parts + [tail], axis=0)


def _unpack_small(pack):
    out = []
    for l in range(DEPTH):
        r = l * ROWS_LAYER
        P = {}
        for n in VECS:
            P[n] = pack[r:r + ROWS_VEC].reshape(D)
            r += ROWS_VEC
        P["b_s"] = pack[r:r + ROWS_BS].reshape(8, BLK)
        r += ROWS_BS
        P["w_s"] = pack[r:r + ROWS_WS].reshape(8, BLK, BLK)
        r += ROWS_WS
        P["conv_w"] = pack[r:r + ROWS_CONV].reshape(3, D)
        out.append(P)
    return out, pack[DEPTH * ROWS_LAYER:]


def kernel(x, positions, w_in, conv_w, gmlp_ln_g, gmlp_ln_b, w_s, b_s, p_a, p_b, p_c, w_o, ln1_g, ln1_b, w_gate, w_up, w_down, ln2_g, ln2_b, loss_target, m_w_in, m_conv_w, m_gmlp_ln_g, m_gmlp_ln_b, m_w_s, m_b_s, m_p_a, m_p_b, m_p_c, m_w_o, m_ln1_g, m_ln1_b, m_w_gate, m_w_up, m_w_down, m_ln2_g, m_ln2_b, v_w_in, v_conv_w, v_gmlp_ln_g, v_gmlp_ln_b, v_w_s, v_b_s, v_p_a, v_p_b, v_p_c, v_w_o, v_ln1_g, v_ln1_b, v_w_gate, v_w_up, v_w_down, v_ln2_g, v_ln2_b):
    Wt = dict(w_in=w_in, conv_w=conv_w, gmlp_ln_g=gmlp_ln_g, gmlp_ln_b=gmlp_ln_b, w_s=w_s, b_s=b_s, p_a=p_a, p_b=p_b,
              p_c=p_c, w_o=w_o, ln1_g=ln1_g, ln1_b=ln1_b, w_gate=w_gate, w_up=w_up, w_down=w_down, ln2_g=ln2_g, ln2_b=ln2_b)
    Mt = dict(w_in=m_w_in, conv_w=m_conv_w, gmlp_ln_g=m_gmlp_ln_g, gmlp_ln_b=m_gmlp_ln_b, w_s=m_w_s, b_s=m_b_s, p_a=m_p_a,
              p_b=m_p_b, p_c=m_p_c, w_o=m_w_o, ln1_g=m_ln1_g, ln1_b=m_ln1_b, w_gate=m_w_gate, w_up=m_w_up,
              w_down=m_w_down, ln2_g=m_ln2_g, ln2_b=m_ln2_b)
    Vt = dict(w_in=v_w_in, conv_w=v_conv_w, gmlp_ln_g=v_gmlp_ln_g, gmlp_ln_b=v_gmlp_ln_b, w_s=v_w_s, b_s=v_b_s, p_a=v_p_a,
              p_b=v_p_b, p_c=v_p_c, w_o=v_w_o, ln1_g=v_ln1_g, ln1_b=v_ln1_b, w_gate=v_w_gate, w_up=v_w_up,
              w_down=v_w_down, ln2_g=v_ln2_g, ln2_b=v_ln2_b)
    chip = 2 * lax.axis_index("x") + lax.axis_index("y")
    cw = D // NCHIP

    def gathered_weights(names, arrays):
        Wl = dict(zip(names, arrays))
        for n in ("p_a", "p_c", "w_o"):
            Wl[n] = Wl[n].reshape(D, D)
        return Wl

    def small_weights(l, conv_all):
        Wl = {n: Wt[n][l] for n in VECS + ("w_s", "b_s")}
        Wl["conv_w"] = conv_all[:, l].transpose(1, 0, 2).reshape(3, D)
        return Wl

    w_in0, conv_all = gather_halves([Wt["w_in"][0].astype(MX).reshape(2, D // 2, WIN_SHARD), conv_w])
    rest = BIG[1:]
    *late0, coming0 = gather_start("0", [Wt[n][0].astype(MX) for n in rest], [conv_all])
    *late1, coming1 = gather_start("1", [Wt[n][1].astype(MX) for n in BIG], [conv_all, coming0])
    W0 = dict(small_weights(0, conv_all), w_in=w_in0.reshape(D, NIN), after=coming1[0, 0],
              late=lambda y: gathered_weights(rest, gather_wait("0", *late0, [y])))

    def W1(h):
        return dict(small_weights(1, conv_all), **gathered_weights(BIG, gather_wait("1", *late1, [h])))

    layers = [W0, W1]

    rs_state, rs_started, held = {}, {}, {}

    def start_exchange(l, g):
        if "loss" in g:
            held[l] = g
            return None
        if "conv_w" in g:
            held[l] = g
            rs_state[(l, False)], started = reduce_scatter_chips(rs_state[(l, False)], [g["w_s"], g["conv_w"]])
            if l == 0:
                pack = _pack_small([held[j] for j in range(DEPTH)], held[DEPTH]["loss"])
                *held["small"], token = small_start(pack)
                started = started + token[0, 0]
            return started
        if "dx" in g:
            rs_state[(l, True)], rs_started[(l, True)] = reduce_scatter_chips(rs_state[(l, True)], [g["dx"]])
            return rs_started[(l, True)]
        key = (l, "w_in" in g)
        rs_state[key], started = reduce_scatter_pair(f"{l}{'b' if key[1] else 'a'}", g)
        return started

    _, grad_x, _ = local_step(x[0], positions[0], loss_target[0], layers, start_exchange)

    last = jnp.zeros((8, LANES), f32) + rs_started[(0, True)]
    behind = [grad_x, last]
    red = [dict() for _ in range(DEPTH)]
    swaps = {key: reduce_scatter_finish(rs_state[key], behind) for key in ((1, False), (1, True), (0, False))}
    small, tail = _unpack_small(small_sum(*reversed(small_wait(*held["small"], behind))))
    loss = tail[0, 0]

    G, DW, NM, NV = {}, {}, {}, {}
    zc = jnp.zeros((3, D), f32)
    wp = _pack_small([{**{n: Wt[n][l] for n in VECS + ("b_s", "w_s")}, "conv_w": zc} for l in range(DEPTH)], jnp.zeros((8, LANES), f32))
    mp = _pack_small([{**{n: Mt[n][l] for n in VECS + ("b_s", "w_s")}, "conv_w": zc} for l in range(DEPTH)], jnp.zeros((8, LANES), f32))
    vp = _pack_small([{**{n: Vt[n][l] for n in VECS + ("b_s", "w_s")}, "conv_w": zc} for l in range(DEPTH)], jnp.ones((8, LANES), f32))
    gp = _pack_small(small, jnp.zeros((8, LANES), f32))
    outs = [_unpack_small(a)[0] for a in adamw_small("adamw_small", gp, wp, mp, vp)]
    for n in VECS + ("b_s", "w_s"):
        G[n] = jnp.stack([small[l][n] for l in range(DEPTH)])
        DW[n], NM[n], NV[n] = (jnp.stack([o[l][n] for l in range(DEPTH)]) for o in outs)
    gconv = jnp.stack([lax.dynamic_slice(small[l]["conv_w"], (0, chip * cw), (3, cw)) for l in range(DEPTH)])
    G["conv_w"] = gconv
    flat = lambda a: a.reshape(DEPTH * 3, cw)
    d, m2, v2 = adamw_small("adamw_conv", flat(gconv), flat(conv_w), flat(m_conv_w), flat(v_conv_w))
    DW["conv_w"], NM["conv_w"], NV["conv_w"] = (a.reshape(DEPTH, 3, cw) for a in (d, m2, v2))

    for key in swaps:
        red[key[0]].update(reduce_scatter_join(swaps[key], [d, DW["ln2_b"]]))
    updated = {}
    for n in BIG[1:]:
        tr = (lambda a: jnp.swapaxes(a, 1, 2)) if n in ("w_gate", "w_up") else (lambda a: a)
        updated[n] = adamw_big("adamw_" + n, (red[0][n], red[1][n]), tr(Wt[n]), tr(Mt[n]), tr(Vt[n]))
        G[n], DW[n], NM[n], NV[n] = map(tr, updated[n])
    done = [d, DW["ln2_b"], red[1]["w_in"][1]] + [updated[n][1] for n in BIG[1:]]
    red[0].update(reduce_scatter_join(reduce_scatter_finish(rs_state[(0, True)], done), [updated["w_o"][1]]))
    G["w_in"], DW["w_in"], NM["w_in"], NV["w_in"] = adamw_big(
        "adamw_w_in", (red[0]["w_in"], red[1]["w_in"]), Wt["w_in"], Mt["w_in"], Vt["w_in"])

    return (loss, grad_x[None], *[G[n] for n in WEIGHTS], *[DW[n] for n in WEIGHTS], *[NM[n] for n in WEIGHTS],
            *[NV[n] for n in WEIGHTS])
```

```python
import functools
import math

import jax
import jax.numpy as jnp
from jax import lax
from jax.experimental import pallas as pl
from jax.experimental.pallas import tpu as pltpu

D = 1024
NIN = 12800
DFF = 2816
NCHIP = 4
FB = DFF // NCHIP
WIN_SHARD = NIN // NCHIP
DEPTH = 2
GROUPS = ((128, 1), (512, 4), (2048, 16))
HD = 64
BLK = 128
AO = 512
ALPHA = (2 * DEPTH) ** 0.25
EPS = 1e-5
ROPE_THETA = 10000.0
LANES = 128
NEG = -1e30

C_GATES, C_BCH, C_QKV, C_UV = 0, 3 * D, 6 * D, 6 * D + 9 * AO

MX = jnp.bfloat16
ACT = jnp.bfloat16

ADAM_LR, ADAM_B1, ADAM_B2, ADAM_EPS, ADAM_WD, ADAM_STEP = 0.001, 0.9, 0.999, 1e-08, 0.01, 10

f32 = jnp.float32
NT = (((1,), (1,)), ((), ()))
TN = (((0,), (0,)), ((), ()))


def _cp(sem, vmem_mb=48):
    return pltpu.CompilerParams(dimension_semantics=sem, vmem_limit_bytes=vmem_mb << 20)


def _dot(a, b, dims=None):
    if dims is None:
        return jnp.dot(a, b, preferred_element_type=f32)
    return lax.dot_general(a, b, dims, preferred_element_type=f32)


def _ln_stats(r):
    mu = jnp.mean(r, axis=-1, keepdims=True)
    xc = r - mu
    var = jnp.mean(xc * xc, axis=-1, keepdims=True)
    rstd = lax.rsqrt(var + EPS)
    return xc * rstd, rstd


def _ln_bwd(dy, xhat, rstd, g):
    dxh = dy * g
    return rstd * (dxh - jnp.mean(dxh, axis=-1, keepdims=True) - xhat * jnp.mean(dxh * xhat, axis=-1, keepdims=True))


def _gelu(x):
    return 0.5 * x * (1.0 + lax.erf(x * (1.0 / math.sqrt(2.0))))


def _gelu_and_grad(x):
    cdf = 0.5 * (1.0 + lax.erf(x * (1.0 / math.sqrt(2.0))))
    return x * cdf, cdf + x * jnp.exp(-0.5 * x * x) * (1.0 / math.sqrt(2.0 * math.pi))


def _sigmoid(x):
    return 0.5 * jnp.tanh(0.5 * x) + 0.5


def _acc_rows(o_ref, first, val):
    @pl.when(first)
    def _():
        o_ref[...] = jnp.zeros_like(o_ref)
    o_ref[...] += jnp.sum(val, axis=0, keepdims=True)


def mm_in(x, w, bias):
    T = x.shape[0]
    tm, tn = min(2048, T), 1280

    def body(x_ref, w_ref, b_ref, o_ref, xb):
        @pl.when(pl.program_id(1) == 0)
        def _():
            xb[...] = x_ref[...].astype(MX)
        o_ref[...] = (_dot(xb[...], w_ref[...]) + b_ref[...]).astype(o_ref.dtype)

    return pl.pallas_call(
        body, name="mm_in", grid=(T // tm, NIN // tn),
        in_specs=[pl.BlockSpec((tm, D), lambda i, j: (i, 0), pipeline_mode=pl.Buffered(1)),
                  pl.BlockSpec((D, tn), lambda i, j: (0, j)), pl.BlockSpec((1, tn), lambda i, j: (0, j))],
        out_specs=pl.BlockSpec((tm, tn), lambda i, j: (i, j)),
        out_shape=jax.ShapeDtypeStruct((T, NIN), ACT),
        scratch_shapes=[pltpu.VMEM((tm, D), MX)],
        compiler_params=_cp(("parallel", "arbitrary")),
    )(x, w, bias)


HALO = 16
TM_AC = 512


def _uv_specs():
    return [pl.BlockSpec((TM_AC, 512), functools.partial(lambda i, j: (i, j), j=C_UV // 512 + j)) for j in range(4)]


def _gmlp_fwd(up, vp, ws_ref, bs_ref, lg, lb, u=None, gv=None):
    u = _gelu(up) if u is None else u
    xhat, rstd = _ln_stats(_gelu(vp) if gv is None else gv)
    vn = xhat * lg + lb
    vnb = vn.astype(MX)
    rows = []
    for c in range(up.shape[0] // BLK):
        r = slice(c * BLK, (c + 1) * BLK)
        rows.append(jnp.concatenate(
            [_dot(ws_ref[g], vnb[r, g * BLK:(g + 1) * BLK]) + bs_ref[g] for g in range(8)], axis=1))
    return u, vn, xhat, rstd, jnp.concatenate(rows, axis=0)


def mix_ac_fwd(proj, conv_w, wst, bsx, lg, lb):
    T = proj.shape[0]
    tm = TM_AC

    def body(bch, halo, u0, u1, v0, v1, cw, ws, bs, lg_ref, lb_ref, ya, yc, zs):
        i = pl.program_id(0)
        pb = bch[...].astype(f32)
        z = pb[:, D:2 * D] * pb[:, 2 * D:]
        hz = halo[:, :D].astype(f32) * halo[:, D:].astype(f32)
        zs[0:HALO, :] = jnp.where(i > 0, hz, 0.0)
        zs[HALO:HALO + tm, :] = z
        cv = cw[0:1, :] * zs[HALO - 2:HALO - 2 + tm, :] + cw[1:2, :] * zs[HALO - 1:HALO - 1 + tm, :] + cw[2:3, :] * z
        ya[...] = (pb[:, :D] * cv).astype(ya.dtype)
        up = jnp.concatenate([u0[...], u1[...]], axis=1).astype(f32)
        vp = jnp.concatenate([v0[...], v1[...]], axis=1).astype(f32)
        u, _, _, _, sp = _gmlp_fwd(up, vp, ws, bs, lg_ref[...], lb_ref[...])
        yc[...] = (u * sp).astype(yc.dtype)

    full = lambda shape: pl.BlockSpec(shape, lambda i: (0,) * len(shape))
    return pl.pallas_call(
        body, name="mix_ac_fwd", grid=(T // tm,),
        in_specs=[pl.BlockSpec((tm, 3 * D), lambda i: (i, 1)),
                  pl.BlockSpec((HALO, 2 * D), lambda i: (jnp.maximum(i * (tm // HALO) - 1, 0), 2)),
                  *_uv_specs(), full((3, D)), full((8, BLK, BLK)), full((8, BLK, BLK)), full((1, D)), full((1, D))],
        out_specs=[pl.BlockSpec((tm, D), lambda i: (i, 0))] * 2,
        out_shape=[jax.ShapeDtypeStruct((T, D), MX)] * 2,
        scratch_shapes=[pltpu.VMEM((HALO + tm, D), f32)],
        compiler_params=_cp(("parallel",)),
    )(proj, proj, proj, proj, proj, proj, conv_w, wst, bsx, lg, lb)


def _swap_halves(x):
    lane = lax.broadcasted_iota(jnp.int32, x.shape, 1)
    return jnp.where((lane % HD) < HD // 2, pltpu.roll(x, x.shape[1] - HD // 2, 1), pltpu.roll(x, HD // 2, 1))


def _tile4(t):
    return jnp.concatenate([t] * (AO // LANES), axis=1)


TM_FOLD = 1024


def _fold_out(nat, x, out_ref, d):
    if d == 1:
        out_ref[0] = x.astype(out_ref.dtype)
        return
    rows = x.shape[0] // d
    for j in range(AO // LANES):
        nat[j] = x[:, j * LANES:(j + 1) * LANES]
    for r in range(d):
        out_ref[r] = jnp.concatenate(
            [nat.at[j][pl.ds(r, rows, stride=d), :] for j in range(AO // LANES)], axis=1).astype(out_ref.dtype)


def _unfold_in(nat, in_ref, d):
    if d == 1:
        return in_ref[0].astype(f32)
    rows = in_ref.shape[1]
    for r in range(d):
        v = in_ref[r].astype(f32)
        for j in range(AO // LANES):
            nat.at[j][pl.ds(r, rows, stride=d), :] = v[:, j * LANES:(j + 1) * LANES]
    return jnp.concatenate([nat[j] for j in range(AO // LANES)], axis=1)


def fold_rope(proj, cos_t, sin_t, g, d):
    T = proj.shape[0]
    tm = TM_FOLD
    rows = tm // d

    def body(x_ref, c_ref, s_ref, q_o, k_o, v_o, nat):
        cos, sin = _tile4(c_ref[...]), _tile4(s_ref[...])
        for part, out, scale in ((0, q_o, HD ** -0.5), (1, k_o, 1.0), (2, v_o, None)):
            x = x_ref[:, part * AO:(part + 1) * AO].astype(f32)
            if scale is not None:
                x = (x * cos + _swap_halves(x) * sin) * scale
            _fold_out(nat, x, out, d)

    fold_spec = pl.BlockSpec((d, rows, AO), lambda i: (0, i, 0))
    return pl.pallas_call(
        body, name=f"fold_rope{g}", grid=(T // tm,),
        in_specs=[pl.BlockSpec((tm, 3 * AO), lambda i: (i, C_QKV // (3 * AO) + g)),
                  pl.BlockSpec((tm, LANES), lambda i: (i, 0)), pl.BlockSpec((tm, LANES), lambda i: (i, 0))],
        out_specs=[fold_spec] * 3,
        out_shape=[jax.ShapeDtypeStruct((d, T // d, AO), MX)] * 3,
        scratch_shapes=[pltpu.VMEM((AO // LANES, tm, LANES), f32)],
        compiler_params=_cp(("parallel",)),
    )(proj, cos_t, sin_t)


def _stack_heads(x):
    lane = lax.broadcasted_iota(jnp.int32, x.shape, 1)
    z = jnp.zeros_like(x)
    return jnp.concatenate([jnp.where(lane < HD, x, z), jnp.where(lane >= HD, x, z)], axis=0)


def _unstack_heads(y):
    lane = lax.broadcasted_iota(jnp.int32, (BLK, LANES), 1)
    return jnp.where(lane < HD, y[:BLK], y[BLK:])


def _window_masks():
    row = lax.broadcasted_iota(jnp.int32, (2 * BLK, 2 * BLK), 0) % BLK
    col = lax.broadcasted_iota(jnp.int32, (2 * BLK, 2 * BLK), 1)
    return (col < BLK) & (col >= row), (col >= BLK) & (col - BLK <= row)


def _two_blocks(ref, b):
    r0 = pl.multiple_of(b * BLK, BLK)
    rp = pl.multiple_of(jnp.maximum(b - 1, 0) * BLK, BLK)
    return jnp.concatenate([ref[pl.ds(rp, BLK), :], ref[pl.ds(r0, BLK), :]], axis=0)


def _merge_masks():
    row = lax.broadcasted_iota(jnp.int32, (2 * BLK, BLK), 0) % BLK
    col = lax.broadcasted_iota(jnp.int32, (2 * BLK, BLK), 1)
    return col <= row, col == row


def attn_fwd(qf, kf, vf, g, nb):
    T = qf.shape[0]

    def body(q_ref, k_ref, v_ref, o_ref, l_ref):
        cur_m, own_m = _merge_masks()

        def step(b, carry):
            r0 = pl.multiple_of(b * BLK, BLK)
            rp = pl.multiple_of(jnp.maximum(b - 1, 0) * BLK, BLK)
            qs = _stack_heads(q_ref[pl.ds(r0, BLK), :])
            vc, vp = v_ref[pl.ds(r0, BLK), :], v_ref[pl.ds(rp, BLK), :]
            sp = jnp.where((b % nb) != 0, _dot(qs, k_ref[pl.ds(rp, BLK), :], NT), NEG)
            s = jnp.where(cur_m, _dot(qs, k_ref[pl.ds(r0, BLK), :], NT), sp)
            s_own = jnp.sum(jnp.where(own_m, sp, 0.0), axis=-1, keepdims=True)
            m = jnp.maximum(jnp.max(s, axis=-1, keepdims=True), s_own)
            p, p_own = jnp.exp(s - m), jnp.exp(s_own - m)
            l = jnp.sum(p, axis=-1, keepdims=True) + p_own
            pb = p.astype(MX)
            zero = jnp.zeros_like(pb)
            o = _dot(jnp.where(cur_m, pb, zero), vc) + _dot(jnp.where(cur_m, zero, pb), vp)
            o = (o + p_own * jnp.concatenate([vp, vp], axis=0).astype(f32)) / l
            o_ref[pl.ds(r0, BLK), :] = _unstack_heads(o).astype(o_ref.dtype)
            l_ref[pl.ds(r0, BLK), :] = _unstack_heads(jnp.broadcast_to(m + jnp.log(l), (2 * BLK, LANES)))
            return carry

        lax.fori_loop(0, T // BLK, step, 0, unroll=8)

    spec = pl.BlockSpec((T, LANES), lambda j: (0, j))
    return pl.pallas_call(
        body, name=f"attn_fwd{g}", grid=(AO // LANES,),
        in_specs=[spec] * 3, out_specs=[spec] * 2,
        out_shape=[jax.ShapeDtypeStruct((T, AO), ACT), jax.ShapeDtypeStruct((T, AO), f32)],
        compiler_params=_cp(("parallel",), 56),
    )(qf, kf, vf)


def _group_weights(lses):
    m = jnp.maximum(jnp.maximum(lses[0], lses[1]), lses[2])
    e = [jnp.exp(l - m) for l in lses]
    inv = 1.0 / (e[0] + e[1] + e[2])
    return [x * inv for x in e]


def _fold_specs(T, tm):
    specs = []
    for _, d in GROUPS:
        specs.append(pl.BlockSpec((d, tm // d, AO), lambda i: (0, i, 0)))
    return specs


def combine_fwd(os_, lses):
    T = os_[0].shape[0] * os_[0].shape[1]
    tm = TM_FOLD

    def body(o0, o1, o2, l0, l1, l2, y_ref, nat):
        o = [_unfold_in(nat, r, d) for r, (_, d) in zip((o0, o1, o2), GROUPS)]
        ls = [_unfold_in(nat, r, d) for r, (_, d) in zip((l0, l1, l2), GROUPS)]
        w = _group_weights(ls)
        y_ref[...] = (w[0] * o[0] + w[1] * o[1] + w[2] * o[2]).astype(y_ref.dtype)

    specs = _fold_specs(T, tm)
    return pl.pallas_call(
        body, name="combine_fwd", grid=(T // tm,),
        in_specs=specs + specs, out_specs=pl.BlockSpec((tm, AO), lambda i: (i, 0)),
        out_shape=jax.ShapeDtypeStruct((T, AO), MX),
        scratch_shapes=[pltpu.VMEM((AO // LANES, tm, LANES), f32)],
        compiler_params=_cp(("parallel",)),
    )(*os_, *lses)


TM_MIX = 512


def mix_out_fwd(proj, ya, yb, yc, x0, pa, pb, pc, wo, g1, b1):
    T = x0.shape[0]
    tm = min(TM_MIX, T)

    def body(gt, ya_r, yb_r, yc_r, x0_r, pa_r, pb_r, pc_r, wo_r, g_r, b_r, mabc, m_o, r1_o, x1_o, x1b_o):
        ma = _dot(ya_r[...], pa_r[...])
        ybv = yb_r[...]
        mb = jnp.concatenate([_dot(ybv, pb_r[k]) for k in range(NCHIP)], axis=1)
        mc = _dot(yc_r[...], pc_r[...])
        m = jnp.zeros((tm, D), f32)
        for j, mm in enumerate((ma, mb, mc)):
            mabc[:, j * D:(j + 1) * D] = mm.astype(mabc.dtype)
            m = m + _sigmoid(gt[:, j * D:(j + 1) * D].astype(f32)) * mm
        mb16 = m.astype(MX)
        m_o[...] = mb16
        r1 = ALPHA * x0_r[...] + _dot(mb16, wo_r[...])
        r1_o[...] = r1
        xhat, _ = _ln_stats(r1)
        x1 = xhat * g_r[...] + b_r[...]
        x1_o[...] = x1
        x1b_o[...] = x1.astype(MX)

    full = lambda shape: pl.BlockSpec(shape, lambda i: (0,) * len(shape), pipeline_mode=pl.Buffered(1))
    tile = lambda w: pl.BlockSpec((tm, w), lambda i: (i, 0))
    return pl.pallas_call(
        body, name="mix_out_fwd", grid=(T // tm,),
        in_specs=[tile(3 * D), tile(D), tile(AO), tile(D), tile(D), full((D, D)), full((NCHIP, AO, D // NCHIP)),
                  full((D, D)), full((D, D)), full((1, D)), full((1, D))],
        out_specs=[tile(3 * D), tile(D), tile(D), tile(D), tile(D)],
        out_shape=[jax.ShapeDtypeStruct((T, 3 * D), MX), jax.ShapeDtypeStruct((T, D), MX),
                   jax.ShapeDtypeStruct((T, D), f32), jax.ShapeDtypeStruct((T, D), f32), jax.ShapeDtypeStruct((T, D), MX)],
        compiler_params=_cp(("parallel",), 56),
    )(proj, ya, yb, yc, x0, pa, pb, pc, wo, g1, b1)


TM_FF = 512
TM_FFB = 512
ROW_CHUNK = 64


TM_FFW = 256


def ffn_fwd(x1b, x1, wg, wu, wd, g2, b2):
    T = x1.shape[0]
    tm = min(TM_FFW, T)

    def body(xb_r, x_r, wg_r, wu_r, wd_r, g_r, b_r, g_o, u_o, r2_o, x2_o, gs, us, hs):
        xb = xb_r[...]
        r2 = ALPHA * x_r[...]
        for k in range(NCHIP):
            gs[...] = _dot(xb, wg_r[k])
            us[...] = _dot(xb, wu_r[k])
            for r in range(0, tm, ROW_CHUNK):
                rows = pl.ds(r, ROW_CHUNK)
                gate, up = gs[rows, :], us[rows, :]
                g_o[k, rows, :] = gate.astype(g_o.dtype)
                u_o[k, rows, :] = up.astype(u_o.dtype)
                hs[rows, :] = (gate * _sigmoid(gate) * up).astype(hs.dtype)
            r2 = r2 + _dot(hs[...], wd_r[k])
        r2_o[...] = r2
        xhat, _ = _ln_stats(r2)
        x2_o[...] = xhat * g_r[...] + b_r[...]

    once = dict(pipeline_mode=pl.Buffered(1))
    wspec = pl.BlockSpec((NCHIP, D, FB), lambda i: (0, 0, 0), **once)
    ospec = pl.BlockSpec((NCHIP, tm, FB), lambda i: (0, i, 0))
    tile = pl.BlockSpec((tm, D), lambda i: (i, 0))
    vec = pl.BlockSpec((1, D), lambda i: (0, 0))
    return pl.pallas_call(
        body, name="ffn_fwd", grid=(T // tm,),
        in_specs=[tile, tile, wspec, wspec, pl.BlockSpec((NCHIP, FB, D), lambda i: (0, 0, 0), **once), vec, vec],
        out_specs=[ospec] * 2 + [tile, tile],
        out_shape=[jax.ShapeDtypeStruct((NCHIP, T, FB), ACT)] * 2 + [jax.ShapeDtypeStruct((T, D), f32)] * 2,
        scratch_shapes=[pltpu.VMEM((tm, FB), f32)] * 2 + [pltpu.VMEM((tm, FB), MX)],
        compiler_params=_cp(("parallel",), 56),
    )(x1b, x1, wg, wu, wd, g2, b2)


def loss_grad(y, tgt):
    T = y.shape[0]
    tm = min(512, T)

    def body(y_r, t_r, l_o, dy_o):
        e = y_r[...] - t_r[...]
        dy_o[...] = e * (1.0 / D)

        @pl.when(pl.program_id(0) == 0)
        def _():
            l_o[...] = jnp.zeros_like(l_o)
        l_o[...] += (0.5 / D) * jnp.sum(e * e)

    tile = pl.BlockSpec((tm, D), lambda i: (i, 0))
    return pl.pallas_call(
        body, name="loss_grad", grid=(T // tm,),
        in_specs=[tile, tile], out_specs=[pl.BlockSpec((8, LANES), lambda i: (0, 0)), tile],
        out_shape=[jax.ShapeDtypeStruct((8, LANES), f32), jax.ShapeDtypeStruct((T, D), f32)],
        compiler_params=_cp(("arbitrary",)),
    )(y, tgt)


def ffn_down_bwd(dx2, r2, g2, wd, gate, up):
    T = dx2.shape[0]
    tm = min(TM_FFB, T)

    def body(dx_r, r_r, g_r, w_r, ga_r, up_r, dr_o, drb_o, dg_o, du_o, hh_o, dlg_o, dlb_o, hs):
        i = pl.program_id(0)
        xhat, rstd = _ln_stats(r_r[...])
        dx = dx_r[...]
        _acc_rows(dlg_o, i == 0, dx * xhat)
        _acc_rows(dlb_o, i == 0, dx)
        dr = _ln_bwd(dx, xhat, rstd, g_r[...])
        dr_o[...] = dr
        drb = dr.astype(MX)
        drb_o[...] = drb
        for k in range(NCHIP):
            hs[...] = _dot(drb, w_r[k], NT)
            for r in range(0, tm, ROW_CHUNK):
                rows = pl.ds(r, ROW_CHUNK)
                dhh, gate_v, up_v = hs[rows, :], ga_r[k, rows, :].astype(f32), up_r[k, rows, :].astype(f32)
                sg = _sigmoid(gate_v)
                dg_o[k, rows, :] = (dhh * up_v * sg * (1.0 + gate_v * (1.0 - sg))).astype(dg_o.dtype)
                silu = gate_v * sg
                du_o[k, rows, :] = (dhh * silu).astype(du_o.dtype)
                hh_o[k, rows, :] = (silu * up_v).astype(hh_o.dtype)

    tile = pl.BlockSpec((tm, D), lambda i: (i, 0))
    vec = pl.BlockSpec((1, D), lambda i: (0, 0))
    blk = pl.BlockSpec((NCHIP, tm, FB), lambda i: (0, i, 0))
    return pl.pallas_call(
        body, name="ffn_down_bwd", grid=(T // tm,),
        in_specs=[tile, tile, vec, pl.BlockSpec((NCHIP, FB, D), lambda i: (0, 0, 0), pipeline_mode=pl.Buffered(1)), blk, blk],
        out_specs=[tile, tile, blk, blk, blk, vec, vec],
        out_shape=[jax.ShapeDtypeStruct((T, D), f32), jax.ShapeDtypeStruct((T, D), MX)]
        + [jax.ShapeDtypeStruct((NCHIP, T, FB), MX)] * 3 + [jax.ShapeDtypeStruct((1, D), f32)] * 2,
        scratch_shapes=[pltpu.VMEM((tm, FB), f32)],
        compiler_params=_cp(("arbitrary",), 58),
    )(dx2, r2, g2, wd, gate, up)


def ffn_up_bwd(dr2, dgate, dup, wg, wu, r1, g1):
    T = dr2.shape[0]
    tm = min(TM_FFB, T)

    def body(dr2_r, dg_r, du_r, wg_r, wu_r, r1_r, g_r, dr1_o, dr1b_o, dlg_o, dlb_o):
        i = pl.program_id(0)
        dx = ALPHA * dr2_r[...]
        for k in range(NCHIP):
            dx = dx + _dot(dg_r[k], wg_r[k], NT) + _dot(du_r[k], wu_r[k], NT)
        xhat, rstd = _ln_stats(r1_r[...])
        _acc_rows(dlg_o, i == 0, dx * xhat)
        _acc_rows(dlb_o, i == 0, dx)
        dr1 = _ln_bwd(dx, xhat, rstd, g_r[...])
        dr1_o[...] = dr1
        dr1b_o[...] = dr1.astype(MX)

    tile = pl.BlockSpec((tm, D), lambda i: (i, 0))
    vec = pl.BlockSpec((1, D), lambda i: (0, 0))
    blk = pl.BlockSpec((NCHIP, tm, FB), lambda i: (0, i, 0))
    wspec = pl.BlockSpec((NCHIP, D, FB), lambda i: (0, 0, 0), pipeline_mode=pl.Buffered(1))
    return pl.pallas_call(
        body, name="ffn_up_bwd", grid=(T // tm,),
        in_specs=[tile, blk, blk, wspec, wspec, tile, vec],
        out_specs=[tile, tile, vec, vec],
        out_shape=[jax.ShapeDtypeStruct((T, D), f32), jax.ShapeDtypeStruct((T, D), MX)]
        + [jax.ShapeDtypeStruct((1, D), f32)] * 2,
        compiler_params=_cp(("arbitrary",), 58),
    )(dr2, dgate, dup, wg, wu, r1, g1)


TM_MIXB = 256


def mix_out_bwd(dr1, proj, mabc, wo, pa, pb, pc, ya, yb, yc, m):
    T = dr1.shape[0]
    tm = min(TM_MIXB, T)
    cb = D // NCHIP

    def body(dr_r, gt, mabc_r, wo_r, pa_r, pb_r, pc_r, ya_r, yb_r, yc_r, m_r,
             dgt_o, dya_o, dyb_o, dyc_o, dwo_o, dpa_o, dpb_o, dpc_o):
        @pl.when(pl.program_id(0) == 0)
        def _():
            for o in (dwo_o, dpa_o, dpb_o, dpc_o):
                o[...] = jnp.zeros_like(o)

        dr = dr_r[...].astype(MX)
        dm = _dot(dr, wo_r[...], NT)
        dmx = []
        for j in range(3):
            s = _sigmoid(gt[:, j * D:(j + 1) * D].astype(f32))
            dmx.append((dm * s).astype(MX))
            dgt_o[:, j * D:(j + 1) * D] = (dm * mabc_r[:, j * D:(j + 1) * D].astype(f32) * s * (1.0 - s)).astype(dgt_o.dtype)
        dya_o[...] = _dot(dmx[0], pa_r[...], NT).astype(dya_o.dtype)
        dyb = jnp.zeros((tm, AO), f32)
        for k in range(NCHIP):
            dyb = dyb + _dot(dmx[1][:, k * cb:(k + 1) * cb], pb_r[k], NT)
            dpb_o[k] += _dot(yb_r[...], dmx[1][:, k * cb:(k + 1) * cb], TN)
        dyb_o[...] = dyb.astype(dyb_o.dtype)
        dyc_o[...] = _dot(dmx[2], pc_r[...], NT).astype(dyc_o.dtype)
        dwo_o[...] += _dot(m_r[...], dr, TN)
        dpa_o[...] += _dot(ya_r[...], dmx[0], TN)
        dpc_o[...] += _dot(yc_r[...], dmx[2], TN)

    full = lambda shape: pl.BlockSpec(shape, lambda i: (0,) * len(shape), pipeline_mode=pl.Buffered(1))
    tile = lambda w: pl.BlockSpec((tm, w), lambda i: (i, 0))
    return pl.pallas_call(
        body, name="mix_out_bwd", grid=(T // tm,),
        in_specs=[tile(D), tile(3 * D), tile(3 * D), full((D, D)), full((D, D)), full((NCHIP, AO, cb)), full((D, D)),
                  tile(D), tile(AO), tile(D), tile(D)],
        out_specs=[tile(3 * D), tile(D), tile(AO), tile(D), full((D, D)), full((D, D)), full((NCHIP, AO, cb)), full((D, D))],
        out_shape=[jax.ShapeDtypeStruct((T, 3 * D), MX), jax.ShapeDtypeStruct((T, D), ACT),
                   jax.ShapeDtypeStruct((T, AO), ACT), jax.ShapeDtypeStruct((T, D), ACT), jax.ShapeDtypeStruct((D, D), f32),
                   jax.ShapeDtypeStruct((D, D), f32), jax.ShapeDtypeStruct((NCHIP, AO, cb), f32), jax.ShapeDtypeStruct((D, D), f32)],
        compiler_params=_cp(("arbitrary",), 58),
    )(dr1, proj, mabc, wo, pa, pb, pc, ya, yb, yc, m)


def transpose_cast(x):
    T = x.shape[0]
    tm = min(512, T)

    def body(x_r, o_r):
        o_r[...] = x_r[...].T.astype(o_r.dtype)

    return pl.pallas_call(
        body, name="transpose_cast", grid=(T // tm,),
        in_specs=[pl.BlockSpec((tm, D), lambda i: (i, 0))], out_specs=pl.BlockSpec((D, tm), lambda i: (0, i)),
        out_shape=jax.ShapeDtypeStruct((D, T), MX), compiler_params=_cp(("parallel",)),
    )(x)


def tn_matmul(name, a, b, a_spec, b_spec, out_shape, out_spec, grid):
    nt = len(grid) - 1

    def body(a_r, b_r, o_r):
        @pl.when(pl.program_id(nt) == 0)
        def _():
            o_r[...] = jnp.zeros_like(o_r)
        av = a_r[...].reshape(a_r.shape[-2:]).astype(MX)
        bv = b_r[...].reshape(b_r.shape[-2:]).astype(MX)
        o_r[...] += _dot(av, bv, TN).reshape(o_r.shape)

    return pl.pallas_call(
        body, name=name, grid=grid, in_specs=[a_spec, b_spec], out_specs=out_spec,
        out_shape=jax.ShapeDtypeStruct(out_shape, f32),
        compiler_params=_cp(("parallel",) * nt + ("arbitrary",), 56),
    )(a, b)


def attn_pre_bwd(dyb, os_, lses, ones):
    T = dyb.shape[0]
    tm = TM_FOLD

    def body(dy_r, o0, o1, o2, l0, l1, l2, ones_r, d0, d1, d2, f0, f1, f2, nat):
        o = [_unfold_in(nat, r, d) for r, (_, d) in zip((o0, o1, o2), GROUPS)]
        ls = [_unfold_in(nat, r, d) for r, (_, d) in zip((l0, l1, l2), GROUPS)]
        w = _group_weights(ls)
        dy = dy_r[...].astype(f32)
        t = dy * (w[0] * o[0] + w[1] * o[1] + w[2] * o[2])
        hi = t.astype(MX)
        lo = (t - hi.astype(f32)).astype(MX)
        c = _dot(hi, ones_r[...]) + _dot(lo, ones_r[...])
        for wg, do_o, df_o, (_, d) in zip(w, (d0, d1, d2), (f0, f1, f2), GROUPS):
            _fold_out(nat, wg * dy, do_o, d)
            _fold_out(nat, -wg * c, df_o, d)

    specs = _fold_specs(T, tm)
    return pl.pallas_call(
        body, name="attn_pre_bwd", grid=(T // tm,),
        in_specs=[pl.BlockSpec((tm, AO), lambda i: (i, 0))] + specs + specs + [pl.BlockSpec((AO, AO), lambda i: (0, 0))],
        out_specs=specs + specs,
        out_shape=[jax.ShapeDtypeStruct((d, T // d, AO), MX) for _, d in GROUPS]
        + [jax.ShapeDtypeStruct((d, T // d, AO), f32) for _, d in GROUPS],
        scratch_shapes=[pltpu.VMEM((AO // LANES, tm, LANES), f32)],
        compiler_params=_cp(("parallel",), 56),
    )(dyb, *os_, *lses, ones)


def _head_ones():
    i = jnp.arange(AO) // HD
    return (i[:, None] == i[None, :]).astype(MX)


BWD_BLOCKS = 8


def attn_bwd(qf, kf, vf, dof, lse, df, g, nb):
    T = qf.shape[0]

    def body(q_ref, k_ref, v_ref, do_ref, l_ref, d_ref, dq_ref, dk_ref, dv_ref):
        prev_m, cur_m = _window_masks()

        def head_col(ref, r0):
            v = ref[pl.ds(r0, BLK), :]
            return jnp.concatenate([v[:, 0:1], v[:, HD:HD + 1]], axis=0)

        def step(b, carry):
            dk_c, dv_c = carry
            r0 = pl.multiple_of(b * BLK, BLK)
            rp = pl.multiple_of(jnp.maximum(b - 1, 0) * BLK, BLK)
            qs, dos = _stack_heads(q_ref[pl.ds(r0, BLK), :]), _stack_heads(do_ref[pl.ds(r0, BLK), :])
            k2, v2 = _two_blocks(k_ref, b), _two_blocks(v_ref, b)
            valid = cur_m | (prev_m & ((b % nb) != 0))
            p = jnp.where(valid, jnp.exp(_dot(qs, k2, NT) - head_col(l_ref, r0)), 0.0)
            ds = (p * (_dot(dos, v2, NT) + head_col(d_ref, r0))).astype(MX)
            dq_ref[pl.ds(r0, BLK), :] = _unstack_heads(_dot(ds, k2)).astype(dq_ref.dtype)
            dk2 = _dot(ds, qs, TN)
            dv2 = _dot(p.astype(MX), dos, TN)
            dk_ref[pl.ds(rp, BLK), :] = (dk_c + dk2[:BLK]).astype(dk_ref.dtype)
            dv_ref[pl.ds(rp, BLK), :] = (dv_c + dv2[:BLK]).astype(dv_ref.dtype)
            return dk2[BLK:], dv2[BLK:]

        zero = jnp.zeros((BLK, LANES), f32)

        def steps(i, carry):
            for j in range(BWD_BLOCKS):
                carry = step(BWD_BLOCKS * i + j, carry)
            return carry

        dk_c, dv_c = lax.fori_loop(0, T // BLK // BWD_BLOCKS, steps, (zero, zero))
        dk_ref[pl.ds(T - BLK, BLK), :] = dk_c.astype(dk_ref.dtype)
        dv_ref[pl.ds(T - BLK, BLK), :] = dv_c.astype(dv_ref.dtype)

    spec = pl.BlockSpec((T, LANES), lambda j: (0, j))
    return pl.pallas_call(
        body, name=f"attn_bwd{g}", grid=(AO // LANES,),
        in_specs=[spec] * 6, out_specs=[spec] * 3,
        out_shape=[jax.ShapeDtypeStruct((T, AO), MX)] * 3,
        compiler_params=_cp(("parallel",), 60),
    )(qf, kf, vf, dof, lse, df)


def unfold_rope_bwd(dqf, dkf, dvf, cos_t, sin_t, g, d):
    T = dqf.shape[0] * dqf.shape[1]
    tm = TM_FOLD

    def body(q_r, k_r, v_r, c_ref, s_ref, o_ref, nat):
        cos, sin = _tile4(c_ref[...]), _tile4(s_ref[...])
        for part, ref, scale in ((0, q_r, HD ** -0.5), (1, k_r, 1.0), (2, v_r, None)):
            x = _unfold_in(nat, ref, d)
            if scale is not None:
                x = (x * cos - _swap_halves(x) * sin) * scale
            o_ref[:, part * AO:(part + 1) * AO] = x.astype(o_ref.dtype)

    fold_spec = pl.BlockSpec((d, tm // d, AO), lambda i: (0, i, 0))
    tab = pl.BlockSpec((tm, LANES), lambda i: (i, 0))
    return pl.pallas_call(
        body, name=f"unfold_rope_bwd{g}", grid=(T // tm,),
        in_specs=[fold_spec] * 3 + [tab, tab],
        out_specs=pl.BlockSpec((tm, 3 * AO), lambda i: (i, 0)),
        out_shape=jax.ShapeDtypeStruct((T, 3 * AO), MX),
        scratch_shapes=[pltpu.VMEM((AO // LANES, tm, LANES), f32)],
        compiler_params=_cp(("parallel",)),
    )(dqf, dkf, dvf, cos_t, sin_t)


CONV_CHUNK = 32


def conv_bwd(dya, proj, conv_w):
    T = dya.shape[0]
    tm = TM_AC
    last = T // tm - 1

    def body(dy_r, bch, hprev, dy_next, b_next, cw, d_o, dw_o, zs, ds):
        i = pl.program_id(0)
        ch = CONV_CHUNK
        hz = hprev[:, :D].astype(f32) * hprev[:, D:].astype(f32)
        zs[0:HALO, :] = jnp.where(i > 0, hz, 0.0)
        ds[tm:tm + HALO, :] = jnp.where(i < last, dy_next[...].astype(f32) * b_next[...].astype(f32), 0.0)
        for r in range(0, tm, ch):
            zs[HALO + r:HALO + r + ch, :] = bch[r:r + ch, D:2 * D].astype(f32) * bch[r:r + ch, 2 * D:].astype(f32)
            ds[r:r + ch, :] = dy_r[r:r + ch, :].astype(f32) * bch[r:r + ch, :D].astype(f32)

        @pl.when(i == 0)
        def _():
            dw_o[...] = jnp.zeros_like(dw_o)

        sums = [jnp.zeros((1, D), f32) for _ in range(3)]
        for r in range(0, tm, ch):
            z2, z1, z = (zs[HALO + r - s:HALO + r - s + ch, :] for s in (2, 1, 0))
            dcv, d1, d2 = (ds[r + s:r + s + ch, :] for s in (0, 1, 2))
            cv = cw[0:1, :] * z2 + cw[1:2, :] * z1 + cw[2:3, :] * z
            dz = cw[2:3, :] * dcv + cw[1:2, :] * d1 + cw[0:1, :] * d2
            d_o[r:r + ch, :D] = (dy_r[r:r + ch, :].astype(f32) * cv).astype(d_o.dtype)
            d_o[r:r + ch, D:2 * D] = (dz * bch[r:r + ch, 2 * D:].astype(f32)).astype(d_o.dtype)
            d_o[r:r + ch, 2 * D:] = (dz * bch[r:r + ch, D:2 * D].astype(f32)).astype(d_o.dtype)
            for k, zz in enumerate((z2, z1, z)):
                sums[k] = sums[k] + jnp.sum(dcv * zz, axis=0, keepdims=True)
        for k in range(3):
            dw_o[k:k + 1, :] += sums[k]

    nh = tm // HALO
    return pl.pallas_call(
        body, name="conv_bwd", grid=(T // tm,),
        in_specs=[pl.BlockSpec((tm, D), lambda i: (i, 0)), pl.BlockSpec((tm, 3 * D), lambda i: (i, 1)),
                  pl.BlockSpec((HALO, 2 * D), lambda i: (jnp.maximum(i * nh - 1, 0), 2)),
                  pl.BlockSpec((HALO, D), lambda i: (jnp.minimum((i + 1) * nh, T // HALO - 1), 0)),
                  pl.BlockSpec((HALO, D), lambda i: (jnp.minimum((i + 1) * nh, T // HALO - 1), 3)),
                  pl.BlockSpec((3, D), lambda i: (0, 0))],
        out_specs=[pl.BlockSpec((tm, 3 * D), lambda i: (i, 0)), pl.BlockSpec((3, D), lambda i: (0, 0))],
        out_shape=[jax.ShapeDtypeStruct((T, 3 * D), MX), jax.ShapeDtypeStruct((3, D), f32)],
        scratch_shapes=[pltpu.VMEM((HALO + tm, D), f32), pltpu.VMEM((tm + HALO, D), f32)],
        compiler_params=_cp(("arbitrary",)),
    )(dya, proj, proj, dya, proj, conv_w)


def gmlp_bwd(dyc, proj, wst, bsx, lg, lb):
    T = dyc.shape[0]
    tm = TM_AC
    last = T // tm - 1

    def body(dy_r, u0, u1, v0, v1, ws, bs, lg_r, lb_r, d_o, dws_o, dbs_o, dlg_o, dlb_o, bacc):
        i = pl.program_id(0)
        up = jnp.concatenate([u0[...], u1[...]], axis=1).astype(f32)
        vp = jnp.concatenate([v0[...], v1[...]], axis=1).astype(f32)
        u, du = _gelu_and_grad(up)
        gv, dgv = _gelu_and_grad(vp)
        u, vn, xhat, rstd, sp = _gmlp_fwd(up, vp, ws, bs, lg_r[...], lb_r[...], u, gv)
        dy = dy_r[...].astype(f32)
        d_o[:, :D] = (dy * sp * du).astype(d_o.dtype)
        dsp = dy * u
        dspb, vnb = dsp.astype(MX), vn.astype(MX)

        @pl.when(i == 0)
        def _():
            dws_o[...] = jnp.zeros_like(dws_o)
            bacc[...] = jnp.zeros_like(bacc)

        rows = []
        for c in range(tm // BLK):
            r = slice(c * BLK, (c + 1) * BLK)
            cols = []
            for g in range(8):
                cs = slice(g * BLK, (g + 1) * BLK)
                dws_o[g] += _dot(dspb[r, cs], vnb[r, cs], NT)
                bacc[g] += dsp[r, cs]
                cols.append(_dot(ws[g], dspb[r, cs], TN))
            rows.append(jnp.concatenate(cols, axis=1))
        dvn = jnp.concatenate(rows, axis=0)
        _acc_rows(dlg_o, i == 0, dvn * xhat)
        _acc_rows(dlb_o, i == 0, dvn)
        d_o[:, D:] = (_ln_bwd(dvn, xhat, rstd, lg_r[...]) * dgv).astype(d_o.dtype)

        @pl.when(i == last)
        def _():
            row = lax.broadcasted_iota(jnp.int32, (BLK, BLK), 0)
            col = lax.broadcasted_iota(jnp.int32, (BLK, BLK), 1)
            ones = jnp.ones((8, BLK), MX)
            for g in range(8):
                dws_o[g] = jnp.where(col <= row, dws_o[g], 0.0)
                a = bacc[g]
                hi = a.astype(MX)
                lo = (a - hi.astype(f32)).astype(MX)
                dbs_o[g:g + 1, :] = (_dot(ones, hi, NT) + _dot(ones, lo, NT))[0:1, :]

    full = lambda shape: pl.BlockSpec(shape, lambda i: (0,) * len(shape))
    return pl.pallas_call(
        body, name="gmlp_bwd", grid=(T // tm,),
        in_specs=[pl.BlockSpec((tm, D), lambda i: (i, 0)), *_uv_specs(), full((8, BLK, BLK)), full((8, BLK, BLK)),
                  full((1, D)), full((1, D))],
        out_specs=[pl.BlockSpec((tm, 2 * D), lambda i: (i, 0)), full((8, BLK, BLK)), full((8, BLK)), full((1, D)), full((1, D))],
        out_shape=[jax.ShapeDtypeStruct((T, 2 * D), MX), jax.ShapeDtypeStruct((8, BLK, BLK), f32),
                   jax.ShapeDtypeStruct((8, BLK), f32), jax.ShapeDtypeStruct((1, D), f32), jax.ShapeDtypeStruct((1, D), f32)],
        scratch_shapes=[pltpu.VMEM((8, BLK, BLK), f32)],
        compiler_params=_cp(("arbitrary",)),
    )(dyc, proj, proj, proj, proj, wst, bsx, lg, lb)


PART_TILES = (6, 6, 3, 3, 3, 4)
PART_START = (0, 6, 12, 15, 18, 21)
TJ = 512


def _part_specs(tm, rows_axis):
    specs = []
    for n, s in zip(PART_TILES, PART_START):
        def imap(*idx, n=n, s=s):
            i, j = idx[rows_axis], idx[1 - rows_axis]
            inside = (j >= s) & (j < s + n)
            return (jnp.where(inside, i, 0), jnp.clip(j - s, 0, n - 1))
        specs.append(pl.BlockSpec((tm, TJ), imap))
    return specs


def _with_part(j, refs, fn):
    for r, n, s in zip(refs, PART_TILES, PART_START):
        @pl.when((j >= s) & (j < s + n))
        def _():
            fn(r[...])


def dx_in(dr1, parts, w, bias):
    T = dr1.shape[0]
    tm = min(2048, T)

    def body(dr_r, p0, p1, p2, p3, p4, p5, w_r, b_r, o_r):
        j = pl.program_id(1)

        @pl.when(j == 0)
        def _():
            o_r[...] = ALPHA * dr_r[...] + b_r[...]

        def acc(tile):
            o_r[...] += _dot(tile, w_r[...], NT)
        _with_part(j, (p0, p1, p2, p3, p4, p5), acc)

    once = dict(pipeline_mode=pl.Buffered(1))
    return pl.pallas_call(
        body, name="dx_in", grid=(T // tm, NIN // TJ),
        in_specs=[pl.BlockSpec((tm, D), lambda i, j: (i, 0), **once)] + _part_specs(tm, 0)
        + [pl.BlockSpec((D, TJ), lambda i, j: (0, j)), pl.BlockSpec((1, D), lambda i, j: (0, 0))],
        out_specs=pl.BlockSpec((tm, D), lambda i, j: (i, 0), **once),
        out_shape=jax.ShapeDtypeStruct((T, D), f32),
        compiler_params=_cp(("parallel", "arbitrary"), 56),
    )(dr1, *parts, w, bias)


def dw_in(x0t, parts):
    T = x0t.shape[1]
    tk = min(2048, T)

    def body(x_r, p0, p1, p2, p3, p4, p5, o_r):
        j, t = pl.program_id(0), pl.program_id(1)

        @pl.when(t == 0)
        def _():
            o_r[...] = jnp.zeros_like(o_r)

        def acc(tile):
            o_r[...] += _dot(x_r[:, pl.ds(pl.multiple_of(t * tk, tk), tk)], tile)
        _with_part(j, (p0, p1, p2, p3, p4, p5), acc)

    return pl.pallas_call(
        body, name="dw_in", grid=(NIN // TJ, T // tk),
        in_specs=[pl.BlockSpec((D, T), lambda j, t: (0, 0), pipeline_mode=pl.Buffered(1))] + _part_specs(tk, 1),
        out_specs=pl.BlockSpec((D, TJ), lambda j, t: (0, j)),
        out_shape=jax.ShapeDtypeStruct((D, NIN), f32),
        compiler_params=_cp(("parallel", "arbitrary"), 56),
    )(x0t, *parts)


def rope_tables(positions):
    half = HD // 2
    inv_freq = ROPE_THETA ** (-jnp.arange(half, dtype=f32) / half)
    ang = positions.astype(f32)[:, None] * inv_freq
    cos, sin = jnp.cos(ang), jnp.sin(ang)
    return jnp.tile(cos, (1, LANES // half)), jnp.tile(jnp.concatenate([-sin, sin], axis=1), (1, LANES // HD))


def _flat(a):
    return a.reshape(a.shape[0] * a.shape[1], a.shape[2])


def layer_fwd(x0, W, cos_t, sin_t):
    T = x0.shape[0]
    proj = mm_in(x0, W["w_in"], W["in_bias"])
    ya, yc = mix_ac_fwd(proj, W["conv_w"], W["wst"], W["bsx"], W["gmlp_ln_g"], W["gmlp_ln_b"])
    folded, os_, lses = [], [], []
    for g, (_, d) in enumerate(GROUPS):
        qf, kf, vf = fold_rope(proj, cos_t, sin_t, g, d)
        o, lse = attn_fwd(_flat(qf), _flat(kf), _flat(vf), g, T // d // BLK)
        folded.append((qf, kf, vf))
        os_.append(o.reshape(d, T // d, AO))
        lses.append(lse.reshape(d, T // d, AO))
    yb = combine_fwd(os_, lses)
    if "late" in W:
        W = {**W, **W["late"](yb)}
    mabc, m, r1, x1, x1b = mix_out_fwd(proj, ya, yb, yc, x0, W["p_a"], W["p_b"], W["p_c"], W["w_o"], W["ln1_g"], W["ln1_b"])
    gate, up, r2, x2 = ffn_fwd(x1b, x1, W["w_gate"], W["w_up"], W["w_down"], W["ln2_g"], W["ln2_b"])
    saved = dict(x0=x0, proj=proj, ya=ya, yb=yb, yc=yc, folded=folded, os=os_, lses=lses, mabc=mabc, m=m, r1=r1,
                 x1b=x1b, gate=gate, up=up, r2=r2)
    return x2, saved, W


def layer_bwd(dx2, S, W, cos_t, sin_t, on_grads=None):
    T = dx2.shape[0]
    tk = min(4096, T)
    G = {}
    dr2, dr2b, dgate, dup, hh, G["ln2_g"], G["ln2_b"] = ffn_down_bwd(dx2, S["r2"], W["ln2_g"], W["w_down"], S["gate"], S["up"])
    blk_a = pl.BlockSpec((1, tk, FB), lambda k, t: (k, t, 0))
    row_b = pl.BlockSpec((tk, D), lambda k, t: (t, 0))
    G["w_down"] = tn_matmul("dw_down", hh, dr2b, blk_a, row_b, (NCHIP, FB, D),
                            pl.BlockSpec((1, FB, D), lambda k, t: (k, 0, 0)), (NCHIP, T // tk))
    for nm, dv in (("w_gate", dgate), ("w_up", dup)):
        G[nm] = tn_matmul("d" + nm, dv, S["x1b"], blk_a, row_b, (NCHIP, FB, D),
                          pl.BlockSpec((1, FB, D), lambda k, t: (k, 0, 0)), (NCHIP, T // tk))
    dr1, dr1b, G["ln1_g"], G["ln1_b"] = ffn_up_bwd(dr2, dgate, dup, W["w_gate"], W["w_up"], S["r1"], W["ln1_g"])
    dgates, dya, dyb, dyc, G["w_o"], G["p_a"], G["p_b"], G["p_c"] = mix_out_bwd(
        dr1b, S["proj"], S["mabc"], W["w_o"], W["p_a"], W["p_b"], W["p_c"], S["ya"], S["yb"], S["yc"], S["m"])
    conv_w = W["conv_w"]
    if on_grads is not None:
        conv_w = conv_w + on_grads({n: G[n] for n in BIG if n != "w_in"})
    dbch, G["conv_w"] = conv_bwd(dya, S["proj"], conv_w)
    duv, G["w_s"], G["b_s"], G["gmlp_ln_g"], G["gmlp_ln_b"] = gmlp_bwd(
        dyc, S["proj"], W["wst"], W["bsx"], W["gmlp_ln_g"], W["gmlp_ln_b"])
    ones = _head_ones()
    if on_grads is not None:
        small = {n: G[n] for n in VECS + ("b_s", "w_s", "conv_w")}
        ones = ones + on_grads(small).astype(MX)
    pre = attn_pre_bwd(dyb, S["os"], S["lses"], ones)
    dqkv = []
    for g, (_, d) in enumerate(GROUPS):
        qf, kf, vf = S["folded"][g]
        dqf, dkf, dvf = attn_bwd(_flat(qf), _flat(kf), _flat(vf), _flat(pre[g]), _flat(S["lses"][g]), _flat(pre[3 + g]),
                                 g, T // d // BLK)
        shp = (d, T // d, AO)
        dqkv.append(unfold_rope_bwd(dqf.reshape(shp), dkf.reshape(shp), dvf.reshape(shp), cos_t, sin_t, g, d))
    parts = (dgates, dbch, *dqkv, duv)
    G["w_in"] = dw_in(transpose_cast(S["x0"]), parts)
    bias = jnp.zeros((1, D), f32)
    if on_grads is not None:
        bias = bias + on_grads({"w_in": G["w_in"]})
    dx0 = dx_in(dr1, parts, W["w_in"], bias)
    started = on_grads({"dx": dx0}) if on_grads is not None else None
    return dx0, G, started


def prep_layer_weights(Wl):
    W = dict(Wl)
    tril = jnp.tril(jnp.ones((BLK, BLK), f32))
    W["wst"] = (Wl["w_s"] * tril[None]).astype(MX)
    W["bsx"] = jnp.broadcast_to(Wl["b_s"][:, :, None], (8, BLK, BLK))
    for n in ("gmlp_ln_g", "gmlp_ln_b", "ln1_g", "ln1_b", "ln2_g", "ln2_b"):
        W[n] = Wl[n].reshape(1, D)
    W["in_bias"] = jnp.zeros((1, NIN), f32) + Wl.get("after", 0.0)
    return W


def local_step(x, positions, target, layers, on_grads=None):
    cos_t, sin_t = rope_tables(positions)
    Ws, saved = [], []
    h = x
    for Wl in layers:
        h, S, W = layer_fwd(h, prep_layer_weights(Wl(h) if callable(Wl) else Wl), cos_t, sin_t)
        Ws.append(W)
        saved.append(S)
    lsum, dh = loss_grad(h, target)
    if on_grads is not None:
        on_grads(len(Ws), {"loss": lsum})
    grads = [None] * len(Ws)
    started = None
    for l in reversed(range(len(Ws))):
        W = Ws[l]
        if started is not None:
            W = dict(W, ln2_g=W["ln2_g"] + started)
        hook = functools.partial(on_grads, l) if on_grads is not None else None
        dh, grads[l], started = layer_bwd(dh, saved[l], W, cos_t, sin_t, hook)
    return lsum, dh, grads


MESH = pl.DeviceIdType.MESH
ANY = pl.BlockSpec(memory_space=pl.ANY)
BIG = ("w_in", "w_gate", "w_up", "w_down", "p_a", "p_b", "p_c", "w_o")
NBIG = len(BIG)


def _place():
    x, y, c = lax.axis_index("x"), lax.axis_index("y"), lax.axis_index("c")
    return x, y, c, 2 * x + y


def _rcopy(src, dst, send, recv, dev):
    return pltpu.make_async_remote_copy(src_ref=src, dst_ref=dst, send_sem=send, recv_sem=recv, device_id=dev,
                                        device_id_type=MESH)


def _cols(ref, k, width):
    start = k * width if isinstance(k, int) else pl.multiple_of(k * width, LANES)
    return ref.at[:, pl.ds(start, width)]


CHUNK_BYTES = 1 << 20


def _pieces(shape, itemsize, nbytes=CHUNK_BYTES):
    rows, cols = shape[-2], shape[-1]
    per = max(16, nbytes // (cols * itemsize) // 16 * 16)
    out = []
    for lead in (range(shape[0]) if len(shape) == 3 else (None,)):
        for r in range(0, rows, per):
            sl = (pl.ds(r, min(per, rows - r)), slice(None))
            out.append(sl if lead is None else (lead,) + sl)
    return out


def _start_pieces(src, dst, make, nbytes=CHUNK_BYTES):
    for idx in _pieces(src.shape, jnp.dtype(src.dtype).itemsize, nbytes):
        make(src.at[idx], dst.at[idx]).start()


def gather_halves(shards):
    n = len(shards)

    def body(*refs):
        srcs, dsts = refs[:n], refs[n:2 * n]
        send, recv, own_send, own_recv = refs[2 * n:]
        x, y, c, k = _place()
        sib = (x, y, 1 - c)
        chips = [(1 - x, y), (x, 1 - y), (1 - x, 1 - y)]

        def slot(a, layer, pos):
            if a == 0:
                return _cols(dsts[0].at[layer], pos, WIN_SHARD)
            return dsts[a].at[pos, layer]

        def ici(a, j, src, dst):
            return _rcopy(src, dst, send.at[a, j], recv.at[a, j], (*chips[j], c))

        def d2d(a, j, src, dst):
            return _rcopy(src, dst, send.at[a, 3 + j], recv.at[a, 3 + j], sib)

        def own(a, layer, src, dst):
            return _rcopy(src, dst, own_send.at[a, layer], own_recv.at[a, layer], sib)

        for a in range(n):
            for j in range(3):
                _start_pieces(srcs[a].at[c], slot(a, c, k), functools.partial(ici, a, j))
        for a in range(n):
            for layer in range(DEPTH):
                _start_pieces(srcs[a].at[layer], slot(a, layer, k), functools.partial(own, a, layer))
        for a in range(n):
            for j, (cx, cy) in enumerate(chips):
                landed = slot(a, c, 2 * cx + cy)
                ici(a, j, landed, landed).wait_recv()
                _start_pieces(landed, landed, functools.partial(d2d, a, j))
        for a in range(n):
            for j, (cx, cy) in enumerate(chips):
                passed = slot(a, 1 - c, 2 * cx + cy)
                d2d(a, j, passed, passed).wait_recv()
                landed = slot(a, c, 2 * cx + cy)
                d2d(a, j, landed, landed).wait_send()
                ici(a, j, srcs[a].at[c], slot(a, c, k)).wait_send()
            for layer in range(DEPTH):
                own(a, layer, srcs[a].at[layer], slot(a, layer, k)).wait()

    outs = [jax.ShapeDtypeStruct((2, shards[0].shape[1], NIN), shards[0].dtype)]
    outs += [jax.ShapeDtypeStruct((NCHIP,) + s.shape, s.dtype) for s in shards[1:]]
    return pl.pallas_call(
        body, name="gather_halves", in_specs=[ANY] * n, out_specs=[ANY] * n, out_shape=outs,
        scratch_shapes=[pltpu.SemaphoreType.DMA((n, 6)), pltpu.SemaphoreType.DMA((n, 6)),
                        pltpu.SemaphoreType.DMA((n, DEPTH)), pltpu.SemaphoreType.DMA((n, DEPTH))],
    )(*shards)


def _gather_slot(dst, pos):
    return _cols(dst, pos, WIN_SHARD) if len(dst.shape) == 2 else dst.at[pos]


def _gather_copy(a, j, src, dst, send, recv, dev):
    return _rcopy(src, dst, send.at[a * NCHIP + j], recv.at[a * NCHIP + j], dev)


def gather_start(tag, shards, after):
    n = len(shards)

    def body(*refs):
        srcs, dsts = refs[:n], refs[n:2 * n]
        send, recv = refs[2 * n + len(after)], refs[2 * n + len(after) + 1]
        token = refs[-1]
        x, y, c, k = _place()
        peers = [(1 - x, y, c), (x, 1 - y, c), (1 - x, 1 - y, c), (x, y, 1 - c)]
        for a in range(n):
            for j, dev in enumerate(peers):
                _start_pieces(srcs[a], _gather_slot(dsts[a], k),
                              lambda s, d, a=a, j=j, dev=dev: _gather_copy(a, j, s, d, send, recv, dev))
        token[...] = jnp.zeros_like(token)

    gathered = [lax.empty((D, NIN) if s.shape == (D, WIN_SHARD) else (NCHIP,) + s.shape, s.dtype) for s in shards]
    ops = [pltpu.with_memory_space_constraint(v, pltpu.HBM) for v in list(shards) + gathered]
    sem = pltpu.SemaphoreType.DMA((n * NCHIP,))
    res = pl.pallas_call(
        body, name=f"gather_start{tag}", in_specs=[HBM] * (2 * n) + [ANY] * len(after),
        out_specs=[SEMS, SEMS] + [HBM] * (2 * n) + [pl.BlockSpec(memory_space=pltpu.VMEM)],
        out_shape=[sem, sem] + [pltpu.HBM(v.shape, v.dtype) for v in ops] + [jax.ShapeDtypeStruct((8, LANES), f32)],
        input_output_aliases={i: 2 + i for i in range(2 * n)},
        compiler_params=pltpu.CompilerParams(has_side_effects=EFFECT),
    )(*ops, *after)
    return res[0], res[1], res[2:2 + n], res[2 + n:2 + 2 * n], res[-1]


def gather_wait(tag, send, recv, shards, gathered, after):
    n = len(shards)

    def body(*refs):
        srcs, dsts = refs[:n], refs[n:2 * n]
        send_r, recv_r = refs[2 * n], refs[2 * n + 1]
        x, y, c, k = _place()
        peers = [(1 - x, y, c), (x, 1 - y, c), (1 - x, 1 - y, c), (x, y, 1 - c)]
        for a in range(n):
            for j, dev in enumerate(peers):
                _gather_copy(a, j, srcs[a], _gather_slot(dsts[a], k), send_r, recv_r, dev).wait_send()
                pos = 2 * dev[0] + dev[1]
                _gather_copy(a, j, srcs[a], _gather_slot(dsts[a], pos), send_r, recv_r, dev).wait_recv()

    ops = list(shards) + list(gathered)
    res = pl.pallas_call(
        body, name=f"gather_wait{tag}", in_specs=[HBM] * (2 * n) + [SEMS, SEMS] + [ANY] * len(after),
        out_specs=[HBM] * (2 * n), out_shape=[pltpu.HBM(v.shape, v.dtype) for v in ops],
        input_output_aliases={i: i for i in range(2 * n)},
        compiler_params=pltpu.CompilerParams(has_side_effects=EFFECT),
    )(*ops, send, recv, *after)
    return res[n:]


def _half(ref, h):
    rows = ref.shape[-2] // 2
    start = pl.multiple_of(h * rows, 16)
    if len(ref.shape) == 2:
        return ref.at[pl.ds(start, rows), :]
    return ref.at[:, pl.ds(start, rows), :]


HBM = pl.BlockSpec(memory_space=pltpu.HBM)
SEMS = pl.BlockSpec(memory_space=pltpu.SEMAPHORE)
EFFECT = pltpu.SideEffectType.DATAFLOW_SIDE_EFFECTING


def rs_pair_start(tag, grads, halves=True):
    n = len(grads)

    def body(*refs):
        g, theirs = refs[:n], refs[n:2 * n]
        send, recv = refs[2 * n], refs[2 * n + 1]
        x, y, c, _ = _place()
        for a in range(n):
            _start_pieces(_half(g[a], 1 - c) if halves else g[a], theirs[a],
                          lambda s, d, a=a: _rcopy(s, d, send.at[a], recv.at[a], (x, y, 1 - c)))
        refs[-1][...] = jnp.zeros_like(refs[-1])

    lands = [lax.empty(g.shape[:-2] + (g.shape[-2] // 2 if halves else g.shape[-2], g.shape[-1]), g.dtype) for g in grads]
    ops = [pltpu.with_memory_space_constraint(v, pltpu.HBM) for v in list(grads) + lands]
    sem = pltpu.SemaphoreType.DMA((n,))
    res = pl.pallas_call(
        body, name=f"rs_pair_start{tag}", in_specs=[HBM] * (2 * n),
        out_specs=[SEMS, SEMS] + [HBM] * (2 * n) + [pl.BlockSpec(memory_space=pltpu.VMEM)],
        out_shape=[sem, sem] + [pltpu.HBM(v.shape, v.dtype) for v in ops] + [jax.ShapeDtypeStruct((8, LANES), f32)],
        input_output_aliases={i: 2 + i for i in range(2 * n)},
        compiler_params=pltpu.CompilerParams(has_side_effects=EFFECT),
    )(*ops)
    return res[0], res[1], res[2:2 + n], res[2 + n:2 + 2 * n], res[-1]


def rs_pair_wait(tag, send, recv, grads, theirs, after, halves=True):
    n = len(grads)

    def body(*refs):
        g, land = refs[:n], refs[n:2 * n]
        send_r, recv_r = refs[2 * n], refs[2 * n + 1]
        x, y, c, _ = _place()
        for a in range(n):
            cp = _rcopy(_half(g[a], 1 - c) if halves else g[a], land[a], send_r.at[a], recv_r.at[a], (x, y, 1 - c))
            cp.wait_send()
            cp.wait_recv()

    ops = list(grads) + list(theirs)
    res = pl.pallas_call(
        body, name=f"rs_pair_wait{tag}", in_specs=[HBM] * (2 * n) + [SEMS, SEMS] + [ANY] * len(after),
        out_specs=[HBM] * (2 * n), out_shape=[pltpu.HBM(v.shape, v.dtype) for v in ops],
        input_output_aliases={i: i for i in range(2 * n)},
        compiler_params=pltpu.CompilerParams(has_side_effects=EFFECT),
    )(*ops, send, recv, *after)
    return res[:n], res[n:]


def _chip_piece(ref, k):
    return _cols(ref, k, WIN_SHARD) if len(ref.shape) == 2 else ref.at[k]


def _chip_copy(a, k, src, dst, send, recv, me, c):
    return _rcopy(src, dst, send.at[a * NCHIP + k], recv.at[a * NCHIP + me], (k // 2, k % 2, c))


def rs_chips_start(tag, sums):
    n = len(sums)

    def pshape(s):
        return (NCHIP, s[0], WIN_SHARD) if len(s) == 2 else s

    def body(*refs):
        s, land = refs[:n], refs[n:2 * n]
        send, recv = refs[2 * n], refs[2 * n + 1]
        token = refs[-1]
        x, y, c, me = _place()
        for k in range(NCHIP):
            @pl.when(me != k)
            def _():
                for a in range(n):
                    _start_pieces(_chip_piece(s[a], k), land[a].at[me],
                                  lambda src, dst, a=a: _chip_copy(a, k, src, dst, send, recv, me, c))
        token[...] = jnp.zeros_like(token)

    lands = [lax.empty(pshape(v.shape), v.dtype) for v in sums]
    ops = [pltpu.with_memory_space_constraint(v, pltpu.HBM) for v in list(sums) + lands]
    sem = pltpu.SemaphoreType.DMA((n * NCHIP,))
    res = pl.pallas_call(
        body, name=f"rs_chips_start{tag}", in_specs=[HBM] * (2 * n),
        out_specs=[SEMS, SEMS] + [HBM] * (2 * n) + [pl.BlockSpec(memory_space=pltpu.VMEM)],
        out_shape=[sem, sem] + [pltpu.HBM(v.shape, v.dtype) for v in ops] + [jax.ShapeDtypeStruct((8, LANES), f32)],
        input_output_aliases={i: 2 + i for i in range(2 * n)},
        compiler_params=pltpu.CompilerParams(has_side_effects=EFFECT),
    )(*ops)
    return res[0], res[1], res[2:2 + n], res[2 + n:2 + 2 * n], res[-1]


def rs_chips_wait(tag, send, recv, sums, lands, after):
    n = len(sums)

    def body(*refs):
        s, land = refs[:n], refs[n:2 * n]
        send_r, recv_r = refs[2 * n], refs[2 * n + 1]
        x, y, c, me = _place()
        for k in range(NCHIP):
            @pl.when(me != k)
            def _():
                for a in range(n):
                    piece = _chip_piece(s[a], k)
                    _chip_copy(a, k, piece, land[a].at[me], send_r, recv_r, me, c).wait_send()
                    _rcopy(piece, land[a].at[k], send_r.at[a * NCHIP + k], recv_r.at[a * NCHIP + k],
                           (k // 2, k % 2, c)).wait_recv()

    ops = list(sums) + list(lands)
    res = pl.pallas_call(
        body, name=f"rs_chips_wait{tag}", in_specs=[HBM] * (2 * n) + [SEMS, SEMS] + [ANY] * len(after),
        out_specs=[HBM] * (2 * n), out_shape=[pltpu.HBM(v.shape, v.dtype) for v in ops],
        input_output_aliases={i: i for i in range(2 * n)},
        compiler_params=pltpu.CompilerParams(has_side_effects=EFFECT),
    )(*ops, send, recv, *after)
    return res[:n], res[n:]


def _row_tile(rows, cols, itemsize=4, target=2 << 20):
    best = 8
    for t in range(8, rows + 1, 8):
        if rows % t == 0 and t * cols * itemsize <= target:
            best = t
    return best


GRAD_WIRE = jnp.bfloat16


def add_half(name, g, t, c):
    cols, half = t.shape[-1], t.shape[-2]
    nblk = 1 if t.ndim == 2 else t.shape[0]
    tr = _row_tile(half, cols)
    per = half // tr

    def body(c_ref, g_r, t_r, o_r):
        o_r[...] = (g_r[...] + t_r[...]).astype(o_r.dtype)

    tile_t = pl.BlockSpec((tr, cols), lambda i, c_ref: (i, 0))
    tile_g = pl.BlockSpec((tr, cols), lambda i, c_ref: ((i // per) * 2 * per + c_ref[0] * per + i % per, 0))
    out = pl.pallas_call(
        body, name=name, out_shape=jax.ShapeDtypeStruct((nblk * half, cols), GRAD_WIRE),
        grid_spec=pltpu.PrefetchScalarGridSpec(num_scalar_prefetch=1, grid=(nblk * per,), in_specs=[tile_g, tile_t],
                                               out_specs=tile_t),
        compiler_params=_cp(("parallel",)),
    )(c.reshape(1).astype(jnp.int32), g.reshape(nblk * 2 * half, cols), t.reshape(nblk * half, cols))
    return out.reshape(t.shape)


def add_chips(name, land, own):
    _, rows, cols = land.shape
    tr = _row_tile(rows, cols, target=1 << 20)

    def body(land_r, own_r, o_r):
        me = 2 * lax.axis_index("x") + lax.axis_index("y")
        for k in range(NCHIP):
            @pl.when(me == k)
            def _():
                acc = None
                for j in range(NCHIP):
                    t = (own_r[...] if j == k else land_r[j]).astype(f32)
                    acc = t if acc is None else acc + t
                o_r[...] = acc

    tile = pl.BlockSpec((tr, cols), lambda i: (i, 0))
    return pl.pallas_call(
        body, name=name, grid=(rows // tr,), in_specs=[pl.BlockSpec((NCHIP, tr, cols), lambda i: (0, i, 0)), tile],
        out_specs=tile, out_shape=jax.ShapeDtypeStruct((rows, cols), f32), compiler_params=_cp(("parallel",)),
    )(land, own)


def reduce_scatter_pair(tag, G):
    names = tuple(G)
    grads = [G[n] if G[n].ndim == 3 or n == "w_in" else G[n].reshape(NCHIP, D // NCHIP, D) for n in names]
    send, recv, grads, theirs, token = rs_pair_start(tag, grads)
    return (tag, names, send, recv, grads, theirs), token[0, 0]


def reduce_scatter_chips(state, after):
    c = lax.axis_index("c")
    tag, names, send, recv, grads, theirs = state
    grads, theirs = rs_pair_wait(tag, send, recv, grads, theirs, after)
    sums = [add_half(f"rs_add_pair{tag}_{n}", g, t, c) for n, g, t in zip(names, grads, theirs)]
    send, recv, sums, lands, token = rs_chips_start(tag, sums)
    return (tag, names, send, recv, sums, lands), token[0, 0]


def reduce_scatter_finish(state, after):
    me = 2 * lax.axis_index("x") + lax.axis_index("y")
    tag, names, send, recv, sums, lands = state
    sums, landed = rs_chips_wait(tag, send, recv, sums, lands, after)
    halves = []
    for n, s, v in zip(names, sums, landed):
        own = lax.dynamic_slice_in_dim(s, me * WIN_SHARD, WIN_SHARD, axis=1) if s.ndim == 2 else \
            lax.dynamic_index_in_dim(s, me, 0, keepdims=False)
        halves.append(add_chips(f"rs_add_chips{tag}_{n}", v, own))
    send, recv, halves, others, _ = rs_pair_start("_join" + tag, halves, halves=False)
    return tag, names, send, recv, halves, others


def reduce_scatter_join(state, after):
    tag, names, send, recv, halves, others = state
    halves, others = rs_pair_wait("_join" + tag, send, recv, halves, others, after, halves=False)
    return dict(zip(names, zip(halves, others)))


NDEV = 8


def _small_copy(r, src, dst, send, recv, x, y, c):
    return _rcopy(src, dst, send.at[r - 1], recv.at[r - 1], (x ^ (r >> 2), y ^ ((r >> 1) & 1), c ^ (r & 1)))


def small_start(pack):
    def body(p, land, send, recv, p_thru, land_thru, token):
        x, y, c, _ = _place()
        me = 4 * x + 2 * y + c
        for r in range(1, NDEV):
            _start_pieces(p, land.at[me], lambda s, d, r=r: _small_copy(r, s, d, send, recv, x, y, c), 128 << 10)
        token[...] = jnp.zeros_like(token)

    ops = [pltpu.with_memory_space_constraint(v, pltpu.HBM) for v in (pack, lax.empty((NDEV,) + pack.shape, f32))]
    sem = pltpu.SemaphoreType.DMA((NDEV - 1,))
    return pl.pallas_call(
        body, name="small_start", in_specs=[HBM, HBM],
        out_specs=[SEMS, SEMS, HBM, HBM, pl.BlockSpec(memory_space=pltpu.VMEM)],
        out_shape=[sem, sem] + [pltpu.HBM(v.shape, v.dtype) for v in ops] + [jax.ShapeDtypeStruct((8, LANES), f32)],
        input_output_aliases={0: 2, 1: 3}, compiler_params=pltpu.CompilerParams(has_side_effects=EFFECT),
    )(*ops)


def small_wait(send, recv, pack, land, after):
    def body(p, land_r, send_r, recv_r, *rest):
        x, y, c, _ = _place()
        me = 4 * x + 2 * y + c
        for r in range(1, NDEV):
            _small_copy(r, p, land_r.at[me], send_r, recv_r, x, y, c).wait_send()
            src = 4 * (x ^ (r >> 2)) + 2 * (y ^ ((r >> 1) & 1)) + (c ^ (r & 1))
            _small_copy(r, p, land_r.at[src], send_r, recv_r, x, y, c).wait_recv()

    return pl.pallas_call(
        body, name="small_wait", in_specs=[HBM, HBM, SEMS, SEMS] + [ANY] * len(after), out_specs=[HBM, HBM],
        out_shape=[pltpu.HBM(pack.shape, f32), pltpu.HBM(land.shape, f32)], input_output_aliases={0: 0, 1: 1},
        compiler_params=pltpu.CompilerParams(has_side_effects=EFFECT),
    )(pack, land, send, recv, *after)


def small_sum(land, pack):
    def body(land_r, p_r, o_r):
        me = 4 * lax.axis_index("x") + 2 * lax.axis_index("y") + lax.axis_index("c")
        for k in range(NDEV):
            @pl.when(me == k)
            def _():
                acc = None
                for d in range(NDEV):
                    t = p_r[...] if d == k else land_r[d]
                    acc = t if acc is None else acc + t
                o_r[...] = acc

    vm = pl.BlockSpec(memory_space=pltpu.VMEM)
    return pl.pallas_call(
        body, name="small_sum", in_specs=[vm, vm], out_specs=vm, out_shape=jax.ShapeDtypeStruct(pack.shape, f32),
        compiler_params=pltpu.CompilerParams(vmem_limit_bytes=40 << 20),
    )(land, pack)


def _adamw_math(w, g, m, v):
    m = ADAM_B1 * m + (1.0 - ADAM_B1) * g
    v = ADAM_B2 * v + (1.0 - ADAM_B2) * (g * g)
    m_hat = m / (1.0 - ADAM_B1 ** ADAM_STEP)
    v_hat = v / (1.0 - ADAM_B2 ** ADAM_STEP)
    return -ADAM_LR * (m_hat / (jnp.sqrt(v_hat) + ADAM_EPS) + ADAM_WD * w), m, v


def adamw_big(name, halves, w, m, v):
    _, R, C = w.shape
    tr = _row_tile(R // 2, C, target=1 << 20)
    nt = R // 2 // tr

    def body(a0, b0, a1, b1, w_r, m_r, v_r, g_o, d_o, m_o, v_o):
        mine = pl.program_id(1) == lax.axis_index("c")
        g = jnp.where(pl.program_id(0) == 0, jnp.where(mine, a0[...], b0[...]), jnp.where(mine, a1[...], b1[...]))
        g_o[...] = g
        d_o[...], m_o[...], v_o[...] = _adamw_math(w_r[...], g, m_r[...], v_r[...])

    stk = pl.BlockSpec((None, tr, C), lambda l, h, i: (l, h * nt + i, 0))
    lay0 = pl.BlockSpec((tr, C), lambda l, h, i: (jnp.where(l == 0, i, nt - 1), 0))
    lay1 = pl.BlockSpec((tr, C), lambda l, h, i: (jnp.where(l == 0, 0, i), 0))
    return pl.pallas_call(
        body, name=name, grid=(DEPTH, 2, nt),
        in_specs=[lay0, lay0, lay1, lay1, stk, stk, stk],
        out_specs=[stk] * 4, out_shape=[jax.ShapeDtypeStruct(w.shape, f32)] * 4,
        compiler_params=_cp(("arbitrary", "arbitrary", "arbitrary")),
    )(*halves[0], *halves[1], w, m, v)


def adamw_small(name, g, w, m, v):
    def body(g_r, w_r, m_r, v_r, d_o, m_o, v_o):
        d_o[...], m_o[...], v_o[...] = _adamw_math(w_r[...], g_r[...], m_r[...], v_r[...])

    return pl.pallas_call(body, name=name, out_shape=[jax.ShapeDtypeStruct(w.shape, f32)] * 3)(g, w, m, v)


WEIGHTS = ("w_in", "conv_w", "gmlp_ln_g", "gmlp_ln_b", "w_s", "b_s", "p_a", "p_b", "p_c", "w_o", "ln1_g", "ln1_b",
           "w_gate", "w_up", "w_down", "ln2_g", "ln2_b")
VECS = ("ln1_g", "ln1_b", "ln2_g", "ln2_b", "gmlp_ln_g", "gmlp_ln_b")
ROWS_VEC, ROWS_BS, ROWS_WS, ROWS_CONV = D // LANES, 8, 8 * BLK, 3 * D // LANES
ROWS_LAYER = len(VECS) * ROWS_VEC + ROWS_BS + ROWS_WS + ROWS_CONV


def _pack_small(per_layer, tail):
    parts = []
    for P in per_layer:
        parts += [P[n].reshape(ROWS_VEC, LANES) for n in VECS]
        parts += [P["b_s"].reshape(ROWS_BS, LANES), P["w_s"].reshape(ROWS_WS, LANES), P["conv_w"].reshape(ROWS_CONV, LANES)]
    return jnp.concatenate(parts + [tail], axis=0)


def _unpack_small(pack):
    out = []
    for l in range(DEPTH):
        r = l * ROWS_LAYER
        P = {}
        for n in VECS:
            P[n] = pack[r:r + ROWS_VEC].reshape(D)
            r += ROWS_VEC
        P["b_s"] = pack[r:r + ROWS_BS].reshape(8, BLK)
        r += ROWS_BS
        P["w_s"] = pack[r:r + ROWS_WS].reshape(8, BLK, BLK)
        r += ROWS_WS
        P["conv_w"] = pack[r:r + ROWS_CONV].reshape(3, D)
        out.append(P)
    return out, pack[DEPTH * ROWS_LAYER:]


def kernel(x, positions, w_in, conv_w, gmlp_ln_g, gmlp_ln_b, w_s, b_s, p_a, p_b, p_c, w_o, ln1_g, ln1_b, w_gate, w_up, w_down, ln2_g, ln2_b, loss_target, m_w_in, m_conv_w, m_gmlp_ln_g, m_gmlp_ln_b, m_w_s, m_b_s, m_p_a, m_p_b, m_p_c, m_w_o, m_ln1_g, m_ln1_b, m_w_gate, m_w_up, m_w_down, m_ln2_g, m_ln2_b, v_w_in, v_conv_w, v_gmlp_ln_g, v_gmlp_ln_b, v_w_s, v_b_s, v_p_a, v_p_b, v_p_c, v_w_o, v_ln1_g, v_ln1_b, v_w_gate, v_w_up, v_w_down, v_ln2_g, v_ln2_b):
    Wt = dict(w_in=w_in, conv_w=conv_w, gmlp_ln_g=gmlp_ln_g, gmlp_ln_b=gmlp_ln_b, w_s=w_s, b_s=b_s, p_a=p_a, p_b=p_b,
              p_c=p_c, w_o=w_o, ln1_g=ln1_g, ln1_b=ln1_b, w_gate=w_gate, w_up=w_up, w_down=w_down, ln2_g=ln2_g, ln2_b=ln2_b)
    Mt = dict(w_in=m_w_in, conv_w=m_conv_w, gmlp_ln_g=m_gmlp_ln_g, gmlp_ln_b=m_gmlp_ln_b, w_s=m_w_s, b_s=m_b_s, p_a=m_p_a,
              p_b=m_p_b, p_c=m_p_c, w_o=m_w_o, ln1_g=m_ln1_g, ln1_b=m_ln1_b, w_gate=m_w_gate, w_up=m_w_up,
              w_down=m_w_down, ln2_g=m_ln2_g, ln2_b=m_ln2_b)
    Vt = dict(w_in=v_w_in, conv_w=v_conv_w, gmlp_ln_g=v_gmlp_ln_g, gmlp_ln_b=v_gmlp_ln_b, w_s=v_w_s, b_s=v_b_s, p_a=v_p_a,
              p_b=v_p_b, p_c=v_p_c, w_o=v_w_o, ln1_g=v_ln1_g, ln1_b=v_ln1_b, w_gate=v_w_gate, w_up=v_w_up,
              w_down=v_w_down, ln2_g=v_ln2_g, ln2_b=v_ln2_b)
    chip = 2 * lax.axis_index("x") + lax.axis_index("y")
    cw = D // NCHIP

    def gathered_weights(names, arrays):
        Wl = dict(zip(names, arrays))
        for n in ("p_a", "p_c", "w_o"):
            Wl[n] = Wl[n].reshape(D, D)
        return Wl

    def small_weights(l, conv_all):
        Wl = {n: Wt[n][l] for n in VECS + ("w_s", "b_s")}
        Wl["conv_w"] = conv_all[:, l].transpose(1, 0, 2).reshape(3, D)
        return Wl

    w_in0, conv_all = gather_halves([Wt["w_in"][0].astype(MX).reshape(2, D // 2, WIN_SHARD), conv_w])
    rest = BIG[1:]
    *late0, coming0 = gather_start("0", [Wt[n][0].astype(MX) for n in rest], [conv_all])
    *late1, coming1 = gather_start("1", [Wt[n][1].astype(MX) for n in BIG], [conv_all, coming0])
    W0 = dict(small_weights(0, conv_all), w_in=w_in0.reshape(D, NIN), after=coming1[0, 0],
              late=lambda y: gathered_weights(rest, gather_wait("0", *late0, [y])))

    def W1(h):
        return dict(small_weights(1, conv_all), **gathered_weights(BIG, gather_wait("1", *late1, [h])))

    layers = [W0, W1]

    rs_state, rs_started, held = {}, {}, {}

    def start_exchange(l, g):
        if "loss" in g:
            held[l] = g
            return None
        if "conv_w" in g:
            held[l] = g
            rs_state[(l, False)], started = reduce_scatter_chips(rs_state[(l, False)], [g["w_s"], g["conv_w"]])
            if l == 0:
                pack = _pack_small([held[j] for j in range(DEPTH)], held[DEPTH]["loss"])
                *held["small"], token = small_start(pack)
                started = started + token[0, 0]
            return started
        if "dx" in g:
            rs_state[(l, True)], rs_started[(l, True)] = reduce_scatter_chips(rs_state[(l, True)], [g["dx"]])
            return rs_started[(l, True)]
        key = (l, "w_in" in g)
        rs_state[key], started = reduce_scatter_pair(f"{l}{'b' if key[1] else 'a'}", g)
        return started

    _, grad_x, _ = local_step(x[0], positions[0], loss_target[0], layers, start_exchange)

    last = jnp.zeros((8, LANES), f32) + rs_started[(0, True)]
    behind = [grad_x, last]
    red = [dict() for _ in range(DEPTH)]
    swaps = {key: reduce_scatter_finish(rs_state[key], behind) for key in ((1, False), (1, True), (0, False))}
    small, tail = _unpack_small(small_sum(*reversed(small_wait(*held["small"], behind))))
    loss = tail[0, 0]

    G, DW, NM, NV = {}, {}, {}, {}
    zc = jnp.zeros((3, D), f32)
    wp = _pack_small([{**{n: Wt[n][l] for n in VECS + ("b_s", "w_s")}, "conv_w": zc} for l in range(DEPTH)], jnp.zeros((8, LANES), f32))
    mp = _pack_small([{**{n: Mt[n][l] for n in VECS + ("b_s", "w_s")}, "conv_w": zc} for l in range(DEPTH)], jnp.zeros((8, LANES), f32))
    vp = _pack_small([{**{n: Vt[n][l] for n in VECS + ("b_s", "w_s")}, "conv_w": zc} for l in range(DEPTH)], jnp.ones((8, LANES), f32))
    gp = _pack_small(small, jnp.zeros((8, LANES), f32))
    outs = [_unpack_small(a)[0] for a in adamw_small("adamw_small", gp, wp, mp, vp)]
    for n in VECS + ("b_s", "w_s"):
        G[n] = jnp.stack([small[l][n] for l in range(DEPTH)])
        DW[n], NM[n], NV[n] = (jnp.stack([o[l][n] for l in range(DEPTH)]) for o in outs)
    gconv = jnp.stack([lax.dynamic_slice(small[l]["conv_w"], (0, chip * cw), (3, cw)) for l in range(DEPTH)])
    G["conv_w"] = gconv
    flat = lambda a: a.reshape(DEPTH * 3, cw)
    d, m2, v2 = adamw_small("adamw_conv", flat(gconv), flat(conv_w), flat(m_conv_w), flat(v_conv_w))
    DW["conv_w"], NM["conv_w"], NV["conv_w"] = (a.reshape(DEPTH, 3, cw) for a in (d, m2, v2))

    for key in swaps:
        red[key[0]].update(reduce_scatter_join(swaps[key], [d, DW["ln2_b"]]))
    updated = {}
    for n in BIG[1:]:
        tr = (lambda a: jnp.swapaxes(a, 1, 2)) if n in ("w_gate", "w_up") else (lambda a: a)
        updated[n] = adamw_big("adamw_" + n, (red[0][n], red[1][n]), tr(Wt[n]), tr(Mt[n]), tr(Vt[n]))
        G[n], DW[n], NM[n], NV[n] = map(tr, updated[n])
    done = [d, DW["ln2_b"], red[1]["w_in"][1]] + [updated[n][1] for n in BIG[1:]]
    red[0].update(reduce_scatter_join(reduce_scatter_finish(rs_state[(0, True)], done), [updated["w_o"][1]]))
    G["w_in"], DW["w_in"], NM["w_in"], NV["w_in"] = adamw_big(
        "adamw_w_in", (red[0]["w_in"], red[1]["w_in"]), Wt["w_in"], Mt["w_in"], Vt["w_in"])

    return (loss, grad_x[None], *[G[n] for n in WEIGHTS], *[DW[n] for n in WEIGHTS], *[NM[n] for n in WEIGHTS],
            *[NV[n] for n in WEIGHTS])
```

```python
import functools
import math

import jax
import jax.numpy as jnp
from jax import lax
from jax.experimental import pallas as pl
from jax.experimental.pallas import tpu as pltpu

D = 1024
NIN = 12800
DFF = 2816
NCHIP = 4
FB = DFF // NCHIP
WIN_SHARD = NIN // NCHIP
DEPTH = 2
GROUPS = ((128, 1), (512, 4), (2048, 16))
HD = 64
BLK = 128
AO = 512
ALPHA = (2 * DEPTH) ** 0.25
EPS = 1e-5
ROPE_THETA = 10000.0
LANES = 128
NEG = -1e30

C_GATES, C_BCH, C_QKV, C_UV = 0, 3 * D, 6 * D, 6 * D + 9 * AO

MX = jnp.bfloat16
ACT = jnp.bfloat16

ADAM_LR, ADAM_B1, ADAM_B2, ADAM_EPS, ADAM_WD, ADAM_STEP = 0.001, 0.9, 0.999, 1e-08, 0.01, 10

f32 = jnp.float32
NT = (((1,), (1,)), ((), ()))
TN = (((0,), (0,)), ((), ()))


def _cp(sem, vmem_mb=48):
    return pltpu.CompilerParams(dimension_semantics=sem, vmem_limit_bytes=vmem_mb << 20)


def _dot(a, b, dims=None):
    if dims is None:
        return jnp.dot(a, b, preferred_element_type=f32)
    return lax.dot_general(a, b, dims, preferred_element_type=f32)


def _ln_stats(r):
    mu = jnp.mean(r, axis=-1, keepdims=True)
    xc = r - mu
    var = jnp.mean(xc * xc, axis=-1, keepdims=True)
    rstd = lax.rsqrt(var + EPS)
    return xc * rstd, rstd


def _ln_bwd(dy, xhat, rstd, g):
    dxh = dy * g
    return rstd * (dxh - jnp.mean(dxh, axis=-1, keepdims=True) - xhat * jnp.mean(dxh * xhat, axis=-1, keepdims=True))


def _gelu(x):
    return 0.5 * x * (1.0 + lax.erf(x * (1.0 / math.sqrt(2.0))))


def _gelu_and_grad(x):
    cdf = 0.5 * (1.0 + lax.erf(x * (1.0 / math.sqrt(2.0))))
    return x * cdf, cdf + x * jnp.exp(-0.5 * x * x) * (1.0 / math.sqrt(2.0 * math.pi))


def _sigmoid(x):
    return 0.5 * jnp.tanh(0.5 * x) + 0.5


def _acc_rows(o_ref, first, val):
    @pl.when(first)
    def _():
        o_ref[...] = jnp.zeros_like(o_ref)
    o_ref[...] += jnp.sum(val, axis=0, keepdims=True)


def mm_in(x, w, bias):
    T = x.shape[0]
    tm, tn = min(2048, T), 1280

    def body(x_ref, w_ref, b_ref, o_ref, xb):
        @pl.when(pl.program_id(1) == 0)
        def _():
            xb[...] = x_ref[...].astype(MX)
        o_ref[...] = (_dot(xb[...], w_ref[...]) + b_ref[...]).astype(o_ref.dtype)

    return pl.pallas_call(
        body, name="mm_in", grid=(T // tm, NIN // tn),
        in_specs=[pl.BlockSpec((tm, D), lambda i, j: (i, 0), pipeline_mode=pl.Buffered(1)),
                  pl.BlockSpec((D, tn), lambda i, j: (0, j)), pl.BlockSpec((1, tn), lambda i, j: (0, j))],
        out_specs=pl.BlockSpec((tm, tn), lambda i, j: (i, j)),
        out_shape=jax.ShapeDtypeStruct((T, NIN), ACT),
        scratch_shapes=[pltpu.VMEM((tm, D), MX)],
        compiler_params=_cp(("parallel", "arbitrary")),
    )(x, w, bias)


HALO = 16
TM_AC = 512


def _uv_specs():
    return [pl.BlockSpec((TM_AC, 512), functools.partial(lambda i, j: (i, j), j=C_UV // 512 + j)) for j in range(4)]


def _gmlp_fwd(up, vp, ws_ref, bs_ref, lg, lb, u=None, gv=None):
    u = _gelu(up) if u is None else u
    xhat, rstd = _ln_stats(_gelu(vp) if gv is None else gv)
    vn = xhat * lg + lb
    vnb = vn.astype(MX)
    rows = []
    for c in range(up.shape[0] // BLK):
        r = slice(c * BLK, (c + 1) * BLK)
        rows.append(jnp.concatenate(
            [_dot(ws_ref[g], vnb[r, g * BLK:(g + 1) * BLK]) + bs_ref[g] for g in range(8)], axis=1))
    return u, vn, xhat, rstd, jnp.concatenate(rows, axis=0)


def mix_ac_fwd(proj, conv_w, wst, bsx, lg, lb):
    T = proj.shape[0]
    tm = TM_AC

    def body(bch, halo, u0, u1, v0, v1, cw, ws, bs, lg_ref, lb_ref, ya, yc, zs):
        i = pl.program_id(0)
        pb = bch[...].astype(f32)
        z = pb[:, D:2 * D] * pb[:, 2 * D:]
        hz = halo[:, :D].astype(f32) * halo[:, D:].astype(f32)
        zs[0:HALO, :] = jnp.where(i > 0, hz, 0.0)
        zs[HALO:HALO + tm, :] = z
        cv = cw[0:1, :] * zs[HALO - 2:HALO - 2 + tm, :] + cw[1:2, :] * zs[HALO - 1:HALO - 1 + tm, :] + cw[2:3, :] * z
        ya[...] = (pb[:, :D] * cv).astype(ya.dtype)
        up = jnp.concatenate([u0[...], u1[...]], axis=1).astype(f32)
        vp = jnp.concatenate([v0[...], v1[...]], axis=1).astype(f32)
        u, _, _, _, sp = _gmlp_fwd(up, vp, ws, bs, lg_ref[...], lb_ref[...])
        yc[...] = (u * sp).astype(yc.dtype)

    full = lambda shape: pl.BlockSpec(shape, lambda i: (0,) * len(shape))
    return pl.pallas_call(
        body, name="mix_ac_fwd", grid=(T // tm,),
        in_specs=[pl.BlockSpec((tm, 3 * D), lambda i: (i, 1)),
                  pl.BlockSpec((HALO, 2 * D), lambda i: (jnp.maximum(i * (tm // HALO) - 1, 0), 2)),
                  *_uv_specs(), full((3, D)), full((8, BLK, BLK)), full((8, BLK, BLK)), full((1, D)), full((1, D))],
        out_specs=[pl.BlockSpec((tm, D), lambda i: (i, 0))] * 2,
        out_shape=[jax.ShapeDtypeStruct((T, D), MX)] * 2,
        scratch_shapes=[pltpu.VMEM((HALO + tm, D), f32)],
        compiler_params=_cp(("parallel",)),
    )(proj, proj, proj, proj, proj, proj, conv_w, wst, bsx, lg, lb)


def _swap_halves(x):
    lane = lax.broadcasted_iota(jnp.int32, x.shape, 1)
    return jnp.where((lane % HD) < HD // 2, pltpu.roll(x, x.shape[1] - HD // 2, 1), pltpu.roll(x, HD // 2, 1))


def _tile4(t):
    return jnp.concatenate([t] * (AO // LANES), axis=1)


TM_FOLD = 1024


def _fold_out(nat, x, out_ref, d):
    if d == 1:
        out_ref[0] = x.astype(out_ref.dtype)
        return
    rows = x.shape[0] // d
    for j in range(AO // LANES):
        nat[j] = x[:, j * LANES:(j + 1) * LANES]
    for r in range(d):
        out_ref[r] = jnp.concatenate(
            [nat.at[j][pl.ds(r, rows, stride=d), :] for j in range(AO // LANES)], axis=1).astype(out_ref.dtype)


def _unfold_in(nat, in_ref, d):
    if d == 1:
        return in_ref[0].astype(f32)
    rows = in_ref.shape[1]
    for r in range(d):
        v = in_ref[r].astype(f32)
        for j in range(AO // LANES):
            nat.at[j][pl.ds(r, rows, stride=d), :] = v[:, j * LANES:(j + 1) * LANES]
    return jnp.concatenate([nat[j] for j in range(AO // LANES)], axis=1)


def fold_rope(proj, cos_t, sin_t, g, d):
    T = proj.shape[0]
    tm = TM_FOLD
    rows = tm // d

    def body(x_ref, c_ref, s_ref, q_o, k_o, v_o, nat):
        cos, sin = _tile4(c_ref[...]), _tile4(s_ref[...])
        for part, out, scale in ((0, q_o, HD ** -0.5), (1, k_o, 1.0), (2, v_o, None)):
            x = x_ref[:, part * AO:(part + 1) * AO].astype(f32)
            if scale is not None:
                x = (x * cos + _swap_halves(x) * sin) * scale
            _fold_out(nat, x, out, d)

    fold_spec = pl.BlockSpec((d, rows, AO), lambda i: (0, i, 0))
    return pl.pallas_call(
        body, name=f"fold_rope{g}", grid=(T // tm,),
        in_specs=[pl.BlockSpec((tm, 3 * AO), lambda i: (i, C_QKV // (3 * AO) + g)),
                  pl.BlockSpec((tm, LANES), lambda i: (i, 0)), pl.BlockSpec((tm, LANES), lambda i: (i, 0))],
        out_specs=[fold_spec] * 3,
        out_shape=[jax.ShapeDtypeStruct((d, T // d, AO), MX)] * 3,
        scratch_shapes=[pltpu.VMEM((AO // LANES, tm, LANES), f32)],
        compiler_params=_cp(("parallel",)),
    )(proj, cos_t, sin_t)


def _stack_heads(x):
    lane = lax.broadcasted_iota(jnp.int32, x.shape, 1)
    z = jnp.zeros_like(x)
    return jnp.concatenate([jnp.where(lane < HD, x, z), jnp.where(lane >= HD, x, z)], axis=0)


def _unstack_heads(y):
    lane = lax.broadcasted_iota(jnp.int32, (BLK, LANES), 1)
    return jnp.where(lane < HD, y[:BLK], y[BLK:])


def _window_masks():
    row = lax.broadcasted_iota(jnp.int32, (2 * BLK, 2 * BLK), 0) % BLK
    col = lax.broadcasted_iota(jnp.int32, (2 * BLK, 2 * BLK), 1)
    return (col < BLK) & (col >= row), (col >= BLK) & (col - BLK <= row)


def _two_blocks(ref, b):
    r0 = pl.multiple_of(b * BLK, BLK)
    rp = pl.multiple_of(jnp.maximum(b - 1, 0) * BLK, BLK)
    return jnp.concatenate([ref[pl.ds(rp, BLK), :], ref[pl.ds(r0, BLK), :]], axis=0)


def _merge_masks():
    row = lax.broadcasted_iota(jnp.int32, (2 * BLK, BLK), 0) % BLK
    col = lax.broadcasted_iota(jnp.int32, (2 * BLK, BLK), 1)
    return col <= row, col == row


def attn_fwd(qf, kf, vf, g, nb):
    T = qf.shape[0]

    def body(q_ref, k_ref, v_ref, o_ref, l_ref):
        cur_m, own_m = _merge_masks()

        def step(b, carry):
            r0 = pl.multiple_of(b * BLK, BLK)
            rp = pl.multiple_of(jnp.maximum(b - 1, 0) * BLK, BLK)
            qs = _stack_heads(q_ref[pl.ds(r0, BLK), :])
            vc, vp = v_ref[pl.ds(r0, BLK), :], v_ref[pl.ds(rp, BLK), :]
            sp = jnp.where((b % nb) != 0, _dot(qs, k_ref[pl.ds(rp, BLK), :], NT), NEG)
            s = jnp.where(cur_m, _dot(qs, k_ref[pl.ds(r0, BLK), :], NT), sp)
            s_own = jnp.sum(jnp.where(own_m, sp, 0.0), axis=-1, keepdims=True)
            m = jnp.maximum(jnp.max(s, axis=-1, keepdims=True), s_own)
            p, p_own = jnp.exp(s - m), jnp.exp(s_own - m)
            l = jnp.sum(p, axis=-1, keepdims=True) + p_own
            pb = p.astype(MX)
            zero = jnp.zeros_like(pb)
            o = _dot(jnp.where(cur_m, pb, zero), vc) + _dot(jnp.where(cur_m, zero, pb), vp)
            o = (o + p_own * jnp.concatenate([vp, vp], axis=0).astype(f32)) / l
            o_ref[pl.ds(r0, BLK), :] = _unstack_heads(o).astype(o_ref.dtype)
            l_ref[pl.ds(r0, BLK), :] = _unstack_heads(jnp.broadcast_to(m + jnp.log(l), (2 * BLK, LANES)))
            return carry

        lax.fori_loop(0, T // BLK, step, 0, unroll=8)

    spec = pl.BlockSpec((T, LANES), lambda j: (0, j))
    return pl.pallas_call(
        body, name=f"attn_fwd{g}", grid=(AO // LANES,),
        in_specs=[spec] * 3, out_specs=[spec] * 2,
        out_shape=[jax.ShapeDtypeStruct((T, AO), ACT), jax.ShapeDtypeStruct((T, AO), f32)],
        compiler_params=_cp(("parallel",), 56),
    )(qf, kf, vf)


def _group_weights(lses):
    m = jnp.maximum(jnp.maximum(lses[0], lses[1]), lses[2])
    e = [jnp.exp(l - m) for l in lses]
    inv = 1.0 / (e[0] + e[1] + e[2])
    return [x * inv for x in e]


def _fold_specs(T, tm):
    specs = []
    for _, d in GROUPS:
        specs.append(pl.BlockSpec((d, tm // d, AO), lambda i: (0, i, 0)))
    return specs


def combine_fwd(os_, lses):
    T = os_[0].shape[0] * os_[0].shape[1]
    tm = TM_FOLD

    def body(o0, o1, o2, l0, l1, l2, y_ref, nat):
        o = [_unfold_in(nat, r, d) for r, (_, d) in zip((o0, o1, o2), GROUPS)]
        ls = [_unfold_in(nat, r, d) for r, (_, d) in zip((l0, l1, l2), GROUPS)]
        w = _group_weights(ls)
        y_ref[...] = (w[0] * o[0] + w[1] * o[1] + w[2] * o[2]).astype(y_ref.dtype)

    specs = _fold_specs(T, tm)
    return pl.pallas_call(
        body, name="combine_fwd", grid=(T // tm,),
        in_specs=specs + specs, out_specs=pl.BlockSpec((tm, AO), lambda i: (i, 0)),
        out_shape=jax.ShapeDtypeStruct((T, AO), MX),
        scratch_shapes=[pltpu.VMEM((AO // LANES, tm, LANES), f32)],
        compiler_params=_cp(("parallel",)),
    )(*os_, *lses)


TM_MIX = 512


def mix_out_fwd(proj, ya, yb, yc, x0, pa, pb, pc, wo, g1, b1):
    T = x0.shape[0]
    tm = min(TM_MIX, T)

    def body(gt, ya_r, yb_r, yc_r, x0_r, pa_r, pb_r, pc_r, wo_r, g_r, b_r, mabc, m_o, r1_o, x1_o, x1b_o):
        ma = _dot(ya_r[...], pa_r[...])
        ybv = yb_r[...]
        mb = jnp.concatenate([_dot(ybv, pb_r[k]) for k in range(NCHIP)], axis=1)
        mc = _dot(yc_r[...], pc_r[...])
        m = jnp.zeros((tm, D), f32)
        for j, mm in enumerate((ma, mb, mc)):
            mabc[:, j * D:(j + 1) * D] = mm.astype(mabc.dtype)
            m = m + _sigmoid(gt[:, j * D:(j + 1) * D].astype(f32)) * mm
        mb16 = m.astype(MX)
        m_o[...] = mb16
        r1 = ALPHA * x0_r[...] + _dot(mb16, wo_r[...])
        r1_o[...] = r1
        xhat, _ = _ln_stats(r1)
        x1 = xhat * g_r[...] + b_r[...]
        x1_o[...] = x1
        x1b_o[...] = x1.astype(MX)

    full = lambda shape: pl.BlockSpec(shape, lambda i: (0,) * len(shape), pipeline_mode=pl.Buffered(1))
    tile = lambda w: pl.BlockSpec((tm, w), lambda i: (i, 0))
    return pl.pallas_call(
        body, name="mix_out_fwd", grid=(T // tm,),
        in_specs=[tile(3 * D), tile(D), tile(AO), tile(D), tile(D), full((D, D)), full((NCHIP, AO, D // NCHIP)),
                  full((D, D)), full((D, D)), full((1, D)), full((1, D))],
        out_specs=[tile(3 * D), tile(D), tile(D), tile(D), tile(D)],
        out_shape=[jax.ShapeDtypeStruct((T, 3 * D), MX), jax.ShapeDtypeStruct((T, D), MX),
                   jax.ShapeDtypeStruct((T, D), f32), jax.ShapeDtypeStruct((T, D), f32), jax.ShapeDtypeStruct((T, D), MX)],
        compiler_params=_cp(("parallel",), 56),
    )(proj, ya, yb, yc, x0, pa, pb, pc, wo, g1, b1)


TM_FF = 512
TM_FFB = 512
ROW_CHUNK = 64


TM_FFW = 512


def ffn_fwd(x1b, x1, wg, wu, wd, g2, b2):
    T = x1.shape[0]
    tm = min(TM_FFW, T)

    def body(xb_r, x_r, wg_r, wu_r, wd_r, g_r, b_r, g_o, u_o, r2_o, x2_o, gs, us, hs):
        xb = xb_r[...]
        r2 = ALPHA * x_r[...]
        for k in range(NCHIP):
            gs[...] = _dot(xb, wg_r[k])
            us[...] = _dot(xb, wu_r[k])
            for r in range(0, tm, ROW_CHUNK):
                rows = pl.ds(r, ROW_CHUNK)
                gate, up = gs[rows, :], us[rows, :]
                g_o[k, rows, :] = gate.astype(g_o.dtype)
                u_o[k, rows, :] = up.astype(u_o.dtype)
                hs[rows, :] = (gate * _sigmoid(gate) * up).astype(hs.dtype)
            r2 = r2 + _dot(hs[...], wd_r[k])
        r2_o[...] = r2
        xhat, _ = _ln_stats(r2)
        x2_o[...] = xhat * g_r[...] + b_r[...]

    once = dict(pipeline_mode=pl.Buffered(1))
    wspec = pl.BlockSpec((NCHIP, D, FB), lambda i: (0, 0, 0), **once)
    ospec = pl.BlockSpec((NCHIP, tm, FB), lambda i: (0, i, 0))
    tile = pl.BlockSpec((tm, D), lambda i: (i, 0))
    vec = pl.BlockSpec((1, D), lambda i: (0, 0))
    return pl.pallas_call(
        body, name="ffn_fwd", grid=(T // tm,),
        in_specs=[tile, tile, wspec, wspec, pl.BlockSpec((NCHIP, FB, D), lambda i: (0, 0, 0), **once), vec, vec],
        out_specs=[ospec] * 2 + [tile, tile],
        out_shape=[jax.ShapeDtypeStruct((NCHIP, T, FB), ACT)] * 2 + [jax.ShapeDtypeStruct((T, D), f32)] * 2,
        scratch_shapes=[pltpu.VMEM((tm, FB), f32)] * 2 + [pltpu.VMEM((tm, FB), MX)],
        compiler_params=_cp(("parallel",), 56),
    )(x1b, x1, wg, wu, wd, g2, b2)


def loss_grad(y, tgt):
    T = y.shape[0]
    tm = min(512, T)

    def body(y_r, t_r, l_o, dy_o):
        e = y_r[...] - t_r[...]
        dy_o[...] = e * (1.0 / D)

        @pl.when(pl.program_id(0) == 0)
        def _():
            l_o[...] = jnp.zeros_like(l_o)
        l_o[...] += (0.5 / D) * jnp.sum(e * e)

    tile = pl.BlockSpec((tm, D), lambda i: (i, 0))
    return pl.pallas_call(
        body, name="loss_grad", grid=(T // tm,),
        in_specs=[tile, tile], out_specs=[pl.BlockSpec((8, LANES), lambda i: (0, 0)), tile],
        out_shape=[jax.ShapeDtypeStruct((8, LANES), f32), jax.ShapeDtypeStruct((T, D), f32)],
        compiler_params=_cp(("arbitrary",)),
    )(y, tgt)


def ffn_down_bwd(dx2, r2, g2, wd, gate, up):
    T = dx2.shape[0]
    tm = min(TM_FFB, T)

    def body(dx_r, r_r, g_r, w_r, ga_r, up_r, dr_o, drb_o, dg_o, du_o, hh_o, dlg_o, dlb_o, hs):
        i = pl.program_id(0)
        xhat, rstd = _ln_stats(r_r[...])
        dx = dx_r[...]
        _acc_rows(dlg_o, i == 0, dx * xhat)
        _acc_rows(dlb_o, i == 0, dx)
        dr = _ln_bwd(dx, xhat, rstd, g_r[...])
        dr_o[...] = dr
        drb = dr.astype(MX)
        drb_o[...] = drb
        for k in range(NCHIP):
            hs[...] = _dot(drb, w_r[k], NT)
            for r in range(0, tm, ROW_CHUNK):
                rows = pl.ds(r, ROW_CHUNK)
                dhh, gate_v, up_v = hs[rows, :], ga_r[k, rows, :].astype(f32), up_r[k, rows, :].astype(f32)
                sg = _sigmoid(gate_v)
                dg_o[k, rows, :] = (dhh * up_v * sg * (1.0 + gate_v * (1.0 - sg))).astype(dg_o.dtype)
                silu = gate_v * sg
                du_o[k, rows, :] = (dhh * silu).astype(du_o.dtype)
                hh_o[k, rows, :] = (silu * up_v).astype(hh_o.dtype)

    tile = pl.BlockSpec((tm, D), lambda i: (i, 0))
    vec = pl.BlockSpec((1, D), lambda i: (0, 0))
    blk = pl.BlockSpec((NCHIP, tm, FB), lambda i: (0, i, 0))
    return pl.pallas_call(
        body, name="ffn_down_bwd", grid=(T // tm,),
        in_specs=[tile, tile, vec, pl.BlockSpec((NCHIP, FB, D), lambda i: (0, 0, 0), pipeline_mode=pl.Buffered(1)), blk, blk],
        out_specs=[tile, tile, blk, blk, blk, vec, vec],
        out_shape=[jax.ShapeDtypeStruct((T, D), f32), jax.ShapeDtypeStruct((T, D), MX)]
        + [jax.ShapeDtypeStruct((NCHIP, T, FB), MX)] * 3 + [jax.ShapeDtypeStruct((1, D), f32)] * 2,
        scratch_shapes=[pltpu.VMEM((tm, FB), f32)],
        compiler_params=_cp(("arbitrary",), 58),
    )(dx2, r2, g2, wd, gate, up)


def ffn_up_bwd(dr2, dgate, dup, wg, wu, r1, g1):
    T = dr2.shape[0]
    tm = min(TM_FFB, T)

    def body(dr2_r, dg_r, du_r, wg_r, wu_r, r1_r, g_r, dr1_o, dr1b_o, dlg_o, dlb_o):
        i = pl.program_id(0)
        dx = ALPHA * dr2_r[...]
        for k in range(NCHIP):
            dx = dx + _dot(dg_r[k], wg_r[k], NT) + _dot(du_r[k], wu_r[k], NT)
        xhat, rstd = _ln_stats(r1_r[...])
        _acc_rows(dlg_o, i == 0, dx * xhat)
        _acc_rows(dlb_o, i == 0, dx)
        dr1 = _ln_bwd(dx, xhat, rstd, g_r[...])
        dr1_o[...] = dr1
        dr1b_o[...] = dr1.astype(MX)

    tile = pl.BlockSpec((tm, D), lambda i: (i, 0))
    vec = pl.BlockSpec((1, D), lambda i: (0, 0))
    blk = pl.BlockSpec((NCHIP, tm, FB), lambda i: (0, i, 0))
    wspec = pl.BlockSpec((NCHIP, D, FB), lambda i: (0, 0, 0), pipeline_mode=pl.Buffered(1))
    return pl.pallas_call(
        body, name="ffn_up_bwd", grid=(T // tm,),
        in_specs=[tile, blk, blk, wspec, wspec, tile, vec],
        out_specs=[tile, tile, vec, vec],
        out_shape=[jax.ShapeDtypeStruct((T, D), f32), jax.ShapeDtypeStruct((T, D), MX)]
        + [jax.ShapeDtypeStruct((1, D), f32)] * 2,
        compiler_params=_cp(("arbitrary",), 58),
    )(dr2, dgate, dup, wg, wu, r1, g1)


TM_MIXB = 256


def mix_out_bwd(dr1, proj, mabc, wo, pa, pb, pc, ya, yb, yc, m):
    T = dr1.shape[0]
    tm = min(TM_MIXB, T)
    cb = D // NCHIP

    def body(dr_r, gt, mabc_r, wo_r, pa_r, pb_r, pc_r, ya_r, yb_r, yc_r, m_r,
             dgt_o, dya_o, dyb_o, dyc_o, dwo_o, dpa_o, dpb_o, dpc_o):
        @pl.when(pl.program_id(0) == 0)
        def _():
            for o in (dwo_o, dpa_o, dpb_o, dpc_o):
                o[...] = jnp.zeros_like(o)

        dr = dr_r[...].astype(MX)
        dm = _dot(dr, wo_r[...], NT)
        dmx = []
        for j in range(3):
            s = _sigmoid(gt[:, j * D:(j + 1) * D].astype(f32))
            dmx.append((dm * s).astype(MX))
            dgt_o[:, j * D:(j + 1) * D] = (dm * mabc_r[:, j * D:(j + 1) * D].astype(f32) * s * (1.0 - s)).astype(dgt_o.dtype)
        dya_o[...] = _dot(dmx[0], pa_r[...], NT).astype(dya_o.dtype)
        dyb = jnp.zeros((tm, AO), f32)
        for k in range(NCHIP):
            dyb = dyb + _dot(dmx[1][:, k * cb:(k + 1) * cb], pb_r[k], NT)
            dpb_o[k] += _dot(yb_r[...], dmx[1][:, k * cb:(k + 1) * cb], TN)
        dyb_o[...] = dyb.astype(dyb_o.dtype)
        dyc_o[...] = _dot(dmx[2], pc_r[...], NT).astype(dyc_o.dtype)
        dwo_o[...] += _dot(m_r[...], dr, TN)
        dpa_o[...] += _dot(ya_r[...], dmx[0], TN)
        dpc_o[...] += _dot(yc_r[...], dmx[2], TN)

    full = lambda shape: pl.BlockSpec(shape, lambda i: (0,) * len(shape), pipeline_mode=pl.Buffered(1))
    tile = lambda w: pl.BlockSpec((tm, w), lambda i: (i, 0))
    return pl.pallas_call(
        body, name="mix_out_bwd", grid=(T // tm,),
        in_specs=[tile(D), tile(3 * D), tile(3 * D), full((D, D)), full((D, D)), full((NCHIP, AO, cb)), full((D, D)),
                  tile(D), tile(AO), tile(D), tile(D)],
        out_specs=[tile(3 * D), tile(D), tile(AO), tile(D), full((D, D)), full((D, D)), full((NCHIP, AO, cb)), full((D, D))],
        out_shape=[jax.ShapeDtypeStruct((T, 3 * D), MX), jax.ShapeDtypeStruct((T, D), ACT),
                   jax.ShapeDtypeStruct((T, AO), ACT), jax.ShapeDtypeStruct((T, D), ACT), jax.ShapeDtypeStruct((D, D), f32),
                   jax.ShapeDtypeStruct((D, D), f32), jax.ShapeDtypeStruct((NCHIP, AO, cb), f32), jax.ShapeDtypeStruct((D, D), f32)],
        compiler_params=_cp(("arbitrary",), 58),
    )(dr1, proj, mabc, wo, pa, pb, pc, ya, yb, yc, m)


def transpose_cast(x):
    T = x.shape[0]
    tm = min(512, T)

    def body(x_r, o_r):
        o_r[...] = x_r[...].T.astype(o_r.dtype)

    return pl.pallas_call(
        body, name="transpose_cast", grid=(T // tm,),
        in_specs=[pl.BlockSpec((tm, D), lambda i: (i, 0))], out_specs=pl.BlockSpec((D, tm), lambda i: (0, i)),
        out_shape=jax.ShapeDtypeStruct((D, T), MX), compiler_params=_cp(("parallel",)),
    )(x)


def tn_matmul(name, a, b, a_spec, b_spec, out_shape, out_spec, grid):
    nt = len(grid) - 1

    def body(a_r, b_r, o_r):
        @pl.when(pl.program_id(nt) == 0)
        def _():
            o_r[...] = jnp.zeros_like(o_r)
        av = a_r[...].reshape(a_r.shape[-2:]).astype(MX)
        bv = b_r[...].reshape(b_r.shape[-2:]).astype(MX)
        o_r[...] += _dot(av, bv, TN).reshape(o_r.shape)

    return pl.pallas_call(
        body, name=name, grid=grid, in_specs=[a_spec, b_spec], out_specs=out_spec,
        out_shape=jax.ShapeDtypeStruct(out_shape, f32),
        compiler_params=_cp(("parallel",) * nt + ("arbitrary",), 56),
    )(a, b)


def attn_pre_bwd(dyb, os_, lses, ones):
    T = dyb.shape[0]
    tm = TM_FOLD

    def body(dy_r, o0, o1, o2, l0, l1, l2, ones_r, d0, d1, d2, f0, f1, f2, nat):
        o = [_unfold_in(nat, r, d) for r, (_, d) in zip((o0, o1, o2), GROUPS)]
        ls = [_unfold_in(nat, r, d) for r, (_, d) in zip((l0, l1, l2), GROUPS)]
        w = _group_weights(ls)
        dy = dy_r[...].astype(f32)
        t = dy * (w[0] * o[0] + w[1] * o[1] + w[2] * o[2])
        hi = t.astype(MX)
        lo = (t - hi.astype(f32)).astype(MX)
        c = _dot(hi, ones_r[...]) + _dot(lo, ones_r[...])
        for wg, do_o, df_o, (_, d) in zip(w, (d0, d1, d2), (f0, f1, f2), GROUPS):
            _fold_out(nat, wg * dy, do_o, d)
            _fold_out(nat, -wg * c, df_o, d)

    specs = _fold_specs(T, tm)
    return pl.pallas_call(
        body, name="attn_pre_bwd", grid=(T // tm,),
        in_specs=[pl.BlockSpec((tm, AO), lambda i: (i, 0))] + specs + specs + [pl.BlockSpec((AO, AO), lambda i: (0, 0))],
        out_specs=specs + specs,
        out_shape=[jax.ShapeDtypeStruct((d, T // d, AO), MX) for _, d in GROUPS]
        + [jax.ShapeDtypeStruct((d, T // d, AO), f32) for _, d in GROUPS],
        scratch_shapes=[pltpu.VMEM((AO // LANES, tm, LANES), f32)],
        compiler_params=_cp(("parallel",), 56),
    )(dyb, *os_, *lses, ones)


def _head_ones():
    i = jnp.arange(AO) // HD
    return (i[:, None] == i[None, :]).astype(MX)


BWD_BLOCKS = 8


def attn_bwd(qf, kf, vf, dof, lse, df, g, nb):
    T = qf.shape[0]

    def body(q_ref, k_ref, v_ref, do_ref, l_ref, d_ref, dq_ref, dk_ref, dv_ref):
        prev_m, cur_m = _window_masks()

        def head_col(ref, r0):
            v = ref[pl.ds(r0, BLK), :]
            return jnp.concatenate([v[:, 0:1], v[:, HD:HD + 1]], axis=0)

        def step(b, carry):
            dk_c, dv_c = carry
            r0 = pl.multiple_of(b * BLK, BLK)
            rp = pl.multiple_of(jnp.maximum(b - 1, 0) * BLK, BLK)
            qs, dos = _stack_heads(q_ref[pl.ds(r0, BLK), :]), _stack_heads(do_ref[pl.ds(r0, BLK), :])
            k2, v2 = _two_blocks(k_ref, b), _two_blocks(v_ref, b)
            valid = cur_m | (prev_m & ((b % nb) != 0))
            p = jnp.where(valid, jnp.exp(_dot(qs, k2, NT) - head_col(l_ref, r0)), 0.0)
            ds = (p * (_dot(dos, v2, NT) + head_col(d_ref, r0))).astype(MX)
            dq_ref[pl.ds(r0, BLK), :] = _unstack_heads(_dot(ds, k2)).astype(dq_ref.dtype)
            dk2 = _dot(ds, qs, TN)
            dv2 = _dot(p.astype(MX), dos, TN)
            dk_ref[pl.ds(rp, BLK), :] = (dk_c + dk2[:BLK]).astype(dk_ref.dtype)
            dv_ref[pl.ds(rp, BLK), :] = (dv_c + dv2[:BLK]).astype(dv_ref.dtype)
            return dk2[BLK:], dv2[BLK:]

        zero = jnp.zeros((BLK, LANES), f32)

        def steps(i, carry):
            for j in range(BWD_BLOCKS):
                carry = step(BWD_BLOCKS * i + j, carry)
            return carry

        dk_c, dv_c = lax.fori_loop(0, T // BLK // BWD_BLOCKS, steps, (zero, zero))
        dk_ref[pl.ds(T - BLK, BLK), :] = dk_c.astype(dk_ref.dtype)
        dv_ref[pl.ds(T - BLK, BLK), :] = dv_c.astype(dv_ref.dtype)

    spec = pl.BlockSpec((T, LANES), lambda j: (0, j))
    return pl.pallas_call(
        body, name=f"attn_bwd{g}", grid=(AO // LANES,),
        in_specs=[spec] * 6, out_specs=[spec] * 3,
        out_shape=[jax.ShapeDtypeStruct((T, AO), MX)] * 3,
        compiler_params=_cp(("parallel",), 60),
    )(qf, kf, vf, dof, lse, df)


def unfold_rope_bwd(dqf, dkf, dvf, cos_t, sin_t, g, d):
    T = dqf.shape[0] * dqf.shape[1]
    tm = TM_FOLD

    def body(q_r, k_r, v_r, c_ref, s_ref, o_ref, nat):
        cos, sin = _tile4(c_ref[...]), _tile4(s_ref[...])
        for part, ref, scale in ((0, q_r, HD ** -0.5), (1, k_r, 1.0), (2, v_r, None)):
            x = _unfold_in(nat, ref, d)
            if scale is not None:
                x = (x * cos - _swap_halves(x) * sin) * scale
            o_ref[:, part * AO:(part + 1) * AO] = x.astype(o_ref.dtype)

    fold_spec = pl.BlockSpec((d, tm // d, AO), lambda i: (0, i, 0))
    tab = pl.BlockSpec((tm, LANES), lambda i: (i, 0))
    return pl.pallas_call(
        body, name=f"unfold_rope_bwd{g}", grid=(T // tm,),
        in_specs=[fold_spec] * 3 + [tab, tab],
        out_specs=pl.BlockSpec((tm, 3 * AO), lambda i: (i, 0)),
        out_shape=jax.ShapeDtypeStruct((T, 3 * AO), MX),
        scratch_shapes=[pltpu.VMEM((AO // LANES, tm, LANES), f32)],
        compiler_params=_cp(("parallel",)),
    )(dqf, dkf, dvf, cos_t, sin_t)


CONV_CHUNK = 32


def conv_bwd(dya, proj, conv_w):
    T = dya.shape[0]
    tm = TM_AC
    last = T // tm - 1

    def body(dy_r, bch, hprev, dy_next, b_next, cw, d_o, dw_o, zs, ds):
        i = pl.program_id(0)
        ch = CONV_CHUNK
        hz = hprev[:, :D].astype(f32) * hprev[:, D:].astype(f32)
        zs[0:HALO, :] = jnp.where(i > 0, hz, 0.0)
        ds[tm:tm + HALO, :] = jnp.where(i < last, dy_next[...].astype(f32) * b_next[...].astype(f32), 0.0)
        for r in range(0, tm, ch):
            zs[HALO + r:HALO + r + ch, :] = bch[r:r + ch, D:2 * D].astype(f32) * bch[r:r + ch, 2 * D:].astype(f32)
            ds[r:r + ch, :] = dy_r[r:r + ch, :].astype(f32) * bch[r:r + ch, :D].astype(f32)

        @pl.when(i == 0)
        def _():
            dw_o[...] = jnp.zeros_like(dw_o)

        sums = [jnp.zeros((1, D), f32) for _ in range(3)]
        for r in range(0, tm, ch):
            z2, z1, z = (zs[HALO + r - s:HALO + r - s + ch, :] for s in (2, 1, 0))
            dcv, d1, d2 = (ds[r + s:r + s + ch, :] for s in (0, 1, 2))
            cv = cw[0:1, :] * z2 + cw[1:2, :] * z1 + cw[2:3, :] * z
            dz = cw[2:3, :] * dcv + cw[1:2, :] * d1 + cw[0:1, :] * d2
            d_o[r:r + ch, :D] = (dy_r[r:r + ch, :].astype(f32) * cv).astype(d_o.dtype)
            d_o[r:r + ch, D:2 * D] = (dz * bch[r:r + ch, 2 * D:].astype(f32)).astype(d_o.dtype)
            d_o[r:r + ch, 2 * D:] = (dz * bch[r:r + ch, D:2 * D].astype(f32)).astype(d_o.dtype)
            for k, zz in enumerate((z2, z1, z)):
                sums[k] = sums[k] + jnp.sum(dcv * zz, axis=0, keepdims=True)
        for k in range(3):
            dw_o[k:k + 1, :] += sums[k]

    nh = tm // HALO
    return pl.pallas_call(
        body, name="conv_bwd", grid=(T // tm,),
        in_specs=[pl.BlockSpec((tm, D), lambda i: (i, 0)), pl.BlockSpec((tm, 3 * D), lambda i: (i, 1)),
                  pl.BlockSpec((HALO, 2 * D), lambda i: (jnp.maximum(i * nh - 1, 0), 2)),
                  pl.BlockSpec((HALO, D), lambda i: (jnp.minimum((i + 1) * nh, T // HALO - 1), 0)),
                  pl.BlockSpec((HALO, D), lambda i: (jnp.minimum((i + 1) * nh, T // HALO - 1), 3)),
                  pl.BlockSpec((3, D), lambda i: (0, 0))],
        out_specs=[pl.BlockSpec((tm, 3 * D), lambda i: (i, 0)), pl.BlockSpec((3, D), lambda i: (0, 0))],
        out_shape=[jax.ShapeDtypeStruct((T, 3 * D), MX), jax.ShapeDtypeStruct((3, D), f32)],
        scratch_shapes=[pltpu.VMEM((HALO + tm, D), f32), pltpu.VMEM((tm + HALO, D), f32)],
        compiler_params=_cp(("arbitrary",)),
    )(dya, proj, proj, dya, proj, conv_w)


def gmlp_bwd(dyc, proj, wst, bsx, lg, lb):
    T = dyc.shape[0]
    tm = TM_AC
    last = T // tm - 1

    def body(dy_r, u0, u1, v0, v1, ws, bs, lg_r, lb_r, d_o, dws_o, dbs_o, dlg_o, dlb_o, bacc):
        i = pl.program_id(0)
        up = jnp.concatenate([u0[...], u1[...]], axis=1).astype(f32)
        vp = jnp.concatenate([v0[...], v1[...]], axis=1).astype(f32)
        u, du = _gelu_and_grad(up)
        gv, dgv = _gelu_and_grad(vp)
        u, vn, xhat, rstd, sp = _gmlp_fwd(up, vp, ws, bs, lg_r[...], lb_r[...], u, gv)
        dy = dy_r[...].astype(f32)
        d_o[:, :D] = (dy * sp * du).astype(d_o.dtype)
        dsp = dy * u
        dspb, vnb = dsp.astype(MX), vn.astype(MX)

        @pl.when(i == 0)
        def _():
            dws_o[...] = jnp.zeros_like(dws_o)
            bacc[...] = jnp.zeros_like(bacc)

        rows = []
        for c in range(tm // BLK):
            r = slice(c * BLK, (c + 1) * BLK)
            cols = []
            for g in range(8):
                cs = slice(g * BLK, (g + 1) * BLK)
                dws_o[g] += _dot(dspb[r, cs], vnb[r, cs], NT)
                bacc[g] += dsp[r, cs]
                cols.append(_dot(ws[g], dspb[r, cs], TN))
            rows.append(jnp.concatenate(cols, axis=1))
        dvn = jnp.concatenate(rows, axis=0)
        _acc_rows(dlg_o, i == 0, dvn * xhat)
        _acc_rows(dlb_o, i == 0, dvn)
        d_o[:, D:] = (_ln_bwd(dvn, xhat, rstd, lg_r[...]) * dgv).astype(d_o.dtype)

        @pl.when(i == last)
        def _():
            row = lax.broadcasted_iota(jnp.int32, (BLK, BLK), 0)
            col = lax.broadcasted_iota(jnp.int32, (BLK, BLK), 1)
            ones = jnp.ones((8, BLK), MX)
            for g in range(8):
                dws_o[g] = jnp.where(col <= row, dws_o[g], 0.0)
                a = bacc[g]
                hi = a.astype(MX)
                lo = (a - hi.astype(f32)).astype(MX)
                dbs_o[g:g + 1, :] = (_dot(ones, hi, NT) + _dot(ones, lo, NT))[0:1, :]

    full = lambda shape: pl.BlockSpec(shape, lambda i: (0,) * len(shape))
    return pl.pallas_call(
        body, name="gmlp_bwd", grid=(T // tm,),
        in_specs=[pl.BlockSpec((tm, D), lambda i: (i, 0)), *_uv_specs(), full((8, BLK, BLK)), full((8, BLK, BLK)),
                  full((1, D)), full((1, D))],
        out_specs=[pl.BlockSpec((tm, 2 * D), lambda i: (i, 0)), full((8, BLK, BLK)), full((8, BLK)), full((1, D)), full((1, D))],
        out_shape=[jax.ShapeDtypeStruct((T, 2 * D), MX), jax.ShapeDtypeStruct((8, BLK, BLK), f32),
                   jax.ShapeDtypeStruct((8, BLK), f32), jax.ShapeDtypeStruct((1, D), f32), jax.ShapeDtypeStruct((1, D), f32)],
        scratch_shapes=[pltpu.VMEM((8, BLK, BLK), f32)],
        compiler_params=_cp(("arbitrary",)),
    )(dyc, proj, proj, proj, proj, wst, bsx, lg, lb)


PART_TILES = (6, 6, 3, 3, 3, 4)
PART_START = (0, 6, 12, 15, 18, 21)
TJ = 512


def _part_specs(tm, rows_axis):
    specs = []
    for n, s in zip(PART_TILES, PART_START):
        def imap(*idx, n=n, s=s):
            i, j = idx[rows_axis], idx[1 - rows_axis]
            inside = (j >= s) & (j < s + n)
            return (jnp.where(inside, i, 0), jnp.clip(j - s, 0, n - 1))
        specs.append(pl.BlockSpec((tm, TJ), imap))
    return specs


def _with_part(j, refs, fn):
    for r, n, s in zip(refs, PART_TILES, PART_START):
        @pl.when((j >= s) & (j < s + n))
        def _():
            fn(r[...])


def dx_in(dr1, parts, w, bias):
    T = dr1.shape[0]
    tm = min(2048, T)

    def body(dr_r, p0, p1, p2, p3, p4, p5, w_r, b_r, o_r):
        j = pl.program_id(1)

        @pl.when(j == 0)
        def _():
            o_r[...] = ALPHA * dr_r[...] + b_r[...]

        def acc(tile):
            o_r[...] += _dot(tile, w_r[...], NT)
        _with_part(j, (p0, p1, p2, p3, p4, p5), acc)

    once = dict(pipeline_mode=pl.Buffered(1))
    return pl.pallas_call(
        body, name="dx_in", grid=(T // tm, NIN // TJ),
        in_specs=[pl.BlockSpec((tm, D), lambda i, j: (i, 0), **once)] + _part_specs(tm, 0)
        + [pl.BlockSpec((D, TJ), lambda i, j: (0, j)), pl.BlockSpec((1, D), lambda i, j: (0, 0))],
        out_specs=pl.BlockSpec((tm, D), lambda i, j: (i, 0), **once),
        out_shape=jax.ShapeDtypeStruct((T, D), f32),
        compiler_params=_cp(("parallel", "arbitrary"), 56),
    )(dr1, *parts, w, bias)


def dw_in(x0t, parts):
    T = x0t.shape[1]
    tk = min(2048, T)

    def body(x_r, p0, p1, p2, p3, p4, p5, o_r):
        j, t = pl.program_id(0), pl.program_id(1)

        @pl.when(t == 0)
        def _():
            o_r[...] = jnp.zeros_like(o_r)

        def acc(tile):
            o_r[...] += _dot(x_r[:, pl.ds(pl.multiple_of(t * tk, tk), tk)], tile)
        _with_part(j, (p0, p1, p2, p3, p4, p5), acc)

    return pl.pallas_call(
        body, name="dw_in", grid=(NIN // TJ, T // tk),
        in_specs=[pl.BlockSpec((D, T), lambda j, t: (0, 0), pipeline_mode=pl.Buffered(1))] + _part_specs(tk, 1),
        out_specs=pl.BlockSpec((D, TJ), lambda j, t: (0, j)),
        out_shape=jax.ShapeDtypeStruct((D, NIN), f32),
        compiler_params=_cp(("parallel", "arbitrary"), 56),
    )(x0t, *parts)


def rope_tables(positions):
    half = HD // 2
    inv_freq = ROPE_THETA ** (-jnp.arange(half, dtype=f32) / half)
    ang = positions.astype(f32)[:, None] * inv_freq
    cos, sin = jnp.cos(ang), jnp.sin(ang)
    return jnp.tile(cos, (1, LANES // half)), jnp.tile(jnp.concatenate([-sin, sin], axis=1), (1, LANES // HD))


def _flat(a):
    return a.reshape(a.shape[0] * a.shape[1], a.shape[2])


def layer_fwd(x0, W, cos_t, sin_t):
    T = x0.shape[0]
    proj = mm_in(x0, W["w_in"], W["in_bias"])
    ya, yc = mix_ac_fwd(proj, W["conv_w"], W["wst"], W["bsx"], W["gmlp_ln_g"], W["gmlp_ln_b"])
    folded, os_, lses = [], [], []
    for g, (_, d) in enumerate(GROUPS):
        qf, kf, vf = fold_rope(proj, cos_t, sin_t, g, d)
        o, lse = attn_fwd(_flat(qf), _flat(kf), _flat(vf), g, T // d // BLK)
        folded.append((qf, kf, vf))
        os_.append(o.reshape(d, T // d, AO))
        lses.append(lse.reshape(d, T // d, AO))
    yb = combine_fwd(os_, lses)
    if "late" in W:
        W = {**W, **W["late"](yb)}
    mabc, m, r1, x1, x1b = mix_out_fwd(proj, ya, yb, yc, x0, W["p_a"], W["p_b"], W["p_c"], W["w_o"], W["ln1_g"], W["ln1_b"])
    gate, up, r2, x2 = ffn_fwd(x1b, x1, W["w_gate"], W["w_up"], W["w_down"], W["ln2_g"], W["ln2_b"])
    saved = dict(x0=x0, proj=proj, ya=ya, yb=yb, yc=yc, folded=folded, os=os_, lses=lses, mabc=mabc, m=m, r1=r1,
                 x1b=x1b, gate=gate, up=up, r2=r2)
    return x2, saved, W


def layer_bwd(dx2, S, W, cos_t, sin_t, on_grads=None):
    T = dx2.shape[0]
    tk = min(4096, T)
    G = {}
    dr2, dr2b, dgate, dup, hh, G["ln2_g"], G["ln2_b"] = ffn_down_bwd(dx2, S["r2"], W["ln2_g"], W["w_down"], S["gate"], S["up"])
    blk_a = pl.BlockSpec((1, tk, FB), lambda k, t: (k, t, 0))
    row_b = pl.BlockSpec((tk, D), lambda k, t: (t, 0))
    G["w_down"] = tn_matmul("dw_down", hh, dr2b, blk_a, row_b, (NCHIP, FB, D),
                            pl.BlockSpec((1, FB, D), lambda k, t: (k, 0, 0)), (NCHIP, T // tk))
    for nm, dv in (("w_gate", dgate), ("w_up", dup)):
        G[nm] = tn_matmul("d" + nm, dv, S["x1b"], blk_a, row_b, (NCHIP, FB, D),
                          pl.BlockSpec((1, FB, D), lambda k, t: (k, 0, 0)), (NCHIP, T // tk))
    dr1, dr1b, G["ln1_g"], G["ln1_b"] = ffn_up_bwd(dr2, dgate, dup, W["w_gate"], W["w_up"], S["r1"], W["ln1_g"])
    dgates, dya, dyb, dyc, G["w_o"], G["p_a"], G["p_b"], G["p_c"] = mix_out_bwd(
        dr1b, S["proj"], S["mabc"], W["w_o"], W["p_a"], W["p_b"], W["p_c"], S["ya"], S["yb"], S["yc"], S["m"])
    conv_w = W["conv_w"]
    if on_grads is not None:
        conv_w = conv_w + on_grads({n: G[n] for n in BIG if n != "w_in"})
    dbch, G["conv_w"] = conv_bwd(dya, S["proj"], conv_w)
    duv, G["w_s"], G["b_s"], G["gmlp_ln_g"], G["gmlp_ln_b"] = gmlp_bwd(
        dyc, S["proj"], W["wst"], W["bsx"], W["gmlp_ln_g"], W["gmlp_ln_b"])
    ones = _head_ones()
    if on_grads is not None:
        small = {n: G[n] for n in VECS + ("b_s", "w_s", "conv_w")}
        ones = ones + on_grads(small).astype(MX)
    pre = attn_pre_bwd(dyb, S["os"], S["lses"], ones)
    dqkv = []
    for g, (_, d) in enumerate(GROUPS):
        qf, kf, vf = S["folded"][g]
        dqf, dkf, dvf = attn_bwd(_flat(qf), _flat(kf), _flat(vf), _flat(pre[g]), _flat(S["lses"][g]), _flat(pre[3 + g]),
                                 g, T // d // BLK)
        shp = (d, T // d, AO)
        dqkv.append(unfold_rope_bwd(dqf.reshape(shp), dkf.reshape(shp), dvf.reshape(shp), cos_t, sin_t, g, d))
    parts = (dgates, dbch, *dqkv, duv)
    G["w_in"] = dw_in(transpose_cast(S["x0"]), parts)
    bias = jnp.zeros((1, D), f32)
    if on_grads is not None:
        bias = bias + on_grads({"w_in": G["w_in"]})
    dx0 = dx_in(dr1, parts, W["w_in"], bias)
    started = on_grads({"dx": dx0}) if on_grads is not None else None
    return dx0, G, started


def prep_layer_weights(Wl):
    W = dict(Wl)
    tril = jnp.tril(jnp.ones((BLK, BLK), f32))
    W["wst"] = (Wl["w_s"] * tril[None]).astype(MX)
    W["bsx"] = jnp.broadcast_to(Wl["b_s"][:, :, None], (8, BLK, BLK))
    for n in ("gmlp_ln_g", "gmlp_ln_b", "ln1_g", "ln1_b", "ln2_g", "ln2_b"):
        W[n] = Wl[n].reshape(1, D)
    W["in_bias"] = jnp.zeros((1, NIN), f32) + Wl.get("after", 0.0)
    return W


def local_step(x, positions, target, layers, on_grads=None):
    cos_t, sin_t = rope_tables(positions)
    Ws, saved = [], []
    h = x
    for Wl in layers:
        h, S, W = layer_fwd(h, prep_layer_weights(Wl(h) if callable(Wl) else Wl), cos_t, sin_t)
        Ws.append(W)
        saved.append(S)
    lsum, dh = loss_grad(h, target)
    if on_grads is not None:
        on_grads(len(Ws), {"loss": lsum})
    grads = [None] * len(Ws)
    started = None
    for l in reversed(range(len(Ws))):
        W = Ws[l]
        if started is not None:
            W = dict(W, ln2_g=W["ln2_g"] + started)
        hook = functools.partial(on_grads, l) if on_grads is not None else None
        dh, grads[l], started = layer_bwd(dh, saved[l], W, cos_t, sin_t, hook)
    return lsum, dh, grads


MESH = pl.DeviceIdType.MESH
ANY = pl.BlockSpec(memory_space=pl.ANY)
BIG = ("w_in", "w_gate", "w_up", "w_down", "p_a", "p_b", "p_c", "w_o")
NBIG = len(BIG)


def _place():
    x, y, c = lax.axis_index("x"), lax.axis_index("y"), lax.axis_index("c")
    return x, y, c, 2 * x + y


def _rcopy(src, dst, send, recv, dev):
    return pltpu.make_async_remote_copy(src_ref=src, dst_ref=dst, send_sem=send, recv_sem=recv, device_id=dev,
                                        device_id_type=MESH)


def _cols(ref, k, width):
    start = k * width if isinstance(k, int) else pl.multiple_of(k * width, LANES)
    return ref.at[:, pl.ds(start, width)]


CHUNK_BYTES = 1 << 20


def _pieces(shape, itemsize, nbytes=CHUNK_BYTES):
    rows, cols = shape[-2], shape[-1]
    per = max(16, nbytes // (cols * itemsize) // 16 * 16)
    out = []
    for lead in (range(shape[0]) if len(shape) == 3 else (None,)):
        for r in range(0, rows, per):
            sl = (pl.ds(r, min(per, rows - r)), slice(None))
            out.append(sl if lead is None else (lead,) + sl)
    return out


def _start_pieces(src, dst, make, nbytes=CHUNK_BYTES):
    for idx in _pieces(src.shape, jnp.dtype(src.dtype).itemsize, nbytes):
        make(src.at[idx], dst.at[idx]).start()


def gather_halves(shards):
    n = len(shards)

    def body(*refs):
        srcs, dsts = refs[:n], refs[n:2 * n]
        send, recv, own_send, own_recv = refs[2 * n:]
        x, y, c, k = _place()
        sib = (x, y, 1 - c)
        chips = [(1 - x, y), (x, 1 - y), (1 - x, 1 - y)]

        def slot(a, layer, pos):
            if a == 0:
                return _cols(dsts[0].at[layer], pos, WIN_SHARD)
            return dsts[a].at[pos, layer]

        def ici(a, j, src, dst):
            return _rcopy(src, dst, send.at[a, j], recv.at[a, j], (*chips[j], c))

        def d2d(a, j, src, dst):
            return _rcopy(src, dst, send.at[a, 3 + j], recv.at[a, 3 + j], sib)

        def own(a, layer, src, dst):
            return _rcopy(src, dst, own_send.at[a, layer], own_recv.at[a, layer], sib)

        for a in range(n):
            for j in range(3):
                _start_pieces(srcs[a].at[c], slot(a, c, k), functools.partial(ici, a, j))
        for a in range(n):
            for layer in range(DEPTH):
                _start_pieces(srcs[a].at[layer], slot(a, layer, k), functools.partial(own, a, layer))
        for a in range(n):
            for j, (cx, cy) in enumerate(chips):
                landed = slot(a, c, 2 * cx + cy)
                ici(a, j, landed, landed).wait_recv()
                _start_pieces(landed, landed, functools.partial(d2d, a, j))
        for a in range(n):
            for j, (cx, cy) in enumerate(chips):
                passed = slot(a, 1 - c, 2 * cx + cy)
                d2d(a, j, passed, passed).wait_recv()
                landed = slot(a, c, 2 * cx + cy)
                d2d(a, j, landed, landed).wait_send()
                ici(a, j, srcs[a].at[c], slot(a, c, k)).wait_send()
            for layer in range(DEPTH):
                own(a, layer, srcs[a].at[layer], slot(a, layer, k)).wait()

    outs = [jax.ShapeDtypeStruct((2, shards[0].shape[1], NIN), shards[0].dtype)]
    outs += [jax.ShapeDtypeStruct((NCHIP,) + s.shape, s.dtype) for s in shards[1:]]
    return pl.pallas_call(
        body, name="gather_halves", in_specs=[ANY] * n, out_specs=[ANY] * n, out_shape=outs,
        scratch_shapes=[pltpu.SemaphoreType.DMA((n, 6)), pltpu.SemaphoreType.DMA((n, 6)),
                        pltpu.SemaphoreType.DMA((n, DEPTH)), pltpu.SemaphoreType.DMA((n, DEPTH))],
    )(*shards)


def _gather_slot(dst, pos):
    return _cols(dst, pos, WIN_SHARD) if len(dst.shape) == 2 else dst.at[pos]


def _gather_copy(a, j, src, dst, send, recv, dev):
    return _rcopy(src, dst, send.at[a * NCHIP + j], recv.at[a * NCHIP + j], dev)


def gather_start(tag, shards, after):
    n = len(shards)

    def body(*refs):
        srcs, dsts = refs[:n], refs[n:2 * n]
        send, recv = refs[2 * n + len(after)], refs[2 * n + len(after) + 1]
        token = refs[-1]
        x, y, c, k = _place()
        peers = [(1 - x, y, c), (x, 1 - y, c), (1 - x, 1 - y, c), (x, y, 1 - c)]
        for a in range(n):
            for j, dev in enumerate(peers):
                _start_pieces(srcs[a], _gather_slot(dsts[a], k),
                              lambda s, d, a=a, j=j, dev=dev: _gather_copy(a, j, s, d, send, recv, dev))
        token[...] = jnp.zeros_like(token)

    gathered = [lax.empty((D, NIN) if s.shape == (D, WIN_SHARD) else (NCHIP,) + s.shape, s.dtype) for s in shards]
    ops = [pltpu.with_memory_space_constraint(v, pltpu.HBM) for v in list(shards) + gathered]
    sem = pltpu.SemaphoreType.DMA((n * NCHIP,))
    res = pl.pallas_call(
        body, name=f"gather_start{tag}", in_specs=[HBM] * (2 * n) + [ANY] * len(after),
        out_specs=[SEMS, SEMS] + [HBM] * (2 * n) + [pl.BlockSpec(memory_space=pltpu.VMEM)],
        out_shape=[sem, sem] + [pltpu.HBM(v.shape, v.dtype) for v in ops] + [jax.ShapeDtypeStruct((8, LANES), f32)],
        input_output_aliases={i: 2 + i for i in range(2 * n)},
        compiler_params=pltpu.CompilerParams(has_side_effects=EFFECT),
    )(*ops, *after)
    return res[0], res[1], res[2:2 + n], res[2 + n:2 + 2 * n], res[-1]


def gather_wait(tag, send, recv, shards, gathered, after):
    n = len(shards)

    def body(*refs):
        srcs, dsts = refs[:n], refs[n:2 * n]
        send_r, recv_r = refs[2 * n], refs[2 * n + 1]
        x, y, c, k = _place()
        peers = [(1 - x, y, c), (x, 1 - y, c), (1 - x, 1 - y, c), (x, y, 1 - c)]
        for a in range(n):
            for j, dev in enumerate(peers):
                _gather_copy(a, j, srcs[a], _gather_slot(dsts[a], k), send_r, recv_r, dev).wait_send()
                pos = 2 * dev[0] + dev[1]
                _gather_copy(a, j, srcs[a], _gather_slot(dsts[a], pos), send_r, recv_r, dev).wait_recv()

    ops = list(shards) + list(gathered)
    res = pl.pallas_call(
        body, name=f"gather_wait{tag}", in_specs=[HBM] * (2 * n) + [SEMS, SEMS] + [ANY] * len(after),
        out_specs=[HBM] * (2 * n), out_shape=[pltpu.HBM(v.shape, v.dtype) for v in ops],
        input_output_aliases={i: i for i in range(2 * n)},
        compiler_params=pltpu.CompilerParams(has_side_effects=EFFECT),
    )(*ops, send, recv, *after)
    return res[n:]


def _half(ref, h):
    rows = ref.shape[-2] // 2
    start = pl.multiple_of(h * rows, 16)
    if len(ref.shape) == 2:
        return ref.at[pl.ds(start, rows), :]
    return ref.at[:, pl.ds(start, rows), :]


HBM = pl.BlockSpec(memory_space=pltpu.HBM)
SEMS = pl.BlockSpec(memory_space=pltpu.SEMAPHORE)
EFFECT = pltpu.SideEffectType.DATAFLOW_SIDE_EFFECTING


def rs_pair_start(tag, grads, halves=True):
    n = len(grads)

    def body(*refs):
        g, theirs = refs[:n], refs[n:2 * n]
        send, recv = refs[2 * n], refs[2 * n + 1]
        x, y, c, _ = _place()
        for a in range(n):
            _start_pieces(_half(g[a], 1 - c) if halves else g[a], theirs[a],
                          lambda s, d, a=a: _rcopy(s, d, send.at[a], recv.at[a], (x, y, 1 - c)))
        refs[-1][...] = jnp.zeros_like(refs[-1])

    lands = [lax.empty(g.shape[:-2] + (g.shape[-2] // 2 if halves else g.shape[-2], g.shape[-1]), g.dtype) for g in grads]
    ops = [pltpu.with_memory_space_constraint(v, pltpu.HBM) for v in list(grads) + lands]
    sem = pltpu.SemaphoreType.DMA((n,))
    res = pl.pallas_call(
        body, name=f"rs_pair_start{tag}", in_specs=[HBM] * (2 * n),
        out_specs=[SEMS, SEMS] + [HBM] * (2 * n) + [pl.BlockSpec(memory_space=pltpu.VMEM)],
        out_shape=[sem, sem] + [pltpu.HBM(v.shape, v.dtype) for v in ops] + [jax.ShapeDtypeStruct((8, LANES), f32)],
        input_output_aliases={i: 2 + i for i in range(2 * n)},
        compiler_params=pltpu.CompilerParams(has_side_effects=EFFECT),
    )(*ops)
    return res[0], res[1], res[2:2 + n], res[2 + n:2 + 2 * n], res[-1]


def rs_pair_wait(tag, send, recv, grads, theirs, after, halves=True):
    n = len(grads)

    def body(*refs):
        g, land = refs[:n], refs[n:2 * n]
        send_r, recv_r = refs[2 * n], refs[2 * n + 1]
        x, y, c, _ = _place()
        for a in range(n):
            cp = _rcopy(_half(g[a], 1 - c) if halves else g[a], land[a], send_r.at[a], recv_r.at[a], (x, y, 1 - c))
            cp.wait_send()
            cp.wait_recv()

    ops = list(grads) + list(theirs)
    res = pl.pallas_call(
        body, name=f"rs_pair_wait{tag}", in_specs=[HBM] * (2 * n) + [SEMS, SEMS] + [ANY] * len(after),
        out_specs=[HBM] * (2 * n), out_shape=[pltpu.HBM(v.shape, v.dtype) for v in ops],
        input_output_aliases={i: i for i in range(2 * n)},
        compiler_params=pltpu.CompilerParams(has_side_effects=EFFECT),
    )(*ops, send, recv, *after)
    return res[:n], res[n:]


def _chip_piece(ref, k):
    return _cols(ref, k, WIN_SHARD) if len(ref.shape) == 2 else ref.at[k]


def _chip_copy(a, k, src, dst, send, recv, me, c):
    return _rcopy(src, dst, send.at[a * NCHIP + k], recv.at[a * NCHIP + me], (k // 2, k % 2, c))


def rs_chips_start(tag, sums):
    n = len(sums)

    def pshape(s):
        return (NCHIP, s[0], WIN_SHARD) if len(s) == 2 else s

    def body(*refs):
        s, land = refs[:n], refs[n:2 * n]
        send, recv = refs[2 * n], refs[2 * n + 1]
        token = refs[-1]
        x, y, c, me = _place()
        for k in range(NCHIP):
            @pl.when(me != k)
            def _():
                for a in range(n):
                    _start_pieces(_chip_piece(s[a], k), land[a].at[me],
                                  lambda src, dst, a=a: _chip_copy(a, k, src, dst, send, recv, me, c))
        token[...] = jnp.zeros_like(token)

    lands = [lax.empty(pshape(v.shape), v.dtype) for v in sums]
    ops = [pltpu.with_memory_space_constraint(v, pltpu.HBM) for v in list(sums) + lands]
    sem = pltpu.SemaphoreType.DMA((n * NCHIP,))
    res = pl.pallas_call(
        body, name=f"rs_chips_start{tag}", in_specs=[HBM] * (2 * n),
        out_specs=[SEMS, SEMS] + [HBM] * (2 * n) + [pl.BlockSpec(memory_space=pltpu.VMEM)],
        out_shape=[sem, sem] + [pltpu.HBM(v.shape, v.dtype) for v in ops] + [jax.ShapeDtypeStruct((8, LANES), f32)],
        input_output_aliases={i: 2 + i for i in range(2 * n)},
        compiler_params=pltpu.CompilerParams(has_side_effects=EFFECT),
    )(*ops)
    return res[0], res[1], res[2:2 + n], res[2 + n:2 + 2 * n], res[-1]


def rs_chips_wait(tag, send, recv, sums, lands, after):
    n = len(sums)

    def body(*refs):
        s, land = refs[:n], refs[n:2 * n]
        send_r, recv_r = refs[2 * n], refs[2 * n + 1]
        x, y, c, me = _place()
        for k in range(NCHIP):
            @pl.when(me != k)
            def _():
                for a in range(n):
                    piece = _chip_piece(s[a], k)
                    _chip_copy(a, k, piece, land[a].at[me], send_r, recv_r, me, c).wait_send()
                    _rcopy(piece, land[a].at[k], send_r.at[a * NCHIP + k], recv_r.at[a * NCHIP + k],
                           (k // 2, k % 2, c)).wait_recv()

    ops = list(sums) + list(lands)
    res = pl.pallas_call(
        body, name=f"rs_chips_wait{tag}", in_specs=[HBM] * (2 * n) + [SEMS, SEMS] + [ANY] * len(after),
        out_specs=[HBM] * (2 * n), out_shape=[pltpu.HBM(v.shape, v.dtype) for v in ops],
        input_output_aliases={i: i for i in range(2 * n)},
        compiler_params=pltpu.CompilerParams(has_side_effects=EFFECT),
    )(*ops, send, recv, *after)
    return res[:n], res[n:]


def _row_tile(rows, cols, itemsize=4, target=2 << 20):
    best = 8
    for t in range(8, rows + 1, 8):
        if rows % t == 0 and t * cols * itemsize <= target:
            best = t
    return best


GRAD_WIRE = jnp.bfloat16


def add_half(name, g, t, c):
    cols, half = t.shape[-1], t.shape[-2]
    nblk = 1 if t.ndim == 2 else t.shape[0]
    tr = _row_tile(half, cols)
    per = half // tr

    def body(c_ref, g_r, t_r, o_r):
        o_r[...] = (g_r[...] + t_r[...]).astype(o_r.dtype)

    tile_t = pl.BlockSpec((tr, cols), lambda i, c_ref: (i, 0))
    tile_g = pl.BlockSpec((tr, cols), lambda i, c_ref: ((i // per) * 2 * per + c_ref[0] * per + i % per, 0))
    out = pl.pallas_call(
        body, name=name, out_shape=jax.ShapeDtypeStruct((nblk * half, cols), GRAD_WIRE),
        grid_spec=pltpu.PrefetchScalarGridSpec(num_scalar_prefetch=1, grid=(nblk * per,), in_specs=[tile_g, tile_t],
                                               out_specs=tile_t),
        compiler_params=_cp(("parallel",)),
    )(c.reshape(1).astype(jnp.int32), g.reshape(nblk * 2 * half, cols), t.reshape(nblk * half, cols))
    return out.reshape(t.shape)


def add_chips(name, land, own):
    _, rows, cols = land.shape
    tr = _row_tile(rows, cols, target=1 << 20)

    def body(land_r, own_r, o_r):
        me = 2 * lax.axis_index("x") + lax.axis_index("y")
        for k in range(NCHIP):
            @pl.when(me == k)
            def _():
                acc = None
                for j in range(NCHIP):
                    t = (own_r[...] if j == k else land_r[j]).astype(f32)
                    acc = t if acc is None else acc + t
                o_r[...] = acc

    tile = pl.BlockSpec((tr, cols), lambda i: (i, 0))
    return pl.pallas_call(
        body, name=name, grid=(rows // tr,), in_specs=[pl.BlockSpec((NCHIP, tr, cols), lambda i: (0, i, 0)), tile],
        out_specs=tile, out_shape=jax.ShapeDtypeStruct((rows, cols), f32), compiler_params=_cp(("parallel",)),
    )(land, own)


def reduce_scatter_pair(tag, G):
    names = tuple(G)
    grads = [G[n] if G[n].ndim == 3 or n == "w_in" else G[n].reshape(NCHIP, D // NCHIP, D) for n in names]
    send, recv, grads, theirs, token = rs_pair_start(tag, grads)
    return (tag, names, send, recv, grads, theirs), token[0, 0]


def reduce_scatter_chips(state, after):
    c = lax.axis_index("c")
    tag, names, send, recv, grads, theirs = state
    grads, theirs = rs_pair_wait(tag, send, recv, grads, theirs, after)
    sums = [add_half(f"rs_add_pair{tag}_{n}", g, t, c) for n, g, t in zip(names, grads, theirs)]
    send, recv, sums, lands, token = rs_chips_start(tag, sums)
    return (tag, names, send, recv, sums, lands), token[0, 0]


def reduce_scatter_finish(state, after):
    me = 2 * lax.axis_index("x") + lax.axis_index("y")
    tag, names, send, recv, sums, lands = state
    sums, landed = rs_chips_wait(tag, send, recv, sums, lands, after)
    halves = []
    for n, s, v in zip(names, sums, landed):
        own = lax.dynamic_slice_in_dim(s, me * WIN_SHARD, WIN_SHARD, axis=1) if s.ndim == 2 else \
            lax.dynamic_index_in_dim(s, me, 0, keepdims=False)
        halves.append(add_chips(f"rs_add_chips{tag}_{n}", v, own))
    send, recv, halves, others, _ = rs_pair_start("_join" + tag, halves, halves=False)
    return tag, names, send, recv, halves, others


def reduce_scatter_join(state, after):
    tag, names, send, recv, halves, others = state
    halves, others = rs_pair_wait("_join" + tag, send, recv, halves, others, after, halves=False)
    return dict(zip(names, zip(halves, others)))


NDEV = 8


def _small_copy(r, src, dst, send, recv, x, y, c):
    return _rcopy(src, dst, send.at[r - 1], recv.at[r - 1], (x ^ (r >> 2), y ^ ((r >> 1) & 1), c ^ (r & 1)))


def small_start(pack):
    def body(p, land, send, recv, p_thru, land_thru, token):
        x, y, c, _ = _place()
        me = 4 * x + 2 * y + c
        for r in range(1, NDEV):
            _start_pieces(p, land.at[me], lambda s, d, r=r: _small_copy(r, s, d, send, recv, x, y, c), 128 << 10)
        token[...] = jnp.zeros_like(token)

    ops = [pltpu.with_memory_space_constraint(v, pltpu.HBM) for v in (pack, lax.empty((NDEV,) + pack.shape, f32))]
    sem = pltpu.SemaphoreType.DMA((NDEV - 1,))
    return pl.pallas_call(
        body, name="small_start", in_specs=[HBM, HBM],
        out_specs=[SEMS, SEMS, HBM, HBM, pl.BlockSpec(memory_space=pltpu.VMEM)],
        out_shape=[sem, sem] + [pltpu.HBM(v.shape, v.dtype) for v in ops] + [jax.ShapeDtypeStruct((8, LANES), f32)],
        input_output_aliases={0: 2, 1: 3}, compiler_params=pltpu.CompilerParams(has_side_effects=EFFECT),
    )(*ops)


def small_wait(send, recv, pack, land, after):
    def body(p, land_r, send_r, recv_r, *rest):
        x, y, c, _ = _place()
        me = 4 * x + 2 * y + c
        for r in range(1, NDEV):
            _small_copy(r, p, land_r.at[me], send_r, recv_r, x, y, c).wait_send()
            src = 4 * (x ^ (r >> 2)) + 2 * (y ^ ((r >> 1) & 1)) + (c ^ (r & 1))
            _small_copy(r, p, land_r.at[src], send_r, recv_r, x, y, c).wait_recv()

    return pl.pallas_call(
        body, name="small_wait", in_specs=[HBM, HBM, SEMS, SEMS] + [ANY] * len(after), out_specs=[HBM, HBM],
        out_shape=[pltpu.HBM(pack.shape, f32), pltpu.HBM(land.shape, f32)], input_output_aliases={0: 0, 1: 1},
        compiler_params=pltpu.CompilerParams(has_side_effects=EFFECT),
    )(pack, land, send, recv, *after)


def small_sum(land, pack):
    def body(land_r, p_r, o_r):
        me = 4 * lax.axis_index("x") + 2 * lax.axis_index("y") + lax.axis_index("c")
        for k in range(NDEV):
            @pl.when(me == k)
            def _():
                acc = None
                for d in range(NDEV):
                    t = p_r[...] if d == k else land_r[d]
                    acc = t if acc is None else acc + t
                o_r[...] = acc

    vm = pl.BlockSpec(memory_space=pltpu.VMEM)
    return pl.pallas_call(
        body, name="small_sum", in_specs=[vm, vm], out_specs=vm, out_shape=jax.ShapeDtypeStruct(pack.shape, f32),
        compiler_params=pltpu.CompilerParams(vmem_limit_bytes=40 << 20),
    )(land, pack)


def _adamw_math(w, g, m, v):
    m = ADAM_B1 * m + (1.0 - ADAM_B1) * g
    v = ADAM_B2 * v + (1.0 - ADAM_B2) * (g * g)
    m_hat = m / (1.0 - ADAM_B1 ** ADAM_STEP)
    v_hat = v / (1.0 - ADAM_B2 ** ADAM_STEP)
    return -ADAM_LR * (m_hat / (jnp.sqrt(v_hat) + ADAM_EPS) + ADAM_WD * w), m, v


def adamw_big(name, halves, w, m, v):
    _, R, C = w.shape
    tr = _row_tile(R // 2, C, target=1 << 20)
    nt = R // 2 // tr

    def body(a0, b0, a1, b1, w_r, m_r, v_r, g_o, d_o, m_o, v_o):
        mine = pl.program_id(1) == lax.axis_index("c")
        g = jnp.where(pl.program_id(0) == 0, jnp.where(mine, a0[...], b0[...]), jnp.where(mine, a1[...], b1[...]))
        g_o[...] = g
        d_o[...], m_o[...], v_o[...] = _adamw_math(w_r[...], g, m_r[...], v_r[...])

    stk = pl.BlockSpec((None, tr, C), lambda l, h, i: (l, h * nt + i, 0))
    lay0 = pl.BlockSpec((tr, C), lambda l, h, i: (jnp.where(l == 0, i, nt - 1), 0))
    lay1 = pl.BlockSpec((tr, C), lambda l, h, i: (jnp.where(l == 0, 0, i), 0))
    return pl.pallas_call(
        body, name=name, grid=(DEPTH, 2, nt),
        in_specs=[lay0, lay0, lay1, lay1, stk, stk, stk],
        out_specs=[stk] * 4, out_shape=[jax.ShapeDtypeStruct(w.shape, f32)] * 4,
        compiler_params=_cp(("arbitrary", "arbitrary", "arbitrary")),
    )(*halves[0], *halves[1], w, m, v)


def adamw_small(name, g, w, m, v):
    def body(g_r, w_r, m_r, v_r, d_o, m_o, v_o):
        d_o[...], m_o[...], v_o[...] = _adamw_math(w_r[...], g_r[...], m_r[...], v_r[...])

    return pl.pallas_call(body, name=name, out_shape=[jax.ShapeDtypeStruct(w.shape, f32)] * 3)(g, w, m, v)


WEIGHTS = ("w_in", "conv_w", "gmlp_ln_g", "gmlp_ln_b", "w_s", "b_s", "p_a", "p_b", "p_c", "w_o", "ln1_g", "ln1_b",
           "w_gate", "w_up", "w_down", "ln2_g", "ln2_b")
VECS = ("ln1_g", "ln1_b", "ln2_g", "ln2_b", "gmlp_ln_g", "gmlp_ln_b")
ROWS_VEC, ROWS_BS, ROWS_WS, ROWS_CONV = D // LANES, 8, 8 * BLK, 3 * D // LANES
ROWS_LAYER = len(VECS) * ROWS_VEC + ROWS_BS + ROWS_WS + ROWS_CONV


def _pack_small(per_layer, tail):
    parts = []
    for P in per_layer:
        parts += [P[n].reshape(ROWS_VEC, LANES) for n in VECS]
        parts += [P["b_s"].reshape(ROWS_BS, LANES), P["w_s"].reshape(ROWS_WS, LANES), P["conv_w"].reshape(ROWS_CONV, LANES)]
    return jnp.concatenate(parts + [tail], axis=0)


def _unpack_small(pack):
    out = []
    for l in range(DEPTH):
        r = l * ROWS_LAYER
        P = {}
        for n in VECS:
            P[n] = pack[r:r + ROWS_VEC].reshape(D)
            r += ROWS_VEC
        P["b_s"] = pack[r:r + ROWS_BS].reshape(8, BLK)
        r += ROWS_BS
        P["w_s"] = pack[r:r + ROWS_WS].reshape(8, BLK, BLK)
        r += ROWS_WS
        P["conv_w"] = pack[r:r + ROWS_CONV].reshape(3, D)
        out.append(P)
    return out, pack[DEPTH * ROWS_LAYER:]


def kernel(x, positions, w_in, conv_w, gmlp_ln_g, gmlp_ln_b, w_s, b_s, p_a, p_b, p_c, w_o, ln1_g, ln1_b, w_gate, w_up, w_down, ln2_g, ln2_b, loss_target, m_w_in, m_conv_w, m_gmlp_ln_g, m_gmlp_ln_b, m_w_s, m_b_s, m_p_a, m_p_b, m_p_c, m_w_o, m_ln1_g, m_ln1_b, m_w_gate, m_w_up, m_w_down, m_ln2_g, m_ln2_b, v_w_in, v_conv_w, v_gmlp_ln_g, v_gmlp_ln_b, v_w_s, v_b_s, v_p_a, v_p_b, v_p_c, v_w_o, v_ln1_g, v_ln1_b, v_w_gate, v_w_up, v_w_down, v_ln2_g, v_ln2_b):
    Wt = dict(w_in=w_in, conv_w=conv_w, gmlp_ln_g=gmlp_ln_g, gmlp_ln_b=gmlp_ln_b, w_s=w_s, b_s=b_s, p_a=p_a, p_b=p_b,
              p_c=p_c, w_o=w_o, ln1_g=ln1_g, ln1_b=ln1_b, w_gate=w_gate, w_up=w_up, w_down=w_down, ln2_g=ln2_g, ln2_b=ln2_b)
    Mt = dict(w_in=m_w_in, conv_w=m_conv_w, gmlp_ln_g=m_gmlp_ln_g, gmlp_ln_b=m_gmlp_ln_b, w_s=m_w_s, b_s=m_b_s, p_a=m_p_a,
              p_b=m_p_b, p_c=m_p_c, w_o=m_w_o, ln1_g=m_ln1_g, ln1_b=m_ln1_b, w_gate=m_w_gate, w_up=m_w_up,
              w_down=m_w_down, ln2_g=m_ln2_g, ln2_b=m_ln2_b)
    Vt = dict(w_in=v_w_in, conv_w=v_conv_w, gmlp_ln_g=v_gmlp_ln_g, gmlp_ln_b=v_gmlp_ln_b, w_s=v_w_s, b_s=v_b_s, p_a=v_p_a,
              p_b=v_p_b, p_c=v_p_c, w_o=v_w_o, ln1_g=v_ln1_g, ln1_b=v_ln1_b, w_gate=v_w_gate, w_up=v_w_up,
              w_down=v_w_down, ln2_g=v_ln2_g, ln2_b=v_ln2_b)
    chip = 2 * lax.axis_index("x") + lax.axis_index("y")
    cw = D // NCHIP

    def gathered_weights(names, arrays):
        Wl = dict(zip(names, arrays))
        for n in ("p_a", "p_c", "w_o"):
            Wl[n] = Wl[n].reshape(D, D)
        return Wl

    def small_weights(l, conv_all):
        Wl = {n: Wt[n][l] for n in VECS + ("w_s", "b_s")}
        Wl["conv_w"] = conv_all[:, l].transpose(1, 0, 2).reshape(3, D)
        return Wl

    w_in0, conv_all = gather_halves([Wt["w_in"][0].astype(MX).reshape(2, D // 2, WIN_SHARD), conv_w])
    rest = BIG[1:]
    *late0, coming0 = gather_start("0", [Wt[n][0].astype(MX) for n in rest], [conv_all])
    *late1, coming1 = gather_start("1", [Wt[n][1].astype(MX) for n in BIG], [conv_all, coming0])
    W0 = dict(small_weights(0, conv_all), w_in=w_in0.reshape(D, NIN), after=coming1[0, 0],
              late=lambda y: gathered_weights(rest, gather_wait("0", *late0, [y])))

    def W1(h):
        return dict(small_weights(1, conv_all), **gathered_weights(BIG, gather_wait("1", *late1, [h])))

    layers = [W0, W1]

    rs_state, rs_started, held = {}, {}, {}

    def start_exchange(l, g):
        if "loss" in g:
            held[l] = g
            return None
        if "conv_w" in g:
            held[l] = g
            rs_state[(l, False)], started = reduce_scatter_chips(rs_state[(l, False)], [g["w_s"], g["conv_w"]])
            if l == 0:
                pack = _pack_small([held[j] for j in range(DEPTH)], held[DEPTH]["loss"])
                *held["small"], token = small_start(pack)
                started = started + token[0, 0]
            return started
        if "dx" in g:
            rs_state[(l, True)], rs_started[(l, True)] = reduce_scatter_chips(rs_state[(l, True)], [g["dx"]])
            return rs_started[(l, True)]
        key = (l, "w_in" in g)
        rs_state[key], started = reduce_scatter_pair(f"{l}{'b' if key[1] else 'a'}", g)
        return started

    _, grad_x, _ = local_step(x[0], positions[0], loss_target[0], layers, start_exchange)

    last = jnp.zeros((8, LANES), f32) + rs_started[(0, True)]
    behind = [grad_x, last]
    red = [dict() for _ in range(DEPTH)]
    swaps = {key: reduce_scatter_finish(rs_state[key], behind) for key in ((1, False), (1, True), (0, False))}
    small, tail = _unpack_small(small_sum(*reversed(small_wait(*held["small"], behind))))
    loss = tail[0, 0]

    G, DW, NM, NV = {}, {}, {}, {}
    zc = jnp.zeros((3, D), f32)
    wp = _pack_small([{**{n: Wt[n][l] for n in VECS + ("b_s", "w_s")}, "conv_w": zc} for l in range(DEPTH)], jnp.zeros((8, LANES), f32))
    mp = _pack_small([{**{n: Mt[n][l] for n in VECS + ("b_s", "w_s")}, "conv_w": zc} for l in range(DEPTH)], jnp.zeros((8, LANES), f32))
    vp = _pack_small([{**{n: Vt[n][l] for n in VECS + ("b_s", "w_s")}, "conv_w": zc} for l in range(DEPTH)], jnp.ones((8, LANES), f32))
    gp = _pack_small(small, jnp.zeros((8, LANES), f32))
    outs = [_unpack_small(a)[0] for a in adamw_small("adamw_small", gp, wp, mp, vp)]
    for n in VECS + ("b_s", "w_s"):
        G[n] = jnp.stack([small[l][n] for l in range(DEPTH)])
        DW[n], NM[n], NV[n] = (jnp.stack([o[l][n] for l in range(DEPTH)]) for o in outs)
    gconv = jnp.stack([lax.dynamic_slice(small[l]["conv_w"], (0, chip * cw), (3, cw)) for l in range(DEPTH)])
    G["conv_w"] = gconv
    flat = lambda a: a.reshape(DEPTH * 3, cw)
    d, m2, v2 = adamw_small("adamw_conv", flat(gconv), flat(conv_w), flat(m_conv_w), flat(v_conv_w))
    DW["conv_w"], NM["conv_w"], NV["conv_w"] = (a.reshape(DEPTH, 3, cw) for a in (d, m2, v2))

    for key in swaps:
        red[key[0]].update(reduce_scatter_join(swaps[key], [d, DW["ln2_b"]]))
    updated = {}
    for n in BIG[1:]:
        tr = (lambda a: jnp.swapaxes(a, 1, 2)) if n in ("w_gate", "w_up") else (lambda a: a)
        updated[n] = adamw_big("adamw_" + n, (red[0][n], red[1][n]), tr(Wt[n]), tr(Mt[n]), tr(Vt[n]))
        G[n], DW[n], NM[n], NV[n] = map(tr, updated[n])
    done = [d, DW["ln2_b"], red[1]["w_in"][1]] + [updated[n][1] for n in BIG[1:]]
    red[0].update(reduce_scatter_join(reduce_scatter_finish(rs_state[(0, True)], done), [updated["w_o"][1]]))
    G["w_in"], DW["w_in"], NM["w_in"], NV["w_in"] = adamw_big(
        "adamw_w_in", (red[0]["w_in"], red[1]["w_in"]), Wt["w_in"], Mt["w_in"], Vt["w_in"])

    return (loss, grad_x[None], *[G[n] for n in WEIGHTS], *[DW[n] for n in WEIGHTS], *[NM[n] for n in WEIGHTS],
            *[NV[n] for n in WEIGHTS])
```
